```python
import jax, jax.numpy as jnp
from jax import lax
import numpy as np


D_MODEL = 1024
BATCH = 16
SEQ = 2048
DEPTH = 2

DN_HEADS = 4
DN_HEAD_DIM = 128
DN_WIDTH = DN_HEADS * DN_HEAD_DIM
DN_CONV = 4
DN_CHUNK = 64
SB_HEADS = 4
SB_HEAD_DIM = 64
SB_WIDTH = SB_HEADS * SB_HEAD_DIM
SB_BLOCK = 128
SG_GROUPS = 4
SG_GROUP_DIM = 64
SG_WIDTH = SG_GROUPS * SG_GROUP_DIM
SG_CHUNK = 128
MIX_WIDTH = DN_WIDTH + SB_WIDTH + SG_WIDTH
D_FF = 4 * D_MODEL
IN_SPLITS = (3 * DN_WIDTH, DN_WIDTH, DN_HEADS, DN_HEADS, 3 * SB_WIDTH, 2 * SG_WIDTH)
IN_DIM = int(sum(IN_SPLITS))
IN_SPLIT_IDX = tuple(int(i) for i in np.cumsum(IN_SPLITS)[:-1])
NORM_EPS = 1e-6

kernel_name = 'hybrid_deltanet_stickbreaking_sgmlp_block'


def rms_norm(x, gain):
    xf = x.astype(jnp.float32)
    y = xf * lax.rsqrt(jnp.mean(xf * xf, axis=-1, keepdims=True) + NORM_EPS)
    return (y * gain.astype(jnp.float32)).astype(x.dtype)


def l2_norm(x):
    xf = x.astype(jnp.float32)
    return xf * lax.rsqrt(jnp.sum(xf * xf, axis=-1, keepdims=True) + NORM_EPS)


def causal_depthwise_conv(x, w):
    K, C = w.shape
    return lax.conv_general_dilated(
        x, w[:, None, :].astype(x.dtype), window_strides=(1,), padding=[(K - 1, 0)],
        dimension_numbers=('NWC', 'WIO', 'NWC'), feature_group_count=C)


def gated_delta_rule(q, k, v, g, beta):
    f32 = jnp.float32
    q, k, v, g, beta = (t.astype(f32) for t in (q, k, v, g, beta))
    B, T, H, Dk = q.shape
    Dv = v.shape[-1]
    C = DN_CHUNK
    N = T // C
    q = q * (Dk ** -0.5)

    def to_chunks(t):
        return jnp.moveaxis(t.reshape((B, N, C) + t.shape[2:]), 3, 2)

    qc, kc, vc, gc, bc = (to_chunks(t) for t in (q, k, v, g, beta))
    gcum = jnp.cumsum(gc, axis=-1)
    tril_incl = jnp.tril(jnp.ones((C, C), bool))
    tril_strict = jnp.tril(jnp.ones((C, C), bool), -1)
    decay = jnp.exp(jnp.where(tril_incl, gcum[..., :, None] - gcum[..., None, :], -jnp.inf))
    kk = jnp.einsum('bnhid,bnhjd->bnhij', kc, kc)
    lower = jnp.where(tril_strict, bc[..., :, None] * kk * decay, 0.0)
    a_mat = lower + jnp.eye(C, dtype=f32)
    rhs = jnp.concatenate([vc * bc[..., None], kc * (bc * jnp.exp(gcum))[..., None]], axis=-1)
    sol = lax.linalg.triangular_solve(a_mat, rhs, left_side=True, lower=True, unit_diagonal=True)
    u_val, w_dec = sol[..., :Dv], sol[..., Dv:]
    qk = jnp.einsum('bnhid,bnhjd->bnhij', qc, kc) * decay
    q_dec = qc * jnp.exp(gcum)[..., None]
    k_dec = kc * jnp.exp(gcum[..., -1:] - gcum)[..., None]
    chunk_decay = jnp.exp(gcum[..., -1])

    def step(S, inp):
        u_n, w_n, qk_n, qd_n, kd_n, cd_n = inp
        u_new = u_n - jnp.einsum('bhcd,bhde->bhce', w_n, S)
        o = jnp.einsum('bhcd,bhde->bhce', qd_n, S) + jnp.einsum('bhij,bhje->bhie', qk_n, u_new)
        S = S * cd_n[..., None, None] + jnp.einsum('bhcd,bhce->bhde', kd_n, u_new)
        return S, o

    xs = tuple(jnp.moveaxis(t, 1, 0) for t in (u_val, w_dec, qk, q_dec, k_dec, chunk_decay))
    S0 = jnp.zeros((B, H, Dk, Dv), f32)
    _, o = lax.scan(step, S0, xs)
    o = jnp.moveaxis(jnp.moveaxis(o, 0, 1), 2, 3).reshape(B, T, H, Dv)
    return o


def stick_breaking_attention(q, k, v):
    B, T, H, D = q.shape
    scale = D ** -0.5
    outs = []
    for blk in range(T // SB_BLOCK):
        q0 = blk * SB_BLOCK
        q1 = q0 + SB_BLOCK
        qb, kb, vb = q[:, q0:q1], k[:, :q1], v[:, :q1]
        z = jnp.einsum('bthd,bshd->bhts', qb, kb).astype(jnp.float32) * scale
        t_idx = q0 + jnp.arange(SB_BLOCK)
        s_idx = jnp.arange(q1)
        mask = s_idx[None, :] < t_idx[:, None]
        log_1m = jnp.where(mask, jax.nn.log_sigmoid(-z), 0.0)
        remaining = lax.cumsum(log_1m, axis=3, reverse=True) - log_1m
        weights = jnp.where(mask, jnp.exp(jax.nn.log_sigmoid(z) + remaining), 0.0)
        outs.append(jnp.einsum('bhts,bshd->bthd', weights.astype(v.dtype), vb))
    return jnp.concatenate(outs, axis=1)


def chunked_spatial_gating(u, v, v_gain, w_s, b_s):
    B, T, _ = u.shape
    N = T // SG_CHUNK
    u = jax.nn.gelu(u)
    v = jax.nn.gelu(v).reshape(B, T, SG_GROUPS, SG_GROUP_DIM)
    v = rms_norm(v, v_gain.reshape(SG_GROUPS, SG_GROUP_DIM))
    v = v.reshape(B, N, SG_CHUNK, SG_GROUPS, SG_GROUP_DIM)
    w = jnp.where(jnp.tril(jnp.ones((SG_CHUNK, SG_CHUNK), bool)), w_s, 0.0).astype(v.dtype)
    mixed = jnp.einsum('gts,bnsgd->bntgd', w, v) + b_s.T[None, None, :, :, None]
    return u * mixed.reshape(B, T, SG_WIDTH)


def hybrid_mixer(h, w_in, conv_w, a_log, dt_bias, dn_out_g, sb_q_g, sb_k_g, sg_v_g, sg_w, sg_b, w_out):
    B, T, _ = h.shape
    proj = h @ w_in
    dn_qkv, dn_z, dn_a, dn_b, sb_qkv, sg_uv = jnp.split(proj, IN_SPLIT_IDX, axis=-1)
    dn_qkv = jax.nn.silu(causal_depthwise_conv(dn_qkv, conv_w))
    q, k, v = (t.reshape(B, T, DN_HEADS, DN_HEAD_DIM) for t in jnp.split(dn_qkv, 3, axis=-1))
    g = -jnp.exp(a_log.astype(jnp.float32)) * jax.nn.softplus(dn_a.astype(jnp.float32) + dt_bias.astype(jnp.float32))
    beta = jax.nn.sigmoid(dn_b.astype(jnp.float32))
    o_dn = gated_delta_rule(l2_norm(q), l2_norm(k), v, g, beta).astype(h.dtype)
    o_dn = rms_norm(o_dn, dn_out_g) * jax.nn.silu(dn_z.reshape(B, T, DN_HEADS, DN_HEAD_DIM))
    o_dn = o_dn.reshape(B, T, DN_WIDTH)
    sq, sk, sv = (t.reshape(B, T, SB_HEADS, SB_HEAD_DIM) for t in jnp.split(sb_qkv, 3, axis=-1))
    o_sb = stick_breaking_attention(rms_norm(sq, sb_q_g), rms_norm(sk, sb_k_g), sv).reshape(B, T, SB_WIDTH)
    su, svv = jnp.split(sg_uv, 2, axis=-1)
    o_sg = chunked_spatial_gating(su, svv, sg_v_g, sg_w, sg_b)
    return jnp.concatenate([o_dn, o_sb, o_sg], axis=-1) @ w_out


def _fwd_setup_inputs(seed: int = 0) -> dict:
    key = jax.random.key(seed)
    ks = jax.random.split(key, 20)
    f32 = jnp.float32
    nrm = lambda k, shape, s: jax.random.normal(k, shape, f32) * s
    dt = jnp.exp(jax.random.uniform(ks[5], (DEPTH, DN_HEADS), f32, np.log(1e-3), np.log(1e-1)))
    return {
        'x': nrm(ks[0], (BATCH, SEQ, D_MODEL), 1.0),
        'norm1_g': 1.0 + nrm(ks[1], (DEPTH, D_MODEL), 0.1),
        'w_in': nrm(ks[2], (DEPTH, D_MODEL, IN_DIM), D_MODEL ** -0.5),
        'conv_w': nrm(ks[3], (DEPTH, DN_CONV, 3 * DN_WIDTH), DN_CONV ** -0.5),
        'a_log': jnp.log(jax.random.uniform(ks[4], (DEPTH, DN_HEADS), f32, 1.0, 16.0)),
        'dt_bias': jnp.log(jnp.expm1(dt)),
        'dn_out_g': 1.0 + nrm(ks[6], (DEPTH, DN_HEAD_DIM), 0.1),
        'sb_q_g': 1.0 + nrm(ks[7], (DEPTH, SB_HEAD_DIM), 0.1),
        'sb_k_g': 1.0 + nrm(ks[8], (DEPTH, SB_HEAD_DIM), 0.1),
        'sg_v_g': 1.0 + nrm(ks[9], (DEPTH, SG_WIDTH), 0.1),
        'sg_w': nrm(ks[10], (DEPTH, SG_GROUPS, SG_CHUNK, SG_CHUNK), SG_CHUNK ** -0.5),
        'sg_b': 1.0 + nrm(ks[11], (DEPTH, SG_GROUPS, SG_CHUNK), 0.1),
        'w_out': nrm(ks[12], (DEPTH, MIX_WIDTH, D_MODEL), MIX_WIDTH ** -0.5),
        'norm2_g': 1.0 + nrm(ks[13], (DEPTH, D_MODEL), 0.1),
        'w_ff1': nrm(ks[14], (DEPTH, D_MODEL, D_FF), D_MODEL ** -0.5),
        'w_ff2': nrm(ks[15], (DEPTH, D_FF, D_MODEL), D_FF ** -0.5),
    }


def _fwd_reference(x, norm1_g, w_in, conv_w, a_log, dt_bias, dn_out_g, sb_q_g, sb_k_g, sg_v_g, sg_w, sg_b,
              w_out, norm2_g, w_ff1, w_ff2):
    for l in range(DEPTH):
        h = rms_norm(x, norm1_g[l])
        x = x + hybrid_mixer(h, w_in[l], conv_w[l], a_log[l], dt_bias[l], dn_out_g[l], sb_q_g[l], sb_k_g[l],
                             sg_v_g[l], sg_w[l], sg_b[l], w_out[l])
        h = rms_norm(x, norm2_g[l])
        x = x + jnp.square(jax.nn.relu(h @ w_ff1[l])) @ w_ff2[l]
    return x


import jax as _jax
import jax.numpy as _jnp

TWIN_FORMAT = 'train_step'
FWD_PARAMS = ['x', 'norm1_g', 'w_in', 'conv_w', 'a_log', 'dt_bias', 'dn_out_g', 'sb_q_g', 'sb_k_g', 'sg_v_g', 'sg_w', 'sg_b', 'w_out', 'norm2_g', 'w_ff1', 'w_ff2']
TWIN_WEIGHTS = ['norm1_g', 'w_in', 'conv_w', 'a_log', 'dt_bias', 'dn_out_g', 'sb_q_g', 'sb_k_g', 'sg_v_g', 'sg_w', 'sg_b', 'w_out', 'norm2_g', 'w_ff1', 'w_ff2']
TWIN_DIFF_INPUT = 'x'
TWIN_INPUTS = ['x', 'norm1_g', 'w_in', 'conv_w', 'a_log', 'dt_bias', 'dn_out_g', 'sb_q_g', 'sb_k_g', 'sg_v_g', 'sg_w', 'sg_b', 'w_out', 'norm2_g', 'w_ff1', 'w_ff2', 'loss_target', 'm_norm1_g', 'm_w_in', 'm_conv_w', 'm_a_log', 'm_dt_bias', 'm_dn_out_g', 'm_sb_q_g', 'm_sb_k_g', 'm_sg_v_g', 'm_sg_w', 'm_sg_b', 'm_w_out', 'm_norm2_g', 'm_w_ff1', 'm_w_ff2', 'v_norm1_g', 'v_w_in', 'v_conv_w', 'v_a_log', 'v_dt_bias', 'v_dn_out_g', 'v_sb_q_g', 'v_sb_k_g', 'v_sg_v_g', 'v_sg_w', 'v_sg_b', 'v_w_out', 'v_norm2_g', 'v_w_ff1', 'v_w_ff2']
TWIN_OUTPUTS = ['loss', 'grad_x', 'grad_norm1_g', 'grad_w_in', 'grad_conv_w', 'grad_a_log', 'grad_dt_bias', 'grad_dn_out_g', 'grad_sb_q_g', 'grad_sb_k_g', 'grad_sg_v_g', 'grad_sg_w', 'grad_sg_b', 'grad_w_out', 'grad_norm2_g', 'grad_w_ff1', 'grad_w_ff2', 'delta_norm1_g', 'delta_w_in', 'delta_conv_w', 'delta_a_log', 'delta_dt_bias', 'delta_dn_out_g', 'delta_sb_q_g', 'delta_sb_k_g', 'delta_sg_v_g', 'delta_sg_w', 'delta_sg_b', 'delta_w_out', 'delta_norm2_g', 'delta_w_ff1', 'delta_w_ff2', 'new_m_norm1_g', 'new_m_w_in', 'new_m_conv_w', 'new_m_a_log', 'new_m_dt_bias', 'new_m_dn_out_g', 'new_m_sb_q_g', 'new_m_sb_k_g', 'new_m_sg_v_g', 'new_m_sg_w', 'new_m_sg_b', 'new_m_w_out', 'new_m_norm2_g', 'new_m_w_ff1', 'new_m_w_ff2', 'new_v_norm1_g', 'new_v_w_in', 'new_v_conv_w', 'new_v_a_log', 'new_v_dt_bias', 'new_v_dn_out_g', 'new_v_sb_q_g', 'new_v_sb_k_g', 'new_v_sg_v_g', 'new_v_sg_w', 'new_v_sg_b', 'new_v_w_out', 'new_v_norm2_g', 'new_v_w_ff1', 'new_v_w_ff2']
TWIN_LEAF_KINDS = {'loss': 'loss', 'grad_x': 'grad_x', 'grad_norm1_g': 'grad_w', 'grad_w_in': 'grad_w', 'grad_conv_w': 'grad_w', 'grad_a_log': 'grad_w', 'grad_dt_bias': 'grad_w', 'grad_dn_out_g': 'grad_w', 'grad_sb_q_g': 'grad_w', 'grad_sb_k_g': 'grad_w', 'grad_sg_v_g': 'grad_w', 'grad_sg_w': 'grad_w', 'grad_sg_b': 'grad_w', 'grad_w_out': 'grad_w', 'grad_norm2_g': 'grad_w', 'grad_w_ff1': 'grad_w', 'grad_w_ff2': 'grad_w', 'delta_norm1_g': 'delta_w', 'delta_w_in': 'delta_w', 'delta_conv_w': 'delta_w', 'delta_a_log': 'delta_w', 'delta_dt_bias': 'delta_w', 'delta_dn_out_g': 'delta_w', 'delta_sb_q_g': 'delta_w', 'delta_sb_k_g': 'delta_w', 'delta_sg_v_g': 'delta_w', 'delta_sg_w': 'delta_w', 'delta_sg_b': 'delta_w', 'delta_w_out': 'delta_w', 'delta_norm2_g': 'delta_w', 'delta_w_ff1': 'delta_w', 'delta_w_ff2': 'delta_w', 'new_m_norm1_g': 'new_m', 'new_m_w_in': 'new_m', 'new_m_conv_w': 'new_m', 'new_m_a_log': 'new_m', 'new_m_dt_bias': 'new_m', 'new_m_dn_out_g': 'new_m', 'new_m_sb_q_g': 'new_m', 'new_m_sb_k_g': 'new_m', 'new_m_sg_v_g': 'new_m', 'new_m_sg_w': 'new_m', 'new_m_sg_b': 'new_m', 'new_m_w_out': 'new_m', 'new_m_norm2_g': 'new_m', 'new_m_w_ff1': 'new_m', 'new_m_w_ff2': 'new_m', 'new_v_norm1_g': 'new_v', 'new_v_w_in': 'new_v', 'new_v_conv_w': 'new_v', 'new_v_a_log': 'new_v', 'new_v_dt_bias': 'new_v', 'new_v_dn_out_g': 'new_v', 'new_v_sb_q_g': 'new_v', 'new_v_sb_k_g': 'new_v', 'new_v_sg_v_g': 'new_v', 'new_v_sg_w': 'new_v', 'new_v_sg_b': 'new_v', 'new_v_w_out': 'new_v', 'new_v_norm2_g': 'new_v', 'new_v_w_ff1': 'new_v', 'new_v_w_ff2': 'new_v'}


def _forward(args):
    return _fwd_reference(*[args[k] for k in FWD_PARAMS])


def _output_shape():
    out = _jax.eval_shape(lambda: _forward(_fwd_setup_inputs(0)))
    return out.shape, out.dtype

N_MICROBATCH = 1
ADAM_LR = 0.001
ADAM_B1 = 0.9
ADAM_B2 = 0.999
ADAM_EPS = 1e-08
ADAM_WD = 0.01
ADAM_STEP = 10
PER_EXAMPLE_BATCH_AXIS = {'x': 0, 'loss_target': 0}
SHARED_INPUTS = []
_WEIGHT_DTYPES = {'norm1_g': _jnp.float32, 'w_in': _jnp.float32, 'conv_w': _jnp.float32, 'a_log': _jnp.float32, 'dt_bias': _jnp.float32, 'dn_out_g': _jnp.float32, 'sb_q_g': _jnp.float32, 'sb_k_g': _jnp.float32, 'sg_v_g': _jnp.float32, 'sg_w': _jnp.float32, 'sg_b': _jnp.float32, 'w_out': _jnp.float32, 'norm2_g': _jnp.float32, 'w_ff1': _jnp.float32, 'w_ff2': _jnp.float32}
MOMENT_SCALE = {'norm1_g': 1.766666e+01, 'w_in': 3.151679e+00, 'conv_w': 3.211178e+00, 'a_log': 4.904273e+01, 'dt_bias': 4.662951e+01, 'dn_out_g': 5.911272e+01, 'sb_q_g': 7.869410e+00, 'sb_k_g': 7.707778e+00, 'sg_v_g': 7.623242e+00, 'sg_w': 1.941650e+00, 'sg_b': 7.000226e+00, 'w_out': 8.272487e+00, 'norm2_g': 9.909567e+01, 'w_ff1': 4.667220e+00, 'w_ff2': 1.748184e+01}


def _to_microbatches(a, axis):
    t = _jnp.moveaxis(a, axis, 0)
    t = t.reshape((N_MICROBATCH, t.shape[0] // N_MICROBATCH) + t.shape[1:])
    return _jnp.moveaxis(t, 1, axis + 1)


def setup_inputs(seed: int = 0) -> dict:
    inp = _fwd_setup_inputs(seed)
    key = _jax.random.fold_in(_jax.random.key(seed), 7919)
    shape, _ = _output_shape()
    out = dict(inp)
    out["loss_target"] = _jax.random.normal(_jax.random.fold_in(key, 0), shape, _jnp.float32)
    for i, name in enumerate(TWIN_WEIGHTS):
        w = inp[name].astype(_jnp.float32)
        if MOMENT_SCALE is None:
            s = _jnp.sqrt(_jnp.mean(_jnp.square(w)) + 1e-30)
        else:
            s = MOMENT_SCALE[name]
        km, kv = _jax.random.split(_jax.random.fold_in(key, i + 1))
        out[name] = w
        out["m_" + name] = s * _jax.random.normal(km, w.shape, _jnp.float32)
        out["v_" + name] = (s * s) * _jax.random.uniform(kv, w.shape, _jnp.float32, 0.5, 1.5)
    if N_MICROBATCH > 1:
        for name, axis in PER_EXAMPLE_BATCH_AXIS.items():
            out[name] = _to_microbatches(out[name], axis)
    return {'x': out['x'], 'norm1_g': out['norm1_g'], 'w_in': out['w_in'], 'conv_w': out['conv_w'], 'a_log': out['a_log'], 'dt_bias': out['dt_bias'], 'dn_out_g': out['dn_out_g'], 'sb_q_g': out['sb_q_g'], 'sb_k_g': out['sb_k_g'], 'sg_v_g': out['sg_v_g'], 'sg_w': out['sg_w'], 'sg_b': out['sg_b'], 'w_out': out['w_out'], 'norm2_g': out['norm2_g'], 'w_ff1': out['w_ff1'], 'w_ff2': out['w_ff2'], 'loss_target': out['loss_target'], 'm_norm1_g': out['m_norm1_g'], 'm_w_in': out['m_w_in'], 'm_conv_w': out['m_conv_w'], 'm_a_log': out['m_a_log'], 'm_dt_bias': out['m_dt_bias'], 'm_dn_out_g': out['m_dn_out_g'], 'm_sb_q_g': out['m_sb_q_g'], 'm_sb_k_g': out['m_sb_k_g'], 'm_sg_v_g': out['m_sg_v_g'], 'm_sg_w': out['m_sg_w'], 'm_sg_b': out['m_sg_b'], 'm_w_out': out['m_w_out'], 'm_norm2_g': out['m_norm2_g'], 'm_w_ff1': out['m_w_ff1'], 'm_w_ff2': out['m_w_ff2'], 'v_norm1_g': out['v_norm1_g'], 'v_w_in': out['v_w_in'], 'v_conv_w': out['v_conv_w'], 'v_a_log': out['v_a_log'], 'v_dt_bias': out['v_dt_bias'], 'v_dn_out_g': out['v_dn_out_g'], 'v_sb_q_g': out['v_sb_q_g'], 'v_sb_k_g': out['v_sb_k_g'], 'v_sg_v_g': out['v_sg_v_g'], 'v_sg_w': out['v_sg_w'], 'v_sg_b': out['v_sg_b'], 'v_w_out': out['v_w_out'], 'v_norm2_g': out['v_norm2_g'], 'v_w_ff1': out['v_w_ff1'], 'v_w_ff2': out['v_w_ff2']}


def _loss(weights, diff, rest, loss_target):
    with _jax.named_scope("forward"):
        args = {**rest, TWIN_DIFF_INPUT: diff, **{k: w.astype(_WEIGHT_DTYPES[k]) for k, w in weights.items()}}
        y = _forward(args)
    with _jax.named_scope("loss_head"):
        err = _jnp.square(y.astype(_jnp.float32) - loss_target)
        return 0.5 * _jnp.sum(_jnp.mean(err, axis=-1)) if err.ndim else 0.5 * err


def _adamw(w, g, m, v):
    m = ADAM_B1 * m + (1.0 - ADAM_B1) * g
    v = ADAM_B2 * v + (1.0 - ADAM_B2) * _jnp.square(g)
    m_hat = m / (1.0 - ADAM_B1 ** ADAM_STEP)
    v_hat = v / (1.0 - ADAM_B2 ** ADAM_STEP)
    delta = -ADAM_LR * (m_hat / (_jnp.sqrt(v_hat) + ADAM_EPS) + ADAM_WD * w)
    return delta, m, v


def reference(x, norm1_g, w_in, conv_w, a_log, dt_bias, dn_out_g, sb_q_g, sb_k_g, sg_v_g, sg_w, sg_b, w_out, norm2_g, w_ff1, w_ff2, loss_target, m_norm1_g, m_w_in, m_conv_w, m_a_log, m_dt_bias, m_dn_out_g, m_sb_q_g, m_sb_k_g, m_sg_v_g, m_sg_w, m_sg_b, m_w_out, m_norm2_g, m_w_ff1, m_w_ff2, v_norm1_g, v_w_in, v_conv_w, v_a_log, v_dt_bias, v_dn_out_g, v_sb_q_g, v_sb_k_g, v_sg_v_g, v_sg_w, v_sg_b, v_w_out, v_norm2_g, v_w_ff1, v_w_ff2):
    given = dict(x=x, norm1_g=norm1_g, w_in=w_in, conv_w=conv_w, a_log=a_log, dt_bias=dt_bias, dn_out_g=dn_out_g, sb_q_g=sb_q_g, sb_k_g=sb_k_g, sg_v_g=sg_v_g, sg_w=sg_w, sg_b=sg_b, w_out=w_out, norm2_g=norm2_g, w_ff1=w_ff1, w_ff2=w_ff2, loss_target=loss_target, m_norm1_g=m_norm1_g, m_w_in=m_w_in, m_conv_w=m_conv_w, m_a_log=m_a_log, m_dt_bias=m_dt_bias, m_dn_out_g=m_dn_out_g, m_sb_q_g=m_sb_q_g, m_sb_k_g=m_sb_k_g, m_sg_v_g=m_sg_v_g, m_sg_w=m_sg_w, m_sg_b=m_sg_b, m_w_out=m_w_out, m_norm2_g=m_norm2_g, m_w_ff1=m_w_ff1, m_w_ff2=m_w_ff2, v_norm1_g=v_norm1_g, v_w_in=v_w_in, v_conv_w=v_conv_w, v_a_log=v_a_log, v_dt_bias=v_dt_bias, v_dn_out_g=v_dn_out_g, v_sb_q_g=v_sb_q_g, v_sb_k_g=v_sb_k_g, v_sg_v_g=v_sg_v_g, v_sg_w=v_sg_w, v_sg_b=v_sg_b, v_w_out=v_w_out, v_norm2_g=v_norm2_g, v_w_ff1=v_w_ff1, v_w_ff2=v_w_ff2)
    weights = {n: given[n] for n in TWIN_WEIGHTS}
    shared = {n: given[n] for n in SHARED_INPUTS}
    per_example = {n: given[n] for n in ['x']}
    grad_fn = _jax.value_and_grad(_loss, argnums=(0, 1))

    def one_microbatch(ex, loss_target):
        ex = dict(ex)
        diff = ex.pop(TWIN_DIFF_INPUT)
        return grad_fn(weights, diff, {**shared, **ex}, loss_target)

    if N_MICROBATCH == 1:
        loss, (grad_w, grad_x) = one_microbatch(per_example, given["loss_target"])
    else:
        def body(carry, xs):
            loss_sum, grad_sum = carry
            l_k, (gw_k, gx_k) = one_microbatch(xs[0], xs[1])
            with _jax.named_scope("update"):
                return (loss_sum + l_k, _jax.tree.map(_jnp.add, grad_sum, gw_k)), gx_k

        init = (_jnp.zeros((), _jnp.float32), _jax.tree.map(_jnp.zeros_like, weights))
        (loss, grad_w), grad_x = _jax.lax.scan(body, init, (per_example, given["loss_target"]))
    with _jax.named_scope("update"):
        delta_w, new_m, new_v = {}, {}, {}
        for n in TWIN_WEIGHTS:
            delta_w[n], new_m[n], new_v[n] = _adamw(weights[n], grad_w[n], given["m_" + n], given["v_" + n])
    return (loss, grad_x, *[grad_w[n] for n in TWIN_WEIGHTS], *[delta_w[n] for n in TWIN_WEIGHTS],
            *[new_m[n] for n in TWIN_WEIGHTS], *[new_v[n] for n in TWIN_WEIGHTS])
```

```python
import functools

import jax
import jax.numpy as jnp
from jax import lax
from jax.experimental import pallas as pl
from jax.experimental.pallas import tpu as pltpu

f32 = jnp.float32
bf16 = jnp.bfloat16
SDS = jax.ShapeDtypeStruct
MESH = pl.DeviceIdType.MESH

NORM_EPS = 1e-6
D_MODEL = 1024
DEPTH = 2
DN_HEADS, DN_DIM, DN_WIDTH, DN_CONV, DN_CHUNK = 4, 128, 512, 4, 64
SB_HEADS, SB_DIM, SB_WIDTH, SB_BLOCK = 4, 64, 256, 128
SG_GROUPS, SG_DIM, SG_WIDTH, SG_CHUNK = 4, 64, 256, 128
D_FF = 4096
IN_DIM = 3336
C_QKV, C_Z, C_AB, C_SB, C_SG, IN_PAD = 0, 1536, 2048, 2176, 2944, 3456
N_CHIPS = 4

ADAM_LR, ADAM_B1, ADAM_B2, ADAM_EPS, ADAM_WD, ADAM_STEP = 0.001, 0.9, 0.999, 1e-08, 0.01, 10

VMEM_LIMIT = 56 * 1024 * 1024


def _cp(sem=None, **kw):
    if sem is not None:
        kw["dimension_semantics"] = sem
    return pltpu.CompilerParams(vmem_limit_bytes=VMEM_LIMIT, **kw)


def _split2(x):
    hi = x.astype(bf16)
    lo = (x - hi.astype(f32)).astype(bf16)
    return hi, lo


def _pdot(a, b):
    return jnp.dot(a, b, preferred_element_type=f32)


def _dot_hp(a, b):
    ah, al = _split2(a)
    bh, bl = _split2(b)
    return _pdot(ah, bh) + _pdot(ah, bl) + _pdot(al, bh)


def _dot_x2c(a, m):
    ah, al = _split2(a)
    return _pdot(ah, m) + _pdot(al, m)


def _dot_cx2(m, a):
    ah, al = _split2(a)
    return _pdot(m, ah) + _pdot(m, al)


NT = (((1,), (1,)), ((), ()))
TN = (((0,), (0,)), ((), ()))


def _nt(a, b):
    return lax.dot_general(a.astype(bf16), b.astype(bf16), NT, preferred_element_type=f32)


def _tn(a, b):
    return lax.dot_general(a.astype(bf16), b.astype(bf16), TN, preferred_element_type=f32)


def _nn(a, b):
    return _pdot(a.astype(bf16), b.astype(bf16))


@jax.custom_vjp
def mm(a, b):
    return _nn(a, b)


mm.defvjp(lambda a, b: (_nn(a, b), (a, b)), lambda r, g: (_nt(g, r[1]), _tn(r[0], g)))


@jax.custom_vjp
def mm_nt(a, b):
    return _nt(a, b)


mm_nt.defvjp(lambda a, b: (_nt(a, b), (a, b)), lambda r, g: (_nn(g, r[1]), _tn(g, r[0])))


@jax.custom_vjp
def mm_tn(a, b):
    return _tn(a, b)


mm_tn.defvjp(lambda a, b: (_tn(a, b), (a, b)), lambda r, g: (_nt(r[1], g), _nn(r[0], g)))


@jax.custom_vjp
def rmul_const(a, m, mt):
    return _dot_x2c(a, m)


rmul_const.defvjp(lambda a, m, mt: (_dot_x2c(a, m), (m, mt)),
                  lambda r, g: (_dot_x2c(g, r[1]), jnp.zeros_like(r[0]), jnp.zeros_like(r[1])))


@jax.custom_vjp
def lmul_const(m, mt, a):
    return _dot_cx2(m, a)


lmul_const.defvjp(lambda m, mt, a: (_dot_cx2(m, a), (m, mt)),
                  lambda r, g: (jnp.zeros_like(r[0]), jnp.zeros_like(r[1]), _dot_cx2(r[1], g)))


@jax.custom_vjp
def mm_hl(t, x):
    th, tl = _split2(t)
    xb = x.astype(bf16)
    return _pdot(th, xb) + _pdot(tl, xb)


def _mm_hl_bwd(r, g):
    t, x = r
    th, tl = _split2(t)
    gb = g.astype(bf16)
    dx = lax.dot_general(th, gb, TN, preferred_element_type=f32) + lax.dot_general(tl, gb, TN, preferred_element_type=f32)
    return _nt(g, x), dx


mm_hl.defvjp(lambda t, x: (mm_hl(t, x), (t, x)), _mm_hl_bwd)


def _inv_unit_lower_raw(lm):
    c = lm.shape[0]
    r = lax.broadcasted_iota(jnp.int32, (c, c), 0)
    cc = lax.broadcasted_iota(jnp.int32, (c, c), 1)
    eye = (r == cc).astype(f32)
    n = -lm
    t = eye + n
    p = n
    k = 1
    while 2 * k < c:
        p = _dot_hp(p, p)
        t = t + _dot_hp(t, p)
        k *= 2
    return t


@jax.custom_vjp
def inv_unit_lower(lm):
    return _inv_unit_lower_raw(lm)


def _inv_bwd(t, g):
    th, tl = _split2(t)
    gh, gl = _split2(g)
    a = (lax.dot_general(th, gh, TN, preferred_element_type=f32) + lax.dot_general(th, gl, TN, preferred_element_type=f32)
         + lax.dot_general(tl, gh, TN, preferred_element_type=f32))
    ah, al = _split2(a)
    b = (lax.dot_general(ah, th, NT, preferred_element_type=f32) + lax.dot_general(ah, tl, NT, preferred_element_type=f32)
         + lax.dot_general(al, th, NT, preferred_element_type=f32))
    return (-b,)


inv_unit_lower.defvjp(lambda lm: (lambda t: (t, t))(_inv_unit_lower_raw(lm)), _inv_bwd)


def _sigmoid(x):
    return 1.0 / (1.0 + jnp.exp(-x))


def _softplus(x):
    return jnp.maximum(x, 0.0) + jnp.log(1.0 + jnp.exp(-jnp.abs(x)))


def _silu(x):
    return x * _sigmoid(x)


def _gelu(x):
    return 0.5 * x * (1.0 + jnp.tanh(0.7978845608028654 * (x + 0.044715 * (x * x * x))))


def _iota2(shape):
    return lax.broadcasted_iota(jnp.int32, shape, 0), lax.broadcasted_iota(jnp.int32, shape, 1)


def _group_avg_mats():
    r, c = _iota2((128, 128))
    return jnp.where((r // 64) == (c // 64), 1.0 / 64.0, 0.0).astype(bf16)


def _pair_norm(x, gain, bavg):
    ms = rmul_const(x * x, bavg, bavg)
    return x * lax.rsqrt(ms + NORM_EPS) * gain


def _rms(x):
    r = lax.rsqrt(jnp.mean(x * x, axis=-1, keepdims=True) + NORM_EPS)
    return r


_IN_GROUPS = ((C_QKV, C_Z), (C_Z, C_AB), (C_AB, C_SB), (C_SB, C_SG), (C_SG, IN_PAD))


def inproj_fwd(x, g, wp, tm=256):
    m = x.shape[0]

    def body(x_ref, g_ref, w_ref, *outs):
        xv = x_ref[...]
        h = (xv * _rms(xv) * g_ref[...]).astype(bf16)
        for (a, b), o in zip(_IN_GROUPS, outs):
            o[...] = _pdot(h, w_ref[:, a:b])

    return pl.pallas_call(
        body, name="inproj_fwd", grid=(m // tm,),
        in_specs=[pl.BlockSpec((tm, D_MODEL), lambda i: (i, 0)), pl.BlockSpec((1, D_MODEL), lambda i: (0, 0)),
                  pl.BlockSpec((D_MODEL, IN_PAD), lambda i: (0, 0))],
        out_specs=[pl.BlockSpec((tm, b - a), lambda i: (i, 0)) for a, b in _IN_GROUPS],
        out_shape=[SDS((m, b - a), f32) for a, b in _IN_GROUPS],
        compiler_params=_cp(("arbitrary",)),
    )(x, g, wp)


def inproj_bwd(x, g, wp, dproj, dres, tm=256):
    m = x.shape[0]

    def body(x_ref, g_ref, w_ref, dp_ref, dr_ref, dx_ref, dg_ref, h_ref):
        xv = x_ref[...]
        r = _rms(xv)
        xn = xv * r
        gv = g_ref[...]
        h_ref[...] = (xn * gv).astype(bf16)
        dh = lax.dot_general(dp_ref[...], w_ref[...], NT, preferred_element_type=f32)
        dxn = dh * gv
        dx_ref[...] = dr_ref[...] + r * (dxn - xn * jnp.mean(dxn * xn, axis=-1, keepdims=True))

        @pl.when(pl.program_id(0) == 0)
        def _():
            dg_ref[...] = jnp.zeros_like(dg_ref)

        dg_ref[...] += jnp.sum(dh * xn, axis=0, keepdims=True)

    return pl.pallas_call(
        body, name="inproj_bwd", grid=(m // tm,),
        in_specs=[pl.BlockSpec((tm, D_MODEL), lambda i: (i, 0)), pl.BlockSpec((1, D_MODEL), lambda i: (0, 0)),
                  pl.BlockSpec((D_MODEL, IN_PAD), lambda i: (0, 0)), pl.BlockSpec((tm, IN_PAD), lambda i: (i, 0)),
                  pl.BlockSpec((tm, D_MODEL), lambda i: (i, 0))],
        out_specs=[pl.BlockSpec((tm, D_MODEL), lambda i: (i, 0)), pl.BlockSpec((1, D_MODEL), lambda i: (0, 0)),
                   pl.BlockSpec((tm, D_MODEL), lambda i: (i, 0))],
        out_shape=[SDS((m, D_MODEL), f32), SDS((1, D_MODEL), f32), SDS((m, D_MODEL), bf16)],
        compiler_params=_cp(("arbitrary",)),
    )(x, g, wp, dproj, dres)


def outproj_fwd(x, odn, osb, osg, wo, tm=512):
    m = x.shape[0]

    def body(x_ref, a_ref, b_ref, c_ref, w_ref, x2_ref, mix_ref):
        mix_ref[:, 0:DN_WIDTH] = a_ref[...].astype(bf16)
        mix_ref[:, DN_WIDTH:DN_WIDTH + SB_WIDTH] = b_ref[...].astype(bf16)
        mix_ref[:, DN_WIDTH + SB_WIDTH:D_MODEL] = c_ref[...].astype(bf16)
        x2_ref[...] = x_ref[...] + _pdot(mix_ref[...], w_ref[...])

    row = lambda w: pl.BlockSpec((tm, w), lambda i: (i, 0))
    return pl.pallas_call(
        body, name="outproj_fwd", grid=(m // tm,),
        in_specs=[row(D_MODEL), row(DN_WIDTH), row(SB_WIDTH), row(SG_WIDTH), pl.BlockSpec((D_MODEL, D_MODEL), lambda i: (0, 0))],
        out_specs=[row(D_MODEL), row(D_MODEL)],
        out_shape=[SDS((m, D_MODEL), f32), SDS((m, D_MODEL), bf16)],
        compiler_params=_cp(("arbitrary",)),
    )(x, odn, osb, osg, wo)


def outproj_bwd(dx2, wo, tm=512):
    m = dx2.shape[0]

    def body(d_ref, w_ref, a_ref, b_ref, c_ref, db_ref):
        db = d_ref[...].astype(bf16)
        db_ref[...] = db
        dm = lax.dot_general(db, w_ref[...], NT, preferred_element_type=f32)
        a_ref[...] = dm[:, 0:DN_WIDTH]
        b_ref[...] = dm[:, DN_WIDTH:DN_WIDTH + SB_WIDTH]
        c_ref[...] = dm[:, DN_WIDTH + SB_WIDTH:D_MODEL]

    row = lambda w: pl.BlockSpec((tm, w), lambda i: (i, 0))
    return pl.pallas_call(
        body, name="outproj_bwd", grid=(m // tm,),
        in_specs=[row(D_MODEL), pl.BlockSpec((D_MODEL, D_MODEL), lambda i: (0, 0))],
        out_specs=[row(DN_WIDTH), row(SB_WIDTH), row(SG_WIDTH), row(D_MODEL)],
        out_shape=[SDS((m, DN_WIDTH), f32), SDS((m, SB_WIDTH), f32), SDS((m, SG_WIDTH), f32), SDS((m, D_MODEL), bf16)],
        compiler_params=_cp(("arbitrary",)),
    )(dx2, wo)


FF_CHUNK = 1024


def _load_weights_once(pairs, sem):
    @pl.when(pl.program_id(0) == 0)
    def _():
        cps = [pltpu.make_async_copy(h, v, sem.at[i]) for i, (h, v) in enumerate(pairs)]
        for c in cps:
            c.start()
        for c in cps:
            c.wait()


def ffn_fwd(x2, g, w1, w2, tm=256):
    m = x2.shape[0]

    def body(x_ref, g_ref, w1_hbm, w2_hbm, y_ref, w1_v, w2_v, sem):
        _load_weights_once(((w1_hbm, w1_v), (w2_hbm, w2_v)), sem)
        xv = x_ref[...]
        h = (xv * _rms(xv) * g_ref[...]).astype(bf16)
        acc = xv
        for j in range(0, D_FF, FF_CHUNK):
            f = _pdot(h, w1_v[:, j:j + FF_CHUNK])
            rl = jnp.maximum(f, 0.0)
            acc = acc + _pdot((rl * rl).astype(bf16), w2_v[j:j + FF_CHUNK, :])
        y_ref[...] = acc

    return pl.pallas_call(
        body, name="ffn_fwd", grid=(m // tm,),
        in_specs=[pl.BlockSpec((tm, D_MODEL), lambda i: (i, 0)), pl.BlockSpec((1, D_MODEL), lambda i: (0, 0)),
                  pl.BlockSpec(memory_space=pl.ANY), pl.BlockSpec(memory_space=pl.ANY)],
        out_specs=pl.BlockSpec((tm, D_MODEL), lambda i: (i, 0)),
        out_shape=SDS((m, D_MODEL), f32),
        scratch_shapes=[pltpu.VMEM((D_MODEL, D_FF), bf16), pltpu.VMEM((D_FF, D_MODEL), bf16), pltpu.SemaphoreType.DMA((2,))],
        compiler_params=_cp(("arbitrary",)),
    )(x2, g, w1, w2)


def ffn_bwd(x2, g, w1, w2, dy, tm=256):
    m = x2.shape[0]

    def body(x_ref, g_ref, w1_hbm, w2_hbm, dy_ref, dx_ref, dg_ref, h_ref, a_ref, df_ref, dyb_ref, w1_v, w2_v, sem):
        _load_weights_once(((w1_hbm, w1_v), (w2_hbm, w2_v)), sem)
        xv = x_ref[...]
        r = _rms(xv)
        xn = xv * r
        gv = g_ref[...]
        h = (xn * gv).astype(bf16)
        h_ref[...] = h
        dyv = dy_ref[...]
        dyb = dyv.astype(bf16)
        dyb_ref[...] = dyb
        dh = jnp.zeros((tm, D_MODEL), f32)
        for j in range(0, D_FF, FF_CHUNK):
            f = _pdot(h, w1_v[:, j:j + FF_CHUNK])
            rl = jnp.maximum(f, 0.0)
            a_ref[:, j:j + FF_CHUNK] = (rl * rl).astype(bf16)
            da = lax.dot_general(dyb, w2_v[j:j + FF_CHUNK, :], NT, preferred_element_type=f32)
            df = (da * (2.0 * rl)).astype(bf16)
            df_ref[:, j:j + FF_CHUNK] = df
            dh = dh + lax.dot_general(df, w1_v[:, j:j + FF_CHUNK], NT, preferred_element_type=f32)
        dxn = dh * gv
        dx_ref[...] = dyv + r * (dxn - xn * jnp.mean(dxn * xn, axis=-1, keepdims=True))

        @pl.when(pl.program_id(0) == 0)
        def _():
            dg_ref[...] = jnp.zeros_like(dg_ref)

        dg_ref[...] += jnp.sum(dh * xn, axis=0, keepdims=True)

    row = lambda w: pl.BlockSpec((tm, w), lambda i: (i, 0))
    return pl.pallas_call(
        body, name="ffn_bwd", grid=(m // tm,),
        in_specs=[row(D_MODEL), pl.BlockSpec((1, D_MODEL), lambda i: (0, 0)),
                  pl.BlockSpec(memory_space=pl.ANY), pl.BlockSpec(memory_space=pl.ANY), row(D_MODEL)],
        out_specs=[row(D_MODEL), pl.BlockSpec((1, D_MODEL), lambda i: (0, 0)), row(D_MODEL), row(D_FF), row(D_FF), row(D_MODEL)],
        out_shape=[SDS((m, D_MODEL), f32), SDS((1, D_MODEL), f32), SDS((m, D_MODEL), bf16), SDS((m, D_FF), bf16),
                   SDS((m, D_FF), bf16), SDS((m, D_MODEL), bf16)],
        scratch_shapes=[pltpu.VMEM((D_MODEL, D_FF), bf16), pltpu.VMEM((D_FF, D_MODEL), bf16), pltpu.SemaphoreType.DMA((2,))],
        compiler_params=_cp(("arbitrary",)),
    )(x2, g, w1, w2, dy)


def _tile(n, cap):
    best = 128
    for t in range(128, cap + 1, 128):
        if n % t == 0:
            best = t
    return best


def tn_matmul(a, b, name, col_shards=1, tk=512):
    m, ka = a.shape
    n = b.shape[1]
    ti = _tile(ka, 1024)
    tj = _tile(n // col_shards, 1152)
    nk = m // tk
    jps = (n // col_shards) // tj

    def body(a_ref, b_ref, o_ref, acc):
        k = pl.program_id(2)

        @pl.when(k == 0)
        def _():
            acc[...] = jnp.zeros_like(acc)

        acc[...] += lax.dot_general(a_ref[...], b_ref[...], TN, preferred_element_type=f32)

        @pl.when(k == nk - 1)
        def _():
            o_ref[...] = acc[...].astype(bf16).reshape(o_ref.shape)

    if col_shards == 1:
        out_shape, out_spec = SDS((ka, n), bf16), pl.BlockSpec((ti, tj), lambda i, j, k: (i, j))
    else:
        out_shape = SDS((col_shards, ka, n // col_shards), bf16)
        out_spec = pl.BlockSpec((1, ti, tj), lambda i, j, k: (j // jps, i, j % jps))
    return pl.pallas_call(
        body, name=name, grid=(ka // ti, n // tj, nk),
        in_specs=[pl.BlockSpec((tk, ti), lambda i, j, k: (k, i)), pl.BlockSpec((tk, tj), lambda i, j, k: (k, j))],
        out_specs=out_spec, out_shape=out_shape,
        scratch_shapes=[pltpu.VMEM((ti, tj), f32)],
        compiler_params=_cp(("arbitrary", "arbitrary", "arbitrary")),
    )(a, b)


def loss_head(y, tgt, tm=512):
    m = y.shape[0]

    def body(y_ref, t_ref, dy_ref, l_ref):
        e = y_ref[...] - t_ref[...]
        dy_ref[...] = e * (1.0 / D_MODEL)

        @pl.when(pl.program_id(0) == 0)
        def _():
            l_ref[...] = jnp.zeros_like(l_ref)

        l_ref[...] += jnp.sum(e * e, axis=0, keepdims=True) * (0.5 / D_MODEL)

    row = pl.BlockSpec((tm, D_MODEL), lambda i: (i, 0))
    return pl.pallas_call(
        body, name="loss_head", grid=(m // tm,), in_specs=[row, row],
        out_specs=[row, pl.BlockSpec((1, D_MODEL), lambda i: (0, 0))],
        out_shape=[SDS((m, D_MODEL), f32), SDS((1, D_MODEL), f32)],
        compiler_params=_cp(("arbitrary",)),
    )(y, tgt)


def _dn_consts():
    c = DN_CHUNK
    r, cc = _iota2((c, c))
    lt = (cc <= r).astype(bf16)
    ltt = (r <= cc).astype(bf16)
    return lt, ltt


def dn_chunk(cq, ck, cv, a, b, z, s, alog, dtb, gain, lt, ltt):
    c = DN_CHUNK
    r, cc = _iota2((c, c))
    q = cq * lax.rsqrt(jnp.sum(cq * cq, axis=-1, keepdims=True) + NORM_EPS) * (DN_DIM ** -0.5)
    k = ck * lax.rsqrt(jnp.sum(ck * ck, axis=-1, keepdims=True) + NORM_EPS)
    g = -jnp.exp(alog) * _softplus(a + dtb)
    beta = _sigmoid(b)
    r2, c2 = _iota2((c, 128))
    uaug = jnp.where((c2 < c) & (r2 > c2), 1.0, 0.0) + jnp.where(c2 == c, 1.0, 0.0)
    gam_all = lmul_const(lt, ltt, g * uaug)
    gam_cc = gam_all[:, 0:c]
    gam = gam_all[:, c:c + 1]
    dec = jnp.where(cc <= r, jnp.exp(jnp.where(cc <= r, gam_cc, 0.0)), 0.0)
    kk = mm_nt(k, k)
    lm = jnp.where(cc < r, beta * kk * dec, 0.0)
    t = inv_unit_lower(lm)
    eg = jnp.exp(gam)
    sol = mm_hl(t, jnp.concatenate([cv * beta, k * (beta * eg)], axis=1))
    u, w = sol[:, 0:DN_DIM], sol[:, DN_DIM:2 * DN_DIM]
    qk = jnp.where(cc <= r, mm_nt(q, k) * dec, 0.0)
    glast = jnp.sum(g, axis=0, keepdims=True)
    qd = q * eg
    kd = k * jnp.exp(glast - gam)
    un = u - mm(w, s)
    o = mm(qd, s) + mm(qk, un)
    s_new = s * jnp.exp(glast) + mm_tn(kd, un)
    on = o * lax.rsqrt(jnp.mean(o * o, axis=-1, keepdims=True) + NORM_EPS) * gain * _silu(z)
    return on, s_new


def _conv_rows(xe_ref, b, w_ref):
    y = w_ref[0:1, :] * xe_ref[b, pl.ds(5, DN_CHUNK), :]
    for i in range(1, DN_CONV):
        y = y + w_ref[i:i + 1, :] * xe_ref[b, pl.ds(5 + i, DN_CHUNK), :]
    return y


def dn_fwd(qkv, z, ab, conv_w, alog, dtb, gain):
    bsz, t, _ = qkv.shape
    nc = t // DN_CHUNK
    c = DN_CHUNK
    nh = bsz * DN_HEADS

    def body(qkv_ref, z_ref, ab_ref, w_ref, al_ref, dt_ref, g_ref, o_ref, sall_ref, xe, s_sc):
        n = pl.program_id(0)

        @pl.when(n == 0)
        def _():
            xe[:, 0:8, :] = jnp.zeros((bsz, 8, 3 * DN_WIDTH), f32)
            s_sc[...] = jnp.zeros_like(s_sc)

        lt, ltt = _dn_consts()
        for b in range(bsz):
            xe[b, 8:8 + c, :] = qkv_ref[b]
            cact = _silu(_conv_rows(xe, b, w_ref))
            xe[b, 0:8, :] = xe[b, c:c + 8, :]
            for h in range(DN_HEADS):
                hs = slice(h * DN_DIM, (h + 1) * DN_DIM)
                i = b * DN_HEADS + h
                s = s_sc[i]
                sall_ref[0, i] = s
                on, sn = dn_chunk(cact[:, h * DN_DIM:(h + 1) * DN_DIM], cact[:, DN_WIDTH + h * DN_DIM:DN_WIDTH + (h + 1) * DN_DIM],
                                  cact[:, 2 * DN_WIDTH + h * DN_DIM:2 * DN_WIDTH + (h + 1) * DN_DIM],
                                  ab_ref[b, :, h:h + 1], ab_ref[b, :, DN_HEADS + h:DN_HEADS + h + 1], z_ref[b, :, hs], s,
                                  al_ref[0:1, h:h + 1], dt_ref[0:1, h:h + 1], g_ref[...], lt, ltt)
                o_ref[b, :, hs] = on
                s_sc[i] = sn

    blk = lambda w: pl.BlockSpec((bsz, c, w), lambda n: (0, n, 0))
    full = lambda shp: pl.BlockSpec(shp, lambda n: (0,) * len(shp))
    return pl.pallas_call(
        body, name="dn_fwd", grid=(nc,),
        in_specs=[blk(3 * DN_WIDTH), blk(DN_WIDTH), blk(128), full((8, 3 * DN_WIDTH)), full((1, 128)), full((1, 128)), full((1, 128))],
        out_specs=[blk(DN_WIDTH), pl.BlockSpec((1, nh, DN_DIM, DN_DIM), lambda n: (n, 0, 0, 0))],
        out_shape=[SDS((bsz, t, DN_WIDTH), f32), SDS((nc, nh, DN_DIM, DN_DIM), f32)],
        scratch_shapes=[pltpu.VMEM((bsz, c + 8, 3 * DN_WIDTH), f32), pltpu.VMEM((nh, DN_DIM, DN_DIM), f32)],
        compiler_params=_cp(("arbitrary",)),
    )(qkv, z, ab, conv_w, alog, dtb, gain)


def dn_bwd(qkv, z, ab, conv_w, alog, dtb, gain, sall, do):
    bsz, t, _ = qkv.shape
    nc = t // DN_CHUNK
    c = DN_CHUNK
    nh = bsz * DN_HEADS
    w3 = 3 * DN_WIDTH

    def body(qkv_ref, prev_ref, z_ref, ab_ref, w_ref, al_ref, dt_ref, g_ref, sall_ref, do_ref,
             dqkv_ref, dz_ref, dab_ref, dw_ref, dal_ref, ddt_ref, dg_ref, xe, dye, dc_sc, ds_sc):
        n = pl.program_id(0)
        first = (nc - 1 - n) == 0

        @pl.when(n == 0)
        def _():
            dye[:, c:c + 8, :] = jnp.zeros((bsz, 8, w3), f32)
            ds_sc[...] = jnp.zeros_like(ds_sc)
            dw_ref[...] = jnp.zeros_like(dw_ref)
            dal_ref[...] = jnp.zeros_like(dal_ref)
            ddt_ref[...] = jnp.zeros_like(ddt_ref)
            dg_ref[...] = jnp.zeros_like(dg_ref)

        lt, ltt = _dn_consts()
        lane = lax.broadcasted_iota(jnp.int32, (1, 128), 1)
        lane_c = lax.broadcasted_iota(jnp.int32, (c, 128), 1)
        for b in range(bsz):
            xe[b, 0:8, :] = jnp.where(first, 0.0, prev_ref[b])
            xe[b, 8:8 + c, :] = qkv_ref[b]
            y = _conv_rows(xe, b, w_ref)
            sig = _sigmoid(y)
            cact = y * sig
            dab = jnp.zeros((c, 128), f32)
            for h in range(DN_HEADS):
                hs = slice(h * DN_DIM, (h + 1) * DN_DIM)
                i = b * DN_HEADS + h
                prim = (cact[:, h * DN_DIM:(h + 1) * DN_DIM], cact[:, DN_WIDTH + h * DN_DIM:DN_WIDTH + (h + 1) * DN_DIM],
                        cact[:, 2 * DN_WIDTH + h * DN_DIM:2 * DN_WIDTH + (h + 1) * DN_DIM],
                        ab_ref[b, :, h:h + 1], ab_ref[b, :, DN_HEADS + h:DN_HEADS + h + 1], z_ref[b, :, hs], sall_ref[0, i],
                        al_ref[0:1, h:h + 1], dt_ref[0:1, h:h + 1], g_ref[...])
                _, vjp = jax.vjp(lambda *p: dn_chunk(*p, lt, ltt), *prim)
                dcq, dck, dcv, da, db, dzz, dsp, dal, ddt, dgn = vjp((do_ref[b, :, hs], ds_sc[i]))
                ds_sc[i] = dsp
                dc_sc[:, h * DN_DIM:(h + 1) * DN_DIM] = dcq
                dc_sc[:, DN_WIDTH + h * DN_DIM:DN_WIDTH + (h + 1) * DN_DIM] = dck
                dc_sc[:, 2 * DN_WIDTH + h * DN_DIM:2 * DN_WIDTH + (h + 1) * DN_DIM] = dcv
                dz_ref[b, :, hs] = dzz.astype(bf16)
                dab = dab + jnp.where(lane_c == h, da, 0.0) + jnp.where(lane_c == DN_HEADS + h, db, 0.0)
                dal_ref[...] += jnp.where(lane == h, dal, 0.0)
                ddt_ref[...] += jnp.where(lane == h, ddt, 0.0)
                dg_ref[...] += dgn
            dab_ref[b] = dab.astype(bf16)
            dy = dc_sc[...] * (sig * (1.0 + y * (1.0 - sig)))
            dye[b, 0:c, :] = dy
            dx = w_ref[3:4, :] * dy
            for i in range(DN_CONV - 1):
                dx = dx + w_ref[i:i + 1, :] * dye[b, pl.ds(3 - i, c), :]
            dqkv_ref[b] = dx.astype(bf16)
            for i in range(DN_CONV):
                dw_ref[i:i + 1, :] += jnp.sum(dy * xe[b, pl.ds(5 + i, c), :], axis=0, keepdims=True)
            dye[b, c:c + 8, :] = dye[b, 0:8, :]

    rev = lambda w: pl.BlockSpec((bsz, c, w), lambda n: (0, nc - 1 - n, 0))
    full = lambda shp: pl.BlockSpec(shp, lambda n: (0,) * len(shp))
    prev = pl.BlockSpec((bsz, 8, w3), lambda n: (0, jnp.maximum((nc - 1 - n) * (c // 8) - 1, 0), 0))
    return pl.pallas_call(
        body, name="dn_bwd", grid=(nc,),
        in_specs=[rev(w3), prev, rev(DN_WIDTH), rev(128), full((8, w3)), full((1, 128)), full((1, 128)), full((1, 128)),
                  pl.BlockSpec((1, nh, DN_DIM, DN_DIM), lambda n: (nc - 1 - n, 0, 0, 0)), rev(DN_WIDTH)],
        out_specs=[rev(w3), rev(DN_WIDTH), rev(128), full((8, w3)), full((1, 128)), full((1, 128)), full((1, 128))],
        out_shape=[SDS((bsz, t, w3), bf16), SDS((bsz, t, DN_WIDTH), bf16), SDS((bsz, t, 128), bf16),
                   SDS((8, w3), f32), SDS((1, 128), f32), SDS((1, 128), f32), SDS((1, 128), f32)],
        scratch_shapes=[pltpu.VMEM((bsz, c + 8, w3), f32), pltpu.VMEM((bsz, c + 8, w3), f32), pltpu.VMEM((c, w3), f32),
                        pltpu.VMEM((nh, DN_DIM, DN_DIM), f32)],
        compiler_params=_cp(("arbitrary",)),
    )(qkv, qkv, z, ab, conv_w, alog, dtb, gain, sall, do)


def _sb_consts():
    r, c = _iota2((SB_BLOCK, SB_BLOCK))
    return r, c


def sb_fwd(sbqkv, gq, gk):
    bsz, t, _ = sbqkv.shape
    nq = t // SB_BLOCK
    blk = SB_BLOCK
    scale = SB_DIM ** -0.5

    def body(q_ref, k_ref, v_ref, gq_ref, gk_ref, o_ref, l_ref, qa_sc, kn_sc, va_sc):
        bavg = _group_avg_mats()
        lane = lax.broadcasted_iota(jnp.int32, (1, 128), 1)
        qn = _pair_norm(q_ref[0], gq_ref[...], bavg)
        kn_sc[...] = _pair_norm(k_ref[0], gk_ref[...], bavg).astype(bf16)
        vv = v_ref[0]
        for hh in range(2):
            sel = (lane // SB_DIM) == hh
            qa_sc[hh] = jnp.where(sel, qn, 0.0).astype(bf16)
            va_sc[hh] = jnp.where(sel, vv, 0.0).astype(bf16)
        r, c = _sb_consts()
        ustrict = (r > c).astype(bf16)

        def qloop(qi, _):
            qs = pl.multiple_of(qi * blk, blk)
            acc_tot = jnp.zeros((blk, 128), f32)
            ltot = jnp.zeros((blk, 128), f32)
            for hh in range(2):
                qa = qa_sc[hh, pl.ds(qs, blk), :]

                def kloop(i, carry):
                    acc, rr = carry
                    ks = pl.multiple_of((qi - i) * blk, blk)
                    zz = lax.dot_general(qa, kn_sc[pl.ds(ks, blk), :], NT, preferred_element_type=f32) * scale
                    sp = _softplus(zz)
                    mask = (ks + c) < (qs + r)
                    lm = jnp.where(mask, -sp, 0.0)
                    rem = _dot_x2c(lm, ustrict)
                    wgt = jnp.where(mask, jnp.exp(zz - sp + rem + rr), 0.0)
                    acc = acc + _pdot(wgt.astype(bf16), va_sc[hh, pl.ds(ks, blk), :])
                    rr = rr + jnp.sum(lm, axis=1, keepdims=True)
                    return acc, rr

                acc, rr = lax.fori_loop(0, qi + 1, kloop, (jnp.zeros((blk, 128), f32), jnp.zeros((blk, 1), f32)))
                acc_tot = acc_tot + acc
                ltot = ltot + jnp.where((lane // SB_DIM) == hh, rr, 0.0)
            o_ref[0, pl.ds(qs, blk), :] = acc_tot
            l_ref[0, pl.ds(qs, blk), :] = ltot
            return 0

        lax.fori_loop(0, nq, qloop, 0)

    col = lambda off: pl.BlockSpec((1, t, 128), lambda b, p: (b, 0, off + p))
    gsp = pl.BlockSpec((1, 128), lambda b, p: (0, 0))
    return pl.pallas_call(
        body, name="sb_fwd", grid=(bsz, 2),
        in_specs=[col(0), col(2), col(4), gsp, gsp],
        out_specs=[col(0), col(0)],
        out_shape=[SDS((bsz, t, SB_WIDTH), f32), SDS((bsz, t, SB_WIDTH), f32)],
        scratch_shapes=[pltpu.VMEM((2, t, 128), bf16), pltpu.VMEM((t, 128), bf16), pltpu.VMEM((2, t, 128), bf16)],
        compiler_params=_cp(("arbitrary", "arbitrary")),
    )(sbqkv, sbqkv, sbqkv, gq, gk)


def sb_bwd(sbqkv, gq, gk, ltot, do):
    bsz, t, _ = sbqkv.shape
    nq = t // SB_BLOCK
    blk = SB_BLOCK
    scale = SB_DIM ** -0.5

    def body(q_ref, k_ref, v_ref, gq_ref, gk_ref, l_ref, do_ref, dq_ref, dk_ref, dv_ref, dgq_ref, dgk_ref,
             qa_sc, kn_sc, va_sc, doa_sc, dqn_sc, dkn_sc, dv_sc):
        bavg = _group_avg_mats()
        lane = lax.broadcasted_iota(jnp.int32, (1, 128), 1)
        fq = lambda x, g: _pair_norm(x, g, bavg)
        qn, q_vjp = jax.vjp(fq, q_ref[0], gq_ref[...])
        kn, k_vjp = jax.vjp(fq, k_ref[0], gk_ref[...])
        kn_sc[...] = kn.astype(bf16)
        vv = v_ref[0]
        dov = do_ref[0]
        for hh in range(2):
            sel = (lane // SB_DIM) == hh
            qa_sc[hh] = jnp.where(sel, qn, 0.0).astype(bf16)
            va_sc[hh] = jnp.where(sel, vv, 0.0).astype(bf16)
            doa_sc[hh] = jnp.where(sel, dov, 0.0).astype(bf16)
        dkn_sc[...] = jnp.zeros_like(dkn_sc)
        dv_sc[...] = jnp.zeros_like(dv_sc)
        r, c = _sb_consts()
        pincl = (r <= c).astype(bf16)
        pstrict = (r < c).astype(bf16)

        def qloop(qi, _):
            qs = pl.multiple_of(qi * blk, blk)
            dq_tot = jnp.zeros((blk, 128), f32)
            for hh in range(2):
                sel = (lane // SB_DIM) == hh
                qa = qa_sc[hh, pl.ds(qs, blk), :]
                doa = doa_sc[hh, pl.ds(qs, blk), :]
                lt = l_ref[0, pl.ds(qs, blk), hh * SB_DIM:hh * SB_DIM + 1]

                def kloop(kj, carry):
                    dq, cs, ce = carry
                    ks = pl.multiple_of(kj * blk, blk)
                    kb = kn_sc[pl.ds(ks, blk), :]
                    zz = lax.dot_general(qa, kb, NT, preferred_element_type=f32) * scale
                    sp = _softplus(zz)
                    mask = (ks + c) < (qs + r)
                    lm = jnp.where(mask, -sp, 0.0)
                    pre = _dot_x2c(lm, pincl)
                    lp = zz - sp
                    wgt = jnp.where(mask, jnp.exp(lp + (lt - cs - pre)), 0.0)
                    dw = lax.dot_general(doa, va_sc[hh, pl.ds(ks, blk), :], NT, preferred_element_type=f32)
                    e = wgt * dw
                    ee = ce + _dot_x2c(e, pstrict)
                    sig = jnp.exp(lp)
                    dz = (jnp.where(mask, e * (1.0 - sig) - ee * sig, 0.0) * scale).astype(bf16)
                    dq = dq + _pdot(dz, kb)
                    dkn_sc[pl.ds(ks, blk), :] += lax.dot_general(dz, qa, TN, preferred_element_type=f32)
                    dv_sc[pl.ds(ks, blk), :] += lax.dot_general(wgt.astype(bf16), doa, TN, preferred_element_type=f32)
                    return dq, cs + jnp.sum(lm, axis=1, keepdims=True), ce + jnp.sum(e, axis=1, keepdims=True)

                z1 = jnp.zeros((blk, 1), f32)
                dq, _, _ = lax.fori_loop(0, qi + 1, kloop, (jnp.zeros((blk, 128), f32), z1, z1))
                dq_tot = dq_tot + jnp.where(sel, dq, 0.0)
            dqn_sc[pl.ds(qs, blk), :] = dq_tot
            return 0

        lax.fori_loop(0, nq, qloop, 0)
        dq_pre, dgq = q_vjp(dqn_sc[...])
        dk_pre, dgk = k_vjp(dkn_sc[...])
        dq_ref[0] = dq_pre.astype(bf16)
        dk_ref[0] = dk_pre.astype(bf16)
        dv_ref[0] = dv_sc[...].astype(bf16)
        dgq_ref[0] = jnp.broadcast_to(dgq, (8, 128))
        dgk_ref[0] = jnp.broadcast_to(dgk, (8, 128))

    col = lambda off: pl.BlockSpec((1, t, 128), lambda b, p: (b, 0, off + p))
    gsp = pl.BlockSpec((1, 128), lambda b, p: (0, 0))
    gout = pl.BlockSpec((1, 8, 128), lambda b, p: (b * 2 + p, 0, 0))
    return pl.pallas_call(
        body, name="sb_bwd", grid=(bsz, 2),
        in_specs=[col(0), col(2), col(4), gsp, gsp, col(0), col(0)],
        out_specs=[col(0), col(0), col(0), gout, gout],
        out_shape=[SDS((bsz, t, SB_WIDTH), bf16)] * 3 + [SDS((bsz * 2, 8, 128), f32)] * 2,
        scratch_shapes=[pltpu.VMEM((2, t, 128), bf16), pltpu.VMEM((t, 128), bf16), pltpu.VMEM((2, t, 128), bf16),
                        pltpu.VMEM((2, t, 128), bf16), pltpu.VMEM((t, 128), f32), pltpu.VMEM((t, 128), f32), pltpu.VMEM((t, 128), f32)],
        compiler_params=_cp(("arbitrary", "arbitrary")),
    )(sbqkv, sbqkv, sbqkv, gq, gk, ltot, do)


def sg_pair(u, v, gain, wa, wb, ba, bb, bavg):
    r, c = _iota2((SG_CHUNK, SG_CHUNK))
    lane = lax.broadcasted_iota(jnp.int32, (1, 128), 1)
    first = lane < SG_DIM
    vn = _pair_norm(_gelu(v), gain, bavg)
    tri = c <= r
    mixed = (mm(jnp.where(tri, wa, 0.0), jnp.where(first, vn, 0.0)) + mm(jnp.where(tri, wb, 0.0), jnp.where(first, 0.0, vn))
             + jnp.where(first, ba, bb))
    return _gelu(u) * mixed


def sg_fwd(sguv, gain, w, bt):
    bsz, t, _ = sguv.shape
    nch = t // SG_CHUNK

    def body(uv_ref, g_ref, w_ref, b_ref, o_ref):
        bavg = _group_avg_mats()
        for p in range(2):
            ls = slice(p * 128, (p + 1) * 128)
            o_ref[0, :, ls] = sg_pair(uv_ref[0, :, ls], uv_ref[0, :, SG_WIDTH + p * 128:SG_WIDTH + (p + 1) * 128], g_ref[:, ls],
                                      w_ref[2 * p], w_ref[2 * p + 1], b_ref[:, 2 * p:2 * p + 1], b_ref[:, 2 * p + 1:2 * p + 2], bavg)

    full = lambda shp: pl.BlockSpec(shp, lambda b, n: (0,) * len(shp))
    return pl.pallas_call(
        body, name="sg_fwd", grid=(bsz, nch),
        in_specs=[pl.BlockSpec((1, SG_CHUNK, 2 * SG_WIDTH), lambda b, n: (b, n, 0)), full((1, SG_WIDTH)),
                  full((SG_GROUPS, SG_CHUNK, SG_CHUNK)), full((SG_CHUNK, 128))],
        out_specs=pl.BlockSpec((1, SG_CHUNK, SG_WIDTH), lambda b, n: (b, n, 0)),
        out_shape=SDS((bsz, t, SG_WIDTH), f32),
        compiler_params=_cp(("arbitrary", "arbitrary")),
    )(sguv, gain, w, bt)


def sg_bwd(sguv, gain, w, bt, do):
    bsz, t, _ = sguv.shape
    nch = t // SG_CHUNK

    def body(uv_ref, g_ref, w_ref, b_ref, do_ref, duv_ref, dg_ref, dw_ref, db_ref):
        @pl.when((pl.program_id(0) == 0) & (pl.program_id(1) == 0))
        def _():
            dg_ref[...] = jnp.zeros_like(dg_ref)
            dw_ref[...] = jnp.zeros_like(dw_ref)
            db_ref[...] = jnp.zeros_like(db_ref)

        bavg = _group_avg_mats()
        lane = lax.broadcasted_iota(jnp.int32, (SG_CHUNK, 128), 1)
        dbt = jnp.zeros((SG_CHUNK, 128), f32)
        for p in range(2):
            ls = slice(p * 128, (p + 1) * 128)
            vs = slice(SG_WIDTH + p * 128, SG_WIDTH + (p + 1) * 128)
            prim = (uv_ref[0, :, ls], uv_ref[0, :, vs], g_ref[:, ls], w_ref[2 * p], w_ref[2 * p + 1],
                    b_ref[:, 2 * p:2 * p + 1], b_ref[:, 2 * p + 1:2 * p + 2])
            _, vjp = jax.vjp(lambda *a: sg_pair(*a, bavg), *prim)
            du, dv, dgn, dwa, dwb, dba, dbb = vjp(do_ref[0, :, ls])
            duv_ref[0, :, ls] = du.astype(bf16)
            duv_ref[0, :, vs] = dv.astype(bf16)
            dg_ref[:, ls] += dgn
            dw_ref[2 * p] += dwa
            dw_ref[2 * p + 1] += dwb
            dbt = dbt + jnp.where(lane == 2 * p, dba, 0.0) + jnp.where(lane == 2 * p + 1, dbb, 0.0)
        db_ref[...] += dbt

    full = lambda shp: pl.BlockSpec(shp, lambda b, n: (0,) * len(shp))
    return pl.pallas_call(
        body, name="sg_bwd", grid=(bsz, nch),
        in_specs=[pl.BlockSpec((1, SG_CHUNK, 2 * SG_WIDTH), lambda b, n: (b, n, 0)), full((1, SG_WIDTH)),
                  full((SG_GROUPS, SG_CHUNK, SG_CHUNK)), full((SG_CHUNK, 128)),
                  pl.BlockSpec((1, SG_CHUNK, SG_WIDTH), lambda b, n: (b, n, 0))],
        out_specs=[pl.BlockSpec((1, SG_CHUNK, 2 * SG_WIDTH), lambda b, n: (b, n, 0)), full((1, SG_WIDTH)),
                   full((SG_GROUPS, SG_CHUNK, SG_CHUNK)), full((SG_CHUNK, 128))],
        out_shape=[SDS((bsz, t, 2 * SG_WIDTH), bf16), SDS((1, SG_WIDTH), f32), SDS((SG_GROUPS, SG_CHUNK, SG_CHUNK), f32),
                   SDS((SG_CHUNK, 128), f32)],
        compiler_params=_cp(("arbitrary", "arbitrary")),
    )(sguv, gain, w, bt, do)


def _pad_lanes(v, n=128):
    return jnp.pad(v.reshape(1, -1), ((0, 0), (0, n - v.size)))


def pad_w_in(w):
    return jnp.concatenate([w[:, 0:2048], jnp.pad(w[:, 2048:2056], ((0, 0), (0, 120))), w[:, 2056:]], axis=1)


def unpad_w_in(w):
    return jnp.concatenate([w[:, 0:2048], w[:, C_AB:C_AB + 8], w[:, C_SB:]], axis=1)


def layer_params(p, l):
    return dict(
        g1=p["norm1_g"][l].reshape(1, -1), g2=p["norm2_g"][l].reshape(1, -1),
        conv=jnp.pad(p["conv_w"][l], ((0, 4), (0, 0))), alog=_pad_lanes(p["a_log"][l]), dtb=_pad_lanes(p["dt_bias"][l]),
        dng=p["dn_out_g"][l].reshape(1, -1), gq=jnp.tile(p["sb_q_g"][l].reshape(1, -1), (1, 2)),
        gk=jnp.tile(p["sb_k_g"][l].reshape(1, -1), (1, 2)), sgg=p["sg_v_g"][l].reshape(1, -1), sgw=p["sg_w"][l],
        sgb=jnp.pad(p["sg_b"][l].T, ((0, 0), (0, 124))))


def local_step(x, tgt, small, big):
    bsz, t, _ = x.shape
    m = bsz * t
    r3 = lambda a: a.reshape(bsz, t, a.shape[-1])
    r2 = lambda a: a.reshape(m, a.shape[-1])
    xs, saved = x.reshape(m, D_MODEL), []
    for l in range(DEPTH):
        sp, w = layer_params(small, l), big[l]
        qkv, z, ab, sb, sg = inproj_fwd(xs, sp["g1"], w["w_in"])
        odn, sall = dn_fwd(r3(qkv), r3(z), r3(ab), sp["conv"], sp["alog"], sp["dtb"], sp["dng"])
        osb, ltot = sb_fwd(r3(sb), sp["gq"], sp["gk"])
        osg = sg_fwd(r3(sg), sp["sgg"], sp["sgw"], sp["sgb"])
        x2, mix = outproj_fwd(xs, r2(odn), r2(osb), r2(osg), w["w_out"])
        x3 = ffn_fwd(x2, sp["g2"], w["w_ff1"], w["w_ff2"])
        saved.append(dict(x=xs, qkv=qkv, z=z, ab=ab, sb=sb, sg=sg, sall=sall, ltot=ltot, mix=mix, x2=x2))
        xs = x3
    dx, lossp = loss_head(xs, tgt.reshape(m, D_MODEL))
    gbig, gsmall = [None] * DEPTH, [None] * DEPTH
    for l in reversed(range(DEPTH)):
        sp, w, s = layer_params(small, l), big[l], saved[l]
        dx2, dg2, h2, act, df, dyb = ffn_bwd(s["x2"], sp["g2"], w["w_ff1"], w["w_ff2"], dx)
        g_ff1 = tn_matmul(h2, df, f"dw_ff1_{l}", col_shards=N_CHIPS)
        g_ff2 = tn_matmul(act, dyb, f"dw_ff2_{l}")
        dodn, dosb, dosg, dx2b = outproj_bwd(dx2, w["w_out"])
        g_out = tn_matmul(s["mix"], dx2b, f"dw_out_{l}")
        dqkv, dz, dab, dconv, dalog, ddtb, ddng = dn_bwd(r3(s["qkv"]), r3(s["z"]), r3(s["ab"]), sp["conv"], sp["alog"], sp["dtb"],
                                                        sp["dng"], s["sall"], r3(dodn))
        dsq, dsk, dsv, dgq, dgk = sb_bwd(r3(s["sb"]), sp["gq"], sp["gk"], s["ltot"], r3(dosb))
        dsg, dsgg, dsgw, dsgb = sg_bwd(r3(s["sg"]), sp["sgg"], sp["sgw"], sp["sgb"], r3(dosg))
        dproj = jnp.concatenate([r2(dqkv), r2(dz), r2(dab), r2(dsq), r2(dsk), r2(dsv), r2(dsg)], axis=1)
        dx, dg1, h1 = inproj_bwd(s["x"], sp["g1"], w["w_in"], dproj, dx2)
        g_in = tn_matmul(h1, dproj, f"dw_in_{l}")
        gbig[l] = dict(w_in=g_in, w_out=g_out, w_ff1=g_ff1, w_ff2=g_ff2)
        fold = lambda a: (a[:, 0, :].sum(0).reshape(2, SB_DIM)).sum(0)
        gsmall[l] = dict(norm1_g=dg1[0], conv_w=dconv[0:DN_CONV], a_log=dalog[0, 0:DN_HEADS], dt_bias=ddtb[0, 0:DN_HEADS],
                         dn_out_g=ddng[0], sb_q_g=fold(dgq), sb_k_g=fold(dgk), sg_v_g=dsgg[0], sg_w=dsgw,
                         sg_b=dsgb[:, 0:SG_GROUPS].T, norm2_g=dg2[0])
    return lossp, dx.reshape(bsz, t, D_MODEL), gbig, gsmall


def _chip_peers(x, y):
    return [(1 - x, y), (x, 1 - y), (1 - x, 1 - y)]


def exchange_chips(arrs, name, scatter):
    n = len(arrs)

    def body(*refs):
        ins, outs = refs[:n], refs[n:2 * n]
        send, recv, lsem = refs[2 * n:]
        x, y, c = lax.axis_index("x"), lax.axis_index("y"), lax.axis_index("c")
        me = 2 * x + y
        peers = _chip_peers(x, y)
        started = []
        for i in range(n):
            src = ins[i].at[me] if scatter else ins[i]
            loc = pltpu.make_async_copy(src, outs[i].at[me], lsem.at[i])
            loc.start()
            started.append(loc)
        sends = []
        for i in range(n):
            for j, (px, py) in enumerate(peers):
                src = ins[i].at[2 * px + py] if scatter else ins[i]
                cp = pltpu.make_async_remote_copy(src_ref=src, dst_ref=outs[i].at[me], send_sem=send.at[i * 3 + j],
                                                  recv_sem=recv.at[i * 3 + j], device_id=(px, py, c), device_id_type=MESH)
                cp.start()
                sends.append(cp)
        for i in range(n):
            for j, (px, py) in enumerate(peers):
                src = ins[i].at[me] if scatter else ins[i]
                pltpu.make_async_remote_copy(src_ref=src, dst_ref=outs[i].at[2 * px + py], send_sem=send.at[i * 3 + j],
                                             recv_sem=recv.at[i * 3 + j], device_id=(px, py, c), device_id_type=MESH).wait_recv()
        for cp in sends:
            cp.wait_send()
        for loc in started:
            loc.wait()

    any_spec = pl.BlockSpec(memory_space=pl.ANY)
    out_shape = [SDS(a.shape if scatter else (N_CHIPS,) + a.shape, a.dtype) for a in arrs]
    return pl.pallas_call(
        body, name=name, in_specs=[any_spec] * n, out_specs=[any_spec] * n, out_shape=out_shape,
        scratch_shapes=[pltpu.SemaphoreType.DMA((3 * n,)), pltpu.SemaphoreType.DMA((3 * n,)), pltpu.SemaphoreType.DMA((n,))],
    )(*arrs)


def swap_cores(arrs, name):
    n = len(arrs)

    def body(*refs):
        ins, outs = refs[:n], refs[n:2 * n]
        send, recv = refs[2 * n:]
        sib = (lax.axis_index("x"), lax.axis_index("y"), 1 - lax.axis_index("c"))
        cps = [pltpu.make_async_remote_copy(src_ref=ins[i], dst_ref=outs[i], send_sem=send.at[i], recv_sem=recv.at[i],
                                            device_id=sib, device_id_type=MESH) for i in range(n)]
        for cp in cps:
            cp.start()
        for cp in cps:
            cp.wait()

    any_spec = pl.BlockSpec(memory_space=pl.ANY)
    return pl.pallas_call(
        body, name=name, in_specs=[any_spec] * n, out_specs=[any_spec] * n, out_shape=[SDS(a.shape, a.dtype) for a in arrs],
        scratch_shapes=[pltpu.SemaphoreType.DMA((n,)), pltpu.SemaphoreType.DMA((n,))],
    )(*arrs)


def allreduce_small(v):
    def body(v_ref, o_ref, rbuf, send, recv):
        x, y, c = lax.axis_index("x"), lax.axis_index("y"), lax.axis_index("c")
        o_ref[...] = v_ref[...]
        for s, peer in enumerate([(x, y, 1 - c), (1 - x, y, c), (x, 1 - y, c)]):
            cp = pltpu.make_async_remote_copy(src_ref=o_ref, dst_ref=rbuf.at[s], send_sem=send.at[s], recv_sem=recv.at[s],
                                              device_id=peer, device_id_type=MESH)
            cp.start()
            cp.wait()
            o_ref[...] = o_ref[...] + rbuf[s]

    vm = pl.BlockSpec(memory_space=pltpu.VMEM)
    return pl.pallas_call(
        body, name="allreduce_small", in_specs=[vm], out_specs=vm, out_shape=SDS(v.shape, f32),
        scratch_shapes=[pltpu.VMEM((3,) + v.shape, f32), pltpu.SemaphoreType.DMA((3,)), pltpu.SemaphoreType.DMA((3,))],
        compiler_params=_cp(),
    )(v)


def sum_partials(p, name, tr=256):
    _, rows, cols = p.shape
    tr = min(tr, rows)

    def body(p_ref, o_ref):
        o_ref[...] = ((p_ref[0].astype(f32) + p_ref[1].astype(f32)) + p_ref[2].astype(f32)) + p_ref[3].astype(f32)

    return pl.pallas_call(
        body, name=name, grid=(rows // tr,), in_specs=[pl.BlockSpec((N_CHIPS, tr, cols), lambda i: (0, i, 0))],
        out_specs=pl.BlockSpec((tr, cols), lambda i: (i, 0)), out_shape=SDS((rows, cols), f32),
        compiler_params=_cp(("arbitrary",)),
    )(p)


def adamw(w, m, v, ga, gb, name, tr=256):
    rows, cols = w.shape
    tr = min(tr, rows)
    assert rows % tr == 0

    def body(w_ref, m_ref, v_ref, ga_ref, gb_ref, g_ref, d_ref, mo_ref, vo_ref):
        g = ga_ref[...] + gb_ref[...]
        mn = ADAM_B1 * m_ref[...] + (1.0 - ADAM_B1) * g
        vn = ADAM_B2 * v_ref[...] + (1.0 - ADAM_B2) * jnp.square(g)
        m_hat = mn / (1.0 - ADAM_B1 ** ADAM_STEP)
        v_hat = vn / (1.0 - ADAM_B2 ** ADAM_STEP)
        g_ref[...] = g
        d_ref[...] = -ADAM_LR * (m_hat / (jnp.sqrt(v_hat) + ADAM_EPS) + ADAM_WD * w_ref[...])
        mo_ref[...] = mn
        vo_ref[...] = vn

    spec = pl.BlockSpec((tr, cols), lambda i: (i, 0))
    return pl.pallas_call(
        body, name=name, grid=(rows // tr,), in_specs=[spec] * 5, out_specs=[spec] * 4, out_shape=[SDS((rows, cols), f32)] * 4,
        compiler_params=_cp(("arbitrary",)),
    )(w, m, v, ga, gb)


BIG = ("w_in", "w_out", "w_ff1", "w_ff2")
SMALL = ("norm1_g", "conv_w", "a_log", "dt_bias", "dn_out_g", "sb_q_g", "sb_k_g", "sg_v_g", "sg_w", "sg_b", "norm2_g")
WEIGHTS = ("norm1_g", "w_in", "conv_w", "a_log", "dt_bias", "dn_out_g", "sb_q_g", "sb_k_g", "sg_v_g", "sg_w", "sg_b",
           "w_out", "norm2_g", "w_ff1", "w_ff2")


PACK_ROWS = 256


def _pack(arrs):
    flat = jnp.concatenate([a.reshape(-1) for a in arrs])
    n = flat.shape[0]
    rows = -(-n // (PACK_ROWS * 128)) * PACK_ROWS
    return jnp.pad(flat, (0, rows * 128 - n)).reshape(rows, 128)


def _unpack(packed, shapes):
    flat, out, o = packed.reshape(-1), [], 0
    for s in shapes:
        n = 1
        for d in s:
            n *= d
        out.append(flat[o:o + n].reshape(s))
        o += n
    return out


def kernel(x, norm1_g, w_in, conv_w, a_log, dt_bias, dn_out_g, sb_q_g, sb_k_g, sg_v_g, sg_w, sg_b, w_out, norm2_g, w_ff1, w_ff2, loss_target, m_norm1_g, m_w_in, m_conv_w, m_a_log, m_dt_bias, m_dn_out_g, m_sb_q_g, m_sb_k_g, m_sg_v_g, m_sg_w, m_sg_b, m_w_out, m_norm2_g, m_w_ff1, m_w_ff2, v_norm1_g, v_w_in, v_conv_w, v_a_log, v_dt_bias, v_dn_out_g, v_sb_q_g, v_sb_k_g, v_sg_v_g, v_sg_w, v_sg_b, v_w_out, v_norm2_g, v_w_ff1, v_w_ff2):
    w = dict(norm1_g=norm1_g, w_in=w_in, conv_w=conv_w, a_log=a_log, dt_bias=dt_bias, dn_out_g=dn_out_g, sb_q_g=sb_q_g,
             sb_k_g=sb_k_g, sg_v_g=sg_v_g, sg_w=sg_w, sg_b=sg_b, w_out=w_out, norm2_g=norm2_g, w_ff1=w_ff1, w_ff2=w_ff2)
    mom = dict(norm1_g=m_norm1_g, w_in=m_w_in, conv_w=m_conv_w, a_log=m_a_log, dt_bias=m_dt_bias, dn_out_g=m_dn_out_g,
               sb_q_g=m_sb_q_g, sb_k_g=m_sb_k_g, sg_v_g=m_sg_v_g, sg_w=m_sg_w, sg_b=m_sg_b, w_out=m_w_out, norm2_g=m_norm2_g,
               w_ff1=m_w_ff1, w_ff2=m_w_ff2)
    var = dict(norm1_g=v_norm1_g, w_in=v_w_in, conv_w=v_conv_w, a_log=v_a_log, dt_bias=v_dt_bias, dn_out_g=v_dn_out_g,
               sb_q_g=v_sb_q_g, sb_k_g=v_sb_k_g, sg_v_g=v_sg_v_g, sg_w=v_sg_w, sg_b=v_sg_b, w_out=v_w_out, norm2_g=v_norm2_g,
               w_ff1=v_w_ff1, w_ff2=v_w_ff2)
    chip = 2 * lax.axis_index("x") + lax.axis_index("y")

    g_in, g_out, g_ff1, g_ff2, g_conv = exchange_chips(
        [w_in.astype(bf16), w_out.astype(bf16), w_ff1.astype(bf16), w_ff2.astype(bf16), conv_w], "allgather_weights", scatter=False)
    big = []
    for l in range(DEPTH):
        big.append(dict(
            w_in=pad_w_in(jnp.transpose(g_in[:, l], (1, 0, 2)).reshape(D_MODEL, IN_DIM)),
            w_out=g_out[:, l].reshape(D_MODEL, D_MODEL),
            w_ff1=jnp.transpose(g_ff1[:, l], (1, 0, 2)).reshape(D_MODEL, D_FF),
            w_ff2=g_ff2[:, l].reshape(D_FF, D_MODEL)))
    conv_full = jnp.transpose(g_conv, (1, 2, 0, 3)).reshape(DEPTH, DN_CONV, 3 * DN_WIDTH)
    small = {k: w[k] for k in SMALL}
    small["conv_w"] = conv_full

    lossp, grad_x, gbig, gsmall = local_step(x, loss_target, small, big)
    loss = lax.psum(jnp.sum(lossp), ("x", "y", "c"))

    by_dest = dict(
        w_in=jnp.stack([jnp.transpose(unpad_w_in(gbig[l]["w_in"]).reshape(D_MODEL, N_CHIPS, IN_DIM // N_CHIPS), (1, 0, 2))
                        for l in range(DEPTH)], axis=1),
        w_out=jnp.stack([gbig[l]["w_out"].reshape(N_CHIPS, D_MODEL // N_CHIPS, D_MODEL) for l in range(DEPTH)], axis=1),
        w_ff1=jnp.stack([gbig[l]["w_ff1"] for l in range(DEPTH)], axis=1),
        w_ff2=jnp.stack([gbig[l]["w_ff2"].reshape(N_CHIPS, D_FF // N_CHIPS, D_MODEL) for l in range(DEPTH)], axis=1))
    flat = {k: by_dest[k].reshape(N_CHIPS, -1, by_dest[k].shape[-1]) for k in BIG}
    got = exchange_chips([flat[k] for k in BIG], "scatter_grads", scatter=True)
    sums = [sum_partials(got[i], f"sum_{k}") for i, k in enumerate(BIG)]
    others = swap_cores(sums, "swap_grad_sums")
    res = {}
    for i, k in enumerate(BIG):
        shp = w[k].shape
        r2 = lambda a: a.reshape(-1, shp[-1])
        outs = adamw(r2(w[k]), r2(mom[k]), r2(var[k]), sums[i], others[i], f"adamw_{k}")
        res[k] = [o.reshape(shp) for o in outs]

    full_shapes = [(DEPTH,) + tuple(gsmall[0][k].shape) for k in SMALL]
    packed = _pack([jnp.stack([gsmall[l][k] for l in range(DEPTH)]) for k in SMALL])
    total = allreduce_small(packed)
    gfull = dict(zip(SMALL, _unpack(total, full_shapes)))
    cs = 3 * DN_WIDTH // N_CHIPS
    gfull["conv_w"] = lax.dynamic_slice_in_dim(gfull["conv_w"], chip * cs, cs, axis=2)
    zero = jnp.zeros_like
    gp, wp, mp, vp = (_pack([d[k] for k in SMALL]) for d in (gfull, w, mom, var))
    outs = adamw(wp, mp, vp, gp, zero(gp), "adamw_small")
    loc_shapes = [w[k].shape for k in SMALL]
    unp = [_unpack(o, loc_shapes) for o in outs]
    for i, k in enumerate(SMALL):
        res[k] = [unp[j][i] for j in range(4)]

    return (loss, grad_x, *[res[k][0] for k in WEIGHTS], *[res[k][1] for k in WEIGHTS], *[res[k][2] for k in WEIGHTS],
            *[res[k][3] for k in WEIGHTS])
```

```python
import functools

import jax
import jax.numpy as jnp
from jax import lax
from jax.experimental import pallas as pl
from jax.experimental.pallas import tpu as pltpu

f32 = jnp.float32
bf16 = jnp.bfloat16
SDS = jax.ShapeDtypeStruct
MESH = pl.DeviceIdType.MESH

NORM_EPS = 1e-6
D_MODEL = 1024
DEPTH = 2
DN_HEADS, DN_DIM, DN_WIDTH, DN_CONV, DN_CHUNK = 4, 128, 512, 4, 64
SB_HEADS, SB_DIM, SB_WIDTH, SB_BLOCK = 4, 64, 256, 128
SG_GROUPS, SG_DIM, SG_WIDTH, SG_CHUNK = 4, 64, 256, 128
D_FF = 4096
IN_DIM = 3336
C_QKV, C_Z, C_AB, C_SB, C_SG, IN_PAD = 0, 1536, 2048, 2176, 2944, 3456
N_CHIPS = 4

ADAM_LR, ADAM_B1, ADAM_B2, ADAM_EPS, ADAM_WD, ADAM_STEP = 0.001, 0.9, 0.999, 1e-08, 0.01, 10

VMEM_LIMIT = 56 * 1024 * 1024


def _cp(sem=None, **kw):
    if sem is not None:
        kw["dimension_semantics"] = sem
    return pltpu.CompilerParams(vmem_limit_bytes=VMEM_LIMIT, **kw)


def _split2(x):
    hi = x.astype(bf16)
    lo = (x - hi.astype(f32)).astype(bf16)
    return hi, lo


NT = (((1,), (1,)), ((), ()))
TN = (((0,), (0,)), ((), ()))
_DIMS2 = dict(nn=(((1,), (0,)), ((), ())), nt=NT, tn=TN)
_DIMS3 = dict(nn=(((2,), (1,)), ((0,), (0,))), nt=(((2,), (2,)), ((0,), (0,))), tn=(((1,), (1,)), ((0,), (0,))))


def _dg(a, b, kind):
    return lax.dot_general(a, b, (_DIMS2 if a.ndim == 2 else _DIMS3)[kind], preferred_element_type=f32)


def _pdot(a, b):
    return _dg(a, b, "nn")


def _dot_hp(a, b):
    ah, al = _split2(a)
    bh, bl = _split2(b)
    return _pdot(ah, bh) + _pdot(ah, bl) + _pdot(al, bh)


def _dot_x2c(a, m):
    ah, al = _split2(a)
    return _pdot(ah, m) + _pdot(al, m)


def _dot_cx2(m, a):
    if a.ndim == 3:
        m = jnp.broadcast_to(m, (a.shape[0],) + m.shape)
    ah, al = _split2(a)
    return _pdot(m, ah) + _pdot(m, al)


def _nt(a, b):
    return _dg(a.astype(bf16), b.astype(bf16), "nt")


def _tn(a, b):
    return _dg(a.astype(bf16), b.astype(bf16), "tn")


def _nn(a, b):
    return _dg(a.astype(bf16), b.astype(bf16), "nn")


@jax.custom_vjp
def mm(a, b):
    return _nn(a, b)


mm.defvjp(lambda a, b: (_nn(a, b), (a, b)), lambda r, g: (_nt(g, r[1]), _tn(r[0], g)))


@jax.custom_vjp
def mm_nt(a, b):
    return _nt(a, b)


mm_nt.defvjp(lambda a, b: (_nt(a, b), (a, b)), lambda r, g: (_nn(g, r[1]), _tn(g, r[0])))


@jax.custom_vjp
def mm_tn(a, b):
    return _tn(a, b)


mm_tn.defvjp(lambda a, b: (_tn(a, b), (a, b)), lambda r, g: (_nt(r[1], g), _nn(r[0], g)))


@jax.custom_vjp
def rmul_const(a, m, mt):
    return _dot_x2c(a, m)


rmul_const.defvjp(lambda a, m, mt: (_dot_x2c(a, m), (m, mt)),
                  lambda r, g: (_dot_x2c(g, r[1]), jnp.zeros_like(r[0]), jnp.zeros_like(r[1])))


@jax.custom_vjp
def lmul_const(m, mt, a):
    return _dot_cx2(m, a)


lmul_const.defvjp(lambda m, mt, a: (_dot_cx2(m, a), (m, mt)),
                  lambda r, g: (jnp.zeros_like(r[0]), jnp.zeros_like(r[1]), _dot_cx2(r[1], g)))


@jax.custom_vjp
def mm_hl(t, x):
    th, tl = _split2(t)
    xb = x.astype(bf16)
    return _pdot(th, xb) + _pdot(tl, xb)


def _mm_hl_bwd(r, g):
    t, x = r
    th, tl = _split2(t)
    gb = g.astype(bf16)
    return _nt(g, x), _dg(th, gb, "tn") + _dg(tl, gb, "tn")


mm_hl.defvjp(lambda t, x: (mm_hl(t, x), (t, x)), _mm_hl_bwd)


def inv_unit_lower(lm):
    c = lm.shape[-1]
    r, cc = _iota2((c, c))
    eye = (r == cc).astype(f32)
    t = eye - lm
    p = -lm
    k = 1
    while 2 * k < c:
        p = _nn(p, p)
        t = t + _nn(t, p)
        k *= 2
    res = eye - t - _dot_hp(lm, t)
    return t + _nn(t, res)


@jax.custom_vjp
def inv_given(lm, t):
    return t


inv_given.defvjp(lambda lm, t: (t, t), lambda t, g: (-_nt(_tn(t, g), t), jnp.zeros_like(t)))


def _sigmoid(x):
    return 1.0 / (1.0 + jnp.exp(-x))


def _softplus(x):
    return jnp.maximum(x, 0.0) + jnp.log(1.0 + jnp.exp(-jnp.abs(x)))


def _silu(x):
    return x * _sigmoid(x)


def _gelu(x):
    return 0.5 * x * (1.0 + jnp.tanh(0.7978845608028654 * (x + 0.044715 * (x * x * x))))


def _iota2(shape):
    return lax.broadcasted_iota(jnp.int32, shape, 0), lax.broadcasted_iota(jnp.int32, shape, 1)


def _group_avg_mats():
    r, c = _iota2((128, 128))
    return jnp.where((r // 64) == (c // 64), 1.0 / 64.0, 0.0).astype(bf16)


def _pair_norm(x, gain, bavg):
    ms = rmul_const(x * x, bavg, bavg)
    return x * lax.rsqrt(ms + NORM_EPS) * gain


def _rms(x):
    r = lax.rsqrt(jnp.mean(x * x, axis=-1, keepdims=True) + NORM_EPS)
    return r


_IN_GROUPS = ((C_QKV, C_Z), (C_Z, C_AB), (C_AB, C_SB), (C_SB, C_SG), (C_SG, IN_PAD))


def inproj_fwd(x, g, wp, tm=256):
    m = x.shape[0]

    def body(x_ref, g_ref, w_ref, *outs):
        xv = x_ref[...]
        h = (xv * _rms(xv) * g_ref[...]).astype(bf16)
        for (a, b), o in zip(_IN_GROUPS, outs):
            o[...] = _pdot(h, w_ref[:, a:b])

    return pl.pallas_call(
        body, name="inproj_fwd", grid=(m // tm,),
        in_specs=[pl.BlockSpec((tm, D_MODEL), lambda i: (i, 0)), pl.BlockSpec((1, D_MODEL), lambda i: (0, 0)),
                  pl.BlockSpec((D_MODEL, IN_PAD), lambda i: (0, 0))],
        out_specs=[pl.BlockSpec((tm, b - a), lambda i: (i, 0)) for a, b in _IN_GROUPS],
        out_shape=[SDS((m, b - a), f32) for a, b in _IN_GROUPS],
        compiler_params=_cp(("arbitrary",)),
    )(x, g, wp)


def inproj_bwd(x, g, wp, dproj, dres, tm=256):
    m = x.shape[0]

    def body(x_ref, g_ref, w_ref, dp_ref, dr_ref, dx_ref, dg_ref, h_ref):
        xv = x_ref[...]
        r = _rms(xv)
        xn = xv * r
        gv = g_ref[...]
        h_ref[...] = (xn * gv).astype(bf16)
        dh = lax.dot_general(dp_ref[...], w_ref[...], NT, preferred_element_type=f32)
        dxn = dh * gv
        dx_ref[...] = dr_ref[...] + r * (dxn - xn * jnp.mean(dxn * xn, axis=-1, keepdims=True))

        @pl.when(pl.program_id(0) == 0)
        def _():
            dg_ref[...] = jnp.zeros_like(dg_ref)

        dg_ref[...] += jnp.sum(dh * xn, axis=0, keepdims=True)

    return pl.pallas_call(
        body, name="inproj_bwd", grid=(m // tm,),
        in_specs=[pl.BlockSpec((tm, D_MODEL), lambda i: (i, 0)), pl.BlockSpec((1, D_MODEL), lambda i: (0, 0)),
                  pl.BlockSpec((D_MODEL, IN_PAD), lambda i: (0, 0)), pl.BlockSpec((tm, IN_PAD), lambda i: (i, 0)),
                  pl.BlockSpec((tm, D_MODEL), lambda i: (i, 0))],
        out_specs=[pl.BlockSpec((tm, D_MODEL), lambda i: (i, 0)), pl.BlockSpec((1, D_MODEL), lambda i: (0, 0)),
                   pl.BlockSpec((tm, D_MODEL), lambda i: (i, 0))],
        out_shape=[SDS((m, D_MODEL), f32), SDS((1, D_MODEL), f32), SDS((m, D_MODEL), bf16)],
        compiler_params=_cp(("arbitrary",)),
    )(x, g, wp, dproj, dres)


def outproj_fwd(x, odn, osb, osg, wo, tm=512):
    m = x.shape[0]

    def body(x_ref, a_ref, b_ref, c_ref, w_ref, x2_ref, mix_ref):
        mix_ref[:, 0:DN_WIDTH] = a_ref[...].astype(bf16)
        mix_ref[:, DN_WIDTH:DN_WIDTH + SB_WIDTH] = b_ref[...].astype(bf16)
        mix_ref[:, DN_WIDTH + SB_WIDTH:D_MODEL] = c_ref[...].astype(bf16)
        x2_ref[...] = x_ref[...] + _pdot(mix_ref[...], w_ref[...])

    row = lambda w: pl.BlockSpec((tm, w), lambda i: (i, 0))
    return pl.pallas_call(
        body, name="outproj_fwd", grid=(m // tm,),
        in_specs=[row(D_MODEL), row(DN_WIDTH), row(SB_WIDTH), row(SG_WIDTH), pl.BlockSpec((D_MODEL, D_MODEL), lambda i: (0, 0))],
        out_specs=[row(D_MODEL), row(D_MODEL)],
        out_shape=[SDS((m, D_MODEL), f32), SDS((m, D_MODEL), bf16)],
        compiler_params=_cp(("arbitrary",)),
    )(x, odn, osb, osg, wo)


def outproj_bwd(dx2, wo, tm=512):
    m = dx2.shape[0]

    def body(d_ref, w_ref, a_ref, b_ref, c_ref, db_ref):
        db = d_ref[...].astype(bf16)
        db_ref[...] = db
        dm = lax.dot_general(db, w_ref[...], NT, preferred_element_type=f32)
        a_ref[...] = dm[:, 0:DN_WIDTH]
        b_ref[...] = dm[:, DN_WIDTH:DN_WIDTH + SB_WIDTH]
        c_ref[...] = dm[:, DN_WIDTH + SB_WIDTH:D_MODEL]

    row = lambda w: pl.BlockSpec((tm, w), lambda i: (i, 0))
    return pl.pallas_call(
        body, name="outproj_bwd", grid=(m // tm,),
        in_specs=[row(D_MODEL), pl.BlockSpec((D_MODEL, D_MODEL), lambda i: (0, 0))],
        out_specs=[row(DN_WIDTH), row(SB_WIDTH), row(SG_WIDTH), row(D_MODEL)],
        out_shape=[SDS((m, DN_WIDTH), f32), SDS((m, SB_WIDTH), f32), SDS((m, SG_WIDTH), f32), SDS((m, D_MODEL), bf16)],
        compiler_params=_cp(("arbitrary",)),
    )(dx2, wo)


FF_CHUNK = 1024


def _load_weights_once(pairs, sem):
    @pl.when(pl.program_id(0) == 0)
    def _():
        cps = [pltpu.make_async_copy(h, v, sem.at[i]) for i, (h, v) in enumerate(pairs)]
        for c in cps:
            c.start()
        for c in cps:
            c.wait()


def ffn_fwd(x2, g, w1, w2, tm=256):
    m = x2.shape[0]

    def body(x_ref, g_ref, w1_hbm, w2_hbm, y_ref, w1_v, w2_v, sem):
        _load_weights_once(((w1_hbm, w1_v), (w2_hbm, w2_v)), sem)
        xv = x_ref[...]
        h = (xv * _rms(xv) * g_ref[...]).astype(bf16)
        acc = xv
        for j in range(0, D_FF, FF_CHUNK):
            f = _pdot(h, w1_v[:, j:j + FF_CHUNK])
            rl = jnp.maximum(f, 0.0)
            acc = acc + _pdot((rl * rl).astype(bf16), w2_v[j:j + FF_CHUNK, :])
        y_ref[...] = acc

    return pl.pallas_call(
        body, name="ffn_fwd", grid=(m // tm,),
        in_specs=[pl.BlockSpec((tm, D_MODEL), lambda i: (i, 0)), pl.BlockSpec((1, D_MODEL), lambda i: (0, 0)),
                  pl.BlockSpec(memory_space=pl.ANY), pl.BlockSpec(memory_space=pl.ANY)],
        out_specs=pl.BlockSpec((tm, D_MODEL), lambda i: (i, 0)),
        out_shape=SDS((m, D_MODEL), f32),
        scratch_shapes=[pltpu.VMEM((D_MODEL, D_FF), bf16), pltpu.VMEM((D_FF, D_MODEL), bf16), pltpu.SemaphoreType.DMA((2,))],
        compiler_params=_cp(("arbitrary",)),
    )(x2, g, w1, w2)


def ffn_bwd(x2, g, w1, w2, dy, tm=256):
    m = x2.shape[0]

    def body(x_ref, g_ref, w1_hbm, w2_hbm, dy_ref, dx_ref, dg_ref, h_ref, a_ref, df_ref, dyb_ref, w1_v, w2_v, sem):
        _load_weights_once(((w1_hbm, w1_v), (w2_hbm, w2_v)), sem)
        xv = x_ref[...]
        r = _rms(xv)
        xn = xv * r
        gv = g_ref[...]
        h = (xn * gv).astype(bf16)
        h_ref[...] = h
        dyv = dy_ref[...]
        dyb = dyv.astype(bf16)
        dyb_ref[...] = dyb
        dh = jnp.zeros((tm, D_MODEL), f32)
        for j in range(0, D_FF, FF_CHUNK):
            f = _pdot(h, w1_v[:, j:j + FF_CHUNK])
            rl = jnp.maximum(f, 0.0)
            a_ref[:, j:j + FF_CHUNK] = (rl * rl).astype(bf16)
            da = lax.dot_general(dyb, w2_v[j:j + FF_CHUNK, :], NT, preferred_element_type=f32)
            df = (da * (2.0 * rl)).astype(bf16)
            df_ref[:, j:j + FF_CHUNK] = df
            dh = dh + lax.dot_general(df, w1_v[:, j:j + FF_CHUNK], NT, preferred_element_type=f32)
        dxn = dh * gv
        dx_ref[...] = dyv + r * (dxn - xn * jnp.mean(dxn * xn, axis=-1, keepdims=True))

        @pl.when(pl.program_id(0) == 0)
        def _():
            dg_ref[...] = jnp.zeros_like(dg_ref)

        dg_ref[...] += jnp.sum(dh * xn, axis=0, keepdims=True)

    row = lambda w: pl.BlockSpec((tm, w), lambda i: (i, 0))
    return pl.pallas_call(
        body, name="ffn_bwd", grid=(m // tm,),
        in_specs=[row(D_MODEL), pl.BlockSpec((1, D_MODEL), lambda i: (0, 0)),
                  pl.BlockSpec(memory_space=pl.ANY), pl.BlockSpec(memory_space=pl.ANY), row(D_MODEL)],
        out_specs=[row(D_MODEL), pl.BlockSpec((1, D_MODEL), lambda i: (0, 0)), row(D_MODEL), row(D_FF), row(D_FF), row(D_MODEL)],
        out_shape=[SDS((m, D_MODEL), f32), SDS((1, D_MODEL), f32), SDS((m, D_MODEL), bf16), SDS((m, D_FF), bf16),
                   SDS((m, D_FF), bf16), SDS((m, D_MODEL), bf16)],
        scratch_shapes=[pltpu.VMEM((D_MODEL, D_FF), bf16), pltpu.VMEM((D_FF, D_MODEL), bf16), pltpu.SemaphoreType.DMA((2,))],
        compiler_params=_cp(("arbitrary",)),
    )(x2, g, w1, w2, dy)


def _tile(n, cap):
    best = 128
    for t in range(128, cap + 1, 128):
        if n % t == 0:
            best = t
    return best


def tn_matmul(a, b, name, col_shards=1, tk=512):
    m, ka = a.shape
    n = b.shape[1]
    ti = _tile(ka, 1024)
    tj = _tile(n // col_shards, 1152)
    nk = m // tk
    jps = (n // col_shards) // tj

    def body(a_ref, b_ref, o_ref, acc):
        k = pl.program_id(2)

        @pl.when(k == 0)
        def _():
            acc[...] = jnp.zeros_like(acc)

        acc[...] += lax.dot_general(a_ref[...], b_ref[...], TN, preferred_element_type=f32)

        @pl.when(k == nk - 1)
        def _():
            o_ref[...] = acc[...].astype(bf16).reshape(o_ref.shape)

    if col_shards == 1:
        out_shape, out_spec = SDS((ka, n), bf16), pl.BlockSpec((ti, tj), lambda i, j, k: (i, j))
    else:
        out_shape = SDS((col_shards, ka, n // col_shards), bf16)
        out_spec = pl.BlockSpec((1, ti, tj), lambda i, j, k: (j // jps, i, j % jps))
    return pl.pallas_call(
        body, name=name, grid=(ka // ti, n // tj, nk),
        in_specs=[pl.BlockSpec((tk, ti), lambda i, j, k: (k, i)), pl.BlockSpec((tk, tj), lambda i, j, k: (k, j))],
        out_specs=out_spec, out_shape=out_shape,
        scratch_shapes=[pltpu.VMEM((ti, tj), f32)],
        compiler_params=_cp(("arbitrary", "arbitrary", "arbitrary")),
    )(a, b)


def loss_head(y, tgt, tm=512):
    m = y.shape[0]

    def body(y_ref, t_ref, dy_ref, l_ref):
        e = y_ref[...] - t_ref[...]
        dy_ref[...] = e * (1.0 / D_MODEL)

        @pl.when(pl.program_id(0) == 0)
        def _():
            l_ref[...] = jnp.zeros_like(l_ref)

        l_ref[...] += jnp.sum(e * e, axis=0, keepdims=True) * (0.5 / D_MODEL)

    row = pl.BlockSpec((tm, D_MODEL), lambda i: (i, 0))
    return pl.pallas_call(
        body, name="loss_head", grid=(m // tm,), in_specs=[row, row],
        out_specs=[row, pl.BlockSpec((1, D_MODEL), lambda i: (0, 0))],
        out_shape=[SDS((m, D_MODEL), f32), SDS((1, D_MODEL), f32)],
        compiler_params=_cp(("arbitrary",)),
    )(y, tgt)


def _dn_consts():
    c = DN_CHUNK
    r, cc = _iota2((c, c))
    lt = (cc <= r).astype(bf16)
    ltt = (r <= cc).astype(bf16)
    return lt, ltt


def dn_chunk(cq, ck, cv, a, b, z, s, alog, dtb, gain, lt, ltt, t_given=None):
    c = DN_CHUNK
    r, cc = _iota2((c, c))
    q = cq * lax.rsqrt(jnp.sum(cq * cq, axis=-1, keepdims=True) + NORM_EPS) * (DN_DIM ** -0.5)
    k = ck * lax.rsqrt(jnp.sum(ck * ck, axis=-1, keepdims=True) + NORM_EPS)
    g = -jnp.exp(alog) * _softplus(a + dtb)
    beta = _sigmoid(b)
    r2, c2 = _iota2((c, 128))
    uaug = jnp.where((c2 < c) & (r2 > c2), 1.0, 0.0) + jnp.where(c2 == c, 1.0, 0.0)
    gam_all = lmul_const(lt, ltt, g * uaug)
    gam_cc = gam_all[:, :, 0:c]
    gam = gam_all[:, :, c:c + 1]
    dec = jnp.where(cc <= r, jnp.exp(jnp.where(cc <= r, gam_cc, 0.0)), 0.0)
    kk = mm_nt(k, k)
    lm = jnp.where(cc < r, beta * kk * dec, 0.0)
    t = inv_unit_lower(lm) if t_given is None else inv_given(lm, t_given)
    eg = jnp.exp(gam)
    sol = mm_hl(t, jnp.concatenate([cv * beta, k * (beta * eg)], axis=2))
    u, w = sol[:, :, 0:DN_DIM], sol[:, :, DN_DIM:2 * DN_DIM]
    qk = jnp.where(cc <= r, mm_nt(q, k) * dec, 0.0)
    glast = jnp.sum(g, axis=1, keepdims=True)
    qd = q * eg
    kd = k * jnp.exp(glast - gam)
    un = u - mm(w, s)
    o = mm(qd, s) + mm(qk, un)
    s_new = s * jnp.exp(glast) + mm_tn(kd, un)
    on = o * lax.rsqrt(jnp.mean(o * o, axis=-1, keepdims=True) + NORM_EPS) * gain * _silu(z)
    return on, s_new, t


def _dn_chains(cacts, ab_ref, z_ref, al_ref, dt_ref):
    cq, ck, cv, a, b, z, al, dt = [], [], [], [], [], [], [], []
    for bi, cact in enumerate(cacts):
        for h in range(DN_HEADS):
            hs = slice(h * DN_DIM, (h + 1) * DN_DIM)
            cq.append(cact[:, h * DN_DIM:(h + 1) * DN_DIM])
            ck.append(cact[:, DN_WIDTH + h * DN_DIM:DN_WIDTH + (h + 1) * DN_DIM])
            cv.append(cact[:, 2 * DN_WIDTH + h * DN_DIM:2 * DN_WIDTH + (h + 1) * DN_DIM])
            a.append(ab_ref[bi, :, h:h + 1])
            b.append(ab_ref[bi, :, DN_HEADS + h:DN_HEADS + h + 1])
            z.append(z_ref[bi, :, hs])
            al.append(al_ref[0:1, h:h + 1])
            dt.append(dt_ref[0:1, h:h + 1])
    return tuple(jnp.stack(v) for v in (cq, ck, cv, a, b, z)), jnp.stack(al), jnp.stack(dt)


def _conv_rows(xe_ref, b, w_ref):
    y = w_ref[0:1, :] * xe_ref[b, pl.ds(5, DN_CHUNK), :]
    for i in range(1, DN_CONV):
        y = y + w_ref[i:i + 1, :] * xe_ref[b, pl.ds(5 + i, DN_CHUNK), :]
    return y


def dn_fwd(qkv, z, ab, conv_w, alog, dtb, gain):
    bsz, t, _ = qkv.shape
    nc = t // DN_CHUNK
    c = DN_CHUNK
    nh = bsz * DN_HEADS

    def body(qkv_ref, z_ref, ab_ref, w_ref, al_ref, dt_ref, g_ref, o_ref, sall_ref, tall_ref, xe, s_sc):
        n = pl.program_id(0)

        @pl.when(n == 0)
        def _():
            xe[:, 0:8, :] = jnp.zeros((bsz, 8, 3 * DN_WIDTH), f32)
            s_sc[...] = jnp.zeros_like(s_sc)

        lt, ltt = _dn_consts()
        cacts = []
        for b in range(bsz):
            xe[b, 8:8 + c, :] = qkv_ref[b]
            cacts.append(_silu(_conv_rows(xe, b, w_ref)))
            xe[b, 0:8, :] = xe[b, c:c + 8, :]
        ops, al, dt = _dn_chains(cacts, ab_ref, z_ref, al_ref, dt_ref)
        s = s_sc[...]
        sall_ref[0] = s
        on, sn, tt = dn_chunk(*ops, s, al, dt, g_ref[...], lt, ltt)
        tall_ref[0] = tt
        s_sc[...] = sn
        for b in range(bsz):
            for h in range(DN_HEADS):
                o_ref[b, :, h * DN_DIM:(h + 1) * DN_DIM] = on[b * DN_HEADS + h]

    blk = lambda w: pl.BlockSpec((bsz, c, w), lambda n: (0, n, 0))
    full = lambda shp: pl.BlockSpec(shp, lambda n: (0,) * len(shp))
    return pl.pallas_call(
        body, name="dn_fwd", grid=(nc,),
        in_specs=[blk(3 * DN_WIDTH), blk(DN_WIDTH), blk(128), full((8, 3 * DN_WIDTH)), full((1, 128)), full((1, 128)), full((1, 128))],
        out_specs=[blk(DN_WIDTH), pl.BlockSpec((1, nh, DN_DIM, DN_DIM), lambda n: (n, 0, 0, 0)),
                   pl.BlockSpec((1, nh, c, c), lambda n: (n, 0, 0, 0))],
        out_shape=[SDS((bsz, t, DN_WIDTH), f32), SDS((nc, nh, DN_DIM, DN_DIM), f32), SDS((nc, nh, c, c), f32)],
        scratch_shapes=[pltpu.VMEM((bsz, c + 8, 3 * DN_WIDTH), f32), pltpu.VMEM((nh, DN_DIM, DN_DIM), f32)],
        compiler_params=_cp(("arbitrary",)),
    )(qkv, z, ab, conv_w, alog, dtb, gain)


def dn_bwd(qkv, z, ab, conv_w, alog, dtb, gain, sall, tall, do):
    bsz, t, _ = qkv.shape
    nc = t // DN_CHUNK
    c = DN_CHUNK
    nh = bsz * DN_HEADS
    w3 = 3 * DN_WIDTH

    def body(qkv_ref, prev_ref, z_ref, ab_ref, w_ref, al_ref, dt_ref, g_ref, sall_ref, tall_ref, do_ref,
             dqkv_ref, dz_ref, dab_ref, dw_ref, dal_ref, ddt_ref, dg_ref, xe, dye, dc_sc, ds_sc):
        n = pl.program_id(0)
        first = (nc - 1 - n) == 0

        @pl.when(n == 0)
        def _():
            dye[:, c:c + 8, :] = jnp.zeros((bsz, 8, w3), f32)
            ds_sc[...] = jnp.zeros_like(ds_sc)
            dw_ref[...] = jnp.zeros_like(dw_ref)
            dal_ref[...] = jnp.zeros_like(dal_ref)
            ddt_ref[...] = jnp.zeros_like(ddt_ref)
            dg_ref[...] = jnp.zeros_like(dg_ref)

        lt, ltt = _dn_consts()
        lane = lax.broadcasted_iota(jnp.int32, (1, 128), 1)
        lane_c = lax.broadcasted_iota(jnp.int32, (c, 128), 1)
        ys, sigs = [], []
        for b in range(bsz):
            xe[b, 0:8, :] = jnp.where(first, 0.0, prev_ref[b])
            xe[b, 8:8 + c, :] = qkv_ref[b]
            ys.append(_conv_rows(xe, b, w_ref))
            sigs.append(_sigmoid(ys[b]))
        ops, al, dt = _dn_chains([y * sg for y, sg in zip(ys, sigs)], ab_ref, z_ref, al_ref, dt_ref)
        tt = tall_ref[0]
        _, vjp = jax.vjp(lambda *p: dn_chunk(*p, lt, ltt, t_given=tt)[0:2], *ops, sall_ref[0], al, dt, g_ref[...])
        don = jnp.stack([do_ref[b, :, h * DN_DIM:(h + 1) * DN_DIM] for b in range(bsz) for h in range(DN_HEADS)])
        dcq, dck, dcv, da, db, dzz, dsp, dal, ddt, dgn = vjp((don, ds_sc[...]))
        ds_sc[...] = dsp
        dg_ref[...] += dgn
        for b in range(bsz):
            dab = jnp.zeros((c, 128), f32)
            for h in range(DN_HEADS):
                i = b * DN_HEADS + h
                dc_sc[b, :, h * DN_DIM:(h + 1) * DN_DIM] = dcq[i]
                dc_sc[b, :, DN_WIDTH + h * DN_DIM:DN_WIDTH + (h + 1) * DN_DIM] = dck[i]
                dc_sc[b, :, 2 * DN_WIDTH + h * DN_DIM:2 * DN_WIDTH + (h + 1) * DN_DIM] = dcv[i]
                dz_ref[b, :, h * DN_DIM:(h + 1) * DN_DIM] = dzz[i].astype(bf16)
                dab = dab + jnp.where(lane_c == h, da[i], 0.0) + jnp.where(lane_c == DN_HEADS + h, db[i], 0.0)
                dal_ref[...] += jnp.where(lane == h, dal[i], 0.0)
                ddt_ref[...] += jnp.where(lane == h, ddt[i], 0.0)
            dab_ref[b] = dab.astype(bf16)
            y, sig = ys[b], sigs[b]
            dy = dc_sc[b] * (sig * (1.0 + y * (1.0 - sig)))
            dye[b, 0:c, :] = dy
            dx = w_ref[3:4, :] * dy
            for i in range(DN_CONV - 1):
                dx = dx + w_ref[i:i + 1, :] * dye[b, pl.ds(3 - i, c), :]
            dqkv_ref[b] = dx.astype(bf16)
            for i in range(DN_CONV):
                dw_ref[i:i + 1, :] += jnp.sum(dy * xe[b, pl.ds(5 + i, c), :], axis=0, keepdims=True)
            dye[b, c:c + 8, :] = dye[b, 0:8, :]

    rev = lambda w: pl.BlockSpec((bsz, c, w), lambda n: (0, nc - 1 - n, 0))
    full = lambda shp: pl.BlockSpec(shp, lambda n: (0,) * len(shp))
    prev = pl.BlockSpec((bsz, 8, w3), lambda n: (0, jnp.maximum((nc - 1 - n) * (c // 8) - 1, 0), 0))
    return pl.pallas_call(
        body, name="dn_bwd", grid=(nc,),
        in_specs=[rev(w3), prev, rev(DN_WIDTH), rev(128), full((8, w3)), full((1, 128)), full((1, 128)), full((1, 128)),
                  pl.BlockSpec((1, nh, DN_DIM, DN_DIM), lambda n: (nc - 1 - n, 0, 0, 0)),
                  pl.BlockSpec((1, nh, c, c), lambda n: (nc - 1 - n, 0, 0, 0)), rev(DN_WIDTH)],
        out_specs=[rev(w3), rev(DN_WIDTH), rev(128), full((8, w3)), full((1, 128)), full((1, 128)), full((1, 128))],
        out_shape=[SDS((bsz, t, w3), bf16), SDS((bsz, t, DN_WIDTH), bf16), SDS((bsz, t, 128), bf16),
                   SDS((8, w3), f32), SDS((1, 128), f32), SDS((1, 128), f32), SDS((1, 128), f32)],
        scratch_shapes=[pltpu.VMEM((bsz, c + 8, w3), f32), pltpu.VMEM((bsz, c + 8, w3), f32), pltpu.VMEM((bsz, c, w3), f32),
                        pltpu.VMEM((nh, DN_DIM, DN_DIM), f32)],
        compiler_params=_cp(("arbitrary",)),
    )(qkv, qkv, z, ab, conv_w, alog, dtb, gain, sall, tall, do)


SB_TILE = 256


def sb_fwd(sbqkv, gq, gk):
    bsz, t, _ = sbqkv.shape
    blk = min(SB_TILE, t)
    nq = t // blk
    scale = SB_DIM ** -0.5

    def body(q_ref, k_ref, v_ref, gq_ref, gk_ref, o_ref, l_ref, q2_sc, kn_sc, v_sc):
        bavg = _group_avg_mats()
        lane = lax.broadcasted_iota(jnp.int32, (1, 128), 1)
        first = lane < SB_DIM
        qn = _pair_norm(q_ref[0], gq_ref[...], bavg)
        kn_sc[...] = _pair_norm(k_ref[0], gk_ref[...], bavg).astype(bf16)
        v_sc[...] = v_ref[0].astype(bf16)
        q2_sc[0] = jnp.where(first, qn, 0.0).astype(bf16)
        q2_sc[1] = jnp.where(first, 0.0, qn).astype(bf16)
        r, c = _iota2((blk, blk))
        ustrict = (r > c).astype(bf16)
        r2, c2 = _iota2((2 * blk, blk))
        causal = c2 < (r2 & (blk - 1))

        def tile(q2, ks, acc, rr, diag):
            zz = lax.dot_general(q2, kn_sc[pl.ds(ks, blk), :], NT, preferred_element_type=f32) * scale
            sp = _softplus(zz)
            lm = jnp.where(causal, -sp, 0.0) if diag else -sp
            rem = _dot_x2c(lm, ustrict)
            wgt = jnp.exp(zz - sp + rem + rr)
            if diag:
                wgt = jnp.where(causal, wgt, 0.0)
            acc = acc + _pdot(wgt.astype(bf16), v_sc[pl.ds(ks, blk), :])
            return acc, rr + jnp.sum(lm, axis=1, keepdims=True)

        def qloop(qi, _):
            qs = pl.multiple_of(qi * blk, blk)
            q2 = jnp.concatenate([q2_sc[0, pl.ds(qs, blk), :], q2_sc[1, pl.ds(qs, blk), :]], axis=0)
            carry = tile(q2, qs, jnp.zeros((2 * blk, 128), f32), jnp.zeros((2 * blk, 1), f32), True)
            acc, rr = lax.fori_loop(1, qi + 1, lambda i, cr: tile(q2, pl.multiple_of((qi - i) * blk, blk), *cr, False), carry)
            o_ref[0, pl.ds(qs, blk), :] = jnp.where(first, acc[0:blk], acc[blk:2 * blk])
            l_ref[0, pl.ds(qs, blk), :] = jnp.where(first, rr[0:blk], rr[blk:2 * blk])
            return 0

        lax.fori_loop(0, nq, qloop, 0)

    col = lambda off: pl.BlockSpec((1, t, 128), lambda b, p: (b, 0, off + p))
    gsp = pl.BlockSpec((1, 128), lambda b, p: (0, 0))
    return pl.pallas_call(
        body, name="sb_fwd", grid=(bsz, 2),
        in_specs=[col(0), col(2), col(4), gsp, gsp],
        out_specs=[col(0), col(0)],
        out_shape=[SDS((bsz, t, SB_WIDTH), f32), SDS((bsz, t, SB_WIDTH), f32)],
        scratch_shapes=[pltpu.VMEM((2, t, 128), bf16), pltpu.VMEM((t, 128), bf16), pltpu.VMEM((t, 128), bf16)],
        compiler_params=_cp(("arbitrary", "arbitrary")),
    )(sbqkv, sbqkv, sbqkv, gq, gk)


def sb_bwd(sbqkv, gq, gk, ltot, do):
    bsz, t, _ = sbqkv.shape
    blk = min(SB_TILE, t)
    nq = t // blk
    scale = SB_DIM ** -0.5

    def body(q_ref, k_ref, v_ref, gq_ref, gk_ref, l_ref, do_ref, dq_ref, dk_ref, dv_ref, dgq_ref, dgk_ref,
             q2_sc, kn_sc, v_sc, do2_sc, dqn_sc, dkn_sc, dv_sc):
        bavg = _group_avg_mats()
        lane = lax.broadcasted_iota(jnp.int32, (1, 128), 1)
        first = lane < SB_DIM
        fq = lambda x, g: _pair_norm(x, g, bavg)
        qn, q_vjp = jax.vjp(fq, q_ref[0], gq_ref[...])
        kn, k_vjp = jax.vjp(fq, k_ref[0], gk_ref[...])
        kn_sc[...] = kn.astype(bf16)
        v_sc[...] = v_ref[0].astype(bf16)
        dov = do_ref[0]
        q2_sc[0] = jnp.where(first, qn, 0.0).astype(bf16)
        q2_sc[1] = jnp.where(first, 0.0, qn).astype(bf16)
        do2_sc[0] = jnp.where(first, dov, 0.0).astype(bf16)
        do2_sc[1] = jnp.where(first, 0.0, dov).astype(bf16)
        dkn_sc[...] = jnp.zeros_like(dkn_sc)
        dv_sc[...] = jnp.zeros_like(dv_sc)
        r, c = _iota2((blk, blk))
        pincl = (r <= c).astype(bf16)
        pstrict = (r < c).astype(bf16)
        r2, c2 = _iota2((2 * blk, blk))
        causal = c2 < (r2 & (blk - 1))

        def tile(q2, do2, lt, ks, dq, cs, ce, diag):
            kb = kn_sc[pl.ds(ks, blk), :]
            zz = lax.dot_general(q2, kb, NT, preferred_element_type=f32) * scale
            sp = _softplus(zz)
            lm = jnp.where(causal, -sp, 0.0) if diag else -sp
            pre = _dot_x2c(lm, pincl)
            lp = zz - sp
            wgt = jnp.exp(lp + (lt - cs - pre))
            if diag:
                wgt = jnp.where(causal, wgt, 0.0)
            dw = lax.dot_general(do2, v_sc[pl.ds(ks, blk), :], NT, preferred_element_type=f32)
            e = wgt * dw
            ee = ce + _dot_x2c(e, pstrict)
            sig = jnp.exp(lp)
            dz = (e * (1.0 - sig) - ee * sig) * scale
            if diag:
                dz = jnp.where(causal, dz, 0.0)
            dz = dz.astype(bf16)
            dq = dq + _pdot(dz, kb)
            dkn_sc[pl.ds(ks, blk), :] += lax.dot_general(dz, q2, TN, preferred_element_type=f32)
            dv_sc[pl.ds(ks, blk), :] += lax.dot_general(wgt.astype(bf16), do2, TN, preferred_element_type=f32)
            return dq, cs + jnp.sum(lm, axis=1, keepdims=True), ce + jnp.sum(e, axis=1, keepdims=True)

        def qloop(qi, _):
            qs = pl.multiple_of(qi * blk, blk)
            q2 = jnp.concatenate([q2_sc[0, pl.ds(qs, blk), :], q2_sc[1, pl.ds(qs, blk), :]], axis=0)
            do2 = jnp.concatenate([do2_sc[0, pl.ds(qs, blk), :], do2_sc[1, pl.ds(qs, blk), :]], axis=0)
            lt = jnp.concatenate([l_ref[0, pl.ds(qs, blk), 0:1], l_ref[0, pl.ds(qs, blk), SB_DIM:SB_DIM + 1]], axis=0)
            z1 = jnp.zeros((2 * blk, 1), f32)
            carry = lax.fori_loop(0, qi, lambda kj, cr: tile(q2, do2, lt, pl.multiple_of(kj * blk, blk), *cr, False),
                                  (jnp.zeros((2 * blk, 128), f32), z1, z1))
            dq, _, _ = tile(q2, do2, lt, qs, *carry, True)
            dqn_sc[pl.ds(qs, blk), :] = jnp.where(first, dq[0:blk], dq[blk:2 * blk])
            return 0

        lax.fori_loop(0, nq, qloop, 0)
        dq_pre, dgq = q_vjp(dqn_sc[...])
        dk_pre, dgk = k_vjp(dkn_sc[...])
        dq_ref[0] = dq_pre.astype(bf16)
        dk_ref[0] = dk_pre.astype(bf16)
        dv_ref[0] = dv_sc[...].astype(bf16)
        dgq_ref[0] = jnp.broadcast_to(dgq, (8, 128))
        dgk_ref[0] = jnp.broadcast_to(dgk, (8, 128))

    col = lambda off: pl.BlockSpec((1, t, 128), lambda b, p: (b, 0, off + p))
    gsp = pl.BlockSpec((1, 128), lambda b, p: (0, 0))
    gout = pl.BlockSpec((1, 8, 128), lambda b, p: (b * 2 + p, 0, 0))
    return pl.pallas_call(
        body, name="sb_bwd", grid=(bsz, 2),
        in_specs=[col(0), col(2), col(4), gsp, gsp, col(0), col(0)],
        out_specs=[col(0), col(0), col(0), gout, gout],
        out_shape=[SDS((bsz, t, SB_WIDTH), bf16)] * 3 + [SDS((bsz * 2, 8, 128), f32)] * 2,
        scratch_shapes=[pltpu.VMEM((2, t, 128), bf16), pltpu.VMEM((t, 128), bf16), pltpu.VMEM((t, 128), bf16),
                        pltpu.VMEM((2, t, 128), bf16), pltpu.VMEM((t, 128), f32), pltpu.VMEM((t, 128), f32), pltpu.VMEM((t, 128), f32)],
        compiler_params=_cp(("arbitrary", "arbitrary")),
    )(sbqkv, sbqkv, sbqkv, gq, gk, ltot, do)


def sg_pair(u, v, gain, wa, wb, ba, bb, bavg):
    r, c = _iota2((SG_CHUNK, SG_CHUNK))
    lane = lax.broadcasted_iota(jnp.int32, (1, 128), 1)
    first = lane < SG_DIM
    vn = _pair_norm(_gelu(v), gain, bavg)
    tri = c <= r
    mixed = (mm(jnp.where(tri, wa, 0.0), jnp.where(first, vn, 0.0)) + mm(jnp.where(tri, wb, 0.0), jnp.where(first, 0.0, vn))
             + jnp.where(first, ba, bb))
    return _gelu(u) * mixed


def sg_fwd(sguv, gain, w, bt):
    bsz, t, _ = sguv.shape
    nch = t // SG_CHUNK

    def body(uv_ref, g_ref, w_ref, b_ref, o_ref):
        bavg = _group_avg_mats()
        for p in range(2):
            ls = slice(p * 128, (p + 1) * 128)
            o_ref[0, :, ls] = sg_pair(uv_ref[0, :, ls], uv_ref[0, :, SG_WIDTH + p * 128:SG_WIDTH + (p + 1) * 128], g_ref[:, ls],
                                      w_ref[2 * p], w_ref[2 * p + 1], b_ref[:, 2 * p:2 * p + 1], b_ref[:, 2 * p + 1:2 * p + 2], bavg)

    full = lambda shp: pl.BlockSpec(shp, lambda b, n: (0,) * len(shp))
    return pl.pallas_call(
        body, name="sg_fwd", grid=(bsz, nch),
        in_specs=[pl.BlockSpec((1, SG_CHUNK, 2 * SG_WIDTH), lambda b, n: (b, n, 0)), full((1, SG_WIDTH)),
                  full((SG_GROUPS, SG_CHUNK, SG_CHUNK)), full((SG_CHUNK, 128))],
        out_specs=pl.BlockSpec((1, SG_CHUNK, SG_WIDTH), lambda b, n: (b, n, 0)),
        out_shape=SDS((bsz, t, SG_WIDTH), f32),
        compiler_params=_cp(("arbitrary", "arbitrary")),
    )(sguv, gain, w, bt)


def sg_bwd(sguv, gain, w, bt, do):
    bsz, t, _ = sguv.shape
    nch = t // SG_CHUNK

    def body(uv_ref, g_ref, w_ref, b_ref, do_ref, duv_ref, dg_ref, dw_ref, db_ref):
        @pl.when((pl.program_id(0) == 0) & (pl.program_id(1) == 0))
        def _():
            dg_ref[...] = jnp.zeros_like(dg_ref)
            dw_ref[...] = jnp.zeros_like(dw_ref)
            db_ref[...] = jnp.zeros_like(db_ref)

        bavg = _group_avg_mats()
        lane = lax.broadcasted_iota(jnp.int32, (SG_CHUNK, 128), 1)
        dbt = jnp.zeros((SG_CHUNK, 128), f32)
        for p in range(2):
            ls = slice(p * 128, (p + 1) * 128)
            vs = slice(SG_WIDTH + p * 128, SG_WIDTH + (p + 1) * 128)
            prim = (uv_ref[0, :, ls], uv_ref[0, :, vs], g_ref[:, ls], w_ref[2 * p], w_ref[2 * p + 1],
                    b_ref[:, 2 * p:2 * p + 1], b_ref[:, 2 * p + 1:2 * p + 2])
            _, vjp = jax.vjp(lambda *a: sg_pair(*a, bavg), *prim)
            du, dv, dgn, dwa, dwb, dba, dbb = vjp(do_ref[0, :, ls])
            duv_ref[0, :, ls] = du.astype(bf16)
            duv_ref[0, :, vs] = dv.astype(bf16)
            dg_ref[:, ls] += dgn
            dw_ref[2 * p] += dwa
            dw_ref[2 * p + 1] += dwb
            dbt = dbt + jnp.where(lane == 2 * p, dba, 0.0) + jnp.where(lane == 2 * p + 1, dbb, 0.0)
        db_ref[...] += dbt

    full = lambda shp: pl.BlockSpec(shp, lambda b, n: (0,) * len(shp))
    return pl.pallas_call(
        body, name="sg_bwd", grid=(bsz, nch),
        in_specs=[pl.BlockSpec((1, SG_CHUNK, 2 * SG_WIDTH), lambda b, n: (b, n, 0)), full((1, SG_WIDTH)),
                  full((SG_GROUPS, SG_CHUNK, SG_CHUNK)), full((SG_CHUNK, 128)),
                  pl.BlockSpec((1, SG_CHUNK, SG_WIDTH), lambda b, n: (b, n, 0))],
        out_specs=[pl.BlockSpec((1, SG_CHUNK, 2 * SG_WIDTH), lambda b, n: (b, n, 0)), full((1, SG_WIDTH)),
                   full((SG_GROUPS, SG_CHUNK, SG_CHUNK)), full((SG_CHUNK, 128))],
        out_shape=[SDS((bsz, t, 2 * SG_WIDTH), bf16), SDS((1, SG_WIDTH), f32), SDS((SG_GROUPS, SG_CHUNK, SG_CHUNK), f32),
                   SDS((SG_CHUNK, 128), f32)],
        compiler_params=_cp(("arbitrary", "arbitrary")),
    )(sguv, gain, w, bt, do)


def _pad_lanes(v, n=128):
    return jnp.pad(v.reshape(1, -1), ((0, 0), (0, n - v.size)))


def pad_w_in(w):
    return jnp.concatenate([w[:, 0:2048], jnp.pad(w[:, 2048:2056], ((0, 0), (0, 120))), w[:, 2056:]], axis=1)


def unpad_w_in(w):
    return jnp.concatenate([w[:, 0:2048], w[:, C_AB:C_AB + 8], w[:, C_SB:]], axis=1)


def layer_params(p, l):
    return dict(
        g1=p["norm1_g"][l].reshape(1, -1), g2=p["norm2_g"][l].reshape(1, -1),
        conv=jnp.pad(p["conv_w"][l], ((0, 4), (0, 0))), alog=_pad_lanes(p["a_log"][l]), dtb=_pad_lanes(p["dt_bias"][l]),
        dng=p["dn_out_g"][l].reshape(1, -1), gq=jnp.tile(p["sb_q_g"][l].reshape(1, -1), (1, 2)),
        gk=jnp.tile(p["sb_k_g"][l].reshape(1, -1), (1, 2)), sgg=p["sg_v_g"][l].reshape(1, -1), sgw=p["sg_w"][l],
        sgb=jnp.pad(p["sg_b"][l].T, ((0, 0), (0, 124))))


def local_step(x, tgt, small, big):
    bsz, t, _ = x.shape
    m = bsz * t
    r3 = lambda a: a.reshape(bsz, t, a.shape[-1])
    r2 = lambda a: a.reshape(m, a.shape[-1])
    xs, saved = x.reshape(m, D_MODEL), []
    for l in range(DEPTH):
        sp, w = layer_params(small, l), big[l]
        qkv, z, ab, sb, sg = inproj_fwd(xs, sp["g1"], w["w_in"])
        odn, sall, tall = dn_fwd(r3(qkv), r3(z), r3(ab), sp["conv"], sp["alog"], sp["dtb"], sp["dng"])
        osb, ltot = sb_fwd(r3(sb), sp["gq"], sp["gk"])
        osg = sg_fwd(r3(sg), sp["sgg"], sp["sgw"], sp["sgb"])
        x2, mix = outproj_fwd(xs, r2(odn), r2(osb), r2(osg), w["w_out"])
        x3 = ffn_fwd(x2, sp["g2"], w["w_ff1"], w["w_ff2"])
        saved.append(dict(x=xs, qkv=qkv, z=z, ab=ab, sb=sb, sg=sg, sall=sall, tall=tall, ltot=ltot, mix=mix, x2=x2))
        xs = x3
    dx, lossp = loss_head(xs, tgt.reshape(m, D_MODEL))
    gbig, gsmall = [None] * DEPTH, [None] * DEPTH
    for l in reversed(range(DEPTH)):
        sp, w, s = layer_params(small, l), big[l], saved[l]
        dx2, dg2, h2, act, df, dyb = ffn_bwd(s["x2"], sp["g2"], w["w_ff1"], w["w_ff2"], dx)
        g_ff1 = tn_matmul(h2, df, f"dw_ff1_{l}", col_shards=N_CHIPS)
        g_ff2 = tn_matmul(act, dyb, f"dw_ff2_{l}")
        dodn, dosb, dosg, dx2b = outproj_bwd(dx2, w["w_out"])
        g_out = tn_matmul(s["mix"], dx2b, f"dw_out_{l}")
        dqkv, dz, dab, dconv, dalog, ddtb, ddng = dn_bwd(r3(s["qkv"]), r3(s["z"]), r3(s["ab"]), sp["conv"], sp["alog"], sp["dtb"],
                                                        sp["dng"], s["sall"], s["tall"], r3(dodn))
        dsq, dsk, dsv, dgq, dgk = sb_bwd(r3(s["sb"]), sp["gq"], sp["gk"], s["ltot"], r3(dosb))
        dsg, dsgg, dsgw, dsgb = sg_bwd(r3(s["sg"]), sp["sgg"], sp["sgw"], sp["sgb"], r3(dosg))
        dproj = jnp.concatenate([r2(dqkv), r2(dz), r2(dab), r2(dsq), r2(dsk), r2(dsv), r2(dsg)], axis=1)
        dx, dg1, h1 = inproj_bwd(s["x"], sp["g1"], w["w_in"], dproj, dx2)
        g_in = tn_matmul(h1, dproj, f"dw_in_{l}")
        gbig[l] = dict(w_in=g_in, w_out=g_out, w_ff1=g_ff1, w_ff2=g_ff2)
        fold = lambda a: (a[:, 0, :].sum(0).reshape(2, SB_DIM)).sum(0)
        gsmall[l] = dict(norm1_g=dg1[0], conv_w=dconv[0:DN_CONV], a_log=dalog[0, 0:DN_HEADS], dt_bias=ddtb[0, 0:DN_HEADS],
                         dn_out_g=ddng[0], sb_q_g=fold(dgq), sb_k_g=fold(dgk), sg_v_g=dsgg[0], sg_w=dsgw,
                         sg_b=dsgb[:, 0:SG_GROUPS].T, norm2_g=dg2[0])
    return lossp, dx.reshape(bsz, t, D_MODEL), gbig, gsmall


def _chip_peers(x, y):
    return [(1 - x, y), (x, 1 - y), (1 - x, 1 - y)]


def exchange_chips(arrs, name, scatter):
    n = len(arrs)

    def body(*refs):
        ins, outs = refs[:n], refs[n:2 * n]
        send, recv, lsem = refs[2 * n:]
        x, y, c = lax.axis_index("x"), lax.axis_index("y"), lax.axis_index("c")
        me = 2 * x + y
        peers = _chip_peers(x, y)
        started = []
        for i in range(n):
            src = ins[i].at[me] if scatter else ins[i]
            loc = pltpu.make_async_copy(src, outs[i].at[me], lsem.at[i])
            loc.start()
            started.append(loc)
        sends = []
        for i in range(n):
            for j, (px, py) in enumerate(peers):
                src = ins[i].at[2 * px + py] if scatter else ins[i]
                cp = pltpu.make_async_remote_copy(src_ref=src, dst_ref=outs[i].at[me], send_sem=send.at[i * 3 + j],
                                                  recv_sem=recv.at[i * 3 + j], device_id=(px, py, c), device_id_type=MESH)
                cp.start()
                sends.append(cp)
        for i in range(n):
            for j, (px, py) in enumerate(peers):
                src = ins[i].at[me] if scatter else ins[i]
                pltpu.make_async_remote_copy(src_ref=src, dst_ref=outs[i].at[2 * px + py], send_sem=send.at[i * 3 + j],
                                             recv_sem=recv.at[i * 3 + j], device_id=(px, py, c), device_id_type=MESH).wait_recv()
        for cp in sends:
            cp.wait_send()
        for loc in started:
            loc.wait()

    any_spec = pl.BlockSpec(memory_space=pl.ANY)
    out_shape = [SDS(a.shape if scatter else (N_CHIPS,) + a.shape, a.dtype) for a in arrs]
    return pl.pallas_call(
        body, name=name, in_specs=[any_spec] * n, out_specs=[any_spec] * n, out_shape=out_shape,
        scratch_shapes=[pltpu.SemaphoreType.DMA((3 * n,)), pltpu.SemaphoreType.DMA((3 * n,)), pltpu.SemaphoreType.DMA((n,))],
    )(*arrs)


def swap_cores(arrs, name):
    n = len(arrs)

    def body(*refs):
        ins, outs = refs[:n], refs[n:2 * n]
        send, recv = refs[2 * n:]
        sib = (lax.axis_index("x"), lax.axis_index("y"), 1 - lax.axis_index("c"))
        cps = [pltpu.make_async_remote_copy(src_ref=ins[i], dst_ref=outs[i], send_sem=send.at[i], recv_sem=recv.at[i],
                                            device_id=sib, device_id_type=MESH) for i in range(n)]
        for cp in cps:
            cp.start()
        for cp in cps:
            cp.wait()

    any_spec = pl.BlockSpec(memory_space=pl.ANY)
    return pl.pallas_call(
        body, name=name, in_specs=[any_spec] * n, out_specs=[any_spec] * n, out_shape=[SDS(a.shape, a.dtype) for a in arrs],
        scratch_shapes=[pltpu.SemaphoreType.DMA((n,)), pltpu.SemaphoreType.DMA((n,))],
    )(*arrs)


def allreduce_small(v):
    def body(v_ref, o_ref, rbuf, send, recv):
        x, y, c = lax.axis_index("x"), lax.axis_index("y"), lax.axis_index("c")
        o_ref[...] = v_ref[...]
        for s, peer in enumerate([(x, y, 1 - c), (1 - x, y, c), (x, 1 - y, c)]):
            cp = pltpu.make_async_remote_copy(src_ref=o_ref, dst_ref=rbuf.at[s], send_sem=send.at[s], recv_sem=recv.at[s],
                                              device_id=peer, device_id_type=MESH)
            cp.start()
            cp.wait()
            o_ref[...] = o_ref[...] + rbuf[s]

    vm = pl.BlockSpec(memory_space=pltpu.VMEM)
    return pl.pallas_call(
        body, name="allreduce_small", in_specs=[vm], out_specs=vm, out_shape=SDS(v.shape, f32),
        scratch_shapes=[pltpu.VMEM((3,) + v.shape, f32), pltpu.SemaphoreType.DMA((3,)), pltpu.SemaphoreType.DMA((3,))],
        compiler_params=_cp(),
    )(v)


def sum_partials(p, name, tr=256):
    _, rows, cols = p.shape
    tr = min(tr, rows)

    def body(p_ref, o_ref):
        o_ref[...] = ((p_ref[0].astype(f32) + p_ref[1].astype(f32)) + p_ref[2].astype(f32)) + p_ref[3].astype(f32)

    return pl.pallas_call(
        body, name=name, grid=(rows // tr,), in_specs=[pl.BlockSpec((N_CHIPS, tr, cols), lambda i: (0, i, 0))],
        out_specs=pl.BlockSpec((tr, cols), lambda i: (i, 0)), out_shape=SDS((rows, cols), f32),
        compiler_params=_cp(("arbitrary",)),
    )(p)


def adamw(w, m, v, ga, gb, name, tr=256):
    rows, cols = w.shape
    tr = min(tr, rows)
    assert rows % tr == 0

    def body(w_ref, m_ref, v_ref, ga_ref, gb_ref, g_ref, d_ref, mo_ref, vo_ref):
        g = ga_ref[...] + gb_ref[...]
        mn = ADAM_B1 * m_ref[...] + (1.0 - ADAM_B1) * g
        vn = ADAM_B2 * v_ref[...] + (1.0 - ADAM_B2) * jnp.square(g)
        m_hat = mn / (1.0 - ADAM_B1 ** ADAM_STEP)
        v_hat = vn / (1.0 - ADAM_B2 ** ADAM_STEP)
        g_ref[...] = g
        d_ref[...] = -ADAM_LR * (m_hat / (jnp.sqrt(v_hat) + ADAM_EPS) + ADAM_WD * w_ref[...])
        mo_ref[...] = mn
        vo_ref[...] = vn

    spec = pl.BlockSpec((tr, cols), lambda i: (i, 0))
    return pl.pallas_call(
        body, name=name, grid=(rows // tr,), in_specs=[spec] * 5, out_specs=[spec] * 4, out_shape=[SDS((rows, cols), f32)] * 4,
        compiler_params=_cp(("arbitrary",)),
    )(w, m, v, ga, gb)


BIG = ("w_in", "w_out", "w_ff1", "w_ff2")
SMALL = ("norm1_g", "conv_w", "a_log", "dt_bias", "dn_out_g", "sb_q_g", "sb_k_g", "sg_v_g", "sg_w", "sg_b", "norm2_g")
WEIGHTS = ("norm1_g", "w_in", "conv_w", "a_log", "dt_bias", "dn_out_g", "sb_q_g", "sb_k_g", "sg_v_g", "sg_w", "sg_b",
           "w_out", "norm2_g", "w_ff1", "w_ff2")


PACK_ROWS = 256


def _pack(arrs):
    flat = jnp.concatenate([a.reshape(-1) for a in arrs])
    n = flat.shape[0]
    rows = -(-n // (PACK_ROWS * 128)) * PACK_ROWS
    return jnp.pad(flat, (0, rows * 128 - n)).reshape(rows, 128)


def _unpack(packed, shapes):
    flat, out, o = packed.reshape(-1), [], 0
    for s in shapes:
        n = 1
        for d in s:
            n *= d
        out.append(flat[o:o + n].reshape(s))
        o += n
    return out


def kernel(x, norm1_g, w_in, conv_w, a_log, dt_bias, dn_out_g, sb_q_g, sb_k_g, sg_v_g, sg_w, sg_b, w_out, norm2_g, w_ff1, w_ff2, loss_target, m_norm1_g, m_w_in, m_conv_w, m_a_log, m_dt_bias, m_dn_out_g, m_sb_q_g, m_sb_k_g, m_sg_v_g, m_sg_w, m_sg_b, m_w_out, m_norm2_g, m_w_ff1, m_w_ff2, v_norm1_g, v_w_in, v_conv_w, v_a_log, v_dt_bias, v_dn_out_g, v_sb_q_g, v_sb_k_g, v_sg_v_g, v_sg_w, v_sg_b, v_w_out, v_norm2_g, v_w_ff1, v_w_ff2):
    w = dict(norm1_g=norm1_g, w_in=w_in, conv_w=conv_w, a_log=a_log, dt_bias=dt_bias, dn_out_g=dn_out_g, sb_q_g=sb_q_g,
             sb_k_g=sb_k_g, sg_v_g=sg_v_g, sg_w=sg_w, sg_b=sg_b, w_out=w_out, norm2_g=norm2_g, w_ff1=w_ff1, w_ff2=w_ff2)
    mom = dict(norm1_g=m_norm1_g, w_in=m_w_in, conv_w=m_conv_w, a_log=m_a_log, dt_bias=m_dt_bias, dn_out_g=m_dn_out_g,
               sb_q_g=m_sb_q_g, sb_k_g=m_sb_k_g, sg_v_g=m_sg_v_g, sg_w=m_sg_w, sg_b=m_sg_b, w_out=m_w_out, norm2_g=m_norm2_g,
               w_ff1=m_w_ff1, w_ff2=m_w_ff2)
    var = dict(norm1_g=v_norm1_g, w_in=v_w_in, conv_w=v_conv_w, a_log=v_a_log, dt_bias=v_dt_bias, dn_out_g=v_dn_out_g,
               sb_q_g=v_sb_q_g, sb_k_g=v_sb_k_g, sg_v_g=v_sg_v_g, sg_w=v_sg_w, sg_b=v_sg_b, w_out=v_w_out, norm2_g=v_norm2_g,
               w_ff1=v_w_ff1, w_ff2=v_w_ff2)
    chip = 2 * lax.axis_index("x") + lax.axis_index("y")

    g_in, g_out, g_ff1, g_ff2, g_conv = exchange_chips(
        [w_in.astype(bf16), w_out.astype(bf16), w_ff1.astype(bf16), w_ff2.astype(bf16), conv_w], "allgather_weights", scatter=False)
    big = []
    for l in range(DEPTH):
        big.append(dict(
            w_in=pad_w_in(jnp.transpose(g_in[:, l], (1, 0, 2)).reshape(D_MODEL, IN_DIM)),
            w_out=g_out[:, l].reshape(D_MODEL, D_MODEL),
            w_ff1=jnp.transpose(g_ff1[:, l], (1, 0, 2)).reshape(D_MODEL, D_FF),
            w_ff2=g_ff2[:, l].reshape(D_FF, D_MODEL)))
    conv_full = jnp.transpose(g_conv, (1, 2, 0, 3)).reshape(DEPTH, DN_CONV, 3 * DN_WIDTH)
    small = {k: w[k] for k in SMALL}
    small["conv_w"] = conv_full

    lossp, grad_x, gbig, gsmall = local_step(x, loss_target, small, big)
    loss = lax.psum(jnp.sum(lossp), ("x", "y", "c"))

    by_dest = dict(
        w_in=jnp.stack([jnp.transpose(unpad_w_in(gbig[l]["w_in"]).reshape(D_MODEL, N_CHIPS, IN_DIM // N_CHIPS), (1, 0, 2))
                        for l in range(DEPTH)], axis=1),
        w_out=jnp.stack([gbig[l]["w_out"].reshape(N_CHIPS, D_MODEL // N_CHIPS, D_MODEL) for l in range(DEPTH)], axis=1),
        w_ff1=jnp.stack([gbig[l]["w_ff1"] for l in range(DEPTH)], axis=1),
        w_ff2=jnp.stack([gbig[l]["w_ff2"].reshape(N_CHIPS, D_FF // N_CHIPS, D_MODEL) for l in range(DEPTH)], axis=1))
    flat = {k: by_dest[k].reshape(N_CHIPS, -1, by_dest[k].shape[-1]) for k in BIG}
    got = exchange_chips([flat[k] for k in BIG], "scatter_grads", scatter=True)
    sums = [sum_partials(got[i], f"sum_{k}") for i, k in enumerate(BIG)]
    others = swap_cores(sums, "swap_grad_sums")
    res = {}
    for i, k in enumerate(BIG):
        shp = w[k].shape
        r2 = lambda a: a.reshape(-1, shp[-1])
        outs = adamw(r2(w[k]), r2(mom[k]), r2(var[k]), sums[i], others[i], f"adamw_{k}")
        res[k] = [o.reshape(shp) for o in outs]

    full_shapes = [(DEPTH,) + tuple(gsmall[0][k].shape) for k in SMALL]
    packed = _pack([jnp.stack([gsmall[l][k] for l in range(DEPTH)]) for k in SMALL])
    total = allreduce_small(packed)
    gfull = dict(zip(SMALL, _unpack(total, full_shapes)))
    cs = 3 * DN_WIDTH // N_CHIPS
    gfull["conv_w"] = lax.dynamic_slice_in_dim(gfull["conv_w"], chip * cs, cs, axis=2)
    zero = jnp.zeros_like
    gp, wp, mp, vp = (_pack([d[k] for k in SMALL]) for d in (gfull, w, mom, var))
    outs = adamw(wp, mp, vp, gp, zero(gp), "adamw_small")
    loc_shapes = [w[k].shape for k in SMALL]
    unp = [_unpack(o, loc_shapes) for o in outs]
    for i, k in enumerate(SMALL):
        res[k] = [unp[j][i] for j in range(4)]

    return (loss, grad_x, *[res[k][0] for k in WEIGHTS], *[res[k][1] for k in WEIGHTS], *[res[k][2] for k in WEIGHTS],
            *[res[k][3] for k in WEIGHTS])
```

```python
import functools

import jax
import jax.numpy as jnp
from jax import lax
from jax.experimental import pallas as pl
from jax.experimental.pallas import tpu as pltpu

f32 = jnp.float32
bf16 = jnp.bfloat16
SDS = jax.ShapeDtypeStruct
MESH = pl.DeviceIdType.MESH

NORM_EPS = 1e-6
D_MODEL = 1024
DEPTH = 2
DN_HEADS, DN_DIM, DN_WIDTH, DN_CONV, DN_CHUNK = 4, 128, 512, 4, 64
SB_HEADS, SB_DIM, SB_WIDTH, SB_BLOCK = 4, 64, 256, 128
SG_GROUPS, SG_DIM, SG_WIDTH, SG_CHUNK = 4, 64, 256, 128
D_FF = 4096
IN_DIM = 3336
C_QKV, C_Z, C_AB, C_SB, C_SG, IN_PAD = 0, 1536, 2048, 2176, 2944, 3456
N_CHIPS = 4

ADAM_LR, ADAM_B1, ADAM_B2, ADAM_EPS, ADAM_WD, ADAM_STEP = 0.001, 0.9, 0.999, 1e-08, 0.01, 10

VMEM_LIMIT = 56 * 1024 * 1024


def _cp(sem=None, **kw):
    if sem is not None:
        kw["dimension_semantics"] = sem
    return pltpu.CompilerParams(vmem_limit_bytes=VMEM_LIMIT, **kw)


def _split2(x):
    hi = x.astype(bf16)
    lo = (x - hi.astype(f32)).astype(bf16)
    return hi, lo


NT = (((1,), (1,)), ((), ()))
TN = (((0,), (0,)), ((), ()))
_DIMS2 = dict(nn=(((1,), (0,)), ((), ())), nt=NT, tn=TN)
_DIMS3 = dict(nn=(((2,), (1,)), ((0,), (0,))), nt=(((2,), (2,)), ((0,), (0,))), tn=(((1,), (1,)), ((0,), (0,))))


def _dg(a, b, kind):
    return lax.dot_general(a, b, (_DIMS2 if a.ndim == 2 else _DIMS3)[kind], preferred_element_type=f32)


def _pdot(a, b):
    return _dg(a, b, "nn")


def _dot_hp(a, b):
    ah, al = _split2(a)
    bh, bl = _split2(b)
    return _pdot(ah, bh) + _pdot(ah, bl) + _pdot(al, bh)


def _dot_x2c(a, m):
    ah, al = _split2(a)
    return _pdot(ah, m) + _pdot(al, m)


def _dot_cx2(m, a):
    if a.ndim == 3:
        m = jnp.broadcast_to(m, (a.shape[0],) + m.shape)
    ah, al = _split2(a)
    return _pdot(m, ah) + _pdot(m, al)


def _nt(a, b):
    return _dg(a.astype(bf16), b.astype(bf16), "nt")


def _tn(a, b):
    return _dg(a.astype(bf16), b.astype(bf16), "tn")


def _nn(a, b):
    return _dg(a.astype(bf16), b.astype(bf16), "nn")


@jax.custom_vjp
def mm(a, b):
    return _nn(a, b)


mm.defvjp(lambda a, b: (_nn(a, b), (a, b)), lambda r, g: (_nt(g, r[1]), _tn(r[0], g)))


@jax.custom_vjp
def mm_nt(a, b):
    return _nt(a, b)


mm_nt.defvjp(lambda a, b: (_nt(a, b), (a, b)), lambda r, g: (_nn(g, r[1]), _tn(g, r[0])))


@jax.custom_vjp
def mm_tn(a, b):
    return _tn(a, b)


mm_tn.defvjp(lambda a, b: (_tn(a, b), (a, b)), lambda r, g: (_nt(r[1], g), _nn(r[0], g)))


@jax.custom_vjp
def rmul_const(a, m, mt):
    return _dot_x2c(a, m)


rmul_const.defvjp(lambda a, m, mt: (_dot_x2c(a, m), (m, mt)),
                  lambda r, g: (_dot_x2c(g, r[1]), jnp.zeros_like(r[0]), jnp.zeros_like(r[1])))


@jax.custom_vjp
def lmul_const(m, mt, a):
    return _dot_cx2(m, a)


lmul_const.defvjp(lambda m, mt, a: (_dot_cx2(m, a), (m, mt)),
                  lambda r, g: (jnp.zeros_like(r[0]), jnp.zeros_like(r[1]), _dot_cx2(r[1], g)))


@jax.custom_vjp
def mm_hl(t, x):
    th, tl = _split2(t)
    xb = x.astype(bf16)
    return _pdot(th, xb) + _pdot(tl, xb)


def _mm_hl_bwd(r, g):
    t, x = r
    th, tl = _split2(t)
    gb = g.astype(bf16)
    return _nt(g, x), _dg(th, gb, "tn") + _dg(tl, gb, "tn")


mm_hl.defvjp(lambda t, x: (mm_hl(t, x), (t, x)), _mm_hl_bwd)


def inv_unit_lower(lm):
    c = lm.shape[-1]
    r, cc = _iota2((c, c))
    eye = (r == cc).astype(f32)
    t = eye - lm
    p = -lm
    k = 1
    while 2 * k < c:
        p = _nn(p, p)
        t = t + _nn(t, p)
        k *= 2
    res = eye - t - _dot_hp(lm, t)
    return t + _nn(t, res)


@jax.custom_vjp
def inv_given(lm, t):
    return t


inv_given.defvjp(lambda lm, t: (t, t), lambda t, g: (-_nt(_tn(t, g), t), jnp.zeros_like(t)))


def _sigmoid(x):
    return 1.0 / (1.0 + jnp.exp(-x))


def _softplus(x):
    return jnp.maximum(x, 0.0) + jnp.log(1.0 + jnp.exp(-jnp.abs(x)))


def _silu(x):
    return x * _sigmoid(x)


def _gelu(x):
    return 0.5 * x * (1.0 + jnp.tanh(0.7978845608028654 * (x + 0.044715 * (x * x * x))))


def _iota2(shape):
    return lax.broadcasted_iota(jnp.int32, shape, 0), lax.broadcasted_iota(jnp.int32, shape, 1)


def _group_avg_mats():
    r, c = _iota2((128, 128))
    return jnp.where((r // 64) == (c // 64), 1.0 / 64.0, 0.0).astype(bf16)


def _pair_norm(x, gain, bavg):
    ms = rmul_const(x * x, bavg, bavg)
    return x * lax.rsqrt(ms + NORM_EPS) * gain


def _rms(x):
    r = lax.rsqrt(jnp.mean(x * x, axis=-1, keepdims=True) + NORM_EPS)
    return r


_IN_GROUPS = ((C_QKV, C_Z), (C_Z, C_AB), (C_AB, C_SB), (C_SB, C_SG), (C_SG, IN_PAD))


def inproj_fwd(x, g, wp, tm=256):
    m = x.shape[0]

    def body(x_ref, g_ref, w_ref, *outs):
        xv = x_ref[...]
        h = (xv * _rms(xv) * g_ref[...]).astype(bf16)
        for (a, b), o in zip(_IN_GROUPS, outs):
            o[...] = _pdot(h, w_ref[:, a:b])

    return pl.pallas_call(
        body, name="inproj_fwd", grid=(m // tm,),
        in_specs=[pl.BlockSpec((tm, D_MODEL), lambda i: (i, 0)), pl.BlockSpec((1, D_MODEL), lambda i: (0, 0)),
                  pl.BlockSpec((D_MODEL, IN_PAD), lambda i: (0, 0))],
        out_specs=[pl.BlockSpec((tm, b - a), lambda i: (i, 0)) for a, b in _IN_GROUPS],
        out_shape=[SDS((m, b - a), f32) for a, b in _IN_GROUPS],
        compiler_params=_cp(("arbitrary",)),
    )(x, g, wp)


def inproj_bwd(x, g, wp, dproj, dres, tm=256):
    m = x.shape[0]

    def body(x_ref, g_ref, w_ref, dp_ref, dr_ref, dx_ref, dg_ref, h_ref):
        xv = x_ref[...]
        r = _rms(xv)
        xn = xv * r
        gv = g_ref[...]
        h_ref[...] = (xn * gv).astype(bf16)
        dh = lax.dot_general(dp_ref[...], w_ref[...], NT, preferred_element_type=f32)
        dxn = dh * gv
        dx_ref[...] = dr_ref[...] + r * (dxn - xn * jnp.mean(dxn * xn, axis=-1, keepdims=True))

        @pl.when(pl.program_id(0) == 0)
        def _():
            dg_ref[...] = jnp.zeros_like(dg_ref)

        dg_ref[...] += jnp.sum(dh * xn, axis=0, keepdims=True)

    return pl.pallas_call(
        body, name="inproj_bwd", grid=(m // tm,),
        in_specs=[pl.BlockSpec((tm, D_MODEL), lambda i: (i, 0)), pl.BlockSpec((1, D_MODEL), lambda i: (0, 0)),
                  pl.BlockSpec((D_MODEL, IN_PAD), lambda i: (0, 0)), pl.BlockSpec((tm, IN_PAD), lambda i: (i, 0)),
                  pl.BlockSpec((tm, D_MODEL), lambda i: (i, 0))],
        out_specs=[pl.BlockSpec((tm, D_MODEL), lambda i: (i, 0)), pl.BlockSpec((1, D_MODEL), lambda i: (0, 0)),
                   pl.BlockSpec((tm, D_MODEL), lambda i: (i, 0))],
        out_shape=[SDS((m, D_MODEL), f32), SDS((1, D_MODEL), f32), SDS((m, D_MODEL), bf16)],
        compiler_params=_cp(("arbitrary",)),
    )(x, g, wp, dproj, dres)


def outproj_fwd(x, odn, osb, osg, wo, tm=512):
    m = x.shape[0]

    def body(x_ref, a_ref, b_ref, c_ref, w_ref, x2_ref, mix_ref):
        mix_ref[:, 0:DN_WIDTH] = a_ref[...].astype(bf16)
        mix_ref[:, DN_WIDTH:DN_WIDTH + SB_WIDTH] = b_ref[...].astype(bf16)
        mix_ref[:, DN_WIDTH + SB_WIDTH:D_MODEL] = c_ref[...].astype(bf16)
        x2_ref[...] = x_ref[...] + _pdot(mix_ref[...], w_ref[...])

    row = lambda w: pl.BlockSpec((tm, w), lambda i: (i, 0))
    return pl.pallas_call(
        body, name="outproj_fwd", grid=(m // tm,),
        in_specs=[row(D_MODEL), row(DN_WIDTH), row(SB_WIDTH), row(SG_WIDTH), pl.BlockSpec((D_MODEL, D_MODEL), lambda i: (0, 0))],
        out_specs=[row(D_MODEL), row(D_MODEL)],
        out_shape=[SDS((m, D_MODEL), f32), SDS((m, D_MODEL), bf16)],
        compiler_params=_cp(("arbitrary",)),
    )(x, odn, osb, osg, wo)


def outproj_bwd(dx2, wo, tm=512):
    m = dx2.shape[0]

    def body(d_ref, w_ref, a_ref, b_ref, c_ref, db_ref):
        db = d_ref[...].astype(bf16)
        db_ref[...] = db
        dm = lax.dot_general(db, w_ref[...], NT, preferred_element_type=f32)
        a_ref[...] = dm[:, 0:DN_WIDTH]
        b_ref[...] = dm[:, DN_WIDTH:DN_WIDTH + SB_WIDTH]
        c_ref[...] = dm[:, DN_WIDTH + SB_WIDTH:D_MODEL]

    row = lambda w: pl.BlockSpec((tm, w), lambda i: (i, 0))
    return pl.pallas_call(
        body, name="outproj_bwd", grid=(m // tm,),
        in_specs=[row(D_MODEL), pl.BlockSpec((D_MODEL, D_MODEL), lambda i: (0, 0))],
        out_specs=[row(DN_WIDTH), row(SB_WIDTH), row(SG_WIDTH), row(D_MODEL)],
        out_shape=[SDS((m, DN_WIDTH), f32), SDS((m, SB_WIDTH), f32), SDS((m, SG_WIDTH), f32), SDS((m, D_MODEL), bf16)],
        compiler_params=_cp(("arbitrary",)),
    )(dx2, wo)


FF_CHUNK = 1024


def _load_weights_once(pairs, sem):
    @pl.when(pl.program_id(0) == 0)
    def _():
        cps = [pltpu.make_async_copy(h, v, sem.at[i]) for i, (h, v) in enumerate(pairs)]
        for c in cps:
            c.start()
        for c in cps:
            c.wait()


def ffn_fwd(x2, g, w1, w2, tm=256):
    m = x2.shape[0]

    def body(x_ref, g_ref, w1_hbm, w2_hbm, y_ref, w1_v, w2_v, sem):
        _load_weights_once(((w1_hbm, w1_v), (w2_hbm, w2_v)), sem)
        xv = x_ref[...]
        h = (xv * _rms(xv) * g_ref[...]).astype(bf16)
        acc = xv
        for j in range(0, D_FF, FF_CHUNK):
            f = _pdot(h, w1_v[:, j:j + FF_CHUNK])
            rl = jnp.maximum(f, 0.0)
            acc = acc + _pdot((rl * rl).astype(bf16), w2_v[j:j + FF_CHUNK, :])
        y_ref[...] = acc

    return pl.pallas_call(
        body, name="ffn_fwd", grid=(m // tm,),
        in_specs=[pl.BlockSpec((tm, D_MODEL), lambda i: (i, 0)), pl.BlockSpec((1, D_MODEL), lambda i: (0, 0)),
                  pl.BlockSpec(memory_space=pl.ANY), pl.BlockSpec(memory_space=pl.ANY)],
        out_specs=pl.BlockSpec((tm, D_MODEL), lambda i: (i, 0)),
        out_shape=SDS((m, D_MODEL), f32),
        scratch_shapes=[pltpu.VMEM((D_MODEL, D_FF), bf16), pltpu.VMEM((D_FF, D_MODEL), bf16), pltpu.SemaphoreType.DMA((2,))],
        compiler_params=_cp(("arbitrary",)),
    )(x2, g, w1, w2)


def ffn_bwd(x2, g, w1, w2, dy, tm=256):
    m = x2.shape[0]

    def body(x_ref, g_ref, w1_hbm, w2_hbm, dy_ref, dx_ref, dg_ref, h_ref, a_ref, df_ref, dyb_ref, w1_v, w2_v, sem):
        _load_weights_once(((w1_hbm, w1_v), (w2_hbm, w2_v)), sem)
        xv = x_ref[...]
        r = _rms(xv)
        xn = xv * r
        gv = g_ref[...]
        h = (xn * gv).astype(bf16)
        h_ref[...] = h
        dyv = dy_ref[...]
        dyb = dyv.astype(bf16)
        dyb_ref[...] = dyb
        dh = jnp.zeros((tm, D_MODEL), f32)
        for j in range(0, D_FF, FF_CHUNK):
            f = _pdot(h, w1_v[:, j:j + FF_CHUNK])
            rl = jnp.maximum(f, 0.0)
            a_ref[:, j:j + FF_CHUNK] = (rl * rl).astype(bf16)
            da = lax.dot_general(dyb, w2_v[j:j + FF_CHUNK, :], NT, preferred_element_type=f32)
            df = (da * (2.0 * rl)).astype(bf16)
            df_ref[:, j:j + FF_CHUNK] = df
            dh = dh + lax.dot_general(df, w1_v[:, j:j + FF_CHUNK], NT, preferred_element_type=f32)
        dxn = dh * gv
        dx_ref[...] = dyv + r * (dxn - xn * jnp.mean(dxn * xn, axis=-1, keepdims=True))

        @pl.when(pl.program_id(0) == 0)
        def _():
            dg_ref[...] = jnp.zeros_like(dg_ref)

        dg_ref[...] += jnp.sum(dh * xn, axis=0, keepdims=True)

    row = lambda w: pl.BlockSpec((tm, w), lambda i: (i, 0))
    return pl.pallas_call(
        body, name="ffn_bwd", grid=(m // tm,),
        in_specs=[row(D_MODEL), pl.BlockSpec((1, D_MODEL), lambda i: (0, 0)),
                  pl.BlockSpec(memory_space=pl.ANY), pl.BlockSpec(memory_space=pl.ANY), row(D_MODEL)],
        out_specs=[row(D_MODEL), pl.BlockSpec((1, D_MODEL), lambda i: (0, 0)), row(D_MODEL), row(D_FF), row(D_FF), row(D_MODEL)],
        out_shape=[SDS((m, D_MODEL), f32), SDS((1, D_MODEL), f32), SDS((m, D_MODEL), bf16), SDS((m, D_FF), bf16),
                   SDS((m, D_FF), bf16), SDS((m, D_MODEL), bf16)],
        scratch_shapes=[pltpu.VMEM((D_MODEL, D_FF), bf16), pltpu.VMEM((D_FF, D_MODEL), bf16), pltpu.SemaphoreType.DMA((2,))],
        compiler_params=_cp(("arbitrary",)),
    )(x2, g, w1, w2, dy)


def _tile(n, cap):
    best = 128
    for t in range(128, cap + 1, 128):
        if n % t == 0:
            best = t
    return best


def tn_matmul(a, b, name, col_shards=1, tk=512):
    m, ka = a.shape
    n = b.shape[1]
    ti = _tile(ka, 1024)
    tj = _tile(n // col_shards, 1152)
    nk = m // tk
    jps = (n // col_shards) // tj

    def body(a_ref, b_ref, o_ref, acc):
        k = pl.program_id(2)

        @pl.when(k == 0)
        def _():
            acc[...] = jnp.zeros_like(acc)

        acc[...] += lax.dot_general(a_ref[...], b_ref[...], TN, preferred_element_type=f32)

        @pl.when(k == nk - 1)
        def _():
            o_ref[...] = acc[...].astype(bf16).reshape(o_ref.shape)

    if col_shards == 1:
        out_shape, out_spec = SDS((ka, n), bf16), pl.BlockSpec((ti, tj), lambda i, j, k: (i, j))
    else:
        out_shape = SDS((col_shards, ka, n // col_shards), bf16)
        out_spec = pl.BlockSpec((1, ti, tj), lambda i, j, k: (j // jps, i, j % jps))
    return pl.pallas_call(
        body, name=name, grid=(ka // ti, n // tj, nk),
        in_specs=[pl.BlockSpec((tk, ti), lambda i, j, k: (k, i)), pl.BlockSpec((tk, tj), lambda i, j, k: (k, j))],
        out_specs=out_spec, out_shape=out_shape,
        scratch_shapes=[pltpu.VMEM((ti, tj), f32)],
        compiler_params=_cp(("arbitrary", "arbitrary", "arbitrary")),
    )(a, b)


def loss_head(y, tgt, tm=512):
    m = y.shape[0]

    def body(y_ref, t_ref, dy_ref, l_ref):
        e = y_ref[...] - t_ref[...]
        dy_ref[...] = e * (1.0 / D_MODEL)

        @pl.when(pl.program_id(0) == 0)
        def _():
            l_ref[...] = jnp.zeros_like(l_ref)

        l_ref[...] += jnp.sum(e * e, axis=0, keepdims=True) * (0.5 / D_MODEL)

    row = pl.BlockSpec((tm, D_MODEL), lambda i: (i, 0))
    return pl.pallas_call(
        body, name="loss_head", grid=(m // tm,), in_specs=[row, row],
        out_specs=[row, pl.BlockSpec((1, D_MODEL), lambda i: (0, 0))],
        out_shape=[SDS((m, D_MODEL), f32), SDS((1, D_MODEL), f32)],
        compiler_params=_cp(("arbitrary",)),
    )(y, tgt)


def _dn_consts():
    c = DN_CHUNK
    r, cc = _iota2((c, c))
    lt = (cc <= r).astype(bf16)
    ltt = (r <= cc).astype(bf16)
    return lt, ltt


def dn_chunk(cq, ck, cv, a, b, z, s, alog, dtb, gain, lt, ltt, t_given=None):
    c = DN_CHUNK
    r, cc = _iota2((c, c))
    q = cq * lax.rsqrt(jnp.sum(cq * cq, axis=-1, keepdims=True) + NORM_EPS) * (DN_DIM ** -0.5)
    k = ck * lax.rsqrt(jnp.sum(ck * ck, axis=-1, keepdims=True) + NORM_EPS)
    g = -jnp.exp(alog) * _softplus(a + dtb)
    beta = _sigmoid(b)
    r2, c2 = _iota2((c, 128))
    uaug = jnp.where((c2 < c) & (r2 > c2), 1.0, 0.0) + jnp.where(c2 == c, 1.0, 0.0)
    gam_all = lmul_const(lt, ltt, g * uaug)
    gam_cc = gam_all[:, :, 0:c]
    gam = gam_all[:, :, c:c + 1]
    dec = jnp.where(cc <= r, jnp.exp(jnp.where(cc <= r, gam_cc, 0.0)), 0.0)
    kk = mm_nt(k, k)
    lm = jnp.where(cc < r, beta * kk * dec, 0.0)
    t = inv_unit_lower(lm) if t_given is None else inv_given(lm, t_given)
    eg = jnp.exp(gam)
    sol = mm_hl(t, jnp.concatenate([cv * beta, k * (beta * eg)], axis=2))
    u, w = sol[:, :, 0:DN_DIM], sol[:, :, DN_DIM:2 * DN_DIM]
    qk = jnp.where(cc <= r, mm_nt(q, k) * dec, 0.0)
    glast = jnp.sum(g, axis=1, keepdims=True)
    qd = q * eg
    kd = k * jnp.exp(glast - gam)
    un = u - mm(w, s)
    o = mm(qd, s) + mm(qk, un)
    s_new = s * jnp.exp(glast) + mm_tn(kd, un)
    on = o * lax.rsqrt(jnp.mean(o * o, axis=-1, keepdims=True) + NORM_EPS) * gain * _silu(z)
    return on, s_new, t


def _dn_chains(cacts, ab_ref, z_ref, al_ref, dt_ref):
    cq, ck, cv, a, b, z, al, dt = [], [], [], [], [], [], [], []
    for bi, cact in enumerate(cacts):
        for h in range(DN_HEADS):
            hs = slice(h * DN_DIM, (h + 1) * DN_DIM)
            cq.append(cact[:, h * DN_DIM:(h + 1) * DN_DIM])
            ck.append(cact[:, DN_WIDTH + h * DN_DIM:DN_WIDTH + (h + 1) * DN_DIM])
            cv.append(cact[:, 2 * DN_WIDTH + h * DN_DIM:2 * DN_WIDTH + (h + 1) * DN_DIM])
            a.append(ab_ref[bi, :, h:h + 1])
            b.append(ab_ref[bi, :, DN_HEADS + h:DN_HEADS + h + 1])
            z.append(z_ref[bi, :, hs])
            al.append(al_ref[0:1, h:h + 1])
            dt.append(dt_ref[0:1, h:h + 1])
    return tuple(jnp.stack(v) for v in (cq, ck, cv, a, b, z)), jnp.stack(al), jnp.stack(dt)


def _conv_rows(xe_ref, b, w_ref):
    y = w_ref[0:1, :] * xe_ref[b, pl.ds(5, DN_CHUNK), :]
    for i in range(1, DN_CONV):
        y = y + w_ref[i:i + 1, :] * xe_ref[b, pl.ds(5 + i, DN_CHUNK), :]
    return y


def dn_fwd(qkv, z, ab, conv_w, alog, dtb, gain):
    bsz, t, _ = qkv.shape
    nc = t // DN_CHUNK
    c = DN_CHUNK
    nh = bsz * DN_HEADS

    def body(qkv_ref, z_ref, ab_ref, w_ref, al_ref, dt_ref, g_ref, o_ref, sall_ref, tall_ref, xe, s_sc):
        n = pl.program_id(0)

        @pl.when(n == 0)
        def _():
            xe[:, 0:8, :] = jnp.zeros((bsz, 8, 3 * DN_WIDTH), f32)
            s_sc[...] = jnp.zeros_like(s_sc)

        lt, ltt = _dn_consts()
        cacts = []
        for b in range(bsz):
            xe[b, 8:8 + c, :] = qkv_ref[b]
            cacts.append(_silu(_conv_rows(xe, b, w_ref)))
            xe[b, 0:8, :] = xe[b, c:c + 8, :]
        ops, al, dt = _dn_chains(cacts, ab_ref, z_ref, al_ref, dt_ref)
        s = s_sc[...]
        sall_ref[0] = s
        on, sn, tt = dn_chunk(*ops, s, al, dt, g_ref[...], lt, ltt)
        tall_ref[0] = tt
        s_sc[...] = sn
        for b in range(bsz):
            for h in range(DN_HEADS):
                o_ref[b, :, h * DN_DIM:(h + 1) * DN_DIM] = on[b * DN_HEADS + h]

    blk = lambda w: pl.BlockSpec((bsz, c, w), lambda n: (0, n, 0))
    full = lambda shp: pl.BlockSpec(shp, lambda n: (0,) * len(shp))
    return pl.pallas_call(
        body, name="dn_fwd", grid=(nc,),
        in_specs=[blk(3 * DN_WIDTH), blk(DN_WIDTH), blk(128), full((8, 3 * DN_WIDTH)), full((1, 128)), full((1, 128)), full((1, 128))],
        out_specs=[blk(DN_WIDTH), pl.BlockSpec((1, nh, DN_DIM, DN_DIM), lambda n: (n, 0, 0, 0)),
                   pl.BlockSpec((1, nh, c, c), lambda n: (n, 0, 0, 0))],
        out_shape=[SDS((bsz, t, DN_WIDTH), f32), SDS((nc, nh, DN_DIM, DN_DIM), f32), SDS((nc, nh, c, c), f32)],
        scratch_shapes=[pltpu.VMEM((bsz, c + 8, 3 * DN_WIDTH), f32), pltpu.VMEM((nh, DN_DIM, DN_DIM), f32)],
        compiler_params=_cp(("arbitrary",)),
    )(qkv, z, ab, conv_w, alog, dtb, gain)


def dn_bwd(qkv, z, ab, conv_w, alog, dtb, gain, sall, tall, do):
    bsz, t, _ = qkv.shape
    nc = t // DN_CHUNK
    c = DN_CHUNK
    nh = bsz * DN_HEADS
    w3 = 3 * DN_WIDTH

    def body(qkv_ref, prev_ref, z_ref, ab_ref, w_ref, al_ref, dt_ref, g_ref, sall_ref, tall_ref, do_ref,
             dqkv_ref, dz_ref, dab_ref, dw_ref, dal_ref, ddt_ref, dg_ref, xe, dye, dc_sc, ds_sc):
        n = pl.program_id(0)
        first = (nc - 1 - n) == 0

        @pl.when(n == 0)
        def _():
            dye[:, c:c + 8, :] = jnp.zeros((bsz, 8, w3), f32)
            ds_sc[...] = jnp.zeros_like(ds_sc)
            dw_ref[...] = jnp.zeros_like(dw_ref)
            dal_ref[...] = jnp.zeros_like(dal_ref)
            ddt_ref[...] = jnp.zeros_like(ddt_ref)
            dg_ref[...] = jnp.zeros_like(dg_ref)

        lt, ltt = _dn_consts()
        lane = lax.broadcasted_iota(jnp.int32, (1, 128), 1)
        lane_c = lax.broadcasted_iota(jnp.int32, (c, 128), 1)
        ys, sigs = [], []
        for b in range(bsz):
            xe[b, 0:8, :] = jnp.where(first, 0.0, prev_ref[b])
            xe[b, 8:8 + c, :] = qkv_ref[b]
            ys.append(_conv_rows(xe, b, w_ref))
            sigs.append(_sigmoid(ys[b]))
        ops, al, dt = _dn_chains([y * sg for y, sg in zip(ys, sigs)], ab_ref, z_ref, al_ref, dt_ref)
        tt = tall_ref[0]
        _, vjp = jax.vjp(lambda *p: dn_chunk(*p, lt, ltt, t_given=tt)[0:2], *ops, sall_ref[0], al, dt, g_ref[...])
        don = jnp.stack([do_ref[b, :, h * DN_DIM:(h + 1) * DN_DIM] for b in range(bsz) for h in range(DN_HEADS)])
        dcq, dck, dcv, da, db, dzz, dsp, dal, ddt, dgn = vjp((don, ds_sc[...]))
        ds_sc[...] = dsp
        dg_ref[...] += dgn
        for b in range(bsz):
            dab = jnp.zeros((c, 128), f32)
            for h in range(DN_HEADS):
                i = b * DN_HEADS + h
                dc_sc[b, :, h * DN_DIM:(h + 1) * DN_DIM] = dcq[i]
                dc_sc[b, :, DN_WIDTH + h * DN_DIM:DN_WIDTH + (h + 1) * DN_DIM] = dck[i]
                dc_sc[b, :, 2 * DN_WIDTH + h * DN_DIM:2 * DN_WIDTH + (h + 1) * DN_DIM] = dcv[i]
                dz_ref[b, :, h * DN_DIM:(h + 1) * DN_DIM] = dzz[i].astype(bf16)
                dab = dab + jnp.where(lane_c == h, da[i], 0.0) + jnp.where(lane_c == DN_HEADS + h, db[i], 0.0)
                dal_ref[...] += jnp.where(lane == h, dal[i], 0.0)
                ddt_ref[...] += jnp.where(lane == h, ddt[i], 0.0)
            dab_ref[b] = dab.astype(bf16)
            y, sig = ys[b], sigs[b]
            dy = dc_sc[b] * (sig * (1.0 + y * (1.0 - sig)))
            dye[b, 0:c, :] = dy
            dx = w_ref[3:4, :] * dy
            for i in range(DN_CONV - 1):
                dx = dx + w_ref[i:i + 1, :] * dye[b, pl.ds(3 - i, c), :]
            dqkv_ref[b] = dx.astype(bf16)
            for i in range(DN_CONV):
                dw_ref[i:i + 1, :] += jnp.sum(dy * xe[b, pl.ds(5 + i, c), :], axis=0, keepdims=True)
            dye[b, c:c + 8, :] = dye[b, 0:8, :]

    rev = lambda w: pl.BlockSpec((bsz, c, w), lambda n: (0, nc - 1 - n, 0))
    full = lambda shp: pl.BlockSpec(shp, lambda n: (0,) * len(shp))
    prev = pl.BlockSpec((bsz, 8, w3), lambda n: (0, jnp.maximum((nc - 1 - n) * (c // 8) - 1, 0), 0))
    return pl.pallas_call(
        body, name="dn_bwd", grid=(nc,),
        in_specs=[rev(w3), prev, rev(DN_WIDTH), rev(128), full((8, w3)), full((1, 128)), full((1, 128)), full((1, 128)),
                  pl.BlockSpec((1, nh, DN_DIM, DN_DIM), lambda n: (nc - 1 - n, 0, 0, 0)),
                  pl.BlockSpec((1, nh, c, c), lambda n: (nc - 1 - n, 0, 0, 0)), rev(DN_WIDTH)],
        out_specs=[rev(w3), rev(DN_WIDTH), rev(128), full((8, w3)), full((1, 128)), full((1, 128)), full((1, 128))],
        out_shape=[SDS((bsz, t, w3), bf16), SDS((bsz, t, DN_WIDTH), bf16), SDS((bsz, t, 128), bf16),
                   SDS((8, w3), f32), SDS((1, 128), f32), SDS((1, 128), f32), SDS((1, 128), f32)],
        scratch_shapes=[pltpu.VMEM((bsz, c + 8, w3), f32), pltpu.VMEM((bsz, c + 8, w3), f32), pltpu.VMEM((bsz, c, w3), f32),
                        pltpu.VMEM((nh, DN_DIM, DN_DIM), f32)],
        compiler_params=_cp(("arbitrary",)),
    )(qkv, qkv, z, ab, conv_w, alog, dtb, gain, sall, tall, do)


SB_TILE = 256


def sb_fwd(sbqkv, gq, gk):
    bsz, t, _ = sbqkv.shape
    blk = min(SB_TILE, t)
    nq = t // blk
    scale = SB_DIM ** -0.5

    def body(q_ref, k_ref, v_ref, gq_ref, gk_ref, o_ref, l_ref, q2_sc, kn_sc, v_sc):
        bavg = _group_avg_mats()
        lane = lax.broadcasted_iota(jnp.int32, (1, 128), 1)
        first = lane < SB_DIM
        qn = _pair_norm(q_ref[0], gq_ref[...], bavg)
        kn_sc[...] = _pair_norm(k_ref[0], gk_ref[...], bavg).astype(bf16)
        v_sc[...] = v_ref[0].astype(bf16)
        q2_sc[0] = jnp.where(first, qn, 0.0).astype(bf16)
        q2_sc[1] = jnp.where(first, 0.0, qn).astype(bf16)
        r, c = _iota2((blk, blk))
        ustrict = (r > c).astype(bf16)
        r2, c2 = _iota2((2 * blk, blk))
        causal = c2 < (r2 & (blk - 1))

        def tile(q2, ks, acc, rr, diag):
            zz = lax.dot_general(q2, kn_sc[pl.ds(ks, blk), :], NT, preferred_element_type=f32) * scale
            sp = _softplus(zz)
            lm = jnp.where(causal, -sp, 0.0) if diag else -sp
            rem = _dot_x2c(lm, ustrict)
            wgt = jnp.exp(zz - sp + rem + rr)
            if diag:
                wgt = jnp.where(causal, wgt, 0.0)
            acc = acc + _pdot(wgt.astype(bf16), v_sc[pl.ds(ks, blk), :])
            return acc, rr + jnp.sum(lm, axis=1, keepdims=True)

        def qloop(qi, _):
            qs = pl.multiple_of(qi * blk, blk)
            q2 = jnp.concatenate([q2_sc[0, pl.ds(qs, blk), :], q2_sc[1, pl.ds(qs, blk), :]], axis=0)
            carry = tile(q2, qs, jnp.zeros((2 * blk, 128), f32), jnp.zeros((2 * blk, 1), f32), True)
            acc, rr = lax.fori_loop(1, qi + 1, lambda i, cr: tile(q2, pl.multiple_of((qi - i) * blk, blk), *cr, False), carry)
            o_ref[0, pl.ds(qs, blk), :] = jnp.where(first, acc[0:blk], acc[blk:2 * blk])
            l_ref[0, pl.ds(qs, blk), :] = jnp.where(first, rr[0:blk], rr[blk:2 * blk])
            return 0

        lax.fori_loop(0, nq, qloop, 0)

    col = lambda off: pl.BlockSpec((1, t, 128), lambda b, p: (b, 0, off + p))
    gsp = pl.BlockSpec((1, 128), lambda b, p: (0, 0))
    return pl.pallas_call(
        body, name="sb_fwd", grid=(bsz, 2),
        in_specs=[col(0), col(2), col(4), gsp, gsp],
        out_specs=[col(0), col(0)],
        out_shape=[SDS((bsz, t, SB_WIDTH), f32), SDS((bsz, t, SB_WIDTH), f32)],
        scratch_shapes=[pltpu.VMEM((2, t, 128), bf16), pltpu.VMEM((t, 128), bf16), pltpu.VMEM((t, 128), bf16)],
        compiler_params=_cp(("arbitrary", "arbitrary")),
    )(sbqkv, sbqkv, sbqkv, gq, gk)


def sb_bwd(sbqkv, gq, gk, ltot, do):
    bsz, t, _ = sbqkv.shape
    blk = min(SB_TILE, t)
    nq = t // blk
    scale = SB_DIM ** -0.5

    def body(q_ref, k_ref, v_ref, gq_ref, gk_ref, l_ref, do_ref, dq_ref, dk_ref, dv_ref, dgq_ref, dgk_ref,
             q2_sc, kn_sc, v_sc, do2_sc, dqn_sc, dkn_sc, dv_sc):
        bavg = _group_avg_mats()
        lane = lax.broadcasted_iota(jnp.int32, (1, 128), 1)
        first = lane < SB_DIM
        fq = lambda x, g: _pair_norm(x, g, bavg)
        qn, q_vjp = jax.vjp(fq, q_ref[0], gq_ref[...])
        kn, k_vjp = jax.vjp(fq, k_ref[0], gk_ref[...])
        kn_sc[...] = kn.astype(bf16)
        v_sc[...] = v_ref[0].astype(bf16)
        dov = do_ref[0]
        q2_sc[0] = jnp.where(first, qn, 0.0).astype(bf16)
        q2_sc[1] = jnp.where(first, 0.0, qn).astype(bf16)
        do2_sc[0] = jnp.where(first, dov, 0.0).astype(bf16)
        do2_sc[1] = jnp.where(first, 0.0, dov).astype(bf16)
        dkn_sc[...] = jnp.zeros_like(dkn_sc)
        dv_sc[...] = jnp.zeros_like(dv_sc)
        r, c = _iota2((blk, blk))
        pincl = (r <= c).astype(bf16)
        pstrict = (r < c).astype(bf16)
        r2, c2 = _iota2((2 * blk, blk))
        causal = c2 < (r2 & (blk - 1))

        def tile(q2, do2, lt, ks, dq, cs, ce, diag):
            kb = kn_sc[pl.ds(ks, blk), :]
            zz = lax.dot_general(q2, kb, NT, preferred_element_type=f32) * scale
            sp = _softplus(zz)
            lm = jnp.where(causal, -sp, 0.0) if diag else -sp
            pre = _dot_x2c(lm, pincl)
            lp = zz - sp
            wgt = jnp.exp(lp + (lt - cs - pre))
            if diag:
                wgt = jnp.where(causal, wgt, 0.0)
            dw = lax.dot_general(do2, v_sc[pl.ds(ks, blk), :], NT, preferred_element_type=f32)
            e = wgt * dw
            ee = ce + _dot_x2c(e, pstrict)
            sig = jnp.exp(lp)
            dz = (e * (1.0 - sig) - ee * sig) * scale
            if diag:
                dz = jnp.where(causal, dz, 0.0)
            dz = dz.astype(bf16)
            dq = dq + _pdot(dz, kb)
            dkn_sc[pl.ds(ks, blk), :] += lax.dot_general(dz, q2, TN, preferred_element_type=f32)
            dv_sc[pl.ds(ks, blk), :] += lax.dot_general(wgt.astype(bf16), do2, TN, preferred_element_type=f32)
            return dq, cs + jnp.sum(lm, axis=1, keepdims=True), ce + jnp.sum(e, axis=1, keepdims=True)

        def qloop(qi, _):
            qs = pl.multiple_of(qi * blk, blk)
            q2 = jnp.concatenate([q2_sc[0, pl.ds(qs, blk), :], q2_sc[1, pl.ds(qs, blk), :]], axis=0)
            do2 = jnp.concatenate([do2_sc[0, pl.ds(qs, blk), :], do2_sc[1, pl.ds(qs, blk), :]], axis=0)
            lt = jnp.concatenate([l_ref[0, pl.ds(qs, blk), 0:1], l_ref[0, pl.ds(qs, blk), SB_DIM:SB_DIM + 1]], axis=0)
            z1 = jnp.zeros((2 * blk, 1), f32)
            carry = lax.fori_loop(0, qi, lambda kj, cr: tile(q2, do2, lt, pl.multiple_of(kj * blk, blk), *cr, False),
                                  (jnp.zeros((2 * blk, 128), f32), z1, z1))
            dq, _, _ = tile(q2, do2, lt, qs, *carry, True)
            dqn_sc[pl.ds(qs, blk), :] = jnp.where(first, dq[0:blk], dq[blk:2 * blk])
            return 0

        lax.fori_loop(0, nq, qloop, 0)
        dq_pre, dgq = q_vjp(dqn_sc[...])
        dk_pre, dgk = k_vjp(dkn_sc[...])
        dq_ref[0] = dq_pre.astype(bf16)
        dk_ref[0] = dk_pre.astype(bf16)
        dv_ref[0] = dv_sc[...].astype(bf16)
        dgq_ref[0] = jnp.broadcast_to(dgq, (8, 128))
        dgk_ref[0] = jnp.broadcast_to(dgk, (8, 128))

    col = lambda off: pl.BlockSpec((1, t, 128), lambda b, p: (b, 0, off + p))
    gsp = pl.BlockSpec((1, 128), lambda b, p: (0, 0))
    gout = pl.BlockSpec((1, 8, 128), lambda b, p: (b * 2 + p, 0, 0))
    return pl.pallas_call(
        body, name="sb_bwd", grid=(bsz, 2),
        in_specs=[col(0), col(2), col(4), gsp, gsp, col(0), col(0)],
        out_specs=[col(0), col(0), col(0), gout, gout],
        out_shape=[SDS((bsz, t, SB_WIDTH), bf16)] * 3 + [SDS((bsz * 2, 8, 128), f32)] * 2,
        scratch_shapes=[pltpu.VMEM((2, t, 128), bf16), pltpu.VMEM((t, 128), bf16), pltpu.VMEM((t, 128), bf16),
                        pltpu.VMEM((2, t, 128), bf16), pltpu.VMEM((t, 128), f32), pltpu.VMEM((t, 128), f32), pltpu.VMEM((t, 128), f32)],
        compiler_params=_cp(("arbitrary", "arbitrary")),
    )(sbqkv, sbqkv, sbqkv, gq, gk, ltot, do)


def sg_pair(u, v, gain, wa, wb, ba, bb, bavg):
    r, c = _iota2((SG_CHUNK, SG_CHUNK))
    lane = lax.broadcasted_iota(jnp.int32, (1, 128), 1)
    first = lane < SG_DIM
    vn = _pair_norm(_gelu(v), gain, bavg)
    tri = c <= r
    mixed = (mm(jnp.where(tri, wa, 0.0), jnp.where(first, vn, 0.0)) + mm(jnp.where(tri, wb, 0.0), jnp.where(first, 0.0, vn))
             + jnp.where(first, ba, bb))
    return _gelu(u) * mixed


def sg_fwd(sguv, gain, w, bt):
    bsz, t, _ = sguv.shape
    nch = t // SG_CHUNK

    def body(uv_ref, g_ref, w_ref, b_ref, o_ref):
        bavg = _group_avg_mats()
        for p in range(2):
            ls = slice(p * 128, (p + 1) * 128)
            o_ref[0, :, ls] = sg_pair(uv_ref[0, :, ls], uv_ref[0, :, SG_WIDTH + p * 128:SG_WIDTH + (p + 1) * 128], g_ref[:, ls],
                                      w_ref[2 * p], w_ref[2 * p + 1], b_ref[:, 2 * p:2 * p + 1], b_ref[:, 2 * p + 1:2 * p + 2], bavg)

    full = lambda shp: pl.BlockSpec(shp, lambda b, n: (0,) * len(shp))
    return pl.pallas_call(
        body, name="sg_fwd", grid=(bsz, nch),
        in_specs=[pl.BlockSpec((1, SG_CHUNK, 2 * SG_WIDTH), lambda b, n: (b, n, 0)), full((1, SG_WIDTH)),
                  full((SG_GROUPS, SG_CHUNK, SG_CHUNK)), full((SG_CHUNK, 128))],
        out_specs=pl.BlockSpec((1, SG_CHUNK, SG_WIDTH), lambda b, n: (b, n, 0)),
        out_shape=SDS((bsz, t, SG_WIDTH), f32),
        compiler_params=_cp(("arbitrary", "arbitrary")),
    )(sguv, gain, w, bt)


def sg_bwd(sguv, gain, w, bt, do):
    bsz, t, _ = sguv.shape
    nch = t // SG_CHUNK

    def body(uv_ref, g_ref, w_ref, b_ref, do_ref, duv_ref, dg_ref, dw_ref, db_ref):
        @pl.when((pl.program_id(0) == 0) & (pl.program_id(1) == 0))
        def _():
            dg_ref[...] = jnp.zeros_like(dg_ref)
            dw_ref[...] = jnp.zeros_like(dw_ref)
            db_ref[...] = jnp.zeros_like(db_ref)

        bavg = _group_avg_mats()
        lane = lax.broadcasted_iota(jnp.int32, (SG_CHUNK, 128), 1)
        dbt = jnp.zeros((SG_CHUNK, 128), f32)
        for p in range(2):
            ls = slice(p * 128, (p + 1) * 128)
            vs = slice(SG_WIDTH + p * 128, SG_WIDTH + (p + 1) * 128)
            prim = (uv_ref[0, :, ls], uv_ref[0, :, vs], g_ref[:, ls], w_ref[2 * p], w_ref[2 * p + 1],
                    b_ref[:, 2 * p:2 * p + 1], b_ref[:, 2 * p + 1:2 * p + 2])
            _, vjp = jax.vjp(lambda *a: sg_pair(*a, bavg), *prim)
            du, dv, dgn, dwa, dwb, dba, dbb = vjp(do_ref[0, :, ls])
            duv_ref[0, :, ls] = du.astype(bf16)
            duv_ref[0, :, vs] = dv.astype(bf16)
            dg_ref[:, ls] += dgn
            dw_ref[2 * p] += dwa
            dw_ref[2 * p + 1] += dwb
            dbt = dbt + jnp.where(lane == 2 * p, dba, 0.0) + jnp.where(lane == 2 * p + 1, dbb, 0.0)
        db_ref[...] += dbt

    full = lambda shp: pl.BlockSpec(shp, lambda b, n: (0,) * len(shp))
    return pl.pallas_call(
        body, name="sg_bwd", grid=(bsz, nch),
        in_specs=[pl.BlockSpec((1, SG_CHUNK, 2 * SG_WIDTH), lambda b, n: (b, n, 0)), full((1, SG_WIDTH)),
                  full((SG_GROUPS, SG_CHUNK, SG_CHUNK)), full((SG_CHUNK, 128)),
                  pl.BlockSpec((1, SG_CHUNK, SG_WIDTH), lambda b, n: (b, n, 0))],
        out_specs=[pl.BlockSpec((1, SG_CHUNK, 2 * SG_WIDTH), lambda b, n: (b, n, 0)), full((1, SG_WIDTH)),
                   full((SG_GROUPS, SG_CHUNK, SG_CHUNK)), full((SG_CHUNK, 128))],
        out_shape=[SDS((bsz, t, 2 * SG_WIDTH), bf16), SDS((1, SG_WIDTH), f32), SDS((SG_GROUPS, SG_CHUNK, SG_CHUNK), f32),
                   SDS((SG_CHUNK, 128), f32)],
        compiler_params=_cp(("arbitrary", "arbitrary")),
    )(sguv, gain, w, bt, do)


def _pad_lanes(v, n=128):
    return jnp.pad(v.reshape(1, -1), ((0, 0), (0, n - v.size)))


def pad_w_in(w):
    return jnp.concatenate([w[:, 0:2048], jnp.pad(w[:, 2048:2056], ((0, 0), (0, 120))), w[:, 2056:]], axis=1)


def unpad_w_in(w):
    return jnp.concatenate([w[:, 0:2048], w[:, C_AB:C_AB + 8], w[:, C_SB:]], axis=1)


def layer_params(p, l):
    return dict(
        g1=p["norm1_g"][l].reshape(1, -1), g2=p["norm2_g"][l].reshape(1, -1),
        conv=jnp.pad(p["conv_w"][l], ((0, 4), (0, 0))), alog=_pad_lanes(p["a_log"][l]), dtb=_pad_lanes(p["dt_bias"][l]),
        dng=p["dn_out_g"][l].reshape(1, -1), gq=jnp.tile(p["sb_q_g"][l].reshape(1, -1), (1, 2)),
        gk=jnp.tile(p["sb_k_g"][l].reshape(1, -1), (1, 2)), sgg=p["sg_v_g"][l].reshape(1, -1), sgw=p["sg_w"][l],
        sgb=jnp.pad(p["sg_b"][l].T, ((0, 0), (0, 124))))


def local_step(x, tgt, small, get_w, put_g):
    bsz, t, _ = x.shape
    m = bsz * t
    r3 = lambda a: a.reshape(bsz, t, a.shape[-1])
    r2 = lambda a: a.reshape(m, a.shape[-1])
    xs, saved, ws = x.reshape(m, D_MODEL), [], []
    for l in range(DEPTH):
        sp, w = layer_params(small, l), {}
        w["w_in"] = get_w(l, "in", xs)
        qkv, z, ab, sb, sg = inproj_fwd(xs, sp["g1"], w["w_in"])
        odn, sall, tall = dn_fwd(r3(qkv), r3(z), r3(ab), sp["conv"], sp["alog"], sp["dtb"], sp["dng"])
        osb, ltot = sb_fwd(r3(sb), sp["gq"], sp["gk"])
        osg = sg_fwd(r3(sg), sp["sgg"], sp["sgw"], sp["sgb"])
        w["w_out"] = get_w(l, "out", osg)
        x2, mix = outproj_fwd(xs, r2(odn), r2(osb), r2(osg), w["w_out"])
        w["w_ff1"], w["w_ff2"] = get_w(l, "ff", x2)
        x3 = ffn_fwd(x2, sp["g2"], w["w_ff1"], w["w_ff2"])
        saved.append(dict(x=xs, qkv=qkv, z=z, ab=ab, sb=sb, sg=sg, sall=sall, tall=tall, ltot=ltot, mix=mix, x2=x2))
        ws.append(w)
        xs = x3
    dx, lossp = loss_head(xs, tgt.reshape(m, D_MODEL))
    gsmall = [None] * DEPTH
    token = jnp.zeros((), f32)
    for l in reversed(range(DEPTH)):
        sp, w, s = layer_params(small, l), ws[l], saved[l]
        dx2, dg2, h2, act, df, dyb = ffn_bwd(s["x2"], sp["g2"] + token, w["w_ff1"], w["w_ff2"], dx)
        g_ff1 = tn_matmul(h2, df, f"dw_ff1_{l}", col_shards=N_CHIPS)
        g_ff2 = tn_matmul(act, dyb, f"dw_ff2_{l}")
        dodn, dosb, dosg, dx2b = outproj_bwd(dx2, w["w_out"])
        g_out = tn_matmul(s["mix"], dx2b, f"dw_out_{l}")
        dqkv, dz, dab, dconv, dalog, ddtb, ddng = dn_bwd(r3(s["qkv"]), r3(s["z"]), r3(s["ab"]), sp["conv"], sp["alog"], sp["dtb"],
                                                        sp["dng"], s["sall"], s["tall"], r3(dodn))
        dsq, dsk, dsv, dgq, dgk = sb_bwd(r3(s["sb"]), sp["gq"], sp["gk"], s["ltot"], r3(dosb))
        dsg, dsgg, dsgw, dsgb = sg_bwd(r3(s["sg"]), sp["sgg"], sp["sgw"], sp["sgb"], r3(dosg))
        dproj = jnp.concatenate([r2(dqkv), r2(dz), r2(dab), r2(dsq), r2(dsk), r2(dsv), r2(dsg)], axis=1)
        dx, dg1, h1 = inproj_bwd(s["x"], sp["g1"], w["w_in"], dproj, dx2)
        g_in = tn_matmul(h1, dproj, f"dw_in_{l}")
        token = put_g(l, dict(w_in=g_in, w_out=g_out, w_ff1=g_ff1, w_ff2=g_ff2))
        fold = lambda a: (a[:, 0, :].sum(0).reshape(2, SB_DIM)).sum(0)
        gsmall[l] = dict(norm1_g=dg1[0], conv_w=dconv[0:DN_CONV], a_log=dalog[0, 0:DN_HEADS], dt_bias=ddtb[0, 0:DN_HEADS],
                         dn_out_g=ddng[0], sb_q_g=fold(dgq), sb_k_g=fold(dgk), sg_v_g=dsgg[0], sg_w=dsgw,
                         sg_b=dsgb[:, 0:SG_GROUPS].T, norm2_g=dg2[0])
    return lossp, dx.reshape(bsz, t, D_MODEL), gsmall


def _chip_peers(x, y):
    return [(1 - x, y), (x, 1 - y), (1 - x, 1 - y)]


_HBM = pl.BlockSpec(memory_space=pltpu.HBM)
_SEM = pl.BlockSpec(memory_space=pltpu.SEMAPHORE)
_EFFECT = pltpu.SideEffectType.DATAFLOW_SIDE_EFFECTING


def _hbm(a):
    return pltpu.with_memory_space_constraint(a, pltpu.HBM)


def _exchange_copy(src, land, k, j, send, recv, scatter, waiting):
    x, y, c = lax.axis_index("x"), lax.axis_index("y"), lax.axis_index("c")
    px, py = _chip_peers(x, y)[j]
    me, peer = 2 * x + y, 2 * px + py
    if scatter:
        src = src.at[me if waiting else peer]
    return pltpu.make_async_remote_copy(src_ref=src, dst_ref=land.at[peer if waiting else me], send_sem=send.at[3 * k + j],
                                        recv_sem=recv.at[3 * k + j], device_id=(px, py, c), device_id_type=MESH)


def exchange_start(items, name, scatter):
    arrs = []
    for a, _ in items:
        if not any(a is b for b in arrs):
            arrs.append(a)
    pos = [next(i for i, b in enumerate(arrs) if b is a) for a, _ in items]
    shapes = [a.shape if idx is None else a.shape[1:] for a, idx in items]
    lands = [lax.empty(s if scatter else (N_CHIPS,) + s, a.dtype) for (a, _), s in zip(items, shapes)]
    na, nl = len(arrs), len(lands)

    def body(*refs):
        ins, lnd = refs[:na], refs[na:na + nl]
        send, recv = refs[na + nl], refs[na + nl + 1]
        token = refs[-1]
        for k, (_, idx) in enumerate(items):
            src = ins[pos[k]] if idx is None else ins[pos[k]].at[idx]
            for j in range(3):
                _exchange_copy(src, lnd[k], k, j, send, recv, scatter, False).start()
        token[...] = jnp.zeros_like(token)

    sems = pltpu.SemaphoreType.DMA((3 * nl,))
    out = pl.pallas_call(
        body, name=name,
        out_shape=(sems, sems, *[pltpu.HBM(a.shape, a.dtype) for a in arrs + lands], SDS((8, 128), f32)),
        in_specs=[_HBM] * (na + nl), out_specs=(_SEM, _SEM, *[_HBM] * (na + nl), pl.BlockSpec(memory_space=pltpu.VMEM)),
        input_output_aliases={i: 2 + i for i in range(na + nl)},
        compiler_params=pltpu.CompilerParams(has_side_effects=_EFFECT),
    )(*[_hbm(a) for a in arrs + lands])
    thru = out[2:2 + na]
    return dict(send=out[0], recv=out[1], src=[(thru[pos[k]], idx) for k, (_, idx) in enumerate(items)],
                land=list(out[2 + na:2 + na + nl]), token=out[-1], scatter=scatter)


def exchange_wait(st, ks, after, name):
    arrs = []
    for k in ks:
        if not any(st["src"][k][0] is b for b in arrs):
            arrs.append(st["src"][k][0])
    pos = [next(i for i, b in enumerate(arrs) if b is st["src"][k][0]) for k in ks]
    lands = [st["land"][k] for k in ks]
    na, nl = len(arrs), len(lands)

    def body(*refs):
        ins, lnd = refs[:na], refs[na:na + nl]
        send, recv = refs[na + nl], refs[na + nl + 1]
        for t, k in enumerate(ks):
            idx = st["src"][k][1]
            src = ins[pos[t]] if idx is None else ins[pos[t]].at[idx]
            for j in range(3):
                cp = _exchange_copy(src, lnd[t], k, j, send, recv, st["scatter"], True)
                cp.wait_send()
                cp.wait_recv()

    out = pl.pallas_call(
        body, name=name, out_shape=tuple(pltpu.HBM(a.shape, a.dtype) for a in arrs + lands),
        in_specs=[_HBM] * (na + nl) + [_SEM, _SEM, pl.BlockSpec(memory_space=pl.ANY)], out_specs=tuple([_HBM] * (na + nl)),
        input_output_aliases={i: i for i in range(na + nl)},
        compiler_params=pltpu.CompilerParams(has_side_effects=_EFFECT),
    )(*arrs, *lands, st["send"], st["recv"], after)
    for k, (a, idx) in enumerate(st["src"]):
        for p, b in enumerate(arrs):
            if a is b:
                st["src"][k] = (out[p], idx)
    return list(out[na:na + nl])


def swap_cores(arrs, name):
    n = len(arrs)

    def body(*refs):
        ins, outs = refs[:n], refs[n:2 * n]
        send, recv = refs[2 * n:]
        sib = (lax.axis_index("x"), lax.axis_index("y"), 1 - lax.axis_index("c"))
        cps = [pltpu.make_async_remote_copy(src_ref=ins[i], dst_ref=outs[i], send_sem=send.at[i], recv_sem=recv.at[i],
                                            device_id=sib, device_id_type=MESH) for i in range(n)]
        for cp in cps:
            cp.start()
        for cp in cps:
            cp.wait()

    any_spec = pl.BlockSpec(memory_space=pl.ANY)
    return pl.pallas_call(
        body, name=name, in_specs=[any_spec] * n, out_specs=[any_spec] * n, out_shape=[SDS(a.shape, a.dtype) for a in arrs],
        scratch_shapes=[pltpu.SemaphoreType.DMA((n,)), pltpu.SemaphoreType.DMA((n,))],
    )(*arrs)


def allreduce_small(v):
    def body(v_ref, o_ref, rbuf, send, recv):
        x, y, c = lax.axis_index("x"), lax.axis_index("y"), lax.axis_index("c")
        o_ref[...] = v_ref[...]
        for s, peer in enumerate([(x, y, 1 - c), (1 - x, y, c), (x, 1 - y, c)]):
            cp = pltpu.make_async_remote_copy(src_ref=o_ref, dst_ref=rbuf.at[s], send_sem=send.at[s], recv_sem=recv.at[s],
                                              device_id=peer, device_id_type=MESH)
            cp.start()
            cp.wait()
            o_ref[...] = o_ref[...] + rbuf[s]

    vm = pl.BlockSpec(memory_space=pltpu.VMEM)
    return pl.pallas_call(
        body, name="allreduce_small", in_specs=[vm], out_specs=vm, out_shape=SDS(v.shape, f32),
        scratch_shapes=[pltpu.VMEM((3,) + v.shape, f32), pltpu.SemaphoreType.DMA((3,)), pltpu.SemaphoreType.DMA((3,))],
        compiler_params=_cp(),
    )(v)


def sum_partials(p, name, tr=256):
    _, rows, cols = p.shape
    tr = min(tr, rows)

    def body(p_ref, o_ref):
        o_ref[...] = ((p_ref[0].astype(f32) + p_ref[1].astype(f32)) + p_ref[2].astype(f32)) + p_ref[3].astype(f32)

    return pl.pallas_call(
        body, name=name, grid=(rows // tr,), in_specs=[pl.BlockSpec((N_CHIPS, tr, cols), lambda i: (0, i, 0))],
        out_specs=pl.BlockSpec((tr, cols), lambda i: (i, 0)), out_shape=SDS((rows, cols), f32),
        compiler_params=_cp(("arbitrary",)),
    )(p)


def adamw(w, m, v, ga, gb, name, layer=0, prev=None, tr=256):
    rows, cols = ga.shape
    tr = min(tr, rows)
    assert rows % tr == 0 and w.shape[0] % rows == 0
    off = layer * (rows // tr)

    def body(w_ref, m_ref, v_ref, ga_ref, gb_ref, *rest):
        g_ref, d_ref, mo_ref, vo_ref = rest[-4:]
        g = ga_ref[...] + gb_ref[...]
        mn = ADAM_B1 * m_ref[...] + (1.0 - ADAM_B1) * g
        vn = ADAM_B2 * v_ref[...] + (1.0 - ADAM_B2) * jnp.square(g)
        m_hat = mn / (1.0 - ADAM_B1 ** ADAM_STEP)
        v_hat = vn / (1.0 - ADAM_B2 ** ADAM_STEP)
        g_ref[...] = g
        d_ref[...] = -ADAM_LR * (m_hat / (jnp.sqrt(v_hat) + ADAM_EPS) + ADAM_WD * w_ref[...])
        mo_ref[...] = mn
        vo_ref[...] = vn

    loc = pl.BlockSpec((tr, cols), lambda i: (i, 0))
    glob = pl.BlockSpec((tr, cols), lambda i: (off + i, 0))
    extra = [] if prev is None else list(prev)
    return pl.pallas_call(
        body, name=name, grid=(rows // tr,),
        in_specs=[glob] * 3 + [loc] * 2 + [pl.BlockSpec(memory_space=pl.ANY)] * len(extra),
        out_specs=[glob] * 4, out_shape=[SDS(w.shape, f32)] * 4,
        input_output_aliases={5 + j: j for j in range(len(extra))},
        compiler_params=_cp(("arbitrary",)),
    )(w, m, v, ga, gb, *extra)


BIG = ("w_in", "w_out", "w_ff1", "w_ff2")
SMALL = ("norm1_g", "conv_w", "a_log", "dt_bias", "dn_out_g", "sb_q_g", "sb_k_g", "sg_v_g", "sg_w", "sg_b", "norm2_g")
WEIGHTS = ("norm1_g", "w_in", "conv_w", "a_log", "dt_bias", "dn_out_g", "sb_q_g", "sb_k_g", "sg_v_g", "sg_w", "sg_b",
           "w_out", "norm2_g", "w_ff1", "w_ff2")


PACK_ROWS = 256


def _pack(arrs):
    flat = jnp.concatenate([a.reshape(-1) for a in arrs])
    n = flat.shape[0]
    rows = -(-n // (PACK_ROWS * 128)) * PACK_ROWS
    return jnp.pad(flat, (0, rows * 128 - n)).reshape(rows, 128)


def _unpack(packed, shapes):
    flat, out, o = packed.reshape(-1), [], 0
    for s in shapes:
        n = 1
        for d in s:
            n *= d
        out.append(flat[o:o + n].reshape(s))
        o += n
    return out


def kernel(x, norm1_g, w_in, conv_w, a_log, dt_bias, dn_out_g, sb_q_g, sb_k_g, sg_v_g, sg_w, sg_b, w_out, norm2_g, w_ff1, w_ff2, loss_target, m_norm1_g, m_w_in, m_conv_w, m_a_log, m_dt_bias, m_dn_out_g, m_sb_q_g, m_sb_k_g, m_sg_v_g, m_sg_w, m_sg_b, m_w_out, m_norm2_g, m_w_ff1, m_w_ff2, v_norm1_g, v_w_in, v_conv_w, v_a_log, v_dt_bias, v_dn_out_g, v_sb_q_g, v_sb_k_g, v_sg_v_g, v_sg_w, v_sg_b, v_w_out, v_norm2_g, v_w_ff1, v_w_ff2):
    w = dict(norm1_g=norm1_g, w_in=w_in, conv_w=conv_w, a_log=a_log, dt_bias=dt_bias, dn_out_g=dn_out_g, sb_q_g=sb_q_g,
             sb_k_g=sb_k_g, sg_v_g=sg_v_g, sg_w=sg_w, sg_b=sg_b, w_out=w_out, norm2_g=norm2_g, w_ff1=w_ff1, w_ff2=w_ff2)
    mom = dict(norm1_g=m_norm1_g, w_in=m_w_in, conv_w=m_conv_w, a_log=m_a_log, dt_bias=m_dt_bias, dn_out_g=m_dn_out_g,
               sb_q_g=m_sb_q_g, sb_k_g=m_sb_k_g, sg_v_g=m_sg_v_g, sg_w=m_sg_w, sg_b=m_sg_b, w_out=m_w_out, norm2_g=m_norm2_g,
               w_ff1=m_w_ff1, w_ff2=m_w_ff2)
    var = dict(norm1_g=v_norm1_g, w_in=v_w_in, conv_w=v_conv_w, a_log=v_a_log, dt_bias=v_dt_bias, dn_out_g=v_dn_out_g,
               sb_q_g=v_sb_q_g, sb_k_g=v_sb_k_g, sg_v_g=v_sg_v_g, sg_w=v_sg_w, sg_b=v_sg_b, w_out=v_w_out, norm2_g=v_norm2_g,
               w_ff1=v_w_ff1, w_ff2=v_w_ff2)
    chip = 2 * lax.axis_index("x") + lax.axis_index("y")

    wb = {k: w[k].astype(bf16) for k in BIG}
    ag = exchange_start([(conv_w, None)] + [(wb[k], l) for l in range(DEPTH) for k in BIG], "allgather_start", scatter=False)
    item = lambda l, k: 1 + l * len(BIG) + BIG.index(k)

    def landed(ks, after, name):
        zones = exchange_wait(ag, ks, after, name)
        own = [ag["src"][k][0] if ag["src"][k][1] is None else ag["src"][k][0][ag["src"][k][1]] for k in ks]
        return [lax.dynamic_update_slice_in_dim(z, o[None], chip, axis=0) for z, o in zip(zones, own)]

    def whole(k, z):
        if k == "w_in":
            return pad_w_in(jnp.transpose(z, (1, 0, 2)).reshape(D_MODEL, IN_DIM))
        if k == "w_ff1":
            return jnp.transpose(z, (1, 0, 2)).reshape(D_MODEL, D_FF)
        return z.reshape(-1, D_MODEL)

    g_conv, first_in = landed([0, item(0, "w_in")], x, "allgather_wait_in0")
    small = {k: w[k] for k in SMALL}
    small["conv_w"] = jnp.transpose(g_conv, (1, 2, 0, 3)).reshape(DEPTH, DN_CONV, 3 * DN_WIDTH)
    cache = {}

    def get_w(l, part, after):
        if part == "in":
            return whole("w_in", first_in if l == 0 else landed([item(l, "w_in")], after, f"allgather_wait_in{l}")[0])
        if part == "out":
            zs = landed([item(l, k) for k in ("w_out", "w_ff1", "w_ff2")], after, f"allgather_wait_rest{l}")
            cache[l] = (whole("w_ff1", zs[1]), whole("w_ff2", zs[2]))
            return whole("w_out", zs[0])
        return cache[l]

    rs = {}

    def put_g(l, g):
        by_dest = [jnp.transpose(unpad_w_in(g["w_in"]).reshape(D_MODEL, N_CHIPS, IN_DIM // N_CHIPS), (1, 0, 2)),
                   g["w_out"].reshape(N_CHIPS, D_MODEL // N_CHIPS, D_MODEL), g["w_ff1"],
                   g["w_ff2"].reshape(N_CHIPS, D_FF // N_CHIPS, D_MODEL)]
        rs[l] = exchange_start([(a, None) for a in by_dest], f"scatter_start_{l}", scatter=True)
        return rs[l]["token"][0, 0]

    lossp, grad_x, gsmall = local_step(x, loss_target, small, get_w, put_g)
    loss = lax.psum(jnp.sum(lossp), ("x", "y", "c"))

    def finish(l, after, prev):
        ks = list(range(len(BIG)))
        zones = exchange_wait(rs[l], ks, after, f"scatter_wait_{l}")
        own = [lax.dynamic_index_in_dim(rs[l]["src"][k][0], chip, 0, keepdims=True) for k in ks]
        got = [lax.dynamic_update_slice_in_dim(z, o, chip, axis=0) for z, o in zip(zones, own)]
        sums = [sum_partials(got[i], f"sum_{k}_{l}") for i, k in enumerate(BIG)]
        others = swap_cores(sums, f"swap_grad_sums_{l}")
        outs = {}
        for i, k in enumerate(BIG):
            r2 = lambda a: a.reshape(-1, a.shape[-1])
            outs[k] = adamw(r2(w[k]), r2(mom[k]), r2(var[k]), sums[i], others[i], f"adamw_{k}_{l}", layer=l,
                            prev=None if prev is None else prev[k])
        return outs

    done1 = finish(1, rs[0]["token"], None)
    res = {}

    full_shapes = [(DEPTH,) + tuple(gsmall[0][k].shape) for k in SMALL]
    packed = _pack([jnp.stack([gsmall[l][k] for l in range(DEPTH)]) for k in SMALL])
    total = allreduce_small(packed)
    gfull = dict(zip(SMALL, _unpack(total, full_shapes)))
    cs = 3 * DN_WIDTH // N_CHIPS
    gfull["conv_w"] = lax.dynamic_slice_in_dim(gfull["conv_w"], chip * cs, cs, axis=2)
    zero = jnp.zeros_like
    gp, wp, mp, vp = (_pack([d[k] for k in SMALL]) for d in (gfull, w, mom, var))
    outs = adamw(wp, mp, vp, gp, zero(gp), "adamw_small")
    loc_shapes = [w[k].shape for k in SMALL]
    unp = [_unpack(o, loc_shapes) for o in outs]
    for i, k in enumerate(SMALL):
        res[k] = [unp[j][i] for j in range(4)]

    done = finish(0, outs[0], done1)
    for k in BIG:
        res[k] = [o.reshape(w[k].shape) for o in done[k]]

    return (loss, grad_x, *[res[k][0] for k in WEIGHTS], *[res[k][1] for k in WEIGHTS], *[res[k][2] for k in WEIGHTS],
            *[res[k][3] for k in WEIGHTS])
```

```python
import functools

import jax
import jax.numpy as jnp
from jax import lax
from jax.experimental import pallas as pl
from jax.experimental.pallas import tpu as pltpu

f32 = jnp.float32
bf16 = jnp.bfloat16
SDS = jax.ShapeDtypeStruct
MESH = pl.DeviceIdType.MESH

NORM_EPS = 1e-6
D_MODEL = 1024
DEPTH = 2
DN_HEADS, DN_DIM, DN_WIDTH, DN_CONV, DN_CHUNK = 4, 128, 512, 4, 64
SB_HEADS, SB_DIM, SB_WIDTH, SB_BLOCK = 4, 64, 256, 128
SG_GROUPS, SG_DIM, SG_WIDTH, SG_CHUNK = 4, 64, 256, 128
D_FF = 4096
IN_DIM = 3336
C_QKV, C_Z, C_AB, C_SB, C_SG, IN_PAD = 0, 1536, 2048, 2176, 2944, 3456
N_CHIPS = 4

ADAM_LR, ADAM_B1, ADAM_B2, ADAM_EPS, ADAM_WD, ADAM_STEP = 0.001, 0.9, 0.999, 1e-08, 0.01, 10

VMEM_LIMIT = 56 * 1024 * 1024


def _cp(sem=None, **kw):
    if sem is not None:
        kw["dimension_semantics"] = sem
    return pltpu.CompilerParams(vmem_limit_bytes=VMEM_LIMIT, **kw)


def _split2(x):
    hi = x.astype(bf16)
    lo = (x - hi.astype(f32)).astype(bf16)
    return hi, lo


NT = (((1,), (1,)), ((), ()))
TN = (((0,), (0,)), ((), ()))
_DIMS2 = dict(nn=(((1,), (0,)), ((), ())), nt=NT, tn=TN)
_DIMS3 = dict(nn=(((2,), (1,)), ((0,), (0,))), nt=(((2,), (2,)), ((0,), (0,))), tn=(((1,), (1,)), ((0,), (0,))))


def _dg(a, b, kind):
    return lax.dot_general(a, b, (_DIMS2 if a.ndim == 2 else _DIMS3)[kind], preferred_element_type=f32)


def _pdot(a, b):
    return _dg(a, b, "nn")


def _dot_hp(a, b):
    ah, al = _split2(a)
    bh, bl = _split2(b)
    return _pdot(ah, bh) + _pdot(ah, bl) + _pdot(al, bh)


def _dot_x2c(a, m):
    ah, al = _split2(a)
    return _pdot(ah, m) + _pdot(al, m)


def _dot_cx2(m, a):
    if a.ndim == 3:
        m = jnp.broadcast_to(m, (a.shape[0],) + m.shape)
    ah, al = _split2(a)
    return _pdot(m, ah) + _pdot(m, al)


def _nt(a, b):
    return _dg(a.astype(bf16), b.astype(bf16), "nt")


def _tn(a, b):
    return _dg(a.astype(bf16), b.astype(bf16), "tn")


def _nn(a, b):
    return _dg(a.astype(bf16), b.astype(bf16), "nn")


@jax.custom_vjp
def mm(a, b):
    return _nn(a, b)


mm.defvjp(lambda a, b: (_nn(a, b), (a, b)), lambda r, g: (_nt(g, r[1]), _tn(r[0], g)))


@jax.custom_vjp
def mm_nt(a, b):
    return _nt(a, b)


mm_nt.defvjp(lambda a, b: (_nt(a, b), (a, b)), lambda r, g: (_nn(g, r[1]), _tn(g, r[0])))


@jax.custom_vjp
def mm_tn(a, b):
    return _tn(a, b)


mm_tn.defvjp(lambda a, b: (_tn(a, b), (a, b)), lambda r, g: (_nt(r[1], g), _nn(r[0], g)))


@jax.custom_vjp
def rmul_const(a, m, mt):
    return _dot_x2c(a, m)


rmul_const.defvjp(lambda a, m, mt: (_dot_x2c(a, m), (m, mt)),
                  lambda r, g: (_dot_x2c(g, r[1]), jnp.zeros_like(r[0]), jnp.zeros_like(r[1])))


@jax.custom_vjp
def lmul_const(m, mt, a):
    return _dot_cx2(m, a)


lmul_const.defvjp(lambda m, mt, a: (_dot_cx2(m, a), (m, mt)),
                  lambda r, g: (jnp.zeros_like(r[0]), jnp.zeros_like(r[1]), _dot_cx2(r[1], g)))


@jax.custom_vjp
def mm_hl(t, x):
    th, tl = _split2(t)
    xb = x.astype(bf16)
    return _pdot(th, xb) + _pdot(tl, xb)


def _mm_hl_bwd(r, g):
    t, x = r
    th, tl = _split2(t)
    gb = g.astype(bf16)
    return _nt(g, x), _dg(th, gb, "tn") + _dg(tl, gb, "tn")


mm_hl.defvjp(lambda t, x: (mm_hl(t, x), (t, x)), _mm_hl_bwd)


def inv_unit_lower(lm):
    c = lm.shape[-1]
    r, cc = _iota2((c, c))
    eye = (r == cc).astype(f32)
    t = eye - lm
    p = -lm
    k = 1
    while 2 * k < c:
        p = _nn(p, p)
        t = t + _nn(t, p)
        k *= 2
    res = eye - t - _dot_hp(lm, t)
    return t + _nn(t, res)


@jax.custom_vjp
def inv_given(lm, t):
    return t


inv_given.defvjp(lambda lm, t: (t, t), lambda t, g: (-_nt(_tn(t, g), t), jnp.zeros_like(t)))


def _sigmoid(x):
    return 1.0 / (1.0 + jnp.exp(-x))


def _softplus(x):
    return jnp.maximum(x, 0.0) + jnp.log(1.0 + jnp.exp(-jnp.abs(x)))


def _silu(x):
    return x * _sigmoid(x)


def _gelu(x):
    return 0.5 * x * (1.0 + jnp.tanh(0.7978845608028654 * (x + 0.044715 * (x * x * x))))


def _iota2(shape):
    return lax.broadcasted_iota(jnp.int32, shape, 0), lax.broadcasted_iota(jnp.int32, shape, 1)


def _group_avg_mats():
    r, c = _iota2((128, 128))
    return jnp.where((r // 64) == (c // 64), 1.0 / 64.0, 0.0).astype(bf16)


def _pair_norm(x, gain, bavg):
    ms = rmul_const(x * x, bavg, bavg)
    return x * lax.rsqrt(ms + NORM_EPS) * gain


def _rms(x):
    r = lax.rsqrt(jnp.mean(x * x, axis=-1, keepdims=True) + NORM_EPS)
    return r


_IN_GROUPS = ((C_QKV, C_Z), (C_Z, C_AB), (C_AB, C_SB), (C_SB, C_SG), (C_SG, IN_PAD))


def inproj_fwd(x, g, wp, tm=256):
    m = x.shape[0]

    def body(x_ref, g_ref, w_ref, *outs):
        xv = x_ref[...]
        h = (xv * _rms(xv) * g_ref[...]).astype(bf16)
        for (a, b), o in zip(_IN_GROUPS, outs):
            o[...] = _pdot(h, w_ref[:, a:b])

    return pl.pallas_call(
        body, name="inproj_fwd", grid=(m // tm,),
        in_specs=[pl.BlockSpec((tm, D_MODEL), lambda i: (i, 0)), pl.BlockSpec((1, D_MODEL), lambda i: (0, 0)),
                  pl.BlockSpec((D_MODEL, IN_PAD), lambda i: (0, 0))],
        out_specs=[pl.BlockSpec((tm, b - a), lambda i: (i, 0)) for a, b in _IN_GROUPS],
        out_shape=[SDS((m, b - a), f32) for a, b in _IN_GROUPS],
        compiler_params=_cp(("arbitrary",)),
    )(x, g, wp)


def inproj_bwd(x, g, wp, dproj, dres, tm=256):
    m = x.shape[0]

    def body(x_ref, g_ref, w_ref, dp_ref, dr_ref, dx_ref, dg_ref, h_ref):
        xv = x_ref[...]
        r = _rms(xv)
        xn = xv * r
        gv = g_ref[...]
        h_ref[...] = (xn * gv).astype(bf16)
        dh = lax.dot_general(dp_ref[...], w_ref[...], NT, preferred_element_type=f32)
        dxn = dh * gv
        dx_ref[...] = dr_ref[...] + r * (dxn - xn * jnp.mean(dxn * xn, axis=-1, keepdims=True))

        @pl.when(pl.program_id(0) == 0)
        def _():
            dg_ref[...] = jnp.zeros_like(dg_ref)

        dg_ref[...] += jnp.sum(dh * xn, axis=0, keepdims=True)

    return pl.pallas_call(
        body, name="inproj_bwd", grid=(m // tm,),
        in_specs=[pl.BlockSpec((tm, D_MODEL), lambda i: (i, 0)), pl.BlockSpec((1, D_MODEL), lambda i: (0, 0)),
                  pl.BlockSpec((D_MODEL, IN_PAD), lambda i: (0, 0)), pl.BlockSpec((tm, IN_PAD), lambda i: (i, 0)),
                  pl.BlockSpec((tm, D_MODEL), lambda i: (i, 0))],
        out_specs=[pl.BlockSpec((tm, D_MODEL), lambda i: (i, 0)), pl.BlockSpec((1, D_MODEL), lambda i: (0, 0)),
                   pl.BlockSpec((tm, D_MODEL), lambda i: (i, 0))],
        out_shape=[SDS((m, D_MODEL), f32), SDS((1, D_MODEL), f32), SDS((m, D_MODEL), bf16)],
        compiler_params=_cp(("arbitrary",)),
    )(x, g, wp, dproj, dres)


def outproj_fwd(x, odn, osb, osg, wo, tm=512):
    m = x.shape[0]

    def body(x_ref, a_ref, b_ref, c_ref, w_ref, x2_ref, mix_ref):
        mix_ref[:, 0:DN_WIDTH] = a_ref[...].astype(bf16)
        mix_ref[:, DN_WIDTH:DN_WIDTH + SB_WIDTH] = b_ref[...].astype(bf16)
        mix_ref[:, DN_WIDTH + SB_WIDTH:D_MODEL] = c_ref[...].astype(bf16)
        x2_ref[...] = x_ref[...] + _pdot(mix_ref[...], w_ref[...])

    row = lambda w: pl.BlockSpec((tm, w), lambda i: (i, 0))
    return pl.pallas_call(
        body, name="outproj_fwd", grid=(m // tm,),
        in_specs=[row(D_MODEL), row(DN_WIDTH), row(SB_WIDTH), row(SG_WIDTH), pl.BlockSpec((D_MODEL, D_MODEL), lambda i: (0, 0))],
        out_specs=[row(D_MODEL), row(D_MODEL)],
        out_shape=[SDS((m, D_MODEL), f32), SDS((m, D_MODEL), bf16)],
        compiler_params=_cp(("arbitrary",)),
    )(x, odn, osb, osg, wo)


def outproj_bwd(dx2, wo, tm=512):
    m = dx2.shape[0]

    def body(d_ref, w_ref, a_ref, b_ref, c_ref, db_ref):
        db = d_ref[...].astype(bf16)
        db_ref[...] = db
        dm = lax.dot_general(db, w_ref[...], NT, preferred_element_type=f32)
        a_ref[...] = dm[:, 0:DN_WIDTH]
        b_ref[...] = dm[:, DN_WIDTH:DN_WIDTH + SB_WIDTH]
        c_ref[...] = dm[:, DN_WIDTH + SB_WIDTH:D_MODEL]

    row = lambda w: pl.BlockSpec((tm, w), lambda i: (i, 0))
    return pl.pallas_call(
        body, name="outproj_bwd", grid=(m // tm,),
        in_specs=[row(D_MODEL), pl.BlockSpec((D_MODEL, D_MODEL), lambda i: (0, 0))],
        out_specs=[row(DN_WIDTH), row(SB_WIDTH), row(SG_WIDTH), row(D_MODEL)],
        out_shape=[SDS((m, DN_WIDTH), f32), SDS((m, SB_WIDTH), f32), SDS((m, SG_WIDTH), f32), SDS((m, D_MODEL), bf16)],
        compiler_params=_cp(("arbitrary",)),
    )(dx2, wo)


FF_CHUNK = 1024


def _load_weights_once(pairs, sem):
    @pl.when(pl.program_id(0) == 0)
    def _():
        cps = [pltpu.make_async_copy(h, v, sem.at[i]) for i, (h, v) in enumerate(pairs)]
        for c in cps:
            c.start()
        for c in cps:
            c.wait()


def ffn_fwd(x2, g, w1, w2, tm=256):
    m = x2.shape[0]

    def body(x_ref, g_ref, w1_hbm, w2_hbm, y_ref, w1_v, w2_v, sem):
        _load_weights_once(((w1_hbm, w1_v), (w2_hbm, w2_v)), sem)
        xv = x_ref[...]
        h = (xv * _rms(xv) * g_ref[...]).astype(bf16)
        acc = xv
        for j in range(0, D_FF, FF_CHUNK):
            f = _pdot(h, w1_v[:, j:j + FF_CHUNK])
            rl = jnp.maximum(f, 0.0)
            acc = acc + _pdot((rl * rl).astype(bf16), w2_v[j:j + FF_CHUNK, :])
        y_ref[...] = acc

    return pl.pallas_call(
        body, name="ffn_fwd", grid=(m // tm,),
        in_specs=[pl.BlockSpec((tm, D_MODEL), lambda i: (i, 0)), pl.BlockSpec((1, D_MODEL), lambda i: (0, 0)),
                  pl.BlockSpec(memory_space=pl.ANY), pl.BlockSpec(memory_space=pl.ANY)],
        out_specs=pl.BlockSpec((tm, D_MODEL), lambda i: (i, 0)),
        out_shape=SDS((m, D_MODEL), f32),
        scratch_shapes=[pltpu.VMEM((D_MODEL, D_FF), bf16), pltpu.VMEM((D_FF, D_MODEL), bf16), pltpu.SemaphoreType.DMA((2,))],
        compiler_params=_cp(("arbitrary",)),
    )(x2, g, w1, w2)


def ffn_bwd(x2, g, w1, w2, dy, tm=256):
    m = x2.shape[0]

    def body(x_ref, g_ref, w1_hbm, w2_hbm, dy_ref, dx_ref, dg_ref, h_ref, a_ref, df_ref, dyb_ref, w1_v, w2_v, sem):
        _load_weights_once(((w1_hbm, w1_v), (w2_hbm, w2_v)), sem)
        xv = x_ref[...]
        r = _rms(xv)
        xn = xv * r
        gv = g_ref[...]
        h = (xn * gv).astype(bf16)
        h_ref[...] = h
        dyv = dy_ref[...]
        dyb = dyv.astype(bf16)
        dyb_ref[...] = dyb
        dh = jnp.zeros((tm, D_MODEL), f32)
        for j in range(0, D_FF, FF_CHUNK):
            f = _pdot(h, w1_v[:, j:j + FF_CHUNK])
            rl = jnp.maximum(f, 0.0)
            a_ref[:, j:j + FF_CHUNK] = (rl * rl).astype(bf16)
            da = lax.dot_general(dyb, w2_v[j:j + FF_CHUNK, :], NT, preferred_element_type=f32)
            df = (da * (2.0 * rl)).astype(bf16)
            df_ref[:, j:j + FF_CHUNK] = df
            dh = dh + lax.dot_general(df, w1_v[:, j:j + FF_CHUNK], NT, preferred_element_type=f32)
        dxn = dh * gv
        dx_ref[...] = dyv + r * (dxn - xn * jnp.mean(dxn * xn, axis=-1, keepdims=True))

        @pl.when(pl.program_id(0) == 0)
        def _():
            dg_ref[...] = jnp.zeros_like(dg_ref)

        dg_ref[...] += jnp.sum(dh * xn, axis=0, keepdims=True)

    row = lambda w: pl.BlockSpec((tm, w), lambda i: (i, 0))
    return pl.pallas_call(
        body, name="ffn_bwd", grid=(m // tm,),
        in_specs=[row(D_MODEL), pl.BlockSpec((1, D_MODEL), lambda i: (0, 0)),
                  pl.BlockSpec(memory_space=pl.ANY), pl.BlockSpec(memory_space=pl.ANY), row(D_MODEL)],
        out_specs=[row(D_MODEL), pl.BlockSpec((1, D_MODEL), lambda i: (0, 0)), row(D_MODEL), row(D_FF), row(D_FF), row(D_MODEL)],
        out_shape=[SDS((m, D_MODEL), f32), SDS((1, D_MODEL), f32), SDS((m, D_MODEL), bf16), SDS((m, D_FF), bf16),
                   SDS((m, D_FF), bf16), SDS((m, D_MODEL), bf16)],
        scratch_shapes=[pltpu.VMEM((D_MODEL, D_FF), bf16), pltpu.VMEM((D_FF, D_MODEL), bf16), pltpu.SemaphoreType.DMA((2,))],
        compiler_params=_cp(("arbitrary",)),
    )(x2, g, w1, w2, dy)


def _tile(n, cap):
    best = 128
    for t in range(128, cap + 1, 128):
        if n % t == 0:
            best = t
    return best


def tn_matmul(a, b, name, col_shards=1, tk=512):
    m, ka = a.shape
    n = b.shape[1]
    ti = _tile(ka, 1024)
    tj = _tile(n // col_shards, 1152)
    nk = m // tk
    jps = (n // col_shards) // tj

    def body(a_ref, b_ref, o_ref, acc):
        k = pl.program_id(2)

        @pl.when(k == 0)
        def _():
            acc[...] = jnp.zeros_like(acc)

        acc[...] += lax.dot_general(a_ref[...], b_ref[...], TN, preferred_element_type=f32)

        @pl.when(k == nk - 1)
        def _():
            o_ref[...] = acc[...].astype(bf16).reshape(o_ref.shape)

    if col_shards == 1:
        out_shape, out_spec = SDS((ka, n), bf16), pl.BlockSpec((ti, tj), lambda i, j, k: (i, j))
    else:
        out_shape = SDS((col_shards, ka, n // col_shards), bf16)
        out_spec = pl.BlockSpec((1, ti, tj), lambda i, j, k: (j // jps, i, j % jps))
    return pl.pallas_call(
        body, name=name, grid=(ka // ti, n // tj, nk),
        in_specs=[pl.BlockSpec((tk, ti), lambda i, j, k: (k, i)), pl.BlockSpec((tk, tj), lambda i, j, k: (k, j))],
        out_specs=out_spec, out_shape=out_shape,
        scratch_shapes=[pltpu.VMEM((ti, tj), f32)],
        compiler_params=_cp(("arbitrary", "arbitrary", "arbitrary")),
    )(a, b)


def loss_head(y, tgt, tm=512):
    m = y.shape[0]

    def body(y_ref, t_ref, dy_ref, l_ref):
        e = y_ref[...] - t_ref[...]
        dy_ref[...] = e * (1.0 / D_MODEL)

        @pl.when(pl.program_id(0) == 0)
        def _():
            l_ref[...] = jnp.zeros_like(l_ref)

        l_ref[...] += jnp.sum(e * e, axis=0, keepdims=True) * (0.5 / D_MODEL)

    row = pl.BlockSpec((tm, D_MODEL), lambda i: (i, 0))
    return pl.pallas_call(
        body, name="loss_head", grid=(m // tm,), in_specs=[row, row],
        out_specs=[row, pl.BlockSpec((1, D_MODEL), lambda i: (0, 0))],
        out_shape=[SDS((m, D_MODEL), f32), SDS((1, D_MODEL), f32)],
        compiler_params=_cp(("arbitrary",)),
    )(y, tgt)


def _dn_consts():
    c = DN_CHUNK
    r, cc = _iota2((c, c))
    lt = (cc <= r).astype(bf16)
    ltt = (r <= cc).astype(bf16)
    return lt, ltt


def dn_chunk(cq, ck, cv, a, b, z, s, alog, dtb, gain, lt, ltt, t_given=None):
    c = DN_CHUNK
    r, cc = _iota2((c, c))
    q = cq * lax.rsqrt(jnp.sum(cq * cq, axis=-1, keepdims=True) + NORM_EPS) * (DN_DIM ** -0.5)
    k = ck * lax.rsqrt(jnp.sum(ck * ck, axis=-1, keepdims=True) + NORM_EPS)
    g = -jnp.exp(alog) * _softplus(a + dtb)
    beta = _sigmoid(b)
    r2, c2 = _iota2((c, 128))
    uaug = jnp.where((c2 < c) & (r2 > c2), 1.0, 0.0) + jnp.where(c2 == c, 1.0, 0.0)
    gam_all = lmul_const(lt, ltt, g * uaug)
    gam_cc = gam_all[:, :, 0:c]
    gam = gam_all[:, :, c:c + 1]
    dec = jnp.where(cc <= r, jnp.exp(jnp.where(cc <= r, gam_cc, 0.0)), 0.0)
    kk = mm_nt(k, k)
    lm = jnp.where(cc < r, beta * kk * dec, 0.0)
    t = inv_unit_lower(lm) if t_given is None else inv_given(lm, t_given)
    eg = jnp.exp(gam)
    sol = mm_hl(t, jnp.concatenate([cv * beta, k * (beta * eg)], axis=2))
    u, w = sol[:, :, 0:DN_DIM], sol[:, :, DN_DIM:2 * DN_DIM]
    qk = jnp.where(cc <= r, mm_nt(q, k) * dec, 0.0)
    glast = jnp.sum(g, axis=1, keepdims=True)
    qd = q * eg
    kd = k * jnp.exp(glast - gam)
    un = u - mm(w, s)
    o = mm(qd, s) + mm(qk, un)
    s_new = s * jnp.exp(glast) + mm_tn(kd, un)
    on = o * lax.rsqrt(jnp.mean(o * o, axis=-1, keepdims=True) + NORM_EPS) * gain * _silu(z)
    return on, s_new, t


def _dn_chains(cacts, ab_ref, z_ref, al_ref, dt_ref):
    cq, ck, cv, a, b, z, al, dt = [], [], [], [], [], [], [], []
    for bi, cact in enumerate(cacts):
        for h in range(DN_HEADS):
            hs = slice(h * DN_DIM, (h + 1) * DN_DIM)
            cq.append(cact[:, h * DN_DIM:(h + 1) * DN_DIM])
            ck.append(cact[:, DN_WIDTH + h * DN_DIM:DN_WIDTH + (h + 1) * DN_DIM])
            cv.append(cact[:, 2 * DN_WIDTH + h * DN_DIM:2 * DN_WIDTH + (h + 1) * DN_DIM])
            a.append(ab_ref[bi, :, h:h + 1])
            b.append(ab_ref[bi, :, DN_HEADS + h:DN_HEADS + h + 1])
            z.append(z_ref[bi, :, hs])
            al.append(al_ref[0:1, h:h + 1])
            dt.append(dt_ref[0:1, h:h + 1])
    return tuple(jnp.stack(v) for v in (cq, ck, cv, a, b, z)), jnp.stack(al), jnp.stack(dt)


def _conv_rows(xe_ref, b, w_ref):
    y = w_ref[0:1, :] * xe_ref[b, pl.ds(5, DN_CHUNK), :]
    for i in range(1, DN_CONV):
        y = y + w_ref[i:i + 1, :] * xe_ref[b, pl.ds(5 + i, DN_CHUNK), :]
    return y


def dn_fwd(qkv, z, ab, conv_w, alog, dtb, gain):
    bsz, t, _ = qkv.shape
    nc = t // DN_CHUNK
    c = DN_CHUNK
    nh = bsz * DN_HEADS

    def body(qkv_ref, z_ref, ab_ref, w_ref, al_ref, dt_ref, g_ref, o_ref, sall_ref, tall_ref, xe, s_sc):
        n = pl.program_id(0)

        @pl.when(n == 0)
        def _():
            xe[:, 0:8, :] = jnp.zeros((bsz, 8, 3 * DN_WIDTH), f32)
            s_sc[...] = jnp.zeros_like(s_sc)

        lt, ltt = _dn_consts()
        cacts = []
        for b in range(bsz):
            xe[b, 8:8 + c, :] = qkv_ref[b]
            cacts.append(_silu(_conv_rows(xe, b, w_ref)))
            xe[b, 0:8, :] = xe[b, c:c + 8, :]
        ops, al, dt = _dn_chains(cacts, ab_ref, z_ref, al_ref, dt_ref)
        s = s_sc[...]
        sall_ref[0] = s
        on, sn, tt = dn_chunk(*ops, s, al, dt, g_ref[...], lt, ltt)
        tall_ref[0] = tt
        s_sc[...] = sn
        for b in range(bsz):
            for h in range(DN_HEADS):
                o_ref[b, :, h * DN_DIM:(h + 1) * DN_DIM] = on[b * DN_HEADS + h]

    blk = lambda w: pl.BlockSpec((bsz, c, w), lambda n: (0, n, 0))
    full = lambda shp: pl.BlockSpec(shp, lambda n: (0,) * len(shp))
    return pl.pallas_call(
        body, name="dn_fwd", grid=(nc,),
        in_specs=[blk(3 * DN_WIDTH), blk(DN_WIDTH), blk(128), full((8, 3 * DN_WIDTH)), full((1, 128)), full((1, 128)), full((1, 128))],
        out_specs=[blk(DN_WIDTH), pl.BlockSpec((1, nh, DN_DIM, DN_DIM), lambda n: (n, 0, 0, 0)),
                   pl.BlockSpec((1, nh, c, c), lambda n: (n, 0, 0, 0))],
        out_shape=[SDS((bsz, t, DN_WIDTH), f32), SDS((nc, nh, DN_DIM, DN_DIM), f32), SDS((nc, nh, c, c), f32)],
        scratch_shapes=[pltpu.VMEM((bsz, c + 8, 3 * DN_WIDTH), f32), pltpu.VMEM((nh, DN_DIM, DN_DIM), f32)],
        compiler_params=_cp(("arbitrary",)),
    )(qkv, z, ab, conv_w, alog, dtb, gain)


def dn_bwd(qkv, z, ab, conv_w, alog, dtb, gain, sall, tall, do):
    bsz, t, _ = qkv.shape
    nc = t // DN_CHUNK
    c = DN_CHUNK
    nh = bsz * DN_HEADS
    w3 = 3 * DN_WIDTH

    def body(qkv_ref, prev_ref, z_ref, ab_ref, w_ref, al_ref, dt_ref, g_ref, sall_ref, tall_ref, do_ref,
             dqkv_ref, dz_ref, dab_ref, dw_ref, dal_ref, ddt_ref, dg_ref, xe, dye, dc_sc, ds_sc):
        n = pl.program_id(0)
        first = (nc - 1 - n) == 0

        @pl.when(n == 0)
        def _():
            dye[:, c:c + 8, :] = jnp.zeros((bsz, 8, w3), f32)
            ds_sc[...] = jnp.zeros_like(ds_sc)
            dw_ref[...] = jnp.zeros_like(dw_ref)
            dal_ref[...] = jnp.zeros_like(dal_ref)
            ddt_ref[...] = jnp.zeros_like(ddt_ref)
            dg_ref[...] = jnp.zeros_like(dg_ref)

        lt, ltt = _dn_consts()
        lane = lax.broadcasted_iota(jnp.int32, (1, 128), 1)
        lane_c = lax.broadcasted_iota(jnp.int32, (c, 128), 1)
        ys, sigs = [], []
        for b in range(bsz):
            xe[b, 0:8, :] = jnp.where(first, 0.0, prev_ref[b])
            xe[b, 8:8 + c, :] = qkv_ref[b]
            ys.append(_conv_rows(xe, b, w_ref))
            sigs.append(_sigmoid(ys[b]))
        ops, al, dt = _dn_chains([y * sg for y, sg in zip(ys, sigs)], ab_ref, z_ref, al_ref, dt_ref)
        tt = tall_ref[0]
        _, vjp = jax.vjp(lambda *p: dn_chunk(*p, lt, ltt, t_given=tt)[0:2], *ops, sall_ref[0], al, dt, g_ref[...])
        don = jnp.stack([do_ref[b, :, h * DN_DIM:(h + 1) * DN_DIM] for b in range(bsz) for h in range(DN_HEADS)])
        dcq, dck, dcv, da, db, dzz, dsp, dal, ddt, dgn = vjp((don, ds_sc[...]))
        ds_sc[...] = dsp
        dg_ref[...] += dgn
        for b in range(bsz):
            dab = jnp.zeros((c, 128), f32)
            for h in range(DN_HEADS):
                i = b * DN_HEADS + h
                dc_sc[b, :, h * DN_DIM:(h + 1) * DN_DIM] = dcq[i]
                dc_sc[b, :, DN_WIDTH + h * DN_DIM:DN_WIDTH + (h + 1) * DN_DIM] = dck[i]
                dc_sc[b, :, 2 * DN_WIDTH + h * DN_DIM:2 * DN_WIDTH + (h + 1) * DN_DIM] = dcv[i]
                dz_ref[b, :, h * DN_DIM:(h + 1) * DN_DIM] = dzz[i].astype(bf16)
                dab = dab + jnp.where(lane_c == h, da[i], 0.0) + jnp.where(lane_c == DN_HEADS + h, db[i], 0.0)
                dal_ref[...] += jnp.where(lane == h, dal[i], 0.0)
                ddt_ref[...] += jnp.where(lane == h, ddt[i], 0.0)
            dab_ref[b] = dab.astype(bf16)
            y, sig = ys[b], sigs[b]
            dy = dc_sc[b] * (sig * (1.0 + y * (1.0 - sig)))
            dye[b, 0:c, :] = dy
            dx = w_ref[3:4, :] * dy
            for i in range(DN_CONV - 1):
                dx = dx + w_ref[i:i + 1, :] * dye[b, pl.ds(3 - i, c), :]
            dqkv_ref[b] = dx.astype(bf16)
            for i in range(DN_CONV):
                dw_ref[i:i + 1, :] += jnp.sum(dy * xe[b, pl.ds(5 + i, c), :], axis=0, keepdims=True)
            dye[b, c:c + 8, :] = dye[b, 0:8, :]

    rev = lambda w: pl.BlockSpec((bsz, c, w), lambda n: (0, nc - 1 - n, 0))
    full = lambda shp: pl.BlockSpec(shp, lambda n: (0,) * len(shp))
    prev = pl.BlockSpec((bsz, 8, w3), lambda n: (0, jnp.maximum((nc - 1 - n) * (c // 8) - 1, 0), 0))
    return pl.pallas_call(
        body, name="dn_bwd", grid=(nc,),
        in_specs=[rev(w3), prev, rev(DN_WIDTH), rev(128), full((8, w3)), full((1, 128)), full((1, 128)), full((1, 128)),
                  pl.BlockSpec((1, nh, DN_DIM, DN_DIM), lambda n: (nc - 1 - n, 0, 0, 0)),
                  pl.BlockSpec((1, nh, c, c), lambda n: (nc - 1 - n, 0, 0, 0)), rev(DN_WIDTH)],
        out_specs=[rev(w3), rev(DN_WIDTH), rev(128), full((8, w3)), full((1, 128)), full((1, 128)), full((1, 128))],
        out_shape=[SDS((bsz, t, w3), bf16), SDS((bsz, t, DN_WIDTH), bf16), SDS((bsz, t, 128), bf16),
                   SDS((8, w3), f32), SDS((1, 128), f32), SDS((1, 128), f32), SDS((1, 128), f32)],
        scratch_shapes=[pltpu.VMEM((bsz, c + 8, w3), f32), pltpu.VMEM((bsz, c + 8, w3), f32), pltpu.VMEM((bsz, c, w3), f32),
                        pltpu.VMEM((nh, DN_DIM, DN_DIM), f32)],
        compiler_params=_cp(("arbitrary",)),
    )(qkv, qkv, z, ab, conv_w, alog, dtb, gain, sall, tall, do)


SB_TILE = 256


def sb_fwd(sbqkv, gq, gk):
    bsz, t, _ = sbqkv.shape
    blk = min(SB_TILE, t)
    nq = t // blk
    scale = SB_DIM ** -0.5

    def body(q_ref, k_ref, v_ref, gq_ref, gk_ref, o_ref, l_ref, q2_sc, kn_sc, v_sc):
        bavg = _group_avg_mats()
        lane = lax.broadcasted_iota(jnp.int32, (1, 128), 1)
        first = lane < SB_DIM
        qn = _pair_norm(q_ref[0], gq_ref[...], bavg)
        kn_sc[...] = _pair_norm(k_ref[0], gk_ref[...], bavg).astype(bf16)
        v_sc[...] = v_ref[0].astype(bf16)
        q2_sc[0] = jnp.where(first, qn, 0.0).astype(bf16)
        q2_sc[1] = jnp.where(first, 0.0, qn).astype(bf16)
        r, c = _iota2((blk, blk))
        ustrict = (r > c).astype(bf16)
        r2, c2 = _iota2((2 * blk, blk))
        causal = c2 < (r2 & (blk - 1))

        def tile(q2, ks, acc, rr, diag):
            zz = lax.dot_general(q2, kn_sc[pl.ds(ks, blk), :], NT, preferred_element_type=f32) * scale
            sp = _softplus(zz)
            lm = jnp.where(causal, -sp, 0.0) if diag else -sp
            rem = _dot_x2c(lm, ustrict)
            wgt = jnp.exp(zz - sp + rem + rr)
            if diag:
                wgt = jnp.where(causal, wgt, 0.0)
            acc = acc + _pdot(wgt.astype(bf16), v_sc[pl.ds(ks, blk), :])
            return acc, rr + jnp.sum(lm, axis=1, keepdims=True)

        def qloop(qi, _):
            qs = pl.multiple_of(qi * blk, blk)
            q2 = jnp.concatenate([q2_sc[0, pl.ds(qs, blk), :], q2_sc[1, pl.ds(qs, blk), :]], axis=0)
            carry = tile(q2, qs, jnp.zeros((2 * blk, 128), f32), jnp.zeros((2 * blk, 1), f32), True)
            acc, rr = lax.fori_loop(1, qi + 1, lambda i, cr: tile(q2, pl.multiple_of((qi - i) * blk, blk), *cr, False), carry)
            o_ref[0, pl.ds(qs, blk), :] = jnp.where(first, acc[0:blk], acc[blk:2 * blk])
            l_ref[0, pl.ds(qs, blk), :] = jnp.where(first, rr[0:blk], rr[blk:2 * blk])
            return 0

        lax.fori_loop(0, nq, qloop, 0)

    col = lambda off: pl.BlockSpec((1, t, 128), lambda b, p: (b, 0, off + p))
    gsp = pl.BlockSpec((1, 128), lambda b, p: (0, 0))
    return pl.pallas_call(
        body, name="sb_fwd", grid=(bsz, 2),
        in_specs=[col(0), col(2), col(4), gsp, gsp],
        out_specs=[col(0), col(0)],
        out_shape=[SDS((bsz, t, SB_WIDTH), f32), SDS((bsz, t, SB_WIDTH), f32)],
        scratch_shapes=[pltpu.VMEM((2, t, 128), bf16), pltpu.VMEM((t, 128), bf16), pltpu.VMEM((t, 128), bf16)],
        compiler_params=_cp(("arbitrary", "arbitrary")),
    )(sbqkv, sbqkv, sbqkv, gq, gk)


def sb_bwd(sbqkv, gq, gk, ltot, do):
    bsz, t, _ = sbqkv.shape
    blk = min(SB_TILE, t)
    nq = t // blk
    scale = SB_DIM ** -0.5

    def body(q_ref, k_ref, v_ref, gq_ref, gk_ref, l_ref, do_ref, dq_ref, dk_ref, dv_ref, dgq_ref, dgk_ref,
             q2_sc, kn_sc, v_sc, do2_sc, dqn_sc, dkn_sc, dv_sc):
        bavg = _group_avg_mats()
        lane = lax.broadcasted_iota(jnp.int32, (1, 128), 1)
        first = lane < SB_DIM
        fq = lambda x, g: _pair_norm(x, g, bavg)
        qn, q_vjp = jax.vjp(fq, q_ref[0], gq_ref[...])
        kn, k_vjp = jax.vjp(fq, k_ref[0], gk_ref[...])
        kn_sc[...] = kn.astype(bf16)
        v_sc[...] = v_ref[0].astype(bf16)
        dov = do_ref[0]
        q2_sc[0] = jnp.where(first, qn, 0.0).astype(bf16)
        q2_sc[1] = jnp.where(first, 0.0, qn).astype(bf16)
        do2_sc[0] = jnp.where(first, dov, 0.0).astype(bf16)
        do2_sc[1] = jnp.where(first, 0.0, dov).astype(bf16)
        dkn_sc[...] = jnp.zeros_like(dkn_sc)
        dv_sc[...] = jnp.zeros_like(dv_sc)
        r, c = _iota2((blk, blk))
        pincl = (r <= c).astype(bf16)
        pstrict = (r < c).astype(bf16)
        r2, c2 = _iota2((2 * blk, blk))
        causal = c2 < (r2 & (blk - 1))

        def tile(q2, do2, lt, ks, dq, cs, ce, diag):
            kb = kn_sc[pl.ds(ks, blk), :]
            zz = lax.dot_general(q2, kb, NT, preferred_element_type=f32) * scale
            sp = _softplus(zz)
            lm = jnp.where(causal, -sp, 0.0) if diag else -sp
            pre = _dot_x2c(lm, pincl)
            lp = zz - sp
            wgt = jnp.exp(lp + (lt - cs - pre))
            if diag:
                wgt = jnp.where(causal, wgt, 0.0)
            dw = lax.dot_general(do2, v_sc[pl.ds(ks, blk), :], NT, preferred_element_type=f32)
            e = wgt * dw
            ee = ce + _dot_x2c(e, pstrict)
            sig = jnp.exp(lp)
            dz = (e * (1.0 - sig) - ee * sig) * scale
            if diag:
                dz = jnp.where(causal, dz, 0.0)
            dz = dz.astype(bf16)
            dq = dq + _pdot(dz, kb)
            dkn_sc[pl.ds(ks, blk), :] += lax.dot_general(dz, q2, TN, preferred_element_type=f32)
            dv_sc[pl.ds(ks, blk), :] += lax.dot_general(wgt.astype(bf16), do2, TN, preferred_element_type=f32)
            return dq, cs + jnp.sum(lm, axis=1, keepdims=True), ce + jnp.sum(e, axis=1, keepdims=True)

        def qloop(qi, _):
            qs = pl.multiple_of(qi * blk, blk)
            q2 = jnp.concatenate([q2_sc[0, pl.ds(qs, blk), :], q2_sc[1, pl.ds(qs, blk), :]], axis=0)
            do2 = jnp.concatenate([do2_sc[0, pl.ds(qs, blk), :], do2_sc[1, pl.ds(qs, blk), :]], axis=0)
            lt = jnp.concatenate([l_ref[0, pl.ds(qs, blk), 0:1], l_ref[0, pl.ds(qs, blk), SB_DIM:SB_DIM + 1]], axis=0)
            z1 = jnp.zeros((2 * blk, 1), f32)
            carry = lax.fori_loop(0, qi, lambda kj, cr: tile(q2, do2, lt, pl.multiple_of(kj * blk, blk), *cr, False),
                                  (jnp.zeros((2 * blk, 128), f32), z1, z1))
            dq, _, _ = tile(q2, do2, lt, qs, *carry, True)
            dqn_sc[pl.ds(qs, blk), :] = jnp.where(first, dq[0:blk], dq[blk:2 * blk])
            return 0

        lax.fori_loop(0, nq, qloop, 0)
        dq_pre, dgq = q_vjp(dqn_sc[...])
        dk_pre, dgk = k_vjp(dkn_sc[...])
        dq_ref[0] = dq_pre.astype(bf16)
        dk_ref[0] = dk_pre.astype(bf16)
        dv_ref[0] = dv_sc[...].astype(bf16)
        dgq_ref[0] = jnp.broadcast_to(dgq, (8, 128))
        dgk_ref[0] = jnp.broadcast_to(dgk, (8, 128))

    col = lambda off: pl.BlockSpec((1, t, 128), lambda b, p: (b, 0, off + p))
    gsp = pl.BlockSpec((1, 128), lambda b, p: (0, 0))
    gout = pl.BlockSpec((1, 8, 128), lambda b, p: (b * 2 + p, 0, 0))
    return pl.pallas_call(
        body, name="sb_bwd", grid=(bsz, 2),
        in_specs=[col(0), col(2), col(4), gsp, gsp, col(0), col(0)],
        out_specs=[col(0), col(0), col(0), gout, gout],
        out_shape=[SDS((bsz, t, SB_WIDTH), bf16)] * 3 + [SDS((bsz * 2, 8, 128), f32)] * 2,
        scratch_shapes=[pltpu.VMEM((2, t, 128), bf16), pltpu.VMEM((t, 128), bf16), pltpu.VMEM((t, 128), bf16),
                        pltpu.VMEM((2, t, 128), bf16), pltpu.VMEM((t, 128), f32), pltpu.VMEM((t, 128), f32), pltpu.VMEM((t, 128), f32)],
        compiler_params=_cp(("arbitrary", "arbitrary")),
    )(sbqkv, sbqkv, sbqkv, gq, gk, ltot, do)


def sg_pair(u, v, gain, wa, wb, ba, bb, bavg):
    r, c = _iota2((SG_CHUNK, SG_CHUNK))
    lane = lax.broadcasted_iota(jnp.int32, (1, 128), 1)
    first = lane < SG_DIM
    vn = _pair_norm(_gelu(v), gain, bavg)
    tri = c <= r
    mixed = (mm(jnp.where(tri, wa, 0.0), jnp.where(first, vn, 0.0)) + mm(jnp.where(tri, wb, 0.0), jnp.where(first, 0.0, vn))
             + jnp.where(first, ba, bb))
    return _gelu(u) * mixed


def sg_fwd(sguv, gain, w, bt):
    bsz, t, _ = sguv.shape
    nch = t // SG_CHUNK

    def body(uv_ref, g_ref, w_ref, b_ref, o_ref):
        bavg = _group_avg_mats()
        for p in range(2):
            ls = slice(p * 128, (p + 1) * 128)
            o_ref[0, :, ls] = sg_pair(uv_ref[0, :, ls], uv_ref[0, :, SG_WIDTH + p * 128:SG_WIDTH + (p + 1) * 128], g_ref[:, ls],
                                      w_ref[2 * p], w_ref[2 * p + 1], b_ref[:, 2 * p:2 * p + 1], b_ref[:, 2 * p + 1:2 * p + 2], bavg)

    full = lambda shp: pl.BlockSpec(shp, lambda b, n: (0,) * len(shp))
    return pl.pallas_call(
        body, name="sg_fwd", grid=(bsz, nch),
        in_specs=[pl.BlockSpec((1, SG_CHUNK, 2 * SG_WIDTH), lambda b, n: (b, n, 0)), full((1, SG_WIDTH)),
                  full((SG_GROUPS, SG_CHUNK, SG_CHUNK)), full((SG_CHUNK, 128))],
        out_specs=pl.BlockSpec((1, SG_CHUNK, SG_WIDTH), lambda b, n: (b, n, 0)),
        out_shape=SDS((bsz, t, SG_WIDTH), f32),
        compiler_params=_cp(("arbitrary", "arbitrary")),
    )(sguv, gain, w, bt)


def sg_bwd(sguv, gain, w, bt, do):
    bsz, t, _ = sguv.shape
    nch = t // SG_CHUNK

    def body(uv_ref, g_ref, w_ref, b_ref, do_ref, duv_ref, dg_ref, dw_ref, db_ref):
        @pl.when((pl.program_id(0) == 0) & (pl.program_id(1) == 0))
        def _():
            dg_ref[...] = jnp.zeros_like(dg_ref)
            dw_ref[...] = jnp.zeros_like(dw_ref)
            db_ref[...] = jnp.zeros_like(db_ref)

        bavg = _group_avg_mats()
        lane = lax.broadcasted_iota(jnp.int32, (SG_CHUNK, 128), 1)
        dbt = jnp.zeros((SG_CHUNK, 128), f32)
        for p in range(2):
            ls = slice(p * 128, (p + 1) * 128)
            vs = slice(SG_WIDTH + p * 128, SG_WIDTH + (p + 1) * 128)
            prim = (uv_ref[0, :, ls], uv_ref[0, :, vs], g_ref[:, ls], w_ref[2 * p], w_ref[2 * p + 1],
                    b_ref[:, 2 * p:2 * p + 1], b_ref[:, 2 * p + 1:2 * p + 2])
            _, vjp = jax.vjp(lambda *a: sg_pair(*a, bavg), *prim)
            du, dv, dgn, dwa, dwb, dba, dbb = vjp(do_ref[0, :, ls])
            duv_ref[0, :, ls] = du.astype(bf16)
            duv_ref[0, :, vs] = dv.astype(bf16)
            dg_ref[:, ls] += dgn
            dw_ref[2 * p] += dwa
            dw_ref[2 * p + 1] += dwb
            dbt = dbt + jnp.where(lane == 2 * p, dba, 0.0) + jnp.where(lane == 2 * p + 1, dbb, 0.0)
        db_ref[...] += dbt

    full = lambda shp: pl.BlockSpec(shp, lambda b, n: (0,) * len(shp))
    return pl.pallas_call(
        body, name="sg_bwd", grid=(bsz, nch),
        in_specs=[pl.BlockSpec((1, SG_CHUNK, 2 * SG_WIDTH), lambda b, n: (b, n, 0)), full((1, SG_WIDTH)),
                  full((SG_GROUPS, SG_CHUNK, SG_CHUNK)), full((SG_CHUNK, 128)),
                  pl.BlockSpec((1, SG_CHUNK, SG_WIDTH), lambda b, n: (b, n, 0))],
        out_specs=[pl.BlockSpec((1, SG_CHUNK, 2 * SG_WIDTH), lambda b, n: (b, n, 0)), full((1, SG_WIDTH)),
                   full((SG_GROUPS, SG_CHUNK, SG_CHUNK)), full((SG_CHUNK, 128))],
        out_shape=[SDS((bsz, t, 2 * SG_WIDTH), bf16), SDS((1, SG_WIDTH), f32), SDS((SG_GROUPS, SG_CHUNK, SG_CHUNK), f32),
                   SDS((SG_CHUNK, 128), f32)],
        compiler_params=_cp(("arbitrary", "arbitrary")),
    )(sguv, gain, w, bt, do)


def _pad_lanes(v, n=128):
    return jnp.pad(v.reshape(1, -1), ((0, 0), (0, n - v.size)))


def pad_w_in(w):
    return jnp.concatenate([w[:, 0:2048], jnp.pad(w[:, 2048:2056], ((0, 0), (0, 120))), w[:, 2056:]], axis=1)


def unpad_w_in(w):
    return jnp.concatenate([w[:, 0:2048], w[:, C_AB:C_AB + 8], w[:, C_SB:]], axis=1)


def layer_params(p, l):
    return dict(
        g1=p["norm1_g"][l].reshape(1, -1), g2=p["norm2_g"][l].reshape(1, -1),
        conv=jnp.pad(p["conv_w"][l], ((0, 4), (0, 0))), alog=_pad_lanes(p["a_log"][l]), dtb=_pad_lanes(p["dt_bias"][l]),
        dng=p["dn_out_g"][l].reshape(1, -1), gq=jnp.tile(p["sb_q_g"][l].reshape(1, -1), (1, 2)),
        gk=jnp.tile(p["sb_k_g"][l].reshape(1, -1), (1, 2)), sgg=p["sg_v_g"][l].reshape(1, -1), sgw=p["sg_w"][l],
        sgb=jnp.pad(p["sg_b"][l].T, ((0, 0), (0, 124))))


def local_step(x, tgt, small, get_w, put_g):
    bsz, t, _ = x.shape
    m = bsz * t
    r3 = lambda a: a.reshape(bsz, t, a.shape[-1])
    r2 = lambda a: a.reshape(m, a.shape[-1])
    xs, saved, ws = x.reshape(m, D_MODEL), [], []
    for l in range(DEPTH):
        sp, w = layer_params(small, l), {}
        w["w_in"] = get_w(l, "in", xs)
        qkv, z, ab, sb, sg = inproj_fwd(xs, sp["g1"], w["w_in"])
        odn, sall, tall = dn_fwd(r3(qkv), r3(z), r3(ab), sp["conv"], sp["alog"], sp["dtb"], sp["dng"])
        osb, ltot = sb_fwd(r3(sb), sp["gq"], sp["gk"])
        osg = sg_fwd(r3(sg), sp["sgg"], sp["sgw"], sp["sgb"])
        w["w_out"] = get_w(l, "out", osg)
        x2, mix = outproj_fwd(xs, r2(odn), r2(osb), r2(osg), w["w_out"])
        w["w_ff1"], w["w_ff2"] = get_w(l, "ff", x2)
        x3 = ffn_fwd(x2, sp["g2"], w["w_ff1"], w["w_ff2"])
        saved.append(dict(x=xs, qkv=qkv, z=z, ab=ab, sb=sb, sg=sg, sall=sall, tall=tall, ltot=ltot, mix=mix, x2=x2))
        ws.append(w)
        xs = x3
    dx, lossp = loss_head(xs, tgt.reshape(m, D_MODEL))
    gsmall = [None] * DEPTH
    token = jnp.zeros((), f32)
    for l in reversed(range(DEPTH)):
        sp, w, s = layer_params(small, l), ws[l], saved[l]
        dx2, dg2, h2, act, df, dyb = ffn_bwd(s["x2"], sp["g2"] + token, w["w_ff1"], w["w_ff2"], dx)
        g_ff1 = tn_matmul(h2, df, f"dw_ff1_{l}", col_shards=N_CHIPS)
        g_ff2 = tn_matmul(act, dyb, f"dw_ff2_{l}")
        dodn, dosb, dosg, dx2b = outproj_bwd(dx2, w["w_out"])
        g_out = tn_matmul(s["mix"], dx2b, f"dw_out_{l}")
        token = put_g(l, "rest", dict(w_out=g_out, w_ff1=g_ff1, w_ff2=g_ff2))
        dqkv, dz, dab, dconv, dalog, ddtb, ddng = dn_bwd(r3(s["qkv"]), r3(s["z"]), r3(s["ab"]), sp["conv"], sp["alog"], sp["dtb"],
                                                        sp["dng"] + token, s["sall"], s["tall"], r3(dodn))
        dsq, dsk, dsv, dgq, dgk = sb_bwd(r3(s["sb"]), sp["gq"], sp["gk"], s["ltot"], r3(dosb))
        dsg, dsgg, dsgw, dsgb = sg_bwd(r3(s["sg"]), sp["sgg"], sp["sgw"], sp["sgb"], r3(dosg))
        dproj = jnp.concatenate([r2(dqkv), r2(dz), r2(dab), r2(dsq), r2(dsk), r2(dsv), r2(dsg)], axis=1)
        dx, dg1, h1 = inproj_bwd(s["x"], sp["g1"], w["w_in"], dproj, dx2)
        g_in = tn_matmul(h1, dproj, f"dw_in_{l}")
        token = put_g(l, "in", dict(w_in=g_in))
        fold = lambda a: (a[:, 0, :].sum(0).reshape(2, SB_DIM)).sum(0)
        gsmall[l] = dict(norm1_g=dg1[0], conv_w=dconv[0:DN_CONV], a_log=dalog[0, 0:DN_HEADS], dt_bias=ddtb[0, 0:DN_HEADS],
                         dn_out_g=ddng[0], sb_q_g=fold(dgq), sb_k_g=fold(dgk), sg_v_g=dsgg[0], sg_w=dsgw,
                         sg_b=dsgb[:, 0:SG_GROUPS].T, norm2_g=dg2[0])
    return lossp, dx.reshape(bsz, t, D_MODEL), gsmall


def _chip_peers(x, y):
    return [(1 - x, y), (x, 1 - y), (1 - x, 1 - y)]


_HBM = pl.BlockSpec(memory_space=pltpu.HBM)
_SEM = pl.BlockSpec(memory_space=pltpu.SEMAPHORE)
_EFFECT = pltpu.SideEffectType.DATAFLOW_SIDE_EFFECTING


def _hbm(a):
    return pltpu.with_memory_space_constraint(a, pltpu.HBM)


def _my_half(ref):
    half = ref.shape[0] // 2
    return ref.at[pl.ds(pl.multiple_of(lax.axis_index("c") * half, 8), half)]


def _exchange_copy(src, land, k, j, send, recv, scatter, halve, waiting):
    x, y, c = lax.axis_index("x"), lax.axis_index("y"), lax.axis_index("c")
    px, py = _chip_peers(x, y)[j]
    me, peer = 2 * x + y, 2 * px + py
    if scatter:
        src = src.at[me if waiting else peer]
    dst = land.at[peer if waiting else me]
    if halve:
        src, dst = _my_half(src), _my_half(dst)
    return pltpu.make_async_remote_copy(src_ref=src, dst_ref=dst, send_sem=send.at[3 * k + j],
                                        recv_sem=recv.at[3 * k + j], device_id=(px, py, c), device_id_type=MESH)


def exchange_start(items, name, scatter):
    arrs = []
    for a, _, _ in items:
        if not any(a is b for b in arrs):
            arrs.append(a)
    pos = [next(i for i, b in enumerate(arrs) if b is a) for a, _, _ in items]
    shapes = [a.shape if idx is None else a.shape[1:] for a, idx, _ in items]
    lands = [lax.empty(s if scatter else (N_CHIPS,) + s, a.dtype) for (a, _, _), s in zip(items, shapes)]
    na, nl = len(arrs), len(lands)

    def body(*refs):
        ins, lnd = refs[:na], refs[na:na + nl]
        send, recv = refs[na + nl], refs[na + nl + 1]
        token = refs[-1]
        for k, (_, idx, halve) in enumerate(items):
            src = ins[pos[k]] if idx is None else ins[pos[k]].at[idx]
            for j in range(3):
                _exchange_copy(src, lnd[k], k, j, send, recv, scatter, halve, False).start()
        token[...] = jnp.zeros_like(token)

    sems = pltpu.SemaphoreType.DMA((3 * nl,))
    out = pl.pallas_call(
        body, name=name,
        out_shape=(sems, sems, *[pltpu.HBM(a.shape, a.dtype) for a in arrs + lands], SDS((8, 128), f32)),
        in_specs=[_HBM] * (na + nl), out_specs=(_SEM, _SEM, *[_HBM] * (na + nl), pl.BlockSpec(memory_space=pltpu.VMEM)),
        input_output_aliases={i: 2 + i for i in range(na + nl)},
        compiler_params=pltpu.CompilerParams(has_side_effects=_EFFECT),
    )(*[_hbm(a) for a in arrs + lands])
    thru = out[2:2 + na]
    return dict(send=out[0], recv=out[1], src=[(thru[pos[k]], idx) for k, (_, idx, _) in enumerate(items)],
                halve=[h for _, _, h in items], land=list(out[2 + na:2 + na + nl]), token=out[-1], scatter=scatter)


def exchange_wait(st, ks, after, name):
    arrs = []
    for k in ks:
        if not any(st["src"][k][0] is b for b in arrs):
            arrs.append(st["src"][k][0])
    pos = [next(i for i, b in enumerate(arrs) if b is st["src"][k][0]) for k in ks]
    lands = [st["land"][k] for k in ks]
    na, nl = len(arrs), len(lands)

    def body(*refs):
        ins, lnd = refs[:na], refs[na:na + nl]
        send, recv = refs[na + nl], refs[na + nl + 1]
        for t, k in enumerate(ks):
            idx = st["src"][k][1]
            src = ins[pos[t]] if idx is None else ins[pos[t]].at[idx]
            for j in range(3):
                cp = _exchange_copy(src, lnd[t], k, j, send, recv, st["scatter"], st["halve"][k], True)
                cp.wait_send()
                cp.wait_recv()

    out = pl.pallas_call(
        body, name=name, out_shape=tuple(pltpu.HBM(a.shape, a.dtype) for a in arrs + lands),
        in_specs=[_HBM] * (na + nl) + [_SEM, _SEM, pl.BlockSpec(memory_space=pl.ANY)], out_specs=tuple([_HBM] * (na + nl)),
        input_output_aliases={i: i for i in range(na + nl)},
        compiler_params=pltpu.CompilerParams(has_side_effects=_EFFECT),
    )(*arrs, *lands, st["send"], st["recv"], after)
    for k, (a, idx) in enumerate(st["src"]):
        for p, b in enumerate(arrs):
            if a is b:
                st["src"][k] = (out[p], idx)
    return list(out[na:na + nl])


def swap_cores(arrs, name):
    n = len(arrs)

    def body(*refs):
        ins, outs = refs[:n], refs[n:2 * n]
        send, recv = refs[2 * n:]
        sib = (lax.axis_index("x"), lax.axis_index("y"), 1 - lax.axis_index("c"))
        cps = [pltpu.make_async_remote_copy(src_ref=ins[i], dst_ref=outs[i], send_sem=send.at[i], recv_sem=recv.at[i],
                                            device_id=sib, device_id_type=MESH) for i in range(n)]
        for cp in cps:
            cp.start()
        for cp in cps:
            cp.wait()

    any_spec = pl.BlockSpec(memory_space=pl.ANY)
    return pl.pallas_call(
        body, name=name, in_specs=[any_spec] * n, out_specs=[any_spec] * n, out_shape=[SDS(a.shape, a.dtype) for a in arrs],
        scratch_shapes=[pltpu.SemaphoreType.DMA((n,)), pltpu.SemaphoreType.DMA((n,))],
    )(*arrs)


def swap_halves(zones, name):
    n = len(zones)

    def body(*refs):
        outs = refs[n:2 * n]
        send, recv = refs[2 * n:]
        x, y, c = lax.axis_index("x"), lax.axis_index("y"), lax.axis_index("c")
        cps = []
        for i in range(n):
            for j, (px, py) in enumerate(_chip_peers(x, y)):
                part = _my_half(outs[i].at[2 * px + py])
                cps.append(pltpu.make_async_remote_copy(src_ref=part, dst_ref=part, send_sem=send.at[3 * i + j],
                                                        recv_sem=recv.at[3 * i + j], device_id=(x, y, 1 - c), device_id_type=MESH))
        for cp in cps:
            cp.start()
        for i in range(n):
            for j in range(3):
                cps[3 * i + j].wait_send()
                cps[3 * i + j].wait_recv()

    any_spec = pl.BlockSpec(memory_space=pl.ANY)
    return pl.pallas_call(
        body, name=name, in_specs=[any_spec] * n, out_specs=[any_spec] * n, out_shape=[SDS(a.shape, a.dtype) for a in zones],
        input_output_aliases={i: i for i in range(n)},
        scratch_shapes=[pltpu.SemaphoreType.DMA((3 * n,)), pltpu.SemaphoreType.DMA((3 * n,))],
    )(*zones)


def pair_sum(a, b, name, tr=512):
    rows, cols = a.shape
    tr = min(tr, rows)
    assert rows % tr == 0

    def body(a_ref, b_ref, o_ref):
        o_ref[...] = (a_ref[...].astype(f32) + b_ref[...].astype(f32)).astype(bf16)

    spec = pl.BlockSpec((tr, cols), lambda i: (i, 0))
    return pl.pallas_call(body, name=name, grid=(rows // tr,), in_specs=[spec, spec], out_specs=spec,
                          out_shape=SDS((rows, cols), bf16), compiler_params=_cp(("arbitrary",)))(a, b)


def allreduce_small(v):
    def body(v_ref, o_ref, rbuf, send, recv):
        x, y, c = lax.axis_index("x"), lax.axis_index("y"), lax.axis_index("c")
        o_ref[...] = v_ref[...]
        for s, peer in enumerate([(x, y, 1 - c), (1 - x, y, c), (x, 1 - y, c)]):
            cp = pltpu.make_async_remote_copy(src_ref=o_ref, dst_ref=rbuf.at[s], send_sem=send.at[s], recv_sem=recv.at[s],
                                              device_id=peer, device_id_type=MESH)
            cp.start()
            cp.wait()
            o_ref[...] = o_ref[...] + rbuf[s]

    vm = pl.BlockSpec(memory_space=pltpu.VMEM)
    return pl.pallas_call(
        body, name="allreduce_small", in_specs=[vm], out_specs=vm, out_shape=SDS(v.shape, f32),
        scratch_shapes=[pltpu.VMEM((3,) + v.shape, f32), pltpu.SemaphoreType.DMA((3,)), pltpu.SemaphoreType.DMA((3,))],
        compiler_params=_cp(),
    )(v)


def sum_partials(p, name, tr=256):
    _, rows, cols = p.shape
    tr = min(tr, rows)

    def body(p_ref, o_ref):
        o_ref[...] = ((p_ref[0].astype(f32) + p_ref[1].astype(f32)) + p_ref[2].astype(f32)) + p_ref[3].astype(f32)

    return pl.pallas_call(
        body, name=name, grid=(rows // tr,), in_specs=[pl.BlockSpec((N_CHIPS, tr, cols), lambda i: (0, i, 0))],
        out_specs=pl.BlockSpec((tr, cols), lambda i: (i, 0)), out_shape=SDS((rows, cols), f32),
        compiler_params=_cp(("arbitrary",)),
    )(p)


def adamw(w, m, v, gs, name, layer=0, prev=None, tr=256):
    hrows, cols = gs[0].shape
    rows = hrows * len(gs)
    tr = min(tr, hrows)
    assert hrows % tr == 0 and w.shape[0] % rows == 0
    off, nth = layer * (rows // tr), hrows // tr

    def body(w_ref, m_ref, v_ref, *rest):
        g_ref, d_ref, mo_ref, vo_ref = rest[-4:]
        if len(gs) == 1:
            g = rest[0][...]
        else:
            g = jnp.where(pl.program_id(0) // nth == lax.axis_index("c"), rest[0][...], rest[1][...])
        mn = ADAM_B1 * m_ref[...] + (1.0 - ADAM_B1) * g
        vn = ADAM_B2 * v_ref[...] + (1.0 - ADAM_B2) * jnp.square(g)
        m_hat = mn / (1.0 - ADAM_B1 ** ADAM_STEP)
        v_hat = vn / (1.0 - ADAM_B2 ** ADAM_STEP)
        g_ref[...] = g
        d_ref[...] = -ADAM_LR * (m_hat / (jnp.sqrt(v_hat) + ADAM_EPS) + ADAM_WD * w_ref[...])
        mo_ref[...] = mn
        vo_ref[...] = vn

    loc = pl.BlockSpec((tr, cols), lambda i: (i % nth, 0))
    glob = pl.BlockSpec((tr, cols), lambda i: (off + i, 0))
    extra = [] if prev is None else list(prev)
    return pl.pallas_call(
        body, name=name, grid=(rows // tr,),
        in_specs=[glob] * 3 + [loc] * len(gs) + [pl.BlockSpec(memory_space=pl.ANY)] * len(extra),
        out_specs=[glob] * 4, out_shape=[SDS(w.shape, f32)] * 4,
        input_output_aliases={3 + len(gs) + j: j for j in range(len(extra))},
        compiler_params=_cp(("arbitrary",)),
    )(w, m, v, *gs, *extra)


BIG = ("w_in", "w_out", "w_ff1", "w_ff2")
SMALL = ("norm1_g", "conv_w", "a_log", "dt_bias", "dn_out_g", "sb_q_g", "sb_k_g", "sg_v_g", "sg_w", "sg_b", "norm2_g")
WEIGHTS = ("norm1_g", "w_in", "conv_w", "a_log", "dt_bias", "dn_out_g", "sb_q_g", "sb_k_g", "sg_v_g", "sg_w", "sg_b",
           "w_out", "norm2_g", "w_ff1", "w_ff2")


PACK_ROWS = 256


def _pack(arrs):
    flat = jnp.concatenate([a.reshape(-1) for a in arrs])
    n = flat.shape[0]
    rows = -(-n // (PACK_ROWS * 128)) * PACK_ROWS
    return jnp.pad(flat, (0, rows * 128 - n)).reshape(rows, 128)


def _unpack(packed, shapes):
    flat, out, o = packed.reshape(-1), [], 0
    for s in shapes:
        n = 1
        for d in s:
            n *= d
        out.append(flat[o:o + n].reshape(s))
        o += n
    return out


def kernel(x, norm1_g, w_in, conv_w, a_log, dt_bias, dn_out_g, sb_q_g, sb_k_g, sg_v_g, sg_w, sg_b, w_out, norm2_g, w_ff1, w_ff2, loss_target, m_norm1_g, m_w_in, m_conv_w, m_a_log, m_dt_bias, m_dn_out_g, m_sb_q_g, m_sb_k_g, m_sg_v_g, m_sg_w, m_sg_b, m_w_out, m_norm2_g, m_w_ff1, m_w_ff2, v_norm1_g, v_w_in, v_conv_w, v_a_log, v_dt_bias, v_dn_out_g, v_sb_q_g, v_sb_k_g, v_sg_v_g, v_sg_w, v_sg_b, v_w_out, v_norm2_g, v_w_ff1, v_w_ff2):
    w = dict(norm1_g=norm1_g, w_in=w_in, conv_w=conv_w, a_log=a_log, dt_bias=dt_bias, dn_out_g=dn_out_g, sb_q_g=sb_q_g,
             sb_k_g=sb_k_g, sg_v_g=sg_v_g, sg_w=sg_w, sg_b=sg_b, w_out=w_out, norm2_g=norm2_g, w_ff1=w_ff1, w_ff2=w_ff2)
    mom = dict(norm1_g=m_norm1_g, w_in=m_w_in, conv_w=m_conv_w, a_log=m_a_log, dt_bias=m_dt_bias, dn_out_g=m_dn_out_g,
               sb_q_g=m_sb_q_g, sb_k_g=m_sb_k_g, sg_v_g=m_sg_v_g, sg_w=m_sg_w, sg_b=m_sg_b, w_out=m_w_out, norm2_g=m_norm2_g,
               w_ff1=m_w_ff1, w_ff2=m_w_ff2)
    var = dict(norm1_g=v_norm1_g, w_in=v_w_in, conv_w=v_conv_w, a_log=v_a_log, dt_bias=v_dt_bias, dn_out_g=v_dn_out_g,
               sb_q_g=v_sb_q_g, sb_k_g=v_sb_k_g, sg_v_g=v_sg_v_g, sg_w=v_sg_w, sg_b=v_sg_b, w_out=v_w_out, norm2_g=v_norm2_g,
               w_ff1=v_w_ff1, w_ff2=v_w_ff2)
    chip = 2 * lax.axis_index("x") + lax.axis_index("y")

    wb = {k: w[k].astype(bf16) for k in BIG}
    ag = exchange_start([(conv_w, None, False)] + [(wb[k], l, True) for l in range(DEPTH) for k in BIG], "allgather_start",
                        scatter=False)
    item = lambda l, k: 1 + l * len(BIG) + BIG.index(k)

    def landed(ks, after, name):
        zones = exchange_wait(ag, ks, after, name)
        halved = [t for t, k in enumerate(ks) if ag["halve"][k]]
        for t, z in zip(halved, swap_halves([zones[t] for t in halved], name.replace("wait", "pass"))):
            zones[t] = z
        own = [ag["src"][k][0] if ag["src"][k][1] is None else ag["src"][k][0][ag["src"][k][1]] for k in ks]
        return [lax.dynamic_update_slice_in_dim(z, o[None], chip, axis=0) for z, o in zip(zones, own)]

    def whole(k, z):
        if k == "w_in":
            return pad_w_in(jnp.transpose(z, (1, 0, 2)).reshape(D_MODEL, IN_DIM))
        if k == "w_ff1":
            return jnp.transpose(z, (1, 0, 2)).reshape(D_MODEL, D_FF)
        return z.reshape(-1, D_MODEL)

    g_conv, first_in = landed([0, item(0, "w_in")], x, "allgather_wait_in0")
    small = {k: w[k] for k in SMALL}
    small["conv_w"] = jnp.transpose(g_conv, (1, 2, 0, 3)).reshape(DEPTH, DN_CONV, 3 * DN_WIDTH)
    cache = {}

    def get_w(l, part, after):
        if part == "in":
            return whole("w_in", first_in if l == 0 else landed([item(l, "w_in")], after, f"allgather_wait_in{l}")[0])
        if part == "out":
            zs = landed([item(l, k) for k in ("w_out", "w_ff1", "w_ff2")], after, f"allgather_wait_rest{l}")
            cache[l] = (whole("w_ff1", zs[1]), whole("w_ff2", zs[2]))
            return whole("w_out", zs[0])
        return cache[l]

    rs = {}
    core = lax.axis_index("c")

    def put_g(l, tag, g):
        by_dest = dict(
            w_in=lambda a: jnp.transpose(unpad_w_in(a).reshape(D_MODEL, N_CHIPS, IN_DIM // N_CHIPS), (1, 0, 2)),
            w_out=lambda a: a.reshape(N_CHIPS, D_MODEL // N_CHIPS, D_MODEL), w_ff1=lambda a: a,
            w_ff2=lambda a: a.reshape(N_CHIPS, D_FF // N_CHIPS, D_MODEL))
        names = [k for k in BIG if k in g]
        halves = [by_dest[k](g[k]).reshape(N_CHIPS, 2, -1, g[k].shape[-1] if k != "w_in" else IN_DIM // N_CHIPS) for k in names]
        keep = [lax.dynamic_index_in_dim(h, core, 1, keepdims=False) for h in halves]
        give = [lax.dynamic_index_in_dim(h, 1 - core, 1, keepdims=False) for h in halves]
        got = swap_cores(give, f"pair_swap_{tag}{l}")
        pair = [pair_sum(a.reshape(-1, a.shape[-1]), b.reshape(-1, b.shape[-1]), f"pair_sum_{k}_{l}").reshape(a.shape)
                for k, a, b in zip(names, keep, got)]
        rs[l, tag] = dict(exchange_start([(a, None, False) for a in pair], f"scatter_start_{tag}{l}", scatter=True), names=names)
        return rs[l, tag]["token"][0, 0]

    lossp, grad_x, gsmall = local_step(x, loss_target, small, get_w, put_g)
    loss = lax.psum(jnp.sum(lossp), ("x", "y", "c"))

    def finish(l, tag, after, prev):
        st = rs[l, tag]
        ks = list(range(len(st["names"])))
        zones = exchange_wait(st, ks, after, f"scatter_wait_{tag}{l}")
        own = [lax.dynamic_index_in_dim(st["src"][k][0], chip, 0, keepdims=True) for k in ks]
        got = [lax.dynamic_update_slice_in_dim(z, o, chip, axis=0) for z, o in zip(zones, own)]
        sums = [sum_partials(got[i], f"sum_{k}_{l}") for i, k in enumerate(st["names"])]
        others = swap_cores(sums, f"swap_grad_sums_{tag}{l}")
        outs = dict(prev)
        for i, k in enumerate(st["names"]):
            r2 = lambda a: a.reshape(-1, a.shape[-1])
            outs[k] = adamw(r2(w[k]), r2(mom[k]), r2(var[k]), (sums[i], others[i]), f"adamw_{k}_{l}", layer=l, prev=prev.get(k))
        return outs

    done = finish(1, "rest", rs[0, "in"]["token"], {})
    done = finish(1, "in", done["w_ff2"][0], done)
    res = {}

    full_shapes = [(DEPTH,) + tuple(gsmall[0][k].shape) for k in SMALL]
    packed = _pack([jnp.stack([gsmall[l][k] for l in range(DEPTH)]) for k in SMALL])
    total = allreduce_small(packed)
    gfull = dict(zip(SMALL, _unpack(total, full_shapes)))
    cs = 3 * DN_WIDTH // N_CHIPS
    gfull["conv_w"] = lax.dynamic_slice_in_dim(gfull["conv_w"], chip * cs, cs, axis=2)
    gp, wp, mp, vp = (_pack([d[k] for k in SMALL]) for d in (gfull, w, mom, var))
    outs = adamw(wp, mp, vp, (gp,), "adamw_small")
    loc_shapes = [w[k].shape for k in SMALL]
    unp = [_unpack(o, loc_shapes) for o in outs]
    for i, k in enumerate(SMALL):
        res[k] = [unp[j][i] for j in range(4)]

    done = finish(0, "rest", outs[0], done)
    done = finish(0, "in", done["w_ff2"][0], done)
    for k in BIG:
        res[k] = [o.reshape(w[k].shape) for o in done[k]]

    return (loss, grad_x, *[res[k][0] for k in WEIGHTS], *[res[k][1] for k in WEIGHTS], *[res[k][2] for k in WEIGHTS],
            *[res[k][3] for k in WEIGHTS])
```

```python
import functools

import jax
import jax.numpy as jnp
from jax import lax
from jax.experimental import pallas as pl
from jax.experimental.pallas import tpu as pltpu

f32 = jnp.float32
bf16 = jnp.bfloat16
SDS = jax.ShapeDtypeStruct
MESH = pl.DeviceIdType.MESH

NORM_EPS = 1e-6
D_MODEL = 1024
DEPTH = 2
DN_HEADS, DN_DIM, DN_WIDTH, DN_CONV, DN_CHUNK = 4, 128, 512, 4, 64
SB_HEADS, SB_DIM, SB_WIDTH, SB_BLOCK = 4, 64, 256, 128
SG_GROUPS, SG_DIM, SG_WIDTH, SG_CHUNK = 4, 64, 256, 128
D_FF = 4096
IN_DIM = 3336
C_QKV, C_Z, C_AB, C_SB, C_SG, IN_PAD = 0, 1536, 2048, 2176, 2944, 3456
N_CHIPS = 4

ADAM_LR, ADAM_B1, ADAM_B2, ADAM_EPS, ADAM_WD, ADAM_STEP = 0.001, 0.9, 0.999, 1e-08, 0.01, 10

VMEM_LIMIT = 56 * 1024 * 1024


def _cp(sem=None, **kw):
    if sem is not None:
        kw["dimension_semantics"] = sem
    return pltpu.CompilerParams(vmem_limit_bytes=VMEM_LIMIT, **kw)


def _split2(x):
    hi = x.astype(bf16)
    lo = (x - hi.astype(f32)).astype(bf16)
    return hi, lo


NT = (((1,), (1,)), ((), ()))
TN = (((0,), (0,)), ((), ()))
_DIMS2 = dict(nn=(((1,), (0,)), ((), ())), nt=NT, tn=TN)
_DIMS3 = dict(nn=(((2,), (1,)), ((0,), (0,))), nt=(((2,), (2,)), ((0,), (0,))), tn=(((1,), (1,)), ((0,), (0,))))


def _dg(a, b, kind):
    return lax.dot_general(a, b, (_DIMS2 if a.ndim == 2 else _DIMS3)[kind], preferred_element_type=f32)


def _pdot(a, b):
    return _dg(a, b, "nn")


def _dot_hp(a, b):
    ah, al = _split2(a)
    bh, bl = _split2(b)
    return _pdot(ah, bh) + _pdot(ah, bl) + _pdot(al, bh)


def _dot_x2c(a, m):
    ah, al = _split2(a)
    return _pdot(ah, m) + _pdot(al, m)


def _dot_cx2(m, a):
    if a.ndim == 3:
        m = jnp.broadcast_to(m, (a.shape[0],) + m.shape)
    ah, al = _split2(a)
    return _pdot(m, ah) + _pdot(m, al)


def _nt(a, b):
    return _dg(a.astype(bf16), b.astype(bf16), "nt")


def _tn(a, b):
    return _dg(a.astype(bf16), b.astype(bf16), "tn")


def _nn(a, b):
    return _dg(a.astype(bf16), b.astype(bf16), "nn")


@jax.custom_vjp
def mm(a, b):
    return _nn(a, b)


mm.defvjp(lambda a, b: (_nn(a, b), (a, b)), lambda r, g: (_nt(g, r[1]), _tn(r[0], g)))


@jax.custom_vjp
def mm_nt(a, b):
    return _nt(a, b)


mm_nt.defvjp(lambda a, b: (_nt(a, b), (a, b)), lambda r, g: (_nn(g, r[1]), _tn(g, r[0])))


@jax.custom_vjp
def mm_tn(a, b):
    return _tn(a, b)


mm_tn.defvjp(lambda a, b: (_tn(a, b), (a, b)), lambda r, g: (_nt(r[1], g), _nn(r[0], g)))


@jax.custom_vjp
def rmul_const(a, m, mt):
    return _dot_x2c(a, m)


rmul_const.defvjp(lambda a, m, mt: (_dot_x2c(a, m), (m, mt)),
                  lambda r, g: (_dot_x2c(g, r[1]), jnp.zeros_like(r[0]), jnp.zeros_like(r[1])))


@jax.custom_vjp
def lmul_const(m, mt, a):
    return _dot_cx2(m, a)


lmul_const.defvjp(lambda m, mt, a: (_dot_cx2(m, a), (m, mt)),
                  lambda r, g: (jnp.zeros_like(r[0]), jnp.zeros_like(r[1]), _dot_cx2(r[1], g)))


@jax.custom_vjp
def mm_hl(t, x):
    th, tl = _split2(t)
    xb = x.astype(bf16)
    return _pdot(th, xb) + _pdot(tl, xb)


def _mm_hl_bwd(r, g):
    t, x = r
    th, tl = _split2(t)
    gb = g.astype(bf16)
    return _nt(g, x), _dg(th, gb, "tn") + _dg(tl, gb, "tn")


mm_hl.defvjp(lambda t, x: (mm_hl(t, x), (t, x)), _mm_hl_bwd)


def inv_unit_lower(lm):
    c = lm.shape[-1]
    r, cc = _iota2((c, c))
    eye = (r == cc).astype(f32)
    t = eye - lm
    p = -lm
    k = 1
    while 2 * k < c:
        p = _nn(p, p)
        t = t + _nn(t, p)
        k *= 2
    res = eye - t - _dot_hp(lm, t)
    return t + _nn(t, res)


@jax.custom_vjp
def inv_given(lm, t):
    return t


inv_given.defvjp(lambda lm, t: (t, t), lambda t, g: (-_nt(_tn(t, g), t), jnp.zeros_like(t)))


def _sigmoid(x):
    return 1.0 / (1.0 + jnp.exp(-x))


def _softplus(x):
    return jnp.maximum(x, 0.0) + jnp.log(1.0 + jnp.exp(-jnp.abs(x)))


def _silu(x):
    return x * _sigmoid(x)


def _gelu(x):
    return 0.5 * x * (1.0 + jnp.tanh(0.7978845608028654 * (x + 0.044715 * (x * x * x))))


def _iota2(shape):
    return lax.broadcasted_iota(jnp.int32, shape, 0), lax.broadcasted_iota(jnp.int32, shape, 1)


def _group_avg_mats():
    r, c = _iota2((128, 128))
    return jnp.where((r // 64) == (c // 64), 1.0 / 64.0, 0.0).astype(bf16)


def _pair_norm(x, gain, bavg):
    ms = rmul_const(x * x, bavg, bavg)
    return x * lax.rsqrt(ms + NORM_EPS) * gain


def _rms(x):
    r = lax.rsqrt(jnp.mean(x * x, axis=-1, keepdims=True) + NORM_EPS)
    return r


_IN_GROUPS = ((C_QKV, C_Z), (C_Z, C_AB), (C_AB, C_SB), (C_SB, C_SG), (C_SG, IN_PAD))


def inproj_fwd(x, g, wp, tm=256):
    m = x.shape[0]

    def body(x_ref, g_ref, w_ref, *outs):
        xv = x_ref[...]
        h = (xv * _rms(xv) * g_ref[...]).astype(bf16)
        for (a, b), o in zip(_IN_GROUPS, outs):
            o[...] = _pdot(h, w_ref[:, a:b])

    return pl.pallas_call(
        body, name="inproj_fwd", grid=(m // tm,),
        in_specs=[pl.BlockSpec((tm, D_MODEL), lambda i: (i, 0)), pl.BlockSpec((1, D_MODEL), lambda i: (0, 0)),
                  pl.BlockSpec((D_MODEL, IN_PAD), lambda i: (0, 0))],
        out_specs=[pl.BlockSpec((tm, b - a), lambda i: (i, 0)) for a, b in _IN_GROUPS],
        out_shape=[SDS((m, b - a), f32) for a, b in _IN_GROUPS],
        compiler_params=_cp(("arbitrary",)),
    )(x, g, wp)


def inproj_bwd(x, g, wp, dproj, dres, tm=256):
    m = x.shape[0]

    def body(x_ref, g_ref, w_ref, dp_ref, dr_ref, dx_ref, dg_ref, h_ref):
        xv = x_ref[...]
        r = _rms(xv)
        xn = xv * r
        gv = g_ref[...]
        h_ref[...] = (xn * gv).astype(bf16)
        dh = lax.dot_general(dp_ref[...], w_ref[...], NT, preferred_element_type=f32)
        dxn = dh * gv
        dx_ref[...] = dr_ref[...] + r * (dxn - xn * jnp.mean(dxn * xn, axis=-1, keepdims=True))

        @pl.when(pl.program_id(0) == 0)
        def _():
            dg_ref[...] = jnp.zeros_like(dg_ref)

        dg_ref[...] += jnp.sum(dh * xn, axis=0, keepdims=True)

    return pl.pallas_call(
        body, name="inproj_bwd", grid=(m // tm,),
        in_specs=[pl.BlockSpec((tm, D_MODEL), lambda i: (i, 0)), pl.BlockSpec((1, D_MODEL), lambda i: (0, 0)),
                  pl.BlockSpec((D_MODEL, IN_PAD), lambda i: (0, 0)), pl.BlockSpec((tm, IN_PAD), lambda i: (i, 0)),
                  pl.BlockSpec((tm, D_MODEL), lambda i: (i, 0))],
        out_specs=[pl.BlockSpec((tm, D_MODEL), lambda i: (i, 0)), pl.BlockSpec((1, D_MODEL), lambda i: (0, 0)),
                   pl.BlockSpec((tm, D_MODEL), lambda i: (i, 0))],
        out_shape=[SDS((m, D_MODEL), f32), SDS((1, D_MODEL), f32), SDS((m, D_MODEL), bf16)],
        compiler_params=_cp(("arbitrary",)),
    )(x, g, wp, dproj, dres)


def outproj_fwd(x, odn, osb, osg, wo, tm=512):
    m = x.shape[0]

    def body(x_ref, a_ref, b_ref, c_ref, w_ref, x2_ref, mix_ref):
        mix_ref[:, 0:DN_WIDTH] = a_ref[...].astype(bf16)
        mix_ref[:, DN_WIDTH:DN_WIDTH + SB_WIDTH] = b_ref[...].astype(bf16)
        mix_ref[:, DN_WIDTH + SB_WIDTH:D_MODEL] = c_ref[...].astype(bf16)
        x2_ref[...] = x_ref[...] + _pdot(mix_ref[...], w_ref[...])

    row = lambda w: pl.BlockSpec((tm, w), lambda i: (i, 0))
    return pl.pallas_call(
        body, name="outproj_fwd", grid=(m // tm,),
        in_specs=[row(D_MODEL), row(DN_WIDTH), row(SB_WIDTH), row(SG_WIDTH), pl.BlockSpec((D_MODEL, D_MODEL), lambda i: (0, 0))],
        out_specs=[row(D_MODEL), row(D_MODEL)],
        out_shape=[SDS((m, D_MODEL), f32), SDS((m, D_MODEL), bf16)],
        compiler_params=_cp(("arbitrary",)),
    )(x, odn, osb, osg, wo)


def outproj_bwd(dx2, wo, tm=512):
    m = dx2.shape[0]

    def body(d_ref, w_ref, a_ref, b_ref, c_ref, db_ref):
        db = d_ref[...].astype(bf16)
        db_ref[...] = db
        dm = lax.dot_general(db, w_ref[...], NT, preferred_element_type=f32)
        a_ref[...] = dm[:, 0:DN_WIDTH]
        b_ref[...] = dm[:, DN_WIDTH:DN_WIDTH + SB_WIDTH]
        c_ref[...] = dm[:, DN_WIDTH + SB_WIDTH:D_MODEL]

    row = lambda w: pl.BlockSpec((tm, w), lambda i: (i, 0))
    return pl.pallas_call(
        body, name="outproj_bwd", grid=(m // tm,),
        in_specs=[row(D_MODEL), pl.BlockSpec((D_MODEL, D_MODEL), lambda i: (0, 0))],
        out_specs=[row(DN_WIDTH), row(SB_WIDTH), row(SG_WIDTH), row(D_MODEL)],
        out_shape=[SDS((m, DN_WIDTH), f32), SDS((m, SB_WIDTH), f32), SDS((m, SG_WIDTH), f32), SDS((m, D_MODEL), bf16)],
        compiler_params=_cp(("arbitrary",)),
    )(dx2, wo)


FF_CHUNK = 1024


def _load_weights_once(pairs, sem):
    @pl.when(pl.program_id(0) == 0)
    def _():
        cps = [pltpu.make_async_copy(h, v, sem.at[i]) for i, (h, v) in enumerate(pairs)]
        for c in cps:
            c.start()
        for c in cps:
            c.wait()


def ffn_fwd(x2, g, w1, w2, tm=256):
    m = x2.shape[0]

    def body(x_ref, g_ref, w1_hbm, w2_hbm, y_ref, w1_v, w2_v, sem):
        _load_weights_once(((w1_hbm, w1_v), (w2_hbm, w2_v)), sem)
        xv = x_ref[...]
        h = (xv * _rms(xv) * g_ref[...]).astype(bf16)
        acc = xv
        for j in range(0, D_FF, FF_CHUNK):
            f = _pdot(h, w1_v[:, j:j + FF_CHUNK])
            rl = jnp.maximum(f, 0.0)
            acc = acc + _pdot((rl * rl).astype(bf16), w2_v[j:j + FF_CHUNK, :])
        y_ref[...] = acc

    return pl.pallas_call(
        body, name="ffn_fwd", grid=(m // tm,),
        in_specs=[pl.BlockSpec((tm, D_MODEL), lambda i: (i, 0)), pl.BlockSpec((1, D_MODEL), lambda i: (0, 0)),
                  pl.BlockSpec(memory_space=pl.ANY), pl.BlockSpec(memory_space=pl.ANY)],
        out_specs=pl.BlockSpec((tm, D_MODEL), lambda i: (i, 0)),
        out_shape=SDS((m, D_MODEL), f32),
        scratch_shapes=[pltpu.VMEM((D_MODEL, D_FF), bf16), pltpu.VMEM((D_FF, D_MODEL), bf16), pltpu.SemaphoreType.DMA((2,))],
        compiler_params=_cp(("arbitrary",)),
    )(x2, g, w1, w2)


def ffn_bwd(x2, g, w1, w2, dy, tm=256):
    m = x2.shape[0]

    def body(x_ref, g_ref, w1_hbm, w2_hbm, dy_ref, dx_ref, dg_ref, h_ref, a_ref, df_ref, dyb_ref, w1_v, w2_v, sem):
        _load_weights_once(((w1_hbm, w1_v), (w2_hbm, w2_v)), sem)
        xv = x_ref[...]
        r = _rms(xv)
        xn = xv * r
        gv = g_ref[...]
        h = (xn * gv).astype(bf16)
        h_ref[...] = h
        dyv = dy_ref[...]
        dyb = dyv.astype(bf16)
        dyb_ref[...] = dyb
        dh = jnp.zeros((tm, D_MODEL), f32)
        for j in range(0, D_FF, FF_CHUNK):
            f = _pdot(h, w1_v[:, j:j + FF_CHUNK])
            rl = jnp.maximum(f, 0.0)
            a_ref[:, j:j + FF_CHUNK] = (rl * rl).astype(bf16)
            da = lax.dot_general(dyb, w2_v[j:j + FF_CHUNK, :], NT, preferred_element_type=f32)
            df = (da * (2.0 * rl)).astype(bf16)
            df_ref[:, j:j + FF_CHUNK] = df
            dh = dh + lax.dot_general(df, w1_v[:, j:j + FF_CHUNK], NT, preferred_element_type=f32)
        dxn = dh * gv
        dx_ref[...] = dyv + r * (dxn - xn * jnp.mean(dxn * xn, axis=-1, keepdims=True))

        @pl.when(pl.program_id(0) == 0)
        def _():
            dg_ref[...] = jnp.zeros_like(dg_ref)

        dg_ref[...] += jnp.sum(dh * xn, axis=0, keepdims=True)

    row = lambda w: pl.BlockSpec((tm, w), lambda i: (i, 0))
    return pl.pallas_call(
        body, name="ffn_bwd", grid=(m // tm,),
        in_specs=[row(D_MODEL), pl.BlockSpec((1, D_MODEL), lambda i: (0, 0)),
                  pl.BlockSpec(memory_space=pl.ANY), pl.BlockSpec(memory_space=pl.ANY), row(D_MODEL)],
        out_specs=[row(D_MODEL), pl.BlockSpec((1, D_MODEL), lambda i: (0, 0)), row(D_MODEL), row(D_FF), row(D_FF), row(D_MODEL)],
        out_shape=[SDS((m, D_MODEL), f32), SDS((1, D_MODEL), f32), SDS((m, D_MODEL), bf16), SDS((m, D_FF), bf16),
                   SDS((m, D_FF), bf16), SDS((m, D_MODEL), bf16)],
        scratch_shapes=[pltpu.VMEM((D_MODEL, D_FF), bf16), pltpu.VMEM((D_FF, D_MODEL), bf16), pltpu.SemaphoreType.DMA((2,))],
        compiler_params=_cp(("arbitrary",)),
    )(x2, g, w1, w2, dy)


def _tile(n, cap):
    best = 128
    for t in range(128, cap + 1, 128):
        if n % t == 0:
            best = t
    return best


def tn_matmul(a, b, name, col_shards=1, tk=512):
    m, ka = a.shape
    n = b.shape[1]
    ti = _tile(ka, 1024)
    tj = _tile(n // col_shards, 1152)
    nk = m // tk
    jps = (n // col_shards) // tj

    def body(a_ref, b_ref, o_ref, acc):
        k = pl.program_id(2)

        @pl.when(k == 0)
        def _():
            acc[...] = jnp.zeros_like(acc)

        acc[...] += lax.dot_general(a_ref[...], b_ref[...], TN, preferred_element_type=f32)

        @pl.when(k == nk - 1)
        def _():
            o_ref[...] = acc[...].astype(bf16).reshape(o_ref.shape)

    if col_shards == 1:
        out_shape, out_spec = SDS((ka, n), bf16), pl.BlockSpec((ti, tj), lambda i, j, k: (i, j))
    else:
        out_shape = SDS((col_shards, ka, n // col_shards), bf16)
        out_spec = pl.BlockSpec((1, ti, tj), lambda i, j, k: (j // jps, i, j % jps))
    return pl.pallas_call(
        body, name=name, grid=(ka // ti, n // tj, nk),
        in_specs=[pl.BlockSpec((tk, ti), lambda i, j, k: (k, i)), pl.BlockSpec((tk, tj), lambda i, j, k: (k, j))],
        out_specs=out_spec, out_shape=out_shape,
        scratch_shapes=[pltpu.VMEM((ti, tj), f32)],
        compiler_params=_cp(("arbitrary", "arbitrary", "arbitrary")),
    )(a, b)


def loss_head(y, tgt, tm=512):
    m = y.shape[0]

    def body(y_ref, t_ref, dy_ref, l_ref):
        e = y_ref[...] - t_ref[...]
        dy_ref[...] = e * (1.0 / D_MODEL)

        @pl.when(pl.program_id(0) == 0)
        def _():
            l_ref[...] = jnp.zeros_like(l_ref)

        l_ref[...] += jnp.sum(e * e, axis=0, keepdims=True) * (0.5 / D_MODEL)

    row = pl.BlockSpec((tm, D_MODEL), lambda i: (i, 0))
    return pl.pallas_call(
        body, name="loss_head", grid=(m // tm,), in_specs=[row, row],
        out_specs=[row, pl.BlockSpec((1, D_MODEL), lambda i: (0, 0))],
        out_shape=[SDS((m, D_MODEL), f32), SDS((1, D_MODEL), f32)],
        compiler_params=_cp(("arbitrary",)),
    )(y, tgt)


def _dn_consts():
    c = DN_CHUNK
    r, cc = _iota2((c, c))
    lt = (cc <= r).astype(bf16)
    ltt = (r <= cc).astype(bf16)
    return lt, ltt


def dn_chunk(cq, ck, cv, a, b, z, s, alog, dtb, gain, lt, ltt, t_given=None):
    c = DN_CHUNK
    r, cc = _iota2((c, c))
    q = cq * lax.rsqrt(jnp.sum(cq * cq, axis=-1, keepdims=True) + NORM_EPS) * (DN_DIM ** -0.5)
    k = ck * lax.rsqrt(jnp.sum(ck * ck, axis=-1, keepdims=True) + NORM_EPS)
    g = -jnp.exp(alog) * _softplus(a + dtb)
    beta = _sigmoid(b)
    r2, c2 = _iota2((c, 128))
    uaug = jnp.where((c2 < c) & (r2 > c2), 1.0, 0.0) + jnp.where(c2 == c, 1.0, 0.0)
    gam_all = lmul_const(lt, ltt, g * uaug)
    gam_cc = gam_all[:, :, 0:c]
    gam = gam_all[:, :, c:c + 1]
    dec = jnp.where(cc <= r, jnp.exp(jnp.where(cc <= r, gam_cc, 0.0)), 0.0)
    kk = mm_nt(k, k)
    lm = jnp.where(cc < r, beta * kk * dec, 0.0)
    t = inv_unit_lower(lm) if t_given is None else inv_given(lm, t_given)
    eg = jnp.exp(gam)
    sol = mm_hl(t, jnp.concatenate([cv * beta, k * (beta * eg)], axis=2))
    u, w = sol[:, :, 0:DN_DIM], sol[:, :, DN_DIM:2 * DN_DIM]
    qk = jnp.where(cc <= r, mm_nt(q, k) * dec, 0.0)
    glast = jnp.sum(g, axis=1, keepdims=True)
    qd = q * eg
    kd = k * jnp.exp(glast - gam)
    un = u - mm(w, s)
    o = mm(qd, s) + mm(qk, un)
    s_new = s * jnp.exp(glast) + mm_tn(kd, un)
    on = o * lax.rsqrt(jnp.mean(o * o, axis=-1, keepdims=True) + NORM_EPS) * gain * _silu(z)
    return on, s_new, t


def _dn_chains(cacts, ab_ref, z_ref, al_ref, dt_ref):
    cq, ck, cv, a, b, z, al, dt = [], [], [], [], [], [], [], []
    for bi, cact in enumerate(cacts):
        for h in range(DN_HEADS):
            hs = slice(h * DN_DIM, (h + 1) * DN_DIM)
            cq.append(cact[:, h * DN_DIM:(h + 1) * DN_DIM])
            ck.append(cact[:, DN_WIDTH + h * DN_DIM:DN_WIDTH + (h + 1) * DN_DIM])
            cv.append(cact[:, 2 * DN_WIDTH + h * DN_DIM:2 * DN_WIDTH + (h + 1) * DN_DIM])
            a.append(ab_ref[bi, :, h:h + 1])
            b.append(ab_ref[bi, :, DN_HEADS + h:DN_HEADS + h + 1])
            z.append(z_ref[bi, :, hs])
            al.append(al_ref[0:1, h:h + 1])
            dt.append(dt_ref[0:1, h:h + 1])
    return tuple(jnp.stack(v) for v in (cq, ck, cv, a, b, z)), jnp.stack(al), jnp.stack(dt)


def _conv_rows(xe_ref, b, w_ref):
    y = w_ref[0:1, :] * xe_ref[b, pl.ds(5, DN_CHUNK), :]
    for i in range(1, DN_CONV):
        y = y + w_ref[i:i + 1, :] * xe_ref[b, pl.ds(5 + i, DN_CHUNK), :]
    return y


def dn_fwd(qkv, z, ab, conv_w, alog, dtb, gain):
    bsz, t, _ = qkv.shape
    nc = t // DN_CHUNK
    c = DN_CHUNK
    nh = bsz * DN_HEADS

    def body(qkv_ref, z_ref, ab_ref, w_ref, al_ref, dt_ref, g_ref, o_ref, sall_ref, tall_ref, xe, s_sc):
        n = pl.program_id(0)

        @pl.when(n == 0)
        def _():
            xe[:, 0:8, :] = jnp.zeros((bsz, 8, 3 * DN_WIDTH), f32)
            s_sc[...] = jnp.zeros_like(s_sc)

        lt, ltt = _dn_consts()
        cacts = []
        for b in range(bsz):
            xe[b, 8:8 + c, :] = qkv_ref[b]
            cacts.append(_silu(_conv_rows(xe, b, w_ref)))
            xe[b, 0:8, :] = xe[b, c:c + 8, :]
        ops, al, dt = _dn_chains(cacts, ab_ref, z_ref, al_ref, dt_ref)
        s = s_sc[...]
        sall_ref[0] = s
        on, sn, tt = dn_chunk(*ops, s, al, dt, g_ref[...], lt, ltt)
        tall_ref[0] = tt
        s_sc[...] = sn
        for b in range(bsz):
            for h in range(DN_HEADS):
                o_ref[b, :, h * DN_DIM:(h + 1) * DN_DIM] = on[b * DN_HEADS + h]

    blk = lambda w: pl.BlockSpec((bsz, c, w), lambda n: (0, n, 0))
    full = lambda shp: pl.BlockSpec(shp, lambda n: (0,) * len(shp))
    return pl.pallas_call(
        body, name="dn_fwd", grid=(nc,),
        in_specs=[blk(3 * DN_WIDTH), blk(DN_WIDTH), blk(128), full((8, 3 * DN_WIDTH)), full((1, 128)), full((1, 128)), full((1, 128))],
        out_specs=[blk(DN_WIDTH), pl.BlockSpec((1, nh, DN_DIM, DN_DIM), lambda n: (n, 0, 0, 0)),
                   pl.BlockSpec((1, nh, c, c), lambda n: (n, 0, 0, 0))],
        out_shape=[SDS((bsz, t, DN_WIDTH), f32), SDS((nc, nh, DN_DIM, DN_DIM), f32), SDS((nc, nh, c, c), f32)],
        scratch_shapes=[pltpu.VMEM((bsz, c + 8, 3 * DN_WIDTH), f32), pltpu.VMEM((nh, DN_DIM, DN_DIM), f32)],
        compiler_params=_cp(("arbitrary",)),
    )(qkv, z, ab, conv_w, alog, dtb, gain)


def dn_bwd(qkv, z, ab, conv_w, alog, dtb, gain, sall, tall, do):
    bsz, t, _ = qkv.shape
    nc = t // DN_CHUNK
    c = DN_CHUNK
    nh = bsz * DN_HEADS
    w3 = 3 * DN_WIDTH

    def body(qkv_ref, prev_ref, z_ref, ab_ref, w_ref, al_ref, dt_ref, g_ref, sall_ref, tall_ref, do_ref,
             dqkv_ref, dz_ref, dab_ref, dw_ref, dal_ref, ddt_ref, dg_ref, xe, dye, dc_sc, ds_sc):
        n = pl.program_id(0)
        first = (nc - 1 - n) == 0

        @pl.when(n == 0)
        def _():
            dye[:, c:c + 8, :] = jnp.zeros((bsz, 8, w3), f32)
            ds_sc[...] = jnp.zeros_like(ds_sc)
            dw_ref[...] = jnp.zeros_like(dw_ref)
            dal_ref[...] = jnp.zeros_like(dal_ref)
            ddt_ref[...] = jnp.zeros_like(ddt_ref)
            dg_ref[...] = jnp.zeros_like(dg_ref)

        lt, ltt = _dn_consts()
        lane = lax.broadcasted_iota(jnp.int32, (1, 128), 1)
        lane_c = lax.broadcasted_iota(jnp.int32, (c, 128), 1)
        ys, sigs = [], []
        for b in range(bsz):
            xe[b, 0:8, :] = jnp.where(first, 0.0, prev_ref[b])
            xe[b, 8:8 + c, :] = qkv_ref[b]
            ys.append(_conv_rows(xe, b, w_ref))
            sigs.append(_sigmoid(ys[b]))
        ops, al, dt = _dn_chains([y * sg for y, sg in zip(ys, sigs)], ab_ref, z_ref, al_ref, dt_ref)
        tt = tall_ref[0]
        _, vjp = jax.vjp(lambda *p: dn_chunk(*p, lt, ltt, t_given=tt)[0:2], *ops, sall_ref[0], al, dt, g_ref[...])
        don = jnp.stack([do_ref[b, :, h * DN_DIM:(h + 1) * DN_DIM] for b in range(bsz) for h in range(DN_HEADS)])
        dcq, dck, dcv, da, db, dzz, dsp, dal, ddt, dgn = vjp((don, ds_sc[...]))
        ds_sc[...] = dsp
        dg_ref[...] += dgn
        for b in range(bsz):
            dab = jnp.zeros((c, 128), f32)
            for h in range(DN_HEADS):
                i = b * DN_HEADS + h
                dc_sc[b, :, h * DN_DIM:(h + 1) * DN_DIM] = dcq[i]
                dc_sc[b, :, DN_WIDTH + h * DN_DIM:DN_WIDTH + (h + 1) * DN_DIM] = dck[i]
                dc_sc[b, :, 2 * DN_WIDTH + h * DN_DIM:2 * DN_WIDTH + (h + 1) * DN_DIM] = dcv[i]
                dz_ref[b, :, h * DN_DIM:(h + 1) * DN_DIM] = dzz[i].astype(bf16)
                dab = dab + jnp.where(lane_c == h, da[i], 0.0) + jnp.where(lane_c == DN_HEADS + h, db[i], 0.0)
                dal_ref[...] += jnp.where(lane == h, dal[i], 0.0)
                ddt_ref[...] += jnp.where(lane == h, ddt[i], 0.0)
            dab_ref[b] = dab.astype(bf16)
            y, sig = ys[b], sigs[b]
            dy = dc_sc[b] * (sig * (1.0 + y * (1.0 - sig)))
            dye[b, 0:c, :] = dy
            dx = w_ref[3:4, :] * dy
            for i in range(DN_CONV - 1):
                dx = dx + w_ref[i:i + 1, :] * dye[b, pl.ds(3 - i, c), :]
            dqkv_ref[b] = dx.astype(bf16)
            for i in range(DN_CONV):
                dw_ref[i:i + 1, :] += jnp.sum(dy * xe[b, pl.ds(5 + i, c), :], axis=0, keepdims=True)
            dye[b, c:c + 8, :] = dye[b, 0:8, :]

    rev = lambda w: pl.BlockSpec((bsz, c, w), lambda n: (0, nc - 1 - n, 0))
    full = lambda shp: pl.BlockSpec(shp, lambda n: (0,) * len(shp))
    prev = pl.BlockSpec((bsz, 8, w3), lambda n: (0, jnp.maximum((nc - 1 - n) * (c // 8) - 1, 0), 0))
    return pl.pallas_call(
        body, name="dn_bwd", grid=(nc,),
        in_specs=[rev(w3), prev, rev(DN_WIDTH), rev(128), full((8, w3)), full((1, 128)), full((1, 128)), full((1, 128)),
                  pl.BlockSpec((1, nh, DN_DIM, DN_DIM), lambda n: (nc - 1 - n, 0, 0, 0)),
                  pl.BlockSpec((1, nh, c, c), lambda n: (nc - 1 - n, 0, 0, 0)), rev(DN_WIDTH)],
        out_specs=[rev(w3), rev(DN_WIDTH), rev(128), full((8, w3)), full((1, 128)), full((1, 128)), full((1, 128))],
        out_shape=[SDS((bsz, t, w3), bf16), SDS((bsz, t, DN_WIDTH), bf16), SDS((bsz, t, 128), bf16),
                   SDS((8, w3), f32), SDS((1, 128), f32), SDS((1, 128), f32), SDS((1, 128), f32)],
        scratch_shapes=[pltpu.VMEM((bsz, c + 8, w3), f32), pltpu.VMEM((bsz, c + 8, w3), f32), pltpu.VMEM((bsz, c, w3), f32),
                        pltpu.VMEM((nh, DN_DIM, DN_DIM), f32)],
        compiler_params=_cp(("arbitrary",)),
    )(qkv, qkv, z, ab, conv_w, alog, dtb, gain, sall, tall, do)


SB_TILE = 256


def sb_fwd(sbqkv, gq, gk):
    bsz, t, _ = sbqkv.shape
    blk = min(SB_TILE, t)
    nq = t // blk
    scale = SB_DIM ** -0.5

    def body(q_ref, k_ref, v_ref, gq_ref, gk_ref, o_ref, l_ref, q2_sc, kn_sc, v_sc):
        bavg = _group_avg_mats()
        lane = lax.broadcasted_iota(jnp.int32, (1, 128), 1)
        first = lane < SB_DIM
        qn = _pair_norm(q_ref[0], gq_ref[...], bavg)
        kn_sc[...] = _pair_norm(k_ref[0], gk_ref[...], bavg).astype(bf16)
        v_sc[...] = v_ref[0].astype(bf16)
        q2_sc[0] = jnp.where(first, qn, 0.0).astype(bf16)
        q2_sc[1] = jnp.where(first, 0.0, qn).astype(bf16)
        r, c = _iota2((blk, blk))
        ustrict = (r > c).astype(bf16)
        r2, c2 = _iota2((2 * blk, blk))
        causal = c2 < (r2 & (blk - 1))

        def tile(q2, ks, acc, rr, diag):
            zz = lax.dot_general(q2, kn_sc[pl.ds(ks, blk), :], NT, preferred_element_type=f32) * scale
            sp = _softplus(zz)
            lm = jnp.where(causal, -sp, 0.0) if diag else -sp
            rem = _dot_x2c(lm, ustrict)
            wgt = jnp.exp(zz - sp + rem + rr)
            if diag:
                wgt = jnp.where(causal, wgt, 0.0)
            acc = acc + _pdot(wgt.astype(bf16), v_sc[pl.ds(ks, blk), :])
            return acc, rr + jnp.sum(lm, axis=1, keepdims=True)

        def qloop(qi, _):
            qs = pl.multiple_of(qi * blk, blk)
            q2 = jnp.concatenate([q2_sc[0, pl.ds(qs, blk), :], q2_sc[1, pl.ds(qs, blk), :]], axis=0)
            carry = tile(q2, qs, jnp.zeros((2 * blk, 128), f32), jnp.zeros((2 * blk, 1), f32), True)
            acc, rr = lax.fori_loop(1, qi + 1, lambda i, cr: tile(q2, pl.multiple_of((qi - i) * blk, blk), *cr, False), carry)
            o_ref[0, pl.ds(qs, blk), :] = jnp.where(first, acc[0:blk], acc[blk:2 * blk])
            l_ref[0, pl.ds(qs, blk), :] = jnp.where(first, rr[0:blk], rr[blk:2 * blk])
            return 0

        lax.fori_loop(0, nq, qloop, 0)

    col = lambda off: pl.BlockSpec((1, t, 128), lambda b, p: (b, 0, off + p))
    gsp = pl.BlockSpec((1, 128), lambda b, p: (0, 0))
    return pl.pallas_call(
        body, name="sb_fwd", grid=(bsz, 2),
        in_specs=[col(0), col(2), col(4), gsp, gsp],
        out_specs=[col(0), col(0)],
        out_shape=[SDS((bsz, t, SB_WIDTH), f32), SDS((bsz, t, SB_WIDTH), f32)],
        scratch_shapes=[pltpu.VMEM((2, t, 128), bf16), pltpu.VMEM((t, 128), bf16), pltpu.VMEM((t, 128), bf16)],
        compiler_params=_cp(("arbitrary", "arbitrary")),
    )(sbqkv, sbqkv, sbqkv, gq, gk)


def sb_bwd(sbqkv, gq, gk, ltot, do):
    bsz, t, _ = sbqkv.shape
    blk = min(SB_TILE, t)
    nq = t // blk
    scale = SB_DIM ** -0.5

    def body(q_ref, k_ref, v_ref, gq_ref, gk_ref, l_ref, do_ref, dq_ref, dk_ref, dv_ref, dgq_ref, dgk_ref,
             q2_sc, kn_sc, v_sc, do2_sc, dqn_sc, dkn_sc, dv_sc):
        bavg = _group_avg_mats()
        lane = lax.broadcasted_iota(jnp.int32, (1, 128), 1)
        first = lane < SB_DIM
        fq = lambda x, g: _pair_norm(x, g, bavg)
        qn, q_vjp = jax.vjp(fq, q_ref[0], gq_ref[...])
        kn, k_vjp = jax.vjp(fq, k_ref[0], gk_ref[...])
        kn_sc[...] = kn.astype(bf16)
        v_sc[...] = v_ref[0].astype(bf16)
        dov = do_ref[0]
        q2_sc[0] = jnp.where(first, qn, 0.0).astype(bf16)
        q2_sc[1] = jnp.where(first, 0.0, qn).astype(bf16)
        do2_sc[0] = jnp.where(first, dov, 0.0).astype(bf16)
        do2_sc[1] = jnp.where(first, 0.0, dov).astype(bf16)
        dkn_sc[...] = jnp.zeros_like(dkn_sc)
        dv_sc[...] = jnp.zeros_like(dv_sc)
        r, c = _iota2((blk, blk))
        pincl = (r <= c).astype(bf16)
        pstrict = (r < c).astype(bf16)
        r2, c2 = _iota2((2 * blk, blk))
        causal = c2 < (r2 & (blk - 1))

        def tile(q2, do2, lt, ks, dq, cs, ce, diag):
            kb = kn_sc[pl.ds(ks, blk), :]
            zz = lax.dot_general(q2, kb, NT, preferred_element_type=f32) * scale
            sp = _softplus(zz)
            lm = jnp.where(causal, -sp, 0.0) if diag else -sp
            pre = _dot_x2c(lm, pincl)
            lp = zz - sp
            wgt = jnp.exp(lp + (lt - cs - pre))
            if diag:
                wgt = jnp.where(causal, wgt, 0.0)
            dw = lax.dot_general(do2, v_sc[pl.ds(ks, blk), :], NT, preferred_element_type=f32)
            e = wgt * dw
            ee = ce + _dot_x2c(e, pstrict)
            sig = jnp.exp(lp)
            dz = (e * (1.0 - sig) - ee * sig) * scale
            if diag:
                dz = jnp.where(causal, dz, 0.0)
            dz = dz.astype(bf16)
            dq = dq + _pdot(dz, kb)
            dkn_sc[pl.ds(ks, blk), :] += lax.dot_general(dz, q2, TN, preferred_element_type=f32)
            dv_sc[pl.ds(ks, blk), :] += lax.dot_general(wgt.astype(bf16), do2, TN, preferred_element_type=f32)
            return dq, cs + jnp.sum(lm, axis=1, keepdims=True), ce + jnp.sum(e, axis=1, keepdims=True)

        def qloop(qi, _):
            qs = pl.multiple_of(qi * blk, blk)
            q2 = jnp.concatenate([q2_sc[0, pl.ds(qs, blk), :], q2_sc[1, pl.ds(qs, blk), :]], axis=0)
            do2 = jnp.concatenate([do2_sc[0, pl.ds(qs, blk), :], do2_sc[1, pl.ds(qs, blk), :]], axis=0)
            lt = jnp.concatenate([l_ref[0, pl.ds(qs, blk), 0:1], l_ref[0, pl.ds(qs, blk), SB_DIM:SB_DIM + 1]], axis=0)
            z1 = jnp.zeros((2 * blk, 1), f32)
            carry = lax.fori_loop(0, qi, lambda kj, cr: tile(q2, do2, lt, pl.multiple_of(kj * blk, blk), *cr, False),
                                  (jnp.zeros((2 * blk, 128), f32), z1, z1))
            dq, _, _ = tile(q2, do2, lt, qs, *carry, True)
            dqn_sc[pl.ds(qs, blk), :] = jnp.where(first, dq[0:blk], dq[blk:2 * blk])
            return 0

        lax.fori_loop(0, nq, qloop, 0)
        dq_pre, dgq = q_vjp(dqn_sc[...])
        dk_pre, dgk = k_vjp(dkn_sc[...])
        dq_ref[0] = dq_pre.astype(bf16)
        dk_ref[0] = dk_pre.astype(bf16)
        dv_ref[0] = dv_sc[...].astype(bf16)
        dgq_ref[0] = jnp.broadcast_to(dgq, (8, 128))
        dgk_ref[0] = jnp.broadcast_to(dgk, (8, 128))

    col = lambda off: pl.BlockSpec((1, t, 128), lambda b, p: (b, 0, off + p))
    gsp = pl.BlockSpec((1, 128), lambda b, p: (0, 0))
    gout = pl.BlockSpec((1, 8, 128), lambda b, p: (b * 2 + p, 0, 0))
    return pl.pallas_call(
        body, name="sb_bwd", grid=(bsz, 2),
        in_specs=[col(0), col(2), col(4), gsp, gsp, col(0), col(0)],
        out_specs=[col(0), col(0), col(0), gout, gout],
        out_shape=[SDS((bsz, t, SB_WIDTH), bf16)] * 3 + [SDS((bsz * 2, 8, 128), f32)] * 2,
        scratch_shapes=[pltpu.VMEM((2, t, 128), bf16), pltpu.VMEM((t, 128), bf16), pltpu.VMEM((t, 128), bf16),
                        pltpu.VMEM((2, t, 128), bf16), pltpu.VMEM((t, 128), f32), pltpu.VMEM((t, 128), f32), pltpu.VMEM((t, 128), f32)],
        compiler_params=_cp(("arbitrary", "arbitrary")),
    )(sbqkv, sbqkv, sbqkv, gq, gk, ltot, do)


def sg_pair(u, v, gain, wa, wb, ba, bb, bavg):
    r, c = _iota2((SG_CHUNK, SG_CHUNK))
    lane = lax.broadcasted_iota(jnp.int32, (1, 128), 1)
    first = lane < SG_DIM
    vn = _pair_norm(_gelu(v), gain, bavg)
    tri = c <= r
    mixed = (mm(jnp.where(tri, wa, 0.0), jnp.where(first, vn, 0.0)) + mm(jnp.where(tri, wb, 0.0), jnp.where(first, 0.0, vn))
             + jnp.where(first, ba, bb))
    return _gelu(u) * mixed


def sg_fwd(sguv, gain, w, bt):
    bsz, t, _ = sguv.shape
    nch = t // SG_CHUNK

    def body(uv_ref, g_ref, w_ref, b_ref, o_ref):
        bavg = _group_avg_mats()
        for p in range(2):
            ls = slice(p * 128, (p + 1) * 128)
            o_ref[0, :, ls] = sg_pair(uv_ref[0, :, ls], uv_ref[0, :, SG_WIDTH + p * 128:SG_WIDTH + (p + 1) * 128], g_ref[:, ls],
                                      w_ref[2 * p], w_ref[2 * p + 1], b_ref[:, 2 * p:2 * p + 1], b_ref[:, 2 * p + 1:2 * p + 2], bavg)

    full = lambda shp: pl.BlockSpec(shp, lambda b, n: (0,) * len(shp))
    return pl.pallas_call(
        body, name="sg_fwd", grid=(bsz, nch),
        in_specs=[pl.BlockSpec((1, SG_CHUNK, 2 * SG_WIDTH), lambda b, n: (b, n, 0)), full((1, SG_WIDTH)),
                  full((SG_GROUPS, SG_CHUNK, SG_CHUNK)), full((SG_CHUNK, 128))],
        out_specs=pl.BlockSpec((1, SG_CHUNK, SG_WIDTH), lambda b, n: (b, n, 0)),
        out_shape=SDS((bsz, t, SG_WIDTH), f32),
        compiler_params=_cp(("arbitrary", "arbitrary")),
    )(sguv, gain, w, bt)


def sg_bwd(sguv, gain, w, bt, do):
    bsz, t, _ = sguv.shape
    nch = t // SG_CHUNK

    def body(uv_ref, g_ref, w_ref, b_ref, do_ref, duv_ref, dg_ref, dw_ref, db_ref):
        @pl.when((pl.program_id(0) == 0) & (pl.program_id(1) == 0))
        def _():
            dg_ref[...] = jnp.zeros_like(dg_ref)
            dw_ref[...] = jnp.zeros_like(dw_ref)
            db_ref[...] = jnp.zeros_like(db_ref)

        bavg = _group_avg_mats()
        lane = lax.broadcasted_iota(jnp.int32, (SG_CHUNK, 128), 1)
        dbt = jnp.zeros((SG_CHUNK, 128), f32)
        for p in range(2):
            ls = slice(p * 128, (p + 1) * 128)
            vs = slice(SG_WIDTH + p * 128, SG_WIDTH + (p + 1) * 128)
            prim = (uv_ref[0, :, ls], uv_ref[0, :, vs], g_ref[:, ls], w_ref[2 * p], w_ref[2 * p + 1],
                    b_ref[:, 2 * p:2 * p + 1], b_ref[:, 2 * p + 1:2 * p + 2])
            _, vjp = jax.vjp(lambda *a: sg_pair(*a, bavg), *prim)
            du, dv, dgn, dwa, dwb, dba, dbb = vjp(do_ref[0, :, ls])
            duv_ref[0, :, ls] = du.astype(bf16)
            duv_ref[0, :, vs] = dv.astype(bf16)
            dg_ref[:, ls] += dgn
            dw_ref[2 * p] += dwa
            dw_ref[2 * p + 1] += dwb
            dbt = dbt + jnp.where(lane == 2 * p, dba, 0.0) + jnp.where(lane == 2 * p + 1, dbb, 0.0)
        db_ref[...] += dbt

    full = lambda shp: pl.BlockSpec(shp, lambda b, n: (0,) * len(shp))
    return pl.pallas_call(
        body, name="sg_bwd", grid=(bsz, nch),
        in_specs=[pl.BlockSpec((1, SG_CHUNK, 2 * SG_WIDTH), lambda b, n: (b, n, 0)), full((1, SG_WIDTH)),
                  full((SG_GROUPS, SG_CHUNK, SG_CHUNK)), full((SG_CHUNK, 128)),
                  pl.BlockSpec((1, SG_CHUNK, SG_WIDTH), lambda b, n: (b, n, 0))],
        out_specs=[pl.BlockSpec((1, SG_CHUNK, 2 * SG_WIDTH), lambda b, n: (b, n, 0)), full((1, SG_WIDTH)),
                   full((SG_GROUPS, SG_CHUNK, SG_CHUNK)), full((SG_CHUNK, 128))],
        out_shape=[SDS((bsz, t, 2 * SG_WIDTH), bf16), SDS((1, SG_WIDTH), f32), SDS((SG_GROUPS, SG_CHUNK, SG_CHUNK), f32),
                   SDS((SG_CHUNK, 128), f32)],
        compiler_params=_cp(("arbitrary", "arbitrary")),
    )(sguv, gain, w, bt, do)


def _pad_lanes(v, n=128):
    return jnp.pad(v.reshape(1, -1), ((0, 0), (0, n - v.size)))


def pad_w_in(w):
    return jnp.concatenate([w[:, 0:2048], jnp.pad(w[:, 2048:2056], ((0, 0), (0, 120))), w[:, 2056:]], axis=1)


def unpad_w_in(w):
    return jnp.concatenate([w[:, 0:2048], w[:, C_AB:C_AB + 8], w[:, C_SB:]], axis=1)


def _w_in_runs():
    shard, runs = IN_DIM // N_CHIPS, []
    for s in range(N_CHIPS):
        for a, b, d in ((0, 2048, 0), (2048, 2056, C_AB), (2056, IN_DIM, C_SB)):
            lo, hi = max(shard * s, a), min(shard * (s + 1), b)
            if lo < hi:
                runs.append((s, lo - shard * s, hi - shard * s, d + lo - a))
    return runs


def w_in_from_shards(zone, tr=256):
    def body(z_ref, o_ref):
        o_ref[:, C_AB:C_SB] = jnp.zeros((tr, C_SB - C_AB), zone.dtype)
        for s, a, b, d in _w_in_runs():
            o_ref[:, d:d + b - a] = z_ref[s, :, a:b]

    return pl.pallas_call(
        body, name="w_in_from_shards", grid=(D_MODEL // tr,),
        in_specs=[pl.BlockSpec((N_CHIPS, tr, IN_DIM // N_CHIPS), lambda i: (0, i, 0))],
        out_specs=pl.BlockSpec((tr, IN_PAD), lambda i: (i, 0)), out_shape=SDS((D_MODEL, IN_PAD), zone.dtype),
        compiler_params=_cp(("arbitrary",)))(zone)


def w_in_grad_to_shards(g, tr=256):
    def body(g_ref, o_ref):
        for s, a, b, d in _w_in_runs():
            o_ref[s, :, a:b] = g_ref[:, d:d + b - a]

    return pl.pallas_call(
        body, name="w_in_grad_to_shards", grid=(D_MODEL // tr,),
        in_specs=[pl.BlockSpec((tr, IN_PAD), lambda i: (i, 0))],
        out_specs=pl.BlockSpec((N_CHIPS, tr, IN_DIM // N_CHIPS), lambda i: (0, i, 0)),
        out_shape=SDS((N_CHIPS, D_MODEL, IN_DIM // N_CHIPS), g.dtype), compiler_params=_cp(("arbitrary",)))(g)


def layer_params(p, l):
    return dict(
        g1=p["norm1_g"][l].reshape(1, -1), g2=p["norm2_g"][l].reshape(1, -1),
        conv=jnp.pad(p["conv_w"][l], ((0, 4), (0, 0))), alog=_pad_lanes(p["a_log"][l]), dtb=_pad_lanes(p["dt_bias"][l]),
        dng=p["dn_out_g"][l].reshape(1, -1), gq=jnp.tile(p["sb_q_g"][l].reshape(1, -1), (1, 2)),
        gk=jnp.tile(p["sb_k_g"][l].reshape(1, -1), (1, 2)), sgg=p["sg_v_g"][l].reshape(1, -1), sgw=p["sg_w"][l],
        sgb=jnp.pad(p["sg_b"][l].T, ((0, 0), (0, 124))))


def local_step(x, tgt, small, get_w, put_g):
    bsz, t, _ = x.shape
    m = bsz * t
    r3 = lambda a: a.reshape(bsz, t, a.shape[-1])
    r2 = lambda a: a.reshape(m, a.shape[-1])
    xs, saved, ws = x.reshape(m, D_MODEL), [], []
    for l in range(DEPTH):
        sp, w = layer_params(small, l), {}
        w["w_in"] = get_w(l, "in", xs)
        qkv, z, ab, sb, sg = inproj_fwd(xs, sp["g1"], w["w_in"])
        odn, sall, tall = dn_fwd(r3(qkv), r3(z), r3(ab), sp["conv"], sp["alog"], sp["dtb"], sp["dng"])
        osb, ltot = sb_fwd(r3(sb), sp["gq"], sp["gk"])
        osg = sg_fwd(r3(sg), sp["sgg"], sp["sgw"], sp["sgb"])
        w["w_out"] = get_w(l, "out", osg)
        x2, mix = outproj_fwd(xs, r2(odn), r2(osb), r2(osg), w["w_out"])
        w["w_ff1"], w["w_ff2"] = get_w(l, "ff", x2)
        x3 = ffn_fwd(x2, sp["g2"], w["w_ff1"], w["w_ff2"])
        saved.append(dict(x=xs, qkv=qkv, z=z, ab=ab, sb=sb, sg=sg, sall=sall, tall=tall, ltot=ltot, mix=mix, x2=x2))
        ws.append(w)
        xs = x3
    dx, lossp = loss_head(xs, tgt.reshape(m, D_MODEL))
    gsmall = [None] * DEPTH
    token = jnp.zeros((), f32)
    for l in reversed(range(DEPTH)):
        sp, w, s = layer_params(small, l), ws[l], saved[l]
        dx2, dg2, h2, act, df, dyb = ffn_bwd(s["x2"], sp["g2"] + token, w["w_ff1"], w["w_ff2"], dx)
        g_ff1 = tn_matmul(h2, df, f"dw_ff1_{l}", col_shards=N_CHIPS)
        g_ff2 = tn_matmul(act, dyb, f"dw_ff2_{l}")
        dodn, dosb, dosg, dx2b = outproj_bwd(dx2, w["w_out"])
        g_out = tn_matmul(s["mix"], dx2b, f"dw_out_{l}")
        token = put_g(l, "rest", dict(w_out=g_out, w_ff1=g_ff1, w_ff2=g_ff2))
        dqkv, dz, dab, dconv, dalog, ddtb, ddng = dn_bwd(r3(s["qkv"]), r3(s["z"]), r3(s["ab"]), sp["conv"], sp["alog"], sp["dtb"],
                                                        sp["dng"] + token, s["sall"], s["tall"], r3(dodn))
        dsq, dsk, dsv, dgq, dgk = sb_bwd(r3(s["sb"]), sp["gq"], sp["gk"], s["ltot"], r3(dosb))
        dsg, dsgg, dsgw, dsgb = sg_bwd(r3(s["sg"]), sp["sgg"], sp["sgw"], sp["sgb"], r3(dosg))
        dproj = jnp.concatenate([r2(dqkv), r2(dz), r2(dab), r2(dsq), r2(dsk), r2(dsv), r2(dsg)], axis=1)
        dx, dg1, h1 = inproj_bwd(s["x"], sp["g1"], w["w_in"], dproj, dx2)
        g_in = tn_matmul(h1, dproj, f"dw_in_{l}")
        token = put_g(l, "in", dict(w_in=g_in))
        fold = lambda a: (a[:, 0, :].sum(0).reshape(2, SB_DIM)).sum(0)
        gsmall[l] = dict(norm1_g=dg1[0], conv_w=dconv[0:DN_CONV], a_log=dalog[0, 0:DN_HEADS], dt_bias=ddtb[0, 0:DN_HEADS],
                         dn_out_g=ddng[0], sb_q_g=fold(dgq), sb_k_g=fold(dgk), sg_v_g=dsgg[0], sg_w=dsgw,
                         sg_b=dsgb[:, 0:SG_GROUPS].T, norm2_g=dg2[0])
    return lossp, dx.reshape(bsz, t, D_MODEL), gsmall


def _chip_peers(x, y):
    return [(1 - x, y), (x, 1 - y), (1 - x, 1 - y)]


_HBM = pl.BlockSpec(memory_space=pltpu.HBM)
_SEM = pl.BlockSpec(memory_space=pltpu.SEMAPHORE)
_EFFECT = pltpu.SideEffectType.DATAFLOW_SIDE_EFFECTING


def _hbm(a):
    return pltpu.with_memory_space_constraint(a, pltpu.HBM)


def _my_half(ref):
    half = ref.shape[0] // 2
    return ref.at[pl.ds(pl.multiple_of(lax.axis_index("c") * half, 8), half)]


def _exchange_copy(src, land, k, j, send, recv, scatter, halve, waiting):
    x, y, c = lax.axis_index("x"), lax.axis_index("y"), lax.axis_index("c")
    px, py = _chip_peers(x, y)[j]
    me, peer = 2 * x + y, 2 * px + py
    if scatter:
        src = src.at[me if waiting else peer]
    dst = land.at[peer if waiting else me]
    if halve:
        src, dst = _my_half(src), _my_half(dst)
    return pltpu.make_async_remote_copy(src_ref=src, dst_ref=dst, send_sem=send.at[3 * k + j],
                                        recv_sem=recv.at[3 * k + j], device_id=(px, py, c), device_id_type=MESH)


def exchange_start(items, name, scatter):
    arrs = []
    for a, _, _ in items:
        if not any(a is b for b in arrs):
            arrs.append(a)
    pos = [next(i for i, b in enumerate(arrs) if b is a) for a, _, _ in items]
    shapes = [a.shape if idx is None else a.shape[1:] for a, idx, _ in items]
    lands = [lax.empty(s if scatter else (N_CHIPS,) + s, a.dtype) for (a, _, _), s in zip(items, shapes)]
    na, nl = len(arrs), len(lands)

    def body(*refs):
        ins, lnd = refs[:na], refs[na:na + nl]
        send, recv = refs[na + nl], refs[na + nl + 1]
        token = refs[-1]
        for k, (_, idx, halve) in enumerate(items):
            src = ins[pos[k]] if idx is None else ins[pos[k]].at[idx]
            for j in range(3):
                _exchange_copy(src, lnd[k], k, j, send, recv, scatter, halve, False).start()
        token[...] = jnp.zeros_like(token)

    sems = pltpu.SemaphoreType.DMA((3 * nl,))
    out = pl.pallas_call(
        body, name=name,
        out_shape=(sems, sems, *[pltpu.HBM(a.shape, a.dtype) for a in arrs + lands], SDS((8, 128), f32)),
        in_specs=[_HBM] * (na + nl), out_specs=(_SEM, _SEM, *[_HBM] * (na + nl), pl.BlockSpec(memory_space=pltpu.VMEM)),
        input_output_aliases={i: 2 + i for i in range(na + nl)},
        compiler_params=pltpu.CompilerParams(has_side_effects=_EFFECT),
    )(*[_hbm(a) for a in arrs + lands])
    thru = out[2:2 + na]
    return dict(send=out[0], recv=out[1], src=[(thru[pos[k]], idx) for k, (_, idx, _) in enumerate(items)],
                halve=[h for _, _, h in items], land=list(out[2 + na:2 + na + nl]), token=out[-1], scatter=scatter)


def exchange_wait(st, ks, after, name):
    arrs = []
    for k in ks:
        if not any(st["src"][k][0] is b for b in arrs):
            arrs.append(st["src"][k][0])
    pos = [next(i for i, b in enumerate(arrs) if b is st["src"][k][0]) for k in ks]
    lands = [st["land"][k] for k in ks]
    na, nl = len(arrs), len(lands)

    def body(*refs):
        ins, lnd = refs[:na], refs[na:na + nl]
        send, recv = refs[na + nl], refs[na + nl + 1]
        for t, k in enumerate(ks):
            idx = st["src"][k][1]
            src = ins[pos[t]] if idx is None else ins[pos[t]].at[idx]
            for j in range(3):
                cp = _exchange_copy(src, lnd[t], k, j, send, recv, st["scatter"], st["halve"][k], True)
                cp.wait_send()
                cp.wait_recv()

    out = pl.pallas_call(
        body, name=name, out_shape=tuple(pltpu.HBM(a.shape, a.dtype) for a in arrs + lands),
        in_specs=[_HBM] * (na + nl) + [_SEM, _SEM, pl.BlockSpec(memory_space=pl.ANY)], out_specs=tuple([_HBM] * (na + nl)),
        input_output_aliases={i: i for i in range(na + nl)},
        compiler_params=pltpu.CompilerParams(has_side_effects=_EFFECT),
    )(*arrs, *lands, st["send"], st["recv"], after)
    for k, (a, idx) in enumerate(st["src"]):
        for p, b in enumerate(arrs):
            if a is b:
                st["src"][k] = (out[p], idx)
    return list(out[na:na + nl])


def swap_cores(arrs, name):
    n = len(arrs)

    def body(*refs):
        ins, outs = refs[:n], refs[n:2 * n]
        send, recv = refs[2 * n:]
        sib = (lax.axis_index("x"), lax.axis_index("y"), 1 - lax.axis_index("c"))
        cps = [pltpu.make_async_remote_copy(src_ref=ins[i], dst_ref=outs[i], send_sem=send.at[i], recv_sem=recv.at[i],
                                            device_id=sib, device_id_type=MESH) for i in range(n)]
        for cp in cps:
            cp.start()
        for cp in cps:
            cp.wait()

    any_spec = pl.BlockSpec(memory_space=pl.ANY)
    return pl.pallas_call(
        body, name=name, in_specs=[any_spec] * n, out_specs=[any_spec] * n, out_shape=[SDS(a.shape, a.dtype) for a in arrs],
        scratch_shapes=[pltpu.SemaphoreType.DMA((n,)), pltpu.SemaphoreType.DMA((n,))],
    )(*arrs)


def swap_halves(zones, name):
    n = len(zones)

    def body(*refs):
        outs = refs[n:2 * n]
        send, recv = refs[2 * n:]
        x, y, c = lax.axis_index("x"), lax.axis_index("y"), lax.axis_index("c")
        cps = []
        for i in range(n):
            for j, (px, py) in enumerate(_chip_peers(x, y)):
                part = _my_half(outs[i].at[2 * px + py])
                cps.append(pltpu.make_async_remote_copy(src_ref=part, dst_ref=part, send_sem=send.at[3 * i + j],
                                                        recv_sem=recv.at[3 * i + j], device_id=(x, y, 1 - c), device_id_type=MESH))
        for cp in cps:
            cp.start()
        for i in range(n):
            for j in range(3):
                cps[3 * i + j].wait_send()
                cps[3 * i + j].wait_recv()

    any_spec = pl.BlockSpec(memory_space=pl.ANY)
    return pl.pallas_call(
        body, name=name, in_specs=[any_spec] * n, out_specs=[any_spec] * n, out_shape=[SDS(a.shape, a.dtype) for a in zones],
        input_output_aliases={i: i for i in range(n)},
        scratch_shapes=[pltpu.SemaphoreType.DMA((3 * n,)), pltpu.SemaphoreType.DMA((3 * n,))],
    )(*zones)


def swap_other_halves(arrs, name):
    n = len(arrs)

    def body(*refs):
        ins, outs = refs[:n], refs[n:2 * n]
        send, recv = refs[2 * n:]
        x, y, c = lax.axis_index("x"), lax.axis_index("y"), lax.axis_index("c")
        cps = [pltpu.make_async_remote_copy(src_ref=ins[i].at[:, 1 - c], dst_ref=outs[i], send_sem=send.at[i], recv_sem=recv.at[i],
                                            device_id=(x, y, 1 - c), device_id_type=MESH) for i in range(n)]
        for cp in cps:
            cp.start()
        for cp in cps:
            cp.wait()

    any_spec = pl.BlockSpec(memory_space=pl.ANY)
    return pl.pallas_call(
        body, name=name, in_specs=[any_spec] * n, out_specs=[any_spec] * n,
        out_shape=[SDS((a.shape[0],) + a.shape[2:], a.dtype) for a in arrs],
        scratch_shapes=[pltpu.SemaphoreType.DMA((n,)), pltpu.SemaphoreType.DMA((n,))],
    )(*arrs)


def _ids_spec(grid, in_specs, out_specs):
    return pltpu.PrefetchScalarGridSpec(num_scalar_prefetch=1, grid=grid, in_specs=in_specs, out_specs=out_specs)


def pair_sum(ids, a, b, name, tr=512):
    nd, _, rows, cols = a.shape
    tr = min(tr, rows)
    assert rows % tr == 0

    def body(ids_ref, a_ref, b_ref, o_ref):
        o_ref[...] = (a_ref[0].astype(f32) + b_ref[...].astype(f32)).astype(bf16)

    spec = pl.BlockSpec((1, tr, cols), lambda d, i, ids: (d, i, 0))
    return pl.pallas_call(
        body, name=name,
        grid_spec=_ids_spec((nd, rows // tr), [pl.BlockSpec((1, 1, tr, cols), lambda d, i, ids: (d, ids[1], i, 0)), spec], spec),
        out_shape=SDS((nd, rows, cols), bf16), compiler_params=_cp(("arbitrary", "arbitrary")))(ids, a, b)


def allreduce_small(v):
    def body(v_ref, o_ref, rbuf, send, recv):
        x, y, c = lax.axis_index("x"), lax.axis_index("y"), lax.axis_index("c")
        o_ref[...] = v_ref[...]
        for s, peer in enumerate([(x, y, 1 - c), (1 - x, y, c), (x, 1 - y, c)]):
            cp = pltpu.make_async_remote_copy(src_ref=o_ref, dst_ref=rbuf.at[s], send_sem=send.at[s], recv_sem=recv.at[s],
                                              device_id=peer, device_id_type=MESH)
            cp.start()
            cp.wait()
            o_ref[...] = o_ref[...] + rbuf[s]

    vm = pl.BlockSpec(memory_space=pltpu.VMEM)
    return pl.pallas_call(
        body, name="allreduce_small", in_specs=[vm], out_specs=vm, out_shape=SDS(v.shape, f32),
        scratch_shapes=[pltpu.VMEM((3,) + v.shape, f32), pltpu.SemaphoreType.DMA((3,)), pltpu.SemaphoreType.DMA((3,))],
        compiler_params=_cp(),
    )(v)


def sum_partials(ids, zone, mine, name, tr=256):
    _, rows, cols = zone.shape
    tr = min(tr, rows)
    assert rows % tr == 0

    def body(ids_ref, m_ref, z1_ref, z2_ref, z3_ref, o_ref):
        o_ref[...] = ((m_ref[0].astype(f32) + z1_ref[0].astype(f32)) + z2_ref[0].astype(f32)) + z3_ref[0].astype(f32)

    slot = lambda flip: pl.BlockSpec((1, tr, cols), lambda i, ids: (ids[0] ^ flip, i, 0))
    return pl.pallas_call(
        body, name=name,
        grid_spec=_ids_spec((rows // tr,), [slot(0), slot(1), slot(2), slot(3)], pl.BlockSpec((tr, cols), lambda i, ids: (i, 0))),
        out_shape=SDS((rows, cols), f32), compiler_params=_cp(("arbitrary",)),
    )(ids, mine, zone, zone, zone)


def adamw(w, m, v, gs, name, layer=0, prev=None, tr=256):
    hrows, cols = gs[0].shape
    rows = hrows * len(gs)
    tr = min(tr, hrows)
    assert hrows % tr == 0 and w.shape[0] % rows == 0
    off, nth = layer * (rows // tr), hrows // tr

    def body(w_ref, m_ref, v_ref, *rest):
        g_ref, d_ref, mo_ref, vo_ref = rest[-4:]
        if len(gs) == 1:
            g = rest[0][...]
        else:
            g = jnp.where(pl.program_id(0) // nth == lax.axis_index("c"), rest[0][...], rest[1][...])
        mn = ADAM_B1 * m_ref[...] + (1.0 - ADAM_B1) * g
        vn = ADAM_B2 * v_ref[...] + (1.0 - ADAM_B2) * jnp.square(g)
        m_hat = mn / (1.0 - ADAM_B1 ** ADAM_STEP)
        v_hat = vn / (1.0 - ADAM_B2 ** ADAM_STEP)
        g_ref[...] = g
        d_ref[...] = -ADAM_LR * (m_hat / (jnp.sqrt(v_hat) + ADAM_EPS) + ADAM_WD * w_ref[...])
        mo_ref[...] = mn
        vo_ref[...] = vn

    loc = pl.BlockSpec((tr, cols), lambda i: (i % nth, 0))
    glob = pl.BlockSpec((tr, cols), lambda i: (off + i, 0))
    extra = [] if prev is None else list(prev)
    return pl.pallas_call(
        body, name=name, grid=(rows // tr,),
        in_specs=[glob] * 3 + [loc] * len(gs) + [pl.BlockSpec(memory_space=pl.ANY)] * len(extra),
        out_specs=[glob] * 4, out_shape=[SDS(w.shape, f32)] * 4,
        input_output_aliases={3 + len(gs) + j: j for j in range(len(extra))},
        compiler_params=_cp(("arbitrary",)),
    )(w, m, v, *gs, *extra)


BIG = ("w_in", "w_out", "w_ff1", "w_ff2")
SMALL = ("norm1_g", "conv_w", "a_log", "dt_bias", "dn_out_g", "sb_q_g", "sb_k_g", "sg_v_g", "sg_w", "sg_b", "norm2_g")
WEIGHTS = ("norm1_g", "w_in", "conv_w", "a_log", "dt_bias", "dn_out_g", "sb_q_g", "sb_k_g", "sg_v_g", "sg_w", "sg_b",
           "w_out", "norm2_g", "w_ff1", "w_ff2")


PACK_ROWS = 256


def _rows_of(shape):
    n = 1
    for d in shape:
        n *= d
    return -(-n // 1024) * 8, n


def _pack(arrs):
    parts = []
    for a in arrs:
        r, n = _rows_of(a.shape)
        parts.append(jnp.pad(a.reshape(-1), (0, r * 128 - n)).reshape(r, 128))
    rows = sum(p.shape[0] for p in parts)
    parts.append(jnp.zeros((-rows % PACK_ROWS, 128), arrs[0].dtype))
    return jnp.concatenate(parts, axis=0)


def _unpack(packed, shapes):
    out, o = [], 0
    for s in shapes:
        r, n = _rows_of(s)
        out.append(packed[o:o + r].reshape(-1)[0:n].reshape(s))
        o += r
    return out


def kernel(x, norm1_g, w_in, conv_w, a_log, dt_bias, dn_out_g, sb_q_g, sb_k_g, sg_v_g, sg_w, sg_b, w_out, norm2_g, w_ff1, w_ff2, loss_target, m_norm1_g, m_w_in, m_conv_w, m_a_log, m_dt_bias, m_dn_out_g, m_sb_q_g, m_sb_k_g, m_sg_v_g, m_sg_w, m_sg_b, m_w_out, m_norm2_g, m_w_ff1, m_w_ff2, v_norm1_g, v_w_in, v_conv_w, v_a_log, v_dt_bias, v_dn_out_g, v_sb_q_g, v_sb_k_g, v_sg_v_g, v_sg_w, v_sg_b, v_w_out, v_norm2_g, v_w_ff1, v_w_ff2):
    w = dict(norm1_g=norm1_g, w_in=w_in, conv_w=conv_w, a_log=a_log, dt_bias=dt_bias, dn_out_g=dn_out_g, sb_q_g=sb_q_g,
             sb_k_g=sb_k_g, sg_v_g=sg_v_g, sg_w=sg_w, sg_b=sg_b, w_out=w_out, norm2_g=norm2_g, w_ff1=w_ff1, w_ff2=w_ff2)
    mom = dict(norm1_g=m_norm1_g, w_in=m_w_in, conv_w=m_conv_w, a_log=m_a_log, dt_bias=m_dt_bias, dn_out_g=m_dn_out_g,
               sb_q_g=m_sb_q_g, sb_k_g=m_sb_k_g, sg_v_g=m_sg_v_g, sg_w=m_sg_w, sg_b=m_sg_b, w_out=m_w_out, norm2_g=m_norm2_g,
               w_ff1=m_w_ff1, w_ff2=m_w_ff2)
    var = dict(norm1_g=v_norm1_g, w_in=v_w_in, conv_w=v_conv_w, a_log=v_a_log, dt_bias=v_dt_bias, dn_out_g=v_dn_out_g,
               sb_q_g=v_sb_q_g, sb_k_g=v_sb_k_g, sg_v_g=v_sg_v_g, sg_w=v_sg_w, sg_b=v_sg_b, w_out=v_w_out, norm2_g=v_norm2_g,
               w_ff1=v_w_ff1, w_ff2=v_w_ff2)
    chip = 2 * lax.axis_index("x") + lax.axis_index("y")

    wb = {k: w[k].astype(bf16) for k in BIG}
    ag = exchange_start([(conv_w, None, False)] + [(wb[k], l, True) for l in range(DEPTH) for k in BIG], "allgather_start",
                        scatter=False)
    item = lambda l, k: 1 + l * len(BIG) + BIG.index(k)

    def landed(ks, after, name):
        zones = exchange_wait(ag, ks, after, name)
        halved = [t for t, k in enumerate(ks) if ag["halve"][k]]
        for t, z in zip(halved, swap_halves([zones[t] for t in halved], name.replace("wait", "pass"))):
            zones[t] = z
        own = [ag["src"][k][0] if ag["src"][k][1] is None else ag["src"][k][0][ag["src"][k][1]] for k in ks]
        return [lax.dynamic_update_slice_in_dim(z, o[None], chip, axis=0) for z, o in zip(zones, own)]

    def whole(k, z):
        if k == "w_in":
            return w_in_from_shards(z)
        if k == "w_ff1":
            return jnp.transpose(z, (1, 0, 2)).reshape(D_MODEL, D_FF)
        return z.reshape(-1, D_MODEL)

    g_conv, first_in = landed([0, item(0, "w_in")], x, "allgather_wait_in0")
    small = {k: w[k] for k in SMALL}
    small["conv_w"] = jnp.transpose(g_conv, (1, 2, 0, 3)).reshape(DEPTH, DN_CONV, 3 * DN_WIDTH)
    cache = {}

    def get_w(l, part, after):
        if part == "in":
            return whole("w_in", first_in if l == 0 else landed([item(l, "w_in")], after, f"allgather_wait_in{l}")[0])
        if part == "out":
            zs = landed([item(l, k) for k in ("w_out", "w_ff1", "w_ff2")], after, f"allgather_wait_rest{l}")
            cache[l] = (whole("w_ff1", zs[1]), whole("w_ff2", zs[2]))
            return whole("w_out", zs[0])
        return cache[l]

    rs = {}
    ids = jnp.stack([chip, lax.axis_index("c")]).astype(jnp.int32)

    def put_g(l, tag, g):
        names = [k for k in BIG if k in g]
        by_dest = [w_in_grad_to_shards(g[k]) if k == "w_in" else g[k] for k in names]
        halves = [a.reshape(N_CHIPS, 2, -1, a.shape[-1]) for a in by_dest]
        got = swap_other_halves(halves, f"pair_swap_{tag}{l}")
        pair = [pair_sum(ids, a, b, f"pair_sum_{k}_{l}") for k, a, b in zip(names, halves, got)]
        rs[l, tag] = dict(exchange_start([(a, None, False) for a in pair], f"scatter_start_{tag}{l}", scatter=True), names=names)
        return rs[l, tag]["token"][0, 0]

    lossp, grad_x, gsmall = local_step(x, loss_target, small, get_w, put_g)
    loss = lax.psum(jnp.sum(lossp), ("x", "y", "c"))

    def finish(l, tag, after, prev):
        st = rs[l, tag]
        ks = list(range(len(st["names"])))
        zones = exchange_wait(st, ks, after, f"scatter_wait_{tag}{l}")
        sums = [sum_partials(ids, zones[i], st["src"][i][0], f"sum_{k}_{l}") for i, k in enumerate(st["names"])]
        others = swap_cores(sums, f"swap_grad_sums_{tag}{l}")
        outs = dict(prev)
        for i, k in enumerate(st["names"]):
            r2 = lambda a: a.reshape(-1, a.shape[-1])
            outs[k] = adamw(r2(w[k]), r2(mom[k]), r2(var[k]), (sums[i], others[i]), f"adamw_{k}_{l}", layer=l, prev=prev.get(k))
        return outs

    done = finish(1, "rest", rs[0, "in"]["token"], {})
    done = finish(1, "in", done["w_ff2"][0], done)
    res = {}

    full_shapes = [(DEPTH,) + tuple(gsmall[0][k].shape) for k in SMALL]
    packed = _pack([jnp.stack([gsmall[l][k] for l in range(DEPTH)]) for k in SMALL])
    total = allreduce_small(packed)
    gfull = dict(zip(SMALL, _unpack(total, full_shapes)))
    cs = 3 * DN_WIDTH // N_CHIPS
    gfull["conv_w"] = lax.dynamic_slice_in_dim(gfull["conv_w"], chip * cs, cs, axis=2)
    gp, wp, mp, vp = (_pack([d[k] for k in SMALL]) for d in (gfull, w, mom, var))
    outs = adamw(wp, mp, vp, (gp,), "adamw_small")
    loc_shapes = [w[k].shape for k in SMALL]
    unp = [_unpack(o, loc_shapes) for o in outs]
    for i, k in enumerate(SMALL):
        res[k] = [unp[j][i] for j in range(4)]

    done = finish(0, "rest", outs[0], done)
    done = finish(0, "in", done["w_ff2"][0], done)
    for k in BIG:
        res[k] = [o.reshape(w[k].shape) for o in done[k]]

    return (loss, grad_x, *[res[k][0] for k in WEIGHTS], *[res[k][1] for k in WEIGHTS], *[res[k][2] for k in WEIGHTS],
            *[res[k][3] for k in WEIGHTS])
```

```python
import functools

import jax
import jax.numpy as jnp
from jax import lax
from jax.experimental import pallas as pl
from jax.experimental.pallas import tpu as pltpu

f32 = jnp.float32
bf16 = jnp.bfloat16
SDS = jax.ShapeDtypeStruct
MESH = pl.DeviceIdType.MESH

NORM_EPS = 1e-6
D_MODEL = 1024
DEPTH = 2
DN_HEADS, DN_DIM, DN_WIDTH, DN_CONV, DN_CHUNK = 4, 128, 512, 4, 64
SB_HEADS, SB_DIM, SB_WIDTH, SB_BLOCK = 4, 64, 256, 128
SG_GROUPS, SG_DIM, SG_WIDTH, SG_CHUNK = 4, 64, 256, 128
D_FF = 4096
IN_DIM = 3336
C_QKV, C_Z, C_AB, C_SB, C_SG, IN_PAD = 0, 1536, 2048, 2176, 2944, 3456
N_CHIPS = 4

ADAM_LR, ADAM_B1, ADAM_B2, ADAM_EPS, ADAM_WD, ADAM_STEP = 0.001, 0.9, 0.999, 1e-08, 0.01, 10

VMEM_LIMIT = 56 * 1024 * 1024


def _cp(sem=None, **kw):
    if sem is not None:
        kw["dimension_semantics"] = sem
    return pltpu.CompilerParams(vmem_limit_bytes=VMEM_LIMIT, **kw)


def _split2(x):
    hi = x.astype(bf16)
    lo = (x - hi.astype(f32)).astype(bf16)
    return hi, lo


NT = (((1,), (1,)), ((), ()))
TN = (((0,), (0,)), ((), ()))
_DIMS2 = dict(nn=(((1,), (0,)), ((), ())), nt=NT, tn=TN)
_DIMS3 = dict(nn=(((2,), (1,)), ((0,), (0,))), nt=(((2,), (2,)), ((0,), (0,))), tn=(((1,), (1,)), ((0,), (0,))))


def _dg(a, b, kind):
    return lax.dot_general(a, b, (_DIMS2 if a.ndim == 2 else _DIMS3)[kind], preferred_element_type=f32)


def _pdot(a, b):
    return _dg(a, b, "nn")


def _dot_hp(a, b):
    ah, al = _split2(a)
    bh, bl = _split2(b)
    return _pdot(ah, bh) + _pdot(ah, bl) + _pdot(al, bh)


def _dot_x2c(a, m):
    lead = a.shape[:-1]
    ah, al = _split2(a.reshape(-1, a.shape[-1]))
    return (_pdot(ah, m) + _pdot(al, m)).reshape(lead + (m.shape[1],))


def _dot_cx2(m, a):
    if a.ndim == 3:
        m = jnp.broadcast_to(m, (a.shape[0],) + m.shape)
    ah, al = _split2(a)
    return _pdot(m, ah) + _pdot(m, al)


def _nt(a, b):
    return _dg(a.astype(bf16), b.astype(bf16), "nt")


def _tn(a, b):
    return _dg(a.astype(bf16), b.astype(bf16), "tn")


def _nn(a, b):
    return _dg(a.astype(bf16), b.astype(bf16), "nn")


@jax.custom_vjp
def mm(a, b):
    return _nn(a, b)


mm.defvjp(lambda a, b: (_nn(a, b), (a, b)), lambda r, g: (_nt(g, r[1]), _tn(r[0], g)))


@jax.custom_vjp
def mm_nt(a, b):
    return _nt(a, b)


mm_nt.defvjp(lambda a, b: (_nt(a, b), (a, b)), lambda r, g: (_nn(g, r[1]), _tn(g, r[0])))


@jax.custom_vjp
def mm_tn(a, b):
    return _tn(a, b)


mm_tn.defvjp(lambda a, b: (_tn(a, b), (a, b)), lambda r, g: (_nt(r[1], g), _nn(r[0], g)))


@jax.custom_vjp
def rmul_const(a, m, mt):
    return _dot_x2c(a, m)


rmul_const.defvjp(lambda a, m, mt: (_dot_x2c(a, m), (m, mt)),
                  lambda r, g: (_dot_x2c(g, r[1]), jnp.zeros_like(r[0]), jnp.zeros_like(r[1])))


@jax.custom_vjp
def lmul_const(m, mt, a):
    return _dot_cx2(m, a)


lmul_const.defvjp(lambda m, mt, a: (_dot_cx2(m, a), (m, mt)),
                  lambda r, g: (jnp.zeros_like(r[0]), jnp.zeros_like(r[1]), _dot_cx2(r[1], g)))


@jax.custom_vjp
def mm_hl(t, x):
    th, tl = _split2(t)
    xb = x.astype(bf16)
    return _pdot(th, xb) + _pdot(tl, xb)


def _mm_hl_bwd(r, g):
    t, x = r
    th, tl = _split2(t)
    gb = g.astype(bf16)
    return _nt(g, x), _dg(th, gb, "tn") + _dg(tl, gb, "tn")


mm_hl.defvjp(lambda t, x: (mm_hl(t, x), (t, x)), _mm_hl_bwd)


def inv_unit_lower(lm):
    c = lm.shape[-1]
    r, cc = _iota2((c, c))
    eye = (r == cc).astype(f32)
    t = eye - lm
    p = -lm
    k = 1
    while 2 * k < c:
        p = _nn(p, p)
        t = t + _nn(t, p)
        k *= 2
    res = eye - t - _dot_hp(lm, t)
    return t + _nn(t, res)


@jax.custom_vjp
def inv_given(lm, t):
    return t


inv_given.defvjp(lambda lm, t: (t, t), lambda t, g: (-_nt(_tn(t, g), t), jnp.zeros_like(t)))


def _sigmoid(x):
    return 1.0 / (1.0 + jnp.exp(-x))


def _softplus(x):
    return jnp.maximum(x, 0.0) + jnp.log(1.0 + jnp.exp(-jnp.abs(x)))


def _silu(x):
    return x * _sigmoid(x)


def _gelu(x):
    return 0.5 * x * (1.0 + jnp.tanh(0.7978845608028654 * (x + 0.044715 * (x * x * x))))


def _iota2(shape):
    return lax.broadcasted_iota(jnp.int32, shape, 0), lax.broadcasted_iota(jnp.int32, shape, 1)


def _group_avg_mats():
    r, c = _iota2((128, 128))
    return jnp.where((r // 64) == (c // 64), 1.0 / 64.0, 0.0).astype(bf16)


def _pair_norm(x, gain, bavg):
    ms = rmul_const(x * x, bavg, bavg)
    return x * lax.rsqrt(ms + NORM_EPS) * gain


def _rms(x):
    r = lax.rsqrt(jnp.mean(x * x, axis=-1, keepdims=True) + NORM_EPS)
    return r


_IN_GROUPS = ((C_QKV, C_Z), (C_Z, C_AB), (C_AB, C_SB), (C_SB, C_SG), (C_SG, IN_PAD))


def inproj_fwd(x, g, wp, tm=256):
    m = x.shape[0]

    def body(x_ref, g_ref, w_ref, *outs):
        xv = x_ref[...]
        h = (xv * _rms(xv) * g_ref[...]).astype(bf16)
        for (a, b), o in zip(_IN_GROUPS, outs):
            o[...] = _pdot(h, w_ref[:, a:b])

    return pl.pallas_call(
        body, name="inproj_fwd", grid=(m // tm,),
        in_specs=[pl.BlockSpec((tm, D_MODEL), lambda i: (i, 0)), pl.BlockSpec((1, D_MODEL), lambda i: (0, 0)),
                  pl.BlockSpec((D_MODEL, IN_PAD), lambda i: (0, 0))],
        out_specs=[pl.BlockSpec((tm, b - a), lambda i: (i, 0)) for a, b in _IN_GROUPS],
        out_shape=[SDS((m, b - a), f32) for a, b in _IN_GROUPS],
        compiler_params=_cp(("arbitrary",)),
    )(x, g, wp)


def inproj_bwd(x, g, wp, dproj, dres, tm=256):
    m = x.shape[0]

    def body(x_ref, g_ref, w_ref, dp_ref, dr_ref, dx_ref, dg_ref, h_ref):
        xv = x_ref[...]
        r = _rms(xv)
        xn = xv * r
        gv = g_ref[...]
        h_ref[...] = (xn * gv).astype(bf16)
        dh = lax.dot_general(dp_ref[...], w_ref[...], NT, preferred_element_type=f32)
        dxn = dh * gv
        dx_ref[...] = dr_ref[...] + r * (dxn - xn * jnp.mean(dxn * xn, axis=-1, keepdims=True))

        @pl.when(pl.program_id(0) == 0)
        def _():
            dg_ref[...] = jnp.zeros_like(dg_ref)

        dg_ref[...] += jnp.sum(dh * xn, axis=0, keepdims=True)

    return pl.pallas_call(
        body, name="inproj_bwd", grid=(m // tm,),
        in_specs=[pl.BlockSpec((tm, D_MODEL), lambda i: (i, 0)), pl.BlockSpec((1, D_MODEL), lambda i: (0, 0)),
                  pl.BlockSpec((D_MODEL, IN_PAD), lambda i: (0, 0)), pl.BlockSpec((tm, IN_PAD), lambda i: (i, 0)),
                  pl.BlockSpec((tm, D_MODEL), lambda i: (i, 0))],
        out_specs=[pl.BlockSpec((tm, D_MODEL), lambda i: (i, 0)), pl.BlockSpec((1, D_MODEL), lambda i: (0, 0)),
                   pl.BlockSpec((tm, D_MODEL), lambda i: (i, 0))],
        out_shape=[SDS((m, D_MODEL), f32), SDS((1, D_MODEL), f32), SDS((m, D_MODEL), bf16)],
        compiler_params=_cp(("arbitrary",)),
    )(x, g, wp, dproj, dres)


def outproj_fwd(x, odn, osb, osg, wo, tm=512):
    m = x.shape[0]

    def body(x_ref, a_ref, b_ref, c_ref, w_ref, x2_ref, mix_ref):
        mix_ref[:, 0:DN_WIDTH] = a_ref[...].astype(bf16)
        mix_ref[:, DN_WIDTH:DN_WIDTH + SB_WIDTH] = b_ref[...].astype(bf16)
        mix_ref[:, DN_WIDTH + SB_WIDTH:D_MODEL] = c_ref[...].astype(bf16)
        x2_ref[...] = x_ref[...] + _pdot(mix_ref[...], w_ref[...])

    row = lambda w: pl.BlockSpec((tm, w), lambda i: (i, 0))
    return pl.pallas_call(
        body, name="outproj_fwd", grid=(m // tm,),
        in_specs=[row(D_MODEL), row(DN_WIDTH), row(SB_WIDTH), row(SG_WIDTH), pl.BlockSpec((D_MODEL, D_MODEL), lambda i: (0, 0))],
        out_specs=[row(D_MODEL), row(D_MODEL)],
        out_shape=[SDS((m, D_MODEL), f32), SDS((m, D_MODEL), bf16)],
        compiler_params=_cp(("arbitrary",)),
    )(x, odn, osb, osg, wo)


def outproj_bwd(dx2, wo, tm=512):
    m = dx2.shape[0]

    def body(d_ref, w_ref, a_ref, b_ref, c_ref, db_ref):
        db = d_ref[...].astype(bf16)
        db_ref[...] = db
        dm = lax.dot_general(db, w_ref[...], NT, preferred_element_type=f32)
        a_ref[...] = dm[:, 0:DN_WIDTH]
        b_ref[...] = dm[:, DN_WIDTH:DN_WIDTH + SB_WIDTH]
        c_ref[...] = dm[:, DN_WIDTH + SB_WIDTH:D_MODEL]

    row = lambda w: pl.BlockSpec((tm, w), lambda i: (i, 0))
    return pl.pallas_call(
        body, name="outproj_bwd", grid=(m // tm,),
        in_specs=[row(D_MODEL), pl.BlockSpec((D_MODEL, D_MODEL), lambda i: (0, 0))],
        out_specs=[row(DN_WIDTH), row(SB_WIDTH), row(SG_WIDTH), row(D_MODEL)],
        out_shape=[SDS((m, DN_WIDTH), f32), SDS((m, SB_WIDTH), f32), SDS((m, SG_WIDTH), f32), SDS((m, D_MODEL), bf16)],
        compiler_params=_cp(("arbitrary",)),
    )(dx2, wo)


FF_CHUNK = 1024


def _load_weights_once(pairs, sem):
    @pl.when(pl.program_id(0) == 0)
    def _():
        cps = [pltpu.make_async_copy(h, v, sem.at[i]) for i, (h, v) in enumerate(pairs)]
        for c in cps:
            c.start()
        for c in cps:
            c.wait()


def ffn_fwd(x2, g, w1, w2, tm=256):
    m = x2.shape[0]

    def body(x_ref, g_ref, w1_hbm, w2_hbm, y_ref, w1_v, w2_v, sem):
        _load_weights_once(((w1_hbm, w1_v), (w2_hbm, w2_v)), sem)
        xv = x_ref[...]
        h = (xv * _rms(xv) * g_ref[...]).astype(bf16)
        acc = xv
        for j in range(0, D_FF, FF_CHUNK):
            f = _pdot(h, w1_v[:, j:j + FF_CHUNK])
            rl = jnp.maximum(f, 0.0)
            acc = acc + _pdot((rl * rl).astype(bf16), w2_v[j:j + FF_CHUNK, :])
        y_ref[...] = acc

    return pl.pallas_call(
        body, name="ffn_fwd", grid=(m // tm,),
        in_specs=[pl.BlockSpec((tm, D_MODEL), lambda i: (i, 0)), pl.BlockSpec((1, D_MODEL), lambda i: (0, 0)),
                  pl.BlockSpec(memory_space=pl.ANY), pl.BlockSpec(memory_space=pl.ANY)],
        out_specs=pl.BlockSpec((tm, D_MODEL), lambda i: (i, 0)),
        out_shape=SDS((m, D_MODEL), f32),
        scratch_shapes=[pltpu.VMEM((D_MODEL, D_FF), bf16), pltpu.VMEM((D_FF, D_MODEL), bf16), pltpu.SemaphoreType.DMA((2,))],
        compiler_params=_cp(("arbitrary",)),
    )(x2, g, w1, w2)


def ffn_bwd(x2, g, w1, w2, dy, tm=256):
    m = x2.shape[0]

    def body(x_ref, g_ref, w1_hbm, w2_hbm, dy_ref, dx_ref, dg_ref, h_ref, a_ref, df_ref, dyb_ref, w1_v, w2_v, sem):
        _load_weights_once(((w1_hbm, w1_v), (w2_hbm, w2_v)), sem)
        xv = x_ref[...]
        r = _rms(xv)
        xn = xv * r
        gv = g_ref[...]
        h = (xn * gv).astype(bf16)
        h_ref[...] = h
        dyv = dy_ref[...]
        dyb = dyv.astype(bf16)
        dyb_ref[...] = dyb
        dh = jnp.zeros((tm, D_MODEL), f32)
        for j in range(0, D_FF, FF_CHUNK):
            f = _pdot(h, w1_v[:, j:j + FF_CHUNK])
            rl = jnp.maximum(f, 0.0)
            a_ref[:, j:j + FF_CHUNK] = (rl * rl).astype(bf16)
            da = lax.dot_general(dyb, w2_v[j:j + FF_CHUNK, :], NT, preferred_element_type=f32)
            df = (da * (2.0 * rl)).astype(bf16)
            df_ref[:, j:j + FF_CHUNK] = df
            dh = dh + lax.dot_general(df, w1_v[:, j:j + FF_CHUNK], NT, preferred_element_type=f32)
        dxn = dh * gv
        dx_ref[...] = dyv + r * (dxn - xn * jnp.mean(dxn * xn, axis=-1, keepdims=True))

        @pl.when(pl.program_id(0) == 0)
        def _():
            dg_ref[...] = jnp.zeros_like(dg_ref)

        dg_ref[...] += jnp.sum(dh * xn, axis=0, keepdims=True)

    row = lambda w: pl.BlockSpec((tm, w), lambda i: (i, 0))
    return pl.pallas_call(
        body, name="ffn_bwd", grid=(m // tm,),
        in_specs=[row(D_MODEL), pl.BlockSpec((1, D_MODEL), lambda i: (0, 0)),
                  pl.BlockSpec(memory_space=pl.ANY), pl.BlockSpec(memory_space=pl.ANY), row(D_MODEL)],
        out_specs=[row(D_MODEL), pl.BlockSpec((1, D_MODEL), lambda i: (0, 0)), row(D_MODEL), row(D_FF), row(D_FF), row(D_MODEL)],
        out_shape=[SDS((m, D_MODEL), f32), SDS((1, D_MODEL), f32), SDS((m, D_MODEL), bf16), SDS((m, D_FF), bf16),
                   SDS((m, D_FF), bf16), SDS((m, D_MODEL), bf16)],
        scratch_shapes=[pltpu.VMEM((D_MODEL, D_FF), bf16), pltpu.VMEM((D_FF, D_MODEL), bf16), pltpu.SemaphoreType.DMA((2,))],
        compiler_params=_cp(("arbitrary",)),
    )(x2, g, w1, w2, dy)


def _tile(n, cap):
    best = 128
    for t in range(128, cap + 1, 128):
        if n % t == 0:
            best = t
    return best


def tn_matmul(a, b, name, col_shards=1, tk=2048):
    m, ka = a.shape
    n = b.shape[1]
    ti = _tile(ka, 1024)
    tj = _tile(n // col_shards, 1152)
    tk = min(tk, m)
    nk = m // tk
    jps = (n // col_shards) // tj

    def body(a_ref, b_ref, o_ref, acc):
        k = pl.program_id(2)

        @pl.when(k == 0)
        def _():
            acc[...] = jnp.zeros_like(acc)

        acc[...] += lax.dot_general(a_ref[...], b_ref[...], TN, preferred_element_type=f32)

        @pl.when(k == nk - 1)
        def _():
            o_ref[...] = acc[...].astype(bf16).reshape(o_ref.shape)

    if col_shards == 1:
        out_shape, out_spec = SDS((ka, n), bf16), pl.BlockSpec((ti, tj), lambda i, j, k: (i, j))
    else:
        out_shape = SDS((col_shards, ka, n // col_shards), bf16)
        out_spec = pl.BlockSpec((1, ti, tj), lambda i, j, k: (j // jps, i, j % jps))
    return pl.pallas_call(
        body, name=name, grid=(ka // ti, n // tj, nk),
        in_specs=[pl.BlockSpec((tk, ti), lambda i, j, k: (k, i)), pl.BlockSpec((tk, tj), lambda i, j, k: (k, j))],
        out_specs=out_spec, out_shape=out_shape,
        scratch_shapes=[pltpu.VMEM((ti, tj), f32)],
        compiler_params=_cp(("arbitrary", "arbitrary", "arbitrary")),
    )(a, b)


def loss_head(y, tgt, tm=512):
    m = y.shape[0]

    def body(y_ref, t_ref, dy_ref, l_ref):
        e = y_ref[...] - t_ref[...]
        dy_ref[...] = e * (1.0 / D_MODEL)

        @pl.when(pl.program_id(0) == 0)
        def _():
            l_ref[...] = jnp.zeros_like(l_ref)

        l_ref[...] += jnp.sum(e * e, axis=0, keepdims=True) * (0.5 / D_MODEL)

    row = pl.BlockSpec((tm, D_MODEL), lambda i: (i, 0))
    return pl.pallas_call(
        body, name="loss_head", grid=(m // tm,), in_specs=[row, row],
        out_specs=[row, pl.BlockSpec((1, D_MODEL), lambda i: (0, 0))],
        out_shape=[SDS((m, D_MODEL), f32), SDS((1, D_MODEL), f32)],
        compiler_params=_cp(("arbitrary",)),
    )(y, tgt)


def _dn_consts():
    c = DN_CHUNK
    r, cc = _iota2((c, c))
    lt = (cc <= r).astype(bf16)
    ltt = (r <= cc).astype(bf16)
    return lt, ltt


def dn_chunk(cq, ck, cv, g, beta, z, s, gain, lt, ltt, t_given=None):
    c = DN_CHUNK
    r, cc = _iota2((c, c))
    q = cq * lax.rsqrt(jnp.sum(cq * cq, axis=-1, keepdims=True) + NORM_EPS) * (DN_DIM ** -0.5)
    k = ck * lax.rsqrt(jnp.sum(ck * ck, axis=-1, keepdims=True) + NORM_EPS)
    r2, c2 = _iota2((c, 128))
    uaug = jnp.where((c2 < c) & (r2 > c2), 1.0, 0.0) + jnp.where(c2 == c, 1.0, 0.0)
    gam_all = lmul_const(lt, ltt, g * uaug)
    gam_cc = gam_all[:, :, 0:c]
    gam = gam_all[:, :, c:c + 1]
    dec = jnp.where(cc <= r, jnp.exp(jnp.where(cc <= r, gam_cc, 0.0)), 0.0)
    kk = mm_nt(k, k)
    lm = jnp.where(cc < r, beta * kk * dec, 0.0)
    t = inv_unit_lower(lm) if t_given is None else inv_given(lm, t_given)
    eg = jnp.exp(gam)
    sol = mm_hl(t, jnp.concatenate([cv * beta, k * (beta * eg)], axis=2))
    u, w = sol[:, :, 0:DN_DIM], sol[:, :, DN_DIM:2 * DN_DIM]
    qk = jnp.where(cc <= r, mm_nt(q, k) * dec, 0.0)
    glast = jnp.sum(g, axis=1, keepdims=True)
    qd = q * eg
    kd = k * jnp.exp(glast - gam)
    un = u - mm(w, s)
    o = mm(qd, s) + mm(qk, un)
    s_new = s * jnp.exp(glast) + mm_tn(kd, un)
    on = o * lax.rsqrt(jnp.mean(o * o, axis=-1, keepdims=True) + NORM_EPS) * gain * _silu(z)
    return on, s_new, t


def _dn_gates(ab, al_row, dt_row):
    pre = ab + dt_row
    return -jnp.exp(al_row) * _softplus(pre), _sigmoid(ab), _sigmoid(pre)


def _dn_chains(cacts, gates, z_ref):
    cq, ck, cv, g, beta, z = [], [], [], [], [], []
    for bi, cact in enumerate(cacts):
        for h in range(DN_HEADS):
            cq.append(cact[:, h * DN_DIM:(h + 1) * DN_DIM])
            ck.append(cact[:, DN_WIDTH + h * DN_DIM:DN_WIDTH + (h + 1) * DN_DIM])
            cv.append(cact[:, 2 * DN_WIDTH + h * DN_DIM:2 * DN_WIDTH + (h + 1) * DN_DIM])
            g.append(gates[bi][0][:, h:h + 1])
            beta.append(gates[bi][1][:, DN_HEADS + h:DN_HEADS + h + 1])
            z.append(z_ref[bi, :, h * DN_DIM:(h + 1) * DN_DIM])
    return tuple(jnp.stack(v) for v in (cq, ck, cv, g, beta, z))


def _conv_rows(xe_ref, b, w_ref):
    y = w_ref[0:1, :] * xe_ref[b, pl.ds(5, DN_CHUNK), :]
    for i in range(1, DN_CONV):
        y = y + w_ref[i:i + 1, :] * xe_ref[b, pl.ds(5 + i, DN_CHUNK), :]
    return y


def dn_fwd(qkv, z, ab, conv_w, alog, dtb, gain):
    bsz, t, _ = qkv.shape
    nc = t // DN_CHUNK
    c = DN_CHUNK
    nh = bsz * DN_HEADS

    def body(qkv_ref, z_ref, ab_ref, w_ref, al_ref, dt_ref, g_ref, o_ref, sall_ref, tall_ref, xe, s_sc):
        n = pl.program_id(0)

        @pl.when(n == 0)
        def _():
            xe[:, 0:8, :] = jnp.zeros((bsz, 8, 3 * DN_WIDTH), f32)
            s_sc[...] = jnp.zeros_like(s_sc)

        lt, ltt = _dn_consts()
        cacts = []
        for b in range(bsz):
            xe[b, 8:8 + c, :] = qkv_ref[b]
            cacts.append(_silu(_conv_rows(xe, b, w_ref)))
            xe[b, 0:8, :] = xe[b, c:c + 8, :]
        gates = [_dn_gates(ab_ref[b], al_ref[...], dt_ref[...]) for b in range(bsz)]
        s = s_sc[...]
        sall_ref[0] = s
        on, sn, tt = dn_chunk(*_dn_chains(cacts, gates, z_ref), s, g_ref[...], lt, ltt)
        tall_ref[0] = tt
        s_sc[...] = sn
        for b in range(bsz):
            for h in range(DN_HEADS):
                o_ref[b, :, h * DN_DIM:(h + 1) * DN_DIM] = on[b * DN_HEADS + h]

    blk = lambda w: pl.BlockSpec((bsz, c, w), lambda n: (0, n, 0))
    full = lambda shp: pl.BlockSpec(shp, lambda n: (0,) * len(shp))
    return pl.pallas_call(
        body, name="dn_fwd", grid=(nc,),
        in_specs=[blk(3 * DN_WIDTH), blk(DN_WIDTH), blk(128), full((8, 3 * DN_WIDTH)), full((1, 128)), full((1, 128)), full((1, 128))],
        out_specs=[blk(DN_WIDTH), pl.BlockSpec((1, nh, DN_DIM, DN_DIM), lambda n: (n, 0, 0, 0)),
                   pl.BlockSpec((1, nh, c, c), lambda n: (n, 0, 0, 0))],
        out_shape=[SDS((bsz, t, DN_WIDTH), f32), SDS((nc, nh, DN_DIM, DN_DIM), f32), SDS((nc, nh, c, c), f32)],
        scratch_shapes=[pltpu.VMEM((bsz, c + 8, 3 * DN_WIDTH), f32), pltpu.VMEM((nh, DN_DIM, DN_DIM), f32)],
        compiler_params=_cp(("arbitrary",)),
    )(qkv, z, ab, conv_w, alog, dtb, gain)


def dn_bwd(qkv, z, ab, conv_w, alog, dtb, gain, sall, tall, do):
    bsz, t, _ = qkv.shape
    nc = t // DN_CHUNK
    c = DN_CHUNK
    nh = bsz * DN_HEADS
    w3 = 3 * DN_WIDTH

    def body(qkv_ref, prev_ref, z_ref, ab_ref, w_ref, al_ref, dt_ref, g_ref, sall_ref, tall_ref, do_ref,
             dqkv_ref, dz_ref, dab_ref, dw_ref, dal_ref, ddt_ref, dg_ref, xe, dye, dc_sc, ds_sc):
        n = pl.program_id(0)
        first = (nc - 1 - n) == 0

        @pl.when(n == 0)
        def _():
            dye[:, c:c + 8, :] = jnp.zeros((bsz, 8, w3), f32)
            ds_sc[...] = jnp.zeros_like(ds_sc)
            dw_ref[...] = jnp.zeros_like(dw_ref)
            dal_ref[...] = jnp.zeros_like(dal_ref)
            ddt_ref[...] = jnp.zeros_like(ddt_ref)
            dg_ref[...] = jnp.zeros_like(dg_ref)

        lt, ltt = _dn_consts()
        lane = lax.broadcasted_iota(jnp.int32, (1, 128), 1)
        lane_c = lax.broadcasted_iota(jnp.int32, (c, 128), 1)
        ys, sigs = [], []
        for b in range(bsz):
            xe[b, 0:8, :] = jnp.where(first, 0.0, prev_ref[b])
            xe[b, 8:8 + c, :] = qkv_ref[b]
            ys.append(_conv_rows(xe, b, w_ref))
            sigs.append(_sigmoid(ys[b]))
        gates = [_dn_gates(ab_ref[b], al_ref[...], dt_ref[...]) for b in range(bsz)]
        ops = _dn_chains([y * sg for y, sg in zip(ys, sigs)], gates, z_ref)
        tt = tall_ref[0]
        _, vjp = jax.vjp(lambda *p: dn_chunk(*p, lt, ltt, t_given=tt)[0:2], *ops, sall_ref[0], g_ref[...])
        don = jnp.stack([do_ref[b, :, h * DN_DIM:(h + 1) * DN_DIM] for b in range(bsz) for h in range(DN_HEADS)])
        dcq, dck, dcv, dg, dbeta, dzz, dsp, dgn = vjp((don, ds_sc[...]))
        ds_sc[...] = dsp
        dg_ref[...] += dgn
        for b in range(bsz):
            dgate = jnp.zeros((c, 128), f32)
            for h in range(DN_HEADS):
                i = b * DN_HEADS + h
                dc_sc[b, :, h * DN_DIM:(h + 1) * DN_DIM] = dcq[i]
                dc_sc[b, :, DN_WIDTH + h * DN_DIM:DN_WIDTH + (h + 1) * DN_DIM] = dck[i]
                dc_sc[b, :, 2 * DN_WIDTH + h * DN_DIM:2 * DN_WIDTH + (h + 1) * DN_DIM] = dcv[i]
                dz_ref[b, :, h * DN_DIM:(h + 1) * DN_DIM] = dzz[i].astype(bf16)
                dgate = dgate + jnp.where(lane_c == h, dg[i], 0.0) + jnp.where(lane_c == DN_HEADS + h, dbeta[i], 0.0)
            gg, beta, sig_pre = gates[b]
            is_g = lane_c < DN_HEADS
            dpre = jnp.where(is_g, dgate * (-jnp.exp(al_ref[...])) * sig_pre, 0.0)
            dab_ref[b] = (dpre + jnp.where(is_g, 0.0, dgate * beta * (1.0 - beta))).astype(bf16)
            dal_ref[...] += jnp.sum(jnp.where(is_g, dgate * gg, 0.0), axis=0, keepdims=True)
            ddt_ref[...] += jnp.sum(dpre, axis=0, keepdims=True)
            y, sig = ys[b], sigs[b]
            dy = dc_sc[b] * (sig * (1.0 + y * (1.0 - sig)))
            dye[b, 0:c, :] = dy
            dx = w_ref[3:4, :] * dy
            for i in range(DN_CONV - 1):
                dx = dx + w_ref[i:i + 1, :] * dye[b, pl.ds(3 - i, c), :]
            dqkv_ref[b] = dx.astype(bf16)
            for i in range(DN_CONV):
                dw_ref[i:i + 1, :] += jnp.sum(dy * xe[b, pl.ds(5 + i, c), :], axis=0, keepdims=True)
            dye[b, c:c + 8, :] = dye[b, 0:8, :]

    rev = lambda w: pl.BlockSpec((bsz, c, w), lambda n: (0, nc - 1 - n, 0))
    full = lambda shp: pl.BlockSpec(shp, lambda n: (0,) * len(shp))
    prev = pl.BlockSpec((bsz, 8, w3), lambda n: (0, jnp.maximum((nc - 1 - n) * (c // 8) - 1, 0), 0))
    return pl.pallas_call(
        body, name="dn_bwd", grid=(nc,),
        in_specs=[rev(w3), prev, rev(DN_WIDTH), rev(128), full((8, w3)), full((1, 128)), full((1, 128)), full((1, 128)),
                  pl.BlockSpec((1, nh, DN_DIM, DN_DIM), lambda n: (nc - 1 - n, 0, 0, 0)),
                  pl.BlockSpec((1, nh, c, c), lambda n: (nc - 1 - n, 0, 0, 0)), rev(DN_WIDTH)],
        out_specs=[rev(w3), rev(DN_WIDTH), rev(128), full((8, w3)), full((1, 128)), full((1, 128)), full((1, 128))],
        out_shape=[SDS((bsz, t, w3), bf16), SDS((bsz, t, DN_WIDTH), bf16), SDS((bsz, t, 128), bf16),
                   SDS((8, w3), f32), SDS((1, 128), f32), SDS((1, 128), f32), SDS((1, 128), f32)],
        scratch_shapes=[pltpu.VMEM((bsz, c + 8, w3), f32), pltpu.VMEM((bsz, c + 8, w3), f32), pltpu.VMEM((bsz, c, w3), f32),
                        pltpu.VMEM((nh, DN_DIM, DN_DIM), f32)],
        compiler_params=_cp(("arbitrary",)),
    )(qkv, qkv, z, ab, conv_w, alog, dtb, gain, sall, tall, do)


SB_TILE = 256
SB_PAIRS = SB_HEADS // 2


def sb_fwd(sbqkv, gq, gk):
    bsz, t, _ = sbqkv.shape
    blk = min(SB_TILE, t)
    nq = t // blk
    scale = SB_DIM ** -0.5

    def body(q_ref, k_ref, v_ref, gq_ref, gk_ref, o_ref, l_ref, q2_sc, kn_sc, v_sc):
        bavg = _group_avg_mats()
        lane = lax.broadcasted_iota(jnp.int32, (1, 128), 1)
        first = lane < SB_DIM
        for p in range(SB_PAIRS):
            ls = slice(p * 128, (p + 1) * 128)
            qn = _pair_norm(q_ref[0, :, ls], gq_ref[...], bavg)
            kn_sc[p] = _pair_norm(k_ref[0, :, ls], gk_ref[...], bavg).astype(bf16)
            v_sc[p] = v_ref[0, :, ls].astype(bf16)
            q2_sc[2 * p] = jnp.where(first, qn, 0.0).astype(bf16)
            q2_sc[2 * p + 1] = jnp.where(first, 0.0, qn).astype(bf16)
        r, c = _iota2((blk, blk))
        ustrict = (r > c).astype(bf16)
        r2, c2 = _iota2((2 * blk, blk))
        causal = c2 < (r2 & (blk - 1))

        def tile(q2s, ks, carry, diag):
            out = []
            for p in range(SB_PAIRS):
                acc, rr = carry[2 * p], carry[2 * p + 1]
                zz = lax.dot_general(q2s[p], kn_sc[p, pl.ds(ks, blk), :], NT, preferred_element_type=f32) * scale
                sp = _softplus(zz)
                lm = jnp.where(causal, -sp, 0.0) if diag else -sp
                rem = _dot_x2c(lm, ustrict)
                wgt = jnp.exp(zz - sp + rem + rr)
                if diag:
                    wgt = jnp.where(causal, wgt, 0.0)
                out += [acc + _pdot(wgt.astype(bf16), v_sc[p, pl.ds(ks, blk), :]), rr + jnp.sum(lm, axis=1, keepdims=True)]
            return tuple(out)

        def qloop(qi, _):
            qs = pl.multiple_of(qi * blk, blk)
            q2s = [jnp.concatenate([q2_sc[2 * p, pl.ds(qs, blk), :], q2_sc[2 * p + 1, pl.ds(qs, blk), :]], axis=0)
                   for p in range(SB_PAIRS)]
            zero = (jnp.zeros((2 * blk, 128), f32), jnp.zeros((2 * blk, 1), f32)) * SB_PAIRS
            carry = lax.fori_loop(1, qi + 1, lambda i, cr: tile(q2s, pl.multiple_of((qi - i) * blk, blk), cr, False),
                                  tile(q2s, qs, zero, True))
            for p in range(SB_PAIRS):
                acc, rr = carry[2 * p], carry[2 * p + 1]
                o_ref[0, pl.ds(qs, blk), p * 128:(p + 1) * 128] = jnp.where(first, acc[0:blk], acc[blk:2 * blk])
                l_ref[0, pl.ds(qs, blk), p * 128:(p + 1) * 128] = jnp.where(first, rr[0:blk], rr[blk:2 * blk])
            return 0

        lax.fori_loop(0, nq, qloop, 0)

    col = lambda off: pl.BlockSpec((1, t, SB_WIDTH), lambda b: (b, 0, off))
    gsp = pl.BlockSpec((1, 128), lambda b: (0, 0))
    return pl.pallas_call(
        body, name="sb_fwd", grid=(bsz,),
        in_specs=[col(0), col(1), col(2), gsp, gsp],
        out_specs=[col(0), col(0)],
        out_shape=[SDS((bsz, t, SB_WIDTH), f32), SDS((bsz, t, SB_WIDTH), f32)],
        scratch_shapes=[pltpu.VMEM((2 * SB_PAIRS, t, 128), bf16), pltpu.VMEM((SB_PAIRS, t, 128), bf16),
                        pltpu.VMEM((SB_PAIRS, t, 128), bf16)],
        compiler_params=_cp(("arbitrary",)),
    )(sbqkv, sbqkv, sbqkv, gq, gk)


def sb_bwd(sbqkv, gq, gk, ltot, do):
    bsz, t, _ = sbqkv.shape
    blk = min(SB_TILE, t)
    nq = t // blk
    scale = SB_DIM ** -0.5

    def body(q_ref, k_ref, v_ref, gq_ref, gk_ref, l_ref, do_ref, dq_ref, dk_ref, dv_ref, dgq_ref, dgk_ref,
             q2_sc, kn_sc, v_sc, do2_sc, dqn_sc, dkn_sc, dv_sc):
        bavg = _group_avg_mats()
        lane = lax.broadcasted_iota(jnp.int32, (1, 128), 1)
        first = lane < SB_DIM
        fq = lambda x, g: _pair_norm(x, g, bavg)
        vjps = []
        for p in range(SB_PAIRS):
            ls = slice(p * 128, (p + 1) * 128)
            qn, q_vjp = jax.vjp(fq, q_ref[0, :, ls], gq_ref[...])
            kn, k_vjp = jax.vjp(fq, k_ref[0, :, ls], gk_ref[...])
            vjps.append((q_vjp, k_vjp))
            kn_sc[p] = kn.astype(bf16)
            v_sc[p] = v_ref[0, :, ls].astype(bf16)
            dov = do_ref[0, :, ls]
            q2_sc[2 * p] = jnp.where(first, qn, 0.0).astype(bf16)
            q2_sc[2 * p + 1] = jnp.where(first, 0.0, qn).astype(bf16)
            do2_sc[2 * p] = jnp.where(first, dov, 0.0).astype(bf16)
            do2_sc[2 * p + 1] = jnp.where(first, 0.0, dov).astype(bf16)
        dkn_sc[...] = jnp.zeros_like(dkn_sc)
        dv_sc[...] = jnp.zeros_like(dv_sc)
        r, c = _iota2((blk, blk))
        pincl = (r <= c).astype(bf16)
        pstrict = (r < c).astype(bf16)
        r2, c2 = _iota2((2 * blk, blk))
        causal = c2 < (r2 & (blk - 1))

        def tile(q2s, do2s, lts, ks, carry, diag):
            out = []
            for p in range(SB_PAIRS):
                dq, cs, ce = carry[3 * p:3 * p + 3]
                q2, do2 = q2s[p], do2s[p]
                kb = kn_sc[p, pl.ds(ks, blk), :]
                zz = lax.dot_general(q2, kb, NT, preferred_element_type=f32) * scale
                sp = _softplus(zz)
                lm = jnp.where(causal, -sp, 0.0) if diag else -sp
                pre = _dot_x2c(lm, pincl)
                lp = zz - sp
                wgt = jnp.exp(lp + (lts[p] - cs - pre))
                if diag:
                    wgt = jnp.where(causal, wgt, 0.0)
                dw = lax.dot_general(do2, v_sc[p, pl.ds(ks, blk), :], NT, preferred_element_type=f32)
                e = wgt * dw
                ee = ce + _dot_x2c(e, pstrict)
                sig = jnp.exp(lp)
                dz = (e * (1.0 - sig) - ee * sig) * scale
                if diag:
                    dz = jnp.where(causal, dz, 0.0)
                dz = dz.astype(bf16)
                dkn_sc[p, pl.ds(ks, blk), :] += lax.dot_general(dz, q2, TN, preferred_element_type=f32)
                dv_sc[p, pl.ds(ks, blk), :] += lax.dot_general(wgt.astype(bf16), do2, TN, preferred_element_type=f32)
                out += [dq + _pdot(dz, kb), cs + jnp.sum(lm, axis=1, keepdims=True), ce + jnp.sum(e, axis=1, keepdims=True)]
            return tuple(out)

        def qloop(qi, _):
            qs = pl.multiple_of(qi * blk, blk)
            rows = pl.ds(qs, blk)
            q2s = [jnp.concatenate([q2_sc[2 * p, rows, :], q2_sc[2 * p + 1, rows, :]], axis=0) for p in range(SB_PAIRS)]
            do2s = [jnp.concatenate([do2_sc[2 * p, rows, :], do2_sc[2 * p + 1, rows, :]], axis=0) for p in range(SB_PAIRS)]
            lts = [jnp.concatenate([l_ref[0, rows, p * 128:p * 128 + 1], l_ref[0, rows, p * 128 + SB_DIM:p * 128 + SB_DIM + 1]],
                                   axis=0) for p in range(SB_PAIRS)]
            z1 = jnp.zeros((2 * blk, 1), f32)
            carry = lax.fori_loop(0, qi, lambda kj, cr: tile(q2s, do2s, lts, pl.multiple_of(kj * blk, blk), cr, False),
                                  (jnp.zeros((2 * blk, 128), f32), z1, z1) * SB_PAIRS)
            carry = tile(q2s, do2s, lts, qs, carry, True)
            for p in range(SB_PAIRS):
                dq = carry[3 * p]
                dqn_sc[p, rows, :] = jnp.where(first, dq[0:blk], dq[blk:2 * blk])
            return 0

        lax.fori_loop(0, nq, qloop, 0)
        dgq_tot, dgk_tot = jnp.zeros((1, 128), f32), jnp.zeros((1, 128), f32)
        for p in range(SB_PAIRS):
            ls = slice(p * 128, (p + 1) * 128)
            dq_pre, dgq = vjps[p][0](dqn_sc[p])
            dk_pre, dgk = vjps[p][1](dkn_sc[p])
            dq_ref[0, :, ls] = dq_pre.astype(bf16)
            dk_ref[0, :, ls] = dk_pre.astype(bf16)
            dv_ref[0, :, ls] = dv_sc[p].astype(bf16)
            dgq_tot, dgk_tot = dgq_tot + dgq, dgk_tot + dgk
        dgq_ref[0] = jnp.broadcast_to(dgq_tot, (8, 128))
        dgk_ref[0] = jnp.broadcast_to(dgk_tot, (8, 128))

    col = lambda off: pl.BlockSpec((1, t, SB_WIDTH), lambda b: (b, 0, off), pipeline_mode=pl.Buffered(1))
    gsp = pl.BlockSpec((1, 128), lambda b: (0, 0))
    gout = pl.BlockSpec((1, 8, 128), lambda b: (b, 0, 0))
    return pl.pallas_call(
        body, name="sb_bwd", grid=(bsz,),
        in_specs=[col(0), col(1), col(2), gsp, gsp, col(0), col(0)],
        out_specs=[col(0), col(0), col(0), gout, gout],
        out_shape=[SDS((bsz, t, SB_WIDTH), bf16)] * 3 + [SDS((bsz, 8, 128), f32)] * 2,
        scratch_shapes=[pltpu.VMEM((2 * SB_PAIRS, t, 128), bf16), pltpu.VMEM((SB_PAIRS, t, 128), bf16),
                        pltpu.VMEM((SB_PAIRS, t, 128), bf16), pltpu.VMEM((2 * SB_PAIRS, t, 128), bf16),
                        pltpu.VMEM((SB_PAIRS, t, 128), f32), pltpu.VMEM((SB_PAIRS, t, 128), f32), pltpu.VMEM((SB_PAIRS, t, 128), f32)],
        compiler_params=_cp(("arbitrary",)),
    )(sbqkv, sbqkv, sbqkv, gq, gk, ltot, do)


def sg_pair(u, v, gain, wa, wb, ba, bb, bavg):
    r, c = _iota2((SG_CHUNK, SG_CHUNK))
    lane = lax.broadcasted_iota(jnp.int32, (1, 128), 1)
    first = lane < SG_DIM
    vn = _pair_norm(_gelu(v), gain, bavg)
    tri = c <= r
    mixed = (mm(jnp.where(tri, wa, 0.0), jnp.where(first, vn, 0.0)) + mm(jnp.where(tri, wb, 0.0), jnp.where(first, 0.0, vn))
             + jnp.where(first, ba, bb))
    return _gelu(u) * mixed


def sg_fwd(sguv, gain, w, bt):
    bsz, t, _ = sguv.shape
    nch = t // SG_CHUNK

    def body(uv_ref, g_ref, w_ref, b_ref, o_ref):
        bavg = _group_avg_mats()
        for p in range(2):
            ls = slice(p * 128, (p + 1) * 128)
            o_ref[0, :, ls] = sg_pair(uv_ref[0, :, ls], uv_ref[0, :, SG_WIDTH + p * 128:SG_WIDTH + (p + 1) * 128], g_ref[:, ls],
                                      w_ref[2 * p], w_ref[2 * p + 1], b_ref[:, 2 * p:2 * p + 1], b_ref[:, 2 * p + 1:2 * p + 2], bavg)

    full = lambda shp: pl.BlockSpec(shp, lambda b, n: (0,) * len(shp))
    return pl.pallas_call(
        body, name="sg_fwd", grid=(bsz, nch),
        in_specs=[pl.BlockSpec((1, SG_CHUNK, 2 * SG_WIDTH), lambda b, n: (b, n, 0)), full((1, SG_WIDTH)),
                  full((SG_GROUPS, SG_CHUNK, SG_CHUNK)), full((SG_CHUNK, 128))],
        out_specs=pl.BlockSpec((1, SG_CHUNK, SG_WIDTH), lambda b, n: (b, n, 0)),
        out_shape=SDS((bsz, t, SG_WIDTH), f32),
        compiler_params=_cp(("arbitrary", "arbitrary")),
    )(sguv, gain, w, bt)


def sg_bwd(sguv, gain, w, bt, do):
    bsz, t, _ = sguv.shape
    nch = t // SG_CHUNK

    def body(uv_ref, g_ref, w_ref, b_ref, do_ref, duv_ref, dg_ref, dw_ref, db_ref):
        @pl.when((pl.program_id(0) == 0) & (pl.program_id(1) == 0))
        def _():
            dg_ref[...] = jnp.zeros_like(dg_ref)
            dw_ref[...] = jnp.zeros_like(dw_ref)
            db_ref[...] = jnp.zeros_like(db_ref)

        bavg = _group_avg_mats()
        lane = lax.broadcasted_iota(jnp.int32, (SG_CHUNK, 128), 1)
        dbt = jnp.zeros((SG_CHUNK, 128), f32)
        for p in range(2):
            ls = slice(p * 128, (p + 1) * 128)
            vs = slice(SG_WIDTH + p * 128, SG_WIDTH + (p + 1) * 128)
            prim = (uv_ref[0, :, ls], uv_ref[0, :, vs], g_ref[:, ls], w_ref[2 * p], w_ref[2 * p + 1],
                    b_ref[:, 2 * p:2 * p + 1], b_ref[:, 2 * p + 1:2 * p + 2])
            _, vjp = jax.vjp(lambda *a: sg_pair(*a, bavg), *prim)
            du, dv, dgn, dwa, dwb, dba, dbb = vjp(do_ref[0, :, ls])
            duv_ref[0, :, ls] = du.astype(bf16)
            duv_ref[0, :, vs] = dv.astype(bf16)
            dg_ref[:, ls] += dgn
            dw_ref[2 * p] += dwa
            dw_ref[2 * p + 1] += dwb
            dbt = dbt + jnp.where(lane == 2 * p, dba, 0.0) + jnp.where(lane == 2 * p + 1, dbb, 0.0)
        db_ref[...] += dbt

    full = lambda shp: pl.BlockSpec(shp, lambda b, n: (0,) * len(shp))
    return pl.pallas_call(
        body, name="sg_bwd", grid=(bsz, nch),
        in_specs=[pl.BlockSpec((1, SG_CHUNK, 2 * SG_WIDTH), lambda b, n: (b, n, 0)), full((1, SG_WIDTH)),
                  full((SG_GROUPS, SG_CHUNK, SG_CHUNK)), full((SG_CHUNK, 128)),
                  pl.BlockSpec((1, SG_CHUNK, SG_WIDTH), lambda b, n: (b, n, 0))],
        out_specs=[pl.BlockSpec((1, SG_CHUNK, 2 * SG_WIDTH), lambda b, n: (b, n, 0)), full((1, SG_WIDTH)),
                   full((SG_GROUPS, SG_CHUNK, SG_CHUNK)), full((SG_CHUNK, 128))],
        out_shape=[SDS((bsz, t, 2 * SG_WIDTH), bf16), SDS((1, SG_WIDTH), f32), SDS((SG_GROUPS, SG_CHUNK, SG_CHUNK), f32),
                   SDS((SG_CHUNK, 128), f32)],
        compiler_params=_cp(("arbitrary", "arbitrary")),
    )(sguv, gain, w, bt, do)


def _pad_lanes(v, n=128):
    return jnp.pad(v.reshape(1, -1), ((0, 0), (0, n - v.size)))


def pad_w_in(w):
    return jnp.concatenate([w[:, 0:2048], jnp.pad(w[:, 2048:2056], ((0, 0), (0, 120))), w[:, 2056:]], axis=1)


def unpad_w_in(w):
    return jnp.concatenate([w[:, 0:2048], w[:, C_AB:C_AB + 8], w[:, C_SB:]], axis=1)


def _w_in_runs():
    shard, runs = IN_DIM // N_CHIPS, []
    for s in range(N_CHIPS):
        for a, b, d in ((0, 2048, 0), (2048, 2056, C_AB), (2056, IN_DIM, C_SB)):
            lo, hi = max(shard * s, a), min(shard * (s + 1), b)
            if lo < hi:
                runs.append((s, lo - shard * s, hi - shard * s, d + lo - a))
    return runs


def w_in_from_shards(zone, tr=256):
    def body(z_ref, o_ref):
        o_ref[:, C_AB:C_SB] = jnp.zeros((tr, C_SB - C_AB), zone.dtype)
        for s, a, b, d in _w_in_runs():
            o_ref[:, d:d + b - a] = z_ref[s, :, a:b]

    return pl.pallas_call(
        body, name="w_in_from_shards", grid=(D_MODEL // tr,),
        in_specs=[pl.BlockSpec((N_CHIPS, tr, IN_DIM // N_CHIPS), lambda i: (0, i, 0))],
        out_specs=pl.BlockSpec((tr, IN_PAD), lambda i: (i, 0)), out_shape=SDS((D_MODEL, IN_PAD), zone.dtype),
        compiler_params=_cp(("arbitrary",)))(zone)


def w_in_grad_to_shards(g, tr=256):
    def body(g_ref, o_ref):
        for s, a, b, d in _w_in_runs():
            o_ref[s, :, a:b] = g_ref[:, d:d + b - a]

    return pl.pallas_call(
        body, name="w_in_grad_to_shards", grid=(D_MODEL // tr,),
        in_specs=[pl.BlockSpec((tr, IN_PAD), lambda i: (i, 0))],
        out_specs=pl.BlockSpec((N_CHIPS, tr, IN_DIM // N_CHIPS), lambda i: (0, i, 0)),
        out_shape=SDS((N_CHIPS, D_MODEL, IN_DIM // N_CHIPS), g.dtype), compiler_params=_cp(("arbitrary",)))(g)


def layer_params(p, l):
    return dict(
        g1=p["norm1_g"][l].reshape(1, -1), g2=p["norm2_g"][l].reshape(1, -1),
        conv=jnp.pad(p["conv_w"][l], ((0, 4), (0, 0))), alog=_pad_lanes(p["a_log"][l]), dtb=_pad_lanes(p["dt_bias"][l]),
        dng=p["dn_out_g"][l].reshape(1, -1), gq=jnp.tile(p["sb_q_g"][l].reshape(1, -1), (1, 2)),
        gk=jnp.tile(p["sb_k_g"][l].reshape(1, -1), (1, 2)), sgg=p["sg_v_g"][l].reshape(1, -1), sgw=p["sg_w"][l],
        sgb=jnp.pad(p["sg_b"][l].T, ((0, 0), (0, 124))))


def local_step(x, tgt, small, get_w, put_g):
    bsz, t, _ = x.shape
    m = bsz * t
    r3 = lambda a: a.reshape(bsz, t, a.shape[-1])
    r2 = lambda a: a.reshape(m, a.shape[-1])
    xs, saved, ws = x.reshape(m, D_MODEL), [], []
    for l in range(DEPTH):
        sp, w = layer_params(small, l), {}
        w["w_in"] = get_w(l, "in", xs)
        qkv, z, ab, sb, sg = inproj_fwd(xs, sp["g1"], w["w_in"])
        odn, sall, tall = dn_fwd(r3(qkv), r3(z), r3(ab), sp["conv"], sp["alog"], sp["dtb"], sp["dng"])
        osb, ltot = sb_fwd(r3(sb), sp["gq"], sp["gk"])
        osg = sg_fwd(r3(sg), sp["sgg"], sp["sgw"], sp["sgb"])
        w["w_out"] = get_w(l, "out", osg)
        x2, mix = outproj_fwd(xs, r2(odn), r2(osb), r2(osg), w["w_out"])
        w["w_ff1"], w["w_ff2"] = get_w(l, "ff", x2)
        x3 = ffn_fwd(x2, sp["g2"], w["w_ff1"], w["w_ff2"])
        saved.append(dict(x=xs, qkv=qkv, z=z, ab=ab, sb=sb, sg=sg, sall=sall, tall=tall, ltot=ltot, mix=mix, x2=x2))
        ws.append(w)
        xs = x3
    dx, lossp = loss_head(xs, tgt.reshape(m, D_MODEL))
    gsmall = [None] * DEPTH
    token = jnp.zeros((), f32)
    for l in reversed(range(DEPTH)):
        sp, w, s = layer_params(small, l), ws[l], saved[l]
        dx2, dg2, h2, act, df, dyb = ffn_bwd(s["x2"], sp["g2"] + token, w["w_ff1"], w["w_ff2"], dx)
        g_ff1 = tn_matmul(h2, df, f"dw_ff1_{l}", col_shards=N_CHIPS)
        g_ff2 = tn_matmul(act, dyb, f"dw_ff2_{l}")
        dodn, dosb, dosg, dx2b = outproj_bwd(dx2, w["w_out"])
        g_out = tn_matmul(s["mix"], dx2b, f"dw_out_{l}")
        token = put_g(l, "rest", dict(w_out=g_out, w_ff1=g_ff1, w_ff2=g_ff2))
        dqkv, dz, dab, dconv, dalog, ddtb, ddng = dn_bwd(r3(s["qkv"]), r3(s["z"]), r3(s["ab"]), sp["conv"], sp["alog"], sp["dtb"],
                                                        sp["dng"] + token, s["sall"], s["tall"], r3(dodn))
        dsq, dsk, dsv, dgq, dgk = sb_bwd(r3(s["sb"]), sp["gq"], sp["gk"], s["ltot"], r3(dosb))
        dsg, dsgg, dsgw, dsgb = sg_bwd(r3(s["sg"]), sp["sgg"], sp["sgw"], sp["sgb"], r3(dosg))
        dproj = jnp.concatenate([r2(dqkv), r2(dz), r2(dab), r2(dsq), r2(dsk), r2(dsv), r2(dsg)], axis=1)
        dx, dg1, h1 = inproj_bwd(s["x"], sp["g1"], w["w_in"], dproj, dx2)
        g_in = tn_matmul(h1, dproj, f"dw_in_{l}")
        token = put_g(l, "in", dict(w_in=g_in))
        fold = lambda a: (a[:, 0, :].sum(0).reshape(2, SB_DIM)).sum(0)
        gsmall[l] = dict(norm1_g=dg1[0], conv_w=dconv[0:DN_CONV], a_log=dalog[0, 0:DN_HEADS], dt_bias=ddtb[0, 0:DN_HEADS],
                         dn_out_g=ddng[0], sb_q_g=fold(dgq), sb_k_g=fold(dgk), sg_v_g=dsgg[0], sg_w=dsgw,
                         sg_b=dsgb[:, 0:SG_GROUPS].T, norm2_g=dg2[0])
    return lossp, dx.reshape(bsz, t, D_MODEL), gsmall


def _chip_peers(x, y):
    return [(1 - x, y), (x, 1 - y), (1 - x, 1 - y)]


_HBM = pl.BlockSpec(memory_space=pltpu.HBM)
_SEM = pl.BlockSpec(memory_space=pltpu.SEMAPHORE)
_EFFECT = pltpu.SideEffectType.DATAFLOW_SIDE_EFFECTING


def _hbm(a):
    return pltpu.with_memory_space_constraint(a, pltpu.HBM)


def _my_half(ref):
    half = ref.shape[0] // 2
    return ref.at[pl.ds(pl.multiple_of(lax.axis_index("c") * half, 8), half)]


def _slot(zone, s, cols):
    if not cols:
        return zone.at[s]
    width = zone.shape[1] // N_CHIPS
    return zone.at[:, pl.ds(pl.multiple_of(s * width, 128), width)]


def _exchange_copy(src, land, k, j, send, recv, scatter, halve, waiting):
    x, y, c = lax.axis_index("x"), lax.axis_index("y"), lax.axis_index("c")
    px, py = _chip_peers(x, y)[j]
    me, peer = 2 * x + y, 2 * px + py
    if scatter:
        src = src.at[me if waiting else peer]
    dst = _slot(land, peer if waiting else me, halve == "cols")
    if halve:
        src, dst = _my_half(src), _my_half(dst)
    return pltpu.make_async_remote_copy(src_ref=src, dst_ref=dst, send_sem=send.at[3 * k + j],
                                        recv_sem=recv.at[3 * k + j], device_id=(px, py, c), device_id_type=MESH)


def exchange_start(items, name, scatter):
    arrs = []
    for a, _, _ in items:
        if not any(a is b for b in arrs):
            arrs.append(a)
    pos = [next(i for i, b in enumerate(arrs) if b is a) for a, _, _ in items]
    shapes = [a.shape if idx is None else a.shape[1:] for a, idx, _ in items]
    lands = [lax.empty(s if scatter else ((s[0], N_CHIPS * s[1]) if h == "cols" else (N_CHIPS,) + s), a.dtype)
             for (a, _, h), s in zip(items, shapes)]
    na, nl = len(arrs), len(lands)

    def body(*refs):
        ins, lnd = refs[:na], refs[na:na + nl]
        send, recv = refs[na + nl], refs[na + nl + 1]
        token = refs[-1]
        for k, (_, idx, halve) in enumerate(items):
            src = ins[pos[k]] if idx is None else ins[pos[k]].at[idx]
            for j in range(3):
                _exchange_copy(src, lnd[k], k, j, send, recv, scatter, halve, False).start()
        token[...] = jnp.zeros_like(token)

    sems = pltpu.SemaphoreType.DMA((3 * nl,))
    out = pl.pallas_call(
        body, name=name,
        out_shape=(sems, sems, *[pltpu.HBM(a.shape, a.dtype) for a in arrs + lands], SDS((8, 128), f32)),
        in_specs=[_HBM] * (na + nl), out_specs=(_SEM, _SEM, *[_HBM] * (na + nl), pl.BlockSpec(memory_space=pltpu.VMEM)),
        input_output_aliases={i: 2 + i for i in range(na + nl)},
        compiler_params=pltpu.CompilerParams(has_side_effects=_EFFECT),
    )(*[_hbm(a) for a in arrs + lands])
    thru = out[2:2 + na]
    return dict(send=out[0], recv=out[1], src=[(thru[pos[k]], idx) for k, (_, idx, _) in enumerate(items)],
                halve=[h for _, _, h in items], land=list(out[2 + na:2 + na + nl]), token=out[-1], scatter=scatter)


def exchange_wait(st, ks, after, name):
    arrs = []
    for k in ks:
        if not any(st["src"][k][0] is b for b in arrs):
            arrs.append(st["src"][k][0])
    pos = [next(i for i, b in enumerate(arrs) if b is st["src"][k][0]) for k in ks]
    lands = [st["land"][k] for k in ks]
    na, nl = len(arrs), len(lands)

    def body(*refs):
        ins, lnd = refs[:na], refs[na:na + nl]
        send, recv = refs[na + nl], refs[na + nl + 1]
        for t, k in enumerate(ks):
            idx = st["src"][k][1]
            src = ins[pos[t]] if idx is None else ins[pos[t]].at[idx]
            for j in range(3):
                cp = _exchange_copy(src, lnd[t], k, j, send, recv, st["scatter"], st["halve"][k], True)
                cp.wait_send()
                cp.wait_recv()

    out = pl.pallas_call(
        body, name=name, out_shape=tuple(pltpu.HBM(a.shape, a.dtype) for a in arrs + lands),
        in_specs=[_HBM] * (na + nl) + [_SEM, _SEM, pl.BlockSpec(memory_space=pl.ANY)], out_specs=tuple([_HBM] * (na + nl)),
        input_output_aliases={i: i for i in range(na + nl)},
        compiler_params=pltpu.CompilerParams(has_side_effects=_EFFECT),
    )(*arrs, *lands, st["send"], st["recv"], after)
    for k, (a, idx) in enumerate(st["src"]):
        for p, b in enumerate(arrs):
            if a is b:
                st["src"][k] = (out[p], idx)
    return list(out[na:na + nl])


def swap_cores(arrs, name):
    n = len(arrs)

    def body(*refs):
        ins, outs = refs[:n], refs[n:2 * n]
        send, recv = refs[2 * n:]
        sib = (lax.axis_index("x"), lax.axis_index("y"), 1 - lax.axis_index("c"))
        cps = [pltpu.make_async_remote_copy(src_ref=ins[i], dst_ref=outs[i], send_sem=send.at[i], recv_sem=recv.at[i],
                                            device_id=sib, device_id_type=MESH) for i in range(n)]
        for cp in cps:
            cp.start()
        for cp in cps:
            cp.wait()

    any_spec = pl.BlockSpec(memory_space=pl.ANY)
    return pl.pallas_call(
        body, name=name, in_specs=[any_spec] * n, out_specs=[any_spec] * n, out_shape=[SDS(a.shape, a.dtype) for a in arrs],
        scratch_shapes=[pltpu.SemaphoreType.DMA((n,)), pltpu.SemaphoreType.DMA((n,))],
    )(*arrs)


def swap_halves(zones, owns, cols, name):
    n = len(zones)

    def body(*refs):
        own_refs, outs = refs[n:2 * n], refs[2 * n:3 * n]
        send, recv, lsem = refs[3 * n:]
        x, y, c = lax.axis_index("x"), lax.axis_index("y"), lax.axis_index("c")
        cps, local = [], []
        for i in range(n):
            src = own_refs[i] if owns[i][1] is None else own_refs[i].at[owns[i][1]]
            local.append(pltpu.make_async_copy(src, _slot(outs[i], 2 * x + y, cols[i]), lsem.at[i]))
            for j, (px, py) in enumerate(_chip_peers(x, y)):
                part = _my_half(_slot(outs[i], 2 * px + py, cols[i]))
                cps.append(pltpu.make_async_remote_copy(src_ref=part, dst_ref=part, send_sem=send.at[3 * i + j],
                                                        recv_sem=recv.at[3 * i + j], device_id=(x, y, 1 - c), device_id_type=MESH))
        for cp in cps + local:
            cp.start()
        for cp in cps:
            cp.wait_send()
            cp.wait_recv()
        for cp in local:
            cp.wait()

    any_spec = pl.BlockSpec(memory_space=pl.ANY)
    return pl.pallas_call(
        body, name=name, in_specs=[any_spec] * (2 * n), out_specs=[any_spec] * n, out_shape=[SDS(a.shape, a.dtype) for a in zones],
        input_output_aliases={i: i for i in range(n)},
        scratch_shapes=[pltpu.SemaphoreType.DMA((3 * n,)), pltpu.SemaphoreType.DMA((3 * n,)), pltpu.SemaphoreType.DMA((n,))],
    )(*zones, *[o[0] for o in owns])


def swap_other_halves(arrs, name):
    n = len(arrs)

    def body(*refs):
        ins, outs = refs[:n], refs[n:2 * n]
        send, recv = refs[2 * n:]
        x, y, c = lax.axis_index("x"), lax.axis_index("y"), lax.axis_index("c")
        cps = [pltpu.make_async_remote_copy(src_ref=ins[i].at[:, 1 - c], dst_ref=outs[i], send_sem=send.at[i], recv_sem=recv.at[i],
                                            device_id=(x, y, 1 - c), device_id_type=MESH) for i in range(n)]
        for cp in cps:
            cp.start()
        for cp in cps:
            cp.wait()

    any_spec = pl.BlockSpec(memory_space=pl.ANY)
    return pl.pallas_call(
        body, name=name, in_specs=[any_spec] * n, out_specs=[any_spec] * n,
        out_shape=[SDS((a.shape[0],) + a.shape[2:], a.dtype) for a in arrs],
        scratch_shapes=[pltpu.SemaphoreType.DMA((n,)), pltpu.SemaphoreType.DMA((n,))],
    )(*arrs)


def _ids_spec(grid, in_specs, out_specs):
    return pltpu.PrefetchScalarGridSpec(num_scalar_prefetch=1, grid=grid, in_specs=in_specs, out_specs=out_specs)


def pair_sum(ids, a, b, name, tr=512):
    nd, _, rows, cols = a.shape
    tr = min(tr, rows)
    assert rows % tr == 0

    def body(ids_ref, a_ref, b_ref, o_ref):
        o_ref[...] = (a_ref[0].astype(f32) + b_ref[...].astype(f32)).astype(bf16)

    spec = pl.BlockSpec((1, tr, cols), lambda d, i, ids: (d, i, 0))
    return pl.pallas_call(
        body, name=name,
        grid_spec=_ids_spec((nd, rows // tr), [pl.BlockSpec((1, 1, tr, cols), lambda d, i, ids: (d, ids[1], i, 0)), spec], spec),
        out_shape=SDS((nd, rows, cols), bf16), compiler_params=_cp(("arbitrary", "arbitrary")))(ids, a, b)


def allreduce_small(v):
    def body(v_ref, o_ref, rbuf, send, recv):
        x, y, c = lax.axis_index("x"), lax.axis_index("y"), lax.axis_index("c")
        o_ref[...] = v_ref[...]
        for s, peer in enumerate([(x, y, 1 - c), (1 - x, y, c), (x, 1 - y, c)]):
            cp = pltpu.make_async_remote_copy(src_ref=o_ref, dst_ref=rbuf.at[s], send_sem=send.at[s], recv_sem=recv.at[s],
                                              device_id=peer, device_id_type=MESH)
            cp.start()
            cp.wait()
            o_ref[...] = o_ref[...] + rbuf[s]

    vm = pl.BlockSpec(memory_space=pltpu.VMEM)
    return pl.pallas_call(
        body, name="allreduce_small", in_specs=[vm], out_specs=vm, out_shape=SDS(v.shape, f32),
        scratch_shapes=[pltpu.VMEM((3,) + v.shape, f32), pltpu.SemaphoreType.DMA((3,)), pltpu.SemaphoreType.DMA((3,))],
        compiler_params=_cp(),
    )(v)


def sum_partials(ids, zone, mine, name, tr=256):
    _, rows, cols = zone.shape
    tr = min(tr, rows)
    assert rows % tr == 0

    def body(ids_ref, m_ref, z1_ref, z2_ref, z3_ref, o_ref):
        o_ref[...] = ((m_ref[0].astype(f32) + z1_ref[0].astype(f32)) + z2_ref[0].astype(f32)) + z3_ref[0].astype(f32)

    slot = lambda flip: pl.BlockSpec((1, tr, cols), lambda i, ids: (ids[0] ^ flip, i, 0))
    return pl.pallas_call(
        body, name=name,
        grid_spec=_ids_spec((rows // tr,), [slot(0), slot(1), slot(2), slot(3)], pl.BlockSpec((tr, cols), lambda i, ids: (i, 0))),
        out_shape=SDS((rows, cols), f32), compiler_params=_cp(("arbitrary",)),
    )(ids, mine, zone, zone, zone)


def adamw(w, m, v, gs, name, layer=0, prev=None, tr=256):
    hrows, cols = gs[0].shape
    rows = hrows * len(gs)
    tr = min(tr, hrows)
    assert hrows % tr == 0 and w.shape[0] % rows == 0
    off, nth = layer * (rows // tr), hrows // tr

    def body(w_ref, m_ref, v_ref, *rest):
        g_ref, d_ref, mo_ref, vo_ref = rest[-4:]
        if len(gs) == 1:
            g = rest[0][...]
        else:
            g = jnp.where(pl.program_id(0) // nth == lax.axis_index("c"), rest[0][...], rest[1][...])
        mn = ADAM_B1 * m_ref[...] + (1.0 - ADAM_B1) * g
        vn = ADAM_B2 * v_ref[...] + (1.0 - ADAM_B2) * jnp.square(g)
        m_hat = mn / (1.0 - ADAM_B1 ** ADAM_STEP)
        v_hat = vn / (1.0 - ADAM_B2 ** ADAM_STEP)
        g_ref[...] = g
        d_ref[...] = -ADAM_LR * (m_hat / (jnp.sqrt(v_hat) + ADAM_EPS) + ADAM_WD * w_ref[...])
        mo_ref[...] = mn
        vo_ref[...] = vn

    loc = pl.BlockSpec((tr, cols), lambda i: (i % nth, 0))
    glob = pl.BlockSpec((tr, cols), lambda i: (off + i, 0))
    extra = [] if prev is None else list(prev)
    return pl.pallas_call(
        body, name=name, grid=(rows // tr,),
        in_specs=[glob] * 3 + [loc] * len(gs) + [pl.BlockSpec(memory_space=pl.ANY)] * len(extra),
        out_specs=[glob] * 4, out_shape=[SDS(w.shape, f32)] * 4,
        input_output_aliases={3 + len(gs) + j: j for j in range(len(extra))},
        compiler_params=_cp(("arbitrary",)),
    )(w, m, v, *gs, *extra)


BIG = ("w_in", "w_out", "w_ff1", "w_ff2")
SMALL = ("norm1_g", "conv_w", "a_log", "dt_bias", "dn_out_g", "sb_q_g", "sb_k_g", "sg_v_g", "sg_w", "sg_b", "norm2_g")
WEIGHTS = ("norm1_g", "w_in", "conv_w", "a_log", "dt_bias", "dn_out_g", "sb_q_g", "sb_k_g", "sg_v_g", "sg_w", "sg_b",
           "w_out", "norm2_g", "w_ff1", "w_ff2")


PACK_ROWS = 256


def _rows_of(shape):
    n = 1
    for d in shape:
        n *= d
    return -(-n // 1024) * 8, n


def _pack(arrs):
    parts = []
    for a in arrs:
        r, n = _rows_of(a.shape)
        parts.append(jnp.pad(a.reshape(-1), (0, r * 128 - n)).reshape(r, 128))
    rows = sum(p.shape[0] for p in parts)
    parts.append(jnp.zeros((-rows % PACK_ROWS, 128), arrs[0].dtype))
    return jnp.concatenate(parts, axis=0)


def _unpack(packed, shapes):
    out, o = [], 0
    for s in shapes:
        r, n = _rows_of(s)
        out.append(packed[o:o + r].reshape(-1)[0:n].reshape(s))
        o += r
    return out


def kernel(x, norm1_g, w_in, conv_w, a_log, dt_bias, dn_out_g, sb_q_g, sb_k_g, sg_v_g, sg_w, sg_b, w_out, norm2_g, w_ff1, w_ff2, loss_target, m_norm1_g, m_w_in, m_conv_w, m_a_log, m_dt_bias, m_dn_out_g, m_sb_q_g, m_sb_k_g, m_sg_v_g, m_sg_w, m_sg_b, m_w_out, m_norm2_g, m_w_ff1, m_w_ff2, v_norm1_g, v_w_in, v_conv_w, v_a_log, v_dt_bias, v_dn_out_g, v_sb_q_g, v_sb_k_g, v_sg_v_g, v_sg_w, v_sg_b, v_w_out, v_norm2_g, v_w_ff1, v_w_ff2):
    w = dict(norm1_g=norm1_g, w_in=w_in, conv_w=conv_w, a_log=a_log, dt_bias=dt_bias, dn_out_g=dn_out_g, sb_q_g=sb_q_g,
             sb_k_g=sb_k_g, sg_v_g=sg_v_g, sg_w=sg_w, sg_b=sg_b, w_out=w_out, norm2_g=norm2_g, w_ff1=w_ff1, w_ff2=w_ff2)
    mom = dict(norm1_g=m_norm1_g, w_in=m_w_in, conv_w=m_conv_w, a_log=m_a_log, dt_bias=m_dt_bias, dn_out_g=m_dn_out_g,
               sb_q_g=m_sb_q_g, sb_k_g=m_sb_k_g, sg_v_g=m_sg_v_g, sg_w=m_sg_w, sg_b=m_sg_b, w_out=m_w_out, norm2_g=m_norm2_g,
               w_ff1=m_w_ff1, w_ff2=m_w_ff2)
    var = dict(norm1_g=v_norm1_g, w_in=v_w_in, conv_w=v_conv_w, a_log=v_a_log, dt_bias=v_dt_bias, dn_out_g=v_dn_out_g,
               sb_q_g=v_sb_q_g, sb_k_g=v_sb_k_g, sg_v_g=v_sg_v_g, sg_w=v_sg_w, sg_b=v_sg_b, w_out=v_w_out, norm2_g=v_norm2_g,
               w_ff1=v_w_ff1, w_ff2=v_w_ff2)
    chip = 2 * lax.axis_index("x") + lax.axis_index("y")

    wb = {k: w[k].astype(bf16) for k in BIG}
    ag = exchange_start([(conv_w, None, None)] + [(wb[k], l, "cols" if k == "w_ff1" else "rows") for l in range(DEPTH) for k in BIG],
                        "allgather_start", scatter=False)
    item = lambda l, k: 1 + l * len(BIG) + BIG.index(k)

    def landed(ks, after, name):
        zones = exchange_wait(ag, ks, after, name)
        halved = [t for t, k in enumerate(ks) if ag["halve"][k]]
        passed = swap_halves([zones[t] for t in halved], [ag["src"][ks[t]] for t in halved],
                             [ag["halve"][ks[t]] == "cols" for t in halved], name.replace("wait", "pass"))
        for t, z in zip(halved, passed):
            zones[t] = z
        for t, k in enumerate(ks):
            if not ag["halve"][k]:
                zones[t] = lax.dynamic_update_slice_in_dim(zones[t], ag["src"][k][0][None], chip, axis=0)
        return zones

    def whole(k, z):
        return w_in_from_shards(z) if k == "w_in" else z.reshape(-1, z.shape[-1])

    g_conv, first_in = landed([0, item(0, "w_in")], x, "allgather_wait_in0")
    small = {k: w[k] for k in SMALL}
    small["conv_w"] = jnp.transpose(g_conv, (1, 2, 0, 3)).reshape(DEPTH, DN_CONV, 3 * DN_WIDTH)
    cache = {}

    def get_w(l, part, after):
        if part == "in":
            return whole("w_in", first_in if l == 0 else landed([item(l, "w_in")], after, f"allgather_wait_in{l}")[0])
        if part == "out":
            zs = landed([item(l, k) for k in ("w_out", "w_ff1", "w_ff2")], after, f"allgather_wait_rest{l}")
            cache[l] = (whole("w_ff1", zs[1]), whole("w_ff2", zs[2]))
            return whole("w_out", zs[0])
        return cache[l]

    rs = {}
    ids = jnp.stack([chip, lax.axis_index("c")]).astype(jnp.int32)

    def put_g(l, tag, g):
        names = [k for k in BIG if k in g]
        by_dest = [w_in_grad_to_shards(g[k]) if k == "w_in" else g[k] for k in names]
        halves = [a.reshape(N_CHIPS, 2, -1, a.shape[-1]) for a in by_dest]
        got = swap_other_halves(halves, f"pair_swap_{tag}{l}")
        pair = [pair_sum(ids, a, b, f"pair_sum_{k}_{l}") for k, a, b in zip(names, halves, got)]
        rs[l, tag] = dict(exchange_start([(a, None, None) for a in pair], f"scatter_start_{tag}{l}", scatter=True), names=names)
        return rs[l, tag]["token"][0, 0]

    lossp, grad_x, gsmall = local_step(x, loss_target, small, get_w, put_g)
    loss = lax.psum(jnp.sum(lossp), ("x", "y", "c"))

    def finish(l, tag, after, prev):
        st = rs[l, tag]
        ks = list(range(len(st["names"])))
        zones = exchange_wait(st, ks, after, f"scatter_wait_{tag}{l}")
        sums = [sum_partials(ids, zones[i], st["src"][i][0], f"sum_{k}_{l}") for i, k in enumerate(st["names"])]
        others = swap_cores(sums, f"swap_grad_sums_{tag}{l}")
        outs = dict(prev)
        for i, k in enumerate(st["names"]):
            r2 = lambda a: a.reshape(-1, a.shape[-1])
            outs[k] = adamw(r2(w[k]), r2(mom[k]), r2(var[k]), (sums[i], others[i]), f"adamw_{k}_{l}", layer=l, prev=prev.get(k))
        return outs

    done = finish(1, "rest", rs[0, "in"]["token"], {})
    done = finish(1, "in", done["w_ff2"][0], done)
    res = {}

    full_shapes = [(DEPTH,) + tuple(gsmall[0][k].shape) for k in SMALL]
    packed = _pack([jnp.stack([gsmall[l][k] for l in range(DEPTH)]) for k in SMALL])
    total = allreduce_small(packed)
    gfull = dict(zip(SMALL, _unpack(total, full_shapes)))
    cs = 3 * DN_WIDTH // N_CHIPS
    gfull["conv_w"] = lax.dynamic_slice_in_dim(gfull["conv_w"], chip * cs, cs, axis=2)
    gp, wp, mp, vp = (_pack([d[k] for k in SMALL]) for d in (gfull, w, mom, var))
    outs = adamw(wp, mp, vp, (gp,), "adamw_small")
    loc_shapes = [w[k].shape for k in SMALL]
    unp = [_unpack(o, loc_shapes) for o in outs]
    for i, k in enumerate(SMALL):
        res[k] = [unp[j][i] for j in range(4)]

    done = finish(0, "rest", outs[0], done)
    done = finish(0, "in", done["w_ff2"][0], done)
    for k in BIG:
        res[k] = [o.reshape(w[k].shape) for o in done[k]]

    return (loss, grad_x, *[res[k][0] for k in WEIGHTS], *[res[k][1] for k in WEIGHTS], *[res[k][2] for k in WEIGHTS],
            *[res[k][3] for k in WEIGHTS])
```

```python
import functools

import jax
import jax.numpy as jnp
from jax import lax
from jax.experimental import pallas as pl
from jax.experimental.pallas import tpu as pltpu

f32 = jnp.float32
bf16 = jnp.bfloat16
SDS = jax.ShapeDtypeStruct
MESH = pl.DeviceIdType.MESH

NORM_EPS = 1e-6
D_MODEL = 1024
DEPTH = 2
DN_HEADS, DN_DIM, DN_WIDTH, DN_CONV, DN_CHUNK = 4, 128, 512, 4, 64
SB_HEADS, SB_DIM, SB_WIDTH, SB_BLOCK = 4, 64, 256, 128
SG_GROUPS, SG_DIM, SG_WIDTH, SG_CHUNK = 4, 64, 256, 128
D_FF = 4096
IN_DIM = 3336
C_QKV, C_Z, C_AB, C_SB, C_SG, IN_PAD = 0, 1536, 2048, 2304, 3072, 3584
DN_COLS = C_SB
N_CHIPS = 4

ADAM_LR, ADAM_B1, ADAM_B2, ADAM_EPS, ADAM_WD, ADAM_STEP = 0.001, 0.9, 0.999, 1e-08, 0.01, 10

VMEM_LIMIT = 56 * 1024 * 1024


def _cp(sem=None, **kw):
    if sem is not None:
        kw["dimension_semantics"] = sem
    return pltpu.CompilerParams(vmem_limit_bytes=VMEM_LIMIT, **kw)


def _split2(x):
    hi = x.astype(bf16)
    lo = (x - hi.astype(f32)).astype(bf16)
    return hi, lo


NT = (((1,), (1,)), ((), ()))
TN = (((0,), (0,)), ((), ()))
_DIMS2 = dict(nn=(((1,), (0,)), ((), ())), nt=NT, tn=TN)
_DIMS3 = dict(nn=(((2,), (1,)), ((0,), (0,))), nt=(((2,), (2,)), ((0,), (0,))), tn=(((1,), (1,)), ((0,), (0,))))


def _dg(a, b, kind):
    return lax.dot_general(a, b, (_DIMS2 if a.ndim == 2 else _DIMS3)[kind], preferred_element_type=f32)


def _pdot(a, b):
    return _dg(a, b, "nn")


def _dot_hp(a, b):
    ah, al = _split2(a)
    bh, bl = _split2(b)
    return _pdot(ah, bh) + _pdot(ah, bl) + _pdot(al, bh)


def _dot_x2c(a, m):
    lead = a.shape[:-1]
    ah, al = _split2(a.reshape(-1, a.shape[-1]))
    return (_pdot(ah, m) + _pdot(al, m)).reshape(lead + (m.shape[1],))


def _dot_cx2(m, a):
    if a.ndim == 3:
        m = jnp.broadcast_to(m, (a.shape[0],) + m.shape)
    ah, al = _split2(a)
    return _pdot(m, ah) + _pdot(m, al)


def _nt(a, b):
    return _dg(a.astype(bf16), b.astype(bf16), "nt")


def _tn(a, b):
    return _dg(a.astype(bf16), b.astype(bf16), "tn")


def _nn(a, b):
    return _dg(a.astype(bf16), b.astype(bf16), "nn")


@jax.custom_vjp
def mm(a, b):
    return _nn(a, b)


mm.defvjp(lambda a, b: (_nn(a, b), (a, b)), lambda r, g: (_nt(g, r[1]), _tn(r[0], g)))


@jax.custom_vjp
def mm_nt(a, b):
    return _nt(a, b)


mm_nt.defvjp(lambda a, b: (_nt(a, b), (a, b)), lambda r, g: (_nn(g, r[1]), _tn(g, r[0])))


@jax.custom_vjp
def mm_tn(a, b):
    return _tn(a, b)


mm_tn.defvjp(lambda a, b: (_tn(a, b), (a, b)), lambda r, g: (_nt(r[1], g), _nn(r[0], g)))


@jax.custom_vjp
def rmul_const(a, m, mt):
    return _dot_x2c(a, m)


rmul_const.defvjp(lambda a, m, mt: (_dot_x2c(a, m), (m, mt)),
                  lambda r, g: (_dot_x2c(g, r[1]), jnp.zeros_like(r[0]), jnp.zeros_like(r[1])))


@jax.custom_vjp
def lmul_const(m, mt, a):
    return _dot_cx2(m, a)


lmul_const.defvjp(lambda m, mt, a: (_dot_cx2(m, a), (m, mt)),
                  lambda r, g: (jnp.zeros_like(r[0]), jnp.zeros_like(r[1]), _dot_cx2(r[1], g)))


@jax.custom_vjp
def mm_hl(t, x):
    th, tl = _split2(t)
    xb = x.astype(bf16)
    return _pdot(th, xb) + _pdot(tl, xb)


def _mm_hl_bwd(r, g):
    t, x = r
    th, tl = _split2(t)
    gb = g.astype(bf16)
    return _nt(g, x), _dg(th, gb, "tn") + _dg(tl, gb, "tn")


mm_hl.defvjp(lambda t, x: (mm_hl(t, x), (t, x)), _mm_hl_bwd)


def inv_unit_lower(lm):
    c = lm.shape[-1]
    r, cc = _iota2((c, c))
    eye = (r == cc).astype(f32)
    t = eye - lm
    p = -lm
    k = 1
    while 2 * k < c:
        p = _nn(p, p)
        t = t + _nn(t, p)
        k *= 2
    res = eye - t - _dot_hp(lm, t)
    return t + _nn(t, res)


@jax.custom_vjp
def inv_given(lm, t):
    return t


inv_given.defvjp(lambda lm, t: (t, t), lambda t, g: (-_nt(_tn(t, g), t), jnp.zeros_like(t)))


def _sigmoid(x):
    return 1.0 / (1.0 + jnp.exp(-x))


def _softplus(x):
    return jnp.maximum(x, 0.0) + jnp.log(1.0 + jnp.exp(-jnp.abs(x)))


def _silu(x):
    return x * _sigmoid(x)


def _gelu(x):
    return 0.5 * x * (1.0 + jnp.tanh(0.7978845608028654 * (x + 0.044715 * (x * x * x))))


def _iota2(shape):
    return lax.broadcasted_iota(jnp.int32, shape, 0), lax.broadcasted_iota(jnp.int32, shape, 1)


def _group_avg_mats():
    r, c = _iota2((128, 128))
    return jnp.where((r // 64) == (c // 64), 1.0 / 64.0, 0.0).astype(bf16)


def _pair_norm(x, gain, bavg):
    ms = rmul_const(x * x, bavg, bavg)
    return x * lax.rsqrt(ms + NORM_EPS) * gain


def _rms(x):
    r = lax.rsqrt(jnp.mean(x * x, axis=-1, keepdims=True) + NORM_EPS)
    return r


_IN_GROUPS = ((C_QKV, C_Z), (C_Z, C_AB), (C_AB, C_AB + 128), (C_SB, C_SG), (C_SG, IN_PAD))


def inproj_fwd(x, g, wp, tm=256):
    m = x.shape[0]

    def body(x_ref, g_ref, w_ref, *outs):
        xv = x_ref[...]
        h = (xv * _rms(xv) * g_ref[...]).astype(bf16)
        for (a, b), o in zip(_IN_GROUPS, outs):
            o[...] = _pdot(h, w_ref[:, a:b])

    return pl.pallas_call(
        body, name="inproj_fwd", grid=(m // tm,),
        in_specs=[pl.BlockSpec((tm, D_MODEL), lambda i: (i, 0)), pl.BlockSpec((1, D_MODEL), lambda i: (0, 0)),
                  pl.BlockSpec((D_MODEL, IN_PAD), lambda i: (0, 0))],
        out_specs=[pl.BlockSpec((tm, b - a), lambda i: (i, 0)) for a, b in _IN_GROUPS],
        out_shape=[SDS((m, b - a), f32) for a, b in _IN_GROUPS],
        compiler_params=_cp(("arbitrary",)),
    )(x, g, wp)


def inproj_bwd(x, g, wp, dproj, dres, tm=256):
    m = x.shape[0]

    def body(x_ref, g_ref, w_ref, dp_ref, dr_ref, dx_ref, dg_ref, h_ref):
        xv = x_ref[...]
        r = _rms(xv)
        xn = xv * r
        gv = g_ref[...]
        h_ref[...] = (xn * gv).astype(bf16)
        dh = lax.dot_general(dp_ref[...], w_ref[...], NT, preferred_element_type=f32)
        dxn = dh * gv
        dx_ref[...] = dr_ref[...] + r * (dxn - xn * jnp.mean(dxn * xn, axis=-1, keepdims=True))

        @pl.when(pl.program_id(0) == 0)
        def _():
            dg_ref[...] = jnp.zeros_like(dg_ref)

        dg_ref[...] += jnp.sum(dh * xn, axis=0, keepdims=True)

    return pl.pallas_call(
        body, name="inproj_bwd", grid=(m // tm,),
        in_specs=[pl.BlockSpec((tm, D_MODEL), lambda i: (i, 0)), pl.BlockSpec((1, D_MODEL), lambda i: (0, 0)),
                  pl.BlockSpec((D_MODEL, IN_PAD), lambda i: (0, 0)), pl.BlockSpec((tm, IN_PAD), lambda i: (i, 0)),
                  pl.BlockSpec((tm, D_MODEL), lambda i: (i, 0))],
        out_specs=[pl.BlockSpec((tm, D_MODEL), lambda i: (i, 0)), pl.BlockSpec((1, D_MODEL), lambda i: (0, 0)),
                   pl.BlockSpec((tm, D_MODEL), lambda i: (i, 0))],
        out_shape=[SDS((m, D_MODEL), f32), SDS((1, D_MODEL), f32), SDS((m, D_MODEL), bf16)],
        compiler_params=_cp(("arbitrary",)),
    )(x, g, wp, dproj, dres)


def outproj_fwd(x, odn, osb, osg, wo, tm=512):
    m = x.shape[0]

    def body(x_ref, a_ref, b_ref, c_ref, w_ref, x2_ref, mix_ref):
        mix_ref[:, 0:DN_WIDTH] = a_ref[...].astype(bf16)
        mix_ref[:, DN_WIDTH:DN_WIDTH + SB_WIDTH] = b_ref[...].astype(bf16)
        mix_ref[:, DN_WIDTH + SB_WIDTH:D_MODEL] = c_ref[...].astype(bf16)
        x2_ref[...] = x_ref[...] + _pdot(mix_ref[...], w_ref[...])

    row = lambda w: pl.BlockSpec((tm, w), lambda i: (i, 0))
    return pl.pallas_call(
        body, name="outproj_fwd", grid=(m // tm,),
        in_specs=[row(D_MODEL), row(DN_WIDTH), row(SB_WIDTH), row(SG_WIDTH), pl.BlockSpec((D_MODEL, D_MODEL), lambda i: (0, 0))],
        out_specs=[row(D_MODEL), row(D_MODEL)],
        out_shape=[SDS((m, D_MODEL), f32), SDS((m, D_MODEL), bf16)],
        compiler_params=_cp(("arbitrary",)),
    )(x, odn, osb, osg, wo)


def outproj_bwd(dx2, wo, tm=512):
    m = dx2.shape[0]

    def body(d_ref, w_ref, a_ref, b_ref, c_ref, db_ref):
        db = d_ref[...].astype(bf16)
        db_ref[...] = db
        dm = lax.dot_general(db, w_ref[...], NT, preferred_element_type=f32)
        a_ref[...] = dm[:, 0:DN_WIDTH]
        b_ref[...] = dm[:, DN_WIDTH:DN_WIDTH + SB_WIDTH]
        c_ref[...] = dm[:, DN_WIDTH + SB_WIDTH:D_MODEL]

    row = lambda w: pl.BlockSpec((tm, w), lambda i: (i, 0))
    return pl.pallas_call(
        body, name="outproj_bwd", grid=(m // tm,),
        in_specs=[row(D_MODEL), pl.BlockSpec((D_MODEL, D_MODEL), lambda i: (0, 0))],
        out_specs=[row(DN_WIDTH), row(SB_WIDTH), row(SG_WIDTH), row(D_MODEL)],
        out_shape=[SDS((m, DN_WIDTH), f32), SDS((m, SB_WIDTH), f32), SDS((m, SG_WIDTH), f32), SDS((m, D_MODEL), bf16)],
        compiler_params=_cp(("arbitrary",)),
    )(dx2, wo)


FF_CHUNK = 1024


def _load_weights_once(pairs, sem):
    @pl.when(pl.program_id(0) == 0)
    def _():
        cps = [pltpu.make_async_copy(h, v, sem.at[i]) for i, (h, v) in enumerate(pairs)]
        for c in cps:
            c.start()
        for c in cps:
            c.wait()


def ffn_fwd(x2, g, w1, w2, tm=256):
    m = x2.shape[0]

    def body(x_ref, g_ref, w1_hbm, w2_hbm, y_ref, w1_v, w2_v, sem):
        _load_weights_once(((w1_hbm, w1_v), (w2_hbm, w2_v)), sem)
        xv = x_ref[...]
        h = (xv * _rms(xv) * g_ref[...]).astype(bf16)
        acc = xv
        for j in range(0, D_FF, FF_CHUNK):
            f = _pdot(h, w1_v[:, j:j + FF_CHUNK])
            rl = jnp.maximum(f, 0.0)
            acc = acc + _pdot((rl * rl).astype(bf16), w2_v[j:j + FF_CHUNK, :])
        y_ref[...] = acc

    return pl.pallas_call(
        body, name="ffn_fwd", grid=(m // tm,),
        in_specs=[pl.BlockSpec((tm, D_MODEL), lambda i: (i, 0)), pl.BlockSpec((1, D_MODEL), lambda i: (0, 0)),
                  pl.BlockSpec(memory_space=pl.ANY), pl.BlockSpec(memory_space=pl.ANY)],
        out_specs=pl.BlockSpec((tm, D_MODEL), lambda i: (i, 0)),
        out_shape=SDS((m, D_MODEL), f32),
        scratch_shapes=[pltpu.VMEM((D_MODEL, D_FF), bf16), pltpu.VMEM((D_FF, D_MODEL), bf16), pltpu.SemaphoreType.DMA((2,))],
        compiler_params=_cp(("arbitrary",)),
    )(x2, g, w1, w2)


def ffn_bwd(x2, g, w1, w2, dy, tm=256):
    m = x2.shape[0]

    def body(x_ref, g_ref, w1_hbm, w2_hbm, dy_ref, dx_ref, dg_ref, h_ref, a_ref, df_ref, dyb_ref, w1_v, w2_v, sem):
        _load_weights_once(((w1_hbm, w1_v), (w2_hbm, w2_v)), sem)
        xv = x_ref[...]
        r = _rms(xv)
        xn = xv * r
        gv = g_ref[...]
        h = (xn * gv).astype(bf16)
        h_ref[...] = h
        dyv = dy_ref[...]
        dyb = dyv.astype(bf16)
        dyb_ref[...] = dyb
        dh = jnp.zeros((tm, D_MODEL), f32)
        for j in range(0, D_FF, FF_CHUNK):
            f = _pdot(h, w1_v[:, j:j + FF_CHUNK])
            rl = jnp.maximum(f, 0.0)
            a_ref[:, j:j + FF_CHUNK] = (rl * rl).astype(bf16)
            da = lax.dot_general(dyb, w2_v[j:j + FF_CHUNK, :], NT, preferred_element_type=f32)
            df = (da * (2.0 * rl)).astype(bf16)
            df_ref[:, j:j + FF_CHUNK] = df
            dh = dh + lax.dot_general(df, w1_v[:, j:j + FF_CHUNK], NT, preferred_element_type=f32)
        dxn = dh * gv
        dx_ref[...] = dyv + r * (dxn - xn * jnp.mean(dxn * xn, axis=-1, keepdims=True))

        @pl.when(pl.program_id(0) == 0)
        def _():
            dg_ref[...] = jnp.zeros_like(dg_ref)

        dg_ref[...] += jnp.sum(dh * xn, axis=0, keepdims=True)

    row = lambda w: pl.BlockSpec((tm, w), lambda i: (i, 0))
    return pl.pallas_call(
        body, name="ffn_bwd", grid=(m // tm,),
        in_specs=[row(D_MODEL), pl.BlockSpec((1, D_MODEL), lambda i: (0, 0)),
                  pl.BlockSpec(memory_space=pl.ANY), pl.BlockSpec(memory_space=pl.ANY), row(D_MODEL)],
        out_specs=[row(D_MODEL), pl.BlockSpec((1, D_MODEL), lambda i: (0, 0)), row(D_MODEL), row(D_FF), row(D_FF), row(D_MODEL)],
        out_shape=[SDS((m, D_MODEL), f32), SDS((1, D_MODEL), f32), SDS((m, D_MODEL), bf16), SDS((m, D_FF), bf16),
                   SDS((m, D_FF), bf16), SDS((m, D_MODEL), bf16)],
        scratch_shapes=[pltpu.VMEM((D_MODEL, D_FF), bf16), pltpu.VMEM((D_FF, D_MODEL), bf16), pltpu.SemaphoreType.DMA((2,))],
        compiler_params=_cp(("arbitrary",)),
    )(x2, g, w1, w2, dy)


def _tile(n, cap):
    best = 128
    for t in range(128, cap + 1, 128):
        if n % t == 0:
            best = t
    return best


def tn_matmul(a, b, name, col_shards=1, tk=2048):
    m, ka = a.shape
    n = b.shape[1]
    ti = _tile(ka, 1024)
    tj = _tile(n // col_shards, 1152)
    tk = min(tk, m)
    nk = m // tk
    jps = (n // col_shards) // tj

    def body(a_ref, b_ref, o_ref, acc):
        k = pl.program_id(2)

        @pl.when(k == 0)
        def _():
            acc[...] = jnp.zeros_like(acc)

        acc[...] += lax.dot_general(a_ref[...], b_ref[...], TN, preferred_element_type=f32)

        @pl.when(k == nk - 1)
        def _():
            o_ref[...] = acc[...].astype(bf16).reshape(o_ref.shape)

    if col_shards == 1:
        out_shape, out_spec = SDS((ka, n), bf16), pl.BlockSpec((ti, tj), lambda i, j, k: (i, j))
    else:
        out_shape = SDS((col_shards, ka, n // col_shards), bf16)
        out_spec = pl.BlockSpec((1, ti, tj), lambda i, j, k: (j // jps, i, j % jps))
    return pl.pallas_call(
        body, name=name, grid=(ka // ti, n // tj, nk),
        in_specs=[pl.BlockSpec((tk, ti), lambda i, j, k: (k, i)), pl.BlockSpec((tk, tj), lambda i, j, k: (k, j))],
        out_specs=out_spec, out_shape=out_shape,
        scratch_shapes=[pltpu.VMEM((ti, tj), f32)],
        compiler_params=_cp(("arbitrary", "arbitrary", "arbitrary")),
    )(a, b)


def loss_head(y, tgt, tm=512):
    m = y.shape[0]

    def body(y_ref, t_ref, dy_ref, l_ref):
        e = y_ref[...] - t_ref[...]
        dy_ref[...] = e * (1.0 / D_MODEL)

        @pl.when(pl.program_id(0) == 0)
        def _():
            l_ref[...] = jnp.zeros_like(l_ref)

        l_ref[...] += jnp.sum(e * e, axis=0, keepdims=True) * (0.5 / D_MODEL)

    row = pl.BlockSpec((tm, D_MODEL), lambda i: (i, 0))
    return pl.pallas_call(
        body, name="loss_head", grid=(m // tm,), in_specs=[row, row],
        out_specs=[row, pl.BlockSpec((1, D_MODEL), lambda i: (0, 0))],
        out_shape=[SDS((m, D_MODEL), f32), SDS((1, D_MODEL), f32)],
        compiler_params=_cp(("arbitrary",)),
    )(y, tgt)


def _dn_consts():
    c = DN_CHUNK
    r, cc = _iota2((c, c))
    lt = (cc <= r).astype(bf16)
    ltt = (r <= cc).astype(bf16)
    return lt, ltt


def dn_chunk(cq, ck, cv, g, beta, z, s, gain, lt, ltt, t_given=None):
    c = DN_CHUNK
    r, cc = _iota2((c, c))
    q = cq * lax.rsqrt(jnp.sum(cq * cq, axis=-1, keepdims=True) + NORM_EPS) * (DN_DIM ** -0.5)
    k = ck * lax.rsqrt(jnp.sum(ck * ck, axis=-1, keepdims=True) + NORM_EPS)
    r2, c2 = _iota2((c, 128))
    uaug = jnp.where((c2 < c) & (r2 > c2), 1.0, 0.0) + jnp.where(c2 == c, 1.0, 0.0)
    gam_all = lmul_const(lt, ltt, g * uaug)
    gam_cc = gam_all[:, :, 0:c]
    gam = gam_all[:, :, c:c + 1]
    dec = jnp.where(cc <= r, jnp.exp(jnp.where(cc <= r, gam_cc, 0.0)), 0.0)
    kk = mm_nt(k, k)
    lm = jnp.where(cc < r, beta * kk * dec, 0.0)
    t = inv_unit_lower(lm) if t_given is None else inv_given(lm, t_given)
    eg = jnp.exp(gam)
    sol = mm_hl(t, jnp.concatenate([cv * beta, k * (beta * eg)], axis=2))
    u, w = sol[:, :, 0:DN_DIM], sol[:, :, DN_DIM:2 * DN_DIM]
    qk = jnp.where(cc <= r, mm_nt(q, k) * dec, 0.0)
    glast = jnp.sum(g, axis=1, keepdims=True)
    qd = q * eg
    kd = k * jnp.exp(glast - gam)
    un = u - mm(w, s)
    o = mm(qd, s) + mm(qk, un)
    s_new = s * jnp.exp(glast) + mm_tn(kd, un)
    on = o * lax.rsqrt(jnp.mean(o * o, axis=-1, keepdims=True) + NORM_EPS) * gain * _silu(z)
    return on, s_new, t


def _dn_gates(ab, al_row, dt_row):
    pre = ab + dt_row
    return -jnp.exp(al_row) * _softplus(pre), _sigmoid(ab), _sigmoid(pre)


def _dn_chains(cacts, gates, z_ref):
    cq, ck, cv, g, beta, z = [], [], [], [], [], []
    for bi, cact in enumerate(cacts):
        for h in range(DN_HEADS):
            cq.append(cact[:, h * DN_DIM:(h + 1) * DN_DIM])
            ck.append(cact[:, DN_WIDTH + h * DN_DIM:DN_WIDTH + (h + 1) * DN_DIM])
            cv.append(cact[:, 2 * DN_WIDTH + h * DN_DIM:2 * DN_WIDTH + (h + 1) * DN_DIM])
            g.append(gates[bi][0][:, h:h + 1])
            beta.append(gates[bi][1][:, DN_HEADS + h:DN_HEADS + h + 1])
            z.append(z_ref[bi, :, h * DN_DIM:(h + 1) * DN_DIM])
    return tuple(jnp.stack(v) for v in (cq, ck, cv, g, beta, z))


def _conv_rows(xe_ref, b, w_ref):
    y = w_ref[0:1, :] * xe_ref[b, pl.ds(5, DN_CHUNK), :]
    for i in range(1, DN_CONV):
        y = y + w_ref[i:i + 1, :] * xe_ref[b, pl.ds(5 + i, DN_CHUNK), :]
    return y


def dn_fwd(qkv, z, ab, conv_w, alog, dtb, gain):
    bsz, t, _ = qkv.shape
    nc = t // DN_CHUNK
    c = DN_CHUNK
    nh = bsz * DN_HEADS

    def body(qkv_ref, z_ref, ab_ref, w_ref, al_ref, dt_ref, g_ref, o_ref, sall_ref, tall_ref, xe, s_sc):
        n = pl.program_id(0)

        @pl.when(n == 0)
        def _():
            xe[:, 0:8, :] = jnp.zeros((bsz, 8, 3 * DN_WIDTH), f32)
            s_sc[...] = jnp.zeros_like(s_sc)

        lt, ltt = _dn_consts()
        cacts = []
        for b in range(bsz):
            xe[b, 8:8 + c, :] = qkv_ref[b]
            cacts.append(_silu(_conv_rows(xe, b, w_ref)))
            xe[b, 0:8, :] = xe[b, c:c + 8, :]
        gates = [_dn_gates(ab_ref[b], al_ref[...], dt_ref[...]) for b in range(bsz)]
        s = s_sc[...]
        sall_ref[0] = s
        on, sn, tt = dn_chunk(*_dn_chains(cacts, gates, z_ref), s, g_ref[...], lt, ltt)
        tall_ref[0] = tt
        s_sc[...] = sn
        for b in range(bsz):
            for h in range(DN_HEADS):
                o_ref[b, :, h * DN_DIM:(h + 1) * DN_DIM] = on[b * DN_HEADS + h]

    blk = lambda w: pl.BlockSpec((bsz, c, w), lambda n: (0, n, 0))
    full = lambda shp: pl.BlockSpec(shp, lambda n: (0,) * len(shp))
    return pl.pallas_call(
        body, name="dn_fwd", grid=(nc,),
        in_specs=[blk(3 * DN_WIDTH), blk(DN_WIDTH), blk(128), full((8, 3 * DN_WIDTH)), full((1, 128)), full((1, 128)), full((1, 128))],
        out_specs=[blk(DN_WIDTH), pl.BlockSpec((1, nh, DN_DIM, DN_DIM), lambda n: (n, 0, 0, 0)),
                   pl.BlockSpec((1, nh, c, c), lambda n: (n, 0, 0, 0))],
        out_shape=[SDS((bsz, t, DN_WIDTH), f32), SDS((nc, nh, DN_DIM, DN_DIM), f32), SDS((nc, nh, c, c), f32)],
        scratch_shapes=[pltpu.VMEM((bsz, c + 8, 3 * DN_WIDTH), f32), pltpu.VMEM((nh, DN_DIM, DN_DIM), f32)],
        compiler_params=_cp(("arbitrary",)),
    )(qkv, z, ab, conv_w, alog, dtb, gain)


def dn_bwd(qkv, z, ab, conv_w, alog, dtb, gain, sall, tall, do):
    bsz, t, _ = qkv.shape
    nc = t // DN_CHUNK
    c = DN_CHUNK
    nh = bsz * DN_HEADS
    w3 = 3 * DN_WIDTH

    def body(qkv_ref, prev_ref, z_ref, ab_ref, w_ref, al_ref, dt_ref, g_ref, sall_ref, tall_ref, do_ref,
             dp_ref, dw_ref, dal_ref, ddt_ref, dg_ref, xe, dye, dc_sc, ds_sc):
        n = pl.program_id(0)
        first = (nc - 1 - n) == 0

        @pl.when(n == 0)
        def _():
            dye[:, c:c + 8, :] = jnp.zeros((bsz, 8, w3), f32)
            ds_sc[...] = jnp.zeros_like(ds_sc)
            dw_ref[...] = jnp.zeros_like(dw_ref)
            dal_ref[...] = jnp.zeros_like(dal_ref)
            ddt_ref[...] = jnp.zeros_like(ddt_ref)
            dg_ref[...] = jnp.zeros_like(dg_ref)

        lt, ltt = _dn_consts()
        lane = lax.broadcasted_iota(jnp.int32, (1, 128), 1)
        lane_c = lax.broadcasted_iota(jnp.int32, (c, 128), 1)
        ys, sigs = [], []
        for b in range(bsz):
            xe[b, 0:8, :] = jnp.where(first, 0.0, prev_ref[b])
            xe[b, 8:8 + c, :] = qkv_ref[b]
            ys.append(_conv_rows(xe, b, w_ref))
            sigs.append(_sigmoid(ys[b]))
        gates = [_dn_gates(ab_ref[b], al_ref[...], dt_ref[...]) for b in range(bsz)]
        ops = _dn_chains([y * sg for y, sg in zip(ys, sigs)], gates, z_ref)
        tt = tall_ref[0]
        _, vjp = jax.vjp(lambda *p: dn_chunk(*p, lt, ltt, t_given=tt)[0:2], *ops, sall_ref[0], g_ref[...])
        don = jnp.stack([do_ref[b, :, h * DN_DIM:(h + 1) * DN_DIM] for b in range(bsz) for h in range(DN_HEADS)])
        dcq, dck, dcv, dg, dbeta, dzz, dsp, dgn = vjp((don, ds_sc[...]))
        ds_sc[...] = dsp
        dg_ref[...] += dgn
        for b in range(bsz):
            dgate = jnp.zeros((c, 128), f32)
            for h in range(DN_HEADS):
                i = b * DN_HEADS + h
                dc_sc[b, :, h * DN_DIM:(h + 1) * DN_DIM] = dcq[i]
                dc_sc[b, :, DN_WIDTH + h * DN_DIM:DN_WIDTH + (h + 1) * DN_DIM] = dck[i]
                dc_sc[b, :, 2 * DN_WIDTH + h * DN_DIM:2 * DN_WIDTH + (h + 1) * DN_DIM] = dcv[i]
                dp_ref[b, :, C_Z + h * DN_DIM:C_Z + (h + 1) * DN_DIM] = dzz[i].astype(bf16)
                dgate = dgate + jnp.where(lane_c == h, dg[i], 0.0) + jnp.where(lane_c == DN_HEADS + h, dbeta[i], 0.0)
            gg, beta, sig_pre = gates[b]
            is_g = lane_c < DN_HEADS
            dpre = jnp.where(is_g, dgate * (-jnp.exp(al_ref[...])) * sig_pre, 0.0)
            dp_ref[b, :, C_AB:C_AB + 128] = (dpre + jnp.where(is_g, 0.0, dgate * beta * (1.0 - beta))).astype(bf16)
            dp_ref[b, :, C_AB + 128:DN_COLS] = jnp.zeros((c, DN_COLS - C_AB - 128), bf16)
            dal_ref[...] += jnp.sum(jnp.where(is_g, dgate * gg, 0.0), axis=0, keepdims=True)
            ddt_ref[...] += jnp.sum(dpre, axis=0, keepdims=True)
            y, sig = ys[b], sigs[b]
            dy = dc_sc[b] * (sig * (1.0 + y * (1.0 - sig)))
            dye[b, 0:c, :] = dy
            dx = w_ref[3:4, :] * dy
            for i in range(DN_CONV - 1):
                dx = dx + w_ref[i:i + 1, :] * dye[b, pl.ds(3 - i, c), :]
            dp_ref[b, :, 0:w3] = dx.astype(bf16)
            for i in range(DN_CONV):
                dw_ref[i:i + 1, :] += jnp.sum(dy * xe[b, pl.ds(5 + i, c), :], axis=0, keepdims=True)
            dye[b, c:c + 8, :] = dye[b, 0:8, :]

    rev = lambda w: pl.BlockSpec((bsz, c, w), lambda n: (0, nc - 1 - n, 0))
    full = lambda shp: pl.BlockSpec(shp, lambda n: (0,) * len(shp))
    prev = pl.BlockSpec((bsz, 8, w3), lambda n: (0, jnp.maximum((nc - 1 - n) * (c // 8) - 1, 0), 0))
    return pl.pallas_call(
        body, name="dn_bwd", grid=(nc,),
        in_specs=[rev(w3), prev, rev(DN_WIDTH), rev(128), full((8, w3)), full((1, 128)), full((1, 128)), full((1, 128)),
                  pl.BlockSpec((1, nh, DN_DIM, DN_DIM), lambda n: (nc - 1 - n, 0, 0, 0)),
                  pl.BlockSpec((1, nh, c, c), lambda n: (nc - 1 - n, 0, 0, 0)), rev(DN_WIDTH)],
        out_specs=[rev(DN_COLS), full((8, w3)), full((1, 128)), full((1, 128)), full((1, 128))],
        out_shape=[SDS((bsz, t, IN_PAD), bf16), SDS((8, w3), f32), SDS((1, 128), f32), SDS((1, 128), f32), SDS((1, 128), f32)],
        scratch_shapes=[pltpu.VMEM((bsz, c + 8, w3), f32), pltpu.VMEM((bsz, c + 8, w3), f32), pltpu.VMEM((bsz, c, w3), f32),
                        pltpu.VMEM((nh, DN_DIM, DN_DIM), f32)],
        compiler_params=_cp(("arbitrary",)),
    )(qkv, qkv, z, ab, conv_w, alog, dtb, gain, sall, tall, do)


SB_TILE = 256
SB_PAIRS = SB_HEADS // 2


def sb_fwd(sbqkv, gq, gk):
    bsz, t, _ = sbqkv.shape
    blk = min(SB_TILE, t)
    nq = t // blk
    scale = SB_DIM ** -0.5

    def body(q_ref, k_ref, v_ref, gq_ref, gk_ref, o_ref, l_ref, q2_sc, kn_sc, v_sc):
        bavg = _group_avg_mats()
        lane = lax.broadcasted_iota(jnp.int32, (1, 128), 1)
        first = lane < SB_DIM
        for p in range(SB_PAIRS):
            ls = slice(p * 128, (p + 1) * 128)
            qn = _pair_norm(q_ref[0, :, ls], gq_ref[...], bavg)
            kn_sc[p] = _pair_norm(k_ref[0, :, ls], gk_ref[...], bavg).astype(bf16)
            v_sc[p] = v_ref[0, :, ls].astype(bf16)
            q2_sc[2 * p] = jnp.where(first, qn, 0.0).astype(bf16)
            q2_sc[2 * p + 1] = jnp.where(first, 0.0, qn).astype(bf16)
        r, c = _iota2((blk, blk))
        ustrict = (r > c).astype(bf16)
        r2, c2 = _iota2((2 * blk, blk))
        causal = c2 < (r2 & (blk - 1))

        def tile(q2s, ks, carry, diag):
            out = []
            for p in range(SB_PAIRS):
                acc, rr = carry[2 * p], carry[2 * p + 1]
                zz = lax.dot_general(q2s[p], kn_sc[p, pl.ds(ks, blk), :], NT, preferred_element_type=f32) * scale
                sp = _softplus(zz)
                lm = jnp.where(causal, -sp, 0.0) if diag else -sp
                rem = _dot_x2c(lm, ustrict)
                wgt = jnp.exp(zz - sp + rem + rr)
                if diag:
                    wgt = jnp.where(causal, wgt, 0.0)
                out += [acc + _pdot(wgt.astype(bf16), v_sc[p, pl.ds(ks, blk), :]), rr + jnp.sum(lm, axis=1, keepdims=True)]
            return tuple(out)

        def qloop(qi, _):
            qs = pl.multiple_of(qi * blk, blk)
            q2s = [jnp.concatenate([q2_sc[2 * p, pl.ds(qs, blk), :], q2_sc[2 * p + 1, pl.ds(qs, blk), :]], axis=0)
                   for p in range(SB_PAIRS)]
            zero = (jnp.zeros((2 * blk, 128), f32), jnp.zeros((2 * blk, 1), f32)) * SB_PAIRS
            carry = lax.fori_loop(1, qi + 1, lambda i, cr: tile(q2s, pl.multiple_of((qi - i) * blk, blk), cr, False),
                                  tile(q2s, qs, zero, True))
            for p in range(SB_PAIRS):
                acc, rr = carry[2 * p], carry[2 * p + 1]
                o_ref[0, pl.ds(qs, blk), p * 128:(p + 1) * 128] = jnp.where(first, acc[0:blk], acc[blk:2 * blk])
                l_ref[0, pl.ds(qs, blk), p * 128:(p + 1) * 128] = jnp.where(first, rr[0:blk], rr[blk:2 * blk])
            return 0

        lax.fori_loop(0, nq, qloop, 0)

    col = lambda off: pl.BlockSpec((1, t, SB_WIDTH), lambda b: (b, 0, off))
    gsp = pl.BlockSpec((1, 128), lambda b: (0, 0))
    return pl.pallas_call(
        body, name="sb_fwd", grid=(bsz,),
        in_specs=[col(0), col(1), col(2), gsp, gsp],
        out_specs=[col(0), col(0)],
        out_shape=[SDS((bsz, t, SB_WIDTH), f32), SDS((bsz, t, SB_WIDTH), f32)],
        scratch_shapes=[pltpu.VMEM((2 * SB_PAIRS, t, 128), bf16), pltpu.VMEM((SB_PAIRS, t, 128), bf16),
                        pltpu.VMEM((SB_PAIRS, t, 128), bf16)],
        compiler_params=_cp(("arbitrary",)),
    )(sbqkv, sbqkv, sbqkv, gq, gk)


def sb_bwd(sbqkv, gq, gk, ltot, do, dproj):
    bsz, t, _ = sbqkv.shape
    blk = min(SB_TILE, t)
    nq = t // blk
    scale = SB_DIM ** -0.5

    def body(q_ref, k_ref, v_ref, gq_ref, gk_ref, l_ref, do_ref, dp_in, dp_ref, dgq_ref, dgk_ref,
             q2_sc, kn_sc, v_sc, do2_sc, dqn_sc, dkn_sc, dv_sc):
        bavg = _group_avg_mats()
        lane = lax.broadcasted_iota(jnp.int32, (1, 128), 1)
        first = lane < SB_DIM
        fq = lambda x, g: _pair_norm(x, g, bavg)
        vjps = []
        for p in range(SB_PAIRS):
            ls = slice(p * 128, (p + 1) * 128)
            qn, q_vjp = jax.vjp(fq, q_ref[0, :, ls], gq_ref[...])
            kn, k_vjp = jax.vjp(fq, k_ref[0, :, ls], gk_ref[...])
            vjps.append((q_vjp, k_vjp))
            kn_sc[p] = kn.astype(bf16)
            v_sc[p] = v_ref[0, :, ls].astype(bf16)
            dov = do_ref[0, :, ls]
            q2_sc[2 * p] = jnp.where(first, qn, 0.0).astype(bf16)
            q2_sc[2 * p + 1] = jnp.where(first, 0.0, qn).astype(bf16)
            do2_sc[2 * p] = jnp.where(first, dov, 0.0).astype(bf16)
            do2_sc[2 * p + 1] = jnp.where(first, 0.0, dov).astype(bf16)
        dkn_sc[...] = jnp.zeros_like(dkn_sc)
        dv_sc[...] = jnp.zeros_like(dv_sc)
        r, c = _iota2((blk, blk))
        pincl = (r <= c).astype(bf16)
        pstrict = (r < c).astype(bf16)
        r2, c2 = _iota2((2 * blk, blk))
        causal = c2 < (r2 & (blk - 1))

        def tile(q2s, do2s, lts, ks, carry, diag):
            out = []
            for p in range(SB_PAIRS):
                dq, cs, ce = carry[3 * p:3 * p + 3]
                q2, do2 = q2s[p], do2s[p]
                kb = kn_sc[p, pl.ds(ks, blk), :]
                zz = lax.dot_general(q2, kb, NT, preferred_element_type=f32) * scale
                sp = _softplus(zz)
                lm = jnp.where(causal, -sp, 0.0) if diag else -sp
                pre = _dot_x2c(lm, pincl)
                lp = zz - sp
                wgt = jnp.exp(lp + (lts[p] - cs - pre))
                if diag:
                    wgt = jnp.where(causal, wgt, 0.0)
                dw = lax.dot_general(do2, v_sc[p, pl.ds(ks, blk), :], NT, preferred_element_type=f32)
                e = wgt * dw
                ee = ce + _dot_x2c(e, pstrict)
                sig = jnp.exp(lp)
                dz = (e * (1.0 - sig) - ee * sig) * scale
                if diag:
                    dz = jnp.where(causal, dz, 0.0)
                dz = dz.astype(bf16)
                dkn_sc[p, pl.ds(ks, blk), :] += lax.dot_general(dz, q2, TN, preferred_element_type=f32)
                dv_sc[p, pl.ds(ks, blk), :] += lax.dot_general(wgt.astype(bf16), do2, TN, preferred_element_type=f32)
                out += [dq + _pdot(dz, kb), cs + jnp.sum(lm, axis=1, keepdims=True), ce + jnp.sum(e, axis=1, keepdims=True)]
            return tuple(out)

        def qloop(qi, _):
            qs = pl.multiple_of(qi * blk, blk)
            rows = pl.ds(qs, blk)
            q2s = [jnp.concatenate([q2_sc[2 * p, rows, :], q2_sc[2 * p + 1, rows, :]], axis=0) for p in range(SB_PAIRS)]
            do2s = [jnp.concatenate([do2_sc[2 * p, rows, :], do2_sc[2 * p + 1, rows, :]], axis=0) for p in range(SB_PAIRS)]
            lts = [jnp.concatenate([l_ref[0, rows, p * 128:p * 128 + 1], l_ref[0, rows, p * 128 + SB_DIM:p * 128 + SB_DIM + 1]],
                                   axis=0) for p in range(SB_PAIRS)]
            z1 = jnp.zeros((2 * blk, 1), f32)
            carry = lax.fori_loop(0, qi, lambda kj, cr: tile(q2s, do2s, lts, pl.multiple_of(kj * blk, blk), cr, False),
                                  (jnp.zeros((2 * blk, 128), f32), z1, z1) * SB_PAIRS)
            carry = tile(q2s, do2s, lts, qs, carry, True)
            for p in range(SB_PAIRS):
                dq = carry[3 * p]
                dqn_sc[p, rows, :] = jnp.where(first, dq[0:blk], dq[blk:2 * blk])
            return 0

        lax.fori_loop(0, nq, qloop, 0)
        dgq_tot, dgk_tot = jnp.zeros((1, 128), f32), jnp.zeros((1, 128), f32)
        for p in range(SB_PAIRS):
            ls = slice(p * 128, (p + 1) * 128)
            dq_pre, dgq = vjps[p][0](dqn_sc[p])
            dk_pre, dgk = vjps[p][1](dkn_sc[p])
            dp_ref[0, :, p * 128:(p + 1) * 128] = dq_pre.astype(bf16)
            dp_ref[0, :, SB_WIDTH + p * 128:SB_WIDTH + (p + 1) * 128] = dk_pre.astype(bf16)
            dp_ref[0, :, 2 * SB_WIDTH + p * 128:2 * SB_WIDTH + (p + 1) * 128] = dv_sc[p].astype(bf16)
            dgq_tot, dgk_tot = dgq_tot + dgq, dgk_tot + dgk
        dgq_ref[0] = jnp.broadcast_to(dgq_tot, (8, 128))
        dgk_ref[0] = jnp.broadcast_to(dgk_tot, (8, 128))

    col = lambda off: pl.BlockSpec((1, t, SB_WIDTH), lambda b: (b, 0, off), pipeline_mode=pl.Buffered(1))
    gsp = pl.BlockSpec((1, 128), lambda b: (0, 0))
    gout = pl.BlockSpec((1, 8, 128), lambda b: (b, 0, 0))
    return pl.pallas_call(
        body, name="sb_bwd", grid=(bsz,),
        in_specs=[col(0), col(1), col(2), gsp, gsp, col(0), col(0), pl.BlockSpec(memory_space=pl.ANY)],
        out_specs=[pl.BlockSpec((1, t, 3 * SB_WIDTH), lambda b: (b, 0, C_SB // (3 * SB_WIDTH)), pipeline_mode=pl.Buffered(1)),
                   gout, gout],
        out_shape=[SDS(dproj.shape, bf16)] + [SDS((bsz, 8, 128), f32)] * 2,
        input_output_aliases={7: 0},
        scratch_shapes=[pltpu.VMEM((2 * SB_PAIRS, t, 128), bf16), pltpu.VMEM((SB_PAIRS, t, 128), bf16),
                        pltpu.VMEM((SB_PAIRS, t, 128), bf16), pltpu.VMEM((2 * SB_PAIRS, t, 128), bf16),
                        pltpu.VMEM((SB_PAIRS, t, 128), f32), pltpu.VMEM((SB_PAIRS, t, 128), f32), pltpu.VMEM((SB_PAIRS, t, 128), f32)],
        compiler_params=_cp(("arbitrary",)),
    )(sbqkv, sbqkv, sbqkv, gq, gk, ltot, do, dproj)


def sg_pair(u, v, gain, wa, wb, ba, bb, bavg):
    r, c = _iota2((SG_CHUNK, SG_CHUNK))
    lane = lax.broadcasted_iota(jnp.int32, (1, 128), 1)
    first = lane < SG_DIM
    vn = _pair_norm(_gelu(v), gain, bavg)
    tri = c <= r
    mixed = (mm(jnp.where(tri, wa, 0.0), jnp.where(first, vn, 0.0)) + mm(jnp.where(tri, wb, 0.0), jnp.where(first, 0.0, vn))
             + jnp.where(first, ba, bb))
    return _gelu(u) * mixed


def sg_fwd(sguv, gain, w, bt):
    bsz, t, _ = sguv.shape
    nch = t // SG_CHUNK

    def body(uv_ref, g_ref, w_ref, b_ref, o_ref):
        bavg = _group_avg_mats()
        for p in range(2):
            ls = slice(p * 128, (p + 1) * 128)
            o_ref[0, :, ls] = sg_pair(uv_ref[0, :, ls], uv_ref[0, :, SG_WIDTH + p * 128:SG_WIDTH + (p + 1) * 128], g_ref[:, ls],
                                      w_ref[2 * p], w_ref[2 * p + 1], b_ref[:, 2 * p:2 * p + 1], b_ref[:, 2 * p + 1:2 * p + 2], bavg)

    full = lambda shp: pl.BlockSpec(shp, lambda b, n: (0,) * len(shp))
    return pl.pallas_call(
        body, name="sg_fwd", grid=(bsz, nch),
        in_specs=[pl.BlockSpec((1, SG_CHUNK, 2 * SG_WIDTH), lambda b, n: (b, n, 0)), full((1, SG_WIDTH)),
                  full((SG_GROUPS, SG_CHUNK, SG_CHUNK)), full((SG_CHUNK, 128))],
        out_specs=pl.BlockSpec((1, SG_CHUNK, SG_WIDTH), lambda b, n: (b, n, 0)),
        out_shape=SDS((bsz, t, SG_WIDTH), f32),
        compiler_params=_cp(("arbitrary", "arbitrary")),
    )(sguv, gain, w, bt)


def sg_bwd(sguv, gain, w, bt, do, dproj):
    bsz, t, _ = sguv.shape
    nch = t // SG_CHUNK

    def body(uv_ref, g_ref, w_ref, b_ref, do_ref, dp_in, duv_ref, dg_ref, dw_ref, db_ref):
        @pl.when((pl.program_id(0) == 0) & (pl.program_id(1) == 0))
        def _():
            dg_ref[...] = jnp.zeros_like(dg_ref)
            dw_ref[...] = jnp.zeros_like(dw_ref)
            db_ref[...] = jnp.zeros_like(db_ref)

        bavg = _group_avg_mats()
        lane = lax.broadcasted_iota(jnp.int32, (SG_CHUNK, 128), 1)
        dbt = jnp.zeros((SG_CHUNK, 128), f32)
        for p in range(2):
            ls = slice(p * 128, (p + 1) * 128)
            vs = slice(SG_WIDTH + p * 128, SG_WIDTH + (p + 1) * 128)
            prim = (uv_ref[0, :, ls], uv_ref[0, :, vs], g_ref[:, ls], w_ref[2 * p], w_ref[2 * p + 1],
                    b_ref[:, 2 * p:2 * p + 1], b_ref[:, 2 * p + 1:2 * p + 2])
            _, vjp = jax.vjp(lambda *a: sg_pair(*a, bavg), *prim)
            du, dv, dgn, dwa, dwb, dba, dbb = vjp(do_ref[0, :, ls])
            duv_ref[0, :, ls] = du.astype(bf16)
            duv_ref[0, :, vs] = dv.astype(bf16)
            dg_ref[:, ls] += dgn
            dw_ref[2 * p] += dwa
            dw_ref[2 * p + 1] += dwb
            dbt = dbt + jnp.where(lane == 2 * p, dba, 0.0) + jnp.where(lane == 2 * p + 1, dbb, 0.0)
        db_ref[...] += dbt

    full = lambda shp: pl.BlockSpec(shp, lambda b, n: (0,) * len(shp))
    return pl.pallas_call(
        body, name="sg_bwd", grid=(bsz, nch),
        in_specs=[pl.BlockSpec((1, SG_CHUNK, 2 * SG_WIDTH), lambda b, n: (b, n, 0)), full((1, SG_WIDTH)),
                  full((SG_GROUPS, SG_CHUNK, SG_CHUNK)), full((SG_CHUNK, 128)),
                  pl.BlockSpec((1, SG_CHUNK, SG_WIDTH), lambda b, n: (b, n, 0)), pl.BlockSpec(memory_space=pl.ANY)],
        out_specs=[pl.BlockSpec((1, SG_CHUNK, 2 * SG_WIDTH), lambda b, n: (b, n, C_SG // (2 * SG_WIDTH))), full((1, SG_WIDTH)),
                   full((SG_GROUPS, SG_CHUNK, SG_CHUNK)), full((SG_CHUNK, 128))],
        out_shape=[SDS(dproj.shape, bf16), SDS((1, SG_WIDTH), f32), SDS((SG_GROUPS, SG_CHUNK, SG_CHUNK), f32),
                   SDS((SG_CHUNK, 128), f32)],
        input_output_aliases={5: 0},
        compiler_params=_cp(("arbitrary", "arbitrary")),
    )(sguv, gain, w, bt, do, dproj)


def _pad_lanes(v, n=128):
    return jnp.pad(v.reshape(1, -1), ((0, 0), (0, n - v.size)))


def pad_w_in(w):
    return jnp.concatenate([w[:, 0:2048], jnp.pad(w[:, 2048:2056], ((0, 0), (0, C_SB - C_AB - 8))), w[:, 2056:]], axis=1)


def unpad_w_in(w):
    return jnp.concatenate([w[:, 0:2048], w[:, C_AB:C_AB + 8], w[:, C_SB:]], axis=1)


def _w_in_runs():
    shard, runs = IN_DIM // N_CHIPS, []
    for s in range(N_CHIPS):
        for a, b, d in ((0, 2048, 0), (2048, 2056, C_AB), (2056, IN_DIM, C_SB)):
            lo, hi = max(shard * s, a), min(shard * (s + 1), b)
            if lo < hi:
                runs.append((s, lo - shard * s, hi - shard * s, d + lo - a))
    return runs


def w_in_from_shards(zone, tr=256):
    def body(z_ref, o_ref):
        o_ref[:, C_AB:C_SB] = jnp.zeros((tr, C_SB - C_AB), zone.dtype)
        for s, a, b, d in _w_in_runs():
            o_ref[:, d:d + b - a] = z_ref[s, :, a:b]

    return pl.pallas_call(
        body, name="w_in_from_shards", grid=(D_MODEL // tr,),
        in_specs=[pl.BlockSpec((N_CHIPS, tr, IN_DIM // N_CHIPS), lambda i: (0, i, 0))],
        out_specs=pl.BlockSpec((tr, IN_PAD), lambda i: (i, 0)), out_shape=SDS((D_MODEL, IN_PAD), zone.dtype),
        compiler_params=_cp(("arbitrary",)))(zone)


def w_in_grad_to_shards(g, tr=256):
    def body(g_ref, o_ref):
        for s, a, b, d in _w_in_runs():
            o_ref[s, :, a:b] = g_ref[:, d:d + b - a]

    return pl.pallas_call(
        body, name="w_in_grad_to_shards", grid=(D_MODEL // tr,),
        in_specs=[pl.BlockSpec((tr, IN_PAD), lambda i: (i, 0))],
        out_specs=pl.BlockSpec((N_CHIPS, tr, IN_DIM // N_CHIPS), lambda i: (0, i, 0)),
        out_shape=SDS((N_CHIPS, D_MODEL, IN_DIM // N_CHIPS), g.dtype), compiler_params=_cp(("arbitrary",)))(g)


def layer_params(p, l):
    return dict(
        g1=p["norm1_g"][l].reshape(1, -1), g2=p["norm2_g"][l].reshape(1, -1),
        conv=jnp.pad(p["conv_w"][l], ((0, 4), (0, 0))), alog=_pad_lanes(p["a_log"][l]), dtb=_pad_lanes(p["dt_bias"][l]),
        dng=p["dn_out_g"][l].reshape(1, -1), gq=jnp.tile(p["sb_q_g"][l].reshape(1, -1), (1, 2)),
        gk=jnp.tile(p["sb_k_g"][l].reshape(1, -1), (1, 2)), sgg=p["sg_v_g"][l].reshape(1, -1), sgw=p["sg_w"][l],
        sgb=jnp.pad(p["sg_b"][l].T, ((0, 0), (0, 124))))


def local_step(x, tgt, small, get_w, put_g):
    bsz, t, _ = x.shape
    m = bsz * t
    r3 = lambda a: a.reshape(bsz, t, a.shape[-1])
    r2 = lambda a: a.reshape(m, a.shape[-1])
    xs, saved, ws = x.reshape(m, D_MODEL), [], []
    for l in range(DEPTH):
        sp, w = layer_params(small, l), {}
        w["w_in"] = get_w(l, "in", xs)
        qkv, z, ab, sb, sg = inproj_fwd(xs, sp["g1"], w["w_in"])
        odn, sall, tall = dn_fwd(r3(qkv), r3(z), r3(ab), sp["conv"], sp["alog"], sp["dtb"], sp["dng"])
        osb, ltot = sb_fwd(r3(sb), sp["gq"], sp["gk"])
        osg = sg_fwd(r3(sg), sp["sgg"], sp["sgw"], sp["sgb"])
        w["w_out"] = get_w(l, "out", osg)
        x2, mix = outproj_fwd(xs, r2(odn), r2(osb), r2(osg), w["w_out"])
        w["w_ff1"], w["w_ff2"] = get_w(l, "ff", x2)
        x3 = ffn_fwd(x2, sp["g2"], w["w_ff1"], w["w_ff2"])
        saved.append(dict(x=xs, qkv=qkv, z=z, ab=ab, sb=sb, sg=sg, sall=sall, tall=tall, ltot=ltot, mix=mix, x2=x2))
        ws.append(w)
        xs = x3
    dx, lossp = loss_head(xs, tgt.reshape(m, D_MODEL))
    gsmall = [None] * DEPTH
    token = jnp.zeros((), f32)
    for l in reversed(range(DEPTH)):
        sp, w, s = layer_params(small, l), ws[l], saved[l]
        dx2, dg2, h2, act, df, dyb = ffn_bwd(s["x2"], sp["g2"] + token, w["w_ff1"], w["w_ff2"], dx)
        g_ff1 = tn_matmul(h2, df, f"dw_ff1_{l}", col_shards=N_CHIPS)
        g_ff2 = tn_matmul(act, dyb, f"dw_ff2_{l}")
        dodn, dosb, dosg, dx2b = outproj_bwd(dx2, w["w_out"])
        g_out = tn_matmul(s["mix"], dx2b, f"dw_out_{l}")
        token = put_g(l, "rest", dict(w_out=g_out, w_ff1=g_ff1, w_ff2=g_ff2))
        dproj, dconv, dalog, ddtb, ddng = dn_bwd(r3(s["qkv"]), r3(s["z"]), r3(s["ab"]), sp["conv"], sp["alog"], sp["dtb"],
                                                 sp["dng"] + token, s["sall"], s["tall"], r3(dodn))
        dproj, dgq, dgk = sb_bwd(r3(s["sb"]), sp["gq"], sp["gk"], s["ltot"], r3(dosb), dproj)
        dproj, dsgg, dsgw, dsgb = sg_bwd(r3(s["sg"]), sp["sgg"], sp["sgw"], sp["sgb"], r3(dosg), dproj)
        dproj = r2(dproj)
        dx, dg1, h1 = inproj_bwd(s["x"], sp["g1"], w["w_in"], dproj, dx2)
        g_in = tn_matmul(h1, dproj, f"dw_in_{l}")
        token = put_g(l, "in", dict(w_in=g_in))
        fold = lambda a: (a[:, 0, :].sum(0).reshape(2, SB_DIM)).sum(0)
        gsmall[l] = dict(norm1_g=dg1[0], conv_w=dconv[0:DN_CONV], a_log=dalog[0, 0:DN_HEADS], dt_bias=ddtb[0, 0:DN_HEADS],
                         dn_out_g=ddng[0], sb_q_g=fold(dgq), sb_k_g=fold(dgk), sg_v_g=dsgg[0], sg_w=dsgw,
                         sg_b=dsgb[:, 0:SG_GROUPS].T, norm2_g=dg2[0])
    return lossp, dx.reshape(bsz, t, D_MODEL), gsmall


def _chip_peers(x, y):
    return [(1 - x, y), (x, 1 - y), (1 - x, 1 - y)]


_HBM = pl.BlockSpec(memory_space=pltpu.HBM)
_SEM = pl.BlockSpec(memory_space=pltpu.SEMAPHORE)
_EFFECT = pltpu.SideEffectType.DATAFLOW_SIDE_EFFECTING


def _hbm(a):
    return pltpu.with_memory_space_constraint(a, pltpu.HBM)


def _my_half(ref):
    half = ref.shape[0] // 2
    return ref.at[pl.ds(pl.multiple_of(lax.axis_index("c") * half, 8), half)]


def _slot(zone, s, cols):
    if not cols:
        return zone.at[s]
    width = zone.shape[1] // N_CHIPS
    return zone.at[:, pl.ds(pl.multiple_of(s * width, 128), width)]


def _exchange_copy(src, land, k, j, send, recv, scatter, halve, waiting):
    x, y, c = lax.axis_index("x"), lax.axis_index("y"), lax.axis_index("c")
    px, py = _chip_peers(x, y)[j]
    me, peer = 2 * x + y, 2 * px + py
    if scatter:
        src = src.at[me if waiting else peer]
    dst = _slot(land, peer if waiting else me, halve == "cols")
    if halve:
        src, dst = _my_half(src), _my_half(dst)
    return pltpu.make_async_remote_copy(src_ref=src, dst_ref=dst, send_sem=send.at[3 * k + j],
                                        recv_sem=recv.at[3 * k + j], device_id=(px, py, c), device_id_type=MESH)


def exchange_start(items, name, scatter):
    arrs = []
    for a, _, _ in items:
        if not any(a is b for b in arrs):
            arrs.append(a)
    pos = [next(i for i, b in enumerate(arrs) if b is a) for a, _, _ in items]
    shapes = [a.shape if idx is None else a.shape[1:] for a, idx, _ in items]
    lands = [lax.empty(s if scatter else ((s[0], N_CHIPS * s[1]) if h == "cols" else (N_CHIPS,) + s), a.dtype)
             for (a, _, h), s in zip(items, shapes)]
    na, nl = len(arrs), len(lands)

    def body(*refs):
        ins, lnd = refs[:na], refs[na:na + nl]
        send, recv = refs[na + nl], refs[na + nl + 1]
        token = refs[-1]
        for k, (_, idx, halve) in enumerate(items):
            src = ins[pos[k]] if idx is None else ins[pos[k]].at[idx]
            for j in range(3):
                _exchange_copy(src, lnd[k], k, j, send, recv, scatter, halve, False).start()
        token[...] = jnp.zeros_like(token)

    sems = pltpu.SemaphoreType.DMA((3 * nl,))
    out = pl.pallas_call(
        body, name=name,
        out_shape=(sems, sems, *[pltpu.HBM(a.shape, a.dtype) for a in arrs + lands], SDS((8, 128), f32)),
        in_specs=[_HBM] * (na + nl), out_specs=(_SEM, _SEM, *[_HBM] * (na + nl), pl.BlockSpec(memory_space=pltpu.VMEM)),
        input_output_aliases={i: 2 + i for i in range(na + nl)},
        compiler_params=pltpu.CompilerParams(has_side_effects=_EFFECT),
    )(*[_hbm(a) for a in arrs + lands])
    thru = out[2:2 + na]
    return dict(send=out[0], recv=out[1], src=[(thru[pos[k]], idx) for k, (_, idx, _) in enumerate(items)],
                halve=[h for _, _, h in items], land=list(out[2 + na:2 + na + nl]), token=out[-1], scatter=scatter)


def exchange_wait(st, ks, after, name):
    arrs = []
    for k in ks:
        if not any(st["src"][k][0] is b for b in arrs):
            arrs.append(st["src"][k][0])
    pos = [next(i for i, b in enumerate(arrs) if b is st["src"][k][0]) for k in ks]
    lands = [st["land"][k] for k in ks]
    na, nl = len(arrs), len(lands)

    def body(*refs):
        ins, lnd = refs[:na], refs[na:na + nl]
        send, recv = refs[na + nl], refs[na + nl + 1]
        for t, k in enumerate(ks):
            idx = st["src"][k][1]
            src = ins[pos[t]] if idx is None else ins[pos[t]].at[idx]
            for j in range(3):
                cp = _exchange_copy(src, lnd[t], k, j, send, recv, st["scatter"], st["halve"][k], True)
                cp.wait_send()
                cp.wait_recv()

    out = pl.pallas_call(
        body, name=name, out_shape=tuple(pltpu.HBM(a.shape, a.dtype) for a in arrs + lands),
        in_specs=[_HBM] * (na + nl) + [_SEM, _SEM, pl.BlockSpec(memory_space=pl.ANY)], out_specs=tuple([_HBM] * (na + nl)),
        input_output_aliases={i: i for i in range(na + nl)},
        compiler_params=pltpu.CompilerParams(has_side_effects=_EFFECT),
    )(*arrs, *lands, st["send"], st["recv"], after)
    for k, (a, idx) in enumerate(st["src"]):
        for p, b in enumerate(arrs):
            if a is b:
                st["src"][k] = (out[p], idx)
    return list(out[na:na + nl])


def swap_cores(arrs, name):
    n = len(arrs)

    def body(*refs):
        ins, outs = refs[:n], refs[n:2 * n]
        send, recv = refs[2 * n:]
        sib = (lax.axis_index("x"), lax.axis_index("y"), 1 - lax.axis_index("c"))
        cps = [pltpu.make_async_remote_copy(src_ref=ins[i], dst_ref=outs[i], send_sem=send.at[i], recv_sem=recv.at[i],
                                            device_id=sib, device_id_type=MESH) for i in range(n)]
        for cp in cps:
            cp.start()
        for cp in cps:
            cp.wait()

    any_spec = pl.BlockSpec(memory_space=pl.ANY)
    return pl.pallas_call(
        body, name=name, in_specs=[any_spec] * n, out_specs=[any_spec] * n, out_shape=[SDS(a.shape, a.dtype) for a in arrs],
        scratch_shapes=[pltpu.SemaphoreType.DMA((n,)), pltpu.SemaphoreType.DMA((n,))],
    )(*arrs)


def swap_halves(zones, name):
    n = len(zones)

    def body(*refs):
        outs = refs[n:2 * n]
        send, recv = refs[2 * n:]
        x, y, c = lax.axis_index("x"), lax.axis_index("y"), lax.axis_index("c")
        cps = []
        for i in range(n):
            for j, (px, py) in enumerate(_chip_peers(x, y)):
                part = _my_half(outs[i].at[2 * px + py])
                cps.append(pltpu.make_async_remote_copy(src_ref=part, dst_ref=part, send_sem=send.at[3 * i + j],
                                                        recv_sem=recv.at[3 * i + j], device_id=(x, y, 1 - c), device_id_type=MESH))
        for cp in cps:
            cp.start()
        for cp in cps:
            cp.wait_send()
            cp.wait_recv()

    any_spec = pl.BlockSpec(memory_space=pl.ANY)
    return pl.pallas_call(
        body, name=name, in_specs=[any_spec] * n, out_specs=[any_spec] * n, out_shape=[SDS(a.shape, a.dtype) for a in zones],
        input_output_aliases={i: i for i in range(n)},
        scratch_shapes=[pltpu.SemaphoreType.DMA((3 * n,)), pltpu.SemaphoreType.DMA((3 * n,))],
    )(*zones)


def swap_other_halves(arrs, name):
    n = len(arrs)

    def body(*refs):
        ins, outs = refs[:n], refs[n:2 * n]
        send, recv = refs[2 * n:]
        x, y, c = lax.axis_index("x"), lax.axis_index("y"), lax.axis_index("c")
        cps = [pltpu.make_async_remote_copy(src_ref=ins[i].at[:, 1 - c], dst_ref=outs[i], send_sem=send.at[i], recv_sem=recv.at[i],
                                            device_id=(x, y, 1 - c), device_id_type=MESH) for i in range(n)]
        for cp in cps:
            cp.start()
        for cp in cps:
            cp.wait()

    any_spec = pl.BlockSpec(memory_space=pl.ANY)
    return pl.pallas_call(
        body, name=name, in_specs=[any_spec] * n, out_specs=[any_spec] * n,
        out_shape=[SDS((a.shape[0],) + a.shape[2:], a.dtype) for a in arrs],
        scratch_shapes=[pltpu.SemaphoreType.DMA((n,)), pltpu.SemaphoreType.DMA((n,))],
    )(*arrs)


def _ids_spec(grid, in_specs, out_specs):
    return pltpu.PrefetchScalarGridSpec(num_scalar_prefetch=1, grid=grid, in_specs=in_specs, out_specs=out_specs)


def pair_sum(ids, a, b, name, tr=512):
    nd, _, rows, cols = a.shape
    tr = min(tr, rows)
    assert rows % tr == 0

    def body(ids_ref, a_ref, b_ref, o_ref):
        o_ref[...] = (a_ref[0].astype(f32) + b_ref[...].astype(f32)).astype(bf16)

    spec = pl.BlockSpec((1, tr, cols), lambda d, i, ids: (d, i, 0))
    return pl.pallas_call(
        body, name=name,
        grid_spec=_ids_spec((nd, rows // tr), [pl.BlockSpec((1, 1, tr, cols), lambda d, i, ids: (d, ids[1], i, 0)), spec], spec),
        out_shape=SDS((nd, rows, cols), bf16), compiler_params=_cp(("arbitrary", "arbitrary")))(ids, a, b)


def allreduce_small(v):
    def body(v_ref, o_ref, rbuf, send, recv):
        x, y, c = lax.axis_index("x"), lax.axis_index("y"), lax.axis_index("c")
        o_ref[...] = v_ref[...]
        for s, peer in enumerate([(x, y, 1 - c), (1 - x, y, c), (x, 1 - y, c)]):
            cp = pltpu.make_async_remote_copy(src_ref=o_ref, dst_ref=rbuf.at[s], send_sem=send.at[s], recv_sem=recv.at[s],
                                              device_id=peer, device_id_type=MESH)
            cp.start()
            cp.wait()
            o_ref[...] = o_ref[...] + rbuf[s]

    vm = pl.BlockSpec(memory_space=pltpu.VMEM)
    return pl.pallas_call(
        body, name="allreduce_small", in_specs=[vm], out_specs=vm, out_shape=SDS(v.shape, f32),
        scratch_shapes=[pltpu.VMEM((3,) + v.shape, f32), pltpu.SemaphoreType.DMA((3,)), pltpu.SemaphoreType.DMA((3,))],
        compiler_params=_cp(),
    )(v)


def sum_partials(ids, zone, mine, name, tr=256):
    _, rows, cols = zone.shape
    tr = min(tr, rows)
    assert rows % tr == 0

    def body(ids_ref, m_ref, z1_ref, z2_ref, z3_ref, o_ref):
        o_ref[...] = ((m_ref[0].astype(f32) + z1_ref[0].astype(f32)) + z2_ref[0].astype(f32)) + z3_ref[0].astype(f32)

    slot = lambda flip: pl.BlockSpec((1, tr, cols), lambda i, ids: (ids[0] ^ flip, i, 0))
    return pl.pallas_call(
        body, name=name,
        grid_spec=_ids_spec((rows // tr,), [slot(0), slot(1), slot(2), slot(3)], pl.BlockSpec((tr, cols), lambda i, ids: (i, 0))),
        out_shape=SDS((rows, cols), f32), compiler_params=_cp(("arbitrary",)),
    )(ids, mine, zone, zone, zone)


def adamw(w, m, v, gs, name, layer=0, prev=None, tr=256):
    hrows, cols = gs[0].shape
    rows = hrows * len(gs)
    tr = min(tr, hrows)
    assert hrows % tr == 0 and w.shape[0] % rows == 0
    off, nth = layer * (rows // tr), hrows // tr

    def body(w_ref, m_ref, v_ref, *rest):
        g_ref, d_ref, mo_ref, vo_ref = rest[-4:]
        if len(gs) == 1:
            g = rest[0][...]
        else:
            g = jnp.where(pl.program_id(0) // nth == lax.axis_index("c"), rest[0][...], rest[1][...])
        mn = ADAM_B1 * m_ref[...] + (1.0 - ADAM_B1) * g
        vn = ADAM_B2 * v_ref[...] + (1.0 - ADAM_B2) * jnp.square(g)
        m_hat = mn / (1.0 - ADAM_B1 ** ADAM_STEP)
        v_hat = vn / (1.0 - ADAM_B2 ** ADAM_STEP)
        g_ref[...] = g
        d_ref[...] = -ADAM_LR * (m_hat / (jnp.sqrt(v_hat) + ADAM_EPS) + ADAM_WD * w_ref[...])
        mo_ref[...] = mn
        vo_ref[...] = vn

    loc = pl.BlockSpec((tr, cols), lambda i: (i % nth, 0))
    glob = pl.BlockSpec((tr, cols), lambda i: (off + i, 0))
    extra = [] if prev is None else list(prev)
    return pl.pallas_call(
        body, name=name, grid=(rows // tr,),
        in_specs=[glob] * 3 + [loc] * len(gs) + [pl.BlockSpec(memory_space=pl.ANY)] * len(extra),
        out_specs=[glob] * 4, out_shape=[SDS(w.shape, f32)] * 4,
        input_output_aliases={3 + len(gs) + j: j for j in range(len(extra))},
        compiler_params=_cp(("arbitrary",)),
    )(w, m, v, *gs, *extra)


BIG = ("w_in", "w_out", "w_ff1", "w_ff2")
SMALL = ("norm1_g", "conv_w", "a_log", "dt_bias", "dn_out_g", "sb_q_g", "sb_k_g", "sg_v_g", "sg_w", "sg_b", "norm2_g")
WEIGHTS = ("norm1_g", "w_in", "conv_w", "a_log", "dt_bias", "dn_out_g", "sb_q_g", "sb_k_g", "sg_v_g", "sg_w", "sg_b",
           "w_out", "norm2_g", "w_ff1", "w_ff2")


PACK_ROWS = 256


def _rows_of(shape):
    n = 1
    for d in shape:
        n *= d
    return -(-n // 1024) * 8, n


def _pack(arrs):
    parts = []
    for a in arrs:
        r, n = _rows_of(a.shape)
        parts.append(jnp.pad(a.reshape(-1), (0, r * 128 - n)).reshape(r, 128))
    rows = sum(p.shape[0] for p in parts)
    parts.append(jnp.zeros((-rows % PACK_ROWS, 128), arrs[0].dtype))
    return jnp.concatenate(parts, axis=0)


def _unpack(packed, shapes):
    out, o = [], 0
    for s in shapes:
        r, n = _rows_of(s)
        out.append(packed[o:o + r].reshape(-1)[0:n].reshape(s))
        o += r
    return out


def kernel(x, norm1_g, w_in, conv_w, a_log, dt_bias, dn_out_g, sb_q_g, sb_k_g, sg_v_g, sg_w, sg_b, w_out, norm2_g, w_ff1, w_ff2, loss_target, m_norm1_g, m_w_in, m_conv_w, m_a_log, m_dt_bias, m_dn_out_g, m_sb_q_g, m_sb_k_g, m_sg_v_g, m_sg_w, m_sg_b, m_w_out, m_norm2_g, m_w_ff1, m_w_ff2, v_norm1_g, v_w_in, v_conv_w, v_a_log, v_dt_bias, v_dn_out_g, v_sb_q_g, v_sb_k_g, v_sg_v_g, v_sg_w, v_sg_b, v_w_out, v_norm2_g, v_w_ff1, v_w_ff2):
    w = dict(norm1_g=norm1_g, w_in=w_in, conv_w=conv_w, a_log=a_log, dt_bias=dt_bias, dn_out_g=dn_out_g, sb_q_g=sb_q_g,
             sb_k_g=sb_k_g, sg_v_g=sg_v_g, sg_w=sg_w, sg_b=sg_b, w_out=w_out, norm2_g=norm2_g, w_ff1=w_ff1, w_ff2=w_ff2)
    mom = dict(norm1_g=m_norm1_g, w_in=m_w_in, conv_w=m_conv_w, a_log=m_a_log, dt_bias=m_dt_bias, dn_out_g=m_dn_out_g,
               sb_q_g=m_sb_q_g, sb_k_g=m_sb_k_g, sg_v_g=m_sg_v_g, sg_w=m_sg_w, sg_b=m_sg_b, w_out=m_w_out, norm2_g=m_norm2_g,
               w_ff1=m_w_ff1, w_ff2=m_w_ff2)
    var = dict(norm1_g=v_norm1_g, w_in=v_w_in, conv_w=v_conv_w, a_log=v_a_log, dt_bias=v_dt_bias, dn_out_g=v_dn_out_g,
               sb_q_g=v_sb_q_g, sb_k_g=v_sb_k_g, sg_v_g=v_sg_v_g, sg_w=v_sg_w, sg_b=v_sg_b, w_out=v_w_out, norm2_g=v_norm2_g,
               w_ff1=v_w_ff1, w_ff2=v_w_ff2)
    chip = 2 * lax.axis_index("x") + lax.axis_index("y")

    wb = {k: w[k].astype(bf16) for k in BIG}
    ag = exchange_start([(conv_w, None, None)] + [(wb[k], l, "rows") for l in range(DEPTH) for k in BIG],
                        "allgather_start", scatter=False)
    item = lambda l, k: 1 + l * len(BIG) + BIG.index(k)

    def landed(ks, after, name):
        zones = exchange_wait(ag, ks, after, name)
        halved = [t for t, k in enumerate(ks) if ag["halve"][k]]
        for t, z in zip(halved, swap_halves([zones[t] for t in halved], name.replace("wait", "pass"))):
            zones[t] = z
        own = [ag["src"][k][0] if ag["src"][k][1] is None else ag["src"][k][0][ag["src"][k][1]] for k in ks]
        return [lax.dynamic_update_slice_in_dim(z, o[None], chip, axis=0) for z, o in zip(zones, own)]

    def whole(k, z):
        if k == "w_in":
            return w_in_from_shards(z)
        if k == "w_ff1":
            return jnp.transpose(z, (1, 0, 2)).reshape(D_MODEL, D_FF)
        return z.reshape(-1, D_MODEL)

    g_conv, first_in = landed([0, item(0, "w_in")], x, "allgather_wait_in0")
    small = {k: w[k] for k in SMALL}
    small["conv_w"] = jnp.transpose(g_conv, (1, 2, 0, 3)).reshape(DEPTH, DN_CONV, 3 * DN_WIDTH)
    cache = {}

    def get_w(l, part, after):
        if part == "in":
            return whole("w_in", first_in if l == 0 else landed([item(l, "w_in")], after, f"allgather_wait_in{l}")[0])
        if part == "out":
            zs = landed([item(l, k) for k in ("w_out", "w_ff1", "w_ff2")], after, f"allgather_wait_rest{l}")
            cache[l] = (whole("w_ff1", zs[1]), whole("w_ff2", zs[2]))
            return whole("w_out", zs[0])
        return cache[l]

    rs = {}
    ids = jnp.stack([chip, lax.axis_index("c")]).astype(jnp.int32)

    def put_g(l, tag, g):
        names = [k for k in BIG if k in g]
        by_dest = [w_in_grad_to_shards(g[k]) if k == "w_in" else g[k] for k in names]
        halves = [a.reshape(N_CHIPS, 2, -1, a.shape[-1]) for a in by_dest]
        got = swap_other_halves(halves, f"pair_swap_{tag}{l}")
        pair = [pair_sum(ids, a, b, f"pair_sum_{k}_{l}") for k, a, b in zip(names, halves, got)]
        rs[l, tag] = dict(exchange_start([(a, None, None) for a in pair], f"scatter_start_{tag}{l}", scatter=True), names=names)
        return rs[l, tag]["token"][0, 0]

    lossp, grad_x, gsmall = local_step(x, loss_target, small, get_w, put_g)
    loss = lax.psum(jnp.sum(lossp), ("x", "y", "c"))

    def finish(l, tag, after, prev):
        st = rs[l, tag]
        ks = list(range(len(st["names"])))
        zones = exchange_wait(st, ks, after, f"scatter_wait_{tag}{l}")
        sums = [sum_partials(ids, zones[i], st["src"][i][0], f"sum_{k}_{l}") for i, k in enumerate(st["names"])]
        others = swap_cores(sums, f"swap_grad_sums_{tag}{l}")
        outs = dict(prev)
        for i, k in enumerate(st["names"]):
            r2 = lambda a: a.reshape(-1, a.shape[-1])
            outs[k] = adamw(r2(w[k]), r2(mom[k]), r2(var[k]), (sums[i], others[i]), f"adamw_{k}_{l}", layer=l, prev=prev.get(k))
        return outs

    done = finish(1, "rest", rs[0, "in"]["token"], {})
    done = finish(1, "in", done["w_ff2"][0], done)
    res = {}

    full_shapes = [(DEPTH,) + tuple(gsmall[0][k].shape) for k in SMALL]
    packed = _pack([jnp.stack([gsmall[l][k] for l in range(DEPTH)]) for k in SMALL])
    total = allreduce_small(packed)
    gfull = dict(zip(SMALL, _unpack(total, full_shapes)))
    cs = 3 * DN_WIDTH // N_CHIPS
    gfull["conv_w"] = lax.dynamic_slice_in_dim(gfull["conv_w"], chip * cs, cs, axis=2)
    gp, wp, mp, vp = (_pack([d[k] for k in SMALL]) for d in (gfull, w, mom, var))
    outs = adamw(wp, mp, vp, (gp,), "adamw_small")
    loc_shapes = [w[k].shape for k in SMALL]
    unp = [_unpack(o, loc_shapes) for o in outs]
    for i, k in enumerate(SMALL):
        res[k] = [unp[j][i] for j in range(4)]

    done = finish(0, "rest", outs[0], done)
    done = finish(0, "in", done["w_ff2"][0], done)
    for k in BIG:
        res[k] = [o.reshape(w[k].shape) for o in done[k]]

    return (loss, grad_x, *[res[k][0] for k in WEIGHTS], *[res[k][1] for k in WEIGHTS], *[res[k][2] for k in WEIGHTS],
            *[res[k][3] for k in WEIGHTS])
```

```python
import functools

import jax
import jax.numpy as jnp
from jax import lax
from jax.experimental import pallas as pl
from jax.experimental.pallas import tpu as pltpu

f32 = jnp.float32
bf16 = jnp.bfloat16
SDS = jax.ShapeDtypeStruct
MESH = pl.DeviceIdType.MESH

NORM_EPS = 1e-6
D_MODEL = 1024
DEPTH = 2
DN_HEADS, DN_DIM, DN_WIDTH, DN_CONV, DN_CHUNK = 4, 128, 512, 4, 64
SB_HEADS, SB_DIM, SB_WIDTH, SB_BLOCK = 4, 64, 256, 128
SG_GROUPS, SG_DIM, SG_WIDTH, SG_CHUNK = 4, 64, 256, 128
D_FF = 4096
IN_DIM = 3336
C_QKV, C_Z, C_AB, C_SB, C_SG, IN_PAD = 0, 1536, 2048, 2304, 3072, 3584
DN_COLS = C_SB
N_CHIPS = 4

ADAM_LR, ADAM_B1, ADAM_B2, ADAM_EPS, ADAM_WD, ADAM_STEP = 0.001, 0.9, 0.999, 1e-08, 0.01, 10

VMEM_LIMIT = 56 * 1024 * 1024


def _cp(sem=None, **kw):
    if sem is not None:
        kw["dimension_semantics"] = sem
    return pltpu.CompilerParams(vmem_limit_bytes=VMEM_LIMIT, **kw)


def _split2(x):
    hi = x.astype(bf16)
    lo = (x - hi.astype(f32)).astype(bf16)
    return hi, lo


NT = (((1,), (1,)), ((), ()))
TN = (((0,), (0,)), ((), ()))
_DIMS2 = dict(nn=(((1,), (0,)), ((), ())), nt=NT, tn=TN)
_DIMS3 = dict(nn=(((2,), (1,)), ((0,), (0,))), nt=(((2,), (2,)), ((0,), (0,))), tn=(((1,), (1,)), ((0,), (0,))))


def _dg(a, b, kind):
    return lax.dot_general(a, b, (_DIMS2 if a.ndim == 2 else _DIMS3)[kind], preferred_element_type=f32)


def _pdot(a, b):
    return _dg(a, b, "nn")


def _dot_hp(a, b):
    ah, al = _split2(a)
    bh, bl = _split2(b)
    return _pdot(ah, bh) + _pdot(ah, bl) + _pdot(al, bh)


def _dot_x2c(a, m):
    lead = a.shape[:-1]
    ah, al = _split2(a.reshape(-1, a.shape[-1]))
    return (_pdot(ah, m) + _pdot(al, m)).reshape(lead + (m.shape[1],))


def _dot_cx2(m, a):
    if a.ndim == 3:
        m = jnp.broadcast_to(m, (a.shape[0],) + m.shape)
    ah, al = _split2(a)
    return _pdot(m, ah) + _pdot(m, al)


def _nt(a, b):
    return _dg(a.astype(bf16), b.astype(bf16), "nt")


def _tn(a, b):
    return _dg(a.astype(bf16), b.astype(bf16), "tn")


def _nn(a, b):
    return _dg(a.astype(bf16), b.astype(bf16), "nn")


@jax.custom_vjp
def mm(a, b):
    return _nn(a, b)


mm.defvjp(lambda a, b: (_nn(a, b), (a, b)), lambda r, g: (_nt(g, r[1]), _tn(r[0], g)))


@jax.custom_vjp
def mm_nt(a, b):
    return _nt(a, b)


mm_nt.defvjp(lambda a, b: (_nt(a, b), (a, b)), lambda r, g: (_nn(g, r[1]), _tn(g, r[0])))


@jax.custom_vjp
def mm_tn(a, b):
    return _tn(a, b)


mm_tn.defvjp(lambda a, b: (_tn(a, b), (a, b)), lambda r, g: (_nt(r[1], g), _nn(r[0], g)))


@jax.custom_vjp
def rmul_const(a, m, mt):
    return _dot_x2c(a, m)


rmul_const.defvjp(lambda a, m, mt: (_dot_x2c(a, m), (m, mt)),
                  lambda r, g: (_dot_x2c(g, r[1]), jnp.zeros_like(r[0]), jnp.zeros_like(r[1])))


@jax.custom_vjp
def lmul_const(m, mt, a):
    return _dot_cx2(m, a)


lmul_const.defvjp(lambda m, mt, a: (_dot_cx2(m, a), (m, mt)),
                  lambda r, g: (jnp.zeros_like(r[0]), jnp.zeros_like(r[1]), _dot_cx2(r[1], g)))


@jax.custom_vjp
def mm_hl(t, x):
    th, tl = _split2(t)
    xb = x.astype(bf16)
    return _pdot(th, xb) + _pdot(tl, xb)


def _mm_hl_bwd(r, g):
    t, x = r
    th, tl = _split2(t)
    gb = g.astype(bf16)
    return _nt(g, x), _dg(th, gb, "tn") + _dg(tl, gb, "tn")


mm_hl.defvjp(lambda t, x: (mm_hl(t, x), (t, x)), _mm_hl_bwd)


def inv_unit_lower(lm):
    c = lm.shape[-1]
    r, cc = _iota2((c, c))
    eye = (r == cc).astype(f32)
    t = eye - lm
    p = -lm
    k = 1
    while 2 * k < c:
        p = _nn(p, p)
        t = t + _nn(t, p)
        k *= 2
    res = eye - t - _dot_hp(lm, t)
    return t + _nn(t, res)


@jax.custom_vjp
def inv_given(lm, t):
    return t


inv_given.defvjp(lambda lm, t: (t, t), lambda t, g: (-_nt(_tn(t, g), t), jnp.zeros_like(t)))


def _sigmoid(x):
    return 1.0 / (1.0 + jnp.exp(-x))


def _softplus(x):
    return jnp.maximum(x, 0.0) + jnp.log(1.0 + jnp.exp(-jnp.abs(x)))


def _silu(x):
    return x * _sigmoid(x)


def _gelu(x):
    return 0.5 * x * (1.0 + jnp.tanh(0.7978845608028654 * (x + 0.044715 * (x * x * x))))


def _iota2(shape):
    return lax.broadcasted_iota(jnp.int32, shape, 0), lax.broadcasted_iota(jnp.int32, shape, 1)


def _group_avg_mats():
    r, c = _iota2((128, 128))
    return jnp.where((r // 64) == (c // 64), 1.0 / 64.0, 0.0).astype(bf16)


def _pair_norm(x, gain, bavg):
    ms = rmul_const(x * x, bavg, bavg)
    return x * lax.rsqrt(ms + NORM_EPS) * gain


def _rms(x):
    r = lax.rsqrt(jnp.mean(x * x, axis=-1, keepdims=True) + NORM_EPS)
    return r


_IN_GROUPS = ((C_QKV, C_Z), (C_Z, C_AB), (C_AB, C_AB + 128), (C_SB, C_SG), (C_SG, IN_PAD))


def inproj_fwd(x, g, wp, tm=256):
    m = x.shape[0]

    def body(x_ref, g_ref, w_ref, *outs):
        xv = x_ref[...]
        h = (xv * _rms(xv) * g_ref[...]).astype(bf16)
        for (a, b), o in zip(_IN_GROUPS, outs):
            o[...] = _pdot(h, w_ref[:, a:b])

    return pl.pallas_call(
        body, name="inproj_fwd", grid=(m // tm,),
        in_specs=[pl.BlockSpec((tm, D_MODEL), lambda i: (i, 0)), pl.BlockSpec((1, D_MODEL), lambda i: (0, 0)),
                  pl.BlockSpec((D_MODEL, IN_PAD), lambda i: (0, 0))],
        out_specs=[pl.BlockSpec((tm, b - a), lambda i: (i, 0)) for a, b in _IN_GROUPS],
        out_shape=[SDS((m, b - a), f32) for a, b in _IN_GROUPS],
        compiler_params=_cp(("arbitrary",)),
    )(x, g, wp)


def inproj_bwd(x, g, wp, dproj, dres, tm=256):
    m = x.shape[0]

    def body(x_ref, g_ref, w_ref, dp_ref, dr_ref, dx_ref, dg_ref, h_ref):
        xv = x_ref[...]
        r = _rms(xv)
        xn = xv * r
        gv = g_ref[...]
        h_ref[...] = (xn * gv).astype(bf16)
        dh = lax.dot_general(dp_ref[...], w_ref[...], NT, preferred_element_type=f32)
        dxn = dh * gv
        dx_ref[...] = dr_ref[...] + r * (dxn - xn * jnp.mean(dxn * xn, axis=-1, keepdims=True))

        @pl.when(pl.program_id(0) == 0)
        def _():
            dg_ref[...] = jnp.zeros_like(dg_ref)

        dg_ref[...] += jnp.sum(dh * xn, axis=0, keepdims=True)

    return pl.pallas_call(
        body, name="inproj_bwd", grid=(m // tm,),
        in_specs=[pl.BlockSpec((tm, D_MODEL), lambda i: (i, 0)), pl.BlockSpec((1, D_MODEL), lambda i: (0, 0)),
                  pl.BlockSpec((D_MODEL, IN_PAD), lambda i: (0, 0)), pl.BlockSpec((tm, IN_PAD), lambda i: (i, 0)),
                  pl.BlockSpec((tm, D_MODEL), lambda i: (i, 0))],
        out_specs=[pl.BlockSpec((tm, D_MODEL), lambda i: (i, 0)), pl.BlockSpec((1, D_MODEL), lambda i: (0, 0)),
                   pl.BlockSpec((tm, D_MODEL), lambda i: (i, 0))],
        out_shape=[SDS((m, D_MODEL), f32), SDS((1, D_MODEL), f32), SDS((m, D_MODEL), bf16)],
        compiler_params=_cp(("arbitrary",)),
    )(x, g, wp, dproj, dres)


def outproj_fwd(x, odn, osb, osg, wo, tm=512):
    m = x.shape[0]

    def body(x_ref, a_ref, b_ref, c_ref, w_ref, x2_ref, mix_ref):
        mix_ref[:, 0:DN_WIDTH] = a_ref[...].astype(bf16)
        mix_ref[:, DN_WIDTH:DN_WIDTH + SB_WIDTH] = b_ref[...].astype(bf16)
        mix_ref[:, DN_WIDTH + SB_WIDTH:D_MODEL] = c_ref[...].astype(bf16)
        x2_ref[...] = x_ref[...] + _pdot(mix_ref[...], w_ref[...])

    row = lambda w: pl.BlockSpec((tm, w), lambda i: (i, 0))
    return pl.pallas_call(
        body, name="outproj_fwd", grid=(m // tm,),
        in_specs=[row(D_MODEL), row(DN_WIDTH), row(SB_WIDTH), row(SG_WIDTH), pl.BlockSpec((D_MODEL, D_MODEL), lambda i: (0, 0))],
        out_specs=[row(D_MODEL), row(D_MODEL)],
        out_shape=[SDS((m, D_MODEL), f32), SDS((m, D_MODEL), bf16)],
        compiler_params=_cp(("arbitrary",)),
    )(x, odn, osb, osg, wo)


def outproj_bwd(dx2, wo, tm=512):
    m = dx2.shape[0]

    def body(d_ref, w_ref, a_ref, b_ref, c_ref, db_ref):
        db = d_ref[...].astype(bf16)
        db_ref[...] = db
        dm = lax.dot_general(db, w_ref[...], NT, preferred_element_type=f32)
        a_ref[...] = dm[:, 0:DN_WIDTH]
        b_ref[...] = dm[:, DN_WIDTH:DN_WIDTH + SB_WIDTH]
        c_ref[...] = dm[:, DN_WIDTH + SB_WIDTH:D_MODEL]

    row = lambda w: pl.BlockSpec((tm, w), lambda i: (i, 0))
    return pl.pallas_call(
        body, name="outproj_bwd", grid=(m // tm,),
        in_specs=[row(D_MODEL), pl.BlockSpec((D_MODEL, D_MODEL), lambda i: (0, 0))],
        out_specs=[row(DN_WIDTH), row(SB_WIDTH), row(SG_WIDTH), row(D_MODEL)],
        out_shape=[SDS((m, DN_WIDTH), f32), SDS((m, SB_WIDTH), f32), SDS((m, SG_WIDTH), f32), SDS((m, D_MODEL), bf16)],
        compiler_params=_cp(("arbitrary",)),
    )(dx2, wo)


FF_CHUNK = 1024


def _load_weights_once(pairs, sem):
    @pl.when(pl.program_id(0) == 0)
    def _():
        cps = [pltpu.make_async_copy(h, v, sem.at[i]) for i, (h, v) in enumerate(pairs)]
        for c in cps:
            c.start()
        for c in cps:
            c.wait()


def ffn_fwd(x2, g, w1, w2, tm=256):
    m = x2.shape[0]

    def body(x_ref, g_ref, w1_hbm, w2_hbm, y_ref, rl_ref, w1_v, w2_v, sem):
        _load_weights_once(((w1_hbm, w1_v), (w2_hbm, w2_v)), sem)
        xv = x_ref[...]
        h = (xv * _rms(xv) * g_ref[...]).astype(bf16)
        acc = xv
        for j in range(0, D_FF, FF_CHUNK):
            f = _pdot(h, w1_v[:, j:j + FF_CHUNK])
            rl = jnp.maximum(f, 0.0)
            rl_ref[:, j:j + FF_CHUNK] = rl.astype(bf16)
            acc = acc + _pdot((rl * rl).astype(bf16), w2_v[j:j + FF_CHUNK, :])
        y_ref[...] = acc

    return pl.pallas_call(
        body, name="ffn_fwd", grid=(m // tm,),
        in_specs=[pl.BlockSpec((tm, D_MODEL), lambda i: (i, 0)), pl.BlockSpec((1, D_MODEL), lambda i: (0, 0)),
                  pl.BlockSpec(memory_space=pl.ANY), pl.BlockSpec(memory_space=pl.ANY)],
        out_specs=[pl.BlockSpec((tm, D_MODEL), lambda i: (i, 0)), pl.BlockSpec((tm, D_FF), lambda i: (i, 0))],
        out_shape=[SDS((m, D_MODEL), f32), SDS((m, D_FF), bf16)],
        scratch_shapes=[pltpu.VMEM((D_MODEL, D_FF), bf16), pltpu.VMEM((D_FF, D_MODEL), bf16), pltpu.SemaphoreType.DMA((2,))],
        compiler_params=_cp(("arbitrary",)),
    )(x2, g, w1, w2)


def ffn_bwd(x2, g, w1, w2, rlb, dy, tm=256):
    m = x2.shape[0]

    def body(x_ref, g_ref, w1_hbm, w2_hbm, rl_ref, dy_ref, dx_ref, dg_ref, h_ref, a_ref, df_ref, dyb_ref, w1_v, w2_v, sem):
        _load_weights_once(((w1_hbm, w1_v), (w2_hbm, w2_v)), sem)
        xv = x_ref[...]
        r = _rms(xv)
        xn = xv * r
        gv = g_ref[...]
        h = (xn * gv).astype(bf16)
        h_ref[...] = h
        dyv = dy_ref[...]
        dyb = dyv.astype(bf16)
        dyb_ref[...] = dyb
        dh = jnp.zeros((tm, D_MODEL), f32)
        for j in range(0, D_FF, FF_CHUNK):
            rl = rl_ref[:, j:j + FF_CHUNK].astype(f32)
            a_ref[:, j:j + FF_CHUNK] = (rl * rl).astype(bf16)
            da = lax.dot_general(dyb, w2_v[j:j + FF_CHUNK, :], NT, preferred_element_type=f32)
            df = (da * (2.0 * rl)).astype(bf16)
            df_ref[:, j:j + FF_CHUNK] = df
            dh = dh + lax.dot_general(df, w1_v[:, j:j + FF_CHUNK], NT, preferred_element_type=f32)
        dxn = dh * gv
        dx_ref[...] = dyv + r * (dxn - xn * jnp.mean(dxn * xn, axis=-1, keepdims=True))

        @pl.when(pl.program_id(0) == 0)
        def _():
            dg_ref[...] = jnp.zeros_like(dg_ref)

        dg_ref[...] += jnp.sum(dh * xn, axis=0, keepdims=True)

    row = lambda w: pl.BlockSpec((tm, w), lambda i: (i, 0))
    return pl.pallas_call(
        body, name="ffn_bwd", grid=(m // tm,),
        in_specs=[row(D_MODEL), pl.BlockSpec((1, D_MODEL), lambda i: (0, 0)),
                  pl.BlockSpec(memory_space=pl.ANY), pl.BlockSpec(memory_space=pl.ANY), row(D_FF), row(D_MODEL)],
        out_specs=[row(D_MODEL), pl.BlockSpec((1, D_MODEL), lambda i: (0, 0)), row(D_MODEL), row(D_FF), row(D_FF), row(D_MODEL)],
        out_shape=[SDS((m, D_MODEL), f32), SDS((1, D_MODEL), f32), SDS((m, D_MODEL), bf16), SDS((m, D_FF), bf16),
                   SDS((m, D_FF), bf16), SDS((m, D_MODEL), bf16)],
        scratch_shapes=[pltpu.VMEM((D_MODEL, D_FF), bf16), pltpu.VMEM((D_FF, D_MODEL), bf16), pltpu.SemaphoreType.DMA((2,))],
        compiler_params=_cp(("arbitrary",)),
    )(x2, g, w1, w2, rlb, dy)


def _tile(n, cap):
    best = 128
    for t in range(128, cap + 1, 128):
        if n % t == 0:
            best = t
    return best


def tn_matmul(a, b, name, col_shards=1, tk=2048):
    m, ka = a.shape
    n = b.shape[1]
    ti = _tile(ka, 1024)
    tj = _tile(n // col_shards, 1152)
    tk = min(tk, m)
    nk = m // tk
    jps = (n // col_shards) // tj

    def body(a_ref, b_ref, o_ref, acc):
        k = pl.program_id(2)

        @pl.when(k == 0)
        def _():
            acc[...] = jnp.zeros_like(acc)

        acc[...] += lax.dot_general(a_ref[...], b_ref[...], TN, preferred_element_type=f32)

        @pl.when(k == nk - 1)
        def _():
            o_ref[...] = acc[...].astype(bf16).reshape(o_ref.shape)

    if col_shards == 1:
        out_shape, out_spec = SDS((ka, n), bf16), pl.BlockSpec((ti, tj), lambda i, j, k: (i, j))
    else:
        out_shape = SDS((col_shards, ka, n // col_shards), bf16)
        out_spec = pl.BlockSpec((1, ti, tj), lambda i, j, k: (j // jps, i, j % jps))
    return pl.pallas_call(
        body, name=name, grid=(ka // ti, n // tj, nk),
        in_specs=[pl.BlockSpec((tk, ti), lambda i, j, k: (k, i)), pl.BlockSpec((tk, tj), lambda i, j, k: (k, j))],
        out_specs=out_spec, out_shape=out_shape,
        scratch_shapes=[pltpu.VMEM((ti, tj), f32)],
        compiler_params=_cp(("arbitrary", "arbitrary", "arbitrary")),
    )(a, b)


def loss_head(y, tgt, tm=512):
    m = y.shape[0]

    def body(y_ref, t_ref, dy_ref, l_ref):
        e = y_ref[...] - t_ref[...]
        dy_ref[...] = e * (1.0 / D_MODEL)

        @pl.when(pl.program_id(0) == 0)
        def _():
            l_ref[...] = jnp.zeros_like(l_ref)

        l_ref[...] += jnp.sum(e * e, axis=0, keepdims=True) * (0.5 / D_MODEL)

    row = pl.BlockSpec((tm, D_MODEL), lambda i: (i, 0))
    return pl.pallas_call(
        body, name="loss_head", grid=(m // tm,), in_specs=[row, row],
        out_specs=[row, pl.BlockSpec((1, D_MODEL), lambda i: (0, 0))],
        out_shape=[SDS((m, D_MODEL), f32), SDS((1, D_MODEL), f32)],
        compiler_params=_cp(("arbitrary",)),
    )(y, tgt)


def _dn_consts():
    c = DN_CHUNK
    r, cc = _iota2((c, c))
    lt = (cc <= r).astype(bf16)
    ltt = (r <= cc).astype(bf16)
    return lt, ltt


def dn_chunk(cq, ck, cv, g, beta, z, s, gain, lt, ltt, t_given=None):
    c = DN_CHUNK
    r, cc = _iota2((c, c))
    q = cq * lax.rsqrt(jnp.sum(cq * cq, axis=-1, keepdims=True) + NORM_EPS) * (DN_DIM ** -0.5)
    k = ck * lax.rsqrt(jnp.sum(ck * ck, axis=-1, keepdims=True) + NORM_EPS)
    r2, c2 = _iota2((c, 128))
    uaug = jnp.where((c2 < c) & (r2 > c2), 1.0, 0.0) + jnp.where(c2 == c, 1.0, 0.0)
    gam_all = lmul_const(lt, ltt, g * uaug)
    gam_cc = gam_all[:, :, 0:c]
    gam = gam_all[:, :, c:c + 1]
    dec = jnp.where(cc <= r, jnp.exp(jnp.where(cc <= r, gam_cc, 0.0)), 0.0)
    kk = mm_nt(k, k)
    lm = jnp.where(cc < r, beta * kk * dec, 0.0)
    t = inv_unit_lower(lm) if t_given is None else inv_given(lm, t_given)
    eg = jnp.exp(gam)
    sol = mm_hl(t, jnp.concatenate([cv * beta, k * (beta * eg)], axis=2))
    u, w = sol[:, :, 0:DN_DIM], sol[:, :, DN_DIM:2 * DN_DIM]
    qk = jnp.where(cc <= r, mm_nt(q, k) * dec, 0.0)
    glast = jnp.sum(g, axis=1, keepdims=True)
    qd = q * eg
    kd = k * jnp.exp(glast - gam)
    un = u - mm(w, s)
    o = mm(qd, s) + mm(qk, un)
    s_new = s * jnp.exp(glast) + mm_tn(kd, un)
    on = o * lax.rsqrt(jnp.mean(o * o, axis=-1, keepdims=True) + NORM_EPS) * gain * _silu(z)
    return on, s_new, t


def _dn_gates(ab, al_row, dt_row):
    pre = ab + dt_row
    return -jnp.exp(al_row) * _softplus(pre), _sigmoid(ab), _sigmoid(pre)


def _dn_chains(cacts, gates, z_ref):
    cq, ck, cv, g, beta, z = [], [], [], [], [], []
    for bi, cact in enumerate(cacts):
        for h in range(DN_HEADS):
            cq.append(cact[:, h * DN_DIM:(h + 1) * DN_DIM])
            ck.append(cact[:, DN_WIDTH + h * DN_DIM:DN_WIDTH + (h + 1) * DN_DIM])
            cv.append(cact[:, 2 * DN_WIDTH + h * DN_DIM:2 * DN_WIDTH + (h + 1) * DN_DIM])
            g.append(gates[bi][0][:, h:h + 1])
            beta.append(gates[bi][1][:, DN_HEADS + h:DN_HEADS + h + 1])
            z.append(z_ref[bi, :, h * DN_DIM:(h + 1) * DN_DIM])
    return tuple(jnp.stack(v) for v in (cq, ck, cv, g, beta, z))


def _conv_rows(xe_ref, b, w_ref):
    y = w_ref[0:1, :] * xe_ref[b, pl.ds(5, DN_CHUNK), :]
    for i in range(1, DN_CONV):
        y = y + w_ref[i:i + 1, :] * xe_ref[b, pl.ds(5 + i, DN_CHUNK), :]
    return y


def dn_fwd(qkv, z, ab, conv_w, alog, dtb, gain):
    bsz, t, _ = qkv.shape
    nc = t // DN_CHUNK
    c = DN_CHUNK
    nh = bsz * DN_HEADS

    def body(qkv_ref, z_ref, ab_ref, w_ref, al_ref, dt_ref, g_ref, o_ref, sall_ref, tall_ref, xe, s_sc):
        n = pl.program_id(0)

        @pl.when(n == 0)
        def _():
            xe[:, 0:8, :] = jnp.zeros((bsz, 8, 3 * DN_WIDTH), f32)
            s_sc[...] = jnp.zeros_like(s_sc)

        lt, ltt = _dn_consts()
        cacts = []
        for b in range(bsz):
            xe[b, 8:8 + c, :] = qkv_ref[b]
            cacts.append(_silu(_conv_rows(xe, b, w_ref)))
            xe[b, 0:8, :] = xe[b, c:c + 8, :]
        gates = [_dn_gates(ab_ref[b], al_ref[...], dt_ref[...]) for b in range(bsz)]
        s = s_sc[...]
        sall_ref[0] = s
        on, sn, tt = dn_chunk(*_dn_chains(cacts, gates, z_ref), s, g_ref[...], lt, ltt)
        tall_ref[0] = tt
        s_sc[...] = sn
        for b in range(bsz):
            for h in range(DN_HEADS):
                o_ref[b, :, h * DN_DIM:(h + 1) * DN_DIM] = on[b * DN_HEADS + h]

    blk = lambda w: pl.BlockSpec((bsz, c, w), lambda n: (0, n, 0))
    full = lambda shp: pl.BlockSpec(shp, lambda n: (0,) * len(shp))
    return pl.pallas_call(
        body, name="dn_fwd", grid=(nc,),
        in_specs=[blk(3 * DN_WIDTH), blk(DN_WIDTH), blk(128), full((8, 3 * DN_WIDTH)), full((1, 128)), full((1, 128)), full((1, 128))],
        out_specs=[blk(DN_WIDTH), pl.BlockSpec((1, nh, DN_DIM, DN_DIM), lambda n: (n, 0, 0, 0)),
                   pl.BlockSpec((1, nh, c, c), lambda n: (n, 0, 0, 0))],
        out_shape=[SDS((bsz, t, DN_WIDTH), f32), SDS((nc, nh, DN_DIM, DN_DIM), f32), SDS((nc, nh, c, c), f32)],
        scratch_shapes=[pltpu.VMEM((bsz, c + 8, 3 * DN_WIDTH), f32), pltpu.VMEM((nh, DN_DIM, DN_DIM), f32)],
        compiler_params=_cp(("arbitrary",)),
    )(qkv, z, ab, conv_w, alog, dtb, gain)


def dn_bwd(qkv, z, ab, conv_w, alog, dtb, gain, sall, tall, do):
    bsz, t, _ = qkv.shape
    nc = t // DN_CHUNK
    c = DN_CHUNK
    nh = bsz * DN_HEADS
    w3 = 3 * DN_WIDTH

    def body(qkv_ref, prev_ref, z_ref, ab_ref, w_ref, al_ref, dt_ref, g_ref, sall_ref, tall_ref, do_ref,
             dp_ref, dw_ref, dal_ref, ddt_ref, dg_ref, xe, dye, dc_sc, ds_sc):
        n = pl.program_id(0)
        first = (nc - 1 - n) == 0

        @pl.when(n == 0)
        def _():
            dye[:, c:c + 8, :] = jnp.zeros((bsz, 8, w3), f32)
            ds_sc[...] = jnp.zeros_like(ds_sc)
            dw_ref[...] = jnp.zeros_like(dw_ref)
            dal_ref[...] = jnp.zeros_like(dal_ref)
            ddt_ref[...] = jnp.zeros_like(ddt_ref)
            dg_ref[...] = jnp.zeros_like(dg_ref)

        lt, ltt = _dn_consts()
        lane = lax.broadcasted_iota(jnp.int32, (1, 128), 1)
        lane_c = lax.broadcasted_iota(jnp.int32, (c, 128), 1)
        ys, sigs = [], []
        for b in range(bsz):
            xe[b, 0:8, :] = jnp.where(first, 0.0, prev_ref[b])
            xe[b, 8:8 + c, :] = qkv_ref[b]
            ys.append(_conv_rows(xe, b, w_ref))
            sigs.append(_sigmoid(ys[b]))
        gates = [_dn_gates(ab_ref[b], al_ref[...], dt_ref[...]) for b in range(bsz)]
        ops = _dn_chains([y * sg for y, sg in zip(ys, sigs)], gates, z_ref)
        tt = tall_ref[0]
        _, vjp = jax.vjp(lambda *p: dn_chunk(*p, lt, ltt, t_given=tt)[0:2], *ops, sall_ref[0], g_ref[...])
        don = jnp.stack([do_ref[b, :, h * DN_DIM:(h + 1) * DN_DIM] for b in range(bsz) for h in range(DN_HEADS)])
        dcq, dck, dcv, dg, dbeta, dzz, dsp, dgn = vjp((don, ds_sc[...]))
        ds_sc[...] = dsp
        dg_ref[...] += dgn
        for b in range(bsz):
            dgate = jnp.zeros((c, 128), f32)
            for h in range(DN_HEADS):
                i = b * DN_HEADS + h
                dc_sc[b, :, h * DN_DIM:(h + 1) * DN_DIM] = dcq[i]
                dc_sc[b, :, DN_WIDTH + h * DN_DIM:DN_WIDTH + (h + 1) * DN_DIM] = dck[i]
                dc_sc[b, :, 2 * DN_WIDTH + h * DN_DIM:2 * DN_WIDTH + (h + 1) * DN_DIM] = dcv[i]
                dp_ref[b, :, C_Z + h * DN_DIM:C_Z + (h + 1) * DN_DIM] = dzz[i].astype(bf16)
                dgate = dgate + jnp.where(lane_c == h, dg[i], 0.0) + jnp.where(lane_c == DN_HEADS + h, dbeta[i], 0.0)
            gg, beta, sig_pre = gates[b]
            is_g = lane_c < DN_HEADS
            dpre = jnp.where(is_g, dgate * (-jnp.exp(al_ref[...])) * sig_pre, 0.0)
            dp_ref[b, :, C_AB:C_AB + 128] = (dpre + jnp.where(is_g, 0.0, dgate * beta * (1.0 - beta))).astype(bf16)
            dp_ref[b, :, C_AB + 128:DN_COLS] = jnp.zeros((c, DN_COLS - C_AB - 128), bf16)
            dal_ref[...] += jnp.sum(jnp.where(is_g, dgate * gg, 0.0), axis=0, keepdims=True)
            ddt_ref[...] += jnp.sum(dpre, axis=0, keepdims=True)
            y, sig = ys[b], sigs[b]
            dy = dc_sc[b] * (sig * (1.0 + y * (1.0 - sig)))
            dye[b, 0:c, :] = dy
            dx = w_ref[3:4, :] * dy
            for i in range(DN_CONV - 1):
                dx = dx + w_ref[i:i + 1, :] * dye[b, pl.ds(3 - i, c), :]
            dp_ref[b, :, 0:w3] = dx.astype(bf16)
            for i in range(DN_CONV):
                dw_ref[i:i + 1, :] += jnp.sum(dy * xe[b, pl.ds(5 + i, c), :], axis=0, keepdims=True)
            dye[b, c:c + 8, :] = dye[b, 0:8, :]

    rev = lambda w: pl.BlockSpec((bsz, c, w), lambda n: (0, nc - 1 - n, 0))
    full = lambda shp: pl.BlockSpec(shp, lambda n: (0,) * len(shp))
    prev = pl.BlockSpec((bsz, 8, w3), lambda n: (0, jnp.maximum((nc - 1 - n) * (c // 8) - 1, 0), 0))
    return pl.pallas_call(
        body, name="dn_bwd", grid=(nc,),
        in_specs=[rev(w3), prev, rev(DN_WIDTH), rev(128), full((8, w3)), full((1, 128)), full((1, 128)), full((1, 128)),
                  pl.BlockSpec((1, nh, DN_DIM, DN_DIM), lambda n: (nc - 1 - n, 0, 0, 0)),
                  pl.BlockSpec((1, nh, c, c), lambda n: (nc - 1 - n, 0, 0, 0)), rev(DN_WIDTH)],
        out_specs=[rev(DN_COLS), full((8, w3)), full((1, 128)), full((1, 128)), full((1, 128))],
        out_shape=[SDS((bsz, t, IN_PAD), bf16), SDS((8, w3), f32), SDS((1, 128), f32), SDS((1, 128), f32), SDS((1, 128), f32)],
        scratch_shapes=[pltpu.VMEM((bsz, c + 8, w3), f32), pltpu.VMEM((bsz, c + 8, w3), f32), pltpu.VMEM((bsz, c, w3), f32),
                        pltpu.VMEM((nh, DN_DIM, DN_DIM), f32)],
        compiler_params=_cp(("arbitrary",)),
    )(qkv, qkv, z, ab, conv_w, alog, dtb, gain, sall, tall, do)


SB_TILE = 256
SB_QTILE, SB_KTILE = 256, 256
SB_PAIRS = SB_HEADS // 2


def sb_fwd(sbqkv, gq, gk):
    bsz, t, _ = sbqkv.shape
    bq = min(SB_QTILE, t)
    blk = max(min(SB_KTILE, t), bq)
    nq = t // bq
    scale = SB_DIM ** -0.5

    def body(q_ref, k_ref, v_ref, gq_ref, gk_ref, o_ref, l_ref, q2_sc, kn_sc, v_sc):
        bavg = _group_avg_mats()
        lane = lax.broadcasted_iota(jnp.int32, (1, 128), 1)
        first = lane < SB_DIM
        for p in range(SB_PAIRS):
            ls = slice(p * 128, (p + 1) * 128)
            qn = _pair_norm(q_ref[0, :, ls], gq_ref[...], bavg)
            kn_sc[p] = _pair_norm(k_ref[0, :, ls], gk_ref[...], bavg).astype(bf16)
            v_sc[p] = v_ref[0, :, ls].astype(bf16)
            q2_sc[2 * p] = jnp.where(first, qn, 0.0).astype(bf16)
            q2_sc[2 * p + 1] = jnp.where(first, 0.0, qn).astype(bf16)
        r, c = _iota2((blk, blk))
        ustrict = (r > c).astype(bf16)
        r2, c2 = _iota2((2 * bq, blk))

        def tile(q2s, ks, carry, causal):
            out = []
            for p in range(SB_PAIRS):
                acc, rr = carry[2 * p], carry[2 * p + 1]
                zz = lax.dot_general(q2s[p], kn_sc[p, pl.ds(ks, blk), :], NT, preferred_element_type=f32) * scale
                sp = _softplus(zz)
                lm = -sp if causal is None else jnp.where(causal, -sp, 0.0)
                rem = _dot_x2c(lm, ustrict)
                wgt = jnp.exp(zz - sp + rem + rr)
                if causal is not None:
                    wgt = jnp.where(causal, wgt, 0.0)
                out += [acc + _pdot(wgt.astype(bf16), v_sc[p, pl.ds(ks, blk), :]), rr + jnp.sum(lm, axis=1, keepdims=True)]
            return tuple(out)

        def qloop(qi, _):
            qs = pl.multiple_of(qi * bq, bq)
            kd = qs // blk
            causal = c2 < (r2 & (bq - 1)) + (qs - kd * blk)
            q2s = [jnp.concatenate([q2_sc[2 * p, pl.ds(qs, bq), :], q2_sc[2 * p + 1, pl.ds(qs, bq), :]], axis=0)
                   for p in range(SB_PAIRS)]
            zero = (jnp.zeros((2 * bq, 128), f32), jnp.zeros((2 * bq, 1), f32)) * SB_PAIRS
            carry = lax.fori_loop(1, kd + 1, lambda i, cr: tile(q2s, pl.multiple_of((kd - i) * blk, blk), cr, None),
                                  tile(q2s, pl.multiple_of(kd * blk, blk), zero, causal))
            for p in range(SB_PAIRS):
                acc, rr = carry[2 * p], carry[2 * p + 1]
                o_ref[0, pl.ds(qs, bq), p * 128:(p + 1) * 128] = jnp.where(first, acc[0:bq], acc[bq:2 * bq])
                l_ref[0, pl.ds(qs, bq), p * 128:(p + 1) * 128] = jnp.where(first, rr[0:bq], rr[bq:2 * bq])
            return 0

        lax.fori_loop(0, nq, qloop, 0)

    col = lambda off: pl.BlockSpec((1, t, SB_WIDTH), lambda b: (b, 0, off))
    gsp = pl.BlockSpec((1, 128), lambda b: (0, 0))
    return pl.pallas_call(
        body, name="sb_fwd", grid=(bsz,),
        in_specs=[col(0), col(1), col(2), gsp, gsp],
        out_specs=[col(0), col(0)],
        out_shape=[SDS((bsz, t, SB_WIDTH), f32), SDS((bsz, t, SB_WIDTH), f32)],
        scratch_shapes=[pltpu.VMEM((2 * SB_PAIRS, t, 128), bf16), pltpu.VMEM((SB_PAIRS, t, 128), bf16),
                        pltpu.VMEM((SB_PAIRS, t, 128), bf16)],
        compiler_params=_cp(("arbitrary",)),
    )(sbqkv, sbqkv, sbqkv, gq, gk)


def sb_bwd(sbqkv, gq, gk, ltot, do, dproj):
    bsz, t, _ = sbqkv.shape
    blk = min(SB_TILE, t)
    nq = t // blk
    scale = SB_DIM ** -0.5

    def body(q_ref, k_ref, v_ref, gq_ref, gk_ref, l_ref, do_ref, dp_in, dp_ref, dgq_ref, dgk_ref,
             q2_sc, kn_sc, v_sc, do2_sc, dqn_sc, dkn_sc, dv_sc):
        bavg = _group_avg_mats()
        lane = lax.broadcasted_iota(jnp.int32, (1, 128), 1)
        first = lane < SB_DIM
        fq = lambda x, g: _pair_norm(x, g, bavg)
        vjps = []
        for p in range(SB_PAIRS):
            ls = slice(p * 128, (p + 1) * 128)
            qn, q_vjp = jax.vjp(fq, q_ref[0, :, ls], gq_ref[...])
            kn, k_vjp = jax.vjp(fq, k_ref[0, :, ls], gk_ref[...])
            vjps.append((q_vjp, k_vjp))
            kn_sc[p] = kn.astype(bf16)
            v_sc[p] = v_ref[0, :, ls].astype(bf16)
            dov = do_ref[0, :, ls]
            q2_sc[2 * p] = jnp.where(first, qn, 0.0).astype(bf16)
            q2_sc[2 * p + 1] = jnp.where(first, 0.0, qn).astype(bf16)
            do2_sc[2 * p] = jnp.where(first, dov, 0.0).astype(bf16)
            do2_sc[2 * p + 1] = jnp.where(first, 0.0, dov).astype(bf16)
        dkn_sc[...] = jnp.zeros_like(dkn_sc)
        dv_sc[...] = jnp.zeros_like(dv_sc)
        r, c = _iota2((blk, blk))
        pincl = (r <= c).astype(bf16)
        pstrict = (r < c).astype(bf16)
        r2, c2 = _iota2((2 * blk, blk))
        causal = c2 < (r2 & (blk - 1))

        def tile(q2s, do2s, lts, ks, carry, diag):
            out = []
            for p in range(SB_PAIRS):
                dq, cs, ce = carry[3 * p:3 * p + 3]
                q2, do2 = q2s[p], do2s[p]
                kb = kn_sc[p, pl.ds(ks, blk), :]
                zz = lax.dot_general(q2, kb, NT, preferred_element_type=f32) * scale
                sp = _softplus(zz)
                lm = jnp.where(causal, -sp, 0.0) if diag else -sp
                pre = _dot_x2c(lm, pincl)
                lp = zz - sp
                wgt = jnp.exp(lp + (lts[p] - cs - pre))
                if diag:
                    wgt = jnp.where(causal, wgt, 0.0)
                dw = lax.dot_general(do2, v_sc[p, pl.ds(ks, blk), :], NT, preferred_element_type=f32)
                e = wgt * dw
                ee = ce + _dot_x2c(e, pstrict)
                sig = jnp.exp(lp)
                dz = (e * (1.0 - sig) - ee * sig) * scale
                if diag:
                    dz = jnp.where(causal, dz, 0.0)
                dz = dz.astype(bf16)
                dkn_sc[p, pl.ds(ks, blk), :] += lax.dot_general(dz, q2, TN, preferred_element_type=f32)
                dv_sc[p, pl.ds(ks, blk), :] += lax.dot_general(wgt.astype(bf16), do2, TN, preferred_element_type=f32)
                out += [dq + _pdot(dz, kb), cs + jnp.sum(lm, axis=1, keepdims=True), ce + jnp.sum(e, axis=1, keepdims=True)]
            return tuple(out)

        def qloop(qi, _):
            qs = pl.multiple_of(qi * blk, blk)
            rows = pl.ds(qs, blk)
            q2s = [jnp.concatenate([q2_sc[2 * p, rows, :], q2_sc[2 * p + 1, rows, :]], axis=0) for p in range(SB_PAIRS)]
            do2s = [jnp.concatenate([do2_sc[2 * p, rows, :], do2_sc[2 * p + 1, rows, :]], axis=0) for p in range(SB_PAIRS)]
            lts = [jnp.concatenate([l_ref[0, rows, p * 128:p * 128 + 1], l_ref[0, rows, p * 128 + SB_DIM:p * 128 + SB_DIM + 1]],
                                   axis=0) for p in range(SB_PAIRS)]
            z1 = jnp.zeros((2 * blk, 1), f32)
            carry = lax.fori_loop(0, qi, lambda kj, cr: tile(q2s, do2s, lts, pl.multiple_of(kj * blk, blk), cr, False),
                                  (jnp.zeros((2 * blk, 128), f32), z1, z1) * SB_PAIRS)
            carry = tile(q2s, do2s, lts, qs, carry, True)
            for p in range(SB_PAIRS):
                dq = carry[3 * p]
                dqn_sc[p, rows, :] = jnp.where(first, dq[0:blk], dq[blk:2 * blk])
            return 0

        lax.fori_loop(0, nq, qloop, 0)
        dgq_tot, dgk_tot = jnp.zeros((1, 128), f32), jnp.zeros((1, 128), f32)
        for p in range(SB_PAIRS):
            ls = slice(p * 128, (p + 1) * 128)
            dq_pre, dgq = vjps[p][0](dqn_sc[p])
            dk_pre, dgk = vjps[p][1](dkn_sc[p])
            dp_ref[0, :, p * 128:(p + 1) * 128] = dq_pre.astype(bf16)
            dp_ref[0, :, SB_WIDTH + p * 128:SB_WIDTH + (p + 1) * 128] = dk_pre.astype(bf16)
            dp_ref[0, :, 2 * SB_WIDTH + p * 128:2 * SB_WIDTH + (p + 1) * 128] = dv_sc[p].astype(bf16)
            dgq_tot, dgk_tot = dgq_tot + dgq, dgk_tot + dgk
        dgq_ref[0] = jnp.broadcast_to(dgq_tot, (8, 128))
        dgk_ref[0] = jnp.broadcast_to(dgk_tot, (8, 128))

    col = lambda off: pl.BlockSpec((1, t, SB_WIDTH), lambda b: (b, 0, off), pipeline_mode=pl.Buffered(1))
    gsp = pl.BlockSpec((1, 128), lambda b: (0, 0))
    gout = pl.BlockSpec((1, 8, 128), lambda b: (b, 0, 0))
    return pl.pallas_call(
        body, name="sb_bwd", grid=(bsz,),
        in_specs=[col(0), col(1), col(2), gsp, gsp, col(0), col(0), pl.BlockSpec(memory_space=pl.ANY)],
        out_specs=[pl.BlockSpec((1, t, 3 * SB_WIDTH), lambda b: (b, 0, C_SB // (3 * SB_WIDTH)), pipeline_mode=pl.Buffered(1)),
                   gout, gout],
        out_shape=[SDS(dproj.shape, bf16)] + [SDS((bsz, 8, 128), f32)] * 2,
        input_output_aliases={7: 0},
        scratch_shapes=[pltpu.VMEM((2 * SB_PAIRS, t, 128), bf16), pltpu.VMEM((SB_PAIRS, t, 128), bf16),
                        pltpu.VMEM((SB_PAIRS, t, 128), bf16), pltpu.VMEM((2 * SB_PAIRS, t, 128), bf16),
                        pltpu.VMEM((SB_PAIRS, t, 128), f32), pltpu.VMEM((SB_PAIRS, t, 128), f32), pltpu.VMEM((SB_PAIRS, t, 128), f32)],
        compiler_params=_cp(("arbitrary",)),
    )(sbqkv, sbqkv, sbqkv, gq, gk, ltot, do, dproj)


def sg_pair(u, v, gain, wa, wb, ba, bb, bavg):
    r, c = _iota2((SG_CHUNK, SG_CHUNK))
    lane = lax.broadcasted_iota(jnp.int32, (1, 128), 1)
    first = lane < SG_DIM
    vn = _pair_norm(_gelu(v), gain, bavg)
    tri = c <= r
    mixed = (mm(jnp.where(tri, wa, 0.0), jnp.where(first, vn, 0.0)) + mm(jnp.where(tri, wb, 0.0), jnp.where(first, 0.0, vn))
             + jnp.where(first, ba, bb))
    return _gelu(u) * mixed


def sg_fwd(sguv, gain, w, bt):
    bsz, t, _ = sguv.shape
    nch = t // SG_CHUNK

    def body(uv_ref, g_ref, w_ref, b_ref, o_ref):
        bavg = _group_avg_mats()
        for p in range(2):
            ls = slice(p * 128, (p + 1) * 128)
            o_ref[0, :, ls] = sg_pair(uv_ref[0, :, ls], uv_ref[0, :, SG_WIDTH + p * 128:SG_WIDTH + (p + 1) * 128], g_ref[:, ls],
                                      w_ref[2 * p], w_ref[2 * p + 1], b_ref[:, 2 * p:2 * p + 1], b_ref[:, 2 * p + 1:2 * p + 2], bavg)

    full = lambda shp: pl.BlockSpec(shp, lambda b, n: (0,) * len(shp))
    return pl.pallas_call(
        body, name="sg_fwd", grid=(bsz, nch),
        in_specs=[pl.BlockSpec((1, SG_CHUNK, 2 * SG_WIDTH), lambda b, n: (b, n, 0)), full((1, SG_WIDTH)),
                  full((SG_GROUPS, SG_CHUNK, SG_CHUNK)), full((SG_CHUNK, 128))],
        out_specs=pl.BlockSpec((1, SG_CHUNK, SG_WIDTH), lambda b, n: (b, n, 0)),
        out_shape=SDS((bsz, t, SG_WIDTH), f32),
        compiler_params=_cp(("arbitrary", "arbitrary")),
    )(sguv, gain, w, bt)


def sg_bwd(sguv, gain, w, bt, do, dproj):
    bsz, t, _ = sguv.shape
    nch = t // SG_CHUNK

    def body(uv_ref, g_ref, w_ref, b_ref, do_ref, dp_in, duv_ref, dg_ref, dw_ref, db_ref):
        @pl.when((pl.program_id(0) == 0) & (pl.program_id(1) == 0))
        def _():
            dg_ref[...] = jnp.zeros_like(dg_ref)
            dw_ref[...] = jnp.zeros_like(dw_ref)
            db_ref[...] = jnp.zeros_like(db_ref)

        bavg = _group_avg_mats()
        lane = lax.broadcasted_iota(jnp.int32, (SG_CHUNK, 128), 1)
        dbt = jnp.zeros((SG_CHUNK, 128), f32)
        for p in range(2):
            ls = slice(p * 128, (p + 1) * 128)
            vs = slice(SG_WIDTH + p * 128, SG_WIDTH + (p + 1) * 128)
            prim = (uv_ref[0, :, ls], uv_ref[0, :, vs], g_ref[:, ls], w_ref[2 * p], w_ref[2 * p + 1],
                    b_ref[:, 2 * p:2 * p + 1], b_ref[:, 2 * p + 1:2 * p + 2])
            _, vjp = jax.vjp(lambda *a: sg_pair(*a, bavg), *prim)
            du, dv, dgn, dwa, dwb, dba, dbb = vjp(do_ref[0, :, ls])
            duv_ref[0, :, ls] = du.astype(bf16)
            duv_ref[0, :, vs] = dv.astype(bf16)
            dg_ref[:, ls] += dgn
            dw_ref[2 * p] += dwa
            dw_ref[2 * p + 1] += dwb
            dbt = dbt + jnp.where(lane == 2 * p, dba, 0.0) + jnp.where(lane == 2 * p + 1, dbb, 0.0)
        db_ref[...] += dbt

    full = lambda shp: pl.BlockSpec(shp, lambda b, n: (0,) * len(shp))
    return pl.pallas_call(
        body, name="sg_bwd", grid=(bsz, nch),
        in_specs=[pl.BlockSpec((1, SG_CHUNK, 2 * SG_WIDTH), lambda b, n: (b, n, 0)), full((1, SG_WIDTH)),
                  full((SG_GROUPS, SG_CHUNK, SG_CHUNK)), full((SG_CHUNK, 128)),
                  pl.BlockSpec((1, SG_CHUNK, SG_WIDTH), lambda b, n: (b, n, 0)), pl.BlockSpec(memory_space=pl.ANY)],
        out_specs=[pl.BlockSpec((1, SG_CHUNK, 2 * SG_WIDTH), lambda b, n: (b, n, C_SG // (2 * SG_WIDTH))), full((1, SG_WIDTH)),
                   full((SG_GROUPS, SG_CHUNK, SG_CHUNK)), full((SG_CHUNK, 128))],
        out_shape=[SDS(dproj.shape, bf16), SDS((1, SG_WIDTH), f32), SDS((SG_GROUPS, SG_CHUNK, SG_CHUNK), f32),
                   SDS((SG_CHUNK, 128), f32)],
        input_output_aliases={5: 0},
        compiler_params=_cp(("arbitrary", "arbitrary")),
    )(sguv, gain, w, bt, do, dproj)


def _pad_lanes(v, n=128):
    return jnp.pad(v.reshape(1, -1), ((0, 0), (0, n - v.size)))


def pad_w_in(w):
    return jnp.concatenate([w[:, 0:2048], jnp.pad(w[:, 2048:2056], ((0, 0), (0, C_SB - C_AB - 8))), w[:, 2056:]], axis=1)


def unpad_w_in(w):
    return jnp.concatenate([w[:, 0:2048], w[:, C_AB:C_AB + 8], w[:, C_SB:]], axis=1)


def _w_in_runs():
    shard, runs = IN_DIM // N_CHIPS, []
    for s in range(N_CHIPS):
        for a, b, d in ((0, 2048, 0), (2048, 2056, C_AB), (2056, IN_DIM, C_SB)):
            lo, hi = max(shard * s, a), min(shard * (s + 1), b)
            if lo < hi:
                runs.append((s, lo - shard * s, hi - shard * s, d + lo - a))
    return runs


def w_in_from_shards(zone, tr=256):
    def body(z_ref, o_ref):
        o_ref[:, C_AB:C_SB] = jnp.zeros((tr, C_SB - C_AB), zone.dtype)
        for s, a, b, d in _w_in_runs():
            o_ref[:, d:d + b - a] = z_ref[s, :, a:b]

    return pl.pallas_call(
        body, name="w_in_from_shards", grid=(D_MODEL // tr,),
        in_specs=[pl.BlockSpec((N_CHIPS, tr, IN_DIM // N_CHIPS), lambda i: (0, i, 0))],
        out_specs=pl.BlockSpec((tr, IN_PAD), lambda i: (i, 0)), out_shape=SDS((D_MODEL, IN_PAD), zone.dtype),
        compiler_params=_cp(("arbitrary",)))(zone)


def w_in_grad_to_shards(g, tr=256):
    def body(g_ref, o_ref):
        for s, a, b, d in _w_in_runs():
            o_ref[s, :, a:b] = g_ref[:, d:d + b - a]

    return pl.pallas_call(
        body, name="w_in_grad_to_shards", grid=(D_MODEL // tr,),
        in_specs=[pl.BlockSpec((tr, IN_PAD), lambda i: (i, 0))],
        out_specs=pl.BlockSpec((N_CHIPS, tr, IN_DIM // N_CHIPS), lambda i: (0, i, 0)),
        out_shape=SDS((N_CHIPS, D_MODEL, IN_DIM // N_CHIPS), g.dtype), compiler_params=_cp(("arbitrary",)))(g)


def layer_params(p, l):
    return dict(
        g1=p["norm1_g"][l].reshape(1, -1), g2=p["norm2_g"][l].reshape(1, -1),
        conv=jnp.pad(p["conv_w"][l], ((0, 4), (0, 0))), alog=_pad_lanes(p["a_log"][l]), dtb=_pad_lanes(p["dt_bias"][l]),
        dng=p["dn_out_g"][l].reshape(1, -1), gq=jnp.tile(p["sb_q_g"][l].reshape(1, -1), (1, 2)),
        gk=jnp.tile(p["sb_k_g"][l].reshape(1, -1), (1, 2)), sgg=p["sg_v_g"][l].reshape(1, -1), sgw=p["sg_w"][l],
        sgb=jnp.pad(p["sg_b"][l].T, ((0, 0), (0, 124))))


def local_step(x, tgt, small, get_w, put_g, sync_g):
    bsz, t, _ = x.shape
    m = bsz * t
    r3 = lambda a: a.reshape(bsz, t, a.shape[-1])
    r2 = lambda a: a.reshape(m, a.shape[-1])
    xs, saved, ws = x.reshape(m, D_MODEL), [], []
    for l in range(DEPTH):
        sp, w = layer_params(small, l), {}
        w["w_in"] = get_w(l, "in", xs)
        qkv, z, ab, sb, sg = inproj_fwd(xs, sp["g1"], w["w_in"])
        odn, sall, tall = dn_fwd(r3(qkv), r3(z), r3(ab), sp["conv"], sp["alog"], sp["dtb"], sp["dng"])
        osb, ltot = sb_fwd(r3(sb), sp["gq"], sp["gk"])
        osg = sg_fwd(r3(sg), sp["sgg"], sp["sgw"], sp["sgb"])
        w["w_out"] = get_w(l, "out", osg)
        x2, mix = outproj_fwd(xs, r2(odn), r2(osb), r2(osg), w["w_out"])
        w["w_ff1"], w["w_ff2"] = get_w(l, "ff", x2)
        x3, rlb = ffn_fwd(x2, sp["g2"], w["w_ff1"], w["w_ff2"])
        saved.append(dict(rlb=rlb, x=xs,qkv=qkv, z=z, ab=ab, sb=sb, sg=sg, sall=sall, tall=tall, ltot=ltot, mix=mix, x2=x2))
        ws.append(w)
        xs = x3
    dx, lossp = loss_head(xs, tgt.reshape(m, D_MODEL))
    gsmall = [None] * DEPTH
    token = jnp.zeros((), f32)
    for l in reversed(range(DEPTH)):
        sp, w, s = layer_params(small, l), ws[l], saved[l]
        dx2, dg2, h2, act, df, dyb = ffn_bwd(s["x2"], sp["g2"] + token, w["w_ff1"], w["w_ff2"], s["rlb"], dx)
        token = sync_g(dg2)
        g_ff1 = tn_matmul(h2, df, f"dw_ff1_{l}", col_shards=N_CHIPS)
        g_ff2 = tn_matmul(act, dyb, f"dw_ff2_{l}")
        dodn, dosb, dosg, dx2b = outproj_bwd(dx2, w["w_out"])
        g_out = tn_matmul(s["mix"], dx2b, f"dw_out_{l}")
        token = token + put_g(l, "rest", dict(w_out=g_out, w_ff1=g_ff1, w_ff2=g_ff2))
        dproj, dconv, dalog, ddtb, ddng = dn_bwd(r3(s["qkv"]), r3(s["z"]), r3(s["ab"]), sp["conv"], sp["alog"], sp["dtb"],
                                                 sp["dng"] + token, s["sall"], s["tall"], r3(dodn))
        token = sync_g(ddng)
        dproj, dgq, dgk = sb_bwd(r3(s["sb"]), sp["gq"] + token, sp["gk"], s["ltot"], r3(dosb), dproj)
        dproj, dsgg, dsgw, dsgb = sg_bwd(r3(s["sg"]), sp["sgg"], sp["sgw"], sp["sgb"], r3(dosg), dproj)
        dproj = r2(dproj)
        dx, dg1, h1 = inproj_bwd(s["x"], sp["g1"], w["w_in"], dproj, dx2)
        g_in = tn_matmul(h1, dproj, f"dw_in_{l}")
        token = put_g(l, "in", dict(w_in=g_in))
        fold = lambda a: (a[:, 0, :].sum(0).reshape(2, SB_DIM)).sum(0)
        gsmall[l] = dict(norm1_g=dg1[0], conv_w=dconv[0:DN_CONV], a_log=dalog[0, 0:DN_HEADS], dt_bias=ddtb[0, 0:DN_HEADS],
                         dn_out_g=ddng[0], sb_q_g=fold(dgq), sb_k_g=fold(dgk), sg_v_g=dsgg[0], sg_w=dsgw,
                         sg_b=dsgb[:, 0:SG_GROUPS].T, norm2_g=dg2[0])
    return lossp, dx.reshape(bsz, t, D_MODEL), gsmall


def _chip_peers(x, y):
    return [(1 - x, y), (x, 1 - y), (1 - x, 1 - y)]


_HBM = pl.BlockSpec(memory_space=pltpu.HBM)
_SEM = pl.BlockSpec(memory_space=pltpu.SEMAPHORE)
_EFFECT = pltpu.SideEffectType.DATAFLOW_SIDE_EFFECTING


def _hbm(a):
    return pltpu.with_memory_space_constraint(a, pltpu.HBM)


def _my_half(ref):
    half = ref.shape[0] // 2
    return ref.at[pl.ds(pl.multiple_of(lax.axis_index("c") * half, 8), half)]


def _slot(zone, s, cols):
    if not cols:
        return zone.at[s]
    width = zone.shape[1] // N_CHIPS
    return zone.at[:, pl.ds(pl.multiple_of(s * width, 128), width)]


def _exchange_copy(src, land, k, j, send, recv, scatter, halve, waiting):
    x, y, c = lax.axis_index("x"), lax.axis_index("y"), lax.axis_index("c")
    px, py = _chip_peers(x, y)[j]
    me, peer = 2 * x + y, 2 * px + py
    if scatter:
        src = src.at[me if waiting else peer]
    dst = _slot(land, peer if waiting else me, halve == "cols")
    if halve:
        src, dst = _my_half(src), _my_half(dst)
    return pltpu.make_async_remote_copy(src_ref=src, dst_ref=dst, send_sem=send.at[3 * k + j],
                                        recv_sem=recv.at[3 * k + j], device_id=(px, py, c), device_id_type=MESH)


def exchange_start(items, name, scatter):
    arrs = []
    for a, _, _ in items:
        if not any(a is b for b in arrs):
            arrs.append(a)
    pos = [next(i for i, b in enumerate(arrs) if b is a) for a, _, _ in items]
    shapes = [a.shape if idx is None else a.shape[1:] for a, idx, _ in items]
    lands = [lax.empty(s if scatter else ((s[0], N_CHIPS * s[1]) if h == "cols" else (N_CHIPS,) + s), a.dtype)
             for (a, _, h), s in zip(items, shapes)]
    na, nl = len(arrs), len(lands)

    def body(*refs):
        ins, lnd = refs[:na], refs[na:na + nl]
        send, recv = refs[na + nl], refs[na + nl + 1]
        token = refs[-1]
        for k, (_, idx, halve) in enumerate(items):
            src = ins[pos[k]] if idx is None else ins[pos[k]].at[idx]
            for j in range(3):
                _exchange_copy(src, lnd[k], k, j, send, recv, scatter, halve, False).start()
        token[...] = jnp.zeros_like(token)

    sems = pltpu.SemaphoreType.DMA((3 * nl,))
    out = pl.pallas_call(
        body, name=name,
        out_shape=(sems, sems, *[pltpu.HBM(a.shape, a.dtype) for a in arrs + lands], SDS((8, 128), f32)),
        in_specs=[_HBM] * (na + nl), out_specs=(_SEM, _SEM, *[_HBM] * (na + nl), pl.BlockSpec(memory_space=pltpu.VMEM)),
        input_output_aliases={i: 2 + i for i in range(na + nl)},
        compiler_params=pltpu.CompilerParams(has_side_effects=_EFFECT),
    )(*[_hbm(a) for a in arrs + lands])
    thru = out[2:2 + na]
    return dict(send=out[0], recv=out[1], src=[(thru[pos[k]], idx) for k, (_, idx, _) in enumerate(items)],
                halve=[h for _, _, h in items], land=list(out[2 + na:2 + na + nl]), token=out[-1], scatter=scatter)


def exchange_wait(st, ks, after, name):
    arrs = []
    for k in ks:
        if not any(st["src"][k][0] is b for b in arrs):
            arrs.append(st["src"][k][0])
    pos = [next(i for i, b in enumerate(arrs) if b is st["src"][k][0]) for k in ks]
    lands = [st["land"][k] for k in ks]
    na, nl = len(arrs), len(lands)

    def body(*refs):
        ins, lnd = refs[:na], refs[na:na + nl]
        send, recv = refs[na + nl], refs[na + nl + 1]
        for t, k in enumerate(ks):
            idx = st["src"][k][1]
            src = ins[pos[t]] if idx is None else ins[pos[t]].at[idx]
            for j in range(3):
                cp = _exchange_copy(src, lnd[t], k, j, send, recv, st["scatter"], st["halve"][k], True)
                cp.wait_send()
                cp.wait_recv()

    out = pl.pallas_call(
        body, name=name, out_shape=tuple(pltpu.HBM(a.shape, a.dtype) for a in arrs + lands),
        in_specs=[_HBM] * (na + nl) + [_SEM, _SEM, pl.BlockSpec(memory_space=pl.ANY)], out_specs=tuple([_HBM] * (na + nl)),
        input_output_aliases={i: i for i in range(na + nl)},
        compiler_params=pltpu.CompilerParams(has_side_effects=_EFFECT),
    )(*arrs, *lands, st["send"], st["recv"], after)
    for k, (a, idx) in enumerate(st["src"]):
        for p, b in enumerate(arrs):
            if a is b:
                st["src"][k] = (out[p], idx)
    return list(out[na:na + nl])


def _sibling_copy(src, land, i, send, recv, other_half):
    x, y, c = lax.axis_index("x"), lax.axis_index("y"), lax.axis_index("c")
    return pltpu.make_async_remote_copy(src_ref=src.at[:, 1 - c] if other_half else src, dst_ref=land, send_sem=send.at[i],
                                        recv_sem=recv.at[i], device_id=(x, y, 1 - c), device_id_type=MESH)


def sibling_start(arrs, name, other_half=False):
    n = len(arrs)
    lands = [lax.empty((a.shape[0],) + a.shape[2:] if other_half else a.shape, a.dtype) for a in arrs]

    def body(*refs):
        ins, lnd = refs[:n], refs[n:2 * n]
        send, recv = refs[2 * n], refs[2 * n + 1]
        token = refs[-1]
        for i in range(n):
            _sibling_copy(ins[i], lnd[i], i, send, recv, other_half).start()
        token[...] = jnp.zeros_like(token)

    sems = pltpu.SemaphoreType.DMA((n,))
    out = pl.pallas_call(
        body, name=name,
        out_shape=(sems, sems, *[pltpu.HBM(a.shape, a.dtype) for a in arrs + lands], SDS((8, 128), f32)),
        in_specs=[_HBM] * (2 * n), out_specs=(_SEM, _SEM, *[_HBM] * (2 * n), pl.BlockSpec(memory_space=pltpu.VMEM)),
        input_output_aliases={i: 2 + i for i in range(2 * n)},
        compiler_params=pltpu.CompilerParams(has_side_effects=_EFFECT),
    )(*[_hbm(a) for a in arrs + lands])
    return dict(send=out[0], recv=out[1], src=list(out[2:2 + n]), land=list(out[2 + n:2 + 2 * n]), token=out[-1],
                other_half=other_half)


def sibling_wait(st, after, name):
    n = len(st["src"])

    def body(*refs):
        ins, lnd = refs[:n], refs[n:2 * n]
        send, recv = refs[2 * n], refs[2 * n + 1]
        for i in range(n):
            cp = _sibling_copy(ins[i], lnd[i], i, send, recv, st["other_half"])
            cp.wait_send()
            cp.wait_recv()

    out = pl.pallas_call(
        body, name=name, out_shape=tuple(pltpu.HBM(a.shape, a.dtype) for a in st["src"] + st["land"]),
        in_specs=[_HBM] * (2 * n) + [_SEM, _SEM, pl.BlockSpec(memory_space=pl.ANY)], out_specs=tuple([_HBM] * (2 * n)),
        input_output_aliases={i: i for i in range(2 * n)},
        compiler_params=pltpu.CompilerParams(has_side_effects=_EFFECT),
    )(*st["src"], *st["land"], st["send"], st["recv"], after)
    return list(out[:n]), list(out[n:])


def swap_cores(arrs, name):
    n = len(arrs)

    def body(*refs):
        ins, outs = refs[:n], refs[n:2 * n]
        send, recv = refs[2 * n:]
        sib = (lax.axis_index("x"), lax.axis_index("y"), 1 - lax.axis_index("c"))
        cps = [pltpu.make_async_remote_copy(src_ref=ins[i], dst_ref=outs[i], send_sem=send.at[i], recv_sem=recv.at[i],
                                            device_id=sib, device_id_type=MESH) for i in range(n)]
        for cp in cps:
            cp.start()
        for cp in cps:
            cp.wait()

    any_spec = pl.BlockSpec(memory_space=pl.ANY)
    return pl.pallas_call(
        body, name=name, in_specs=[any_spec] * n, out_specs=[any_spec] * n, out_shape=[SDS(a.shape, a.dtype) for a in arrs],
        scratch_shapes=[pltpu.SemaphoreType.DMA((n,)), pltpu.SemaphoreType.DMA((n,))],
    )(*arrs)


def swap_halves(zones, name):
    n = len(zones)

    def body(*refs):
        outs = refs[n:2 * n]
        send, recv = refs[2 * n:]
        x, y, c = lax.axis_index("x"), lax.axis_index("y"), lax.axis_index("c")
        cps = []
        for i in range(n):
            for j, (px, py) in enumerate(_chip_peers(x, y)):
                part = _my_half(outs[i].at[2 * px + py])
                cps.append(pltpu.make_async_remote_copy(src_ref=part, dst_ref=part, send_sem=send.at[3 * i + j],
                                                        recv_sem=recv.at[3 * i + j], device_id=(x, y, 1 - c), device_id_type=MESH))
        for cp in cps:
            cp.start()
        for cp in cps:
            cp.wait_send()
            cp.wait_recv()

    any_spec = pl.BlockSpec(memory_space=pl.ANY)
    return pl.pallas_call(
        body, name=name, in_specs=[any_spec] * n, out_specs=[any_spec] * n, out_shape=[SDS(a.shape, a.dtype) for a in zones],
        input_output_aliases={i: i for i in range(n)},
        scratch_shapes=[pltpu.SemaphoreType.DMA((3 * n,)), pltpu.SemaphoreType.DMA((3 * n,))],
    )(*zones)


def swap_other_halves(arrs, name):
    n = len(arrs)

    def body(*refs):
        ins, outs = refs[:n], refs[n:2 * n]
        send, recv = refs[2 * n:]
        x, y, c = lax.axis_index("x"), lax.axis_index("y"), lax.axis_index("c")
        cps = [pltpu.make_async_remote_copy(src_ref=ins[i].at[:, 1 - c], dst_ref=outs[i], send_sem=send.at[i], recv_sem=recv.at[i],
                                            device_id=(x, y, 1 - c), device_id_type=MESH) for i in range(n)]
        for cp in cps:
            cp.start()
        for cp in cps:
            cp.wait()

    any_spec = pl.BlockSpec(memory_space=pl.ANY)
    return pl.pallas_call(
        body, name=name, in_specs=[any_spec] * n, out_specs=[any_spec] * n,
        out_shape=[SDS((a.shape[0],) + a.shape[2:], a.dtype) for a in arrs],
        scratch_shapes=[pltpu.SemaphoreType.DMA((n,)), pltpu.SemaphoreType.DMA((n,))],
    )(*arrs)


def _ids_spec(grid, in_specs, out_specs):
    return pltpu.PrefetchScalarGridSpec(num_scalar_prefetch=1, grid=grid, in_specs=in_specs, out_specs=out_specs)


def pair_sum(ids, a, b, name, tr=512):
    nd, _, rows, cols = a.shape
    tr = min(tr, rows)
    assert rows % tr == 0

    def body(ids_ref, a_ref, b_ref, o_ref):
        o_ref[...] = (a_ref[0].astype(f32) + b_ref[...].astype(f32)).astype(bf16)

    spec = pl.BlockSpec((1, tr, cols), lambda d, i, ids: (d, i, 0))
    return pl.pallas_call(
        body, name=name,
        grid_spec=_ids_spec((nd, rows // tr), [pl.BlockSpec((1, 1, tr, cols), lambda d, i, ids: (d, ids[1], i, 0)), spec], spec),
        out_shape=SDS((nd, rows, cols), bf16), compiler_params=_cp(("arbitrary", "arbitrary")))(ids, a, b)


def allreduce_small(v):
    def body(v_ref, o_ref, rbuf, send, recv):
        x, y, c = lax.axis_index("x"), lax.axis_index("y"), lax.axis_index("c")
        o_ref[...] = v_ref[...]
        for s, peer in enumerate([(x, y, 1 - c), (1 - x, y, c), (x, 1 - y, c)]):
            cp = pltpu.make_async_remote_copy(src_ref=o_ref, dst_ref=rbuf.at[s], send_sem=send.at[s], recv_sem=recv.at[s],
                                              device_id=peer, device_id_type=MESH)
            cp.start()
            cp.wait()
            o_ref[...] = o_ref[...] + rbuf[s]

    vm = pl.BlockSpec(memory_space=pltpu.VMEM)
    return pl.pallas_call(
        body, name="allreduce_small", in_specs=[vm], out_specs=vm, out_shape=SDS(v.shape, f32),
        scratch_shapes=[pltpu.VMEM((3,) + v.shape, f32), pltpu.SemaphoreType.DMA((3,)), pltpu.SemaphoreType.DMA((3,))],
        compiler_params=_cp(),
    )(v)


def sum_partials(ids, zone, mine, name, tr=256):
    _, rows, cols = zone.shape
    tr = min(tr, rows)
    assert rows % tr == 0

    def body(ids_ref, m_ref, z1_ref, z2_ref, z3_ref, o_ref):
        o_ref[...] = ((m_ref[0].astype(f32) + z1_ref[0].astype(f32)) + z2_ref[0].astype(f32)) + z3_ref[0].astype(f32)

    slot = lambda flip: pl.BlockSpec((1, tr, cols), lambda i, ids: (ids[0] ^ flip, i, 0))
    return pl.pallas_call(
        body, name=name,
        grid_spec=_ids_spec((rows // tr,), [slot(0), slot(1), slot(2), slot(3)], pl.BlockSpec((tr, cols), lambda i, ids: (i, 0))),
        out_shape=SDS((rows, cols), f32), compiler_params=_cp(("arbitrary",)),
    )(ids, mine, zone, zone, zone)


def adamw(w, m, v, gs, name, layer=0, prev=None, tr=256):
    hrows, cols = gs[0].shape
    rows = hrows * len(gs)
    tr = min(tr, hrows)
    assert hrows % tr == 0 and w.shape[0] % rows == 0
    off, nth = layer * (rows // tr), hrows // tr

    def body(w_ref, m_ref, v_ref, *rest):
        g_ref, d_ref, mo_ref, vo_ref = rest[-4:]
        if len(gs) == 1:
            g = rest[0][...]
        else:
            g = jnp.where(pl.program_id(0) // nth == lax.axis_index("c"), rest[0][...], rest[1][...])
        mn = ADAM_B1 * m_ref[...] + (1.0 - ADAM_B1) * g
        vn = ADAM_B2 * v_ref[...] + (1.0 - ADAM_B2) * jnp.square(g)
        m_hat = mn / (1.0 - ADAM_B1 ** ADAM_STEP)
        v_hat = vn / (1.0 - ADAM_B2 ** ADAM_STEP)
        g_ref[...] = g
        d_ref[...] = -ADAM_LR * (m_hat / (jnp.sqrt(v_hat) + ADAM_EPS) + ADAM_WD * w_ref[...])
        mo_ref[...] = mn
        vo_ref[...] = vn

    loc = pl.BlockSpec((tr, cols), lambda i: (i % nth, 0))
    glob = pl.BlockSpec((tr, cols), lambda i: (off + i, 0))
    extra = [] if prev is None else list(prev)
    return pl.pallas_call(
        body, name=name, grid=(rows // tr,),
        in_specs=[glob] * 3 + [loc] * len(gs) + [pl.BlockSpec(memory_space=pl.ANY)] * len(extra),
        out_specs=[glob] * 4, out_shape=[SDS(w.shape, f32)] * 4,
        input_output_aliases={3 + len(gs) + j: j for j in range(len(extra))},
        compiler_params=_cp(("arbitrary",)),
    )(w, m, v, *gs, *extra)


BIG = ("w_in", "w_out", "w_ff1", "w_ff2")
SMALL = ("norm1_g", "conv_w", "a_log", "dt_bias", "dn_out_g", "sb_q_g", "sb_k_g", "sg_v_g", "sg_w", "sg_b", "norm2_g")
WEIGHTS = ("norm1_g", "w_in", "conv_w", "a_log", "dt_bias", "dn_out_g", "sb_q_g", "sb_k_g", "sg_v_g", "sg_w", "sg_b",
           "w_out", "norm2_g", "w_ff1", "w_ff2")


PACK_ROWS = 256


def _rows_of(shape):
    n = 1
    for d in shape:
        n *= d
    return -(-n // 1024) * 8, n


def _pack(arrs):
    parts = []
    for a in arrs:
        r, n = _rows_of(a.shape)
        parts.append(jnp.pad(a.reshape(-1), (0, r * 128 - n)).reshape(r, 128))
    rows = sum(p.shape[0] for p in parts)
    parts.append(jnp.zeros((-rows % PACK_ROWS, 128), arrs[0].dtype))
    return jnp.concatenate(parts, axis=0)


def _unpack(packed, shapes):
    out, o = [], 0
    for s in shapes:
        r, n = _rows_of(s)
        out.append(packed[o:o + r].reshape(-1)[0:n].reshape(s))
        o += r
    return out


def kernel(x, norm1_g, w_in, conv_w, a_log, dt_bias, dn_out_g, sb_q_g, sb_k_g, sg_v_g, sg_w, sg_b, w_out, norm2_g, w_ff1, w_ff2, loss_target, m_norm1_g, m_w_in, m_conv_w, m_a_log, m_dt_bias, m_dn_out_g, m_sb_q_g, m_sb_k_g, m_sg_v_g, m_sg_w, m_sg_b, m_w_out, m_norm2_g, m_w_ff1, m_w_ff2, v_norm1_g, v_w_in, v_conv_w, v_a_log, v_dt_bias, v_dn_out_g, v_sb_q_g, v_sb_k_g, v_sg_v_g, v_sg_w, v_sg_b, v_w_out, v_norm2_g, v_w_ff1, v_w_ff2):
    w = dict(norm1_g=norm1_g, w_in=w_in, conv_w=conv_w, a_log=a_log, dt_bias=dt_bias, dn_out_g=dn_out_g, sb_q_g=sb_q_g,
             sb_k_g=sb_k_g, sg_v_g=sg_v_g, sg_w=sg_w, sg_b=sg_b, w_out=w_out, norm2_g=norm2_g, w_ff1=w_ff1, w_ff2=w_ff2)
    mom = dict(norm1_g=m_norm1_g, w_in=m_w_in, conv_w=m_conv_w, a_log=m_a_log, dt_bias=m_dt_bias, dn_out_g=m_dn_out_g,
               sb_q_g=m_sb_q_g, sb_k_g=m_sb_k_g, sg_v_g=m_sg_v_g, sg_w=m_sg_w, sg_b=m_sg_b, w_out=m_w_out, norm2_g=m_norm2_g,
               w_ff1=m_w_ff1, w_ff2=m_w_ff2)
    var = dict(norm1_g=v_norm1_g, w_in=v_w_in, conv_w=v_conv_w, a_log=v_a_log, dt_bias=v_dt_bias, dn_out_g=v_dn_out_g,
               sb_q_g=v_sb_q_g, sb_k_g=v_sb_k_g, sg_v_g=v_sg_v_g, sg_w=v_sg_w, sg_b=v_sg_b, w_out=v_w_out, norm2_g=v_norm2_g,
               w_ff1=v_w_ff1, w_ff2=v_w_ff2)
    chip = 2 * lax.axis_index("x") + lax.axis_index("y")

    wb = {k: w[k].astype(bf16) for k in BIG}
    ag = exchange_start([(conv_w, None, None)] + [(wb[k], l, "rows") for l in range(DEPTH) for k in BIG],
                        "allgather_start", scatter=False)
    item = lambda l, k: 1 + l * len(BIG) + BIG.index(k)

    def landed(ks, after, name):
        zones = exchange_wait(ag, ks, after, name)
        halved = [t for t, k in enumerate(ks) if ag["halve"][k]]
        for t, z in zip(halved, swap_halves([zones[t] for t in halved], name.replace("wait", "pass"))):
            zones[t] = z
        own = [ag["src"][k][0] if ag["src"][k][1] is None else ag["src"][k][0][ag["src"][k][1]] for k in ks]
        return [lax.dynamic_update_slice_in_dim(z, o[None], chip, axis=0) for z, o in zip(zones, own)]

    def whole(k, z):
        if k == "w_in":
            return w_in_from_shards(z)
        if k == "w_ff1":
            return jnp.transpose(z, (1, 0, 2)).reshape(D_MODEL, D_FF)
        return z.reshape(-1, D_MODEL)

    g_conv, first_in = landed([0, item(0, "w_in")], x, "allgather_wait_in0")
    small = {k: w[k] for k in SMALL}
    small["conv_w"] = jnp.transpose(g_conv, (1, 2, 0, 3)).reshape(DEPTH, DN_CONV, 3 * DN_WIDTH)
    cache = {}

    def get_w(l, part, after):
        if part == "in":
            return whole("w_in", first_in if l == 0 else landed([item(l, "w_in")], after, f"allgather_wait_in{l}")[0])
        if part == "out":
            zs = landed([item(l, k) for k in ("w_out", "w_ff1", "w_ff2")], after, f"allgather_wait_rest{l}")
            cache[l] = (whole("w_ff1", zs[1]), whole("w_ff2", zs[2]))
            return whole("w_out", zs[0])
        return cache[l]

    rs, pending = {}, []
    ids = jnp.stack([chip, lax.axis_index("c")]).astype(jnp.int32)

    def put_g(l, tag, g):
        names = [k for k in BIG if k in g]
        by_dest = [w_in_grad_to_shards(g[k]) if k == "w_in" else g[k] for k in names]
        halves = [a.reshape(N_CHIPS, 2, -1, a.shape[-1]) for a in by_dest]
        st = sibling_start(halves, f"pair_swap_start_{tag}{l}", other_half=True)
        pending.append((l, tag, names, st))
        return st["token"][0, 0]

    def sync_g(after):
        token = jnp.zeros((), f32)
        while pending:
            l, tag, names, st = pending.pop(0)
            halves, got = sibling_wait(st, after, f"pair_swap_wait_{tag}{l}")
            pair = [pair_sum(ids, a, b, f"pair_sum_{k}_{l}") for k, a, b in zip(names, halves, got)]
            rs[l, tag] = dict(exchange_start([(a, None, None) for a in pair], f"scatter_start_{tag}{l}", scatter=True), names=names)
            token = token + rs[l, tag]["token"][0, 0]
        return token

    lossp, grad_x, gsmall = local_step(x, loss_target, small, get_w, put_g, sync_g)
    loss = lax.psum(jnp.sum(lossp), ("x", "y", "c"))

    def sum_group(l, tag, after):
        st = rs[l, tag]
        zones = exchange_wait(st, list(range(len(st["names"]))), after, f"scatter_wait_{tag}{l}")
        sums = [sum_partials(ids, zones[i], st["src"][i][0], f"sum_{k}_{l}") for i, k in enumerate(st["names"])]
        return sibling_start(sums, f"swap_sums_start_{tag}{l}")

    def update_group(l, tag, swap, after, prev):
        sums, others = sibling_wait(swap, after, f"swap_sums_wait_{tag}{l}")
        outs = dict(prev)
        for i, k in enumerate(rs[l, tag]["names"]):
            r2 = lambda a: a.reshape(-1, a.shape[-1])
            outs[k] = adamw(r2(w[k]), r2(mom[k]), r2(var[k]), (sums[i], others[i]), f"adamw_{k}_{l}", layer=l, prev=prev.get(k))
        return outs

    swap_r = sum_group(1, "rest", pending[0][3]["token"])
    swap_i = sum_group(1, "in", swap_r["token"])
    sync_g(swap_i["token"])
    done = update_group(1, "rest", swap_r, rs[0, "in"]["token"], {})
    done = update_group(1, "in", swap_i, done["w_ff2"][0], done)
    res = {}

    full_shapes = [(DEPTH,) + tuple(gsmall[0][k].shape) for k in SMALL]
    packed = _pack([jnp.stack([gsmall[l][k] for l in range(DEPTH)]) for k in SMALL])
    total = allreduce_small(packed)
    gfull = dict(zip(SMALL, _unpack(total, full_shapes)))
    cs = 3 * DN_WIDTH // N_CHIPS
    gfull["conv_w"] = lax.dynamic_slice_in_dim(gfull["conv_w"], chip * cs, cs, axis=2)
    gp, wp, mp, vp = (_pack([d[k] for k in SMALL]) for d in (gfull, w, mom, var))
    outs = adamw(wp, mp, vp, (gp,), "adamw_small")
    loc_shapes = [w[k].shape for k in SMALL]
    unp = [_unpack(o, loc_shapes) for o in outs]
    for i, k in enumerate(SMALL):
        res[k] = [unp[j][i] for j in range(4)]

    swap_r = sum_group(0, "rest", outs[0])
    swap_i = sum_group(0, "in", swap_r["token"])
    done = update_group(0, "rest", swap_r, swap_i["token"], done)
    done = update_group(0, "in", swap_i, done["w_ff2"][0], done)
    for k in BIG:
        res[k] = [o.reshape(w[k].shape) for o in done[k]]

    return (loss, grad_x, *[res[k][0] for k in WEIGHTS], *[res[k][1] for k in WEIGHTS], *[res[k][2] for k in WEIGHTS],
            *[res[k][3] for k in WEIGHTS])
```

```python
import functools

import jax
import jax.numpy as jnp
from jax import lax
from jax.experimental import pallas as pl
from jax.experimental.pallas import tpu as pltpu

f32 = jnp.float32
bf16 = jnp.bfloat16
SDS = jax.ShapeDtypeStruct
MESH = pl.DeviceIdType.MESH

NORM_EPS = 1e-6
D_MODEL = 1024
DEPTH = 2
DN_HEADS, DN_DIM, DN_WIDTH, DN_CONV, DN_CHUNK = 4, 128, 512, 4, 64
SB_HEADS, SB_DIM, SB_WIDTH, SB_BLOCK = 4, 64, 256, 128
SG_GROUPS, SG_DIM, SG_WIDTH, SG_CHUNK = 4, 64, 256, 128
D_FF = 4096
IN_DIM = 3336
C_QKV, C_Z, C_AB, C_SB, C_SG, IN_PAD = 0, 1536, 2048, 2304, 3072, 3584
DN_COLS = C_SB
N_CHIPS = 4

ADAM_LR, ADAM_B1, ADAM_B2, ADAM_EPS, ADAM_WD, ADAM_STEP = 0.001, 0.9, 0.999, 1e-08, 0.01, 10

VMEM_LIMIT = 56 * 1024 * 1024


def _cp(sem=None, **kw):
    if sem is not None:
        kw["dimension_semantics"] = sem
    return pltpu.CompilerParams(vmem_limit_bytes=VMEM_LIMIT, **kw)


def _split2(x):
    hi = x.astype(bf16)
    lo = (x - hi.astype(f32)).astype(bf16)
    return hi, lo


NT = (((1,), (1,)), ((), ()))
TN = (((0,), (0,)), ((), ()))
_DIMS2 = dict(nn=(((1,), (0,)), ((), ())), nt=NT, tn=TN)
_DIMS3 = dict(nn=(((2,), (1,)), ((0,), (0,))), nt=(((2,), (2,)), ((0,), (0,))), tn=(((1,), (1,)), ((0,), (0,))))


def _dg(a, b, kind):
    return lax.dot_general(a, b, (_DIMS2 if a.ndim == 2 else _DIMS3)[kind], preferred_element_type=f32)


def _pdot(a, b):
    return _dg(a, b, "nn")


def _dot_hp(a, b):
    ah, al = _split2(a)
    bh, bl = _split2(b)
    return _pdot(ah, bh) + _pdot(ah, bl) + _pdot(al, bh)


def _dot_x2c(a, m):
    lead = a.shape[:-1]
    ah, al = _split2(a.reshape(-1, a.shape[-1]))
    return (_pdot(ah, m) + _pdot(al, m)).reshape(lead + (m.shape[1],))


def _dot_cx2(m, a):
    if a.ndim == 3:
        m = jnp.broadcast_to(m, (a.shape[0],) + m.shape)
    ah, al = _split2(a)
    return _pdot(m, ah) + _pdot(m, al)


def _nt(a, b):
    return _dg(a.astype(bf16), b.astype(bf16), "nt")


def _tn(a, b):
    return _dg(a.astype(bf16), b.astype(bf16), "tn")


def _nn(a, b):
    return _dg(a.astype(bf16), b.astype(bf16), "nn")


@jax.custom_vjp
def mm(a, b):
    return _nn(a, b)


mm.defvjp(lambda a, b: (_nn(a, b), (a, b)), lambda r, g: (_nt(g, r[1]), _tn(r[0], g)))


@jax.custom_vjp
def mm_nt(a, b):
    return _nt(a, b)


mm_nt.defvjp(lambda a, b: (_nt(a, b), (a, b)), lambda r, g: (_nn(g, r[1]), _tn(g, r[0])))


@jax.custom_vjp
def mm_tn(a, b):
    return _tn(a, b)


mm_tn.defvjp(lambda a, b: (_tn(a, b), (a, b)), lambda r, g: (_nt(r[1], g), _nn(r[0], g)))


@jax.custom_vjp
def rmul_const(a, m, mt):
    return _dot_x2c(a, m)


rmul_const.defvjp(lambda a, m, mt: (_dot_x2c(a, m), (m, mt)),
                  lambda r, g: (_dot_x2c(g, r[1]), jnp.zeros_like(r[0]), jnp.zeros_like(r[1])))


@jax.custom_vjp
def lmul_const(m, mt, a):
    return _dot_cx2(m, a)


lmul_const.defvjp(lambda m, mt, a: (_dot_cx2(m, a), (m, mt)),
                  lambda r, g: (jnp.zeros_like(r[0]), jnp.zeros_like(r[1]), _dot_cx2(r[1], g)))


@jax.custom_vjp
def mm_hl(t, x):
    th, tl = _split2(t)
    xb = x.astype(bf16)
    return _pdot(th, xb) + _pdot(tl, xb)


def _mm_hl_bwd(r, g):
    t, x = r
    th, tl = _split2(t)
    gb = g.astype(bf16)
    return _nt(g, x), _dg(th, gb, "tn") + _dg(tl, gb, "tn")


mm_hl.defvjp(lambda t, x: (mm_hl(t, x), (t, x)), _mm_hl_bwd)


def inv_unit_lower(lm):
    c = lm.shape[-1]
    r, cc = _iota2((c, c))
    eye = (r == cc).astype(f32)
    t = eye - lm
    p = -lm
    k = 1
    while 2 * k < c:
        p = _nn(p, p)
        t = t + _nn(t, p)
        k *= 2
    res = eye - t - _dot_hp(lm, t)
    return t + _nn(t, res)


@jax.custom_vjp
def inv_given(lm, t):
    return t


inv_given.defvjp(lambda lm, t: (t, t), lambda t, g: (-_nt(_tn(t, g), t), jnp.zeros_like(t)))


def _sigmoid(x):
    return 1.0 / (1.0 + jnp.exp(-x))


def _softplus(x):
    return jnp.maximum(x, 0.0) + jnp.log(1.0 + jnp.exp(-jnp.abs(x)))


def _silu(x):
    return x * _sigmoid(x)


def _gelu(x):
    return 0.5 * x * (1.0 + jnp.tanh(0.7978845608028654 * (x + 0.044715 * (x * x * x))))


def _iota2(shape):
    return lax.broadcasted_iota(jnp.int32, shape, 0), lax.broadcasted_iota(jnp.int32, shape, 1)


def _group_avg_mats():
    r, c = _iota2((128, 128))
    return jnp.where((r // 64) == (c // 64), 1.0 / 64.0, 0.0).astype(bf16)


def _pair_norm(x, gain, bavg):
    ms = rmul_const(x * x, bavg, bavg)
    return x * lax.rsqrt(ms + NORM_EPS) * gain


def _rms(x):
    r = lax.rsqrt(jnp.mean(x * x, axis=-1, keepdims=True) + NORM_EPS)
    return r


_IN_GROUPS = ((C_QKV, C_Z), (C_Z, C_AB), (C_AB, C_AB + 128), (C_SB, C_SG), (C_SG, IN_PAD))


def inproj_fwd(x, g, wp, tm=256):
    m = x.shape[0]

    def body(x_ref, g_ref, w_ref, *outs):
        xv = x_ref[...]
        h = (xv * _rms(xv) * g_ref[...]).astype(bf16)
        outs[-1][...] = h
        for (a, b), o in zip(_IN_GROUPS, outs):
            o[...] = _pdot(h, w_ref[:, a:b])

    widths = [b - a for a, b in _IN_GROUPS]
    return pl.pallas_call(
        body, name="inproj_fwd", grid=(m // tm,),
        in_specs=[pl.BlockSpec((tm, D_MODEL), lambda i: (i, 0)), pl.BlockSpec((1, D_MODEL), lambda i: (0, 0)),
                  pl.BlockSpec((D_MODEL, IN_PAD), lambda i: (0, 0))],
        out_specs=[pl.BlockSpec((tm, wd), lambda i: (i, 0)) for wd in widths + [D_MODEL]],
        out_shape=[SDS((m, wd), f32) for wd in widths] + [SDS((m, D_MODEL), bf16)],
        compiler_params=_cp(("arbitrary",)),
    )(x, g, wp)


def inproj_bwd(x, g, wp, dproj, dres, tm=256):
    m = x.shape[0]

    def body(x_ref, g_ref, w_ref, dp_ref, dr_ref, dx_ref, dg_ref):
        xv = x_ref[...]
        r = _rms(xv)
        xn = xv * r
        gv = g_ref[...]
        dh = lax.dot_general(dp_ref[...], w_ref[...], NT, preferred_element_type=f32)
        dxn = dh * gv
        dx_ref[...] = dr_ref[...] + r * (dxn - xn * jnp.mean(dxn * xn, axis=-1, keepdims=True))

        @pl.when(pl.program_id(0) == 0)
        def _():
            dg_ref[...] = jnp.zeros_like(dg_ref)

        dg_ref[...] += jnp.sum(dh * xn, axis=0, keepdims=True)

    return pl.pallas_call(
        body, name="inproj_bwd", grid=(m // tm,),
        in_specs=[pl.BlockSpec((tm, D_MODEL), lambda i: (i, 0)), pl.BlockSpec((1, D_MODEL), lambda i: (0, 0)),
                  pl.BlockSpec((D_MODEL, IN_PAD), lambda i: (0, 0)), pl.BlockSpec((tm, IN_PAD), lambda i: (i, 0)),
                  pl.BlockSpec((tm, D_MODEL), lambda i: (i, 0))],
        out_specs=[pl.BlockSpec((tm, D_MODEL), lambda i: (i, 0)), pl.BlockSpec((1, D_MODEL), lambda i: (0, 0))],
        out_shape=[SDS((m, D_MODEL), f32), SDS((1, D_MODEL), f32)],
        compiler_params=_cp(("arbitrary",)),
    )(x, g, wp, dproj, dres)


def outproj_fwd(x, odn, osb, osg, wo, tm=512):
    m = x.shape[0]

    def body(x_ref, a_ref, b_ref, c_ref, w_ref, x2_ref, mix_ref):
        mix_ref[:, 0:DN_WIDTH] = a_ref[...].astype(bf16)
        mix_ref[:, DN_WIDTH:DN_WIDTH + SB_WIDTH] = b_ref[...].astype(bf16)
        mix_ref[:, DN_WIDTH + SB_WIDTH:D_MODEL] = c_ref[...].astype(bf16)
        x2_ref[...] = x_ref[...] + _pdot(mix_ref[...], w_ref[...])

    row = lambda w: pl.BlockSpec((tm, w), lambda i: (i, 0))
    return pl.pallas_call(
        body, name="outproj_fwd", grid=(m // tm,),
        in_specs=[row(D_MODEL), row(DN_WIDTH), row(SB_WIDTH), row(SG_WIDTH), pl.BlockSpec((D_MODEL, D_MODEL), lambda i: (0, 0))],
        out_specs=[row(D_MODEL), row(D_MODEL)],
        out_shape=[SDS((m, D_MODEL), f32), SDS((m, D_MODEL), bf16)],
        compiler_params=_cp(("arbitrary",)),
    )(x, odn, osb, osg, wo)


def outproj_bwd(dx2, wo, tm=512):
    m = dx2.shape[0]

    def body(d_ref, w_ref, a_ref, b_ref, c_ref, db_ref):
        db = d_ref[...].astype(bf16)
        db_ref[...] = db
        dm = lax.dot_general(db, w_ref[...], NT, preferred_element_type=f32)
        a_ref[...] = dm[:, 0:DN_WIDTH]
        b_ref[...] = dm[:, DN_WIDTH:DN_WIDTH + SB_WIDTH]
        c_ref[...] = dm[:, DN_WIDTH + SB_WIDTH:D_MODEL]

    row = lambda w: pl.BlockSpec((tm, w), lambda i: (i, 0))
    return pl.pallas_call(
        body, name="outproj_bwd", grid=(m // tm,),
        in_specs=[row(D_MODEL), pl.BlockSpec((D_MODEL, D_MODEL), lambda i: (0, 0))],
        out_specs=[row(DN_WIDTH), row(SB_WIDTH), row(SG_WIDTH), row(D_MODEL)],
        out_shape=[SDS((m, DN_WIDTH), f32), SDS((m, SB_WIDTH), f32), SDS((m, SG_WIDTH), f32), SDS((m, D_MODEL), bf16)],
        compiler_params=_cp(("arbitrary",)),
    )(dx2, wo)


FF_CHUNK = 1024


def _load_weights_once(pairs, sem):
    @pl.when(pl.program_id(0) == 0)
    def _():
        cps = [pltpu.make_async_copy(h, v, sem.at[i]) for i, (h, v) in enumerate(pairs)]
        for c in cps:
            c.start()
        for c in cps:
            c.wait()


def ffn_fwd(x2, g, w1, w2, tm=256):
    m = x2.shape[0]

    def body(x_ref, g_ref, w1_hbm, w2_hbm, y_ref, rl_ref, w1_v, w2_v, sem):
        _load_weights_once(((w1_hbm, w1_v), (w2_hbm, w2_v)), sem)
        xv = x_ref[...]
        h = (xv * _rms(xv) * g_ref[...]).astype(bf16)
        acc = xv
        for j in range(0, D_FF, FF_CHUNK):
            f = _pdot(h, w1_v[:, j:j + FF_CHUNK])
            rl = jnp.maximum(f, 0.0)
            rl_ref[:, j:j + FF_CHUNK] = rl.astype(bf16)
            acc = acc + _pdot((rl * rl).astype(bf16), w2_v[j:j + FF_CHUNK, :])
        y_ref[...] = acc

    return pl.pallas_call(
        body, name="ffn_fwd", grid=(m // tm,),
        in_specs=[pl.BlockSpec((tm, D_MODEL), lambda i: (i, 0)), pl.BlockSpec((1, D_MODEL), lambda i: (0, 0)),
                  pl.BlockSpec(memory_space=pl.ANY), pl.BlockSpec(memory_space=pl.ANY)],
        out_specs=[pl.BlockSpec((tm, D_MODEL), lambda i: (i, 0)), pl.BlockSpec((tm, D_FF), lambda i: (i, 0))],
        out_shape=[SDS((m, D_MODEL), f32), SDS((m, D_FF), bf16)],
        scratch_shapes=[pltpu.VMEM((D_MODEL, D_FF), bf16), pltpu.VMEM((D_FF, D_MODEL), bf16), pltpu.SemaphoreType.DMA((2,))],
        compiler_params=_cp(("arbitrary",)),
    )(x2, g, w1, w2)


def ffn_bwd(x2, g, w1, w2, rlb, dy, tm=256):
    m = x2.shape[0]

    def body(x_ref, g_ref, w1_hbm, w2_hbm, rl_ref, dy_ref, dx_ref, dg_ref, h_ref, a_ref, df_ref, dyb_ref, w1_v, w2_v, sem):
        _load_weights_once(((w1_hbm, w1_v), (w2_hbm, w2_v)), sem)
        xv = x_ref[...]
        r = _rms(xv)
        xn = xv * r
        gv = g_ref[...]
        h = (xn * gv).astype(bf16)
        h_ref[...] = h
        dyv = dy_ref[...]
        dyb = dyv.astype(bf16)
        dyb_ref[...] = dyb
        dh = jnp.zeros((tm, D_MODEL), f32)
        for j in range(0, D_FF, FF_CHUNK):
            rl = rl_ref[:, j:j + FF_CHUNK].astype(f32)
            a_ref[:, j:j + FF_CHUNK] = (rl * rl).astype(bf16)
            da = lax.dot_general(dyb, w2_v[j:j + FF_CHUNK, :], NT, preferred_element_type=f32)
            df = (da * (2.0 * rl)).astype(bf16)
            df_ref[:, j:j + FF_CHUNK] = df
            dh = dh + lax.dot_general(df, w1_v[:, j:j + FF_CHUNK], NT, preferred_element_type=f32)
        dxn = dh * gv
        dx_ref[...] = dyv + r * (dxn - xn * jnp.mean(dxn * xn, axis=-1, keepdims=True))

        @pl.when(pl.program_id(0) == 0)
        def _():
            dg_ref[...] = jnp.zeros_like(dg_ref)

        dg_ref[...] += jnp.sum(dh * xn, axis=0, keepdims=True)

    row = lambda w: pl.BlockSpec((tm, w), lambda i: (i, 0))
    return pl.pallas_call(
        body, name="ffn_bwd", grid=(m // tm,),
        in_specs=[row(D_MODEL), pl.BlockSpec((1, D_MODEL), lambda i: (0, 0)),
                  pl.BlockSpec(memory_space=pl.ANY), pl.BlockSpec(memory_space=pl.ANY), row(D_FF), row(D_MODEL)],
        out_specs=[row(D_MODEL), pl.BlockSpec((1, D_MODEL), lambda i: (0, 0)), row(D_MODEL), row(D_FF), row(D_FF), row(D_MODEL)],
        out_shape=[SDS((m, D_MODEL), f32), SDS((1, D_MODEL), f32), SDS((m, D_MODEL), bf16), SDS((m, D_FF), bf16),
                   SDS((m, D_FF), bf16), SDS((m, D_MODEL), bf16)],
        scratch_shapes=[pltpu.VMEM((D_MODEL, D_FF), bf16), pltpu.VMEM((D_FF, D_MODEL), bf16), pltpu.SemaphoreType.DMA((2,))],
        compiler_params=_cp(("arbitrary",)),
    )(x2, g, w1, w2, rlb, dy)


def _tile(n, cap):
    best = 128
    for t in range(128, cap + 1, 128):
        if n % t == 0:
            best = t
    return best


def tn_matmul(a, b, name, col_shards=1, tk=2048):
    m, ka = a.shape
    n = b.shape[1]
    ti = _tile(ka, 1024)
    tj = _tile(n // col_shards, 1152)
    tk = min(tk, m)
    nk = m // tk
    jps = (n // col_shards) // tj

    def body(a_ref, b_ref, o_ref, acc):
        k = pl.program_id(2)

        @pl.when(k == 0)
        def _():
            acc[...] = jnp.zeros_like(acc)

        acc[...] += lax.dot_general(a_ref[...], b_ref[...], TN, preferred_element_type=f32)

        @pl.when(k == nk - 1)
        def _():
            o_ref[...] = acc[...].astype(bf16).reshape(o_ref.shape)

    if col_shards == 1:
        out_shape, out_spec = SDS((ka, n), bf16), pl.BlockSpec((ti, tj), lambda i, j, k: (i, j))
    else:
        out_shape = SDS((col_shards, ka, n // col_shards), bf16)
        out_spec = pl.BlockSpec((1, ti, tj), lambda i, j, k: (j // jps, i, j % jps))
    return pl.pallas_call(
        body, name=name, grid=(ka // ti, n // tj, nk),
        in_specs=[pl.BlockSpec((tk, ti), lambda i, j, k: (k, i)), pl.BlockSpec((tk, tj), lambda i, j, k: (k, j))],
        out_specs=out_spec, out_shape=out_shape,
        scratch_shapes=[pltpu.VMEM((ti, tj), f32)],
        compiler_params=_cp(("arbitrary", "arbitrary", "arbitrary")),
    )(a, b)


def loss_head(y, tgt, tm=512):
    m = y.shape[0]

    def body(y_ref, t_ref, dy_ref, l_ref):
        e = y_ref[...] - t_ref[...]
        dy_ref[...] = e * (1.0 / D_MODEL)

        @pl.when(pl.program_id(0) == 0)
        def _():
            l_ref[...] = jnp.zeros_like(l_ref)

        l_ref[...] += jnp.sum(e * e, axis=0, keepdims=True) * (0.5 / D_MODEL)

    row = pl.BlockSpec((tm, D_MODEL), lambda i: (i, 0))
    return pl.pallas_call(
        body, name="loss_head", grid=(m // tm,), in_specs=[row, row],
        out_specs=[row, pl.BlockSpec((1, D_MODEL), lambda i: (0, 0))],
        out_shape=[SDS((m, D_MODEL), f32), SDS((1, D_MODEL), f32)],
        compiler_params=_cp(("arbitrary",)),
    )(y, tgt)


def _dn_consts():
    c = DN_CHUNK
    r, cc = _iota2((c, c))
    lt = (cc <= r).astype(bf16)
    ltt = (r <= cc).astype(bf16)
    return lt, ltt


def dn_chunk(cq, ck, cv, g, beta, z, s, gain, lt, ltt, t_given=None):
    c = DN_CHUNK
    r, cc = _iota2((c, c))
    q = cq * lax.rsqrt(jnp.sum(cq * cq, axis=-1, keepdims=True) + NORM_EPS) * (DN_DIM ** -0.5)
    k = ck * lax.rsqrt(jnp.sum(ck * ck, axis=-1, keepdims=True) + NORM_EPS)
    r2, c2 = _iota2((c, 128))
    uaug = jnp.where((c2 < c) & (r2 > c2), 1.0, 0.0) + jnp.where(c2 == c, 1.0, 0.0)
    gam_all = lmul_const(lt, ltt, g * uaug)
    gam_cc = gam_all[:, :, 0:c]
    gam = gam_all[:, :, c:c + 1]
    dec = jnp.where(cc <= r, jnp.exp(jnp.where(cc <= r, gam_cc, 0.0)), 0.0)
    kk = mm_nt(k, k)
    lm = jnp.where(cc < r, beta * kk * dec, 0.0)
    t = inv_unit_lower(lm) if t_given is None else inv_given(lm, t_given)
    eg = jnp.exp(gam)
    sol = mm_hl(t, jnp.concatenate([cv * beta, k * (beta * eg)], axis=2))
    u, w = sol[:, :, 0:DN_DIM], sol[:, :, DN_DIM:2 * DN_DIM]
    qk = jnp.where(cc <= r, mm_nt(q, k) * dec, 0.0)
    glast = jnp.sum(g, axis=1, keepdims=True)
    qd = q * eg
    kd = k * jnp.exp(glast - gam)
    un = u - mm(w, s)
    o = mm(qd, s) + mm(qk, un)
    s_new = s * jnp.exp(glast) + mm_tn(kd, un)
    on = o * lax.rsqrt(jnp.mean(o * o, axis=-1, keepdims=True) + NORM_EPS) * gain * _silu(z)
    return on, s_new, t


def _dn_gates(ab, al_row, dt_row):
    pre = ab + dt_row
    return -jnp.exp(al_row) * _softplus(pre), _sigmoid(ab), _sigmoid(pre)


def _dn_chains(cacts, gates, z_ref):
    cq, ck, cv, g, beta, z = [], [], [], [], [], []
    for bi, cact in enumerate(cacts):
        for h in range(DN_HEADS):
            cq.append(cact[:, h * DN_DIM:(h + 1) * DN_DIM])
            ck.append(cact[:, DN_WIDTH + h * DN_DIM:DN_WIDTH + (h + 1) * DN_DIM])
            cv.append(cact[:, 2 * DN_WIDTH + h * DN_DIM:2 * DN_WIDTH + (h + 1) * DN_DIM])
            g.append(gates[bi][0][:, h:h + 1])
            beta.append(gates[bi][1][:, DN_HEADS + h:DN_HEADS + h + 1])
            z.append(z_ref[bi, :, h * DN_DIM:(h + 1) * DN_DIM])
    return tuple(jnp.stack(v) for v in (cq, ck, cv, g, beta, z))


def _conv_rows(xe_ref, b, w_ref):
    y = w_ref[0:1, :] * xe_ref[b, pl.ds(5, DN_CHUNK), :]
    for i in range(1, DN_CONV):
        y = y + w_ref[i:i + 1, :] * xe_ref[b, pl.ds(5 + i, DN_CHUNK), :]
    return y


def dn_fwd(qkv, z, ab, conv_w, alog, dtb, gain):
    bsz, t, _ = qkv.shape
    nc = t // DN_CHUNK
    c = DN_CHUNK
    nh = bsz * DN_HEADS

    def body(qkv_ref, z_ref, ab_ref, w_ref, al_ref, dt_ref, g_ref, o_ref, sall_ref, tall_ref, xe, s_sc):
        n = pl.program_id(0)

        @pl.when(n == 0)
        def _():
            xe[:, 0:8, :] = jnp.zeros((bsz, 8, 3 * DN_WIDTH), f32)
            s_sc[...] = jnp.zeros_like(s_sc)

        lt, ltt = _dn_consts()
        cacts = []
        for b in range(bsz):
            xe[b, 8:8 + c, :] = qkv_ref[b]
            cacts.append(_silu(_conv_rows(xe, b, w_ref)))
            xe[b, 0:8, :] = xe[b, c:c + 8, :]
        gates = [_dn_gates(ab_ref[b], al_ref[...], dt_ref[...]) for b in range(bsz)]
        s = s_sc[...]
        sall_ref[0] = s
        on, sn, tt = dn_chunk(*_dn_chains(cacts, gates, z_ref), s, g_ref[...], lt, ltt)
        tall_ref[0] = tt
        s_sc[...] = sn
        for b in range(bsz):
            for h in range(DN_HEADS):
                o_ref[b, :, h * DN_DIM:(h + 1) * DN_DIM] = on[b * DN_HEADS + h]

    blk = lambda w: pl.BlockSpec((bsz, c, w), lambda n: (0, n, 0))
    full = lambda shp: pl.BlockSpec(shp, lambda n: (0,) * len(shp))
    return pl.pallas_call(
        body, name="dn_fwd", grid=(nc,),
        in_specs=[blk(3 * DN_WIDTH), blk(DN_WIDTH), blk(128), full((8, 3 * DN_WIDTH)), full((1, 128)), full((1, 128)), full((1, 128))],
        out_specs=[blk(DN_WIDTH), pl.BlockSpec((1, nh, DN_DIM, DN_DIM), lambda n: (n, 0, 0, 0)),
                   pl.BlockSpec((1, nh, c, c), lambda n: (n, 0, 0, 0))],
        out_shape=[SDS((bsz, t, DN_WIDTH), f32), SDS((nc, nh, DN_DIM, DN_DIM), f32), SDS((nc, nh, c, c), f32)],
        scratch_shapes=[pltpu.VMEM((bsz, c + 8, 3 * DN_WIDTH), f32), pltpu.VMEM((nh, DN_DIM, DN_DIM), f32)],
        compiler_params=_cp(("arbitrary",)),
    )(qkv, z, ab, conv_w, alog, dtb, gain)


def dn_bwd(qkv, z, ab, conv_w, alog, dtb, gain, sall, tall, do):
    bsz, t, _ = qkv.shape
    nc = t // DN_CHUNK
    c = DN_CHUNK
    nh = bsz * DN_HEADS
    w3 = 3 * DN_WIDTH

    def body(qkv_ref, prev_ref, z_ref, ab_ref, w_ref, al_ref, dt_ref, g_ref, sall_ref, tall_ref, do_ref,
             dp_ref, dw_ref, dal_ref, ddt_ref, dg_ref, xe, dye, dc_sc, ds_sc):
        n = pl.program_id(0)
        first = (nc - 1 - n) == 0

        @pl.when(n == 0)
        def _():
            dye[:, c:c + 8, :] = jnp.zeros((bsz, 8, w3), f32)
            ds_sc[...] = jnp.zeros_like(ds_sc)
            dw_ref[...] = jnp.zeros_like(dw_ref)
            dal_ref[...] = jnp.zeros_like(dal_ref)
            ddt_ref[...] = jnp.zeros_like(ddt_ref)
            dg_ref[...] = jnp.zeros_like(dg_ref)

        lt, ltt = _dn_consts()
        lane = lax.broadcasted_iota(jnp.int32, (1, 128), 1)
        lane_c = lax.broadcasted_iota(jnp.int32, (c, 128), 1)
        ys, sigs = [], []
        for b in range(bsz):
            xe[b, 0:8, :] = jnp.where(first, 0.0, prev_ref[b])
            xe[b, 8:8 + c, :] = qkv_ref[b]
            ys.append(_conv_rows(xe, b, w_ref))
            sigs.append(_sigmoid(ys[b]))
        gates = [_dn_gates(ab_ref[b], al_ref[...], dt_ref[...]) for b in range(bsz)]
        ops = _dn_chains([y * sg for y, sg in zip(ys, sigs)], gates, z_ref)
        tt = tall_ref[0]
        _, vjp = jax.vjp(lambda *p: dn_chunk(*p, lt, ltt, t_given=tt)[0:2], *ops, sall_ref[0], g_ref[...])
        don = jnp.stack([do_ref[b, :, h * DN_DIM:(h + 1) * DN_DIM] for b in range(bsz) for h in range(DN_HEADS)])
        dcq, dck, dcv, dg, dbeta, dzz, dsp, dgn = vjp((don, ds_sc[...]))
        ds_sc[...] = dsp
        dg_ref[...] += dgn
        for b in range(bsz):
            dgate = jnp.zeros((c, 128), f32)
            for h in range(DN_HEADS):
                i = b * DN_HEADS + h
                dc_sc[b, :, h * DN_DIM:(h + 1) * DN_DIM] = dcq[i]
                dc_sc[b, :, DN_WIDTH + h * DN_DIM:DN_WIDTH + (h + 1) * DN_DIM] = dck[i]
                dc_sc[b, :, 2 * DN_WIDTH + h * DN_DIM:2 * DN_WIDTH + (h + 1) * DN_DIM] = dcv[i]
                dp_ref[b, :, C_Z + h * DN_DIM:C_Z + (h + 1) * DN_DIM] = dzz[i].astype(bf16)
                dgate = dgate + jnp.where(lane_c == h, dg[i], 0.0) + jnp.where(lane_c == DN_HEADS + h, dbeta[i], 0.0)
            gg, beta, sig_pre = gates[b]
            is_g = lane_c < DN_HEADS
            dpre = jnp.where(is_g, dgate * (-jnp.exp(al_ref[...])) * sig_pre, 0.0)
            dp_ref[b, :, C_AB:C_AB + 128] = (dpre + jnp.where(is_g, 0.0, dgate * beta * (1.0 - beta))).astype(bf16)
            dp_ref[b, :, C_AB + 128:DN_COLS] = jnp.zeros((c, DN_COLS - C_AB - 128), bf16)
            dal_ref[...] += jnp.sum(jnp.where(is_g, dgate * gg, 0.0), axis=0, keepdims=True)
            ddt_ref[...] += jnp.sum(dpre, axis=0, keepdims=True)
            y, sig = ys[b], sigs[b]
            dy = dc_sc[b] * (sig * (1.0 + y * (1.0 - sig)))
            dye[b, 0:c, :] = dy
            dx = w_ref[3:4, :] * dy
            for i in range(DN_CONV - 1):
                dx = dx + w_ref[i:i + 1, :] * dye[b, pl.ds(3 - i, c), :]
            dp_ref[b, :, 0:w3] = dx.astype(bf16)
            for i in range(DN_CONV):
                dw_ref[i:i + 1, :] += jnp.sum(dy * xe[b, pl.ds(5 + i, c), :], axis=0, keepdims=True)
            dye[b, c:c + 8, :] = dye[b, 0:8, :]

    rev = lambda w: pl.BlockSpec((bsz, c, w), lambda n: (0, nc - 1 - n, 0))
    full = lambda shp: pl.BlockSpec(shp, lambda n: (0,) * len(shp))
    prev = pl.BlockSpec((bsz, 8, w3), lambda n: (0, jnp.maximum((nc - 1 - n) * (c // 8) - 1, 0), 0))
    return pl.pallas_call(
        body, name="dn_bwd", grid=(nc,),
        in_specs=[rev(w3), prev, rev(DN_WIDTH), rev(128), full((8, w3)), full((1, 128)), full((1, 128)), full((1, 128)),
                  pl.BlockSpec((1, nh, DN_DIM, DN_DIM), lambda n: (nc - 1 - n, 0, 0, 0)),
                  pl.BlockSpec((1, nh, c, c), lambda n: (nc - 1 - n, 0, 0, 0)), rev(DN_WIDTH)],
        out_specs=[rev(DN_COLS), full((8, w3)), full((1, 128)), full((1, 128)), full((1, 128))],
        out_shape=[SDS((bsz, t, IN_PAD), bf16), SDS((8, w3), f32), SDS((1, 128), f32), SDS((1, 128), f32), SDS((1, 128), f32)],
        scratch_shapes=[pltpu.VMEM((bsz, c + 8, w3), f32), pltpu.VMEM((bsz, c + 8, w3), f32), pltpu.VMEM((bsz, c, w3), f32),
                        pltpu.VMEM((nh, DN_DIM, DN_DIM), f32)],
        compiler_params=_cp(("arbitrary",)),
    )(qkv, qkv, z, ab, conv_w, alog, dtb, gain, sall, tall, do)


SB_TILE = 256
SB_QTILE, SB_KTILE = 256, 256
SB_PAIRS = SB_HEADS // 2


def sb_fwd(sbqkv, gq, gk):
    bsz, t, _ = sbqkv.shape
    bq = min(SB_QTILE, t)
    blk = max(min(SB_KTILE, t), bq)
    nq = t // bq
    scale = SB_DIM ** -0.5

    def body(q_ref, k_ref, v_ref, gq_ref, gk_ref, o_ref, l_ref, q2_sc, kn_sc, v_sc):
        bavg = _group_avg_mats()
        lane = lax.broadcasted_iota(jnp.int32, (1, 128), 1)
        first = lane < SB_DIM
        for p in range(SB_PAIRS):
            ls = slice(p * 128, (p + 1) * 128)
            qn = _pair_norm(q_ref[0, :, ls], gq_ref[...], bavg)
            kn_sc[p] = _pair_norm(k_ref[0, :, ls], gk_ref[...], bavg).astype(bf16)
            v_sc[p] = v_ref[0, :, ls].astype(bf16)
            q2_sc[2 * p] = jnp.where(first, qn, 0.0).astype(bf16)
            q2_sc[2 * p + 1] = jnp.where(first, 0.0, qn).astype(bf16)
        r, c = _iota2((blk, blk))
        ustrict = (r > c).astype(bf16)
        r2, c2 = _iota2((2 * bq, blk))

        def tile(q2s, ks, carry, causal):
            out = []
            for p in range(SB_PAIRS):
                acc, rr = carry[2 * p], carry[2 * p + 1]
                zz = lax.dot_general(q2s[p], kn_sc[p, pl.ds(ks, blk), :], NT, preferred_element_type=f32) * scale
                sp = _softplus(zz)
                lm = -sp if causal is None else jnp.where(causal, -sp, 0.0)
                rem = _dot_x2c(lm, ustrict)
                wgt = jnp.exp(zz - sp + rem + rr)
                if causal is not None:
                    wgt = jnp.where(causal, wgt, 0.0)
                out += [acc + _pdot(wgt.astype(bf16), v_sc[p, pl.ds(ks, blk), :]), rr + jnp.sum(lm, axis=1, keepdims=True)]
            return tuple(out)

        def qloop(qi, _):
            qs = pl.multiple_of(qi * bq, bq)
            kd = qs // blk
            causal = c2 < (r2 & (bq - 1)) + (qs - kd * blk)
            q2s = [jnp.concatenate([q2_sc[2 * p, pl.ds(qs, bq), :], q2_sc[2 * p + 1, pl.ds(qs, bq), :]], axis=0)
                   for p in range(SB_PAIRS)]
            zero = (jnp.zeros((2 * bq, 128), f32), jnp.zeros((2 * bq, 1), f32)) * SB_PAIRS
            carry = lax.fori_loop(1, kd + 1, lambda i, cr: tile(q2s, pl.multiple_of((kd - i) * blk, blk), cr, None),
                                  tile(q2s, pl.multiple_of(kd * blk, blk), zero, causal))
            for p in range(SB_PAIRS):
                acc, rr = carry[2 * p], carry[2 * p + 1]
                o_ref[0, pl.ds(qs, bq), p * 128:(p + 1) * 128] = jnp.where(first, acc[0:bq], acc[bq:2 * bq])
                l_ref[0, pl.ds(qs, bq), p * 128:(p + 1) * 128] = jnp.where(first, rr[0:bq], rr[bq:2 * bq])
            return 0

        lax.fori_loop(0, nq, qloop, 0)

    col = lambda off: pl.BlockSpec((1, t, SB_WIDTH), lambda b: (b, 0, off))
    gsp = pl.BlockSpec((1, 128), lambda b: (0, 0))
    return pl.pallas_call(
        body, name="sb_fwd", grid=(bsz,),
        in_specs=[col(0), col(1), col(2), gsp, gsp],
        out_specs=[col(0), col(0)],
        out_shape=[SDS((bsz, t, SB_WIDTH), f32), SDS((bsz, t, SB_WIDTH), f32)],
        scratch_shapes=[pltpu.VMEM((2 * SB_PAIRS, t, 128), bf16), pltpu.VMEM((SB_PAIRS, t, 128), bf16),
                        pltpu.VMEM((SB_PAIRS, t, 128), bf16)],
        compiler_params=_cp(("arbitrary",)),
    )(sbqkv, sbqkv, sbqkv, gq, gk)


def sb_bwd(sbqkv, gq, gk, ltot, do, dproj):
    bsz, t, _ = sbqkv.shape
    blk = min(SB_TILE, t)
    nq = t // blk
    scale = SB_DIM ** -0.5

    def body(q_ref, k_ref, v_ref, gq_ref, gk_ref, l_ref, do_ref, dp_in, dp_ref, dgq_ref, dgk_ref,
             q2_sc, kn_sc, v_sc, do2_sc, dqn_sc, dkn_sc, dv_sc):
        bavg = _group_avg_mats()
        lane = lax.broadcasted_iota(jnp.int32, (1, 128), 1)
        first = lane < SB_DIM
        fq = lambda x, g: _pair_norm(x, g, bavg)
        vjps = []
        for p in range(SB_PAIRS):
            ls = slice(p * 128, (p + 1) * 128)
            qn, q_vjp = jax.vjp(fq, q_ref[0, :, ls], gq_ref[...])
            kn, k_vjp = jax.vjp(fq, k_ref[0, :, ls], gk_ref[...])
            vjps.append((q_vjp, k_vjp))
            kn_sc[p] = kn.astype(bf16)
            v_sc[p] = v_ref[0, :, ls].astype(bf16)
            dov = do_ref[0, :, ls]
            q2_sc[2 * p] = jnp.where(first, qn, 0.0).astype(bf16)
            q2_sc[2 * p + 1] = jnp.where(first, 0.0, qn).astype(bf16)
            do2_sc[2 * p] = jnp.where(first, dov, 0.0).astype(bf16)
            do2_sc[2 * p + 1] = jnp.where(first, 0.0, dov).astype(bf16)
        dkn_sc[...] = jnp.zeros_like(dkn_sc)
        dv_sc[...] = jnp.zeros_like(dv_sc)
        r, c = _iota2((blk, blk))
        pincl = (r <= c).astype(bf16)
        pstrict = (r < c).astype(bf16)
        r2, c2 = _iota2((2 * blk, blk))
        causal = c2 < (r2 & (blk - 1))

        def tile(q2s, do2s, lts, ks, carry, diag):
            out = []
            for p in range(SB_PAIRS):
                dq, cs, ce = carry[3 * p:3 * p + 3]
                q2, do2 = q2s[p], do2s[p]
                kb = kn_sc[p, pl.ds(ks, blk), :]
                zz = lax.dot_general(q2, kb, NT, preferred_element_type=f32) * scale
                sp = _softplus(zz)
                lm = jnp.where(causal, -sp, 0.0) if diag else -sp
                pre = _dot_x2c(lm, pincl)
                lp = zz - sp
                wgt = jnp.exp(lp + (lts[p] - cs - pre))
                if diag:
                    wgt = jnp.where(causal, wgt, 0.0)
                dw = lax.dot_general(do2, v_sc[p, pl.ds(ks, blk), :], NT, preferred_element_type=f32)
                e = wgt * dw
                ee = ce + _dot_x2c(e, pstrict)
                sig = jnp.exp(lp)
                dz = (e * (1.0 - sig) - ee * sig) * scale
                if diag:
                    dz = jnp.where(causal, dz, 0.0)
                dz = dz.astype(bf16)
                dkn_sc[p, pl.ds(ks, blk), :] += lax.dot_general(dz, q2, TN, preferred_element_type=f32)
                dv_sc[p, pl.ds(ks, blk), :] += lax.dot_general(wgt.astype(bf16), do2, TN, preferred_element_type=f32)
                out += [dq + _pdot(dz, kb), cs + jnp.sum(lm, axis=1, keepdims=True), ce + jnp.sum(e, axis=1, keepdims=True)]
            return tuple(out)

        def qloop(qi, _):
            qs = pl.multiple_of(qi * blk, blk)
            rows = pl.ds(qs, blk)
            q2s = [jnp.concatenate([q2_sc[2 * p, rows, :], q2_sc[2 * p + 1, rows, :]], axis=0) for p in range(SB_PAIRS)]
            do2s = [jnp.concatenate([do2_sc[2 * p, rows, :], do2_sc[2 * p + 1, rows, :]], axis=0) for p in range(SB_PAIRS)]
            lts = [jnp.concatenate([l_ref[0, rows, p * 128:p * 128 + 1], l_ref[0, rows, p * 128 + SB_DIM:p * 128 + SB_DIM + 1]],
                                   axis=0) for p in range(SB_PAIRS)]
            z1 = jnp.zeros((2 * blk, 1), f32)
            carry = lax.fori_loop(0, qi, lambda kj, cr: tile(q2s, do2s, lts, pl.multiple_of(kj * blk, blk), cr, False),
                                  (jnp.zeros((2 * blk, 128), f32), z1, z1) * SB_PAIRS)
            carry = tile(q2s, do2s, lts, qs, carry, True)
            for p in range(SB_PAIRS):
                dq = carry[3 * p]
                dqn_sc[p, rows, :] = jnp.where(first, dq[0:blk], dq[blk:2 * blk])
            return 0

        lax.fori_loop(0, nq, qloop, 0)
        dgq_tot, dgk_tot = jnp.zeros((1, 128), f32), jnp.zeros((1, 128), f32)
        for p in range(SB_PAIRS):
            ls = slice(p * 128, (p + 1) * 128)
            dq_pre, dgq = vjps[p][0](dqn_sc[p])
            dk_pre, dgk = vjps[p][1](dkn_sc[p])
            dp_ref[0, :, p * 128:(p + 1) * 128] = dq_pre.astype(bf16)
            dp_ref[0, :, SB_WIDTH + p * 128:SB_WIDTH + (p + 1) * 128] = dk_pre.astype(bf16)
            dp_ref[0, :, 2 * SB_WIDTH + p * 128:2 * SB_WIDTH + (p + 1) * 128] = dv_sc[p].astype(bf16)
            dgq_tot, dgk_tot = dgq_tot + dgq, dgk_tot + dgk
        dgq_ref[0] = jnp.broadcast_to(dgq_tot, (8, 128))
        dgk_ref[0] = jnp.broadcast_to(dgk_tot, (8, 128))

    col = lambda off: pl.BlockSpec((1, t, SB_WIDTH), lambda b: (b, 0, off), pipeline_mode=pl.Buffered(1))
    gsp = pl.BlockSpec((1, 128), lambda b: (0, 0))
    gout = pl.BlockSpec((1, 8, 128), lambda b: (b, 0, 0))
    return pl.pallas_call(
        body, name="sb_bwd", grid=(bsz,),
        in_specs=[col(0), col(1), col(2), gsp, gsp, col(0), col(0), pl.BlockSpec(memory_space=pl.ANY)],
        out_specs=[pl.BlockSpec((1, t, 3 * SB_WIDTH), lambda b: (b, 0, C_SB // (3 * SB_WIDTH)), pipeline_mode=pl.Buffered(1)),
                   gout, gout],
        out_shape=[SDS(dproj.shape, bf16)] + [SDS((bsz, 8, 128), f32)] * 2,
        input_output_aliases={7: 0},
        scratch_shapes=[pltpu.VMEM((2 * SB_PAIRS, t, 128), bf16), pltpu.VMEM((SB_PAIRS, t, 128), bf16),
                        pltpu.VMEM((SB_PAIRS, t, 128), bf16), pltpu.VMEM((2 * SB_PAIRS, t, 128), bf16),
                        pltpu.VMEM((SB_PAIRS, t, 128), f32), pltpu.VMEM((SB_PAIRS, t, 128), f32), pltpu.VMEM((SB_PAIRS, t, 128), f32)],
        compiler_params=_cp(("arbitrary",)),
    )(sbqkv, sbqkv, sbqkv, gq, gk, ltot, do, dproj)


def sg_pair(u, v, gain, wa, wb, ba, bb, bavg):
    r, c = _iota2((SG_CHUNK, SG_CHUNK))
    lane = lax.broadcasted_iota(jnp.int32, (1, 128), 1)
    first = lane < SG_DIM
    vn = _pair_norm(_gelu(v), gain, bavg)
    tri = c <= r
    mixed = (mm(jnp.where(tri, wa, 0.0), jnp.where(first, vn, 0.0)) + mm(jnp.where(tri, wb, 0.0), jnp.where(first, 0.0, vn))
             + jnp.where(first, ba, bb))
    return _gelu(u) * mixed


def sg_fwd(sguv, gain, w, bt):
    bsz, t, _ = sguv.shape
    nch = t // SG_CHUNK

    def body(uv_ref, g_ref, w_ref, b_ref, o_ref):
        bavg = _group_avg_mats()
        for p in range(2):
            ls = slice(p * 128, (p + 1) * 128)
            o_ref[0, :, ls] = sg_pair(uv_ref[0, :, ls], uv_ref[0, :, SG_WIDTH + p * 128:SG_WIDTH + (p + 1) * 128], g_ref[:, ls],
                                      w_ref[2 * p], w_ref[2 * p + 1], b_ref[:, 2 * p:2 * p + 1], b_ref[:, 2 * p + 1:2 * p + 2], bavg)

    full = lambda shp: pl.BlockSpec(shp, lambda b, n: (0,) * len(shp))
    return pl.pallas_call(
        body, name="sg_fwd", grid=(bsz, nch),
        in_specs=[pl.BlockSpec((1, SG_CHUNK, 2 * SG_WIDTH), lambda b, n: (b, n, 0)), full((1, SG_WIDTH)),
                  full((SG_GROUPS, SG_CHUNK, SG_CHUNK)), full((SG_CHUNK, 128))],
        out_specs=pl.BlockSpec((1, SG_CHUNK, SG_WIDTH), lambda b, n: (b, n, 0)),
        out_shape=SDS((bsz, t, SG_WIDTH), f32),
        compiler_params=_cp(("arbitrary", "arbitrary")),
    )(sguv, gain, w, bt)


def sg_bwd(sguv, gain, w, bt, do, dproj):
    bsz, t, _ = sguv.shape
    nch = t // SG_CHUNK

    def body(uv_ref, g_ref, w_ref, b_ref, do_ref, dp_in, duv_ref, dg_ref, dw_ref, db_ref):
        @pl.when((pl.program_id(0) == 0) & (pl.program_id(1) == 0))
        def _():
            dg_ref[...] = jnp.zeros_like(dg_ref)
            dw_ref[...] = jnp.zeros_like(dw_ref)
            db_ref[...] = jnp.zeros_like(db_ref)

        bavg = _group_avg_mats()
        lane = lax.broadcasted_iota(jnp.int32, (SG_CHUNK, 128), 1)
        dbt = jnp.zeros((SG_CHUNK, 128), f32)
        for p in range(2):
            ls = slice(p * 128, (p + 1) * 128)
            vs = slice(SG_WIDTH + p * 128, SG_WIDTH + (p + 1) * 128)
            prim = (uv_ref[0, :, ls], uv_ref[0, :, vs], g_ref[:, ls], w_ref[2 * p], w_ref[2 * p + 1],
                    b_ref[:, 2 * p:2 * p + 1], b_ref[:, 2 * p + 1:2 * p + 2])
            _, vjp = jax.vjp(lambda *a: sg_pair(*a, bavg), *prim)
            du, dv, dgn, dwa, dwb, dba, dbb = vjp(do_ref[0, :, ls])
            duv_ref[0, :, ls] = du.astype(bf16)
            duv_ref[0, :, vs] = dv.astype(bf16)
            dg_ref[:, ls] += dgn
            dw_ref[2 * p] += dwa
            dw_ref[2 * p + 1] += dwb
            dbt = dbt + jnp.where(lane == 2 * p, dba, 0.0) + jnp.where(lane == 2 * p + 1, dbb, 0.0)
        db_ref[...] += dbt

    full = lambda shp: pl.BlockSpec(shp, lambda b, n: (0,) * len(shp))
    return pl.pallas_call(
        body, name="sg_bwd", grid=(bsz, nch),
        in_specs=[pl.BlockSpec((1, SG_CHUNK, 2 * SG_WIDTH), lambda b, n: (b, n, 0)), full((1, SG_WIDTH)),
                  full((SG_GROUPS, SG_CHUNK, SG_CHUNK)), full((SG_CHUNK, 128)),
                  pl.BlockSpec((1, SG_CHUNK, SG_WIDTH), lambda b, n: (b, n, 0)), pl.BlockSpec(memory_space=pl.ANY)],
        out_specs=[pl.BlockSpec((1, SG_CHUNK, 2 * SG_WIDTH), lambda b, n: (b, n, C_SG // (2 * SG_WIDTH))), full((1, SG_WIDTH)),
                   full((SG_GROUPS, SG_CHUNK, SG_CHUNK)), full((SG_CHUNK, 128))],
        out_shape=[SDS(dproj.shape, bf16), SDS((1, SG_WIDTH), f32), SDS((SG_GROUPS, SG_CHUNK, SG_CHUNK), f32),
                   SDS((SG_CHUNK, 128), f32)],
        input_output_aliases={5: 0},
        compiler_params=_cp(("arbitrary", "arbitrary")),
    )(sguv, gain, w, bt, do, dproj)


def _pad_lanes(v, n=128):
    return jnp.pad(v.reshape(1, -1), ((0, 0), (0, n - v.size)))


def pad_w_in(w):
    return jnp.concatenate([w[:, 0:2048], jnp.pad(w[:, 2048:2056], ((0, 0), (0, C_SB - C_AB - 8))), w[:, 2056:]], axis=1)


def unpad_w_in(w):
    return jnp.concatenate([w[:, 0:2048], w[:, C_AB:C_AB + 8], w[:, C_SB:]], axis=1)


def _w_in_runs():
    shard, runs = IN_DIM // N_CHIPS, []
    for s in range(N_CHIPS):
        for a, b, d in ((0, 2048, 0), (2048, 2056, C_AB), (2056, IN_DIM, C_SB)):
            lo, hi = max(shard * s, a), min(shard * (s + 1), b)
            if lo < hi:
                runs.append((s, lo - shard * s, hi - shard * s, d + lo - a))
    return runs


def w_in_from_shards(zone, tr=256):
    def body(z_ref, o_ref):
        o_ref[:, C_AB:C_SB] = jnp.zeros((tr, C_SB - C_AB), zone.dtype)
        for s, a, b, d in _w_in_runs():
            o_ref[:, d:d + b - a] = z_ref[s, :, a:b]

    return pl.pallas_call(
        body, name="w_in_from_shards", grid=(D_MODEL // tr,),
        in_specs=[pl.BlockSpec((N_CHIPS, tr, IN_DIM // N_CHIPS), lambda i: (0, i, 0))],
        out_specs=pl.BlockSpec((tr, IN_PAD), lambda i: (i, 0)), out_shape=SDS((D_MODEL, IN_PAD), zone.dtype),
        compiler_params=_cp(("arbitrary",)))(zone)


def w_in_grad_to_shards(g, tr=256):
    def body(g_ref, o_ref):
        for s, a, b, d in _w_in_runs():
            o_ref[s, :, a:b] = g_ref[:, d:d + b - a]

    return pl.pallas_call(
        body, name="w_in_grad_to_shards", grid=(D_MODEL // tr,),
        in_specs=[pl.BlockSpec((tr, IN_PAD), lambda i: (i, 0))],
        out_specs=pl.BlockSpec((N_CHIPS, tr, IN_DIM // N_CHIPS), lambda i: (0, i, 0)),
        out_shape=SDS((N_CHIPS, D_MODEL, IN_DIM // N_CHIPS), g.dtype), compiler_params=_cp(("arbitrary",)))(g)


def layer_params(p, l):
    return dict(
        g1=p["norm1_g"][l].reshape(1, -1), g2=p["norm2_g"][l].reshape(1, -1),
        conv=jnp.pad(p["conv_w"][l], ((0, 4), (0, 0))), alog=_pad_lanes(p["a_log"][l]), dtb=_pad_lanes(p["dt_bias"][l]),
        dng=p["dn_out_g"][l].reshape(1, -1), gq=jnp.tile(p["sb_q_g"][l].reshape(1, -1), (1, 2)),
        gk=jnp.tile(p["sb_k_g"][l].reshape(1, -1), (1, 2)), sgg=p["sg_v_g"][l].reshape(1, -1), sgw=p["sg_w"][l],
        sgb=jnp.pad(p["sg_b"][l].T, ((0, 0), (0, 124))))


def local_step(x, tgt, small, get_w, put_g, sync_g):
    bsz, t, _ = x.shape
    m = bsz * t
    r3 = lambda a: a.reshape(bsz, t, a.shape[-1])
    r2 = lambda a: a.reshape(m, a.shape[-1])
    xs, saved, ws = x.reshape(m, D_MODEL), [], []
    for l in range(DEPTH):
        sp, w = layer_params(small, l), {}
        w["w_in"] = get_w(l, "in", xs)
        qkv, z, ab, sb, sg, h1 = inproj_fwd(xs, sp["g1"], w["w_in"])
        odn, sall, tall = dn_fwd(r3(qkv), r3(z), r3(ab), sp["conv"], sp["alog"], sp["dtb"], sp["dng"])
        osb, ltot = sb_fwd(r3(sb), sp["gq"], sp["gk"])
        osg = sg_fwd(r3(sg), sp["sgg"], sp["sgw"], sp["sgb"])
        w["w_out"] = get_w(l, "out", osg)
        x2, mix = outproj_fwd(xs, r2(odn), r2(osb), r2(osg), w["w_out"])
        w["w_ff1"], w["w_ff2"] = get_w(l, "ff", x2)
        x3, rlb = ffn_fwd(x2, sp["g2"], w["w_ff1"], w["w_ff2"])
        saved.append(dict(rlb=rlb, h1=h1, x=xs,qkv=qkv, z=z, ab=ab, sb=sb, sg=sg, sall=sall, tall=tall, ltot=ltot, mix=mix, x2=x2))
        ws.append(w)
        xs = x3
    dx, lossp = loss_head(xs, tgt.reshape(m, D_MODEL))
    gsmall = [None] * DEPTH
    token = jnp.zeros((), f32)
    for l in reversed(range(DEPTH)):
        sp, w, s = layer_params(small, l), ws[l], saved[l]
        dx2, dg2, h2, act, df, dyb = ffn_bwd(s["x2"], sp["g2"] + token, w["w_ff1"], w["w_ff2"], s["rlb"], dx)
        g_ff1 = tn_matmul(h2, df, f"dw_ff1_{l}", col_shards=N_CHIPS)
        g_ff2 = tn_matmul(act, dyb, f"dw_ff2_{l}")
        dodn, dosb, dosg, dx2b = outproj_bwd(dx2, w["w_out"])
        g_out = tn_matmul(s["mix"], dx2b, f"dw_out_{l}")
        token = token + put_g(l, "rest", dict(w_out=g_out, w_ff1=g_ff1, w_ff2=g_ff2))
        dproj, dconv, dalog, ddtb, ddng = dn_bwd(r3(s["qkv"]), r3(s["z"]), r3(s["ab"]), sp["conv"], sp["alog"], sp["dtb"],
                                                 sp["dng"] + token, s["sall"], s["tall"], r3(dodn))
        token = sync_g(ddng)
        dproj, dgq, dgk = sb_bwd(r3(s["sb"]), sp["gq"] + token, sp["gk"], s["ltot"], r3(dosb), dproj)
        dproj, dsgg, dsgw, dsgb = sg_bwd(r3(s["sg"]), sp["sgg"], sp["sgw"], sp["sgb"], r3(dosg), dproj)
        dproj = r2(dproj)
        g_in = tn_matmul(s["h1"], dproj, f"dw_in_{l}")
        token = put_g(l, "in", dict(w_in=g_in))
        dx, dg1 = inproj_bwd(s["x"], sp["g1"] + token, w["w_in"], dproj, dx2)
        token = sync_g(dg1)
        fold = lambda a: (a[:, 0, :].sum(0).reshape(2, SB_DIM)).sum(0)
        gsmall[l] = dict(norm1_g=dg1[0], conv_w=dconv[0:DN_CONV], a_log=dalog[0, 0:DN_HEADS], dt_bias=ddtb[0, 0:DN_HEADS],
                         dn_out_g=ddng[0], sb_q_g=fold(dgq), sb_k_g=fold(dgk), sg_v_g=dsgg[0], sg_w=dsgw,
                         sg_b=dsgb[:, 0:SG_GROUPS].T, norm2_g=dg2[0])
    return lossp, dx.reshape(bsz, t, D_MODEL), gsmall


def _chip_peers(x, y):
    return [(1 - x, y), (x, 1 - y), (1 - x, 1 - y)]


_HBM = pl.BlockSpec(memory_space=pltpu.HBM)
_SEM = pl.BlockSpec(memory_space=pltpu.SEMAPHORE)
_EFFECT = pltpu.SideEffectType.DATAFLOW_SIDE_EFFECTING


def _hbm(a):
    return pltpu.with_memory_space_constraint(a, pltpu.HBM)


def _my_half(ref):
    half = ref.shape[0] // 2
    return ref.at[pl.ds(pl.multiple_of(lax.axis_index("c") * half, 8), half)]


def _slot(zone, s, cols):
    if not cols:
        return zone.at[s]
    width = zone.shape[1] // N_CHIPS
    return zone.at[:, pl.ds(pl.multiple_of(s * width, 128), width)]


def _exchange_copy(src, land, k, j, send, recv, scatter, halve, waiting):
    x, y, c = lax.axis_index("x"), lax.axis_index("y"), lax.axis_index("c")
    px, py = _chip_peers(x, y)[j]
    me, peer = 2 * x + y, 2 * px + py
    if scatter:
        src = src.at[me if waiting else peer]
    dst = _slot(land, peer if waiting else me, halve == "cols")
    if halve:
        src, dst = _my_half(src), _my_half(dst)
    return pltpu.make_async_remote_copy(src_ref=src, dst_ref=dst, send_sem=send.at[3 * k + j],
                                        recv_sem=recv.at[3 * k + j], device_id=(px, py, c), device_id_type=MESH)


def exchange_start(items, name, scatter, after=None):
    arrs = []
    for a, _, _ in items:
        if not any(a is b for b in arrs):
            arrs.append(a)
    pos = [next(i for i, b in enumerate(arrs) if b is a) for a, _, _ in items]
    shapes = [a.shape if idx is None else a.shape[1:] for a, idx, _ in items]
    lands = [lax.empty(s if scatter else ((s[0], N_CHIPS * s[1]) if h == "cols" else (N_CHIPS,) + s), a.dtype)
             for (a, _, h), s in zip(items, shapes)]
    na, nl = len(arrs), len(lands)
    n_in = na + nl + (after is not None)

    def body(*refs):
        ins, lnd = refs[:na], refs[na:na + nl]
        send, recv = refs[n_in], refs[n_in + 1]
        token = refs[-1]
        for k, (_, idx, halve) in enumerate(items):
            src = ins[pos[k]] if idx is None else ins[pos[k]].at[idx]
            for j in range(3):
                _exchange_copy(src, lnd[k], k, j, send, recv, scatter, halve, False).start()
        token[...] = jnp.zeros_like(token)

    sems = pltpu.SemaphoreType.DMA((3 * nl,))
    extra = [] if after is None else [after]
    out = pl.pallas_call(
        body, name=name,
        out_shape=(sems, sems, *[pltpu.HBM(a.shape, a.dtype) for a in arrs + lands], SDS((8, 128), f32)),
        in_specs=[_HBM] * (na + nl) + [pl.BlockSpec(memory_space=pl.ANY)] * len(extra),
        out_specs=(_SEM, _SEM, *[_HBM] * (na + nl), pl.BlockSpec(memory_space=pltpu.VMEM)),
        input_output_aliases={i: 2 + i for i in range(na + nl)},
        compiler_params=pltpu.CompilerParams(has_side_effects=_EFFECT),
    )(*[_hbm(a) for a in arrs + lands], *extra)
    thru = out[2:2 + na]
    return dict(send=out[0], recv=out[1], src=[(thru[pos[k]], idx) for k, (_, idx, _) in enumerate(items)],
                halve=[h for _, _, h in items], land=list(out[2 + na:2 + na + nl]), token=out[-1], scatter=scatter)


def exchange_wait(st, ks, after, name):
    arrs = []
    for k in ks:
        if not any(st["src"][k][0] is b for b in arrs):
            arrs.append(st["src"][k][0])
    pos = [next(i for i, b in enumerate(arrs) if b is st["src"][k][0]) for k in ks]
    lands = [st["land"][k] for k in ks]
    na, nl = len(arrs), len(lands)

    def body(*refs):
        ins, lnd = refs[:na], refs[na:na + nl]
        send, recv = refs[na + nl], refs[na + nl + 1]
        for t, k in enumerate(ks):
            idx = st["src"][k][1]
            src = ins[pos[t]] if idx is None else ins[pos[t]].at[idx]
            for j in range(3):
                cp = _exchange_copy(src, lnd[t], k, j, send, recv, st["scatter"], st["halve"][k], True)
                cp.wait_send()
                cp.wait_recv()

    out = pl.pallas_call(
        body, name=name, out_shape=tuple(pltpu.HBM(a.shape, a.dtype) for a in arrs + lands),
        in_specs=[_HBM] * (na + nl) + [_SEM, _SEM, pl.BlockSpec(memory_space=pl.ANY)], out_specs=tuple([_HBM] * (na + nl)),
        input_output_aliases={i: i for i in range(na + nl)},
        compiler_params=pltpu.CompilerParams(has_side_effects=_EFFECT),
    )(*arrs, *lands, st["send"], st["recv"], after)
    for k, (a, idx) in enumerate(st["src"]):
        for p, b in enumerate(arrs):
            if a is b:
                st["src"][k] = (out[p], idx)
    return list(out[na:na + nl])


def _sibling_copy(src, land, i, send, recv, other_half):
    x, y, c = lax.axis_index("x"), lax.axis_index("y"), lax.axis_index("c")
    return pltpu.make_async_remote_copy(src_ref=src.at[:, 1 - c] if other_half else src, dst_ref=land, send_sem=send.at[i],
                                        recv_sem=recv.at[i], device_id=(x, y, 1 - c), device_id_type=MESH)


def sibling_start(arrs, name, other_half=False):
    n = len(arrs)
    lands = [lax.empty((a.shape[0],) + a.shape[2:] if other_half else a.shape, a.dtype) for a in arrs]

    def body(*refs):
        ins, lnd = refs[:n], refs[n:2 * n]
        send, recv = refs[2 * n], refs[2 * n + 1]
        token = refs[-1]
        for i in range(n):
            _sibling_copy(ins[i], lnd[i], i, send, recv, other_half).start()
        token[...] = jnp.zeros_like(token)

    sems = pltpu.SemaphoreType.DMA((n,))
    out = pl.pallas_call(
        body, name=name,
        out_shape=(sems, sems, *[pltpu.HBM(a.shape, a.dtype) for a in arrs + lands], SDS((8, 128), f32)),
        in_specs=[_HBM] * (2 * n), out_specs=(_SEM, _SEM, *[_HBM] * (2 * n), pl.BlockSpec(memory_space=pltpu.VMEM)),
        input_output_aliases={i: 2 + i for i in range(2 * n)},
        compiler_params=pltpu.CompilerParams(has_side_effects=_EFFECT),
    )(*[_hbm(a) for a in arrs + lands])
    return dict(send=out[0], recv=out[1], src=list(out[2:2 + n]), land=list(out[2 + n:2 + 2 * n]), token=out[-1],
                other_half=other_half)


def sibling_wait(st, after, name):
    n = len(st["src"])

    def body(*refs):
        ins, lnd = refs[:n], refs[n:2 * n]
        send, recv = refs[2 * n], refs[2 * n + 1]
        for i in range(n):
            cp = _sibling_copy(ins[i], lnd[i], i, send, recv, st["other_half"])
            cp.wait_send()
            cp.wait_recv()

    out = pl.pallas_call(
        body, name=name, out_shape=tuple(pltpu.HBM(a.shape, a.dtype) for a in st["src"] + st["land"]),
        in_specs=[_HBM] * (2 * n) + [_SEM, _SEM, pl.BlockSpec(memory_space=pl.ANY)], out_specs=tuple([_HBM] * (2 * n)),
        input_output_aliases={i: i for i in range(2 * n)},
        compiler_params=pltpu.CompilerParams(has_side_effects=_EFFECT),
    )(*st["src"], *st["land"], st["send"], st["recv"], after)
    return list(out[:n]), list(out[n:])


def swap_cores(arrs, name):
    n = len(arrs)

    def body(*refs):
        ins, outs = refs[:n], refs[n:2 * n]
        send, recv = refs[2 * n:]
        sib = (lax.axis_index("x"), lax.axis_index("y"), 1 - lax.axis_index("c"))
        cps = [pltpu.make_async_remote_copy(src_ref=ins[i], dst_ref=outs[i], send_sem=send.at[i], recv_sem=recv.at[i],
                                            device_id=sib, device_id_type=MESH) for i in range(n)]
        for cp in cps:
            cp.start()
        for cp in cps:
            cp.wait()

    any_spec = pl.BlockSpec(memory_space=pl.ANY)
    return pl.pallas_call(
        body, name=name, in_specs=[any_spec] * n, out_specs=[any_spec] * n, out_shape=[SDS(a.shape, a.dtype) for a in arrs],
        scratch_shapes=[pltpu.SemaphoreType.DMA((n,)), pltpu.SemaphoreType.DMA((n,))],
    )(*arrs)


def swap_halves(zones, name):
    n = len(zones)

    def body(*refs):
        outs = refs[n:2 * n]
        send, recv = refs[2 * n:]
        x, y, c = lax.axis_index("x"), lax.axis_index("y"), lax.axis_index("c")
        cps = []
        for i in range(n):
            for j, (px, py) in enumerate(_chip_peers(x, y)):
                part = _my_half(outs[i].at[2 * px + py])
                cps.append(pltpu.make_async_remote_copy(src_ref=part, dst_ref=part, send_sem=send.at[3 * i + j],
                                                        recv_sem=recv.at[3 * i + j], device_id=(x, y, 1 - c), device_id_type=MESH))
        for cp in cps:
            cp.start()
        for cp in cps:
            cp.wait_send()
            cp.wait_recv()

    any_spec = pl.BlockSpec(memory_space=pl.ANY)
    return pl.pallas_call(
        body, name=name, in_specs=[any_spec] * n, out_specs=[any_spec] * n, out_shape=[SDS(a.shape, a.dtype) for a in zones],
        input_output_aliases={i: i for i in range(n)},
        scratch_shapes=[pltpu.SemaphoreType.DMA((3 * n,)), pltpu.SemaphoreType.DMA((3 * n,))],
    )(*zones)


def swap_other_halves(arrs, name):
    n = len(arrs)

    def body(*refs):
        ins, outs = refs[:n], refs[n:2 * n]
        send, recv = refs[2 * n:]
        x, y, c = lax.axis_index("x"), lax.axis_index("y"), lax.axis_index("c")
        cps = [pltpu.make_async_remote_copy(src_ref=ins[i].at[:, 1 - c], dst_ref=outs[i], send_sem=send.at[i], recv_sem=recv.at[i],
                                            device_id=(x, y, 1 - c), device_id_type=MESH) for i in range(n)]
        for cp in cps:
            cp.start()
        for cp in cps:
            cp.wait()

    any_spec = pl.BlockSpec(memory_space=pl.ANY)
    return pl.pallas_call(
        body, name=name, in_specs=[any_spec] * n, out_specs=[any_spec] * n,
        out_shape=[SDS((a.shape[0],) + a.shape[2:], a.dtype) for a in arrs],
        scratch_shapes=[pltpu.SemaphoreType.DMA((n,)), pltpu.SemaphoreType.DMA((n,))],
    )(*arrs)


def _ids_spec(grid, in_specs, out_specs):
    return pltpu.PrefetchScalarGridSpec(num_scalar_prefetch=1, grid=grid, in_specs=in_specs, out_specs=out_specs)


def pair_sum(ids, a, b, name, tr=512):
    nd, _, rows, cols = a.shape
    tr = min(tr, rows)
    assert rows % tr == 0

    def body(ids_ref, a_ref, b_ref, o_ref):
        o_ref[...] = (a_ref[0].astype(f32) + b_ref[...].astype(f32)).astype(bf16)

    spec = pl.BlockSpec((1, tr, cols), lambda d, i, ids: (d, i, 0))
    return pl.pallas_call(
        body, name=name,
        grid_spec=_ids_spec((nd, rows // tr), [pl.BlockSpec((1, 1, tr, cols), lambda d, i, ids: (d, ids[1], i, 0)), spec], spec),
        out_shape=SDS((nd, rows, cols), bf16), compiler_params=_cp(("arbitrary", "arbitrary")))(ids, a, b)


def allreduce_small(v):
    def body(v_ref, o_ref, rbuf, send, recv):
        x, y, c = lax.axis_index("x"), lax.axis_index("y"), lax.axis_index("c")
        o_ref[...] = v_ref[...]
        for s, peer in enumerate([(x, y, 1 - c), (1 - x, y, c), (x, 1 - y, c)]):
            cp = pltpu.make_async_remote_copy(src_ref=o_ref, dst_ref=rbuf.at[s], send_sem=send.at[s], recv_sem=recv.at[s],
                                              device_id=peer, device_id_type=MESH)
            cp.start()
            cp.wait()
            o_ref[...] = o_ref[...] + rbuf[s]

    vm = pl.BlockSpec(memory_space=pltpu.VMEM)
    return pl.pallas_call(
        body, name="allreduce_small", in_specs=[vm], out_specs=vm, out_shape=SDS(v.shape, f32),
        scratch_shapes=[pltpu.VMEM((3,) + v.shape, f32), pltpu.SemaphoreType.DMA((3,)), pltpu.SemaphoreType.DMA((3,))],
        compiler_params=_cp(),
    )(v)


def sum_partials(ids, zone, mine, name, tr=256):
    _, rows, cols = zone.shape
    tr = min(tr, rows)
    assert rows % tr == 0

    def body(ids_ref, m_ref, z1_ref, z2_ref, z3_ref, o_ref):
        o_ref[...] = ((m_ref[0].astype(f32) + z1_ref[0].astype(f32)) + z2_ref[0].astype(f32)) + z3_ref[0].astype(f32)

    slot = lambda flip: pl.BlockSpec((1, tr, cols), lambda i, ids: (ids[0] ^ flip, i, 0))
    return pl.pallas_call(
        body, name=name,
        grid_spec=_ids_spec((rows // tr,), [slot(0), slot(1), slot(2), slot(3)], pl.BlockSpec((tr, cols), lambda i, ids: (i, 0))),
        out_shape=SDS((rows, cols), f32), compiler_params=_cp(("arbitrary",)),
    )(ids, mine, zone, zone, zone)


def adamw(w, m, v, gs, name, layer=0, prev=None, tr=256):
    hrows, cols = gs[0].shape
    rows = hrows * len(gs)
    tr = min(tr, hrows)
    assert hrows % tr == 0 and w.shape[0] % rows == 0
    off, nth = layer * (rows // tr), hrows // tr

    def body(w_ref, m_ref, v_ref, *rest):
        g_ref, d_ref, mo_ref, vo_ref = rest[-4:]
        if len(gs) == 1:
            g = rest[0][...]
        else:
            g = jnp.where(pl.program_id(0) // nth == lax.axis_index("c"), rest[0][...], rest[1][...])
        mn = ADAM_B1 * m_ref[...] + (1.0 - ADAM_B1) * g
        vn = ADAM_B2 * v_ref[...] + (1.0 - ADAM_B2) * jnp.square(g)
        m_hat = mn / (1.0 - ADAM_B1 ** ADAM_STEP)
        v_hat = vn / (1.0 - ADAM_B2 ** ADAM_STEP)
        g_ref[...] = g
        d_ref[...] = -ADAM_LR * (m_hat / (jnp.sqrt(v_hat) + ADAM_EPS) + ADAM_WD * w_ref[...])
        mo_ref[...] = mn
        vo_ref[...] = vn

    loc = pl.BlockSpec((tr, cols), lambda i: (i % nth, 0))
    glob = pl.BlockSpec((tr, cols), lambda i: (off + i, 0))
    extra = [] if prev is None else list(prev)
    return pl.pallas_call(
        body, name=name, grid=(rows // tr,),
        in_specs=[glob] * 3 + [loc] * len(gs) + [pl.BlockSpec(memory_space=pl.ANY)] * len(extra),
        out_specs=[glob] * 4, out_shape=[SDS(w.shape, f32)] * 4,
        input_output_aliases={3 + len(gs) + j: j for j in range(len(extra))},
        compiler_params=_cp(("arbitrary",)),
    )(w, m, v, *gs, *extra)


BIG = ("w_in", "w_out", "w_ff1", "w_ff2")
SMALL = ("norm1_g", "conv_w", "a_log", "dt_bias", "dn_out_g", "sb_q_g", "sb_k_g", "sg_v_g", "sg_w", "sg_b", "norm2_g")
WEIGHTS = ("norm1_g", "w_in", "conv_w", "a_log", "dt_bias", "dn_out_g", "sb_q_g", "sb_k_g", "sg_v_g", "sg_w", "sg_b",
           "w_out", "norm2_g", "w_ff1", "w_ff2")


PACK_ROWS = 256


def _rows_of(shape):
    n = 1
    for d in shape:
        n *= d
    return -(-n // 1024) * 8, n


def _pack(arrs):
    parts = []
    for a in arrs:
        r, n = _rows_of(a.shape)
        parts.append(jnp.pad(a.reshape(-1), (0, r * 128 - n)).reshape(r, 128))
    rows = sum(p.shape[0] for p in parts)
    parts.append(jnp.zeros((-rows % PACK_ROWS, 128), arrs[0].dtype))
    return jnp.concatenate(parts, axis=0)


def _unpack(packed, shapes):
    out, o = [], 0
    for s in shapes:
        r, n = _rows_of(s)
        out.append(packed[o:o + r].reshape(-1)[0:n].reshape(s))
        o += r
    return out


def kernel(x, norm1_g, w_in, conv_w, a_log, dt_bias, dn_out_g, sb_q_g, sb_k_g, sg_v_g, sg_w, sg_b, w_out, norm2_g, w_ff1, w_ff2, loss_target, m_norm1_g, m_w_in, m_conv_w, m_a_log, m_dt_bias, m_dn_out_g, m_sb_q_g, m_sb_k_g, m_sg_v_g, m_sg_w, m_sg_b, m_w_out, m_norm2_g, m_w_ff1, m_w_ff2, v_norm1_g, v_w_in, v_conv_w, v_a_log, v_dt_bias, v_dn_out_g, v_sb_q_g, v_sb_k_g, v_sg_v_g, v_sg_w, v_sg_b, v_w_out, v_norm2_g, v_w_ff1, v_w_ff2):
    w = dict(norm1_g=norm1_g, w_in=w_in, conv_w=conv_w, a_log=a_log, dt_bias=dt_bias, dn_out_g=dn_out_g, sb_q_g=sb_q_g,
             sb_k_g=sb_k_g, sg_v_g=sg_v_g, sg_w=sg_w, sg_b=sg_b, w_out=w_out, norm2_g=norm2_g, w_ff1=w_ff1, w_ff2=w_ff2)
    mom = dict(norm1_g=m_norm1_g, w_in=m_w_in, conv_w=m_conv_w, a_log=m_a_log, dt_bias=m_dt_bias, dn_out_g=m_dn_out_g,
               sb_q_g=m_sb_q_g, sb_k_g=m_sb_k_g, sg_v_g=m_sg_v_g, sg_w=m_sg_w, sg_b=m_sg_b, w_out=m_w_out, norm2_g=m_norm2_g,
               w_ff1=m_w_ff1, w_ff2=m_w_ff2)
    var = dict(norm1_g=v_norm1_g, w_in=v_w_in, conv_w=v_conv_w, a_log=v_a_log, dt_bias=v_dt_bias, dn_out_g=v_dn_out_g,
               sb_q_g=v_sb_q_g, sb_k_g=v_sb_k_g, sg_v_g=v_sg_v_g, sg_w=v_sg_w, sg_b=v_sg_b, w_out=v_w_out, norm2_g=v_norm2_g,
               w_ff1=v_w_ff1, w_ff2=v_w_ff2)
    chip = 2 * lax.axis_index("x") + lax.axis_index("y")

    wb = [{k: w[k][l].astype(bf16) for k in BIG} for l in range(DEPTH)]
    ags = {0: exchange_start([(conv_w, None, None)] + [(wb[0][k], None, "rows") for k in BIG], "allgather_start_0", scatter=False)}
    item = lambda l, k: (l, (l == 0) + BIG.index(k))

    def landed(items, after, name):
        ag, ks = ags[items[0][0]], [k for _, k in items]
        zones = exchange_wait(ag, ks, after, name)
        halved = [t for t, k in enumerate(ks) if ag["halve"][k]]
        for t, z in zip(halved, swap_halves([zones[t] for t in halved], name.replace("wait", "pass"))):
            zones[t] = z
        return [lax.dynamic_update_slice_in_dim(z, ag["src"][k][0][None], chip, axis=0) for z, k in zip(zones, ks)]

    def whole(k, z):
        if k == "w_in":
            return w_in_from_shards(z)
        if k == "w_ff1":
            return jnp.transpose(z, (1, 0, 2)).reshape(D_MODEL, D_FF)
        return z.reshape(-1, D_MODEL)

    g_conv, first_in = landed([(0, 0), item(0, "w_in")], x, "allgather_wait_in0")
    small = {k: w[k] for k in SMALL}
    small["conv_w"] = jnp.transpose(g_conv, (1, 2, 0, 3)).reshape(DEPTH, DN_CONV, 3 * DN_WIDTH)
    cache = {}

    def get_w(l, part, after):
        if part == "in":
            return whole("w_in", first_in if l == 0 else landed([item(l, "w_in")], after, f"allgather_wait_in{l}")[0])
        if part == "out":
            zs = landed([item(l, k) for k in ("w_out", "w_ff1", "w_ff2")], after, f"allgather_wait_rest{l}")
            if l + 1 < DEPTH:
                ags[l + 1] = exchange_start([(wb[l + 1][k], None, "rows") for k in BIG], f"allgather_start_{l + 1}",
                                            scatter=False, after=zs[0])
            cache[l] = (whole("w_ff1", zs[1]), whole("w_ff2", zs[2]))
            return whole("w_out", zs[0])
        return cache[l]

    rs, pending = {}, []
    ids = jnp.stack([chip, lax.axis_index("c")]).astype(jnp.int32)

    def put_g(l, tag, g):
        names = [k for k in BIG if k in g]
        by_dest = [w_in_grad_to_shards(g[k]) if k == "w_in" else g[k] for k in names]
        halves = [a.reshape(N_CHIPS, 2, -1, a.shape[-1]) for a in by_dest]
        st = sibling_start(halves, f"pair_swap_start_{tag}{l}", other_half=True)
        pending.append((l, tag, names, st))
        return st["token"][0, 0]

    def sync_g(after):
        token = jnp.zeros((), f32)
        while pending:
            l, tag, names, st = pending.pop(0)
            halves, got = sibling_wait(st, after, f"pair_swap_wait_{tag}{l}")
            pair = [pair_sum(ids, a, b, f"pair_sum_{k}_{l}") for k, a, b in zip(names, halves, got)]
            rs[l, tag] = dict(exchange_start([(a, None, None) for a in pair], f"scatter_start_{tag}{l}", scatter=True), names=names)
            token = token + rs[l, tag]["token"][0, 0]
        return token

    lossp, grad_x, gsmall = local_step(x, loss_target, small, get_w, put_g, sync_g)
    loss = lax.psum(jnp.sum(lossp), ("x", "y", "c"))

    def sum_group(l, tag, after):
        st = rs[l, tag]
        zones = exchange_wait(st, list(range(len(st["names"]))), after, f"scatter_wait_{tag}{l}")
        sums = [sum_partials(ids, zones[i], st["src"][i][0], f"sum_{k}_{l}") for i, k in enumerate(st["names"])]
        return sibling_start(sums, f"swap_sums_start_{tag}{l}")

    def update_group(l, tag, swap, after, prev):
        sums, others = sibling_wait(swap, after, f"swap_sums_wait_{tag}{l}")
        outs = dict(prev)
        for i, k in enumerate(rs[l, tag]["names"]):
            r2 = lambda a: a.reshape(-1, a.shape[-1])
            outs[k] = adamw(r2(w[k]), r2(mom[k]), r2(var[k]), (sums[i], others[i]), f"adamw_{k}_{l}", layer=l, prev=prev.get(k))
        return outs

    swap_r = sum_group(1, "rest", rs[0, "in"]["token"])
    swap_i = sum_group(1, "in", swap_r["token"])
    done = update_group(1, "rest", swap_r, swap_i["token"], {})
    done = update_group(1, "in", swap_i, done["w_ff2"][0], done)
    res = {}

    full_shapes = [(DEPTH,) + tuple(gsmall[0][k].shape) for k in SMALL]
    packed = _pack([jnp.stack([gsmall[l][k] for l in range(DEPTH)]) for k in SMALL])
    total = allreduce_small(packed)
    gfull = dict(zip(SMALL, _unpack(total, full_shapes)))
    cs = 3 * DN_WIDTH // N_CHIPS
    gfull["conv_w"] = lax.dynamic_slice_in_dim(gfull["conv_w"], chip * cs, cs, axis=2)
    gp, wp, mp, vp = (_pack([d[k] for k in SMALL]) for d in (gfull, w, mom, var))
    outs = adamw(wp, mp, vp, (gp,), "adamw_small")
    loc_shapes = [w[k].shape for k in SMALL]
    unp = [_unpack(o, loc_shapes) for o in outs]
    for i, k in enumerate(SMALL):
        res[k] = [unp[j][i] for j in range(4)]

    swap_r = sum_group(0, "rest", outs[0])
    swap_i = sum_group(0, "in", swap_r["token"])
    done = update_group(0, "rest", swap_r, swap_i["token"], done)
    done = update_group(0, "in", swap_i, done["w_ff2"][0], done)
    for k in BIG:
        res[k] = [o.reshape(w[k].shape) for o in done[k]]

    return (loss, grad_x, *[res[k][0] for k in WEIGHTS], *[res[k][1] for k in WEIGHTS], *[res[k][2] for k in WEIGHTS],
            *[res[k][3] for k in WEIGHTS])
```

```python
import functools

import jax
import jax.numpy as jnp
from jax import lax
from jax.experimental import pallas as pl
from jax.experimental.pallas import tpu as pltpu

f32 = jnp.float32
bf16 = jnp.bfloat16
SDS = jax.ShapeDtypeStruct
MESH = pl.DeviceIdType.MESH

NORM_EPS = 1e-6
D_MODEL = 1024
DEPTH = 2
DN_HEADS, DN_DIM, DN_WIDTH, DN_CONV, DN_CHUNK = 4, 128, 512, 4, 64
SB_HEADS, SB_DIM, SB_WIDTH = 4, 64, 256
SG_GROUPS, SG_DIM, SG_WIDTH, SG_CHUNK = 4, 64, 256, 128
D_FF = 4096
IN_DIM = 3336
C_QKV, C_Z, C_AB, C_SB, C_SG, IN_PAD = 0, 1536, 2048, 2304, 3072, 3584
DN_COLS = C_SB
N_CHIPS = 4

ADAM_LR, ADAM_B1, ADAM_B2, ADAM_EPS, ADAM_WD, ADAM_STEP = 0.001, 0.9, 0.999, 1e-08, 0.01, 10

VMEM_LIMIT = 56 * 1024 * 1024


def _cp(sem=None, **kw):
    if sem is not None:
        kw["dimension_semantics"] = sem
    return pltpu.CompilerParams(vmem_limit_bytes=VMEM_LIMIT, **kw)


def _split2(x):
    hi = x.astype(bf16)
    lo = (x - hi.astype(f32)).astype(bf16)
    return hi, lo


NT = (((1,), (1,)), ((), ()))
TN = (((0,), (0,)), ((), ()))
_DIMS2 = dict(nn=(((1,), (0,)), ((), ())), nt=NT, tn=TN)
_DIMS3 = dict(nn=(((2,), (1,)), ((0,), (0,))), nt=(((2,), (2,)), ((0,), (0,))), tn=(((1,), (1,)), ((0,), (0,))))


def _dg(a, b, kind):
    return lax.dot_general(a, b, (_DIMS2 if a.ndim == 2 else _DIMS3)[kind], preferred_element_type=f32)


def _pdot(a, b):
    return _dg(a, b, "nn")


def _dot_hp(a, b):
    ah, al = _split2(a)
    bh, bl = _split2(b)
    return _pdot(ah, bh) + _pdot(ah, bl) + _pdot(al, bh)


def _dot_x2c(a, m):
    lead = a.shape[:-1]
    ah, al = _split2(a.reshape(-1, a.shape[-1]))
    return (_pdot(ah, m) + _pdot(al, m)).reshape(lead + (m.shape[1],))


def _dot_cx2(m, a):
    if a.ndim == 3:
        m = jnp.broadcast_to(m, (a.shape[0],) + m.shape)
    ah, al = _split2(a)
    return _pdot(m, ah) + _pdot(m, al)


def _nt(a, b):
    return _dg(a.astype(bf16), b.astype(bf16), "nt")


def _tn(a, b):
    return _dg(a.astype(bf16), b.astype(bf16), "tn")


def _nn(a, b):
    return _dg(a.astype(bf16), b.astype(bf16), "nn")


@jax.custom_vjp
def mm(a, b):
    return _nn(a, b)


mm.defvjp(lambda a, b: (_nn(a, b), (a, b)), lambda r, g: (_nt(g, r[1]), _tn(r[0], g)))


@jax.custom_vjp
def mm_nt(a, b):
    return _nt(a, b)


mm_nt.defvjp(lambda a, b: (_nt(a, b), (a, b)), lambda r, g: (_nn(g, r[1]), _tn(g, r[0])))


@jax.custom_vjp
def mm_tn(a, b):
    return _tn(a, b)


mm_tn.defvjp(lambda a, b: (_tn(a, b), (a, b)), lambda r, g: (_nt(r[1], g), _nn(r[0], g)))


@jax.custom_vjp
def rmul_const(a, m, mt):
    return _dot_x2c(a, m)


rmul_const.defvjp(lambda a, m, mt: (_dot_x2c(a, m), (m, mt)),
                  lambda r, g: (_dot_x2c(g, r[1]), jnp.zeros_like(r[0]), jnp.zeros_like(r[1])))


@jax.custom_vjp
def lmul_const(m, mt, a):
    return _dot_cx2(m, a)


lmul_const.defvjp(lambda m, mt, a: (_dot_cx2(m, a), (m, mt)),
                  lambda r, g: (jnp.zeros_like(r[0]), jnp.zeros_like(r[1]), _dot_cx2(r[1], g)))


@jax.custom_vjp
def mm_hl(t, x):
    th, tl = _split2(t)
    xb = x.astype(bf16)
    return _pdot(th, xb) + _pdot(tl, xb)


def _mm_hl_bwd(r, g):
    t, x = r
    th, tl = _split2(t)
    gb = g.astype(bf16)
    return _nt(g, x), _dg(th, gb, "tn") + _dg(tl, gb, "tn")


mm_hl.defvjp(lambda t, x: (mm_hl(t, x), (t, x)), _mm_hl_bwd)


def inv_unit_lower(lm):
    c = lm.shape[-1]
    r, cc = _iota2((c, c))
    eye = (r == cc).astype(f32)
    t = eye - lm
    p = -lm
    k = 1
    while 2 * k < c:
        p = _nn(p, p)
        t = t + _nn(t, p)
        k *= 2
    res = eye - t - _dot_hp(lm, t)
    return t + _nn(t, res)


@jax.custom_vjp
def inv_given(lm, t):
    return t


inv_given.defvjp(lambda lm, t: (t, t), lambda t, g: (-_nt(_tn(t, g), t), jnp.zeros_like(t)))


def _sigmoid(x):
    return 1.0 / (1.0 + jnp.exp(-x))


def _softplus(x):
    return jnp.maximum(x, 0.0) + jnp.log(1.0 + jnp.exp(-jnp.abs(x)))


def _silu(x):
    return x * _sigmoid(x)


def _gelu(x):
    return 0.5 * x * (1.0 + jnp.tanh(0.7978845608028654 * (x + 0.044715 * (x * x * x))))


def _iota2(shape):
    return lax.broadcasted_iota(jnp.int32, shape, 0), lax.broadcasted_iota(jnp.int32, shape, 1)


def _group_avg_mats():
    r, c = _iota2((128, 128))
    return jnp.where((r // 64) == (c // 64), 1.0 / 64.0, 0.0).astype(bf16)


def _pair_norm(x, gain, bavg):
    ms = rmul_const(x * x, bavg, bavg)
    return x * lax.rsqrt(ms + NORM_EPS) * gain


def _rms(x):
    r = lax.rsqrt(jnp.mean(x * x, axis=-1, keepdims=True) + NORM_EPS)
    return r


_IN_GROUPS = ((C_QKV, C_Z), (C_Z, C_AB), (C_AB, C_AB + 128), (C_SB, C_SG), (C_SG, IN_PAD))


def inproj_fwd(x, g, wp, tm=256):
    m = x.shape[0]

    def body(x_ref, g_ref, w_ref, *outs):
        xv = x_ref[...]
        h = (xv * _rms(xv) * g_ref[...]).astype(bf16)
        outs[-1][...] = h
        for (a, b), o in zip(_IN_GROUPS, outs):
            o[...] = _pdot(h, w_ref[:, a:b])

    widths = [b - a for a, b in _IN_GROUPS]
    return pl.pallas_call(
        body, name="inproj_fwd", grid=(m // tm,),
        in_specs=[pl.BlockSpec((tm, D_MODEL), lambda i: (i, 0)), pl.BlockSpec((1, D_MODEL), lambda i: (0, 0)),
                  pl.BlockSpec((D_MODEL, IN_PAD), lambda i: (0, 0))],
        out_specs=[pl.BlockSpec((tm, wd), lambda i: (i, 0)) for wd in widths + [D_MODEL]],
        out_shape=[SDS((m, wd), f32) for wd in widths] + [SDS((m, D_MODEL), bf16)],
        compiler_params=_cp(("arbitrary",)),
    )(x, g, wp)


def inproj_bwd(x, g, wp, dproj, dres, tm=256):
    m = x.shape[0]

    def body(x_ref, g_ref, w_ref, dp_ref, dr_ref, dx_ref, dg_ref):
        xv = x_ref[...]
        r = _rms(xv)
        xn = xv * r
        gv = g_ref[...]
        dh = lax.dot_general(dp_ref[...], w_ref[...], NT, preferred_element_type=f32)
        dxn = dh * gv
        dx_ref[...] = dr_ref[...] + r * (dxn - xn * jnp.mean(dxn * xn, axis=-1, keepdims=True))

        @pl.when(pl.program_id(0) == 0)
        def _():
            dg_ref[...] = jnp.zeros_like(dg_ref)

        dg_ref[...] += jnp.sum(dh * xn, axis=0, keepdims=True)

    return pl.pallas_call(
        body, name="inproj_bwd", grid=(m // tm,),
        in_specs=[pl.BlockSpec((tm, D_MODEL), lambda i: (i, 0)), pl.BlockSpec((1, D_MODEL), lambda i: (0, 0)),
                  pl.BlockSpec((D_MODEL, IN_PAD), lambda i: (0, 0)), pl.BlockSpec((tm, IN_PAD), lambda i: (i, 0)),
                  pl.BlockSpec((tm, D_MODEL), lambda i: (i, 0))],
        out_specs=[pl.BlockSpec((tm, D_MODEL), lambda i: (i, 0)), pl.BlockSpec((1, D_MODEL), lambda i: (0, 0))],
        out_shape=[SDS((m, D_MODEL), f32), SDS((1, D_MODEL), f32)],
        compiler_params=_cp(("arbitrary",)),
    )(x, g, wp, dproj, dres)


def outproj_fwd(x, odn, osb, osg, wo, tm=512):
    m = x.shape[0]

    def body(x_ref, a_ref, b_ref, c_ref, w_ref, x2_ref, mix_ref):
        mix_ref[:, 0:DN_WIDTH] = a_ref[...].astype(bf16)
        mix_ref[:, DN_WIDTH:DN_WIDTH + SB_WIDTH] = b_ref[...].astype(bf16)
        mix_ref[:, DN_WIDTH + SB_WIDTH:D_MODEL] = c_ref[...].astype(bf16)
        x2_ref[...] = x_ref[...] + _pdot(mix_ref[...], w_ref[...])

    row = lambda w: pl.BlockSpec((tm, w), lambda i: (i, 0))
    return pl.pallas_call(
        body, name="outproj_fwd", grid=(m // tm,),
        in_specs=[row(D_MODEL), row(DN_WIDTH), row(SB_WIDTH), row(SG_WIDTH), pl.BlockSpec((D_MODEL, D_MODEL), lambda i: (0, 0))],
        out_specs=[row(D_MODEL), row(D_MODEL)],
        out_shape=[SDS((m, D_MODEL), f32), SDS((m, D_MODEL), bf16)],
        compiler_params=_cp(("arbitrary",)),
    )(x, odn, osb, osg, wo)


def outproj_bwd(dx2, wo, tm=512):
    m = dx2.shape[0]

    def body(d_ref, w_ref, a_ref, b_ref, c_ref, db_ref):
        db = d_ref[...].astype(bf16)
        db_ref[...] = db
        dm = lax.dot_general(db, w_ref[...], NT, preferred_element_type=f32)
        a_ref[...] = dm[:, 0:DN_WIDTH]
        b_ref[...] = dm[:, DN_WIDTH:DN_WIDTH + SB_WIDTH]
        c_ref[...] = dm[:, DN_WIDTH + SB_WIDTH:D_MODEL]

    row = lambda w: pl.BlockSpec((tm, w), lambda i: (i, 0))
    return pl.pallas_call(
        body, name="outproj_bwd", grid=(m // tm,),
        in_specs=[row(D_MODEL), pl.BlockSpec((D_MODEL, D_MODEL), lambda i: (0, 0))],
        out_specs=[row(DN_WIDTH), row(SB_WIDTH), row(SG_WIDTH), row(D_MODEL)],
        out_shape=[SDS((m, DN_WIDTH), f32), SDS((m, SB_WIDTH), f32), SDS((m, SG_WIDTH), f32), SDS((m, D_MODEL), bf16)],
        compiler_params=_cp(("arbitrary",)),
    )(dx2, wo)


FF_CHUNK = D_FF // N_CHIPS


def _load_weights_once(pairs, sem):
    @pl.when(pl.program_id(0) == 0)
    def _():
        cps = [pltpu.make_async_copy(h, v, sem.at[i]) for i, (h, v) in enumerate(pairs)]
        for c in cps:
            c.start()
        for c in cps:
            c.wait()


def ffn_fwd(x2, g, w1, w2, tm=256):
    m = x2.shape[0]

    def body(x_ref, g_ref, w1_hbm, w2_hbm, y_ref, rl_ref, w1_v, w2_v, sem):
        _load_weights_once(((w1_hbm, w1_v), (w2_hbm, w2_v)), sem)
        xv = x_ref[...]
        h = (xv * _rms(xv) * g_ref[...]).astype(bf16)
        acc = xv
        for j in range(0, D_FF, FF_CHUNK):
            f = _pdot(h, w1_v[j // FF_CHUNK])
            rl = jnp.maximum(f, 0.0)
            rl_ref[:, j:j + FF_CHUNK] = rl.astype(bf16)
            acc = acc + _pdot((rl * rl).astype(bf16), w2_v[j:j + FF_CHUNK, :])
        y_ref[...] = acc

    return pl.pallas_call(
        body, name="ffn_fwd", grid=(m // tm,),
        in_specs=[pl.BlockSpec((tm, D_MODEL), lambda i: (i, 0)), pl.BlockSpec((1, D_MODEL), lambda i: (0, 0)),
                  pl.BlockSpec(memory_space=pl.ANY), pl.BlockSpec(memory_space=pl.ANY)],
        out_specs=[pl.BlockSpec((tm, D_MODEL), lambda i: (i, 0)), pl.BlockSpec((tm, D_FF), lambda i: (i, 0))],
        out_shape=[SDS((m, D_MODEL), f32), SDS((m, D_FF), bf16)],
        scratch_shapes=[pltpu.VMEM((N_CHIPS, D_MODEL, FF_CHUNK), bf16), pltpu.VMEM((D_FF, D_MODEL), bf16), pltpu.SemaphoreType.DMA((2,))],
        compiler_params=_cp(("arbitrary",)),
    )(x2, g, w1, w2)


def ffn_bwd(x2, g, w1, w2, rlb, dy, tm=256):
    m = x2.shape[0]

    def body(x_ref, g_ref, w1_hbm, w2_hbm, rl_ref, dy_ref, dx_ref, dg_ref, h_ref, a_ref, df_ref, dyb_ref, w1_v, w2_v, sem):
        _load_weights_once(((w1_hbm, w1_v), (w2_hbm, w2_v)), sem)
        xv = x_ref[...]
        r = _rms(xv)
        xn = xv * r
        gv = g_ref[...]
        h = (xn * gv).astype(bf16)
        h_ref[...] = h
        dyv = dy_ref[...]
        dyb = dyv.astype(bf16)
        dyb_ref[...] = dyb
        dh = jnp.zeros((tm, D_MODEL), f32)
        for j in range(0, D_FF, FF_CHUNK):
            rl = rl_ref[:, j:j + FF_CHUNK].astype(f32)
            a_ref[:, j:j + FF_CHUNK] = (rl * rl).astype(bf16)
            da = lax.dot_general(dyb, w2_v[j:j + FF_CHUNK, :], NT, preferred_element_type=f32)
            df = (da * (2.0 * rl)).astype(bf16)
            df_ref[:, j:j + FF_CHUNK] = df
            dh = dh + lax.dot_general(df, w1_v[j // FF_CHUNK], NT, preferred_element_type=f32)
        dxn = dh * gv
        dx_ref[...] = dyv + r * (dxn - xn * jnp.mean(dxn * xn, axis=-1, keepdims=True))

        @pl.when(pl.program_id(0) == 0)
        def _():
            dg_ref[...] = jnp.zeros_like(dg_ref)

        dg_ref[...] += jnp.sum(dh * xn, axis=0, keepdims=True)

    row = lambda w: pl.BlockSpec((tm, w), lambda i: (i, 0))
    return pl.pallas_call(
        body, name="ffn_bwd", grid=(m // tm,),
        in_specs=[row(D_MODEL), pl.BlockSpec((1, D_MODEL), lambda i: (0, 0)),
                  pl.BlockSpec(memory_space=pl.ANY), pl.BlockSpec(memory_space=pl.ANY), row(D_FF), row(D_MODEL)],
        out_specs=[row(D_MODEL), pl.BlockSpec((1, D_MODEL), lambda i: (0, 0)), row(D_MODEL), row(D_FF), row(D_FF), row(D_MODEL)],
        out_shape=[SDS((m, D_MODEL), f32), SDS((1, D_MODEL), f32), SDS((m, D_MODEL), bf16), SDS((m, D_FF), bf16),
                   SDS((m, D_FF), bf16), SDS((m, D_MODEL), bf16)],
        scratch_shapes=[pltpu.VMEM((N_CHIPS, D_MODEL, FF_CHUNK), bf16), pltpu.VMEM((D_FF, D_MODEL), bf16), pltpu.SemaphoreType.DMA((2,))],
        compiler_params=_cp(("arbitrary",)),
    )(x2, g, w1, w2, rlb, dy)


def _tile(n, cap):
    best = 128
    for t in range(128, cap + 1, 128):
        if n % t == 0:
            best = t
    return best


def tn_matmul(a, b, name, col_shards=1, tk=2048):
    m, ka = a.shape
    n = b.shape[1]
    ti = _tile(ka, 1024)
    tj = _tile(n // col_shards, 1152)
    tk = min(tk, m)
    nk = m // tk
    jps = (n // col_shards) // tj

    def body(a_ref, b_ref, o_ref, acc):
        k = pl.program_id(2)

        @pl.when(k == 0)
        def _():
            acc[...] = jnp.zeros_like(acc)

        acc[...] += lax.dot_general(a_ref[...], b_ref[...], TN, preferred_element_type=f32)

        @pl.when(k == nk - 1)
        def _():
            o_ref[...] = acc[...].astype(bf16).reshape(o_ref.shape)

    if col_shards == 1:
        out_shape, out_spec = SDS((ka, n), bf16), pl.BlockSpec((ti, tj), lambda i, j, k: (i, j))
    else:
        out_shape = SDS((col_shards, ka, n // col_shards), bf16)
        out_spec = pl.BlockSpec((1, ti, tj), lambda i, j, k: (j // jps, i, j % jps))
    return pl.pallas_call(
        body, name=name, grid=(ka // ti, n // tj, nk),
        in_specs=[pl.BlockSpec((tk, ti), lambda i, j, k: (k, i)), pl.BlockSpec((tk, tj), lambda i, j, k: (k, j))],
        out_specs=out_spec, out_shape=out_shape,
        scratch_shapes=[pltpu.VMEM((ti, tj), f32)],
        compiler_params=_cp(("arbitrary", "arbitrary", "arbitrary")),
    )(a, b)


def loss_head(y, tgt, tm=512):
    m = y.shape[0]

    def body(y_ref, t_ref, dy_ref, l_ref):
        e = y_ref[...] - t_ref[...]
        dy_ref[...] = e * (1.0 / D_MODEL)

        @pl.when(pl.program_id(0) == 0)
        def _():
            l_ref[...] = jnp.zeros_like(l_ref)

        l_ref[...] += jnp.sum(e * e, axis=0, keepdims=True) * (0.5 / D_MODEL)

    row = pl.BlockSpec((tm, D_MODEL), lambda i: (i, 0))
    return pl.pallas_call(
        body, name="loss_head", grid=(m // tm,), in_specs=[row, row],
        out_specs=[row, pl.BlockSpec((1, D_MODEL), lambda i: (0, 0))],
        out_shape=[SDS((m, D_MODEL), f32), SDS((1, D_MODEL), f32)],
        compiler_params=_cp(("arbitrary",)),
    )(y, tgt)


def _dn_consts():
    c = DN_CHUNK
    r, cc = _iota2((c, c))
    lt = (cc <= r).astype(bf16)
    ltt = (r <= cc).astype(bf16)
    return lt, ltt


def dn_chunk(cq, ck, cv, g, beta, z, s, gain, lt, ltt, t_given=None):
    c = DN_CHUNK
    r, cc = _iota2((c, c))
    q = cq * lax.rsqrt(jnp.sum(cq * cq, axis=-1, keepdims=True) + NORM_EPS) * (DN_DIM ** -0.5)
    k = ck * lax.rsqrt(jnp.sum(ck * ck, axis=-1, keepdims=True) + NORM_EPS)
    r2, c2 = _iota2((c, 128))
    uaug = jnp.where((c2 < c) & (r2 > c2), 1.0, 0.0) + jnp.where(c2 == c, 1.0, 0.0)
    gam_all = lmul_const(lt, ltt, g * uaug)
    gam_cc = gam_all[:, :, 0:c]
    gam = gam_all[:, :, c:c + 1]
    dec = jnp.where(cc <= r, jnp.exp(jnp.where(cc <= r, gam_cc, 0.0)), 0.0)
    kk = mm_nt(k, k)
    lm = jnp.where(cc < r, beta * kk * dec, 0.0)
    t = inv_unit_lower(lm) if t_given is None else inv_given(lm, t_given)
    eg = jnp.exp(gam)
    sol = mm_hl(t, jnp.concatenate([cv * beta, k * (beta * eg)], axis=2))
    u, w = sol[:, :, 0:DN_DIM], sol[:, :, DN_DIM:2 * DN_DIM]
    qk = jnp.where(cc <= r, mm_nt(q, k) * dec, 0.0)
    glast = jnp.sum(g, axis=1, keepdims=True)
    qd = q * eg
    kd = k * jnp.exp(glast - gam)
    un = u - mm(w, s)
    o = mm(qd, s) + mm(qk, un)
    s_new = s * jnp.exp(glast) + mm_tn(kd, un)
    on = o * lax.rsqrt(jnp.mean(o * o, axis=-1, keepdims=True) + NORM_EPS) * gain * _silu(z)
    return on, s_new, t


def _dn_gates(ab, al_row, dt_row):
    pre = ab + dt_row
    return -jnp.exp(al_row) * _softplus(pre), _sigmoid(ab), _sigmoid(pre)


def _dn_chains(cacts, gates, z_ref):
    cq, ck, cv, g, beta, z = [], [], [], [], [], []
    for bi, cact in enumerate(cacts):
        for h in range(DN_HEADS):
            cq.append(cact[:, h * DN_DIM:(h + 1) * DN_DIM])
            ck.append(cact[:, DN_WIDTH + h * DN_DIM:DN_WIDTH + (h + 1) * DN_DIM])
            cv.append(cact[:, 2 * DN_WIDTH + h * DN_DIM:2 * DN_WIDTH + (h + 1) * DN_DIM])
            g.append(gates[bi][0][:, h:h + 1])
            beta.append(gates[bi][1][:, DN_HEADS + h:DN_HEADS + h + 1])
            z.append(z_ref[bi, :, h * DN_DIM:(h + 1) * DN_DIM])
    return tuple(jnp.stack(v) for v in (cq, ck, cv, g, beta, z))


def _conv_rows(xe_ref, b, w_ref):
    y = w_ref[0:1, :] * xe_ref[b, pl.ds(5, DN_CHUNK), :]
    for i in range(1, DN_CONV):
        y = y + w_ref[i:i + 1, :] * xe_ref[b, pl.ds(5 + i, DN_CHUNK), :]
    return y


def dn_fwd(qkv, z, ab, conv_w, alog, dtb, gain):
    bsz, t, _ = qkv.shape
    nc = t // DN_CHUNK
    c = DN_CHUNK
    nh = bsz * DN_HEADS

    def body(qkv_ref, z_ref, ab_ref, w_ref, al_ref, dt_ref, g_ref, o_ref, sall_ref, tall_ref, xe, s_sc):
        n = pl.program_id(0)

        @pl.when(n == 0)
        def _():
            xe[:, 0:8, :] = jnp.zeros((bsz, 8, 3 * DN_WIDTH), f32)
            s_sc[...] = jnp.zeros_like(s_sc)

        lt, ltt = _dn_consts()
        cacts = []
        for b in range(bsz):
            xe[b, 8:8 + c, :] = qkv_ref[b]
            cacts.append(_silu(_conv_rows(xe, b, w_ref)))
            xe[b, 0:8, :] = xe[b, c:c + 8, :]
        gates = [_dn_gates(ab_ref[b], al_ref[...], dt_ref[...]) for b in range(bsz)]
        s = s_sc[...]
        sall_ref[0] = s
        on, sn, tt = dn_chunk(*_dn_chains(cacts, gates, z_ref), s, g_ref[...], lt, ltt)
        tall_ref[0] = tt
        s_sc[...] = sn
        for b in range(bsz):
            for h in range(DN_HEADS):
                o_ref[b, :, h * DN_DIM:(h + 1) * DN_DIM] = on[b * DN_HEADS + h]

    blk = lambda w: pl.BlockSpec((bsz, c, w), lambda n: (0, n, 0))
    full = lambda shp: pl.BlockSpec(shp, lambda n: (0,) * len(shp))
    return pl.pallas_call(
        body, name="dn_fwd", grid=(nc,),
        in_specs=[blk(3 * DN_WIDTH), blk(DN_WIDTH), blk(128), full((8, 3 * DN_WIDTH)), full((1, 128)), full((1, 128)), full((1, 128))],
        out_specs=[blk(DN_WIDTH), pl.BlockSpec((1, nh, DN_DIM, DN_DIM), lambda n: (n, 0, 0, 0)),
                   pl.BlockSpec((1, nh, c, c), lambda n: (n, 0, 0, 0))],
        out_shape=[SDS((bsz, t, DN_WIDTH), f32), SDS((nc, nh, DN_DIM, DN_DIM), f32), SDS((nc, nh, c, c), f32)],
        scratch_shapes=[pltpu.VMEM((bsz, c + 8, 3 * DN_WIDTH), f32), pltpu.VMEM((nh, DN_DIM, DN_DIM), f32)],
        compiler_params=_cp(("arbitrary",)),
    )(qkv, z, ab, conv_w, alog, dtb, gain)


def dn_bwd(qkv, z, ab, conv_w, alog, dtb, gain, sall, tall, do):
    bsz, t, _ = qkv.shape
    nc = t // DN_CHUNK
    c = DN_CHUNK
    nh = bsz * DN_HEADS
    w3 = 3 * DN_WIDTH

    def body(qkv_ref, prev_ref, z_ref, ab_ref, w_ref, al_ref, dt_ref, g_ref, sall_ref, tall_ref, do_ref,
             dp_ref, dw_ref, dal_ref, ddt_ref, dg_ref, xe, dye, dc_sc, ds_sc):
        n = pl.program_id(0)
        first = (nc - 1 - n) == 0

        @pl.when(n == 0)
        def _():
            dye[:, c:c + 8, :] = jnp.zeros((bsz, 8, w3), f32)
            ds_sc[...] = jnp.zeros_like(ds_sc)
            dw_ref[...] = jnp.zeros_like(dw_ref)
            dal_ref[...] = jnp.zeros_like(dal_ref)
            ddt_ref[...] = jnp.zeros_like(ddt_ref)
            dg_ref[...] = jnp.zeros_like(dg_ref)

        lt, ltt = _dn_consts()
        lane_c = lax.broadcasted_iota(jnp.int32, (c, 128), 1)
        ys, sigs = [], []
        for b in range(bsz):
            xe[b, 0:8, :] = jnp.where(first, 0.0, prev_ref[b])
            xe[b, 8:8 + c, :] = qkv_ref[b]
            ys.append(_conv_rows(xe, b, w_ref))
            sigs.append(_sigmoid(ys[b]))
        gates = [_dn_gates(ab_ref[b], al_ref[...], dt_ref[...]) for b in range(bsz)]
        ops = _dn_chains([y * sg for y, sg in zip(ys, sigs)], gates, z_ref)
        tt = tall_ref[0]
        _, vjp = jax.vjp(lambda *p: dn_chunk(*p, lt, ltt, t_given=tt)[0:2], *ops, sall_ref[0], g_ref[...])
        don = jnp.stack([do_ref[b, :, h * DN_DIM:(h + 1) * DN_DIM] for b in range(bsz) for h in range(DN_HEADS)])
        dcq, dck, dcv, dg, dbeta, dzz, dsp, dgn = vjp((don, ds_sc[...]))
        ds_sc[...] = dsp
        dg_ref[...] += dgn
        for b in range(bsz):
            dgate = jnp.zeros((c, 128), f32)
            for h in range(DN_HEADS):
                i = b * DN_HEADS + h
                dc_sc[b, :, h * DN_DIM:(h + 1) * DN_DIM] = dcq[i]
                dc_sc[b, :, DN_WIDTH + h * DN_DIM:DN_WIDTH + (h + 1) * DN_DIM] = dck[i]
                dc_sc[b, :, 2 * DN_WIDTH + h * DN_DIM:2 * DN_WIDTH + (h + 1) * DN_DIM] = dcv[i]
                dp_ref[b, :, C_Z + h * DN_DIM:C_Z + (h + 1) * DN_DIM] = dzz[i].astype(bf16)
                dgate = dgate + jnp.where(lane_c == h, dg[i], 0.0) + jnp.where(lane_c == DN_HEADS + h, dbeta[i], 0.0)
            gg, beta, sig_pre = gates[b]
            is_g = lane_c < DN_HEADS
            dpre = jnp.where(is_g, dgate * (-jnp.exp(al_ref[...])) * sig_pre, 0.0)
            dp_ref[b, :, C_AB:C_AB + 128] = (dpre + jnp.where(is_g, 0.0, dgate * beta * (1.0 - beta))).astype(bf16)
            dp_ref[b, :, C_AB + 128:DN_COLS] = jnp.zeros((c, DN_COLS - C_AB - 128), bf16)
            dal_ref[...] += jnp.sum(jnp.where(is_g, dgate * gg, 0.0), axis=0, keepdims=True)
            ddt_ref[...] += jnp.sum(dpre, axis=0, keepdims=True)
            y, sig = ys[b], sigs[b]
            dy = dc_sc[b] * (sig * (1.0 + y * (1.0 - sig)))
            dye[b, 0:c, :] = dy
            dx = w_ref[3:4, :] * dy
            for i in range(DN_CONV - 1):
                dx = dx + w_ref[i:i + 1, :] * dye[b, pl.ds(3 - i, c), :]
            dp_ref[b, :, 0:w3] = dx.astype(bf16)
            for i in range(DN_CONV):
                dw_ref[i:i + 1, :] += jnp.sum(dy * xe[b, pl.ds(5 + i, c), :], axis=0, keepdims=True)
            dye[b, c:c + 8, :] = dye[b, 0:8, :]

    rev = lambda w: pl.BlockSpec((bsz, c, w), lambda n: (0, nc - 1 - n, 0))
    full = lambda shp: pl.BlockSpec(shp, lambda n: (0,) * len(shp))
    prev = pl.BlockSpec((bsz, 8, w3), lambda n: (0, jnp.maximum((nc - 1 - n) * (c // 8) - 1, 0), 0))
    return pl.pallas_call(
        body, name="dn_bwd", grid=(nc,),
        in_specs=[rev(w3), prev, rev(DN_WIDTH), rev(128), full((8, w3)), full((1, 128)), full((1, 128)), full((1, 128)),
                  pl.BlockSpec((1, nh, DN_DIM, DN_DIM), lambda n: (nc - 1 - n, 0, 0, 0)),
                  pl.BlockSpec((1, nh, c, c), lambda n: (nc - 1 - n, 0, 0, 0)), rev(DN_WIDTH)],
        out_specs=[rev(DN_COLS), full((8, w3)), full((1, 128)), full((1, 128)), full((1, 128))],
        out_shape=[SDS((bsz, t, IN_PAD), bf16), SDS((8, w3), f32), SDS((1, 128), f32), SDS((1, 128), f32), SDS((1, 128), f32)],
        scratch_shapes=[pltpu.VMEM((bsz, c + 8, w3), f32), pltpu.VMEM((bsz, c + 8, w3), f32), pltpu.VMEM((bsz, c, w3), f32),
                        pltpu.VMEM((nh, DN_DIM, DN_DIM), f32)],
        compiler_params=_cp(("arbitrary",)),
    )(qkv, qkv, z, ab, conv_w, alog, dtb, gain, sall, tall, do)


SB_TILE = 256
SB_QTILE, SB_KTILE = 256, 256
SB_PAIRS = SB_HEADS // 2


def sb_fwd(sbqkv, gq, gk):
    bsz, t, _ = sbqkv.shape
    bq = min(SB_QTILE, t)
    blk = max(min(SB_KTILE, t), bq)
    nq = t // bq
    scale = SB_DIM ** -0.5

    def body(q_ref, k_ref, v_ref, gq_ref, gk_ref, o_ref, l_ref, q2_sc, kn_sc, v_sc):
        bavg = _group_avg_mats()
        lane = lax.broadcasted_iota(jnp.int32, (1, 128), 1)
        first = lane < SB_DIM
        for p in range(SB_PAIRS):
            ls = slice(p * 128, (p + 1) * 128)
            qn = _pair_norm(q_ref[0, :, ls], gq_ref[...], bavg)
            kn_sc[p] = _pair_norm(k_ref[0, :, ls], gk_ref[...], bavg).astype(bf16)
            v_sc[p] = v_ref[0, :, ls].astype(bf16)
            q2_sc[2 * p] = jnp.where(first, qn, 0.0).astype(bf16)
            q2_sc[2 * p + 1] = jnp.where(first, 0.0, qn).astype(bf16)
        r, c = _iota2((blk, blk))
        ustrict = (r > c).astype(bf16)
        r2, c2 = _iota2((2 * bq, blk))

        def tile(q2s, ks, carry, causal):
            out = []
            for p in range(SB_PAIRS):
                acc, rr = carry[2 * p], carry[2 * p + 1]
                zz = lax.dot_general(q2s[p], kn_sc[p, pl.ds(ks, blk), :], NT, preferred_element_type=f32) * scale
                sp = _softplus(zz)
                lm = -sp if causal is None else jnp.where(causal, -sp, 0.0)
                rem = _dot_x2c(lm, ustrict)
                wgt = jnp.exp(zz - sp + rem + rr)
                if causal is not None:
                    wgt = jnp.where(causal, wgt, 0.0)
                out += [acc + _pdot(wgt.astype(bf16), v_sc[p, pl.ds(ks, blk), :]), rr + jnp.sum(lm, axis=1, keepdims=True)]
            return tuple(out)

        def qloop(qi, _):
            qs = pl.multiple_of(qi * bq, bq)
            kd = qs // blk
            causal = c2 < (r2 & (bq - 1)) + (qs - kd * blk)
            q2s = [jnp.concatenate([q2_sc[2 * p, pl.ds(qs, bq), :], q2_sc[2 * p + 1, pl.ds(qs, bq), :]], axis=0)
                   for p in range(SB_PAIRS)]
            zero = (jnp.zeros((2 * bq, 128), f32), jnp.zeros((2 * bq, 1), f32)) * SB_PAIRS
            carry = lax.fori_loop(1, kd + 1, lambda i, cr: tile(q2s, pl.multiple_of((kd - i) * blk, blk), cr, None),
                                  tile(q2s, pl.multiple_of(kd * blk, blk), zero, causal))
            for p in range(SB_PAIRS):
                acc, rr = carry[2 * p], carry[2 * p + 1]
                o_ref[0, pl.ds(qs, bq), p * 128:(p + 1) * 128] = jnp.where(first, acc[0:bq], acc[bq:2 * bq])
                l_ref[0, pl.ds(qs, bq), p * 128:(p + 1) * 128] = jnp.where(first, rr[0:bq], rr[bq:2 * bq])
            return 0

        lax.fori_loop(0, nq, qloop, 0)

    col = lambda off: pl.BlockSpec((1, t, SB_WIDTH), lambda b: (b, 0, off))
    gsp = pl.BlockSpec((1, 128), lambda b: (0, 0))
    return pl.pallas_call(
        body, name="sb_fwd", grid=(bsz,),
        in_specs=[col(0), col(1), col(2), gsp, gsp],
        out_specs=[col(0), col(0)],
        out_shape=[SDS((bsz, t, SB_WIDTH), f32), SDS((bsz, t, SB_WIDTH), f32)],
        scratch_shapes=[pltpu.VMEM((2 * SB_PAIRS, t, 128), bf16), pltpu.VMEM((SB_PAIRS, t, 128), bf16),
                        pltpu.VMEM((SB_PAIRS, t, 128), bf16)],
        compiler_params=_cp(("arbitrary",)),
    )(sbqkv, sbqkv, sbqkv, gq, gk)


def sb_bwd(sbqkv, gq, gk, ltot, do, dproj):
    bsz, t, _ = sbqkv.shape
    blk = min(SB_TILE, t)
    nq = t // blk
    scale = SB_DIM ** -0.5

    def body(q_ref, k_ref, v_ref, gq_ref, gk_ref, l_ref, do_ref, dp_in, dp_ref, dgq_ref, dgk_ref,
             q2_sc, kn_sc, v_sc, do2_sc, dqn_sc, dkn_sc, dv_sc):
        bavg = _group_avg_mats()
        lane = lax.broadcasted_iota(jnp.int32, (1, 128), 1)
        first = lane < SB_DIM
        fq = lambda x, g: _pair_norm(x, g, bavg)
        vjps = []
        for p in range(SB_PAIRS):
            ls = slice(p * 128, (p + 1) * 128)
            qn, q_vjp = jax.vjp(fq, q_ref[0, :, ls], gq_ref[...])
            kn, k_vjp = jax.vjp(fq, k_ref[0, :, ls], gk_ref[...])
            vjps.append((q_vjp, k_vjp))
            kn_sc[p] = kn.astype(bf16)
            v_sc[p] = v_ref[0, :, ls].astype(bf16)
            dov = do_ref[0, :, ls]
            q2_sc[2 * p] = jnp.where(first, qn, 0.0).astype(bf16)
            q2_sc[2 * p + 1] = jnp.where(first, 0.0, qn).astype(bf16)
            do2_sc[2 * p] = jnp.where(first, dov, 0.0).astype(bf16)
            do2_sc[2 * p + 1] = jnp.where(first, 0.0, dov).astype(bf16)
        dkn_sc[...] = jnp.zeros_like(dkn_sc)
        dv_sc[...] = jnp.zeros_like(dv_sc)
        r, c = _iota2((blk, blk))
        pincl = (r <= c).astype(bf16)
        pstrict = (r < c).astype(bf16)
        r2, c2 = _iota2((2 * blk, blk))
        causal = c2 < (r2 & (blk - 1))

        def tile(q2s, do2s, lts, ks, carry, diag):
            out = []
            for p in range(SB_PAIRS):
                dq, cs, ce = carry[3 * p:3 * p + 3]
                q2, do2 = q2s[p], do2s[p]
                kb = kn_sc[p, pl.ds(ks, blk), :]
                zz = lax.dot_general(q2, kb, NT, preferred_element_type=f32) * scale
                sp = _softplus(zz)
                lm = jnp.where(causal, -sp, 0.0) if diag else -sp
                pre = _dot_x2c(lm, pincl)
                lp = zz - sp
                wgt = jnp.exp(lp + (lts[p] - cs - pre))
                if diag:
                    wgt = jnp.where(causal, wgt, 0.0)
                dw = lax.dot_general(do2, v_sc[p, pl.ds(ks, blk), :], NT, preferred_element_type=f32)
                e = wgt * dw
                ee = ce + _dot_x2c(e, pstrict)
                sig = jnp.exp(lp)
                dz = (e * (1.0 - sig) - ee * sig) * scale
                if diag:
                    dz = jnp.where(causal, dz, 0.0)
                dz = dz.astype(bf16)
                dkn_sc[p, pl.ds(ks, blk), :] += lax.dot_general(dz, q2, TN, preferred_element_type=f32)
                dv_sc[p, pl.ds(ks, blk), :] += lax.dot_general(wgt.astype(bf16), do2, TN, preferred_element_type=f32)
                out += [dq + _pdot(dz, kb), cs + jnp.sum(lm, axis=1, keepdims=True), ce + jnp.sum(e, axis=1, keepdims=True)]
            return tuple(out)

        def qloop(qi, _):
            qs = pl.multiple_of(qi * blk, blk)
            rows = pl.ds(qs, blk)
            q2s = [jnp.concatenate([q2_sc[2 * p, rows, :], q2_sc[2 * p + 1, rows, :]], axis=0) for p in range(SB_PAIRS)]
            do2s = [jnp.concatenate([do2_sc[2 * p, rows, :], do2_sc[2 * p + 1, rows, :]], axis=0) for p in range(SB_PAIRS)]
            lts = [jnp.concatenate([l_ref[0, rows, p * 128:p * 128 + 1], l_ref[0, rows, p * 128 + SB_DIM:p * 128 + SB_DIM + 1]],
                                   axis=0) for p in range(SB_PAIRS)]
            z1 = jnp.zeros((2 * blk, 1), f32)
            carry = lax.fori_loop(0, qi, lambda kj, cr: tile(q2s, do2s, lts, pl.multiple_of(kj * blk, blk), cr, False),
                                  (jnp.zeros((2 * blk, 128), f32), z1, z1) * SB_PAIRS)
            carry = tile(q2s, do2s, lts, qs, carry, True)
            for p in range(SB_PAIRS):
                dq = carry[3 * p]
                dqn_sc[p, rows, :] = jnp.where(first, dq[0:blk], dq[blk:2 * blk])
            return 0

        lax.fori_loop(0, nq, qloop, 0)
        dgq_tot, dgk_tot = jnp.zeros((1, 128), f32), jnp.zeros((1, 128), f32)
        for p in range(SB_PAIRS):
            ls = slice(p * 128, (p + 1) * 128)
            dq_pre, dgq = vjps[p][0](dqn_sc[p])
            dk_pre, dgk = vjps[p][1](dkn_sc[p])
            dp_ref[0, :, p * 128:(p + 1) * 128] = dq_pre.astype(bf16)
            dp_ref[0, :, SB_WIDTH + p * 128:SB_WIDTH + (p + 1) * 128] = dk_pre.astype(bf16)
            dp_ref[0, :, 2 * SB_WIDTH + p * 128:2 * SB_WIDTH + (p + 1) * 128] = dv_sc[p].astype(bf16)
            dgq_tot, dgk_tot = dgq_tot + dgq, dgk_tot + dgk
        dgq_ref[0] = jnp.broadcast_to(dgq_tot, (8, 128))
        dgk_ref[0] = jnp.broadcast_to(dgk_tot, (8, 128))

    col = lambda off: pl.BlockSpec((1, t, SB_WIDTH), lambda b: (b, 0, off), pipeline_mode=pl.Buffered(1))
    gsp = pl.BlockSpec((1, 128), lambda b: (0, 0))
    gout = pl.BlockSpec((1, 8, 128), lambda b: (b, 0, 0))
    return pl.pallas_call(
        body, name="sb_bwd", grid=(bsz,),
        in_specs=[col(0), col(1), col(2), gsp, gsp, col(0), col(0), pl.BlockSpec(memory_space=pl.ANY)],
        out_specs=[pl.BlockSpec((1, t, 3 * SB_WIDTH), lambda b: (b, 0, C_SB // (3 * SB_WIDTH)), pipeline_mode=pl.Buffered(1)),
                   gout, gout],
        out_shape=[SDS(dproj.shape, bf16)] + [SDS((bsz, 8, 128), f32)] * 2,
        input_output_aliases={7: 0},
        scratch_shapes=[pltpu.VMEM((2 * SB_PAIRS, t, 128), bf16), pltpu.VMEM((SB_PAIRS, t, 128), bf16),
                        pltpu.VMEM((SB_PAIRS, t, 128), bf16), pltpu.VMEM((2 * SB_PAIRS, t, 128), bf16),
                        pltpu.VMEM((SB_PAIRS, t, 128), f32), pltpu.VMEM((SB_PAIRS, t, 128), f32), pltpu.VMEM((SB_PAIRS, t, 128), f32)],
        compiler_params=_cp(("arbitrary",)),
    )(sbqkv, sbqkv, sbqkv, gq, gk, ltot, do, dproj)


def sg_pair(u, v, gain, wa, wb, ba, bb, bavg):
    r, c = _iota2((SG_CHUNK, SG_CHUNK))
    lane = lax.broadcasted_iota(jnp.int32, (1, 128), 1)
    first = lane < SG_DIM
    vn = _pair_norm(_gelu(v), gain, bavg)
    tri = c <= r
    mixed = (mm(jnp.where(tri, wa, 0.0), jnp.where(first, vn, 0.0)) + mm(jnp.where(tri, wb, 0.0), jnp.where(first, 0.0, vn))
             + jnp.where(first, ba, bb))
    return _gelu(u) * mixed


def sg_fwd(sguv, gain, w, bt):
    bsz, t, _ = sguv.shape
    nch = t // SG_CHUNK

    def body(uv_ref, g_ref, w_ref, b_ref, o_ref):
        bavg = _group_avg_mats()
        for p in range(2):
            ls = slice(p * 128, (p + 1) * 128)
            o_ref[0, :, ls] = sg_pair(uv_ref[0, :, ls], uv_ref[0, :, SG_WIDTH + p * 128:SG_WIDTH + (p + 1) * 128], g_ref[:, ls],
                                      w_ref[2 * p], w_ref[2 * p + 1], b_ref[:, 2 * p:2 * p + 1], b_ref[:, 2 * p + 1:2 * p + 2], bavg)

    full = lambda shp: pl.BlockSpec(shp, lambda b, n: (0,) * len(shp))
    return pl.pallas_call(
        body, name="sg_fwd", grid=(bsz, nch),
        in_specs=[pl.BlockSpec((1, SG_CHUNK, 2 * SG_WIDTH), lambda b, n: (b, n, 0)), full((1, SG_WIDTH)),
                  full((SG_GROUPS, SG_CHUNK, SG_CHUNK)), full((SG_CHUNK, 128))],
        out_specs=pl.BlockSpec((1, SG_CHUNK, SG_WIDTH), lambda b, n: (b, n, 0)),
        out_shape=SDS((bsz, t, SG_WIDTH), f32),
        compiler_params=_cp(("arbitrary", "arbitrary")),
    )(sguv, gain, w, bt)


def sg_bwd(sguv, gain, w, bt, do, dproj):
    bsz, t, _ = sguv.shape
    nch = t // SG_CHUNK

    def body(uv_ref, g_ref, w_ref, b_ref, do_ref, dp_in, duv_ref, dg_ref, dw_ref, db_ref):
        @pl.when((pl.program_id(0) == 0) & (pl.program_id(1) == 0))
        def _():
            dg_ref[...] = jnp.zeros_like(dg_ref)
            dw_ref[...] = jnp.zeros_like(dw_ref)
            db_ref[...] = jnp.zeros_like(db_ref)

        bavg = _group_avg_mats()
        lane = lax.broadcasted_iota(jnp.int32, (SG_CHUNK, 128), 1)
        dbt = jnp.zeros((SG_CHUNK, 128), f32)
        for p in range(2):
            ls = slice(p * 128, (p + 1) * 128)
            vs = slice(SG_WIDTH + p * 128, SG_WIDTH + (p + 1) * 128)
            prim = (uv_ref[0, :, ls], uv_ref[0, :, vs], g_ref[:, ls], w_ref[2 * p], w_ref[2 * p + 1],
                    b_ref[:, 2 * p:2 * p + 1], b_ref[:, 2 * p + 1:2 * p + 2])
            _, vjp = jax.vjp(lambda *a: sg_pair(*a, bavg), *prim)
            du, dv, dgn, dwa, dwb, dba, dbb = vjp(do_ref[0, :, ls])
            duv_ref[0, :, ls] = du.astype(bf16)
            duv_ref[0, :, vs] = dv.astype(bf16)
            dg_ref[:, ls] += dgn
            dw_ref[2 * p] += dwa
            dw_ref[2 * p + 1] += dwb
            dbt = dbt + jnp.where(lane == 2 * p, dba, 0.0) + jnp.where(lane == 2 * p + 1, dbb, 0.0)
        db_ref[...] += dbt

    full = lambda shp: pl.BlockSpec(shp, lambda b, n: (0,) * len(shp))
    return pl.pallas_call(
        body, name="sg_bwd", grid=(bsz, nch),
        in_specs=[pl.BlockSpec((1, SG_CHUNK, 2 * SG_WIDTH), lambda b, n: (b, n, 0)), full((1, SG_WIDTH)),
                  full((SG_GROUPS, SG_CHUNK, SG_CHUNK)), full((SG_CHUNK, 128)),
                  pl.BlockSpec((1, SG_CHUNK, SG_WIDTH), lambda b, n: (b, n, 0)), pl.BlockSpec(memory_space=pl.ANY)],
        out_specs=[pl.BlockSpec((1, SG_CHUNK, 2 * SG_WIDTH), lambda b, n: (b, n, C_SG // (2 * SG_WIDTH))), full((1, SG_WIDTH)),
                   full((SG_GROUPS, SG_CHUNK, SG_CHUNK)), full((SG_CHUNK, 128))],
        out_shape=[SDS(dproj.shape, bf16), SDS((1, SG_WIDTH), f32), SDS((SG_GROUPS, SG_CHUNK, SG_CHUNK), f32),
                   SDS((SG_CHUNK, 128), f32)],
        input_output_aliases={5: 0},
        compiler_params=_cp(("arbitrary", "arbitrary")),
    )(sguv, gain, w, bt, do, dproj)


def _pad_lanes(v, n=128):
    return jnp.pad(v.reshape(1, -1), ((0, 0), (0, n - v.size)))


def _w_in_runs():
    shard, runs = IN_DIM // N_CHIPS, []
    for s in range(N_CHIPS):
        for a, b, d in ((0, 2048, 0), (2048, 2056, C_AB), (2056, IN_DIM, C_SB)):
            lo, hi = max(shard * s, a), min(shard * (s + 1), b)
            if lo < hi:
                runs.append((s, lo - shard * s, hi - shard * s, d + lo - a))
    return runs


def w_in_from_shards(zone, tr=256):
    def body(z_ref, o_ref):
        o_ref[:, C_AB:C_SB] = jnp.zeros((tr, C_SB - C_AB), zone.dtype)
        for s, a, b, d in _w_in_runs():
            o_ref[:, d:d + b - a] = z_ref[s, :, a:b]

    return pl.pallas_call(
        body, name="w_in_from_shards", grid=(D_MODEL // tr,),
        in_specs=[pl.BlockSpec((N_CHIPS, tr, IN_DIM // N_CHIPS), lambda i: (0, i, 0))],
        out_specs=pl.BlockSpec((tr, IN_PAD), lambda i: (i, 0)), out_shape=SDS((D_MODEL, IN_PAD), zone.dtype),
        compiler_params=_cp(("arbitrary",)))(zone)


def w_in_grad_to_shards(g, tr=256):
    def body(g_ref, o_ref):
        for s, a, b, d in _w_in_runs():
            o_ref[s, :, a:b] = g_ref[:, d:d + b - a]

    return pl.pallas_call(
        body, name="w_in_grad_to_shards", grid=(D_MODEL // tr,),
        in_specs=[pl.BlockSpec((tr, IN_PAD), lambda i: (i, 0))],
        out_specs=pl.BlockSpec((N_CHIPS, tr, IN_DIM // N_CHIPS), lambda i: (0, i, 0)),
        out_shape=SDS((N_CHIPS, D_MODEL, IN_DIM // N_CHIPS), g.dtype), compiler_params=_cp(("arbitrary",)))(g)


def layer_params(p, l):
    return dict(
        g1=p["norm1_g"][l].reshape(1, -1), g2=p["norm2_g"][l].reshape(1, -1),
        conv=jnp.pad(p["conv_w"][l], ((0, 4), (0, 0))), alog=_pad_lanes(p["a_log"][l]), dtb=_pad_lanes(p["dt_bias"][l]),
        dng=p["dn_out_g"][l].reshape(1, -1), gq=jnp.tile(p["sb_q_g"][l].reshape(1, -1), (1, 2)),
        gk=jnp.tile(p["sb_k_g"][l].reshape(1, -1), (1, 2)), sgg=p["sg_v_g"][l].reshape(1, -1), sgw=p["sg_w"][l],
        sgb=jnp.pad(p["sg_b"][l].T, ((0, 0), (0, 124))))


def local_step(x, tgt, small, get_w, put_g, sync_g):
    bsz, t, _ = x.shape
    m = bsz * t
    r3 = lambda a: a.reshape(bsz, t, a.shape[-1])
    r2 = lambda a: a.reshape(m, a.shape[-1])
    xs, saved, ws = x.reshape(m, D_MODEL), [], []
    for l in range(DEPTH):
        sp, w = layer_params(small, l), {}
        w["w_in"] = get_w(l, "in", xs)
        qkv, z, ab, sb, sg, h1 = inproj_fwd(xs, sp["g1"], w["w_in"])
        odn, sall, tall = dn_fwd(r3(qkv), r3(z), r3(ab), sp["conv"], sp["alog"], sp["dtb"], sp["dng"])
        osb, ltot = sb_fwd(r3(sb), sp["gq"], sp["gk"])
        osg = sg_fwd(r3(sg), sp["sgg"], sp["sgw"], sp["sgb"])
        w["w_out"] = get_w(l, "out", osg)
        x2, mix = outproj_fwd(xs, r2(odn), r2(osb), r2(osg), w["w_out"])
        w["w_ff1"], w["w_ff2"] = get_w(l, "ff", x2)
        x3, rlb = ffn_fwd(x2, sp["g2"], w["w_ff1"], w["w_ff2"])
        saved.append(dict(rlb=rlb, h1=h1, x=xs,qkv=qkv, z=z, ab=ab, sb=sb, sg=sg, sall=sall, tall=tall, ltot=ltot, mix=mix, x2=x2))
        ws.append(w)
        xs = x3
    dx, lossp = loss_head(xs, tgt.reshape(m, D_MODEL))
    gsmall = [None] * DEPTH
    token = jnp.zeros((), f32)
    for l in reversed(range(DEPTH)):
        sp, w, s = layer_params(small, l), ws[l], saved[l]
        dx2, dg2, h2, act, df, dyb = ffn_bwd(s["x2"], sp["g2"] + token, w["w_ff1"], w["w_ff2"], s["rlb"], dx)
        g_ff1 = tn_matmul(h2, df, f"dw_ff1_{l}", col_shards=N_CHIPS)
        g_ff2 = tn_matmul(act, dyb, f"dw_ff2_{l}")
        dodn, dosb, dosg, dx2b = outproj_bwd(dx2, w["w_out"])
        g_out = tn_matmul(s["mix"], dx2b, f"dw_out_{l}")
        token = token + put_g(l, "rest", dict(w_out=g_out, w_ff1=g_ff1, w_ff2=g_ff2))
        dproj, dconv, dalog, ddtb, ddng = dn_bwd(r3(s["qkv"]), r3(s["z"]), r3(s["ab"]), sp["conv"], sp["alog"], sp["dtb"],
                                                 sp["dng"] + token, s["sall"], s["tall"], r3(dodn))
        token = sync_g(ddng)
        dproj, dgq, dgk = sb_bwd(r3(s["sb"]), sp["gq"] + token, sp["gk"], s["ltot"], r3(dosb), dproj)
        dproj, dsgg, dsgw, dsgb = sg_bwd(r3(s["sg"]), sp["sgg"], sp["sgw"], sp["sgb"], r3(dosg), dproj)
        dproj = r2(dproj)
        g_in = tn_matmul(s["h1"], dproj, f"dw_in_{l}")
        token = put_g(l, "in", dict(w_in=g_in))
        dx, dg1 = inproj_bwd(s["x"], sp["g1"] + token, w["w_in"], dproj, dx2)
        token = sync_g(dg1)
        fold = lambda a: (a[:, 0, :].sum(0).reshape(2, SB_DIM)).sum(0)
        gsmall[l] = dict(norm1_g=dg1[0], conv_w=dconv[0:DN_CONV], a_log=dalog[0, 0:DN_HEADS], dt_bias=ddtb[0, 0:DN_HEADS],
                         dn_out_g=ddng[0], sb_q_g=fold(dgq), sb_k_g=fold(dgk), sg_v_g=dsgg[0], sg_w=dsgw,
                         sg_b=dsgb[:, 0:SG_GROUPS].T, norm2_g=dg2[0])
    return lossp, dx.reshape(bsz, t, D_MODEL), gsmall


def _chip_peers(x, y):
    return [(1 - x, y), (x, 1 - y), (1 - x, 1 - y)]


_HBM = pl.BlockSpec(memory_space=pltpu.HBM)
_SEM = pl.BlockSpec(memory_space=pltpu.SEMAPHORE)
_EFFECT = pltpu.SideEffectType.DATAFLOW_SIDE_EFFECTING


def _hbm(a):
    return pltpu.with_memory_space_constraint(a, pltpu.HBM)


def _my_half(ref):
    half = ref.shape[0] // 2
    return ref.at[pl.ds(pl.multiple_of(lax.axis_index("c") * half, 8), half)]


def _exchange_copy(src, land, k, j, send, recv, scatter, halve, waiting):
    x, y, c = lax.axis_index("x"), lax.axis_index("y"), lax.axis_index("c")
    px, py = _chip_peers(x, y)[j]
    me, peer = 2 * x + y, 2 * px + py
    if scatter:
        src = src.at[me if waiting else peer]
    dst = land.at[peer if waiting else me]
    if halve:
        src, dst = _my_half(src), _my_half(dst)
    return pltpu.make_async_remote_copy(src_ref=src, dst_ref=dst, send_sem=send.at[3 * k + j],
                                        recv_sem=recv.at[3 * k + j], device_id=(px, py, c), device_id_type=MESH)


def exchange_start(items, name, scatter, after=None):
    arrs = []
    for a, _, _ in items:
        if not any(a is b for b in arrs):
            arrs.append(a)
    pos = [next(i for i, b in enumerate(arrs) if b is a) for a, _, _ in items]
    shapes = [a.shape if idx is None else a.shape[1:] for a, idx, _ in items]
    lands = [lax.empty(s if scatter else (N_CHIPS,) + s, a.dtype) for (a, _, _), s in zip(items, shapes)]
    na, nl = len(arrs), len(lands)
    n_in = na + nl + (after is not None)

    def body(*refs):
        ins, lnd = refs[:na], refs[na:na + nl]
        send, recv = refs[n_in], refs[n_in + 1]
        token = refs[-1]
        for k, (_, idx, halve) in enumerate(items):
            src = ins[pos[k]] if idx is None else ins[pos[k]].at[idx]
            for j in range(3):
                _exchange_copy(src, lnd[k], k, j, send, recv, scatter, halve, False).start()
        token[...] = jnp.zeros_like(token)

    sems = pltpu.SemaphoreType.DMA((3 * nl,))
    extra = [] if after is None else [after]
    out = pl.pallas_call(
        body, name=name,
        out_shape=(sems, sems, *[pltpu.HBM(a.shape, a.dtype) for a in arrs + lands], SDS((8, 128), f32)),
        in_specs=[_HBM] * (na + nl) + [pl.BlockSpec(memory_space=pl.ANY)] * len(extra),
        out_specs=(_SEM, _SEM, *[_HBM] * (na + nl), pl.BlockSpec(memory_space=pltpu.VMEM)),
        input_output_aliases={i: 2 + i for i in range(na + nl)},
        compiler_params=pltpu.CompilerParams(has_side_effects=_EFFECT),
    )(*[_hbm(a) for a in arrs + lands], *extra)
    thru = out[2:2 + na]
    return dict(send=out[0], recv=out[1], src=[(thru[pos[k]], idx) for k, (_, idx, _) in enumerate(items)],
                halve=[h for _, _, h in items], land=list(out[2 + na:2 + na + nl]), token=out[-1], scatter=scatter)


def exchange_wait(st, ks, after, name):
    arrs = []
    for k in ks:
        if not any(st["src"][k][0] is b for b in arrs):
            arrs.append(st["src"][k][0])
    pos = [next(i for i, b in enumerate(arrs) if b is st["src"][k][0]) for k in ks]
    lands = [st["land"][k] for k in ks]
    na, nl = len(arrs), len(lands)

    def body(*refs):
        ins, lnd = refs[:na], refs[na:na + nl]
        send, recv = refs[na + nl], refs[na + nl + 1]
        for t, k in enumerate(ks):
            idx = st["src"][k][1]
            src = ins[pos[t]] if idx is None else ins[pos[t]].at[idx]
            for j in range(3):
                cp = _exchange_copy(src, lnd[t], k, j, send, recv, st["scatter"], st["halve"][k], True)
                cp.wait_send()
                cp.wait_recv()

    out = pl.pallas_call(
        body, name=name, out_shape=tuple(pltpu.HBM(a.shape, a.dtype) for a in arrs + lands),
        in_specs=[_HBM] * (na + nl) + [_SEM, _SEM, pl.BlockSpec(memory_space=pl.ANY)], out_specs=tuple([_HBM] * (na + nl)),
        input_output_aliases={i: i for i in range(na + nl)},
        compiler_params=pltpu.CompilerParams(has_side_effects=_EFFECT),
    )(*arrs, *lands, st["send"], st["recv"], after)
    for k, (a, idx) in enumerate(st["src"]):
        for p, b in enumerate(arrs):
            if a is b:
                st["src"][k] = (out[p], idx)
    return list(out[na:na + nl])


def _sibling_copy(src, land, i, send, recv, other_half):
    x, y, c = lax.axis_index("x"), lax.axis_index("y"), lax.axis_index("c")
    return pltpu.make_async_remote_copy(src_ref=src.at[:, 1 - c] if other_half else src, dst_ref=land, send_sem=send.at[i],
                                        recv_sem=recv.at[i], device_id=(x, y, 1 - c), device_id_type=MESH)


def sibling_start(arrs, name, other_half=False):
    n = len(arrs)
    lands = [lax.empty((a.shape[0],) + a.shape[2:] if other_half else a.shape, a.dtype) for a in arrs]

    def body(*refs):
        ins, lnd = refs[:n], refs[n:2 * n]
        send, recv = refs[2 * n], refs[2 * n + 1]
        token = refs[-1]
        for i in range(n):
            _sibling_copy(ins[i], lnd[i], i, send, recv, other_half).start()
        token[...] = jnp.zeros_like(token)

    sems = pltpu.SemaphoreType.DMA((n,))
    out = pl.pallas_call(
        body, name=name,
        out_shape=(sems, sems, *[pltpu.HBM(a.shape, a.dtype) for a in arrs + lands], SDS((8, 128), f32)),
        in_specs=[_HBM] * (2 * n), out_specs=(_SEM, _SEM, *[_HBM] * (2 * n), pl.BlockSpec(memory_space=pltpu.VMEM)),
        input_output_aliases={i: 2 + i for i in range(2 * n)},
        compiler_params=pltpu.CompilerParams(has_side_effects=_EFFECT),
    )(*[_hbm(a) for a in arrs + lands])
    return dict(send=out[0], recv=out[1], src=list(out[2:2 + n]), land=list(out[2 + n:2 + 2 * n]), token=out[-1],
                other_half=other_half)


def sibling_wait(st, after, name):
    n = len(st["src"])

    def body(*refs):
        ins, lnd = refs[:n], refs[n:2 * n]
        send, recv = refs[2 * n], refs[2 * n + 1]
        for i in range(n):
            cp = _sibling_copy(ins[i], lnd[i], i, send, recv, st["other_half"])
            cp.wait_send()
            cp.wait_recv()

    out = pl.pallas_call(
        body, name=name, out_shape=tuple(pltpu.HBM(a.shape, a.dtype) for a in st["src"] + st["land"]),
        in_specs=[_HBM] * (2 * n) + [_SEM, _SEM, pl.BlockSpec(memory_space=pl.ANY)], out_specs=tuple([_HBM] * (2 * n)),
        input_output_aliases={i: i for i in range(2 * n)},
        compiler_params=pltpu.CompilerParams(has_side_effects=_EFFECT),
    )(*st["src"], *st["land"], st["send"], st["recv"], after)
    return list(out[:n]), list(out[n:])


def swap_halves(zones, name):
    n = len(zones)

    def body(*refs):
        outs = refs[n:2 * n]
        send, recv = refs[2 * n:]
        x, y, c = lax.axis_index("x"), lax.axis_index("y"), lax.axis_index("c")
        cps = []
        for i in range(n):
            for j, (px, py) in enumerate(_chip_peers(x, y)):
                part = _my_half(outs[i].at[2 * px + py])
                cps.append(pltpu.make_async_remote_copy(src_ref=part, dst_ref=part, send_sem=send.at[3 * i + j],
                                                        recv_sem=recv.at[3 * i + j], device_id=(x, y, 1 - c), device_id_type=MESH))
        for cp in cps:
            cp.start()
        for cp in cps:
            cp.wait_send()
            cp.wait_recv()

    any_spec = pl.BlockSpec(memory_space=pl.ANY)
    return pl.pallas_call(
        body, name=name, in_specs=[any_spec] * n, out_specs=[any_spec] * n, out_shape=[SDS(a.shape, a.dtype) for a in zones],
        input_output_aliases={i: i for i in range(n)},
        scratch_shapes=[pltpu.SemaphoreType.DMA((3 * n,)), pltpu.SemaphoreType.DMA((3 * n,))],
    )(*zones)


def _ids_spec(grid, in_specs, out_specs):
    return pltpu.PrefetchScalarGridSpec(num_scalar_prefetch=1, grid=grid, in_specs=in_specs, out_specs=out_specs)


def pair_sum(ids, a, b, name, tr=512):
    nd, _, rows, cols = a.shape
    tr = min(tr, rows)
    assert rows % tr == 0

    def body(ids_ref, a_ref, b_ref, o_ref):
        o_ref[...] = (a_ref[0].astype(f32) + b_ref[...].astype(f32)).astype(bf16)

    spec = pl.BlockSpec((1, tr, cols), lambda d, i, ids: (d, i, 0))
    return pl.pallas_call(
        body, name=name,
        grid_spec=_ids_spec((nd, rows // tr), [pl.BlockSpec((1, 1, tr, cols), lambda d, i, ids: (d, ids[1], i, 0)), spec], spec),
        out_shape=SDS((nd, rows, cols), bf16), compiler_params=_cp(("arbitrary", "arbitrary")))(ids, a, b)


def allreduce_small(v):
    def body(v_ref, o_ref, rbuf, send, recv):
        x, y, c = lax.axis_index("x"), lax.axis_index("y"), lax.axis_index("c")
        o_ref[...] = v_ref[...]
        for s, peer in enumerate([(x, y, 1 - c), (1 - x, y, c), (x, 1 - y, c)]):
            cp = pltpu.make_async_remote_copy(src_ref=o_ref, dst_ref=rbuf.at[s], send_sem=send.at[s], recv_sem=recv.at[s],
                                              device_id=peer, device_id_type=MESH)
            cp.start()
            cp.wait()
            o_ref[...] = o_ref[...] + rbuf[s]

    vm = pl.BlockSpec(memory_space=pltpu.VMEM)
    return pl.pallas_call(
        body, name="allreduce_small", in_specs=[vm], out_specs=vm, out_shape=SDS(v.shape, f32),
        scratch_shapes=[pltpu.VMEM((3,) + v.shape, f32), pltpu.SemaphoreType.DMA((3,)), pltpu.SemaphoreType.DMA((3,))],
        compiler_params=_cp(),
    )(v)


def sum_partials(ids, zone, mine, name, tr=256):
    _, rows, cols = zone.shape
    tr = min(tr, rows)
    assert rows % tr == 0

    def body(ids_ref, m_ref, z1_ref, z2_ref, z3_ref, o_ref):
        o_ref[...] = ((m_ref[0].astype(f32) + z1_ref[0].astype(f32)) + z2_ref[0].astype(f32)) + z3_ref[0].astype(f32)

    slot = lambda flip: pl.BlockSpec((1, tr, cols), lambda i, ids: (ids[0] ^ flip, i, 0))
    return pl.pallas_call(
        body, name=name,
        grid_spec=_ids_spec((rows // tr,), [slot(0), slot(1), slot(2), slot(3)], pl.BlockSpec((tr, cols), lambda i, ids: (i, 0))),
        out_shape=SDS((rows, cols), f32), compiler_params=_cp(("arbitrary",)),
    )(ids, mine, zone, zone, zone)


def adamw(w, m, v, gs, name, layer=0, prev=None, tr=256):
    hrows, cols = gs[0].shape
    rows = hrows * len(gs)
    tr = min(tr, hrows)
    assert hrows % tr == 0 and w.shape[0] % rows == 0
    off, nth = layer * (rows // tr), hrows // tr

    def body(w_ref, m_ref, v_ref, *rest):
        g_ref, d_ref, mo_ref, vo_ref = rest[-4:]
        if len(gs) == 1:
            g = rest[0][...]
        else:
            g = jnp.where(pl.program_id(0) // nth == lax.axis_index("c"), rest[0][...], rest[1][...])
        mn = ADAM_B1 * m_ref[...] + (1.0 - ADAM_B1) * g
        vn = ADAM_B2 * v_ref[...] + (1.0 - ADAM_B2) * jnp.square(g)
        m_hat = mn / (1.0 - ADAM_B1 ** ADAM_STEP)
        v_hat = vn / (1.0 - ADAM_B2 ** ADAM_STEP)
        g_ref[...] = g
        d_ref[...] = -ADAM_LR * (m_hat / (jnp.sqrt(v_hat) + ADAM_EPS) + ADAM_WD * w_ref[...])
        mo_ref[...] = mn
        vo_ref[...] = vn

    loc = pl.BlockSpec((tr, cols), lambda i: (i % nth, 0))
    glob = pl.BlockSpec((tr, cols), lambda i: (off + i, 0))
    extra = [] if prev is None else list(prev)
    return pl.pallas_call(
        body, name=name, grid=(rows // tr,),
        in_specs=[glob] * 3 + [loc] * len(gs) + [pl.BlockSpec(memory_space=pl.ANY)] * len(extra),
        out_specs=[glob] * 4, out_shape=[SDS(w.shape, f32)] * 4,
        input_output_aliases={3 + len(gs) + j: j for j in range(len(extra))},
        compiler_params=_cp(("arbitrary",)),
    )(w, m, v, *gs, *extra)


BIG = ("w_in", "w_out", "w_ff1", "w_ff2")
SMALL = ("norm1_g", "conv_w", "a_log", "dt_bias", "dn_out_g", "sb_q_g", "sb_k_g", "sg_v_g", "sg_w", "sg_b", "norm2_g")
WEIGHTS = ("norm1_g", "w_in", "conv_w", "a_log", "dt_bias", "dn_out_g", "sb_q_g", "sb_k_g", "sg_v_g", "sg_w", "sg_b",
           "w_out", "norm2_g", "w_ff1", "w_ff2")


PACK_ROWS = 256


def _rows_of(shape):
    n = 1
    for d in shape:
        n *= d
    return -(-n // 1024) * 8, n


def _pack(arrs):
    parts = []
    for a in arrs:
        r, n = _rows_of(a.shape)
        parts.append(jnp.pad(a.reshape(-1), (0, r * 128 - n)).reshape(r, 128))
    rows = sum(p.shape[0] for p in parts)
    parts.append(jnp.zeros((-rows % PACK_ROWS, 128), arrs[0].dtype))
    return jnp.concatenate(parts, axis=0)


def _unpack(packed, shapes):
    out, o = [], 0
    for s in shapes:
        r, n = _rows_of(s)
        out.append(packed[o:o + r].reshape(-1)[0:n].reshape(s))
        o += r
    return out


def kernel(x, norm1_g, w_in, conv_w, a_log, dt_bias, dn_out_g, sb_q_g, sb_k_g, sg_v_g, sg_w, sg_b, w_out, norm2_g, w_ff1, w_ff2, loss_target, m_norm1_g, m_w_in, m_conv_w, m_a_log, m_dt_bias, m_dn_out_g, m_sb_q_g, m_sb_k_g, m_sg_v_g, m_sg_w, m_sg_b, m_w_out, m_norm2_g, m_w_ff1, m_w_ff2, v_norm1_g, v_w_in, v_conv_w, v_a_log, v_dt_bias, v_dn_out_g, v_sb_q_g, v_sb_k_g, v_sg_v_g, v_sg_w, v_sg_b, v_w_out, v_norm2_g, v_w_ff1, v_w_ff2):
    w = dict(norm1_g=norm1_g, w_in=w_in, conv_w=conv_w, a_log=a_log, dt_bias=dt_bias, dn_out_g=dn_out_g, sb_q_g=sb_q_g,
             sb_k_g=sb_k_g, sg_v_g=sg_v_g, sg_w=sg_w, sg_b=sg_b, w_out=w_out, norm2_g=norm2_g, w_ff1=w_ff1, w_ff2=w_ff2)
    mom = dict(norm1_g=m_norm1_g, w_in=m_w_in, conv_w=m_conv_w, a_log=m_a_log, dt_bias=m_dt_bias, dn_out_g=m_dn_out_g,
               sb_q_g=m_sb_q_g, sb_k_g=m_sb_k_g, sg_v_g=m_sg_v_g, sg_w=m_sg_w, sg_b=m_sg_b, w_out=m_w_out, norm2_g=m_norm2_g,
               w_ff1=m_w_ff1, w_ff2=m_w_ff2)
    var = dict(norm1_g=v_norm1_g, w_in=v_w_in, conv_w=v_conv_w, a_log=v_a_log, dt_bias=v_dt_bias, dn_out_g=v_dn_out_g,
               sb_q_g=v_sb_q_g, sb_k_g=v_sb_k_g, sg_v_g=v_sg_v_g, sg_w=v_sg_w, sg_b=v_sg_b, w_out=v_w_out, norm2_g=v_norm2_g,
               w_ff1=v_w_ff1, w_ff2=v_w_ff2)
    chip = 2 * lax.axis_index("x") + lax.axis_index("y")

    wb = [{k: w[k][l].astype(bf16) for k in BIG} for l in range(DEPTH)]
    ags = {0: exchange_start([(conv_w, None, False)] + [(wb[0][k], None, True) for k in BIG], "allgather_start_0", scatter=False)}
    item = lambda l, k: (l, (l == 0) + BIG.index(k))

    def landed(items, after, name):
        ag, ks = ags[items[0][0]], [k for _, k in items]
        zones = exchange_wait(ag, ks, after, name)
        halved = [t for t, k in enumerate(ks) if ag["halve"][k]]
        for t, z in zip(halved, swap_halves([zones[t] for t in halved], name.replace("wait", "pass"))):
            zones[t] = z
        return [lax.dynamic_update_slice_in_dim(z, ag["src"][k][0][None], chip, axis=0) for z, k in zip(zones, ks)]

    def whole(k, z):
        if k == "w_in":
            return w_in_from_shards(z)
        return z if k == "w_ff1" else z.reshape(-1, D_MODEL)

    g_conv, first_in = landed([(0, 0), item(0, "w_in")], x, "allgather_wait_in0")
    small = {k: w[k] for k in SMALL}
    small["conv_w"] = jnp.transpose(g_conv, (1, 2, 0, 3)).reshape(DEPTH, DN_CONV, 3 * DN_WIDTH)
    cache = {}

    def get_w(l, part, after):
        if part == "in":
            return whole("w_in", first_in if l == 0 else landed([item(l, "w_in")], after, f"allgather_wait_in{l}")[0])
        if part == "out":
            zs = landed([item(l, k) for k in ("w_out", "w_ff1", "w_ff2")], after, f"allgather_wait_rest{l}")
            if l + 1 < DEPTH:
                ags[l + 1] = exchange_start([(wb[l + 1][k], None, True) for k in BIG], f"allgather_start_{l + 1}",
                                            scatter=False, after=zs[0])
            cache[l] = (whole("w_ff1", zs[1]), whole("w_ff2", zs[2]))
            return whole("w_out", zs[0])
        return cache[l]

    rs, pending = {}, []
    ids = jnp.stack([chip, lax.axis_index("c")]).astype(jnp.int32)

    def put_g(l, tag, g):
        names = [k for k in BIG if k in g]
        by_dest = [w_in_grad_to_shards(g[k]) if k == "w_in" else g[k] for k in names]
        halves = [a.reshape(N_CHIPS, 2, -1, a.shape[-1]) for a in by_dest]
        st = sibling_start(halves, f"pair_swap_start_{tag}{l}", other_half=True)
        pending.append((l, tag, names, st))
        return st["token"][0, 0]

    def sync_g(after):
        token = jnp.zeros((), f32)
        while pending:
            l, tag, names, st = pending.pop(0)
            halves, got = sibling_wait(st, after, f"pair_swap_wait_{tag}{l}")
            pair = [pair_sum(ids, a, b, f"pair_sum_{k}_{l}") for k, a, b in zip(names, halves, got)]
            rs[l, tag] = dict(exchange_start([(a, None, False) for a in pair], f"scatter_start_{tag}{l}", scatter=True), names=names)
            token = token + rs[l, tag]["token"][0, 0]
        return token

    lossp, grad_x, gsmall = local_step(x, loss_target, small, get_w, put_g, sync_g)

    def sum_group(l, tag, after):
        st = rs[l, tag]
        zones = exchange_wait(st, list(range(len(st["names"]))), after, f"scatter_wait_{tag}{l}")
        sums = [sum_partials(ids, zones[i], st["src"][i][0], f"sum_{k}_{l}") for i, k in enumerate(st["names"])]
        return sibling_start(sums, f"swap_sums_start_{tag}{l}")

    def update_group(l, tag, swap, after, prev):
        sums, others = sibling_wait(swap, after, f"swap_sums_wait_{tag}{l}")
        outs = dict(prev)
        for i, k in enumerate(rs[l, tag]["names"]):
            r2 = lambda a: a.reshape(-1, a.shape[-1])
            outs[k] = adamw(r2(w[k]), r2(mom[k]), r2(var[k]), (sums[i], others[i]), f"adamw_{k}_{l}", layer=l, prev=prev.get(k))
        return outs

    swap_r = sum_group(1, "rest", rs[0, "in"]["token"])
    swap_i = sum_group(1, "in", swap_r["token"])
    done = update_group(1, "rest", swap_r, swap_i["token"], {})
    done = update_group(1, "in", swap_i, done["w_ff2"][0], done)
    res = {}

    full_shapes = [(DEPTH,) + tuple(gsmall[0][k].shape) for k in SMALL]
    packed = _pack([jnp.stack([gsmall[l][k] for l in range(DEPTH)]) for k in SMALL] + [jnp.sum(lossp).reshape(1)])
    *totals, loss = _unpack(allreduce_small(packed), full_shapes + [(1,)])
    loss = loss[0]
    gfull = dict(zip(SMALL, totals))
    cs = 3 * DN_WIDTH // N_CHIPS
    gfull["conv_w"] = lax.dynamic_slice_in_dim(gfull["conv_w"], chip * cs, cs, axis=2)
    gp, wp, mp, vp = (_pack([d[k] for k in SMALL]) for d in (gfull, w, mom, var))
    outs = adamw(wp, mp, vp, (gp,), "adamw_small")
    loc_shapes = [w[k].shape for k in SMALL]
    unp = [_unpack(o, loc_shapes) for o in outs]
    for i, k in enumerate(SMALL):
        res[k] = [unp[j][i] for j in range(4)]

    swap_r = sum_group(0, "rest", outs[0])
    swap_i = sum_group(0, "in", swap_r["token"])
    done = update_group(0, "rest", swap_r, swap_i["token"], done)
    done = update_group(0, "in", swap_i, done["w_ff2"][0], done)
    for k in BIG:
        res[k] = [o.reshape(w[k].shape) for o in done[k]]

    return (loss, grad_x, *[res[k][0] for k in WEIGHTS], *[res[k][1] for k in WEIGHTS], *[res[k][2] for k in WEIGHTS],
            *[res[k][3] for k in WEIGHTS])
```

```python
import functools

import jax
import jax.numpy as jnp
from jax import lax
from jax.experimental import pallas as pl
from jax.experimental.pallas import tpu as pltpu

f32 = jnp.float32
bf16 = jnp.bfloat16
SDS = jax.ShapeDtypeStruct
MESH = pl.DeviceIdType.MESH

NORM_EPS = 1e-6
D_MODEL = 1024
DEPTH = 2
DN_HEADS, DN_DIM, DN_WIDTH, DN_CONV, DN_CHUNK = 4, 128, 512, 4, 64
SB_HEADS, SB_DIM, SB_WIDTH = 4, 64, 256
SG_GROUPS, SG_DIM, SG_WIDTH, SG_CHUNK = 4, 64, 256, 128
D_FF = 4096
IN_DIM = 3336
C_QKV, C_Z, C_AB, C_SB, C_SG, IN_PAD = 0, 1536, 2048, 2304, 3072, 3584
DN_COLS = C_SB
N_CHIPS = 4

ADAM_LR, ADAM_B1, ADAM_B2, ADAM_EPS, ADAM_WD, ADAM_STEP = 0.001, 0.9, 0.999, 1e-08, 0.01, 10

VMEM_LIMIT = 56 * 1024 * 1024


def _cp(sem=None, **kw):
    if sem is not None:
        kw["dimension_semantics"] = sem
    return pltpu.CompilerParams(vmem_limit_bytes=VMEM_LIMIT, **kw)


def _split2(x):
    hi = x.astype(bf16)
    lo = (x - hi.astype(f32)).astype(bf16)
    return hi, lo


NT = (((1,), (1,)), ((), ()))
TN = (((0,), (0,)), ((), ()))
_DIMS2 = dict(nn=(((1,), (0,)), ((), ())), nt=NT, tn=TN)
_DIMS3 = dict(nn=(((2,), (1,)), ((0,), (0,))), nt=(((2,), (2,)), ((0,), (0,))), tn=(((1,), (1,)), ((0,), (0,))))


def _dg(a, b, kind):
    return lax.dot_general(a, b, (_DIMS2 if a.ndim == 2 else _DIMS3)[kind], preferred_element_type=f32)


def _pdot(a, b):
    return _dg(a, b, "nn")


def _dot_hp(a, b):
    ah, al = _split2(a)
    bh, bl = _split2(b)
    return _pdot(ah, bh) + _pdot(ah, bl) + _pdot(al, bh)


def _dot_x2c(a, m):
    lead = a.shape[:-1]
    ah, al = _split2(a.reshape(-1, a.shape[-1]))
    return (_pdot(ah, m) + _pdot(al, m)).reshape(lead + (m.shape[1],))


def _dot_cx2(m, a):
    if a.ndim == 3:
        m = jnp.broadcast_to(m, (a.shape[0],) + m.shape)
    ah, al = _split2(a)
    return _pdot(m, ah) + _pdot(m, al)


def _nt(a, b):
    return _dg(a.astype(bf16), b.astype(bf16), "nt")


def _tn(a, b):
    return _dg(a.astype(bf16), b.astype(bf16), "tn")


def _nn(a, b):
    return _dg(a.astype(bf16), b.astype(bf16), "nn")


@jax.custom_vjp
def mm(a, b):
    return _nn(a, b)


mm.defvjp(lambda a, b: (_nn(a, b), (a, b)), lambda r, g: (_nt(g, r[1]), _tn(r[0], g)))


@jax.custom_vjp
def mm_nt(a, b):
    return _nt(a, b)


mm_nt.defvjp(lambda a, b: (_nt(a, b), (a, b)), lambda r, g: (_nn(g, r[1]), _tn(g, r[0])))


@jax.custom_vjp
def mm_tn(a, b):
    return _tn(a, b)


mm_tn.defvjp(lambda a, b: (_tn(a, b), (a, b)), lambda r, g: (_nt(r[1], g), _nn(r[0], g)))


@jax.custom_vjp
def rmul_const(a, m, mt):
    return _dot_x2c(a, m)


rmul_const.defvjp(lambda a, m, mt: (_dot_x2c(a, m), (m, mt)),
                  lambda r, g: (_dot_x2c(g, r[1]), jnp.zeros_like(r[0]), jnp.zeros_like(r[1])))


@jax.custom_vjp
def lmul_const(m, mt, a):
    return _dot_cx2(m, a)


lmul_const.defvjp(lambda m, mt, a: (_dot_cx2(m, a), (m, mt)),
                  lambda r, g: (jnp.zeros_like(r[0]), jnp.zeros_like(r[1]), _dot_cx2(r[1], g)))


@jax.custom_vjp
def mm_hl(t, x):
    th, tl = _split2(t)
    xb = x.astype(bf16)
    return _pdot(th, xb) + _pdot(tl, xb)


def _mm_hl_bwd(r, g):
    t, x = r
    th, tl = _split2(t)
    gb = g.astype(bf16)
    return _nt(g, x), _dg(th, gb, "tn") + _dg(tl, gb, "tn")


mm_hl.defvjp(lambda t, x: (mm_hl(t, x), (t, x)), _mm_hl_bwd)


def inv_unit_lower(lm):
    c = lm.shape[-1]
    r, cc = _iota2((c, c))
    eye = (r == cc).astype(f32)
    t = eye - lm
    p = -lm
    k = 1
    while 2 * k < c:
        p = _nn(p, p)
        t = t + _nn(t, p)
        k *= 2
    res = eye - t - _dot_hp(lm, t)
    return t + _nn(t, res)


@jax.custom_vjp
def inv_given(lm, t):
    return t


inv_given.defvjp(lambda lm, t: (t, t), lambda t, g: (-_nt(_tn(t, g), t), jnp.zeros_like(t)))


def _sigmoid(x):
    return 1.0 / (1.0 + jnp.exp(-x))


def _softplus(x):
    return jnp.maximum(x, 0.0) + jnp.log(1.0 + jnp.exp(-jnp.abs(x)))


def _silu(x):
    return x * _sigmoid(x)


def _gelu(x):
    return 0.5 * x * (1.0 + jnp.tanh(0.7978845608028654 * (x + 0.044715 * (x * x * x))))


def _iota2(shape):
    return lax.broadcasted_iota(jnp.int32, shape, 0), lax.broadcasted_iota(jnp.int32, shape, 1)


def _group_avg_mats():
    r, c = _iota2((128, 128))
    return jnp.where((r // 64) == (c // 64), 1.0 / 64.0, 0.0).astype(bf16)


def _pair_norm(x, gain, bavg):
    ms = rmul_const(x * x, bavg, bavg)
    return x * lax.rsqrt(ms + NORM_EPS) * gain


def _rms(x):
    r = lax.rsqrt(jnp.mean(x * x, axis=-1, keepdims=True) + NORM_EPS)
    return r


_IN_GROUPS = ((C_QKV, C_Z), (C_Z, C_AB), (C_AB, C_AB + 128), (C_SB, C_SG), (C_SG, IN_PAD))


def inproj_fwd(x, g, wp, tm=256):
    m = x.shape[0]

    def body(x_ref, g_ref, w_ref, *outs):
        xv = x_ref[...]
        h = (xv * _rms(xv) * g_ref[...]).astype(bf16)
        outs[-1][...] = h
        for (a, b), o in zip(_IN_GROUPS, outs):
            o[...] = _pdot(h, w_ref[:, a:b])

    widths = [b - a for a, b in _IN_GROUPS]
    return pl.pallas_call(
        body, name="inproj_fwd", grid=(m // tm,),
        in_specs=[pl.BlockSpec((tm, D_MODEL), lambda i: (i, 0)), pl.BlockSpec((1, D_MODEL), lambda i: (0, 0)),
                  pl.BlockSpec((D_MODEL, IN_PAD), lambda i: (0, 0))],
        out_specs=[pl.BlockSpec((tm, wd), lambda i: (i, 0)) for wd in widths + [D_MODEL]],
        out_shape=[SDS((m, wd), f32) for wd in widths] + [SDS((m, D_MODEL), bf16)],
        compiler_params=_cp(("arbitrary",)),
    )(x, g, wp)


def inproj_bwd(x, g, wp, dproj, dres, tm=256):
    m = x.shape[0]

    def body(x_ref, g_ref, w_ref, dp_ref, dr_ref, dx_ref, dg_ref):
        xv = x_ref[...]
        r = _rms(xv)
        xn = xv * r
        gv = g_ref[...]
        dh = lax.dot_general(dp_ref[...], w_ref[...], NT, preferred_element_type=f32)
        dxn = dh * gv
        dx_ref[...] = dr_ref[...] + r * (dxn - xn * jnp.mean(dxn * xn, axis=-1, keepdims=True))

        @pl.when(pl.program_id(0) == 0)
        def _():
            dg_ref[...] = jnp.zeros_like(dg_ref)

        dg_ref[...] += jnp.sum(dh * xn, axis=0, keepdims=True)

    return pl.pallas_call(
        body, name="inproj_bwd", grid=(m // tm,),
        in_specs=[pl.BlockSpec((tm, D_MODEL), lambda i: (i, 0)), pl.BlockSpec((1, D_MODEL), lambda i: (0, 0)),
                  pl.BlockSpec((D_MODEL, IN_PAD), lambda i: (0, 0)), pl.BlockSpec((tm, IN_PAD), lambda i: (i, 0)),
                  pl.BlockSpec((tm, D_MODEL), lambda i: (i, 0))],
        out_specs=[pl.BlockSpec((tm, D_MODEL), lambda i: (i, 0)), pl.BlockSpec((1, D_MODEL), lambda i: (0, 0))],
        out_shape=[SDS((m, D_MODEL), f32), SDS((1, D_MODEL), f32)],
        compiler_params=_cp(("arbitrary",)),
    )(x, g, wp, dproj, dres)


def outproj_fwd(x, odn, osb, osg, wo, tm=512):
    m = x.shape[0]

    def body(x_ref, a_ref, b_ref, c_ref, w_ref, x2_ref, mix_ref):
        mix_ref[:, 0:DN_WIDTH] = a_ref[...].astype(bf16)
        mix_ref[:, DN_WIDTH:DN_WIDTH + SB_WIDTH] = b_ref[...].astype(bf16)
        mix_ref[:, DN_WIDTH + SB_WIDTH:D_MODEL] = c_ref[...].astype(bf16)
        x2_ref[...] = x_ref[...] + _pdot(mix_ref[...], w_ref[...])

    row = lambda w: pl.BlockSpec((tm, w), lambda i: (i, 0))
    return pl.pallas_call(
        body, name="outproj_fwd", grid=(m // tm,),
        in_specs=[row(D_MODEL), row(DN_WIDTH), row(SB_WIDTH), row(SG_WIDTH), pl.BlockSpec((D_MODEL, D_MODEL), lambda i: (0, 0))],
        out_specs=[row(D_MODEL), row(D_MODEL)],
        out_shape=[SDS((m, D_MODEL), f32), SDS((m, D_MODEL), bf16)],
        compiler_params=_cp(("arbitrary",)),
    )(x, odn, osb, osg, wo)


def outproj_bwd(dx2, wo, tm=512):
    m = dx2.shape[0]

    def body(d_ref, w_ref, a_ref, b_ref, c_ref, db_ref):
        db = d_ref[...].astype(bf16)
        db_ref[...] = db
        dm = lax.dot_general(db, w_ref[...], NT, preferred_element_type=f32)
        a_ref[...] = dm[:, 0:DN_WIDTH]
        b_ref[...] = dm[:, DN_WIDTH:DN_WIDTH + SB_WIDTH]
        c_ref[...] = dm[:, DN_WIDTH + SB_WIDTH:D_MODEL]

    row = lambda w: pl.BlockSpec((tm, w), lambda i: (i, 0))
    return pl.pallas_call(
        body, name="outproj_bwd", grid=(m // tm,),
        in_specs=[row(D_MODEL), pl.BlockSpec((D_MODEL, D_MODEL), lambda i: (0, 0))],
        out_specs=[row(DN_WIDTH), row(SB_WIDTH), row(SG_WIDTH), row(D_MODEL)],
        out_shape=[SDS((m, DN_WIDTH), f32), SDS((m, SB_WIDTH), f32), SDS((m, SG_WIDTH), f32), SDS((m, D_MODEL), bf16)],
        compiler_params=_cp(("arbitrary",)),
    )(dx2, wo)


FF_CHUNK = D_FF // N_CHIPS


def _load_weights_once(pairs, sem):
    @pl.when(pl.program_id(0) == 0)
    def _():
        cps = [pltpu.make_async_copy(h, v, sem.at[i]) for i, (h, v) in enumerate(pairs)]
        for c in cps:
            c.start()
        for c in cps:
            c.wait()


def ffn_fwd(x2, g, w1, w2, tm=256):
    m = x2.shape[0]

    def body(x_ref, g_ref, w1_hbm, w2_hbm, y_ref, rl_ref, w1_v, w2_v, sem):
        _load_weights_once(((w1_hbm, w1_v), (w2_hbm, w2_v)), sem)
        xv = x_ref[...]
        h = (xv * _rms(xv) * g_ref[...]).astype(bf16)
        acc = xv
        for j in range(0, D_FF, FF_CHUNK):
            f = _pdot(h, w1_v[j // FF_CHUNK])
            rl = jnp.maximum(f, 0.0)
            rl_ref[:, j:j + FF_CHUNK] = rl.astype(bf16)
            acc = acc + _pdot((rl * rl).astype(bf16), w2_v[j:j + FF_CHUNK, :])
        y_ref[...] = acc

    return pl.pallas_call(
        body, name="ffn_fwd", grid=(m // tm,),
        in_specs=[pl.BlockSpec((tm, D_MODEL), lambda i: (i, 0)), pl.BlockSpec((1, D_MODEL), lambda i: (0, 0)),
                  pl.BlockSpec(memory_space=pl.ANY), pl.BlockSpec(memory_space=pl.ANY)],
        out_specs=[pl.BlockSpec((tm, D_MODEL), lambda i: (i, 0)), pl.BlockSpec((tm, D_FF), lambda i: (i, 0))],
        out_shape=[SDS((m, D_MODEL), f32), SDS((m, D_FF), bf16)],
        scratch_shapes=[pltpu.VMEM((N_CHIPS, D_MODEL, FF_CHUNK), bf16), pltpu.VMEM((D_FF, D_MODEL), bf16), pltpu.SemaphoreType.DMA((2,))],
        compiler_params=_cp(("arbitrary",)),
    )(x2, g, w1, w2)


def ffn_bwd(x2, g, w1, w2, rlb, dy, tm=256):
    m = x2.shape[0]

    def body(x_ref, g_ref, w1_hbm, w2_hbm, rl_ref, dy_ref, dx_ref, dg_ref, h_ref, a_ref, df_ref, dyb_ref, w1_v, w2_v, sem):
        _load_weights_once(((w1_hbm, w1_v), (w2_hbm, w2_v)), sem)
        xv = x_ref[...]
        r = _rms(xv)
        xn = xv * r
        gv = g_ref[...]
        h = (xn * gv).astype(bf16)
        h_ref[...] = h
        dyv = dy_ref[...]
        dyb = dyv.astype(bf16)
        dyb_ref[...] = dyb
        dh = jnp.zeros((tm, D_MODEL), f32)
        for j in range(0, D_FF, FF_CHUNK):
            rl = rl_ref[:, j:j + FF_CHUNK].astype(f32)
            a_ref[:, j:j + FF_CHUNK] = (rl * rl).astype(bf16)
            da = lax.dot_general(dyb, w2_v[j:j + FF_CHUNK, :], NT, preferred_element_type=f32)
            df = (da * (2.0 * rl)).astype(bf16)
            df_ref[:, j:j + FF_CHUNK] = df
            dh = dh + lax.dot_general(df, w1_v[j // FF_CHUNK], NT, preferred_element_type=f32)
        dxn = dh * gv
        dx_ref[...] = dyv + r * (dxn - xn * jnp.mean(dxn * xn, axis=-1, keepdims=True))

        @pl.when(pl.program_id(0) == 0)
        def _():
            dg_ref[...] = jnp.zeros_like(dg_ref)

        dg_ref[...] += jnp.sum(dh * xn, axis=0, keepdims=True)

    row = lambda w: pl.BlockSpec((tm, w), lambda i: (i, 0))
    return pl.pallas_call(
        body, name="ffn_bwd", grid=(m // tm,),
        in_specs=[row(D_MODEL), pl.BlockSpec((1, D_MODEL), lambda i: (0, 0)),
                  pl.BlockSpec(memory_space=pl.ANY), pl.BlockSpec(memory_space=pl.ANY), row(D_FF), row(D_MODEL)],
        out_specs=[row(D_MODEL), pl.BlockSpec((1, D_MODEL), lambda i: (0, 0)), row(D_MODEL), row(D_FF), row(D_FF), row(D_MODEL)],
        out_shape=[SDS((m, D_MODEL), f32), SDS((1, D_MODEL), f32), SDS((m, D_MODEL), bf16), SDS((m, D_FF), bf16),
                   SDS((m, D_FF), bf16), SDS((m, D_MODEL), bf16)],
        scratch_shapes=[pltpu.VMEM((N_CHIPS, D_MODEL, FF_CHUNK), bf16), pltpu.VMEM((D_FF, D_MODEL), bf16), pltpu.SemaphoreType.DMA((2,))],
        compiler_params=_cp(("arbitrary",)),
    )(x2, g, w1, w2, rlb, dy)


def _tile(n, cap):
    best = 128
    for t in range(128, cap + 1, 128):
        if n % t == 0:
            best = t
    return best


def tn_matmul(a, b, name, col_shards=1, tk=2048):
    m, ka = a.shape
    n = b.shape[1]
    ti = _tile(ka, 1024)
    tj = _tile(n // col_shards, 1152)
    tk = min(tk, m)
    nk = m // tk
    jps = (n // col_shards) // tj

    def body(a_ref, b_ref, o_ref, acc):
        k = pl.program_id(2)

        @pl.when(k == 0)
        def _():
            acc[...] = jnp.zeros_like(acc)

        acc[...] += lax.dot_general(a_ref[...], b_ref[...], TN, preferred_element_type=f32)

        @pl.when(k == nk - 1)
        def _():
            o_ref[...] = acc[...].astype(bf16).reshape(o_ref.shape)

    if col_shards == 1:
        out_shape, out_spec = SDS((ka, n), bf16), pl.BlockSpec((ti, tj), lambda i, j, k: (i, j))
    else:
        out_shape = SDS((col_shards, ka, n // col_shards), bf16)
        out_spec = pl.BlockSpec((1, ti, tj), lambda i, j, k: (j // jps, i, j % jps))
    return pl.pallas_call(
        body, name=name, grid=(ka // ti, n // tj, nk),
        in_specs=[pl.BlockSpec((tk, ti), lambda i, j, k: (k, i)), pl.BlockSpec((tk, tj), lambda i, j, k: (k, j))],
        out_specs=out_spec, out_shape=out_shape,
        scratch_shapes=[pltpu.VMEM((ti, tj), f32)],
        compiler_params=_cp(("arbitrary", "arbitrary", "arbitrary")),
    )(a, b)


def loss_head(y, tgt, tm=512):
    m = y.shape[0]

    def body(y_ref, t_ref, dy_ref, l_ref):
        e = y_ref[...] - t_ref[...]
        dy_ref[...] = e * (1.0 / D_MODEL)

        @pl.when(pl.program_id(0) == 0)
        def _():
            l_ref[...] = jnp.zeros_like(l_ref)

        l_ref[...] += jnp.sum(e * e, axis=0, keepdims=True) * (0.5 / D_MODEL)

    row = pl.BlockSpec((tm, D_MODEL), lambda i: (i, 0))
    return pl.pallas_call(
        body, name="loss_head", grid=(m // tm,), in_specs=[row, row],
        out_specs=[row, pl.BlockSpec((1, D_MODEL), lambda i: (0, 0))],
        out_shape=[SDS((m, D_MODEL), f32), SDS((1, D_MODEL), f32)],
        compiler_params=_cp(("arbitrary",)),
    )(y, tgt)


def _dn_consts():
    c = DN_CHUNK
    r, cc = _iota2((c, c))
    lt = (cc <= r).astype(bf16)
    ltt = (r <= cc).astype(bf16)
    return lt, ltt


def dn_chunk(cq, ck, cv, g, beta, z, s, gain, lt, ltt, t_given=None):
    c = DN_CHUNK
    r, cc = _iota2((c, c))
    q = cq * lax.rsqrt(jnp.sum(cq * cq, axis=-1, keepdims=True) + NORM_EPS) * (DN_DIM ** -0.5)
    k = ck * lax.rsqrt(jnp.sum(ck * ck, axis=-1, keepdims=True) + NORM_EPS)
    r2, c2 = _iota2((c, 128))
    uaug = jnp.where((c2 < c) & (r2 > c2), 1.0, 0.0) + jnp.where(c2 == c, 1.0, 0.0)
    gam_all = lmul_const(lt, ltt, g * uaug)
    gam_cc = gam_all[:, :, 0:c]
    gam = gam_all[:, :, c:c + 1]
    dec = jnp.where(cc <= r, jnp.exp(jnp.where(cc <= r, gam_cc, 0.0)), 0.0)
    kk = mm_nt(k, k)
    lm = jnp.where(cc < r, beta * kk * dec, 0.0)
    t = inv_unit_lower(lm) if t_given is None else inv_given(lm, t_given)
    eg = jnp.exp(gam)
    sol = mm_hl(t, jnp.concatenate([cv * beta, k * (beta * eg)], axis=2))
    u, w = sol[:, :, 0:DN_DIM], sol[:, :, DN_DIM:2 * DN_DIM]
    qk = jnp.where(cc <= r, mm_nt(q, k) * dec, 0.0)
    glast = jnp.sum(g, axis=1, keepdims=True)
    qd = q * eg
    kd = k * jnp.exp(glast - gam)
    un = u - mm(w, s)
    o = mm(qd, s) + mm(qk, un)
    s_new = s * jnp.exp(glast) + mm_tn(kd, un)
    on = o * lax.rsqrt(jnp.mean(o * o, axis=-1, keepdims=True) + NORM_EPS) * gain * _silu(z)
    return on, s_new, t


def _dn_gates(ab, al_row, dt_row):
    pre = ab + dt_row
    return -jnp.exp(al_row) * _softplus(pre), _sigmoid(ab), _sigmoid(pre)


def _dn_chains(cacts, gates, z_ref):
    cq, ck, cv, g, beta, z = [], [], [], [], [], []
    for bi, cact in enumerate(cacts):
        for h in range(DN_HEADS):
            cq.append(cact[:, h * DN_DIM:(h + 1) * DN_DIM])
            ck.append(cact[:, DN_WIDTH + h * DN_DIM:DN_WIDTH + (h + 1) * DN_DIM])
            cv.append(cact[:, 2 * DN_WIDTH + h * DN_DIM:2 * DN_WIDTH + (h + 1) * DN_DIM])
            g.append(gates[bi][0][:, h:h + 1])
            beta.append(gates[bi][1][:, DN_HEADS + h:DN_HEADS + h + 1])
            z.append(z_ref[bi, :, h * DN_DIM:(h + 1) * DN_DIM])
    return tuple(jnp.stack(v) for v in (cq, ck, cv, g, beta, z))


def _conv_rows(xe_ref, b, w_ref):
    y = w_ref[0:1, :] * xe_ref[b, pl.ds(5, DN_CHUNK), :]
    for i in range(1, DN_CONV):
        y = y + w_ref[i:i + 1, :] * xe_ref[b, pl.ds(5 + i, DN_CHUNK), :]
    return y


def dn_fwd(qkv, z, ab, conv_w, alog, dtb, gain):
    bsz, t, _ = qkv.shape
    nc = t // DN_CHUNK
    c = DN_CHUNK
    nh = bsz * DN_HEADS

    def body(qkv_ref, z_ref, ab_ref, w_ref, al_ref, dt_ref, g_ref, o_ref, sall_ref, tall_ref, xe, s_sc):
        n = pl.program_id(0)

        @pl.when(n == 0)
        def _():
            xe[:, 0:8, :] = jnp.zeros((bsz, 8, 3 * DN_WIDTH), f32)
            s_sc[...] = jnp.zeros_like(s_sc)

        lt, ltt = _dn_consts()
        cacts = []
        for b in range(bsz):
            xe[b, 8:8 + c, :] = qkv_ref[b]
            cacts.append(_silu(_conv_rows(xe, b, w_ref)))
            xe[b, 0:8, :] = xe[b, c:c + 8, :]
        gates = [_dn_gates(ab_ref[b], al_ref[...], dt_ref[...]) for b in range(bsz)]
        s = s_sc[...]
        sall_ref[0] = s
        on, sn, tt = dn_chunk(*_dn_chains(cacts, gates, z_ref), s, g_ref[...], lt, ltt)
        tall_ref[0] = tt
        s_sc[...] = sn
        for b in range(bsz):
            for h in range(DN_HEADS):
                o_ref[b, :, h * DN_DIM:(h + 1) * DN_DIM] = on[b * DN_HEADS + h]

    blk = lambda w: pl.BlockSpec((bsz, c, w), lambda n: (0, n, 0))
    full = lambda shp: pl.BlockSpec(shp, lambda n: (0,) * len(shp))
    return pl.pallas_call(
        body, name="dn_fwd", grid=(nc,),
        in_specs=[blk(3 * DN_WIDTH), blk(DN_WIDTH), blk(128), full((8, 3 * DN_WIDTH)), full((1, 128)), full((1, 128)), full((1, 128))],
        out_specs=[blk(DN_WIDTH), pl.BlockSpec((1, nh, DN_DIM, DN_DIM), lambda n: (n, 0, 0, 0)),
                   pl.BlockSpec((1, nh, c, c), lambda n: (n, 0, 0, 0))],
        out_shape=[SDS((bsz, t, DN_WIDTH), f32), SDS((nc, nh, DN_DIM, DN_DIM), f32), SDS((nc, nh, c, c), f32)],
        scratch_shapes=[pltpu.VMEM((bsz, c + 8, 3 * DN_WIDTH), f32), pltpu.VMEM((nh, DN_DIM, DN_DIM), f32)],
        compiler_params=_cp(("arbitrary",)),
    )(qkv, z, ab, conv_w, alog, dtb, gain)


def dn_bwd(qkv, z, ab, conv_w, alog, dtb, gain, sall, tall, do):
    bsz, t, _ = qkv.shape
    nc = t // DN_CHUNK
    c = DN_CHUNK
    nh = bsz * DN_HEADS
    w3 = 3 * DN_WIDTH

    def body(qkv_ref, prev_ref, z_ref, ab_ref, w_ref, al_ref, dt_ref, g_ref, sall_ref, tall_ref, do_ref,
             dp_ref, dw_ref, dal_ref, ddt_ref, dg_ref, xe, dye, dc_sc, ds_sc):
        n = pl.program_id(0)
        first = (nc - 1 - n) == 0

        @pl.when(n == 0)
        def _():
            dye[:, c:c + 8, :] = jnp.zeros((bsz, 8, w3), f32)
            ds_sc[...] = jnp.zeros_like(ds_sc)
            dw_ref[...] = jnp.zeros_like(dw_ref)
            dal_ref[...] = jnp.zeros_like(dal_ref)
            ddt_ref[...] = jnp.zeros_like(ddt_ref)
            dg_ref[...] = jnp.zeros_like(dg_ref)

        lt, ltt = _dn_consts()
        lane_c = lax.broadcasted_iota(jnp.int32, (c, 128), 1)
        ys, sigs = [], []
        for b in range(bsz):
            xe[b, 0:8, :] = jnp.where(first, 0.0, prev_ref[b])
            xe[b, 8:8 + c, :] = qkv_ref[b]
            ys.append(_conv_rows(xe, b, w_ref))
            sigs.append(_sigmoid(ys[b]))
        gates = [_dn_gates(ab_ref[b], al_ref[...], dt_ref[...]) for b in range(bsz)]
        ops = _dn_chains([y * sg for y, sg in zip(ys, sigs)], gates, z_ref)
        tt = tall_ref[0]
        _, vjp = jax.vjp(lambda *p: dn_chunk(*p, lt, ltt, t_given=tt)[0:2], *ops, sall_ref[0], g_ref[...])
        don = jnp.stack([do_ref[b, :, h * DN_DIM:(h + 1) * DN_DIM] for b in range(bsz) for h in range(DN_HEADS)])
        dcq, dck, dcv, dg, dbeta, dzz, dsp, dgn = vjp((don, ds_sc[...]))
        ds_sc[...] = dsp
        dg_ref[...] += dgn
        for b in range(bsz):
            dgate = jnp.zeros((c, 128), f32)
            for h in range(DN_HEADS):
                i = b * DN_HEADS + h
                dc_sc[b, :, h * DN_DIM:(h + 1) * DN_DIM] = dcq[i]
                dc_sc[b, :, DN_WIDTH + h * DN_DIM:DN_WIDTH + (h + 1) * DN_DIM] = dck[i]
                dc_sc[b, :, 2 * DN_WIDTH + h * DN_DIM:2 * DN_WIDTH + (h + 1) * DN_DIM] = dcv[i]
                dp_ref[b, :, C_Z + h * DN_DIM:C_Z + (h + 1) * DN_DIM] = dzz[i].astype(bf16)
                dgate = dgate + jnp.where(lane_c == h, dg[i], 0.0) + jnp.where(lane_c == DN_HEADS + h, dbeta[i], 0.0)
            gg, beta, sig_pre = gates[b]
            is_g = lane_c < DN_HEADS
            dpre = jnp.where(is_g, dgate * (-jnp.exp(al_ref[...])) * sig_pre, 0.0)
            dp_ref[b, :, C_AB:C_AB + 128] = (dpre + jnp.where(is_g, 0.0, dgate * beta * (1.0 - beta))).astype(bf16)
            dp_ref[b, :, C_AB + 128:DN_COLS] = jnp.zeros((c, DN_COLS - C_AB - 128), bf16)
            dal_ref[...] += jnp.sum(jnp.where(is_g, dgate * gg, 0.0), axis=0, keepdims=True)
            ddt_ref[...] += jnp.sum(dpre, axis=0, keepdims=True)
            y, sig = ys[b], sigs[b]
            dy = dc_sc[b] * (sig * (1.0 + y * (1.0 - sig)))
            dye[b, 0:c, :] = dy
            dx = w_ref[3:4, :] * dy
            for i in range(DN_CONV - 1):
                dx = dx + w_ref[i:i + 1, :] * dye[b, pl.ds(3 - i, c), :]
            dp_ref[b, :, 0:w3] = dx.astype(bf16)
            for i in range(DN_CONV):
                dw_ref[i:i + 1, :] += jnp.sum(dy * xe[b, pl.ds(5 + i, c), :], axis=0, keepdims=True)
            dye[b, c:c + 8, :] = dye[b, 0:8, :]

    rev = lambda w: pl.BlockSpec((bsz, c, w), lambda n: (0, nc - 1 - n, 0))
    full = lambda shp: pl.BlockSpec(shp, lambda n: (0,) * len(shp))
    prev = pl.BlockSpec((bsz, 8, w3), lambda n: (0, jnp.maximum((nc - 1 - n) * (c // 8) - 1, 0), 0))
    return pl.pallas_call(
        body, name="dn_bwd", grid=(nc,),
        in_specs=[rev(w3), prev, rev(DN_WIDTH), rev(128), full((8, w3)), full((1, 128)), full((1, 128)), full((1, 128)),
                  pl.BlockSpec((1, nh, DN_DIM, DN_DIM), lambda n: (nc - 1 - n, 0, 0, 0)),
                  pl.BlockSpec((1, nh, c, c), lambda n: (nc - 1 - n, 0, 0, 0)), rev(DN_WIDTH)],
        out_specs=[rev(DN_COLS), full((8, w3)), full((1, 128)), full((1, 128)), full((1, 128))],
        out_shape=[SDS((bsz, t, IN_PAD), bf16), SDS((8, w3), f32), SDS((1, 128), f32), SDS((1, 128), f32), SDS((1, 128), f32)],
        scratch_shapes=[pltpu.VMEM((bsz, c + 8, w3), f32), pltpu.VMEM((bsz, c + 8, w3), f32), pltpu.VMEM((bsz, c, w3), f32),
                        pltpu.VMEM((nh, DN_DIM, DN_DIM), f32)],
        compiler_params=_cp(("arbitrary",)),
    )(qkv, qkv, z, ab, conv_w, alog, dtb, gain, sall, tall, do)


SB_TILE = 256
SB_QTILE, SB_KTILE = 256, 256
SB_PAIRS = SB_HEADS // 2


def sb_fwd(sbqkv, gq, gk):
    bsz, t, _ = sbqkv.shape
    bq = min(SB_QTILE, t)
    blk = max(min(SB_KTILE, t), bq)
    nq = t // bq
    scale = SB_DIM ** -0.5

    def body(q_ref, k_ref, v_ref, gq_ref, gk_ref, o_ref, l_ref, q2_sc, kn_sc, v_sc):
        bavg = _group_avg_mats()
        lane = lax.broadcasted_iota(jnp.int32, (1, 128), 1)
        first = lane < SB_DIM
        for p in range(SB_PAIRS):
            ls = slice(p * 128, (p + 1) * 128)
            qn = _pair_norm(q_ref[0, :, ls], gq_ref[...], bavg)
            kn_sc[p] = _pair_norm(k_ref[0, :, ls], gk_ref[...], bavg).astype(bf16)
            v_sc[p] = v_ref[0, :, ls].astype(bf16)
            q2_sc[2 * p] = jnp.where(first, qn, 0.0).astype(bf16)
            q2_sc[2 * p + 1] = jnp.where(first, 0.0, qn).astype(bf16)
        r, c = _iota2((blk, blk))
        ustrict = (r > c).astype(bf16)
        r2, c2 = _iota2((2 * bq, blk))

        def tile(q2s, ks, carry, causal):
            out = []
            for p in range(SB_PAIRS):
                acc, rr = carry[2 * p], carry[2 * p + 1]
                zz = lax.dot_general(q2s[p], kn_sc[p, pl.ds(ks, blk), :], NT, preferred_element_type=f32) * scale
                sp = _softplus(zz)
                lm = -sp if causal is None else jnp.where(causal, -sp, 0.0)
                rem = _dot_x2c(lm, ustrict)
                wgt = jnp.exp(zz - sp + rem + rr)
                if causal is not None:
                    wgt = jnp.where(causal, wgt, 0.0)
                out += [acc + _pdot(wgt.astype(bf16), v_sc[p, pl.ds(ks, blk), :]), rr + jnp.sum(lm, axis=1, keepdims=True)]
            return tuple(out)

        def qloop(qi, _):
            qs = pl.multiple_of(qi * bq, bq)
            kd = qs // blk
            causal = c2 < (r2 & (bq - 1)) + (qs - kd * blk)
            q2s = [jnp.concatenate([q2_sc[2 * p, pl.ds(qs, bq), :], q2_sc[2 * p + 1, pl.ds(qs, bq), :]], axis=0)
                   for p in range(SB_PAIRS)]
            zero = (jnp.zeros((2 * bq, 128), f32), jnp.zeros((2 * bq, 1), f32)) * SB_PAIRS
            carry = lax.fori_loop(1, kd + 1, lambda i, cr: tile(q2s, pl.multiple_of((kd - i) * blk, blk), cr, None),
                                  tile(q2s, pl.multiple_of(kd * blk, blk), zero, causal))
            for p in range(SB_PAIRS):
                acc, rr = carry[2 * p], carry[2 * p + 1]
                o_ref[0, pl.ds(qs, bq), p * 128:(p + 1) * 128] = jnp.where(first, acc[0:bq], acc[bq:2 * bq])
                l_ref[0, pl.ds(qs, bq), p * 128:(p + 1) * 128] = jnp.where(first, rr[0:bq], rr[bq:2 * bq])
            return 0

        lax.fori_loop(0, nq, qloop, 0)

    col = lambda off: pl.BlockSpec((1, t, SB_WIDTH), lambda b: (b, 0, off))
    gsp = pl.BlockSpec((1, 128), lambda b: (0, 0))
    return pl.pallas_call(
        body, name="sb_fwd", grid=(bsz,),
        in_specs=[col(0), col(1), col(2), gsp, gsp],
        out_specs=[col(0), col(0)],
        out_shape=[SDS((bsz, t, SB_WIDTH), f32), SDS((bsz, t, SB_WIDTH), f32)],
        scratch_shapes=[pltpu.VMEM((2 * SB_PAIRS, t, 128), bf16), pltpu.VMEM((SB_PAIRS, t, 128), bf16),
                        pltpu.VMEM((SB_PAIRS, t, 128), bf16)],
        compiler_params=_cp(("arbitrary",)),
    )(sbqkv, sbqkv, sbqkv, gq, gk)


def sb_bwd(sbqkv, gq, gk, ltot, do, dproj):
    bsz, t, _ = sbqkv.shape
    blk = min(SB_TILE, t)
    nq = t // blk
    scale = SB_DIM ** -0.5

    def body(q_ref, k_ref, v_ref, gq_ref, gk_ref, l_ref, do_ref, dp_in, dp_ref, dgq_ref, dgk_ref,
             q2_sc, kn_sc, v_sc, do2_sc, dqn_sc, dkn_sc, dv_sc):
        bavg = _group_avg_mats()
        lane = lax.broadcasted_iota(jnp.int32, (1, 128), 1)
        first = lane < SB_DIM
        fq = lambda x, g: _pair_norm(x, g, bavg)
        vjps = []
        for p in range(SB_PAIRS):
            ls = slice(p * 128, (p + 1) * 128)
            qn, q_vjp = jax.vjp(fq, q_ref[0, :, ls], gq_ref[...])
            kn, k_vjp = jax.vjp(fq, k_ref[0, :, ls], gk_ref[...])
            vjps.append((q_vjp, k_vjp))
            kn_sc[p] = kn.astype(bf16)
            v_sc[p] = v_ref[0, :, ls].astype(bf16)
            dov = do_ref[0, :, ls]
            q2_sc[2 * p] = jnp.where(first, qn, 0.0).astype(bf16)
            q2_sc[2 * p + 1] = jnp.where(first, 0.0, qn).astype(bf16)
            do2_sc[2 * p] = jnp.where(first, dov, 0.0).astype(bf16)
            do2_sc[2 * p + 1] = jnp.where(first, 0.0, dov).astype(bf16)
        dkn_sc[...] = jnp.zeros_like(dkn_sc)
        dv_sc[...] = jnp.zeros_like(dv_sc)
        r, c = _iota2((blk, blk))
        pincl = (r <= c).astype(bf16)
        pstrict = (r < c).astype(bf16)
        r2, c2 = _iota2((2 * blk, blk))
        causal = c2 < (r2 & (blk - 1))

        def tile(q2s, do2s, lts, ks, carry, diag):
            out = []
            for p in range(SB_PAIRS):
                dq, cs, ce = carry[3 * p:3 * p + 3]
                q2, do2 = q2s[p], do2s[p]
                kb = kn_sc[p, pl.ds(ks, blk), :]
                zz = lax.dot_general(q2, kb, NT, preferred_element_type=f32) * scale
                sp = _softplus(zz)
                lm = jnp.where(causal, -sp, 0.0) if diag else -sp
                pre = _dot_x2c(lm, pincl)
                lp = zz - sp
                wgt = jnp.exp(lp + (lts[p] - cs - pre))
                if diag:
                    wgt = jnp.where(causal, wgt, 0.0)
                dw = lax.dot_general(do2, v_sc[p, pl.ds(ks, blk), :], NT, preferred_element_type=f32)
                e = wgt * dw
                ee = ce + _dot_x2c(e, pstrict)
                sig = jnp.exp(lp)
                dz = (e * (1.0 - sig) - ee * sig) * scale
                if diag:
                    dz = jnp.where(causal, dz, 0.0)
                dz = dz.astype(bf16)
                dkn_sc[p, pl.ds(ks, blk), :] += lax.dot_general(dz, q2, TN, preferred_element_type=f32)
                dv_sc[p, pl.ds(ks, blk), :] += lax.dot_general(wgt.astype(bf16), do2, TN, preferred_element_type=f32)
                out += [dq + _pdot(dz, kb), cs + jnp.sum(lm, axis=1, keepdims=True), ce + jnp.sum(e, axis=1, keepdims=True)]
            return tuple(out)

        def qloop(qi, _):
            qs = pl.multiple_of(qi * blk, blk)
            rows = pl.ds(qs, blk)
            q2s = [jnp.concatenate([q2_sc[2 * p, rows, :], q2_sc[2 * p + 1, rows, :]], axis=0) for p in range(SB_PAIRS)]
            do2s = [jnp.concatenate([do2_sc[2 * p, rows, :], do2_sc[2 * p + 1, rows, :]], axis=0) for p in range(SB_PAIRS)]
            lts = [jnp.concatenate([l_ref[0, rows, p * 128:p * 128 + 1], l_ref[0, rows, p * 128 + SB_DIM:p * 128 + SB_DIM + 1]],
                                   axis=0) for p in range(SB_PAIRS)]
            z1 = jnp.zeros((2 * blk, 1), f32)
            carry = lax.fori_loop(0, qi, lambda kj, cr: tile(q2s, do2s, lts, pl.multiple_of(kj * blk, blk), cr, False),
                                  (jnp.zeros((2 * blk, 128), f32), z1, z1) * SB_PAIRS)
            carry = tile(q2s, do2s, lts, qs, carry, True)
            for p in range(SB_PAIRS):
                dq = carry[3 * p]
                dqn_sc[p, rows, :] = jnp.where(first, dq[0:blk], dq[blk:2 * blk])
            return 0

        lax.fori_loop(0, nq, qloop, 0)
        dgq_tot, dgk_tot = jnp.zeros((1, 128), f32), jnp.zeros((1, 128), f32)
        for p in range(SB_PAIRS):
            ls = slice(p * 128, (p + 1) * 128)
            dq_pre, dgq = vjps[p][0](dqn_sc[p])
            dk_pre, dgk = vjps[p][1](dkn_sc[p])
            dp_ref[0, :, p * 128:(p + 1) * 128] = dq_pre.astype(bf16)
            dp_ref[0, :, SB_WIDTH + p * 128:SB_WIDTH + (p + 1) * 128] = dk_pre.astype(bf16)
            dp_ref[0, :, 2 * SB_WIDTH + p * 128:2 * SB_WIDTH + (p + 1) * 128] = dv_sc[p].astype(bf16)
            dgq_tot, dgk_tot = dgq_tot + dgq, dgk_tot + dgk
        dgq_ref[0] = jnp.broadcast_to(dgq_tot, (8, 128))
        dgk_ref[0] = jnp.broadcast_to(dgk_tot, (8, 128))

    col = lambda off: pl.BlockSpec((1, t, SB_WIDTH), lambda b: (b, 0, off), pipeline_mode=pl.Buffered(1))
    gsp = pl.BlockSpec((1, 128), lambda b: (0, 0))
    gout = pl.BlockSpec((1, 8, 128), lambda b: (b, 0, 0))
    return pl.pallas_call(
        body, name="sb_bwd", grid=(bsz,),
        in_specs=[col(0), col(1), col(2), gsp, gsp, col(0), col(0), pl.BlockSpec(memory_space=pl.ANY)],
        out_specs=[pl.BlockSpec((1, t, 3 * SB_WIDTH), lambda b: (b, 0, C_SB // (3 * SB_WIDTH)), pipeline_mode=pl.Buffered(1)),
                   gout, gout],
        out_shape=[SDS(dproj.shape, bf16)] + [SDS((bsz, 8, 128), f32)] * 2,
        input_output_aliases={7: 0},
        scratch_shapes=[pltpu.VMEM((2 * SB_PAIRS, t, 128), bf16), pltpu.VMEM((SB_PAIRS, t, 128), bf16),
                        pltpu.VMEM((SB_PAIRS, t, 128), bf16), pltpu.VMEM((2 * SB_PAIRS, t, 128), bf16),
                        pltpu.VMEM((SB_PAIRS, t, 128), f32), pltpu.VMEM((SB_PAIRS, t, 128), f32), pltpu.VMEM((SB_PAIRS, t, 128), f32)],
        compiler_params=_cp(("arbitrary",)),
    )(sbqkv, sbqkv, sbqkv, gq, gk, ltot, do, dproj)


def sg_pair(u, v, gain, wa, wb, ba, bb, bavg):
    r, c = _iota2((SG_CHUNK, SG_CHUNK))
    lane = lax.broadcasted_iota(jnp.int32, (1, 128), 1)
    first = lane < SG_DIM
    vn = _pair_norm(_gelu(v), gain, bavg)
    tri = c <= r
    mixed = (mm(jnp.where(tri, wa, 0.0), jnp.where(first, vn, 0.0)) + mm(jnp.where(tri, wb, 0.0), jnp.where(first, 0.0, vn))
             + jnp.where(first, ba, bb))
    return _gelu(u) * mixed


def sg_fwd(sguv, gain, w, bt):
    bsz, t, _ = sguv.shape
    nch = t // SG_CHUNK

    def body(uv_ref, g_ref, w_ref, b_ref, o_ref):
        bavg = _group_avg_mats()
        for p in range(2):
            ls = slice(p * 128, (p + 1) * 128)
            o_ref[0, :, ls] = sg_pair(uv_ref[0, :, ls], uv_ref[0, :, SG_WIDTH + p * 128:SG_WIDTH + (p + 1) * 128], g_ref[:, ls],
                                      w_ref[2 * p], w_ref[2 * p + 1], b_ref[:, 2 * p:2 * p + 1], b_ref[:, 2 * p + 1:2 * p + 2], bavg)

    full = lambda shp: pl.BlockSpec(shp, lambda b, n: (0,) * len(shp))
    return pl.pallas_call(
        body, name="sg_fwd", grid=(bsz, nch),
        in_specs=[pl.BlockSpec((1, SG_CHUNK, 2 * SG_WIDTH), lambda b, n: (b, n, 0)), full((1, SG_WIDTH)),
                  full((SG_GROUPS, SG_CHUNK, SG_CHUNK)), full((SG_CHUNK, 128))],
        out_specs=pl.BlockSpec((1, SG_CHUNK, SG_WIDTH), lambda b, n: (b, n, 0)),
        out_shape=SDS((bsz, t, SG_WIDTH), f32),
        compiler_params=_cp(("arbitrary", "arbitrary")),
    )(sguv, gain, w, bt)


def sg_bwd(sguv, gain, w, bt, do, dproj):
    bsz, t, _ = sguv.shape
    nch = t // SG_CHUNK

    def body(uv_ref, g_ref, w_ref, b_ref, do_ref, dp_in, duv_ref, dg_ref, dw_ref, db_ref):
        @pl.when((pl.program_id(0) == 0) & (pl.program_id(1) == 0))
        def _():
            dg_ref[...] = jnp.zeros_like(dg_ref)
            dw_ref[...] = jnp.zeros_like(dw_ref)
            db_ref[...] = jnp.zeros_like(db_ref)

        bavg = _group_avg_mats()
        lane = lax.broadcasted_iota(jnp.int32, (SG_CHUNK, 128), 1)
        dbt = jnp.zeros((SG_CHUNK, 128), f32)
        for p in range(2):
            ls = slice(p * 128, (p + 1) * 128)
            vs = slice(SG_WIDTH + p * 128, SG_WIDTH + (p + 1) * 128)
            prim = (uv_ref[0, :, ls], uv_ref[0, :, vs], g_ref[:, ls], w_ref[2 * p], w_ref[2 * p + 1],
                    b_ref[:, 2 * p:2 * p + 1], b_ref[:, 2 * p + 1:2 * p + 2])
            _, vjp = jax.vjp(lambda *a: sg_pair(*a, bavg), *prim)
            du, dv, dgn, dwa, dwb, dba, dbb = vjp(do_ref[0, :, ls])
            duv_ref[0, :, ls] = du.astype(bf16)
            duv_ref[0, :, vs] = dv.astype(bf16)
            dg_ref[:, ls] += dgn
            dw_ref[2 * p] += dwa
            dw_ref[2 * p + 1] += dwb
            dbt = dbt + jnp.where(lane == 2 * p, dba, 0.0) + jnp.where(lane == 2 * p + 1, dbb, 0.0)
        db_ref[...] += dbt

    full = lambda shp: pl.BlockSpec(shp, lambda b, n: (0,) * len(shp))
    return pl.pallas_call(
        body, name="sg_bwd", grid=(bsz, nch),
        in_specs=[pl.BlockSpec((1, SG_CHUNK, 2 * SG_WIDTH), lambda b, n: (b, n, 0)), full((1, SG_WIDTH)),
                  full((SG_GROUPS, SG_CHUNK, SG_CHUNK)), full((SG_CHUNK, 128)),
                  pl.BlockSpec((1, SG_CHUNK, SG_WIDTH), lambda b, n: (b, n, 0)), pl.BlockSpec(memory_space=pl.ANY)],
        out_specs=[pl.BlockSpec((1, SG_CHUNK, 2 * SG_WIDTH), lambda b, n: (b, n, C_SG // (2 * SG_WIDTH))), full((1, SG_WIDTH)),
                   full((SG_GROUPS, SG_CHUNK, SG_CHUNK)), full((SG_CHUNK, 128))],
        out_shape=[SDS(dproj.shape, bf16), SDS((1, SG_WIDTH), f32), SDS((SG_GROUPS, SG_CHUNK, SG_CHUNK), f32),
                   SDS((SG_CHUNK, 128), f32)],
        input_output_aliases={5: 0},
        compiler_params=_cp(("arbitrary", "arbitrary")),
    )(sguv, gain, w, bt, do, dproj)


def _pad_lanes(v, n=128):
    return jnp.pad(v.reshape(1, -1), ((0, 0), (0, n - v.size)))


def _w_in_runs():
    shard, runs = IN_DIM // N_CHIPS, []
    for s in range(N_CHIPS):
        for a, b, d in ((0, 2048, 0), (2048, 2056, C_AB), (2056, IN_DIM, C_SB)):
            lo, hi = max(shard * s, a), min(shard * (s + 1), b)
            if lo < hi:
                runs.append((s, lo - shard * s, hi - shard * s, d + lo - a))
    return runs


def w_in_from_shards(zone, tr=256):
    def body(z_ref, o_ref):
        o_ref[:, C_AB:C_SB] = jnp.zeros((tr, C_SB - C_AB), zone.dtype)
        for s, a, b, d in _w_in_runs():
            o_ref[:, d:d + b - a] = z_ref[s, :, a:b]

    return pl.pallas_call(
        body, name="w_in_from_shards", grid=(D_MODEL // tr,),
        in_specs=[pl.BlockSpec((N_CHIPS, tr, IN_DIM // N_CHIPS), lambda i: (0, i, 0))],
        out_specs=pl.BlockSpec((tr, IN_PAD), lambda i: (i, 0)), out_shape=SDS((D_MODEL, IN_PAD), zone.dtype),
        compiler_params=_cp(("arbitrary",)))(zone)


def w_in_grad_to_shards(g, tr=256):
    def body(g_ref, o_ref):
        for s, a, b, d in _w_in_runs():
            o_ref[s, :, a:b] = g_ref[:, d:d + b - a]

    return pl.pallas_call(
        body, name="w_in_grad_to_shards", grid=(D_MODEL // tr,),
        in_specs=[pl.BlockSpec((tr, IN_PAD), lambda i: (i, 0))],
        out_specs=pl.BlockSpec((N_CHIPS, tr, IN_DIM // N_CHIPS), lambda i: (0, i, 0)),
        out_shape=SDS((N_CHIPS, D_MODEL, IN_DIM // N_CHIPS), g.dtype), compiler_params=_cp(("arbitrary",)))(g)


def layer_params(p, l):
    return dict(
        g1=p["norm1_g"][l].reshape(1, -1), g2=p["norm2_g"][l].reshape(1, -1),
        conv=jnp.pad(p["conv_w"][l], ((0, 4), (0, 0))), alog=_pad_lanes(p["a_log"][l]), dtb=_pad_lanes(p["dt_bias"][l]),
        dng=p["dn_out_g"][l].reshape(1, -1), gq=jnp.tile(p["sb_q_g"][l].reshape(1, -1), (1, 2)),
        gk=jnp.tile(p["sb_k_g"][l].reshape(1, -1), (1, 2)), sgg=p["sg_v_g"][l].reshape(1, -1), sgw=p["sg_w"][l],
        sgb=jnp.pad(p["sg_b"][l].T, ((0, 0), (0, 124))))


def local_step(x, tgt, small, get_w, put_g, sync_g):
    bsz, t, _ = x.shape
    m = bsz * t
    r3 = lambda a: a.reshape(bsz, t, a.shape[-1])
    r2 = lambda a: a.reshape(m, a.shape[-1])
    xs, saved, ws = x.reshape(m, D_MODEL), [], []
    for l in range(DEPTH):
        sp, w = layer_params(small, l), {}
        w["w_in"] = get_w(l, "in", xs)
        qkv, z, ab, sb, sg, h1 = inproj_fwd(xs, sp["g1"], w["w_in"])
        odn, sall, tall = dn_fwd(r3(qkv), r3(z), r3(ab), sp["conv"], sp["alog"], sp["dtb"], sp["dng"])
        osb, ltot = sb_fwd(r3(sb), sp["gq"], sp["gk"])
        osg = sg_fwd(r3(sg), sp["sgg"], sp["sgw"], sp["sgb"])
        w["w_out"] = get_w(l, "out", osg)
        x2, mix = outproj_fwd(xs, r2(odn), r2(osb), r2(osg), w["w_out"])
        w["w_ff1"], w["w_ff2"], started = get_w(l, "ff", x2)
        x3, rlb = ffn_fwd(x2, sp["g2"] + started, w["w_ff1"], w["w_ff2"])
        saved.append(dict(rlb=rlb, h1=h1, x=xs,qkv=qkv, z=z, ab=ab, sb=sb, sg=sg, sall=sall, tall=tall, ltot=ltot, mix=mix, x2=x2))
        ws.append(w)
        xs = x3
    dx, lossp = loss_head(xs, tgt.reshape(m, D_MODEL))
    gsmall = [None] * DEPTH
    token = jnp.zeros((), f32)
    for l in reversed(range(DEPTH)):
        sp, w, s = layer_params(small, l), ws[l], saved[l]
        dx2, dg2, h2, act, df, dyb = ffn_bwd(s["x2"], sp["g2"] + token, w["w_ff1"], w["w_ff2"], s["rlb"], dx)
        g_ff1 = tn_matmul(h2, df, f"dw_ff1_{l}", col_shards=N_CHIPS)
        g_ff2 = tn_matmul(act, dyb, f"dw_ff2_{l}")
        dodn, dosb, dosg, dx2b = outproj_bwd(dx2, w["w_out"])
        g_out = tn_matmul(s["mix"], dx2b, f"dw_out_{l}")
        token = token + put_g(l, "rest", dict(w_out=g_out, w_ff1=g_ff1, w_ff2=g_ff2))
        dproj, dconv, dalog, ddtb, ddng = dn_bwd(r3(s["qkv"]), r3(s["z"]), r3(s["ab"]), sp["conv"], sp["alog"], sp["dtb"],
                                                 sp["dng"] + token, s["sall"], s["tall"], r3(dodn))
        token = sync_g(ddng)
        dproj, dgq, dgk = sb_bwd(r3(s["sb"]), sp["gq"] + token, sp["gk"], s["ltot"], r3(dosb), dproj)
        dproj, dsgg, dsgw, dsgb = sg_bwd(r3(s["sg"]), sp["sgg"], sp["sgw"], sp["sgb"], r3(dosg), dproj)
        dproj = r2(dproj)
        g_in = tn_matmul(s["h1"], dproj, f"dw_in_{l}")
        token = put_g(l, "in", dict(w_in=g_in))
        dx, dg1 = inproj_bwd(s["x"], sp["g1"] + token, w["w_in"], dproj, dx2)
        token = sync_g(dg1)
        fold = lambda a: (a[:, 0, :].sum(0).reshape(2, SB_DIM)).sum(0)
        gsmall[l] = dict(norm1_g=dg1[0], conv_w=dconv[0:DN_CONV], a_log=dalog[0, 0:DN_HEADS], dt_bias=ddtb[0, 0:DN_HEADS],
                         dn_out_g=ddng[0], sb_q_g=fold(dgq), sb_k_g=fold(dgk), sg_v_g=dsgg[0], sg_w=dsgw,
                         sg_b=dsgb[:, 0:SG_GROUPS].T, norm2_g=dg2[0])
    return lossp, dx.reshape(bsz, t, D_MODEL), gsmall


def _chip_peers(x, y):
    return [(1 - x, y), (x, 1 - y), (1 - x, 1 - y)]


_HBM = pl.BlockSpec(memory_space=pltpu.HBM)
_SEM = pl.BlockSpec(memory_space=pltpu.SEMAPHORE)
_EFFECT = pltpu.SideEffectType.DATAFLOW_SIDE_EFFECTING


def _hbm(a):
    return pltpu.with_memory_space_constraint(a, pltpu.HBM)


def _my_half(ref):
    half = ref.shape[0] // 2
    return ref.at[pl.ds(pl.multiple_of(lax.axis_index("c") * half, 8), half)]


def _exchange_copy(src, land, k, j, send, recv, scatter, halve, waiting):
    x, y, c = lax.axis_index("x"), lax.axis_index("y"), lax.axis_index("c")
    px, py = _chip_peers(x, y)[j]
    me, peer = 2 * x + y, 2 * px + py
    if scatter:
        src = src.at[me if waiting else peer]
    dst = land.at[peer if waiting else me]
    if halve:
        src, dst = _my_half(src), _my_half(dst)
    return pltpu.make_async_remote_copy(src_ref=src, dst_ref=dst, send_sem=send.at[3 * k + j],
                                        recv_sem=recv.at[3 * k + j], device_id=(px, py, c), device_id_type=MESH)


def exchange_start(items, name, scatter, after=None):
    arrs = []
    for a, _, _ in items:
        if not any(a is b for b in arrs):
            arrs.append(a)
    pos = [next(i for i, b in enumerate(arrs) if b is a) for a, _, _ in items]
    shapes = [a.shape if idx is None else a.shape[1:] for a, idx, _ in items]
    lands = [lax.empty(s if scatter else (N_CHIPS,) + s, a.dtype) for (a, _, _), s in zip(items, shapes)]
    na, nl = len(arrs), len(lands)
    n_in = na + nl + (after is not None)

    def body(*refs):
        ins, lnd = refs[:na], refs[na:na + nl]
        send, recv = refs[n_in], refs[n_in + 1]
        token = refs[-1]
        for k, (_, idx, halve) in enumerate(items):
            src = ins[pos[k]] if idx is None else ins[pos[k]].at[idx]
            for j in range(3):
                _exchange_copy(src, lnd[k], k, j, send, recv, scatter, halve, False).start()
        token[...] = jnp.zeros_like(token)

    sems = pltpu.SemaphoreType.DMA((3 * nl,))
    extra = [] if after is None else [after]
    out = pl.pallas_call(
        body, name=name,
        out_shape=(sems, sems, *[pltpu.HBM(a.shape, a.dtype) for a in arrs + lands], SDS((8, 128), f32)),
        in_specs=[_HBM] * (na + nl) + [pl.BlockSpec(memory_space=pl.ANY)] * len(extra),
        out_specs=(_SEM, _SEM, *[_HBM] * (na + nl), pl.BlockSpec(memory_space=pltpu.VMEM)),
        input_output_aliases={i: 2 + i for i in range(na + nl)},
        compiler_params=pltpu.CompilerParams(has_side_effects=_EFFECT),
    )(*[_hbm(a) for a in arrs + lands], *extra)
    thru = out[2:2 + na]
    return dict(send=out[0], recv=out[1], src=[(thru[pos[k]], idx) for k, (_, idx, _) in enumerate(items)],
                halve=[h for _, _, h in items], land=list(out[2 + na:2 + na + nl]), token=out[-1], scatter=scatter)


def exchange_wait(st, ks, after, name):
    arrs = []
    for k in ks:
        if not any(st["src"][k][0] is b for b in arrs):
            arrs.append(st["src"][k][0])
    pos = [next(i for i, b in enumerate(arrs) if b is st["src"][k][0]) for k in ks]
    lands = [st["land"][k] for k in ks]
    na, nl = len(arrs), len(lands)

    def body(*refs):
        ins, lnd = refs[:na], refs[na:na + nl]
        send, recv = refs[na + nl], refs[na + nl + 1]
        for t, k in enumerate(ks):
            idx = st["src"][k][1]
            src = ins[pos[t]] if idx is None else ins[pos[t]].at[idx]
            for j in range(3):
                cp = _exchange_copy(src, lnd[t], k, j, send, recv, st["scatter"], st["halve"][k], True)
                cp.wait_send()
                cp.wait_recv()

    out = pl.pallas_call(
        body, name=name, out_shape=tuple(pltpu.HBM(a.shape, a.dtype) for a in arrs + lands),
        in_specs=[_HBM] * (na + nl) + [_SEM, _SEM, pl.BlockSpec(memory_space=pl.ANY)], out_specs=tuple([_HBM] * (na + nl)),
        input_output_aliases={i: i for i in range(na + nl)},
        compiler_params=pltpu.CompilerParams(has_side_effects=_EFFECT),
    )(*arrs, *lands, st["send"], st["recv"], after)
    for k, (a, idx) in enumerate(st["src"]):
        for p, b in enumerate(arrs):
            if a is b:
                st["src"][k] = (out[p], idx)
    return list(out[na:na + nl])


def _sibling_copy(src, land, i, send, recv, other_half):
    x, y, c = lax.axis_index("x"), lax.axis_index("y"), lax.axis_index("c")
    return pltpu.make_async_remote_copy(src_ref=src.at[:, 1 - c] if other_half else src, dst_ref=land, send_sem=send.at[i],
                                        recv_sem=recv.at[i], device_id=(x, y, 1 - c), device_id_type=MESH)


def sibling_start(arrs, name, other_half=False):
    n = len(arrs)
    lands = [lax.empty((a.shape[0],) + a.shape[2:] if other_half else a.shape, a.dtype) for a in arrs]

    def body(*refs):
        ins, lnd = refs[:n], refs[n:2 * n]
        send, recv = refs[2 * n], refs[2 * n + 1]
        token = refs[-1]
        for i in range(n):
            _sibling_copy(ins[i], lnd[i], i, send, recv, other_half).start()
        token[...] = jnp.zeros_like(token)

    sems = pltpu.SemaphoreType.DMA((n,))
    out = pl.pallas_call(
        body, name=name,
        out_shape=(sems, sems, *[pltpu.HBM(a.shape, a.dtype) for a in arrs + lands], SDS((8, 128), f32)),
        in_specs=[_HBM] * (2 * n), out_specs=(_SEM, _SEM, *[_HBM] * (2 * n), pl.BlockSpec(memory_space=pltpu.VMEM)),
        input_output_aliases={i: 2 + i for i in range(2 * n)},
        compiler_params=pltpu.CompilerParams(has_side_effects=_EFFECT),
    )(*[_hbm(a) for a in arrs + lands])
    return dict(send=out[0], recv=out[1], src=list(out[2:2 + n]), land=list(out[2 + n:2 + 2 * n]), token=out[-1],
                other_half=other_half)


def sibling_wait(st, after, name):
    n = len(st["src"])

    def body(*refs):
        ins, lnd = refs[:n], refs[n:2 * n]
        send, recv = refs[2 * n], refs[2 * n + 1]
        for i in range(n):
            cp = _sibling_copy(ins[i], lnd[i], i, send, recv, st["other_half"])
            cp.wait_send()
            cp.wait_recv()

    out = pl.pallas_call(
        body, name=name, out_shape=tuple(pltpu.HBM(a.shape, a.dtype) for a in st["src"] + st["land"]),
        in_specs=[_HBM] * (2 * n) + [_SEM, _SEM, pl.BlockSpec(memory_space=pl.ANY)], out_specs=tuple([_HBM] * (2 * n)),
        input_output_aliases={i: i for i in range(2 * n)},
        compiler_params=pltpu.CompilerParams(has_side_effects=_EFFECT),
    )(*st["src"], *st["land"], st["send"], st["recv"], after)
    return list(out[:n]), list(out[n:])


def swap_halves(zones, name):
    n = len(zones)

    def body(*refs):
        outs = refs[n:2 * n]
        send, recv = refs[2 * n:]
        x, y, c = lax.axis_index("x"), lax.axis_index("y"), lax.axis_index("c")
        cps = []
        for i in range(n):
            for j, (px, py) in enumerate(_chip_peers(x, y)):
                part = _my_half(outs[i].at[2 * px + py])
                cps.append(pltpu.make_async_remote_copy(src_ref=part, dst_ref=part, send_sem=send.at[3 * i + j],
                                                        recv_sem=recv.at[3 * i + j], device_id=(x, y, 1 - c), device_id_type=MESH))
        for cp in cps:
            cp.start()
        for cp in cps:
            cp.wait_send()
            cp.wait_recv()

    any_spec = pl.BlockSpec(memory_space=pl.ANY)
    return pl.pallas_call(
        body, name=name, in_specs=[any_spec] * n, out_specs=[any_spec] * n, out_shape=[SDS(a.shape, a.dtype) for a in zones],
        input_output_aliases={i: i for i in range(n)},
        scratch_shapes=[pltpu.SemaphoreType.DMA((3 * n,)), pltpu.SemaphoreType.DMA((3 * n,))],
    )(*zones)


def _ids_spec(grid, in_specs, out_specs):
    return pltpu.PrefetchScalarGridSpec(num_scalar_prefetch=1, grid=grid, in_specs=in_specs, out_specs=out_specs)


def pair_sum(ids, a, b, name, tr=512):
    nd, _, rows, cols = a.shape
    tr = min(tr, rows)
    assert rows % tr == 0

    def body(ids_ref, a_ref, b_ref, o_ref):
        o_ref[...] = (a_ref[0].astype(f32) + b_ref[...].astype(f32)).astype(bf16)

    spec = pl.BlockSpec((1, tr, cols), lambda d, i, ids: (d, i, 0))
    return pl.pallas_call(
        body, name=name,
        grid_spec=_ids_spec((nd, rows // tr), [pl.BlockSpec((1, 1, tr, cols), lambda d, i, ids: (d, ids[1], i, 0)), spec], spec),
        out_shape=SDS((nd, rows, cols), bf16), compiler_params=_cp(("arbitrary", "arbitrary")))(ids, a, b)


def allreduce_small(v):
    def body(v_ref, o_ref, rbuf, send, recv):
        x, y, c = lax.axis_index("x"), lax.axis_index("y"), lax.axis_index("c")
        o_ref[...] = v_ref[...]
        for s, peer in enumerate([(x, y, 1 - c), (1 - x, y, c), (x, 1 - y, c)]):
            cp = pltpu.make_async_remote_copy(src_ref=o_ref, dst_ref=rbuf.at[s], send_sem=send.at[s], recv_sem=recv.at[s],
                                              device_id=peer, device_id_type=MESH)
            cp.start()
            cp.wait()
            o_ref[...] = o_ref[...] + rbuf[s]

    vm = pl.BlockSpec(memory_space=pltpu.VMEM)
    return pl.pallas_call(
        body, name="allreduce_small", in_specs=[vm], out_specs=vm, out_shape=SDS(v.shape, f32),
        scratch_shapes=[pltpu.VMEM((3,) + v.shape, f32), pltpu.SemaphoreType.DMA((3,)), pltpu.SemaphoreType.DMA((3,))],
        compiler_params=_cp(),
    )(v)


def sum_partials(ids, zone, mine, name, tr=256):
    _, rows, cols = zone.shape
    tr = min(tr, rows)
    assert rows % tr == 0

    def body(ids_ref, m_ref, z1_ref, z2_ref, z3_ref, o_ref):
        o_ref[...] = ((m_ref[0].astype(f32) + z1_ref[0].astype(f32)) + z2_ref[0].astype(f32)) + z3_ref[0].astype(f32)

    slot = lambda flip: pl.BlockSpec((1, tr, cols), lambda i, ids: (ids[0] ^ flip, i, 0))
    return pl.pallas_call(
        body, name=name,
        grid_spec=_ids_spec((rows // tr,), [slot(0), slot(1), slot(2), slot(3)], pl.BlockSpec((tr, cols), lambda i, ids: (i, 0))),
        out_shape=SDS((rows, cols), f32), compiler_params=_cp(("arbitrary",)),
    )(ids, mine, zone, zone, zone)


def adamw(w, m, v, gs, name, layer=0, prev=None, tr=256):
    hrows, cols = gs[0].shape
    rows = hrows * len(gs)
    tr = min(tr, hrows)
    assert hrows % tr == 0 and w.shape[0] % rows == 0
    off, nth = layer * (rows // tr), hrows // tr

    def body(w_ref, m_ref, v_ref, *rest):
        g_ref, d_ref, mo_ref, vo_ref = rest[-4:]
        if len(gs) == 1:
            g = rest[0][...]
        else:
            g = jnp.where(pl.program_id(0) // nth == lax.axis_index("c"), rest[0][...], rest[1][...])
        mn = ADAM_B1 * m_ref[...] + (1.0 - ADAM_B1) * g
        vn = ADAM_B2 * v_ref[...] + (1.0 - ADAM_B2) * jnp.square(g)
        m_hat = mn / (1.0 - ADAM_B1 ** ADAM_STEP)
        v_hat = vn / (1.0 - ADAM_B2 ** ADAM_STEP)
        g_ref[...] = g
        d_ref[...] = -ADAM_LR * (m_hat / (jnp.sqrt(v_hat) + ADAM_EPS) + ADAM_WD * w_ref[...])
        mo_ref[...] = mn
        vo_ref[...] = vn

    loc = pl.BlockSpec((tr, cols), lambda i: (i % nth, 0))
    glob = pl.BlockSpec((tr, cols), lambda i: (off + i, 0))
    extra = [] if prev is None else list(prev)
    return pl.pallas_call(
        body, name=name, grid=(rows // tr,),
        in_specs=[glob] * 3 + [loc] * len(gs) + [pl.BlockSpec(memory_space=pl.ANY)] * len(extra),
        out_specs=[glob] * 4, out_shape=[SDS(w.shape, f32)] * 4,
        input_output_aliases={3 + len(gs) + j: j for j in range(len(extra))},
        compiler_params=_cp(("arbitrary",)),
    )(w, m, v, *gs, *extra)


BIG = ("w_in", "w_out", "w_ff1", "w_ff2")
SMALL = ("norm1_g", "conv_w", "a_log", "dt_bias", "dn_out_g", "sb_q_g", "sb_k_g", "sg_v_g", "sg_w", "sg_b", "norm2_g")
WEIGHTS = ("norm1_g", "w_in", "conv_w", "a_log", "dt_bias", "dn_out_g", "sb_q_g", "sb_k_g", "sg_v_g", "sg_w", "sg_b",
           "w_out", "norm2_g", "w_ff1", "w_ff2")


PACK_ROWS = 256


def _rows_of(shape):
    n = 1
    for d in shape:
        n *= d
    return -(-n // 1024) * 8, n


def _pack(arrs):
    parts = []
    for a in arrs:
        r, n = _rows_of(a.shape)
        parts.append(jnp.pad(a.reshape(-1), (0, r * 128 - n)).reshape(r, 128))
    rows = sum(p.shape[0] for p in parts)
    parts.append(jnp.zeros((-rows % PACK_ROWS, 128), arrs[0].dtype))
    return jnp.concatenate(parts, axis=0)


def _unpack(packed, shapes):
    out, o = [], 0
    for s in shapes:
        r, n = _rows_of(s)
        out.append(packed[o:o + r].reshape(-1)[0:n].reshape(s))
        o += r
    return out


def kernel(x, norm1_g, w_in, conv_w, a_log, dt_bias, dn_out_g, sb_q_g, sb_k_g, sg_v_g, sg_w, sg_b, w_out, norm2_g, w_ff1, w_ff2, loss_target, m_norm1_g, m_w_in, m_conv_w, m_a_log, m_dt_bias, m_dn_out_g, m_sb_q_g, m_sb_k_g, m_sg_v_g, m_sg_w, m_sg_b, m_w_out, m_norm2_g, m_w_ff1, m_w_ff2, v_norm1_g, v_w_in, v_conv_w, v_a_log, v_dt_bias, v_dn_out_g, v_sb_q_g, v_sb_k_g, v_sg_v_g, v_sg_w, v_sg_b, v_w_out, v_norm2_g, v_w_ff1, v_w_ff2):
    w = dict(norm1_g=norm1_g, w_in=w_in, conv_w=conv_w, a_log=a_log, dt_bias=dt_bias, dn_out_g=dn_out_g, sb_q_g=sb_q_g,
             sb_k_g=sb_k_g, sg_v_g=sg_v_g, sg_w=sg_w, sg_b=sg_b, w_out=w_out, norm2_g=norm2_g, w_ff1=w_ff1, w_ff2=w_ff2)
    mom = dict(norm1_g=m_norm1_g, w_in=m_w_in, conv_w=m_conv_w, a_log=m_a_log, dt_bias=m_dt_bias, dn_out_g=m_dn_out_g,
               sb_q_g=m_sb_q_g, sb_k_g=m_sb_k_g, sg_v_g=m_sg_v_g, sg_w=m_sg_w, sg_b=m_sg_b, w_out=m_w_out, norm2_g=m_norm2_g,
               w_ff1=m_w_ff1, w_ff2=m_w_ff2)
    var = dict(norm1_g=v_norm1_g, w_in=v_w_in, conv_w=v_conv_w, a_log=v_a_log, dt_bias=v_dt_bias, dn_out_g=v_dn_out_g,
               sb_q_g=v_sb_q_g, sb_k_g=v_sb_k_g, sg_v_g=v_sg_v_g, sg_w=v_sg_w, sg_b=v_sg_b, w_out=v_w_out, norm2_g=v_norm2_g,
               w_ff1=v_w_ff1, w_ff2=v_w_ff2)
    chip = 2 * lax.axis_index("x") + lax.axis_index("y")

    wb = [{k: w[k][l].astype(bf16) for k in BIG} for l in range(DEPTH)]
    ags = {0: exchange_start([(conv_w, None, False)] + [(wb[0][k], None, True) for k in BIG], "allgather_start_0", scatter=False)}
    item = lambda l, k: (l, (l == 0) + BIG.index(k))

    def landed(items, after, name):
        ag, ks = ags[items[0][0]], [k for _, k in items]
        zones = exchange_wait(ag, ks, after, name)
        halved = [t for t, k in enumerate(ks) if ag["halve"][k]]
        for t, z in zip(halved, swap_halves([zones[t] for t in halved], name.replace("wait", "pass"))):
            zones[t] = z
        return [lax.dynamic_update_slice_in_dim(z, ag["src"][k][0][None], chip, axis=0) for z, k in zip(zones, ks)]

    def whole(k, z):
        if k == "w_in":
            return w_in_from_shards(z)
        return z if k == "w_ff1" else z.reshape(-1, D_MODEL)

    g_conv, first_in = landed([(0, 0), item(0, "w_in")], x, "allgather_wait_in0")
    small = {k: w[k] for k in SMALL}
    small["conv_w"] = jnp.transpose(g_conv, (1, 2, 0, 3)).reshape(DEPTH, DN_CONV, 3 * DN_WIDTH)
    cache = {}

    def get_w(l, part, after):
        if part == "in":
            return whole("w_in", first_in if l == 0 else landed([item(l, "w_in")], after, f"allgather_wait_in{l}")[0])
        if part == "out":
            zs = landed([item(l, k) for k in ("w_out", "w_ff1", "w_ff2")], after, f"allgather_wait_rest{l}")
            token = jnp.zeros((), f32)
            if l + 1 < DEPTH:
                ags[l + 1] = exchange_start([(wb[l + 1][k], None, True) for k in BIG], f"allgather_start_{l + 1}",
                                            scatter=False, after=zs[0])
                token = ags[l + 1]["token"][0, 0]
            cache[l] = (whole("w_ff1", zs[1]), whole("w_ff2", zs[2]), token)
            return whole("w_out", zs[0])
        return cache[l]

    rs, pending = {}, []
    ids = jnp.stack([chip, lax.axis_index("c")]).astype(jnp.int32)

    def put_g(l, tag, g):
        names = [k for k in BIG if k in g]
        by_dest = [w_in_grad_to_shards(g[k]) if k == "w_in" else g[k] for k in names]
        halves = [a.reshape(N_CHIPS, 2, -1, a.shape[-1]) for a in by_dest]
        st = sibling_start(halves, f"pair_swap_start_{tag}{l}", other_half=True)
        pending.append((l, tag, names, st))
        return st["token"][0, 0]

    def sync_g(after):
        token = jnp.zeros((), f32)
        while pending:
            l, tag, names, st = pending.pop(0)
            halves, got = sibling_wait(st, after, f"pair_swap_wait_{tag}{l}")
            pair = [pair_sum(ids, a, b, f"pair_sum_{k}_{l}") for k, a, b in zip(names, halves, got)]
            rs[l, tag] = dict(exchange_start([(a, None, False) for a in pair], f"scatter_start_{tag}{l}", scatter=True), names=names)
            token = token + rs[l, tag]["token"][0, 0]
        return token

    lossp, grad_x, gsmall = local_step(x, loss_target, small, get_w, put_g, sync_g)

    def sum_group(l, tag, after):
        st = rs[l, tag]
        zones = exchange_wait(st, list(range(len(st["names"]))), after, f"scatter_wait_{tag}{l}")
        sums = [sum_partials(ids, zones[i], st["src"][i][0], f"sum_{k}_{l}") for i, k in enumerate(st["names"])]
        return sibling_start(sums, f"swap_sums_start_{tag}{l}")

    def update_group(l, tag, swap, after, prev):
        sums, others = sibling_wait(swap, after, f"swap_sums_wait_{tag}{l}")
        outs = dict(prev)
        for i, k in enumerate(rs[l, tag]["names"]):
            r2 = lambda a: a.reshape(-1, a.shape[-1])
            outs[k] = adamw(r2(w[k]), r2(mom[k]), r2(var[k]), (sums[i], others[i]), f"adamw_{k}_{l}", layer=l, prev=prev.get(k))
        return outs

    swap_r = sum_group(1, "rest", rs[0, "in"]["token"])
    swap_i = sum_group(1, "in", swap_r["token"])
    done = update_group(1, "rest", swap_r, swap_i["token"], {})
    done = update_group(1, "in", swap_i, done["w_ff2"][0], done)
    res = {}

    full_shapes = [(DEPTH,) + tuple(gsmall[0][k].shape) for k in SMALL]
    packed = _pack([jnp.stack([gsmall[l][k] for l in range(DEPTH)]) for k in SMALL] + [jnp.sum(lossp).reshape(1)])
    *totals, loss = _unpack(allreduce_small(packed), full_shapes + [(1,)])
    loss = loss[0]
    gfull = dict(zip(SMALL, totals))
    cs = 3 * DN_WIDTH // N_CHIPS
    gfull["conv_w"] = lax.dynamic_slice_in_dim(gfull["conv_w"], chip * cs, cs, axis=2)
    gp, wp, mp, vp = (_pack([d[k] for k in SMALL]) for d in (gfull, w, mom, var))
    outs = adamw(wp, mp, vp, (gp,), "adamw_small")
    loc_shapes = [w[k].shape for k in SMALL]
    unp = [_unpack(o, loc_shapes) for o in outs]
    for i, k in enumerate(SMALL):
        res[k] = [unp[j][i] for j in range(4)]

    swap_r = sum_group(0, "rest", outs[0])
    swap_i = sum_group(0, "in", swap_r["token"])
    done = update_group(0, "rest", swap_r, swap_i["token"], done)
    done = update_group(0, "in", swap_i, done["w_ff2"][0], done)
    for k in BIG:
        res[k] = [o.reshape(w[k].shape) for o in done[k]]

    return (loss, grad_x, *[res[k][0] for k in WEIGHTS], *[res[k][1] for k in WEIGHTS], *[res[k][2] for k in WEIGHTS],
            *[res[k][3] for k in WEIGHTS])
```

```python
import functools

import jax
import jax.numpy as jnp
from jax import lax
from jax.experimental import pallas as pl
from jax.experimental.pallas import tpu as pltpu

f32 = jnp.float32
bf16 = jnp.bfloat16
SDS = jax.ShapeDtypeStruct
MESH = pl.DeviceIdType.MESH

NORM_EPS = 1e-6
D_MODEL = 1024
DEPTH = 2
DN_HEADS, DN_DIM, DN_WIDTH, DN_CONV, DN_CHUNK = 4, 128, 512, 4, 64
SB_HEADS, SB_DIM, SB_WIDTH = 4, 64, 256
SG_GROUPS, SG_DIM, SG_WIDTH, SG_CHUNK = 4, 64, 256, 128
D_FF = 4096
IN_DIM = 3336
C_QKV, C_Z, C_AB, C_SB, C_SG, IN_PAD = 0, 1536, 2048, 2304, 3072, 3584
DN_COLS = C_SB
N_CHIPS = 4

ADAM_LR, ADAM_B1, ADAM_B2, ADAM_EPS, ADAM_WD, ADAM_STEP = 0.001, 0.9, 0.999, 1e-08, 0.01, 10

VMEM_LIMIT = 56 * 1024 * 1024


def _cp(sem=None, **kw):
    if sem is not None:
        kw["dimension_semantics"] = sem
    return pltpu.CompilerParams(vmem_limit_bytes=VMEM_LIMIT, **kw)


def _split2(x):
    hi = x.astype(bf16)
    lo = (x - hi.astype(f32)).astype(bf16)
    return hi, lo


NT = (((1,), (1,)), ((), ()))
TN = (((0,), (0,)), ((), ()))
_DIMS2 = dict(nn=(((1,), (0,)), ((), ())), nt=NT, tn=TN)
_DIMS3 = dict(nn=(((2,), (1,)), ((0,), (0,))), nt=(((2,), (2,)), ((0,), (0,))), tn=(((1,), (1,)), ((0,), (0,))))


def _dg(a, b, kind):
    return lax.dot_general(a, b, (_DIMS2 if a.ndim == 2 else _DIMS3)[kind], preferred_element_type=f32)


def _pdot(a, b):
    return _dg(a, b, "nn")


def _dot_hp(a, b):
    ah, al = _split2(a)
    bh, bl = _split2(b)
    return _pdot(ah, bh) + _pdot(ah, bl) + _pdot(al, bh)


def _dot_x2c(a, m):
    lead = a.shape[:-1]
    ah, al = _split2(a.reshape(-1, a.shape[-1]))
    return (_pdot(ah, m) + _pdot(al, m)).reshape(lead + (m.shape[1],))


def _dot_cx2(m, a):
    if a.ndim == 3:
        m = jnp.broadcast_to(m, (a.shape[0],) + m.shape)
    ah, al = _split2(a)
    return _pdot(m, ah) + _pdot(m, al)


def _nt(a, b):
    return _dg(a.astype(bf16), b.astype(bf16), "nt")


def _tn(a, b):
    return _dg(a.astype(bf16), b.astype(bf16), "tn")


def _nn(a, b):
    return _dg(a.astype(bf16), b.astype(bf16), "nn")


@jax.custom_vjp
def mm(a, b):
    return _nn(a, b)


mm.defvjp(lambda a, b: (_nn(a, b), (a, b)), lambda r, g: (_nt(g, r[1]), _tn(r[0], g)))


@jax.custom_vjp
def mm_nt(a, b):
    return _nt(a, b)


mm_nt.defvjp(lambda a, b: (_nt(a, b), (a, b)), lambda r, g: (_nn(g, r[1]), _tn(g, r[0])))


@jax.custom_vjp
def mm_tn(a, b):
    return _tn(a, b)


mm_tn.defvjp(lambda a, b: (_tn(a, b), (a, b)), lambda r, g: (_nt(r[1], g), _nn(r[0], g)))


@jax.custom_vjp
def rmul_const(a, m, mt):
    return _dot_x2c(a, m)


rmul_const.defvjp(lambda a, m, mt: (_dot_x2c(a, m), (m, mt)),
                  lambda r, g: (_dot_x2c(g, r[1]), jnp.zeros_like(r[0]), jnp.zeros_like(r[1])))


@jax.custom_vjp
def lmul_const(m, mt, a):
    return _dot_cx2(m, a)


lmul_const.defvjp(lambda m, mt, a: (_dot_cx2(m, a), (m, mt)),
                  lambda r, g: (jnp.zeros_like(r[0]), jnp.zeros_like(r[1]), _dot_cx2(r[1], g)))


@jax.custom_vjp
def mm_hl(t, x):
    th, tl = _split2(t)
    xb = x.astype(bf16)
    return _pdot(th, xb) + _pdot(tl, xb)


def _mm_hl_bwd(r, g):
    t, x = r
    th, tl = _split2(t)
    gb = g.astype(bf16)
    return _nt(g, x), _dg(th, gb, "tn") + _dg(tl, gb, "tn")


mm_hl.defvjp(lambda t, x: (mm_hl(t, x), (t, x)), _mm_hl_bwd)


def inv_unit_lower(lm):
    c = lm.shape[-1]
    r, cc = _iota2((c, c))
    eye = (r == cc).astype(f32)
    t = eye - lm
    p = -lm
    k = 1
    while 2 * k < c:
        p = _nn(p, p)
        t = t + _nn(t, p)
        k *= 2
    res = eye - t - _dot_hp(lm, t)
    return t + _nn(t, res)


@jax.custom_vjp
def inv_given(lm, t):
    return t


inv_given.defvjp(lambda lm, t: (t, t), lambda t, g: (-_nt(_tn(t, g), t), jnp.zeros_like(t)))


def _sigmoid(x):
    return 1.0 / (1.0 + jnp.exp(-x))


def _softplus(x):
    return jnp.maximum(x, 0.0) + jnp.log(1.0 + jnp.exp(-jnp.abs(x)))


def _silu(x):
    return x * _sigmoid(x)


def _gelu(x):
    return 0.5 * x * (1.0 + jnp.tanh(0.7978845608028654 * (x + 0.044715 * (x * x * x))))


def _iota2(shape):
    return lax.broadcasted_iota(jnp.int32, shape, 0), lax.broadcasted_iota(jnp.int32, shape, 1)


def _group_avg_mats():
    r, c = _iota2((128, 128))
    return jnp.where((r // 64) == (c // 64), 1.0 / 64.0, 0.0).astype(bf16)


def _pair_norm(x, gain, bavg):
    ms = rmul_const(x * x, bavg, bavg)
    return x * lax.rsqrt(ms + NORM_EPS) * gain


def _rms(x):
    r = lax.rsqrt(jnp.mean(x * x, axis=-1, keepdims=True) + NORM_EPS)
    return r


_IN_GROUPS = ((C_QKV, C_Z), (C_Z, C_AB), (C_AB, C_AB + 128), (C_SB, C_SG), (C_SG, IN_PAD))


def inproj_fwd(x, g, wp, tm=256):
    m = x.shape[0]

    def body(x_ref, g_ref, w_ref, *outs):
        xv = x_ref[...]
        h = (xv * _rms(xv) * g_ref[...]).astype(bf16)
        outs[-1][...] = h
        for (a, b), o in zip(_IN_GROUPS, outs):
            o[...] = _pdot(h, w_ref[:, a:b])

    widths = [b - a for a, b in _IN_GROUPS]
    return pl.pallas_call(
        body, name="inproj_fwd", grid=(m // tm,),
        in_specs=[pl.BlockSpec((tm, D_MODEL), lambda i: (i, 0)), pl.BlockSpec((1, D_MODEL), lambda i: (0, 0)),
                  pl.BlockSpec((D_MODEL, IN_PAD), lambda i: (0, 0))],
        out_specs=[pl.BlockSpec((tm, wd), lambda i: (i, 0)) for wd in widths + [D_MODEL]],
        out_shape=[SDS((m, wd), f32) for wd in widths] + [SDS((m, D_MODEL), bf16)],
        compiler_params=_cp(("arbitrary",)),
    )(x, g, wp)


def inproj_bwd(x, g, wp, dproj, dres, tm=256):
    m = x.shape[0]

    def body(x_ref, g_ref, w_ref, dp_ref, dr_ref, dx_ref, dg_ref):
        xv = x_ref[...]
        r = _rms(xv)
        xn = xv * r
        gv = g_ref[...]
        dh = lax.dot_general(dp_ref[...], w_ref[...], NT, preferred_element_type=f32)
        dxn = dh * gv
        dx_ref[...] = dr_ref[...] + r * (dxn - xn * jnp.mean(dxn * xn, axis=-1, keepdims=True))

        @pl.when(pl.program_id(0) == 0)
        def _():
            dg_ref[...] = jnp.zeros_like(dg_ref)

        dg_ref[...] += jnp.sum(dh * xn, axis=0, keepdims=True)

    return pl.pallas_call(
        body, name="inproj_bwd", grid=(m // tm,),
        in_specs=[pl.BlockSpec((tm, D_MODEL), lambda i: (i, 0)), pl.BlockSpec((1, D_MODEL), lambda i: (0, 0)),
                  pl.BlockSpec((D_MODEL, IN_PAD), lambda i: (0, 0)), pl.BlockSpec((tm, IN_PAD), lambda i: (i, 0)),
                  pl.BlockSpec((tm, D_MODEL), lambda i: (i, 0))],
        out_specs=[pl.BlockSpec((tm, D_MODEL), lambda i: (i, 0)), pl.BlockSpec((1, D_MODEL), lambda i: (0, 0))],
        out_shape=[SDS((m, D_MODEL), f32), SDS((1, D_MODEL), f32)],
        compiler_params=_cp(("arbitrary",)),
    )(x, g, wp, dproj, dres)


def outproj_fwd(x, odn, osb, osg, wo, tm=512):
    m = x.shape[0]

    def body(x_ref, a_ref, b_ref, c_ref, w_ref, x2_ref, mix_ref):
        mix_ref[:, 0:DN_WIDTH] = a_ref[...].astype(bf16)
        mix_ref[:, DN_WIDTH:DN_WIDTH + SB_WIDTH] = b_ref[...].astype(bf16)
        mix_ref[:, DN_WIDTH + SB_WIDTH:D_MODEL] = c_ref[...].astype(bf16)
        x2_ref[...] = x_ref[...] + _pdot(mix_ref[...], w_ref[...])

    row = lambda w: pl.BlockSpec((tm, w), lambda i: (i, 0))
    return pl.pallas_call(
        body, name="outproj_fwd", grid=(m // tm,),
        in_specs=[row(D_MODEL), row(DN_WIDTH), row(SB_WIDTH), row(SG_WIDTH), pl.BlockSpec((D_MODEL, D_MODEL), lambda i: (0, 0))],
        out_specs=[row(D_MODEL), row(D_MODEL)],
        out_shape=[SDS((m, D_MODEL), f32), SDS((m, D_MODEL), bf16)],
        compiler_params=_cp(("arbitrary",)),
    )(x, odn, osb, osg, wo)


def outproj_bwd(dx2, wo, tm=512):
    m = dx2.shape[0]

    def body(d_ref, w_ref, a_ref, b_ref, c_ref, db_ref):
        db = d_ref[...].astype(bf16)
        db_ref[...] = db
        dm = lax.dot_general(db, w_ref[...], NT, preferred_element_type=f32)
        a_ref[...] = dm[:, 0:DN_WIDTH]
        b_ref[...] = dm[:, DN_WIDTH:DN_WIDTH + SB_WIDTH]
        c_ref[...] = dm[:, DN_WIDTH + SB_WIDTH:D_MODEL]

    row = lambda w: pl.BlockSpec((tm, w), lambda i: (i, 0))
    return pl.pallas_call(
        body, name="outproj_bwd", grid=(m // tm,),
        in_specs=[row(D_MODEL), pl.BlockSpec((D_MODEL, D_MODEL), lambda i: (0, 0))],
        out_specs=[row(DN_WIDTH), row(SB_WIDTH), row(SG_WIDTH), row(D_MODEL)],
        out_shape=[SDS((m, DN_WIDTH), f32), SDS((m, SB_WIDTH), f32), SDS((m, SG_WIDTH), f32), SDS((m, D_MODEL), bf16)],
        compiler_params=_cp(("arbitrary",)),
    )(dx2, wo)


FF_CHUNK = D_FF // N_CHIPS


def _load_weights_once(pairs, sem):
    @pl.when(pl.program_id(0) == 0)
    def _():
        cps = [pltpu.make_async_copy(h, v, sem.at[i]) for i, (h, v) in enumerate(pairs)]
        for c in cps:
            c.start()
        for c in cps:
            c.wait()


def ffn_fwd(x2, g, w1, w2, tm=256):
    m = x2.shape[0]

    def body(x_ref, g_ref, w1_hbm, w2_hbm, y_ref, rl_ref, w1_v, w2_v, sem):
        _load_weights_once(((w1_hbm, w1_v), (w2_hbm, w2_v)), sem)
        xv = x_ref[...]
        h = (xv * _rms(xv) * g_ref[...]).astype(bf16)
        acc = xv
        for j in range(0, D_FF, FF_CHUNK):
            f = _pdot(h, w1_v[j // FF_CHUNK])
            rl = jnp.maximum(f, 0.0)
            rl_ref[:, j:j + FF_CHUNK] = rl.astype(bf16)
            acc = acc + _pdot((rl * rl).astype(bf16), w2_v[j:j + FF_CHUNK, :])
        y_ref[...] = acc

    return pl.pallas_call(
        body, name="ffn_fwd", grid=(m // tm,),
        in_specs=[pl.BlockSpec((tm, D_MODEL), lambda i: (i, 0)), pl.BlockSpec((1, D_MODEL), lambda i: (0, 0)),
                  pl.BlockSpec(memory_space=pl.ANY), pl.BlockSpec(memory_space=pl.ANY)],
        out_specs=[pl.BlockSpec((tm, D_MODEL), lambda i: (i, 0)), pl.BlockSpec((tm, D_FF), lambda i: (i, 0))],
        out_shape=[SDS((m, D_MODEL), f32), SDS((m, D_FF), bf16)],
        scratch_shapes=[pltpu.VMEM((N_CHIPS, D_MODEL, FF_CHUNK), bf16), pltpu.VMEM((D_FF, D_MODEL), bf16), pltpu.SemaphoreType.DMA((2,))],
        compiler_params=_cp(("arbitrary",)),
    )(x2, g, w1, w2)


def ffn_bwd(x2, g, w1, w2, rlb, dy, tm=256):
    m = x2.shape[0]

    def body(x_ref, g_ref, w1_hbm, w2_hbm, rl_ref, dy_ref, dx_ref, dg_ref, h_ref, a_ref, df_ref, dyb_ref, w1_v, w2_v, sem):
        _load_weights_once(((w1_hbm, w1_v), (w2_hbm, w2_v)), sem)
        xv = x_ref[...]
        r = _rms(xv)
        xn = xv * r
        gv = g_ref[...]
        h = (xn * gv).astype(bf16)
        h_ref[...] = h
        dyv = dy_ref[...]
        dyb = dyv.astype(bf16)
        dyb_ref[...] = dyb
        dh = jnp.zeros((tm, D_MODEL), f32)
        for j in range(0, D_FF, FF_CHUNK):
            rl = rl_ref[:, j:j + FF_CHUNK].astype(f32)
            a_ref[:, j:j + FF_CHUNK] = (rl * rl).astype(bf16)
            da = lax.dot_general(dyb, w2_v[j:j + FF_CHUNK, :], NT, preferred_element_type=f32)
            df = (da * (2.0 * rl)).astype(bf16)
            df_ref[:, j:j + FF_CHUNK] = df
            dh = dh + lax.dot_general(df, w1_v[j // FF_CHUNK], NT, preferred_element_type=f32)
        dxn = dh * gv
        dx_ref[...] = dyv + r * (dxn - xn * jnp.mean(dxn * xn, axis=-1, keepdims=True))

        @pl.when(pl.program_id(0) == 0)
        def _():
            dg_ref[...] = jnp.zeros_like(dg_ref)

        dg_ref[...] += jnp.sum(dh * xn, axis=0, keepdims=True)

    row = lambda w: pl.BlockSpec((tm, w), lambda i: (i, 0))
    return pl.pallas_call(
        body, name="ffn_bwd", grid=(m // tm,),
        in_specs=[row(D_MODEL), pl.BlockSpec((1, D_MODEL), lambda i: (0, 0)),
                  pl.BlockSpec(memory_space=pl.ANY), pl.BlockSpec(memory_space=pl.ANY), row(D_FF), row(D_MODEL)],
        out_specs=[row(D_MODEL), pl.BlockSpec((1, D_MODEL), lambda i: (0, 0)), row(D_MODEL), row(D_FF), row(D_FF), row(D_MODEL)],
        out_shape=[SDS((m, D_MODEL), f32), SDS((1, D_MODEL), f32), SDS((m, D_MODEL), bf16), SDS((m, D_FF), bf16),
                   SDS((m, D_FF), bf16), SDS((m, D_MODEL), bf16)],
        scratch_shapes=[pltpu.VMEM((N_CHIPS, D_MODEL, FF_CHUNK), bf16), pltpu.VMEM((D_FF, D_MODEL), bf16), pltpu.SemaphoreType.DMA((2,))],
        compiler_params=_cp(("arbitrary",)),
    )(x2, g, w1, w2, rlb, dy)


def _tile(n, cap):
    best = 128
    for t in range(128, cap + 1, 128):
        if n % t == 0:
            best = t
    return best


def tn_matmul(a, b, name, col_shards=1, tk=2048):
    m, ka = a.shape
    n = b.shape[1]
    ti = _tile(ka, 1024)
    tj = _tile(n // col_shards, 1152)
    tk = min(tk, m)
    nk = m // tk
    jps = (n // col_shards) // tj

    def body(a_ref, b_ref, o_ref, acc):
        k = pl.program_id(2)

        @pl.when(k == 0)
        def _():
            acc[...] = jnp.zeros_like(acc)

        acc[...] += lax.dot_general(a_ref[...], b_ref[...], TN, preferred_element_type=f32)

        @pl.when(k == nk - 1)
        def _():
            o_ref[...] = acc[...].astype(bf16).reshape(o_ref.shape)

    if col_shards == 1:
        out_shape, out_spec = SDS((ka, n), bf16), pl.BlockSpec((ti, tj), lambda i, j, k: (i, j))
    else:
        out_shape = SDS((col_shards, ka, n // col_shards), bf16)
        out_spec = pl.BlockSpec((1, ti, tj), lambda i, j, k: (j // jps, i, j % jps))
    return pl.pallas_call(
        body, name=name, grid=(ka // ti, n // tj, nk),
        in_specs=[pl.BlockSpec((tk, ti), lambda i, j, k: (k, i)), pl.BlockSpec((tk, tj), lambda i, j, k: (k, j))],
        out_specs=out_spec, out_shape=out_shape,
        scratch_shapes=[pltpu.VMEM((ti, tj), f32)],
        compiler_params=_cp(("arbitrary", "arbitrary", "arbitrary")),
    )(a, b)


def loss_head(y, tgt, tm=512):
    m = y.shape[0]

    def body(y_ref, t_ref, dy_ref, l_ref):
        e = y_ref[...] - t_ref[...]
        dy_ref[...] = e * (1.0 / D_MODEL)

        @pl.when(pl.program_id(0) == 0)
        def _():
            l_ref[...] = jnp.zeros_like(l_ref)

        l_ref[...] += jnp.sum(e * e, axis=0, keepdims=True) * (0.5 / D_MODEL)

    row = pl.BlockSpec((tm, D_MODEL), lambda i: (i, 0))
    return pl.pallas_call(
        body, name="loss_head", grid=(m // tm,), in_specs=[row, row],
        out_specs=[row, pl.BlockSpec((1, D_MODEL), lambda i: (0, 0))],
        out_shape=[SDS((m, D_MODEL), f32), SDS((1, D_MODEL), f32)],
        compiler_params=_cp(("arbitrary",)),
    )(y, tgt)


def _dn_consts():
    c = DN_CHUNK
    r, cc = _iota2((c, c))
    lt = (cc <= r).astype(bf16)
    ltt = (r <= cc).astype(bf16)
    return lt, ltt


def dn_chunk(cq, ck, cv, g, beta, z, s, gain, lt, ltt, t_given=None):
    c = DN_CHUNK
    r, cc = _iota2((c, c))
    q = cq * lax.rsqrt(jnp.sum(cq * cq, axis=-1, keepdims=True) + NORM_EPS) * (DN_DIM ** -0.5)
    k = ck * lax.rsqrt(jnp.sum(ck * ck, axis=-1, keepdims=True) + NORM_EPS)
    r2, c2 = _iota2((c, 128))
    uaug = jnp.where((c2 < c) & (r2 > c2), 1.0, 0.0) + jnp.where(c2 == c, 1.0, 0.0)
    gam_all = lmul_const(lt, ltt, g * uaug)
    gam_cc = gam_all[:, :, 0:c]
    gam = gam_all[:, :, c:c + 1]
    dec = jnp.where(cc <= r, jnp.exp(jnp.where(cc <= r, gam_cc, 0.0)), 0.0)
    kk = mm_nt(k, k)
    lm = jnp.where(cc < r, beta * kk * dec, 0.0)
    t = inv_unit_lower(lm) if t_given is None else inv_given(lm, t_given)
    eg = jnp.exp(gam)
    sol = mm_hl(t, jnp.concatenate([cv * beta, k * (beta * eg)], axis=2))
    u, w = sol[:, :, 0:DN_DIM], sol[:, :, DN_DIM:2 * DN_DIM]
    qk = jnp.where(cc <= r, mm_nt(q, k) * dec, 0.0)
    glast = jnp.sum(g, axis=1, keepdims=True)
    qd = q * eg
    kd = k * jnp.exp(glast - gam)
    un = u - mm(w, s)
    o = mm(qd, s) + mm(qk, un)
    s_new = s * jnp.exp(glast) + mm_tn(kd, un)
    on = o * lax.rsqrt(jnp.mean(o * o, axis=-1, keepdims=True) + NORM_EPS) * gain * _silu(z)
    return on, s_new, t


def _dn_gates(ab, al_row, dt_row):
    pre = ab + dt_row
    return -jnp.exp(al_row) * _softplus(pre), _sigmoid(ab), _sigmoid(pre)


def _dn_chains(cacts, gates, z_ref):
    cq, ck, cv, g, beta, z = [], [], [], [], [], []
    for bi, cact in enumerate(cacts):
        for h in range(DN_HEADS):
            cq.append(cact[:, h * DN_DIM:(h + 1) * DN_DIM])
            ck.append(cact[:, DN_WIDTH + h * DN_DIM:DN_WIDTH + (h + 1) * DN_DIM])
            cv.append(cact[:, 2 * DN_WIDTH + h * DN_DIM:2 * DN_WIDTH + (h + 1) * DN_DIM])
            g.append(gates[bi][0][:, h:h + 1])
            beta.append(gates[bi][1][:, DN_HEADS + h:DN_HEADS + h + 1])
            z.append(z_ref[bi, :, h * DN_DIM:(h + 1) * DN_DIM])
    return tuple(jnp.stack(v) for v in (cq, ck, cv, g, beta, z))


def _conv_rows(xe_ref, b, w_ref):
    y = w_ref[0:1, :] * xe_ref[b, pl.ds(5, DN_CHUNK), :]
    for i in range(1, DN_CONV):
        y = y + w_ref[i:i + 1, :] * xe_ref[b, pl.ds(5 + i, DN_CHUNK), :]
    return y


def dn_fwd(qkv, z, ab, conv_w, alog, dtb, gain):
    bsz, t, _ = qkv.shape
    nc = t // DN_CHUNK
    c = DN_CHUNK
    nh = bsz * DN_HEADS

    def body(qkv_ref, z_ref, ab_ref, w_ref, al_ref, dt_ref, g_ref, o_ref, sall_ref, tall_ref, xe, s_sc):
        n = pl.program_id(0)

        @pl.when(n == 0)
        def _():
            xe[:, 0:8, :] = jnp.zeros((bsz, 8, 3 * DN_WIDTH), f32)
            s_sc[...] = jnp.zeros_like(s_sc)

        lt, ltt = _dn_consts()
        cacts = []
        for b in range(bsz):
            xe[b, 8:8 + c, :] = qkv_ref[b]
            cacts.append(_silu(_conv_rows(xe, b, w_ref)))
            xe[b, 0:8, :] = xe[b, c:c + 8, :]
        gates = [_dn_gates(ab_ref[b], al_ref[...], dt_ref[...]) for b in range(bsz)]
        s = s_sc[...]
        sall_ref[0] = s
        on, sn, tt = dn_chunk(*_dn_chains(cacts, gates, z_ref), s, g_ref[...], lt, ltt)
        tall_ref[0] = tt
        s_sc[...] = sn
        for b in range(bsz):
            for h in range(DN_HEADS):
                o_ref[b, :, h * DN_DIM:(h + 1) * DN_DIM] = on[b * DN_HEADS + h]

    blk = lambda w: pl.BlockSpec((bsz, c, w), lambda n: (0, n, 0))
    full = lambda shp: pl.BlockSpec(shp, lambda n: (0,) * len(shp))
    return pl.pallas_call(
        body, name="dn_fwd", grid=(nc,),
        in_specs=[blk(3 * DN_WIDTH), blk(DN_WIDTH), blk(128), full((8, 3 * DN_WIDTH)), full((1, 128)), full((1, 128)), full((1, 128))],
        out_specs=[blk(DN_WIDTH), pl.BlockSpec((1, nh, DN_DIM, DN_DIM), lambda n: (n, 0, 0, 0)),
                   pl.BlockSpec((1, nh, c, c), lambda n: (n, 0, 0, 0))],
        out_shape=[SDS((bsz, t, DN_WIDTH), f32), SDS((nc, nh, DN_DIM, DN_DIM), f32), SDS((nc, nh, c, c), f32)],
        scratch_shapes=[pltpu.VMEM((bsz, c + 8, 3 * DN_WIDTH), f32), pltpu.VMEM((nh, DN_DIM, DN_DIM), f32)],
        compiler_params=_cp(("arbitrary",)),
    )(qkv, z, ab, conv_w, alog, dtb, gain)


def dn_bwd(qkv, z, ab, conv_w, alog, dtb, gain, sall, tall, do):
    bsz, t, _ = qkv.shape
    nc = t // DN_CHUNK
    c = DN_CHUNK
    nh = bsz * DN_HEADS
    w3 = 3 * DN_WIDTH

    def body(qkv_ref, prev_ref, z_ref, ab_ref, w_ref, al_ref, dt_ref, g_ref, sall_ref, tall_ref, do_ref,
             dp_ref, dw_ref, dal_ref, ddt_ref, dg_ref, xe, dye, dc_sc, ds_sc):
        n = pl.program_id(0)
        first = (nc - 1 - n) == 0

        @pl.when(n == 0)
        def _():
            dye[:, c:c + 8, :] = jnp.zeros((bsz, 8, w3), f32)
            ds_sc[...] = jnp.zeros_like(ds_sc)
            dw_ref[...] = jnp.zeros_like(dw_ref)
            dal_ref[...] = jnp.zeros_like(dal_ref)
            ddt_ref[...] = jnp.zeros_like(ddt_ref)
            dg_ref[...] = jnp.zeros_like(dg_ref)

        lt, ltt = _dn_consts()
        lane_c = lax.broadcasted_iota(jnp.int32, (c, 128), 1)
        ys, sigs = [], []
        for b in range(bsz):
            xe[b, 0:8, :] = jnp.where(first, 0.0, prev_ref[b])
            xe[b, 8:8 + c, :] = qkv_ref[b]
            ys.append(_conv_rows(xe, b, w_ref))
            sigs.append(_sigmoid(ys[b]))
        gates = [_dn_gates(ab_ref[b], al_ref[...], dt_ref[...]) for b in range(bsz)]
        ops = _dn_chains([y * sg for y, sg in zip(ys, sigs)], gates, z_ref)
        tt = tall_ref[0]
        _, vjp = jax.vjp(lambda *p: dn_chunk(*p, lt, ltt, t_given=tt)[0:2], *ops, sall_ref[0], g_ref[...])
        don = jnp.stack([do_ref[b, :, h * DN_DIM:(h + 1) * DN_DIM] for b in range(bsz) for h in range(DN_HEADS)])
        dcq, dck, dcv, dg, dbeta, dzz, dsp, dgn = vjp((don, ds_sc[...]))
        ds_sc[...] = dsp
        dg_ref[...] += dgn
        for b in range(bsz):
            dgate = jnp.zeros((c, 128), f32)
            for h in range(DN_HEADS):
                i = b * DN_HEADS + h
                dc_sc[b, :, h * DN_DIM:(h + 1) * DN_DIM] = dcq[i]
                dc_sc[b, :, DN_WIDTH + h * DN_DIM:DN_WIDTH + (h + 1) * DN_DIM] = dck[i]
                dc_sc[b, :, 2 * DN_WIDTH + h * DN_DIM:2 * DN_WIDTH + (h + 1) * DN_DIM] = dcv[i]
                dp_ref[b, :, C_Z + h * DN_DIM:C_Z + (h + 1) * DN_DIM] = dzz[i].astype(bf16)
                dgate = dgate + jnp.where(lane_c == h, dg[i], 0.0) + jnp.where(lane_c == DN_HEADS + h, dbeta[i], 0.0)
            gg, beta, sig_pre = gates[b]
            is_g = lane_c < DN_HEADS
            dpre = jnp.where(is_g, dgate * (-jnp.exp(al_ref[...])) * sig_pre, 0.0)
            dp_ref[b, :, C_AB:C_AB + 128] = (dpre + jnp.where(is_g, 0.0, dgate * beta * (1.0 - beta))).astype(bf16)
            dp_ref[b, :, C_AB + 128:DN_COLS] = jnp.zeros((c, DN_COLS - C_AB - 128), bf16)
            dal_ref[...] += jnp.sum(jnp.where(is_g, dgate * gg, 0.0), axis=0, keepdims=True)
            ddt_ref[...] += jnp.sum(dpre, axis=0, keepdims=True)
            y, sig = ys[b], sigs[b]
            dy = dc_sc[b] * (sig * (1.0 + y * (1.0 - sig)))
            dye[b, 0:c, :] = dy
            dx = w_ref[3:4, :] * dy
            for i in range(DN_CONV - 1):
                dx = dx + w_ref[i:i + 1, :] * dye[b, pl.ds(3 - i, c), :]
            dp_ref[b, :, 0:w3] = dx.astype(bf16)
            for i in range(DN_CONV):
                dw_ref[i:i + 1, :] += jnp.sum(dy * xe[b, pl.ds(5 + i, c), :], axis=0, keepdims=True)
            dye[b, c:c + 8, :] = dye[b, 0:8, :]

    rev = lambda w: pl.BlockSpec((bsz, c, w), lambda n: (0, nc - 1 - n, 0))
    full = lambda shp: pl.BlockSpec(shp, lambda n: (0,) * len(shp))
    prev = pl.BlockSpec((bsz, 8, w3), lambda n: (0, jnp.maximum((nc - 1 - n) * (c // 8) - 1, 0), 0))
    return pl.pallas_call(
        body, name="dn_bwd", grid=(nc,),
        in_specs=[rev(w3), prev, rev(DN_WIDTH), rev(128), full((8, w3)), full((1, 128)), full((1, 128)), full((1, 128)),
                  pl.BlockSpec((1, nh, DN_DIM, DN_DIM), lambda n: (nc - 1 - n, 0, 0, 0)),
                  pl.BlockSpec((1, nh, c, c), lambda n: (nc - 1 - n, 0, 0, 0)), rev(DN_WIDTH)],
        out_specs=[rev(DN_COLS), full((8, w3)), full((1, 128)), full((1, 128)), full((1, 128))],
        out_shape=[SDS((bsz, t, IN_PAD), bf16), SDS((8, w3), f32), SDS((1, 128), f32), SDS((1, 128), f32), SDS((1, 128), f32)],
        scratch_shapes=[pltpu.VMEM((bsz, c + 8, w3), f32), pltpu.VMEM((bsz, c + 8, w3), f32), pltpu.VMEM((bsz, c, w3), f32),
                        pltpu.VMEM((nh, DN_DIM, DN_DIM), f32)],
        compiler_params=_cp(("arbitrary",)),
    )(qkv, qkv, z, ab, conv_w, alog, dtb, gain, sall, tall, do)


SB_TILE = 256
SB_QTILE, SB_KTILE = 256, 256
SB_PAIRS = SB_HEADS // 2


def sb_fwd(sbqkv, gq, gk):
    bsz, t, _ = sbqkv.shape
    bq = min(SB_QTILE, t)
    blk = max(min(SB_KTILE, t), bq)
    nq = t // bq
    scale = SB_DIM ** -0.5

    def body(q_ref, k_ref, v_ref, gq_ref, gk_ref, o_ref, l_ref, q2_sc, kn_sc, v_sc):
        bavg = _group_avg_mats()
        lane = lax.broadcasted_iota(jnp.int32, (1, 128), 1)
        first = lane < SB_DIM
        for p in range(SB_PAIRS):
            ls = slice(p * 128, (p + 1) * 128)
            qn = _pair_norm(q_ref[0, :, ls], gq_ref[...], bavg)
            kn_sc[p] = _pair_norm(k_ref[0, :, ls], gk_ref[...], bavg).astype(bf16)
            v_sc[p] = v_ref[0, :, ls].astype(bf16)
            q2_sc[2 * p] = jnp.where(first, qn, 0.0).astype(bf16)
            q2_sc[2 * p + 1] = jnp.where(first, 0.0, qn).astype(bf16)
        r, c = _iota2((blk, blk))
        ustrict = (r > c).astype(bf16)
        r2, c2 = _iota2((2 * bq, blk))

        def tile(q2s, ks, carry, causal):
            out = []
            for p in range(SB_PAIRS):
                acc, rr = carry[2 * p], carry[2 * p + 1]
                zz = lax.dot_general(q2s[p], kn_sc[p, pl.ds(ks, blk), :], NT, preferred_element_type=f32) * scale
                sp = _softplus(zz)
                lm = -sp if causal is None else jnp.where(causal, -sp, 0.0)
                rem = _dot_x2c(lm, ustrict)
                wgt = jnp.exp(zz - sp + rem + rr)
                if causal is not None:
                    wgt = jnp.where(causal, wgt, 0.0)
                out += [acc + _pdot(wgt.astype(bf16), v_sc[p, pl.ds(ks, blk), :]), rr + jnp.sum(lm, axis=1, keepdims=True)]
            return tuple(out)

        def qloop(qi, _):
            qs = pl.multiple_of(qi * bq, bq)
            kd = qs // blk
            causal = c2 < (r2 & (bq - 1)) + (qs - kd * blk)
            q2s = [jnp.concatenate([q2_sc[2 * p, pl.ds(qs, bq), :], q2_sc[2 * p + 1, pl.ds(qs, bq), :]], axis=0)
                   for p in range(SB_PAIRS)]
            zero = (jnp.zeros((2 * bq, 128), f32), jnp.zeros((2 * bq, 1), f32)) * SB_PAIRS
            carry = lax.fori_loop(1, kd + 1, lambda i, cr: tile(q2s, pl.multiple_of((kd - i) * blk, blk), cr, None),
                                  tile(q2s, pl.multiple_of(kd * blk, blk), zero, causal))
            for p in range(SB_PAIRS):
                acc, rr = carry[2 * p], carry[2 * p + 1]
                o_ref[0, pl.ds(qs, bq), p * 128:(p + 1) * 128] = jnp.where(first, acc[0:bq], acc[bq:2 * bq])
                l_ref[0, pl.ds(qs, bq), p * 128:(p + 1) * 128] = jnp.where(first, rr[0:bq], rr[bq:2 * bq])
            return 0

        lax.fori_loop(0, nq, qloop, 0)

    col = lambda off: pl.BlockSpec((1, t, SB_WIDTH), lambda b: (b, 0, off))
    gsp = pl.BlockSpec((1, 128), lambda b: (0, 0))
    return pl.pallas_call(
        body, name="sb_fwd", grid=(bsz,),
        in_specs=[col(0), col(1), col(2), gsp, gsp],
        out_specs=[col(0), col(0)],
        out_shape=[SDS((bsz, t, SB_WIDTH), f32), SDS((bsz, t, SB_WIDTH), f32)],
        scratch_shapes=[pltpu.VMEM((2 * SB_PAIRS, t, 128), bf16), pltpu.VMEM((SB_PAIRS, t, 128), bf16),
                        pltpu.VMEM((SB_PAIRS, t, 128), bf16)],
        compiler_params=_cp(("arbitrary",)),
    )(sbqkv, sbqkv, sbqkv, gq, gk)


def sb_bwd(sbqkv, gq, gk, ltot, do, dproj):
    bsz, t, _ = sbqkv.shape
    blk = min(SB_TILE, t)
    nq = t // blk
    scale = SB_DIM ** -0.5

    def body(q_ref, k_ref, v_ref, gq_ref, gk_ref, l_ref, do_ref, dp_in, dp_ref, dgq_ref, dgk_ref,
             q2_sc, kn_sc, v_sc, do2_sc, dqn_sc, dkn_sc, dv_sc):
        bavg = _group_avg_mats()
        lane = lax.broadcasted_iota(jnp.int32, (1, 128), 1)
        first = lane < SB_DIM
        fq = lambda x, g: _pair_norm(x, g, bavg)
        vjps = []
        for p in range(SB_PAIRS):
            ls = slice(p * 128, (p + 1) * 128)
            qn, q_vjp = jax.vjp(fq, q_ref[0, :, ls], gq_ref[...])
            kn, k_vjp = jax.vjp(fq, k_ref[0, :, ls], gk_ref[...])
            vjps.append((q_vjp, k_vjp))
            kn_sc[p] = kn.astype(bf16)
            v_sc[p] = v_ref[0, :, ls].astype(bf16)
            dov = do_ref[0, :, ls]
            q2_sc[2 * p] = jnp.where(first, qn, 0.0).astype(bf16)
            q2_sc[2 * p + 1] = jnp.where(first, 0.0, qn).astype(bf16)
            do2_sc[2 * p] = jnp.where(first, dov, 0.0).astype(bf16)
            do2_sc[2 * p + 1] = jnp.where(first, 0.0, dov).astype(bf16)
        dkn_sc[...] = jnp.zeros_like(dkn_sc)
        dv_sc[...] = jnp.zeros_like(dv_sc)
        r, c = _iota2((blk, blk))
        pincl = (r <= c).astype(bf16)
        pstrict = (r < c).astype(bf16)
        r2, c2 = _iota2((2 * blk, blk))
        causal = c2 < (r2 & (blk - 1))

        def tile(q2s, do2s, lts, ks, carry, diag):
            out = []
            for p in range(SB_PAIRS):
                dq, cs, ce = carry[3 * p:3 * p + 3]
                q2, do2 = q2s[p], do2s[p]
                kb = kn_sc[p, pl.ds(ks, blk), :]
                zz = lax.dot_general(q2, kb, NT, preferred_element_type=f32) * scale
                sp = _softplus(zz)
                lm = jnp.where(causal, -sp, 0.0) if diag else -sp
                pre = _dot_x2c(lm, pincl)
                lp = zz - sp
                wgt = jnp.exp(lp + (lts[p] - cs - pre))
                if diag:
                    wgt = jnp.where(causal, wgt, 0.0)
                dw = lax.dot_general(do2, v_sc[p, pl.ds(ks, blk), :], NT, preferred_element_type=f32)
                e = wgt * dw
                ee = ce + _dot_x2c(e, pstrict)
                sig = jnp.exp(lp)
                dz = (e * (1.0 - sig) - ee * sig) * scale
                if diag:
                    dz = jnp.where(causal, dz, 0.0)
                dz = dz.astype(bf16)
                dkn_sc[p, pl.ds(ks, blk), :] += lax.dot_general(dz, q2, TN, preferred_element_type=f32)
                dv_sc[p, pl.ds(ks, blk), :] += lax.dot_general(wgt.astype(bf16), do2, TN, preferred_element_type=f32)
                out += [dq + _pdot(dz, kb), cs + jnp.sum(lm, axis=1, keepdims=True), ce + jnp.sum(e, axis=1, keepdims=True)]
            return tuple(out)

        def qloop(qi, _):
            qs = pl.multiple_of(qi * blk, blk)
            rows = pl.ds(qs, blk)
            q2s = [jnp.concatenate([q2_sc[2 * p, rows, :], q2_sc[2 * p + 1, rows, :]], axis=0) for p in range(SB_PAIRS)]
            do2s = [jnp.concatenate([do2_sc[2 * p, rows, :], do2_sc[2 * p + 1, rows, :]], axis=0) for p in range(SB_PAIRS)]
            lts = [jnp.concatenate([l_ref[0, rows, p * 128:p * 128 + 1], l_ref[0, rows, p * 128 + SB_DIM:p * 128 + SB_DIM + 1]],
                                   axis=0) for p in range(SB_PAIRS)]
            z1 = jnp.zeros((2 * blk, 1), f32)
            carry = lax.fori_loop(0, qi, lambda kj, cr: tile(q2s, do2s, lts, pl.multiple_of(kj * blk, blk), cr, False),
                                  (jnp.zeros((2 * blk, 128), f32), z1, z1) * SB_PAIRS)
            carry = tile(q2s, do2s, lts, qs, carry, True)
            for p in range(SB_PAIRS):
                dq = carry[3 * p]
                dqn_sc[p, rows, :] = jnp.where(first, dq[0:blk], dq[blk:2 * blk])
            return 0

        lax.fori_loop(0, nq, qloop, 0)
        dgq_tot, dgk_tot = jnp.zeros((1, 128), f32), jnp.zeros((1, 128), f32)
        for p in range(SB_PAIRS):
            ls = slice(p * 128, (p + 1) * 128)
            dq_pre, dgq = vjps[p][0](dqn_sc[p])
            dk_pre, dgk = vjps[p][1](dkn_sc[p])
            dp_ref[0, :, p * 128:(p + 1) * 128] = dq_pre.astype(bf16)
            dp_ref[0, :, SB_WIDTH + p * 128:SB_WIDTH + (p + 1) * 128] = dk_pre.astype(bf16)
            dp_ref[0, :, 2 * SB_WIDTH + p * 128:2 * SB_WIDTH + (p + 1) * 128] = dv_sc[p].astype(bf16)
            dgq_tot, dgk_tot = dgq_tot + dgq, dgk_tot + dgk
        dgq_ref[0] = jnp.broadcast_to(dgq_tot, (8, 128))
        dgk_ref[0] = jnp.broadcast_to(dgk_tot, (8, 128))

    col = lambda off: pl.BlockSpec((1, t, SB_WIDTH), lambda b: (b, 0, off), pipeline_mode=pl.Buffered(1))
    gsp = pl.BlockSpec((1, 128), lambda b: (0, 0))
    gout = pl.BlockSpec((1, 8, 128), lambda b: (b, 0, 0))
    return pl.pallas_call(
        body, name="sb_bwd", grid=(bsz,),
        in_specs=[col(0), col(1), col(2), gsp, gsp, col(0), col(0), pl.BlockSpec(memory_space=pl.ANY)],
        out_specs=[pl.BlockSpec((1, t, 3 * SB_WIDTH), lambda b: (b, 0, C_SB // (3 * SB_WIDTH)), pipeline_mode=pl.Buffered(1)),
                   gout, gout],
        out_shape=[SDS(dproj.shape, bf16)] + [SDS((bsz, 8, 128), f32)] * 2,
        input_output_aliases={7: 0},
        scratch_shapes=[pltpu.VMEM((2 * SB_PAIRS, t, 128), bf16), pltpu.VMEM((SB_PAIRS, t, 128), bf16),
                        pltpu.VMEM((SB_PAIRS, t, 128), bf16), pltpu.VMEM((2 * SB_PAIRS, t, 128), bf16),
                        pltpu.VMEM((SB_PAIRS, t, 128), f32), pltpu.VMEM((SB_PAIRS, t, 128), f32), pltpu.VMEM((SB_PAIRS, t, 128), f32)],
        compiler_params=_cp(("arbitrary",)),
    )(sbqkv, sbqkv, sbqkv, gq, gk, ltot, do, dproj)


def sg_pair(u, v, gain, wa, wb, ba, bb, bavg):
    r, c = _iota2((SG_CHUNK, SG_CHUNK))
    lane = lax.broadcasted_iota(jnp.int32, (1, 128), 1)
    first = lane < SG_DIM
    vn = _pair_norm(_gelu(v), gain, bavg)
    tri = c <= r
    mixed = (mm(jnp.where(tri, wa, 0.0), jnp.where(first, vn, 0.0)) + mm(jnp.where(tri, wb, 0.0), jnp.where(first, 0.0, vn))
             + jnp.where(first, ba, bb))
    return _gelu(u) * mixed


def sg_fwd(sguv, gain, w, bt):
    bsz, t, _ = sguv.shape
    nch = t // SG_CHUNK

    def body(uv_ref, g_ref, w_ref, b_ref, o_ref):
        bavg = _group_avg_mats()
        for p in range(2):
            ls = slice(p * 128, (p + 1) * 128)
            o_ref[0, :, ls] = sg_pair(uv_ref[0, :, ls], uv_ref[0, :, SG_WIDTH + p * 128:SG_WIDTH + (p + 1) * 128], g_ref[:, ls],
                                      w_ref[2 * p], w_ref[2 * p + 1], b_ref[:, 2 * p:2 * p + 1], b_ref[:, 2 * p + 1:2 * p + 2], bavg)

    full = lambda shp: pl.BlockSpec(shp, lambda b, n: (0,) * len(shp))
    return pl.pallas_call(
        body, name="sg_fwd", grid=(bsz, nch),
        in_specs=[pl.BlockSpec((1, SG_CHUNK, 2 * SG_WIDTH), lambda b, n: (b, n, 0)), full((1, SG_WIDTH)),
                  full((SG_GROUPS, SG_CHUNK, SG_CHUNK)), full((SG_CHUNK, 128))],
        out_specs=pl.BlockSpec((1, SG_CHUNK, SG_WIDTH), lambda b, n: (b, n, 0)),
        out_shape=SDS((bsz, t, SG_WIDTH), f32),
        compiler_params=_cp(("arbitrary", "arbitrary")),
    )(sguv, gain, w, bt)


def sg_bwd(sguv, gain, w, bt, do, dproj):
    bsz, t, _ = sguv.shape
    nch = t // SG_CHUNK

    def body(uv_ref, g_ref, w_ref, b_ref, do_ref, dp_in, duv_ref, dg_ref, dw_ref, db_ref):
        @pl.when((pl.program_id(0) == 0) & (pl.program_id(1) == 0))
        def _():
            dg_ref[...] = jnp.zeros_like(dg_ref)
            dw_ref[...] = jnp.zeros_like(dw_ref)
            db_ref[...] = jnp.zeros_like(db_ref)

        bavg = _group_avg_mats()
        lane = lax.broadcasted_iota(jnp.int32, (SG_CHUNK, 128), 1)
        dbt = jnp.zeros((SG_CHUNK, 128), f32)
        for p in range(2):
            ls = slice(p * 128, (p + 1) * 128)
            vs = slice(SG_WIDTH + p * 128, SG_WIDTH + (p + 1) * 128)
            prim = (uv_ref[0, :, ls], uv_ref[0, :, vs], g_ref[:, ls], w_ref[2 * p], w_ref[2 * p + 1],
                    b_ref[:, 2 * p:2 * p + 1], b_ref[:, 2 * p + 1:2 * p + 2])
            _, vjp = jax.vjp(lambda *a: sg_pair(*a, bavg), *prim)
            du, dv, dgn, dwa, dwb, dba, dbb = vjp(do_ref[0, :, ls])
            duv_ref[0, :, ls] = du.astype(bf16)
            duv_ref[0, :, vs] = dv.astype(bf16)
            dg_ref[:, ls] += dgn
            dw_ref[2 * p] += dwa
            dw_ref[2 * p + 1] += dwb
            dbt = dbt + jnp.where(lane == 2 * p, dba, 0.0) + jnp.where(lane == 2 * p + 1, dbb, 0.0)
        db_ref[...] += dbt

    full = lambda shp: pl.BlockSpec(shp, lambda b, n: (0,) * len(shp))
    return pl.pallas_call(
        body, name="sg_bwd", grid=(bsz, nch),
        in_specs=[pl.BlockSpec((1, SG_CHUNK, 2 * SG_WIDTH), lambda b, n: (b, n, 0)), full((1, SG_WIDTH)),
                  full((SG_GROUPS, SG_CHUNK, SG_CHUNK)), full((SG_CHUNK, 128)),
                  pl.BlockSpec((1, SG_CHUNK, SG_WIDTH), lambda b, n: (b, n, 0)), pl.BlockSpec(memory_space=pl.ANY)],
        out_specs=[pl.BlockSpec((1, SG_CHUNK, 2 * SG_WIDTH), lambda b, n: (b, n, C_SG // (2 * SG_WIDTH))), full((1, SG_WIDTH)),
                   full((SG_GROUPS, SG_CHUNK, SG_CHUNK)), full((SG_CHUNK, 128))],
        out_shape=[SDS(dproj.shape, bf16), SDS((1, SG_WIDTH), f32), SDS((SG_GROUPS, SG_CHUNK, SG_CHUNK), f32),
                   SDS((SG_CHUNK, 128), f32)],
        input_output_aliases={5: 0},
        compiler_params=_cp(("arbitrary", "arbitrary")),
    )(sguv, gain, w, bt, do, dproj)


def _pad_lanes(v, n=128):
    return jnp.pad(v.reshape(1, -1), ((0, 0), (0, n - v.size)))


def _w_in_runs():
    shard, runs = IN_DIM // N_CHIPS, []
    for s in range(N_CHIPS):
        for a, b, d in ((0, 2048, 0), (2048, 2056, C_AB), (2056, IN_DIM, C_SB)):
            lo, hi = max(shard * s, a), min(shard * (s + 1), b)
            if lo < hi:
                runs.append((s, lo - shard * s, hi - shard * s, d + lo - a))
    return runs


def w_in_from_shards(zone, tr=256):
    def body(z_ref, o_ref):
        o_ref[:, C_AB:C_SB] = jnp.zeros((tr, C_SB - C_AB), zone.dtype)
        for s, a, b, d in _w_in_runs():
            o_ref[:, d:d + b - a] = z_ref[s, :, a:b]

    return pl.pallas_call(
        body, name="w_in_from_shards", grid=(D_MODEL // tr,),
        in_specs=[pl.BlockSpec((N_CHIPS, tr, IN_DIM // N_CHIPS), lambda i: (0, i, 0))],
        out_specs=pl.BlockSpec((tr, IN_PAD), lambda i: (i, 0)), out_shape=SDS((D_MODEL, IN_PAD), zone.dtype),
        compiler_params=_cp(("arbitrary",)))(zone)


def w_in_grad_to_shards(g, tr=256):
    def body(g_ref, o_ref):
        for s, a, b, d in _w_in_runs():
            o_ref[s, :, a:b] = g_ref[:, d:d + b - a]

    return pl.pallas_call(
        body, name="w_in_grad_to_shards", grid=(D_MODEL // tr,),
        in_specs=[pl.BlockSpec((tr, IN_PAD), lambda i: (i, 0))],
        out_specs=pl.BlockSpec((N_CHIPS, tr, IN_DIM // N_CHIPS), lambda i: (0, i, 0)),
        out_shape=SDS((N_CHIPS, D_MODEL, IN_DIM // N_CHIPS), g.dtype), compiler_params=_cp(("arbitrary",)))(g)


def layer_params(p, l):
    return dict(
        g1=p["norm1_g"][l].reshape(1, -1), g2=p["norm2_g"][l].reshape(1, -1),
        conv=jnp.pad(p["conv_w"][l], ((0, 4), (0, 0))), alog=_pad_lanes(p["a_log"][l]), dtb=_pad_lanes(p["dt_bias"][l]),
        dng=p["dn_out_g"][l].reshape(1, -1), gq=jnp.tile(p["sb_q_g"][l].reshape(1, -1), (1, 2)),
        gk=jnp.tile(p["sb_k_g"][l].reshape(1, -1), (1, 2)), sgg=p["sg_v_g"][l].reshape(1, -1), sgw=p["sg_w"][l],
        sgb=jnp.pad(p["sg_b"][l].T, ((0, 0), (0, 124))))


def local_step(x, tgt, small, get_w, put_g, sync_g):
    bsz, t, _ = x.shape
    m = bsz * t
    r3 = lambda a: a.reshape(bsz, t, a.shape[-1])
    r2 = lambda a: a.reshape(m, a.shape[-1])
    xs, saved, ws = x.reshape(m, D_MODEL), [], []
    for l in range(DEPTH):
        sp, w = layer_params(small, l), {}
        w["w_in"] = get_w(l, "in", xs)
        qkv, z, ab, sb, sg, h1 = inproj_fwd(xs, sp["g1"], w["w_in"])
        odn, sall, tall = dn_fwd(r3(qkv), r3(z), r3(ab), sp["conv"], sp["alog"], sp["dtb"], sp["dng"])
        osb, ltot = sb_fwd(r3(sb), sp["gq"], sp["gk"])
        osg = sg_fwd(r3(sg), sp["sgg"], sp["sgw"], sp["sgb"])
        w["w_out"] = get_w(l, "out", osg)
        x2, mix = outproj_fwd(xs, r2(odn), r2(osb), r2(osg), w["w_out"])
        w["w_ff1"], w["w_ff2"], started = get_w(l, "ff", x2)
        x3, rlb = ffn_fwd(x2, sp["g2"] + started, w["w_ff1"], w["w_ff2"])
        saved.append(dict(rlb=rlb, h1=h1, x=xs,qkv=qkv, z=z, ab=ab, sb=sb, sg=sg, sall=sall, tall=tall, ltot=ltot, mix=mix, x2=x2))
        ws.append(w)
        xs = x3
    dx, lossp = loss_head(xs, tgt.reshape(m, D_MODEL))
    gsmall = [None] * DEPTH
    token = jnp.zeros((), f32)
    for l in reversed(range(DEPTH)):
        sp, w, s = layer_params(small, l), ws[l], saved[l]
        dx2, dg2, h2, act, df, dyb = ffn_bwd(s["x2"], sp["g2"] + token, w["w_ff1"], w["w_ff2"], s["rlb"], dx)
        g_ff1 = tn_matmul(h2, df, f"dw_ff1_{l}", col_shards=N_CHIPS)
        g_ff2 = tn_matmul(act, dyb, f"dw_ff2_{l}")
        dodn, dosb, dosg, dx2b = outproj_bwd(dx2, w["w_out"])
        g_out = tn_matmul(s["mix"], dx2b, f"dw_out_{l}")
        token = token + put_g(l, "rest", dict(w_out=g_out, w_ff1=g_ff1, w_ff2=g_ff2))
        dproj, dconv, dalog, ddtb, ddng = dn_bwd(r3(s["qkv"]), r3(s["z"]), r3(s["ab"]), sp["conv"], sp["alog"], sp["dtb"],
                                                 sp["dng"] + token, s["sall"], s["tall"], r3(dodn))
        token = sync_g(ddng)
        dproj, dgq, dgk = sb_bwd(r3(s["sb"]), sp["gq"] + token, sp["gk"], s["ltot"], r3(dosb), dproj)
        dproj, dsgg, dsgw, dsgb = sg_bwd(r3(s["sg"]), sp["sgg"], sp["sgw"], sp["sgb"], r3(dosg), dproj)
        dproj = r2(dproj)
        g_in = tn_matmul(s["h1"], dproj, f"dw_in_{l}")
        token = put_g(l, "in", dict(w_in=g_in))
        dx, dg1 = inproj_bwd(s["x"], sp["g1"] + token, w["w_in"], dproj, dx2)
        token = sync_g(dg1)
        fold = lambda a: (a[:, 0, :].sum(0).reshape(2, SB_DIM)).sum(0)
        gsmall[l] = dict(norm1_g=dg1[0], conv_w=dconv[0:DN_CONV], a_log=dalog[0, 0:DN_HEADS], dt_bias=ddtb[0, 0:DN_HEADS],
                         dn_out_g=ddng[0], sb_q_g=fold(dgq), sb_k_g=fold(dgk), sg_v_g=dsgg[0], sg_w=dsgw,
                         sg_b=dsgb[:, 0:SG_GROUPS].T, norm2_g=dg2[0])
    return lossp, dx.reshape(bsz, t, D_MODEL), gsmall


def _chip_peers(x, y):
    return [(1 - x, y), (x, 1 - y), (1 - x, 1 - y)]


_HBM = pl.BlockSpec(memory_space=pltpu.HBM)
_SEM = pl.BlockSpec(memory_space=pltpu.SEMAPHORE)
_EFFECT = pltpu.SideEffectType.DATAFLOW_SIDE_EFFECTING


def _hbm(a):
    return pltpu.with_memory_space_constraint(a, pltpu.HBM)


def _my_half(ref):
    half = ref.shape[0] // 2
    return ref.at[pl.ds(pl.multiple_of(lax.axis_index("c") * half, 8), half)]


def _exchange_copy(src, land, k, j, send, recv, scatter, halve, waiting):
    x, y, c = lax.axis_index("x"), lax.axis_index("y"), lax.axis_index("c")
    px, py = _chip_peers(x, y)[j]
    me, peer = 2 * x + y, 2 * px + py
    if scatter:
        src = src.at[me if waiting else peer]
    dst = land.at[peer if waiting else me]
    if halve:
        src, dst = _my_half(src), _my_half(dst)
    return pltpu.make_async_remote_copy(src_ref=src, dst_ref=dst, send_sem=send.at[3 * k + j],
                                        recv_sem=recv.at[3 * k + j], device_id=(px, py, c), device_id_type=MESH)


def exchange_start(items, name, scatter, after=None):
    arrs = []
    for a, _, _ in items:
        if not any(a is b for b in arrs):
            arrs.append(a)
    pos = [next(i for i, b in enumerate(arrs) if b is a) for a, _, _ in items]
    shapes = [a.shape if idx is None else a.shape[1:] for a, idx, _ in items]
    lands = [lax.empty(s if scatter else (N_CHIPS,) + s, a.dtype) for (a, _, _), s in zip(items, shapes)]
    na, nl = len(arrs), len(lands)
    n_in = na + nl + (after is not None)

    def body(*refs):
        ins, lnd = refs[:na], refs[na:na + nl]
        send, recv = refs[n_in], refs[n_in + 1]
        token = refs[-1]
        for k, (_, idx, halve) in enumerate(items):
            src = ins[pos[k]] if idx is None else ins[pos[k]].at[idx]
            for j in range(3):
                _exchange_copy(src, lnd[k], k, j, send, recv, scatter, halve, False).start()
        token[...] = jnp.zeros_like(token)

    sems = pltpu.SemaphoreType.DMA((3 * nl,))
    extra = [] if after is None else [after]
    out = pl.pallas_call(
        body, name=name,
        out_shape=(sems, sems, *[pltpu.HBM(a.shape, a.dtype) for a in arrs + lands], SDS((8, 128), f32)),
        in_specs=[_HBM] * (na + nl) + [pl.BlockSpec(memory_space=pl.ANY)] * len(extra),
        out_specs=(_SEM, _SEM, *[_HBM] * (na + nl), pl.BlockSpec(memory_space=pltpu.VMEM)),
        input_output_aliases={i: 2 + i for i in range(na + nl)},
        compiler_params=pltpu.CompilerParams(has_side_effects=_EFFECT),
    )(*[_hbm(a) for a in arrs + lands], *extra)
    thru = out[2:2 + na]
    return dict(send=out[0], recv=out[1], src=[(thru[pos[k]], idx) for k, (_, idx, _) in enumerate(items)],
                halve=[h for _, _, h in items], land=list(out[2 + na:2 + na + nl]), token=out[-1], scatter=scatter)


def exchange_wait(st, ks, after, name):
    arrs = []
    for k in ks:
        if not any(st["src"][k][0] is b for b in arrs):
            arrs.append(st["src"][k][0])
    pos = [next(i for i, b in enumerate(arrs) if b is st["src"][k][0]) for k in ks]
    lands = [st["land"][k] for k in ks]
    na, nl = len(arrs), len(lands)

    def body(*refs):
        ins, lnd = refs[:na], refs[na:na + nl]
        send, recv = refs[na + nl], refs[na + nl + 1]
        for t, k in enumerate(ks):
            idx = st["src"][k][1]
            src = ins[pos[t]] if idx is None else ins[pos[t]].at[idx]
            for j in range(3):
                cp = _exchange_copy(src, lnd[t], k, j, send, recv, st["scatter"], st["halve"][k], True)
                cp.wait_send()
                cp.wait_recv()

    out = pl.pallas_call(
        body, name=name, out_shape=tuple(pltpu.HBM(a.shape, a.dtype) for a in arrs + lands),
        in_specs=[_HBM] * (na + nl) + [_SEM, _SEM, pl.BlockSpec(memory_space=pl.ANY)], out_specs=tuple([_HBM] * (na + nl)),
        input_output_aliases={i: i for i in range(na + nl)},
        compiler_params=pltpu.CompilerParams(has_side_effects=_EFFECT),
    )(*arrs, *lands, st["send"], st["recv"], after)
    for k, (a, idx) in enumerate(st["src"]):
        for p, b in enumerate(arrs):
            if a is b:
                st["src"][k] = (out[p], idx)
    return list(out[na:na + nl])


def _sibling_copy(src, land, i, send, recv, other_half):
    x, y, c = lax.axis_index("x"), lax.axis_index("y"), lax.axis_index("c")
    return pltpu.make_async_remote_copy(src_ref=src.at[:, 1 - c] if other_half else src, dst_ref=land, send_sem=send.at[i],
                                        recv_sem=recv.at[i], device_id=(x, y, 1 - c), device_id_type=MESH)


def sibling_start(arrs, name, other_half=False):
    n = len(arrs)
    lands = [lax.empty((a.shape[0],) + a.shape[2:] if other_half else a.shape, a.dtype) for a in arrs]

    def body(*refs):
        ins, lnd = refs[:n], refs[n:2 * n]
        send, recv = refs[2 * n], refs[2 * n + 1]
        token = refs[-1]
        for i in range(n):
            _sibling_copy(ins[i], lnd[i], i, send, recv, other_half).start()
        token[...] = jnp.zeros_like(token)

    sems = pltpu.SemaphoreType.DMA((n,))
    out = pl.pallas_call(
        body, name=name,
        out_shape=(sems, sems, *[pltpu.HBM(a.shape, a.dtype) for a in arrs + lands], SDS((8, 128), f32)),
        in_specs=[_HBM] * (2 * n), out_specs=(_SEM, _SEM, *[_HBM] * (2 * n), pl.BlockSpec(memory_space=pltpu.VMEM)),
        input_output_aliases={i: 2 + i for i in range(2 * n)},
        compiler_params=pltpu.CompilerParams(has_side_effects=_EFFECT),
    )(*[_hbm(a) for a in arrs + lands])
    return dict(send=out[0], recv=out[1], src=list(out[2:2 + n]), land=list(out[2 + n:2 + 2 * n]), token=out[-1],
                other_half=other_half)


def sibling_wait(st, after, name):
    n = len(st["src"])

    def body(*refs):
        ins, lnd = refs[:n], refs[n:2 * n]
        send, recv = refs[2 * n], refs[2 * n + 1]
        for i in range(n):
            cp = _sibling_copy(ins[i], lnd[i], i, send, recv, st["other_half"])
            cp.wait_send()
            cp.wait_recv()

    out = pl.pallas_call(
        body, name=name, out_shape=tuple(pltpu.HBM(a.shape, a.dtype) for a in st["src"] + st["land"]),
        in_specs=[_HBM] * (2 * n) + [_SEM, _SEM, pl.BlockSpec(memory_space=pl.ANY)], out_specs=tuple([_HBM] * (2 * n)),
        input_output_aliases={i: i for i in range(2 * n)},
        compiler_params=pltpu.CompilerParams(has_side_effects=_EFFECT),
    )(*st["src"], *st["land"], st["send"], st["recv"], after)
    return list(out[:n]), list(out[n:])


def swap_halves(zones, name):
    n = len(zones)

    def body(*refs):
        outs = refs[n:2 * n]
        send, recv = refs[2 * n:]
        x, y, c = lax.axis_index("x"), lax.axis_index("y"), lax.axis_index("c")
        cps = []
        for i in range(n):
            for j, (px, py) in enumerate(_chip_peers(x, y)):
                part = _my_half(outs[i].at[2 * px + py])
                cps.append(pltpu.make_async_remote_copy(src_ref=part, dst_ref=part, send_sem=send.at[3 * i + j],
                                                        recv_sem=recv.at[3 * i + j], device_id=(x, y, 1 - c), device_id_type=MESH))
        for cp in cps:
            cp.start()
        for cp in cps:
            cp.wait_send()
            cp.wait_recv()

    any_spec = pl.BlockSpec(memory_space=pl.ANY)
    return pl.pallas_call(
        body, name=name, in_specs=[any_spec] * n, out_specs=[any_spec] * n, out_shape=[SDS(a.shape, a.dtype) for a in zones],
        input_output_aliases={i: i for i in range(n)},
        scratch_shapes=[pltpu.SemaphoreType.DMA((3 * n,)), pltpu.SemaphoreType.DMA((3 * n,))],
    )(*zones)


def _ids_spec(grid, in_specs, out_specs):
    return pltpu.PrefetchScalarGridSpec(num_scalar_prefetch=1, grid=grid, in_specs=in_specs, out_specs=out_specs)


def pair_sum(ids, a, b, name, tr=512):
    nd, _, rows, cols = a.shape
    tr = min(tr, rows)
    assert rows % tr == 0

    def body(ids_ref, a_ref, b_ref, o_ref):
        o_ref[...] = (a_ref[0].astype(f32) + b_ref[...].astype(f32)).astype(bf16)

    spec = pl.BlockSpec((1, tr, cols), lambda d, i, ids: (d, i, 0))
    return pl.pallas_call(
        body, name=name,
        grid_spec=_ids_spec((nd, rows // tr), [pl.BlockSpec((1, 1, tr, cols), lambda d, i, ids: (d, ids[1], i, 0)), spec], spec),
        out_shape=SDS((nd, rows, cols), bf16), compiler_params=_cp(("arbitrary", "arbitrary")))(ids, a, b)


N_DEV = 8


def _device_copy(src, land, r, send, recv, waiting):
    x, y, c = lax.axis_index("x"), lax.axis_index("y"), lax.axis_index("c")
    px, py, pc = (1 - x if r & 4 else x), (1 - y if r & 2 else y), (1 - c if r & 1 else c)
    me, peer = 4 * x + 2 * y + c, 4 * px + 2 * py + pc
    return pltpu.make_async_remote_copy(src_ref=src, dst_ref=land.at[peer if waiting else me], send_sem=send.at[r - 1],
                                        recv_sem=recv.at[r - 1], device_id=(px, py, pc), device_id_type=MESH)


def gather_devices_start(v, name, after):
    land = lax.empty((N_DEV,) + v.shape, v.dtype)

    def body(v_ref, land_ref, after_ref, send, recv, v_thru, land_thru, token):
        for r in range(1, N_DEV):
            _device_copy(v_ref, land_ref, r, send, recv, False).start()
        token[...] = jnp.zeros_like(token)

    sems = pltpu.SemaphoreType.DMA((N_DEV - 1,))
    out = pl.pallas_call(
        body, name=name, out_shape=(sems, sems, pltpu.HBM(v.shape, v.dtype), pltpu.HBM(land.shape, land.dtype), SDS((8, 128), f32)),
        in_specs=[_HBM, _HBM, pl.BlockSpec(memory_space=pl.ANY)],
        out_specs=(_SEM, _SEM, _HBM, _HBM, pl.BlockSpec(memory_space=pltpu.VMEM)),
        input_output_aliases={0: 2, 1: 3}, compiler_params=pltpu.CompilerParams(has_side_effects=_EFFECT),
    )(_hbm(v), _hbm(land), after)
    return dict(send=out[0], recv=out[1], src=out[2], land=out[3], token=out[4])


def gather_devices_wait(st, after, name):
    def body(v_ref, land_ref, send, recv, after_ref, v_thru, land_thru):
        for r in range(1, N_DEV):
            cp = _device_copy(v_ref, land_ref, r, send, recv, True)
            cp.wait_send()
            cp.wait_recv()

    return pl.pallas_call(
        body, name=name, out_shape=(pltpu.HBM(st["src"].shape, st["src"].dtype), pltpu.HBM(st["land"].shape, st["land"].dtype)),
        in_specs=[_HBM, _HBM, _SEM, _SEM, pl.BlockSpec(memory_space=pl.ANY)], out_specs=(_HBM, _HBM),
        input_output_aliases={0: 0, 1: 1}, compiler_params=pltpu.CompilerParams(has_side_effects=_EFFECT),
    )(st["src"], st["land"], st["send"], st["recv"], after)


def sum_devices(p, tr=256):
    _, rows, cols = p.shape
    assert rows % tr == 0

    def body(p_ref, o_ref):
        acc = p_ref[0]
        for d in range(1, N_DEV):
            acc = acc + p_ref[d]
        o_ref[...] = acc

    return pl.pallas_call(
        body, name="sum_devices", grid=(rows // tr,), in_specs=[pl.BlockSpec((N_DEV, tr, cols), lambda i: (0, i, 0))],
        out_specs=pl.BlockSpec((tr, cols), lambda i: (i, 0)), out_shape=SDS((rows, cols), f32),
        compiler_params=_cp(("arbitrary",)))(p)


def sum_partials(ids, zone, mine, name, tr=256):
    _, rows, cols = zone.shape
    tr = min(tr, rows)
    assert rows % tr == 0

    def body(ids_ref, m_ref, z1_ref, z2_ref, z3_ref, o_ref):
        o_ref[...] = ((m_ref[0].astype(f32) + z1_ref[0].astype(f32)) + z2_ref[0].astype(f32)) + z3_ref[0].astype(f32)

    slot = lambda flip: pl.BlockSpec((1, tr, cols), lambda i, ids: (ids[0] ^ flip, i, 0))
    return pl.pallas_call(
        body, name=name,
        grid_spec=_ids_spec((rows // tr,), [slot(0), slot(1), slot(2), slot(3)], pl.BlockSpec((tr, cols), lambda i, ids: (i, 0))),
        out_shape=SDS((rows, cols), f32), compiler_params=_cp(("arbitrary",)),
    )(ids, mine, zone, zone, zone)


def adamw(w, m, v, gs, name, layer=0, prev=None, tr=256):
    hrows, cols = gs[0].shape
    rows = hrows * len(gs)
    tr = min(tr, hrows)
    assert hrows % tr == 0 and w.shape[0] % rows == 0
    off, nth = layer * (rows // tr), hrows // tr

    def body(w_ref, m_ref, v_ref, *rest):
        g_ref, d_ref, mo_ref, vo_ref = rest[-4:]
        if len(gs) == 1:
            g = rest[0][...]
        else:
            g = jnp.where(pl.program_id(0) // nth == lax.axis_index("c"), rest[0][...], rest[1][...])
        mn = ADAM_B1 * m_ref[...] + (1.0 - ADAM_B1) * g
        vn = ADAM_B2 * v_ref[...] + (1.0 - ADAM_B2) * jnp.square(g)
        m_hat = mn / (1.0 - ADAM_B1 ** ADAM_STEP)
        v_hat = vn / (1.0 - ADAM_B2 ** ADAM_STEP)
        g_ref[...] = g
        d_ref[...] = -ADAM_LR * (m_hat / (jnp.sqrt(v_hat) + ADAM_EPS) + ADAM_WD * w_ref[...])
        mo_ref[...] = mn
        vo_ref[...] = vn

    loc = pl.BlockSpec((tr, cols), lambda i: (i % nth, 0))
    glob = pl.BlockSpec((tr, cols), lambda i: (off + i, 0))
    extra = [] if prev is None else list(prev)
    return pl.pallas_call(
        body, name=name, grid=(rows // tr,),
        in_specs=[glob] * 3 + [loc] * len(gs) + [pl.BlockSpec(memory_space=pl.ANY)] * len(extra),
        out_specs=[glob] * 4, out_shape=[SDS(w.shape, f32)] * 4,
        input_output_aliases={3 + len(gs) + j: j for j in range(len(extra))},
        compiler_params=_cp(("arbitrary",)),
    )(w, m, v, *gs, *extra)


BIG = ("w_in", "w_out", "w_ff1", "w_ff2")
SMALL = ("norm1_g", "conv_w", "a_log", "dt_bias", "dn_out_g", "sb_q_g", "sb_k_g", "sg_v_g", "sg_w", "sg_b", "norm2_g")
WEIGHTS = ("norm1_g", "w_in", "conv_w", "a_log", "dt_bias", "dn_out_g", "sb_q_g", "sb_k_g", "sg_v_g", "sg_w", "sg_b",
           "w_out", "norm2_g", "w_ff1", "w_ff2")


PACK_ROWS = 256


def _rows_of(shape):
    n = 1
    for d in shape:
        n *= d
    return -(-n // 1024) * 8, n


def _pack(arrs):
    parts = []
    for a in arrs:
        r, n = _rows_of(a.shape)
        parts.append(jnp.pad(a.reshape(-1), (0, r * 128 - n)).reshape(r, 128))
    rows = sum(p.shape[0] for p in parts)
    parts.append(jnp.zeros((-rows % PACK_ROWS, 128), arrs[0].dtype))
    return jnp.concatenate(parts, axis=0)


def _unpack(packed, shapes):
    out, o = [], 0
    for s in shapes:
        r, n = _rows_of(s)
        out.append(packed[o:o + r].reshape(-1)[0:n].reshape(s))
        o += r
    return out


def kernel(x, norm1_g, w_in, conv_w, a_log, dt_bias, dn_out_g, sb_q_g, sb_k_g, sg_v_g, sg_w, sg_b, w_out, norm2_g, w_ff1, w_ff2, loss_target, m_norm1_g, m_w_in, m_conv_w, m_a_log, m_dt_bias, m_dn_out_g, m_sb_q_g, m_sb_k_g, m_sg_v_g, m_sg_w, m_sg_b, m_w_out, m_norm2_g, m_w_ff1, m_w_ff2, v_norm1_g, v_w_in, v_conv_w, v_a_log, v_dt_bias, v_dn_out_g, v_sb_q_g, v_sb_k_g, v_sg_v_g, v_sg_w, v_sg_b, v_w_out, v_norm2_g, v_w_ff1, v_w_ff2):
    w = dict(norm1_g=norm1_g, w_in=w_in, conv_w=conv_w, a_log=a_log, dt_bias=dt_bias, dn_out_g=dn_out_g, sb_q_g=sb_q_g,
             sb_k_g=sb_k_g, sg_v_g=sg_v_g, sg_w=sg_w, sg_b=sg_b, w_out=w_out, norm2_g=norm2_g, w_ff1=w_ff1, w_ff2=w_ff2)
    mom = dict(norm1_g=m_norm1_g, w_in=m_w_in, conv_w=m_conv_w, a_log=m_a_log, dt_bias=m_dt_bias, dn_out_g=m_dn_out_g,
               sb_q_g=m_sb_q_g, sb_k_g=m_sb_k_g, sg_v_g=m_sg_v_g, sg_w=m_sg_w, sg_b=m_sg_b, w_out=m_w_out, norm2_g=m_norm2_g,
               w_ff1=m_w_ff1, w_ff2=m_w_ff2)
    var = dict(norm1_g=v_norm1_g, w_in=v_w_in, conv_w=v_conv_w, a_log=v_a_log, dt_bias=v_dt_bias, dn_out_g=v_dn_out_g,
               sb_q_g=v_sb_q_g, sb_k_g=v_sb_k_g, sg_v_g=v_sg_v_g, sg_w=v_sg_w, sg_b=v_sg_b, w_out=v_w_out, norm2_g=v_norm2_g,
               w_ff1=v_w_ff1, w_ff2=v_w_ff2)
    chip = 2 * lax.axis_index("x") + lax.axis_index("y")

    wb = [{k: w[k][l].astype(bf16) for k in BIG} for l in range(DEPTH)]
    ags = {0: exchange_start([(conv_w, None, False)] + [(wb[0][k], None, True) for k in BIG], "allgather_start_0", scatter=False)}
    item = lambda l, k: (l, (l == 0) + BIG.index(k))

    def landed(items, after, name):
        ag, ks = ags[items[0][0]], [k for _, k in items]
        zones = exchange_wait(ag, ks, after, name)
        halved = [t for t, k in enumerate(ks) if ag["halve"][k]]
        for t, z in zip(halved, swap_halves([zones[t] for t in halved], name.replace("wait", "pass"))):
            zones[t] = z
        return [lax.dynamic_update_slice_in_dim(z, ag["src"][k][0][None], chip, axis=0) for z, k in zip(zones, ks)]

    def whole(k, z):
        if k == "w_in":
            return w_in_from_shards(z)
        return z if k == "w_ff1" else z.reshape(-1, D_MODEL)

    g_conv, first_in = landed([(0, 0), item(0, "w_in")], x, "allgather_wait_in0")
    small = {k: w[k] for k in SMALL}
    small["conv_w"] = jnp.transpose(g_conv, (1, 2, 0, 3)).reshape(DEPTH, DN_CONV, 3 * DN_WIDTH)
    cache = {}

    def get_w(l, part, after):
        if part == "in":
            return whole("w_in", first_in if l == 0 else landed([item(l, "w_in")], after, f"allgather_wait_in{l}")[0])
        if part == "out":
            zs = landed([item(l, k) for k in ("w_out", "w_ff1", "w_ff2")], after, f"allgather_wait_rest{l}")
            token = jnp.zeros((), f32)
            if l + 1 < DEPTH:
                ags[l + 1] = exchange_start([(wb[l + 1][k], None, True) for k in BIG], f"allgather_start_{l + 1}",
                                            scatter=False, after=zs[0])
                token = ags[l + 1]["token"][0, 0]
            cache[l] = (whole("w_ff1", zs[1]), whole("w_ff2", zs[2]), token)
            return whole("w_out", zs[0])
        return cache[l]

    rs, pending = {}, []
    ids = jnp.stack([chip, lax.axis_index("c")]).astype(jnp.int32)

    def put_g(l, tag, g):
        names = [k for k in BIG if k in g]
        by_dest = [w_in_grad_to_shards(g[k]) if k == "w_in" else g[k] for k in names]
        halves = [a.reshape(N_CHIPS, 2, -1, a.shape[-1]) for a in by_dest]
        st = sibling_start(halves, f"pair_swap_start_{tag}{l}", other_half=True)
        pending.append((l, tag, names, st))
        return st["token"][0, 0]

    def sync_g(after):
        token = jnp.zeros((), f32)
        while pending:
            l, tag, names, st = pending.pop(0)
            halves, got = sibling_wait(st, after, f"pair_swap_wait_{tag}{l}")
            pair = [pair_sum(ids, a, b, f"pair_sum_{k}_{l}") for k, a, b in zip(names, halves, got)]
            rs[l, tag] = dict(exchange_start([(a, None, False) for a in pair], f"scatter_start_{tag}{l}", scatter=True), names=names)
            token = token + rs[l, tag]["token"][0, 0]
        return token

    lossp, grad_x, gsmall = local_step(x, loss_target, small, get_w, put_g, sync_g)

    def sum_group(l, tag, after):
        st = rs[l, tag]
        zones = exchange_wait(st, list(range(len(st["names"]))), after, f"scatter_wait_{tag}{l}")
        sums = [sum_partials(ids, zones[i], st["src"][i][0], f"sum_{k}_{l}") for i, k in enumerate(st["names"])]
        return sibling_start(sums, f"swap_sums_start_{tag}{l}")

    def update_group(l, tag, swap, after, prev):
        sums, others = sibling_wait(swap, after, f"swap_sums_wait_{tag}{l}")
        outs = dict(prev)
        for i, k in enumerate(rs[l, tag]["names"]):
            r2 = lambda a: a.reshape(-1, a.shape[-1])
            outs[k] = adamw(r2(w[k]), r2(mom[k]), r2(var[k]), (sums[i], others[i]), f"adamw_{k}_{l}", layer=l, prev=prev.get(k))
        return outs

    full_shapes = [(DEPTH,) + tuple(gsmall[0][k].shape) for k in SMALL]
    packed = _pack([jnp.stack([gsmall[l][k] for l in range(DEPTH)]) for k in SMALL] + [jnp.sum(lossp).reshape(1)])
    gather = gather_devices_start(packed, "small_gather_start", rs[0, "in"]["token"])

    swap_r = sum_group(1, "rest", gather["token"])
    swap_i = sum_group(1, "in", swap_r["token"])
    done = update_group(1, "rest", swap_r, swap_i["token"], {})
    done = update_group(1, "in", swap_i, done["w_ff2"][0], done)
    res = {}

    mine, parts = gather_devices_wait(gather, done["w_in"][0], "small_gather_wait")
    parts = lax.dynamic_update_slice_in_dim(parts, mine[None], 2 * chip + lax.axis_index("c"), axis=0)
    *totals, loss = _unpack(sum_devices(parts), full_shapes + [(1,)])
    loss = loss[0]
    gfull = dict(zip(SMALL, totals))
    cs = 3 * DN_WIDTH // N_CHIPS
    gfull["conv_w"] = lax.dynamic_slice_in_dim(gfull["conv_w"], chip * cs, cs, axis=2)
    gp, wp, mp, vp = (_pack([d[k] for k in SMALL]) for d in (gfull, w, mom, var))
    outs = adamw(wp, mp, vp, (gp,), "adamw_small")
    loc_shapes = [w[k].shape for k in SMALL]
    unp = [_unpack(o, loc_shapes) for o in outs]
    for i, k in enumerate(SMALL):
        res[k] = [unp[j][i] for j in range(4)]

    swap_r = sum_group(0, "rest", outs[0])
    swap_i = sum_group(0, "in", swap_r["token"])
    done = update_group(0, "rest", swap_r, swap_i["token"], done)
    done = update_group(0, "in", swap_i, done["w_ff2"][0], done)
    for k in BIG:
        res[k] = [o.reshape(w[k].shape) for o in done[k]]

    return (loss, grad_x, *[res[k][0] for k in WEIGHTS], *[res[k][1] for k in WEIGHTS], *[res[k][2] for k in WEIGHTS],
            *[res[k][3] for k in WEIGHTS])
```

```python
import functools

import jax
import jax.numpy as jnp
from jax import lax
from jax.experimental import pallas as pl
from jax.experimental.pallas import tpu as pltpu

f32 = jnp.float32
bf16 = jnp.bfloat16
SDS = jax.ShapeDtypeStruct
MESH = pl.DeviceIdType.MESH

NORM_EPS = 1e-6
D_MODEL = 1024
DEPTH = 2
DN_HEADS, DN_DIM, DN_WIDTH, DN_CONV, DN_CHUNK = 4, 128, 512, 4, 64
SB_HEADS, SB_DIM, SB_WIDTH = 4, 64, 256
SG_GROUPS, SG_DIM, SG_WIDTH, SG_CHUNK = 4, 64, 256, 128
D_FF = 4096
IN_DIM = 3336
C_QKV, C_Z, C_AB, C_SB, C_SG, IN_PAD = 0, 1536, 2048, 2304, 3072, 3584
DN_COLS = C_SB
N_CHIPS = 4

ADAM_LR, ADAM_B1, ADAM_B2, ADAM_EPS, ADAM_WD, ADAM_STEP = 0.001, 0.9, 0.999, 1e-08, 0.01, 10

VMEM_LIMIT = 56 * 1024 * 1024


def _cp(sem=None, **kw):
    if sem is not None:
        kw["dimension_semantics"] = sem
    return pltpu.CompilerParams(vmem_limit_bytes=VMEM_LIMIT, **kw)


def _split2(x):
    hi = x.astype(bf16)
    lo = (x - hi.astype(f32)).astype(bf16)
    return hi, lo


NT = (((1,), (1,)), ((), ()))
TN = (((0,), (0,)), ((), ()))
_DIMS2 = dict(nn=(((1,), (0,)), ((), ())), nt=NT, tn=TN)
_DIMS3 = dict(nn=(((2,), (1,)), ((0,), (0,))), nt=(((2,), (2,)), ((0,), (0,))), tn=(((1,), (1,)), ((0,), (0,))))


def _dg(a, b, kind):
    return lax.dot_general(a, b, (_DIMS2 if a.ndim == 2 else _DIMS3)[kind], preferred_element_type=f32)


def _pdot(a, b):
    return _dg(a, b, "nn")


def _dot_hp(a, b):
    ah, al = _split2(a)
    bh, bl = _split2(b)
    return _pdot(ah, bh) + _pdot(ah, bl) + _pdot(al, bh)


def _dot_x2c(a, m):
    lead = a.shape[:-1]
    ah, al = _split2(a.reshape(-1, a.shape[-1]))
    return (_pdot(ah, m) + _pdot(al, m)).reshape(lead + (m.shape[1],))


def _dot_cx2(m, a):
    if a.ndim == 3:
        m = jnp.broadcast_to(m, (a.shape[0],) + m.shape)
    ah, al = _split2(a)
    return _pdot(m, ah) + _pdot(m, al)


def _nt(a, b):
    return _dg(a.astype(bf16), b.astype(bf16), "nt")


def _tn(a, b):
    return _dg(a.astype(bf16), b.astype(bf16), "tn")


def _nn(a, b):
    return _dg(a.astype(bf16), b.astype(bf16), "nn")


@jax.custom_vjp
def mm(a, b):
    return _nn(a, b)


mm.defvjp(lambda a, b: (_nn(a, b), (a, b)), lambda r, g: (_nt(g, r[1]), _tn(r[0], g)))


@jax.custom_vjp
def mm_nt(a, b):
    return _nt(a, b)


mm_nt.defvjp(lambda a, b: (_nt(a, b), (a, b)), lambda r, g: (_nn(g, r[1]), _tn(g, r[0])))


@jax.custom_vjp
def mm_tn(a, b):
    return _tn(a, b)


mm_tn.defvjp(lambda a, b: (_tn(a, b), (a, b)), lambda r, g: (_nt(r[1], g), _nn(r[0], g)))


@jax.custom_vjp
def rmul_const(a, m, mt):
    return _dot_x2c(a, m)


rmul_const.defvjp(lambda a, m, mt: (_dot_x2c(a, m), (m, mt)),
                  lambda r, g: (_dot_x2c(g, r[1]), jnp.zeros_like(r[0]), jnp.zeros_like(r[1])))


@jax.custom_vjp
def lmul_const(m, mt, a):
    return _dot_cx2(m, a)


lmul_const.defvjp(lambda m, mt, a: (_dot_cx2(m, a), (m, mt)),
                  lambda r, g: (jnp.zeros_like(r[0]), jnp.zeros_like(r[1]), _dot_cx2(r[1], g)))


@jax.custom_vjp
def mm_hl(t, x):
    th, tl = _split2(t)
    xb = x.astype(bf16)
    return _pdot(th, xb) + _pdot(tl, xb)


def _mm_hl_bwd(r, g):
    t, x = r
    th, tl = _split2(t)
    gb = g.astype(bf16)
    return _nt(g, x), _dg(th, gb, "tn") + _dg(tl, gb, "tn")


mm_hl.defvjp(lambda t, x: (mm_hl(t, x), (t, x)), _mm_hl_bwd)


def inv_unit_lower(lm):
    c = lm.shape[-1]
    r, cc = _iota2((c, c))
    eye = (r == cc).astype(f32)
    t = eye - lm
    p = -lm
    k = 1
    while 2 * k < c:
        p = _nn(p, p)
        t = t + _nn(t, p)
        k *= 2
    res = eye - t - _dot_hp(lm, t)
    return t + _nn(t, res)


@jax.custom_vjp
def inv_given(lm, t):
    return t


inv_given.defvjp(lambda lm, t: (t, t), lambda t, g: (-_nt(_tn(t, g), t), jnp.zeros_like(t)))


def _sigmoid(x):
    return 1.0 / (1.0 + jnp.exp(-x))


def _softplus(x):
    return jnp.maximum(x, 0.0) + jnp.log(1.0 + jnp.exp(-jnp.abs(x)))


def _silu(x):
    return x * _sigmoid(x)


def _gelu(x):
    return 0.5 * x * (1.0 + jnp.tanh(0.7978845608028654 * (x + 0.044715 * (x * x * x))))


def _iota2(shape):
    return lax.broadcasted_iota(jnp.int32, shape, 0), lax.broadcasted_iota(jnp.int32, shape, 1)


def _group_avg_mats():
    r, c = _iota2((128, 128))
    return jnp.where((r // 64) == (c // 64), 1.0 / 64.0, 0.0).astype(bf16)


def _pair_norm(x, gain, bavg):
    ms = rmul_const(x * x, bavg, bavg)
    return x * lax.rsqrt(ms + NORM_EPS) * gain


def _rms(x):
    r = lax.rsqrt(jnp.mean(x * x, axis=-1, keepdims=True) + NORM_EPS)
    return r


_IN_GROUPS = ((C_QKV, C_Z), (C_Z, C_AB), (C_AB, C_AB + 128), (C_SB, C_SG), (C_SG, IN_PAD))


def inproj_fwd(x, g, wp, tm=256):
    m = x.shape[0]

    def body(x_ref, g_ref, w_ref, *outs):
        xv = x_ref[...]
        h = (xv * _rms(xv) * g_ref[...]).astype(bf16)
        outs[-1][...] = h
        for (a, b), o in zip(_IN_GROUPS, outs):
            o[...] = _pdot(h, w_ref[:, a:b])

    widths = [b - a for a, b in _IN_GROUPS]
    return pl.pallas_call(
        body, name="inproj_fwd", grid=(m // tm,),
        in_specs=[pl.BlockSpec((tm, D_MODEL), lambda i: (i, 0)), pl.BlockSpec((1, D_MODEL), lambda i: (0, 0)),
                  pl.BlockSpec((D_MODEL, IN_PAD), lambda i: (0, 0))],
        out_specs=[pl.BlockSpec((tm, wd), lambda i: (i, 0)) for wd in widths + [D_MODEL]],
        out_shape=[SDS((m, wd), f32) for wd in widths] + [SDS((m, D_MODEL), bf16)],
        compiler_params=_cp(("arbitrary",)),
    )(x, g, wp)


def inproj_bwd(x, g, wp, dproj, dres, tm=256):
    m = x.shape[0]

    def body(x_ref, g_ref, w_ref, dp_ref, dr_ref, dx_ref, dg_ref):
        xv = x_ref[...]
        r = _rms(xv)
        xn = xv * r
        gv = g_ref[...]
        dh = lax.dot_general(dp_ref[...], w_ref[...], NT, preferred_element_type=f32)
        dxn = dh * gv
        dx_ref[...] = dr_ref[...] + r * (dxn - xn * jnp.mean(dxn * xn, axis=-1, keepdims=True))

        @pl.when(pl.program_id(0) == 0)
        def _():
            dg_ref[...] = jnp.zeros_like(dg_ref)

        dg_ref[...] += jnp.sum(dh * xn, axis=0, keepdims=True)

    return pl.pallas_call(
        body, name="inproj_bwd", grid=(m // tm,),
        in_specs=[pl.BlockSpec((tm, D_MODEL), lambda i: (i, 0)), pl.BlockSpec((1, D_MODEL), lambda i: (0, 0)),
                  pl.BlockSpec((D_MODEL, IN_PAD), lambda i: (0, 0)), pl.BlockSpec((tm, IN_PAD), lambda i: (i, 0)),
                  pl.BlockSpec((tm, D_MODEL), lambda i: (i, 0))],
        out_specs=[pl.BlockSpec((tm, D_MODEL), lambda i: (i, 0)), pl.BlockSpec((1, D_MODEL), lambda i: (0, 0))],
        out_shape=[SDS((m, D_MODEL), f32), SDS((1, D_MODEL), f32)],
        compiler_params=_cp(("arbitrary",)),
    )(x, g, wp, dproj, dres)


def outproj_fwd(x, odn, osb, osg, wo, tm=512):
    m = x.shape[0]

    def body(x_ref, a_ref, b_ref, c_ref, w_ref, x2_ref, mix_ref):
        mix_ref[:, 0:DN_WIDTH] = a_ref[...].astype(bf16)
        mix_ref[:, DN_WIDTH:DN_WIDTH + SB_WIDTH] = b_ref[...].astype(bf16)
        mix_ref[:, DN_WIDTH + SB_WIDTH:D_MODEL] = c_ref[...].astype(bf16)
        x2_ref[...] = x_ref[...] + _pdot(mix_ref[...], w_ref[...])

    row = lambda w: pl.BlockSpec((tm, w), lambda i: (i, 0))
    return pl.pallas_call(
        body, name="outproj_fwd", grid=(m // tm,),
        in_specs=[row(D_MODEL), row(DN_WIDTH), row(SB_WIDTH), row(SG_WIDTH), pl.BlockSpec((D_MODEL, D_MODEL), lambda i: (0, 0))],
        out_specs=[row(D_MODEL), row(D_MODEL)],
        out_shape=[SDS((m, D_MODEL), f32), SDS((m, D_MODEL), bf16)],
        compiler_params=_cp(("arbitrary",)),
    )(x, odn, osb, osg, wo)


def outproj_bwd(dx2, wo, tm=512):
    m = dx2.shape[0]

    def body(d_ref, w_ref, a_ref, b_ref, c_ref, db_ref):
        db = d_ref[...].astype(bf16)
        db_ref[...] = db
        dm = lax.dot_general(db, w_ref[...], NT, preferred_element_type=f32)
        a_ref[...] = dm[:, 0:DN_WIDTH]
        b_ref[...] = dm[:, DN_WIDTH:DN_WIDTH + SB_WIDTH]
        c_ref[...] = dm[:, DN_WIDTH + SB_WIDTH:D_MODEL]

    row = lambda w: pl.BlockSpec((tm, w), lambda i: (i, 0))
    return pl.pallas_call(
        body, name="outproj_bwd", grid=(m // tm,),
        in_specs=[row(D_MODEL), pl.BlockSpec((D_MODEL, D_MODEL), lambda i: (0, 0))],
        out_specs=[row(DN_WIDTH), row(SB_WIDTH), row(SG_WIDTH), row(D_MODEL)],
        out_shape=[SDS((m, DN_WIDTH), f32), SDS((m, SB_WIDTH), f32), SDS((m, SG_WIDTH), f32), SDS((m, D_MODEL), bf16)],
        compiler_params=_cp(("arbitrary",)),
    )(dx2, wo)


FF_CHUNK = D_FF // N_CHIPS


def _load_weights_once(pairs, sem):
    @pl.when(pl.program_id(0) == 0)
    def _():
        cps = [pltpu.make_async_copy(h, v, sem.at[i]) for i, (h, v) in enumerate(pairs)]
        for c in cps:
            c.start()
        for c in cps:
            c.wait()


def ffn_fwd(x2, g, w1, w2, tm=256):
    m = x2.shape[0]

    def body(x_ref, g_ref, w1_hbm, w2_hbm, y_ref, rl_ref, w1_v, w2_v, sem):
        _load_weights_once(((w1_hbm, w1_v), (w2_hbm, w2_v)), sem)
        xv = x_ref[...]
        h = (xv * _rms(xv) * g_ref[...]).astype(bf16)
        acc = xv
        for j in range(0, D_FF, FF_CHUNK):
            f = _pdot(h, w1_v[j // FF_CHUNK])
            rl = jnp.maximum(f, 0.0)
            rl_ref[:, j:j + FF_CHUNK] = rl.astype(bf16)
            acc = acc + _pdot((rl * rl).astype(bf16), w2_v[j:j + FF_CHUNK, :])
        y_ref[...] = acc

    return pl.pallas_call(
        body, name="ffn_fwd", grid=(m // tm,),
        in_specs=[pl.BlockSpec((tm, D_MODEL), lambda i: (i, 0)), pl.BlockSpec((1, D_MODEL), lambda i: (0, 0)),
                  pl.BlockSpec(memory_space=pl.ANY), pl.BlockSpec(memory_space=pl.ANY)],
        out_specs=[pl.BlockSpec((tm, D_MODEL), lambda i: (i, 0)), pl.BlockSpec((tm, D_FF), lambda i: (i, 0))],
        out_shape=[SDS((m, D_MODEL), f32), SDS((m, D_FF), bf16)],
        scratch_shapes=[pltpu.VMEM((N_CHIPS, D_MODEL, FF_CHUNK), bf16), pltpu.VMEM((D_FF, D_MODEL), bf16), pltpu.SemaphoreType.DMA((2,))],
        compiler_params=_cp(("arbitrary",)),
    )(x2, g, w1, w2)


def ffn_bwd(x2, g, w1, w2, rlb, dy, tm=256):
    m = x2.shape[0]

    def body(x_ref, g_ref, w1_hbm, w2_hbm, rl_ref, dy_ref, dx_ref, dg_ref, h_ref, a_ref, df_ref, dyb_ref, w1_v, w2_v, sem):
        _load_weights_once(((w1_hbm, w1_v), (w2_hbm, w2_v)), sem)
        xv = x_ref[...]
        r = _rms(xv)
        xn = xv * r
        gv = g_ref[...]
        h = (xn * gv).astype(bf16)
        h_ref[...] = h
        dyv = dy_ref[...]
        dyb = dyv.astype(bf16)
        dyb_ref[...] = dyb
        dh = jnp.zeros((tm, D_MODEL), f32)
        for j in range(0, D_FF, FF_CHUNK):
            rl = rl_ref[:, j:j + FF_CHUNK].astype(f32)
            a_ref[:, j:j + FF_CHUNK] = (rl * rl).astype(bf16)
            da = lax.dot_general(dyb, w2_v[j:j + FF_CHUNK, :], NT, preferred_element_type=f32)
            df = (da * (2.0 * rl)).astype(bf16)
            df_ref[:, j:j + FF_CHUNK] = df
            dh = dh + lax.dot_general(df, w1_v[j // FF_CHUNK], NT, preferred_element_type=f32)
        dxn = dh * gv
        dx_ref[...] = dyv + r * (dxn - xn * jnp.mean(dxn * xn, axis=-1, keepdims=True))

        @pl.when(pl.program_id(0) == 0)
        def _():
            dg_ref[...] = jnp.zeros_like(dg_ref)

        dg_ref[...] += jnp.sum(dh * xn, axis=0, keepdims=True)

    row = lambda w: pl.BlockSpec((tm, w), lambda i: (i, 0))
    return pl.pallas_call(
        body, name="ffn_bwd", grid=(m // tm,),
        in_specs=[row(D_MODEL), pl.BlockSpec((1, D_MODEL), lambda i: (0, 0)),
                  pl.BlockSpec(memory_space=pl.ANY), pl.BlockSpec(memory_space=pl.ANY), row(D_FF), row(D_MODEL)],
        out_specs=[row(D_MODEL), pl.BlockSpec((1, D_MODEL), lambda i: (0, 0)), row(D_MODEL), row(D_FF), row(D_FF), row(D_MODEL)],
        out_shape=[SDS((m, D_MODEL), f32), SDS((1, D_MODEL), f32), SDS((m, D_MODEL), bf16), SDS((m, D_FF), bf16),
                   SDS((m, D_FF), bf16), SDS((m, D_MODEL), bf16)],
        scratch_shapes=[pltpu.VMEM((N_CHIPS, D_MODEL, FF_CHUNK), bf16), pltpu.VMEM((D_FF, D_MODEL), bf16), pltpu.SemaphoreType.DMA((2,))],
        compiler_params=_cp(("arbitrary",)),
    )(x2, g, w1, w2, rlb, dy)


def _tile(n, cap):
    best = 128
    for t in range(128, cap + 1, 128):
        if n % t == 0:
            best = t
    return best


def tn_matmul(a, b, name, col_shards=1, tk=2048):
    m, ka = a.shape
    n = b.shape[1]
    ti = _tile(ka, 1024)
    tj = _tile(n // col_shards, 1152)
    tk = min(tk, m)
    nk = m // tk
    jps = (n // col_shards) // tj

    def body(a_ref, b_ref, o_ref, acc):
        k = pl.program_id(2)

        @pl.when(k == 0)
        def _():
            acc[...] = jnp.zeros_like(acc)

        acc[...] += lax.dot_general(a_ref[...], b_ref[...], TN, preferred_element_type=f32)

        @pl.when(k == nk - 1)
        def _():
            o_ref[...] = acc[...].astype(bf16).reshape(o_ref.shape)

    if col_shards == 1:
        out_shape, out_spec = SDS((ka, n), bf16), pl.BlockSpec((ti, tj), lambda i, j, k: (i, j))
    else:
        out_shape = SDS((col_shards, ka, n // col_shards), bf16)
        out_spec = pl.BlockSpec((1, ti, tj), lambda i, j, k: (j // jps, i, j % jps))
    return pl.pallas_call(
        body, name=name, grid=(ka // ti, n // tj, nk),
        in_specs=[pl.BlockSpec((tk, ti), lambda i, j, k: (k, i)), pl.BlockSpec((tk, tj), lambda i, j, k: (k, j))],
        out_specs=out_spec, out_shape=out_shape,
        scratch_shapes=[pltpu.VMEM((ti, tj), f32)],
        compiler_params=_cp(("arbitrary", "arbitrary", "arbitrary")),
    )(a, b)


def loss_head(y, tgt, tm=512):
    m = y.shape[0]

    def body(y_ref, t_ref, dy_ref, l_ref):
        e = y_ref[...] - t_ref[...]
        dy_ref[...] = e * (1.0 / D_MODEL)

        @pl.when(pl.program_id(0) == 0)
        def _():
            l_ref[...] = jnp.zeros_like(l_ref)

        l_ref[...] += jnp.sum(e * e, axis=0, keepdims=True) * (0.5 / D_MODEL)

    row = pl.BlockSpec((tm, D_MODEL), lambda i: (i, 0))
    return pl.pallas_call(
        body, name="loss_head", grid=(m // tm,), in_specs=[row, row],
        out_specs=[row, pl.BlockSpec((1, D_MODEL), lambda i: (0, 0))],
        out_shape=[SDS((m, D_MODEL), f32), SDS((1, D_MODEL), f32)],
        compiler_params=_cp(("arbitrary",)),
    )(y, tgt)


def _dn_consts():
    c = DN_CHUNK
    r, cc = _iota2((c, c))
    lt = (cc <= r).astype(bf16)
    ltt = (r <= cc).astype(bf16)
    return lt, ltt


def dn_chunk(cq, ck, cv, g, beta, z, s, gain, lt, ltt, t_given=None):
    c = DN_CHUNK
    r, cc = _iota2((c, c))
    q = cq * lax.rsqrt(jnp.sum(cq * cq, axis=-1, keepdims=True) + NORM_EPS) * (DN_DIM ** -0.5)
    k = ck * lax.rsqrt(jnp.sum(ck * ck, axis=-1, keepdims=True) + NORM_EPS)
    r2, c2 = _iota2((c, 128))
    uaug = jnp.where((c2 < c) & (r2 > c2), 1.0, 0.0) + jnp.where(c2 == c, 1.0, 0.0)
    gam_all = lmul_const(lt, ltt, g * uaug)
    gam_cc = gam_all[:, :, 0:c]
    gam = gam_all[:, :, c:c + 1]
    dec = jnp.where(cc <= r, jnp.exp(jnp.where(cc <= r, gam_cc, 0.0)), 0.0)
    kk = mm_nt(k, k)
    lm = jnp.where(cc < r, beta * kk * dec, 0.0)
    t = inv_unit_lower(lm) if t_given is None else inv_given(lm, t_given)
    eg = jnp.exp(gam)
    sol = mm_hl(t, jnp.concatenate([cv * beta, k * (beta * eg)], axis=2))
    u, w = sol[:, :, 0:DN_DIM], sol[:, :, DN_DIM:2 * DN_DIM]
    qk = jnp.where(cc <= r, mm_nt(q, k) * dec, 0.0)
    glast = jnp.sum(g, axis=1, keepdims=True)
    qd = q * eg
    kd = k * jnp.exp(glast - gam)
    un = u - mm(w, s)
    o = mm(qd, s) + mm(qk, un)
    s_new = s * jnp.exp(glast) + mm_tn(kd, un)
    on = o * lax.rsqrt(jnp.mean(o * o, axis=-1, keepdims=True) + NORM_EPS) * gain * _silu(z)
    return on, s_new, t


def _dn_gates(ab, al_row, dt_row):
    pre = ab + dt_row
    return -jnp.exp(al_row) * _softplus(pre), _sigmoid(ab), _sigmoid(pre)


def _dn_chains(cacts, gates, z_ref):
    cq, ck, cv, g, beta, z = [], [], [], [], [], []
    for bi, cact in enumerate(cacts):
        for h in range(DN_HEADS):
            cq.append(cact[:, h * DN_DIM:(h + 1) * DN_DIM])
            ck.append(cact[:, DN_WIDTH + h * DN_DIM:DN_WIDTH + (h + 1) * DN_DIM])
            cv.append(cact[:, 2 * DN_WIDTH + h * DN_DIM:2 * DN_WIDTH + (h + 1) * DN_DIM])
            g.append(gates[bi][0][:, h:h + 1])
            beta.append(gates[bi][1][:, DN_HEADS + h:DN_HEADS + h + 1])
            z.append(z_ref[bi, :, h * DN_DIM:(h + 1) * DN_DIM])
    return tuple(jnp.stack(v) for v in (cq, ck, cv, g, beta, z))


def _conv_rows(xe_ref, b, w_ref):
    y = w_ref[0:1, :] * xe_ref[b, pl.ds(5, DN_CHUNK), :]
    for i in range(1, DN_CONV):
        y = y + w_ref[i:i + 1, :] * xe_ref[b, pl.ds(5 + i, DN_CHUNK), :]
    return y


def dn_fwd(qkv, z, ab, conv_w, alog, dtb, gain):
    bsz, t, _ = qkv.shape
    nc = t // DN_CHUNK
    c = DN_CHUNK
    nh = bsz * DN_HEADS

    def body(qkv_ref, z_ref, ab_ref, w_ref, al_ref, dt_ref, g_ref, o_ref, sall_ref, tall_ref, xe, s_sc):
        n = pl.program_id(0)

        @pl.when(n == 0)
        def _():
            xe[:, 0:8, :] = jnp.zeros((bsz, 8, 3 * DN_WIDTH), f32)
            s_sc[...] = jnp.zeros_like(s_sc)

        lt, ltt = _dn_consts()
        cacts = []
        for b in range(bsz):
            xe[b, 8:8 + c, :] = qkv_ref[b]
            cacts.append(_silu(_conv_rows(xe, b, w_ref)))
            xe[b, 0:8, :] = xe[b, c:c + 8, :]
        gates = [_dn_gates(ab_ref[b], al_ref[...], dt_ref[...]) for b in range(bsz)]
        s = s_sc[...]
        sall_ref[0] = s
        on, sn, tt = dn_chunk(*_dn_chains(cacts, gates, z_ref), s, g_ref[...], lt, ltt)
        tall_ref[0] = tt
        s_sc[...] = sn
        for b in range(bsz):
            for h in range(DN_HEADS):
                o_ref[b, :, h * DN_DIM:(h + 1) * DN_DIM] = on[b * DN_HEADS + h]

    blk = lambda w: pl.BlockSpec((bsz, c, w), lambda n: (0, n, 0))
    full = lambda shp: pl.BlockSpec(shp, lambda n: (0,) * len(shp))
    return pl.pallas_call(
        body, name="dn_fwd", grid=(nc,),
        in_specs=[blk(3 * DN_WIDTH), blk(DN_WIDTH), blk(128), full((8, 3 * DN_WIDTH)), full((1, 128)), full((1, 128)), full((1, 128))],
        out_specs=[blk(DN_WIDTH), pl.BlockSpec((1, nh, DN_DIM, DN_DIM), lambda n: (n, 0, 0, 0)),
                   pl.BlockSpec((1, nh, c, c), lambda n: (n, 0, 0, 0))],
        out_shape=[SDS((bsz, t, DN_WIDTH), f32), SDS((nc, nh, DN_DIM, DN_DIM), f32), SDS((nc, nh, c, c), f32)],
        scratch_shapes=[pltpu.VMEM((bsz, c + 8, 3 * DN_WIDTH), f32), pltpu.VMEM((nh, DN_DIM, DN_DIM), f32)],
        compiler_params=_cp(("arbitrary",)),
    )(qkv, z, ab, conv_w, alog, dtb, gain)


def dn_bwd(qkv, z, ab, conv_w, alog, dtb, gain, sall, tall, do):
    bsz, t, _ = qkv.shape
    nc = t // DN_CHUNK
    c = DN_CHUNK
    nh = bsz * DN_HEADS
    w3 = 3 * DN_WIDTH

    def body(qkv_ref, prev_ref, z_ref, ab_ref, w_ref, al_ref, dt_ref, g_ref, sall_ref, tall_ref, do_ref,
             dp_ref, dw_ref, dal_ref, ddt_ref, dg_ref, xe, dye, dc_sc, ds_sc):
        n = pl.program_id(0)
        first = (nc - 1 - n) == 0

        @pl.when(n == 0)
        def _():
            dye[:, c:c + 8, :] = jnp.zeros((bsz, 8, w3), f32)
            ds_sc[...] = jnp.zeros_like(ds_sc)
            dw_ref[...] = jnp.zeros_like(dw_ref)
            dal_ref[...] = jnp.zeros_like(dal_ref)
            ddt_ref[...] = jnp.zeros_like(ddt_ref)
            dg_ref[...] = jnp.zeros_like(dg_ref)

        lt, ltt = _dn_consts()
        lane_c = lax.broadcasted_iota(jnp.int32, (c, 128), 1)
        ys, sigs = [], []
        for b in range(bsz):
            xe[b, 0:8, :] = jnp.where(first, 0.0, prev_ref[b])
            xe[b, 8:8 + c, :] = qkv_ref[b]
            ys.append(_conv_rows(xe, b, w_ref))
            sigs.append(_sigmoid(ys[b]))
        gates = [_dn_gates(ab_ref[b], al_ref[...], dt_ref[...]) for b in range(bsz)]
        ops = _dn_chains([y * sg for y, sg in zip(ys, sigs)], gates, z_ref)
        tt = tall_ref[0]
        _, vjp = jax.vjp(lambda *p: dn_chunk(*p, lt, ltt, t_given=tt)[0:2], *ops, sall_ref[0], g_ref[...])
        don = jnp.stack([do_ref[b, :, h * DN_DIM:(h + 1) * DN_DIM] for b in range(bsz) for h in range(DN_HEADS)])
        dcq, dck, dcv, dg, dbeta, dzz, dsp, dgn = vjp((don, ds_sc[...]))
        ds_sc[...] = dsp
        dg_ref[...] += dgn
        for b in range(bsz):
            dgate = jnp.zeros((c, 128), f32)
            for h in range(DN_HEADS):
                i = b * DN_HEADS + h
                dc_sc[b, :, h * DN_DIM:(h + 1) * DN_DIM] = dcq[i]
                dc_sc[b, :, DN_WIDTH + h * DN_DIM:DN_WIDTH + (h + 1) * DN_DIM] = dck[i]
                dc_sc[b, :, 2 * DN_WIDTH + h * DN_DIM:2 * DN_WIDTH + (h + 1) * DN_DIM] = dcv[i]
                dp_ref[b, :, C_Z + h * DN_DIM:C_Z + (h + 1) * DN_DIM] = dzz[i].astype(bf16)
                dgate = dgate + jnp.where(lane_c == h, dg[i], 0.0) + jnp.where(lane_c == DN_HEADS + h, dbeta[i], 0.0)
            gg, beta, sig_pre = gates[b]
            is_g = lane_c < DN_HEADS
            dpre = jnp.where(is_g, dgate * (-jnp.exp(al_ref[...])) * sig_pre, 0.0)
            dp_ref[b, :, C_AB:C_AB + 128] = (dpre + jnp.where(is_g, 0.0, dgate * beta * (1.0 - beta))).astype(bf16)
            dp_ref[b, :, C_AB + 128:DN_COLS] = jnp.zeros((c, DN_COLS - C_AB - 128), bf16)
            dal_ref[...] += jnp.sum(jnp.where(is_g, dgate * gg, 0.0), axis=0, keepdims=True)
            ddt_ref[...] += jnp.sum(dpre, axis=0, keepdims=True)
            y, sig = ys[b], sigs[b]
            dy = dc_sc[b] * (sig * (1.0 + y * (1.0 - sig)))
            dye[b, 0:c, :] = dy
            dx = w_ref[3:4, :] * dy
            for i in range(DN_CONV - 1):
                dx = dx + w_ref[i:i + 1, :] * dye[b, pl.ds(3 - i, c), :]
            dp_ref[b, :, 0:w3] = dx.astype(bf16)
            for i in range(DN_CONV):
                dw_ref[i:i + 1, :] += jnp.sum(dy * xe[b, pl.ds(5 + i, c), :], axis=0, keepdims=True)
            dye[b, c:c + 8, :] = dye[b, 0:8, :]

    rev = lambda w: pl.BlockSpec((bsz, c, w), lambda n: (0, nc - 1 - n, 0))
    full = lambda shp: pl.BlockSpec(shp, lambda n: (0,) * len(shp))
    prev = pl.BlockSpec((bsz, 8, w3), lambda n: (0, jnp.maximum((nc - 1 - n) * (c // 8) - 1, 0), 0))
    return pl.pallas_call(
        body, name="dn_bwd", grid=(nc,),
        in_specs=[rev(w3), prev, rev(DN_WIDTH), rev(128), full((8, w3)), full((1, 128)), full((1, 128)), full((1, 128)),
                  pl.BlockSpec((1, nh, DN_DIM, DN_DIM), lambda n: (nc - 1 - n, 0, 0, 0)),
                  pl.BlockSpec((1, nh, c, c), lambda n: (nc - 1 - n, 0, 0, 0)), rev(DN_WIDTH)],
        out_specs=[rev(DN_COLS), full((8, w3)), full((1, 128)), full((1, 128)), full((1, 128))],
        out_shape=[SDS((bsz, t, IN_PAD), bf16), SDS((8, w3), f32), SDS((1, 128), f32), SDS((1, 128), f32), SDS((1, 128), f32)],
        scratch_shapes=[pltpu.VMEM((bsz, c + 8, w3), f32), pltpu.VMEM((bsz, c + 8, w3), f32), pltpu.VMEM((bsz, c, w3), f32),
                        pltpu.VMEM((nh, DN_DIM, DN_DIM), f32)],
        compiler_params=_cp(("arbitrary",)),
    )(qkv, qkv, z, ab, conv_w, alog, dtb, gain, sall, tall, do)


SB_TILE = 256
SB_QTILE, SB_KTILE = 256, 256
SB_PAIRS = SB_HEADS // 2


def sb_fwd(sbqkv, gq, gk):
    bsz, t, _ = sbqkv.shape
    bq = min(SB_QTILE, t)
    blk = max(min(SB_KTILE, t), bq)
    nq = t // bq
    scale = SB_DIM ** -0.5

    def body(q_ref, k_ref, v_ref, gq_ref, gk_ref, o_ref, l_ref, q2_sc, kn_sc, v_sc):
        bavg = _group_avg_mats()
        lane = lax.broadcasted_iota(jnp.int32, (1, 128), 1)
        first = lane < SB_DIM
        for p in range(SB_PAIRS):
            ls = slice(p * 128, (p + 1) * 128)
            qn = _pair_norm(q_ref[0, :, ls], gq_ref[...], bavg)
            kn_sc[p] = _pair_norm(k_ref[0, :, ls], gk_ref[...], bavg).astype(bf16)
            v_sc[p] = v_ref[0, :, ls].astype(bf16)
            q2_sc[2 * p] = jnp.where(first, qn, 0.0).astype(bf16)
            q2_sc[2 * p + 1] = jnp.where(first, 0.0, qn).astype(bf16)
        r, c = _iota2((blk, blk))
        ustrict = (r > c).astype(bf16)
        r2, c2 = _iota2((2 * bq, blk))

        def tile(q2s, ks, carry, causal):
            out = []
            for p in range(SB_PAIRS):
                acc, rr = carry[2 * p], carry[2 * p + 1]
                zz = lax.dot_general(q2s[p], kn_sc[p, pl.ds(ks, blk), :], NT, preferred_element_type=f32) * scale
                sp = _softplus(zz)
                lm = -sp if causal is None else jnp.where(causal, -sp, 0.0)
                rem = _dot_x2c(lm, ustrict)
                wgt = jnp.exp(zz - sp + rem + rr)
                if causal is not None:
                    wgt = jnp.where(causal, wgt, 0.0)
                out += [acc + _pdot(wgt.astype(bf16), v_sc[p, pl.ds(ks, blk), :]), rr + jnp.sum(lm, axis=1, keepdims=True)]
            return tuple(out)

        def qloop(qi, _):
            qs = pl.multiple_of(qi * bq, bq)
            kd = qs // blk
            causal = c2 < (r2 & (bq - 1)) + (qs - kd * blk)
            q2s = [jnp.concatenate([q2_sc[2 * p, pl.ds(qs, bq), :], q2_sc[2 * p + 1, pl.ds(qs, bq), :]], axis=0)
                   for p in range(SB_PAIRS)]
            zero = (jnp.zeros((2 * bq, 128), f32), jnp.zeros((2 * bq, 1), f32)) * SB_PAIRS
            carry = lax.fori_loop(1, kd + 1, lambda i, cr: tile(q2s, pl.multiple_of((kd - i) * blk, blk), cr, None),
                                  tile(q2s, pl.multiple_of(kd * blk, blk), zero, causal))
            for p in range(SB_PAIRS):
                acc, rr = carry[2 * p], carry[2 * p + 1]
                o_ref[0, pl.ds(qs, bq), p * 128:(p + 1) * 128] = jnp.where(first, acc[0:bq], acc[bq:2 * bq])
                l_ref[0, pl.ds(qs, bq), p * 128:(p + 1) * 128] = jnp.where(first, rr[0:bq], rr[bq:2 * bq])
            return 0

        lax.fori_loop(0, nq, qloop, 0)

    col = lambda off: pl.BlockSpec((1, t, SB_WIDTH), lambda b: (b, 0, off))
    gsp = pl.BlockSpec((1, 128), lambda b: (0, 0))
    return pl.pallas_call(
        body, name="sb_fwd", grid=(bsz,),
        in_specs=[col(0), col(1), col(2), gsp, gsp],
        out_specs=[col(0), col(0)],
        out_shape=[SDS((bsz, t, SB_WIDTH), f32), SDS((bsz, t, SB_WIDTH), f32)],
        scratch_shapes=[pltpu.VMEM((2 * SB_PAIRS, t, 128), bf16), pltpu.VMEM((SB_PAIRS, t, 128), bf16),
                        pltpu.VMEM((SB_PAIRS, t, 128), bf16)],
        compiler_params=_cp(("arbitrary",)),
    )(sbqkv, sbqkv, sbqkv, gq, gk)


def sb_bwd(sbqkv, gq, gk, ltot, do, dproj):
    bsz, t, _ = sbqkv.shape
    blk = min(SB_TILE, t)
    nq = t // blk
    scale = SB_DIM ** -0.5

    def body(q_ref, k_ref, v_ref, gq_ref, gk_ref, l_ref, do_ref, dp_in, dp_ref, dgq_ref, dgk_ref,
             q2_sc, kn_sc, v_sc, do2_sc, dqn_sc, dkn_sc, dv_sc):
        bavg = _group_avg_mats()
        lane = lax.broadcasted_iota(jnp.int32, (1, 128), 1)
        first = lane < SB_DIM
        fq = lambda x, g: _pair_norm(x, g, bavg)
        vjps = []
        for p in range(SB_PAIRS):
            ls = slice(p * 128, (p + 1) * 128)
            qn, q_vjp = jax.vjp(fq, q_ref[0, :, ls], gq_ref[...])
            kn, k_vjp = jax.vjp(fq, k_ref[0, :, ls], gk_ref[...])
            vjps.append((q_vjp, k_vjp))
            kn_sc[p] = kn.astype(bf16)
            v_sc[p] = v_ref[0, :, ls].astype(bf16)
            dov = do_ref[0, :, ls]
            q2_sc[2 * p] = jnp.where(first, qn, 0.0).astype(bf16)
            q2_sc[2 * p + 1] = jnp.where(first, 0.0, qn).astype(bf16)
            do2_sc[2 * p] = jnp.where(first, dov, 0.0).astype(bf16)
            do2_sc[2 * p + 1] = jnp.where(first, 0.0, dov).astype(bf16)
        dkn_sc[...] = jnp.zeros_like(dkn_sc)
        dv_sc[...] = jnp.zeros_like(dv_sc)
        r, c = _iota2((blk, blk))
        pincl = (r <= c).astype(bf16)
        pstrict = (r < c).astype(bf16)
        r2, c2 = _iota2((2 * blk, blk))
        causal = c2 < (r2 & (blk - 1))

        def tile(q2s, do2s, lts, ks, carry, diag):
            out = []
            for p in range(SB_PAIRS):
                dq, cs, ce = carry[3 * p:3 * p + 3]
                q2, do2 = q2s[p], do2s[p]
                kb = kn_sc[p, pl.ds(ks, blk), :]
                zz = lax.dot_general(q2, kb, NT, preferred_element_type=f32) * scale
                sp = _softplus(zz)
                lm = jnp.where(causal, -sp, 0.0) if diag else -sp
                pre = _dot_x2c(lm, pincl)
                lp = zz - sp
                wgt = jnp.exp(lp + (lts[p] - cs - pre))
                if diag:
                    wgt = jnp.where(causal, wgt, 0.0)
                dw = lax.dot_general(do2, v_sc[p, pl.ds(ks, blk), :], NT, preferred_element_type=f32)
                e = wgt * dw
                ee = ce + _dot_x2c(e, pstrict)
                sig = jnp.exp(lp)
                dz = (e * (1.0 - sig) - ee * sig) * scale
                if diag:
                    dz = jnp.where(causal, dz, 0.0)
                dz = dz.astype(bf16)
                dkn_sc[p, pl.ds(ks, blk), :] += lax.dot_general(dz, q2, TN, preferred_element_type=f32)
                dv_sc[p, pl.ds(ks, blk), :] += lax.dot_general(wgt.astype(bf16), do2, TN, preferred_element_type=f32)
                out += [dq + _pdot(dz, kb), cs + jnp.sum(lm, axis=1, keepdims=True), ce + jnp.sum(e, axis=1, keepdims=True)]
            return tuple(out)

        def qloop(qi, _):
            qs = pl.multiple_of(qi * blk, blk)
            rows = pl.ds(qs, blk)
            q2s = [jnp.concatenate([q2_sc[2 * p, rows, :], q2_sc[2 * p + 1, rows, :]], axis=0) for p in range(SB_PAIRS)]
            do2s = [jnp.concatenate([do2_sc[2 * p, rows, :], do2_sc[2 * p + 1, rows, :]], axis=0) for p in range(SB_PAIRS)]
            lts = [jnp.concatenate([l_ref[0, rows, p * 128:p * 128 + 1], l_ref[0, rows, p * 128 + SB_DIM:p * 128 + SB_DIM + 1]],
                                   axis=0) for p in range(SB_PAIRS)]
            z1 = jnp.zeros((2 * blk, 1), f32)
            carry = lax.fori_loop(0, qi, lambda kj, cr: tile(q2s, do2s, lts, pl.multiple_of(kj * blk, blk), cr, False),
                                  (jnp.zeros((2 * blk, 128), f32), z1, z1) * SB_PAIRS)
            carry = tile(q2s, do2s, lts, qs, carry, True)
            for p in range(SB_PAIRS):
                dq = carry[3 * p]
                dqn_sc[p, rows, :] = jnp.where(first, dq[0:blk], dq[blk:2 * blk])
            return 0

        lax.fori_loop(0, nq, qloop, 0)
        dgq_tot, dgk_tot = jnp.zeros((1, 128), f32), jnp.zeros((1, 128), f32)
        for p in range(SB_PAIRS):
            ls = slice(p * 128, (p + 1) * 128)
            dq_pre, dgq = vjps[p][0](dqn_sc[p])
            dk_pre, dgk = vjps[p][1](dkn_sc[p])
            dp_ref[0, :, p * 128:(p + 1) * 128] = dq_pre.astype(bf16)
            dp_ref[0, :, SB_WIDTH + p * 128:SB_WIDTH + (p + 1) * 128] = dk_pre.astype(bf16)
            dp_ref[0, :, 2 * SB_WIDTH + p * 128:2 * SB_WIDTH + (p + 1) * 128] = dv_sc[p].astype(bf16)
            dgq_tot, dgk_tot = dgq_tot + dgq, dgk_tot + dgk
        dgq_ref[0] = jnp.broadcast_to(dgq_tot, (8, 128))
        dgk_ref[0] = jnp.broadcast_to(dgk_tot, (8, 128))

    col = lambda off: pl.BlockSpec((1, t, SB_WIDTH), lambda b: (b, 0, off), pipeline_mode=pl.Buffered(1))
    gsp = pl.BlockSpec((1, 128), lambda b: (0, 0))
    gout = pl.BlockSpec((1, 8, 128), lambda b: (b, 0, 0))
    return pl.pallas_call(
        body, name="sb_bwd", grid=(bsz,),
        in_specs=[col(0), col(1), col(2), gsp, gsp, col(0), col(0), pl.BlockSpec(memory_space=pl.ANY)],
        out_specs=[pl.BlockSpec((1, t, 3 * SB_WIDTH), lambda b: (b, 0, C_SB // (3 * SB_WIDTH)), pipeline_mode=pl.Buffered(1)),
                   gout, gout],
        out_shape=[SDS(dproj.shape, bf16)] + [SDS((bsz, 8, 128), f32)] * 2,
        input_output_aliases={7: 0},
        scratch_shapes=[pltpu.VMEM((2 * SB_PAIRS, t, 128), bf16), pltpu.VMEM((SB_PAIRS, t, 128), bf16),
                        pltpu.VMEM((SB_PAIRS, t, 128), bf16), pltpu.VMEM((2 * SB_PAIRS, t, 128), bf16),
                        pltpu.VMEM((SB_PAIRS, t, 128), f32), pltpu.VMEM((SB_PAIRS, t, 128), f32), pltpu.VMEM((SB_PAIRS, t, 128), f32)],
        compiler_params=_cp(("arbitrary",)),
    )(sbqkv, sbqkv, sbqkv, gq, gk, ltot, do, dproj)


def sg_pair(u, v, gain, wa, wb, ba, bb, bavg):
    r, c = _iota2((SG_CHUNK, SG_CHUNK))
    lane = lax.broadcasted_iota(jnp.int32, (1, 128), 1)
    first = lane < SG_DIM
    vn = _pair_norm(_gelu(v), gain, bavg)
    tri = c <= r
    mixed = (mm(jnp.where(tri, wa, 0.0), jnp.where(first, vn, 0.0)) + mm(jnp.where(tri, wb, 0.0), jnp.where(first, 0.0, vn))
             + jnp.where(first, ba, bb))
    return _gelu(u) * mixed


def sg_fwd(sguv, gain, w, bt):
    bsz, t, _ = sguv.shape
    nch = t // SG_CHUNK

    def body(uv_ref, g_ref, w_ref, b_ref, o_ref):
        bavg = _group_avg_mats()
        for p in range(2):
            ls = slice(p * 128, (p + 1) * 128)
            o_ref[0, :, ls] = sg_pair(uv_ref[0, :, ls], uv_ref[0, :, SG_WIDTH + p * 128:SG_WIDTH + (p + 1) * 128], g_ref[:, ls],
                                      w_ref[2 * p], w_ref[2 * p + 1], b_ref[:, 2 * p:2 * p + 1], b_ref[:, 2 * p + 1:2 * p + 2], bavg)

    full = lambda shp: pl.BlockSpec(shp, lambda b, n: (0,) * len(shp))
    return pl.pallas_call(
        body, name="sg_fwd", grid=(bsz, nch),
        in_specs=[pl.BlockSpec((1, SG_CHUNK, 2 * SG_WIDTH), lambda b, n: (b, n, 0)), full((1, SG_WIDTH)),
                  full((SG_GROUPS, SG_CHUNK, SG_CHUNK)), full((SG_CHUNK, 128))],
        out_specs=pl.BlockSpec((1, SG_CHUNK, SG_WIDTH), lambda b, n: (b, n, 0)),
        out_shape=SDS((bsz, t, SG_WIDTH), f32),
        compiler_params=_cp(("arbitrary", "arbitrary")),
    )(sguv, gain, w, bt)


def sg_bwd(sguv, gain, w, bt, do, dproj):
    bsz, t, _ = sguv.shape
    nch = t // SG_CHUNK

    def body(uv_ref, g_ref, w_ref, b_ref, do_ref, dp_in, duv_ref, dg_ref, dw_ref, db_ref):
        @pl.when((pl.program_id(0) == 0) & (pl.program_id(1) == 0))
        def _():
            dg_ref[...] = jnp.zeros_like(dg_ref)
            dw_ref[...] = jnp.zeros_like(dw_ref)
            db_ref[...] = jnp.zeros_like(db_ref)

        bavg = _group_avg_mats()
        lane = lax.broadcasted_iota(jnp.int32, (SG_CHUNK, 128), 1)
        dbt = jnp.zeros((SG_CHUNK, 128), f32)
        for p in range(2):
            ls = slice(p * 128, (p + 1) * 128)
            vs = slice(SG_WIDTH + p * 128, SG_WIDTH + (p + 1) * 128)
            prim = (uv_ref[0, :, ls], uv_ref[0, :, vs], g_ref[:, ls], w_ref[2 * p], w_ref[2 * p + 1],
                    b_ref[:, 2 * p:2 * p + 1], b_ref[:, 2 * p + 1:2 * p + 2])
            _, vjp = jax.vjp(lambda *a: sg_pair(*a, bavg), *prim)
            du, dv, dgn, dwa, dwb, dba, dbb = vjp(do_ref[0, :, ls])
            duv_ref[0, :, ls] = du.astype(bf16)
            duv_ref[0, :, vs] = dv.astype(bf16)
            dg_ref[:, ls] += dgn
            dw_ref[2 * p] += dwa
            dw_ref[2 * p + 1] += dwb
            dbt = dbt + jnp.where(lane == 2 * p, dba, 0.0) + jnp.where(lane == 2 * p + 1, dbb, 0.0)
        db_ref[...] += dbt

    full = lambda shp: pl.BlockSpec(shp, lambda b, n: (0,) * len(shp))
    return pl.pallas_call(
        body, name="sg_bwd", grid=(bsz, nch),
        in_specs=[pl.BlockSpec((1, SG_CHUNK, 2 * SG_WIDTH), lambda b, n: (b, n, 0)), full((1, SG_WIDTH)),
                  full((SG_GROUPS, SG_CHUNK, SG_CHUNK)), full((SG_CHUNK, 128)),
                  pl.BlockSpec((1, SG_CHUNK, SG_WIDTH), lambda b, n: (b, n, 0)), pl.BlockSpec(memory_space=pl.ANY)],
        out_specs=[pl.BlockSpec((1, SG_CHUNK, 2 * SG_WIDTH), lambda b, n: (b, n, C_SG // (2 * SG_WIDTH))), full((1, SG_WIDTH)),
                   full((SG_GROUPS, SG_CHUNK, SG_CHUNK)), full((SG_CHUNK, 128))],
        out_shape=[SDS(dproj.shape, bf16), SDS((1, SG_WIDTH), f32), SDS((SG_GROUPS, SG_CHUNK, SG_CHUNK), f32),
                   SDS((SG_CHUNK, 128), f32)],
        input_output_aliases={5: 0},
        compiler_params=_cp(("arbitrary", "arbitrary")),
    )(sguv, gain, w, bt, do, dproj)


def _pad_lanes(v, n=128):
    return jnp.pad(v.reshape(1, -1), ((0, 0), (0, n - v.size)))


def _w_in_runs():
    shard, runs = IN_DIM // N_CHIPS, []
    for s in range(N_CHIPS):
        for a, b, d in ((0, 2048, 0), (2048, 2056, C_AB), (2056, IN_DIM, C_SB)):
            lo, hi = max(shard * s, a), min(shard * (s + 1), b)
            if lo < hi:
                runs.append((s, lo - shard * s, hi - shard * s, d + lo - a))
    return runs


def w_in_from_shards(zone, tr=256):
    def body(z_ref, o_ref):
        o_ref[:, C_AB:C_SB] = jnp.zeros((tr, C_SB - C_AB), zone.dtype)
        for s, a, b, d in _w_in_runs():
            o_ref[:, d:d + b - a] = z_ref[s, :, a:b]

    return pl.pallas_call(
        body, name="w_in_from_shards", grid=(D_MODEL // tr,),
        in_specs=[pl.BlockSpec((N_CHIPS, tr, IN_DIM // N_CHIPS), lambda i: (0, i, 0))],
        out_specs=pl.BlockSpec((tr, IN_PAD), lambda i: (i, 0)), out_shape=SDS((D_MODEL, IN_PAD), zone.dtype),
        compiler_params=_cp(("arbitrary",)))(zone)


def w_in_grad_to_shards(g, tr=256):
    def body(g_ref, o_ref):
        for s, a, b, d in _w_in_runs():
            o_ref[s, :, a:b] = g_ref[:, d:d + b - a]

    return pl.pallas_call(
        body, name="w_in_grad_to_shards", grid=(D_MODEL // tr,),
        in_specs=[pl.BlockSpec((tr, IN_PAD), lambda i: (i, 0))],
        out_specs=pl.BlockSpec((N_CHIPS, tr, IN_DIM // N_CHIPS), lambda i: (0, i, 0)),
        out_shape=SDS((N_CHIPS, D_MODEL, IN_DIM // N_CHIPS), g.dtype), compiler_params=_cp(("arbitrary",)))(g)


def layer_params(p, l):
    return dict(
        g1=p["norm1_g"][l].reshape(1, -1), g2=p["norm2_g"][l].reshape(1, -1),
        conv=jnp.pad(p["conv_w"][l], ((0, 4), (0, 0))), alog=_pad_lanes(p["a_log"][l]), dtb=_pad_lanes(p["dt_bias"][l]),
        dng=p["dn_out_g"][l].reshape(1, -1), gq=jnp.tile(p["sb_q_g"][l].reshape(1, -1), (1, 2)),
        gk=jnp.tile(p["sb_k_g"][l].reshape(1, -1), (1, 2)), sgg=p["sg_v_g"][l].reshape(1, -1), sgw=p["sg_w"][l],
        sgb=jnp.pad(p["sg_b"][l].T, ((0, 0), (0, 124))))


def local_step(x, tgt, small, get_w, put_g, sync_g, put_small):
    bsz, t, _ = x.shape
    m = bsz * t
    r3 = lambda a: a.reshape(bsz, t, a.shape[-1])
    r2 = lambda a: a.reshape(m, a.shape[-1])
    xs, saved, ws = x.reshape(m, D_MODEL), [], []
    for l in range(DEPTH):
        sp, w = layer_params(small, l), {}
        w["w_in"] = get_w(l, "in", xs)
        qkv, z, ab, sb, sg, h1 = inproj_fwd(xs, sp["g1"], w["w_in"])
        odn, sall, tall = dn_fwd(r3(qkv), r3(z), r3(ab), sp["conv"], sp["alog"], sp["dtb"], sp["dng"])
        osb, ltot = sb_fwd(r3(sb), sp["gq"], sp["gk"])
        osg = sg_fwd(r3(sg), sp["sgg"], sp["sgw"], sp["sgb"])
        w["w_out"] = get_w(l, "out", osg)
        x2, mix = outproj_fwd(xs, r2(odn), r2(osb), r2(osg), w["w_out"])
        w["w_ff1"], w["w_ff2"], started = get_w(l, "ff", x2)
        x3, rlb = ffn_fwd(x2, sp["g2"] + started, w["w_ff1"], w["w_ff2"])
        saved.append(dict(rlb=rlb, h1=h1, x=xs,qkv=qkv, z=z, ab=ab, sb=sb, sg=sg, sall=sall, tall=tall, ltot=ltot, mix=mix, x2=x2))
        ws.append(w)
        xs = x3
    dx, lossp = loss_head(xs, tgt.reshape(m, D_MODEL))
    gsmall = [None] * DEPTH
    token = jnp.zeros((), f32)
    for l in reversed(range(DEPTH)):
        sp, w, s = layer_params(small, l), ws[l], saved[l]
        dx2, dg2, h2, act, df, dyb = ffn_bwd(s["x2"], sp["g2"] + token, w["w_ff1"], w["w_ff2"], s["rlb"], dx)
        g_ff1 = tn_matmul(h2, df, f"dw_ff1_{l}", col_shards=N_CHIPS)
        g_ff2 = tn_matmul(act, dyb, f"dw_ff2_{l}")
        dodn, dosb, dosg, dx2b = outproj_bwd(dx2, w["w_out"])
        g_out = tn_matmul(s["mix"], dx2b, f"dw_out_{l}")
        token = token + put_g(l, "rest", dict(w_out=g_out, w_ff1=g_ff1, w_ff2=g_ff2))
        dproj, dconv, dalog, ddtb, ddng = dn_bwd(r3(s["qkv"]), r3(s["z"]), r3(s["ab"]), sp["conv"], sp["alog"], sp["dtb"],
                                                 sp["dng"] + token, s["sall"], s["tall"], r3(dodn))
        token = sync_g(ddng)
        dproj, dgq, dgk = sb_bwd(r3(s["sb"]), sp["gq"] + token, sp["gk"], s["ltot"], r3(dosb), dproj)
        dproj, dsgg, dsgw, dsgb = sg_bwd(r3(s["sg"]), sp["sgg"], sp["sgw"], sp["sgb"], r3(dosg), dproj)
        dproj = r2(dproj)
        g_in = tn_matmul(s["h1"], dproj, f"dw_in_{l}")
        token = put_g(l, "in", dict(w_in=g_in))
        dx, dg1 = inproj_bwd(s["x"], sp["g1"] + token, w["w_in"], dproj, dx2)
        token = sync_g(dg1)
        fold = lambda a: (a[:, 0, :].sum(0).reshape(2, SB_DIM)).sum(0)
        gsmall[l] = dict(norm1_g=dg1[0], conv_w=dconv[0:DN_CONV], a_log=dalog[0, 0:DN_HEADS], dt_bias=ddtb[0, 0:DN_HEADS],
                         dn_out_g=ddng[0], sb_q_g=fold(dgq), sb_k_g=fold(dgk), sg_v_g=dsgg[0], sg_w=dsgw,
                         sg_b=dsgb[:, 0:SG_GROUPS].T, norm2_g=dg2[0])
        token = token + put_small(l, gsmall[l], lossp)
    return lossp, dx.reshape(bsz, t, D_MODEL), gsmall


def _chip_peers(x, y):
    return [(1 - x, y), (x, 1 - y), (1 - x, 1 - y)]


_HBM = pl.BlockSpec(memory_space=pltpu.HBM)
_SEM = pl.BlockSpec(memory_space=pltpu.SEMAPHORE)
_EFFECT = pltpu.SideEffectType.DATAFLOW_SIDE_EFFECTING


def _hbm(a):
    return pltpu.with_memory_space_constraint(a, pltpu.HBM)


def _my_half(ref):
    half = ref.shape[0] // 2
    return ref.at[pl.ds(pl.multiple_of(lax.axis_index("c") * half, 8), half)]


def _exchange_copy(src, land, k, j, send, recv, scatter, halve, waiting):
    x, y, c = lax.axis_index("x"), lax.axis_index("y"), lax.axis_index("c")
    px, py = _chip_peers(x, y)[j]
    me, peer = 2 * x + y, 2 * px + py
    if scatter:
        src = src.at[me if waiting else peer]
    dst = land.at[peer if waiting else me]
    if halve:
        src, dst = _my_half(src), _my_half(dst)
    return pltpu.make_async_remote_copy(src_ref=src, dst_ref=dst, send_sem=send.at[3 * k + j],
                                        recv_sem=recv.at[3 * k + j], device_id=(px, py, c), device_id_type=MESH)


def exchange_start(items, name, scatter, after=None):
    arrs = []
    for a, _, _ in items:
        if not any(a is b for b in arrs):
            arrs.append(a)
    pos = [next(i for i, b in enumerate(arrs) if b is a) for a, _, _ in items]
    shapes = [a.shape if idx is None else a.shape[1:] for a, idx, _ in items]
    lands = [lax.empty(s if scatter else (N_CHIPS,) + s, a.dtype) for (a, _, _), s in zip(items, shapes)]
    na, nl = len(arrs), len(lands)
    n_in = na + nl + (after is not None)

    def body(*refs):
        ins, lnd = refs[:na], refs[na:na + nl]
        send, recv = refs[n_in], refs[n_in + 1]
        token = refs[-1]
        for k, (_, idx, halve) in enumerate(items):
            src = ins[pos[k]] if idx is None else ins[pos[k]].at[idx]
            for j in range(3):
                _exchange_copy(src, lnd[k], k, j, send, recv, scatter, halve, False).start()
        token[...] = jnp.zeros_like(token)

    sems = pltpu.SemaphoreType.DMA((3 * nl,))
    extra = [] if after is None else [after]
    out = pl.pallas_call(
        body, name=name,
        out_shape=(sems, sems, *[pltpu.HBM(a.shape, a.dtype) for a in arrs + lands], SDS((8, 128), f32)),
        in_specs=[_HBM] * (na + nl) + [pl.BlockSpec(memory_space=pl.ANY)] * len(extra),
        out_specs=(_SEM, _SEM, *[_HBM] * (na + nl), pl.BlockSpec(memory_space=pltpu.VMEM)),
        input_output_aliases={i: 2 + i for i in range(na + nl)},
        compiler_params=pltpu.CompilerParams(has_side_effects=_EFFECT),
    )(*[_hbm(a) for a in arrs + lands], *extra)
    thru = out[2:2 + na]
    return dict(send=out[0], recv=out[1], src=[(thru[pos[k]], idx) for k, (_, idx, _) in enumerate(items)],
                halve=[h for _, _, h in items], land=list(out[2 + na:2 + na + nl]), token=out[-1], scatter=scatter)


def exchange_wait(st, ks, after, name):
    arrs = []
    for k in ks:
        if not any(st["src"][k][0] is b for b in arrs):
            arrs.append(st["src"][k][0])
    pos = [next(i for i, b in enumerate(arrs) if b is st["src"][k][0]) for k in ks]
    lands = [st["land"][k] for k in ks]
    na, nl = len(arrs), len(lands)

    def body(*refs):
        ins, lnd = refs[:na], refs[na:na + nl]
        send, recv = refs[na + nl], refs[na + nl + 1]
        for t, k in enumerate(ks):
            idx = st["src"][k][1]
            src = ins[pos[t]] if idx is None else ins[pos[t]].at[idx]
            for j in range(3):
                cp = _exchange_copy(src, lnd[t], k, j, send, recv, st["scatter"], st["halve"][k], True)
                cp.wait_send()
                cp.wait_recv()

    out = pl.pallas_call(
        body, name=name, out_shape=tuple(pltpu.HBM(a.shape, a.dtype) for a in arrs + lands),
        in_specs=[_HBM] * (na + nl) + [_SEM, _SEM, pl.BlockSpec(memory_space=pl.ANY)], out_specs=tuple([_HBM] * (na + nl)),
        input_output_aliases={i: i for i in range(na + nl)},
        compiler_params=pltpu.CompilerParams(has_side_effects=_EFFECT),
    )(*arrs, *lands, st["send"], st["recv"], after)
    for k, (a, idx) in enumerate(st["src"]):
        for p, b in enumerate(arrs):
            if a is b:
                st["src"][k] = (out[p], idx)
    return list(out[na:na + nl])


def _sibling_copy(src, land, i, send, recv, other_half):
    x, y, c = lax.axis_index("x"), lax.axis_index("y"), lax.axis_index("c")
    return pltpu.make_async_remote_copy(src_ref=src.at[:, 1 - c] if other_half else src, dst_ref=land, send_sem=send.at[i],
                                        recv_sem=recv.at[i], device_id=(x, y, 1 - c), device_id_type=MESH)


def sibling_start(arrs, name, other_half=False):
    n = len(arrs)
    lands = [lax.empty((a.shape[0],) + a.shape[2:] if other_half else a.shape, a.dtype) for a in arrs]

    def body(*refs):
        ins, lnd = refs[:n], refs[n:2 * n]
        send, recv = refs[2 * n], refs[2 * n + 1]
        token = refs[-1]
        for i in range(n):
            _sibling_copy(ins[i], lnd[i], i, send, recv, other_half).start()
        token[...] = jnp.zeros_like(token)

    sems = pltpu.SemaphoreType.DMA((n,))
    out = pl.pallas_call(
        body, name=name,
        out_shape=(sems, sems, *[pltpu.HBM(a.shape, a.dtype) for a in arrs + lands], SDS((8, 128), f32)),
        in_specs=[_HBM] * (2 * n), out_specs=(_SEM, _SEM, *[_HBM] * (2 * n), pl.BlockSpec(memory_space=pltpu.VMEM)),
        input_output_aliases={i: 2 + i for i in range(2 * n)},
        compiler_params=pltpu.CompilerParams(has_side_effects=_EFFECT),
    )(*[_hbm(a) for a in arrs + lands])
    return dict(send=out[0], recv=out[1], src=list(out[2:2 + n]), land=list(out[2 + n:2 + 2 * n]), token=out[-1],
                other_half=other_half)


def sibling_wait(st, after, name):
    n = len(st["src"])

    def body(*refs):
        ins, lnd = refs[:n], refs[n:2 * n]
        send, recv = refs[2 * n], refs[2 * n + 1]
        for i in range(n):
            cp = _sibling_copy(ins[i], lnd[i], i, send, recv, st["other_half"])
            cp.wait_send()
            cp.wait_recv()

    out = pl.pallas_call(
        body, name=name, out_shape=tuple(pltpu.HBM(a.shape, a.dtype) for a in st["src"] + st["land"]),
        in_specs=[_HBM] * (2 * n) + [_SEM, _SEM, pl.BlockSpec(memory_space=pl.ANY)], out_specs=tuple([_HBM] * (2 * n)),
        input_output_aliases={i: i for i in range(2 * n)},
        compiler_params=pltpu.CompilerParams(has_side_effects=_EFFECT),
    )(*st["src"], *st["land"], st["send"], st["recv"], after)
    return list(out[:n]), list(out[n:])


def swap_halves(zones, name):
    n = len(zones)

    def body(*refs):
        outs = refs[n:2 * n]
        send, recv = refs[2 * n:]
        x, y, c = lax.axis_index("x"), lax.axis_index("y"), lax.axis_index("c")
        cps = []
        for i in range(n):
            for j, (px, py) in enumerate(_chip_peers(x, y)):
                part = _my_half(outs[i].at[2 * px + py])
                cps.append(pltpu.make_async_remote_copy(src_ref=part, dst_ref=part, send_sem=send.at[3 * i + j],
                                                        recv_sem=recv.at[3 * i + j], device_id=(x, y, 1 - c), device_id_type=MESH))
        for cp in cps:
            cp.start()
        for cp in cps:
            cp.wait_send()
            cp.wait_recv()

    any_spec = pl.BlockSpec(memory_space=pl.ANY)
    return pl.pallas_call(
        body, name=name, in_specs=[any_spec] * n, out_specs=[any_spec] * n, out_shape=[SDS(a.shape, a.dtype) for a in zones],
        input_output_aliases={i: i for i in range(n)},
        scratch_shapes=[pltpu.SemaphoreType.DMA((3 * n,)), pltpu.SemaphoreType.DMA((3 * n,))],
    )(*zones)


def _ids_spec(grid, in_specs, out_specs):
    return pltpu.PrefetchScalarGridSpec(num_scalar_prefetch=1, grid=grid, in_specs=in_specs, out_specs=out_specs)


def pair_sum(ids, a, b, name, tr=512):
    nd, _, rows, cols = a.shape
    tr = min(tr, rows)
    assert rows % tr == 0

    def body(ids_ref, a_ref, b_ref, o_ref):
        o_ref[...] = (a_ref[0].astype(f32) + b_ref[...].astype(f32)).astype(bf16)

    spec = pl.BlockSpec((1, tr, cols), lambda d, i, ids: (d, i, 0))
    return pl.pallas_call(
        body, name=name,
        grid_spec=_ids_spec((nd, rows // tr), [pl.BlockSpec((1, 1, tr, cols), lambda d, i, ids: (d, ids[1], i, 0)), spec], spec),
        out_shape=SDS((nd, rows, cols), bf16), compiler_params=_cp(("arbitrary", "arbitrary")))(ids, a, b)


N_DEV = 8


def _device_copy(src, land, r, send, recv, waiting):
    x, y, c = lax.axis_index("x"), lax.axis_index("y"), lax.axis_index("c")
    px, py, pc = (1 - x if r & 4 else x), (1 - y if r & 2 else y), (1 - c if r & 1 else c)
    me, peer = 4 * x + 2 * y + c, 4 * px + 2 * py + pc
    return pltpu.make_async_remote_copy(src_ref=src, dst_ref=land.at[peer if waiting else me], send_sem=send.at[r - 1],
                                        recv_sem=recv.at[r - 1], device_id=(px, py, pc), device_id_type=MESH)


def gather_devices_start(v, name, after):
    land = lax.empty((N_DEV,) + v.shape, v.dtype)

    def body(v_ref, land_ref, after_ref, send, recv, v_thru, land_thru, token):
        for r in range(1, N_DEV):
            _device_copy(v_ref, land_ref, r, send, recv, False).start()
        token[...] = jnp.zeros_like(token)

    sems = pltpu.SemaphoreType.DMA((N_DEV - 1,))
    out = pl.pallas_call(
        body, name=name, out_shape=(sems, sems, pltpu.HBM(v.shape, v.dtype), pltpu.HBM(land.shape, land.dtype), SDS((8, 128), f32)),
        in_specs=[_HBM, _HBM, pl.BlockSpec(memory_space=pl.ANY)],
        out_specs=(_SEM, _SEM, _HBM, _HBM, pl.BlockSpec(memory_space=pltpu.VMEM)),
        input_output_aliases={0: 2, 1: 3}, compiler_params=pltpu.CompilerParams(has_side_effects=_EFFECT),
    )(_hbm(v), _hbm(land), after)
    return dict(send=out[0], recv=out[1], src=out[2], land=out[3], token=out[4])


def gather_devices_wait(st, after, name):
    def body(v_ref, land_ref, send, recv, after_ref, v_thru, land_thru):
        for r in range(1, N_DEV):
            cp = _device_copy(v_ref, land_ref, r, send, recv, True)
            cp.wait_send()
            cp.wait_recv()

    return pl.pallas_call(
        body, name=name, out_shape=(pltpu.HBM(st["src"].shape, st["src"].dtype), pltpu.HBM(st["land"].shape, st["land"].dtype)),
        in_specs=[_HBM, _HBM, _SEM, _SEM, pl.BlockSpec(memory_space=pl.ANY)], out_specs=(_HBM, _HBM),
        input_output_aliases={0: 0, 1: 1}, compiler_params=pltpu.CompilerParams(has_side_effects=_EFFECT),
    )(st["src"], st["land"], st["send"], st["recv"], after)


def sum_devices(p, tr=256):
    _, rows, cols = p.shape
    assert rows % tr == 0

    def body(p_ref, o_ref):
        acc = p_ref[0]
        for d in range(1, N_DEV):
            acc = acc + p_ref[d]
        o_ref[...] = acc

    return pl.pallas_call(
        body, name="sum_devices", grid=(rows // tr,), in_specs=[pl.BlockSpec((N_DEV, tr, cols), lambda i: (0, i, 0))],
        out_specs=pl.BlockSpec((tr, cols), lambda i: (i, 0)), out_shape=SDS((rows, cols), f32),
        compiler_params=_cp(("arbitrary",)))(p)


def sum_partials(ids, zone, mine, name, tr=256):
    _, rows, cols = zone.shape
    tr = min(tr, rows)
    assert rows % tr == 0

    def body(ids_ref, m_ref, z1_ref, z2_ref, z3_ref, o_ref):
        o_ref[...] = ((m_ref[0].astype(f32) + z1_ref[0].astype(f32)) + z2_ref[0].astype(f32)) + z3_ref[0].astype(f32)

    slot = lambda flip: pl.BlockSpec((1, tr, cols), lambda i, ids: (ids[0] ^ flip, i, 0))
    return pl.pallas_call(
        body, name=name,
        grid_spec=_ids_spec((rows // tr,), [slot(0), slot(1), slot(2), slot(3)], pl.BlockSpec((tr, cols), lambda i, ids: (i, 0))),
        out_shape=SDS((rows, cols), f32), compiler_params=_cp(("arbitrary",)),
    )(ids, mine, zone, zone, zone)


def adamw(w, m, v, gs, name, layer=0, prev=None, tr=256):
    hrows, cols = gs[0].shape
    rows = hrows * len(gs)
    tr = min(tr, hrows)
    assert hrows % tr == 0 and w.shape[0] % rows == 0
    off, nth = layer * (rows // tr), hrows // tr

    def body(w_ref, m_ref, v_ref, *rest):
        g_ref, d_ref, mo_ref, vo_ref = rest[-4:]
        if len(gs) == 1:
            g = rest[0][...]
        else:
            g = jnp.where(pl.program_id(0) // nth == lax.axis_index("c"), rest[0][...], rest[1][...])
        mn = ADAM_B1 * m_ref[...] + (1.0 - ADAM_B1) * g
        vn = ADAM_B2 * v_ref[...] + (1.0 - ADAM_B2) * jnp.square(g)
        m_hat = mn / (1.0 - ADAM_B1 ** ADAM_STEP)
        v_hat = vn / (1.0 - ADAM_B2 ** ADAM_STEP)
        g_ref[...] = g
        d_ref[...] = -ADAM_LR * (m_hat / (jnp.sqrt(v_hat) + ADAM_EPS) + ADAM_WD * w_ref[...])
        mo_ref[...] = mn
        vo_ref[...] = vn

    loc = pl.BlockSpec((tr, cols), lambda i: (i % nth, 0))
    glob = pl.BlockSpec((tr, cols), lambda i: (off + i, 0))
    extra = [] if prev is None else list(prev)
    return pl.pallas_call(
        body, name=name, grid=(rows // tr,),
        in_specs=[glob] * 3 + [loc] * len(gs) + [pl.BlockSpec(memory_space=pl.ANY)] * len(extra),
        out_specs=[glob] * 4, out_shape=[SDS(w.shape, f32)] * 4,
        input_output_aliases={3 + len(gs) + j: j for j in range(len(extra))},
        compiler_params=_cp(("arbitrary",)),
    )(w, m, v, *gs, *extra)


BIG = ("w_in", "w_out", "w_ff1", "w_ff2")
SMALL = ("norm1_g", "conv_w", "a_log", "dt_bias", "dn_out_g", "sb_q_g", "sb_k_g", "sg_v_g", "sg_w", "sg_b", "norm2_g")
WEIGHTS = ("norm1_g", "w_in", "conv_w", "a_log", "dt_bias", "dn_out_g", "sb_q_g", "sb_k_g", "sg_v_g", "sg_w", "sg_b",
           "w_out", "norm2_g", "w_ff1", "w_ff2")


PACK_ROWS = 256


def _rows_of(shape):
    n = 1
    for d in shape:
        n *= d
    return -(-n // 1024) * 8, n


def _pack(arrs):
    parts = []
    for a in arrs:
        r, n = _rows_of(a.shape)
        parts.append(jnp.pad(a.reshape(-1), (0, r * 128 - n)).reshape(r, 128))
    rows = sum(p.shape[0] for p in parts)
    parts.append(jnp.zeros((-rows % PACK_ROWS, 128), arrs[0].dtype))
    return jnp.concatenate(parts, axis=0)


def _unpack(packed, shapes):
    out, o = [], 0
    for s in shapes:
        r, n = _rows_of(s)
        out.append(packed[o:o + r].reshape(-1)[0:n].reshape(s))
        o += r
    return out


def kernel(x, norm1_g, w_in, conv_w, a_log, dt_bias, dn_out_g, sb_q_g, sb_k_g, sg_v_g, sg_w, sg_b, w_out, norm2_g, w_ff1, w_ff2, loss_target, m_norm1_g, m_w_in, m_conv_w, m_a_log, m_dt_bias, m_dn_out_g, m_sb_q_g, m_sb_k_g, m_sg_v_g, m_sg_w, m_sg_b, m_w_out, m_norm2_g, m_w_ff1, m_w_ff2, v_norm1_g, v_w_in, v_conv_w, v_a_log, v_dt_bias, v_dn_out_g, v_sb_q_g, v_sb_k_g, v_sg_v_g, v_sg_w, v_sg_b, v_w_out, v_norm2_g, v_w_ff1, v_w_ff2):
    w = dict(norm1_g=norm1_g, w_in=w_in, conv_w=conv_w, a_log=a_log, dt_bias=dt_bias, dn_out_g=dn_out_g, sb_q_g=sb_q_g,
             sb_k_g=sb_k_g, sg_v_g=sg_v_g, sg_w=sg_w, sg_b=sg_b, w_out=w_out, norm2_g=norm2_g, w_ff1=w_ff1, w_ff2=w_ff2)
    mom = dict(norm1_g=m_norm1_g, w_in=m_w_in, conv_w=m_conv_w, a_log=m_a_log, dt_bias=m_dt_bias, dn_out_g=m_dn_out_g,
               sb_q_g=m_sb_q_g, sb_k_g=m_sb_k_g, sg_v_g=m_sg_v_g, sg_w=m_sg_w, sg_b=m_sg_b, w_out=m_w_out, norm2_g=m_norm2_g,
               w_ff1=m_w_ff1, w_ff2=m_w_ff2)
    var = dict(norm1_g=v_norm1_g, w_in=v_w_in, conv_w=v_conv_w, a_log=v_a_log, dt_bias=v_dt_bias, dn_out_g=v_dn_out_g,
               sb_q_g=v_sb_q_g, sb_k_g=v_sb_k_g, sg_v_g=v_sg_v_g, sg_w=v_sg_w, sg_b=v_sg_b, w_out=v_w_out, norm2_g=v_norm2_g,
               w_ff1=v_w_ff1, w_ff2=v_w_ff2)
    chip = 2 * lax.axis_index("x") + lax.axis_index("y")

    wb = [{k: w[k][l].astype(bf16) for k in BIG} for l in range(DEPTH)]
    ags = {0: exchange_start([(conv_w, None, False)] + [(wb[0][k], None, True) for k in BIG], "allgather_start_0", scatter=False)}
    item = lambda l, k: (l, (l == 0) + BIG.index(k))

    def landed(items, after, name):
        ag, ks = ags[items[0][0]], [k for _, k in items]
        zones = exchange_wait(ag, ks, after, name)
        halved = [t for t, k in enumerate(ks) if ag["halve"][k]]
        for t, z in zip(halved, swap_halves([zones[t] for t in halved], name.replace("wait", "pass"))):
            zones[t] = z
        return [lax.dynamic_update_slice_in_dim(z, ag["src"][k][0][None], chip, axis=0) for z, k in zip(zones, ks)]

    def whole(k, z):
        if k == "w_in":
            return w_in_from_shards(z)
        return z if k == "w_ff1" else z.reshape(-1, D_MODEL)

    g_conv, first_in = landed([(0, 0), item(0, "w_in")], x, "allgather_wait_in0")
    small = {k: w[k] for k in SMALL}
    small["conv_w"] = jnp.transpose(g_conv, (1, 2, 0, 3)).reshape(DEPTH, DN_CONV, 3 * DN_WIDTH)
    cache = {}

    def get_w(l, part, after):
        if part == "in":
            return whole("w_in", first_in if l == 0 else landed([item(l, "w_in")], after, f"allgather_wait_in{l}")[0])
        if part == "out":
            zs = landed([item(l, k) for k in ("w_out", "w_ff1", "w_ff2")], after, f"allgather_wait_rest{l}")
            token = jnp.zeros((), f32)
            if l + 1 < DEPTH:
                ags[l + 1] = exchange_start([(wb[l + 1][k], None, True) for k in BIG], f"allgather_start_{l + 1}",
                                            scatter=False, after=zs[0])
                token = ags[l + 1]["token"][0, 0]
            cache[l] = (whole("w_ff1", zs[1]), whole("w_ff2", zs[2]), token)
            return whole("w_out", zs[0])
        return cache[l]

    rs, pending = {}, []
    ids = jnp.stack([chip, lax.axis_index("c")]).astype(jnp.int32)

    def put_g(l, tag, g):
        names = [k for k in BIG if k in g]
        by_dest = [w_in_grad_to_shards(g[k]) if k == "w_in" else g[k] for k in names]
        halves = [a.reshape(N_CHIPS, 2, -1, a.shape[-1]) for a in by_dest]
        st = sibling_start(halves, f"pair_swap_start_{tag}{l}", other_half=True)
        pending.append((l, tag, names, st))
        return st["token"][0, 0]

    def sync_g(after):
        token = jnp.zeros((), f32)
        while pending:
            l, tag, names, st = pending.pop(0)
            halves, got = sibling_wait(st, after, f"pair_swap_wait_{tag}{l}")
            pair = [pair_sum(ids, a, b, f"pair_sum_{k}_{l}") for k, a, b in zip(names, halves, got)]
            rs[l, tag] = dict(exchange_start([(a, None, False) for a in pair], f"scatter_start_{tag}{l}", scatter=True), names=names)
            token = token + rs[l, tag]["token"][0, 0]
        return token

    gathers, layer_shapes = {}, {}

    def put_small(l, g, lossp):
        arrs = [g[k] for k in SMALL] + ([jnp.sum(lossp).reshape(1)] if l == DEPTH - 1 else [])
        layer_shapes[l] = [a.shape for a in arrs]
        gathers[l] = gather_devices_start(_pack(arrs), f"small_gather_start_{l}", rs[l, "in"]["token"])
        return gathers[l]["token"][0, 0]

    lossp, grad_x, gsmall = local_step(x, loss_target, small, get_w, put_g, sync_g, put_small)

    def sum_group(l, tag, after):
        st = rs[l, tag]
        zones = exchange_wait(st, list(range(len(st["names"]))), after, f"scatter_wait_{tag}{l}")
        sums = [sum_partials(ids, zones[i], st["src"][i][0], f"sum_{k}_{l}") for i, k in enumerate(st["names"])]
        return sibling_start(sums, f"swap_sums_start_{tag}{l}")

    def update_group(l, tag, swap, after, prev):
        sums, others = sibling_wait(swap, after, f"swap_sums_wait_{tag}{l}")
        outs = dict(prev)
        for i, k in enumerate(rs[l, tag]["names"]):
            r2 = lambda a: a.reshape(-1, a.shape[-1])
            outs[k] = adamw(r2(w[k]), r2(mom[k]), r2(var[k]), (sums[i], others[i]), f"adamw_{k}_{l}", layer=l, prev=prev.get(k))
        return outs

    swap_r = sum_group(1, "rest", gathers[0]["token"])
    swap_i = sum_group(1, "in", swap_r["token"])
    done = update_group(1, "rest", swap_r, swap_i["token"], {})
    done = update_group(1, "in", swap_i, done["w_ff2"][0], done)
    res = {}

    totals = {}
    for l in reversed(range(DEPTH)):
        mine, parts = gather_devices_wait(gathers[l], done["w_in"][0], f"small_gather_wait_{l}")
        parts = lax.dynamic_update_slice_in_dim(parts, mine[None], 2 * chip + lax.axis_index("c"), axis=0)
        totals[l] = _unpack(sum_devices(parts), layer_shapes[l])
    loss = totals[DEPTH - 1][-1][0]
    gfull = {k: jnp.stack([totals[l][i] for l in range(DEPTH)]) for i, k in enumerate(SMALL)}
    cs = 3 * DN_WIDTH // N_CHIPS
    gfull["conv_w"] = lax.dynamic_slice_in_dim(gfull["conv_w"], chip * cs, cs, axis=2)
    gp, wp, mp, vp = (_pack([d[k] for k in SMALL]) for d in (gfull, w, mom, var))
    outs = adamw(wp, mp, vp, (gp,), "adamw_small")
    loc_shapes = [w[k].shape for k in SMALL]
    unp = [_unpack(o, loc_shapes) for o in outs]
    for i, k in enumerate(SMALL):
        res[k] = [unp[j][i] for j in range(4)]

    swap_r = sum_group(0, "rest", outs[0])
    swap_i = sum_group(0, "in", swap_r["token"])
    done = update_group(0, "rest", swap_r, swap_i["token"], done)
    done = update_group(0, "in", swap_i, done["w_ff2"][0], done)
    for k in BIG:
        res[k] = [o.reshape(w[k].shape) for o in done[k]]

    return (loss, grad_x, *[res[k][0] for k in WEIGHTS], *[res[k][1] for k in WEIGHTS], *[res[k][2] for k in WEIGHTS],
            *[res[k][3] for k in WEIGHTS])
```

```python
import functools

import jax
import jax.numpy as jnp
from jax import lax
from jax.experimental import pallas as pl
from jax.experimental.pallas import tpu as pltpu

f32 = jnp.float32
bf16 = jnp.bfloat16
SDS = jax.ShapeDtypeStruct
MESH = pl.DeviceIdType.MESH

NORM_EPS = 1e-6
D_MODEL = 1024
DEPTH = 2
DN_HEADS, DN_DIM, DN_WIDTH, DN_CONV, DN_CHUNK = 4, 128, 512, 4, 64
SB_HEADS, SB_DIM, SB_WIDTH = 4, 64, 256
SG_GROUPS, SG_DIM, SG_WIDTH, SG_CHUNK = 4, 64, 256, 128
D_FF = 4096
IN_DIM = 3336
C_QKV, C_Z, C_AB, C_SB, C_SG, IN_PAD = 0, 1536, 2048, 2304, 3072, 3584
DN_COLS = C_SB
N_CHIPS = 4

ADAM_LR, ADAM_B1, ADAM_B2, ADAM_EPS, ADAM_WD, ADAM_STEP = 0.001, 0.9, 0.999, 1e-08, 0.01, 10

VMEM_LIMIT = 56 * 1024 * 1024


def _cp(sem=None, **kw):
    if sem is not None:
        kw["dimension_semantics"] = sem
    return pltpu.CompilerParams(vmem_limit_bytes=VMEM_LIMIT, **kw)


def _split2(x):
    hi = x.astype(bf16)
    lo = (x - hi.astype(f32)).astype(bf16)
    return hi, lo


NT = (((1,), (1,)), ((), ()))
TN = (((0,), (0,)), ((), ()))
_DIMS2 = dict(nn=(((1,), (0,)), ((), ())), nt=NT, tn=TN)
_DIMS3 = dict(nn=(((2,), (1,)), ((0,), (0,))), nt=(((2,), (2,)), ((0,), (0,))), tn=(((1,), (1,)), ((0,), (0,))))


def _dg(a, b, kind):
    return lax.dot_general(a, b, (_DIMS2 if a.ndim == 2 else _DIMS3)[kind], preferred_element_type=f32)


def _pdot(a, b):
    return _dg(a, b, "nn")


def _dot_hp(a, b):
    ah, al = _split2(a)
    bh, bl = _split2(b)
    return _pdot(ah, bh) + _pdot(ah, bl) + _pdot(al, bh)


def _dot_x2c(a, m):
    lead = a.shape[:-1]
    ah, al = _split2(a.reshape(-1, a.shape[-1]))
    return (_pdot(ah, m) + _pdot(al, m)).reshape(lead + (m.shape[1],))


def _dot_cx2(m, a):
    if a.ndim == 3:
        m = jnp.broadcast_to(m, (a.shape[0],) + m.shape)
    ah, al = _split2(a)
    return _pdot(m, ah) + _pdot(m, al)


def _nt(a, b):
    return _dg(a.astype(bf16), b.astype(bf16), "nt")


def _tn(a, b):
    return _dg(a.astype(bf16), b.astype(bf16), "tn")


def _nn(a, b):
    return _dg(a.astype(bf16), b.astype(bf16), "nn")


@jax.custom_vjp
def mm(a, b):
    return _nn(a, b)


mm.defvjp(lambda a, b: (_nn(a, b), (a, b)), lambda r, g: (_nt(g, r[1]), _tn(r[0], g)))


@jax.custom_vjp
def mm_nt(a, b):
    return _nt(a, b)


mm_nt.defvjp(lambda a, b: (_nt(a, b), (a, b)), lambda r, g: (_nn(g, r[1]), _tn(g, r[0])))


@jax.custom_vjp
def mm_tn(a, b):
    return _tn(a, b)


mm_tn.defvjp(lambda a, b: (_tn(a, b), (a, b)), lambda r, g: (_nt(r[1], g), _nn(r[0], g)))


@jax.custom_vjp
def rmul_const(a, m, mt):
    return _dot_x2c(a, m)


rmul_const.defvjp(lambda a, m, mt: (_dot_x2c(a, m), (m, mt)),
                  lambda r, g: (_dot_x2c(g, r[1]), jnp.zeros_like(r[0]), jnp.zeros_like(r[1])))


@jax.custom_vjp
def lmul_const(m, mt, a):
    return _dot_cx2(m, a)


lmul_const.defvjp(lambda m, mt, a: (_dot_cx2(m, a), (m, mt)),
                  lambda r, g: (jnp.zeros_like(r[0]), jnp.zeros_like(r[1]), _dot_cx2(r[1], g)))


@jax.custom_vjp
def mm_hl(t, x):
    th, tl = _split2(t)
    xb = x.astype(bf16)
    return _pdot(th, xb) + _pdot(tl, xb)


def _mm_hl_bwd(r, g):
    t, x = r
    th, tl = _split2(t)
    gb = g.astype(bf16)
    return _nt(g, x), _dg(th, gb, "tn") + _dg(tl, gb, "tn")


mm_hl.defvjp(lambda t, x: (mm_hl(t, x), (t, x)), _mm_hl_bwd)


def inv_unit_lower(lm):
    c = lm.shape[-1]
    r, cc = _iota2((c, c))
    eye = (r == cc).astype(f32)
    t = eye - lm
    p = -lm
    k = 1
    while 2 * k < c:
        p = _nn(p, p)
        t = t + _nn(t, p)
        k *= 2
    res = eye - t - _dot_hp(lm, t)
    return t + _nn(t, res)


@jax.custom_vjp
def inv_given(lm, t):
    return t


inv_given.defvjp(lambda lm, t: (t, t), lambda t, g: (-_nt(_tn(t, g), t), jnp.zeros_like(t)))


def _sigmoid(x):
    return 1.0 / (1.0 + jnp.exp(-x))


def _softplus(x):
    return jnp.maximum(x, 0.0) + jnp.log(1.0 + jnp.exp(-jnp.abs(x)))


def _silu(x):
    return x * _sigmoid(x)


def _gelu(x):
    return 0.5 * x * (1.0 + jnp.tanh(0.7978845608028654 * (x + 0.044715 * (x * x * x))))


def _iota2(shape):
    return lax.broadcasted_iota(jnp.int32, shape, 0), lax.broadcasted_iota(jnp.int32, shape, 1)


def _group_avg_mats():
    r, c = _iota2((128, 128))
    return jnp.where((r // 64) == (c // 64), 1.0 / 64.0, 0.0).astype(bf16)


def _pair_norm(x, gain, bavg):
    ms = rmul_const(x * x, bavg, bavg)
    return x * lax.rsqrt(ms + NORM_EPS) * gain


def _rms(x):
    r = lax.rsqrt(jnp.mean(x * x, axis=-1, keepdims=True) + NORM_EPS)
    return r


_IN_GROUPS = ((C_QKV, C_Z), (C_Z, C_AB), (C_AB, C_AB + 128), (C_SB, C_SG), (C_SG, IN_PAD))


def inproj_fwd(x, g, wp, tm=256):
    m = x.shape[0]

    def body(x_ref, g_ref, w_ref, *outs):
        xv = x_ref[...]
        h = (xv * _rms(xv) * g_ref[...]).astype(bf16)
        outs[-1][...] = h
        for (a, b), o in zip(_IN_GROUPS, outs):
            o[...] = _pdot(h, w_ref[:, a:b])

    widths = [b - a for a, b in _IN_GROUPS]
    return pl.pallas_call(
        body, name="inproj_fwd", grid=(m // tm,),
        in_specs=[pl.BlockSpec((tm, D_MODEL), lambda i: (i, 0)), pl.BlockSpec((1, D_MODEL), lambda i: (0, 0)),
                  pl.BlockSpec((D_MODEL, IN_PAD), lambda i: (0, 0))],
        out_specs=[pl.BlockSpec((tm, wd), lambda i: (i, 0)) for wd in widths + [D_MODEL]],
        out_shape=[SDS((m, wd), f32) for wd in widths] + [SDS((m, D_MODEL), bf16)],
        compiler_params=_cp(("arbitrary",)),
    )(x, g, wp)


def inproj_bwd(x, g, wp, dproj, dres, tm=256):
    m = x.shape[0]

    def body(x_ref, g_ref, w_ref, dp_ref, dr_ref, dx_ref, dg_ref):
        xv = x_ref[...]
        r = _rms(xv)
        xn = xv * r
        gv = g_ref[...]
        dh = lax.dot_general(dp_ref[...], w_ref[...], NT, preferred_element_type=f32)
        dxn = dh * gv
        dx_ref[...] = dr_ref[...] + r * (dxn - xn * jnp.mean(dxn * xn, axis=-1, keepdims=True))

        @pl.when(pl.program_id(0) == 0)
        def _():
            dg_ref[...] = jnp.zeros_like(dg_ref)

        dg_ref[...] += jnp.sum(dh * xn, axis=0, keepdims=True)

    return pl.pallas_call(
        body, name="inproj_bwd", grid=(m // tm,),
        in_specs=[pl.BlockSpec((tm, D_MODEL), lambda i: (i, 0)), pl.BlockSpec((1, D_MODEL), lambda i: (0, 0)),
                  pl.BlockSpec((D_MODEL, IN_PAD), lambda i: (0, 0)), pl.BlockSpec((tm, IN_PAD), lambda i: (i, 0)),
                  pl.BlockSpec((tm, D_MODEL), lambda i: (i, 0))],
        out_specs=[pl.BlockSpec((tm, D_MODEL), lambda i: (i, 0)), pl.BlockSpec((1, D_MODEL), lambda i: (0, 0))],
        out_shape=[SDS((m, D_MODEL), f32), SDS((1, D_MODEL), f32)],
        compiler_params=_cp(("arbitrary",)),
    )(x, g, wp, dproj, dres)


def outproj_fwd(x, odn, osb, osg, wo, tm=512):
    m = x.shape[0]

    def body(x_ref, a_ref, b_ref, c_ref, w_ref, x2_ref, mix_ref):
        mix_ref[:, 0:DN_WIDTH] = a_ref[...].astype(bf16)
        mix_ref[:, DN_WIDTH:DN_WIDTH + SB_WIDTH] = b_ref[...].astype(bf16)
        mix_ref[:, DN_WIDTH + SB_WIDTH:D_MODEL] = c_ref[...].astype(bf16)
        x2_ref[...] = x_ref[...] + _pdot(mix_ref[...], w_ref[...])

    row = lambda w: pl.BlockSpec((tm, w), lambda i: (i, 0))
    return pl.pallas_call(
        body, name="outproj_fwd", grid=(m // tm,),
        in_specs=[row(D_MODEL), row(DN_WIDTH), row(SB_WIDTH), row(SG_WIDTH), pl.BlockSpec((D_MODEL, D_MODEL), lambda i: (0, 0))],
        out_specs=[row(D_MODEL), row(D_MODEL)],
        out_shape=[SDS((m, D_MODEL), f32), SDS((m, D_MODEL), bf16)],
        compiler_params=_cp(("arbitrary",)),
    )(x, odn, osb, osg, wo)


def outproj_bwd(dx2, wo, tm=512):
    m = dx2.shape[0]

    def body(d_ref, w_ref, a_ref, b_ref, c_ref, db_ref):
        db = d_ref[...].astype(bf16)
        db_ref[...] = db
        dm = lax.dot_general(db, w_ref[...], NT, preferred_element_type=f32)
        a_ref[...] = dm[:, 0:DN_WIDTH]
        b_ref[...] = dm[:, DN_WIDTH:DN_WIDTH + SB_WIDTH]
        c_ref[...] = dm[:, DN_WIDTH + SB_WIDTH:D_MODEL]

    row = lambda w: pl.BlockSpec((tm, w), lambda i: (i, 0))
    return pl.pallas_call(
        body, name="outproj_bwd", grid=(m // tm,),
        in_specs=[row(D_MODEL), pl.BlockSpec((D_MODEL, D_MODEL), lambda i: (0, 0))],
        out_specs=[row(DN_WIDTH), row(SB_WIDTH), row(SG_WIDTH), row(D_MODEL)],
        out_shape=[SDS((m, DN_WIDTH), f32), SDS((m, SB_WIDTH), f32), SDS((m, SG_WIDTH), f32), SDS((m, D_MODEL), bf16)],
        compiler_params=_cp(("arbitrary",)),
    )(dx2, wo)


FF_CHUNK = D_FF // N_CHIPS


def _load_weights_once(pairs, sem):
    @pl.when(pl.program_id(0) == 0)
    def _():
        cps = [pltpu.make_async_copy(h, v, sem.at[i]) for i, (h, v) in enumerate(pairs)]
        for c in cps:
            c.start()
        for c in cps:
            c.wait()


def ffn_fwd(x2, g, w1, w2, tm=256):
    m = x2.shape[0]

    def body(x_ref, g_ref, w1_hbm, w2_hbm, y_ref, rl_ref, w1_v, w2_v, sem):
        _load_weights_once(((w1_hbm, w1_v), (w2_hbm, w2_v)), sem)
        xv = x_ref[...]
        h = (xv * _rms(xv) * g_ref[...]).astype(bf16)
        acc = xv
        for j in range(0, D_FF, FF_CHUNK):
            f = _pdot(h, w1_v[j // FF_CHUNK])
            rl = jnp.maximum(f, 0.0)
            rl_ref[:, j:j + FF_CHUNK] = rl.astype(bf16)
            acc = acc + _pdot((rl * rl).astype(bf16), w2_v[j:j + FF_CHUNK, :])
        y_ref[...] = acc

    return pl.pallas_call(
        body, name="ffn_fwd", grid=(m // tm,),
        in_specs=[pl.BlockSpec((tm, D_MODEL), lambda i: (i, 0)), pl.BlockSpec((1, D_MODEL), lambda i: (0, 0)),
                  pl.BlockSpec(memory_space=pl.ANY), pl.BlockSpec(memory_space=pl.ANY)],
        out_specs=[pl.BlockSpec((tm, D_MODEL), lambda i: (i, 0)), pl.BlockSpec((tm, D_FF), lambda i: (i, 0))],
        out_shape=[SDS((m, D_MODEL), f32), SDS((m, D_FF), bf16)],
        scratch_shapes=[pltpu.VMEM((N_CHIPS, D_MODEL, FF_CHUNK), bf16), pltpu.VMEM((D_FF, D_MODEL), bf16), pltpu.SemaphoreType.DMA((2,))],
        compiler_params=_cp(("arbitrary",)),
    )(x2, g, w1, w2)


def ffn_bwd(x2, g, w1, w2, rlb, dy, tm=256):
    m = x2.shape[0]

    def body(x_ref, g_ref, w1_hbm, w2_hbm, rl_ref, dy_ref, dx_ref, dg_ref, h_ref, a_ref, df_ref, dyb_ref, w1_v, w2_v, sem):
        _load_weights_once(((w1_hbm, w1_v), (w2_hbm, w2_v)), sem)
        xv = x_ref[...]
        r = _rms(xv)
        xn = xv * r
        gv = g_ref[...]
        h = (xn * gv).astype(bf16)
        h_ref[...] = h
        dyv = dy_ref[...]
        dyb = dyv.astype(bf16)
        dyb_ref[...] = dyb
        dh = jnp.zeros((tm, D_MODEL), f32)
        for j in range(0, D_FF, FF_CHUNK):
            rl = rl_ref[:, j:j + FF_CHUNK].astype(f32)
            a_ref[:, j:j + FF_CHUNK] = (rl * rl).astype(bf16)
            da = lax.dot_general(dyb, w2_v[j:j + FF_CHUNK, :], NT, preferred_element_type=f32)
            df = (da * (2.0 * rl)).astype(bf16)
            df_ref[:, j:j + FF_CHUNK] = df
            dh = dh + lax.dot_general(df, w1_v[j // FF_CHUNK], NT, preferred_element_type=f32)
        dxn = dh * gv
        dx_ref[...] = dyv + r * (dxn - xn * jnp.mean(dxn * xn, axis=-1, keepdims=True))

        @pl.when(pl.program_id(0) == 0)
        def _():
            dg_ref[...] = jnp.zeros_like(dg_ref)

        dg_ref[...] += jnp.sum(dh * xn, axis=0, keepdims=True)

    row = lambda w: pl.BlockSpec((tm, w), lambda i: (i, 0))
    return pl.pallas_call(
        body, name="ffn_bwd", grid=(m // tm,),
        in_specs=[row(D_MODEL), pl.BlockSpec((1, D_MODEL), lambda i: (0, 0)),
                  pl.BlockSpec(memory_space=pl.ANY), pl.BlockSpec(memory_space=pl.ANY), row(D_FF), row(D_MODEL)],
        out_specs=[row(D_MODEL), pl.BlockSpec((1, D_MODEL), lambda i: (0, 0)), row(D_MODEL), row(D_FF), row(D_FF), row(D_MODEL)],
        out_shape=[SDS((m, D_MODEL), f32), SDS((1, D_MODEL), f32), SDS((m, D_MODEL), bf16), SDS((m, D_FF), bf16),
                   SDS((m, D_FF), bf16), SDS((m, D_MODEL), bf16)],
        scratch_shapes=[pltpu.VMEM((N_CHIPS, D_MODEL, FF_CHUNK), bf16), pltpu.VMEM((D_FF, D_MODEL), bf16), pltpu.SemaphoreType.DMA((2,))],
        compiler_params=_cp(("arbitrary",)),
    )(x2, g, w1, w2, rlb, dy)


def _tile(n, cap):
    best = 128
    for t in range(128, cap + 1, 128):
        if n % t == 0:
            best = t
    return best


def tn_matmul(a, b, name, col_shards=1, tk=2048):
    m, ka = a.shape
    n = b.shape[1]
    ti = _tile(ka, 1024)
    tj = _tile(n // col_shards, 1152)
    tk = min(tk, m)
    nk = m // tk
    jps = (n // col_shards) // tj

    def body(a_ref, b_ref, o_ref, acc):
        k = pl.program_id(2)

        @pl.when(k == 0)
        def _():
            acc[...] = jnp.zeros_like(acc)

        acc[...] += lax.dot_general(a_ref[...], b_ref[...], TN, preferred_element_type=f32)

        @pl.when(k == nk - 1)
        def _():
            o_ref[...] = acc[...].astype(bf16).reshape(o_ref.shape)

    if col_shards == 1:
        out_shape, out_spec = SDS((ka, n), bf16), pl.BlockSpec((ti, tj), lambda i, j, k: (i, j))
    else:
        out_shape = SDS((col_shards, ka, n // col_shards), bf16)
        out_spec = pl.BlockSpec((1, ti, tj), lambda i, j, k: (j // jps, i, j % jps))
    return pl.pallas_call(
        body, name=name, grid=(ka // ti, n // tj, nk),
        in_specs=[pl.BlockSpec((tk, ti), lambda i, j, k: (k, i)), pl.BlockSpec((tk, tj), lambda i, j, k: (k, j))],
        out_specs=out_spec, out_shape=out_shape,
        scratch_shapes=[pltpu.VMEM((ti, tj), f32)],
        compiler_params=_cp(("arbitrary", "arbitrary", "arbitrary")),
    )(a, b)


def loss_head(y, tgt, tm=512):
    m = y.shape[0]

    def body(y_ref, t_ref, dy_ref, l_ref):
        e = y_ref[...] - t_ref[...]
        dy_ref[...] = e * (1.0 / D_MODEL)

        @pl.when(pl.program_id(0) == 0)
        def _():
            l_ref[...] = jnp.zeros_like(l_ref)

        l_ref[...] += jnp.sum(e * e, axis=0, keepdims=True) * (0.5 / D_MODEL)

    row = pl.BlockSpec((tm, D_MODEL), lambda i: (i, 0))
    return pl.pallas_call(
        body, name="loss_head", grid=(m // tm,), in_specs=[row, row],
        out_specs=[row, pl.BlockSpec((1, D_MODEL), lambda i: (0, 0))],
        out_shape=[SDS((m, D_MODEL), f32), SDS((1, D_MODEL), f32)],
        compiler_params=_cp(("arbitrary",)),
    )(y, tgt)


def _dn_consts():
    c = DN_CHUNK
    r, cc = _iota2((c, c))
    lt = (cc <= r).astype(bf16)
    ltt = (r <= cc).astype(bf16)
    return lt, ltt


def dn_chunk(cq, ck, cv, g, beta, z, s, gain, lt, ltt, t_given=None):
    c = DN_CHUNK
    r, cc = _iota2((c, c))
    q = cq * lax.rsqrt(jnp.sum(cq * cq, axis=-1, keepdims=True) + NORM_EPS) * (DN_DIM ** -0.5)
    k = ck * lax.rsqrt(jnp.sum(ck * ck, axis=-1, keepdims=True) + NORM_EPS)
    r2, c2 = _iota2((c, 128))
    uaug = jnp.where((c2 < c) & (r2 > c2), 1.0, 0.0) + jnp.where(c2 == c, 1.0, 0.0)
    gam_all = lmul_const(lt, ltt, g * uaug)
    gam_cc = gam_all[:, :, 0:c]
    gam = gam_all[:, :, c:c + 1]
    dec = jnp.where(cc <= r, jnp.exp(jnp.where(cc <= r, gam_cc, 0.0)), 0.0)
    kk = mm_nt(k, k)
    lm = jnp.where(cc < r, beta * kk * dec, 0.0)
    t = inv_unit_lower(lm) if t_given is None else inv_given(lm, t_given)
    eg = jnp.exp(gam)
    sol = mm_hl(t, jnp.concatenate([cv * beta, k * (beta * eg)], axis=2))
    u, w = sol[:, :, 0:DN_DIM], sol[:, :, DN_DIM:2 * DN_DIM]
    qk = jnp.where(cc <= r, mm_nt(q, k) * dec, 0.0)
    glast = jnp.sum(g, axis=1, keepdims=True)
    qd = q * eg
    kd = k * jnp.exp(glast - gam)
    un = u - mm(w, s)
    o = mm(qd, s) + mm(qk, un)
    s_new = s * jnp.exp(glast) + mm_tn(kd, un)
    on = o * lax.rsqrt(jnp.mean(o * o, axis=-1, keepdims=True) + NORM_EPS) * gain * _silu(z)
    return on, s_new, t


def _dn_gates(ab, al_row, dt_row):
    pre = ab + dt_row
    return -jnp.exp(al_row) * _softplus(pre), _sigmoid(ab), _sigmoid(pre)


def _dn_chains(cacts, gates, z_ref):
    cq, ck, cv, g, beta, z = [], [], [], [], [], []
    for bi, cact in enumerate(cacts):
        for h in range(DN_HEADS):
            cq.append(cact[:, h * DN_DIM:(h + 1) * DN_DIM])
            ck.append(cact[:, DN_WIDTH + h * DN_DIM:DN_WIDTH + (h + 1) * DN_DIM])
            cv.append(cact[:, 2 * DN_WIDTH + h * DN_DIM:2 * DN_WIDTH + (h + 1) * DN_DIM])
            g.append(gates[bi][0][:, h:h + 1])
            beta.append(gates[bi][1][:, DN_HEADS + h:DN_HEADS + h + 1])
            z.append(z_ref[bi, :, h * DN_DIM:(h + 1) * DN_DIM])
    return tuple(jnp.stack(v) for v in (cq, ck, cv, g, beta, z))


def _conv_rows(xe_ref, b, w_ref):
    y = w_ref[0:1, :] * xe_ref[b, pl.ds(5, DN_CHUNK), :]
    for i in range(1, DN_CONV):
        y = y + w_ref[i:i + 1, :] * xe_ref[b, pl.ds(5 + i, DN_CHUNK), :]
    return y


def dn_fwd(qkv, z, ab, conv_w, alog, dtb, gain):
    bsz, t, _ = qkv.shape
    nc = t // DN_CHUNK
    c = DN_CHUNK
    nh = bsz * DN_HEADS

    def body(qkv_ref, z_ref, ab_ref, w_ref, al_ref, dt_ref, g_ref, o_ref, sall_ref, tall_ref, xe, s_sc):
        n = pl.program_id(0)

        @pl.when(n == 0)
        def _():
            xe[:, 0:8, :] = jnp.zeros((bsz, 8, 3 * DN_WIDTH), f32)
            s_sc[...] = jnp.zeros_like(s_sc)

        lt, ltt = _dn_consts()
        cacts = []
        for b in range(bsz):
            xe[b, 8:8 + c, :] = qkv_ref[b]
            cacts.append(_silu(_conv_rows(xe, b, w_ref)))
            xe[b, 0:8, :] = xe[b, c:c + 8, :]
        gates = [_dn_gates(ab_ref[b], al_ref[...], dt_ref[...]) for b in range(bsz)]
        s = s_sc[...]
        sall_ref[0] = s
        on, sn, tt = dn_chunk(*_dn_chains(cacts, gates, z_ref), s, g_ref[...], lt, ltt)
        tall_ref[0] = tt
        s_sc[...] = sn
        for b in range(bsz):
            for h in range(DN_HEADS):
                o_ref[b, :, h * DN_DIM:(h + 1) * DN_DIM] = on[b * DN_HEADS + h]

    blk = lambda w: pl.BlockSpec((bsz, c, w), lambda n: (0, n, 0))
    full = lambda shp: pl.BlockSpec(shp, lambda n: (0,) * len(shp))
    return pl.pallas_call(
        body, name="dn_fwd", grid=(nc,),
        in_specs=[blk(3 * DN_WIDTH), blk(DN_WIDTH), blk(128), full((8, 3 * DN_WIDTH)), full((1, 128)), full((1, 128)), full((1, 128))],
        out_specs=[blk(DN_WIDTH), pl.BlockSpec((1, nh, DN_DIM, DN_DIM), lambda n: (n, 0, 0, 0)),
                   pl.BlockSpec((1, nh, c, c), lambda n: (n, 0, 0, 0))],
        out_shape=[SDS((bsz, t, DN_WIDTH), f32), SDS((nc, nh, DN_DIM, DN_DIM), f32), SDS((nc, nh, c, c), f32)],
        scratch_shapes=[pltpu.VMEM((bsz, c + 8, 3 * DN_WIDTH), f32), pltpu.VMEM((nh, DN_DIM, DN_DIM), f32)],
        compiler_params=_cp(("arbitrary",)),
    )(qkv, z, ab, conv_w, alog, dtb, gain)


def dn_bwd(qkv, z, ab, conv_w, alog, dtb, gain, sall, tall, do):
    bsz, t, _ = qkv.shape
    nc = t // DN_CHUNK
    c = DN_CHUNK
    nh = bsz * DN_HEADS
    w3 = 3 * DN_WIDTH

    def body(qkv_ref, prev_ref, z_ref, ab_ref, w_ref, al_ref, dt_ref, g_ref, sall_ref, tall_ref, do_ref,
             dp_ref, dw_ref, dal_ref, ddt_ref, dg_ref, xe, dye, dc_sc, ds_sc):
        n = pl.program_id(0)
        first = (nc - 1 - n) == 0

        @pl.when(n == 0)
        def _():
            dye[:, c:c + 8, :] = jnp.zeros((bsz, 8, w3), f32)
            ds_sc[...] = jnp.zeros_like(ds_sc)
            dw_ref[...] = jnp.zeros_like(dw_ref)
            dal_ref[...] = jnp.zeros_like(dal_ref)
            ddt_ref[...] = jnp.zeros_like(ddt_ref)
            dg_ref[...] = jnp.zeros_like(dg_ref)

        lt, ltt = _dn_consts()
        lane_c = lax.broadcasted_iota(jnp.int32, (c, 128), 1)
        ys, sigs = [], []
        for b in range(bsz):
            xe[b, 0:8, :] = jnp.where(first, 0.0, prev_ref[b])
            xe[b, 8:8 + c, :] = qkv_ref[b]
            ys.append(_conv_rows(xe, b, w_ref))
            sigs.append(_sigmoid(ys[b]))
        gates = [_dn_gates(ab_ref[b], al_ref[...], dt_ref[...]) for b in range(bsz)]
        ops = _dn_chains([y * sg for y, sg in zip(ys, sigs)], gates, z_ref)
        tt = tall_ref[0]
        _, vjp = jax.vjp(lambda *p: dn_chunk(*p, lt, ltt, t_given=tt)[0:2], *ops, sall_ref[0], g_ref[...])
        don = jnp.stack([do_ref[b, :, h * DN_DIM:(h + 1) * DN_DIM] for b in range(bsz) for h in range(DN_HEADS)])
        dcq, dck, dcv, dg, dbeta, dzz, dsp, dgn = vjp((don, ds_sc[...]))
        ds_sc[...] = dsp
        dg_ref[...] += dgn
        for b in range(bsz):
            dgate = jnp.zeros((c, 128), f32)
            for h in range(DN_HEADS):
                i = b * DN_HEADS + h
                dc_sc[b, :, h * DN_DIM:(h + 1) * DN_DIM] = dcq[i]
                dc_sc[b, :, DN_WIDTH + h * DN_DIM:DN_WIDTH + (h + 1) * DN_DIM] = dck[i]
                dc_sc[b, :, 2 * DN_WIDTH + h * DN_DIM:2 * DN_WIDTH + (h + 1) * DN_DIM] = dcv[i]
                dp_ref[b, :, C_Z + h * DN_DIM:C_Z + (h + 1) * DN_DIM] = dzz[i].astype(bf16)
                dgate = dgate + jnp.where(lane_c == h, dg[i], 0.0) + jnp.where(lane_c == DN_HEADS + h, dbeta[i], 0.0)
            gg, beta, sig_pre = gates[b]
            is_g = lane_c < DN_HEADS
            dpre = jnp.where(is_g, dgate * (-jnp.exp(al_ref[...])) * sig_pre, 0.0)
            dp_ref[b, :, C_AB:C_AB + 128] = (dpre + jnp.where(is_g, 0.0, dgate * beta * (1.0 - beta))).astype(bf16)
            dp_ref[b, :, C_AB + 128:DN_COLS] = jnp.zeros((c, DN_COLS - C_AB - 128), bf16)
            dal_ref[...] += jnp.sum(jnp.where(is_g, dgate * gg, 0.0), axis=0, keepdims=True)
            ddt_ref[...] += jnp.sum(dpre, axis=0, keepdims=True)
            y, sig = ys[b], sigs[b]
            dy = dc_sc[b] * (sig * (1.0 + y * (1.0 - sig)))
            dye[b, 0:c, :] = dy
            dx = w_ref[3:4, :] * dy
            for i in range(DN_CONV - 1):
                dx = dx + w_ref[i:i + 1, :] * dye[b, pl.ds(3 - i, c), :]
            dp_ref[b, :, 0:w3] = dx.astype(bf16)
            for i in range(DN_CONV):
                dw_ref[i:i + 1, :] += jnp.sum(dy * xe[b, pl.ds(5 + i, c), :], axis=0, keepdims=True)
            dye[b, c:c + 8, :] = dye[b, 0:8, :]

    rev = lambda w: pl.BlockSpec((bsz, c, w), lambda n: (0, nc - 1 - n, 0))
    full = lambda shp: pl.BlockSpec(shp, lambda n: (0,) * len(shp))
    prev = pl.BlockSpec((bsz, 8, w3), lambda n: (0, jnp.maximum((nc - 1 - n) * (c // 8) - 1, 0), 0))
    return pl.pallas_call(
        body, name="dn_bwd", grid=(nc,),
        in_specs=[rev(w3), prev, rev(DN_WIDTH), rev(128), full((8, w3)), full((1, 128)), full((1, 128)), full((1, 128)),
                  pl.BlockSpec((1, nh, DN_DIM, DN_DIM), lambda n: (nc - 1 - n, 0, 0, 0)),
                  pl.BlockSpec((1, nh, c, c), lambda n: (nc - 1 - n, 0, 0, 0)), rev(DN_WIDTH)],
        out_specs=[rev(DN_COLS), full((8, w3)), full((1, 128)), full((1, 128)), full((1, 128))],
        out_shape=[SDS((bsz, t, IN_PAD), bf16), SDS((8, w3), f32), SDS((1, 128), f32), SDS((1, 128), f32), SDS((1, 128), f32)],
        scratch_shapes=[pltpu.VMEM((bsz, c + 8, w3), f32), pltpu.VMEM((bsz, c + 8, w3), f32), pltpu.VMEM((bsz, c, w3), f32),
                        pltpu.VMEM((nh, DN_DIM, DN_DIM), f32)],
        compiler_params=_cp(("arbitrary",)),
    )(qkv, qkv, z, ab, conv_w, alog, dtb, gain, sall, tall, do)


SB_TILE = 256
SB_QTILE, SB_KTILE = 256, 256
SB_PAIRS = SB_HEADS // 2


def sb_fwd(sbqkv, gq, gk):
    bsz, t, _ = sbqkv.shape
    bq = min(SB_QTILE, t)
    blk = max(min(SB_KTILE, t), bq)
    nq = t // bq
    scale = SB_DIM ** -0.5

    def body(q_ref, k_ref, v_ref, gq_ref, gk_ref, o_ref, l_ref, q2_sc, kn_sc, v_sc):
        bavg = _group_avg_mats()
        lane = lax.broadcasted_iota(jnp.int32, (1, 128), 1)
        first = lane < SB_DIM
        for p in range(SB_PAIRS):
            ls = slice(p * 128, (p + 1) * 128)
            qn = _pair_norm(q_ref[0, :, ls], gq_ref[...], bavg)
            kn_sc[p] = _pair_norm(k_ref[0, :, ls], gk_ref[...], bavg).astype(bf16)
            v_sc[p] = v_ref[0, :, ls].astype(bf16)
            q2_sc[2 * p] = jnp.where(first, qn, 0.0).astype(bf16)
            q2_sc[2 * p + 1] = jnp.where(first, 0.0, qn).astype(bf16)
        r, c = _iota2((blk, blk))
        ustrict = (r > c).astype(bf16)
        r2, c2 = _iota2((2 * bq, blk))

        def tile(q2s, ks, carry, causal):
            out = []
            for p in range(SB_PAIRS):
                acc, rr = carry[2 * p], carry[2 * p + 1]
                zz = lax.dot_general(q2s[p], kn_sc[p, pl.ds(ks, blk), :], NT, preferred_element_type=f32) * scale
                sp = _softplus(zz)
                lm = -sp if causal is None else jnp.where(causal, -sp, 0.0)
                rem = _dot_x2c(lm, ustrict)
                wgt = jnp.exp(zz - sp + rem + rr)
                if causal is not None:
                    wgt = jnp.where(causal, wgt, 0.0)
                out += [acc + _pdot(wgt.astype(bf16), v_sc[p, pl.ds(ks, blk), :]), rr + jnp.sum(lm, axis=1, keepdims=True)]
            return tuple(out)

        def qloop(qi, _):
            qs = pl.multiple_of(qi * bq, bq)
            kd = qs // blk
            causal = c2 < (r2 & (bq - 1)) + (qs - kd * blk)
            q2s = [jnp.concatenate([q2_sc[2 * p, pl.ds(qs, bq), :], q2_sc[2 * p + 1, pl.ds(qs, bq), :]], axis=0)
                   for p in range(SB_PAIRS)]
            zero = (jnp.zeros((2 * bq, 128), f32), jnp.zeros((2 * bq, 1), f32)) * SB_PAIRS
            carry = lax.fori_loop(1, kd + 1, lambda i, cr: tile(q2s, pl.multiple_of((kd - i) * blk, blk), cr, None),
                                  tile(q2s, pl.multiple_of(kd * blk, blk), zero, causal))
            for p in range(SB_PAIRS):
                acc, rr = carry[2 * p], carry[2 * p + 1]
                o_ref[0, pl.ds(qs, bq), p * 128:(p + 1) * 128] = jnp.where(first, acc[0:bq], acc[bq:2 * bq])
                l_ref[0, pl.ds(qs, bq), p * 128:(p + 1) * 128] = jnp.where(first, rr[0:bq], rr[bq:2 * bq])
            return 0

        lax.fori_loop(0, nq, qloop, 0)

    col = lambda off: pl.BlockSpec((1, t, SB_WIDTH), lambda b: (b, 0, off))
    gsp = pl.BlockSpec((1, 128), lambda b: (0, 0))
    return pl.pallas_call(
        body, name="sb_fwd", grid=(bsz,),
        in_specs=[col(0), col(1), col(2), gsp, gsp],
        out_specs=[col(0), col(0)],
        out_shape=[SDS((bsz, t, SB_WIDTH), f32), SDS((bsz, t, SB_WIDTH), f32)],
        scratch_shapes=[pltpu.VMEM((2 * SB_PAIRS, t, 128), bf16), pltpu.VMEM((SB_PAIRS, t, 128), bf16),
                        pltpu.VMEM((SB_PAIRS, t, 128), bf16)],
        compiler_params=_cp(("arbitrary",)),
    )(sbqkv, sbqkv, sbqkv, gq, gk)


def sb_bwd(sbqkv, gq, gk, ltot, do, dproj):
    bsz, t, _ = sbqkv.shape
    blk = min(SB_TILE, t)
    nq = t // blk
    scale = SB_DIM ** -0.5

    def body(q_ref, k_ref, v_ref, gq_ref, gk_ref, l_ref, do_ref, dp_in, dp_ref, dgq_ref, dgk_ref,
             q2_sc, kn_sc, v_sc, do2_sc, dqn_sc, dkn_sc, dv_sc):
        bavg = _group_avg_mats()
        lane = lax.broadcasted_iota(jnp.int32, (1, 128), 1)
        first = lane < SB_DIM
        fq = lambda x, g: _pair_norm(x, g, bavg)
        vjps = []
        for p in range(SB_PAIRS):
            ls = slice(p * 128, (p + 1) * 128)
            qn, q_vjp = jax.vjp(fq, q_ref[0, :, ls], gq_ref[...])
            kn, k_vjp = jax.vjp(fq, k_ref[0, :, ls], gk_ref[...])
            vjps.append((q_vjp, k_vjp))
            kn_sc[p] = kn.astype(bf16)
            v_sc[p] = v_ref[0, :, ls].astype(bf16)
            dov = do_ref[0, :, ls]
            q2_sc[2 * p] = jnp.where(first, qn, 0.0).astype(bf16)
            q2_sc[2 * p + 1] = jnp.where(first, 0.0, qn).astype(bf16)
            do2_sc[2 * p] = jnp.where(first, dov, 0.0).astype(bf16)
            do2_sc[2 * p + 1] = jnp.where(first, 0.0, dov).astype(bf16)
        dkn_sc[...] = jnp.zeros_like(dkn_sc)
        dv_sc[...] = jnp.zeros_like(dv_sc)
        r, c = _iota2((blk, blk))
        pincl = (r <= c).astype(bf16)
        pstrict = (r < c).astype(bf16)
        r2, c2 = _iota2((2 * blk, blk))
        causal = c2 < (r2 & (blk - 1))

        def tile(q2s, do2s, lts, ks, carry, diag):
            out = []
            for p in range(SB_PAIRS):
                dq, cs, ce = carry[3 * p:3 * p + 3]
                q2, do2 = q2s[p], do2s[p]
                kb = kn_sc[p, pl.ds(ks, blk), :]
                zz = lax.dot_general(q2, kb, NT, preferred_element_type=f32) * scale
                sp = _softplus(zz)
                lm = jnp.where(causal, -sp, 0.0) if diag else -sp
                pre = _dot_x2c(lm, pincl)
                lp = zz - sp
                wgt = jnp.exp(lp + (lts[p] - cs - pre))
                if diag:
                    wgt = jnp.where(causal, wgt, 0.0)
                dw = lax.dot_general(do2, v_sc[p, pl.ds(ks, blk), :], NT, preferred_element_type=f32)
                e = wgt * dw
                ee = ce + _dot_x2c(e, pstrict)
                sig = jnp.exp(lp)
                dz = (e * (1.0 - sig) - ee * sig) * scale
                if diag:
                    dz = jnp.where(causal, dz, 0.0)
                dz = dz.astype(bf16)
                dkn_sc[p, pl.ds(ks, blk), :] += lax.dot_general(dz, q2, TN, preferred_element_type=f32)
                dv_sc[p, pl.ds(ks, blk), :] += lax.dot_general(wgt.astype(bf16), do2, TN, preferred_element_type=f32)
                out += [dq + _pdot(dz, kb), cs + jnp.sum(lm, axis=1, keepdims=True), ce + jnp.sum(e, axis=1, keepdims=True)]
            return tuple(out)

        def qloop(qi, _):
            qs = pl.multiple_of(qi * blk, blk)
            rows = pl.ds(qs, blk)
            q2s = [jnp.concatenate([q2_sc[2 * p, rows, :], q2_sc[2 * p + 1, rows, :]], axis=0) for p in range(SB_PAIRS)]
            do2s = [jnp.concatenate([do2_sc[2 * p, rows, :], do2_sc[2 * p + 1, rows, :]], axis=0) for p in range(SB_PAIRS)]
            lts = [jnp.concatenate([l_ref[0, rows, p * 128:p * 128 + 1], l_ref[0, rows, p * 128 + SB_DIM:p * 128 + SB_DIM + 1]],
                                   axis=0) for p in range(SB_PAIRS)]
            z1 = jnp.zeros((2 * blk, 1), f32)
            carry = lax.fori_loop(0, qi, lambda kj, cr: tile(q2s, do2s, lts, pl.multiple_of(kj * blk, blk), cr, False),
                                  (jnp.zeros((2 * blk, 128), f32), z1, z1) * SB_PAIRS)
            carry = tile(q2s, do2s, lts, qs, carry, True)
            for p in range(SB_PAIRS):
                dq = carry[3 * p]
                dqn_sc[p, rows, :] = jnp.where(first, dq[0:blk], dq[blk:2 * blk])
            return 0

        lax.fori_loop(0, nq, qloop, 0)
        dgq_tot, dgk_tot = jnp.zeros((1, 128), f32), jnp.zeros((1, 128), f32)
        for p in range(SB_PAIRS):
            ls = slice(p * 128, (p + 1) * 128)
            dq_pre, dgq = vjps[p][0](dqn_sc[p])
            dk_pre, dgk = vjps[p][1](dkn_sc[p])
            dp_ref[0, :, p * 128:(p + 1) * 128] = dq_pre.astype(bf16)
            dp_ref[0, :, SB_WIDTH + p * 128:SB_WIDTH + (p + 1) * 128] = dk_pre.astype(bf16)
            dp_ref[0, :, 2 * SB_WIDTH + p * 128:2 * SB_WIDTH + (p + 1) * 128] = dv_sc[p].astype(bf16)
            dgq_tot, dgk_tot = dgq_tot + dgq, dgk_tot + dgk
        dgq_ref[0] = jnp.broadcast_to(dgq_tot, (8, 128))
        dgk_ref[0] = jnp.broadcast_to(dgk_tot, (8, 128))

    col = lambda off: pl.BlockSpec((1, t, SB_WIDTH), lambda b: (b, 0, off), pipeline_mode=pl.Buffered(1))
    gsp = pl.BlockSpec((1, 128), lambda b: (0, 0))
    gout = pl.BlockSpec((1, 8, 128), lambda b: (b, 0, 0))
    return pl.pallas_call(
        body, name="sb_bwd", grid=(bsz,),
        in_specs=[col(0), col(1), col(2), gsp, gsp, col(0), col(0), pl.BlockSpec(memory_space=pl.ANY)],
        out_specs=[pl.BlockSpec((1, t, 3 * SB_WIDTH), lambda b: (b, 0, C_SB // (3 * SB_WIDTH)), pipeline_mode=pl.Buffered(1)),
                   gout, gout],
        out_shape=[SDS(dproj.shape, bf16)] + [SDS((bsz, 8, 128), f32)] * 2,
        input_output_aliases={7: 0},
        scratch_shapes=[pltpu.VMEM((2 * SB_PAIRS, t, 128), bf16), pltpu.VMEM((SB_PAIRS, t, 128), bf16),
                        pltpu.VMEM((SB_PAIRS, t, 128), bf16), pltpu.VMEM((2 * SB_PAIRS, t, 128), bf16),
                        pltpu.VMEM((SB_PAIRS, t, 128), f32), pltpu.VMEM((SB_PAIRS, t, 128), f32), pltpu.VMEM((SB_PAIRS, t, 128), f32)],
        compiler_params=_cp(("arbitrary",)),
    )(sbqkv, sbqkv, sbqkv, gq, gk, ltot, do, dproj)


def sg_pair(u, v, gain, wa, wb, ba, bb, bavg):
    r, c = _iota2((SG_CHUNK, SG_CHUNK))
    lane = lax.broadcasted_iota(jnp.int32, (1, 128), 1)
    first = lane < SG_DIM
    vn = _pair_norm(_gelu(v), gain, bavg)
    tri = c <= r
    mixed = (mm(jnp.where(tri, wa, 0.0), jnp.where(first, vn, 0.0)) + mm(jnp.where(tri, wb, 0.0), jnp.where(first, 0.0, vn))
             + jnp.where(first, ba, bb))
    return _gelu(u) * mixed


def sg_fwd(sguv, gain, w, bt):
    bsz, t, _ = sguv.shape
    nch = t // SG_CHUNK

    def body(uv_ref, g_ref, w_ref, b_ref, o_ref):
        bavg = _group_avg_mats()
        for p in range(2):
            ls = slice(p * 128, (p + 1) * 128)
            o_ref[0, :, ls] = sg_pair(uv_ref[0, :, ls], uv_ref[0, :, SG_WIDTH + p * 128:SG_WIDTH + (p + 1) * 128], g_ref[:, ls],
                                      w_ref[2 * p], w_ref[2 * p + 1], b_ref[:, 2 * p:2 * p + 1], b_ref[:, 2 * p + 1:2 * p + 2], bavg)

    full = lambda shp: pl.BlockSpec(shp, lambda b, n: (0,) * len(shp))
    return pl.pallas_call(
        body, name="sg_fwd", grid=(bsz, nch),
        in_specs=[pl.BlockSpec((1, SG_CHUNK, 2 * SG_WIDTH), lambda b, n: (b, n, 0)), full((1, SG_WIDTH)),
                  full((SG_GROUPS, SG_CHUNK, SG_CHUNK)), full((SG_CHUNK, 128))],
        out_specs=pl.BlockSpec((1, SG_CHUNK, SG_WIDTH), lambda b, n: (b, n, 0)),
        out_shape=SDS((bsz, t, SG_WIDTH), f32),
        compiler_params=_cp(("arbitrary", "arbitrary")),
    )(sguv, gain, w, bt)


def sg_bwd(sguv, gain, w, bt, do, dproj):
    bsz, t, _ = sguv.shape
    nch = t // SG_CHUNK

    def body(uv_ref, g_ref, w_ref, b_ref, do_ref, dp_in, duv_ref, dg_ref, dw_ref, db_ref):
        @pl.when((pl.program_id(0) == 0) & (pl.program_id(1) == 0))
        def _():
            dg_ref[...] = jnp.zeros_like(dg_ref)
            dw_ref[...] = jnp.zeros_like(dw_ref)
            db_ref[...] = jnp.zeros_like(db_ref)

        bavg = _group_avg_mats()
        lane = lax.broadcasted_iota(jnp.int32, (SG_CHUNK, 128), 1)
        dbt = jnp.zeros((SG_CHUNK, 128), f32)
        for p in range(2):
            ls = slice(p * 128, (p + 1) * 128)
            vs = slice(SG_WIDTH + p * 128, SG_WIDTH + (p + 1) * 128)
            prim = (uv_ref[0, :, ls], uv_ref[0, :, vs], g_ref[:, ls], w_ref[2 * p], w_ref[2 * p + 1],
                    b_ref[:, 2 * p:2 * p + 1], b_ref[:, 2 * p + 1:2 * p + 2])
            _, vjp = jax.vjp(lambda *a: sg_pair(*a, bavg), *prim)
            du, dv, dgn, dwa, dwb, dba, dbb = vjp(do_ref[0, :, ls])
            duv_ref[0, :, ls] = du.astype(bf16)
            duv_ref[0, :, vs] = dv.astype(bf16)
            dg_ref[:, ls] += dgn
            dw_ref[2 * p] += dwa
            dw_ref[2 * p + 1] += dwb
            dbt = dbt + jnp.where(lane == 2 * p, dba, 0.0) + jnp.where(lane == 2 * p + 1, dbb, 0.0)
        db_ref[...] += dbt

    full = lambda shp: pl.BlockSpec(shp, lambda b, n: (0,) * len(shp))
    return pl.pallas_call(
        body, name="sg_bwd", grid=(bsz, nch),
        in_specs=[pl.BlockSpec((1, SG_CHUNK, 2 * SG_WIDTH), lambda b, n: (b, n, 0)), full((1, SG_WIDTH)),
                  full((SG_GROUPS, SG_CHUNK, SG_CHUNK)), full((SG_CHUNK, 128)),
                  pl.BlockSpec((1, SG_CHUNK, SG_WIDTH), lambda b, n: (b, n, 0)), pl.BlockSpec(memory_space=pl.ANY)],
        out_specs=[pl.BlockSpec((1, SG_CHUNK, 2 * SG_WIDTH), lambda b, n: (b, n, C_SG // (2 * SG_WIDTH))), full((1, SG_WIDTH)),
                   full((SG_GROUPS, SG_CHUNK, SG_CHUNK)), full((SG_CHUNK, 128))],
        out_shape=[SDS(dproj.shape, bf16), SDS((1, SG_WIDTH), f32), SDS((SG_GROUPS, SG_CHUNK, SG_CHUNK), f32),
                   SDS((SG_CHUNK, 128), f32)],
        input_output_aliases={5: 0},
        compiler_params=_cp(("arbitrary", "arbitrary")),
    )(sguv, gain, w, bt, do, dproj)


def _pad_lanes(v, n=128):
    return jnp.pad(v.reshape(1, -1), ((0, 0), (0, n - v.size)))


def _w_in_runs():
    shard, runs = IN_DIM // N_CHIPS, []
    for s in range(N_CHIPS):
        for a, b, d in ((0, 2048, 0), (2048, 2056, C_AB), (2056, IN_DIM, C_SB)):
            lo, hi = max(shard * s, a), min(shard * (s + 1), b)
            if lo < hi:
                runs.append((s, lo - shard * s, hi - shard * s, d + lo - a))
    return runs


def w_in_from_shards(zone, tr=256):
    def body(z_ref, o_ref):
        o_ref[:, C_AB:C_SB] = jnp.zeros((tr, C_SB - C_AB), zone.dtype)
        for s, a, b, d in _w_in_runs():
            o_ref[:, d:d + b - a] = z_ref[s, :, a:b]

    return pl.pallas_call(
        body, name="w_in_from_shards", grid=(D_MODEL // tr,),
        in_specs=[pl.BlockSpec((N_CHIPS, tr, IN_DIM // N_CHIPS), lambda i: (0, i, 0))],
        out_specs=pl.BlockSpec((tr, IN_PAD), lambda i: (i, 0)), out_shape=SDS((D_MODEL, IN_PAD), zone.dtype),
        compiler_params=_cp(("arbitrary",)))(zone)


def w_in_grad_to_shards(g, tr=256):
    def body(g_ref, o_ref):
        for s, a, b, d in _w_in_runs():
            o_ref[s, :, a:b] = g_ref[:, d:d + b - a]

    return pl.pallas_call(
        body, name="w_in_grad_to_shards", grid=(D_MODEL // tr,),
        in_specs=[pl.BlockSpec((tr, IN_PAD), lambda i: (i, 0))],
        out_specs=pl.BlockSpec((N_CHIPS, tr, IN_DIM // N_CHIPS), lambda i: (0, i, 0)),
        out_shape=SDS((N_CHIPS, D_MODEL, IN_DIM // N_CHIPS), g.dtype), compiler_params=_cp(("arbitrary",)))(g)


def layer_params(p, l):
    return dict(
        g1=p["norm1_g"][l].reshape(1, -1), g2=p["norm2_g"][l].reshape(1, -1),
        conv=jnp.pad(p["conv_w"][l], ((0, 4), (0, 0))), alog=_pad_lanes(p["a_log"][l]), dtb=_pad_lanes(p["dt_bias"][l]),
        dng=p["dn_out_g"][l].reshape(1, -1), gq=jnp.tile(p["sb_q_g"][l].reshape(1, -1), (1, 2)),
        gk=jnp.tile(p["sb_k_g"][l].reshape(1, -1), (1, 2)), sgg=p["sg_v_g"][l].reshape(1, -1), sgw=p["sg_w"][l],
        sgb=jnp.pad(p["sg_b"][l].T, ((0, 0), (0, 124))))


def local_step(x, tgt, small, get_w, put_g, sync_g):
    bsz, t, _ = x.shape
    m = bsz * t
    r3 = lambda a: a.reshape(bsz, t, a.shape[-1])
    r2 = lambda a: a.reshape(m, a.shape[-1])
    xs, saved, ws = x.reshape(m, D_MODEL), [], []
    for l in range(DEPTH):
        sp, w = layer_params(small, l), {}
        w["w_in"] = get_w(l, "in", xs)
        qkv, z, ab, sb, sg, h1 = inproj_fwd(xs, sp["g1"], w["w_in"])
        odn, sall, tall = dn_fwd(r3(qkv), r3(z), r3(ab), sp["conv"], sp["alog"], sp["dtb"], sp["dng"])
        osb, ltot = sb_fwd(r3(sb), sp["gq"], sp["gk"])
        osg = sg_fwd(r3(sg), sp["sgg"], sp["sgw"], sp["sgb"])
        w["w_out"] = get_w(l, "out", osg)
        x2, mix = outproj_fwd(xs, r2(odn), r2(osb), r2(osg), w["w_out"])
        w["w_ff1"], w["w_ff2"], started = get_w(l, "ff", x2)
        x3, rlb = ffn_fwd(x2, sp["g2"] + started, w["w_ff1"], w["w_ff2"])
        saved.append(dict(rlb=rlb, h1=h1, x=xs,qkv=qkv, z=z, ab=ab, sb=sb, sg=sg, sall=sall, tall=tall, ltot=ltot, mix=mix, x2=x2))
        ws.append(w)
        xs = x3
    dx, lossp = loss_head(xs, tgt.reshape(m, D_MODEL))
    gsmall = [None] * DEPTH
    token = jnp.zeros((), f32)
    for l in reversed(range(DEPTH)):
        sp, w, s = layer_params(small, l), ws[l], saved[l]
        dx2, dg2, h2, act, df, dyb = ffn_bwd(s["x2"], sp["g2"] + token, w["w_ff1"], w["w_ff2"], s["rlb"], dx)
        g_ff1 = tn_matmul(h2, df, f"dw_ff1_{l}", col_shards=N_CHIPS)
        g_ff2 = tn_matmul(act, dyb, f"dw_ff2_{l}")
        dodn, dosb, dosg, dx2b = outproj_bwd(dx2, w["w_out"])
        g_out = tn_matmul(s["mix"], dx2b, f"dw_out_{l}")
        token = token + put_g(l, "rest", dict(w_out=g_out, w_ff1=g_ff1, w_ff2=g_ff2))
        dproj, dconv, dalog, ddtb, ddng = dn_bwd(r3(s["qkv"]), r3(s["z"]), r3(s["ab"]), sp["conv"], sp["alog"], sp["dtb"],
                                                 sp["dng"] + token, s["sall"], s["tall"], r3(dodn))
        token = sync_g(ddng)
        dproj, dgq, dgk = sb_bwd(r3(s["sb"]), sp["gq"] + token, sp["gk"], s["ltot"], r3(dosb), dproj)
        dproj, dsgg, dsgw, dsgb = sg_bwd(r3(s["sg"]), sp["sgg"], sp["sgw"], sp["sgb"], r3(dosg), dproj)
        dproj = r2(dproj)
        g_in = tn_matmul(s["h1"], dproj, f"dw_in_{l}")
        token = put_g(l, "in", dict(w_in=g_in))
        dx, dg1 = inproj_bwd(s["x"], sp["g1"] + token, w["w_in"], dproj, dx2)
        token = sync_g(dg1)
        fold = lambda a: (a[:, 0, :].sum(0).reshape(2, SB_DIM)).sum(0)
        gsmall[l] = dict(norm1_g=dg1[0], conv_w=dconv[0:DN_CONV], a_log=dalog[0, 0:DN_HEADS], dt_bias=ddtb[0, 0:DN_HEADS],
                         dn_out_g=ddng[0], sb_q_g=fold(dgq), sb_k_g=fold(dgk), sg_v_g=dsgg[0], sg_w=dsgw,
                         sg_b=dsgb[:, 0:SG_GROUPS].T, norm2_g=dg2[0])
    return lossp, dx.reshape(bsz, t, D_MODEL), gsmall


def _chip_peers(x, y):
    return [(1 - x, y), (x, 1 - y), (1 - x, 1 - y)]


_HBM = pl.BlockSpec(memory_space=pltpu.HBM)
_SEM = pl.BlockSpec(memory_space=pltpu.SEMAPHORE)
_EFFECT = pltpu.SideEffectType.DATAFLOW_SIDE_EFFECTING


def _hbm(a):
    return pltpu.with_memory_space_constraint(a, pltpu.HBM)


def _my_half(ref):
    half = ref.shape[0] // 2
    return ref.at[pl.ds(pl.multiple_of(lax.axis_index("c") * half, 8), half)]


def _exchange_copy(src, land, k, j, send, recv, scatter, halve, waiting):
    x, y, c = lax.axis_index("x"), lax.axis_index("y"), lax.axis_index("c")
    px, py = _chip_peers(x, y)[j]
    me, peer = 2 * x + y, 2 * px + py
    if scatter:
        src = src.at[me if waiting else peer]
    dst = land.at[peer if waiting else me]
    if halve:
        src, dst = _my_half(src), _my_half(dst)
    return pltpu.make_async_remote_copy(src_ref=src, dst_ref=dst, send_sem=send.at[3 * k + j],
                                        recv_sem=recv.at[3 * k + j], device_id=(px, py, c), device_id_type=MESH)


def exchange_start(items, name, scatter, after=None):
    arrs = []
    for a, _, _ in items:
        if not any(a is b for b in arrs):
            arrs.append(a)
    pos = [next(i for i, b in enumerate(arrs) if b is a) for a, _, _ in items]
    shapes = [a.shape if idx is None else a.shape[1:] for a, idx, _ in items]
    lands = [lax.empty(s if scatter else (N_CHIPS,) + s, a.dtype) for (a, _, _), s in zip(items, shapes)]
    na, nl = len(arrs), len(lands)
    n_in = na + nl + (after is not None)

    def body(*refs):
        ins, lnd = refs[:na], refs[na:na + nl]
        send, recv = refs[n_in], refs[n_in + 1]
        token = refs[-1]
        for k, (_, idx, halve) in enumerate(items):
            src = ins[pos[k]] if idx is None else ins[pos[k]].at[idx]
            for j in range(3):
                _exchange_copy(src, lnd[k], k, j, send, recv, scatter, halve, False).start()
        token[...] = jnp.zeros_like(token)

    sems = pltpu.SemaphoreType.DMA((3 * nl,))
    extra = [] if after is None else [after]
    out = pl.pallas_call(
        body, name=name,
        out_shape=(sems, sems, *[pltpu.HBM(a.shape, a.dtype) for a in arrs + lands], SDS((8, 128), f32)),
        in_specs=[_HBM] * (na + nl) + [pl.BlockSpec(memory_space=pl.ANY)] * len(extra),
        out_specs=(_SEM, _SEM, *[_HBM] * (na + nl), pl.BlockSpec(memory_space=pltpu.VMEM)),
        input_output_aliases={i: 2 + i for i in range(na + nl)},
        compiler_params=pltpu.CompilerParams(has_side_effects=_EFFECT),
    )(*[_hbm(a) for a in arrs + lands], *extra)
    thru = out[2:2 + na]
    return dict(send=out[0], recv=out[1], src=[(thru[pos[k]], idx) for k, (_, idx, _) in enumerate(items)],
                halve=[h for _, _, h in items], land=list(out[2 + na:2 + na + nl]), token=out[-1], scatter=scatter)


def exchange_wait(st, ks, after, name):
    after = list(after) if isinstance(after, (list, tuple)) else [after]
    arrs = []
    for k in ks:
        if not any(st["src"][k][0] is b for b in arrs):
            arrs.append(st["src"][k][0])
    pos = [next(i for i, b in enumerate(arrs) if b is st["src"][k][0]) for k in ks]
    lands = [st["land"][k] for k in ks]
    na, nl = len(arrs), len(lands)

    def body(*refs):
        ins, lnd = refs[:na], refs[na:na + nl]
        send, recv = refs[na + nl], refs[na + nl + 1]
        for t, k in enumerate(ks):
            idx = st["src"][k][1]
            src = ins[pos[t]] if idx is None else ins[pos[t]].at[idx]
            for j in range(3):
                cp = _exchange_copy(src, lnd[t], k, j, send, recv, st["scatter"], st["halve"][k], True)
                cp.wait_send()
                cp.wait_recv()

    out = pl.pallas_call(
        body, name=name, out_shape=tuple(pltpu.HBM(a.shape, a.dtype) for a in arrs + lands),
        in_specs=[_HBM] * (na + nl) + [_SEM, _SEM] + [pl.BlockSpec(memory_space=pl.ANY)] * len(after),
        out_specs=tuple([_HBM] * (na + nl)),
        input_output_aliases={i: i for i in range(na + nl)},
        compiler_params=pltpu.CompilerParams(has_side_effects=_EFFECT),
    )(*arrs, *lands, st["send"], st["recv"], *after)
    for k, (a, idx) in enumerate(st["src"]):
        for p, b in enumerate(arrs):
            if a is b:
                st["src"][k] = (out[p], idx)
    return list(out[na:na + nl])


def _sibling_copy(src, land, i, send, recv, other_half):
    x, y, c = lax.axis_index("x"), lax.axis_index("y"), lax.axis_index("c")
    return pltpu.make_async_remote_copy(src_ref=src.at[:, 1 - c] if other_half else src, dst_ref=land, send_sem=send.at[i],
                                        recv_sem=recv.at[i], device_id=(x, y, 1 - c), device_id_type=MESH)


def sibling_start(arrs, name, other_half=False):
    n = len(arrs)
    lands = [lax.empty((a.shape[0],) + a.shape[2:] if other_half else a.shape, a.dtype) for a in arrs]

    def body(*refs):
        ins, lnd = refs[:n], refs[n:2 * n]
        send, recv = refs[2 * n], refs[2 * n + 1]
        token = refs[-1]
        for i in range(n):
            _sibling_copy(ins[i], lnd[i], i, send, recv, other_half).start()
        token[...] = jnp.zeros_like(token)

    sems = pltpu.SemaphoreType.DMA((n,))
    out = pl.pallas_call(
        body, name=name,
        out_shape=(sems, sems, *[pltpu.HBM(a.shape, a.dtype) for a in arrs + lands], SDS((8, 128), f32)),
        in_specs=[_HBM] * (2 * n), out_specs=(_SEM, _SEM, *[_HBM] * (2 * n), pl.BlockSpec(memory_space=pltpu.VMEM)),
        input_output_aliases={i: 2 + i for i in range(2 * n)},
        compiler_params=pltpu.CompilerParams(has_side_effects=_EFFECT),
    )(*[_hbm(a) for a in arrs + lands])
    return dict(send=out[0], recv=out[1], src=list(out[2:2 + n]), land=list(out[2 + n:2 + 2 * n]), token=out[-1],
                other_half=other_half)


def sibling_wait(st, after, name):
    n = len(st["src"])

    def body(*refs):
        ins, lnd = refs[:n], refs[n:2 * n]
        send, recv = refs[2 * n], refs[2 * n + 1]
        for i in range(n):
            cp = _sibling_copy(ins[i], lnd[i], i, send, recv, st["other_half"])
            cp.wait_send()
            cp.wait_recv()

    out = pl.pallas_call(
        body, name=name, out_shape=tuple(pltpu.HBM(a.shape, a.dtype) for a in st["src"] + st["land"]),
        in_specs=[_HBM] * (2 * n) + [_SEM, _SEM, pl.BlockSpec(memory_space=pl.ANY)], out_specs=tuple([_HBM] * (2 * n)),
        input_output_aliases={i: i for i in range(2 * n)},
        compiler_params=pltpu.CompilerParams(has_side_effects=_EFFECT),
    )(*st["src"], *st["land"], st["send"], st["recv"], after)
    return list(out[:n]), list(out[n:])


def swap_halves(zones, name):
    n = len(zones)

    def body(*refs):
        outs = refs[n:2 * n]
        send, recv = refs[2 * n:]
        x, y, c = lax.axis_index("x"), lax.axis_index("y"), lax.axis_index("c")
        cps = []
        for i in range(n):
            for j, (px, py) in enumerate(_chip_peers(x, y)):
                part = _my_half(outs[i].at[2 * px + py])
                cps.append(pltpu.make_async_remote_copy(src_ref=part, dst_ref=part, send_sem=send.at[3 * i + j],
                                                        recv_sem=recv.at[3 * i + j], device_id=(x, y, 1 - c), device_id_type=MESH))
        for cp in cps:
            cp.start()
        for cp in cps:
            cp.wait_send()
            cp.wait_recv()

    any_spec = pl.BlockSpec(memory_space=pl.ANY)
    return pl.pallas_call(
        body, name=name, in_specs=[any_spec] * n, out_specs=[any_spec] * n, out_shape=[SDS(a.shape, a.dtype) for a in zones],
        input_output_aliases={i: i for i in range(n)},
        scratch_shapes=[pltpu.SemaphoreType.DMA((3 * n,)), pltpu.SemaphoreType.DMA((3 * n,))],
    )(*zones)


def _ids_spec(grid, in_specs, out_specs):
    return pltpu.PrefetchScalarGridSpec(num_scalar_prefetch=1, grid=grid, in_specs=in_specs, out_specs=out_specs)


def pair_sum(ids, a, b, name, tr=512):
    nd, _, rows, cols = a.shape
    tr = min(tr, rows)
    assert rows % tr == 0

    def body(ids_ref, a_ref, b_ref, o_ref):
        o_ref[...] = (a_ref[0].astype(f32) + b_ref[...].astype(f32)).astype(bf16)

    spec = pl.BlockSpec((1, tr, cols), lambda d, i, ids: (d, i, 0))
    return pl.pallas_call(
        body, name=name,
        grid_spec=_ids_spec((nd, rows // tr), [pl.BlockSpec((1, 1, tr, cols), lambda d, i, ids: (d, ids[1], i, 0)), spec], spec),
        out_shape=SDS((nd, rows, cols), bf16), compiler_params=_cp(("arbitrary", "arbitrary")))(ids, a, b)


def allreduce_small(v):
    def body(v_ref, o_ref, rbuf, send, recv):
        x, y, c = lax.axis_index("x"), lax.axis_index("y"), lax.axis_index("c")
        o_ref[...] = v_ref[...]
        for s, peer in enumerate([(x, y, 1 - c), (1 - x, y, c), (x, 1 - y, c)]):
            cp = pltpu.make_async_remote_copy(src_ref=o_ref, dst_ref=rbuf.at[s], send_sem=send.at[s], recv_sem=recv.at[s],
                                              device_id=peer, device_id_type=MESH)
            cp.start()
            cp.wait()
            o_ref[...] = o_ref[...] + rbuf[s]

    vm = pl.BlockSpec(memory_space=pltpu.VMEM)
    return pl.pallas_call(
        body, name="allreduce_small", in_specs=[vm], out_specs=vm, out_shape=SDS(v.shape, f32),
        scratch_shapes=[pltpu.VMEM((3,) + v.shape, f32), pltpu.SemaphoreType.DMA((3,)), pltpu.SemaphoreType.DMA((3,))],
        compiler_params=_cp(),
    )(v)


def sum_partials(ids, zone, mine, name, tr=256):
    _, rows, cols = zone.shape
    tr = min(tr, rows)
    assert rows % tr == 0

    def body(ids_ref, m_ref, z1_ref, z2_ref, z3_ref, o_ref):
        o_ref[...] = ((m_ref[0].astype(f32) + z1_ref[0].astype(f32)) + z2_ref[0].astype(f32)) + z3_ref[0].astype(f32)

    slot = lambda flip: pl.BlockSpec((1, tr, cols), lambda i, ids: (ids[0] ^ flip, i, 0))
    return pl.pallas_call(
        body, name=name,
        grid_spec=_ids_spec((rows // tr,), [slot(0), slot(1), slot(2), slot(3)], pl.BlockSpec((tr, cols), lambda i, ids: (i, 0))),
        out_shape=SDS((rows, cols), f32), compiler_params=_cp(("arbitrary",)),
    )(ids, mine, zone, zone, zone)


def adamw(w, m, v, gs, name, layer=0, prev=None, tr=256):
    hrows, cols = gs[0].shape
    rows = hrows * len(gs)
    tr = min(tr, hrows)
    assert hrows % tr == 0 and w.shape[0] % rows == 0
    off, nth = layer * (rows // tr), hrows // tr

    def body(w_ref, m_ref, v_ref, *rest):
        g_ref, d_ref, mo_ref, vo_ref = rest[-4:]
        if len(gs) == 1:
            g = rest[0][...]
        else:
            g = jnp.where(pl.program_id(0) // nth == lax.axis_index("c"), rest[0][...], rest[1][...])
        mn = ADAM_B1 * m_ref[...] + (1.0 - ADAM_B1) * g
        vn = ADAM_B2 * v_ref[...] + (1.0 - ADAM_B2) * jnp.square(g)
        m_hat = mn / (1.0 - ADAM_B1 ** ADAM_STEP)
        v_hat = vn / (1.0 - ADAM_B2 ** ADAM_STEP)
        g_ref[...] = g
        d_ref[...] = -ADAM_LR * (m_hat / (jnp.sqrt(v_hat) + ADAM_EPS) + ADAM_WD * w_ref[...])
        mo_ref[...] = mn
        vo_ref[...] = vn

    loc = pl.BlockSpec((tr, cols), lambda i: (i % nth, 0))
    glob = pl.BlockSpec((tr, cols), lambda i: (off + i, 0))
    extra = [] if prev is None else list(prev)
    return pl.pallas_call(
        body, name=name, grid=(rows // tr,),
        in_specs=[glob] * 3 + [loc] * len(gs) + [pl.BlockSpec(memory_space=pl.ANY)] * len(extra),
        out_specs=[glob] * 4, out_shape=[SDS(w.shape, f32)] * 4,
        input_output_aliases={3 + len(gs) + j: j for j in range(len(extra))},
        compiler_params=_cp(("arbitrary",)),
    )(w, m, v, *gs, *extra)


BIG = ("w_in", "w_out", "w_ff1", "w_ff2")
SMALL = ("norm1_g", "conv_w", "a_log", "dt_bias", "dn_out_g", "sb_q_g", "sb_k_g", "sg_v_g", "sg_w", "sg_b", "norm2_g")
WEIGHTS = ("norm1_g", "w_in", "conv_w", "a_log", "dt_bias", "dn_out_g", "sb_q_g", "sb_k_g", "sg_v_g", "sg_w", "sg_b",
           "w_out", "norm2_g", "w_ff1", "w_ff2")


PACK_ROWS = 256


def _rows_of(shape):
    n = 1
    for d in shape:
        n *= d
    return -(-n // 1024) * 8, n


def _pack(arrs):
    parts = []
    for a in arrs:
        r, n = _rows_of(a.shape)
        parts.append(jnp.pad(a.reshape(-1), (0, r * 128 - n)).reshape(r, 128))
    rows = sum(p.shape[0] for p in parts)
    parts.append(jnp.zeros((-rows % PACK_ROWS, 128), arrs[0].dtype))
    return jnp.concatenate(parts, axis=0)


def _unpack(packed, shapes):
    out, o = [], 0
    for s in shapes:
        r, n = _rows_of(s)
        out.append(packed[o:o + r].reshape(-1)[0:n].reshape(s))
        o += r
    return out


def kernel(x, norm1_g, w_in, conv_w, a_log, dt_bias, dn_out_g, sb_q_g, sb_k_g, sg_v_g, sg_w, sg_b, w_out, norm2_g, w_ff1, w_ff2, loss_target, m_norm1_g, m_w_in, m_conv_w, m_a_log, m_dt_bias, m_dn_out_g, m_sb_q_g, m_sb_k_g, m_sg_v_g, m_sg_w, m_sg_b, m_w_out, m_norm2_g, m_w_ff1, m_w_ff2, v_norm1_g, v_w_in, v_conv_w, v_a_log, v_dt_bias, v_dn_out_g, v_sb_q_g, v_sb_k_g, v_sg_v_g, v_sg_w, v_sg_b, v_w_out, v_norm2_g, v_w_ff1, v_w_ff2):
    w = dict(norm1_g=norm1_g, w_in=w_in, conv_w=conv_w, a_log=a_log, dt_bias=dt_bias, dn_out_g=dn_out_g, sb_q_g=sb_q_g,
             sb_k_g=sb_k_g, sg_v_g=sg_v_g, sg_w=sg_w, sg_b=sg_b, w_out=w_out, norm2_g=norm2_g, w_ff1=w_ff1, w_ff2=w_ff2)
    mom = dict(norm1_g=m_norm1_g, w_in=m_w_in, conv_w=m_conv_w, a_log=m_a_log, dt_bias=m_dt_bias, dn_out_g=m_dn_out_g,
               sb_q_g=m_sb_q_g, sb_k_g=m_sb_k_g, sg_v_g=m_sg_v_g, sg_w=m_sg_w, sg_b=m_sg_b, w_out=m_w_out, norm2_g=m_norm2_g,
               w_ff1=m_w_ff1, w_ff2=m_w_ff2)
    var = dict(norm1_g=v_norm1_g, w_in=v_w_in, conv_w=v_conv_w, a_log=v_a_log, dt_bias=v_dt_bias, dn_out_g=v_dn_out_g,
               sb_q_g=v_sb_q_g, sb_k_g=v_sb_k_g, sg_v_g=v_sg_v_g, sg_w=v_sg_w, sg_b=v_sg_b, w_out=v_w_out, norm2_g=v_norm2_g,
               w_ff1=v_w_ff1, w_ff2=v_w_ff2)
    chip = 2 * lax.axis_index("x") + lax.axis_index("y")

    wb = [{k: w[k][l].astype(bf16) for k in BIG} for l in range(DEPTH)]
    ags = {0: exchange_start([(conv_w, None, False)] + [(wb[0][k], None, True) for k in BIG], "allgather_start_0", scatter=False)}
    item = lambda l, k: (l, (l == 0) + BIG.index(k))

    def landed(items, after, name):
        ag, ks = ags[items[0][0]], [k for _, k in items]
        zones = exchange_wait(ag, ks, after, name)
        halved = [t for t, k in enumerate(ks) if ag["halve"][k]]
        for t, z in zip(halved, swap_halves([zones[t] for t in halved], name.replace("wait", "pass"))):
            zones[t] = z
        return [lax.dynamic_update_slice_in_dim(z, ag["src"][k][0][None], chip, axis=0) for z, k in zip(zones, ks)]

    def whole(k, z):
        if k == "w_in":
            return w_in_from_shards(z)
        return z if k == "w_ff1" else z.reshape(-1, D_MODEL)

    early = [d["w_in"].reshape(-1, d["w_in"].shape[-1]) for d in (w, mom, var)] + [wb[1][k] for k in BIG]
    g_conv, first_in = landed([(0, 0), item(0, "w_in")], early, "allgather_wait_in0")
    small = {k: w[k] for k in SMALL}
    small["conv_w"] = jnp.transpose(g_conv, (1, 2, 0, 3)).reshape(DEPTH, DN_CONV, 3 * DN_WIDTH)
    cache = {}

    def get_w(l, part, after):
        if part == "in":
            return whole("w_in", first_in if l == 0 else landed([item(l, "w_in")], after, f"allgather_wait_in{l}")[0])
        if part == "out":
            zs = landed([item(l, k) for k in ("w_out", "w_ff1", "w_ff2")], after, f"allgather_wait_rest{l}")
            token = jnp.zeros((), f32)
            if l + 1 < DEPTH:
                ags[l + 1] = exchange_start([(wb[l + 1][k], None, True) for k in BIG], f"allgather_start_{l + 1}",
                                            scatter=False, after=zs[0])
                token = ags[l + 1]["token"][0, 0]
            cache[l] = (whole("w_ff1", zs[1]), whole("w_ff2", zs[2]), token)
            return whole("w_out", zs[0])
        return cache[l]

    rs, pending = {}, []
    ids = jnp.stack([chip, lax.axis_index("c")]).astype(jnp.int32)

    def put_g(l, tag, g):
        names = [k for k in BIG if k in g]
        by_dest = [w_in_grad_to_shards(g[k]) if k == "w_in" else g[k] for k in names]
        halves = [a.reshape(N_CHIPS, 2, -1, a.shape[-1]) for a in by_dest]
        st = sibling_start(halves, f"pair_swap_start_{tag}{l}", other_half=True)
        pending.append((l, tag, names, st))
        return st["token"][0, 0]

    def sync_g(after):
        token = jnp.zeros((), f32)
        while pending:
            l, tag, names, st = pending.pop(0)
            halves, got = sibling_wait(st, after, f"pair_swap_wait_{tag}{l}")
            pair = [pair_sum(ids, a, b, f"pair_sum_{k}_{l}") for k, a, b in zip(names, halves, got)]
            rs[l, tag] = dict(exchange_start([(a, None, False) for a in pair], f"scatter_start_{tag}{l}", scatter=True), names=names)
            token = token + rs[l, tag]["token"][0, 0]
        return token

    lossp, grad_x, gsmall = local_step(x, loss_target, small, get_w, put_g, sync_g)

    def sum_group(l, tag, after):
        st = rs[l, tag]
        zones = exchange_wait(st, list(range(len(st["names"]))), after, f"scatter_wait_{tag}{l}")
        sums = [sum_partials(ids, zones[i], st["src"][i][0], f"sum_{k}_{l}") for i, k in enumerate(st["names"])]
        return sibling_start(sums, f"swap_sums_start_{tag}{l}")

    def update_group(l, tag, swap, after, prev):
        sums, others = sibling_wait(swap, after, f"swap_sums_wait_{tag}{l}")
        outs = dict(prev)
        for i, k in enumerate(rs[l, tag]["names"]):
            r2 = lambda a: a.reshape(-1, a.shape[-1])
            outs[k] = adamw(r2(w[k]), r2(mom[k]), r2(var[k]), (sums[i], others[i]), f"adamw_{k}_{l}", layer=l, prev=prev.get(k))
        return outs

    swap_r = sum_group(1, "rest", rs[0, "in"]["token"])
    swap_i = sum_group(1, "in", swap_r["token"])
    done = update_group(1, "rest", swap_r, swap_i["token"], {})
    done = update_group(1, "in", swap_i, done["w_ff2"][0], done)
    res = {}

    full_shapes = [(DEPTH,) + tuple(gsmall[0][k].shape) for k in SMALL]
    packed = _pack([jnp.stack([gsmall[l][k] for l in range(DEPTH)]) for k in SMALL] + [jnp.sum(lossp).reshape(1)])
    *totals, loss = _unpack(allreduce_small(packed), full_shapes + [(1,)])
    loss = loss[0]
    gfull = dict(zip(SMALL, totals))
    cs = 3 * DN_WIDTH // N_CHIPS
    gfull["conv_w"] = lax.dynamic_slice_in_dim(gfull["conv_w"], chip * cs, cs, axis=2)
    gp, wp, mp, vp = (_pack([d[k] for k in SMALL]) for d in (gfull, w, mom, var))
    outs = adamw(wp, mp, vp, (gp,), "adamw_small")
    loc_shapes = [w[k].shape for k in SMALL]
    unp = [_unpack(o, loc_shapes) for o in outs]
    for i, k in enumerate(SMALL):
        res[k] = [unp[j][i] for j in range(4)]

    swap_r = sum_group(0, "rest", outs[0])
    swap_i = sum_group(0, "in", swap_r["token"])
    done = update_group(0, "rest", swap_r, swap_i["token"], done)
    done = update_group(0, "in", swap_i, done["w_ff2"][0], done)
    for k in BIG:
        res[k] = [o.reshape(w[k].shape) for o in done[k]]

    return (loss, grad_x, *[res[k][0] for k in WEIGHTS], *[res[k][1] for k in WEIGHTS], *[res[k][2] for k in WEIGHTS],
            *[res[k][3] for k in WEIGHTS])
```

```python
import functools

import jax
import jax.numpy as jnp
from jax import lax
from jax.experimental import pallas as pl
from jax.experimental.pallas import tpu as pltpu

f32 = jnp.float32
bf16 = jnp.bfloat16
SDS = jax.ShapeDtypeStruct
MESH = pl.DeviceIdType.MESH

NORM_EPS = 1e-6
D_MODEL = 1024
DEPTH = 2
DN_HEADS, DN_DIM, DN_WIDTH, DN_CONV, DN_CHUNK = 4, 128, 512, 4, 64
SB_HEADS, SB_DIM, SB_WIDTH = 4, 64, 256
SG_GROUPS, SG_DIM, SG_WIDTH, SG_CHUNK = 4, 64, 256, 128
D_FF = 4096
IN_DIM = 3336
C_QKV, C_Z, C_AB, C_SB, C_SG, IN_PAD = 0, 1536, 2048, 2304, 3072, 3584
DN_COLS = C_SB
N_CHIPS = 4

ADAM_LR, ADAM_B1, ADAM_B2, ADAM_EPS, ADAM_WD, ADAM_STEP = 0.001, 0.9, 0.999, 1e-08, 0.01, 10

VMEM_LIMIT = 56 * 1024 * 1024


def _cp(sem=None, **kw):
    if sem is not None:
        kw["dimension_semantics"] = sem
    return pltpu.CompilerParams(vmem_limit_bytes=VMEM_LIMIT, **kw)


def _split2(x):
    hi = x.astype(bf16)
    lo = (x - hi.astype(f32)).astype(bf16)
    return hi, lo


NT = (((1,), (1,)), ((), ()))
TN = (((0,), (0,)), ((), ()))
_DIMS2 = dict(nn=(((1,), (0,)), ((), ())), nt=NT, tn=TN)
_DIMS3 = dict(nn=(((2,), (1,)), ((0,), (0,))), nt=(((2,), (2,)), ((0,), (0,))), tn=(((1,), (1,)), ((0,), (0,))))


def _dg(a, b, kind):
    return lax.dot_general(a, b, (_DIMS2 if a.ndim == 2 else _DIMS3)[kind], preferred_element_type=f32)


def _pdot(a, b):
    return _dg(a, b, "nn")


def _dot_hp(a, b):
    ah, al = _split2(a)
    bh, bl = _split2(b)
    return _pdot(ah, bh) + _pdot(ah, bl) + _pdot(al, bh)


def _dot_x2c(a, m):
    lead = a.shape[:-1]
    ah, al = _split2(a.reshape(-1, a.shape[-1]))
    return (_pdot(ah, m) + _pdot(al, m)).reshape(lead + (m.shape[1],))


def _dot_cx2(m, a):
    if a.ndim == 3:
        m = jnp.broadcast_to(m, (a.shape[0],) + m.shape)
    ah, al = _split2(a)
    return _pdot(m, ah) + _pdot(m, al)


def _nt(a, b):
    return _dg(a.astype(bf16), b.astype(bf16), "nt")


def _tn(a, b):
    return _dg(a.astype(bf16), b.astype(bf16), "tn")


def _nn(a, b):
    return _dg(a.astype(bf16), b.astype(bf16), "nn")


@jax.custom_vjp
def mm(a, b):
    return _nn(a, b)


mm.defvjp(lambda a, b: (_nn(a, b), (a, b)), lambda r, g: (_nt(g, r[1]), _tn(r[0], g)))


@jax.custom_vjp
def mm_nt(a, b):
    return _nt(a, b)


mm_nt.defvjp(lambda a, b: (_nt(a, b), (a, b)), lambda r, g: (_nn(g, r[1]), _tn(g, r[0])))


@jax.custom_vjp
def mm_tn(a, b):
    return _tn(a, b)


mm_tn.defvjp(lambda a, b: (_tn(a, b), (a, b)), lambda r, g: (_nt(r[1], g), _nn(r[0], g)))


@jax.custom_vjp
def rmul_const(a, m, mt):
    return _dot_x2c(a, m)


rmul_const.defvjp(lambda a, m, mt: (_dot_x2c(a, m), (m, mt)),
                  lambda r, g: (_dot_x2c(g, r[1]), jnp.zeros_like(r[0]), jnp.zeros_like(r[1])))


@jax.custom_vjp
def lmul_const(m, mt, a):
    return _dot_cx2(m, a)


lmul_const.defvjp(lambda m, mt, a: (_dot_cx2(m, a), (m, mt)),
                  lambda r, g: (jnp.zeros_like(r[0]), jnp.zeros_like(r[1]), _dot_cx2(r[1], g)))


@jax.custom_vjp
def mm_hl(t, x):
    th, tl = _split2(t)
    xb = x.astype(bf16)
    return _pdot(th, xb) + _pdot(tl, xb)


def _mm_hl_bwd(r, g):
    t, x = r
    th, tl = _split2(t)
    gb = g.astype(bf16)
    return _nt(g, x), _dg(th, gb, "tn") + _dg(tl, gb, "tn")


mm_hl.defvjp(lambda t, x: (mm_hl(t, x), (t, x)), _mm_hl_bwd)


def inv_unit_lower(lm):
    c = lm.shape[-1]
    r, cc = _iota2((c, c))
    eye = (r == cc).astype(f32)
    t = eye - lm
    p = -lm
    k = 1
    while 2 * k < c:
        p = _nn(p, p)
        t = t + _nn(t, p)
        k *= 2
    res = eye - t - _dot_hp(lm, t)
    return t + _nn(t, res)


@jax.custom_vjp
def inv_given(lm, t):
    return t


inv_given.defvjp(lambda lm, t: (t, t), lambda t, g: (-_nt(_tn(t, g), t), jnp.zeros_like(t)))


def _sigmoid(x):
    return 1.0 / (1.0 + jnp.exp(-x))


def _softplus(x):
    return jnp.maximum(x, 0.0) + jnp.log(1.0 + jnp.exp(-jnp.abs(x)))


def _silu(x):
    return x * _sigmoid(x)


def _gelu(x):
    return 0.5 * x * (1.0 + jnp.tanh(0.7978845608028654 * (x + 0.044715 * (x * x * x))))


def _iota2(shape):
    return lax.broadcasted_iota(jnp.int32, shape, 0), lax.broadcasted_iota(jnp.int32, shape, 1)


def _group_avg_mats():
    r, c = _iota2((128, 128))
    return jnp.where((r // 64) == (c // 64), 1.0 / 64.0, 0.0).astype(bf16)


def _pair_norm(x, gain, bavg):
    ms = rmul_const(x * x, bavg, bavg)
    return x * lax.rsqrt(ms + NORM_EPS) * gain


def _rms(x):
    r = lax.rsqrt(jnp.mean(x * x, axis=-1, keepdims=True) + NORM_EPS)
    return r


_IN_GROUPS = ((C_QKV, C_Z), (C_Z, C_AB), (C_AB, C_AB + 128), (C_SB, C_SG), (C_SG, IN_PAD))


def inproj_fwd(x, g, wp, tm=256):
    m = x.shape[0]

    def body(x_ref, g_ref, w_ref, *outs):
        xv = x_ref[...]
        h = (xv * _rms(xv) * g_ref[...]).astype(bf16)
        outs[-1][...] = h
        for (a, b), o in zip(_IN_GROUPS, outs):
            o[...] = _pdot(h, w_ref[:, a:b])

    widths = [b - a for a, b in _IN_GROUPS]
    return pl.pallas_call(
        body, name="inproj_fwd", grid=(m // tm,),
        in_specs=[pl.BlockSpec((tm, D_MODEL), lambda i: (i, 0)), pl.BlockSpec((1, D_MODEL), lambda i: (0, 0)),
                  pl.BlockSpec((D_MODEL, IN_PAD), lambda i: (0, 0))],
        out_specs=[pl.BlockSpec((tm, wd), lambda i: (i, 0)) for wd in widths + [D_MODEL]],
        out_shape=[SDS((m, wd), f32) for wd in widths] + [SDS((m, D_MODEL), bf16)],
        compiler_params=_cp(("arbitrary",)),
    )(x, g, wp)


def inproj_bwd(x, g, wp, dproj, dres, tm=256):
    m = x.shape[0]

    def body(x_ref, g_ref, w_ref, dp_ref, dr_ref, dx_ref, dg_ref):
        xv = x_ref[...]
        r = _rms(xv)
        xn = xv * r
        gv = g_ref[...]
        dh = lax.dot_general(dp_ref[...], w_ref[...], NT, preferred_element_type=f32)
        dxn = dh * gv
        dx_ref[...] = dr_ref[...] + r * (dxn - xn * jnp.mean(dxn * xn, axis=-1, keepdims=True))

        @pl.when(pl.program_id(0) == 0)
        def _():
            dg_ref[...] = jnp.zeros_like(dg_ref)

        dg_ref[...] += jnp.sum(dh * xn, axis=0, keepdims=True)

    return pl.pallas_call(
        body, name="inproj_bwd", grid=(m // tm,),
        in_specs=[pl.BlockSpec((tm, D_MODEL), lambda i: (i, 0)), pl.BlockSpec((1, D_MODEL), lambda i: (0, 0)),
                  pl.BlockSpec((D_MODEL, IN_PAD), lambda i: (0, 0)), pl.BlockSpec((tm, IN_PAD), lambda i: (i, 0)),
                  pl.BlockSpec((tm, D_MODEL), lambda i: (i, 0))],
        out_specs=[pl.BlockSpec((tm, D_MODEL), lambda i: (i, 0)), pl.BlockSpec((1, D_MODEL), lambda i: (0, 0))],
        out_shape=[SDS((m, D_MODEL), f32), SDS((1, D_MODEL), f32)],
        compiler_params=_cp(("arbitrary",)),
    )(x, g, wp, dproj, dres)


def outproj_fwd(x, odn, osb, osg, wo, tm=512):
    m = x.shape[0]

    def body(x_ref, a_ref, b_ref, c_ref, w_ref, x2_ref, mix_ref):
        mix_ref[:, 0:DN_WIDTH] = a_ref[...].astype(bf16)
        mix_ref[:, DN_WIDTH:DN_WIDTH + SB_WIDTH] = b_ref[...].astype(bf16)
        mix_ref[:, DN_WIDTH + SB_WIDTH:D_MODEL] = c_ref[...].astype(bf16)
        x2_ref[...] = x_ref[...] + _pdot(mix_ref[...], w_ref[...])

    row = lambda w: pl.BlockSpec((tm, w), lambda i: (i, 0))
    return pl.pallas_call(
        body, name="outproj_fwd", grid=(m // tm,),
        in_specs=[row(D_MODEL), row(DN_WIDTH), row(SB_WIDTH), row(SG_WIDTH), pl.BlockSpec((D_MODEL, D_MODEL), lambda i: (0, 0))],
        out_specs=[row(D_MODEL), row(D_MODEL)],
        out_shape=[SDS((m, D_MODEL), f32), SDS((m, D_MODEL), bf16)],
        compiler_params=_cp(("arbitrary",)),
    )(x, odn, osb, osg, wo)


def outproj_bwd(dx2, wo, tm=512):
    m = dx2.shape[0]

    def body(d_ref, w_ref, a_ref, b_ref, c_ref, db_ref):
        db = d_ref[...].astype(bf16)
        db_ref[...] = db
        dm = lax.dot_general(db, w_ref[...], NT, preferred_element_type=f32)
        a_ref[...] = dm[:, 0:DN_WIDTH]
        b_ref[...] = dm[:, DN_WIDTH:DN_WIDTH + SB_WIDTH]
        c_ref[...] = dm[:, DN_WIDTH + SB_WIDTH:D_MODEL]

    row = lambda w: pl.BlockSpec((tm, w), lambda i: (i, 0))
    return pl.pallas_call(
        body, name="outproj_bwd", grid=(m // tm,),
        in_specs=[row(D_MODEL), pl.BlockSpec((D_MODEL, D_MODEL), lambda i: (0, 0))],
        out_specs=[row(DN_WIDTH), row(SB_WIDTH), row(SG_WIDTH), row(D_MODEL)],
        out_shape=[SDS((m, DN_WIDTH), f32), SDS((m, SB_WIDTH), f32), SDS((m, SG_WIDTH), f32), SDS((m, D_MODEL), bf16)],
        compiler_params=_cp(("arbitrary",)),
    )(dx2, wo)


FF_CHUNK = D_FF // N_CHIPS


def _load_weights_once(pairs, sem):
    @pl.when(pl.program_id(0) == 0)
    def _():
        cps = [pltpu.make_async_copy(h, v, sem.at[i]) for i, (h, v) in enumerate(pairs)]
        for c in cps:
            c.start()
        for c in cps:
            c.wait()


def ffn_fwd(x2, g, w1, w2, tgt=None, tm=256):
    m = x2.shape[0]
    head = tgt is not None

    def body(x_ref, g_ref, w1_hbm, w2_hbm, *rest):
        (y_ref, rl_ref), (w1_v, w2_v, sem) = rest[head:head + 2], rest[-3:]
        _load_weights_once(((w1_hbm, w1_v), (w2_hbm, w2_v)), sem)
        xv = x_ref[...]
        h = (xv * _rms(xv) * g_ref[...]).astype(bf16)
        acc = xv
        for j in range(0, D_FF, FF_CHUNK):
            f = _pdot(h, w1_v[j // FF_CHUNK])
            rl = jnp.maximum(f, 0.0)
            rl_ref[:, j:j + FF_CHUNK] = rl.astype(bf16)
            acc = acc + _pdot((rl * rl).astype(bf16), w2_v[j:j + FF_CHUNK, :])
        if not head:
            y_ref[...] = acc
            return
        t_ref, l_ref = rest[0], rest[3]
        e = acc - t_ref[...]
        y_ref[...] = e * (1.0 / D_MODEL)

        @pl.when(pl.program_id(0) == 0)
        def _():
            l_ref[...] = jnp.zeros_like(l_ref)

        l_ref[...] += jnp.sum(e * e, axis=0, keepdims=True) * (0.5 / D_MODEL)

    row = pl.BlockSpec((tm, D_MODEL), lambda i: (i, 0))
    return pl.pallas_call(
        body, name="ffn_fwd_loss" if head else "ffn_fwd", grid=(m // tm,),
        in_specs=[row, pl.BlockSpec((1, D_MODEL), lambda i: (0, 0)), pl.BlockSpec(memory_space=pl.ANY),
                  pl.BlockSpec(memory_space=pl.ANY)] + [row] * head,
        out_specs=[row, pl.BlockSpec((tm, D_FF), lambda i: (i, 0))] + [pl.BlockSpec((1, D_MODEL), lambda i: (0, 0))] * head,
        out_shape=[SDS((m, D_MODEL), f32), SDS((m, D_FF), bf16)] + [SDS((1, D_MODEL), f32)] * head,
        scratch_shapes=[pltpu.VMEM((N_CHIPS, D_MODEL, FF_CHUNK), bf16), pltpu.VMEM((D_FF, D_MODEL), bf16), pltpu.SemaphoreType.DMA((2,))],
        compiler_params=_cp(("arbitrary",)),
    )(x2, g, w1, w2, *([tgt] if head else []))


def ffn_bwd(x2, g, w1, w2, rlb, dy, tm=256):
    m = x2.shape[0]

    def body(x_ref, g_ref, w1_hbm, w2_hbm, rl_ref, dy_ref, dx_ref, dg_ref, h_ref, a_ref, df_ref, dyb_ref, w1_v, w2_v, sem):
        _load_weights_once(((w1_hbm, w1_v), (w2_hbm, w2_v)), sem)
        xv = x_ref[...]
        r = _rms(xv)
        xn = xv * r
        gv = g_ref[...]
        h = (xn * gv).astype(bf16)
        h_ref[...] = h
        dyv = dy_ref[...]
        dyb = dyv.astype(bf16)
        dyb_ref[...] = dyb
        dh = jnp.zeros((tm, D_MODEL), f32)
        for j in range(0, D_FF, FF_CHUNK):
            rl = rl_ref[:, j:j + FF_CHUNK].astype(f32)
            a_ref[:, j:j + FF_CHUNK] = (rl * rl).astype(bf16)
            da = lax.dot_general(dyb, w2_v[j:j + FF_CHUNK, :], NT, preferred_element_type=f32)
            df = (da * (2.0 * rl)).astype(bf16)
            df_ref[:, j:j + FF_CHUNK] = df
            dh = dh + lax.dot_general(df, w1_v[j // FF_CHUNK], NT, preferred_element_type=f32)
        dxn = dh * gv
        dx_ref[...] = dyv + r * (dxn - xn * jnp.mean(dxn * xn, axis=-1, keepdims=True))

        @pl.when(pl.program_id(0) == 0)
        def _():
            dg_ref[...] = jnp.zeros_like(dg_ref)

        dg_ref[...] += jnp.sum(dh * xn, axis=0, keepdims=True)

    row = lambda w: pl.BlockSpec((tm, w), lambda i: (i, 0))
    return pl.pallas_call(
        body, name="ffn_bwd", grid=(m // tm,),
        in_specs=[row(D_MODEL), pl.BlockSpec((1, D_MODEL), lambda i: (0, 0)),
                  pl.BlockSpec(memory_space=pl.ANY), pl.BlockSpec(memory_space=pl.ANY), row(D_FF), row(D_MODEL)],
        out_specs=[row(D_MODEL), pl.BlockSpec((1, D_MODEL), lambda i: (0, 0)), row(D_MODEL), row(D_FF), row(D_FF), row(D_MODEL)],
        out_shape=[SDS((m, D_MODEL), f32), SDS((1, D_MODEL), f32), SDS((m, D_MODEL), bf16), SDS((m, D_FF), bf16),
                   SDS((m, D_FF), bf16), SDS((m, D_MODEL), bf16)],
        scratch_shapes=[pltpu.VMEM((N_CHIPS, D_MODEL, FF_CHUNK), bf16), pltpu.VMEM((D_FF, D_MODEL), bf16), pltpu.SemaphoreType.DMA((2,))],
        compiler_params=_cp(("arbitrary",)),
    )(x2, g, w1, w2, rlb, dy)


def _tile(n, cap):
    best = 128
    for t in range(128, cap + 1, 128):
        if n % t == 0:
            best = t
    return best


def tn_matmul(a, b, name, col_shards=1, tk=2048):
    m, ka = a.shape
    n = b.shape[1]
    ti = _tile(ka, 1024)
    tj = _tile(n // col_shards, 1152)
    tk = min(tk, m)
    nk = m // tk
    jps = (n // col_shards) // tj

    def body(a_ref, b_ref, o_ref, acc):
        k = pl.program_id(2)

        @pl.when(k == 0)
        def _():
            acc[...] = jnp.zeros_like(acc)

        acc[...] += lax.dot_general(a_ref[...], b_ref[...], TN, preferred_element_type=f32)

        @pl.when(k == nk - 1)
        def _():
            o_ref[...] = acc[...].astype(bf16).reshape(o_ref.shape)

    if col_shards == 1:
        out_shape, out_spec = SDS((ka, n), bf16), pl.BlockSpec((ti, tj), lambda i, j, k: (i, j))
    else:
        out_shape = SDS((col_shards, ka, n // col_shards), bf16)
        out_spec = pl.BlockSpec((1, ti, tj), lambda i, j, k: (j // jps, i, j % jps))
    return pl.pallas_call(
        body, name=name, grid=(ka // ti, n // tj, nk),
        in_specs=[pl.BlockSpec((tk, ti), lambda i, j, k: (k, i)), pl.BlockSpec((tk, tj), lambda i, j, k: (k, j))],
        out_specs=out_spec, out_shape=out_shape,
        scratch_shapes=[pltpu.VMEM((ti, tj), f32)],
        compiler_params=_cp(("arbitrary", "arbitrary", "arbitrary")),
    )(a, b)


def _dn_consts():
    c = DN_CHUNK
    r, cc = _iota2((c, c))
    lt = (cc <= r).astype(bf16)
    ltt = (r <= cc).astype(bf16)
    return lt, ltt


def dn_chunk(cq, ck, cv, g, beta, z, s, gain, lt, ltt, t_given=None):
    c = DN_CHUNK
    r, cc = _iota2((c, c))
    q = cq * lax.rsqrt(jnp.sum(cq * cq, axis=-1, keepdims=True) + NORM_EPS) * (DN_DIM ** -0.5)
    k = ck * lax.rsqrt(jnp.sum(ck * ck, axis=-1, keepdims=True) + NORM_EPS)
    r2, c2 = _iota2((c, 128))
    uaug = jnp.where((c2 < c) & (r2 > c2), 1.0, 0.0) + jnp.where(c2 == c, 1.0, 0.0)
    gam_all = lmul_const(lt, ltt, g * uaug)
    gam_cc = gam_all[:, :, 0:c]
    gam = gam_all[:, :, c:c + 1]
    dec = jnp.where(cc <= r, jnp.exp(jnp.where(cc <= r, gam_cc, 0.0)), 0.0)
    kk = mm_nt(k, k)
    lm = jnp.where(cc < r, beta * kk * dec, 0.0)
    t = inv_unit_lower(lm) if t_given is None else inv_given(lm, t_given)
    eg = jnp.exp(gam)
    sol = mm_hl(t, jnp.concatenate([cv * beta, k * (beta * eg)], axis=2))
    u, w = sol[:, :, 0:DN_DIM], sol[:, :, DN_DIM:2 * DN_DIM]
    qk = jnp.where(cc <= r, mm_nt(q, k) * dec, 0.0)
    glast = jnp.sum(g, axis=1, keepdims=True)
    qd = q * eg
    kd = k * jnp.exp(glast - gam)
    un = u - mm(w, s)
    o = mm(qd, s) + mm(qk, un)
    s_new = s * jnp.exp(glast) + mm_tn(kd, un)
    on = o * lax.rsqrt(jnp.mean(o * o, axis=-1, keepdims=True) + NORM_EPS) * gain * _silu(z)
    return on, s_new, t


def _dn_gates(ab, al_row, dt_row):
    pre = ab + dt_row
    return -jnp.exp(al_row) * _softplus(pre), _sigmoid(ab), _sigmoid(pre)


def _dn_chains(cacts, gates, z_ref):
    cq, ck, cv, g, beta, z = [], [], [], [], [], []
    for bi, cact in enumerate(cacts):
        for h in range(DN_HEADS):
            cq.append(cact[:, h * DN_DIM:(h + 1) * DN_DIM])
            ck.append(cact[:, DN_WIDTH + h * DN_DIM:DN_WIDTH + (h + 1) * DN_DIM])
            cv.append(cact[:, 2 * DN_WIDTH + h * DN_DIM:2 * DN_WIDTH + (h + 1) * DN_DIM])
            g.append(gates[bi][0][:, h:h + 1])
            beta.append(gates[bi][1][:, DN_HEADS + h:DN_HEADS + h + 1])
            z.append(z_ref[bi, :, h * DN_DIM:(h + 1) * DN_DIM])
    return tuple(jnp.stack(v) for v in (cq, ck, cv, g, beta, z))


def _conv_rows(xe_ref, b, w_ref):
    y = w_ref[0:1, :] * xe_ref[b, pl.ds(5, DN_CHUNK), :]
    for i in range(1, DN_CONV):
        y = y + w_ref[i:i + 1, :] * xe_ref[b, pl.ds(5 + i, DN_CHUNK), :]
    return y


def dn_fwd(qkv, z, ab, conv_w, alog, dtb, gain):
    bsz, t, _ = qkv.shape
    nc = t // DN_CHUNK
    c = DN_CHUNK
    nh = bsz * DN_HEADS

    def body(qkv_ref, z_ref, ab_ref, w_ref, al_ref, dt_ref, g_ref, o_ref, sall_ref, tall_ref, xe, s_sc):
        n = pl.program_id(0)

        @pl.when(n == 0)
        def _():
            xe[:, 0:8, :] = jnp.zeros((bsz, 8, 3 * DN_WIDTH), f32)
            s_sc[...] = jnp.zeros_like(s_sc)

        lt, ltt = _dn_consts()
        cacts = []
        for b in range(bsz):
            xe[b, 8:8 + c, :] = qkv_ref[b]
            cacts.append(_silu(_conv_rows(xe, b, w_ref)))
            xe[b, 0:8, :] = xe[b, c:c + 8, :]
        gates = [_dn_gates(ab_ref[b], al_ref[...], dt_ref[...]) for b in range(bsz)]
        s = s_sc[...]
        sall_ref[0] = s
        on, sn, tt = dn_chunk(*_dn_chains(cacts, gates, z_ref), s, g_ref[...], lt, ltt)
        tall_ref[0] = tt
        s_sc[...] = sn
        for b in range(bsz):
            for h in range(DN_HEADS):
                o_ref[b, :, h * DN_DIM:(h + 1) * DN_DIM] = on[b * DN_HEADS + h]

    blk = lambda w: pl.BlockSpec((bsz, c, w), lambda n: (0, n, 0))
    full = lambda shp: pl.BlockSpec(shp, lambda n: (0,) * len(shp))
    return pl.pallas_call(
        body, name="dn_fwd", grid=(nc,),
        in_specs=[blk(3 * DN_WIDTH), blk(DN_WIDTH), blk(128), full((8, 3 * DN_WIDTH)), full((1, 128)), full((1, 128)), full((1, 128))],
        out_specs=[blk(DN_WIDTH), pl.BlockSpec((1, nh, DN_DIM, DN_DIM), lambda n: (n, 0, 0, 0)),
                   pl.BlockSpec((1, nh, c, c), lambda n: (n, 0, 0, 0))],
        out_shape=[SDS((bsz, t, DN_WIDTH), f32), SDS((nc, nh, DN_DIM, DN_DIM), f32), SDS((nc, nh, c, c), f32)],
        scratch_shapes=[pltpu.VMEM((bsz, c + 8, 3 * DN_WIDTH), f32), pltpu.VMEM((nh, DN_DIM, DN_DIM), f32)],
        compiler_params=_cp(("arbitrary",)),
    )(qkv, z, ab, conv_w, alog, dtb, gain)


def dn_bwd(qkv, z, ab, conv_w, alog, dtb, gain, sall, tall, do):
    bsz, t, _ = qkv.shape
    nc = t // DN_CHUNK
    c = DN_CHUNK
    nh = bsz * DN_HEADS
    w3 = 3 * DN_WIDTH

    def body(qkv_ref, prev_ref, z_ref, ab_ref, w_ref, al_ref, dt_ref, g_ref, sall_ref, tall_ref, do_ref,
             dp_ref, dw_ref, dal_ref, ddt_ref, dg_ref, xe, dye, dc_sc, ds_sc):
        n = pl.program_id(0)
        first = (nc - 1 - n) == 0

        @pl.when(n == 0)
        def _():
            dye[:, c:c + 8, :] = jnp.zeros((bsz, 8, w3), f32)
            ds_sc[...] = jnp.zeros_like(ds_sc)
            dw_ref[...] = jnp.zeros_like(dw_ref)
            dal_ref[...] = jnp.zeros_like(dal_ref)
            ddt_ref[...] = jnp.zeros_like(ddt_ref)
            dg_ref[...] = jnp.zeros_like(dg_ref)

        lt, ltt = _dn_consts()
        lane_c = lax.broadcasted_iota(jnp.int32, (c, 128), 1)
        ys, sigs = [], []
        for b in range(bsz):
            xe[b, 0:8, :] = jnp.where(first, 0.0, prev_ref[b])
            xe[b, 8:8 + c, :] = qkv_ref[b]
            ys.append(_conv_rows(xe, b, w_ref))
            sigs.append(_sigmoid(ys[b]))
        gates = [_dn_gates(ab_ref[b], al_ref[...], dt_ref[...]) for b in range(bsz)]
        ops = _dn_chains([y * sg for y, sg in zip(ys, sigs)], gates, z_ref)
        tt = tall_ref[0]
        _, vjp = jax.vjp(lambda *p: dn_chunk(*p, lt, ltt, t_given=tt)[0:2], *ops, sall_ref[0], g_ref[...])
        don = jnp.stack([do_ref[b, :, h * DN_DIM:(h + 1) * DN_DIM] for b in range(bsz) for h in range(DN_HEADS)])
        dcq, dck, dcv, dg, dbeta, dzz, dsp, dgn = vjp((don, ds_sc[...]))
        ds_sc[...] = dsp
        dg_ref[...] += dgn
        for b in range(bsz):
            dgate = jnp.zeros((c, 128), f32)
            for h in range(DN_HEADS):
                i = b * DN_HEADS + h
                dc_sc[b, :, h * DN_DIM:(h + 1) * DN_DIM] = dcq[i]
                dc_sc[b, :, DN_WIDTH + h * DN_DIM:DN_WIDTH + (h + 1) * DN_DIM] = dck[i]
                dc_sc[b, :, 2 * DN_WIDTH + h * DN_DIM:2 * DN_WIDTH + (h + 1) * DN_DIM] = dcv[i]
                dp_ref[b, :, C_Z + h * DN_DIM:C_Z + (h + 1) * DN_DIM] = dzz[i].astype(bf16)
                dgate = dgate + jnp.where(lane_c == h, dg[i], 0.0) + jnp.where(lane_c == DN_HEADS + h, dbeta[i], 0.0)
            gg, beta, sig_pre = gates[b]
            is_g = lane_c < DN_HEADS
            dpre = jnp.where(is_g, dgate * (-jnp.exp(al_ref[...])) * sig_pre, 0.0)
            dp_ref[b, :, C_AB:C_AB + 128] = (dpre + jnp.where(is_g, 0.0, dgate * beta * (1.0 - beta))).astype(bf16)
            dp_ref[b, :, C_AB + 128:DN_COLS] = jnp.zeros((c, DN_COLS - C_AB - 128), bf16)
            dal_ref[...] += jnp.sum(jnp.where(is_g, dgate * gg, 0.0), axis=0, keepdims=True)
            ddt_ref[...] += jnp.sum(dpre, axis=0, keepdims=True)
            y, sig = ys[b], sigs[b]
            dy = dc_sc[b] * (sig * (1.0 + y * (1.0 - sig)))
            dye[b, 0:c, :] = dy
            dx = w_ref[3:4, :] * dy
            for i in range(DN_CONV - 1):
                dx = dx + w_ref[i:i + 1, :] * dye[b, pl.ds(3 - i, c), :]
            dp_ref[b, :, 0:w3] = dx.astype(bf16)
            for i in range(DN_CONV):
                dw_ref[i:i + 1, :] += jnp.sum(dy * xe[b, pl.ds(5 + i, c), :], axis=0, keepdims=True)
            dye[b, c:c + 8, :] = dye[b, 0:8, :]

    rev = lambda w: pl.BlockSpec((bsz, c, w), lambda n: (0, nc - 1 - n, 0))
    full = lambda shp: pl.BlockSpec(shp, lambda n: (0,) * len(shp))
    prev = pl.BlockSpec((bsz, 8, w3), lambda n: (0, jnp.maximum((nc - 1 - n) * (c // 8) - 1, 0), 0))
    return pl.pallas_call(
        body, name="dn_bwd", grid=(nc,),
        in_specs=[rev(w3), prev, rev(DN_WIDTH), rev(128), full((8, w3)), full((1, 128)), full((1, 128)), full((1, 128)),
                  pl.BlockSpec((1, nh, DN_DIM, DN_DIM), lambda n: (nc - 1 - n, 0, 0, 0)),
                  pl.BlockSpec((1, nh, c, c), lambda n: (nc - 1 - n, 0, 0, 0)), rev(DN_WIDTH)],
        out_specs=[rev(DN_COLS), full((8, w3)), full((1, 128)), full((1, 128)), full((1, 128))],
        out_shape=[SDS((bsz, t, IN_PAD), bf16), SDS((8, w3), f32), SDS((1, 128), f32), SDS((1, 128), f32), SDS((1, 128), f32)],
        scratch_shapes=[pltpu.VMEM((bsz, c + 8, w3), f32), pltpu.VMEM((bsz, c + 8, w3), f32), pltpu.VMEM((bsz, c, w3), f32),
                        pltpu.VMEM((nh, DN_DIM, DN_DIM), f32)],
        compiler_params=_cp(("arbitrary",)),
    )(qkv, qkv, z, ab, conv_w, alog, dtb, gain, sall, tall, do)


SB_TILE = 256
SB_QTILE, SB_KTILE = 256, 256
SB_PAIRS = SB_HEADS // 2


def sb_fwd(sbqkv, gq, gk):
    bsz, t, _ = sbqkv.shape
    bq = min(SB_QTILE, t)
    blk = max(min(SB_KTILE, t), bq)
    nq = t // bq
    scale = SB_DIM ** -0.5

    def body(q_ref, k_ref, v_ref, gq_ref, gk_ref, o_ref, l_ref, q2_sc, kn_sc, v_sc):
        bavg = _group_avg_mats()
        lane = lax.broadcasted_iota(jnp.int32, (1, 128), 1)
        first = lane < SB_DIM
        for p in range(SB_PAIRS):
            ls = slice(p * 128, (p + 1) * 128)
            qn = _pair_norm(q_ref[0, :, ls], gq_ref[...], bavg)
            kn_sc[p] = _pair_norm(k_ref[0, :, ls], gk_ref[...], bavg).astype(bf16)
            v_sc[p] = v_ref[0, :, ls].astype(bf16)
            q2_sc[2 * p] = jnp.where(first, qn, 0.0).astype(bf16)
            q2_sc[2 * p + 1] = jnp.where(first, 0.0, qn).astype(bf16)
        r, c = _iota2((blk, blk))
        ustrict = (r > c).astype(bf16)
        r2, c2 = _iota2((2 * bq, blk))

        def tile(q2s, ks, carry, causal):
            out = []
            for p in range(SB_PAIRS):
                acc, rr = carry[2 * p], carry[2 * p + 1]
                zz = lax.dot_general(q2s[p], kn_sc[p, pl.ds(ks, blk), :], NT, preferred_element_type=f32) * scale
                sp = _softplus(zz)
                lm = -sp if causal is None else jnp.where(causal, -sp, 0.0)
                rem = _dot_x2c(lm, ustrict)
                wgt = jnp.exp(zz - sp + rem + rr)
                if causal is not None:
                    wgt = jnp.where(causal, wgt, 0.0)
                out += [acc + _pdot(wgt.astype(bf16), v_sc[p, pl.ds(ks, blk), :]), rr + jnp.sum(lm, axis=1, keepdims=True)]
            return tuple(out)

        def qloop(qi, _):
            qs = pl.multiple_of(qi * bq, bq)
            kd = qs // blk
            causal = c2 < (r2 & (bq - 1)) + (qs - kd * blk)
            q2s = [jnp.concatenate([q2_sc[2 * p, pl.ds(qs, bq), :], q2_sc[2 * p + 1, pl.ds(qs, bq), :]], axis=0)
                   for p in range(SB_PAIRS)]
            zero = (jnp.zeros((2 * bq, 128), f32), jnp.zeros((2 * bq, 1), f32)) * SB_PAIRS
            carry = lax.fori_loop(1, kd + 1, lambda i, cr: tile(q2s, pl.multiple_of((kd - i) * blk, blk), cr, None),
                                  tile(q2s, pl.multiple_of(kd * blk, blk), zero, causal))
            for p in range(SB_PAIRS):
                acc, rr = carry[2 * p], carry[2 * p + 1]
                o_ref[0, pl.ds(qs, bq), p * 128:(p + 1) * 128] = jnp.where(first, acc[0:bq], acc[bq:2 * bq])
                l_ref[0, pl.ds(qs, bq), p * 128:(p + 1) * 128] = jnp.where(first, rr[0:bq], rr[bq:2 * bq])
            return 0

        lax.fori_loop(0, nq, qloop, 0)

    col = lambda off: pl.BlockSpec((1, t, SB_WIDTH), lambda b: (b, 0, off))
    gsp = pl.BlockSpec((1, 128), lambda b: (0, 0))
    return pl.pallas_call(
        body, name="sb_fwd", grid=(bsz,),
        in_specs=[col(0), col(1), col(2), gsp, gsp],
        out_specs=[col(0), col(0)],
        out_shape=[SDS((bsz, t, SB_WIDTH), f32), SDS((bsz, t, SB_WIDTH), f32)],
        scratch_shapes=[pltpu.VMEM((2 * SB_PAIRS, t, 128), bf16), pltpu.VMEM((SB_PAIRS, t, 128), bf16),
                        pltpu.VMEM((SB_PAIRS, t, 128), bf16)],
        compiler_params=_cp(("arbitrary",)),
    )(sbqkv, sbqkv, sbqkv, gq, gk)


def sb_bwd(sbqkv, gq, gk, ltot, do, dproj):
    bsz, t, _ = sbqkv.shape
    blk = min(SB_TILE, t)
    nq = t // blk
    scale = SB_DIM ** -0.5

    def body(q_ref, k_ref, v_ref, gq_ref, gk_ref, l_ref, do_ref, dp_in, dp_ref, dgq_ref, dgk_ref,
             q2_sc, kn_sc, v_sc, do2_sc, dqn_sc, dkn_sc, dv_sc):
        bavg = _group_avg_mats()
        lane = lax.broadcasted_iota(jnp.int32, (1, 128), 1)
        first = lane < SB_DIM
        fq = lambda x, g: _pair_norm(x, g, bavg)
        vjps = []
        for p in range(SB_PAIRS):
            ls = slice(p * 128, (p + 1) * 128)
            qn, q_vjp = jax.vjp(fq, q_ref[0, :, ls], gq_ref[...])
            kn, k_vjp = jax.vjp(fq, k_ref[0, :, ls], gk_ref[...])
            vjps.append((q_vjp, k_vjp))
            kn_sc[p] = kn.astype(bf16)
            v_sc[p] = v_ref[0, :, ls].astype(bf16)
            dov = do_ref[0, :, ls]
            q2_sc[2 * p] = jnp.where(first, qn, 0.0).astype(bf16)
            q2_sc[2 * p + 1] = jnp.where(first, 0.0, qn).astype(bf16)
            do2_sc[2 * p] = jnp.where(first, dov, 0.0).astype(bf16)
            do2_sc[2 * p + 1] = jnp.where(first, 0.0, dov).astype(bf16)
        dkn_sc[...] = jnp.zeros_like(dkn_sc)
        dv_sc[...] = jnp.zeros_like(dv_sc)
        r, c = _iota2((blk, blk))
        pincl = (r <= c).astype(bf16)
        pstrict = (r < c).astype(bf16)
        r2, c2 = _iota2((2 * blk, blk))
        causal = c2 < (r2 & (blk - 1))

        def tile(q2s, do2s, lts, ks, carry, diag):
            out = []
            for p in range(SB_PAIRS):
                dq, cs, ce = carry[3 * p:3 * p + 3]
                q2, do2 = q2s[p], do2s[p]
                kb = kn_sc[p, pl.ds(ks, blk), :]
                zz = lax.dot_general(q2, kb, NT, preferred_element_type=f32) * scale
                sp = _softplus(zz)
                lm = jnp.where(causal, -sp, 0.0) if diag else -sp
                pre = _dot_x2c(lm, pincl)
                lp = zz - sp
                wgt = jnp.exp(lp + (lts[p] - cs - pre))
                if diag:
                    wgt = jnp.where(causal, wgt, 0.0)
                dw = lax.dot_general(do2, v_sc[p, pl.ds(ks, blk), :], NT, preferred_element_type=f32)
                e = wgt * dw
                ee = ce + _dot_x2c(e, pstrict)
                sig = jnp.exp(lp)
                dz = (e * (1.0 - sig) - ee * sig) * scale
                if diag:
                    dz = jnp.where(causal, dz, 0.0)
                dz = dz.astype(bf16)
                dkn_sc[p, pl.ds(ks, blk), :] += lax.dot_general(dz, q2, TN, preferred_element_type=f32)
                dv_sc[p, pl.ds(ks, blk), :] += lax.dot_general(wgt.astype(bf16), do2, TN, preferred_element_type=f32)
                out += [dq + _pdot(dz, kb), cs + jnp.sum(lm, axis=1, keepdims=True), ce + jnp.sum(e, axis=1, keepdims=True)]
            return tuple(out)

        def qloop(qi, _):
            qs = pl.multiple_of(qi * blk, blk)
            rows = pl.ds(qs, blk)
            q2s = [jnp.concatenate([q2_sc[2 * p, rows, :], q2_sc[2 * p + 1, rows, :]], axis=0) for p in range(SB_PAIRS)]
            do2s = [jnp.concatenate([do2_sc[2 * p, rows, :], do2_sc[2 * p + 1, rows, :]], axis=0) for p in range(SB_PAIRS)]
            lts = [jnp.concatenate([l_ref[0, rows, p * 128:p * 128 + 1], l_ref[0, rows, p * 128 + SB_DIM:p * 128 + SB_DIM + 1]],
                                   axis=0) for p in range(SB_PAIRS)]
            z1 = jnp.zeros((2 * blk, 1), f32)
            carry = lax.fori_loop(0, qi, lambda kj, cr: tile(q2s, do2s, lts, pl.multiple_of(kj * blk, blk), cr, False),
                                  (jnp.zeros((2 * blk, 128), f32), z1, z1) * SB_PAIRS)
            carry = tile(q2s, do2s, lts, qs, carry, True)
            for p in range(SB_PAIRS):
                dq = carry[3 * p]
                dqn_sc[p, rows, :] = jnp.where(first, dq[0:blk], dq[blk:2 * blk])
            return 0

        lax.fori_loop(0, nq, qloop, 0)
        dgq_tot, dgk_tot = jnp.zeros((1, 128), f32), jnp.zeros((1, 128), f32)
        for p in range(SB_PAIRS):
            ls = slice(p * 128, (p + 1) * 128)
            dq_pre, dgq = vjps[p][0](dqn_sc[p])
            dk_pre, dgk = vjps[p][1](dkn_sc[p])
            dp_ref[0, :, p * 128:(p + 1) * 128] = dq_pre.astype(bf16)
            dp_ref[0, :, SB_WIDTH + p * 128:SB_WIDTH + (p + 1) * 128] = dk_pre.astype(bf16)
            dp_ref[0, :, 2 * SB_WIDTH + p * 128:2 * SB_WIDTH + (p + 1) * 128] = dv_sc[p].astype(bf16)
            dgq_tot, dgk_tot = dgq_tot + dgq, dgk_tot + dgk
        dgq_ref[0] = jnp.broadcast_to(dgq_tot, (8, 128))
        dgk_ref[0] = jnp.broadcast_to(dgk_tot, (8, 128))

    col = lambda off: pl.BlockSpec((1, t, SB_WIDTH), lambda b: (b, 0, off), pipeline_mode=pl.Buffered(1))
    gsp = pl.BlockSpec((1, 128), lambda b: (0, 0))
    gout = pl.BlockSpec((1, 8, 128), lambda b: (b, 0, 0))
    return pl.pallas_call(
        body, name="sb_bwd", grid=(bsz,),
        in_specs=[col(0), col(1), col(2), gsp, gsp, col(0), col(0), pl.BlockSpec(memory_space=pl.ANY)],
        out_specs=[pl.BlockSpec((1, t, 3 * SB_WIDTH), lambda b: (b, 0, C_SB // (3 * SB_WIDTH)), pipeline_mode=pl.Buffered(1)),
                   gout, gout],
        out_shape=[SDS(dproj.shape, bf16)] + [SDS((bsz, 8, 128), f32)] * 2,
        input_output_aliases={7: 0},
        scratch_shapes=[pltpu.VMEM((2 * SB_PAIRS, t, 128), bf16), pltpu.VMEM((SB_PAIRS, t, 128), bf16),
                        pltpu.VMEM((SB_PAIRS, t, 128), bf16), pltpu.VMEM((2 * SB_PAIRS, t, 128), bf16),
                        pltpu.VMEM((SB_PAIRS, t, 128), f32), pltpu.VMEM((SB_PAIRS, t, 128), f32), pltpu.VMEM((SB_PAIRS, t, 128), f32)],
        compiler_params=_cp(("arbitrary",)),
    )(sbqkv, sbqkv, sbqkv, gq, gk, ltot, do, dproj)


def sg_pair(u, v, gain, wa, wb, ba, bb, bavg):
    r, c = _iota2((SG_CHUNK, SG_CHUNK))
    lane = lax.broadcasted_iota(jnp.int32, (1, 128), 1)
    first = lane < SG_DIM
    vn = _pair_norm(_gelu(v), gain, bavg)
    tri = c <= r
    mixed = (mm(jnp.where(tri, wa, 0.0), jnp.where(first, vn, 0.0)) + mm(jnp.where(tri, wb, 0.0), jnp.where(first, 0.0, vn))
             + jnp.where(first, ba, bb))
    return _gelu(u) * mixed


def sg_fwd(sguv, gain, w, bt):
    bsz, t, _ = sguv.shape
    nch = t // SG_CHUNK

    def body(uv_ref, g_ref, w_ref, b_ref, o_ref):
        bavg = _group_avg_mats()
        for p in range(2):
            ls = slice(p * 128, (p + 1) * 128)
            o_ref[0, :, ls] = sg_pair(uv_ref[0, :, ls], uv_ref[0, :, SG_WIDTH + p * 128:SG_WIDTH + (p + 1) * 128], g_ref[:, ls],
                                      w_ref[2 * p], w_ref[2 * p + 1], b_ref[:, 2 * p:2 * p + 1], b_ref[:, 2 * p + 1:2 * p + 2], bavg)

    full = lambda shp: pl.BlockSpec(shp, lambda b, n: (0,) * len(shp))
    return pl.pallas_call(
        body, name="sg_fwd", grid=(bsz, nch),
        in_specs=[pl.BlockSpec((1, SG_CHUNK, 2 * SG_WIDTH), lambda b, n: (b, n, 0)), full((1, SG_WIDTH)),
                  full((SG_GROUPS, SG_CHUNK, SG_CHUNK)), full((SG_CHUNK, 128))],
        out_specs=pl.BlockSpec((1, SG_CHUNK, SG_WIDTH), lambda b, n: (b, n, 0)),
        out_shape=SDS((bsz, t, SG_WIDTH), f32),
        compiler_params=_cp(("arbitrary", "arbitrary")),
    )(sguv, gain, w, bt)


def sg_bwd(sguv, gain, w, bt, do, dproj):
    bsz, t, _ = sguv.shape
    nch = t // SG_CHUNK

    def body(uv_ref, g_ref, w_ref, b_ref, do_ref, dp_in, duv_ref, dg_ref, dw_ref, db_ref):
        @pl.when((pl.program_id(0) == 0) & (pl.program_id(1) == 0))
        def _():
            dg_ref[...] = jnp.zeros_like(dg_ref)
            dw_ref[...] = jnp.zeros_like(dw_ref)
            db_ref[...] = jnp.zeros_like(db_ref)

        bavg = _group_avg_mats()
        lane = lax.broadcasted_iota(jnp.int32, (SG_CHUNK, 128), 1)
        dbt = jnp.zeros((SG_CHUNK, 128), f32)
        for p in range(2):
            ls = slice(p * 128, (p + 1) * 128)
            vs = slice(SG_WIDTH + p * 128, SG_WIDTH + (p + 1) * 128)
            prim = (uv_ref[0, :, ls], uv_ref[0, :, vs], g_ref[:, ls], w_ref[2 * p], w_ref[2 * p + 1],
                    b_ref[:, 2 * p:2 * p + 1], b_ref[:, 2 * p + 1:2 * p + 2])
            _, vjp = jax.vjp(lambda *a: sg_pair(*a, bavg), *prim)
            du, dv, dgn, dwa, dwb, dba, dbb = vjp(do_ref[0, :, ls])
            duv_ref[0, :, ls] = du.astype(bf16)
            duv_ref[0, :, vs] = dv.astype(bf16)
            dg_ref[:, ls] += dgn
            dw_ref[2 * p] += dwa
            dw_ref[2 * p + 1] += dwb
            dbt = dbt + jnp.where(lane == 2 * p, dba, 0.0) + jnp.where(lane == 2 * p + 1, dbb, 0.0)
        db_ref[...] += dbt

    full = lambda shp: pl.BlockSpec(shp, lambda b, n: (0,) * len(shp))
    return pl.pallas_call(
        body, name="sg_bwd", grid=(bsz, nch),
        in_specs=[pl.BlockSpec((1, SG_CHUNK, 2 * SG_WIDTH), lambda b, n: (b, n, 0)), full((1, SG_WIDTH)),
                  full((SG_GROUPS, SG_CHUNK, SG_CHUNK)), full((SG_CHUNK, 128)),
                  pl.BlockSpec((1, SG_CHUNK, SG_WIDTH), lambda b, n: (b, n, 0)), pl.BlockSpec(memory_space=pl.ANY)],
        out_specs=[pl.BlockSpec((1, SG_CHUNK, 2 * SG_WIDTH), lambda b, n: (b, n, C_SG // (2 * SG_WIDTH))), full((1, SG_WIDTH)),
                   full((SG_GROUPS, SG_CHUNK, SG_CHUNK)), full((SG_CHUNK, 128))],
        out_shape=[SDS(dproj.shape, bf16), SDS((1, SG_WIDTH), f32), SDS((SG_GROUPS, SG_CHUNK, SG_CHUNK), f32),
                   SDS((SG_CHUNK, 128), f32)],
        input_output_aliases={5: 0},
        compiler_params=_cp(("arbitrary", "arbitrary")),
    )(sguv, gain, w, bt, do, dproj)


def _pad_lanes(v, n=128):
    return jnp.pad(v.reshape(1, -1), ((0, 0), (0, n - v.size)))


def _w_in_runs():
    shard, runs = IN_DIM // N_CHIPS, []
    for s in range(N_CHIPS):
        for a, b, d in ((0, 2048, 0), (2048, 2056, C_AB), (2056, IN_DIM, C_SB)):
            lo, hi = max(shard * s, a), min(shard * (s + 1), b)
            if lo < hi:
                runs.append((s, lo - shard * s, hi - shard * s, d + lo - a))
    return runs


def w_in_from_shards(zone, tr=256):
    def body(z_ref, o_ref):
        o_ref[:, C_AB:C_SB] = jnp.zeros((tr, C_SB - C_AB), zone.dtype)
        for s, a, b, d in _w_in_runs():
            o_ref[:, d:d + b - a] = z_ref[s, :, a:b]

    return pl.pallas_call(
        body, name="w_in_from_shards", grid=(D_MODEL // tr,),
        in_specs=[pl.BlockSpec((N_CHIPS, tr, IN_DIM // N_CHIPS), lambda i: (0, i, 0))],
        out_specs=pl.BlockSpec((tr, IN_PAD), lambda i: (i, 0)), out_shape=SDS((D_MODEL, IN_PAD), zone.dtype),
        compiler_params=_cp(("arbitrary",)))(zone)


def w_in_grad_to_shards(g, tr=256):
    def body(g_ref, o_ref):
        for s, a, b, d in _w_in_runs():
            o_ref[s, :, a:b] = g_ref[:, d:d + b - a]

    return pl.pallas_call(
        body, name="w_in_grad_to_shards", grid=(D_MODEL // tr,),
        in_specs=[pl.BlockSpec((tr, IN_PAD), lambda i: (i, 0))],
        out_specs=pl.BlockSpec((N_CHIPS, tr, IN_DIM // N_CHIPS), lambda i: (0, i, 0)),
        out_shape=SDS((N_CHIPS, D_MODEL, IN_DIM // N_CHIPS), g.dtype), compiler_params=_cp(("arbitrary",)))(g)


def layer_params(p, l):
    return dict(
        g1=p["norm1_g"][l].reshape(1, -1), g2=p["norm2_g"][l].reshape(1, -1),
        conv=jnp.pad(p["conv_w"][l], ((0, 4), (0, 0))), alog=_pad_lanes(p["a_log"][l]), dtb=_pad_lanes(p["dt_bias"][l]),
        dng=p["dn_out_g"][l].reshape(1, -1), gq=jnp.tile(p["sb_q_g"][l].reshape(1, -1), (1, 2)),
        gk=jnp.tile(p["sb_k_g"][l].reshape(1, -1), (1, 2)), sgg=p["sg_v_g"][l].reshape(1, -1), sgw=p["sg_w"][l],
        sgb=jnp.pad(p["sg_b"][l].T, ((0, 0), (0, 124))))


def local_step(x, tgt, small, get_w, put_g, sync_g):
    bsz, t, _ = x.shape
    m = bsz * t
    r3 = lambda a: a.reshape(bsz, t, a.shape[-1])
    r2 = lambda a: a.reshape(m, a.shape[-1])
    xs, saved, ws = x.reshape(m, D_MODEL), [], []
    for l in range(DEPTH):
        sp, w = layer_params(small, l), {}
        w["w_in"] = get_w(l, "in", xs)
        qkv, z, ab, sb, sg, h1 = inproj_fwd(xs, sp["g1"], w["w_in"])
        odn, sall, tall = dn_fwd(r3(qkv), r3(z), r3(ab), sp["conv"], sp["alog"], sp["dtb"], sp["dng"])
        osb, ltot = sb_fwd(r3(sb), sp["gq"], sp["gk"])
        osg = sg_fwd(r3(sg), sp["sgg"], sp["sgw"], sp["sgb"])
        w["w_out"] = get_w(l, "out", osg)
        x2, mix = outproj_fwd(xs, r2(odn), r2(osb), r2(osg), w["w_out"])
        w["w_ff1"], w["w_ff2"], started = get_w(l, "ff", x2)
        if l + 1 < DEPTH:
            xs_next, rlb = ffn_fwd(x2, sp["g2"] + started, w["w_ff1"], w["w_ff2"])
        else:
            dx, rlb, lossp = ffn_fwd(x2, sp["g2"] + started, w["w_ff1"], w["w_ff2"], tgt=tgt.reshape(m, D_MODEL))
        saved.append(dict(rlb=rlb, h1=h1, x=xs, qkv=qkv, z=z, ab=ab, sb=sb, sg=sg, sall=sall, tall=tall, ltot=ltot, mix=mix, x2=x2))
        ws.append(w)
        xs = xs_next
    gsmall = [None] * DEPTH
    token = jnp.zeros((), f32)
    for l in reversed(range(DEPTH)):
        sp, w, s = layer_params(small, l), ws[l], saved[l]
        dx2, dg2, h2, act, df, dyb = ffn_bwd(s["x2"], sp["g2"] + token, w["w_ff1"], w["w_ff2"], s["rlb"], dx)
        g_ff1 = tn_matmul(h2, df, f"dw_ff1_{l}", col_shards=N_CHIPS)
        g_ff2 = tn_matmul(act, dyb, f"dw_ff2_{l}")
        dodn, dosb, dosg, dx2b = outproj_bwd(dx2, w["w_out"])
        g_out = tn_matmul(s["mix"], dx2b, f"dw_out_{l}")
        token = token + put_g(l, "rest", dict(w_out=g_out, w_ff1=g_ff1, w_ff2=g_ff2))
        dproj, dconv, dalog, ddtb, ddng = dn_bwd(r3(s["qkv"]), r3(s["z"]), r3(s["ab"]), sp["conv"], sp["alog"], sp["dtb"],
                                                 sp["dng"] + token, s["sall"], s["tall"], r3(dodn))
        token = sync_g(ddng)
        dproj, dgq, dgk = sb_bwd(r3(s["sb"]), sp["gq"] + token, sp["gk"], s["ltot"], r3(dosb), dproj)
        dproj, dsgg, dsgw, dsgb = sg_bwd(r3(s["sg"]), sp["sgg"], sp["sgw"], sp["sgb"], r3(dosg), dproj)
        dproj = r2(dproj)
        g_in = tn_matmul(s["h1"], dproj, f"dw_in_{l}")
        token = put_g(l, "in", dict(w_in=g_in))
        dx, dg1 = inproj_bwd(s["x"], sp["g1"] + token, w["w_in"], dproj, dx2)
        token = sync_g(dg1)
        fold = lambda a: (a[:, 0, :].sum(0).reshape(2, SB_DIM)).sum(0)
        gsmall[l] = dict(norm1_g=dg1[0], conv_w=dconv[0:DN_CONV], a_log=dalog[0, 0:DN_HEADS], dt_bias=ddtb[0, 0:DN_HEADS],
                         dn_out_g=ddng[0], sb_q_g=fold(dgq), sb_k_g=fold(dgk), sg_v_g=dsgg[0], sg_w=dsgw,
                         sg_b=dsgb[:, 0:SG_GROUPS].T, norm2_g=dg2[0])
    return lossp, dx.reshape(bsz, t, D_MODEL), gsmall


def _chip_peers(x, y):
    return [(1 - x, y), (x, 1 - y), (1 - x, 1 - y)]


_HBM = pl.BlockSpec(memory_space=pltpu.HBM)
_SEM = pl.BlockSpec(memory_space=pltpu.SEMAPHORE)
_EFFECT = pltpu.SideEffectType.DATAFLOW_SIDE_EFFECTING


def _hbm(a):
    return pltpu.with_memory_space_constraint(a, pltpu.HBM)


def _my_half(ref):
    half = ref.shape[0] // 2
    return ref.at[pl.ds(pl.multiple_of(lax.axis_index("c") * half, 8), half)]


def _exchange_copy(src, land, k, j, send, recv, scatter, halve, waiting):
    x, y, c = lax.axis_index("x"), lax.axis_index("y"), lax.axis_index("c")
    px, py = _chip_peers(x, y)[j]
    me, peer = 2 * x + y, 2 * px + py
    if scatter:
        src = src.at[me if waiting else peer]
    dst = land.at[peer if waiting else me]
    if halve:
        src, dst = _my_half(src), _my_half(dst)
    return pltpu.make_async_remote_copy(src_ref=src, dst_ref=dst, send_sem=send.at[3 * k + j],
                                        recv_sem=recv.at[3 * k + j], device_id=(px, py, c), device_id_type=MESH)


def exchange_start(items, name, scatter, after=None):
    arrs = []
    for a, _, _ in items:
        if not any(a is b for b in arrs):
            arrs.append(a)
    pos = [next(i for i, b in enumerate(arrs) if b is a) for a, _, _ in items]
    shapes = [a.shape if idx is None else a.shape[1:] for a, idx, _ in items]
    lands = [lax.empty(s if scatter else (N_CHIPS,) + s, a.dtype) for (a, _, _), s in zip(items, shapes)]
    na, nl = len(arrs), len(lands)
    n_in = na + nl + (after is not None)

    def body(*refs):
        ins, lnd = refs[:na], refs[na:na + nl]
        send, recv = refs[n_in], refs[n_in + 1]
        token = refs[-1]
        for k, (_, idx, halve) in enumerate(items):
            src = ins[pos[k]] if idx is None else ins[pos[k]].at[idx]
            for j in range(3):
                _exchange_copy(src, lnd[k], k, j, send, recv, scatter, halve, False).start()
        token[...] = jnp.zeros_like(token)

    sems = pltpu.SemaphoreType.DMA((3 * nl,))
    extra = [] if after is None else [after]
    out = pl.pallas_call(
        body, name=name,
        out_shape=(sems, sems, *[pltpu.HBM(a.shape, a.dtype) for a in arrs + lands], SDS((8, 128), f32)),
        in_specs=[_HBM] * (na + nl) + [pl.BlockSpec(memory_space=pl.ANY)] * len(extra),
        out_specs=(_SEM, _SEM, *[_HBM] * (na + nl), pl.BlockSpec(memory_space=pltpu.VMEM)),
        input_output_aliases={i: 2 + i for i in range(na + nl)},
        compiler_params=pltpu.CompilerParams(has_side_effects=_EFFECT),
    )(*[_hbm(a) for a in arrs + lands], *extra)
    thru = out[2:2 + na]
    return dict(send=out[0], recv=out[1], src=[(thru[pos[k]], idx) for k, (_, idx, _) in enumerate(items)],
                halve=[h for _, _, h in items], land=list(out[2 + na:2 + na + nl]), token=out[-1], scatter=scatter)


def exchange_wait(st, ks, after, name):
    arrs = []
    for k in ks:
        if not any(st["src"][k][0] is b for b in arrs):
            arrs.append(st["src"][k][0])
    pos = [next(i for i, b in enumerate(arrs) if b is st["src"][k][0]) for k in ks]
    lands = [st["land"][k] for k in ks]
    na, nl = len(arrs), len(lands)

    def body(*refs):
        ins, lnd = refs[:na], refs[na:na + nl]
        send, recv = refs[na + nl], refs[na + nl + 1]
        for t, k in enumerate(ks):
            idx = st["src"][k][1]
            src = ins[pos[t]] if idx is None else ins[pos[t]].at[idx]
            for j in range(3):
                cp = _exchange_copy(src, lnd[t], k, j, send, recv, st["scatter"], st["halve"][k], True)
                cp.wait_send()
                cp.wait_recv()

    out = pl.pallas_call(
        body, name=name, out_shape=tuple(pltpu.HBM(a.shape, a.dtype) for a in arrs + lands),
        in_specs=[_HBM] * (na + nl) + [_SEM, _SEM, pl.BlockSpec(memory_space=pl.ANY)], out_specs=tuple([_HBM] * (na + nl)),
        input_output_aliases={i: i for i in range(na + nl)},
        compiler_params=pltpu.CompilerParams(has_side_effects=_EFFECT),
    )(*arrs, *lands, st["send"], st["recv"], after)
    for k, (a, idx) in enumerate(st["src"]):
        for p, b in enumerate(arrs):
            if a is b:
                st["src"][k] = (out[p], idx)
    return list(out[na:na + nl])


def _sibling_copy(src, land, i, send, recv, other_half):
    x, y, c = lax.axis_index("x"), lax.axis_index("y"), lax.axis_index("c")
    return pltpu.make_async_remote_copy(src_ref=src.at[:, 1 - c] if other_half else src, dst_ref=land, send_sem=send.at[i],
                                        recv_sem=recv.at[i], device_id=(x, y, 1 - c), device_id_type=MESH)


def sibling_start(arrs, name, other_half=False):
    n = len(arrs)
    lands = [lax.empty((a.shape[0],) + a.shape[2:] if other_half else a.shape, a.dtype) for a in arrs]

    def body(*refs):
        ins, lnd = refs[:n], refs[n:2 * n]
        send, recv = refs[2 * n], refs[2 * n + 1]
        token = refs[-1]
        for i in range(n):
            _sibling_copy(ins[i], lnd[i], i, send, recv, other_half).start()
        token[...] = jnp.zeros_like(token)

    sems = pltpu.SemaphoreType.DMA((n,))
    out = pl.pallas_call(
        body, name=name,
        out_shape=(sems, sems, *[pltpu.HBM(a.shape, a.dtype) for a in arrs + lands], SDS((8, 128), f32)),
        in_specs=[_HBM] * (2 * n), out_specs=(_SEM, _SEM, *[_HBM] * (2 * n), pl.BlockSpec(memory_space=pltpu.VMEM)),
        input_output_aliases={i: 2 + i for i in range(2 * n)},
        compiler_params=pltpu.CompilerParams(has_side_effects=_EFFECT),
    )(*[_hbm(a) for a in arrs + lands])
    return dict(send=out[0], recv=out[1], src=list(out[2:2 + n]), land=list(out[2 + n:2 + 2 * n]), token=out[-1],
                other_half=other_half)


def sibling_wait(st, after, name):
    n = len(st["src"])

    def body(*refs):
        ins, lnd = refs[:n], refs[n:2 * n]
        send, recv = refs[2 * n], refs[2 * n + 1]
        for i in range(n):
            cp = _sibling_copy(ins[i], lnd[i], i, send, recv, st["other_half"])
            cp.wait_send()
            cp.wait_recv()

    out = pl.pallas_call(
        body, name=name, out_shape=tuple(pltpu.HBM(a.shape, a.dtype) for a in st["src"] + st["land"]),
        in_specs=[_HBM] * (2 * n) + [_SEM, _SEM, pl.BlockSpec(memory_space=pl.ANY)], out_specs=tuple([_HBM] * (2 * n)),
        input_output_aliases={i: i for i in range(2 * n)},
        compiler_params=pltpu.CompilerParams(has_side_effects=_EFFECT),
    )(*st["src"], *st["land"], st["send"], st["recv"], after)
    return list(out[:n]), list(out[n:])


def swap_halves(zones, name):
    n = len(zones)

    def body(*refs):
        outs = refs[n:2 * n]
        send, recv = refs[2 * n:]
        x, y, c = lax.axis_index("x"), lax.axis_index("y"), lax.axis_index("c")
        cps = []
        for i in range(n):
            for j, (px, py) in enumerate(_chip_peers(x, y)):
                part = _my_half(outs[i].at[2 * px + py])
                cps.append(pltpu.make_async_remote_copy(src_ref=part, dst_ref=part, send_sem=send.at[3 * i + j],
                                                        recv_sem=recv.at[3 * i + j], device_id=(x, y, 1 - c), device_id_type=MESH))
        for cp in cps:
            cp.start()
        for cp in cps:
            cp.wait_send()
            cp.wait_recv()

    any_spec = pl.BlockSpec(memory_space=pl.ANY)
    return pl.pallas_call(
        body, name=name, in_specs=[any_spec] * n, out_specs=[any_spec] * n, out_shape=[SDS(a.shape, a.dtype) for a in zones],
        input_output_aliases={i: i for i in range(n)},
        scratch_shapes=[pltpu.SemaphoreType.DMA((3 * n,)), pltpu.SemaphoreType.DMA((3 * n,))],
    )(*zones)


def _ids_spec(grid, in_specs, out_specs):
    return pltpu.PrefetchScalarGridSpec(num_scalar_prefetch=1, grid=grid, in_specs=in_specs, out_specs=out_specs)


def pair_sum(ids, a, b, name, tr=512):
    nd, _, rows, cols = a.shape
    tr = min(tr, rows)
    assert rows % tr == 0

    def body(ids_ref, a_ref, b_ref, o_ref):
        o_ref[...] = (a_ref[0].astype(f32) + b_ref[...].astype(f32)).astype(bf16)

    spec = pl.BlockSpec((1, tr, cols), lambda d, i, ids: (d, i, 0))
    return pl.pallas_call(
        body, name=name,
        grid_spec=_ids_spec((nd, rows // tr), [pl.BlockSpec((1, 1, tr, cols), lambda d, i, ids: (d, ids[1], i, 0)), spec], spec),
        out_shape=SDS((nd, rows, cols), bf16), compiler_params=_cp(("arbitrary", "arbitrary")))(ids, a, b)


def allreduce_small(v):
    def body(v_ref, o_ref, rbuf, send, recv):
        x, y, c = lax.axis_index("x"), lax.axis_index("y"), lax.axis_index("c")
        o_ref[...] = v_ref[...]
        for s, peer in enumerate([(x, y, 1 - c), (1 - x, y, c), (x, 1 - y, c)]):
            cp = pltpu.make_async_remote_copy(src_ref=o_ref, dst_ref=rbuf.at[s], send_sem=send.at[s], recv_sem=recv.at[s],
                                              device_id=peer, device_id_type=MESH)
            cp.start()
            cp.wait()
            o_ref[...] = o_ref[...] + rbuf[s]

    vm = pl.BlockSpec(memory_space=pltpu.VMEM)
    return pl.pallas_call(
        body, name="allreduce_small", in_specs=[vm], out_specs=vm, out_shape=SDS(v.shape, f32),
        scratch_shapes=[pltpu.VMEM((3,) + v.shape, f32), pltpu.SemaphoreType.DMA((3,)), pltpu.SemaphoreType.DMA((3,))],
        compiler_params=_cp(),
    )(v)


def sum_partials(ids, zone, mine, name, tr=256):
    _, rows, cols = zone.shape
    tr = min(tr, rows)
    assert rows % tr == 0

    def body(ids_ref, m_ref, z1_ref, z2_ref, z3_ref, o_ref):
        o_ref[...] = ((m_ref[0].astype(f32) + z1_ref[0].astype(f32)) + z2_ref[0].astype(f32)) + z3_ref[0].astype(f32)

    slot = lambda flip: pl.BlockSpec((1, tr, cols), lambda i, ids: (ids[0] ^ flip, i, 0))
    return pl.pallas_call(
        body, name=name,
        grid_spec=_ids_spec((rows // tr,), [slot(0), slot(1), slot(2), slot(3)], pl.BlockSpec((tr, cols), lambda i, ids: (i, 0))),
        out_shape=SDS((rows, cols), f32), compiler_params=_cp(("arbitrary",)),
    )(ids, mine, zone, zone, zone)


def adamw(w, m, v, gs, name, layer=0, prev=None, tr=256):
    hrows, cols = gs[0].shape
    rows = hrows * len(gs)
    tr = min(tr, hrows)
    assert hrows % tr == 0 and w.shape[0] % rows == 0
    off, nth = layer * (rows // tr), hrows // tr

    def body(w_ref, m_ref, v_ref, *rest):
        g_ref, d_ref, mo_ref, vo_ref = rest[-4:]
        if len(gs) == 1:
            g = rest[0][...]
        else:
            g = jnp.where(pl.program_id(0) // nth == lax.axis_index("c"), rest[0][...], rest[1][...])
        mn = ADAM_B1 * m_ref[...] + (1.0 - ADAM_B1) * g
        vn = ADAM_B2 * v_ref[...] + (1.0 - ADAM_B2) * jnp.square(g)
        m_hat = mn / (1.0 - ADAM_B1 ** ADAM_STEP)
        v_hat = vn / (1.0 - ADAM_B2 ** ADAM_STEP)
        g_ref[...] = g
        d_ref[...] = -ADAM_LR * (m_hat / (jnp.sqrt(v_hat) + ADAM_EPS) + ADAM_WD * w_ref[...])
        mo_ref[...] = mn
        vo_ref[...] = vn

    loc = pl.BlockSpec((tr, cols), lambda i: (i % nth, 0))
    glob = pl.BlockSpec((tr, cols), lambda i: (off + i, 0))
    extra = [] if prev is None else list(prev)
    return pl.pallas_call(
        body, name=name, grid=(rows // tr,),
        in_specs=[glob] * 3 + [loc] * len(gs) + [pl.BlockSpec(memory_space=pl.ANY)] * len(extra),
        out_specs=[glob] * 4, out_shape=[SDS(w.shape, f32)] * 4,
        input_output_aliases={3 + len(gs) + j: j for j in range(len(extra))},
        compiler_params=_cp(("arbitrary",)),
    )(w, m, v, *gs, *extra)


BIG = ("w_in", "w_out", "w_ff1", "w_ff2")
SMALL = ("norm1_g", "conv_w", "a_log", "dt_bias", "dn_out_g", "sb_q_g", "sb_k_g", "sg_v_g", "sg_w", "sg_b", "norm2_g")
WEIGHTS = ("norm1_g", "w_in", "conv_w", "a_log", "dt_bias", "dn_out_g", "sb_q_g", "sb_k_g", "sg_v_g", "sg_w", "sg_b",
           "w_out", "norm2_g", "w_ff1", "w_ff2")


PACK_ROWS = 256


def _rows_of(shape):
    n = 1
    for d in shape:
        n *= d
    return -(-n // 1024) * 8, n


def _pack(arrs):
    parts = []
    for a in arrs:
        r, n = _rows_of(a.shape)
        parts.append(jnp.pad(a.reshape(-1), (0, r * 128 - n)).reshape(r, 128))
    rows = sum(p.shape[0] for p in parts)
    parts.append(jnp.zeros((-rows % PACK_ROWS, 128), arrs[0].dtype))
    return jnp.concatenate(parts, axis=0)


def _unpack(packed, shapes):
    out, o = [], 0
    for s in shapes:
        r, n = _rows_of(s)
        out.append(packed[o:o + r].reshape(-1)[0:n].reshape(s))
        o += r
    return out


def kernel(x, norm1_g, w_in, conv_w, a_log, dt_bias, dn_out_g, sb_q_g, sb_k_g, sg_v_g, sg_w, sg_b, w_out, norm2_g, w_ff1, w_ff2, loss_target, m_norm1_g, m_w_in, m_conv_w, m_a_log, m_dt_bias, m_dn_out_g, m_sb_q_g, m_sb_k_g, m_sg_v_g, m_sg_w, m_sg_b, m_w_out, m_norm2_g, m_w_ff1, m_w_ff2, v_norm1_g, v_w_in, v_conv_w, v_a_log, v_dt_bias, v_dn_out_g, v_sb_q_g, v_sb_k_g, v_sg_v_g, v_sg_w, v_sg_b, v_w_out, v_norm2_g, v_w_ff1, v_w_ff2):
    w = dict(norm1_g=norm1_g, w_in=w_in, conv_w=conv_w, a_log=a_log, dt_bias=dt_bias, dn_out_g=dn_out_g, sb_q_g=sb_q_g,
             sb_k_g=sb_k_g, sg_v_g=sg_v_g, sg_w=sg_w, sg_b=sg_b, w_out=w_out, norm2_g=norm2_g, w_ff1=w_ff1, w_ff2=w_ff2)
    mom = dict(norm1_g=m_norm1_g, w_in=m_w_in, conv_w=m_conv_w, a_log=m_a_log, dt_bias=m_dt_bias, dn_out_g=m_dn_out_g,
               sb_q_g=m_sb_q_g, sb_k_g=m_sb_k_g, sg_v_g=m_sg_v_g, sg_w=m_sg_w, sg_b=m_sg_b, w_out=m_w_out, norm2_g=m_norm2_g,
               w_ff1=m_w_ff1, w_ff2=m_w_ff2)
    var = dict(norm1_g=v_norm1_g, w_in=v_w_in, conv_w=v_conv_w, a_log=v_a_log, dt_bias=v_dt_bias, dn_out_g=v_dn_out_g,
               sb_q_g=v_sb_q_g, sb_k_g=v_sb_k_g, sg_v_g=v_sg_v_g, sg_w=v_sg_w, sg_b=v_sg_b, w_out=v_w_out, norm2_g=v_norm2_g,
               w_ff1=v_w_ff1, w_ff2=v_w_ff2)
    chip = 2 * lax.axis_index("x") + lax.axis_index("y")

    wb = [{k: w[k][l].astype(bf16) for k in BIG} for l in range(DEPTH)]
    ags = {0: exchange_start([(conv_w, None, False)] + [(wb[0][k], None, True) for k in BIG], "allgather_start_0", scatter=False)}
    item = lambda l, k: (l, (l == 0) + BIG.index(k))

    def landed(items, after, name):
        ag, ks = ags[items[0][0]], [k for _, k in items]
        zones = exchange_wait(ag, ks, after, name)
        halved = [t for t, k in enumerate(ks) if ag["halve"][k]]
        for t, z in zip(halved, swap_halves([zones[t] for t in halved], name.replace("wait", "pass"))):
            zones[t] = z
        return [lax.dynamic_update_slice_in_dim(z, ag["src"][k][0][None], chip, axis=0) for z, k in zip(zones, ks)]

    def whole(k, z):
        if k == "w_in":
            return w_in_from_shards(z)
        return z if k == "w_ff1" else z.reshape(-1, D_MODEL)

    g_conv, first_in = landed([(0, 0), item(0, "w_in")], x, "allgather_wait_in0")
    small = {k: w[k] for k in SMALL}
    small["conv_w"] = jnp.transpose(g_conv, (1, 2, 0, 3)).reshape(DEPTH, DN_CONV, 3 * DN_WIDTH)
    cache = {}

    def get_w(l, part, after):
        if part == "in":
            return whole("w_in", first_in if l == 0 else landed([item(l, "w_in")], after, f"allgather_wait_in{l}")[0])
        if part == "out":
            zs = landed([item(l, k) for k in ("w_out", "w_ff1", "w_ff2")], after, f"allgather_wait_rest{l}")
            token = jnp.zeros((), f32)
            if l + 1 < DEPTH:
                ags[l + 1] = exchange_start([(wb[l + 1][k], None, True) for k in BIG], f"allgather_start_{l + 1}",
                                            scatter=False, after=zs[0])
                token = ags[l + 1]["token"][0, 0]
            cache[l] = (whole("w_ff1", zs[1]), whole("w_ff2", zs[2]), token)
            return whole("w_out", zs[0])
        return cache[l]

    rs, pending = {}, []
    ids = jnp.stack([chip, lax.axis_index("c")]).astype(jnp.int32)

    def put_g(l, tag, g):
        names = [k for k in BIG if k in g]
        by_dest = [w_in_grad_to_shards(g[k]) if k == "w_in" else g[k] for k in names]
        halves = [a.reshape(N_CHIPS, 2, -1, a.shape[-1]) for a in by_dest]
        st = sibling_start(halves, f"pair_swap_start_{tag}{l}", other_half=True)
        pending.append((l, tag, names, st))
        return st["token"][0, 0]

    def sync_g(after):
        token = jnp.zeros((), f32)
        while pending:
            l, tag, names, st = pending.pop(0)
            halves, got = sibling_wait(st, after, f"pair_swap_wait_{tag}{l}")
            pair = [pair_sum(ids, a, b, f"pair_sum_{k}_{l}") for k, a, b in zip(names, halves, got)]
            rs[l, tag] = dict(exchange_start([(a, None, False) for a in pair], f"scatter_start_{tag}{l}", scatter=True), names=names)
            token = token + rs[l, tag]["token"][0, 0]
        return token

    lossp, grad_x, gsmall = local_step(x, loss_target, small, get_w, put_g, sync_g)

    def sum_group(l, tag, after):
        st = rs[l, tag]
        zones = exchange_wait(st, list(range(len(st["names"]))), after, f"scatter_wait_{tag}{l}")
        sums = [sum_partials(ids, zones[i], st["src"][i][0], f"sum_{k}_{l}") for i, k in enumerate(st["names"])]
        return sibling_start(sums, f"swap_sums_start_{tag}{l}")

    def update_group(l, tag, swap, after, prev):
        sums, others = sibling_wait(swap, after, f"swap_sums_wait_{tag}{l}")
        outs = dict(prev)
        for i, k in enumerate(rs[l, tag]["names"]):
            r2 = lambda a: a.reshape(-1, a.shape[-1])
            outs[k] = adamw(r2(w[k]), r2(mom[k]), r2(var[k]), (sums[i], others[i]), f"adamw_{k}_{l}", layer=l, prev=prev.get(k))
        return outs

    swap_r = sum_group(1, "rest", rs[0, "in"]["token"])
    swap_i = sum_group(1, "in", swap_r["token"])
    done = update_group(1, "rest", swap_r, swap_i["token"], {})
    done = update_group(1, "in", swap_i, done["w_ff2"][0], done)
    res = {}

    full_shapes = [(DEPTH,) + tuple(gsmall[0][k].shape) for k in SMALL]
    packed = _pack([jnp.stack([gsmall[l][k] for l in range(DEPTH)]) for k in SMALL] + [jnp.sum(lossp).reshape(1)])
    *totals, loss = _unpack(allreduce_small(packed), full_shapes + [(1,)])
    loss = loss[0]
    gfull = dict(zip(SMALL, totals))
    cs = 3 * DN_WIDTH // N_CHIPS
    gfull["conv_w"] = lax.dynamic_slice_in_dim(gfull["conv_w"], chip * cs, cs, axis=2)
    gp, wp, mp, vp = (_pack([d[k] for k in SMALL]) for d in (gfull, w, mom, var))
    outs = adamw(wp, mp, vp, (gp,), "adamw_small")
    loc_shapes = [w[k].shape for k in SMALL]
    unp = [_unpack(o, loc_shapes) for o in outs]
    for i, k in enumerate(SMALL):
        res[k] = [unp[j][i] for j in range(4)]

    swap_r = sum_group(0, "rest", outs[0])
    swap_i = sum_group(0, "in", swap_r["token"])
    done = update_group(0, "rest", swap_r, swap_i["token"], done)
    done = update_group(0, "in", swap_i, done["w_ff2"][0], done)
    for k in BIG:
        res[k] = [o.reshape(w[k].shape) for o in done[k]]

    return (loss, grad_x, *[res[k][0] for k in WEIGHTS], *[res[k][1] for k in WEIGHTS], *[res[k][2] for k in WEIGHTS],
            *[res[k][3] for k in WEIGHTS])
```

```python
import functools

import jax
import jax.numpy as jnp
from jax import lax
from jax.experimental import pallas as pl
from jax.experimental.pallas import tpu as pltpu

f32 = jnp.float32
bf16 = jnp.bfloat16
SDS = jax.ShapeDtypeStruct
MESH = pl.DeviceIdType.MESH

NORM_EPS = 1e-6
D_MODEL = 1024
DEPTH = 2
DN_HEADS, DN_DIM, DN_WIDTH, DN_CONV, DN_CHUNK = 4, 128, 512, 4, 64
SB_HEADS, SB_DIM, SB_WIDTH = 4, 64, 256
SG_GROUPS, SG_DIM, SG_WIDTH, SG_CHUNK = 4, 64, 256, 128
D_FF = 4096
IN_DIM = 3336
C_QKV, C_Z, C_AB, C_SB, C_SG, IN_PAD = 0, 1536, 2048, 2304, 3072, 3584
DN_COLS = C_SB
N_CHIPS = 4

ADAM_LR, ADAM_B1, ADAM_B2, ADAM_EPS, ADAM_WD, ADAM_STEP = 0.001, 0.9, 0.999, 1e-08, 0.01, 10

VMEM_LIMIT = 56 * 1024 * 1024


def _cp(sem=None, **kw):
    if sem is not None:
        kw["dimension_semantics"] = sem
    return pltpu.CompilerParams(vmem_limit_bytes=VMEM_LIMIT, **kw)


def _split2(x):
    hi = x.astype(bf16)
    lo = (x - hi.astype(f32)).astype(bf16)
    return hi, lo


NT = (((1,), (1,)), ((), ()))
TN = (((0,), (0,)), ((), ()))
_DIMS2 = dict(nn=(((1,), (0,)), ((), ())), nt=NT, tn=TN)
_DIMS3 = dict(nn=(((2,), (1,)), ((0,), (0,))), nt=(((2,), (2,)), ((0,), (0,))), tn=(((1,), (1,)), ((0,), (0,))))


def _dg(a, b, kind):
    return lax.dot_general(a, b, (_DIMS2 if a.ndim == 2 else _DIMS3)[kind], preferred_element_type=f32)


def _pdot(a, b):
    return _dg(a, b, "nn")


def _dot_hp(a, b):
    ah, al = _split2(a)
    bh, bl = _split2(b)
    return _pdot(ah, bh) + _pdot(ah, bl) + _pdot(al, bh)


def _dot_x2c(a, m):
    lead = a.shape[:-1]
    ah, al = _split2(a.reshape(-1, a.shape[-1]))
    return (_pdot(ah, m) + _pdot(al, m)).reshape(lead + (m.shape[1],))


def _dot_cx2(m, a):
    if a.ndim == 3:
        m = jnp.broadcast_to(m, (a.shape[0],) + m.shape)
    ah, al = _split2(a)
    return _pdot(m, ah) + _pdot(m, al)


def _nt(a, b):
    return _dg(a.astype(bf16), b.astype(bf16), "nt")


def _tn(a, b):
    return _dg(a.astype(bf16), b.astype(bf16), "tn")


def _nn(a, b):
    return _dg(a.astype(bf16), b.astype(bf16), "nn")


@jax.custom_vjp
def mm(a, b):
    return _nn(a, b)


mm.defvjp(lambda a, b: (_nn(a, b), (a, b)), lambda r, g: (_nt(g, r[1]), _tn(r[0], g)))


@jax.custom_vjp
def mm_nt(a, b):
    return _nt(a, b)


mm_nt.defvjp(lambda a, b: (_nt(a, b), (a, b)), lambda r, g: (_nn(g, r[1]), _tn(g, r[0])))


@jax.custom_vjp
def mm_tn(a, b):
    return _tn(a, b)


mm_tn.defvjp(lambda a, b: (_tn(a, b), (a, b)), lambda r, g: (_nt(r[1], g), _nn(r[0], g)))


@jax.custom_vjp
def rmul_const(a, m, mt):
    return _dot_x2c(a, m)


rmul_const.defvjp(lambda a, m, mt: (_dot_x2c(a, m), (m, mt)),
                  lambda r, g: (_dot_x2c(g, r[1]), jnp.zeros_like(r[0]), jnp.zeros_like(r[1])))


@jax.custom_vjp
def lmul_const(m, mt, a):
    return _dot_cx2(m, a)


lmul_const.defvjp(lambda m, mt, a: (_dot_cx2(m, a), (m, mt)),
                  lambda r, g: (jnp.zeros_like(r[0]), jnp.zeros_like(r[1]), _dot_cx2(r[1], g)))


@jax.custom_vjp
def mm_hl(t, x):
    th, tl = _split2(t)
    xb = x.astype(bf16)
    return _pdot(th, xb) + _pdot(tl, xb)


def _mm_hl_bwd(r, g):
    t, x = r
    th, tl = _split2(t)
    gb = g.astype(bf16)
    return _nt(g, x), _dg(th, gb, "tn") + _dg(tl, gb, "tn")


mm_hl.defvjp(lambda t, x: (mm_hl(t, x), (t, x)), _mm_hl_bwd)


def inv_unit_lower(lm):
    c = lm.shape[-1]
    r, cc = _iota2((c, c))
    eye = (r == cc).astype(f32)
    t = eye - lm
    p = -lm
    k = 1
    while 2 * k < c:
        p = _nn(p, p)
        t = t + _nn(t, p)
        k *= 2
    res = eye - t - _dot_hp(lm, t)
    return t + _nn(t, res)


@jax.custom_vjp
def inv_given(lm, t):
    return t


inv_given.defvjp(lambda lm, t: (t, t), lambda t, g: (-_nt(_tn(t, g), t), jnp.zeros_like(t)))


def _sigmoid(x):
    return 1.0 / (1.0 + jnp.exp(-x))


def _softplus(x):
    return jnp.maximum(x, 0.0) + jnp.log(1.0 + jnp.exp(-jnp.abs(x)))


def _silu(x):
    return x * _sigmoid(x)


def _gelu(x):
    return 0.5 * x * (1.0 + jnp.tanh(0.7978845608028654 * (x + 0.044715 * (x * x * x))))


def _iota2(shape):
    return lax.broadcasted_iota(jnp.int32, shape, 0), lax.broadcasted_iota(jnp.int32, shape, 1)


def _group_avg_mats():
    r, c = _iota2((128, 128))
    return jnp.where((r // 64) == (c // 64), 1.0 / 64.0, 0.0).astype(bf16)


def _pair_norm(x, gain, bavg):
    ms = rmul_const(x * x, bavg, bavg)
    return x * lax.rsqrt(ms + NORM_EPS) * gain


def _rms(x):
    r = lax.rsqrt(jnp.mean(x * x, axis=-1, keepdims=True) + NORM_EPS)
    return r


_IN_GROUPS = ((C_QKV, C_Z), (C_Z, C_AB), (C_AB, C_AB + 128), (C_SB, C_SG), (C_SG, IN_PAD))


def inproj_fwd(x, g, wp, tm=256):
    m = x.shape[0]

    def body(x_ref, g_ref, w_ref, *outs):
        xv = x_ref[...]
        h = (xv * _rms(xv) * g_ref[...]).astype(bf16)
        outs[-1][...] = h
        for (a, b), o in zip(_IN_GROUPS, outs):
            o[...] = _pdot(h, w_ref[:, a:b])

    widths = [b - a for a, b in _IN_GROUPS]
    return pl.pallas_call(
        body, name="inproj_fwd", grid=(m // tm,),
        in_specs=[pl.BlockSpec((tm, D_MODEL), lambda i: (i, 0)), pl.BlockSpec((1, D_MODEL), lambda i: (0, 0)),
                  pl.BlockSpec((D_MODEL, IN_PAD), lambda i: (0, 0))],
        out_specs=[pl.BlockSpec((tm, wd), lambda i: (i, 0)) for wd in widths + [D_MODEL]],
        out_shape=[SDS((m, wd), f32) for wd in widths] + [SDS((m, D_MODEL), bf16)],
        compiler_params=_cp(("arbitrary",)),
    )(x, g, wp)


def inproj_bwd(x, g, wp, dproj, dres, tm=256):
    m = x.shape[0]

    def body(x_ref, g_ref, w_ref, dp_ref, dr_ref, dx_ref, dg_ref):
        xv = x_ref[...]
        r = _rms(xv)
        xn = xv * r
        gv = g_ref[...]
        dh = lax.dot_general(dp_ref[...], w_ref[...], NT, preferred_element_type=f32)
        dxn = dh * gv
        dx_ref[...] = dr_ref[...] + r * (dxn - xn * jnp.mean(dxn * xn, axis=-1, keepdims=True))

        @pl.when(pl.program_id(0) == 0)
        def _():
            dg_ref[...] = jnp.zeros_like(dg_ref)

        dg_ref[...] += jnp.sum(dh * xn, axis=0, keepdims=True)

    return pl.pallas_call(
        body, name="inproj_bwd", grid=(m // tm,),
        in_specs=[pl.BlockSpec((tm, D_MODEL), lambda i: (i, 0)), pl.BlockSpec((1, D_MODEL), lambda i: (0, 0)),
                  pl.BlockSpec((D_MODEL, IN_PAD), lambda i: (0, 0)), pl.BlockSpec((tm, IN_PAD), lambda i: (i, 0)),
                  pl.BlockSpec((tm, D_MODEL), lambda i: (i, 0))],
        out_specs=[pl.BlockSpec((tm, D_MODEL), lambda i: (i, 0)), pl.BlockSpec((1, D_MODEL), lambda i: (0, 0))],
        out_shape=[SDS((m, D_MODEL), f32), SDS((1, D_MODEL), f32)],
        compiler_params=_cp(("arbitrary",)),
    )(x, g, wp, dproj, dres)


def outproj_fwd(x, odn, osb, osg, wo, tm=512):
    m = x.shape[0]

    def body(x_ref, a_ref, b_ref, c_ref, w_ref, x2_ref, mix_ref):
        mix_ref[:, 0:DN_WIDTH] = a_ref[...].astype(bf16)
        mix_ref[:, DN_WIDTH:DN_WIDTH + SB_WIDTH] = b_ref[...].astype(bf16)
        mix_ref[:, DN_WIDTH + SB_WIDTH:D_MODEL] = c_ref[...].astype(bf16)
        x2_ref[...] = x_ref[...] + _pdot(mix_ref[...], w_ref[...])

    row = lambda w: pl.BlockSpec((tm, w), lambda i: (i, 0))
    return pl.pallas_call(
        body, name="outproj_fwd", grid=(m // tm,),
        in_specs=[row(D_MODEL), row(DN_WIDTH), row(SB_WIDTH), row(SG_WIDTH), pl.BlockSpec((D_MODEL, D_MODEL), lambda i: (0, 0))],
        out_specs=[row(D_MODEL), row(D_MODEL)],
        out_shape=[SDS((m, D_MODEL), f32), SDS((m, D_MODEL), bf16)],
        compiler_params=_cp(("arbitrary",)),
    )(x, odn, osb, osg, wo)


def outproj_bwd(dx2, wo, tm=512):
    m = dx2.shape[0]

    def body(d_ref, w_ref, a_ref, b_ref, c_ref, db_ref):
        db = d_ref[...].astype(bf16)
        db_ref[...] = db
        dm = lax.dot_general(db, w_ref[...], NT, preferred_element_type=f32)
        a_ref[...] = dm[:, 0:DN_WIDTH]
        b_ref[...] = dm[:, DN_WIDTH:DN_WIDTH + SB_WIDTH]
        c_ref[...] = dm[:, DN_WIDTH + SB_WIDTH:D_MODEL]

    row = lambda w: pl.BlockSpec((tm, w), lambda i: (i, 0))
    return pl.pallas_call(
        body, name="outproj_bwd", grid=(m // tm,),
        in_specs=[row(D_MODEL), pl.BlockSpec((D_MODEL, D_MODEL), lambda i: (0, 0))],
        out_specs=[row(DN_WIDTH), row(SB_WIDTH), row(SG_WIDTH), row(D_MODEL)],
        out_shape=[SDS((m, DN_WIDTH), f32), SDS((m, SB_WIDTH), f32), SDS((m, SG_WIDTH), f32), SDS((m, D_MODEL), bf16)],
        compiler_params=_cp(("arbitrary",)),
    )(dx2, wo)


FF_CHUNK = D_FF // N_CHIPS


def _load_weights_once(pairs, sem):
    @pl.when(pl.program_id(0) == 0)
    def _():
        cps = [pltpu.make_async_copy(h, v, sem.at[i]) for i, (h, v) in enumerate(pairs)]
        for c in cps:
            c.start()
        for c in cps:
            c.wait()


def ffn_fwd(x2, g, w1, w2, tgt=None, tm=256):
    m = x2.shape[0]
    head = tgt is not None

    def body(x_ref, g_ref, w1_hbm, w2_hbm, *rest):
        (y_ref, rl_ref), (w1_v, w2_v, sem) = rest[head:head + 2], rest[-3:]
        _load_weights_once(((w1_hbm, w1_v), (w2_hbm, w2_v)), sem)
        xv = x_ref[...]
        h = (xv * _rms(xv) * g_ref[...]).astype(bf16)
        acc = xv
        for j in range(0, D_FF, FF_CHUNK):
            f = _pdot(h, w1_v[j // FF_CHUNK])
            rl = jnp.maximum(f, 0.0)
            rl_ref[:, j:j + FF_CHUNK] = rl.astype(bf16)
            acc = acc + _pdot((rl * rl).astype(bf16), w2_v[j:j + FF_CHUNK, :])
        if not head:
            y_ref[...] = acc
            return
        t_ref, l_ref = rest[0], rest[3]
        e = acc - t_ref[...]
        y_ref[...] = e * (1.0 / D_MODEL)

        @pl.when(pl.program_id(0) == 0)
        def _():
            l_ref[...] = jnp.zeros_like(l_ref)

        l_ref[...] += jnp.sum(e * e, axis=0, keepdims=True) * (0.5 / D_MODEL)

    row = pl.BlockSpec((tm, D_MODEL), lambda i: (i, 0))
    return pl.pallas_call(
        body, name="ffn_fwd_loss" if head else "ffn_fwd", grid=(m // tm,),
        in_specs=[row, pl.BlockSpec((1, D_MODEL), lambda i: (0, 0)), pl.BlockSpec(memory_space=pl.ANY),
                  pl.BlockSpec(memory_space=pl.ANY)] + [row] * head,
        out_specs=[row, pl.BlockSpec((tm, D_FF), lambda i: (i, 0))] + [pl.BlockSpec((1, D_MODEL), lambda i: (0, 0))] * head,
        out_shape=[SDS((m, D_MODEL), f32), SDS((m, D_FF), bf16)] + [SDS((1, D_MODEL), f32)] * head,
        scratch_shapes=[pltpu.VMEM((N_CHIPS, D_MODEL, FF_CHUNK), bf16), pltpu.VMEM((D_FF, D_MODEL), bf16), pltpu.SemaphoreType.DMA((2,))],
        compiler_params=_cp(("arbitrary",)),
    )(x2, g, w1, w2, *([tgt] if head else []))


def ffn_bwd(x2, g, w1, w2, rlb, dy, tm=256):
    m = x2.shape[0]

    def body(x_ref, g_ref, w1_hbm, w2_hbm, rl_ref, dy_ref, dx_ref, dg_ref, h_ref, a_ref, df_ref, dyb_ref, w1_v, w2_v, sem):
        _load_weights_once(((w1_hbm, w1_v), (w2_hbm, w2_v)), sem)
        xv = x_ref[...]
        r = _rms(xv)
        xn = xv * r
        gv = g_ref[...]
        h = (xn * gv).astype(bf16)
        h_ref[...] = h
        dyv = dy_ref[...]
        dyb = dyv.astype(bf16)
        dyb_ref[...] = dyb
        dh = jnp.zeros((tm, D_MODEL), f32)
        for j in range(0, D_FF, FF_CHUNK):
            rl = rl_ref[:, j:j + FF_CHUNK].astype(f32)
            a_ref[:, j:j + FF_CHUNK] = (rl * rl).astype(bf16)
            da = lax.dot_general(dyb, w2_v[j:j + FF_CHUNK, :], NT, preferred_element_type=f32)
            df = (da * (2.0 * rl)).astype(bf16)
            df_ref[:, j:j + FF_CHUNK] = df
            dh = dh + lax.dot_general(df, w1_v[j // FF_CHUNK], NT, preferred_element_type=f32)
        dxn = dh * gv
        dx_ref[...] = dyv + r * (dxn - xn * jnp.mean(dxn * xn, axis=-1, keepdims=True))

        @pl.when(pl.program_id(0) == 0)
        def _():
            dg_ref[...] = jnp.zeros_like(dg_ref)

        dg_ref[...] += jnp.sum(dh * xn, axis=0, keepdims=True)

    row = lambda w: pl.BlockSpec((tm, w), lambda i: (i, 0))
    return pl.pallas_call(
        body, name="ffn_bwd", grid=(m // tm,),
        in_specs=[row(D_MODEL), pl.BlockSpec((1, D_MODEL), lambda i: (0, 0)),
                  pl.BlockSpec(memory_space=pl.ANY), pl.BlockSpec(memory_space=pl.ANY), row(D_FF), row(D_MODEL)],
        out_specs=[row(D_MODEL), pl.BlockSpec((1, D_MODEL), lambda i: (0, 0)), row(D_MODEL), row(D_FF), row(D_FF), row(D_MODEL)],
        out_shape=[SDS((m, D_MODEL), f32), SDS((1, D_MODEL), f32), SDS((m, D_MODEL), bf16), SDS((m, D_FF), bf16),
                   SDS((m, D_FF), bf16), SDS((m, D_MODEL), bf16)],
        scratch_shapes=[pltpu.VMEM((N_CHIPS, D_MODEL, FF_CHUNK), bf16), pltpu.VMEM((D_FF, D_MODEL), bf16), pltpu.SemaphoreType.DMA((2,))],
        compiler_params=_cp(("arbitrary",)),
    )(x2, g, w1, w2, rlb, dy)


def _tile(n, cap):
    best = 128
    for t in range(128, cap + 1, 128):
        if n % t == 0:
            best = t
    return best


def tn_matmul(a, b, name, col_shards=1, tk=2048):
    m, ka = a.shape
    n = b.shape[1]
    ti = _tile(ka, 1024)
    tj = _tile(n // col_shards, 1152)
    tk = min(tk, m)
    nk = m // tk
    jps = (n // col_shards) // tj

    def body(a_ref, b_ref, o_ref, acc):
        k = pl.program_id(2)

        @pl.when(k == 0)
        def _():
            acc[...] = jnp.zeros_like(acc)

        acc[...] += lax.dot_general(a_ref[...], b_ref[...], TN, preferred_element_type=f32)

        @pl.when(k == nk - 1)
        def _():
            o_ref[...] = acc[...].astype(bf16).reshape(o_ref.shape)

    if col_shards == 1:
        out_shape, out_spec = SDS((ka, n), bf16), pl.BlockSpec((ti, tj), lambda i, j, k: (i, j))
    else:
        out_shape = SDS((col_shards, ka, n // col_shards), bf16)
        out_spec = pl.BlockSpec((1, ti, tj), lambda i, j, k: (j // jps, i, j % jps))
    return pl.pallas_call(
        body, name=name, grid=(ka // ti, n // tj, nk),
        in_specs=[pl.BlockSpec((tk, ti), lambda i, j, k: (k, i)), pl.BlockSpec((tk, tj), lambda i, j, k: (k, j))],
        out_specs=out_spec, out_shape=out_shape,
        scratch_shapes=[pltpu.VMEM((ti, tj), f32)],
        compiler_params=_cp(("arbitrary", "arbitrary", "arbitrary")),
    )(a, b)


def _dn_consts():
    c = DN_CHUNK
    r, cc = _iota2((c, c))
    lt = (cc <= r).astype(bf16)
    ltt = (r <= cc).astype(bf16)
    return lt, ltt


def dn_chunk(cq, ck, cv, g, beta, z, s, gain, lt, ltt, t_given=None):
    c = DN_CHUNK
    r, cc = _iota2((c, c))
    q = cq * lax.rsqrt(jnp.sum(cq * cq, axis=-1, keepdims=True) + NORM_EPS) * (DN_DIM ** -0.5)
    k = ck * lax.rsqrt(jnp.sum(ck * ck, axis=-1, keepdims=True) + NORM_EPS)
    r2, c2 = _iota2((c, 128))
    uaug = jnp.where((c2 < c) & (r2 > c2), 1.0, 0.0) + jnp.where(c2 == c, 1.0, 0.0)
    gam_all = lmul_const(lt, ltt, g * uaug)
    gam_cc = gam_all[:, :, 0:c]
    gam = gam_all[:, :, c:c + 1]
    dec = jnp.where(cc <= r, jnp.exp(jnp.where(cc <= r, gam_cc, 0.0)), 0.0)
    kk = mm_nt(k, k)
    lm = jnp.where(cc < r, beta * kk * dec, 0.0)
    t = inv_unit_lower(lm) if t_given is None else inv_given(lm, t_given)
    eg = jnp.exp(gam)
    sol = mm_hl(t, jnp.concatenate([cv * beta, k * (beta * eg)], axis=2))
    u, w = sol[:, :, 0:DN_DIM], sol[:, :, DN_DIM:2 * DN_DIM]
    qk = jnp.where(cc <= r, mm_nt(q, k) * dec, 0.0)
    glast = jnp.sum(g, axis=1, keepdims=True)
    qd = q * eg
    kd = k * jnp.exp(glast - gam)
    un = u - mm(w, s)
    o = mm(qd, s) + mm(qk, un)
    s_new = s * jnp.exp(glast) + mm_tn(kd, un)
    on = o * lax.rsqrt(jnp.mean(o * o, axis=-1, keepdims=True) + NORM_EPS) * gain * _silu(z)
    return on, s_new, t


def _dn_gates(ab, al_row, dt_row):
    pre = ab + dt_row
    return -jnp.exp(al_row) * _softplus(pre), _sigmoid(ab), _sigmoid(pre)


def _dn_chains(cacts, gates, z_ref):
    cq, ck, cv, g, beta, z = [], [], [], [], [], []
    for bi, cact in enumerate(cacts):
        for h in range(DN_HEADS):
            cq.append(cact[:, h * DN_DIM:(h + 1) * DN_DIM])
            ck.append(cact[:, DN_WIDTH + h * DN_DIM:DN_WIDTH + (h + 1) * DN_DIM])
            cv.append(cact[:, 2 * DN_WIDTH + h * DN_DIM:2 * DN_WIDTH + (h + 1) * DN_DIM])
            g.append(gates[bi][0][:, h:h + 1])
            beta.append(gates[bi][1][:, DN_HEADS + h:DN_HEADS + h + 1])
            z.append(z_ref[bi, :, h * DN_DIM:(h + 1) * DN_DIM])
    return tuple(jnp.stack(v) for v in (cq, ck, cv, g, beta, z))


def _conv_rows(xe_ref, b, w_ref):
    y = w_ref[0:1, :] * xe_ref[b, pl.ds(5, DN_CHUNK), :]
    for i in range(1, DN_CONV):
        y = y + w_ref[i:i + 1, :] * xe_ref[b, pl.ds(5 + i, DN_CHUNK), :]
    return y


def dn_fwd(qkv, z, ab, conv_w, alog, dtb, gain):
    bsz, t, _ = qkv.shape
    nc = t // DN_CHUNK
    c = DN_CHUNK
    nh = bsz * DN_HEADS

    def body(qkv_ref, z_ref, ab_ref, w_ref, al_ref, dt_ref, g_ref, o_ref, sall_ref, tall_ref, xe, s_sc):
        n = pl.program_id(0)

        @pl.when(n == 0)
        def _():
            xe[:, 0:8, :] = jnp.zeros((bsz, 8, 3 * DN_WIDTH), f32)
            s_sc[...] = jnp.zeros_like(s_sc)

        lt, ltt = _dn_consts()
        cacts = []
        for b in range(bsz):
            xe[b, 8:8 + c, :] = qkv_ref[b]
            cacts.append(_silu(_conv_rows(xe, b, w_ref)))
            xe[b, 0:8, :] = xe[b, c:c + 8, :]
        gates = [_dn_gates(ab_ref[b], al_ref[...], dt_ref[...]) for b in range(bsz)]
        s = s_sc[...]
        sall_ref[0] = s
        on, sn, tt = dn_chunk(*_dn_chains(cacts, gates, z_ref), s, g_ref[...], lt, ltt)
        tall_ref[0] = tt
        s_sc[...] = sn
        for b in range(bsz):
            for h in range(DN_HEADS):
                o_ref[b, :, h * DN_DIM:(h + 1) * DN_DIM] = on[b * DN_HEADS + h]

    blk = lambda w: pl.BlockSpec((bsz, c, w), lambda n: (0, n, 0))
    full = lambda shp: pl.BlockSpec(shp, lambda n: (0,) * len(shp))
    return pl.pallas_call(
        body, name="dn_fwd", grid=(nc,),
        in_specs=[blk(3 * DN_WIDTH), blk(DN_WIDTH), blk(128), full((8, 3 * DN_WIDTH)), full((1, 128)), full((1, 128)), full((1, 128))],
        out_specs=[blk(DN_WIDTH), pl.BlockSpec((1, nh, DN_DIM, DN_DIM), lambda n: (n, 0, 0, 0)),
                   pl.BlockSpec((1, nh, c, c), lambda n: (n, 0, 0, 0))],
        out_shape=[SDS((bsz, t, DN_WIDTH), f32), SDS((nc, nh, DN_DIM, DN_DIM), f32), SDS((nc, nh, c, c), f32)],
        scratch_shapes=[pltpu.VMEM((bsz, c + 8, 3 * DN_WIDTH), f32), pltpu.VMEM((nh, DN_DIM, DN_DIM), f32)],
        compiler_params=_cp(("arbitrary",)),
    )(qkv, z, ab, conv_w, alog, dtb, gain)


def dn_bwd(qkv, z, ab, conv_w, alog, dtb, gain, sall, tall, do):
    bsz, t, _ = qkv.shape
    nc = t // DN_CHUNK
    c = DN_CHUNK
    nh = bsz * DN_HEADS
    w3 = 3 * DN_WIDTH

    def body(qkv_ref, prev_ref, z_ref, ab_ref, w_ref, al_ref, dt_ref, g_ref, sall_ref, tall_ref, do_ref,
             dp_ref, dw_ref, dal_ref, ddt_ref, dg_ref, xe, dye, dc_sc, ds_sc):
        n = pl.program_id(0)
        first = (nc - 1 - n) == 0

        @pl.when(n == 0)
        def _():
            dye[:, c:c + 8, :] = jnp.zeros((bsz, 8, w3), f32)
            ds_sc[...] = jnp.zeros_like(ds_sc)
            dw_ref[...] = jnp.zeros_like(dw_ref)
            dal_ref[...] = jnp.zeros_like(dal_ref)
            ddt_ref[...] = jnp.zeros_like(ddt_ref)
            dg_ref[...] = jnp.zeros_like(dg_ref)

        lt, ltt = _dn_consts()
        lane_c = lax.broadcasted_iota(jnp.int32, (c, 128), 1)
        ys, sigs = [], []
        for b in range(bsz):
            xe[b, 0:8, :] = jnp.where(first, 0.0, prev_ref[b])
            xe[b, 8:8 + c, :] = qkv_ref[b]
            ys.append(_conv_rows(xe, b, w_ref))
            sigs.append(_sigmoid(ys[b]))
        gates = [_dn_gates(ab_ref[b], al_ref[...], dt_ref[...]) for b in range(bsz)]
        ops = _dn_chains([y * sg for y, sg in zip(ys, sigs)], gates, z_ref)
        tt = tall_ref[0]
        _, vjp = jax.vjp(lambda *p: dn_chunk(*p, lt, ltt, t_given=tt)[0:2], *ops, sall_ref[0], g_ref[...])
        don = jnp.stack([do_ref[b, :, h * DN_DIM:(h + 1) * DN_DIM] for b in range(bsz) for h in range(DN_HEADS)])
        dcq, dck, dcv, dg, dbeta, dzz, dsp, dgn = vjp((don, ds_sc[...]))
        ds_sc[...] = dsp
        dg_ref[...] += dgn
        for b in range(bsz):
            dgate = jnp.zeros((c, 128), f32)
            for h in range(DN_HEADS):
                i = b * DN_HEADS + h
                dc_sc[b, :, h * DN_DIM:(h + 1) * DN_DIM] = dcq[i]
                dc_sc[b, :, DN_WIDTH + h * DN_DIM:DN_WIDTH + (h + 1) * DN_DIM] = dck[i]
                dc_sc[b, :, 2 * DN_WIDTH + h * DN_DIM:2 * DN_WIDTH + (h + 1) * DN_DIM] = dcv[i]
                dp_ref[b, :, C_Z + h * DN_DIM:C_Z + (h + 1) * DN_DIM] = dzz[i].astype(bf16)
                dgate = dgate + jnp.where(lane_c == h, dg[i], 0.0) + jnp.where(lane_c == DN_HEADS + h, dbeta[i], 0.0)
            gg, beta, sig_pre = gates[b]
            is_g = lane_c < DN_HEADS
            dpre = jnp.where(is_g, dgate * (-jnp.exp(al_ref[...])) * sig_pre, 0.0)
            dp_ref[b, :, C_AB:C_AB + 128] = (dpre + jnp.where(is_g, 0.0, dgate * beta * (1.0 - beta))).astype(bf16)
            dp_ref[b, :, C_AB + 128:DN_COLS] = jnp.zeros((c, DN_COLS - C_AB - 128), bf16)
            dal_ref[...] += jnp.sum(jnp.where(is_g, dgate * gg, 0.0), axis=0, keepdims=True)
            ddt_ref[...] += jnp.sum(dpre, axis=0, keepdims=True)
            y, sig = ys[b], sigs[b]
            dy = dc_sc[b] * (sig * (1.0 + y * (1.0 - sig)))
            dye[b, 0:c, :] = dy
            dx = w_ref[3:4, :] * dy
            for i in range(DN_CONV - 1):
                dx = dx + w_ref[i:i + 1, :] * dye[b, pl.ds(3 - i, c), :]
            dp_ref[b, :, 0:w3] = dx.astype(bf16)
            for i in range(DN_CONV):
                dw_ref[i:i + 1, :] += jnp.sum(dy * xe[b, pl.ds(5 + i, c), :], axis=0, keepdims=True)
            dye[b, c:c + 8, :] = dye[b, 0:8, :]

    rev = lambda w: pl.BlockSpec((bsz, c, w), lambda n: (0, nc - 1 - n, 0))
    full = lambda shp: pl.BlockSpec(shp, lambda n: (0,) * len(shp))
    prev = pl.BlockSpec((bsz, 8, w3), lambda n: (0, jnp.maximum((nc - 1 - n) * (c // 8) - 1, 0), 0))
    return pl.pallas_call(
        body, name="dn_bwd", grid=(nc,),
        in_specs=[rev(w3), prev, rev(DN_WIDTH), rev(128), full((8, w3)), full((1, 128)), full((1, 128)), full((1, 128)),
                  pl.BlockSpec((1, nh, DN_DIM, DN_DIM), lambda n: (nc - 1 - n, 0, 0, 0)),
                  pl.BlockSpec((1, nh, c, c), lambda n: (nc - 1 - n, 0, 0, 0)), rev(DN_WIDTH)],
        out_specs=[rev(DN_COLS), full((8, w3)), full((1, 128)), full((1, 128)), full((1, 128))],
        out_shape=[SDS((bsz, t, IN_PAD), bf16), SDS((8, w3), f32), SDS((1, 128), f32), SDS((1, 128), f32), SDS((1, 128), f32)],
        scratch_shapes=[pltpu.VMEM((bsz, c + 8, w3), f32), pltpu.VMEM((bsz, c + 8, w3), f32), pltpu.VMEM((bsz, c, w3), f32),
                        pltpu.VMEM((nh, DN_DIM, DN_DIM), f32)],
        compiler_params=_cp(("arbitrary",)),
    )(qkv, qkv, z, ab, conv_w, alog, dtb, gain, sall, tall, do)


SB_TILE = 256
SB_QTILE, SB_KTILE = 256, 256
SB_PAIRS = SB_HEADS // 2


def sb_fwd(sbqkv, gq, gk):
    bsz, t, _ = sbqkv.shape
    bq = min(SB_QTILE, t)
    blk = max(min(SB_KTILE, t), bq)
    nq = t // bq
    scale = SB_DIM ** -0.5

    def body(q_ref, k_ref, v_ref, gq_ref, gk_ref, o_ref, l_ref, q2_sc, kn_sc, v_sc):
        bavg = _group_avg_mats()
        lane = lax.broadcasted_iota(jnp.int32, (1, 128), 1)
        first = lane < SB_DIM
        for p in range(SB_PAIRS):
            ls = slice(p * 128, (p + 1) * 128)
            qn = _pair_norm(q_ref[0, :, ls], gq_ref[...], bavg)
            kn_sc[p] = _pair_norm(k_ref[0, :, ls], gk_ref[...], bavg).astype(bf16)
            v_sc[p] = v_ref[0, :, ls].astype(bf16)
            q2_sc[2 * p] = jnp.where(first, qn, 0.0).astype(bf16)
            q2_sc[2 * p + 1] = jnp.where(first, 0.0, qn).astype(bf16)
        r, c = _iota2((blk, blk))
        ustrict = (r > c).astype(bf16)
        r2, c2 = _iota2((2 * bq, blk))

        def tile(q2s, ks, carry, causal):
            out = []
            for p in range(SB_PAIRS):
                acc, rr = carry[2 * p], carry[2 * p + 1]
                zz = lax.dot_general(q2s[p], kn_sc[p, pl.ds(ks, blk), :], NT, preferred_element_type=f32) * scale
                sp = _softplus(zz)
                lm = -sp if causal is None else jnp.where(causal, -sp, 0.0)
                rem = _dot_x2c(lm, ustrict)
                wgt = jnp.exp(zz - sp + rem + rr)
                if causal is not None:
                    wgt = jnp.where(causal, wgt, 0.0)
                out += [acc + _pdot(wgt.astype(bf16), v_sc[p, pl.ds(ks, blk), :]), rr + jnp.sum(lm, axis=1, keepdims=True)]
            return tuple(out)

        def qloop(qi, _):
            qs = pl.multiple_of(qi * bq, bq)
            kd = qs // blk
            causal = c2 < (r2 & (bq - 1)) + (qs - kd * blk)
            q2s = [jnp.concatenate([q2_sc[2 * p, pl.ds(qs, bq), :], q2_sc[2 * p + 1, pl.ds(qs, bq), :]], axis=0)
                   for p in range(SB_PAIRS)]
            zero = (jnp.zeros((2 * bq, 128), f32), jnp.zeros((2 * bq, 1), f32)) * SB_PAIRS
            carry = lax.fori_loop(1, kd + 1, lambda i, cr: tile(q2s, pl.multiple_of((kd - i) * blk, blk), cr, None),
                                  tile(q2s, pl.multiple_of(kd * blk, blk), zero, causal))
            for p in range(SB_PAIRS):
                acc, rr = carry[2 * p], carry[2 * p + 1]
                o_ref[0, pl.ds(qs, bq), p * 128:(p + 1) * 128] = jnp.where(first, acc[0:bq], acc[bq:2 * bq])
                l_ref[0, pl.ds(qs, bq), p * 128:(p + 1) * 128] = jnp.where(first, rr[0:bq], rr[bq:2 * bq])
            return 0

        lax.fori_loop(0, nq, qloop, 0)

    col = lambda off: pl.BlockSpec((1, t, SB_WIDTH), lambda b: (b, 0, off))
    gsp = pl.BlockSpec((1, 128), lambda b: (0, 0))
    return pl.pallas_call(
        body, name="sb_fwd", grid=(bsz,),
        in_specs=[col(0), col(1), col(2), gsp, gsp],
        out_specs=[col(0), col(0)],
        out_shape=[SDS((bsz, t, SB_WIDTH), f32), SDS((bsz, t, SB_WIDTH), f32)],
        scratch_shapes=[pltpu.VMEM((2 * SB_PAIRS, t, 128), bf16), pltpu.VMEM((SB_PAIRS, t, 128), bf16),
                        pltpu.VMEM((SB_PAIRS, t, 128), bf16)],
        compiler_params=_cp(("arbitrary",)),
    )(sbqkv, sbqkv, sbqkv, gq, gk)


def sb_bwd(sbqkv, gq, gk, ltot, do, dproj):
    bsz, t, _ = sbqkv.shape
    blk = min(SB_TILE, t)
    nq = t // blk
    scale = SB_DIM ** -0.5

    def body(q_ref, k_ref, v_ref, gq_ref, gk_ref, l_ref, do_ref, dp_in, dp_ref, dgq_ref, dgk_ref,
             q2_sc, kn_sc, v_sc, do2_sc, dqn_sc, dkn_sc, dv_sc):
        bavg = _group_avg_mats()
        lane = lax.broadcasted_iota(jnp.int32, (1, 128), 1)
        first = lane < SB_DIM
        fq = lambda x, g: _pair_norm(x, g, bavg)
        vjps = []
        for p in range(SB_PAIRS):
            ls = slice(p * 128, (p + 1) * 128)
            qn, q_vjp = jax.vjp(fq, q_ref[0, :, ls], gq_ref[...])
            kn, k_vjp = jax.vjp(fq, k_ref[0, :, ls], gk_ref[...])
            vjps.append((q_vjp, k_vjp))
            kn_sc[p] = kn.astype(bf16)
            v_sc[p] = v_ref[0, :, ls].astype(bf16)
            dov = do_ref[0, :, ls]
            q2_sc[2 * p] = jnp.where(first, qn, 0.0).astype(bf16)
            q2_sc[2 * p + 1] = jnp.where(first, 0.0, qn).astype(bf16)
            do2_sc[2 * p] = jnp.where(first, dov, 0.0).astype(bf16)
            do2_sc[2 * p + 1] = jnp.where(first, 0.0, dov).astype(bf16)
        dkn_sc[...] = jnp.zeros_like(dkn_sc)
        dv_sc[...] = jnp.zeros_like(dv_sc)
        r, c = _iota2((blk, blk))
        pincl = (r <= c).astype(bf16)
        pstrict = (r < c).astype(bf16)
        r2, c2 = _iota2((2 * blk, blk))
        causal = c2 < (r2 & (blk - 1))

        def tile(q2s, do2s, lts, ks, carry, diag):
            out = []
            for p in range(SB_PAIRS):
                dq, cs, ce = carry[3 * p:3 * p + 3]
                q2, do2 = q2s[p], do2s[p]
                kb = kn_sc[p, pl.ds(ks, blk), :]
                zz = lax.dot_general(q2, kb, NT, preferred_element_type=f32) * scale
                sp = _softplus(zz)
                lm = jnp.where(causal, -sp, 0.0) if diag else -sp
                pre = _dot_x2c(lm, pincl)
                lp = zz - sp
                wgt = jnp.exp(lp + (lts[p] - cs - pre))
                if diag:
                    wgt = jnp.where(causal, wgt, 0.0)
                dw = lax.dot_general(do2, v_sc[p, pl.ds(ks, blk), :], NT, preferred_element_type=f32)
                e = wgt * dw
                ee = ce + _dot_x2c(e, pstrict)
                sig = jnp.exp(lp)
                dz = (e * (1.0 - sig) - ee * sig) * scale
                if diag:
                    dz = jnp.where(causal, dz, 0.0)
                dz = dz.astype(bf16)
                dkn_sc[p, pl.ds(ks, blk), :] += lax.dot_general(dz, q2, TN, preferred_element_type=f32)
                dv_sc[p, pl.ds(ks, blk), :] += lax.dot_general(wgt.astype(bf16), do2, TN, preferred_element_type=f32)
                out += [dq + _pdot(dz, kb), cs + jnp.sum(lm, axis=1, keepdims=True), ce + jnp.sum(e, axis=1, keepdims=True)]
            return tuple(out)

        def qloop(qi, _):
            qs = pl.multiple_of(qi * blk, blk)
            rows = pl.ds(qs, blk)
            q2s = [jnp.concatenate([q2_sc[2 * p, rows, :], q2_sc[2 * p + 1, rows, :]], axis=0) for p in range(SB_PAIRS)]
            do2s = [jnp.concatenate([do2_sc[2 * p, rows, :], do2_sc[2 * p + 1, rows, :]], axis=0) for p in range(SB_PAIRS)]
            lts = [jnp.concatenate([l_ref[0, rows, p * 128:p * 128 + 1], l_ref[0, rows, p * 128 + SB_DIM:p * 128 + SB_DIM + 1]],
                                   axis=0) for p in range(SB_PAIRS)]
            z1 = jnp.zeros((2 * blk, 1), f32)
            carry = lax.fori_loop(0, qi, lambda kj, cr: tile(q2s, do2s, lts, pl.multiple_of(kj * blk, blk), cr, False),
                                  (jnp.zeros((2 * blk, 128), f32), z1, z1) * SB_PAIRS)
            carry = tile(q2s, do2s, lts, qs, carry, True)
            for p in range(SB_PAIRS):
                dq = carry[3 * p]
                dqn_sc[p, rows, :] = jnp.where(first, dq[0:blk], dq[blk:2 * blk])
            return 0

        lax.fori_loop(0, nq, qloop, 0)
        dgq_tot, dgk_tot = jnp.zeros((1, 128), f32), jnp.zeros((1, 128), f32)
        for p in range(SB_PAIRS):
            ls = slice(p * 128, (p + 1) * 128)
            dq_pre, dgq = vjps[p][0](dqn_sc[p])
            dk_pre, dgk = vjps[p][1](dkn_sc[p])
            dp_ref[0, :, p * 128:(p + 1) * 128] = dq_pre.astype(bf16)
            dp_ref[0, :, SB_WIDTH + p * 128:SB_WIDTH + (p + 1) * 128] = dk_pre.astype(bf16)
            dp_ref[0, :, 2 * SB_WIDTH + p * 128:2 * SB_WIDTH + (p + 1) * 128] = dv_sc[p].astype(bf16)
            dgq_tot, dgk_tot = dgq_tot + dgq, dgk_tot + dgk
        dgq_ref[0] = jnp.broadcast_to(dgq_tot, (8, 128))
        dgk_ref[0] = jnp.broadcast_to(dgk_tot, (8, 128))

    col = lambda off: pl.BlockSpec((1, t, SB_WIDTH), lambda b: (b, 0, off), pipeline_mode=pl.Buffered(1))
    gsp = pl.BlockSpec((1, 128), lambda b: (0, 0))
    gout = pl.BlockSpec((1, 8, 128), lambda b: (b, 0, 0))
    return pl.pallas_call(
        body, name="sb_bwd", grid=(bsz,),
        in_specs=[col(0), col(1), col(2), gsp, gsp, col(0), col(0), pl.BlockSpec(memory_space=pl.ANY)],
        out_specs=[pl.BlockSpec((1, t, 3 * SB_WIDTH), lambda b: (b, 0, C_SB // (3 * SB_WIDTH)), pipeline_mode=pl.Buffered(1)),
                   gout, gout],
        out_shape=[SDS(dproj.shape, bf16)] + [SDS((bsz, 8, 128), f32)] * 2,
        input_output_aliases={7: 0},
        scratch_shapes=[pltpu.VMEM((2 * SB_PAIRS, t, 128), bf16), pltpu.VMEM((SB_PAIRS, t, 128), bf16),
                        pltpu.VMEM((SB_PAIRS, t, 128), bf16), pltpu.VMEM((2 * SB_PAIRS, t, 128), bf16),
                        pltpu.VMEM((SB_PAIRS, t, 128), f32), pltpu.VMEM((SB_PAIRS, t, 128), f32), pltpu.VMEM((SB_PAIRS, t, 128), f32)],
        compiler_params=_cp(("arbitrary",)),
    )(sbqkv, sbqkv, sbqkv, gq, gk, ltot, do, dproj)


def sg_pair(u, v, gain, wa, wb, ba, bb, bavg):
    r, c = _iota2((SG_CHUNK, SG_CHUNK))
    lane = lax.broadcasted_iota(jnp.int32, (1, 128), 1)
    first = lane < SG_DIM
    vn = _pair_norm(_gelu(v), gain, bavg)
    tri = c <= r
    mixed = (mm(jnp.where(tri, wa, 0.0), jnp.where(first, vn, 0.0)) + mm(jnp.where(tri, wb, 0.0), jnp.where(first, 0.0, vn))
             + jnp.where(first, ba, bb))
    return _gelu(u) * mixed


def sg_fwd(sguv, gain, w, bt):
    bsz, t, _ = sguv.shape
    nch = t // SG_CHUNK

    def body(uv_ref, g_ref, w_ref, b_ref, o_ref):
        bavg = _group_avg_mats()
        for p in range(2):
            ls = slice(p * 128, (p + 1) * 128)
            o_ref[0, :, ls] = sg_pair(uv_ref[0, :, ls], uv_ref[0, :, SG_WIDTH + p * 128:SG_WIDTH + (p + 1) * 128], g_ref[:, ls],
                                      w_ref[2 * p], w_ref[2 * p + 1], b_ref[:, 2 * p:2 * p + 1], b_ref[:, 2 * p + 1:2 * p + 2], bavg)

    full = lambda shp: pl.BlockSpec(shp, lambda b, n: (0,) * len(shp))
    return pl.pallas_call(
        body, name="sg_fwd", grid=(bsz, nch),
        in_specs=[pl.BlockSpec((1, SG_CHUNK, 2 * SG_WIDTH), lambda b, n: (b, n, 0)), full((1, SG_WIDTH)),
                  full((SG_GROUPS, SG_CHUNK, SG_CHUNK)), full((SG_CHUNK, 128))],
        out_specs=pl.BlockSpec((1, SG_CHUNK, SG_WIDTH), lambda b, n: (b, n, 0)),
        out_shape=SDS((bsz, t, SG_WIDTH), f32),
        compiler_params=_cp(("arbitrary", "arbitrary")),
    )(sguv, gain, w, bt)


def sg_bwd(sguv, gain, w, bt, do, dproj):
    bsz, t, _ = sguv.shape
    nch = t // SG_CHUNK

    def body(uv_ref, g_ref, w_ref, b_ref, do_ref, dp_in, duv_ref, dg_ref, dw_ref, db_ref):
        @pl.when((pl.program_id(0) == 0) & (pl.program_id(1) == 0))
        def _():
            dg_ref[...] = jnp.zeros_like(dg_ref)
            dw_ref[...] = jnp.zeros_like(dw_ref)
            db_ref[...] = jnp.zeros_like(db_ref)

        bavg = _group_avg_mats()
        lane = lax.broadcasted_iota(jnp.int32, (SG_CHUNK, 128), 1)
        dbt = jnp.zeros((SG_CHUNK, 128), f32)
        for p in range(2):
            ls = slice(p * 128, (p + 1) * 128)
            vs = slice(SG_WIDTH + p * 128, SG_WIDTH + (p + 1) * 128)
            prim = (uv_ref[0, :, ls], uv_ref[0, :, vs], g_ref[:, ls], w_ref[2 * p], w_ref[2 * p + 1],
                    b_ref[:, 2 * p:2 * p + 1], b_ref[:, 2 * p + 1:2 * p + 2])
            _, vjp = jax.vjp(lambda *a: sg_pair(*a, bavg), *prim)
            du, dv, dgn, dwa, dwb, dba, dbb = vjp(do_ref[0, :, ls])
            duv_ref[0, :, ls] = du.astype(bf16)
            duv_ref[0, :, vs] = dv.astype(bf16)
            dg_ref[:, ls] += dgn
            dw_ref[2 * p] += dwa
            dw_ref[2 * p + 1] += dwb
            dbt = dbt + jnp.where(lane == 2 * p, dba, 0.0) + jnp.where(lane == 2 * p + 1, dbb, 0.0)
        db_ref[...] += dbt

    full = lambda shp: pl.BlockSpec(shp, lambda b, n: (0,) * len(shp))
    return pl.pallas_call(
        body, name="sg_bwd", grid=(bsz, nch),
        in_specs=[pl.BlockSpec((1, SG_CHUNK, 2 * SG_WIDTH), lambda b, n: (b, n, 0)), full((1, SG_WIDTH)),
                  full((SG_GROUPS, SG_CHUNK, SG_CHUNK)), full((SG_CHUNK, 128)),
                  pl.BlockSpec((1, SG_CHUNK, SG_WIDTH), lambda b, n: (b, n, 0)), pl.BlockSpec(memory_space=pl.ANY)],
        out_specs=[pl.BlockSpec((1, SG_CHUNK, 2 * SG_WIDTH), lambda b, n: (b, n, C_SG // (2 * SG_WIDTH))), full((1, SG_WIDTH)),
                   full((SG_GROUPS, SG_CHUNK, SG_CHUNK)), full((SG_CHUNK, 128))],
        out_shape=[SDS(dproj.shape, bf16), SDS((1, SG_WIDTH), f32), SDS((SG_GROUPS, SG_CHUNK, SG_CHUNK), f32),
                   SDS((SG_CHUNK, 128), f32)],
        input_output_aliases={5: 0},
        compiler_params=_cp(("arbitrary", "arbitrary")),
    )(sguv, gain, w, bt, do, dproj)


def _pad_lanes(v, n=128):
    return jnp.pad(v.reshape(1, -1), ((0, 0), (0, n - v.size)))


def _w_in_runs():
    shard, runs = IN_DIM // N_CHIPS, []
    for s in range(N_CHIPS):
        for a, b, d in ((0, 2048, 0), (2048, 2056, C_AB), (2056, IN_DIM, C_SB)):
            lo, hi = max(shard * s, a), min(shard * (s + 1), b)
            if lo < hi:
                runs.append((s, lo - shard * s, hi - shard * s, d + lo - a))
    return runs


def w_in_from_shards(zone, tr=256):
    def body(z_ref, o_ref):
        o_ref[:, C_AB:C_SB] = jnp.zeros((tr, C_SB - C_AB), zone.dtype)
        for s, a, b, d in _w_in_runs():
            o_ref[:, d:d + b - a] = z_ref[s, :, a:b]

    return pl.pallas_call(
        body, name="w_in_from_shards", grid=(D_MODEL // tr,),
        in_specs=[pl.BlockSpec((N_CHIPS, tr, IN_DIM // N_CHIPS), lambda i: (0, i, 0))],
        out_specs=pl.BlockSpec((tr, IN_PAD), lambda i: (i, 0)), out_shape=SDS((D_MODEL, IN_PAD), zone.dtype),
        compiler_params=_cp(("arbitrary",)))(zone)


def w_in_grad_to_shards(g, tr=256):
    def body(g_ref, o_ref):
        for s, a, b, d in _w_in_runs():
            o_ref[s, :, a:b] = g_ref[:, d:d + b - a]

    return pl.pallas_call(
        body, name="w_in_grad_to_shards", grid=(D_MODEL // tr,),
        in_specs=[pl.BlockSpec((tr, IN_PAD), lambda i: (i, 0))],
        out_specs=pl.BlockSpec((N_CHIPS, tr, IN_DIM // N_CHIPS), lambda i: (0, i, 0)),
        out_shape=SDS((N_CHIPS, D_MODEL, IN_DIM // N_CHIPS), g.dtype), compiler_params=_cp(("arbitrary",)))(g)


def layer_params(p, l):
    return dict(
        g1=p["norm1_g"][l].reshape(1, -1), g2=p["norm2_g"][l].reshape(1, -1),
        conv=jnp.pad(p["conv_w"][l], ((0, 4), (0, 0))), alog=_pad_lanes(p["a_log"][l]), dtb=_pad_lanes(p["dt_bias"][l]),
        dng=p["dn_out_g"][l].reshape(1, -1), gq=jnp.tile(p["sb_q_g"][l].reshape(1, -1), (1, 2)),
        gk=jnp.tile(p["sb_k_g"][l].reshape(1, -1), (1, 2)), sgg=p["sg_v_g"][l].reshape(1, -1), sgw=p["sg_w"][l],
        sgb=jnp.pad(p["sg_b"][l].T, ((0, 0), (0, 124))))


def local_step(x, tgt, small, get_w, put_g, sync_g):
    bsz, t, _ = x.shape
    m = bsz * t
    r3 = lambda a: a.reshape(bsz, t, a.shape[-1])
    r2 = lambda a: a.reshape(m, a.shape[-1])
    xs, saved, ws = x.reshape(m, D_MODEL), [], []
    for l in range(DEPTH):
        sp, w = layer_params(small, l), {}
        w["w_in"] = get_w(l, "in", xs)
        qkv, z, ab, sb, sg, h1 = inproj_fwd(xs, sp["g1"], w["w_in"])
        odn, sall, tall = dn_fwd(r3(qkv), r3(z), r3(ab), sp["conv"], sp["alog"], sp["dtb"], sp["dng"])
        osb, ltot = sb_fwd(r3(sb), sp["gq"], sp["gk"])
        osg = sg_fwd(r3(sg), sp["sgg"], sp["sgw"], sp["sgb"])
        w["w_out"] = get_w(l, "out", osg)
        x2, mix = outproj_fwd(xs, r2(odn), r2(osb), r2(osg), w["w_out"])
        w["w_ff1"], w["w_ff2"], started = get_w(l, "ff", x2)
        if l + 1 < DEPTH:
            xs_next, rlb = ffn_fwd(x2, sp["g2"] + started, w["w_ff1"], w["w_ff2"])
        else:
            dx, rlb, lossp = ffn_fwd(x2, sp["g2"] + started, w["w_ff1"], w["w_ff2"], tgt=tgt.reshape(m, D_MODEL))
        saved.append(dict(rlb=rlb, h1=h1, x=xs, qkv=qkv, z=z, ab=ab, sb=sb, sg=sg, sall=sall, tall=tall, ltot=ltot, mix=mix, x2=x2))
        ws.append(w)
        xs = xs_next
    gsmall = [None] * DEPTH
    token = jnp.zeros((), f32)
    for l in reversed(range(DEPTH)):
        sp, w, s = layer_params(small, l), ws[l], saved[l]
        dx2, dg2, h2, act, df, dyb = ffn_bwd(s["x2"], sp["g2"] + token, w["w_ff1"], w["w_ff2"], s["rlb"], dx)
        g_ff1 = tn_matmul(h2, df, f"dw_ff1_{l}", col_shards=N_CHIPS)
        g_ff2 = tn_matmul(act, dyb, f"dw_ff2_{l}")
        dodn, dosb, dosg, dx2b = outproj_bwd(dx2, w["w_out"])
        g_out = tn_matmul(s["mix"], dx2b, f"dw_out_{l}")
        token = token + put_g(l, "rest", dict(w_out=g_out, w_ff1=g_ff1, w_ff2=g_ff2))
        dproj, dconv, dalog, ddtb, ddng = dn_bwd(r3(s["qkv"]), r3(s["z"]), r3(s["ab"]), sp["conv"], sp["alog"], sp["dtb"],
                                                 sp["dng"] + token, s["sall"], s["tall"], r3(dodn))
        token = sync_g(ddng)
        dproj, dgq, dgk = sb_bwd(r3(s["sb"]), sp["gq"] + token, sp["gk"], s["ltot"], r3(dosb), dproj)
        dproj, dsgg, dsgw, dsgb = sg_bwd(r3(s["sg"]), sp["sgg"], sp["sgw"], sp["sgb"], r3(dosg), dproj)
        dproj = r2(dproj)
        g_in = tn_matmul(s["h1"], dproj, f"dw_in_{l}")
        token = put_g(l, "in", dict(w_in=g_in))
        dx, dg1 = inproj_bwd(s["x"], sp["g1"] + token, w["w_in"], dproj, dx2)
        token = sync_g(dg1)
        fold = lambda a: (a[:, 0, :].sum(0).reshape(2, SB_DIM)).sum(0)
        gsmall[l] = dict(norm1_g=dg1[0], conv_w=dconv[0:DN_CONV], a_log=dalog[0, 0:DN_HEADS], dt_bias=ddtb[0, 0:DN_HEADS],
                         dn_out_g=ddng[0], sb_q_g=fold(dgq), sb_k_g=fold(dgk), sg_v_g=dsgg[0], sg_w=dsgw,
                         sg_b=dsgb[:, 0:SG_GROUPS].T, norm2_g=dg2[0])
    return lossp, dx.reshape(bsz, t, D_MODEL), gsmall


def _chip_peers(x, y):
    return [(1 - x, y), (x, 1 - y), (1 - x, 1 - y)]


_HBM = pl.BlockSpec(memory_space=pltpu.HBM)
_SEM = pl.BlockSpec(memory_space=pltpu.SEMAPHORE)
_EFFECT = pltpu.SideEffectType.DATAFLOW_SIDE_EFFECTING


def _hbm(a):
    return pltpu.with_memory_space_constraint(a, pltpu.HBM)


def _my_half(ref):
    half = ref.shape[0] // 2
    return ref.at[pl.ds(pl.multiple_of(lax.axis_index("c") * half, 8), half)]


def _exchange_copy(src, land, k, j, send, recv, scatter, halve, waiting):
    x, y, c = lax.axis_index("x"), lax.axis_index("y"), lax.axis_index("c")
    px, py = _chip_peers(x, y)[j]
    me, peer = 2 * x + y, 2 * px + py
    if scatter:
        src = src.at[me if waiting else peer]
    dst = land.at[peer if waiting else me]
    if halve:
        src, dst = _my_half(src), _my_half(dst)
    return pltpu.make_async_remote_copy(src_ref=src, dst_ref=dst, send_sem=send.at[3 * k + j],
                                        recv_sem=recv.at[3 * k + j], device_id=(px, py, c), device_id_type=MESH)


def exchange_start(items, name, scatter, after=None):
    arrs = []
    for a, _, _ in items:
        if not any(a is b for b in arrs):
            arrs.append(a)
    pos = [next(i for i, b in enumerate(arrs) if b is a) for a, _, _ in items]
    shapes = [a.shape if idx is None else a.shape[1:] for a, idx, _ in items]
    lands = [lax.empty(s if scatter else (N_CHIPS,) + s, a.dtype) for (a, _, _), s in zip(items, shapes)]
    na, nl = len(arrs), len(lands)
    n_in = na + nl + (after is not None)

    def body(*refs):
        ins, lnd = refs[:na], refs[na:na + nl]
        send, recv = refs[n_in], refs[n_in + 1]
        token = refs[-1]
        for k, (_, idx, halve) in enumerate(items):
            src = ins[pos[k]] if idx is None else ins[pos[k]].at[idx]
            for j in range(3):
                _exchange_copy(src, lnd[k], k, j, send, recv, scatter, halve, False).start()
        token[...] = jnp.zeros_like(token)

    sems = pltpu.SemaphoreType.DMA((3 * nl,))
    extra = [] if after is None else [after]
    out = pl.pallas_call(
        body, name=name,
        out_shape=(sems, sems, *[pltpu.HBM(a.shape, a.dtype) for a in arrs + lands], SDS((8, 128), f32)),
        in_specs=[_HBM] * (na + nl) + [pl.BlockSpec(memory_space=pl.ANY)] * len(extra),
        out_specs=(_SEM, _SEM, *[_HBM] * (na + nl), pl.BlockSpec(memory_space=pltpu.VMEM)),
        input_output_aliases={i: 2 + i for i in range(na + nl)},
        compiler_params=pltpu.CompilerParams(has_side_effects=_EFFECT),
    )(*[_hbm(a) for a in arrs + lands], *extra)
    thru = out[2:2 + na]
    return dict(send=out[0], recv=out[1], src=[(thru[pos[k]], idx) for k, (_, idx, _) in enumerate(items)],
                halve=[h for _, _, h in items], land=list(out[2 + na:2 + na + nl]), token=out[-1], scatter=scatter)


def exchange_wait(st, ks, after, name):
    arrs = []
    for k in ks:
        if not any(st["src"][k][0] is b for b in arrs):
            arrs.append(st["src"][k][0])
    pos = [next(i for i, b in enumerate(arrs) if b is st["src"][k][0]) for k in ks]
    lands = [st["land"][k] for k in ks]
    na, nl = len(arrs), len(lands)

    def body(*refs):
        ins, lnd = refs[:na], refs[na:na + nl]
        send, recv = refs[na + nl], refs[na + nl + 1]
        for t, k in enumerate(ks):
            idx = st["src"][k][1]
            src = ins[pos[t]] if idx is None else ins[pos[t]].at[idx]
            for j in range(3):
                cp = _exchange_copy(src, lnd[t], k, j, send, recv, st["scatter"], st["halve"][k], True)
                cp.wait_send()
                cp.wait_recv()

    out = pl.pallas_call(
        body, name=name, out_shape=tuple(pltpu.HBM(a.shape, a.dtype) for a in arrs + lands),
        in_specs=[_HBM] * (na + nl) + [_SEM, _SEM, pl.BlockSpec(memory_space=pl.ANY)], out_specs=tuple([_HBM] * (na + nl)),
        input_output_aliases={i: i for i in range(na + nl)},
        compiler_params=pltpu.CompilerParams(has_side_effects=_EFFECT),
    )(*arrs, *lands, st["send"], st["recv"], after)
    for k, (a, idx) in enumerate(st["src"]):
        for p, b in enumerate(arrs):
            if a is b:
                st["src"][k] = (out[p], idx)
    return list(out[na:na + nl])


def _sibling_copy(src, land, i, send, recv, other_half):
    x, y, c = lax.axis_index("x"), lax.axis_index("y"), lax.axis_index("c")
    return pltpu.make_async_remote_copy(src_ref=src.at[:, 1 - c] if other_half else src, dst_ref=land, send_sem=send.at[i],
                                        recv_sem=recv.at[i], device_id=(x, y, 1 - c), device_id_type=MESH)


def sibling_start(arrs, name, other_half=False):
    n = len(arrs)
    lands = [lax.empty((a.shape[0],) + a.shape[2:] if other_half else a.shape, a.dtype) for a in arrs]

    def body(*refs):
        ins, lnd = refs[:n], refs[n:2 * n]
        send, recv = refs[2 * n], refs[2 * n + 1]
        token = refs[-1]
        for i in range(n):
            _sibling_copy(ins[i], lnd[i], i, send, recv, other_half).start()
        token[...] = jnp.zeros_like(token)

    sems = pltpu.SemaphoreType.DMA((n,))
    out = pl.pallas_call(
        body, name=name,
        out_shape=(sems, sems, *[pltpu.HBM(a.shape, a.dtype) for a in arrs + lands], SDS((8, 128), f32)),
        in_specs=[_HBM] * (2 * n), out_specs=(_SEM, _SEM, *[_HBM] * (2 * n), pl.BlockSpec(memory_space=pltpu.VMEM)),
        input_output_aliases={i: 2 + i for i in range(2 * n)},
        compiler_params=pltpu.CompilerParams(has_side_effects=_EFFECT),
    )(*[_hbm(a) for a in arrs + lands])
    return dict(send=out[0], recv=out[1], src=list(out[2:2 + n]), land=list(out[2 + n:2 + 2 * n]), token=out[-1],
                other_half=other_half)


def sibling_wait(st, after, name):
    n = len(st["src"])

    def body(*refs):
        ins, lnd = refs[:n], refs[n:2 * n]
        send, recv = refs[2 * n], refs[2 * n + 1]
        for i in range(n):
            cp = _sibling_copy(ins[i], lnd[i], i, send, recv, st["other_half"])
            cp.wait_send()
            cp.wait_recv()

    out = pl.pallas_call(
        body, name=name, out_shape=tuple(pltpu.HBM(a.shape, a.dtype) for a in st["src"] + st["land"]),
        in_specs=[_HBM] * (2 * n) + [_SEM, _SEM, pl.BlockSpec(memory_space=pl.ANY)], out_specs=tuple([_HBM] * (2 * n)),
        input_output_aliases={i: i for i in range(2 * n)},
        compiler_params=pltpu.CompilerParams(has_side_effects=_EFFECT),
    )(*st["src"], *st["land"], st["send"], st["recv"], after)
    return list(out[:n]), list(out[n:])


def swap_halves(zones, name):
    n = len(zones)

    def body(*refs):
        outs = refs[n:2 * n]
        send, recv = refs[2 * n:]
        x, y, c = lax.axis_index("x"), lax.axis_index("y"), lax.axis_index("c")
        cps = []
        for i in range(n):
            for j, (px, py) in enumerate(_chip_peers(x, y)):
                part = _my_half(outs[i].at[2 * px + py])
                cps.append(pltpu.make_async_remote_copy(src_ref=part, dst_ref=part, send_sem=send.at[3 * i + j],
                                                        recv_sem=recv.at[3 * i + j], device_id=(x, y, 1 - c), device_id_type=MESH))
        for cp in cps:
            cp.start()
        for cp in cps:
            cp.wait_send()
            cp.wait_recv()

    any_spec = pl.BlockSpec(memory_space=pl.ANY)
    return pl.pallas_call(
        body, name=name, in_specs=[any_spec] * n, out_specs=[any_spec] * n, out_shape=[SDS(a.shape, a.dtype) for a in zones],
        input_output_aliases={i: i for i in range(n)},
        scratch_shapes=[pltpu.SemaphoreType.DMA((3 * n,)), pltpu.SemaphoreType.DMA((3 * n,))],
    )(*zones)


def _ids_spec(grid, in_specs, out_specs):
    return pltpu.PrefetchScalarGridSpec(num_scalar_prefetch=1, grid=grid, in_specs=in_specs, out_specs=out_specs)


def pair_sum(ids, a, b, name, tr=512):
    nd, _, rows, cols = a.shape
    tr = min(tr, rows)
    assert rows % tr == 0

    def body(ids_ref, a_ref, b_ref, o_ref):
        o_ref[...] = (a_ref[0].astype(f32) + b_ref[...].astype(f32)).astype(bf16)

    spec = pl.BlockSpec((1, tr, cols), lambda d, i, ids: (d, i, 0))
    return pl.pallas_call(
        body, name=name,
        grid_spec=_ids_spec((nd, rows // tr), [pl.BlockSpec((1, 1, tr, cols), lambda d, i, ids: (d, ids[1], i, 0)), spec], spec),
        out_shape=SDS((nd, rows, cols), bf16), compiler_params=_cp(("arbitrary", "arbitrary")))(ids, a, b)


def allreduce_small(v):
    half = v.shape[0] // 2
    assert half % 8 == 0

    def body(v_ref, o_ref, rbuf, send, recv):
        x, y, c = lax.axis_index("x"), lax.axis_index("y"), lax.axis_index("c")
        o_ref[...] = v_ref[...]

        def exchange(rows, peer, k):
            return pltpu.make_async_remote_copy(src_ref=o_ref.at[rows], dst_ref=rbuf.at[k, rows], send_sem=send.at[k],
                                                recv_sem=recv.at[k], device_id=peer, device_id_type=MESH)

        lo, hi, across_x, across_y = pl.ds(0, half), pl.ds(half, half), (1 - x, y, c), (x, 1 - y, c)
        stages = [[(pl.ds(0, 2 * half), (x, y, 1 - c))], [(lo, across_x), (hi, across_y)], [(lo, across_y), (hi, across_x)]]
        k = 0
        for stage in stages:
            cps = [exchange(rows, peer, k + i) for i, (rows, peer) in enumerate(stage)]
            for cp in cps:
                cp.start()
            for cp in cps:
                cp.wait()
            for i, (rows, _) in enumerate(stage):
                o_ref[rows] = o_ref[rows] + rbuf[k + i, rows]
            k += len(stage)

    vm = pl.BlockSpec(memory_space=pltpu.VMEM)
    return pl.pallas_call(
        body, name="allreduce_small", in_specs=[vm], out_specs=vm, out_shape=SDS(v.shape, f32),
        scratch_shapes=[pltpu.VMEM((5,) + v.shape, f32), pltpu.SemaphoreType.DMA((5,)), pltpu.SemaphoreType.DMA((5,))],
        compiler_params=_cp(),
    )(v)


def sum_partials(ids, zone, mine, name, tr=256):
    _, rows, cols = zone.shape
    tr = min(tr, rows)
    assert rows % tr == 0

    def body(ids_ref, m_ref, z1_ref, z2_ref, z3_ref, o_ref):
        o_ref[...] = ((m_ref[0].astype(f32) + z1_ref[0].astype(f32)) + z2_ref[0].astype(f32)) + z3_ref[0].astype(f32)

    slot = lambda flip: pl.BlockSpec((1, tr, cols), lambda i, ids: (ids[0] ^ flip, i, 0))
    return pl.pallas_call(
        body, name=name,
        grid_spec=_ids_spec((rows // tr,), [slot(0), slot(1), slot(2), slot(3)], pl.BlockSpec((tr, cols), lambda i, ids: (i, 0))),
        out_shape=SDS((rows, cols), f32), compiler_params=_cp(("arbitrary",)),
    )(ids, mine, zone, zone, zone)


def adamw(w, m, v, gs, name, layer=0, prev=None, tr=256):
    hrows, cols = gs[0].shape
    rows = hrows * len(gs)
    tr = min(tr, hrows)
    assert hrows % tr == 0 and w.shape[0] % rows == 0
    off, nth = layer * (rows // tr), hrows // tr

    def body(w_ref, m_ref, v_ref, *rest):
        g_ref, d_ref, mo_ref, vo_ref = rest[-4:]
        if len(gs) == 1:
            g = rest[0][...]
        else:
            g = jnp.where(pl.program_id(0) // nth == lax.axis_index("c"), rest[0][...], rest[1][...])
        mn = ADAM_B1 * m_ref[...] + (1.0 - ADAM_B1) * g
        vn = ADAM_B2 * v_ref[...] + (1.0 - ADAM_B2) * jnp.square(g)
        m_hat = mn / (1.0 - ADAM_B1 ** ADAM_STEP)
        v_hat = vn / (1.0 - ADAM_B2 ** ADAM_STEP)
        g_ref[...] = g
        d_ref[...] = -ADAM_LR * (m_hat / (jnp.sqrt(v_hat) + ADAM_EPS) + ADAM_WD * w_ref[...])
        mo_ref[...] = mn
        vo_ref[...] = vn

    loc = pl.BlockSpec((tr, cols), lambda i: (i % nth, 0))
    glob = pl.BlockSpec((tr, cols), lambda i: (off + i, 0))
    extra = [] if prev is None else list(prev)
    return pl.pallas_call(
        body, name=name, grid=(rows // tr,),
        in_specs=[glob] * 3 + [loc] * len(gs) + [pl.BlockSpec(memory_space=pl.ANY)] * len(extra),
        out_specs=[glob] * 4, out_shape=[SDS(w.shape, f32)] * 4,
        input_output_aliases={3 + len(gs) + j: j for j in range(len(extra))},
        compiler_params=_cp(("arbitrary",)),
    )(w, m, v, *gs, *extra)


BIG = ("w_in", "w_out", "w_ff1", "w_ff2")
SMALL = ("norm1_g", "conv_w", "a_log", "dt_bias", "dn_out_g", "sb_q_g", "sb_k_g", "sg_v_g", "sg_w", "sg_b", "norm2_g")
WEIGHTS = ("norm1_g", "w_in", "conv_w", "a_log", "dt_bias", "dn_out_g", "sb_q_g", "sb_k_g", "sg_v_g", "sg_w", "sg_b",
           "w_out", "norm2_g", "w_ff1", "w_ff2")


PACK_ROWS = 256


def _rows_of(shape):
    n = 1
    for d in shape:
        n *= d
    return -(-n // 1024) * 8, n


def _pack(arrs):
    parts = []
    for a in arrs:
        r, n = _rows_of(a.shape)
        parts.append(jnp.pad(a.reshape(-1), (0, r * 128 - n)).reshape(r, 128))
    rows = sum(p.shape[0] for p in parts)
    parts.append(jnp.zeros((-rows % PACK_ROWS, 128), arrs[0].dtype))
    return jnp.concatenate(parts, axis=0)


def _unpack(packed, shapes):
    out, o = [], 0
    for s in shapes:
        r, n = _rows_of(s)
        out.append(packed[o:o + r].reshape(-1)[0:n].reshape(s))
        o += r
    return out


def kernel(x, norm1_g, w_in, conv_w, a_log, dt_bias, dn_out_g, sb_q_g, sb_k_g, sg_v_g, sg_w, sg_b, w_out, norm2_g, w_ff1, w_ff2, loss_target, m_norm1_g, m_w_in, m_conv_w, m_a_log, m_dt_bias, m_dn_out_g, m_sb_q_g, m_sb_k_g, m_sg_v_g, m_sg_w, m_sg_b, m_w_out, m_norm2_g, m_w_ff1, m_w_ff2, v_norm1_g, v_w_in, v_conv_w, v_a_log, v_dt_bias, v_dn_out_g, v_sb_q_g, v_sb_k_g, v_sg_v_g, v_sg_w, v_sg_b, v_w_out, v_norm2_g, v_w_ff1, v_w_ff2):
    w = dict(norm1_g=norm1_g, w_in=w_in, conv_w=conv_w, a_log=a_log, dt_bias=dt_bias, dn_out_g=dn_out_g, sb_q_g=sb_q_g,
             sb_k_g=sb_k_g, sg_v_g=sg_v_g, sg_w=sg_w, sg_b=sg_b, w_out=w_out, norm2_g=norm2_g, w_ff1=w_ff1, w_ff2=w_ff2)
    mom = dict(norm1_g=m_norm1_g, w_in=m_w_in, conv_w=m_conv_w, a_log=m_a_log, dt_bias=m_dt_bias, dn_out_g=m_dn_out_g,
               sb_q_g=m_sb_q_g, sb_k_g=m_sb_k_g, sg_v_g=m_sg_v_g, sg_w=m_sg_w, sg_b=m_sg_b, w_out=m_w_out, norm2_g=m_norm2_g,
               w_ff1=m_w_ff1, w_ff2=m_w_ff2)
    var = dict(norm1_g=v_norm1_g, w_in=v_w_in, conv_w=v_conv_w, a_log=v_a_log, dt_bias=v_dt_bias, dn_out_g=v_dn_out_g,
               sb_q_g=v_sb_q_g, sb_k_g=v_sb_k_g, sg_v_g=v_sg_v_g, sg_w=v_sg_w, sg_b=v_sg_b, w_out=v_w_out, norm2_g=v_norm2_g,
               w_ff1=v_w_ff1, w_ff2=v_w_ff2)
    chip = 2 * lax.axis_index("x") + lax.axis_index("y")

    wb = [{k: w[k][l].astype(bf16) for k in BIG} for l in range(DEPTH)]
    ags = {0: exchange_start([(conv_w, None, False)] + [(wb[0][k], None, True) for k in BIG], "allgather_start_0", scatter=False)}
    item = lambda l, k: (l, (l == 0) + BIG.index(k))

    def landed(items, after, name):
        ag, ks = ags[items[0][0]], [k for _, k in items]
        zones = exchange_wait(ag, ks, after, name)
        halved = [t for t, k in enumerate(ks) if ag["halve"][k]]
        for t, z in zip(halved, swap_halves([zones[t] for t in halved], name.replace("wait", "pass"))):
            zones[t] = z
        return [lax.dynamic_update_slice_in_dim(z, ag["src"][k][0][None], chip, axis=0) for z, k in zip(zones, ks)]

    def whole(k, z):
        if k == "w_in":
            return w_in_from_shards(z)
        return z if k == "w_ff1" else z.reshape(-1, D_MODEL)

    g_conv, first_in = landed([(0, 0), item(0, "w_in")], x, "allgather_wait_in0")
    small = {k: w[k] for k in SMALL}
    small["conv_w"] = jnp.transpose(g_conv, (1, 2, 0, 3)).reshape(DEPTH, DN_CONV, 3 * DN_WIDTH)
    cache = {}

    def get_w(l, part, after):
        if part == "in":
            return whole("w_in", first_in if l == 0 else landed([item(l, "w_in")], after, f"allgather_wait_in{l}")[0])
        if part == "out":
            zs = landed([item(l, k) for k in ("w_out", "w_ff1", "w_ff2")], after, f"allgather_wait_rest{l}")
            token = jnp.zeros((), f32)
            if l + 1 < DEPTH:
                ags[l + 1] = exchange_start([(wb[l + 1][k], None, True) for k in BIG], f"allgather_start_{l + 1}",
                                            scatter=False, after=zs[0])
                token = ags[l + 1]["token"][0, 0]
            cache[l] = (whole("w_ff1", zs[1]), whole("w_ff2", zs[2]), token)
            return whole("w_out", zs[0])
        return cache[l]

    rs, pending = {}, []
    ids = jnp.stack([chip, lax.axis_index("c")]).astype(jnp.int32)

    def put_g(l, tag, g):
        names = [k for k in BIG if k in g]
        by_dest = [w_in_grad_to_shards(g[k]) if k == "w_in" else g[k] for k in names]
        halves = [a.reshape(N_CHIPS, 2, -1, a.shape[-1]) for a in by_dest]
        st = sibling_start(halves, f"pair_swap_start_{tag}{l}", other_half=True)
        pending.append((l, tag, names, st))
        return st["token"][0, 0]

    def sync_g(after):
        token = jnp.zeros((), f32)
        while pending:
            l, tag, names, st = pending.pop(0)
            halves, got = sibling_wait(st, after, f"pair_swap_wait_{tag}{l}")
            pair = [pair_sum(ids, a, b, f"pair_sum_{k}_{l}") for k, a, b in zip(names, halves, got)]
            rs[l, tag] = dict(exchange_start([(a, None, False) for a in pair], f"scatter_start_{tag}{l}", scatter=True), names=names)
            token = token + rs[l, tag]["token"][0, 0]
        return token

    lossp, grad_x, gsmall = local_step(x, loss_target, small, get_w, put_g, sync_g)

    def sum_group(l, tag, after):
        st = rs[l, tag]
        zones = exchange_wait(st, list(range(len(st["names"]))), after, f"scatter_wait_{tag}{l}")
        sums = [sum_partials(ids, zones[i], st["src"][i][0], f"sum_{k}_{l}") for i, k in enumerate(st["names"])]
        return sibling_start(sums, f"swap_sums_start_{tag}{l}")

    def update_group(l, tag, swap, after, prev):
        sums, others = sibling_wait(swap, after, f"swap_sums_wait_{tag}{l}")
        outs = dict(prev)
        for i, k in enumerate(rs[l, tag]["names"]):
            r2 = lambda a: a.reshape(-1, a.shape[-1])
            outs[k] = adamw(r2(w[k]), r2(mom[k]), r2(var[k]), (sums[i], others[i]), f"adamw_{k}_{l}", layer=l, prev=prev.get(k))
        return outs

    swap_r = sum_group(1, "rest", rs[0, "in"]["token"])
    swap_i = sum_group(1, "in", swap_r["token"])
    done = update_group(1, "rest", swap_r, swap_i["token"], {})
    done = update_group(1, "in", swap_i, done["w_ff2"][0], done)
    res = {}

    full_shapes = [(DEPTH,) + tuple(gsmall[0][k].shape) for k in SMALL]
    packed = _pack([jnp.stack([gsmall[l][k] for l in range(DEPTH)]) for k in SMALL] + [jnp.sum(lossp).reshape(1)])
    *totals, loss = _unpack(allreduce_small(packed), full_shapes + [(1,)])
    loss = loss[0]
    gfull = dict(zip(SMALL, totals))
    cs = 3 * DN_WIDTH // N_CHIPS
    gfull["conv_w"] = lax.dynamic_slice_in_dim(gfull["conv_w"], chip * cs, cs, axis=2)
    gp, wp, mp, vp = (_pack([d[k] for k in SMALL]) for d in (gfull, w, mom, var))
    outs = adamw(wp, mp, vp, (gp,), "adamw_small")
    loc_shapes = [w[k].shape for k in SMALL]
    unp = [_unpack(o, loc_shapes) for o in outs]
    for i, k in enumerate(SMALL):
        res[k] = [unp[j][i] for j in range(4)]

    swap_r = sum_group(0, "rest", outs[0])
    swap_i = sum_group(0, "in", swap_r["token"])
    done = update_group(0, "rest", swap_r, swap_i["token"], done)
    done = update_group(0, "in", swap_i, done["w_ff2"][0], done)
    for k in BIG:
        res[k] = [o.reshape(w[k].shape) for o in done[k]]

    return (loss, grad_x, *[res[k][0] for k in WEIGHTS], *[res[k][1] for k in WEIGHTS], *[res[k][2] for k in WEIGHTS],
            *[res[k][3] for k in WEIGHTS])
```

```python
import functools

import jax
import jax.numpy as jnp
from jax import lax
from jax.experimental import pallas as pl
from jax.experimental.pallas import tpu as pltpu

f32 = jnp.float32
bf16 = jnp.bfloat16
SDS = jax.ShapeDtypeStruct
MESH = pl.DeviceIdType.MESH

NORM_EPS = 1e-6
D_MODEL = 1024
DEPTH = 2
DN_HEADS, DN_DIM, DN_WIDTH, DN_CONV, DN_CHUNK = 4, 128, 512, 4, 64
SB_HEADS, SB_DIM, SB_WIDTH = 4, 64, 256
SG_GROUPS, SG_DIM, SG_WIDTH, SG_CHUNK = 4, 64, 256, 128
D_FF = 4096
IN_DIM = 3336
C_QKV, C_Z, C_AB, C_SB, C_SG, IN_PAD = 0, 1536, 2048, 2304, 3072, 3584
DN_COLS = C_SB
N_CHIPS = 4

ADAM_LR, ADAM_B1, ADAM_B2, ADAM_EPS, ADAM_WD, ADAM_STEP = 0.001, 0.9, 0.999, 1e-08, 0.01, 10

VMEM_LIMIT = 56 * 1024 * 1024


def _cp(sem=None, **kw):
    if sem is not None:
        kw["dimension_semantics"] = sem
    return pltpu.CompilerParams(vmem_limit_bytes=VMEM_LIMIT, **kw)


def _split2(x):
    hi = x.astype(bf16)
    lo = (x - hi.astype(f32)).astype(bf16)
    return hi, lo


NT = (((1,), (1,)), ((), ()))
TN = (((0,), (0,)), ((), ()))
_DIMS2 = dict(nn=(((1,), (0,)), ((), ())), nt=NT, tn=TN)
_DIMS3 = dict(nn=(((2,), (1,)), ((0,), (0,))), nt=(((2,), (2,)), ((0,), (0,))), tn=(((1,), (1,)), ((0,), (0,))))


def _dg(a, b, kind):
    return lax.dot_general(a, b, (_DIMS2 if a.ndim == 2 else _DIMS3)[kind], preferred_element_type=f32)


def _pdot(a, b):
    return _dg(a, b, "nn")


def _dot_hp(a, b):
    ah, al = _split2(a)
    bh, bl = _split2(b)
    return _pdot(ah, bh) + _pdot(ah, bl) + _pdot(al, bh)


def _dot_x2c(a, m):
    lead = a.shape[:-1]
    ah, al = _split2(a.reshape(-1, a.shape[-1]))
    return (_pdot(ah, m) + _pdot(al, m)).reshape(lead + (m.shape[1],))


def _dot_cx2(m, a):
    if a.ndim == 3:
        m = jnp.broadcast_to(m, (a.shape[0],) + m.shape)
    ah, al = _split2(a)
    return _pdot(m, ah) + _pdot(m, al)


def _nt(a, b):
    return _dg(a.astype(bf16), b.astype(bf16), "nt")


def _tn(a, b):
    return _dg(a.astype(bf16), b.astype(bf16), "tn")


def _nn(a, b):
    return _dg(a.astype(bf16), b.astype(bf16), "nn")


@jax.custom_vjp
def mm(a, b):
    return _nn(a, b)


mm.defvjp(lambda a, b: (_nn(a, b), (a, b)), lambda r, g: (_nt(g, r[1]), _tn(r[0], g)))


@jax.custom_vjp
def mm_nt(a, b):
    return _nt(a, b)


mm_nt.defvjp(lambda a, b: (_nt(a, b), (a, b)), lambda r, g: (_nn(g, r[1]), _tn(g, r[0])))


@jax.custom_vjp
def mm_tn(a, b):
    return _tn(a, b)


mm_tn.defvjp(lambda a, b: (_tn(a, b), (a, b)), lambda r, g: (_nt(r[1], g), _nn(r[0], g)))


@jax.custom_vjp
def rmul_const(a, m, mt):
    return _dot_x2c(a, m)


rmul_const.defvjp(lambda a, m, mt: (_dot_x2c(a, m), (m, mt)),
                  lambda r, g: (_dot_x2c(g, r[1]), jnp.zeros_like(r[0]), jnp.zeros_like(r[1])))


@jax.custom_vjp
def lmul_const(m, mt, a):
    return _dot_cx2(m, a)


lmul_const.defvjp(lambda m, mt, a: (_dot_cx2(m, a), (m, mt)),
                  lambda r, g: (jnp.zeros_like(r[0]), jnp.zeros_like(r[1]), _dot_cx2(r[1], g)))


@jax.custom_vjp
def mm_hl(t, x):
    th, tl = _split2(t)
    xb = x.astype(bf16)
    return _pdot(th, xb) + _pdot(tl, xb)


def _mm_hl_bwd(r, g):
    t, x = r
    th, tl = _split2(t)
    gb = g.astype(bf16)
    return _nt(g, x), _dg(th, gb, "tn") + _dg(tl, gb, "tn")


mm_hl.defvjp(lambda t, x: (mm_hl(t, x), (t, x)), _mm_hl_bwd)


def inv_unit_lower(lm):
    c = lm.shape[-1]
    r, cc = _iota2((c, c))
    eye = (r == cc).astype(f32)
    t = eye - lm
    p = -lm
    k = 1
    while 2 * k < c:
        p = _nn(p, p)
        t = t + _nn(t, p)
        k *= 2
    res = eye - t - _dot_hp(lm, t)
    return t + _nn(t, res)


@jax.custom_vjp
def inv_given(lm, t):
    return t


inv_given.defvjp(lambda lm, t: (t, t), lambda t, g: (-_nt(_tn(t, g), t), jnp.zeros_like(t)))


def _sigmoid(x):
    return 1.0 / (1.0 + jnp.exp(-x))


def _softplus(x):
    return jnp.maximum(x, 0.0) + jnp.log(1.0 + jnp.exp(-jnp.abs(x)))


def _silu(x):
    return x * _sigmoid(x)


def _gelu(x):
    return 0.5 * x * (1.0 + jnp.tanh(0.7978845608028654 * (x + 0.044715 * (x * x * x))))


def _iota2(shape):
    return lax.broadcasted_iota(jnp.int32, shape, 0), lax.broadcasted_iota(jnp.int32, shape, 1)


def _group_avg_mats():
    r, c = _iota2((128, 128))
    return jnp.where((r // 64) == (c // 64), 1.0 / 64.0, 0.0).astype(bf16)


def _pair_norm(x, gain, bavg):
    ms = rmul_const(x * x, bavg, bavg)
    return x * lax.rsqrt(ms + NORM_EPS) * gain


def _rms(x):
    r = lax.rsqrt(jnp.mean(x * x, axis=-1, keepdims=True) + NORM_EPS)
    return r


_IN_GROUPS = ((C_QKV, C_Z), (C_Z, C_AB), (C_AB, C_AB + 128), (C_SB, C_SG), (C_SG, IN_PAD))


def inproj_fwd(x, g, wp, tm=256):
    m = x.shape[0]

    def body(x_ref, g_ref, w_ref, *outs):
        xv = x_ref[...]
        h = (xv * _rms(xv) * g_ref[...]).astype(bf16)
        outs[-1][...] = h
        for (a, b), o in zip(_IN_GROUPS, outs):
            o[...] = _pdot(h, w_ref[:, a:b])

    widths = [b - a for a, b in _IN_GROUPS]
    return pl.pallas_call(
        body, name="inproj_fwd", grid=(m // tm,),
        in_specs=[pl.BlockSpec((tm, D_MODEL), lambda i: (i, 0)), pl.BlockSpec((1, D_MODEL), lambda i: (0, 0)),
                  pl.BlockSpec((D_MODEL, IN_PAD), lambda i: (0, 0))],
        out_specs=[pl.BlockSpec((tm, wd), lambda i: (i, 0)) for wd in widths + [D_MODEL]],
        out_shape=[SDS((m, wd), f32) for wd in widths] + [SDS((m, D_MODEL), bf16)],
        compiler_params=_cp(("arbitrary",)),
    )(x, g, wp)


def inproj_bwd(x, g, wp, dproj, dres, tm=256):
    m = x.shape[0]

    def body(x_ref, g_ref, w_ref, dp_ref, dr_ref, dx_ref, dg_ref):
        xv = x_ref[...]
        r = _rms(xv)
        xn = xv * r
        gv = g_ref[...]
        dh = lax.dot_general(dp_ref[...], w_ref[...], NT, preferred_element_type=f32)
        dxn = dh * gv
        dx_ref[...] = dr_ref[...] + r * (dxn - xn * jnp.mean(dxn * xn, axis=-1, keepdims=True))

        @pl.when(pl.program_id(0) == 0)
        def _():
            dg_ref[...] = jnp.zeros_like(dg_ref)

        dg_ref[...] += jnp.sum(dh * xn, axis=0, keepdims=True)

    return pl.pallas_call(
        body, name="inproj_bwd", grid=(m // tm,),
        in_specs=[pl.BlockSpec((tm, D_MODEL), lambda i: (i, 0)), pl.BlockSpec((1, D_MODEL), lambda i: (0, 0)),
                  pl.BlockSpec((D_MODEL, IN_PAD), lambda i: (0, 0)), pl.BlockSpec((tm, IN_PAD), lambda i: (i, 0)),
                  pl.BlockSpec((tm, D_MODEL), lambda i: (i, 0))],
        out_specs=[pl.BlockSpec((tm, D_MODEL), lambda i: (i, 0)), pl.BlockSpec((1, D_MODEL), lambda i: (0, 0))],
        out_shape=[SDS((m, D_MODEL), f32), SDS((1, D_MODEL), f32)],
        compiler_params=_cp(("arbitrary",)),
    )(x, g, wp, dproj, dres)


def outproj_fwd(x, odn, osb, osg, wo, tm=512):
    m = x.shape[0]

    def body(x_ref, a_ref, b_ref, c_ref, w_ref, x2_ref, mix_ref):
        mix_ref[:, 0:DN_WIDTH] = a_ref[...].astype(bf16)
        mix_ref[:, DN_WIDTH:DN_WIDTH + SB_WIDTH] = b_ref[...].astype(bf16)
        mix_ref[:, DN_WIDTH + SB_WIDTH:D_MODEL] = c_ref[...].astype(bf16)
        x2_ref[...] = x_ref[...] + _pdot(mix_ref[...], w_ref[...])

    row = lambda w: pl.BlockSpec((tm, w), lambda i: (i, 0))
    return pl.pallas_call(
        body, name="outproj_fwd", grid=(m // tm,),
        in_specs=[row(D_MODEL), row(DN_WIDTH), row(SB_WIDTH), row(SG_WIDTH), pl.BlockSpec((D_MODEL, D_MODEL), lambda i: (0, 0))],
        out_specs=[row(D_MODEL), row(D_MODEL)],
        out_shape=[SDS((m, D_MODEL), f32), SDS((m, D_MODEL), bf16)],
        compiler_params=_cp(("arbitrary",)),
    )(x, odn, osb, osg, wo)


def outproj_bwd(dx2, wo, tm=512):
    m = dx2.shape[0]

    def body(d_ref, w_ref, a_ref, b_ref, c_ref, db_ref):
        db = d_ref[...].astype(bf16)
        db_ref[...] = db
        dm = lax.dot_general(db, w_ref[...], NT, preferred_element_type=f32)
        a_ref[...] = dm[:, 0:DN_WIDTH]
        b_ref[...] = dm[:, DN_WIDTH:DN_WIDTH + SB_WIDTH]
        c_ref[...] = dm[:, DN_WIDTH + SB_WIDTH:D_MODEL]

    row = lambda w: pl.BlockSpec((tm, w), lambda i: (i, 0))
    return pl.pallas_call(
        body, name="outproj_bwd", grid=(m // tm,),
        in_specs=[row(D_MODEL), pl.BlockSpec((D_MODEL, D_MODEL), lambda i: (0, 0))],
        out_specs=[row(DN_WIDTH), row(SB_WIDTH), row(SG_WIDTH), row(D_MODEL)],
        out_shape=[SDS((m, DN_WIDTH), f32), SDS((m, SB_WIDTH), f32), SDS((m, SG_WIDTH), f32), SDS((m, D_MODEL), bf16)],
        compiler_params=_cp(("arbitrary",)),
    )(dx2, wo)


FF_CHUNK = D_FF // N_CHIPS


def _load_weights_once(pairs, sem):
    @pl.when(pl.program_id(0) == 0)
    def _():
        cps = [pltpu.make_async_copy(h, v, sem.at[i]) for i, (h, v) in enumerate(pairs)]
        for c in cps:
            c.start()
        for c in cps:
            c.wait()


def ffn_fwd(x2, g, w1, w2, tgt=None, tm=256):
    m = x2.shape[0]
    head = tgt is not None

    def body(x_ref, g_ref, w1_hbm, w2_hbm, *rest):
        (y_ref, rl_ref), (w1_v, w2_v, sem) = rest[head:head + 2], rest[-3:]
        _load_weights_once(((w1_hbm, w1_v), (w2_hbm, w2_v)), sem)
        xv = x_ref[...]
        h = (xv * _rms(xv) * g_ref[...]).astype(bf16)
        acc = xv
        for j in range(0, D_FF, FF_CHUNK):
            f = _pdot(h, w1_v[j // FF_CHUNK])
            rl = jnp.maximum(f, 0.0)
            rl_ref[:, j:j + FF_CHUNK] = rl.astype(bf16)
            acc = acc + _pdot((rl * rl).astype(bf16), w2_v[j:j + FF_CHUNK, :])
        if not head:
            y_ref[...] = acc
            return
        t_ref, l_ref = rest[0], rest[3]
        e = acc - t_ref[...]
        y_ref[...] = e * (1.0 / D_MODEL)

        @pl.when(pl.program_id(0) == 0)
        def _():
            l_ref[...] = jnp.zeros_like(l_ref)

        l_ref[...] += jnp.sum(e * e, axis=0, keepdims=True) * (0.5 / D_MODEL)

    row = pl.BlockSpec((tm, D_MODEL), lambda i: (i, 0))
    return pl.pallas_call(
        body, name="ffn_fwd_loss" if head else "ffn_fwd", grid=(m // tm,),
        in_specs=[row, pl.BlockSpec((1, D_MODEL), lambda i: (0, 0)), pl.BlockSpec(memory_space=pl.ANY),
                  pl.BlockSpec(memory_space=pl.ANY)] + [row] * head,
        out_specs=[row, pl.BlockSpec((tm, D_FF), lambda i: (i, 0))] + [pl.BlockSpec((1, D_MODEL), lambda i: (0, 0))] * head,
        out_shape=[SDS((m, D_MODEL), f32), SDS((m, D_FF), bf16)] + [SDS((1, D_MODEL), f32)] * head,
        scratch_shapes=[pltpu.VMEM((N_CHIPS, D_MODEL, FF_CHUNK), bf16), pltpu.VMEM((D_FF, D_MODEL), bf16), pltpu.SemaphoreType.DMA((2,))],
        compiler_params=_cp(("arbitrary",)),
    )(x2, g, w1, w2, *([tgt] if head else []))


def ffn_bwd(x2, g, w1, w2, rlb, dy, tm=256):
    m = x2.shape[0]

    def body(x_ref, g_ref, w1_hbm, w2_hbm, rl_ref, dy_ref, dx_ref, dg_ref, h_ref, a_ref, df_ref, dyb_ref, w1_v, w2_v, sem):
        _load_weights_once(((w1_hbm, w1_v), (w2_hbm, w2_v)), sem)
        xv = x_ref[...]
        r = _rms(xv)
        xn = xv * r
        gv = g_ref[...]
        h = (xn * gv).astype(bf16)
        h_ref[...] = h
        dyv = dy_ref[...]
        dyb = dyv.astype(bf16)
        dyb_ref[...] = dyb
        dh = jnp.zeros((tm, D_MODEL), f32)
        for j in range(0, D_FF, FF_CHUNK):
            rl = rl_ref[:, j:j + FF_CHUNK].astype(f32)
            a_ref[:, j:j + FF_CHUNK] = (rl * rl).astype(bf16)
            da = lax.dot_general(dyb, w2_v[j:j + FF_CHUNK, :], NT, preferred_element_type=f32)
            df = (da * (2.0 * rl)).astype(bf16)
            df_ref[:, j:j + FF_CHUNK] = df
            dh = dh + lax.dot_general(df, w1_v[j // FF_CHUNK], NT, preferred_element_type=f32)
        dxn = dh * gv
        dx_ref[...] = dyv + r * (dxn - xn * jnp.mean(dxn * xn, axis=-1, keepdims=True))

        @pl.when(pl.program_id(0) == 0)
        def _():
            dg_ref[...] = jnp.zeros_like(dg_ref)

        dg_ref[...] += jnp.sum(dh * xn, axis=0, keepdims=True)

    row = lambda w: pl.BlockSpec((tm, w), lambda i: (i, 0))
    return pl.pallas_call(
        body, name="ffn_bwd", grid=(m // tm,),
        in_specs=[row(D_MODEL), pl.BlockSpec((1, D_MODEL), lambda i: (0, 0)),
                  pl.BlockSpec(memory_space=pl.ANY), pl.BlockSpec(memory_space=pl.ANY), row(D_FF), row(D_MODEL)],
        out_specs=[row(D_MODEL), pl.BlockSpec((1, D_MODEL), lambda i: (0, 0)), row(D_MODEL), row(D_FF), row(D_FF), row(D_MODEL)],
        out_shape=[SDS((m, D_MODEL), f32), SDS((1, D_MODEL), f32), SDS((m, D_MODEL), bf16), SDS((m, D_FF), bf16),
                   SDS((m, D_FF), bf16), SDS((m, D_MODEL), bf16)],
        scratch_shapes=[pltpu.VMEM((N_CHIPS, D_MODEL, FF_CHUNK), bf16), pltpu.VMEM((D_FF, D_MODEL), bf16), pltpu.SemaphoreType.DMA((2,))],
        compiler_params=_cp(("arbitrary",)),
    )(x2, g, w1, w2, rlb, dy)


def _tile(n, cap):
    best = 128
    for t in range(128, cap + 1, 128):
        if n % t == 0:
            best = t
    return best


def tn_matmul(a, b, name, col_shards=1, tk=2048):
    m, ka = a.shape
    n = b.shape[1]
    ti = _tile(ka, 1024)
    tj = _tile(n // col_shards, 1152)
    tk = min(tk, m)
    nk = m // tk
    jps = (n // col_shards) // tj

    def body(a_ref, b_ref, o_ref, acc):
        k = pl.program_id(2)

        @pl.when(k == 0)
        def _():
            acc[...] = jnp.zeros_like(acc)

        acc[...] += lax.dot_general(a_ref[...], b_ref[...], TN, preferred_element_type=f32)

        @pl.when(k == nk - 1)
        def _():
            o_ref[...] = acc[...].astype(bf16).reshape(o_ref.shape)

    if col_shards == 1:
        out_shape, out_spec = SDS((ka, n), bf16), pl.BlockSpec((ti, tj), lambda i, j, k: (i, j))
    else:
        out_shape = SDS((col_shards, ka, n // col_shards), bf16)
        out_spec = pl.BlockSpec((1, ti, tj), lambda i, j, k: (j // jps, i, j % jps))
    return pl.pallas_call(
        body, name=name, grid=(ka // ti, n // tj, nk),
        in_specs=[pl.BlockSpec((tk, ti), lambda i, j, k: (k, i)), pl.BlockSpec((tk, tj), lambda i, j, k: (k, j))],
        out_specs=out_spec, out_shape=out_shape,
        scratch_shapes=[pltpu.VMEM((ti, tj), f32)],
        compiler_params=_cp(("arbitrary", "arbitrary", "arbitrary")),
    )(a, b)


def _dn_consts():
    c = DN_CHUNK
    r, cc = _iota2((c, c))
    lt = (cc <= r).astype(bf16)
    ltt = (r <= cc).astype(bf16)
    return lt, ltt


def dn_chunk(cq, ck, cv, g, beta, z, s, gain, lt, ltt, t_given=None):
    c = DN_CHUNK
    r, cc = _iota2((c, c))
    q = cq * lax.rsqrt(jnp.sum(cq * cq, axis=-1, keepdims=True) + NORM_EPS) * (DN_DIM ** -0.5)
    k = ck * lax.rsqrt(jnp.sum(ck * ck, axis=-1, keepdims=True) + NORM_EPS)
    r2, c2 = _iota2((c, 128))
    uaug = jnp.where((c2 < c) & (r2 > c2), 1.0, 0.0) + jnp.where(c2 == c, 1.0, 0.0)
    gam_all = lmul_const(lt, ltt, g * uaug)
    gam_cc = gam_all[:, :, 0:c]
    gam = gam_all[:, :, c:c + 1]
    dec = jnp.where(cc <= r, jnp.exp(jnp.where(cc <= r, gam_cc, 0.0)), 0.0)
    kk = mm_nt(k, k)
    lm = jnp.where(cc < r, beta * kk * dec, 0.0)
    t = inv_unit_lower(lm) if t_given is None else inv_given(lm, t_given)
    eg = jnp.exp(gam)
    sol = mm_hl(t, jnp.concatenate([cv * beta, k * (beta * eg)], axis=2))
    u, w = sol[:, :, 0:DN_DIM], sol[:, :, DN_DIM:2 * DN_DIM]
    qk = jnp.where(cc <= r, mm_nt(q, k) * dec, 0.0)
    glast = jnp.sum(g, axis=1, keepdims=True)
    qd = q * eg
    kd = k * jnp.exp(glast - gam)
    un = u - mm(w, s)
    o = mm(qd, s) + mm(qk, un)
    s_new = s * jnp.exp(glast) + mm_tn(kd, un)
    on = o * lax.rsqrt(jnp.mean(o * o, axis=-1, keepdims=True) + NORM_EPS) * gain * _silu(z)
    return on, s_new, t


def _dn_gates(ab, al_row, dt_row):
    pre = ab + dt_row
    return -jnp.exp(al_row) * _softplus(pre), _sigmoid(ab), _sigmoid(pre)


def _dn_chains(cacts, gates, z_ref):
    cq, ck, cv, g, beta, z = [], [], [], [], [], []
    for bi, cact in enumerate(cacts):
        for h in range(DN_HEADS):
            cq.append(cact[:, h * DN_DIM:(h + 1) * DN_DIM])
            ck.append(cact[:, DN_WIDTH + h * DN_DIM:DN_WIDTH + (h + 1) * DN_DIM])
            cv.append(cact[:, 2 * DN_WIDTH + h * DN_DIM:2 * DN_WIDTH + (h + 1) * DN_DIM])
            g.append(gates[bi][0][:, h:h + 1])
            beta.append(gates[bi][1][:, DN_HEADS + h:DN_HEADS + h + 1])
            z.append(z_ref[bi, :, h * DN_DIM:(h + 1) * DN_DIM])
    return tuple(jnp.stack(v) for v in (cq, ck, cv, g, beta, z))


def _conv_rows(xe_ref, b, w_ref):
    y = w_ref[0:1, :] * xe_ref[b, pl.ds(5, DN_CHUNK), :]
    for i in range(1, DN_CONV):
        y = y + w_ref[i:i + 1, :] * xe_ref[b, pl.ds(5 + i, DN_CHUNK), :]
    return y


def dn_fwd(qkv, z, ab, conv_w, alog, dtb, gain):
    bsz, t, _ = qkv.shape
    nc = t // DN_CHUNK
    c = DN_CHUNK
    nh = bsz * DN_HEADS

    def body(qkv_ref, z_ref, ab_ref, w_ref, al_ref, dt_ref, g_ref, o_ref, sall_ref, tall_ref, xe, s_sc):
        n = pl.program_id(0)

        @pl.when(n == 0)
        def _():
            xe[:, 0:8, :] = jnp.zeros((bsz, 8, 3 * DN_WIDTH), f32)
            s_sc[...] = jnp.zeros_like(s_sc)

        lt, ltt = _dn_consts()
        cacts = []
        for b in range(bsz):
            xe[b, 8:8 + c, :] = qkv_ref[b]
            cacts.append(_silu(_conv_rows(xe, b, w_ref)))
            xe[b, 0:8, :] = xe[b, c:c + 8, :]
        gates = [_dn_gates(ab_ref[b], al_ref[...], dt_ref[...]) for b in range(bsz)]
        s = s_sc[...]
        sall_ref[0] = s
        on, sn, tt = dn_chunk(*_dn_chains(cacts, gates, z_ref), s, g_ref[...], lt, ltt)
        tall_ref[0] = tt
        s_sc[...] = sn
        for b in range(bsz):
            for h in range(DN_HEADS):
                o_ref[b, :, h * DN_DIM:(h + 1) * DN_DIM] = on[b * DN_HEADS + h]

    blk = lambda w: pl.BlockSpec((bsz, c, w), lambda n: (0, n, 0))
    full = lambda shp: pl.BlockSpec(shp, lambda n: (0,) * len(shp))
    return pl.pallas_call(
        body, name="dn_fwd", grid=(nc,),
        in_specs=[blk(3 * DN_WIDTH), blk(DN_WIDTH), blk(128), full((8, 3 * DN_WIDTH)), full((1, 128)), full((1, 128)), full((1, 128))],
        out_specs=[blk(DN_WIDTH), pl.BlockSpec((1, nh, DN_DIM, DN_DIM), lambda n: (n, 0, 0, 0)),
                   pl.BlockSpec((1, nh, c, c), lambda n: (n, 0, 0, 0))],
        out_shape=[SDS((bsz, t, DN_WIDTH), f32), SDS((nc, nh, DN_DIM, DN_DIM), f32), SDS((nc, nh, c, c), f32)],
        scratch_shapes=[pltpu.VMEM((bsz, c + 8, 3 * DN_WIDTH), f32), pltpu.VMEM((nh, DN_DIM, DN_DIM), f32)],
        compiler_params=_cp(("arbitrary",)),
    )(qkv, z, ab, conv_w, alog, dtb, gain)


def dn_bwd(qkv, z, ab, conv_w, alog, dtb, gain, sall, tall, do):
    bsz, t, _ = qkv.shape
    nc = t // DN_CHUNK
    c = DN_CHUNK
    nh = bsz * DN_HEADS
    w3 = 3 * DN_WIDTH

    def body(qkv_ref, prev_ref, z_ref, ab_ref, w_ref, al_ref, dt_ref, g_ref, sall_ref, tall_ref, do_ref,
             dp_ref, dw_ref, dal_ref, ddt_ref, dg_ref, xe, dye, dc_sc, ds_sc):
        n = pl.program_id(0)
        first = (nc - 1 - n) == 0

        @pl.when(n == 0)
        def _():
            dye[:, c:c + 8, :] = jnp.zeros((bsz, 8, w3), f32)
            ds_sc[...] = jnp.zeros_like(ds_sc)
            dw_ref[...] = jnp.zeros_like(dw_ref)
            dal_ref[...] = jnp.zeros_like(dal_ref)
            ddt_ref[...] = jnp.zeros_like(ddt_ref)
            dg_ref[...] = jnp.zeros_like(dg_ref)

        lt, ltt = _dn_consts()
        lane_c = lax.broadcasted_iota(jnp.int32, (c, 128), 1)
        ys, sigs = [], []
        for b in range(bsz):
            xe[b, 0:8, :] = jnp.where(first, 0.0, prev_ref[b])
            xe[b, 8:8 + c, :] = qkv_ref[b]
            ys.append(_conv_rows(xe, b, w_ref))
            sigs.append(_sigmoid(ys[b]))
        gates = [_dn_gates(ab_ref[b], al_ref[...], dt_ref[...]) for b in range(bsz)]
        ops = _dn_chains([y * sg for y, sg in zip(ys, sigs)], gates, z_ref)
        tt = tall_ref[0]
        _, vjp = jax.vjp(lambda *p: dn_chunk(*p, lt, ltt, t_given=tt)[0:2], *ops, sall_ref[0], g_ref[...])
        don = jnp.stack([do_ref[b, :, h * DN_DIM:(h + 1) * DN_DIM] for b in range(bsz) for h in range(DN_HEADS)])
        dcq, dck, dcv, dg, dbeta, dzz, dsp, dgn = vjp((don, ds_sc[...]))
        ds_sc[...] = dsp
        dg_ref[...] += dgn
        for b in range(bsz):
            dgate = jnp.zeros((c, 128), f32)
            for h in range(DN_HEADS):
                i = b * DN_HEADS + h
                dc_sc[b, :, h * DN_DIM:(h + 1) * DN_DIM] = dcq[i]
                dc_sc[b, :, DN_WIDTH + h * DN_DIM:DN_WIDTH + (h + 1) * DN_DIM] = dck[i]
                dc_sc[b, :, 2 * DN_WIDTH + h * DN_DIM:2 * DN_WIDTH + (h + 1) * DN_DIM] = dcv[i]
                dp_ref[b, :, C_Z + h * DN_DIM:C_Z + (h + 1) * DN_DIM] = dzz[i].astype(bf16)
                dgate = dgate + jnp.where(lane_c == h, dg[i], 0.0) + jnp.where(lane_c == DN_HEADS + h, dbeta[i], 0.0)
            gg, beta, sig_pre = gates[b]
            is_g = lane_c < DN_HEADS
            dpre = jnp.where(is_g, dgate * (-jnp.exp(al_ref[...])) * sig_pre, 0.0)
            dp_ref[b, :, C_AB:C_AB + 128] = (dpre + jnp.where(is_g, 0.0, dgate * beta * (1.0 - beta))).astype(bf16)
            dp_ref[b, :, C_AB + 128:DN_COLS] = jnp.zeros((c, DN_COLS - C_AB - 128), bf16)
            dal_ref[...] += jnp.sum(jnp.where(is_g, dgate * gg, 0.0), axis=0, keepdims=True)
            ddt_ref[...] += jnp.sum(dpre, axis=0, keepdims=True)
            y, sig = ys[b], sigs[b]
            dy = dc_sc[b] * (sig * (1.0 + y * (1.0 - sig)))
            dye[b, 0:c, :] = dy
            dx = w_ref[3:4, :] * dy
            for i in range(DN_CONV - 1):
                dx = dx + w_ref[i:i + 1, :] * dye[b, pl.ds(3 - i, c), :]
            dp_ref[b, :, 0:w3] = dx.astype(bf16)
            for i in range(DN_CONV):
                dw_ref[i:i + 1, :] += jnp.sum(dy * xe[b, pl.ds(5 + i, c), :], axis=0, keepdims=True)
            dye[b, c:c + 8, :] = dye[b, 0:8, :]

    rev = lambda w: pl.BlockSpec((bsz, c, w), lambda n: (0, nc - 1 - n, 0))
    full = lambda shp: pl.BlockSpec(shp, lambda n: (0,) * len(shp))
    prev = pl.BlockSpec((bsz, 8, w3), lambda n: (0, jnp.maximum((nc - 1 - n) * (c // 8) - 1, 0), 0))
    return pl.pallas_call(
        body, name="dn_bwd", grid=(nc,),
        in_specs=[rev(w3), prev, rev(DN_WIDTH), rev(128), full((8, w3)), full((1, 128)), full((1, 128)), full((1, 128)),
                  pl.BlockSpec((1, nh, DN_DIM, DN_DIM), lambda n: (nc - 1 - n, 0, 0, 0)),
                  pl.BlockSpec((1, nh, c, c), lambda n: (nc - 1 - n, 0, 0, 0)), rev(DN_WIDTH)],
        out_specs=[rev(DN_COLS), full((8, w3)), full((1, 128)), full((1, 128)), full((1, 128))],
        out_shape=[SDS((bsz, t, IN_PAD), bf16), SDS((8, w3), f32), SDS((1, 128), f32), SDS((1, 128), f32), SDS((1, 128), f32)],
        scratch_shapes=[pltpu.VMEM((bsz, c + 8, w3), f32), pltpu.VMEM((bsz, c + 8, w3), f32), pltpu.VMEM((bsz, c, w3), f32),
                        pltpu.VMEM((nh, DN_DIM, DN_DIM), f32)],
        compiler_params=_cp(("arbitrary",)),
    )(qkv, qkv, z, ab, conv_w, alog, dtb, gain, sall, tall, do)


SB_TILE = 256
SB_QTILE, SB_KTILE = 256, 256
SB_PAIRS = SB_HEADS // 2


def sb_fwd(sbqkv, gq, gk):
    bsz, t, _ = sbqkv.shape
    bq = min(SB_QTILE, t)
    blk = max(min(SB_KTILE, t), bq)
    nq = t // bq
    scale = SB_DIM ** -0.5

    def body(q_ref, k_ref, v_ref, gq_ref, gk_ref, o_ref, l_ref, q2_sc, kn_sc, v_sc):
        bavg = _group_avg_mats()
        lane = lax.broadcasted_iota(jnp.int32, (1, 128), 1)
        first = lane < SB_DIM
        for p in range(SB_PAIRS):
            ls = slice(p * 128, (p + 1) * 128)
            qn = _pair_norm(q_ref[0, :, ls], gq_ref[...], bavg)
            kn_sc[p] = _pair_norm(k_ref[0, :, ls], gk_ref[...], bavg).astype(bf16)
            v_sc[p] = v_ref[0, :, ls].astype(bf16)
            q2_sc[2 * p] = jnp.where(first, qn, 0.0).astype(bf16)
            q2_sc[2 * p + 1] = jnp.where(first, 0.0, qn).astype(bf16)
        r, c = _iota2((blk, blk))
        ustrict = (r > c).astype(bf16)
        r2, c2 = _iota2((2 * bq, blk))

        def tile(q2s, ks, carry, causal):
            out = []
            for p in range(SB_PAIRS):
                acc, rr = carry[2 * p], carry[2 * p + 1]
                zz = lax.dot_general(q2s[p], kn_sc[p, pl.ds(ks, blk), :], NT, preferred_element_type=f32) * scale
                sp = _softplus(zz)
                lm = -sp if causal is None else jnp.where(causal, -sp, 0.0)
                rem = _dot_x2c(lm, ustrict)
                wgt = jnp.exp(zz - sp + rem + rr)
                if causal is not None:
                    wgt = jnp.where(causal, wgt, 0.0)
                out += [acc + _pdot(wgt.astype(bf16), v_sc[p, pl.ds(ks, blk), :]), rr + jnp.sum(lm, axis=1, keepdims=True)]
            return tuple(out)

        def qloop(qi, _):
            qs = pl.multiple_of(qi * bq, bq)
            kd = qs // blk
            causal = c2 < (r2 & (bq - 1)) + (qs - kd * blk)
            q2s = [jnp.concatenate([q2_sc[2 * p, pl.ds(qs, bq), :], q2_sc[2 * p + 1, pl.ds(qs, bq), :]], axis=0)
                   for p in range(SB_PAIRS)]
            zero = (jnp.zeros((2 * bq, 128), f32), jnp.zeros((2 * bq, 1), f32)) * SB_PAIRS
            carry = lax.fori_loop(1, kd + 1, lambda i, cr: tile(q2s, pl.multiple_of((kd - i) * blk, blk), cr, None),
                                  tile(q2s, pl.multiple_of(kd * blk, blk), zero, causal))
            for p in range(SB_PAIRS):
                acc, rr = carry[2 * p], carry[2 * p + 1]
                o_ref[0, pl.ds(qs, bq), p * 128:(p + 1) * 128] = jnp.where(first, acc[0:bq], acc[bq:2 * bq])
                l_ref[0, pl.ds(qs, bq), p * 128:(p + 1) * 128] = jnp.where(first, rr[0:bq], rr[bq:2 * bq])
            return 0

        lax.fori_loop(0, nq, qloop, 0)

    col = lambda off: pl.BlockSpec((1, t, SB_WIDTH), lambda b: (b, 0, off))
    gsp = pl.BlockSpec((1, 128), lambda b: (0, 0))
    return pl.pallas_call(
        body, name="sb_fwd", grid=(bsz,),
        in_specs=[col(0), col(1), col(2), gsp, gsp],
        out_specs=[col(0), col(0)],
        out_shape=[SDS((bsz, t, SB_WIDTH), f32), SDS((bsz, t, SB_WIDTH), f32)],
        scratch_shapes=[pltpu.VMEM((2 * SB_PAIRS, t, 128), bf16), pltpu.VMEM((SB_PAIRS, t, 128), bf16),
                        pltpu.VMEM((SB_PAIRS, t, 128), bf16)],
        compiler_params=_cp(("arbitrary",)),
    )(sbqkv, sbqkv, sbqkv, gq, gk)


def sb_bwd(sbqkv, gq, gk, ltot, do, dproj):
    bsz, t, _ = sbqkv.shape
    blk = min(SB_TILE, t)
    nq = t // blk
    scale = SB_DIM ** -0.5

    def body(q_ref, k_ref, v_ref, gq_ref, gk_ref, l_ref, do_ref, dp_in, dp_ref, dgq_ref, dgk_ref,
             q2_sc, kn_sc, v_sc, do2_sc, dqn_sc, dkn_sc, dv_sc):
        bavg = _group_avg_mats()
        lane = lax.broadcasted_iota(jnp.int32, (1, 128), 1)
        first = lane < SB_DIM
        fq = lambda x, g: _pair_norm(x, g, bavg)
        vjps = []
        for p in range(SB_PAIRS):
            ls = slice(p * 128, (p + 1) * 128)
            qn, q_vjp = jax.vjp(fq, q_ref[0, :, ls], gq_ref[...])
            kn, k_vjp = jax.vjp(fq, k_ref[0, :, ls], gk_ref[...])
            vjps.append((q_vjp, k_vjp))
            kn_sc[p] = kn.astype(bf16)
            v_sc[p] = v_ref[0, :, ls].astype(bf16)
            dov = do_ref[0, :, ls]
            q2_sc[2 * p] = jnp.where(first, qn, 0.0).astype(bf16)
            q2_sc[2 * p + 1] = jnp.where(first, 0.0, qn).astype(bf16)
            do2_sc[2 * p] = jnp.where(first, dov, 0.0).astype(bf16)
            do2_sc[2 * p + 1] = jnp.where(first, 0.0, dov).astype(bf16)
        dkn_sc[...] = jnp.zeros_like(dkn_sc)
        dv_sc[...] = jnp.zeros_like(dv_sc)
        r, c = _iota2((blk, blk))
        pincl = (r <= c).astype(bf16)
        pstrict = (r < c).astype(bf16)
        r2, c2 = _iota2((2 * blk, blk))
        causal = c2 < (r2 & (blk - 1))

        def tile(q2s, do2s, lts, ks, carry, diag):
            out = []
            for p in range(SB_PAIRS):
                dq, cs, ce = carry[3 * p:3 * p + 3]
                q2, do2 = q2s[p], do2s[p]
                kb = kn_sc[p, pl.ds(ks, blk), :]
                zz = lax.dot_general(q2, kb, NT, preferred_element_type=f32) * scale
                sp = _softplus(zz)
                lm = jnp.where(causal, -sp, 0.0) if diag else -sp
                pre = _dot_x2c(lm, pincl)
                lp = zz - sp
                wgt = jnp.exp(lp + (lts[p] - cs - pre))
                if diag:
                    wgt = jnp.where(causal, wgt, 0.0)
                dw = lax.dot_general(do2, v_sc[p, pl.ds(ks, blk), :], NT, preferred_element_type=f32)
                e = wgt * dw
                ee = ce + _dot_x2c(e, pstrict)
                sig = jnp.exp(lp)
                dz = (e * (1.0 - sig) - ee * sig) * scale
                if diag:
                    dz = jnp.where(causal, dz, 0.0)
                dz = dz.astype(bf16)
                dkn_sc[p, pl.ds(ks, blk), :] += lax.dot_general(dz, q2, TN, preferred_element_type=f32)
                dv_sc[p, pl.ds(ks, blk), :] += lax.dot_general(wgt.astype(bf16), do2, TN, preferred_element_type=f32)
                out += [dq + _pdot(dz, kb), cs + jnp.sum(lm, axis=1, keepdims=True), ce + jnp.sum(e, axis=1, keepdims=True)]
            return tuple(out)

        def qloop(qi, _):
            qs = pl.multiple_of(qi * blk, blk)
            rows = pl.ds(qs, blk)
            q2s = [jnp.concatenate([q2_sc[2 * p, rows, :], q2_sc[2 * p + 1, rows, :]], axis=0) for p in range(SB_PAIRS)]
            do2s = [jnp.concatenate([do2_sc[2 * p, rows, :], do2_sc[2 * p + 1, rows, :]], axis=0) for p in range(SB_PAIRS)]
            lts = [jnp.concatenate([l_ref[0, rows, p * 128:p * 128 + 1], l_ref[0, rows, p * 128 + SB_DIM:p * 128 + SB_DIM + 1]],
                                   axis=0) for p in range(SB_PAIRS)]
            z1 = jnp.zeros((2 * blk, 1), f32)
            carry = lax.fori_loop(0, qi, lambda kj, cr: tile(q2s, do2s, lts, pl.multiple_of(kj * blk, blk), cr, False),
                                  (jnp.zeros((2 * blk, 128), f32), z1, z1) * SB_PAIRS)
            carry = tile(q2s, do2s, lts, qs, carry, True)
            for p in range(SB_PAIRS):
                dq = carry[3 * p]
                dqn_sc[p, rows, :] = jnp.where(first, dq[0:blk], dq[blk:2 * blk])
            return 0

        lax.fori_loop(0, nq, qloop, 0)
        dgq_tot, dgk_tot = jnp.zeros((1, 128), f32), jnp.zeros((1, 128), f32)
        for p in range(SB_PAIRS):
            ls = slice(p * 128, (p + 1) * 128)
            dq_pre, dgq = vjps[p][0](dqn_sc[p])
            dk_pre, dgk = vjps[p][1](dkn_sc[p])
            dp_ref[0, :, p * 128:(p + 1) * 128] = dq_pre.astype(bf16)
            dp_ref[0, :, SB_WIDTH + p * 128:SB_WIDTH + (p + 1) * 128] = dk_pre.astype(bf16)
            dp_ref[0, :, 2 * SB_WIDTH + p * 128:2 * SB_WIDTH + (p + 1) * 128] = dv_sc[p].astype(bf16)
            dgq_tot, dgk_tot = dgq_tot + dgq, dgk_tot + dgk
        dgq_ref[0] = jnp.broadcast_to(dgq_tot, (8, 128))
        dgk_ref[0] = jnp.broadcast_to(dgk_tot, (8, 128))

    col = lambda off: pl.BlockSpec((1, t, SB_WIDTH), lambda b: (b, 0, off), pipeline_mode=pl.Buffered(1))
    gsp = pl.BlockSpec((1, 128), lambda b: (0, 0))
    gout = pl.BlockSpec((1, 8, 128), lambda b: (b, 0, 0))
    return pl.pallas_call(
        body, name="sb_bwd", grid=(bsz,),
        in_specs=[col(0), col(1), col(2), gsp, gsp, col(0), col(0), pl.BlockSpec(memory_space=pl.ANY)],
        out_specs=[pl.BlockSpec((1, t, 3 * SB_WIDTH), lambda b: (b, 0, C_SB // (3 * SB_WIDTH)), pipeline_mode=pl.Buffered(1)),
                   gout, gout],
        out_shape=[SDS(dproj.shape, bf16)] + [SDS((bsz, 8, 128), f32)] * 2,
        input_output_aliases={7: 0},
        scratch_shapes=[pltpu.VMEM((2 * SB_PAIRS, t, 128), bf16), pltpu.VMEM((SB_PAIRS, t, 128), bf16),
                        pltpu.VMEM((SB_PAIRS, t, 128), bf16), pltpu.VMEM((2 * SB_PAIRS, t, 128), bf16),
                        pltpu.VMEM((SB_PAIRS, t, 128), f32), pltpu.VMEM((SB_PAIRS, t, 128), f32), pltpu.VMEM((SB_PAIRS, t, 128), f32)],
        compiler_params=_cp(("arbitrary",)),
    )(sbqkv, sbqkv, sbqkv, gq, gk, ltot, do, dproj)


SG_STEP = 512


def sg_pair(u, v, gain, wa, wb, ba, bb, bavg):
    r, c = _iota2((SG_CHUNK, SG_CHUNK))
    lane = lax.broadcasted_iota(jnp.int32, (1, 128), 1)
    first = lane < SG_DIM
    vn = _pair_norm(_gelu(v), gain, bavg)
    tri = c <= r
    mixed = (mm(jnp.where(tri, wa, 0.0), jnp.where(first, vn, 0.0)) + mm(jnp.where(tri, wb, 0.0), jnp.where(first, 0.0, vn))
             + jnp.where(first, ba, bb))
    return _gelu(u) * mixed


def sg_fwd(sguv, gain, w, bt):
    bsz, t, _ = sguv.shape
    rows = min(SG_STEP, t)

    def body(uv_ref, g_ref, w_ref, b_ref, o_ref):
        bavg = _group_avg_mats()
        for r0 in range(0, rows, SG_CHUNK):
            rs = slice(r0, r0 + SG_CHUNK)
            for p in range(2):
                ls = slice(p * 128, (p + 1) * 128)
                o_ref[0, rs, ls] = sg_pair(uv_ref[0, rs, ls], uv_ref[0, rs, SG_WIDTH + p * 128:SG_WIDTH + (p + 1) * 128], g_ref[:, ls],
                                           w_ref[2 * p], w_ref[2 * p + 1], b_ref[:, 2 * p:2 * p + 1], b_ref[:, 2 * p + 1:2 * p + 2], bavg)

    full = lambda shp: pl.BlockSpec(shp, lambda b, n: (0,) * len(shp))
    return pl.pallas_call(
        body, name="sg_fwd", grid=(bsz, t // rows),
        in_specs=[pl.BlockSpec((1, rows, 2 * SG_WIDTH), lambda b, n: (b, n, 0)), full((1, SG_WIDTH)),
                  full((SG_GROUPS, SG_CHUNK, SG_CHUNK)), full((SG_CHUNK, 128))],
        out_specs=pl.BlockSpec((1, rows, SG_WIDTH), lambda b, n: (b, n, 0)),
        out_shape=SDS((bsz, t, SG_WIDTH), f32),
        compiler_params=_cp(("arbitrary", "arbitrary")),
    )(sguv, gain, w, bt)


def sg_bwd(sguv, gain, w, bt, do, dproj):
    bsz, t, _ = sguv.shape
    rows = min(SG_STEP, t)

    def body(uv_ref, g_ref, w_ref, b_ref, do_ref, dp_in, duv_ref, dg_ref, dw_ref, db_ref):
        @pl.when((pl.program_id(0) == 0) & (pl.program_id(1) == 0))
        def _():
            dg_ref[...] = jnp.zeros_like(dg_ref)
            dw_ref[...] = jnp.zeros_like(dw_ref)
            db_ref[...] = jnp.zeros_like(db_ref)

        bavg = _group_avg_mats()
        lane = lax.broadcasted_iota(jnp.int32, (SG_CHUNK, 128), 1)
        dbt = jnp.zeros((SG_CHUNK, 128), f32)
        dgs, dws = [jnp.zeros((1, 128), f32)] * 2, [jnp.zeros((SG_CHUNK, SG_CHUNK), f32)] * SG_GROUPS
        for r0 in range(0, rows, SG_CHUNK):
            rs = slice(r0, r0 + SG_CHUNK)
            for p in range(2):
                ls = slice(p * 128, (p + 1) * 128)
                vs = slice(SG_WIDTH + p * 128, SG_WIDTH + (p + 1) * 128)
                prim = (uv_ref[0, rs, ls], uv_ref[0, rs, vs], g_ref[:, ls], w_ref[2 * p], w_ref[2 * p + 1],
                        b_ref[:, 2 * p:2 * p + 1], b_ref[:, 2 * p + 1:2 * p + 2])
                _, vjp = jax.vjp(lambda *a: sg_pair(*a, bavg), *prim)
                du, dv, dgn, dwa, dwb, dba, dbb = vjp(do_ref[0, rs, ls])
                duv_ref[0, rs, ls] = du.astype(bf16)
                duv_ref[0, rs, vs] = dv.astype(bf16)
                dgs[p] = dgs[p] + dgn
                dws[2 * p], dws[2 * p + 1] = dws[2 * p] + dwa, dws[2 * p + 1] + dwb
                dbt = dbt + jnp.where(lane == 2 * p, dba, 0.0) + jnp.where(lane == 2 * p + 1, dbb, 0.0)
        for p in range(2):
            dg_ref[:, p * 128:(p + 1) * 128] += dgs[p]
        for gidx in range(SG_GROUPS):
            dw_ref[gidx] += dws[gidx]
        db_ref[...] += dbt

    full = lambda shp: pl.BlockSpec(shp, lambda b, n: (0,) * len(shp))
    return pl.pallas_call(
        body, name="sg_bwd", grid=(bsz, t // rows),
        in_specs=[pl.BlockSpec((1, rows, 2 * SG_WIDTH), lambda b, n: (b, n, 0)), full((1, SG_WIDTH)),
                  full((SG_GROUPS, SG_CHUNK, SG_CHUNK)), full((SG_CHUNK, 128)),
                  pl.BlockSpec((1, rows, SG_WIDTH), lambda b, n: (b, n, 0)), pl.BlockSpec(memory_space=pl.ANY)],
        out_specs=[pl.BlockSpec((1, rows, 2 * SG_WIDTH), lambda b, n: (b, n, C_SG // (2 * SG_WIDTH))), full((1, SG_WIDTH)),
                   full((SG_GROUPS, SG_CHUNK, SG_CHUNK)), full((SG_CHUNK, 128))],
        out_shape=[SDS(dproj.shape, bf16), SDS((1, SG_WIDTH), f32), SDS((SG_GROUPS, SG_CHUNK, SG_CHUNK), f32),
                   SDS((SG_CHUNK, 128), f32)],
        input_output_aliases={5: 0},
        compiler_params=_cp(("arbitrary", "arbitrary")),
    )(sguv, gain, w, bt, do, dproj)


def _pad_lanes(v, n=128):
    return jnp.pad(v.reshape(1, -1), ((0, 0), (0, n - v.size)))


def _w_in_runs():
    shard, runs = IN_DIM // N_CHIPS, []
    for s in range(N_CHIPS):
        for a, b, d in ((0, 2048, 0), (2048, 2056, C_AB), (2056, IN_DIM, C_SB)):
            lo, hi = max(shard * s, a), min(shard * (s + 1), b)
            if lo < hi:
                runs.append((s, lo - shard * s, hi - shard * s, d + lo - a))
    return runs


def w_in_from_shards(zone, tr=256):
    def body(z_ref, o_ref):
        o_ref[:, C_AB:C_SB] = jnp.zeros((tr, C_SB - C_AB), zone.dtype)
        for s, a, b, d in _w_in_runs():
            o_ref[:, d:d + b - a] = z_ref[s, :, a:b]

    return pl.pallas_call(
        body, name="w_in_from_shards", grid=(D_MODEL // tr,),
        in_specs=[pl.BlockSpec((N_CHIPS, tr, IN_DIM // N_CHIPS), lambda i: (0, i, 0))],
        out_specs=pl.BlockSpec((tr, IN_PAD), lambda i: (i, 0)), out_shape=SDS((D_MODEL, IN_PAD), zone.dtype),
        compiler_params=_cp(("arbitrary",)))(zone)


def w_in_grad_to_shards(g, tr=256):
    def body(g_ref, o_ref):
        for s, a, b, d in _w_in_runs():
            o_ref[s, :, a:b] = g_ref[:, d:d + b - a]

    return pl.pallas_call(
        body, name="w_in_grad_to_shards", grid=(D_MODEL // tr,),
        in_specs=[pl.BlockSpec((tr, IN_PAD), lambda i: (i, 0))],
        out_specs=pl.BlockSpec((N_CHIPS, tr, IN_DIM // N_CHIPS), lambda i: (0, i, 0)),
        out_shape=SDS((N_CHIPS, D_MODEL, IN_DIM // N_CHIPS), g.dtype), compiler_params=_cp(("arbitrary",)))(g)


def layer_params(p, l):
    return dict(
        g1=p["norm1_g"][l].reshape(1, -1), g2=p["norm2_g"][l].reshape(1, -1),
        conv=jnp.pad(p["conv_w"][l], ((0, 4), (0, 0))), alog=_pad_lanes(p["a_log"][l]), dtb=_pad_lanes(p["dt_bias"][l]),
        dng=p["dn_out_g"][l].reshape(1, -1), gq=jnp.tile(p["sb_q_g"][l].reshape(1, -1), (1, 2)),
        gk=jnp.tile(p["sb_k_g"][l].reshape(1, -1), (1, 2)), sgg=p["sg_v_g"][l].reshape(1, -1), sgw=p["sg_w"][l],
        sgb=jnp.pad(p["sg_b"][l].T, ((0, 0), (0, 124))))


def local_step(x, tgt, small, get_w, put_g, sync_g):
    bsz, t, _ = x.shape
    m = bsz * t
    r3 = lambda a: a.reshape(bsz, t, a.shape[-1])
    r2 = lambda a: a.reshape(m, a.shape[-1])
    xs, saved, ws = x.reshape(m, D_MODEL), [], []
    for l in range(DEPTH):
        sp, w = layer_params(small, l), {}
        w["w_in"] = get_w(l, "in", xs)
        qkv, z, ab, sb, sg, h1 = inproj_fwd(xs, sp["g1"], w["w_in"])
        odn, sall, tall = dn_fwd(r3(qkv), r3(z), r3(ab), sp["conv"], sp["alog"], sp["dtb"], sp["dng"])
        osb, ltot = sb_fwd(r3(sb), sp["gq"], sp["gk"])
        osg = sg_fwd(r3(sg), sp["sgg"], sp["sgw"], sp["sgb"])
        w["w_out"] = get_w(l, "out", osg)
        x2, mix = outproj_fwd(xs, r2(odn), r2(osb), r2(osg), w["w_out"])
        w["w_ff1"], w["w_ff2"], started = get_w(l, "ff", x2)
        if l + 1 < DEPTH:
            xs_next, rlb = ffn_fwd(x2, sp["g2"] + started, w["w_ff1"], w["w_ff2"])
        else:
            dx, rlb, lossp = ffn_fwd(x2, sp["g2"] + started, w["w_ff1"], w["w_ff2"], tgt=tgt.reshape(m, D_MODEL))
        saved.append(dict(rlb=rlb, h1=h1, x=xs, qkv=qkv, z=z, ab=ab, sb=sb, sg=sg, sall=sall, tall=tall, ltot=ltot, mix=mix, x2=x2))
        ws.append(w)
        xs = xs_next
    gsmall = [None] * DEPTH
    token = jnp.zeros((), f32)
    for l in reversed(range(DEPTH)):
        sp, w, s = layer_params(small, l), ws[l], saved[l]
        dx2, dg2, h2, act, df, dyb = ffn_bwd(s["x2"], sp["g2"] + token, w["w_ff1"], w["w_ff2"], s["rlb"], dx)
        g_ff1 = tn_matmul(h2, df, f"dw_ff1_{l}", col_shards=N_CHIPS)
        g_ff2 = tn_matmul(act, dyb, f"dw_ff2_{l}")
        dodn, dosb, dosg, dx2b = outproj_bwd(dx2, w["w_out"])
        g_out = tn_matmul(s["mix"], dx2b, f"dw_out_{l}")
        token = token + put_g(l, "rest", dict(w_out=g_out, w_ff1=g_ff1, w_ff2=g_ff2))
        dproj, dconv, dalog, ddtb, ddng = dn_bwd(r3(s["qkv"]), r3(s["z"]), r3(s["ab"]), sp["conv"], sp["alog"], sp["dtb"],
                                                 sp["dng"] + token, s["sall"], s["tall"], r3(dodn))
        token = sync_g(ddng)
        dproj, dgq, dgk = sb_bwd(r3(s["sb"]), sp["gq"] + token, sp["gk"], s["ltot"], r3(dosb), dproj)
        dproj, dsgg, dsgw, dsgb = sg_bwd(r3(s["sg"]), sp["sgg"], sp["sgw"], sp["sgb"], r3(dosg), dproj)
        dproj = r2(dproj)
        g_in = tn_matmul(s["h1"], dproj, f"dw_in_{l}")
        token = put_g(l, "in", dict(w_in=g_in))
        dx, dg1 = inproj_bwd(s["x"], sp["g1"] + token, w["w_in"], dproj, dx2)
        token = sync_g(dg1)
        fold = lambda a: (a[:, 0, :].sum(0).reshape(2, SB_DIM)).sum(0)
        gsmall[l] = dict(norm1_g=dg1[0], conv_w=dconv[0:DN_CONV], a_log=dalog[0, 0:DN_HEADS], dt_bias=ddtb[0, 0:DN_HEADS],
                         dn_out_g=ddng[0], sb_q_g=fold(dgq), sb_k_g=fold(dgk), sg_v_g=dsgg[0], sg_w=dsgw,
                         sg_b=dsgb[:, 0:SG_GROUPS].T, norm2_g=dg2[0])
    return lossp, dx.reshape(bsz, t, D_MODEL), gsmall


def _chip_peers(x, y):
    return [(1 - x, y), (x, 1 - y), (1 - x, 1 - y)]


_HBM = pl.BlockSpec(memory_space=pltpu.HBM)
_SEM = pl.BlockSpec(memory_space=pltpu.SEMAPHORE)
_EFFECT = pltpu.SideEffectType.DATAFLOW_SIDE_EFFECTING


def _hbm(a):
    return pltpu.with_memory_space_constraint(a, pltpu.HBM)


def _my_half(ref):
    half = ref.shape[0] // 2
    return ref.at[pl.ds(pl.multiple_of(lax.axis_index("c") * half, 8), half)]


def _exchange_copy(src, land, k, j, send, recv, scatter, halve, waiting):
    x, y, c = lax.axis_index("x"), lax.axis_index("y"), lax.axis_index("c")
    px, py = _chip_peers(x, y)[j]
    me, peer = 2 * x + y, 2 * px + py
    if scatter:
        src = src.at[me if waiting else peer]
    dst = land.at[peer if waiting else me]
    if halve:
        src, dst = _my_half(src), _my_half(dst)
    return pltpu.make_async_remote_copy(src_ref=src, dst_ref=dst, send_sem=send.at[3 * k + j],
                                        recv_sem=recv.at[3 * k + j], device_id=(px, py, c), device_id_type=MESH)


def exchange_start(items, name, scatter, after=None):
    arrs = []
    for a, _, _ in items:
        if not any(a is b for b in arrs):
            arrs.append(a)
    pos = [next(i for i, b in enumerate(arrs) if b is a) for a, _, _ in items]
    shapes = [a.shape if idx is None else a.shape[1:] for a, idx, _ in items]
    lands = [lax.empty(s if scatter else (N_CHIPS,) + s, a.dtype) for (a, _, _), s in zip(items, shapes)]
    na, nl = len(arrs), len(lands)
    n_in = na + nl + (after is not None)

    def body(*refs):
        ins, lnd = refs[:na], refs[na:na + nl]
        send, recv = refs[n_in], refs[n_in + 1]
        token = refs[-1]
        for k, (_, idx, halve) in enumerate(items):
            src = ins[pos[k]] if idx is None else ins[pos[k]].at[idx]
            for j in range(3):
                _exchange_copy(src, lnd[k], k, j, send, recv, scatter, halve, False).start()
        token[...] = jnp.zeros_like(token)

    sems = pltpu.SemaphoreType.DMA((3 * nl,))
    extra = [] if after is None else [after]
    out = pl.pallas_call(
        body, name=name,
        out_shape=(sems, sems, *[pltpu.HBM(a.shape, a.dtype) for a in arrs + lands], SDS((8, 128), f32)),
        in_specs=[_HBM] * (na + nl) + [pl.BlockSpec(memory_space=pl.ANY)] * len(extra),
        out_specs=(_SEM, _SEM, *[_HBM] * (na + nl), pl.BlockSpec(memory_space=pltpu.VMEM)),
        input_output_aliases={i: 2 + i for i in range(na + nl)},
        compiler_params=pltpu.CompilerParams(has_side_effects=_EFFECT),
    )(*[_hbm(a) for a in arrs + lands], *extra)
    thru = out[2:2 + na]
    return dict(send=out[0], recv=out[1], src=[(thru[pos[k]], idx) for k, (_, idx, _) in enumerate(items)],
                halve=[h for _, _, h in items], land=list(out[2 + na:2 + na + nl]), token=out[-1], scatter=scatter)


def exchange_wait(st, ks, after, name):
    arrs = []
    for k in ks:
        if not any(st["src"][k][0] is b for b in arrs):
            arrs.append(st["src"][k][0])
    pos = [next(i for i, b in enumerate(arrs) if b is st["src"][k][0]) for k in ks]
    lands = [st["land"][k] for k in ks]
    na, nl = len(arrs), len(lands)

    def body(*refs):
        ins, lnd = refs[:na], refs[na:na + nl]
        send, recv = refs[na + nl], refs[na + nl + 1]
        for t, k in enumerate(ks):
            idx = st["src"][k][1]
            src = ins[pos[t]] if idx is None else ins[pos[t]].at[idx]
            for j in range(3):
                cp = _exchange_copy(src, lnd[t], k, j, send, recv, st["scatter"], st["halve"][k], True)
                cp.wait_send()
                cp.wait_recv()

    out = pl.pallas_call(
        body, name=name, out_shape=tuple(pltpu.HBM(a.shape, a.dtype) for a in arrs + lands),
        in_specs=[_HBM] * (na + nl) + [_SEM, _SEM, pl.BlockSpec(memory_space=pl.ANY)], out_specs=tuple([_HBM] * (na + nl)),
        input_output_aliases={i: i for i in range(na + nl)},
        compiler_params=pltpu.CompilerParams(has_side_effects=_EFFECT),
    )(*arrs, *lands, st["send"], st["recv"], after)
    for k, (a, idx) in enumerate(st["src"]):
        for p, b in enumerate(arrs):
            if a is b:
                st["src"][k] = (out[p], idx)
    return list(out[na:na + nl])


def _sibling_copy(src, land, i, send, recv, other_half):
    x, y, c = lax.axis_index("x"), lax.axis_index("y"), lax.axis_index("c")
    return pltpu.make_async_remote_copy(src_ref=src.at[:, 1 - c] if other_half else src, dst_ref=land, send_sem=send.at[i],
                                        recv_sem=recv.at[i], device_id=(x, y, 1 - c), device_id_type=MESH)


def sibling_start(arrs, name, other_half=False):
    n = len(arrs)
    lands = [lax.empty((a.shape[0],) + a.shape[2:] if other_half else a.shape, a.dtype) for a in arrs]

    def body(*refs):
        ins, lnd = refs[:n], refs[n:2 * n]
        send, recv = refs[2 * n], refs[2 * n + 1]
        token = refs[-1]
        for i in range(n):
            _sibling_copy(ins[i], lnd[i], i, send, recv, other_half).start()
        token[...] = jnp.zeros_like(token)

    sems = pltpu.SemaphoreType.DMA((n,))
    out = pl.pallas_call(
        body, name=name,
        out_shape=(sems, sems, *[pltpu.HBM(a.shape, a.dtype) for a in arrs + lands], SDS((8, 128), f32)),
        in_specs=[_HBM] * (2 * n), out_specs=(_SEM, _SEM, *[_HBM] * (2 * n), pl.BlockSpec(memory_space=pltpu.VMEM)),
        input_output_aliases={i: 2 + i for i in range(2 * n)},
        compiler_params=pltpu.CompilerParams(has_side_effects=_EFFECT),
    )(*[_hbm(a) for a in arrs + lands])
    return dict(send=out[0], recv=out[1], src=list(out[2:2 + n]), land=list(out[2 + n:2 + 2 * n]), token=out[-1],
                other_half=other_half)


def sibling_wait(st, after, name):
    n = len(st["src"])

    def body(*refs):
        ins, lnd = refs[:n], refs[n:2 * n]
        send, recv = refs[2 * n], refs[2 * n + 1]
        for i in range(n):
            cp = _sibling_copy(ins[i], lnd[i], i, send, recv, st["other_half"])
            cp.wait_send()
            cp.wait_recv()

    out = pl.pallas_call(
        body, name=name, out_shape=tuple(pltpu.HBM(a.shape, a.dtype) for a in st["src"] + st["land"]),
        in_specs=[_HBM] * (2 * n) + [_SEM, _SEM, pl.BlockSpec(memory_space=pl.ANY)], out_specs=tuple([_HBM] * (2 * n)),
        input_output_aliases={i: i for i in range(2 * n)},
        compiler_params=pltpu.CompilerParams(has_side_effects=_EFFECT),
    )(*st["src"], *st["land"], st["send"], st["recv"], after)
    return list(out[:n]), list(out[n:])


def swap_halves(zones, name):
    n = len(zones)

    def body(*refs):
        outs = refs[n:2 * n]
        send, recv = refs[2 * n:]
        x, y, c = lax.axis_index("x"), lax.axis_index("y"), lax.axis_index("c")
        cps = []
        for i in range(n):
            for j, (px, py) in enumerate(_chip_peers(x, y)):
                part = _my_half(outs[i].at[2 * px + py])
                cps.append(pltpu.make_async_remote_copy(src_ref=part, dst_ref=part, send_sem=send.at[3 * i + j],
                                                        recv_sem=recv.at[3 * i + j], device_id=(x, y, 1 - c), device_id_type=MESH))
        for cp in cps:
            cp.start()
        for cp in cps:
            cp.wait_send()
            cp.wait_recv()

    any_spec = pl.BlockSpec(memory_space=pl.ANY)
    return pl.pallas_call(
        body, name=name, in_specs=[any_spec] * n, out_specs=[any_spec] * n, out_shape=[SDS(a.shape, a.dtype) for a in zones],
        input_output_aliases={i: i for i in range(n)},
        scratch_shapes=[pltpu.SemaphoreType.DMA((3 * n,)), pltpu.SemaphoreType.DMA((3 * n,))],
    )(*zones)


def _ids_spec(grid, in_specs, out_specs):
    return pltpu.PrefetchScalarGridSpec(num_scalar_prefetch=1, grid=grid, in_specs=in_specs, out_specs=out_specs)


def pair_sum(ids, a, b, name, tr=512):
    nd, _, rows, cols = a.shape
    tr = min(tr, rows)
    assert rows % tr == 0

    def body(ids_ref, a_ref, b_ref, o_ref):
        o_ref[...] = (a_ref[0].astype(f32) + b_ref[...].astype(f32)).astype(bf16)

    spec = pl.BlockSpec((1, tr, cols), lambda d, i, ids: (d, i, 0))
    return pl.pallas_call(
        body, name=name,
        grid_spec=_ids_spec((nd, rows // tr), [pl.BlockSpec((1, 1, tr, cols), lambda d, i, ids: (d, ids[1], i, 0)), spec], spec),
        out_shape=SDS((nd, rows, cols), bf16), compiler_params=_cp(("arbitrary", "arbitrary")))(ids, a, b)


def allreduce_small(v):
    half = v.shape[0] // 2
    assert half % 8 == 0

    def body(v_ref, o_ref, rbuf, send, recv):
        x, y, c = lax.axis_index("x"), lax.axis_index("y"), lax.axis_index("c")
        o_ref[...] = v_ref[...]

        def exchange(rows, peer, k):
            return pltpu.make_async_remote_copy(src_ref=o_ref.at[rows], dst_ref=rbuf.at[k, rows], send_sem=send.at[k],
                                                recv_sem=recv.at[k], device_id=peer, device_id_type=MESH)

        lo, hi, across_x, across_y = pl.ds(0, half), pl.ds(half, half), (1 - x, y, c), (x, 1 - y, c)
        stages = [[(pl.ds(0, 2 * half), (x, y, 1 - c))], [(lo, across_x), (hi, across_y)], [(lo, across_y), (hi, across_x)]]
        k = 0
        for stage in stages:
            cps = [exchange(rows, peer, k + i) for i, (rows, peer) in enumerate(stage)]
            for cp in cps:
                cp.start()
            for cp in cps:
                cp.wait()
            for i, (rows, _) in enumerate(stage):
                o_ref[rows] = o_ref[rows] + rbuf[k + i, rows]
            k += len(stage)

    vm = pl.BlockSpec(memory_space=pltpu.VMEM)
    return pl.pallas_call(
        body, name="allreduce_small", in_specs=[vm], out_specs=vm, out_shape=SDS(v.shape, f32),
        scratch_shapes=[pltpu.VMEM((5,) + v.shape, f32), pltpu.SemaphoreType.DMA((5,)), pltpu.SemaphoreType.DMA((5,))],
        compiler_params=_cp(),
    )(v)


def sum_partials(ids, zone, mine, name, tr=256):
    _, rows, cols = zone.shape
    tr = min(tr, rows)
    assert rows % tr == 0

    def body(ids_ref, m_ref, z1_ref, z2_ref, z3_ref, o_ref):
        o_ref[...] = ((m_ref[0].astype(f32) + z1_ref[0].astype(f32)) + z2_ref[0].astype(f32)) + z3_ref[0].astype(f32)

    slot = lambda flip: pl.BlockSpec((1, tr, cols), lambda i, ids: (ids[0] ^ flip, i, 0))
    return pl.pallas_call(
        body, name=name,
        grid_spec=_ids_spec((rows // tr,), [slot(0), slot(1), slot(2), slot(3)], pl.BlockSpec((tr, cols), lambda i, ids: (i, 0))),
        out_shape=SDS((rows, cols), f32), compiler_params=_cp(("arbitrary",)),
    )(ids, mine, zone, zone, zone)


def adamw(w, m, v, gs, name, layer=0, prev=None, tr=256):
    hrows, cols = gs[0].shape
    rows = hrows * len(gs)
    tr = min(tr, hrows)
    assert hrows % tr == 0 and w.shape[0] % rows == 0
    off, nth = layer * (rows // tr), hrows // tr

    def body(w_ref, m_ref, v_ref, *rest):
        g_ref, d_ref, mo_ref, vo_ref = rest[-4:]
        if len(gs) == 1:
            g = rest[0][...]
        else:
            g = jnp.where(pl.program_id(0) // nth == lax.axis_index("c"), rest[0][...], rest[1][...])
        mn = ADAM_B1 * m_ref[...] + (1.0 - ADAM_B1) * g
        vn = ADAM_B2 * v_ref[...] + (1.0 - ADAM_B2) * jnp.square(g)
        m_hat = mn / (1.0 - ADAM_B1 ** ADAM_STEP)
        v_hat = vn / (1.0 - ADAM_B2 ** ADAM_STEP)
        g_ref[...] = g
        d_ref[...] = -ADAM_LR * (m_hat / (jnp.sqrt(v_hat) + ADAM_EPS) + ADAM_WD * w_ref[...])
        mo_ref[...] = mn
        vo_ref[...] = vn

    loc = pl.BlockSpec((tr, cols), lambda i: (i % nth, 0))
    glob = pl.BlockSpec((tr, cols), lambda i: (off + i, 0))
    extra = [] if prev is None else list(prev)
    return pl.pallas_call(
        body, name=name, grid=(rows // tr,),
        in_specs=[glob] * 3 + [loc] * len(gs) + [pl.BlockSpec(memory_space=pl.ANY)] * len(extra),
        out_specs=[glob] * 4, out_shape=[SDS(w.shape, f32)] * 4,
        input_output_aliases={3 + len(gs) + j: j for j in range(len(extra))},
        compiler_params=_cp(("arbitrary",)),
    )(w, m, v, *gs, *extra)


BIG = ("w_in", "w_out", "w_ff1", "w_ff2")
SMALL = ("norm1_g", "conv_w", "a_log", "dt_bias", "dn_out_g", "sb_q_g", "sb_k_g", "sg_v_g", "sg_w", "sg_b", "norm2_g")
WEIGHTS = ("norm1_g", "w_in", "conv_w", "a_log", "dt_bias", "dn_out_g", "sb_q_g", "sb_k_g", "sg_v_g", "sg_w", "sg_b",
           "w_out", "norm2_g", "w_ff1", "w_ff2")


PACK_ROWS = 256


def _rows_of(shape):
    n = 1
    for d in shape:
        n *= d
    return -(-n // 1024) * 8, n


def _pack(arrs):
    parts = []
    for a in arrs:
        r, n = _rows_of(a.shape)
        parts.append(jnp.pad(a.reshape(-1), (0, r * 128 - n)).reshape(r, 128))
    rows = sum(p.shape[0] for p in parts)
    parts.append(jnp.zeros((-rows % PACK_ROWS, 128), arrs[0].dtype))
    return jnp.concatenate(parts, axis=0)


def _unpack(packed, shapes):
    out, o = [], 0
    for s in shapes:
        r, n = _rows_of(s)
        out.append(packed[o:o + r].reshape(-1)[0:n].reshape(s))
        o += r
    return out


def kernel(x, norm1_g, w_in, conv_w, a_log, dt_bias, dn_out_g, sb_q_g, sb_k_g, sg_v_g, sg_w, sg_b, w_out, norm2_g, w_ff1, w_ff2, loss_target, m_norm1_g, m_w_in, m_conv_w, m_a_log, m_dt_bias, m_dn_out_g, m_sb_q_g, m_sb_k_g, m_sg_v_g, m_sg_w, m_sg_b, m_w_out, m_norm2_g, m_w_ff1, m_w_ff2, v_norm1_g, v_w_in, v_conv_w, v_a_log, v_dt_bias, v_dn_out_g, v_sb_q_g, v_sb_k_g, v_sg_v_g, v_sg_w, v_sg_b, v_w_out, v_norm2_g, v_w_ff1, v_w_ff2):
    w = dict(norm1_g=norm1_g, w_in=w_in, conv_w=conv_w, a_log=a_log, dt_bias=dt_bias, dn_out_g=dn_out_g, sb_q_g=sb_q_g,
             sb_k_g=sb_k_g, sg_v_g=sg_v_g, sg_w=sg_w, sg_b=sg_b, w_out=w_out, norm2_g=norm2_g, w_ff1=w_ff1, w_ff2=w_ff2)
    mom = dict(norm1_g=m_norm1_g, w_in=m_w_in, conv_w=m_conv_w, a_log=m_a_log, dt_bias=m_dt_bias, dn_out_g=m_dn_out_g,
               sb_q_g=m_sb_q_g, sb_k_g=m_sb_k_g, sg_v_g=m_sg_v_g, sg_w=m_sg_w, sg_b=m_sg_b, w_out=m_w_out, norm2_g=m_norm2_g,
               w_ff1=m_w_ff1, w_ff2=m_w_ff2)
    var = dict(norm1_g=v_norm1_g, w_in=v_w_in, conv_w=v_conv_w, a_log=v_a_log, dt_bias=v_dt_bias, dn_out_g=v_dn_out_g,
               sb_q_g=v_sb_q_g, sb_k_g=v_sb_k_g, sg_v_g=v_sg_v_g, sg_w=v_sg_w, sg_b=v_sg_b, w_out=v_w_out, norm2_g=v_norm2_g,
               w_ff1=v_w_ff1, w_ff2=v_w_ff2)
    chip = 2 * lax.axis_index("x") + lax.axis_index("y")

    wb = [{k: w[k][l].astype(bf16) for k in BIG} for l in range(DEPTH)]
    ags = {0: exchange_start([(conv_w, None, False)] + [(wb[0][k], None, True) for k in BIG], "allgather_start_0", scatter=False)}
    item = lambda l, k: (l, (l == 0) + BIG.index(k))

    def landed(items, after, name):
        ag, ks = ags[items[0][0]], [k for _, k in items]
        zones = exchange_wait(ag, ks, after, name)
        halved = [t for t, k in enumerate(ks) if ag["halve"][k]]
        for t, z in zip(halved, swap_halves([zones[t] for t in halved], name.replace("wait", "pass"))):
            zones[t] = z
        return [lax.dynamic_update_slice_in_dim(z, ag["src"][k][0][None], chip, axis=0) for z, k in zip(zones, ks)]

    def whole(k, z):
        if k == "w_in":
            return w_in_from_shards(z)
        return z if k == "w_ff1" else z.reshape(-1, D_MODEL)

    g_conv, first_in = landed([(0, 0), item(0, "w_in")], x, "allgather_wait_in0")
    small = {k: w[k] for k in SMALL}
    small["conv_w"] = jnp.transpose(g_conv, (1, 2, 0, 3)).reshape(DEPTH, DN_CONV, 3 * DN_WIDTH)
    cache = {}

    def get_w(l, part, after):
        if part == "in":
            return whole("w_in", first_in if l == 0 else landed([item(l, "w_in")], after, f"allgather_wait_in{l}")[0])
        if part == "out":
            zs = landed([item(l, k) for k in ("w_out", "w_ff1", "w_ff2")], after, f"allgather_wait_rest{l}")
            token = jnp.zeros((), f32)
            if l + 1 < DEPTH:
                ags[l + 1] = exchange_start([(wb[l + 1][k], None, True) for k in BIG], f"allgather_start_{l + 1}",
                                            scatter=False, after=zs[0])
                token = ags[l + 1]["token"][0, 0]
            cache[l] = (whole("w_ff1", zs[1]), whole("w_ff2", zs[2]), token)
            return whole("w_out", zs[0])
        return cache[l]

    rs, pending = {}, []
    ids = jnp.stack([chip, lax.axis_index("c")]).astype(jnp.int32)

    def put_g(l, tag, g):
        names = [k for k in BIG if k in g]
        by_dest = [w_in_grad_to_shards(g[k]) if k == "w_in" else g[k] for k in names]
        halves = [a.reshape(N_CHIPS, 2, -1, a.shape[-1]) for a in by_dest]
        st = sibling_start(halves, f"pair_swap_start_{tag}{l}", other_half=True)
        pending.append((l, tag, names, st))
        return st["token"][0, 0]

    def sync_g(after):
        token = jnp.zeros((), f32)
        while pending:
            l, tag, names, st = pending.pop(0)
            halves, got = sibling_wait(st, after, f"pair_swap_wait_{tag}{l}")
            pair = [pair_sum(ids, a, b, f"pair_sum_{k}_{l}") for k, a, b in zip(names, halves, got)]
            rs[l, tag] = dict(exchange_start([(a, None, False) for a in pair], f"scatter_start_{tag}{l}", scatter=True), names=names)
            token = token + rs[l, tag]["token"][0, 0]
        return token

    lossp, grad_x, gsmall = local_step(x, loss_target, small, get_w, put_g, sync_g)

    def sum_group(l, tag, after):
        st = rs[l, tag]
        zones = exchange_wait(st, list(range(len(st["names"]))), after, f"scatter_wait_{tag}{l}")
        sums = [sum_partials(ids, zones[i], st["src"][i][0], f"sum_{k}_{l}") for i, k in enumerate(st["names"])]
        return sibling_start(sums, f"swap_sums_start_{tag}{l}")

    def update_group(l, tag, swap, after, prev):
        sums, others = sibling_wait(swap, after, f"swap_sums_wait_{tag}{l}")
        outs = dict(prev)
        for i, k in enumerate(rs[l, tag]["names"]):
            r2 = lambda a: a.reshape(-1, a.shape[-1])
            outs[k] = adamw(r2(w[k]), r2(mom[k]), r2(var[k]), (sums[i], others[i]), f"adamw_{k}_{l}", layer=l, prev=prev.get(k))
        return outs

    swap_r = sum_group(1, "rest", rs[0, "in"]["token"])
    swap_i = sum_group(1, "in", swap_r["token"])
    done = update_group(1, "rest", swap_r, swap_i["token"], {})
    done = update_group(1, "in", swap_i, done["w_ff2"][0], done)
    res = {}

    full_shapes = [(DEPTH,) + tuple(gsmall[0][k].shape) for k in SMALL]
    packed = _pack([jnp.stack([gsmall[l][k] for l in range(DEPTH)]) for k in SMALL] + [jnp.sum(lossp).reshape(1)])
    *totals, loss = _unpack(allreduce_small(packed), full_shapes + [(1,)])
    loss = loss[0]
    gfull = dict(zip(SMALL, totals))
    cs = 3 * DN_WIDTH // N_CHIPS
    gfull["conv_w"] = lax.dynamic_slice_in_dim(gfull["conv_w"], chip * cs, cs, axis=2)
    gp, wp, mp, vp = (_pack([d[k] for k in SMALL]) for d in (gfull, w, mom, var))
    outs = adamw(wp, mp, vp, (gp,), "adamw_small")
    loc_shapes = [w[k].shape for k in SMALL]
    unp = [_unpack(o, loc_shapes) for o in outs]
    for i, k in enumerate(SMALL):
        res[k] = [unp[j][i] for j in range(4)]

    swap_r = sum_group(0, "rest", outs[0])
    swap_i = sum_group(0, "in", swap_r["token"])
    done = update_group(0, "rest", swap_r, swap_i["token"], done)
    done = update_group(0, "in", swap_i, done["w_ff2"][0], done)
    for k in BIG:
        res[k] = [o.reshape(w[k].shape) for o in done[k]]

    return (loss, grad_x, *[res[k][0] for k in WEIGHTS], *[res[k][1] for k in WEIGHTS], *[res[k][2] for k in WEIGHTS],
            *[res[k][3] for k in WEIGHTS])
```

```python
import functools

import jax
import jax.numpy as jnp
from jax import lax
from jax.experimental import pallas as pl
from jax.experimental.pallas import tpu as pltpu

f32 = jnp.float32
bf16 = jnp.bfloat16
SDS = jax.ShapeDtypeStruct
MESH = pl.DeviceIdType.MESH

NORM_EPS = 1e-6
D_MODEL = 1024
DEPTH = 2
DN_HEADS, DN_DIM, DN_WIDTH, DN_CONV, DN_CHUNK = 4, 128, 512, 4, 64
SB_HEADS, SB_DIM, SB_WIDTH = 4, 64, 256
SG_GROUPS, SG_DIM, SG_WIDTH, SG_CHUNK = 4, 64, 256, 128
D_FF = 4096
IN_DIM = 3336
C_QKV, C_Z, C_AB, C_SB, C_SG, IN_PAD = 0, 1536, 2048, 2304, 3072, 3584
DN_COLS = C_SB
N_CHIPS = 4

ADAM_LR, ADAM_B1, ADAM_B2, ADAM_EPS, ADAM_WD, ADAM_STEP = 0.001, 0.9, 0.999, 1e-08, 0.01, 10

VMEM_LIMIT = 56 * 1024 * 1024


def _cp(sem=None, **kw):
    if sem is not None:
        kw["dimension_semantics"] = sem
    return pltpu.CompilerParams(vmem_limit_bytes=VMEM_LIMIT, **kw)


def _split2(x):
    hi = x.astype(bf16)
    lo = (x - hi.astype(f32)).astype(bf16)
    return hi, lo


NT = (((1,), (1,)), ((), ()))
TN = (((0,), (0,)), ((), ()))
_DIMS2 = dict(nn=(((1,), (0,)), ((), ())), nt=NT, tn=TN)
_DIMS3 = dict(nn=(((2,), (1,)), ((0,), (0,))), nt=(((2,), (2,)), ((0,), (0,))), tn=(((1,), (1,)), ((0,), (0,))))


def _dg(a, b, kind):
    return lax.dot_general(a, b, (_DIMS2 if a.ndim == 2 else _DIMS3)[kind], preferred_element_type=f32)


def _pdot(a, b):
    return _dg(a, b, "nn")


def _dot_hp(a, b):
    ah, al = _split2(a)
    bh, bl = _split2(b)
    return _pdot(ah, bh) + _pdot(ah, bl) + _pdot(al, bh)


def _dot_x2c(a, m):
    lead = a.shape[:-1]
    ah, al = _split2(a.reshape(-1, a.shape[-1]))
    return (_pdot(ah, m) + _pdot(al, m)).reshape(lead + (m.shape[1],))


def _dot_cx2(m, a):
    if a.ndim == 3:
        m = jnp.broadcast_to(m, (a.shape[0],) + m.shape)
    ah, al = _split2(a)
    return _pdot(m, ah) + _pdot(m, al)


def _nt(a, b):
    return _dg(a.astype(bf16), b.astype(bf16), "nt")


def _tn(a, b):
    return _dg(a.astype(bf16), b.astype(bf16), "tn")


def _nn(a, b):
    return _dg(a.astype(bf16), b.astype(bf16), "nn")


@jax.custom_vjp
def mm(a, b):
    return _nn(a, b)


mm.defvjp(lambda a, b: (_nn(a, b), (a, b)), lambda r, g: (_nt(g, r[1]), _tn(r[0], g)))


@jax.custom_vjp
def mm_nt(a, b):
    return _nt(a, b)


mm_nt.defvjp(lambda a, b: (_nt(a, b), (a, b)), lambda r, g: (_nn(g, r[1]), _tn(g, r[0])))


@jax.custom_vjp
def mm_tn(a, b):
    return _tn(a, b)


mm_tn.defvjp(lambda a, b: (_tn(a, b), (a, b)), lambda r, g: (_nt(r[1], g), _nn(r[0], g)))


@jax.custom_vjp
def rmul_const(a, m, mt):
    return _dot_x2c(a, m)


rmul_const.defvjp(lambda a, m, mt: (_dot_x2c(a, m), (m, mt)),
                  lambda r, g: (_dot_x2c(g, r[1]), jnp.zeros_like(r[0]), jnp.zeros_like(r[1])))


@jax.custom_vjp
def lmul_const(m, mt, a):
    return _dot_cx2(m, a)


lmul_const.defvjp(lambda m, mt, a: (_dot_cx2(m, a), (m, mt)),
                  lambda r, g: (jnp.zeros_like(r[0]), jnp.zeros_like(r[1]), _dot_cx2(r[1], g)))


@jax.custom_vjp
def mm_hl(t, x):
    th, tl = _split2(t)
    xb = x.astype(bf16)
    return _pdot(th, xb) + _pdot(tl, xb)


def _mm_hl_bwd(r, g):
    t, x = r
    th, tl = _split2(t)
    gb = g.astype(bf16)
    return _nt(g, x), _dg(th, gb, "tn") + _dg(tl, gb, "tn")


mm_hl.defvjp(lambda t, x: (mm_hl(t, x), (t, x)), _mm_hl_bwd)


def inv_unit_lower(lm):
    c = lm.shape[-1]
    r, cc = _iota2((c, c))
    eye = (r == cc).astype(f32)
    t = eye - lm
    p = -lm
    k = 1
    while 2 * k < c:
        p = _nn(p, p)
        t = t + _nn(t, p)
        k *= 2
    res = eye - t - _dot_hp(lm, t)
    return t + _nn(t, res)


@jax.custom_vjp
def inv_given(lm, t):
    return t


inv_given.defvjp(lambda lm, t: (t, t), lambda t, g: (-_nt(_tn(t, g), t), jnp.zeros_like(t)))


def _sigmoid(x):
    return 1.0 / (1.0 + jnp.exp(-x))


def _softplus(x):
    return jnp.maximum(x, 0.0) + jnp.log(1.0 + jnp.exp(-jnp.abs(x)))


def _silu(x):
    return x * _sigmoid(x)


def _gelu(x):
    return 0.5 * x * (1.0 + jnp.tanh(0.7978845608028654 * (x + 0.044715 * (x * x * x))))


def _iota2(shape):
    return lax.broadcasted_iota(jnp.int32, shape, 0), lax.broadcasted_iota(jnp.int32, shape, 1)


def _group_avg_mats():
    r, c = _iota2((128, 128))
    return jnp.where((r // 64) == (c // 64), 1.0 / 64.0, 0.0).astype(bf16)


def _pair_norm(x, gain, bavg):
    ms = rmul_const(x * x, bavg, bavg)
    return x * lax.rsqrt(ms + NORM_EPS) * gain


def _rms(x):
    r = lax.rsqrt(jnp.mean(x * x, axis=-1, keepdims=True) + NORM_EPS)
    return r


_IN_GROUPS = ((C_QKV, C_Z), (C_Z, C_AB), (C_AB, C_AB + 128), (C_SB, C_SG), (C_SG, IN_PAD))


def inproj_fwd(x, g, wp, tm=256):
    m = x.shape[0]

    def body(x_ref, g_ref, w_ref, *outs):
        xv = x_ref[...]
        h = (xv * _rms(xv) * g_ref[...]).astype(bf16)
        outs[-1][...] = h
        for (a, b), o in zip(_IN_GROUPS, outs):
            o[...] = _pdot(h, w_ref[:, a:b])

    widths = [b - a for a, b in _IN_GROUPS]
    return pl.pallas_call(
        body, name="inproj_fwd", grid=(m // tm,),
        in_specs=[pl.BlockSpec((tm, D_MODEL), lambda i: (i, 0)), pl.BlockSpec((1, D_MODEL), lambda i: (0, 0)),
                  pl.BlockSpec((D_MODEL, IN_PAD), lambda i: (0, 0))],
        out_specs=[pl.BlockSpec((tm, wd), lambda i: (i, 0)) for wd in widths + [D_MODEL]],
        out_shape=[SDS((m, wd), f32) for wd in widths] + [SDS((m, D_MODEL), bf16)],
        compiler_params=_cp(("arbitrary",)),
    )(x, g, wp)


def inproj_bwd(x, g, wp, dproj, dres, tm=256):
    m = x.shape[0]

    def body(x_ref, g_ref, w_ref, dp_ref, dr_ref, dx_ref, dg_ref):
        xv = x_ref[...]
        r = _rms(xv)
        xn = xv * r
        gv = g_ref[...]
        dh = lax.dot_general(dp_ref[...], w_ref[...], NT, preferred_element_type=f32)
        dxn = dh * gv
        dx_ref[...] = dr_ref[...] + r * (dxn - xn * jnp.mean(dxn * xn, axis=-1, keepdims=True))

        @pl.when(pl.program_id(0) == 0)
        def _():
            dg_ref[...] = jnp.zeros_like(dg_ref)

        dg_ref[...] += jnp.sum(dh * xn, axis=0, keepdims=True)

    return pl.pallas_call(
        body, name="inproj_bwd", grid=(m // tm,),
        in_specs=[pl.BlockSpec((tm, D_MODEL), lambda i: (i, 0)), pl.BlockSpec((1, D_MODEL), lambda i: (0, 0)),
                  pl.BlockSpec((D_MODEL, IN_PAD), lambda i: (0, 0)), pl.BlockSpec((tm, IN_PAD), lambda i: (i, 0)),
                  pl.BlockSpec((tm, D_MODEL), lambda i: (i, 0))],
        out_specs=[pl.BlockSpec((tm, D_MODEL), lambda i: (i, 0)), pl.BlockSpec((1, D_MODEL), lambda i: (0, 0))],
        out_shape=[SDS((m, D_MODEL), f32), SDS((1, D_MODEL), f32)],
        compiler_params=_cp(("arbitrary",)),
    )(x, g, wp, dproj, dres)


def outproj_fwd(x, mix, wo, tm=512):
    m = x.shape[0]

    def body(x_ref, mix_ref, w_ref, x2_ref):
        x2_ref[...] = x_ref[...] + _pdot(mix_ref[...], w_ref[...])

    row = pl.BlockSpec((tm, D_MODEL), lambda i: (i, 0))
    return pl.pallas_call(
        body, name="outproj_fwd", grid=(m // tm,),
        in_specs=[row, row, pl.BlockSpec((D_MODEL, D_MODEL), lambda i: (0, 0))],
        out_specs=row, out_shape=SDS((m, D_MODEL), f32),
        compiler_params=_cp(("arbitrary",)),
    )(x, mix, wo)


def outproj_bwd(dx2, wo, tm=512):
    m = dx2.shape[0]

    def body(d_ref, w_ref, a_ref, b_ref, c_ref, db_ref):
        db = d_ref[...].astype(bf16)
        db_ref[...] = db
        dm = lax.dot_general(db, w_ref[...], NT, preferred_element_type=f32)
        a_ref[...] = dm[:, 0:DN_WIDTH]
        b_ref[...] = dm[:, DN_WIDTH:DN_WIDTH + SB_WIDTH]
        c_ref[...] = dm[:, DN_WIDTH + SB_WIDTH:D_MODEL]

    row = lambda w: pl.BlockSpec((tm, w), lambda i: (i, 0))
    return pl.pallas_call(
        body, name="outproj_bwd", grid=(m // tm,),
        in_specs=[row(D_MODEL), pl.BlockSpec((D_MODEL, D_MODEL), lambda i: (0, 0))],
        out_specs=[row(DN_WIDTH), row(SB_WIDTH), row(SG_WIDTH), row(D_MODEL)],
        out_shape=[SDS((m, DN_WIDTH), f32), SDS((m, SB_WIDTH), f32), SDS((m, SG_WIDTH), f32), SDS((m, D_MODEL), bf16)],
        compiler_params=_cp(("arbitrary",)),
    )(dx2, wo)


FF_CHUNK = D_FF // N_CHIPS


def _load_weights_once(pairs, sem):
    @pl.when(pl.program_id(0) == 0)
    def _():
        cps = [pltpu.make_async_copy(h, v, sem.at[i]) for i, (h, v) in enumerate(pairs)]
        for c in cps:
            c.start()
        for c in cps:
            c.wait()


def ffn_fwd(x2, g, w1, w2, tgt=None, tm=256):
    m = x2.shape[0]
    head = tgt is not None

    def body(x_ref, g_ref, w1_hbm, w2_hbm, *rest):
        (y_ref, rl_ref), (w1_v, w2_v, sem) = rest[head:head + 2], rest[-3:]
        _load_weights_once(((w1_hbm, w1_v), (w2_hbm, w2_v)), sem)
        xv = x_ref[...]
        h = (xv * _rms(xv) * g_ref[...]).astype(bf16)
        acc = xv
        for j in range(0, D_FF, FF_CHUNK):
            f = _pdot(h, w1_v[j // FF_CHUNK])
            rl = jnp.maximum(f, 0.0)
            rl_ref[:, j:j + FF_CHUNK] = rl.astype(bf16)
            acc = acc + _pdot((rl * rl).astype(bf16), w2_v[j:j + FF_CHUNK, :])
        if not head:
            y_ref[...] = acc
            return
        t_ref, l_ref = rest[0], rest[3]
        e = acc - t_ref[...]
        y_ref[...] = e * (1.0 / D_MODEL)

        @pl.when(pl.program_id(0) == 0)
        def _():
            l_ref[...] = jnp.zeros_like(l_ref)

        l_ref[...] += jnp.sum(e * e, axis=0, keepdims=True) * (0.5 / D_MODEL)

    row = pl.BlockSpec((tm, D_MODEL), lambda i: (i, 0))
    return pl.pallas_call(
        body, name="ffn_fwd_loss" if head else "ffn_fwd", grid=(m // tm,),
        in_specs=[row, pl.BlockSpec((1, D_MODEL), lambda i: (0, 0)), pl.BlockSpec(memory_space=pl.ANY),
                  pl.BlockSpec(memory_space=pl.ANY)] + [row] * head,
        out_specs=[row, pl.BlockSpec((tm, D_FF), lambda i: (i, 0))] + [pl.BlockSpec((1, D_MODEL), lambda i: (0, 0))] * head,
        out_shape=[SDS((m, D_MODEL), f32), SDS((m, D_FF), bf16)] + [SDS((1, D_MODEL), f32)] * head,
        scratch_shapes=[pltpu.VMEM((N_CHIPS, D_MODEL, FF_CHUNK), bf16), pltpu.VMEM((D_FF, D_MODEL), bf16), pltpu.SemaphoreType.DMA((2,))],
        compiler_params=_cp(("arbitrary",)),
    )(x2, g, w1, w2, *([tgt] if head else []))


def ffn_bwd(x2, g, w1, w2, rlb, dy, tm=256):
    m = x2.shape[0]

    def body(x_ref, g_ref, w1_hbm, w2_hbm, rl_ref, dy_ref, dx_ref, dg_ref, h_ref, a_ref, df_ref, dyb_ref, w1_v, w2_v, sem):
        _load_weights_once(((w1_hbm, w1_v), (w2_hbm, w2_v)), sem)
        xv = x_ref[...]
        r = _rms(xv)
        xn = xv * r
        gv = g_ref[...]
        h = (xn * gv).astype(bf16)
        h_ref[...] = h
        dyv = dy_ref[...]
        dyb = dyv.astype(bf16)
        dyb_ref[...] = dyb
        dh = jnp.zeros((tm, D_MODEL), f32)
        for j in range(0, D_FF, FF_CHUNK):
            rl = rl_ref[:, j:j + FF_CHUNK].astype(f32)
            a_ref[:, j:j + FF_CHUNK] = (rl * rl).astype(bf16)
            da = lax.dot_general(dyb, w2_v[j:j + FF_CHUNK, :], NT, preferred_element_type=f32)
            df = (da * (2.0 * rl)).astype(bf16)
            df_ref[:, j:j + FF_CHUNK] = df
            dh = dh + lax.dot_general(df, w1_v[j // FF_CHUNK], NT, preferred_element_type=f32)
        dxn = dh * gv
        dx_ref[...] = dyv + r * (dxn - xn * jnp.mean(dxn * xn, axis=-1, keepdims=True))

        @pl.when(pl.program_id(0) == 0)
        def _():
            dg_ref[...] = jnp.zeros_like(dg_ref)

        dg_ref[...] += jnp.sum(dh * xn, axis=0, keepdims=True)

    row = lambda w: pl.BlockSpec((tm, w), lambda i: (i, 0))
    return pl.pallas_call(
        body, name="ffn_bwd", grid=(m // tm,),
        in_specs=[row(D_MODEL), pl.BlockSpec((1, D_MODEL), lambda i: (0, 0)),
                  pl.BlockSpec(memory_space=pl.ANY), pl.BlockSpec(memory_space=pl.ANY), row(D_FF), row(D_MODEL)],
        out_specs=[row(D_MODEL), pl.BlockSpec((1, D_MODEL), lambda i: (0, 0)), row(D_MODEL), row(D_FF), row(D_FF), row(D_MODEL)],
        out_shape=[SDS((m, D_MODEL), f32), SDS((1, D_MODEL), f32), SDS((m, D_MODEL), bf16), SDS((m, D_FF), bf16),
                   SDS((m, D_FF), bf16), SDS((m, D_MODEL), bf16)],
        scratch_shapes=[pltpu.VMEM((N_CHIPS, D_MODEL, FF_CHUNK), bf16), pltpu.VMEM((D_FF, D_MODEL), bf16), pltpu.SemaphoreType.DMA((2,))],
        compiler_params=_cp(("arbitrary",)),
    )(x2, g, w1, w2, rlb, dy)


def _tile(n, cap):
    best = 128
    for t in range(128, cap + 1, 128):
        if n % t == 0:
            best = t
    return best


def tn_matmul(a, b, name, col_shards=1, tk=2048):
    m, ka = a.shape
    n = b.shape[1]
    ti = _tile(ka, 1024)
    tj = _tile(n // col_shards, 1152)
    tk = min(tk, m)
    nk = m // tk
    jps = (n // col_shards) // tj

    def body(a_ref, b_ref, o_ref, acc):
        k = pl.program_id(2)

        @pl.when(k == 0)
        def _():
            acc[...] = jnp.zeros_like(acc)

        acc[...] += lax.dot_general(a_ref[...], b_ref[...], TN, preferred_element_type=f32)

        @pl.when(k == nk - 1)
        def _():
            o_ref[...] = acc[...].astype(bf16).reshape(o_ref.shape)

    if col_shards == 1:
        out_shape, out_spec = SDS((ka, n), bf16), pl.BlockSpec((ti, tj), lambda i, j, k: (i, j))
    else:
        out_shape = SDS((col_shards, ka, n // col_shards), bf16)
        out_spec = pl.BlockSpec((1, ti, tj), lambda i, j, k: (j // jps, i, j % jps))
    return pl.pallas_call(
        body, name=name, grid=(ka // ti, n // tj, nk),
        in_specs=[pl.BlockSpec((tk, ti), lambda i, j, k: (k, i)), pl.BlockSpec((tk, tj), lambda i, j, k: (k, j))],
        out_specs=out_spec, out_shape=out_shape,
        scratch_shapes=[pltpu.VMEM((ti, tj), f32)],
        compiler_params=_cp(("arbitrary", "arbitrary", "arbitrary")),
    )(a, b)


def _dn_consts():
    c = DN_CHUNK
    r, cc = _iota2((c, c))
    lt = (cc <= r).astype(bf16)
    ltt = (r <= cc).astype(bf16)
    return lt, ltt


def dn_chunk(cq, ck, cv, g, beta, z, s, gain, lt, ltt, t_given=None):
    c = DN_CHUNK
    r, cc = _iota2((c, c))
    q = cq * lax.rsqrt(jnp.sum(cq * cq, axis=-1, keepdims=True) + NORM_EPS) * (DN_DIM ** -0.5)
    k = ck * lax.rsqrt(jnp.sum(ck * ck, axis=-1, keepdims=True) + NORM_EPS)
    r2, c2 = _iota2((c, 128))
    uaug = jnp.where((c2 < c) & (r2 > c2), 1.0, 0.0) + jnp.where(c2 == c, 1.0, 0.0)
    gam_all = lmul_const(lt, ltt, g * uaug)
    gam_cc = gam_all[:, :, 0:c]
    gam = gam_all[:, :, c:c + 1]
    dec = jnp.where(cc <= r, jnp.exp(jnp.where(cc <= r, gam_cc, 0.0)), 0.0)
    kk = mm_nt(k, k)
    lm = jnp.where(cc < r, beta * kk * dec, 0.0)
    t = inv_unit_lower(lm) if t_given is None else inv_given(lm, t_given)
    eg = jnp.exp(gam)
    sol = mm_hl(t, jnp.concatenate([cv * beta, k * (beta * eg)], axis=2))
    u, w = sol[:, :, 0:DN_DIM], sol[:, :, DN_DIM:2 * DN_DIM]
    qk = jnp.where(cc <= r, mm_nt(q, k) * dec, 0.0)
    glast = jnp.sum(g, axis=1, keepdims=True)
    qd = q * eg
    kd = k * jnp.exp(glast - gam)
    un = u - mm(w, s)
    o = mm(qd, s) + mm(qk, un)
    s_new = s * jnp.exp(glast) + mm_tn(kd, un)
    on = o * lax.rsqrt(jnp.mean(o * o, axis=-1, keepdims=True) + NORM_EPS) * gain * _silu(z)
    return on, s_new, t


def _dn_gates(ab, al_row, dt_row):
    pre = ab + dt_row
    return -jnp.exp(al_row) * _softplus(pre), _sigmoid(ab), _sigmoid(pre)


def _dn_chains(cacts, gates, z_ref):
    cq, ck, cv, g, beta, z = [], [], [], [], [], []
    for bi, cact in enumerate(cacts):
        for h in range(DN_HEADS):
            cq.append(cact[:, h * DN_DIM:(h + 1) * DN_DIM])
            ck.append(cact[:, DN_WIDTH + h * DN_DIM:DN_WIDTH + (h + 1) * DN_DIM])
            cv.append(cact[:, 2 * DN_WIDTH + h * DN_DIM:2 * DN_WIDTH + (h + 1) * DN_DIM])
            g.append(gates[bi][0][:, h:h + 1])
            beta.append(gates[bi][1][:, DN_HEADS + h:DN_HEADS + h + 1])
            z.append(z_ref[bi, :, h * DN_DIM:(h + 1) * DN_DIM])
    return tuple(jnp.stack(v) for v in (cq, ck, cv, g, beta, z))


def _conv_rows(xe_ref, b, w_ref):
    y = w_ref[0:1, :] * xe_ref[b, pl.ds(5, DN_CHUNK), :]
    for i in range(1, DN_CONV):
        y = y + w_ref[i:i + 1, :] * xe_ref[b, pl.ds(5 + i, DN_CHUNK), :]
    return y


def dn_fwd(qkv, z, ab, conv_w, alog, dtb, gain):
    bsz, t, _ = qkv.shape
    nc = t // DN_CHUNK
    c = DN_CHUNK
    nh = bsz * DN_HEADS

    def body(qkv_ref, z_ref, ab_ref, w_ref, al_ref, dt_ref, g_ref, o_ref, sall_ref, tall_ref, xe, s_sc):
        n = pl.program_id(0)

        @pl.when(n == 0)
        def _():
            xe[:, 0:8, :] = jnp.zeros((bsz, 8, 3 * DN_WIDTH), f32)
            s_sc[...] = jnp.zeros_like(s_sc)

        lt, ltt = _dn_consts()
        cacts = []
        for b in range(bsz):
            xe[b, 8:8 + c, :] = qkv_ref[b]
            cacts.append(_silu(_conv_rows(xe, b, w_ref)))
            xe[b, 0:8, :] = xe[b, c:c + 8, :]
        gates = [_dn_gates(ab_ref[b], al_ref[...], dt_ref[...]) for b in range(bsz)]
        s = s_sc[...]
        sall_ref[0] = s
        on, sn, tt = dn_chunk(*_dn_chains(cacts, gates, z_ref), s, g_ref[...], lt, ltt)
        tall_ref[0] = tt
        s_sc[...] = sn
        for b in range(bsz):
            for h in range(DN_HEADS):
                o_ref[b, :, h * DN_DIM:(h + 1) * DN_DIM] = on[b * DN_HEADS + h].astype(bf16)

    blk = lambda w: pl.BlockSpec((bsz, c, w), lambda n: (0, n, 0))
    full = lambda shp: pl.BlockSpec(shp, lambda n: (0,) * len(shp))
    return pl.pallas_call(
        body, name="dn_fwd", grid=(nc,),
        in_specs=[blk(3 * DN_WIDTH), blk(DN_WIDTH), blk(128), full((8, 3 * DN_WIDTH)), full((1, 128)), full((1, 128)), full((1, 128))],
        out_specs=[blk(DN_WIDTH), pl.BlockSpec((1, nh, DN_DIM, DN_DIM), lambda n: (n, 0, 0, 0)),
                   pl.BlockSpec((1, nh, c, c), lambda n: (n, 0, 0, 0))],
        out_shape=[SDS((bsz, t, D_MODEL), bf16), SDS((nc, nh, DN_DIM, DN_DIM), f32), SDS((nc, nh, c, c), f32)],
        scratch_shapes=[pltpu.VMEM((bsz, c + 8, 3 * DN_WIDTH), f32), pltpu.VMEM((nh, DN_DIM, DN_DIM), f32)],
        compiler_params=_cp(("arbitrary",)),
    )(qkv, z, ab, conv_w, alog, dtb, gain)


def dn_bwd(qkv, z, ab, conv_w, alog, dtb, gain, sall, tall, do):
    bsz, t, _ = qkv.shape
    nc = t // DN_CHUNK
    c = DN_CHUNK
    nh = bsz * DN_HEADS
    w3 = 3 * DN_WIDTH

    def body(qkv_ref, prev_ref, z_ref, ab_ref, w_ref, al_ref, dt_ref, g_ref, sall_ref, tall_ref, do_ref,
             dp_ref, dw_ref, dal_ref, ddt_ref, dg_ref, xe, dye, dc_sc, ds_sc):
        n = pl.program_id(0)
        first = (nc - 1 - n) == 0

        @pl.when(n == 0)
        def _():
            dye[:, c:c + 8, :] = jnp.zeros((bsz, 8, w3), f32)
            ds_sc[...] = jnp.zeros_like(ds_sc)
            dw_ref[...] = jnp.zeros_like(dw_ref)
            dal_ref[...] = jnp.zeros_like(dal_ref)
            ddt_ref[...] = jnp.zeros_like(ddt_ref)
            dg_ref[...] = jnp.zeros_like(dg_ref)

        lt, ltt = _dn_consts()
        lane_c = lax.broadcasted_iota(jnp.int32, (c, 128), 1)
        ys, sigs = [], []
        for b in range(bsz):
            xe[b, 0:8, :] = jnp.where(first, 0.0, prev_ref[b])
            xe[b, 8:8 + c, :] = qkv_ref[b]
            ys.append(_conv_rows(xe, b, w_ref))
            sigs.append(_sigmoid(ys[b]))
        gates = [_dn_gates(ab_ref[b], al_ref[...], dt_ref[...]) for b in range(bsz)]
        ops = _dn_chains([y * sg for y, sg in zip(ys, sigs)], gates, z_ref)
        tt = tall_ref[0]
        _, vjp = jax.vjp(lambda *p: dn_chunk(*p, lt, ltt, t_given=tt)[0:2], *ops, sall_ref[0], g_ref[...])
        don = jnp.stack([do_ref[b, :, h * DN_DIM:(h + 1) * DN_DIM] for b in range(bsz) for h in range(DN_HEADS)])
        dcq, dck, dcv, dg, dbeta, dzz, dsp, dgn = vjp((don, ds_sc[...]))
        ds_sc[...] = dsp
        dg_ref[...] += dgn
        for b in range(bsz):
            dgate = jnp.zeros((c, 128), f32)
            for h in range(DN_HEADS):
                i = b * DN_HEADS + h
                dc_sc[b, :, h * DN_DIM:(h + 1) * DN_DIM] = dcq[i]
                dc_sc[b, :, DN_WIDTH + h * DN_DIM:DN_WIDTH + (h + 1) * DN_DIM] = dck[i]
                dc_sc[b, :, 2 * DN_WIDTH + h * DN_DIM:2 * DN_WIDTH + (h + 1) * DN_DIM] = dcv[i]
                dp_ref[b, :, C_Z + h * DN_DIM:C_Z + (h + 1) * DN_DIM] = dzz[i].astype(bf16)
                dgate = dgate + jnp.where(lane_c == h, dg[i], 0.0) + jnp.where(lane_c == DN_HEADS + h, dbeta[i], 0.0)
            gg, beta, sig_pre = gates[b]
            is_g = lane_c < DN_HEADS
            dpre = jnp.where(is_g, dgate * (-jnp.exp(al_ref[...])) * sig_pre, 0.0)
            dp_ref[b, :, C_AB:C_AB + 128] = (dpre + jnp.where(is_g, 0.0, dgate * beta * (1.0 - beta))).astype(bf16)
            dp_ref[b, :, C_AB + 128:DN_COLS] = jnp.zeros((c, DN_COLS - C_AB - 128), bf16)
            dal_ref[...] += jnp.sum(jnp.where(is_g, dgate * gg, 0.0), axis=0, keepdims=True)
            ddt_ref[...] += jnp.sum(dpre, axis=0, keepdims=True)
            y, sig = ys[b], sigs[b]
            dy = dc_sc[b] * (sig * (1.0 + y * (1.0 - sig)))
            dye[b, 0:c, :] = dy
            dx = w_ref[3:4, :] * dy
            for i in range(DN_CONV - 1):
                dx = dx + w_ref[i:i + 1, :] * dye[b, pl.ds(3 - i, c), :]
            dp_ref[b, :, 0:w3] = dx.astype(bf16)
            for i in range(DN_CONV):
                dw_ref[i:i + 1, :] += jnp.sum(dy * xe[b, pl.ds(5 + i, c), :], axis=0, keepdims=True)
            dye[b, c:c + 8, :] = dye[b, 0:8, :]

    rev = lambda w: pl.BlockSpec((bsz, c, w), lambda n: (0, nc - 1 - n, 0))
    full = lambda shp: pl.BlockSpec(shp, lambda n: (0,) * len(shp))
    prev = pl.BlockSpec((bsz, 8, w3), lambda n: (0, jnp.maximum((nc - 1 - n) * (c // 8) - 1, 0), 0))
    return pl.pallas_call(
        body, name="dn_bwd", grid=(nc,),
        in_specs=[rev(w3), prev, rev(DN_WIDTH), rev(128), full((8, w3)), full((1, 128)), full((1, 128)), full((1, 128)),
                  pl.BlockSpec((1, nh, DN_DIM, DN_DIM), lambda n: (nc - 1 - n, 0, 0, 0)),
                  pl.BlockSpec((1, nh, c, c), lambda n: (nc - 1 - n, 0, 0, 0)), rev(DN_WIDTH)],
        out_specs=[rev(DN_COLS), full((8, w3)), full((1, 128)), full((1, 128)), full((1, 128))],
        out_shape=[SDS((bsz, t, IN_PAD), bf16), SDS((8, w3), f32), SDS((1, 128), f32), SDS((1, 128), f32), SDS((1, 128), f32)],
        scratch_shapes=[pltpu.VMEM((bsz, c + 8, w3), f32), pltpu.VMEM((bsz, c + 8, w3), f32), pltpu.VMEM((bsz, c, w3), f32),
                        pltpu.VMEM((nh, DN_DIM, DN_DIM), f32)],
        compiler_params=_cp(("arbitrary",)),
    )(qkv, qkv, z, ab, conv_w, alog, dtb, gain, sall, tall, do)


SB_TILE = 256
SB_QTILE, SB_KTILE = 256, 256
SB_PAIRS = SB_HEADS // 2


def sb_fwd(sbqkv, gq, gk, mix):
    bsz, t, _ = sbqkv.shape
    bq = min(SB_QTILE, t)
    blk = max(min(SB_KTILE, t), bq)
    nq = t // bq
    scale = SB_DIM ** -0.5

    def body(q_ref, k_ref, v_ref, gq_ref, gk_ref, mix_in, o_ref, l_ref, q2_sc, kn_sc, v_sc):
        bavg = _group_avg_mats()
        lane = lax.broadcasted_iota(jnp.int32, (1, 128), 1)
        first = lane < SB_DIM
        for p in range(SB_PAIRS):
            ls = slice(p * 128, (p + 1) * 128)
            qn = _pair_norm(q_ref[0, :, ls], gq_ref[...], bavg)
            kn_sc[p] = _pair_norm(k_ref[0, :, ls], gk_ref[...], bavg).astype(bf16)
            v_sc[p] = v_ref[0, :, ls].astype(bf16)
            q2_sc[2 * p] = jnp.where(first, qn, 0.0).astype(bf16)
            q2_sc[2 * p + 1] = jnp.where(first, 0.0, qn).astype(bf16)
        r, c = _iota2((blk, blk))
        ustrict = (r > c).astype(bf16)
        r2, c2 = _iota2((2 * bq, blk))

        def tile(q2s, ks, carry, causal):
            out = []
            for p in range(SB_PAIRS):
                acc, rr = carry[2 * p], carry[2 * p + 1]
                zz = lax.dot_general(q2s[p], kn_sc[p, pl.ds(ks, blk), :], NT, preferred_element_type=f32) * scale
                sp = _softplus(zz)
                lm = -sp if causal is None else jnp.where(causal, -sp, 0.0)
                rem = _dot_x2c(lm, ustrict)
                wgt = jnp.exp(zz - sp + rem + rr)
                if causal is not None:
                    wgt = jnp.where(causal, wgt, 0.0)
                out += [acc + _pdot(wgt.astype(bf16), v_sc[p, pl.ds(ks, blk), :]), rr + jnp.sum(lm, axis=1, keepdims=True)]
            return tuple(out)

        def qloop(qi, _):
            qs = pl.multiple_of(qi * bq, bq)
            kd = qs // blk
            causal = c2 < (r2 & (bq - 1)) + (qs - kd * blk)
            q2s = [jnp.concatenate([q2_sc[2 * p, pl.ds(qs, bq), :], q2_sc[2 * p + 1, pl.ds(qs, bq), :]], axis=0)
                   for p in range(SB_PAIRS)]
            zero = (jnp.zeros((2 * bq, 128), f32), jnp.zeros((2 * bq, 1), f32)) * SB_PAIRS
            carry = lax.fori_loop(1, kd + 1, lambda i, cr: tile(q2s, pl.multiple_of((kd - i) * blk, blk), cr, None),
                                  tile(q2s, pl.multiple_of(kd * blk, blk), zero, causal))
            for p in range(SB_PAIRS):
                acc, rr = carry[2 * p], carry[2 * p + 1]
                o_ref[0, pl.ds(qs, bq), p * 128:(p + 1) * 128] = jnp.where(first, acc[0:bq], acc[bq:2 * bq]).astype(bf16)
                l_ref[0, pl.ds(qs, bq), p * 128:(p + 1) * 128] = jnp.where(first, rr[0:bq], rr[bq:2 * bq])
            return 0

        lax.fori_loop(0, nq, qloop, 0)

    col = lambda off: pl.BlockSpec((1, t, SB_WIDTH), lambda b: (b, 0, off))
    gsp = pl.BlockSpec((1, 128), lambda b: (0, 0))
    return pl.pallas_call(
        body, name="sb_fwd", grid=(bsz,),
        in_specs=[col(0), col(1), col(2), gsp, gsp, pl.BlockSpec(memory_space=pl.ANY)],
        out_specs=[col(DN_WIDTH // SB_WIDTH), col(0)],
        out_shape=[SDS(mix.shape, bf16), SDS((bsz, t, SB_WIDTH), f32)],
        input_output_aliases={5: 0},
        scratch_shapes=[pltpu.VMEM((2 * SB_PAIRS, t, 128), bf16), pltpu.VMEM((SB_PAIRS, t, 128), bf16),
                        pltpu.VMEM((SB_PAIRS, t, 128), bf16)],
        compiler_params=_cp(("arbitrary",)),
    )(sbqkv, sbqkv, sbqkv, gq, gk, mix)


def sb_bwd(sbqkv, gq, gk, ltot, do, dproj):
    bsz, t, _ = sbqkv.shape
    blk = min(SB_TILE, t)
    nq = t // blk
    scale = SB_DIM ** -0.5

    def body(q_ref, k_ref, v_ref, gq_ref, gk_ref, l_ref, do_ref, dp_in, dp_ref, dgq_ref, dgk_ref,
             q2_sc, kn_sc, v_sc, do2_sc, dqn_sc, dkn_sc, dv_sc):
        bavg = _group_avg_mats()
        lane = lax.broadcasted_iota(jnp.int32, (1, 128), 1)
        first = lane < SB_DIM
        fq = lambda x, g: _pair_norm(x, g, bavg)
        vjps = []
        for p in range(SB_PAIRS):
            ls = slice(p * 128, (p + 1) * 128)
            qn, q_vjp = jax.vjp(fq, q_ref[0, :, ls], gq_ref[...])
            kn, k_vjp = jax.vjp(fq, k_ref[0, :, ls], gk_ref[...])
            vjps.append((q_vjp, k_vjp))
            kn_sc[p] = kn.astype(bf16)
            v_sc[p] = v_ref[0, :, ls].astype(bf16)
            dov = do_ref[0, :, ls]
            q2_sc[2 * p] = jnp.where(first, qn, 0.0).astype(bf16)
            q2_sc[2 * p + 1] = jnp.where(first, 0.0, qn).astype(bf16)
            do2_sc[2 * p] = jnp.where(first, dov, 0.0).astype(bf16)
            do2_sc[2 * p + 1] = jnp.where(first, 0.0, dov).astype(bf16)
        dkn_sc[...] = jnp.zeros_like(dkn_sc)
        dv_sc[...] = jnp.zeros_like(dv_sc)
        r, c = _iota2((blk, blk))
        pincl = (r <= c).astype(bf16)
        pstrict = (r < c).astype(bf16)
        r2, c2 = _iota2((2 * blk, blk))
        causal = c2 < (r2 & (blk - 1))

        def tile(q2s, do2s, lts, ks, carry, diag):
            out = []
            for p in range(SB_PAIRS):
                dq, cs, ce = carry[3 * p:3 * p + 3]
                q2, do2 = q2s[p], do2s[p]
                kb = kn_sc[p, pl.ds(ks, blk), :]
                zz = lax.dot_general(q2, kb, NT, preferred_element_type=f32) * scale
                sp = _softplus(zz)
                lm = jnp.where(causal, -sp, 0.0) if diag else -sp
                pre = _dot_x2c(lm, pincl)
                lp = zz - sp
                wgt = jnp.exp(lp + (lts[p] - cs - pre))
                if diag:
                    wgt = jnp.where(causal, wgt, 0.0)
                dw = lax.dot_general(do2, v_sc[p, pl.ds(ks, blk), :], NT, preferred_element_type=f32)
                e = wgt * dw
                ee = ce + _dot_x2c(e, pstrict)
                sig = jnp.exp(lp)
                dz = (e * (1.0 - sig) - ee * sig) * scale
                if diag:
                    dz = jnp.where(causal, dz, 0.0)
                dz = dz.astype(bf16)
                dkn_sc[p, pl.ds(ks, blk), :] += lax.dot_general(dz, q2, TN, preferred_element_type=f32)
                dv_sc[p, pl.ds(ks, blk), :] += lax.dot_general(wgt.astype(bf16), do2, TN, preferred_element_type=f32)
                out += [dq + _pdot(dz, kb), cs + jnp.sum(lm, axis=1, keepdims=True), ce + jnp.sum(e, axis=1, keepdims=True)]
            return tuple(out)

        def qloop(qi, _):
            qs = pl.multiple_of(qi * blk, blk)
            rows = pl.ds(qs, blk)
            q2s = [jnp.concatenate([q2_sc[2 * p, rows, :], q2_sc[2 * p + 1, rows, :]], axis=0) for p in range(SB_PAIRS)]
            do2s = [jnp.concatenate([do2_sc[2 * p, rows, :], do2_sc[2 * p + 1, rows, :]], axis=0) for p in range(SB_PAIRS)]
            lts = [jnp.concatenate([l_ref[0, rows, p * 128:p * 128 + 1], l_ref[0, rows, p * 128 + SB_DIM:p * 128 + SB_DIM + 1]],
                                   axis=0) for p in range(SB_PAIRS)]
            z1 = jnp.zeros((2 * blk, 1), f32)
            carry = lax.fori_loop(0, qi, lambda kj, cr: tile(q2s, do2s, lts, pl.multiple_of(kj * blk, blk), cr, False),
                                  (jnp.zeros((2 * blk, 128), f32), z1, z1) * SB_PAIRS)
            carry = tile(q2s, do2s, lts, qs, carry, True)
            for p in range(SB_PAIRS):
                dq = carry[3 * p]
                dqn_sc[p, rows, :] = jnp.where(first, dq[0:blk], dq[blk:2 * blk])
            return 0

        lax.fori_loop(0, nq, qloop, 0)
        dgq_tot, dgk_tot = jnp.zeros((1, 128), f32), jnp.zeros((1, 128), f32)
        for p in range(SB_PAIRS):
            ls = slice(p * 128, (p + 1) * 128)
            dq_pre, dgq = vjps[p][0](dqn_sc[p])
            dk_pre, dgk = vjps[p][1](dkn_sc[p])
            dp_ref[0, :, p * 128:(p + 1) * 128] = dq_pre.astype(bf16)
            dp_ref[0, :, SB_WIDTH + p * 128:SB_WIDTH + (p + 1) * 128] = dk_pre.astype(bf16)
            dp_ref[0, :, 2 * SB_WIDTH + p * 128:2 * SB_WIDTH + (p + 1) * 128] = dv_sc[p].astype(bf16)
            dgq_tot, dgk_tot = dgq_tot + dgq, dgk_tot + dgk
        dgq_ref[0] = jnp.broadcast_to(dgq_tot, (8, 128))
        dgk_ref[0] = jnp.broadcast_to(dgk_tot, (8, 128))

    col = lambda off: pl.BlockSpec((1, t, SB_WIDTH), lambda b: (b, 0, off), pipeline_mode=pl.Buffered(1))
    gsp = pl.BlockSpec((1, 128), lambda b: (0, 0))
    gout = pl.BlockSpec((1, 8, 128), lambda b: (b, 0, 0))
    return pl.pallas_call(
        body, name="sb_bwd", grid=(bsz,),
        in_specs=[col(0), col(1), col(2), gsp, gsp, col(0), col(0), pl.BlockSpec(memory_space=pl.ANY)],
        out_specs=[pl.BlockSpec((1, t, 3 * SB_WIDTH), lambda b: (b, 0, C_SB // (3 * SB_WIDTH)), pipeline_mode=pl.Buffered(1)),
                   gout, gout],
        out_shape=[SDS(dproj.shape, bf16)] + [SDS((bsz, 8, 128), f32)] * 2,
        input_output_aliases={7: 0},
        scratch_shapes=[pltpu.VMEM((2 * SB_PAIRS, t, 128), bf16), pltpu.VMEM((SB_PAIRS, t, 128), bf16),
                        pltpu.VMEM((SB_PAIRS, t, 128), bf16), pltpu.VMEM((2 * SB_PAIRS, t, 128), bf16),
                        pltpu.VMEM((SB_PAIRS, t, 128), f32), pltpu.VMEM((SB_PAIRS, t, 128), f32), pltpu.VMEM((SB_PAIRS, t, 128), f32)],
        compiler_params=_cp(("arbitrary",)),
    )(sbqkv, sbqkv, sbqkv, gq, gk, ltot, do, dproj)


SG_STEP = 512


def sg_pair(u, v, gain, wa, wb, ba, bb, bavg):
    r, c = _iota2((SG_CHUNK, SG_CHUNK))
    lane = lax.broadcasted_iota(jnp.int32, (1, 128), 1)
    first = lane < SG_DIM
    vn = _pair_norm(_gelu(v), gain, bavg)
    tri = c <= r
    mixed = (mm(jnp.where(tri, wa, 0.0), jnp.where(first, vn, 0.0)) + mm(jnp.where(tri, wb, 0.0), jnp.where(first, 0.0, vn))
             + jnp.where(first, ba, bb))
    return _gelu(u) * mixed


def sg_fwd(sguv, gain, w, bt, mix):
    bsz, t, _ = sguv.shape
    rows = min(SG_STEP, t)

    def body(uv_ref, g_ref, w_ref, b_ref, mix_in, o_ref):
        bavg = _group_avg_mats()
        for r0 in range(0, rows, SG_CHUNK):
            rs = slice(r0, r0 + SG_CHUNK)
            for p in range(2):
                ls = slice(p * 128, (p + 1) * 128)
                o_ref[0, rs, ls] = sg_pair(uv_ref[0, rs, ls], uv_ref[0, rs, SG_WIDTH + p * 128:SG_WIDTH + (p + 1) * 128], g_ref[:, ls],
                                           w_ref[2 * p], w_ref[2 * p + 1], b_ref[:, 2 * p:2 * p + 1], b_ref[:, 2 * p + 1:2 * p + 2],
                                           bavg).astype(bf16)

    full = lambda shp: pl.BlockSpec(shp, lambda b, n: (0,) * len(shp))
    return pl.pallas_call(
        body, name="sg_fwd", grid=(bsz, t // rows),
        in_specs=[pl.BlockSpec((1, rows, 2 * SG_WIDTH), lambda b, n: (b, n, 0)), full((1, SG_WIDTH)),
                  full((SG_GROUPS, SG_CHUNK, SG_CHUNK)), full((SG_CHUNK, 128)), pl.BlockSpec(memory_space=pl.ANY)],
        out_specs=pl.BlockSpec((1, rows, SG_WIDTH), lambda b, n: (b, n, (DN_WIDTH + SB_WIDTH) // SG_WIDTH)),
        out_shape=SDS(mix.shape, bf16), input_output_aliases={4: 0},
        compiler_params=_cp(("arbitrary", "arbitrary")),
    )(sguv, gain, w, bt, mix)


def sg_bwd(sguv, gain, w, bt, do, dproj):
    bsz, t, _ = sguv.shape
    rows = min(SG_STEP, t)

    def body(uv_ref, g_ref, w_ref, b_ref, do_ref, dp_in, duv_ref, dg_ref, dw_ref, db_ref):
        @pl.when((pl.program_id(0) == 0) & (pl.program_id(1) == 0))
        def _():
            dg_ref[...] = jnp.zeros_like(dg_ref)
            dw_ref[...] = jnp.zeros_like(dw_ref)
            db_ref[...] = jnp.zeros_like(db_ref)

        bavg = _group_avg_mats()
        lane = lax.broadcasted_iota(jnp.int32, (SG_CHUNK, 128), 1)
        dbt = jnp.zeros((SG_CHUNK, 128), f32)
        dgs, dws = [jnp.zeros((1, 128), f32)] * 2, [jnp.zeros((SG_CHUNK, SG_CHUNK), f32)] * SG_GROUPS
        for r0 in range(0, rows, SG_CHUNK):
            rs = slice(r0, r0 + SG_CHUNK)
            for p in range(2):
                ls = slice(p * 128, (p + 1) * 128)
                vs = slice(SG_WIDTH + p * 128, SG_WIDTH + (p + 1) * 128)
                prim = (uv_ref[0, rs, ls], uv_ref[0, rs, vs], g_ref[:, ls], w_ref[2 * p], w_ref[2 * p + 1],
                        b_ref[:, 2 * p:2 * p + 1], b_ref[:, 2 * p + 1:2 * p + 2])
                _, vjp = jax.vjp(lambda *a: sg_pair(*a, bavg), *prim)
                du, dv, dgn, dwa, dwb, dba, dbb = vjp(do_ref[0, rs, ls])
                duv_ref[0, rs, ls] = du.astype(bf16)
                duv_ref[0, rs, vs] = dv.astype(bf16)
                dgs[p] = dgs[p] + dgn
                dws[2 * p], dws[2 * p + 1] = dws[2 * p] + dwa, dws[2 * p + 1] + dwb
                dbt = dbt + jnp.where(lane == 2 * p, dba, 0.0) + jnp.where(lane == 2 * p + 1, dbb, 0.0)
        for p in range(2):
            dg_ref[:, p * 128:(p + 1) * 128] += dgs[p]
        for gidx in range(SG_GROUPS):
            dw_ref[gidx] += dws[gidx]
        db_ref[...] += dbt

    full = lambda shp: pl.BlockSpec(shp, lambda b, n: (0,) * len(shp))
    return pl.pallas_call(
        body, name="sg_bwd", grid=(bsz, t // rows),
        in_specs=[pl.BlockSpec((1, rows, 2 * SG_WIDTH), lambda b, n: (b, n, 0)), full((1, SG_WIDTH)),
                  full((SG_GROUPS, SG_CHUNK, SG_CHUNK)), full((SG_CHUNK, 128)),
                  pl.BlockSpec((1, rows, SG_WIDTH), lambda b, n: (b, n, 0)), pl.BlockSpec(memory_space=pl.ANY)],
        out_specs=[pl.BlockSpec((1, rows, 2 * SG_WIDTH), lambda b, n: (b, n, C_SG // (2 * SG_WIDTH))), full((1, SG_WIDTH)),
                   full((SG_GROUPS, SG_CHUNK, SG_CHUNK)), full((SG_CHUNK, 128))],
        out_shape=[SDS(dproj.shape, bf16), SDS((1, SG_WIDTH), f32), SDS((SG_GROUPS, SG_CHUNK, SG_CHUNK), f32),
                   SDS((SG_CHUNK, 128), f32)],
        input_output_aliases={5: 0},
        compiler_params=_cp(("arbitrary", "arbitrary")),
    )(sguv, gain, w, bt, do, dproj)


def _pad_lanes(v, n=128):
    return jnp.pad(v.reshape(1, -1), ((0, 0), (0, n - v.size)))


def _w_in_runs():
    shard, runs = IN_DIM // N_CHIPS, []
    for s in range(N_CHIPS):
        for a, b, d in ((0, 2048, 0), (2048, 2056, C_AB), (2056, IN_DIM, C_SB)):
            lo, hi = max(shard * s, a), min(shard * (s + 1), b)
            if lo < hi:
                runs.append((s, lo - shard * s, hi - shard * s, d + lo - a))
    return runs


def w_in_from_shards(zone, tr=256):
    def body(z_ref, o_ref):
        o_ref[:, C_AB:C_SB] = jnp.zeros((tr, C_SB - C_AB), zone.dtype)
        for s, a, b, d in _w_in_runs():
            o_ref[:, d:d + b - a] = z_ref[s, :, a:b]

    return pl.pallas_call(
        body, name="w_in_from_shards", grid=(D_MODEL // tr,),
        in_specs=[pl.BlockSpec((N_CHIPS, tr, IN_DIM // N_CHIPS), lambda i: (0, i, 0))],
        out_specs=pl.BlockSpec((tr, IN_PAD), lambda i: (i, 0)), out_shape=SDS((D_MODEL, IN_PAD), zone.dtype),
        compiler_params=_cp(("arbitrary",)))(zone)


def w_in_grad_to_shards(g, tr=256):
    def body(g_ref, o_ref):
        for s, a, b, d in _w_in_runs():
            o_ref[s, :, a:b] = g_ref[:, d:d + b - a]

    return pl.pallas_call(
        body, name="w_in_grad_to_shards", grid=(D_MODEL // tr,),
        in_specs=[pl.BlockSpec((tr, IN_PAD), lambda i: (i, 0))],
        out_specs=pl.BlockSpec((N_CHIPS, tr, IN_DIM // N_CHIPS), lambda i: (0, i, 0)),
        out_shape=SDS((N_CHIPS, D_MODEL, IN_DIM // N_CHIPS), g.dtype), compiler_params=_cp(("arbitrary",)))(g)


def layer_params(p, l):
    return dict(
        g1=p["norm1_g"][l].reshape(1, -1), g2=p["norm2_g"][l].reshape(1, -1),
        conv=jnp.pad(p["conv_w"][l], ((0, 4), (0, 0))), alog=_pad_lanes(p["a_log"][l]), dtb=_pad_lanes(p["dt_bias"][l]),
        dng=p["dn_out_g"][l].reshape(1, -1), gq=jnp.tile(p["sb_q_g"][l].reshape(1, -1), (1, 2)),
        gk=jnp.tile(p["sb_k_g"][l].reshape(1, -1), (1, 2)), sgg=p["sg_v_g"][l].reshape(1, -1), sgw=p["sg_w"][l],
        sgb=jnp.pad(p["sg_b"][l].T, ((0, 0), (0, 124))))


def local_step(x, tgt, small, get_w, put_g, sync_g):
    bsz, t, _ = x.shape
    m = bsz * t
    r3 = lambda a: a.reshape(bsz, t, a.shape[-1])
    r2 = lambda a: a.reshape(m, a.shape[-1])
    xs, saved, ws = x.reshape(m, D_MODEL), [], []
    for l in range(DEPTH):
        sp, w = layer_params(small, l), {}
        w["w_in"] = get_w(l, "in", xs)
        qkv, z, ab, sb, sg, h1 = inproj_fwd(xs, sp["g1"], w["w_in"])
        mix, sall, tall = dn_fwd(r3(qkv), r3(z), r3(ab), sp["conv"], sp["alog"], sp["dtb"], sp["dng"])
        mix, ltot = sb_fwd(r3(sb), sp["gq"], sp["gk"], mix)
        mix = r2(sg_fwd(r3(sg), sp["sgg"], sp["sgw"], sp["sgb"], mix))
        w["w_out"] = get_w(l, "out", mix)
        x2 = outproj_fwd(xs, mix, w["w_out"])
        w["w_ff1"], w["w_ff2"], started = get_w(l, "ff", x2)
        if l + 1 < DEPTH:
            xs_next, rlb = ffn_fwd(x2, sp["g2"] + started, w["w_ff1"], w["w_ff2"])
        else:
            dx, rlb, lossp = ffn_fwd(x2, sp["g2"] + started, w["w_ff1"], w["w_ff2"], tgt=tgt.reshape(m, D_MODEL))
        saved.append(dict(rlb=rlb, h1=h1, x=xs, qkv=qkv, z=z, ab=ab, sb=sb, sg=sg, sall=sall, tall=tall, ltot=ltot, mix=mix, x2=x2))
        ws.append(w)
        xs = xs_next
    gsmall = [None] * DEPTH
    token = jnp.zeros((), f32)
    for l in reversed(range(DEPTH)):
        sp, w, s = layer_params(small, l), ws[l], saved[l]
        dx2, dg2, h2, act, df, dyb = ffn_bwd(s["x2"], sp["g2"] + token, w["w_ff1"], w["w_ff2"], s["rlb"], dx)
        g_ff1 = tn_matmul(h2, df, f"dw_ff1_{l}", col_shards=N_CHIPS)
        g_ff2 = tn_matmul(act, dyb, f"dw_ff2_{l}")
        dodn, dosb, dosg, dx2b = outproj_bwd(dx2, w["w_out"])
        g_out = tn_matmul(s["mix"], dx2b, f"dw_out_{l}")
        token = token + put_g(l, "rest", dict(w_out=g_out, w_ff1=g_ff1, w_ff2=g_ff2))
        dproj, dconv, dalog, ddtb, ddng = dn_bwd(r3(s["qkv"]), r3(s["z"]), r3(s["ab"]), sp["conv"], sp["alog"], sp["dtb"],
                                                 sp["dng"] + token, s["sall"], s["tall"], r3(dodn))
        token = sync_g(ddng)
        dproj, dgq, dgk = sb_bwd(r3(s["sb"]), sp["gq"] + token, sp["gk"], s["ltot"], r3(dosb), dproj)
        dproj, dsgg, dsgw, dsgb = sg_bwd(r3(s["sg"]), sp["sgg"], sp["sgw"], sp["sgb"], r3(dosg), dproj)
        dproj = r2(dproj)
        g_in = tn_matmul(s["h1"], dproj, f"dw_in_{l}")
        token = put_g(l, "in", dict(w_in=g_in))
        dx, dg1 = inproj_bwd(s["x"], sp["g1"] + token, w["w_in"], dproj, dx2)
        token = sync_g(dg1)
        fold = lambda a: (a[:, 0, :].sum(0).reshape(2, SB_DIM)).sum(0)
        gsmall[l] = dict(norm1_g=dg1[0], conv_w=dconv[0:DN_CONV], a_log=dalog[0, 0:DN_HEADS], dt_bias=ddtb[0, 0:DN_HEADS],
                         dn_out_g=ddng[0], sb_q_g=fold(dgq), sb_k_g=fold(dgk), sg_v_g=dsgg[0], sg_w=dsgw,
                         sg_b=dsgb[:, 0:SG_GROUPS].T, norm2_g=dg2[0])
    return lossp, dx.reshape(bsz, t, D_MODEL), gsmall


def _chip_peers(x, y):
    return [(1 - x, y), (x, 1 - y), (1 - x, 1 - y)]


_HBM = pl.BlockSpec(memory_space=pltpu.HBM)
_SEM = pl.BlockSpec(memory_space=pltpu.SEMAPHORE)
_EFFECT = pltpu.SideEffectType.DATAFLOW_SIDE_EFFECTING


def _hbm(a):
    return pltpu.with_memory_space_constraint(a, pltpu.HBM)


def _my_half(ref):
    half = ref.shape[0] // 2
    return ref.at[pl.ds(pl.multiple_of(lax.axis_index("c") * half, 8), half)]


def _exchange_copy(src, land, k, j, send, recv, scatter, halve, waiting):
    x, y, c = lax.axis_index("x"), lax.axis_index("y"), lax.axis_index("c")
    px, py = _chip_peers(x, y)[j]
    me, peer = 2 * x + y, 2 * px + py
    if scatter:
        src = src.at[me if waiting else peer]
    dst = land.at[peer if waiting else me]
    if halve:
        src, dst = _my_half(src), _my_half(dst)
    return pltpu.make_async_remote_copy(src_ref=src, dst_ref=dst, send_sem=send.at[3 * k + j],
                                        recv_sem=recv.at[3 * k + j], device_id=(px, py, c), device_id_type=MESH)


def exchange_start(items, name, scatter, after=None):
    arrs = []
    for a, _, _ in items:
        if not any(a is b for b in arrs):
            arrs.append(a)
    pos = [next(i for i, b in enumerate(arrs) if b is a) for a, _, _ in items]
    shapes = [a.shape if idx is None else a.shape[1:] for a, idx, _ in items]
    lands = [lax.empty(s if scatter else (N_CHIPS,) + s, a.dtype) for (a, _, _), s in zip(items, shapes)]
    na, nl = len(arrs), len(lands)
    n_in = na + nl + (after is not None)

    def body(*refs):
        ins, lnd = refs[:na], refs[na:na + nl]
        send, recv = refs[n_in], refs[n_in + 1]
        token = refs[-1]
        for k, (_, idx, halve) in enumerate(items):
            src = ins[pos[k]] if idx is None else ins[pos[k]].at[idx]
            for j in range(3):
                _exchange_copy(src, lnd[k], k, j, send, recv, scatter, halve, False).start()
        token[...] = jnp.zeros_like(token)

    sems = pltpu.SemaphoreType.DMA((3 * nl,))
    extra = [] if after is None else [after]
    out = pl.pallas_call(
        body, name=name,
        out_shape=(sems, sems, *[pltpu.HBM(a.shape, a.dtype) for a in arrs + lands], SDS((8, 128), f32)),
        in_specs=[_HBM] * (na + nl) + [pl.BlockSpec(memory_space=pl.ANY)] * len(extra),
        out_specs=(_SEM, _SEM, *[_HBM] * (na + nl), pl.BlockSpec(memory_space=pltpu.VMEM)),
        input_output_aliases={i: 2 + i for i in range(na + nl)},
        compiler_params=pltpu.CompilerParams(has_side_effects=_EFFECT),
    )(*[_hbm(a) for a in arrs + lands], *extra)
    thru = out[2:2 + na]
    return dict(send=out[0], recv=out[1], src=[(thru[pos[k]], idx) for k, (_, idx, _) in enumerate(items)],
                halve=[h for _, _, h in items], land=list(out[2 + na:2 + na + nl]), token=out[-1], scatter=scatter)


def exchange_wait(st, ks, after, name):
    arrs = []
    for k in ks:
        if not any(st["src"][k][0] is b for b in arrs):
            arrs.append(st["src"][k][0])
    pos = [next(i for i, b in enumerate(arrs) if b is st["src"][k][0]) for k in ks]
    lands = [st["land"][k] for k in ks]
    na, nl = len(arrs), len(lands)

    def body(*refs):
        ins, lnd = refs[:na], refs[na:na + nl]
        send, recv = refs[na + nl], refs[na + nl + 1]
        for t, k in enumerate(ks):
            idx = st["src"][k][1]
            src = ins[pos[t]] if idx is None else ins[pos[t]].at[idx]
            for j in range(3):
                cp = _exchange_copy(src, lnd[t], k, j, send, recv, st["scatter"], st["halve"][k], True)
                cp.wait_send()
                cp.wait_recv()

    out = pl.pallas_call(
        body, name=name, out_shape=tuple(pltpu.HBM(a.shape, a.dtype) for a in arrs + lands),
        in_specs=[_HBM] * (na + nl) + [_SEM, _SEM, pl.BlockSpec(memory_space=pl.ANY)], out_specs=tuple([_HBM] * (na + nl)),
        input_output_aliases={i: i for i in range(na + nl)},
        compiler_params=pltpu.CompilerParams(has_side_effects=_EFFECT),
    )(*arrs, *lands, st["send"], st["recv"], after)
    for k, (a, idx) in enumerate(st["src"]):
        for p, b in enumerate(arrs):
            if a is b:
                st["src"][k] = (out[p], idx)
    return list(out[na:na + nl])


def _sibling_copy(src, land, i, send, recv, other_half):
    x, y, c = lax.axis_index("x"), lax.axis_index("y"), lax.axis_index("c")
    return pltpu.make_async_remote_copy(src_ref=src.at[:, 1 - c] if other_half else src, dst_ref=land, send_sem=send.at[i],
                                        recv_sem=recv.at[i], device_id=(x, y, 1 - c), device_id_type=MESH)


def sibling_start(arrs, name, other_half=False):
    n = len(arrs)
    lands = [lax.empty((a.shape[0],) + a.shape[2:] if other_half else a.shape, a.dtype) for a in arrs]

    def body(*refs):
        ins, lnd = refs[:n], refs[n:2 * n]
        send, recv = refs[2 * n], refs[2 * n + 1]
        token = refs[-1]
        for i in range(n):
            _sibling_copy(ins[i], lnd[i], i, send, recv, other_half).start()
        token[...] = jnp.zeros_like(token)

    sems = pltpu.SemaphoreType.DMA((n,))
    out = pl.pallas_call(
        body, name=name,
        out_shape=(sems, sems, *[pltpu.HBM(a.shape, a.dtype) for a in arrs + lands], SDS((8, 128), f32)),
        in_specs=[_HBM] * (2 * n), out_specs=(_SEM, _SEM, *[_HBM] * (2 * n), pl.BlockSpec(memory_space=pltpu.VMEM)),
        input_output_aliases={i: 2 + i for i in range(2 * n)},
        compiler_params=pltpu.CompilerParams(has_side_effects=_EFFECT),
    )(*[_hbm(a) for a in arrs + lands])
    return dict(send=out[0], recv=out[1], src=list(out[2:2 + n]), land=list(out[2 + n:2 + 2 * n]), token=out[-1],
                other_half=other_half)


def sibling_wait(st, after, name):
    n = len(st["src"])

    def body(*refs):
        ins, lnd = refs[:n], refs[n:2 * n]
        send, recv = refs[2 * n], refs[2 * n + 1]
        for i in range(n):
            cp = _sibling_copy(ins[i], lnd[i], i, send, recv, st["other_half"])
            cp.wait_send()
            cp.wait_recv()

    out = pl.pallas_call(
        body, name=name, out_shape=tuple(pltpu.HBM(a.shape, a.dtype) for a in st["src"] + st["land"]),
        in_specs=[_HBM] * (2 * n) + [_SEM, _SEM, pl.BlockSpec(memory_space=pl.ANY)], out_specs=tuple([_HBM] * (2 * n)),
        input_output_aliases={i: i for i in range(2 * n)},
        compiler_params=pltpu.CompilerParams(has_side_effects=_EFFECT),
    )(*st["src"], *st["land"], st["send"], st["recv"], after)
    return list(out[:n]), list(out[n:])


def swap_halves(zones, name):
    n = len(zones)

    def body(*refs):
        outs = refs[n:2 * n]
        send, recv = refs[2 * n:]
        x, y, c = lax.axis_index("x"), lax.axis_index("y"), lax.axis_index("c")
        cps = []
        for i in range(n):
            for j, (px, py) in enumerate(_chip_peers(x, y)):
                part = _my_half(outs[i].at[2 * px + py])
                cps.append(pltpu.make_async_remote_copy(src_ref=part, dst_ref=part, send_sem=send.at[3 * i + j],
                                                        recv_sem=recv.at[3 * i + j], device_id=(x, y, 1 - c), device_id_type=MESH))
        for cp in cps:
            cp.start()
        for cp in cps:
            cp.wait_send()
            cp.wait_recv()

    any_spec = pl.BlockSpec(memory_space=pl.ANY)
    return pl.pallas_call(
        body, name=name, in_specs=[any_spec] * n, out_specs=[any_spec] * n, out_shape=[SDS(a.shape, a.dtype) for a in zones],
        input_output_aliases={i: i for i in range(n)},
        scratch_shapes=[pltpu.SemaphoreType.DMA((3 * n,)), pltpu.SemaphoreType.DMA((3 * n,))],
    )(*zones)


def _ids_spec(grid, in_specs, out_specs):
    return pltpu.PrefetchScalarGridSpec(num_scalar_prefetch=1, grid=grid, in_specs=in_specs, out_specs=out_specs)


def pair_sum(ids, a, b, name, tr=512):
    nd, _, rows, cols = a.shape
    tr = min(tr, rows)
    assert rows % tr == 0

    def body(ids_ref, a_ref, b_ref, o_ref):
        o_ref[...] = (a_ref[0].astype(f32) + b_ref[...].astype(f32)).astype(bf16)

    spec = pl.BlockSpec((1, tr, cols), lambda d, i, ids: (d, i, 0))
    return pl.pallas_call(
        body, name=name,
        grid_spec=_ids_spec((nd, rows // tr), [pl.BlockSpec((1, 1, tr, cols), lambda d, i, ids: (d, ids[1], i, 0)), spec], spec),
        out_shape=SDS((nd, rows, cols), bf16), compiler_params=_cp(("arbitrary", "arbitrary")))(ids, a, b)


def allreduce_small(v):
    half = v.shape[0] // 2
    assert half % 8 == 0

    def body(v_ref, o_ref, rbuf, send, recv):
        x, y, c = lax.axis_index("x"), lax.axis_index("y"), lax.axis_index("c")
        o_ref[...] = v_ref[...]

        def exchange(rows, peer, k):
            return pltpu.make_async_remote_copy(src_ref=o_ref.at[rows], dst_ref=rbuf.at[k, rows], send_sem=send.at[k],
                                                recv_sem=recv.at[k], device_id=peer, device_id_type=MESH)

        lo, hi, across_x, across_y = pl.ds(0, half), pl.ds(half, half), (1 - x, y, c), (x, 1 - y, c)
        stages = [[(pl.ds(0, 2 * half), (x, y, 1 - c))], [(lo, across_x), (hi, across_y)], [(lo, across_y), (hi, across_x)]]
        k = 0
        for stage in stages:
            cps = [exchange(rows, peer, k + i) for i, (rows, peer) in enumerate(stage)]
            for cp in cps:
                cp.start()
            for cp in cps:
                cp.wait()
            for i, (rows, _) in enumerate(stage):
                o_ref[rows] = o_ref[rows] + rbuf[k + i, rows]
            k += len(stage)

    vm = pl.BlockSpec(memory_space=pltpu.VMEM)
    return pl.pallas_call(
        body, name="allreduce_small", in_specs=[vm], out_specs=vm, out_shape=SDS(v.shape, f32),
        scratch_shapes=[pltpu.VMEM((5,) + v.shape, f32), pltpu.SemaphoreType.DMA((5,)), pltpu.SemaphoreType.DMA((5,))],
        compiler_params=_cp(),
    )(v)


def sum_partials(ids, zone, mine, name, tr=256):
    _, rows, cols = zone.shape
    tr = min(tr, rows)
    assert rows % tr == 0

    def body(ids_ref, m_ref, z1_ref, z2_ref, z3_ref, o_ref):
        o_ref[...] = ((m_ref[0].astype(f32) + z1_ref[0].astype(f32)) + z2_ref[0].astype(f32)) + z3_ref[0].astype(f32)

    slot = lambda flip: pl.BlockSpec((1, tr, cols), lambda i, ids: (ids[0] ^ flip, i, 0))
    return pl.pallas_call(
        body, name=name,
        grid_spec=_ids_spec((rows // tr,), [slot(0), slot(1), slot(2), slot(3)], pl.BlockSpec((tr, cols), lambda i, ids: (i, 0))),
        out_shape=SDS((rows, cols), f32), compiler_params=_cp(("arbitrary",)),
    )(ids, mine, zone, zone, zone)


def adamw(w, m, v, gs, name, layer=0, prev=None, tr=256):
    hrows, cols = gs[0].shape
    rows = hrows * len(gs)
    tr = min(tr, hrows)
    assert hrows % tr == 0 and w.shape[0] % rows == 0
    off, nth = layer * (rows // tr), hrows // tr

    def body(w_ref, m_ref, v_ref, *rest):
        g_ref, d_ref, mo_ref, vo_ref = rest[-4:]
        if len(gs) == 1:
            g = rest[0][...]
        else:
            g = jnp.where(pl.program_id(0) // nth == lax.axis_index("c"), rest[0][...], rest[1][...])
        mn = ADAM_B1 * m_ref[...] + (1.0 - ADAM_B1) * g
        vn = ADAM_B2 * v_ref[...] + (1.0 - ADAM_B2) * jnp.square(g)
        m_hat = mn / (1.0 - ADAM_B1 ** ADAM_STEP)
        v_hat = vn / (1.0 - ADAM_B2 ** ADAM_STEP)
        g_ref[...] = g
        d_ref[...] = -ADAM_LR * (m_hat / (jnp.sqrt(v_hat) + ADAM_EPS) + ADAM_WD * w_ref[...])
        mo_ref[...] = mn
        vo_ref[...] = vn

    loc = pl.BlockSpec((tr, cols), lambda i: (i % nth, 0))
    glob = pl.BlockSpec((tr, cols), lambda i: (off + i, 0))
    extra = [] if prev is None else list(prev)
    return pl.pallas_call(
        body, name=name, grid=(rows // tr,),
        in_specs=[glob] * 3 + [loc] * len(gs) + [pl.BlockSpec(memory_space=pl.ANY)] * len(extra),
        out_specs=[glob] * 4, out_shape=[SDS(w.shape, f32)] * 4,
        input_output_aliases={3 + len(gs) + j: j for j in range(len(extra))},
        compiler_params=_cp(("arbitrary",)),
    )(w, m, v, *gs, *extra)


BIG = ("w_in", "w_out", "w_ff1", "w_ff2")
SMALL = ("norm1_g", "conv_w", "a_log", "dt_bias", "dn_out_g", "sb_q_g", "sb_k_g", "sg_v_g", "sg_w", "sg_b", "norm2_g")
WEIGHTS = ("norm1_g", "w_in", "conv_w", "a_log", "dt_bias", "dn_out_g", "sb_q_g", "sb_k_g", "sg_v_g", "sg_w", "sg_b",
           "w_out", "norm2_g", "w_ff1", "w_ff2")


PACK_ROWS = 256


def _rows_of(shape):
    n = 1
    for d in shape:
        n *= d
    return -(-n // 1024) * 8, n


def _pack(arrs):
    parts = []
    for a in arrs:
        r, n = _rows_of(a.shape)
        parts.append(jnp.pad(a.reshape(-1), (0, r * 128 - n)).reshape(r, 128))
    rows = sum(p.shape[0] for p in parts)
    parts.append(jnp.zeros((-rows % PACK_ROWS, 128), arrs[0].dtype))
    return jnp.concatenate(parts, axis=0)


def _unpack(packed, shapes):
    out, o = [], 0
    for s in shapes:
        r, n = _rows_of(s)
        out.append(packed[o:o + r].reshape(-1)[0:n].reshape(s))
        o += r
    return out


def kernel(x, norm1_g, w_in, conv_w, a_log, dt_bias, dn_out_g, sb_q_g, sb_k_g, sg_v_g, sg_w, sg_b, w_out, norm2_g, w_ff1, w_ff2, loss_target, m_norm1_g, m_w_in, m_conv_w, m_a_log, m_dt_bias, m_dn_out_g, m_sb_q_g, m_sb_k_g, m_sg_v_g, m_sg_w, m_sg_b, m_w_out, m_norm2_g, m_w_ff1, m_w_ff2, v_norm1_g, v_w_in, v_conv_w, v_a_log, v_dt_bias, v_dn_out_g, v_sb_q_g, v_sb_k_g, v_sg_v_g, v_sg_w, v_sg_b, v_w_out, v_norm2_g, v_w_ff1, v_w_ff2):
    w = dict(norm1_g=norm1_g, w_in=w_in, conv_w=conv_w, a_log=a_log, dt_bias=dt_bias, dn_out_g=dn_out_g, sb_q_g=sb_q_g,
             sb_k_g=sb_k_g, sg_v_g=sg_v_g, sg_w=sg_w, sg_b=sg_b, w_out=w_out, norm2_g=norm2_g, w_ff1=w_ff1, w_ff2=w_ff2)
    mom = dict(norm1_g=m_norm1_g, w_in=m_w_in, conv_w=m_conv_w, a_log=m_a_log, dt_bias=m_dt_bias, dn_out_g=m_dn_out_g,
               sb_q_g=m_sb_q_g, sb_k_g=m_sb_k_g, sg_v_g=m_sg_v_g, sg_w=m_sg_w, sg_b=m_sg_b, w_out=m_w_out, norm2_g=m_norm2_g,
               w_ff1=m_w_ff1, w_ff2=m_w_ff2)
    var = dict(norm1_g=v_norm1_g, w_in=v_w_in, conv_w=v_conv_w, a_log=v_a_log, dt_bias=v_dt_bias, dn_out_g=v_dn_out_g,
               sb_q_g=v_sb_q_g, sb_k_g=v_sb_k_g, sg_v_g=v_sg_v_g, sg_w=v_sg_w, sg_b=v_sg_b, w_out=v_w_out, norm2_g=v_norm2_g,
               w_ff1=v_w_ff1, w_ff2=v_w_ff2)
    chip = 2 * lax.axis_index("x") + lax.axis_index("y")

    wb = [{k: w[k][l].astype(bf16) for k in BIG} for l in range(DEPTH)]
    ags = {0: exchange_start([(conv_w, None, False)] + [(wb[0][k], None, True) for k in BIG], "allgather_start_0", scatter=False)}
    item = lambda l, k: (l, (l == 0) + BIG.index(k))

    def landed(items, after, name):
        ag, ks = ags[items[0][0]], [k for _, k in items]
        zones = exchange_wait(ag, ks, after, name)
        halved = [t for t, k in enumerate(ks) if ag["halve"][k]]
        for t, z in zip(halved, swap_halves([zones[t] for t in halved], name.replace("wait", "pass"))):
            zones[t] = z
        return [lax.dynamic_update_slice_in_dim(z, ag["src"][k][0][None], chip, axis=0) for z, k in zip(zones, ks)]

    def whole(k, z):
        if k == "w_in":
            return w_in_from_shards(z)
        return z if k == "w_ff1" else z.reshape(-1, D_MODEL)

    g_conv, first_in = landed([(0, 0), item(0, "w_in")], x, "allgather_wait_in0")
    small = {k: w[k] for k in SMALL}
    small["conv_w"] = jnp.transpose(g_conv, (1, 2, 0, 3)).reshape(DEPTH, DN_CONV, 3 * DN_WIDTH)
    cache = {}

    def get_w(l, part, after):
        if part == "in":
            return whole("w_in", first_in if l == 0 else landed([item(l, "w_in")], after, f"allgather_wait_in{l}")[0])
        if part == "out":
            zs = landed([item(l, k) for k in ("w_out", "w_ff1", "w_ff2")], after, f"allgather_wait_rest{l}")
            token = jnp.zeros((), f32)
            if l + 1 < DEPTH:
                ags[l + 1] = exchange_start([(wb[l + 1][k], None, True) for k in BIG], f"allgather_start_{l + 1}",
                                            scatter=False, after=zs[0])
                token = ags[l + 1]["token"][0, 0]
            cache[l] = (whole("w_ff1", zs[1]), whole("w_ff2", zs[2]), token)
            return whole("w_out", zs[0])
        return cache[l]

    rs, pending = {}, []
    ids = jnp.stack([chip, lax.axis_index("c")]).astype(jnp.int32)

    def put_g(l, tag, g):
        names = [k for k in BIG if k in g]
        by_dest = [w_in_grad_to_shards(g[k]) if k == "w_in" else g[k] for k in names]
        halves = [a.reshape(N_CHIPS, 2, -1, a.shape[-1]) for a in by_dest]
        st = sibling_start(halves, f"pair_swap_start_{tag}{l}", other_half=True)
        pending.append((l, tag, names, st))
        return st["token"][0, 0]

    def sync_g(after):
        token = jnp.zeros((), f32)
        while pending:
            l, tag, names, st = pending.pop(0)
            halves, got = sibling_wait(st, after, f"pair_swap_wait_{tag}{l}")
            pair = [pair_sum(ids, a, b, f"pair_sum_{k}_{l}") for k, a, b in zip(names, halves, got)]
            rs[l, tag] = dict(exchange_start([(a, None, False) for a in pair], f"scatter_start_{tag}{l}", scatter=True), names=names)
            token = token + rs[l, tag]["token"][0, 0]
        return token

    lossp, grad_x, gsmall = local_step(x, loss_target, small, get_w, put_g, sync_g)

    def sum_group(l, tag, after):
        st = rs[l, tag]
        zones = exchange_wait(st, list(range(len(st["names"]))), after, f"scatter_wait_{tag}{l}")
        sums = [sum_partials(ids, zones[i], st["src"][i][0], f"sum_{k}_{l}") for i, k in enumerate(st["names"])]
        return sibling_start(sums, f"swap_sums_start_{tag}{l}")

    def update_group(l, tag, swap, after, prev):
        sums, others = sibling_wait(swap, after, f"swap_sums_wait_{tag}{l}")
        outs = dict(prev)
        for i, k in enumerate(rs[l, tag]["names"]):
            r2 = lambda a: a.reshape(-1, a.shape[-1])
            outs[k] = adamw(r2(w[k]), r2(mom[k]), r2(var[k]), (sums[i], others[i]), f"adamw_{k}_{l}", layer=l, prev=prev.get(k))
        return outs

    swap_r = sum_group(1, "rest", rs[0, "in"]["token"])
    swap_i = sum_group(1, "in", swap_r["token"])
    done = update_group(1, "rest", swap_r, swap_i["token"], {})
    done = update_group(1, "in", swap_i, done["w_ff2"][0], done)
    res = {}

    full_shapes = [(DEPTH,) + tuple(gsmall[0][k].shape) for k in SMALL]
    packed = _pack([jnp.stack([gsmall[l][k] for l in range(DEPTH)]) for k in SMALL] + [jnp.sum(lossp).reshape(1)])
    *totals, loss = _unpack(allreduce_small(packed), full_shapes + [(1,)])
    loss = loss[0]
    gfull = dict(zip(SMALL, totals))
    cs = 3 * DN_WIDTH // N_CHIPS
    gfull["conv_w"] = lax.dynamic_slice_in_dim(gfull["conv_w"], chip * cs, cs, axis=2)
    gp, wp, mp, vp = (_pack([d[k] for k in SMALL]) for d in (gfull, w, mom, var))
    outs = adamw(wp, mp, vp, (gp,), "adamw_small")
    loc_shapes = [w[k].shape for k in SMALL]
    unp = [_unpack(o, loc_shapes) for o in outs]
    for i, k in enumerate(SMALL):
        res[k] = [unp[j][i] for j in range(4)]

    swap_r = sum_group(0, "rest", outs[0])
    swap_i = sum_group(0, "in", swap_r["token"])
    done = update_group(0, "rest", swap_r, swap_i["token"], done)
    done = update_group(0, "in", swap_i, done["w_ff2"][0], done)
    for k in BIG:
        res[k] = [o.reshape(w[k].shape) for o in done[k]]

    return (loss, grad_x, *[res[k][0] for k in WEIGHTS], *[res[k][1] for k in WEIGHTS], *[res[k][2] for k in WEIGHTS],
            *[res[k][3] for k in WEIGHTS])
```

```python
import functools

import jax
import jax.numpy as jnp
from jax import lax
from jax.experimental import pallas as pl
from jax.experimental.pallas import tpu as pltpu

f32 = jnp.float32
bf16 = jnp.bfloat16
SDS = jax.ShapeDtypeStruct
MESH = pl.DeviceIdType.MESH

NORM_EPS = 1e-6
D_MODEL = 1024
DEPTH = 2
DN_HEADS, DN_DIM, DN_WIDTH, DN_CONV, DN_CHUNK = 4, 128, 512, 4, 64
SB_HEADS, SB_DIM, SB_WIDTH = 4, 64, 256
SG_GROUPS, SG_DIM, SG_WIDTH, SG_CHUNK = 4, 64, 256, 128
D_FF = 4096
IN_DIM = 3336
C_QKV, C_Z, C_AB, C_SB, C_SG, IN_PAD = 0, 1536, 2048, 2304, 3072, 3584
DN_COLS = C_SB
N_CHIPS = 4

ADAM_LR, ADAM_B1, ADAM_B2, ADAM_EPS, ADAM_WD, ADAM_STEP = 0.001, 0.9, 0.999, 1e-08, 0.01, 10

VMEM_LIMIT = 56 * 1024 * 1024


def _cp(sem=None, **kw):
    if sem is not None:
        kw["dimension_semantics"] = sem
    return pltpu.CompilerParams(vmem_limit_bytes=VMEM_LIMIT, **kw)


def _split2(x):
    hi = x.astype(bf16)
    lo = (x - hi.astype(f32)).astype(bf16)
    return hi, lo


NT = (((1,), (1,)), ((), ()))
TN = (((0,), (0,)), ((), ()))
_DIMS2 = dict(nn=(((1,), (0,)), ((), ())), nt=NT, tn=TN)
_DIMS3 = dict(nn=(((2,), (1,)), ((0,), (0,))), nt=(((2,), (2,)), ((0,), (0,))), tn=(((1,), (1,)), ((0,), (0,))))


def _dg(a, b, kind):
    return lax.dot_general(a, b, (_DIMS2 if a.ndim == 2 else _DIMS3)[kind], preferred_element_type=f32)


def _pdot(a, b):
    return _dg(a, b, "nn")


def _dot_hp(a, b):
    ah, al = _split2(a)
    bh, bl = _split2(b)
    return _pdot(ah, bh) + _pdot(ah, bl) + _pdot(al, bh)


def _dot_x2c(a, m):
    lead = a.shape[:-1]
    ah, al = _split2(a.reshape(-1, a.shape[-1]))
    return (_pdot(ah, m) + _pdot(al, m)).reshape(lead + (m.shape[1],))


def _dot_cx2(m, a):
    if a.ndim == 3:
        m = jnp.broadcast_to(m, (a.shape[0],) + m.shape)
    ah, al = _split2(a)
    return _pdot(m, ah) + _pdot(m, al)


def _nt(a, b):
    return _dg(a.astype(bf16), b.astype(bf16), "nt")


def _tn(a, b):
    return _dg(a.astype(bf16), b.astype(bf16), "tn")


def _nn(a, b):
    return _dg(a.astype(bf16), b.astype(bf16), "nn")


@jax.custom_vjp
def mm(a, b):
    return _nn(a, b)


mm.defvjp(lambda a, b: (_nn(a, b), (a, b)), lambda r, g: (_nt(g, r[1]), _tn(r[0], g)))


@jax.custom_vjp
def mm_nt(a, b):
    return _nt(a, b)


mm_nt.defvjp(lambda a, b: (_nt(a, b), (a, b)), lambda r, g: (_nn(g, r[1]), _tn(g, r[0])))


@jax.custom_vjp
def mm_tn(a, b):
    return _tn(a, b)


mm_tn.defvjp(lambda a, b: (_tn(a, b), (a, b)), lambda r, g: (_nt(r[1], g), _nn(r[0], g)))


@jax.custom_vjp
def rmul_const(a, m, mt):
    return _dot_x2c(a, m)


rmul_const.defvjp(lambda a, m, mt: (_dot_x2c(a, m), (m, mt)),
                  lambda r, g: (_dot_x2c(g, r[1]), jnp.zeros_like(r[0]), jnp.zeros_like(r[1])))


@jax.custom_vjp
def lmul_const(m, mt, a):
    return _dot_cx2(m, a)


lmul_const.defvjp(lambda m, mt, a: (_dot_cx2(m, a), (m, mt)),
                  lambda r, g: (jnp.zeros_like(r[0]), jnp.zeros_like(r[1]), _dot_cx2(r[1], g)))


@jax.custom_vjp
def mm_hl(t, x):
    th, tl = _split2(t)
    xb = x.astype(bf16)
    return _pdot(th, xb) + _pdot(tl, xb)


def _mm_hl_bwd(r, g):
    t, x = r
    th, tl = _split2(t)
    gb = g.astype(bf16)
    return _nt(g, x), _dg(th, gb, "tn") + _dg(tl, gb, "tn")


mm_hl.defvjp(lambda t, x: (mm_hl(t, x), (t, x)), _mm_hl_bwd)


def inv_unit_lower(lm):
    c = lm.shape[-1]
    r, cc = _iota2((c, c))
    eye = (r == cc).astype(f32)
    t = eye - lm
    p = -lm
    k = 1
    while 2 * k < c:
        p = _nn(p, p)
        t = t + _nn(t, p)
        k *= 2
    res = eye - t - _dot_hp(lm, t)
    return t + _nn(t, res)


@jax.custom_vjp
def inv_given(lm, t):
    return t


inv_given.defvjp(lambda lm, t: (t, t), lambda t, g: (-_nt(_tn(t, g), t), jnp.zeros_like(t)))


def _sigmoid(x):
    return 1.0 / (1.0 + jnp.exp(-x))


def _softplus(x):
    return jnp.maximum(x, 0.0) + jnp.log(1.0 + jnp.exp(-jnp.abs(x)))


def _silu(x):
    return x * _sigmoid(x)


def _gelu(x):
    return 0.5 * x * (1.0 + jnp.tanh(0.7978845608028654 * (x + 0.044715 * (x * x * x))))


def _iota2(shape):
    return lax.broadcasted_iota(jnp.int32, shape, 0), lax.broadcasted_iota(jnp.int32, shape, 1)


def _group_avg_mats():
    r, c = _iota2((128, 128))
    return jnp.where((r // 64) == (c // 64), 1.0 / 64.0, 0.0).astype(bf16)


def _pair_norm(x, gain, bavg):
    ms = rmul_const(x * x, bavg, bavg)
    return x * lax.rsqrt(ms + NORM_EPS) * gain


def _rms(x):
    r = lax.rsqrt(jnp.mean(x * x, axis=-1, keepdims=True) + NORM_EPS)
    return r


_IN_GROUPS = ((C_QKV, C_Z), (C_Z, C_AB), (C_AB, C_AB + 128), (C_SB, C_SG), (C_SG, IN_PAD))
_IN_DTYPES = (f32, f32, f32, bf16, bf16)


def inproj_fwd(x, g, wp, tm=256):
    m = x.shape[0]

    def body(x_ref, g_ref, w_ref, *outs):
        xv = x_ref[...]
        h = (xv * _rms(xv) * g_ref[...]).astype(bf16)
        outs[-1][...] = h
        for (a, b), o in zip(_IN_GROUPS, outs):
            o[...] = _pdot(h, w_ref[:, a:b]).astype(o.dtype)

    widths = [b - a for a, b in _IN_GROUPS]
    return pl.pallas_call(
        body, name="inproj_fwd", grid=(m // tm,),
        in_specs=[pl.BlockSpec((tm, D_MODEL), lambda i: (i, 0)), pl.BlockSpec((1, D_MODEL), lambda i: (0, 0)),
                  pl.BlockSpec((D_MODEL, IN_PAD), lambda i: (0, 0))],
        out_specs=[pl.BlockSpec((tm, wd), lambda i: (i, 0)) for wd in widths + [D_MODEL]],
        out_shape=[SDS((m, wd), dt) for wd, dt in zip(widths, _IN_DTYPES)] + [SDS((m, D_MODEL), bf16)],
        compiler_params=_cp(("arbitrary",)),
    )(x, g, wp)


def inproj_bwd(x, g, wp, dproj, dres, tm=256):
    m = x.shape[0]

    def body(x_ref, g_ref, w_ref, dp_ref, dr_ref, dx_ref, dg_ref):
        xv = x_ref[...]
        r = _rms(xv)
        xn = xv * r
        gv = g_ref[...]
        dh = lax.dot_general(dp_ref[...], w_ref[...], NT, preferred_element_type=f32)
        dxn = dh * gv
        dx_ref[...] = dr_ref[...] + r * (dxn - xn * jnp.mean(dxn * xn, axis=-1, keepdims=True))

        @pl.when(pl.program_id(0) == 0)
        def _():
            dg_ref[...] = jnp.zeros_like(dg_ref)

        dg_ref[...] += jnp.sum(dh * xn, axis=0, keepdims=True)

    return pl.pallas_call(
        body, name="inproj_bwd", grid=(m // tm,),
        in_specs=[pl.BlockSpec((tm, D_MODEL), lambda i: (i, 0)), pl.BlockSpec((1, D_MODEL), lambda i: (0, 0)),
                  pl.BlockSpec((D_MODEL, IN_PAD), lambda i: (0, 0)), pl.BlockSpec((tm, IN_PAD), lambda i: (i, 0)),
                  pl.BlockSpec((tm, D_MODEL), lambda i: (i, 0))],
        out_specs=[pl.BlockSpec((tm, D_MODEL), lambda i: (i, 0)), pl.BlockSpec((1, D_MODEL), lambda i: (0, 0))],
        out_shape=[SDS((m, D_MODEL), f32), SDS((1, D_MODEL), f32)],
        compiler_params=_cp(("arbitrary",)),
    )(x, g, wp, dproj, dres)


def outproj_fwd(x, mix, wo, tm=512):
    m = x.shape[0]

    def body(x_ref, mix_ref, w_ref, x2_ref):
        x2_ref[...] = x_ref[...] + _pdot(mix_ref[...], w_ref[...])

    row = pl.BlockSpec((tm, D_MODEL), lambda i: (i, 0))
    return pl.pallas_call(
        body, name="outproj_fwd", grid=(m // tm,),
        in_specs=[row, row, pl.BlockSpec((D_MODEL, D_MODEL), lambda i: (0, 0))],
        out_specs=row, out_shape=SDS((m, D_MODEL), f32),
        compiler_params=_cp(("arbitrary",)),
    )(x, mix, wo)


def outproj_bwd(dx2, wo, tm=512):
    m = dx2.shape[0]

    def body(d_ref, w_ref, a_ref, b_ref, c_ref, db_ref):
        db = d_ref[...].astype(bf16)
        db_ref[...] = db
        dm = lax.dot_general(db, w_ref[...], NT, preferred_element_type=f32)
        a_ref[...] = dm[:, 0:DN_WIDTH]
        b_ref[...] = dm[:, DN_WIDTH:DN_WIDTH + SB_WIDTH]
        c_ref[...] = dm[:, DN_WIDTH + SB_WIDTH:D_MODEL]

    row = lambda w: pl.BlockSpec((tm, w), lambda i: (i, 0))
    return pl.pallas_call(
        body, name="outproj_bwd", grid=(m // tm,),
        in_specs=[row(D_MODEL), pl.BlockSpec((D_MODEL, D_MODEL), lambda i: (0, 0))],
        out_specs=[row(DN_WIDTH), row(SB_WIDTH), row(SG_WIDTH), row(D_MODEL)],
        out_shape=[SDS((m, DN_WIDTH), f32), SDS((m, SB_WIDTH), f32), SDS((m, SG_WIDTH), f32), SDS((m, D_MODEL), bf16)],
        compiler_params=_cp(("arbitrary",)),
    )(dx2, wo)


FF_CHUNK = D_FF // N_CHIPS


def _load_weights_once(pairs, sem):
    @pl.when(pl.program_id(0) == 0)
    def _():
        cps = [pltpu.make_async_copy(h, v, sem.at[i]) for i, (h, v) in enumerate(pairs)]
        for c in cps:
            c.start()
        for c in cps:
            c.wait()


def ffn_fwd(x2, g, w1, w2, tgt=None, tm=256):
    m = x2.shape[0]
    head = tgt is not None

    def body(x_ref, g_ref, w1_hbm, w2_hbm, *rest):
        (y_ref, rl_ref), (w1_v, w2_v, sem) = rest[head:head + 2], rest[-3:]
        _load_weights_once(((w1_hbm, w1_v), (w2_hbm, w2_v)), sem)
        xv = x_ref[...]
        h = (xv * _rms(xv) * g_ref[...]).astype(bf16)
        acc = xv
        for j in range(0, D_FF, FF_CHUNK):
            f = _pdot(h, w1_v[j // FF_CHUNK])
            rl = jnp.maximum(f, 0.0)
            rl_ref[:, j:j + FF_CHUNK] = rl.astype(bf16)
            acc = acc + _pdot((rl * rl).astype(bf16), w2_v[j:j + FF_CHUNK, :])
        if not head:
            y_ref[...] = acc
            return
        t_ref, l_ref = rest[0], rest[3]
        e = acc - t_ref[...]
        y_ref[...] = e * (1.0 / D_MODEL)

        @pl.when(pl.program_id(0) == 0)
        def _():
            l_ref[...] = jnp.zeros_like(l_ref)

        l_ref[...] += jnp.sum(e * e, axis=0, keepdims=True) * (0.5 / D_MODEL)

    row = pl.BlockSpec((tm, D_MODEL), lambda i: (i, 0))
    return pl.pallas_call(
        body, name="ffn_fwd_loss" if head else "ffn_fwd", grid=(m // tm,),
        in_specs=[row, pl.BlockSpec((1, D_MODEL), lambda i: (0, 0)), pl.BlockSpec(memory_space=pl.ANY),
                  pl.BlockSpec(memory_space=pl.ANY)] + [row] * head,
        out_specs=[row, pl.BlockSpec((tm, D_FF), lambda i: (i, 0))] + [pl.BlockSpec((1, D_MODEL), lambda i: (0, 0))] * head,
        out_shape=[SDS((m, D_MODEL), f32), SDS((m, D_FF), bf16)] + [SDS((1, D_MODEL), f32)] * head,
        scratch_shapes=[pltpu.VMEM((N_CHIPS, D_MODEL, FF_CHUNK), bf16), pltpu.VMEM((D_FF, D_MODEL), bf16), pltpu.SemaphoreType.DMA((2,))],
        compiler_params=_cp(("arbitrary",)),
    )(x2, g, w1, w2, *([tgt] if head else []))


def ffn_bwd(x2, g, w1, w2, rlb, dy, tm=256):
    m = x2.shape[0]

    def body(x_ref, g_ref, w1_hbm, w2_hbm, rl_ref, dy_ref, dx_ref, dg_ref, h_ref, a_ref, df_ref, dyb_ref, w1_v, w2_v, sem):
        _load_weights_once(((w1_hbm, w1_v), (w2_hbm, w2_v)), sem)
        xv = x_ref[...]
        r = _rms(xv)
        xn = xv * r
        gv = g_ref[...]
        h = (xn * gv).astype(bf16)
        h_ref[...] = h
        dyv = dy_ref[...]
        dyb = dyv.astype(bf16)
        dyb_ref[...] = dyb
        dh = jnp.zeros((tm, D_MODEL), f32)
        for j in range(0, D_FF, FF_CHUNK):
            rl = rl_ref[:, j:j + FF_CHUNK].astype(f32)
            a_ref[:, j:j + FF_CHUNK] = (rl * rl).astype(bf16)
            da = lax.dot_general(dyb, w2_v[j:j + FF_CHUNK, :], NT, preferred_element_type=f32)
            df = (da * (2.0 * rl)).astype(bf16)
            df_ref[:, j:j + FF_CHUNK] = df
            dh = dh + lax.dot_general(df, w1_v[j // FF_CHUNK], NT, preferred_element_type=f32)
        dxn = dh * gv
        dx_ref[...] = dyv + r * (dxn - xn * jnp.mean(dxn * xn, axis=-1, keepdims=True))

        @pl.when(pl.program_id(0) == 0)
        def _():
            dg_ref[...] = jnp.zeros_like(dg_ref)

        dg_ref[...] += jnp.sum(dh * xn, axis=0, keepdims=True)

    row = lambda w: pl.BlockSpec((tm, w), lambda i: (i, 0))
    return pl.pallas_call(
        body, name="ffn_bwd", grid=(m // tm,),
        in_specs=[row(D_MODEL), pl.BlockSpec((1, D_MODEL), lambda i: (0, 0)),
                  pl.BlockSpec(memory_space=pl.ANY), pl.BlockSpec(memory_space=pl.ANY), row(D_FF), row(D_MODEL)],
        out_specs=[row(D_MODEL), pl.BlockSpec((1, D_MODEL), lambda i: (0, 0)), row(D_MODEL), row(D_FF), row(D_FF), row(D_MODEL)],
        out_shape=[SDS((m, D_MODEL), f32), SDS((1, D_MODEL), f32), SDS((m, D_MODEL), bf16), SDS((m, D_FF), bf16),
                   SDS((m, D_FF), bf16), SDS((m, D_MODEL), bf16)],
        scratch_shapes=[pltpu.VMEM((N_CHIPS, D_MODEL, FF_CHUNK), bf16), pltpu.VMEM((D_FF, D_MODEL), bf16), pltpu.SemaphoreType.DMA((2,))],
        compiler_params=_cp(("arbitrary",)),
    )(x2, g, w1, w2, rlb, dy)


def _tile(n, cap):
    best = 128
    for t in range(128, cap + 1, 128):
        if n % t == 0:
            best = t
    return best


def tn_matmul(a, b, name, col_shards=1, tk=2048):
    m, ka = a.shape
    n = b.shape[1]
    ti = _tile(ka, 1024)
    tj = _tile(n // col_shards, 1152)
    tk = min(tk, m)
    nk = m // tk
    jps = (n // col_shards) // tj

    def body(a_ref, b_ref, o_ref, acc):
        k = pl.program_id(2)

        @pl.when(k == 0)
        def _():
            acc[...] = jnp.zeros_like(acc)

        acc[...] += lax.dot_general(a_ref[...], b_ref[...], TN, preferred_element_type=f32)

        @pl.when(k == nk - 1)
        def _():
            o_ref[...] = acc[...].astype(bf16).reshape(o_ref.shape)

    if col_shards == 1:
        out_shape, out_spec = SDS((ka, n), bf16), pl.BlockSpec((ti, tj), lambda i, j, k: (i, j))
    else:
        out_shape = SDS((col_shards, ka, n // col_shards), bf16)
        out_spec = pl.BlockSpec((1, ti, tj), lambda i, j, k: (j // jps, i, j % jps))
    return pl.pallas_call(
        body, name=name, grid=(ka // ti, n // tj, nk),
        in_specs=[pl.BlockSpec((tk, ti), lambda i, j, k: (k, i)), pl.BlockSpec((tk, tj), lambda i, j, k: (k, j))],
        out_specs=out_spec, out_shape=out_shape,
        scratch_shapes=[pltpu.VMEM((ti, tj), f32)],
        compiler_params=_cp(("arbitrary", "arbitrary", "arbitrary")),
    )(a, b)


def _dn_consts():
    c = DN_CHUNK
    r, cc = _iota2((c, c))
    lt = (cc <= r).astype(bf16)
    ltt = (r <= cc).astype(bf16)
    return lt, ltt


def dn_chunk(cq, ck, cv, g, beta, z, s, gain, lt, ltt, t_given=None):
    c = DN_CHUNK
    r, cc = _iota2((c, c))
    q = cq * lax.rsqrt(jnp.sum(cq * cq, axis=-1, keepdims=True) + NORM_EPS) * (DN_DIM ** -0.5)
    k = ck * lax.rsqrt(jnp.sum(ck * ck, axis=-1, keepdims=True) + NORM_EPS)
    r2, c2 = _iota2((c, 128))
    uaug = jnp.where((c2 < c) & (r2 > c2), 1.0, 0.0) + jnp.where(c2 == c, 1.0, 0.0)
    gam_all = lmul_const(lt, ltt, g * uaug)
    gam_cc = gam_all[:, :, 0:c]
    gam = gam_all[:, :, c:c + 1]
    dec = jnp.where(cc <= r, jnp.exp(jnp.where(cc <= r, gam_cc, 0.0)), 0.0)
    kk = mm_nt(k, k)
    lm = jnp.where(cc < r, beta * kk * dec, 0.0)
    t = inv_unit_lower(lm) if t_given is None else inv_given(lm, t_given)
    eg = jnp.exp(gam)
    sol = mm_hl(t, jnp.concatenate([cv * beta, k * (beta * eg)], axis=2))
    u, w = sol[:, :, 0:DN_DIM], sol[:, :, DN_DIM:2 * DN_DIM]
    qk = jnp.where(cc <= r, mm_nt(q, k) * dec, 0.0)
    glast = jnp.sum(g, axis=1, keepdims=True)
    qd = q * eg
    kd = k * jnp.exp(glast - gam)
    un = u - mm(w, s)
    o = mm(qd, s) + mm(qk, un)
    s_new = s * jnp.exp(glast) + mm_tn(kd, un)
    on = o * lax.rsqrt(jnp.mean(o * o, axis=-1, keepdims=True) + NORM_EPS) * gain * _silu(z)
    return on, s_new, t


def _dn_gates(ab, al_row, dt_row):
    pre = ab + dt_row
    return -jnp.exp(al_row) * _softplus(pre), _sigmoid(ab), _sigmoid(pre)


def _dn_chains(cacts, gates, z_ref):
    cq, ck, cv, g, beta, z = [], [], [], [], [], []
    for bi, cact in enumerate(cacts):
        for h in range(DN_HEADS):
            cq.append(cact[:, h * DN_DIM:(h + 1) * DN_DIM])
            ck.append(cact[:, DN_WIDTH + h * DN_DIM:DN_WIDTH + (h + 1) * DN_DIM])
            cv.append(cact[:, 2 * DN_WIDTH + h * DN_DIM:2 * DN_WIDTH + (h + 1) * DN_DIM])
            g.append(gates[bi][0][:, h:h + 1])
            beta.append(gates[bi][1][:, DN_HEADS + h:DN_HEADS + h + 1])
            z.append(z_ref[bi, :, h * DN_DIM:(h + 1) * DN_DIM])
    return tuple(jnp.stack(v) for v in (cq, ck, cv, g, beta, z))


def _conv_rows(xe_ref, b, w_ref):
    y = w_ref[0:1, :] * xe_ref[b, pl.ds(5, DN_CHUNK), :]
    for i in range(1, DN_CONV):
        y = y + w_ref[i:i + 1, :] * xe_ref[b, pl.ds(5 + i, DN_CHUNK), :]
    return y


def dn_fwd(qkv, z, ab, conv_w, alog, dtb, gain):
    bsz, t, _ = qkv.shape
    nc = t // DN_CHUNK
    c = DN_CHUNK
    nh = bsz * DN_HEADS

    def body(qkv_ref, z_ref, ab_ref, w_ref, al_ref, dt_ref, g_ref, o_ref, sall_ref, tall_ref, xe, s_sc):
        n = pl.program_id(0)

        @pl.when(n == 0)
        def _():
            xe[:, 0:8, :] = jnp.zeros((bsz, 8, 3 * DN_WIDTH), f32)
            s_sc[...] = jnp.zeros_like(s_sc)

        lt, ltt = _dn_consts()
        cacts = []
        for b in range(bsz):
            xe[b, 8:8 + c, :] = qkv_ref[b]
            cacts.append(_silu(_conv_rows(xe, b, w_ref)))
            xe[b, 0:8, :] = xe[b, c:c + 8, :]
        gates = [_dn_gates(ab_ref[b], al_ref[...], dt_ref[...]) for b in range(bsz)]
        s = s_sc[...]
        sall_ref[0] = s
        on, sn, tt = dn_chunk(*_dn_chains(cacts, gates, z_ref), s, g_ref[...], lt, ltt)
        tall_ref[0] = tt
        s_sc[...] = sn
        for b in range(bsz):
            for h in range(DN_HEADS):
                o_ref[b, :, h * DN_DIM:(h + 1) * DN_DIM] = on[b * DN_HEADS + h].astype(bf16)

    blk = lambda w: pl.BlockSpec((bsz, c, w), lambda n: (0, n, 0))
    full = lambda shp: pl.BlockSpec(shp, lambda n: (0,) * len(shp))
    return pl.pallas_call(
        body, name="dn_fwd", grid=(nc,),
        in_specs=[blk(3 * DN_WIDTH), blk(DN_WIDTH), blk(128), full((8, 3 * DN_WIDTH)), full((1, 128)), full((1, 128)), full((1, 128))],
        out_specs=[blk(DN_WIDTH), pl.BlockSpec((1, nh, DN_DIM, DN_DIM), lambda n: (n, 0, 0, 0)),
                   pl.BlockSpec((1, nh, c, c), lambda n: (n, 0, 0, 0))],
        out_shape=[SDS((bsz, t, D_MODEL), bf16), SDS((nc, nh, DN_DIM, DN_DIM), f32), SDS((nc, nh, c, c), f32)],
        scratch_shapes=[pltpu.VMEM((bsz, c + 8, 3 * DN_WIDTH), f32), pltpu.VMEM((nh, DN_DIM, DN_DIM), f32)],
        compiler_params=_cp(("arbitrary",)),
    )(qkv, z, ab, conv_w, alog, dtb, gain)


def dn_bwd(qkv, z, ab, conv_w, alog, dtb, gain, sall, tall, do):
    bsz, t, _ = qkv.shape
    nc = t // DN_CHUNK
    c = DN_CHUNK
    nh = bsz * DN_HEADS
    w3 = 3 * DN_WIDTH

    def body(qkv_ref, prev_ref, z_ref, ab_ref, w_ref, al_ref, dt_ref, g_ref, sall_ref, tall_ref, do_ref,
             dp_ref, dw_ref, dal_ref, ddt_ref, dg_ref, xe, dye, dc_sc, ds_sc):
        n = pl.program_id(0)
        first = (nc - 1 - n) == 0

        @pl.when(n == 0)
        def _():
            dye[:, c:c + 8, :] = jnp.zeros((bsz, 8, w3), f32)
            ds_sc[...] = jnp.zeros_like(ds_sc)
            dw_ref[...] = jnp.zeros_like(dw_ref)
            dal_ref[...] = jnp.zeros_like(dal_ref)
            ddt_ref[...] = jnp.zeros_like(ddt_ref)
            dg_ref[...] = jnp.zeros_like(dg_ref)

        lt, ltt = _dn_consts()
        lane_c = lax.broadcasted_iota(jnp.int32, (c, 128), 1)
        ys, sigs = [], []
        for b in range(bsz):
            xe[b, 0:8, :] = jnp.where(first, 0.0, prev_ref[b])
            xe[b, 8:8 + c, :] = qkv_ref[b]
            ys.append(_conv_rows(xe, b, w_ref))
            sigs.append(_sigmoid(ys[b]))
        gates = [_dn_gates(ab_ref[b], al_ref[...], dt_ref[...]) for b in range(bsz)]
        ops = _dn_chains([y * sg for y, sg in zip(ys, sigs)], gates, z_ref)
        tt = tall_ref[0]
        _, vjp = jax.vjp(lambda *p: dn_chunk(*p, lt, ltt, t_given=tt)[0:2], *ops, sall_ref[0], g_ref[...])
        don = jnp.stack([do_ref[b, :, h * DN_DIM:(h + 1) * DN_DIM] for b in range(bsz) for h in range(DN_HEADS)])
        dcq, dck, dcv, dg, dbeta, dzz, dsp, dgn = vjp((don, ds_sc[...]))
        ds_sc[...] = dsp
        dg_ref[...] += dgn
        for b in range(bsz):
            dgate = jnp.zeros((c, 128), f32)
            for h in range(DN_HEADS):
                i = b * DN_HEADS + h
                dc_sc[b, :, h * DN_DIM:(h + 1) * DN_DIM] = dcq[i]
                dc_sc[b, :, DN_WIDTH + h * DN_DIM:DN_WIDTH + (h + 1) * DN_DIM] = dck[i]
                dc_sc[b, :, 2 * DN_WIDTH + h * DN_DIM:2 * DN_WIDTH + (h + 1) * DN_DIM] = dcv[i]
                dp_ref[b, :, C_Z + h * DN_DIM:C_Z + (h + 1) * DN_DIM] = dzz[i].astype(bf16)
                dgate = dgate + jnp.where(lane_c == h, dg[i], 0.0) + jnp.where(lane_c == DN_HEADS + h, dbeta[i], 0.0)
            gg, beta, sig_pre = gates[b]
            is_g = lane_c < DN_HEADS
            dpre = jnp.where(is_g, dgate * (-jnp.exp(al_ref[...])) * sig_pre, 0.0)
            dp_ref[b, :, C_AB:C_AB + 128] = (dpre + jnp.where(is_g, 0.0, dgate * beta * (1.0 - beta))).astype(bf16)
            dp_ref[b, :, C_AB + 128:DN_COLS] = jnp.zeros((c, DN_COLS - C_AB - 128), bf16)
            dal_ref[...] += jnp.sum(jnp.where(is_g, dgate * gg, 0.0), axis=0, keepdims=True)
            ddt_ref[...] += jnp.sum(dpre, axis=0, keepdims=True)
            y, sig = ys[b], sigs[b]
            dy = dc_sc[b] * (sig * (1.0 + y * (1.0 - sig)))
            dye[b, 0:c, :] = dy
            dx = w_ref[3:4, :] * dy
            for i in range(DN_CONV - 1):
                dx = dx + w_ref[i:i + 1, :] * dye[b, pl.ds(3 - i, c), :]
            dp_ref[b, :, 0:w3] = dx.astype(bf16)
            for i in range(DN_CONV):
                dw_ref[i:i + 1, :] += jnp.sum(dy * xe[b, pl.ds(5 + i, c), :], axis=0, keepdims=True)
            dye[b, c:c + 8, :] = dye[b, 0:8, :]

    rev = lambda w: pl.BlockSpec((bsz, c, w), lambda n: (0, nc - 1 - n, 0))
    full = lambda shp: pl.BlockSpec(shp, lambda n: (0,) * len(shp))
    prev = pl.BlockSpec((bsz, 8, w3), lambda n: (0, jnp.maximum((nc - 1 - n) * (c // 8) - 1, 0), 0))
    return pl.pallas_call(
        body, name="dn_bwd", grid=(nc,),
        in_specs=[rev(w3), prev, rev(DN_WIDTH), rev(128), full((8, w3)), full((1, 128)), full((1, 128)), full((1, 128)),
                  pl.BlockSpec((1, nh, DN_DIM, DN_DIM), lambda n: (nc - 1 - n, 0, 0, 0)),
                  pl.BlockSpec((1, nh, c, c), lambda n: (nc - 1 - n, 0, 0, 0)), rev(DN_WIDTH)],
        out_specs=[rev(DN_COLS), full((8, w3)), full((1, 128)), full((1, 128)), full((1, 128))],
        out_shape=[SDS((bsz, t, IN_PAD), bf16), SDS((8, w3), f32), SDS((1, 128), f32), SDS((1, 128), f32), SDS((1, 128), f32)],
        scratch_shapes=[pltpu.VMEM((bsz, c + 8, w3), f32), pltpu.VMEM((bsz, c + 8, w3), f32), pltpu.VMEM((bsz, c, w3), f32),
                        pltpu.VMEM((nh, DN_DIM, DN_DIM), f32)],
        compiler_params=_cp(("arbitrary",)),
    )(qkv, qkv, z, ab, conv_w, alog, dtb, gain, sall, tall, do)


SB_TILE = 256
SB_QTILE, SB_KTILE = 256, 256
SB_PAIRS = SB_HEADS // 2


def sb_fwd(sbqkv, gq, gk, mix):
    bsz, t, _ = sbqkv.shape
    bq = min(SB_QTILE, t)
    blk = max(min(SB_KTILE, t), bq)
    nq = t // bq
    scale = SB_DIM ** -0.5

    def body(q_ref, k_ref, v_ref, gq_ref, gk_ref, mix_in, o_ref, l_ref, q2_sc, kn_sc, v_sc):
        bavg = _group_avg_mats()
        lane = lax.broadcasted_iota(jnp.int32, (1, 128), 1)
        first = lane < SB_DIM
        for p in range(SB_PAIRS):
            ls = slice(p * 128, (p + 1) * 128)
            qn = _pair_norm(q_ref[0, :, ls].astype(f32), gq_ref[...], bavg)
            kn_sc[p] = _pair_norm(k_ref[0, :, ls].astype(f32), gk_ref[...], bavg).astype(bf16)
            v_sc[p] = v_ref[0, :, ls].astype(bf16)
            q2_sc[2 * p] = jnp.where(first, qn, 0.0).astype(bf16)
            q2_sc[2 * p + 1] = jnp.where(first, 0.0, qn).astype(bf16)
        r, c = _iota2((blk, blk))
        ustrict = (r > c).astype(bf16)
        r2, c2 = _iota2((2 * bq, blk))

        def tile(q2s, ks, carry, causal):
            out = []
            for p in range(SB_PAIRS):
                acc, rr = carry[2 * p], carry[2 * p + 1]
                zz = lax.dot_general(q2s[p], kn_sc[p, pl.ds(ks, blk), :], NT, preferred_element_type=f32) * scale
                sp = _softplus(zz)
                lm = -sp if causal is None else jnp.where(causal, -sp, 0.0)
                rem = _dot_x2c(lm, ustrict)
                wgt = jnp.exp(zz - sp + rem + rr)
                if causal is not None:
                    wgt = jnp.where(causal, wgt, 0.0)
                out += [acc + _pdot(wgt.astype(bf16), v_sc[p, pl.ds(ks, blk), :]), rr + jnp.sum(lm, axis=1, keepdims=True)]
            return tuple(out)

        def qloop(qi, _):
            qs = pl.multiple_of(qi * bq, bq)
            kd = qs // blk
            causal = c2 < (r2 & (bq - 1)) + (qs - kd * blk)
            q2s = [jnp.concatenate([q2_sc[2 * p, pl.ds(qs, bq), :], q2_sc[2 * p + 1, pl.ds(qs, bq), :]], axis=0)
                   for p in range(SB_PAIRS)]
            zero = (jnp.zeros((2 * bq, 128), f32), jnp.zeros((2 * bq, 1), f32)) * SB_PAIRS
            carry = lax.fori_loop(1, kd + 1, lambda i, cr: tile(q2s, pl.multiple_of((kd - i) * blk, blk), cr, None),
                                  tile(q2s, pl.multiple_of(kd * blk, blk), zero, causal))
            for p in range(SB_PAIRS):
                acc, rr = carry[2 * p], carry[2 * p + 1]
                o_ref[0, pl.ds(qs, bq), p * 128:(p + 1) * 128] = jnp.where(first, acc[0:bq], acc[bq:2 * bq]).astype(bf16)
                l_ref[0, pl.ds(qs, bq), p * 128:(p + 1) * 128] = jnp.where(first, rr[0:bq], rr[bq:2 * bq])
            return 0

        lax.fori_loop(0, nq, qloop, 0)

    col = lambda off: pl.BlockSpec((1, t, SB_WIDTH), lambda b: (b, 0, off))
    gsp = pl.BlockSpec((1, 128), lambda b: (0, 0))
    return pl.pallas_call(
        body, name="sb_fwd", grid=(bsz,),
        in_specs=[col(0), col(1), col(2), gsp, gsp, pl.BlockSpec(memory_space=pl.ANY)],
        out_specs=[col(DN_WIDTH // SB_WIDTH), col(0)],
        out_shape=[SDS(mix.shape, bf16), SDS((bsz, t, SB_WIDTH), f32)],
        input_output_aliases={5: 0},
        scratch_shapes=[pltpu.VMEM((2 * SB_PAIRS, t, 128), bf16), pltpu.VMEM((SB_PAIRS, t, 128), bf16),
                        pltpu.VMEM((SB_PAIRS, t, 128), bf16)],
        compiler_params=_cp(("arbitrary",)),
    )(sbqkv, sbqkv, sbqkv, gq, gk, mix)


def sb_bwd(sbqkv, gq, gk, ltot, do, dproj):
    bsz, t, _ = sbqkv.shape
    blk = min(SB_TILE, t)
    nq = t // blk
    scale = SB_DIM ** -0.5

    def body(q_ref, k_ref, v_ref, gq_ref, gk_ref, l_ref, do_ref, dp_in, dp_ref, dgq_ref, dgk_ref,
             q2_sc, kn_sc, v_sc, do2_sc, dqn_sc, dkn_sc, dv_sc):
        bavg = _group_avg_mats()
        lane = lax.broadcasted_iota(jnp.int32, (1, 128), 1)
        first = lane < SB_DIM
        fq = lambda x, g: _pair_norm(x, g, bavg)
        vjps = []
        for p in range(SB_PAIRS):
            ls = slice(p * 128, (p + 1) * 128)
            qn, q_vjp = jax.vjp(fq, q_ref[0, :, ls].astype(f32), gq_ref[...])
            kn, k_vjp = jax.vjp(fq, k_ref[0, :, ls].astype(f32), gk_ref[...])
            vjps.append((q_vjp, k_vjp))
            kn_sc[p] = kn.astype(bf16)
            v_sc[p] = v_ref[0, :, ls].astype(bf16)
            dov = do_ref[0, :, ls]
            q2_sc[2 * p] = jnp.where(first, qn, 0.0).astype(bf16)
            q2_sc[2 * p + 1] = jnp.where(first, 0.0, qn).astype(bf16)
            do2_sc[2 * p] = jnp.where(first, dov, 0.0).astype(bf16)
            do2_sc[2 * p + 1] = jnp.where(first, 0.0, dov).astype(bf16)
        dkn_sc[...] = jnp.zeros_like(dkn_sc)
        dv_sc[...] = jnp.zeros_like(dv_sc)
        r, c = _iota2((blk, blk))
        pincl = (r <= c).astype(bf16)
        pstrict = (r < c).astype(bf16)
        r2, c2 = _iota2((2 * blk, blk))
        causal = c2 < (r2 & (blk - 1))

        def tile(q2s, do2s, lts, ks, carry, diag):
            out = []
            for p in range(SB_PAIRS):
                dq, cs, ce = carry[3 * p:3 * p + 3]
                q2, do2 = q2s[p], do2s[p]
                kb = kn_sc[p, pl.ds(ks, blk), :]
                zz = lax.dot_general(q2, kb, NT, preferred_element_type=f32) * scale
                sp = _softplus(zz)
                lm = jnp.where(causal, -sp, 0.0) if diag else -sp
                pre = _dot_x2c(lm, pincl)
                lp = zz - sp
                wgt = jnp.exp(lp + (lts[p] - cs - pre))
                if diag:
                    wgt = jnp.where(causal, wgt, 0.0)
                dw = lax.dot_general(do2, v_sc[p, pl.ds(ks, blk), :], NT, preferred_element_type=f32)
                e = wgt * dw
                ee = ce + _dot_x2c(e, pstrict)
                sig = jnp.exp(lp)
                dz = (e * (1.0 - sig) - ee * sig) * scale
                if diag:
                    dz = jnp.where(causal, dz, 0.0)
                dz = dz.astype(bf16)
                dkn_sc[p, pl.ds(ks, blk), :] += lax.dot_general(dz, q2, TN, preferred_element_type=f32)
                dv_sc[p, pl.ds(ks, blk), :] += lax.dot_general(wgt.astype(bf16), do2, TN, preferred_element_type=f32)
                out += [dq + _pdot(dz, kb), cs + jnp.sum(lm, axis=1, keepdims=True), ce + jnp.sum(e, axis=1, keepdims=True)]
            return tuple(out)

        def qloop(qi, _):
            qs = pl.multiple_of(qi * blk, blk)
            rows = pl.ds(qs, blk)
            q2s = [jnp.concatenate([q2_sc[2 * p, rows, :], q2_sc[2 * p + 1, rows, :]], axis=0) for p in range(SB_PAIRS)]
            do2s = [jnp.concatenate([do2_sc[2 * p, rows, :], do2_sc[2 * p + 1, rows, :]], axis=0) for p in range(SB_PAIRS)]
            lts = [jnp.concatenate([l_ref[0, rows, p * 128:p * 128 + 1], l_ref[0, rows, p * 128 + SB_DIM:p * 128 + SB_DIM + 1]],
                                   axis=0) for p in range(SB_PAIRS)]
            z1 = jnp.zeros((2 * blk, 1), f32)
            carry = lax.fori_loop(0, qi, lambda kj, cr: tile(q2s, do2s, lts, pl.multiple_of(kj * blk, blk), cr, False),
                                  (jnp.zeros((2 * blk, 128), f32), z1, z1) * SB_PAIRS)
            carry = tile(q2s, do2s, lts, qs, carry, True)
            for p in range(SB_PAIRS):
                dq = carry[3 * p]
                dqn_sc[p, rows, :] = jnp.where(first, dq[0:blk], dq[blk:2 * blk])
            return 0

        lax.fori_loop(0, nq, qloop, 0)
        dgq_tot, dgk_tot = jnp.zeros((1, 128), f32), jnp.zeros((1, 128), f32)
        for p in range(SB_PAIRS):
            ls = slice(p * 128, (p + 1) * 128)
            dq_pre, dgq = vjps[p][0](dqn_sc[p])
            dk_pre, dgk = vjps[p][1](dkn_sc[p])
            dp_ref[0, :, p * 128:(p + 1) * 128] = dq_pre.astype(bf16)
            dp_ref[0, :, SB_WIDTH + p * 128:SB_WIDTH + (p + 1) * 128] = dk_pre.astype(bf16)
            dp_ref[0, :, 2 * SB_WIDTH + p * 128:2 * SB_WIDTH + (p + 1) * 128] = dv_sc[p].astype(bf16)
            dgq_tot, dgk_tot = dgq_tot + dgq, dgk_tot + dgk
        dgq_ref[0] = jnp.broadcast_to(dgq_tot, (8, 128))
        dgk_ref[0] = jnp.broadcast_to(dgk_tot, (8, 128))

    col = lambda off: pl.BlockSpec((1, t, SB_WIDTH), lambda b: (b, 0, off), pipeline_mode=pl.Buffered(1))
    gsp = pl.BlockSpec((1, 128), lambda b: (0, 0))
    gout = pl.BlockSpec((1, 8, 128), lambda b: (b, 0, 0))
    return pl.pallas_call(
        body, name="sb_bwd", grid=(bsz,),
        in_specs=[col(0), col(1), col(2), gsp, gsp, col(0), col(0), pl.BlockSpec(memory_space=pl.ANY)],
        out_specs=[pl.BlockSpec((1, t, 3 * SB_WIDTH), lambda b: (b, 0, C_SB // (3 * SB_WIDTH)), pipeline_mode=pl.Buffered(1)),
                   gout, gout],
        out_shape=[SDS(dproj.shape, bf16)] + [SDS((bsz, 8, 128), f32)] * 2,
        input_output_aliases={7: 0},
        scratch_shapes=[pltpu.VMEM((2 * SB_PAIRS, t, 128), bf16), pltpu.VMEM((SB_PAIRS, t, 128), bf16),
                        pltpu.VMEM((SB_PAIRS, t, 128), bf16), pltpu.VMEM((2 * SB_PAIRS, t, 128), bf16),
                        pltpu.VMEM((SB_PAIRS, t, 128), f32), pltpu.VMEM((SB_PAIRS, t, 128), f32), pltpu.VMEM((SB_PAIRS, t, 128), f32)],
        compiler_params=_cp(("arbitrary",)),
    )(sbqkv, sbqkv, sbqkv, gq, gk, ltot, do, dproj)


SG_STEP = 512


def sg_pair(u, v, gain, wa, wb, ba, bb, bavg):
    r, c = _iota2((SG_CHUNK, SG_CHUNK))
    lane = lax.broadcasted_iota(jnp.int32, (1, 128), 1)
    first = lane < SG_DIM
    vn = _pair_norm(_gelu(v), gain, bavg)
    tri = c <= r
    mixed = (mm(jnp.where(tri, wa, 0.0), jnp.where(first, vn, 0.0)) + mm(jnp.where(tri, wb, 0.0), jnp.where(first, 0.0, vn))
             + jnp.where(first, ba, bb))
    return _gelu(u) * mixed


def sg_fwd(sguv, gain, w, bt, mix):
    bsz, t, _ = sguv.shape
    rows = min(SG_STEP, t)

    def body(uv_ref, g_ref, w_ref, b_ref, mix_in, o_ref):
        bavg = _group_avg_mats()
        for r0 in range(0, rows, SG_CHUNK):
            rs = slice(r0, r0 + SG_CHUNK)
            for p in range(2):
                ls = slice(p * 128, (p + 1) * 128)
                o_ref[0, rs, ls] = sg_pair(uv_ref[0, rs, ls].astype(f32),
                                           uv_ref[0, rs, SG_WIDTH + p * 128:SG_WIDTH + (p + 1) * 128].astype(f32), g_ref[:, ls],
                                           w_ref[2 * p], w_ref[2 * p + 1], b_ref[:, 2 * p:2 * p + 1], b_ref[:, 2 * p + 1:2 * p + 2],
                                           bavg).astype(bf16)

    full = lambda shp: pl.BlockSpec(shp, lambda b, n: (0,) * len(shp))
    return pl.pallas_call(
        body, name="sg_fwd", grid=(bsz, t // rows),
        in_specs=[pl.BlockSpec((1, rows, 2 * SG_WIDTH), lambda b, n: (b, n, 0)), full((1, SG_WIDTH)),
                  full((SG_GROUPS, SG_CHUNK, SG_CHUNK)), full((SG_CHUNK, 128)), pl.BlockSpec(memory_space=pl.ANY)],
        out_specs=pl.BlockSpec((1, rows, SG_WIDTH), lambda b, n: (b, n, (DN_WIDTH + SB_WIDTH) // SG_WIDTH)),
        out_shape=SDS(mix.shape, bf16), input_output_aliases={4: 0},
        compiler_params=_cp(("arbitrary", "arbitrary")),
    )(sguv, gain, w, bt, mix)


def sg_bwd(sguv, gain, w, bt, do, dproj):
    bsz, t, _ = sguv.shape
    rows = min(SG_STEP, t)

    def body(uv_ref, g_ref, w_ref, b_ref, do_ref, dp_in, duv_ref, dg_ref, dw_ref, db_ref):
        @pl.when((pl.program_id(0) == 0) & (pl.program_id(1) == 0))
        def _():
            dg_ref[...] = jnp.zeros_like(dg_ref)
            dw_ref[...] = jnp.zeros_like(dw_ref)
            db_ref[...] = jnp.zeros_like(db_ref)

        bavg = _group_avg_mats()
        lane = lax.broadcasted_iota(jnp.int32, (SG_CHUNK, 128), 1)
        dbt = jnp.zeros((SG_CHUNK, 128), f32)
        dgs, dws = [jnp.zeros((1, 128), f32)] * 2, [jnp.zeros((SG_CHUNK, SG_CHUNK), f32)] * SG_GROUPS
        for r0 in range(0, rows, SG_CHUNK):
            rs = slice(r0, r0 + SG_CHUNK)
            for p in range(2):
                ls = slice(p * 128, (p + 1) * 128)
                vs = slice(SG_WIDTH + p * 128, SG_WIDTH + (p + 1) * 128)
                prim = (uv_ref[0, rs, ls].astype(f32), uv_ref[0, rs, vs].astype(f32), g_ref[:, ls], w_ref[2 * p], w_ref[2 * p + 1],
                        b_ref[:, 2 * p:2 * p + 1], b_ref[:, 2 * p + 1:2 * p + 2])
                _, vjp = jax.vjp(lambda *a: sg_pair(*a, bavg), *prim)
                du, dv, dgn, dwa, dwb, dba, dbb = vjp(do_ref[0, rs, ls])
                duv_ref[0, rs, ls] = du.astype(bf16)
                duv_ref[0, rs, vs] = dv.astype(bf16)
                dgs[p] = dgs[p] + dgn
                dws[2 * p], dws[2 * p + 1] = dws[2 * p] + dwa, dws[2 * p + 1] + dwb
                dbt = dbt + jnp.where(lane == 2 * p, dba, 0.0) + jnp.where(lane == 2 * p + 1, dbb, 0.0)
        for p in range(2):
            dg_ref[:, p * 128:(p + 1) * 128] += dgs[p]
        for gidx in range(SG_GROUPS):
            dw_ref[gidx] += dws[gidx]
        db_ref[...] += dbt

    full = lambda shp: pl.BlockSpec(shp, lambda b, n: (0,) * len(shp))
    return pl.pallas_call(
        body, name="sg_bwd", grid=(bsz, t // rows),
        in_specs=[pl.BlockSpec((1, rows, 2 * SG_WIDTH), lambda b, n: (b, n, 0)), full((1, SG_WIDTH)),
                  full((SG_GROUPS, SG_CHUNK, SG_CHUNK)), full((SG_CHUNK, 128)),
                  pl.BlockSpec((1, rows, SG_WIDTH), lambda b, n: (b, n, 0)), pl.BlockSpec(memory_space=pl.ANY)],
        out_specs=[pl.BlockSpec((1, rows, 2 * SG_WIDTH), lambda b, n: (b, n, C_SG // (2 * SG_WIDTH))), full((1, SG_WIDTH)),
                   full((SG_GROUPS, SG_CHUNK, SG_CHUNK)), full((SG_CHUNK, 128))],
        out_shape=[SDS(dproj.shape, bf16), SDS((1, SG_WIDTH), f32), SDS((SG_GROUPS, SG_CHUNK, SG_CHUNK), f32),
                   SDS((SG_CHUNK, 128), f32)],
        input_output_aliases={5: 0},
        compiler_params=_cp(("arbitrary", "arbitrary")),
    )(sguv, gain, w, bt, do, dproj)


def _pad_lanes(v, n=128):
    return jnp.pad(v.reshape(1, -1), ((0, 0), (0, n - v.size)))


def _w_in_runs():
    shard, runs = IN_DIM // N_CHIPS, []
    for s in range(N_CHIPS):
        for a, b, d in ((0, 2048, 0), (2048, 2056, C_AB), (2056, IN_DIM, C_SB)):
            lo, hi = max(shard * s, a), min(shard * (s + 1), b)
            if lo < hi:
                runs.append((s, lo - shard * s, hi - shard * s, d + lo - a))
    return runs


def w_in_from_shards(zone, tr=256):
    def body(z_ref, o_ref):
        o_ref[:, C_AB:C_SB] = jnp.zeros((tr, C_SB - C_AB), zone.dtype)
        for s, a, b, d in _w_in_runs():
            o_ref[:, d:d + b - a] = z_ref[s, :, a:b]

    return pl.pallas_call(
        body, name="w_in_from_shards", grid=(D_MODEL // tr,),
        in_specs=[pl.BlockSpec((N_CHIPS, tr, IN_DIM // N_CHIPS), lambda i: (0, i, 0))],
        out_specs=pl.BlockSpec((tr, IN_PAD), lambda i: (i, 0)), out_shape=SDS((D_MODEL, IN_PAD), zone.dtype),
        compiler_params=_cp(("arbitrary",)))(zone)


def w_in_grad_to_shards(g, tr=256):
    def body(g_ref, o_ref):
        for s, a, b, d in _w_in_runs():
            o_ref[s, :, a:b] = g_ref[:, d:d + b - a]

    return pl.pallas_call(
        body, name="w_in_grad_to_shards", grid=(D_MODEL // tr,),
        in_specs=[pl.BlockSpec((tr, IN_PAD), lambda i: (i, 0))],
        out_specs=pl.BlockSpec((N_CHIPS, tr, IN_DIM // N_CHIPS), lambda i: (0, i, 0)),
        out_shape=SDS((N_CHIPS, D_MODEL, IN_DIM // N_CHIPS), g.dtype), compiler_params=_cp(("arbitrary",)))(g)


def layer_params(p, l):
    return dict(
        g1=p["norm1_g"][l].reshape(1, -1), g2=p["norm2_g"][l].reshape(1, -1),
        conv=jnp.pad(p["conv_w"][l], ((0, 4), (0, 0))), alog=_pad_lanes(p["a_log"][l]), dtb=_pad_lanes(p["dt_bias"][l]),
        dng=p["dn_out_g"][l].reshape(1, -1), gq=jnp.tile(p["sb_q_g"][l].reshape(1, -1), (1, 2)),
        gk=jnp.tile(p["sb_k_g"][l].reshape(1, -1), (1, 2)), sgg=p["sg_v_g"][l].reshape(1, -1), sgw=p["sg_w"][l],
        sgb=jnp.pad(p["sg_b"][l].T, ((0, 0), (0, 124))))


def local_step(x, tgt, small, get_w, put_g, sync_g):
    bsz, t, _ = x.shape
    m = bsz * t
    r3 = lambda a: a.reshape(bsz, t, a.shape[-1])
    r2 = lambda a: a.reshape(m, a.shape[-1])
    xs, saved, ws = x.reshape(m, D_MODEL), [], []
    for l in range(DEPTH):
        sp, w = layer_params(small, l), {}
        w["w_in"] = get_w(l, "in", xs)
        qkv, z, ab, sb, sg, h1 = inproj_fwd(xs, sp["g1"], w["w_in"])
        mix, sall, tall = dn_fwd(r3(qkv), r3(z), r3(ab), sp["conv"], sp["alog"], sp["dtb"], sp["dng"])
        mix, ltot = sb_fwd(r3(sb), sp["gq"], sp["gk"], mix)
        mix = r2(sg_fwd(r3(sg), sp["sgg"], sp["sgw"], sp["sgb"], mix))
        w["w_out"] = get_w(l, "out", mix)
        x2 = outproj_fwd(xs, mix, w["w_out"])
        w["w_ff1"], w["w_ff2"], started = get_w(l, "ff", x2)
        if l + 1 < DEPTH:
            xs_next, rlb = ffn_fwd(x2, sp["g2"] + started, w["w_ff1"], w["w_ff2"])
        else:
            dx, rlb, lossp = ffn_fwd(x2, sp["g2"] + started, w["w_ff1"], w["w_ff2"], tgt=tgt.reshape(m, D_MODEL))
        saved.append(dict(rlb=rlb, h1=h1, x=xs, qkv=qkv, z=z, ab=ab, sb=sb, sg=sg, sall=sall, tall=tall, ltot=ltot, mix=mix, x2=x2))
        ws.append(w)
        xs = xs_next
    gsmall = [None] * DEPTH
    token = jnp.zeros((), f32)
    for l in reversed(range(DEPTH)):
        sp, w, s = layer_params(small, l), ws[l], saved[l]
        dx2, dg2, h2, act, df, dyb = ffn_bwd(s["x2"], sp["g2"] + token, w["w_ff1"], w["w_ff2"], s["rlb"], dx)
        g_ff1 = tn_matmul(h2, df, f"dw_ff1_{l}", col_shards=N_CHIPS)
        g_ff2 = tn_matmul(act, dyb, f"dw_ff2_{l}")
        dodn, dosb, dosg, dx2b = outproj_bwd(dx2, w["w_out"])
        g_out = tn_matmul(s["mix"], dx2b, f"dw_out_{l}")
        token = token + put_g(l, "rest", dict(w_out=g_out, w_ff1=g_ff1, w_ff2=g_ff2))
        dproj, dconv, dalog, ddtb, ddng = dn_bwd(r3(s["qkv"]), r3(s["z"]), r3(s["ab"]), sp["conv"], sp["alog"], sp["dtb"],
                                                 sp["dng"] + token, s["sall"], s["tall"], r3(dodn))
        token = sync_g(ddng)
        dproj, dgq, dgk = sb_bwd(r3(s["sb"]), sp["gq"] + token, sp["gk"], s["ltot"], r3(dosb), dproj)
        dproj, dsgg, dsgw, dsgb = sg_bwd(r3(s["sg"]), sp["sgg"], sp["sgw"], sp["sgb"], r3(dosg), dproj)
        dproj = r2(dproj)
        g_in = tn_matmul(s["h1"], dproj, f"dw_in_{l}")
        token = put_g(l, "in", dict(w_in=g_in))
        dx, dg1 = inproj_bwd(s["x"], sp["g1"] + token, w["w_in"], dproj, dx2)
        token = sync_g(dg1)
        fold = lambda a: (a[:, 0, :].sum(0).reshape(2, SB_DIM)).sum(0)
        gsmall[l] = dict(norm1_g=dg1[0], conv_w=dconv[0:DN_CONV], a_log=dalog[0, 0:DN_HEADS], dt_bias=ddtb[0, 0:DN_HEADS],
                         dn_out_g=ddng[0], sb_q_g=fold(dgq), sb_k_g=fold(dgk), sg_v_g=dsgg[0], sg_w=dsgw,
                         sg_b=dsgb[:, 0:SG_GROUPS].T, norm2_g=dg2[0])
    return lossp, dx.reshape(bsz, t, D_MODEL), gsmall


def _chip_peers(x, y):
    return [(1 - x, y), (x, 1 - y), (1 - x, 1 - y)]


_HBM = pl.BlockSpec(memory_space=pltpu.HBM)
_SEM = pl.BlockSpec(memory_space=pltpu.SEMAPHORE)
_EFFECT = pltpu.SideEffectType.DATAFLOW_SIDE_EFFECTING


def _hbm(a):
    return pltpu.with_memory_space_constraint(a, pltpu.HBM)


def _my_half(ref):
    half = ref.shape[0] // 2
    return ref.at[pl.ds(pl.multiple_of(lax.axis_index("c") * half, 8), half)]


def _exchange_copy(src, land, k, j, send, recv, scatter, halve, waiting):
    x, y, c = lax.axis_index("x"), lax.axis_index("y"), lax.axis_index("c")
    px, py = _chip_peers(x, y)[j]
    me, peer = 2 * x + y, 2 * px + py
    if scatter:
        src = src.at[me if waiting else peer]
    dst = land.at[peer if waiting else me]
    if halve:
        src, dst = _my_half(src), _my_half(dst)
    return pltpu.make_async_remote_copy(src_ref=src, dst_ref=dst, send_sem=send.at[3 * k + j],
                                        recv_sem=recv.at[3 * k + j], device_id=(px, py, c), device_id_type=MESH)


def exchange_start(items, name, scatter, after=None):
    arrs = []
    for a, _, _ in items:
        if not any(a is b for b in arrs):
            arrs.append(a)
    pos = [next(i for i, b in enumerate(arrs) if b is a) for a, _, _ in items]
    shapes = [a.shape if idx is None else a.shape[1:] for a, idx, _ in items]
    lands = [lax.empty(s if scatter else (N_CHIPS,) + s, a.dtype) for (a, _, _), s in zip(items, shapes)]
    na, nl = len(arrs), len(lands)
    n_in = na + nl + (after is not None)

    def body(*refs):
        ins, lnd = refs[:na], refs[na:na + nl]
        send, recv = refs[n_in], refs[n_in + 1]
        token = refs[-1]
        for k, (_, idx, halve) in enumerate(items):
            src = ins[pos[k]] if idx is None else ins[pos[k]].at[idx]
            for j in range(3):
                _exchange_copy(src, lnd[k], k, j, send, recv, scatter, halve, False).start()
        token[...] = jnp.zeros_like(token)

    sems = pltpu.SemaphoreType.DMA((3 * nl,))
    extra = [] if after is None else [after]
    out = pl.pallas_call(
        body, name=name,
        out_shape=(sems, sems, *[pltpu.HBM(a.shape, a.dtype) for a in arrs + lands], SDS((8, 128), f32)),
        in_specs=[_HBM] * (na + nl) + [pl.BlockSpec(memory_space=pl.ANY)] * len(extra),
        out_specs=(_SEM, _SEM, *[_HBM] * (na + nl), pl.BlockSpec(memory_space=pltpu.VMEM)),
        input_output_aliases={i: 2 + i for i in range(na + nl)},
        compiler_params=pltpu.CompilerParams(has_side_effects=_EFFECT),
    )(*[_hbm(a) for a in arrs + lands], *extra)
    thru = out[2:2 + na]
    return dict(send=out[0], recv=out[1], src=[(thru[pos[k]], idx) for k, (_, idx, _) in enumerate(items)],
                halve=[h for _, _, h in items], land=list(out[2 + na:2 + na + nl]), token=out[-1], scatter=scatter)


def exchange_wait(st, ks, after, name):
    arrs = []
    for k in ks:
        if not any(st["src"][k][0] is b for b in arrs):
            arrs.append(st["src"][k][0])
    pos = [next(i for i, b in enumerate(arrs) if b is st["src"][k][0]) for k in ks]
    lands = [st["land"][k] for k in ks]
    na, nl = len(arrs), len(lands)

    def body(*refs):
        ins, lnd = refs[:na], refs[na:na + nl]
        send, recv = refs[na + nl], refs[na + nl + 1]
        for t, k in enumerate(ks):
            idx = st["src"][k][1]
            src = ins[pos[t]] if idx is None else ins[pos[t]].at[idx]
            for j in range(3):
                cp = _exchange_copy(src, lnd[t], k, j, send, recv, st["scatter"], st["halve"][k], True)
                cp.wait_send()
                cp.wait_recv()

    out = pl.pallas_call(
        body, name=name, out_shape=tuple(pltpu.HBM(a.shape, a.dtype) for a in arrs + lands),
        in_specs=[_HBM] * (na + nl) + [_SEM, _SEM, pl.BlockSpec(memory_space=pl.ANY)], out_specs=tuple([_HBM] * (na + nl)),
        input_output_aliases={i: i for i in range(na + nl)},
        compiler_params=pltpu.CompilerParams(has_side_effects=_EFFECT),
    )(*arrs, *lands, st["send"], st["recv"], after)
    for k, (a, idx) in enumerate(st["src"]):
        for p, b in enumerate(arrs):
            if a is b:
                st["src"][k] = (out[p], idx)
    return list(out[na:na + nl])


def _sibling_copy(src, land, i, send, recv, other_half):
    x, y, c = lax.axis_index("x"), lax.axis_index("y"), lax.axis_index("c")
    return pltpu.make_async_remote_copy(src_ref=src.at[:, 1 - c] if other_half else src, dst_ref=land, send_sem=send.at[i],
                                        recv_sem=recv.at[i], device_id=(x, y, 1 - c), device_id_type=MESH)


def sibling_start(arrs, name, other_half=False):
    n = len(arrs)
    lands = [lax.empty((a.shape[0],) + a.shape[2:] if other_half else a.shape, a.dtype) for a in arrs]

    def body(*refs):
        ins, lnd = refs[:n], refs[n:2 * n]
        send, recv = refs[2 * n], refs[2 * n + 1]
        token = refs[-1]
        for i in range(n):
            _sibling_copy(ins[i], lnd[i], i, send, recv, other_half).start()
        token[...] = jnp.zeros_like(token)

    sems = pltpu.SemaphoreType.DMA((n,))
    out = pl.pallas_call(
        body, name=name,
        out_shape=(sems, sems, *[pltpu.HBM(a.shape, a.dtype) for a in arrs + lands], SDS((8, 128), f32)),
        in_specs=[_HBM] * (2 * n), out_specs=(_SEM, _SEM, *[_HBM] * (2 * n), pl.BlockSpec(memory_space=pltpu.VMEM)),
        input_output_aliases={i: 2 + i for i in range(2 * n)},
        compiler_params=pltpu.CompilerParams(has_side_effects=_EFFECT),
    )(*[_hbm(a) for a in arrs + lands])
    return dict(send=out[0], recv=out[1], src=list(out[2:2 + n]), land=list(out[2 + n:2 + 2 * n]), token=out[-1],
                other_half=other_half)


def sibling_wait(st, after, name):
    n = len(st["src"])

    def body(*refs):
        ins, lnd = refs[:n], refs[n:2 * n]
        send, recv = refs[2 * n], refs[2 * n + 1]
        for i in range(n):
            cp = _sibling_copy(ins[i], lnd[i], i, send, recv, st["other_half"])
            cp.wait_send()
            cp.wait_recv()

    out = pl.pallas_call(
        body, name=name, out_shape=tuple(pltpu.HBM(a.shape, a.dtype) for a in st["src"] + st["land"]),
        in_specs=[_HBM] * (2 * n) + [_SEM, _SEM, pl.BlockSpec(memory_space=pl.ANY)], out_specs=tuple([_HBM] * (2 * n)),
        input_output_aliases={i: i for i in range(2 * n)},
        compiler_params=pltpu.CompilerParams(has_side_effects=_EFFECT),
    )(*st["src"], *st["land"], st["send"], st["recv"], after)
    return list(out[:n]), list(out[n:])


def swap_halves(zones, name):
    n = len(zones)

    def body(*refs):
        outs = refs[n:2 * n]
        send, recv = refs[2 * n:]
        x, y, c = lax.axis_index("x"), lax.axis_index("y"), lax.axis_index("c")
        cps = []
        for i in range(n):
            for j, (px, py) in enumerate(_chip_peers(x, y)):
                part = _my_half(outs[i].at[2 * px + py])
                cps.append(pltpu.make_async_remote_copy(src_ref=part, dst_ref=part, send_sem=send.at[3 * i + j],
                                                        recv_sem=recv.at[3 * i + j], device_id=(x, y, 1 - c), device_id_type=MESH))
        for cp in cps:
            cp.start()
        for cp in cps:
            cp.wait_send()
            cp.wait_recv()

    any_spec = pl.BlockSpec(memory_space=pl.ANY)
    return pl.pallas_call(
        body, name=name, in_specs=[any_spec] * n, out_specs=[any_spec] * n, out_shape=[SDS(a.shape, a.dtype) for a in zones],
        input_output_aliases={i: i for i in range(n)},
        scratch_shapes=[pltpu.SemaphoreType.DMA((3 * n,)), pltpu.SemaphoreType.DMA((3 * n,))],
    )(*zones)


def _ids_spec(grid, in_specs, out_specs):
    return pltpu.PrefetchScalarGridSpec(num_scalar_prefetch=1, grid=grid, in_specs=in_specs, out_specs=out_specs)


def pair_sum(ids, a, b, name, tr=512):
    nd, _, rows, cols = a.shape
    tr = min(tr, rows)
    assert rows % tr == 0

    def body(ids_ref, a_ref, b_ref, o_ref):
        o_ref[...] = (a_ref[0].astype(f32) + b_ref[...].astype(f32)).astype(bf16)

    spec = pl.BlockSpec((1, tr, cols), lambda d, i, ids: (d, i, 0))
    return pl.pallas_call(
        body, name=name,
        grid_spec=_ids_spec((nd, rows // tr), [pl.BlockSpec((1, 1, tr, cols), lambda d, i, ids: (d, ids[1], i, 0)), spec], spec),
        out_shape=SDS((nd, rows, cols), bf16), compiler_params=_cp(("arbitrary", "arbitrary")))(ids, a, b)


def allreduce_small(v):
    half = v.shape[0] // 2
    assert half % 8 == 0

    def body(v_ref, o_ref, rbuf, send, recv):
        x, y, c = lax.axis_index("x"), lax.axis_index("y"), lax.axis_index("c")
        o_ref[...] = v_ref[...]

        def exchange(rows, peer, k):
            return pltpu.make_async_remote_copy(src_ref=o_ref.at[rows], dst_ref=rbuf.at[k, rows], send_sem=send.at[k],
                                                recv_sem=recv.at[k], device_id=peer, device_id_type=MESH)

        lo, hi, across_x, across_y = pl.ds(0, half), pl.ds(half, half), (1 - x, y, c), (x, 1 - y, c)
        stages = [[(pl.ds(0, 2 * half), (x, y, 1 - c))], [(lo, across_x), (hi, across_y)], [(lo, across_y), (hi, across_x)]]
        k = 0
        for stage in stages:
            cps = [exchange(rows, peer, k + i) for i, (rows, peer) in enumerate(stage)]
            for cp in cps:
                cp.start()
            for cp in cps:
                cp.wait()
            for i, (rows, _) in enumerate(stage):
                o_ref[rows] = o_ref[rows] + rbuf[k + i, rows]
            k += len(stage)

    vm = pl.BlockSpec(memory_space=pltpu.VMEM)
    return pl.pallas_call(
        body, name="allreduce_small", in_specs=[vm], out_specs=vm, out_shape=SDS(v.shape, f32),
        scratch_shapes=[pltpu.VMEM((5,) + v.shape, f32), pltpu.SemaphoreType.DMA((5,)), pltpu.SemaphoreType.DMA((5,))],
        compiler_params=_cp(),
    )(v)


def sum_partials(ids, zone, mine, name, tr=256):
    _, rows, cols = zone.shape
    tr = min(tr, rows)
    assert rows % tr == 0

    def body(ids_ref, m_ref, z1_ref, z2_ref, z3_ref, o_ref):
        o_ref[...] = ((m_ref[0].astype(f32) + z1_ref[0].astype(f32)) + z2_ref[0].astype(f32)) + z3_ref[0].astype(f32)

    slot = lambda flip: pl.BlockSpec((1, tr, cols), lambda i, ids: (ids[0] ^ flip, i, 0))
    return pl.pallas_call(
        body, name=name,
        grid_spec=_ids_spec((rows // tr,), [slot(0), slot(1), slot(2), slot(3)], pl.BlockSpec((tr, cols), lambda i, ids: (i, 0))),
        out_shape=SDS((rows, cols), f32), compiler_params=_cp(("arbitrary",)),
    )(ids, mine, zone, zone, zone)


def adamw(w, m, v, gs, name, layer=0, prev=None, tr=256):
    hrows, cols = gs[0].shape
    rows = hrows * len(gs)
    tr = min(tr, hrows)
    assert hrows % tr == 0 and w.shape[0] % rows == 0
    off, nth = layer * (rows // tr), hrows // tr

    def body(w_ref, m_ref, v_ref, *rest):
        g_ref, d_ref, mo_ref, vo_ref = rest[-4:]
        if len(gs) == 1:
            g = rest[0][...]
        else:
            g = jnp.where(pl.program_id(0) // nth == lax.axis_index("c"), rest[0][...], rest[1][...])
        mn = ADAM_B1 * m_ref[...] + (1.0 - ADAM_B1) * g
        vn = ADAM_B2 * v_ref[...] + (1.0 - ADAM_B2) * jnp.square(g)
        m_hat = mn / (1.0 - ADAM_B1 ** ADAM_STEP)
        v_hat = vn / (1.0 - ADAM_B2 ** ADAM_STEP)
        g_ref[...] = g
        d_ref[...] = -ADAM_LR * (m_hat / (jnp.sqrt(v_hat) + ADAM_EPS) + ADAM_WD * w_ref[...])
        mo_ref[...] = mn
        vo_ref[...] = vn

    loc = pl.BlockSpec((tr, cols), lambda i: (i % nth, 0))
    glob = pl.BlockSpec((tr, cols), lambda i: (off + i, 0))
    extra = [] if prev is None else list(prev)
    return pl.pallas_call(
        body, name=name, grid=(rows // tr,),
        in_specs=[glob] * 3 + [loc] * len(gs) + [pl.BlockSpec(memory_space=pl.ANY)] * len(extra),
        out_specs=[glob] * 4, out_shape=[SDS(w.shape, f32)] * 4,
        input_output_aliases={3 + len(gs) + j: j for j in range(len(extra))},
        compiler_params=_cp(("arbitrary",)),
    )(w, m, v, *gs, *extra)


BIG = ("w_in", "w_out", "w_ff1", "w_ff2")
SMALL = ("norm1_g", "conv_w", "a_log", "dt_bias", "dn_out_g", "sb_q_g", "sb_k_g", "sg_v_g", "sg_w", "sg_b", "norm2_g")
WEIGHTS = ("norm1_g", "w_in", "conv_w", "a_log", "dt_bias", "dn_out_g", "sb_q_g", "sb_k_g", "sg_v_g", "sg_w", "sg_b",
           "w_out", "norm2_g", "w_ff1", "w_ff2")


PACK_ROWS = 256


def _rows_of(shape):
    n = 1
    for d in shape:
        n *= d
    return -(-n // 1024) * 8, n


def _pack(arrs):
    parts = []
    for a in arrs:
        r, n = _rows_of(a.shape)
        parts.append(jnp.pad(a.reshape(-1), (0, r * 128 - n)).reshape(r, 128))
    rows = sum(p.shape[0] for p in parts)
    parts.append(jnp.zeros((-rows % PACK_ROWS, 128), arrs[0].dtype))
    return jnp.concatenate(parts, axis=0)


def _unpack(packed, shapes):
    out, o = [], 0
    for s in shapes:
        r, n = _rows_of(s)
        out.append(packed[o:o + r].reshape(-1)[0:n].reshape(s))
        o += r
    return out


def kernel(x, norm1_g, w_in, conv_w, a_log, dt_bias, dn_out_g, sb_q_g, sb_k_g, sg_v_g, sg_w, sg_b, w_out, norm2_g, w_ff1, w_ff2, loss_target, m_norm1_g, m_w_in, m_conv_w, m_a_log, m_dt_bias, m_dn_out_g, m_sb_q_g, m_sb_k_g, m_sg_v_g, m_sg_w, m_sg_b, m_w_out, m_norm2_g, m_w_ff1, m_w_ff2, v_norm1_g, v_w_in, v_conv_w, v_a_log, v_dt_bias, v_dn_out_g, v_sb_q_g, v_sb_k_g, v_sg_v_g, v_sg_w, v_sg_b, v_w_out, v_norm2_g, v_w_ff1, v_w_ff2):
    w = dict(norm1_g=norm1_g, w_in=w_in, conv_w=conv_w, a_log=a_log, dt_bias=dt_bias, dn_out_g=dn_out_g, sb_q_g=sb_q_g,
             sb_k_g=sb_k_g, sg_v_g=sg_v_g, sg_w=sg_w, sg_b=sg_b, w_out=w_out, norm2_g=norm2_g, w_ff1=w_ff1, w_ff2=w_ff2)
    mom = dict(norm1_g=m_norm1_g, w_in=m_w_in, conv_w=m_conv_w, a_log=m_a_log, dt_bias=m_dt_bias, dn_out_g=m_dn_out_g,
               sb_q_g=m_sb_q_g, sb_k_g=m_sb_k_g, sg_v_g=m_sg_v_g, sg_w=m_sg_w, sg_b=m_sg_b, w_out=m_w_out, norm2_g=m_norm2_g,
               w_ff1=m_w_ff1, w_ff2=m_w_ff2)
    var = dict(norm1_g=v_norm1_g, w_in=v_w_in, conv_w=v_conv_w, a_log=v_a_log, dt_bias=v_dt_bias, dn_out_g=v_dn_out_g,
               sb_q_g=v_sb_q_g, sb_k_g=v_sb_k_g, sg_v_g=v_sg_v_g, sg_w=v_sg_w, sg_b=v_sg_b, w_out=v_w_out, norm2_g=v_norm2_g,
               w_ff1=v_w_ff1, w_ff2=v_w_ff2)
    chip = 2 * lax.axis_index("x") + lax.axis_index("y")

    wb = [{k: w[k][l].astype(bf16) for k in BIG} for l in range(DEPTH)]
    ags = {0: exchange_start([(conv_w, None, False)] + [(wb[0][k], None, True) for k in BIG], "allgather_start_0", scatter=False)}
    item = lambda l, k: (l, (l == 0) + BIG.index(k))

    def landed(items, after, name):
        ag, ks = ags[items[0][0]], [k for _, k in items]
        zones = exchange_wait(ag, ks, after, name)
        halved = [t for t, k in enumerate(ks) if ag["halve"][k]]
        for t, z in zip(halved, swap_halves([zones[t] for t in halved], name.replace("wait", "pass"))):
            zones[t] = z
        return [lax.dynamic_update_slice_in_dim(z, ag["src"][k][0][None], chip, axis=0) for z, k in zip(zones, ks)]

    def whole(k, z):
        if k == "w_in":
            return w_in_from_shards(z)
        return z if k == "w_ff1" else z.reshape(-1, D_MODEL)

    g_conv, first_in = landed([(0, 0), item(0, "w_in")], x, "allgather_wait_in0")
    small = {k: w[k] for k in SMALL}
    small["conv_w"] = jnp.transpose(g_conv, (1, 2, 0, 3)).reshape(DEPTH, DN_CONV, 3 * DN_WIDTH)
    cache = {}

    def get_w(l, part, after):
        if part == "in":
            return whole("w_in", first_in if l == 0 else landed([item(l, "w_in")], after, f"allgather_wait_in{l}")[0])
        if part == "out":
            zs = landed([item(l, k) for k in ("w_out", "w_ff1", "w_ff2")], after, f"allgather_wait_rest{l}")
            token = jnp.zeros((), f32)
            if l + 1 < DEPTH:
                ags[l + 1] = exchange_start([(wb[l + 1][k], None, True) for k in BIG], f"allgather_start_{l + 1}",
                                            scatter=False, after=zs[0])
                token = ags[l + 1]["token"][0, 0]
            cache[l] = (whole("w_ff1", zs[1]), whole("w_ff2", zs[2]), token)
            return whole("w_out", zs[0])
        return cache[l]

    rs, pending = {}, []
    ids = jnp.stack([chip, lax.axis_index("c")]).astype(jnp.int32)

    def put_g(l, tag, g):
        names = [k for k in BIG if k in g]
        by_dest = [w_in_grad_to_shards(g[k]) if k == "w_in" else g[k] for k in names]
        halves = [a.reshape(N_CHIPS, 2, -1, a.shape[-1]) for a in by_dest]
        st = sibling_start(halves, f"pair_swap_start_{tag}{l}", other_half=True)
        pending.append((l, tag, names, st))
        return st["token"][0, 0]

    def sync_g(after):
        token = jnp.zeros((), f32)
        while pending:
            l, tag, names, st = pending.pop(0)
            halves, got = sibling_wait(st, after, f"pair_swap_wait_{tag}{l}")
            pair = [pair_sum(ids, a, b, f"pair_sum_{k}_{l}") for k, a, b in zip(names, halves, got)]
            rs[l, tag] = dict(exchange_start([(a, None, False) for a in pair], f"scatter_start_{tag}{l}", scatter=True), names=names)
            token = token + rs[l, tag]["token"][0, 0]
        return token

    lossp, grad_x, gsmall = local_step(x, loss_target, small, get_w, put_g, sync_g)

    def sum_group(l, tag, after):
        st = rs[l, tag]
        zones = exchange_wait(st, list(range(len(st["names"]))), after, f"scatter_wait_{tag}{l}")
        sums = [sum_partials(ids, zones[i], st["src"][i][0], f"sum_{k}_{l}") for i, k in enumerate(st["names"])]
        return sibling_start(sums, f"swap_sums_start_{tag}{l}")

    def update_group(l, tag, swap, after, prev):
        sums, others = sibling_wait(swap, after, f"swap_sums_wait_{tag}{l}")
        outs = dict(prev)
        for i, k in enumerate(rs[l, tag]["names"]):
            r2 = lambda a: a.reshape(-1, a.shape[-1])
            outs[k] = adamw(r2(w[k]), r2(mom[k]), r2(var[k]), (sums[i], others[i]), f"adamw_{k}_{l}", layer=l, prev=prev.get(k))
        return outs

    swap_r = sum_group(1, "rest", rs[0, "in"]["token"])
    swap_i = sum_group(1, "in", swap_r["token"])
    done = update_group(1, "rest", swap_r, swap_i["token"], {})
    done = update_group(1, "in", swap_i, done["w_ff2"][0], done)
    res = {}

    full_shapes = [(DEPTH,) + tuple(gsmall[0][k].shape) for k in SMALL]
    packed = _pack([jnp.stack([gsmall[l][k] for l in range(DEPTH)]) for k in SMALL] + [jnp.sum(lossp).reshape(1)])
    *totals, loss = _unpack(allreduce_small(packed), full_shapes + [(1,)])
    loss = loss[0]
    gfull = dict(zip(SMALL, totals))
    cs = 3 * DN_WIDTH // N_CHIPS
    gfull["conv_w"] = lax.dynamic_slice_in_dim(gfull["conv_w"], chip * cs, cs, axis=2)
    gp, wp, mp, vp = (_pack([d[k] for k in SMALL]) for d in (gfull, w, mom, var))
    outs = adamw(wp, mp, vp, (gp,), "adamw_small")
    loc_shapes = [w[k].shape for k in SMALL]
    unp = [_unpack(o, loc_shapes) for o in outs]
    for i, k in enumerate(SMALL):
        res[k] = [unp[j][i] for j in range(4)]

    swap_r = sum_group(0, "rest", outs[0])
    swap_i = sum_group(0, "in", swap_r["token"])
    done = update_group(0, "rest", swap_r, swap_i["token"], done)
    done = update_group(0, "in", swap_i, done["w_ff2"][0], done)
    for k in BIG:
        res[k] = [o.reshape(w[k].shape) for o in done[k]]

    return (loss, grad_x, *[res[k][0] for k in WEIGHTS], *[res[k][1] for k in WEIGHTS], *[res[k][2] for k in WEIGHTS],
            *[res[k][3] for k in WEIGHTS])
```

```python
import functools

import jax
import jax.numpy as jnp
from jax import lax
from jax.experimental import pallas as pl
from jax.experimental.pallas import tpu as pltpu

f32 = jnp.float32
bf16 = jnp.bfloat16
SDS = jax.ShapeDtypeStruct
MESH = pl.DeviceIdType.MESH

NORM_EPS = 1e-6
D_MODEL = 1024
DEPTH = 2
DN_HEADS, DN_DIM, DN_WIDTH, DN_CONV, DN_CHUNK = 4, 128, 512, 4, 64
SB_HEADS, SB_DIM, SB_WIDTH = 4, 64, 256
SG_GROUPS, SG_DIM, SG_WIDTH, SG_CHUNK = 4, 64, 256, 128
D_FF = 4096
IN_DIM = 3336
C_QKV, C_Z, C_AB, C_SB, C_SG, IN_PAD = 0, 1536, 2048, 2304, 3072, 3584
DN_COLS = C_SB
N_CHIPS = 4

ADAM_LR, ADAM_B1, ADAM_B2, ADAM_EPS, ADAM_WD, ADAM_STEP = 0.001, 0.9, 0.999, 1e-08, 0.01, 10

VMEM_LIMIT = 56 * 1024 * 1024


def _cp(sem=None, **kw):
    if sem is not None:
        kw["dimension_semantics"] = sem
    return pltpu.CompilerParams(vmem_limit_bytes=VMEM_LIMIT, **kw)


def _split2(x):
    hi = x.astype(bf16)
    lo = (x - hi.astype(f32)).astype(bf16)
    return hi, lo


NT = (((1,), (1,)), ((), ()))
TN = (((0,), (0,)), ((), ()))
_DIMS2 = dict(nn=(((1,), (0,)), ((), ())), nt=NT, tn=TN)
_DIMS3 = dict(nn=(((2,), (1,)), ((0,), (0,))), nt=(((2,), (2,)), ((0,), (0,))), tn=(((1,), (1,)), ((0,), (0,))))


def _dg(a, b, kind):
    return lax.dot_general(a, b, (_DIMS2 if a.ndim == 2 else _DIMS3)[kind], preferred_element_type=f32)


def _pdot(a, b):
    return _dg(a, b, "nn")


def _dot_hp(a, b):
    ah, al = _split2(a)
    bh, bl = _split2(b)
    return _pdot(ah, bh) + _pdot(ah, bl) + _pdot(al, bh)


def _dot_x2c(a, m):
    lead = a.shape[:-1]
    ah, al = _split2(a.reshape(-1, a.shape[-1]))
    return (_pdot(ah, m) + _pdot(al, m)).reshape(lead + (m.shape[1],))


def _dot_cx2(m, a):
    if a.ndim == 3:
        m = jnp.broadcast_to(m, (a.shape[0],) + m.shape)
    ah, al = _split2(a)
    return _pdot(m, ah) + _pdot(m, al)


def _nt(a, b):
    return _dg(a.astype(bf16), b.astype(bf16), "nt")


def _tn(a, b):
    return _dg(a.astype(bf16), b.astype(bf16), "tn")


def _nn(a, b):
    return _dg(a.astype(bf16), b.astype(bf16), "nn")


@jax.custom_vjp
def mm(a, b):
    return _nn(a, b)


mm.defvjp(lambda a, b: (_nn(a, b), (a, b)), lambda r, g: (_nt(g, r[1]), _tn(r[0], g)))


@jax.custom_vjp
def mm_nt(a, b):
    return _nt(a, b)


mm_nt.defvjp(lambda a, b: (_nt(a, b), (a, b)), lambda r, g: (_nn(g, r[1]), _tn(g, r[0])))


@jax.custom_vjp
def mm_tn(a, b):
    return _tn(a, b)


mm_tn.defvjp(lambda a, b: (_tn(a, b), (a, b)), lambda r, g: (_nt(r[1], g), _nn(r[0], g)))


@jax.custom_vjp
def rmul_const(a, m, mt):
    return _dot_x2c(a, m)


rmul_const.defvjp(lambda a, m, mt: (_dot_x2c(a, m), (m, mt)),
                  lambda r, g: (_dot_x2c(g, r[1]), jnp.zeros_like(r[0]), jnp.zeros_like(r[1])))


@jax.custom_vjp
def lmul_const(m, mt, a):
    return _dot_cx2(m, a)


lmul_const.defvjp(lambda m, mt, a: (_dot_cx2(m, a), (m, mt)),
                  lambda r, g: (jnp.zeros_like(r[0]), jnp.zeros_like(r[1]), _dot_cx2(r[1], g)))


@jax.custom_vjp
def mm_hl(t, x):
    th, tl = _split2(t)
    xb = x.astype(bf16)
    return _pdot(th, xb) + _pdot(tl, xb)


def _mm_hl_bwd(r, g):
    t, x = r
    th, tl = _split2(t)
    gb = g.astype(bf16)
    return _nt(g, x), _dg(th, gb, "tn") + _dg(tl, gb, "tn")


mm_hl.defvjp(lambda t, x: (mm_hl(t, x), (t, x)), _mm_hl_bwd)


def inv_unit_lower(lm):
    c = lm.shape[-1]
    r, cc = _iota2((c, c))
    eye = (r == cc).astype(f32)
    t = eye - lm
    p = -lm
    k = 1
    while 2 * k < c:
        p = _nn(p, p)
        t = t + _nn(t, p)
        k *= 2
    res = eye - t - _dot_hp(lm, t)
    return t + _nn(t, res)


@jax.custom_vjp
def inv_given(lm, t):
    return t


inv_given.defvjp(lambda lm, t: (t, t), lambda t, g: (-_nt(_tn(t, g), t), jnp.zeros_like(t)))


def _sigmoid(x):
    return 1.0 / (1.0 + jnp.exp(-x))


def _softplus(x):
    return jnp.maximum(x, 0.0) + jnp.log(1.0 + jnp.exp(-jnp.abs(x)))


def _silu(x):
    return x * _sigmoid(x)


def _gelu(x):
    return 0.5 * x * (1.0 + jnp.tanh(0.7978845608028654 * (x + 0.044715 * (x * x * x))))


def _iota2(shape):
    return lax.broadcasted_iota(jnp.int32, shape, 0), lax.broadcasted_iota(jnp.int32, shape, 1)


def _group_avg_mats():
    r, c = _iota2((128, 128))
    return jnp.where((r // 64) == (c // 64), 1.0 / 64.0, 0.0).astype(bf16)


def _pair_norm(x, gain, bavg):
    ms = rmul_const(x * x, bavg, bavg)
    return x * lax.rsqrt(ms + NORM_EPS) * gain


def _rms(x):
    r = lax.rsqrt(jnp.mean(x * x, axis=-1, keepdims=True) + NORM_EPS)
    return r


_IN_GROUPS = ((C_QKV, C_Z), (C_Z, C_AB), (C_AB, C_AB + 128), (C_SB, C_SG), (C_SG, IN_PAD))
_IN_DTYPES = (bf16, f32, f32, bf16, bf16)


def inproj_fwd(x, g, wp, tm=256):
    m = x.shape[0]

    def body(x_ref, g_ref, w_ref, *outs):
        xv = x_ref[...]
        h = (xv * _rms(xv) * g_ref[...]).astype(bf16)
        outs[-1][...] = h
        for (a, b), o in zip(_IN_GROUPS, outs):
            o[...] = _pdot(h, w_ref[:, a:b]).astype(o.dtype)

    widths = [b - a for a, b in _IN_GROUPS]
    return pl.pallas_call(
        body, name="inproj_fwd", grid=(m // tm,),
        in_specs=[pl.BlockSpec((tm, D_MODEL), lambda i: (i, 0)), pl.BlockSpec((1, D_MODEL), lambda i: (0, 0)),
                  pl.BlockSpec((D_MODEL, IN_PAD), lambda i: (0, 0))],
        out_specs=[pl.BlockSpec((tm, wd), lambda i: (i, 0)) for wd in widths + [D_MODEL]],
        out_shape=[SDS((m, wd), dt) for wd, dt in zip(widths, _IN_DTYPES)] + [SDS((m, D_MODEL), bf16)],
        compiler_params=_cp(("arbitrary",)),
    )(x, g, wp)


def inproj_bwd(x, g, wp, dproj, dres, tm=256):
    m = x.shape[0]

    def body(x_ref, g_ref, w_ref, dp_ref, dr_ref, dx_ref, dg_ref):
        xv = x_ref[...]
        r = _rms(xv)
        xn = xv * r
        gv = g_ref[...]
        dh = lax.dot_general(dp_ref[...], w_ref[...], NT, preferred_element_type=f32)
        dxn = dh * gv
        dx_ref[...] = dr_ref[...] + r * (dxn - xn * jnp.mean(dxn * xn, axis=-1, keepdims=True))

        @pl.when(pl.program_id(0) == 0)
        def _():
            dg_ref[...] = jnp.zeros_like(dg_ref)

        dg_ref[...] += jnp.sum(dh * xn, axis=0, keepdims=True)

    return pl.pallas_call(
        body, name="inproj_bwd", grid=(m // tm,),
        in_specs=[pl.BlockSpec((tm, D_MODEL), lambda i: (i, 0)), pl.BlockSpec((1, D_MODEL), lambda i: (0, 0)),
                  pl.BlockSpec((D_MODEL, IN_PAD), lambda i: (0, 0)), pl.BlockSpec((tm, IN_PAD), lambda i: (i, 0)),
                  pl.BlockSpec((tm, D_MODEL), lambda i: (i, 0))],
        out_specs=[pl.BlockSpec((tm, D_MODEL), lambda i: (i, 0)), pl.BlockSpec((1, D_MODEL), lambda i: (0, 0))],
        out_shape=[SDS((m, D_MODEL), f32), SDS((1, D_MODEL), f32)],
        compiler_params=_cp(("arbitrary",)),
    )(x, g, wp, dproj, dres)


def outproj_fwd(x, mix, wo, tm=512):
    m = x.shape[0]

    def body(x_ref, mix_ref, w_ref, x2_ref):
        x2_ref[...] = x_ref[...] + _pdot(mix_ref[...], w_ref[...])

    row = pl.BlockSpec((tm, D_MODEL), lambda i: (i, 0))
    return pl.pallas_call(
        body, name="outproj_fwd", grid=(m // tm,),
        in_specs=[row, row, pl.BlockSpec((D_MODEL, D_MODEL), lambda i: (0, 0))],
        out_specs=row, out_shape=SDS((m, D_MODEL), f32),
        compiler_params=_cp(("arbitrary",)),
    )(x, mix, wo)


def outproj_bwd(dx2, wo, tm=512):
    m = dx2.shape[0]

    def body(d_ref, w_ref, a_ref, b_ref, c_ref, db_ref):
        db = d_ref[...].astype(bf16)
        db_ref[...] = db
        dm = lax.dot_general(db, w_ref[...], NT, preferred_element_type=f32)
        a_ref[...] = dm[:, 0:DN_WIDTH]
        b_ref[...] = dm[:, DN_WIDTH:DN_WIDTH + SB_WIDTH]
        c_ref[...] = dm[:, DN_WIDTH + SB_WIDTH:D_MODEL]

    row = lambda w: pl.BlockSpec((tm, w), lambda i: (i, 0))
    return pl.pallas_call(
        body, name="outproj_bwd", grid=(m // tm,),
        in_specs=[row(D_MODEL), pl.BlockSpec((D_MODEL, D_MODEL), lambda i: (0, 0))],
        out_specs=[row(DN_WIDTH), row(SB_WIDTH), row(SG_WIDTH), row(D_MODEL)],
        out_shape=[SDS((m, DN_WIDTH), f32), SDS((m, SB_WIDTH), f32), SDS((m, SG_WIDTH), f32), SDS((m, D_MODEL), bf16)],
        compiler_params=_cp(("arbitrary",)),
    )(dx2, wo)


FF_CHUNK = D_FF // N_CHIPS


def _load_weights_once(pairs, sem):
    @pl.when(pl.program_id(0) == 0)
    def _():
        cps = [pltpu.make_async_copy(h, v, sem.at[i]) for i, (h, v) in enumerate(pairs)]
        for c in cps:
            c.start()
        for c in cps:
            c.wait()


def ffn_fwd(x2, g, w1, w2, tgt=None, tm=256):
    m = x2.shape[0]
    head = tgt is not None

    def body(x_ref, g_ref, w1_hbm, w2_hbm, *rest):
        (y_ref, rl_ref), (w1_v, w2_v, sem) = rest[head:head + 2], rest[-3:]
        _load_weights_once(((w1_hbm, w1_v), (w2_hbm, w2_v)), sem)
        xv = x_ref[...]
        h = (xv * _rms(xv) * g_ref[...]).astype(bf16)
        acc = xv
        for j in range(0, D_FF, FF_CHUNK):
            f = _pdot(h, w1_v[j // FF_CHUNK])
            rl = jnp.maximum(f, 0.0)
            rl_ref[:, j:j + FF_CHUNK] = rl.astype(bf16)
            acc = acc + _pdot((rl * rl).astype(bf16), w2_v[j:j + FF_CHUNK, :])
        if not head:
            y_ref[...] = acc
            return
        t_ref, l_ref = rest[0], rest[3]
        e = acc - t_ref[...]
        y_ref[...] = e * (1.0 / D_MODEL)

        @pl.when(pl.program_id(0) == 0)
        def _():
            l_ref[...] = jnp.zeros_like(l_ref)

        l_ref[...] += jnp.sum(e * e, axis=0, keepdims=True) * (0.5 / D_MODEL)

    row = pl.BlockSpec((tm, D_MODEL), lambda i: (i, 0))
    return pl.pallas_call(
        body, name="ffn_fwd_loss" if head else "ffn_fwd", grid=(m // tm,),
        in_specs=[row, pl.BlockSpec((1, D_MODEL), lambda i: (0, 0)), pl.BlockSpec(memory_space=pl.ANY),
                  pl.BlockSpec(memory_space=pl.ANY)] + [row] * head,
        out_specs=[row, pl.BlockSpec((tm, D_FF), lambda i: (i, 0))] + [pl.BlockSpec((1, D_MODEL), lambda i: (0, 0))] * head,
        out_shape=[SDS((m, D_MODEL), f32), SDS((m, D_FF), bf16)] + [SDS((1, D_MODEL), f32)] * head,
        scratch_shapes=[pltpu.VMEM((N_CHIPS, D_MODEL, FF_CHUNK), bf16), pltpu.VMEM((D_FF, D_MODEL), bf16), pltpu.SemaphoreType.DMA((2,))],
        compiler_params=_cp(("arbitrary",)),
    )(x2, g, w1, w2, *([tgt] if head else []))


def ffn_bwd(x2, g, w1, w2, rlb, dy, tm=256):
    m = x2.shape[0]

    def body(x_ref, g_ref, w1_hbm, w2_hbm, rl_ref, dy_ref, dx_ref, dg_ref, h_ref, a_ref, df_ref, dyb_ref, w1_v, w2_v, sem):
        _load_weights_once(((w1_hbm, w1_v), (w2_hbm, w2_v)), sem)
        xv = x_ref[...]
        r = _rms(xv)
        xn = xv * r
        gv = g_ref[...]
        h = (xn * gv).astype(bf16)
        h_ref[...] = h
        dyv = dy_ref[...]
        dyb = dyv.astype(bf16)
        dyb_ref[...] = dyb
        dh = jnp.zeros((tm, D_MODEL), f32)
        for j in range(0, D_FF, FF_CHUNK):
            rl = rl_ref[:, j:j + FF_CHUNK].astype(f32)
            a_ref[:, j:j + FF_CHUNK] = (rl * rl).astype(bf16)
            da = lax.dot_general(dyb, w2_v[j:j + FF_CHUNK, :], NT, preferred_element_type=f32)
            df = (da * (2.0 * rl)).astype(bf16)
            df_ref[:, j:j + FF_CHUNK] = df
            dh = dh + lax.dot_general(df, w1_v[j // FF_CHUNK], NT, preferred_element_type=f32)
        dxn = dh * gv
        dx_ref[...] = dyv + r * (dxn - xn * jnp.mean(dxn * xn, axis=-1, keepdims=True))

        @pl.when(pl.program_id(0) == 0)
        def _():
            dg_ref[...] = jnp.zeros_like(dg_ref)

        dg_ref[...] += jnp.sum(dh * xn, axis=0, keepdims=True)

    row = lambda w: pl.BlockSpec((tm, w), lambda i: (i, 0))
    return pl.pallas_call(
        body, name="ffn_bwd", grid=(m // tm,),
        in_specs=[row(D_MODEL), pl.BlockSpec((1, D_MODEL), lambda i: (0, 0)),
                  pl.BlockSpec(memory_space=pl.ANY), pl.BlockSpec(memory_space=pl.ANY), row(D_FF), row(D_MODEL)],
        out_specs=[row(D_MODEL), pl.BlockSpec((1, D_MODEL), lambda i: (0, 0)), row(D_MODEL), row(D_FF), row(D_FF), row(D_MODEL)],
        out_shape=[SDS((m, D_MODEL), f32), SDS((1, D_MODEL), f32), SDS((m, D_MODEL), bf16), SDS((m, D_FF), bf16),
                   SDS((m, D_FF), bf16), SDS((m, D_MODEL), bf16)],
        scratch_shapes=[pltpu.VMEM((N_CHIPS, D_MODEL, FF_CHUNK), bf16), pltpu.VMEM((D_FF, D_MODEL), bf16), pltpu.SemaphoreType.DMA((2,))],
        compiler_params=_cp(("arbitrary",)),
    )(x2, g, w1, w2, rlb, dy)


def _tile(n, cap):
    best = 128
    for t in range(128, cap + 1, 128):
        if n % t == 0:
            best = t
    return best


def tn_matmul(a, b, name, col_shards=1, tk=2048):
    m, ka = a.shape
    n = b.shape[1]
    ti = _tile(ka, 1024)
    tj = _tile(n // col_shards, 1152)
    tk = min(tk, m)
    nk = m // tk
    jps = (n // col_shards) // tj

    def body(a_ref, b_ref, o_ref, acc):
        k = pl.program_id(2)

        @pl.when(k == 0)
        def _():
            acc[...] = jnp.zeros_like(acc)

        acc[...] += lax.dot_general(a_ref[...], b_ref[...], TN, preferred_element_type=f32)

        @pl.when(k == nk - 1)
        def _():
            o_ref[...] = acc[...].astype(bf16).reshape(o_ref.shape)

    if col_shards == 1:
        out_shape, out_spec = SDS((ka, n), bf16), pl.BlockSpec((ti, tj), lambda i, j, k: (i, j))
    else:
        out_shape = SDS((col_shards, ka, n // col_shards), bf16)
        out_spec = pl.BlockSpec((1, ti, tj), lambda i, j, k: (j // jps, i, j % jps))
    return pl.pallas_call(
        body, name=name, grid=(ka // ti, n // tj, nk),
        in_specs=[pl.BlockSpec((tk, ti), lambda i, j, k: (k, i)), pl.BlockSpec((tk, tj), lambda i, j, k: (k, j))],
        out_specs=out_spec, out_shape=out_shape,
        scratch_shapes=[pltpu.VMEM((ti, tj), f32)],
        compiler_params=_cp(("arbitrary", "arbitrary", "arbitrary")),
    )(a, b)


def _dn_consts():
    c = DN_CHUNK
    r, cc = _iota2((c, c))
    lt = (cc <= r).astype(bf16)
    ltt = (r <= cc).astype(bf16)
    return lt, ltt


def dn_chunk(cq, ck, cv, g, beta, z, s, gain, lt, ltt, t_given=None):
    c = DN_CHUNK
    r, cc = _iota2((c, c))
    q = cq * lax.rsqrt(jnp.sum(cq * cq, axis=-1, keepdims=True) + NORM_EPS) * (DN_DIM ** -0.5)
    k = ck * lax.rsqrt(jnp.sum(ck * ck, axis=-1, keepdims=True) + NORM_EPS)
    r2, c2 = _iota2((c, 128))
    uaug = jnp.where((c2 < c) & (r2 > c2), 1.0, 0.0) + jnp.where(c2 == c, 1.0, 0.0)
    gam_all = lmul_const(lt, ltt, g * uaug)
    gam_cc = gam_all[:, :, 0:c]
    gam = gam_all[:, :, c:c + 1]
    dec = jnp.where(cc <= r, jnp.exp(jnp.where(cc <= r, gam_cc, 0.0)), 0.0)
    kk = mm_nt(k, k)
    lm = jnp.where(cc < r, beta * kk * dec, 0.0)
    t = inv_unit_lower(lm) if t_given is None else inv_given(lm, t_given)
    eg = jnp.exp(gam)
    sol = mm_hl(t, jnp.concatenate([cv * beta, k * (beta * eg)], axis=2))
    u, w = sol[:, :, 0:DN_DIM], sol[:, :, DN_DIM:2 * DN_DIM]
    qk = jnp.where(cc <= r, mm_nt(q, k) * dec, 0.0)
    glast = jnp.sum(g, axis=1, keepdims=True)
    qd = q * eg
    kd = k * jnp.exp(glast - gam)
    un = u - mm(w, s)
    o = mm(qd, s) + mm(qk, un)
    s_new = s * jnp.exp(glast) + mm_tn(kd, un)
    on = o * lax.rsqrt(jnp.mean(o * o, axis=-1, keepdims=True) + NORM_EPS) * gain * _silu(z)
    return on, s_new, t


def _dn_gates(ab, al_row, dt_row):
    pre = ab + dt_row
    return -jnp.exp(al_row) * _softplus(pre), _sigmoid(ab), _sigmoid(pre)


def _dn_chains(cacts, gates, z_ref):
    cq, ck, cv, g, beta, z = [], [], [], [], [], []
    for bi, cact in enumerate(cacts):
        for h in range(DN_HEADS):
            cq.append(cact[:, h * DN_DIM:(h + 1) * DN_DIM])
            ck.append(cact[:, DN_WIDTH + h * DN_DIM:DN_WIDTH + (h + 1) * DN_DIM])
            cv.append(cact[:, 2 * DN_WIDTH + h * DN_DIM:2 * DN_WIDTH + (h + 1) * DN_DIM])
            g.append(gates[bi][0][:, h:h + 1])
            beta.append(gates[bi][1][:, DN_HEADS + h:DN_HEADS + h + 1])
            z.append(z_ref[bi, :, h * DN_DIM:(h + 1) * DN_DIM])
    return tuple(jnp.stack(v) for v in (cq, ck, cv, g, beta, z))


def _conv_rows(xe_ref, b, w_ref):
    y = w_ref[0:1, :] * xe_ref[b, pl.ds(5, DN_CHUNK), :]
    for i in range(1, DN_CONV):
        y = y + w_ref[i:i + 1, :] * xe_ref[b, pl.ds(5 + i, DN_CHUNK), :]
    return y


def dn_fwd(qkv, z, ab, conv_w, alog, dtb, gain):
    bsz, t, _ = qkv.shape
    nc = t // DN_CHUNK
    c = DN_CHUNK
    nh = bsz * DN_HEADS

    def body(qkv_ref, z_ref, ab_ref, w_ref, al_ref, dt_ref, g_ref, o_ref, sall_ref, tall_ref, xe, s_sc):
        n = pl.program_id(0)

        @pl.when(n == 0)
        def _():
            xe[:, 0:8, :] = jnp.zeros((bsz, 8, 3 * DN_WIDTH), f32)
            s_sc[...] = jnp.zeros_like(s_sc)

        lt, ltt = _dn_consts()
        cacts = []
        for b in range(bsz):
            xe[b, 8:8 + c, :] = qkv_ref[b].astype(f32)
            cacts.append(_silu(_conv_rows(xe, b, w_ref)))
            xe[b, 0:8, :] = xe[b, c:c + 8, :]
        gates = [_dn_gates(ab_ref[b], al_ref[...], dt_ref[...]) for b in range(bsz)]
        s = s_sc[...]
        sall_ref[0] = s
        on, sn, tt = dn_chunk(*_dn_chains(cacts, gates, z_ref), s, g_ref[...], lt, ltt)
        tall_ref[0] = tt
        s_sc[...] = sn
        for b in range(bsz):
            for h in range(DN_HEADS):
                o_ref[b, :, h * DN_DIM:(h + 1) * DN_DIM] = on[b * DN_HEADS + h].astype(bf16)

    blk = lambda w: pl.BlockSpec((bsz, c, w), lambda n: (0, n, 0))
    full = lambda shp: pl.BlockSpec(shp, lambda n: (0,) * len(shp))
    return pl.pallas_call(
        body, name="dn_fwd", grid=(nc,),
        in_specs=[blk(3 * DN_WIDTH), blk(DN_WIDTH), blk(128), full((8, 3 * DN_WIDTH)), full((1, 128)), full((1, 128)), full((1, 128))],
        out_specs=[blk(DN_WIDTH), pl.BlockSpec((1, nh, DN_DIM, DN_DIM), lambda n: (n, 0, 0, 0)),
                   pl.BlockSpec((1, nh, c, c), lambda n: (n, 0, 0, 0))],
        out_shape=[SDS((bsz, t, D_MODEL), bf16), SDS((nc, nh, DN_DIM, DN_DIM), f32), SDS((nc, nh, c, c), f32)],
        scratch_shapes=[pltpu.VMEM((bsz, c + 8, 3 * DN_WIDTH), f32), pltpu.VMEM((nh, DN_DIM, DN_DIM), f32)],
        compiler_params=_cp(("arbitrary",)),
    )(qkv, z, ab, conv_w, alog, dtb, gain)


def dn_bwd(qkv, z, ab, conv_w, alog, dtb, gain, sall, tall, do):
    bsz, t, _ = qkv.shape
    nc = t // DN_CHUNK
    c = DN_CHUNK
    nh = bsz * DN_HEADS
    w3 = 3 * DN_WIDTH

    def body(qkv_ref, prev_ref, z_ref, ab_ref, w_ref, al_ref, dt_ref, g_ref, sall_ref, tall_ref, do_ref,
             dp_ref, dw_ref, dal_ref, ddt_ref, dg_ref, xe, dye, dc_sc, ds_sc):
        n = pl.program_id(0)
        first = (nc - 1 - n) == 0

        @pl.when(n == 0)
        def _():
            dye[:, c:c + 8, :] = jnp.zeros((bsz, 8, w3), f32)
            ds_sc[...] = jnp.zeros_like(ds_sc)
            dw_ref[...] = jnp.zeros_like(dw_ref)
            dal_ref[...] = jnp.zeros_like(dal_ref)
            ddt_ref[...] = jnp.zeros_like(ddt_ref)
            dg_ref[...] = jnp.zeros_like(dg_ref)

        lt, ltt = _dn_consts()
        lane_c = lax.broadcasted_iota(jnp.int32, (c, 128), 1)
        ys, sigs = [], []
        for b in range(bsz):
            xe[b, 0:8, :] = jnp.where(first, 0.0, prev_ref[b, 8:16, :].astype(f32))
            xe[b, 8:8 + c, :] = qkv_ref[b].astype(f32)
            ys.append(_conv_rows(xe, b, w_ref))
            sigs.append(_sigmoid(ys[b]))
        gates = [_dn_gates(ab_ref[b], al_ref[...], dt_ref[...]) for b in range(bsz)]
        ops = _dn_chains([y * sg for y, sg in zip(ys, sigs)], gates, z_ref)
        tt = tall_ref[0]
        _, vjp = jax.vjp(lambda *p: dn_chunk(*p, lt, ltt, t_given=tt)[0:2], *ops, sall_ref[0], g_ref[...])
        don = jnp.stack([do_ref[b, :, h * DN_DIM:(h + 1) * DN_DIM] for b in range(bsz) for h in range(DN_HEADS)])
        dcq, dck, dcv, dg, dbeta, dzz, dsp, dgn = vjp((don, ds_sc[...]))
        ds_sc[...] = dsp
        dg_ref[...] += dgn
        for b in range(bsz):
            dgate = jnp.zeros((c, 128), f32)
            for h in range(DN_HEADS):
                i = b * DN_HEADS + h
                dc_sc[b, :, h * DN_DIM:(h + 1) * DN_DIM] = dcq[i]
                dc_sc[b, :, DN_WIDTH + h * DN_DIM:DN_WIDTH + (h + 1) * DN_DIM] = dck[i]
                dc_sc[b, :, 2 * DN_WIDTH + h * DN_DIM:2 * DN_WIDTH + (h + 1) * DN_DIM] = dcv[i]
                dp_ref[b, :, C_Z + h * DN_DIM:C_Z + (h + 1) * DN_DIM] = dzz[i].astype(bf16)
                dgate = dgate + jnp.where(lane_c == h, dg[i], 0.0) + jnp.where(lane_c == DN_HEADS + h, dbeta[i], 0.0)
            gg, beta, sig_pre = gates[b]
            is_g = lane_c < DN_HEADS
            dpre = jnp.where(is_g, dgate * (-jnp.exp(al_ref[...])) * sig_pre, 0.0)
            dp_ref[b, :, C_AB:C_AB + 128] = (dpre + jnp.where(is_g, 0.0, dgate * beta * (1.0 - beta))).astype(bf16)
            dp_ref[b, :, C_AB + 128:DN_COLS] = jnp.zeros((c, DN_COLS - C_AB - 128), bf16)
            dal_ref[...] += jnp.sum(jnp.where(is_g, dgate * gg, 0.0), axis=0, keepdims=True)
            ddt_ref[...] += jnp.sum(dpre, axis=0, keepdims=True)
            y, sig = ys[b], sigs[b]
            dy = dc_sc[b] * (sig * (1.0 + y * (1.0 - sig)))
            dye[b, 0:c, :] = dy
            dx = w_ref[3:4, :] * dy
            for i in range(DN_CONV - 1):
                dx = dx + w_ref[i:i + 1, :] * dye[b, pl.ds(3 - i, c), :]
            dp_ref[b, :, 0:w3] = dx.astype(bf16)
            for i in range(DN_CONV):
                dw_ref[i:i + 1, :] += jnp.sum(dy * xe[b, pl.ds(5 + i, c), :], axis=0, keepdims=True)
            dye[b, c:c + 8, :] = dye[b, 0:8, :]

    rev = lambda w: pl.BlockSpec((bsz, c, w), lambda n: (0, nc - 1 - n, 0))
    full = lambda shp: pl.BlockSpec(shp, lambda n: (0,) * len(shp))
    prev = pl.BlockSpec((bsz, 16, w3), lambda n: (0, jnp.maximum((nc - 1 - n) * (c // 16) - 1, 0), 0))
    return pl.pallas_call(
        body, name="dn_bwd", grid=(nc,),
        in_specs=[rev(w3), prev, rev(DN_WIDTH), rev(128), full((8, w3)), full((1, 128)), full((1, 128)), full((1, 128)),
                  pl.BlockSpec((1, nh, DN_DIM, DN_DIM), lambda n: (nc - 1 - n, 0, 0, 0)),
                  pl.BlockSpec((1, nh, c, c), lambda n: (nc - 1 - n, 0, 0, 0)), rev(DN_WIDTH)],
        out_specs=[rev(DN_COLS), full((8, w3)), full((1, 128)), full((1, 128)), full((1, 128))],
        out_shape=[SDS((bsz, t, IN_PAD), bf16), SDS((8, w3), f32), SDS((1, 128), f32), SDS((1, 128), f32), SDS((1, 128), f32)],
        scratch_shapes=[pltpu.VMEM((bsz, c + 8, w3), f32), pltpu.VMEM((bsz, c + 8, w3), f32), pltpu.VMEM((bsz, c, w3), f32),
                        pltpu.VMEM((nh, DN_DIM, DN_DIM), f32)],
        compiler_params=_cp(("arbitrary",)),
    )(qkv, qkv, z, ab, conv_w, alog, dtb, gain, sall, tall, do)


SB_TILE = 256
SB_QTILE, SB_KTILE = 256, 256
SB_PAIRS = SB_HEADS // 2


def sb_fwd(sbqkv, gq, gk, mix):
    bsz, t, _ = sbqkv.shape
    bq = min(SB_QTILE, t)
    blk = max(min(SB_KTILE, t), bq)
    nq = t // bq
    scale = SB_DIM ** -0.5

    def body(q_ref, k_ref, v_ref, gq_ref, gk_ref, mix_in, o_ref, l_ref, q2_sc, kn_sc, v_sc):
        bavg = _group_avg_mats()
        lane = lax.broadcasted_iota(jnp.int32, (1, 128), 1)
        first = lane < SB_DIM
        for p in range(SB_PAIRS):
            ls = slice(p * 128, (p + 1) * 128)
            qn = _pair_norm(q_ref[0, :, ls].astype(f32), gq_ref[...], bavg)
            kn_sc[p] = _pair_norm(k_ref[0, :, ls].astype(f32), gk_ref[...], bavg).astype(bf16)
            v_sc[p] = v_ref[0, :, ls].astype(bf16)
            q2_sc[2 * p] = jnp.where(first, qn, 0.0).astype(bf16)
            q2_sc[2 * p + 1] = jnp.where(first, 0.0, qn).astype(bf16)
        r, c = _iota2((blk, blk))
        ustrict = (r > c).astype(bf16)
        r2, c2 = _iota2((2 * bq, blk))

        def tile(q2s, ks, carry, causal):
            out = []
            for p in range(SB_PAIRS):
                acc, rr = carry[2 * p], carry[2 * p + 1]
                zz = lax.dot_general(q2s[p], kn_sc[p, pl.ds(ks, blk), :], NT, preferred_element_type=f32) * scale
                sp = _softplus(zz)
                lm = -sp if causal is None else jnp.where(causal, -sp, 0.0)
                rem = _dot_x2c(lm, ustrict)
                wgt = jnp.exp(zz - sp + rem + rr)
                if causal is not None:
                    wgt = jnp.where(causal, wgt, 0.0)
                out += [acc + _pdot(wgt.astype(bf16), v_sc[p, pl.ds(ks, blk), :]), rr + jnp.sum(lm, axis=1, keepdims=True)]
            return tuple(out)

        def qloop(qi, _):
            qs = pl.multiple_of(qi * bq, bq)
            kd = qs // blk
            causal = c2 < (r2 & (bq - 1)) + (qs - kd * blk)
            q2s = [jnp.concatenate([q2_sc[2 * p, pl.ds(qs, bq), :], q2_sc[2 * p + 1, pl.ds(qs, bq), :]], axis=0)
                   for p in range(SB_PAIRS)]
            zero = (jnp.zeros((2 * bq, 128), f32), jnp.zeros((2 * bq, 1), f32)) * SB_PAIRS
            carry = lax.fori_loop(1, kd + 1, lambda i, cr: tile(q2s, pl.multiple_of((kd - i) * blk, blk), cr, None),
                                  tile(q2s, pl.multiple_of(kd * blk, blk), zero, causal))
            for p in range(SB_PAIRS):
                acc, rr = carry[2 * p], carry[2 * p + 1]
                o_ref[0, pl.ds(qs, bq), p * 128:(p + 1) * 128] = jnp.where(first, acc[0:bq], acc[bq:2 * bq]).astype(bf16)
                l_ref[0, pl.ds(qs, bq), p * 128:(p + 1) * 128] = jnp.where(first, rr[0:bq], rr[bq:2 * bq])
            return 0

        lax.fori_loop(0, nq, qloop, 0)

    col = lambda off: pl.BlockSpec((1, t, SB_WIDTH), lambda b: (b, 0, off))
    gsp = pl.BlockSpec((1, 128), lambda b: (0, 0))
    return pl.pallas_call(
        body, name="sb_fwd", grid=(bsz,),
        in_specs=[col(0), col(1), col(2), gsp, gsp, pl.BlockSpec(memory_space=pl.ANY)],
        out_specs=[col(DN_WIDTH // SB_WIDTH), col(0)],
        out_shape=[SDS(mix.shape, bf16), SDS((bsz, t, SB_WIDTH), f32)],
        input_output_aliases={5: 0},
        scratch_shapes=[pltpu.VMEM((2 * SB_PAIRS, t, 128), bf16), pltpu.VMEM((SB_PAIRS, t, 128), bf16),
                        pltpu.VMEM((SB_PAIRS, t, 128), bf16)],
        compiler_params=_cp(("arbitrary",)),
    )(sbqkv, sbqkv, sbqkv, gq, gk, mix)


def sb_bwd(sbqkv, gq, gk, ltot, do, dproj):
    bsz, t, _ = sbqkv.shape
    blk = min(SB_TILE, t)
    nq = t // blk
    scale = SB_DIM ** -0.5

    def body(q_ref, k_ref, v_ref, gq_ref, gk_ref, l_ref, do_ref, dp_in, dp_ref, dgq_ref, dgk_ref,
             q2_sc, kn_sc, v_sc, do2_sc, dqn_sc, dkn_sc, dv_sc):
        bavg = _group_avg_mats()
        lane = lax.broadcasted_iota(jnp.int32, (1, 128), 1)
        first = lane < SB_DIM
        fq = lambda x, g: _pair_norm(x, g, bavg)
        vjps = []
        for p in range(SB_PAIRS):
            ls = slice(p * 128, (p + 1) * 128)
            qn, q_vjp = jax.vjp(fq, q_ref[0, :, ls].astype(f32), gq_ref[...])
            kn, k_vjp = jax.vjp(fq, k_ref[0, :, ls].astype(f32), gk_ref[...])
            vjps.append((q_vjp, k_vjp))
            kn_sc[p] = kn.astype(bf16)
            v_sc[p] = v_ref[0, :, ls].astype(bf16)
            dov = do_ref[0, :, ls]
            q2_sc[2 * p] = jnp.where(first, qn, 0.0).astype(bf16)
            q2_sc[2 * p + 1] = jnp.where(first, 0.0, qn).astype(bf16)
            do2_sc[2 * p] = jnp.where(first, dov, 0.0).astype(bf16)
            do2_sc[2 * p + 1] = jnp.where(first, 0.0, dov).astype(bf16)
        dkn_sc[...] = jnp.zeros_like(dkn_sc)
        dv_sc[...] = jnp.zeros_like(dv_sc)
        r, c = _iota2((blk, blk))
        pincl = (r <= c).astype(bf16)
        pstrict = (r < c).astype(bf16)
        r2, c2 = _iota2((2 * blk, blk))
        causal = c2 < (r2 & (blk - 1))

        def tile(q2s, do2s, lts, ks, carry, diag):
            out = []
            for p in range(SB_PAIRS):
                dq, cs, ce = carry[3 * p:3 * p + 3]
                q2, do2 = q2s[p], do2s[p]
                kb = kn_sc[p, pl.ds(ks, blk), :]
                zz = lax.dot_general(q2, kb, NT, preferred_element_type=f32) * scale
                sp = _softplus(zz)
                lm = jnp.where(causal, -sp, 0.0) if diag else -sp
                pre = _dot_x2c(lm, pincl)
                lp = zz - sp
                wgt = jnp.exp(lp + (lts[p] - cs - pre))
                if diag:
                    wgt = jnp.where(causal, wgt, 0.0)
                dw = lax.dot_general(do2, v_sc[p, pl.ds(ks, blk), :], NT, preferred_element_type=f32)
                e = wgt * dw
                ee = ce + _dot_x2c(e, pstrict)
                sig = jnp.exp(lp)
                dz = (e * (1.0 - sig) - ee * sig) * scale
                if diag:
                    dz = jnp.where(causal, dz, 0.0)
                dz = dz.astype(bf16)
                dkn_sc[p, pl.ds(ks, blk), :] += lax.dot_general(dz, q2, TN, preferred_element_type=f32)
                dv_sc[p, pl.ds(ks, blk), :] += lax.dot_general(wgt.astype(bf16), do2, TN, preferred_element_type=f32)
                out += [dq + _pdot(dz, kb), cs + jnp.sum(lm, axis=1, keepdims=True), ce + jnp.sum(e, axis=1, keepdims=True)]
            return tuple(out)

        def qloop(qi, _):
            qs = pl.multiple_of(qi * blk, blk)
            rows = pl.ds(qs, blk)
            q2s = [jnp.concatenate([q2_sc[2 * p, rows, :], q2_sc[2 * p + 1, rows, :]], axis=0) for p in range(SB_PAIRS)]
            do2s = [jnp.concatenate([do2_sc[2 * p, rows, :], do2_sc[2 * p + 1, rows, :]], axis=0) for p in range(SB_PAIRS)]
            lts = [jnp.concatenate([l_ref[0, rows, p * 128:p * 128 + 1], l_ref[0, rows, p * 128 + SB_DIM:p * 128 + SB_DIM + 1]],
                                   axis=0) for p in range(SB_PAIRS)]
            z1 = jnp.zeros((2 * blk, 1), f32)
            carry = lax.fori_loop(0, qi, lambda kj, cr: tile(q2s, do2s, lts, pl.multiple_of(kj * blk, blk), cr, False),
                                  (jnp.zeros((2 * blk, 128), f32), z1, z1) * SB_PAIRS)
            carry = tile(q2s, do2s, lts, qs, carry, True)
            for p in range(SB_PAIRS):
                dq = carry[3 * p]
                dqn_sc[p, rows, :] = jnp.where(first, dq[0:blk], dq[blk:2 * blk])
            return 0

        lax.fori_loop(0, nq, qloop, 0)
        dgq_tot, dgk_tot = jnp.zeros((1, 128), f32), jnp.zeros((1, 128), f32)
        for p in range(SB_PAIRS):
            ls = slice(p * 128, (p + 1) * 128)
            dq_pre, dgq = vjps[p][0](dqn_sc[p])
            dk_pre, dgk = vjps[p][1](dkn_sc[p])
            dp_ref[0, :, p * 128:(p + 1) * 128] = dq_pre.astype(bf16)
            dp_ref[0, :, SB_WIDTH + p * 128:SB_WIDTH + (p + 1) * 128] = dk_pre.astype(bf16)
            dp_ref[0, :, 2 * SB_WIDTH + p * 128:2 * SB_WIDTH + (p + 1) * 128] = dv_sc[p].astype(bf16)
            dgq_tot, dgk_tot = dgq_tot + dgq, dgk_tot + dgk
        dgq_ref[0] = jnp.broadcast_to(dgq_tot, (8, 128))
        dgk_ref[0] = jnp.broadcast_to(dgk_tot, (8, 128))

    col = lambda off: pl.BlockSpec((1, t, SB_WIDTH), lambda b: (b, 0, off), pipeline_mode=pl.Buffered(1))
    gsp = pl.BlockSpec((1, 128), lambda b: (0, 0))
    gout = pl.BlockSpec((1, 8, 128), lambda b: (b, 0, 0))
    return pl.pallas_call(
        body, name="sb_bwd", grid=(bsz,),
        in_specs=[col(0), col(1), col(2), gsp, gsp, col(0), col(0), pl.BlockSpec(memory_space=pl.ANY)],
        out_specs=[pl.BlockSpec((1, t, 3 * SB_WIDTH), lambda b: (b, 0, C_SB // (3 * SB_WIDTH)), pipeline_mode=pl.Buffered(1)),
                   gout, gout],
        out_shape=[SDS(dproj.shape, bf16)] + [SDS((bsz, 8, 128), f32)] * 2,
        input_output_aliases={7: 0},
        scratch_shapes=[pltpu.VMEM((2 * SB_PAIRS, t, 128), bf16), pltpu.VMEM((SB_PAIRS, t, 128), bf16),
                        pltpu.VMEM((SB_PAIRS, t, 128), bf16), pltpu.VMEM((2 * SB_PAIRS, t, 128), bf16),
                        pltpu.VMEM((SB_PAIRS, t, 128), f32), pltpu.VMEM((SB_PAIRS, t, 128), f32), pltpu.VMEM((SB_PAIRS, t, 128), f32)],
        compiler_params=_cp(("arbitrary",)),
    )(sbqkv, sbqkv, sbqkv, gq, gk, ltot, do, dproj)


SG_STEP = 512


def sg_pair(u, v, gain, wa, wb, ba, bb, bavg):
    r, c = _iota2((SG_CHUNK, SG_CHUNK))
    lane = lax.broadcasted_iota(jnp.int32, (1, 128), 1)
    first = lane < SG_DIM
    vn = _pair_norm(_gelu(v), gain, bavg)
    tri = c <= r
    mixed = (mm(jnp.where(tri, wa, 0.0), jnp.where(first, vn, 0.0)) + mm(jnp.where(tri, wb, 0.0), jnp.where(first, 0.0, vn))
             + jnp.where(first, ba, bb))
    return _gelu(u) * mixed


def sg_fwd(sguv, gain, w, bt, mix):
    bsz, t, _ = sguv.shape
    rows = min(SG_STEP, t)

    def body(uv_ref, g_ref, w_ref, b_ref, mix_in, o_ref):
        bavg = _group_avg_mats()
        for r0 in range(0, rows, SG_CHUNK):
            rs = slice(r0, r0 + SG_CHUNK)
            for p in range(2):
                ls = slice(p * 128, (p + 1) * 128)
                o_ref[0, rs, ls] = sg_pair(uv_ref[0, rs, ls].astype(f32),
                                           uv_ref[0, rs, SG_WIDTH + p * 128:SG_WIDTH + (p + 1) * 128].astype(f32), g_ref[:, ls],
                                           w_ref[2 * p], w_ref[2 * p + 1], b_ref[:, 2 * p:2 * p + 1], b_ref[:, 2 * p + 1:2 * p + 2],
                                           bavg).astype(bf16)

    full = lambda shp: pl.BlockSpec(shp, lambda b, n: (0,) * len(shp))
    return pl.pallas_call(
        body, name="sg_fwd", grid=(bsz, t // rows),
        in_specs=[pl.BlockSpec((1, rows, 2 * SG_WIDTH), lambda b, n: (b, n, 0)), full((1, SG_WIDTH)),
                  full((SG_GROUPS, SG_CHUNK, SG_CHUNK)), full((SG_CHUNK, 128)), pl.BlockSpec(memory_space=pl.ANY)],
        out_specs=pl.BlockSpec((1, rows, SG_WIDTH), lambda b, n: (b, n, (DN_WIDTH + SB_WIDTH) // SG_WIDTH)),
        out_shape=SDS(mix.shape, bf16), input_output_aliases={4: 0},
        compiler_params=_cp(("arbitrary", "arbitrary")),
    )(sguv, gain, w, bt, mix)


def sg_bwd(sguv, gain, w, bt, do, dproj):
    bsz, t, _ = sguv.shape
    rows = min(SG_STEP, t)

    def body(uv_ref, g_ref, w_ref, b_ref, do_ref, dp_in, duv_ref, dg_ref, dw_ref, db_ref):
        @pl.when((pl.program_id(0) == 0) & (pl.program_id(1) == 0))
        def _():
            dg_ref[...] = jnp.zeros_like(dg_ref)
            dw_ref[...] = jnp.zeros_like(dw_ref)
            db_ref[...] = jnp.zeros_like(db_ref)

        bavg = _group_avg_mats()
        lane = lax.broadcasted_iota(jnp.int32, (SG_CHUNK, 128), 1)
        dbt = jnp.zeros((SG_CHUNK, 128), f32)
        dgs, dws = [jnp.zeros((1, 128), f32)] * 2, [jnp.zeros((SG_CHUNK, SG_CHUNK), f32)] * SG_GROUPS
        for r0 in range(0, rows, SG_CHUNK):
            rs = slice(r0, r0 + SG_CHUNK)
            for p in range(2):
                ls = slice(p * 128, (p + 1) * 128)
                vs = slice(SG_WIDTH + p * 128, SG_WIDTH + (p + 1) * 128)
                prim = (uv_ref[0, rs, ls].astype(f32), uv_ref[0, rs, vs].astype(f32), g_ref[:, ls], w_ref[2 * p], w_ref[2 * p + 1],
                        b_ref[:, 2 * p:2 * p + 1], b_ref[:, 2 * p + 1:2 * p + 2])
                _, vjp = jax.vjp(lambda *a: sg_pair(*a, bavg), *prim)
                du, dv, dgn, dwa, dwb, dba, dbb = vjp(do_ref[0, rs, ls])
                duv_ref[0, rs, ls] = du.astype(bf16)
                duv_ref[0, rs, vs] = dv.astype(bf16)
                dgs[p] = dgs[p] + dgn
                dws[2 * p], dws[2 * p + 1] = dws[2 * p] + dwa, dws[2 * p + 1] + dwb
                dbt = dbt + jnp.where(lane == 2 * p, dba, 0.0) + jnp.where(lane == 2 * p + 1, dbb, 0.0)
        for p in range(2):
            dg_ref[:, p * 128:(p + 1) * 128] += dgs[p]
        for gidx in range(SG_GROUPS):
            dw_ref[gidx] += dws[gidx]
        db_ref[...] += dbt

    full = lambda shp: pl.BlockSpec(shp, lambda b, n: (0,) * len(shp))
    return pl.pallas_call(
        body, name="sg_bwd", grid=(bsz, t // rows),
        in_specs=[pl.BlockSpec((1, rows, 2 * SG_WIDTH), lambda b, n: (b, n, 0)), full((1, SG_WIDTH)),
                  full((SG_GROUPS, SG_CHUNK, SG_CHUNK)), full((SG_CHUNK, 128)),
                  pl.BlockSpec((1, rows, SG_WIDTH), lambda b, n: (b, n, 0)), pl.BlockSpec(memory_space=pl.ANY)],
        out_specs=[pl.BlockSpec((1, rows, 2 * SG_WIDTH), lambda b, n: (b, n, C_SG // (2 * SG_WIDTH))), full((1, SG_WIDTH)),
                   full((SG_GROUPS, SG_CHUNK, SG_CHUNK)), full((SG_CHUNK, 128))],
        out_shape=[SDS(dproj.shape, bf16), SDS((1, SG_WIDTH), f32), SDS((SG_GROUPS, SG_CHUNK, SG_CHUNK), f32),
                   SDS((SG_CHUNK, 128), f32)],
        input_output_aliases={5: 0},
        compiler_params=_cp(("arbitrary", "arbitrary")),
    )(sguv, gain, w, bt, do, dproj)


def _pad_lanes(v, n=128):
    return jnp.pad(v.reshape(1, -1), ((0, 0), (0, n - v.size)))


def _w_in_runs():
    shard, runs = IN_DIM // N_CHIPS, []
    for s in range(N_CHIPS):
        for a, b, d in ((0, 2048, 0), (2048, 2056, C_AB), (2056, IN_DIM, C_SB)):
            lo, hi = max(shard * s, a), min(shard * (s + 1), b)
            if lo < hi:
                runs.append((s, lo - shard * s, hi - shard * s, d + lo - a))
    return runs


def w_in_from_shards(zone, tr=256):
    def body(z_ref, o_ref):
        o_ref[:, C_AB:C_SB] = jnp.zeros((tr, C_SB - C_AB), zone.dtype)
        for s, a, b, d in _w_in_runs():
            o_ref[:, d:d + b - a] = z_ref[s, :, a:b]

    return pl.pallas_call(
        body, name="w_in_from_shards", grid=(D_MODEL // tr,),
        in_specs=[pl.BlockSpec((N_CHIPS, tr, IN_DIM // N_CHIPS), lambda i: (0, i, 0))],
        out_specs=pl.BlockSpec((tr, IN_PAD), lambda i: (i, 0)), out_shape=SDS((D_MODEL, IN_PAD), zone.dtype),
        compiler_params=_cp(("arbitrary",)))(zone)


def w_in_grad_to_shards(g, tr=256):
    def body(g_ref, o_ref):
        for s, a, b, d in _w_in_runs():
            o_ref[s, :, a:b] = g_ref[:, d:d + b - a]

    return pl.pallas_call(
        body, name="w_in_grad_to_shards", grid=(D_MODEL // tr,),
        in_specs=[pl.BlockSpec((tr, IN_PAD), lambda i: (i, 0))],
        out_specs=pl.BlockSpec((N_CHIPS, tr, IN_DIM // N_CHIPS), lambda i: (0, i, 0)),
        out_shape=SDS((N_CHIPS, D_MODEL, IN_DIM // N_CHIPS), g.dtype), compiler_params=_cp(("arbitrary",)))(g)


def layer_params(p, l):
    return dict(
        g1=p["norm1_g"][l].reshape(1, -1), g2=p["norm2_g"][l].reshape(1, -1),
        conv=jnp.pad(p["conv_w"][l], ((0, 4), (0, 0))), alog=_pad_lanes(p["a_log"][l]), dtb=_pad_lanes(p["dt_bias"][l]),
        dng=p["dn_out_g"][l].reshape(1, -1), gq=jnp.tile(p["sb_q_g"][l].reshape(1, -1), (1, 2)),
        gk=jnp.tile(p["sb_k_g"][l].reshape(1, -1), (1, 2)), sgg=p["sg_v_g"][l].reshape(1, -1), sgw=p["sg_w"][l],
        sgb=jnp.pad(p["sg_b"][l].T, ((0, 0), (0, 124))))


def local_step(x, tgt, small, get_w, put_g, sync_g):
    bsz, t, _ = x.shape
    m = bsz * t
    r3 = lambda a: a.reshape(bsz, t, a.shape[-1])
    r2 = lambda a: a.reshape(m, a.shape[-1])
    xs, saved, ws = x.reshape(m, D_MODEL), [], []
    for l in range(DEPTH):
        sp, w = layer_params(small, l), {}
        w["w_in"] = get_w(l, "in", xs)
        qkv, z, ab, sb, sg, h1 = inproj_fwd(xs, sp["g1"], w["w_in"])
        mix, sall, tall = dn_fwd(r3(qkv), r3(z), r3(ab), sp["conv"], sp["alog"], sp["dtb"], sp["dng"])
        mix, ltot = sb_fwd(r3(sb), sp["gq"], sp["gk"], mix)
        mix = r2(sg_fwd(r3(sg), sp["sgg"], sp["sgw"], sp["sgb"], mix))
        w["w_out"] = get_w(l, "out", mix)
        x2 = outproj_fwd(xs, mix, w["w_out"])
        w["w_ff1"], w["w_ff2"], started = get_w(l, "ff", x2)
        if l + 1 < DEPTH:
            xs_next, rlb = ffn_fwd(x2, sp["g2"] + started, w["w_ff1"], w["w_ff2"])
        else:
            dx, rlb, lossp = ffn_fwd(x2, sp["g2"] + started, w["w_ff1"], w["w_ff2"], tgt=tgt.reshape(m, D_MODEL))
        saved.append(dict(rlb=rlb, h1=h1, x=xs, qkv=qkv, z=z, ab=ab, sb=sb, sg=sg, sall=sall, tall=tall, ltot=ltot, mix=mix, x2=x2))
        ws.append(w)
        xs = xs_next
    gsmall = [None] * DEPTH
    token = jnp.zeros((), f32)
    for l in reversed(range(DEPTH)):
        sp, w, s = layer_params(small, l), ws[l], saved[l]
        dx2, dg2, h2, act, df, dyb = ffn_bwd(s["x2"], sp["g2"] + token, w["w_ff1"], w["w_ff2"], s["rlb"], dx)
        g_ff1 = tn_matmul(h2, df, f"dw_ff1_{l}", col_shards=N_CHIPS)
        g_ff2 = tn_matmul(act, dyb, f"dw_ff2_{l}")
        dodn, dosb, dosg, dx2b = outproj_bwd(dx2, w["w_out"])
        g_out = tn_matmul(s["mix"], dx2b, f"dw_out_{l}")
        token = token + put_g(l, "rest", dict(w_out=g_out, w_ff1=g_ff1, w_ff2=g_ff2))
        dproj, dconv, dalog, ddtb, ddng = dn_bwd(r3(s["qkv"]), r3(s["z"]), r3(s["ab"]), sp["conv"], sp["alog"], sp["dtb"],
                                                 sp["dng"] + token, s["sall"], s["tall"], r3(dodn))
        token = sync_g(ddng)
        dproj, dgq, dgk = sb_bwd(r3(s["sb"]), sp["gq"] + token, sp["gk"], s["ltot"], r3(dosb), dproj)
        dproj, dsgg, dsgw, dsgb = sg_bwd(r3(s["sg"]), sp["sgg"], sp["sgw"], sp["sgb"], r3(dosg), dproj)
        dproj = r2(dproj)
        g_in = tn_matmul(s["h1"], dproj, f"dw_in_{l}")
        token = put_g(l, "in", dict(w_in=g_in))
        dx, dg1 = inproj_bwd(s["x"], sp["g1"] + token, w["w_in"], dproj, dx2)
        token = sync_g(dg1)
        fold = lambda a: (a[:, 0, :].sum(0).reshape(2, SB_DIM)).sum(0)
        gsmall[l] = dict(norm1_g=dg1[0], conv_w=dconv[0:DN_CONV], a_log=dalog[0, 0:DN_HEADS], dt_bias=ddtb[0, 0:DN_HEADS],
                         dn_out_g=ddng[0], sb_q_g=fold(dgq), sb_k_g=fold(dgk), sg_v_g=dsgg[0], sg_w=dsgw,
                         sg_b=dsgb[:, 0:SG_GROUPS].T, norm2_g=dg2[0])
    return lossp, dx.reshape(bsz, t, D_MODEL), gsmall


def _chip_peers(x, y):
    return [(1 - x, y), (x, 1 - y), (1 - x, 1 - y)]


_HBM = pl.BlockSpec(memory_space=pltpu.HBM)
_SEM = pl.BlockSpec(memory_space=pltpu.SEMAPHORE)
_EFFECT = pltpu.SideEffectType.DATAFLOW_SIDE_EFFECTING


def _hbm(a):
    return pltpu.with_memory_space_constraint(a, pltpu.HBM)


def _my_half(ref):
    half = ref.shape[0] // 2
    return ref.at[pl.ds(pl.multiple_of(lax.axis_index("c") * half, 8), half)]


def _exchange_copy(src, land, k, j, send, recv, scatter, halve, waiting):
    x, y, c = lax.axis_index("x"), lax.axis_index("y"), lax.axis_index("c")
    px, py = _chip_peers(x, y)[j]
    me, peer = 2 * x + y, 2 * px + py
    if scatter:
        src = src.at[me if waiting else peer]
    dst = land.at[peer if waiting else me]
    if halve:
        src, dst = _my_half(src), _my_half(dst)
    return pltpu.make_async_remote_copy(src_ref=src, dst_ref=dst, send_sem=send.at[3 * k + j],
                                        recv_sem=recv.at[3 * k + j], device_id=(px, py, c), device_id_type=MESH)


def exchange_start(items, name, scatter, after=None):
    arrs = []
    for a, _, _ in items:
        if not any(a is b for b in arrs):
            arrs.append(a)
    pos = [next(i for i, b in enumerate(arrs) if b is a) for a, _, _ in items]
    shapes = [a.shape if idx is None else a.shape[1:] for a, idx, _ in items]
    lands = [lax.empty(s if scatter else (N_CHIPS,) + s, a.dtype) for (a, _, _), s in zip(items, shapes)]
    na, nl = len(arrs), len(lands)
    n_in = na + nl + (after is not None)

    def body(*refs):
        ins, lnd = refs[:na], refs[na:na + nl]
        send, recv = refs[n_in], refs[n_in + 1]
        token = refs[-1]
        for k, (_, idx, halve) in enumerate(items):
            src = ins[pos[k]] if idx is None else ins[pos[k]].at[idx]
            for j in range(3):
                _exchange_copy(src, lnd[k], k, j, send, recv, scatter, halve, False).start()
        token[...] = jnp.zeros_like(token)

    sems = pltpu.SemaphoreType.DMA((3 * nl,))
    extra = [] if after is None else [after]
    out = pl.pallas_call(
        body, name=name,
        out_shape=(sems, sems, *[pltpu.HBM(a.shape, a.dtype) for a in arrs + lands], SDS((8, 128), f32)),
        in_specs=[_HBM] * (na + nl) + [pl.BlockSpec(memory_space=pl.ANY)] * len(extra),
        out_specs=(_SEM, _SEM, *[_HBM] * (na + nl), pl.BlockSpec(memory_space=pltpu.VMEM)),
        input_output_aliases={i: 2 + i for i in range(na + nl)},
        compiler_params=pltpu.CompilerParams(has_side_effects=_EFFECT),
    )(*[_hbm(a) for a in arrs + lands], *extra)
    thru = out[2:2 + na]
    return dict(send=out[0], recv=out[1], src=[(thru[pos[k]], idx) for k, (_, idx, _) in enumerate(items)],
                halve=[h for _, _, h in items], land=list(out[2 + na:2 + na + nl]), token=out[-1], scatter=scatter)


def exchange_wait(st, ks, after, name):
    arrs = []
    for k in ks:
        if not any(st["src"][k][0] is b for b in arrs):
            arrs.append(st["src"][k][0])
    pos = [next(i for i, b in enumerate(arrs) if b is st["src"][k][0]) for k in ks]
    lands = [st["land"][k] for k in ks]
    na, nl = len(arrs), len(lands)

    def body(*refs):
        ins, lnd = refs[:na], refs[na:na + nl]
        send, recv = refs[na + nl], refs[na + nl + 1]
        for t, k in enumerate(ks):
            idx = st["src"][k][1]
            src = ins[pos[t]] if idx is None else ins[pos[t]].at[idx]
            for j in range(3):
                cp = _exchange_copy(src, lnd[t], k, j, send, recv, st["scatter"], st["halve"][k], True)
                cp.wait_send()
                cp.wait_recv()

    out = pl.pallas_call(
        body, name=name, out_shape=tuple(pltpu.HBM(a.shape, a.dtype) for a in arrs + lands),
        in_specs=[_HBM] * (na + nl) + [_SEM, _SEM, pl.BlockSpec(memory_space=pl.ANY)], out_specs=tuple([_HBM] * (na + nl)),
        input_output_aliases={i: i for i in range(na + nl)},
        compiler_params=pltpu.CompilerParams(has_side_effects=_EFFECT),
    )(*arrs, *lands, st["send"], st["recv"], after)
    for k, (a, idx) in enumerate(st["src"]):
        for p, b in enumerate(arrs):
            if a is b:
                st["src"][k] = (out[p], idx)
    return list(out[na:na + nl])


def _sibling_copy(src, land, i, send, recv, other_half):
    x, y, c = lax.axis_index("x"), lax.axis_index("y"), lax.axis_index("c")
    return pltpu.make_async_remote_copy(src_ref=src.at[:, 1 - c] if other_half else src, dst_ref=land, send_sem=send.at[i],
                                        recv_sem=recv.at[i], device_id=(x, y, 1 - c), device_id_type=MESH)


def sibling_start(arrs, name, other_half=False):
    n = len(arrs)
    lands = [lax.empty((a.shape[0],) + a.shape[2:] if other_half else a.shape, a.dtype) for a in arrs]

    def body(*refs):
        ins, lnd = refs[:n], refs[n:2 * n]
        send, recv = refs[2 * n], refs[2 * n + 1]
        token = refs[-1]
        for i in range(n):
            _sibling_copy(ins[i], lnd[i], i, send, recv, other_half).start()
        token[...] = jnp.zeros_like(token)

    sems = pltpu.SemaphoreType.DMA((n,))
    out = pl.pallas_call(
        body, name=name,
        out_shape=(sems, sems, *[pltpu.HBM(a.shape, a.dtype) for a in arrs + lands], SDS((8, 128), f32)),
        in_specs=[_HBM] * (2 * n), out_specs=(_SEM, _SEM, *[_HBM] * (2 * n), pl.BlockSpec(memory_space=pltpu.VMEM)),
        input_output_aliases={i: 2 + i for i in range(2 * n)},
        compiler_params=pltpu.CompilerParams(has_side_effects=_EFFECT),
    )(*[_hbm(a) for a in arrs + lands])
    return dict(send=out[0], recv=out[1], src=list(out[2:2 + n]), land=list(out[2 + n:2 + 2 * n]), token=out[-1],
                other_half=other_half)


def sibling_wait(st, after, name):
    n = len(st["src"])

    def body(*refs):
        ins, lnd = refs[:n], refs[n:2 * n]
        send, recv = refs[2 * n], refs[2 * n + 1]
        for i in range(n):
            cp = _sibling_copy(ins[i], lnd[i], i, send, recv, st["other_half"])
            cp.wait_send()
            cp.wait_recv()

    out = pl.pallas_call(
        body, name=name, out_shape=tuple(pltpu.HBM(a.shape, a.dtype) for a in st["src"] + st["land"]),
        in_specs=[_HBM] * (2 * n) + [_SEM, _SEM, pl.BlockSpec(memory_space=pl.ANY)], out_specs=tuple([_HBM] * (2 * n)),
        input_output_aliases={i: i for i in range(2 * n)},
        compiler_params=pltpu.CompilerParams(has_side_effects=_EFFECT),
    )(*st["src"], *st["land"], st["send"], st["recv"], after)
    return list(out[:n]), list(out[n:])


def swap_halves(zones, name):
    n = len(zones)

    def body(*refs):
        outs = refs[n:2 * n]
        send, recv = refs[2 * n:]
        x, y, c = lax.axis_index("x"), lax.axis_index("y"), lax.axis_index("c")
        cps = []
        for i in range(n):
            for j, (px, py) in enumerate(_chip_peers(x, y)):
                part = _my_half(outs[i].at[2 * px + py])
                cps.append(pltpu.make_async_remote_copy(src_ref=part, dst_ref=part, send_sem=send.at[3 * i + j],
                                                        recv_sem=recv.at[3 * i + j], device_id=(x, y, 1 - c), device_id_type=MESH))
        for cp in cps:
            cp.start()
        for cp in cps:
            cp.wait_send()
            cp.wait_recv()

    any_spec = pl.BlockSpec(memory_space=pl.ANY)
    return pl.pallas_call(
        body, name=name, in_specs=[any_spec] * n, out_specs=[any_spec] * n, out_shape=[SDS(a.shape, a.dtype) for a in zones],
        input_output_aliases={i: i for i in range(n)},
        scratch_shapes=[pltpu.SemaphoreType.DMA((3 * n,)), pltpu.SemaphoreType.DMA((3 * n,))],
    )(*zones)


def _ids_spec(grid, in_specs, out_specs):
    return pltpu.PrefetchScalarGridSpec(num_scalar_prefetch=1, grid=grid, in_specs=in_specs, out_specs=out_specs)


def pair_sum(ids, a, b, name, tr=512):
    nd, _, rows, cols = a.shape
    tr = min(tr, rows)
    assert rows % tr == 0

    def body(ids_ref, a_ref, b_ref, o_ref):
        o_ref[...] = (a_ref[0].astype(f32) + b_ref[...].astype(f32)).astype(bf16)

    spec = pl.BlockSpec((1, tr, cols), lambda d, i, ids: (d, i, 0))
    return pl.pallas_call(
        body, name=name,
        grid_spec=_ids_spec((nd, rows // tr), [pl.BlockSpec((1, 1, tr, cols), lambda d, i, ids: (d, ids[1], i, 0)), spec], spec),
        out_shape=SDS((nd, rows, cols), bf16), compiler_params=_cp(("arbitrary", "arbitrary")))(ids, a, b)


def allreduce_small(v):
    half = v.shape[0] // 2
    assert half % 8 == 0

    def body(v_ref, o_ref, rbuf, send, recv):
        x, y, c = lax.axis_index("x"), lax.axis_index("y"), lax.axis_index("c")
        o_ref[...] = v_ref[...]

        def exchange(rows, peer, k):
            return pltpu.make_async_remote_copy(src_ref=o_ref.at[rows], dst_ref=rbuf.at[k, rows], send_sem=send.at[k],
                                                recv_sem=recv.at[k], device_id=peer, device_id_type=MESH)

        lo, hi, across_x, across_y = pl.ds(0, half), pl.ds(half, half), (1 - x, y, c), (x, 1 - y, c)
        stages = [[(pl.ds(0, 2 * half), (x, y, 1 - c))], [(lo, across_x), (hi, across_y)], [(lo, across_y), (hi, across_x)]]
        k = 0
        for stage in stages:
            cps = [exchange(rows, peer, k + i) for i, (rows, peer) in enumerate(stage)]
            for cp in cps:
                cp.start()
            for cp in cps:
                cp.wait()
            for i, (rows, _) in enumerate(stage):
                o_ref[rows] = o_ref[rows] + rbuf[k + i, rows]
            k += len(stage)

    vm = pl.BlockSpec(memory_space=pltpu.VMEM)
    return pl.pallas_call(
        body, name="allreduce_small", in_specs=[vm], out_specs=vm, out_shape=SDS(v.shape, f32),
        scratch_shapes=[pltpu.VMEM((5,) + v.shape, f32), pltpu.SemaphoreType.DMA((5,)), pltpu.SemaphoreType.DMA((5,))],
        compiler_params=_cp(),
    )(v)


def sum_partials(ids, zone, mine, name, tr=256):
    _, rows, cols = zone.shape
    tr = min(tr, rows)
    assert rows % tr == 0

    def body(ids_ref, m_ref, z1_ref, z2_ref, z3_ref, o_ref):
        o_ref[...] = ((m_ref[0].astype(f32) + z1_ref[0].astype(f32)) + z2_ref[0].astype(f32)) + z3_ref[0].astype(f32)

    slot = lambda flip: pl.BlockSpec((1, tr, cols), lambda i, ids: (ids[0] ^ flip, i, 0))
    return pl.pallas_call(
        body, name=name,
        grid_spec=_ids_spec((rows // tr,), [slot(0), slot(1), slot(2), slot(3)], pl.BlockSpec((tr, cols), lambda i, ids: (i, 0))),
        out_shape=SDS((rows, cols), f32), compiler_params=_cp(("arbitrary",)),
    )(ids, mine, zone, zone, zone)


def adamw(w, m, v, gs, name, layer=0, prev=None, tr=256):
    hrows, cols = gs[0].shape
    rows = hrows * len(gs)
    tr = min(tr, hrows)
    assert hrows % tr == 0 and w.shape[0] % rows == 0
    off, nth = layer * (rows // tr), hrows // tr

    def body(w_ref, m_ref, v_ref, *rest):
        g_ref, d_ref, mo_ref, vo_ref = rest[-4:]
        if len(gs) == 1:
            g = rest[0][...]
        else:
            g = jnp.where(pl.program_id(0) // nth == lax.axis_index("c"), rest[0][...], rest[1][...])
        mn = ADAM_B1 * m_ref[...] + (1.0 - ADAM_B1) * g
        vn = ADAM_B2 * v_ref[...] + (1.0 - ADAM_B2) * jnp.square(g)
        m_hat = mn / (1.0 - ADAM_B1 ** ADAM_STEP)
        v_hat = vn / (1.0 - ADAM_B2 ** ADAM_STEP)
        g_ref[...] = g
        d_ref[...] = -ADAM_LR * (m_hat / (jnp.sqrt(v_hat) + ADAM_EPS) + ADAM_WD * w_ref[...])
        mo_ref[...] = mn
        vo_ref[...] = vn

    loc = pl.BlockSpec((tr, cols), lambda i: (i % nth, 0))
    glob = pl.BlockSpec((tr, cols), lambda i: (off + i, 0))
    extra = [] if prev is None else list(prev)
    return pl.pallas_call(
        body, name=name, grid=(rows // tr,),
        in_specs=[glob] * 3 + [loc] * len(gs) + [pl.BlockSpec(memory_space=pl.ANY)] * len(extra),
        out_specs=[glob] * 4, out_shape=[SDS(w.shape, f32)] * 4,
        input_output_aliases={3 + len(gs) + j: j for j in range(len(extra))},
        compiler_params=_cp(("arbitrary",)),
    )(w, m, v, *gs, *extra)


BIG = ("w_in", "w_out", "w_ff1", "w_ff2")
SMALL = ("norm1_g", "conv_w", "a_log", "dt_bias", "dn_out_g", "sb_q_g", "sb_k_g", "sg_v_g", "sg_w", "sg_b", "norm2_g")
WEIGHTS = ("norm1_g", "w_in", "conv_w", "a_log", "dt_bias", "dn_out_g", "sb_q_g", "sb_k_g", "sg_v_g", "sg_w", "sg_b",
           "w_out", "norm2_g", "w_ff1", "w_ff2")


PACK_ROWS = 256


def _rows_of(shape):
    n = 1
    for d in shape:
        n *= d
    return -(-n // 1024) * 8, n


def _pack(arrs):
    parts = []
    for a in arrs:
        r, n = _rows_of(a.shape)
        parts.append(jnp.pad(a.reshape(-1), (0, r * 128 - n)).reshape(r, 128))
    rows = sum(p.shape[0] for p in parts)
    parts.append(jnp.zeros((-rows % PACK_ROWS, 128), arrs[0].dtype))
    return jnp.concatenate(parts, axis=0)


def _unpack(packed, shapes):
    out, o = [], 0
    for s in shapes:
        r, n = _rows_of(s)
        out.append(packed[o:o + r].reshape(-1)[0:n].reshape(s))
        o += r
    return out


def kernel(x, norm1_g, w_in, conv_w, a_log, dt_bias, dn_out_g, sb_q_g, sb_k_g, sg_v_g, sg_w, sg_b, w_out, norm2_g, w_ff1, w_ff2, loss_target, m_norm1_g, m_w_in, m_conv_w, m_a_log, m_dt_bias, m_dn_out_g, m_sb_q_g, m_sb_k_g, m_sg_v_g, m_sg_w, m_sg_b, m_w_out, m_norm2_g, m_w_ff1, m_w_ff2, v_norm1_g, v_w_in, v_conv_w, v_a_log, v_dt_bias, v_dn_out_g, v_sb_q_g, v_sb_k_g, v_sg_v_g, v_sg_w, v_sg_b, v_w_out, v_norm2_g, v_w_ff1, v_w_ff2):
    w = dict(norm1_g=norm1_g, w_in=w_in, conv_w=conv_w, a_log=a_log, dt_bias=dt_bias, dn_out_g=dn_out_g, sb_q_g=sb_q_g,
             sb_k_g=sb_k_g, sg_v_g=sg_v_g, sg_w=sg_w, sg_b=sg_b, w_out=w_out, norm2_g=norm2_g, w_ff1=w_ff1, w_ff2=w_ff2)
    mom = dict(norm1_g=m_norm1_g, w_in=m_w_in, conv_w=m_conv_w, a_log=m_a_log, dt_bias=m_dt_bias, dn_out_g=m_dn_out_g,
               sb_q_g=m_sb_q_g, sb_k_g=m_sb_k_g, sg_v_g=m_sg_v_g, sg_w=m_sg_w, sg_b=m_sg_b, w_out=m_w_out, norm2_g=m_norm2_g,
               w_ff1=m_w_ff1, w_ff2=m_w_ff2)
    var = dict(norm1_g=v_norm1_g, w_in=v_w_in, conv_w=v_conv_w, a_log=v_a_log, dt_bias=v_dt_bias, dn_out_g=v_dn_out_g,
               sb_q_g=v_sb_q_g, sb_k_g=v_sb_k_g, sg_v_g=v_sg_v_g, sg_w=v_sg_w, sg_b=v_sg_b, w_out=v_w_out, norm2_g=v_norm2_g,
               w_ff1=v_w_ff1, w_ff2=v_w_ff2)
    chip = 2 * lax.axis_index("x") + lax.axis_index("y")

    wb = [{k: w[k][l].astype(bf16) for k in BIG} for l in range(DEPTH)]
    ags = {0: exchange_start([(conv_w, None, False)] + [(wb[0][k], None, True) for k in BIG], "allgather_start_0", scatter=False)}
    item = lambda l, k: (l, (l == 0) + BIG.index(k))

    def landed(items, after, name):
        ag, ks = ags[items[0][0]], [k for _, k in items]
        zones = exchange_wait(ag, ks, after, name)
        halved = [t for t, k in enumerate(ks) if ag["halve"][k]]
        for t, z in zip(halved, swap_halves([zones[t] for t in halved], name.replace("wait", "pass"))):
            zones[t] = z
        return [lax.dynamic_update_slice_in_dim(z, ag["src"][k][0][None], chip, axis=0) for z, k in zip(zones, ks)]

    def whole(k, z):
        if k == "w_in":
            return w_in_from_shards(z)
        return z if k == "w_ff1" else z.reshape(-1, D_MODEL)

    g_conv, first_in = landed([(0, 0), item(0, "w_in")], x, "allgather_wait_in0")
    small = {k: w[k] for k in SMALL}
    small["conv_w"] = jnp.transpose(g_conv, (1, 2, 0, 3)).reshape(DEPTH, DN_CONV, 3 * DN_WIDTH)
    cache = {}

    def get_w(l, part, after):
        if part == "in":
            return whole("w_in", first_in if l == 0 else landed([item(l, "w_in")], after, f"allgather_wait_in{l}")[0])
        if part == "out":
            zs = landed([item(l, k) for k in ("w_out", "w_ff1", "w_ff2")], after, f"allgather_wait_rest{l}")
            token = jnp.zeros((), f32)
            if l + 1 < DEPTH:
                ags[l + 1] = exchange_start([(wb[l + 1][k], None, True) for k in BIG], f"allgather_start_{l + 1}",
                                            scatter=False, after=zs[0])
                token = ags[l + 1]["token"][0, 0]
            cache[l] = (whole("w_ff1", zs[1]), whole("w_ff2", zs[2]), token)
            return whole("w_out", zs[0])
        return cache[l]

    rs, pending = {}, []
    ids = jnp.stack([chip, lax.axis_index("c")]).astype(jnp.int32)

    def put_g(l, tag, g):
        names = [k for k in BIG if k in g]
        by_dest = [w_in_grad_to_shards(g[k]) if k == "w_in" else g[k] for k in names]
        halves = [a.reshape(N_CHIPS, 2, -1, a.shape[-1]) for a in by_dest]
        st = sibling_start(halves, f"pair_swap_start_{tag}{l}", other_half=True)
        pending.append((l, tag, names, st))
        return st["token"][0, 0]

    def sync_g(after):
        token = jnp.zeros((), f32)
        while pending:
            l, tag, names, st = pending.pop(0)
            halves, got = sibling_wait(st, after, f"pair_swap_wait_{tag}{l}")
            pair = [pair_sum(ids, a, b, f"pair_sum_{k}_{l}") for k, a, b in zip(names, halves, got)]
            rs[l, tag] = dict(exchange_start([(a, None, False) for a in pair], f"scatter_start_{tag}{l}", scatter=True), names=names)
            token = token + rs[l, tag]["token"][0, 0]
        return token

    lossp, grad_x, gsmall = local_step(x, loss_target, small, get_w, put_g, sync_g)

    def sum_group(l, tag, after):
        st = rs[l, tag]
        zones = exchange_wait(st, list(range(len(st["names"]))), after, f"scatter_wait_{tag}{l}")
        sums = [sum_partials(ids, zones[i], st["src"][i][0], f"sum_{k}_{l}") for i, k in enumerate(st["names"])]
        return sibling_start(sums, f"swap_sums_start_{tag}{l}")

    def update_group(l, tag, swap, after, prev):
        sums, others = sibling_wait(swap, after, f"swap_sums_wait_{tag}{l}")
        outs = dict(prev)
        for i, k in enumerate(rs[l, tag]["names"]):
            r2 = lambda a: a.reshape(-1, a.shape[-1])
            outs[k] = adamw(r2(w[k]), r2(mom[k]), r2(var[k]), (sums[i], others[i]), f"adamw_{k}_{l}", layer=l, prev=prev.get(k))
        return outs

    swap_r = sum_group(1, "rest", rs[0, "in"]["token"])
    swap_i = sum_group(1, "in", swap_r["token"])
    done = update_group(1, "rest", swap_r, swap_i["token"], {})
    done = update_group(1, "in", swap_i, done["w_ff2"][0], done)
    res = {}

    full_shapes = [(DEPTH,) + tuple(gsmall[0][k].shape) for k in SMALL]
    packed = _pack([jnp.stack([gsmall[l][k] for l in range(DEPTH)]) for k in SMALL] + [jnp.sum(lossp).reshape(1)])
    *totals, loss = _unpack(allreduce_small(packed), full_shapes + [(1,)])
    loss = loss[0]
    gfull = dict(zip(SMALL, totals))
    cs = 3 * DN_WIDTH // N_CHIPS
    gfull["conv_w"] = lax.dynamic_slice_in_dim(gfull["conv_w"], chip * cs, cs, axis=2)
    gp, wp, mp, vp = (_pack([d[k] for k in SMALL]) for d in (gfull, w, mom, var))
    outs = adamw(wp, mp, vp, (gp,), "adamw_small")
    loc_shapes = [w[k].shape for k in SMALL]
    unp = [_unpack(o, loc_shapes) for o in outs]
    for i, k in enumerate(SMALL):
        res[k] = [unp[j][i] for j in range(4)]

    swap_r = sum_group(0, "rest", outs[0])
    swap_i = sum_group(0, "in", swap_r["token"])
    done = update_group(0, "rest", swap_r, swap_i["token"], done)
    done = update_group(0, "in", swap_i, done["w_ff2"][0], done)
    for k in BIG:
        res[k] = [o.reshape(w[k].shape) for o in done[k]]

    return (loss, grad_x, *[res[k][0] for k in WEIGHTS], *[res[k][1] for k in WEIGHTS], *[res[k][2] for k in WEIGHTS],
            *[res[k][3] for k in WEIGHTS])
```

```python
import functools

import jax
import jax.numpy as jnp
from jax import lax
from jax.experimental import pallas as pl
from jax.experimental.pallas import tpu as pltpu

f32 = jnp.float32
bf16 = jnp.bfloat16
SDS = jax.ShapeDtypeStruct
MESH = pl.DeviceIdType.MESH

NORM_EPS = 1e-6
D_MODEL = 1024
DEPTH = 2
DN_HEADS, DN_DIM, DN_WIDTH, DN_CONV, DN_CHUNK = 4, 128, 512, 4, 64
SB_HEADS, SB_DIM, SB_WIDTH = 4, 64, 256
SG_GROUPS, SG_DIM, SG_WIDTH, SG_CHUNK = 4, 64, 256, 128
D_FF = 4096
IN_DIM = 3336
C_QKV, C_Z, C_AB, C_SB, C_SG, IN_PAD = 0, 1536, 2048, 2304, 3072, 3584
DN_COLS = C_SB
N_CHIPS = 4

ADAM_LR, ADAM_B1, ADAM_B2, ADAM_EPS, ADAM_WD, ADAM_STEP = 0.001, 0.9, 0.999, 1e-08, 0.01, 10

VMEM_LIMIT = 56 * 1024 * 1024


def _cp(sem=None, **kw):
    if sem is not None:
        kw["dimension_semantics"] = sem
    return pltpu.CompilerParams(vmem_limit_bytes=VMEM_LIMIT, **kw)


def _split2(x):
    hi = x.astype(bf16)
    lo = (x - hi.astype(f32)).astype(bf16)
    return hi, lo


NT = (((1,), (1,)), ((), ()))
TN = (((0,), (0,)), ((), ()))
_DIMS2 = dict(nn=(((1,), (0,)), ((), ())), nt=NT, tn=TN)
_DIMS3 = dict(nn=(((2,), (1,)), ((0,), (0,))), nt=(((2,), (2,)), ((0,), (0,))), tn=(((1,), (1,)), ((0,), (0,))))


def _dg(a, b, kind):
    return lax.dot_general(a, b, (_DIMS2 if a.ndim == 2 else _DIMS3)[kind], preferred_element_type=f32)


def _pdot(a, b):
    return _dg(a, b, "nn")


def _dot_hp(a, b):
    ah, al = _split2(a)
    bh, bl = _split2(b)
    return _pdot(ah, bh) + _pdot(ah, bl) + _pdot(al, bh)


def _dot_x2c(a, m):
    lead = a.shape[:-1]
    ah, al = _split2(a.reshape(-1, a.shape[-1]))
    return (_pdot(ah, m) + _pdot(al, m)).reshape(lead + (m.shape[1],))


def _dot_cx2(m, a):
    if a.ndim == 3:
        m = jnp.broadcast_to(m, (a.shape[0],) + m.shape)
    ah, al = _split2(a)
    return _pdot(m, ah) + _pdot(m, al)


def _nt(a, b):
    return _dg(a.astype(bf16), b.astype(bf16), "nt")


def _tn(a, b):
    return _dg(a.astype(bf16), b.astype(bf16), "tn")


def _nn(a, b):
    return _dg(a.astype(bf16), b.astype(bf16), "nn")


@jax.custom_vjp
def mm(a, b):
    return _nn(a, b)


mm.defvjp(lambda a, b: (_nn(a, b), (a, b)), lambda r, g: (_nt(g, r[1]), _tn(r[0], g)))


@jax.custom_vjp
def mm_nt(a, b):
    return _nt(a, b)


mm_nt.defvjp(lambda a, b: (_nt(a, b), (a, b)), lambda r, g: (_nn(g, r[1]), _tn(g, r[0])))


@jax.custom_vjp
def mm_tn(a, b):
    return _tn(a, b)


mm_tn.defvjp(lambda a, b: (_tn(a, b), (a, b)), lambda r, g: (_nt(r[1], g), _nn(r[0], g)))


@jax.custom_vjp
def rmul_const(a, m, mt):
    return _dot_x2c(a, m)


rmul_const.defvjp(lambda a, m, mt: (_dot_x2c(a, m), (m, mt)),
                  lambda r, g: (_dot_x2c(g, r[1]), jnp.zeros_like(r[0]), jnp.zeros_like(r[1])))


@jax.custom_vjp
def lmul_const(m, mt, a):
    return _dot_cx2(m, a)


lmul_const.defvjp(lambda m, mt, a: (_dot_cx2(m, a), (m, mt)),
                  lambda r, g: (jnp.zeros_like(r[0]), jnp.zeros_like(r[1]), _dot_cx2(r[1], g)))


@jax.custom_vjp
def mm_hl(t, x):
    th, tl = _split2(t)
    xb = x.astype(bf16)
    return _pdot(th, xb) + _pdot(tl, xb)


def _mm_hl_bwd(r, g):
    t, x = r
    th, tl = _split2(t)
    gb = g.astype(bf16)
    return _nt(g, x), _dg(th, gb, "tn") + _dg(tl, gb, "tn")


mm_hl.defvjp(lambda t, x: (mm_hl(t, x), (t, x)), _mm_hl_bwd)


def inv_unit_lower(lm):
    c = lm.shape[-1]
    r, cc = _iota2((c, c))
    eye = (r == cc).astype(f32)
    t = eye - lm
    p = -lm
    k = 1
    while 2 * k < c:
        p = _nn(p, p)
        t = t + _nn(t, p)
        k *= 2
    res = eye - t - _dot_hp(lm, t)
    return t + _nn(t, res)


@jax.custom_vjp
def inv_given(lm, t):
    return t


inv_given.defvjp(lambda lm, t: (t, t), lambda t, g: (-_nt(_tn(t, g), t), jnp.zeros_like(t)))


def _sigmoid(x):
    return 1.0 / (1.0 + jnp.exp(-x))


def _softplus(x):
    return jnp.maximum(x, 0.0) + jnp.log(1.0 + jnp.exp(-jnp.abs(x)))


def _silu(x):
    return x * _sigmoid(x)


def _gelu(x):
    return 0.5 * x * (1.0 + jnp.tanh(0.7978845608028654 * (x + 0.044715 * (x * x * x))))


def _iota2(shape):
    return lax.broadcasted_iota(jnp.int32, shape, 0), lax.broadcasted_iota(jnp.int32, shape, 1)


def _group_avg_mats():
    r, c = _iota2((128, 128))
    return jnp.where((r // 64) == (c // 64), 1.0 / 64.0, 0.0).astype(bf16)


def _pair_norm(x, gain, bavg):
    ms = rmul_const(x * x, bavg, bavg)
    return x * lax.rsqrt(ms + NORM_EPS) * gain


def _rms(x):
    r = lax.rsqrt(jnp.mean(x * x, axis=-1, keepdims=True) + NORM_EPS)
    return r


_IN_GROUPS = ((C_QKV, C_Z), (C_Z, C_AB), (C_AB, C_AB + 128), (C_SB, C_SG), (C_SG, IN_PAD))
_IN_DTYPES = (f32, f32, f32, bf16, bf16)


def inproj_fwd(x, g, wp, tm=256):
    m = x.shape[0]

    def body(x_ref, g_ref, w_ref, *outs):
        xv = x_ref[...]
        h = (xv * _rms(xv) * g_ref[...]).astype(bf16)
        outs[-1][...] = h
        for (a, b), o in zip(_IN_GROUPS, outs):
            o[...] = _pdot(h, w_ref[:, a:b]).astype(o.dtype)

    widths = [b - a for a, b in _IN_GROUPS]
    return pl.pallas_call(
        body, name="inproj_fwd", grid=(m // tm,),
        in_specs=[pl.BlockSpec((tm, D_MODEL), lambda i: (i, 0)), pl.BlockSpec((1, D_MODEL), lambda i: (0, 0)),
                  pl.BlockSpec((D_MODEL, IN_PAD), lambda i: (0, 0))],
        out_specs=[pl.BlockSpec((tm, wd), lambda i: (i, 0)) for wd in widths + [D_MODEL]],
        out_shape=[SDS((m, wd), dt) for wd, dt in zip(widths, _IN_DTYPES)] + [SDS((m, D_MODEL), bf16)],
        compiler_params=_cp(("arbitrary",)),
    )(x, g, wp)


def inproj_bwd(x, g, wp, dproj, dres, tm=256):
    m = x.shape[0]

    def body(x_ref, g_ref, w_ref, dp_ref, dr_ref, dx_ref, dg_ref):
        xv = x_ref[...]
        r = _rms(xv)
        xn = xv * r
        gv = g_ref[...]
        dh = lax.dot_general(dp_ref[...], w_ref[...], NT, preferred_element_type=f32)
        dxn = dh * gv
        dx_ref[...] = dr_ref[...] + r * (dxn - xn * jnp.mean(dxn * xn, axis=-1, keepdims=True))

        @pl.when(pl.program_id(0) == 0)
        def _():
            dg_ref[...] = jnp.zeros_like(dg_ref)

        dg_ref[...] += jnp.sum(dh * xn, axis=0, keepdims=True)

    return pl.pallas_call(
        body, name="inproj_bwd", grid=(m // tm,),
        in_specs=[pl.BlockSpec((tm, D_MODEL), lambda i: (i, 0)), pl.BlockSpec((1, D_MODEL), lambda i: (0, 0)),
                  pl.BlockSpec((D_MODEL, IN_PAD), lambda i: (0, 0)), pl.BlockSpec((tm, IN_PAD), lambda i: (i, 0)),
                  pl.BlockSpec((tm, D_MODEL), lambda i: (i, 0))],
        out_specs=[pl.BlockSpec((tm, D_MODEL), lambda i: (i, 0)), pl.BlockSpec((1, D_MODEL), lambda i: (0, 0))],
        out_shape=[SDS((m, D_MODEL), f32), SDS((1, D_MODEL), f32)],
        compiler_params=_cp(("arbitrary",)),
    )(x, g, wp, dproj, dres)


def outproj_fwd(x, mix, wo, tm=512):
    m = x.shape[0]

    def body(x_ref, mix_ref, w_ref, x2_ref):
        x2_ref[...] = x_ref[...] + _pdot(mix_ref[...], w_ref[...])

    row = pl.BlockSpec((tm, D_MODEL), lambda i: (i, 0))
    return pl.pallas_call(
        body, name="outproj_fwd", grid=(m // tm,),
        in_specs=[row, row, pl.BlockSpec((D_MODEL, D_MODEL), lambda i: (0, 0))],
        out_specs=row, out_shape=SDS((m, D_MODEL), f32),
        compiler_params=_cp(("arbitrary",)),
    )(x, mix, wo)


def outproj_bwd(dx2, wo, tm=512):
    m = dx2.shape[0]

    def body(d_ref, w_ref, a_ref, b_ref, c_ref, db_ref):
        db = d_ref[...].astype(bf16)
        db_ref[...] = db
        dm = lax.dot_general(db, w_ref[...], NT, preferred_element_type=f32)
        a_ref[...] = dm[:, 0:DN_WIDTH]
        b_ref[...] = dm[:, DN_WIDTH:DN_WIDTH + SB_WIDTH]
        c_ref[...] = dm[:, DN_WIDTH + SB_WIDTH:D_MODEL]

    row = lambda w: pl.BlockSpec((tm, w), lambda i: (i, 0))
    return pl.pallas_call(
        body, name="outproj_bwd", grid=(m // tm,),
        in_specs=[row(D_MODEL), pl.BlockSpec((D_MODEL, D_MODEL), lambda i: (0, 0))],
        out_specs=[row(DN_WIDTH), row(SB_WIDTH), row(SG_WIDTH), row(D_MODEL)],
        out_shape=[SDS((m, DN_WIDTH), f32), SDS((m, SB_WIDTH), f32), SDS((m, SG_WIDTH), f32), SDS((m, D_MODEL), bf16)],
        compiler_params=_cp(("arbitrary",)),
    )(dx2, wo)


FF_CHUNK = D_FF // N_CHIPS


def _load_weights_once(pairs, sem):
    @pl.when(pl.program_id(0) == 0)
    def _():
        cps = [pltpu.make_async_copy(h, v, sem.at[i]) for i, (h, v) in enumerate(pairs)]
        for c in cps:
            c.start()
        for c in cps:
            c.wait()


def ffn_fwd(x2, g, w1, w2, tgt=None, tm=256):
    m = x2.shape[0]
    head = tgt is not None

    def body(x_ref, g_ref, w1_hbm, w2_hbm, *rest):
        (y_ref, rl_ref), (w1_v, w2_v, sem) = rest[head:head + 2], rest[-3:]
        _load_weights_once(((w1_hbm, w1_v), (w2_hbm, w2_v)), sem)
        xv = x_ref[...]
        h = (xv * _rms(xv) * g_ref[...]).astype(bf16)
        acc = xv
        for j in range(0, D_FF, FF_CHUNK):
            f = _pdot(h, w1_v[j // FF_CHUNK])
            rl = jnp.maximum(f, 0.0)
            rl_ref[:, j:j + FF_CHUNK] = rl.astype(bf16)
            acc = acc + _pdot((rl * rl).astype(bf16), w2_v[j:j + FF_CHUNK, :])
        if not head:
            y_ref[...] = acc
            return
        t_ref, l_ref = rest[0], rest[3]
        e = acc - t_ref[...]
        y_ref[...] = e * (1.0 / D_MODEL)

        @pl.when(pl.program_id(0) == 0)
        def _():
            l_ref[...] = jnp.zeros_like(l_ref)

        l_ref[...] += jnp.sum(e * e, axis=0, keepdims=True) * (0.5 / D_MODEL)

    row = pl.BlockSpec((tm, D_MODEL), lambda i: (i, 0))
    return pl.pallas_call(
        body, name="ffn_fwd_loss" if head else "ffn_fwd", grid=(m // tm,),
        in_specs=[row, pl.BlockSpec((1, D_MODEL), lambda i: (0, 0)), pl.BlockSpec(memory_space=pl.ANY),
                  pl.BlockSpec(memory_space=pl.ANY)] + [row] * head,
        out_specs=[row, pl.BlockSpec((tm, D_FF), lambda i: (i, 0))] + [pl.BlockSpec((1, D_MODEL), lambda i: (0, 0))] * head,
        out_shape=[SDS((m, D_MODEL), f32), SDS((m, D_FF), bf16)] + [SDS((1, D_MODEL), f32)] * head,
        scratch_shapes=[pltpu.VMEM((N_CHIPS, D_MODEL, FF_CHUNK), bf16), pltpu.VMEM((D_FF, D_MODEL), bf16), pltpu.SemaphoreType.DMA((2,))],
        compiler_params=_cp(("arbitrary",)),
    )(x2, g, w1, w2, *([tgt] if head else []))


def ffn_bwd(x2, g, w1, w2, rlb, dy, tm=256):
    m = x2.shape[0]

    def body(x_ref, g_ref, w1_hbm, w2_hbm, rl_ref, dy_ref, dx_ref, dg_ref, h_ref, a_ref, df_ref, dyb_ref, w1_v, w2_v, sem):
        _load_weights_once(((w1_hbm, w1_v), (w2_hbm, w2_v)), sem)
        xv = x_ref[...]
        r = _rms(xv)
        xn = xv * r
        gv = g_ref[...]
        h = (xn * gv).astype(bf16)
        h_ref[...] = h
        dyv = dy_ref[...]
        dyb = dyv.astype(bf16)
        dyb_ref[...] = dyb
        dh = jnp.zeros((tm, D_MODEL), f32)
        for j in range(0, D_FF, FF_CHUNK):
            rl = rl_ref[:, j:j + FF_CHUNK].astype(f32)
            a_ref[:, j:j + FF_CHUNK] = (rl * rl).astype(bf16)
            da = lax.dot_general(dyb, w2_v[j:j + FF_CHUNK, :], NT, preferred_element_type=f32)
            df = (da * (2.0 * rl)).astype(bf16)
            df_ref[:, j:j + FF_CHUNK] = df
            dh = dh + lax.dot_general(df, w1_v[j // FF_CHUNK], NT, preferred_element_type=f32)
        dxn = dh * gv
        dx_ref[...] = dyv + r * (dxn - xn * jnp.mean(dxn * xn, axis=-1, keepdims=True))

        @pl.when(pl.program_id(0) == 0)
        def _():
            dg_ref[...] = jnp.zeros_like(dg_ref)

        dg_ref[...] += jnp.sum(dh * xn, axis=0, keepdims=True)

    row = lambda w: pl.BlockSpec((tm, w), lambda i: (i, 0))
    return pl.pallas_call(
        body, name="ffn_bwd", grid=(m // tm,),
        in_specs=[row(D_MODEL), pl.BlockSpec((1, D_MODEL), lambda i: (0, 0)),
                  pl.BlockSpec(memory_space=pl.ANY), pl.BlockSpec(memory_space=pl.ANY), row(D_FF), row(D_MODEL)],
        out_specs=[row(D_MODEL), pl.BlockSpec((1, D_MODEL), lambda i: (0, 0)), row(D_MODEL), row(D_FF), row(D_FF), row(D_MODEL)],
        out_shape=[SDS((m, D_MODEL), f32), SDS((1, D_MODEL), f32), SDS((m, D_MODEL), bf16), SDS((m, D_FF), bf16),
                   SDS((m, D_FF), bf16), SDS((m, D_MODEL), bf16)],
        scratch_shapes=[pltpu.VMEM((N_CHIPS, D_MODEL, FF_CHUNK), bf16), pltpu.VMEM((D_FF, D_MODEL), bf16), pltpu.SemaphoreType.DMA((2,))],
        compiler_params=_cp(("arbitrary",)),
    )(x2, g, w1, w2, rlb, dy)


def _tile(n, cap):
    best = 128
    for t in range(128, cap + 1, 128):
        if n % t == 0:
            best = t
    return best


def tn_matmul(a, b, name, col_shards=1, tk=2048):
    m, ka = a.shape
    n = b.shape[1]
    ti = _tile(ka, 1024)
    tj = _tile(n // col_shards, 1152)
    tk = min(tk, m)
    nk = m // tk
    jps = (n // col_shards) // tj

    def body(a_ref, b_ref, o_ref, acc):
        k = pl.program_id(2)

        @pl.when(k == 0)
        def _():
            acc[...] = jnp.zeros_like(acc)

        acc[...] += lax.dot_general(a_ref[...], b_ref[...], TN, preferred_element_type=f32)

        @pl.when(k == nk - 1)
        def _():
            o_ref[...] = acc[...].astype(bf16).reshape(o_ref.shape)

    if col_shards == 1:
        out_shape, out_spec = SDS((ka, n), bf16), pl.BlockSpec((ti, tj), lambda i, j, k: (i, j))
    else:
        out_shape = SDS((col_shards, ka, n // col_shards), bf16)
        out_spec = pl.BlockSpec((1, ti, tj), lambda i, j, k: (j // jps, i, j % jps))
    return pl.pallas_call(
        body, name=name, grid=(ka // ti, n // tj, nk),
        in_specs=[pl.BlockSpec((tk, ti), lambda i, j, k: (k, i)), pl.BlockSpec((tk, tj), lambda i, j, k: (k, j))],
        out_specs=out_spec, out_shape=out_shape,
        scratch_shapes=[pltpu.VMEM((ti, tj), f32)],
        compiler_params=_cp(("arbitrary", "arbitrary", "arbitrary")),
    )(a, b)


def _dn_consts():
    c = DN_CHUNK
    r, cc = _iota2((c, c))
    lt = (cc <= r).astype(bf16)
    ltt = (r <= cc).astype(bf16)
    return lt, ltt


def dn_chunk(cq, ck, cv, g, beta, z, s, gain, lt, ltt, t_given=None):
    c = DN_CHUNK
    r, cc = _iota2((c, c))
    q = cq * lax.rsqrt(jnp.sum(cq * cq, axis=-1, keepdims=True) + NORM_EPS) * (DN_DIM ** -0.5)
    k = ck * lax.rsqrt(jnp.sum(ck * ck, axis=-1, keepdims=True) + NORM_EPS)
    r2, c2 = _iota2((c, 128))
    uaug = jnp.where((c2 < c) & (r2 > c2), 1.0, 0.0) + jnp.where(c2 == c, 1.0, 0.0)
    gam_all = lmul_const(lt, ltt, g * uaug)
    gam_cc = gam_all[:, :, 0:c]
    gam = gam_all[:, :, c:c + 1]
    dec = jnp.where(cc <= r, jnp.exp(jnp.where(cc <= r, gam_cc, 0.0)), 0.0)
    kk = mm_nt(k, k)
    lm = jnp.where(cc < r, beta * kk * dec, 0.0)
    t = inv_unit_lower(lm) if t_given is None else inv_given(lm, t_given)
    eg = jnp.exp(gam)
    sol = mm_hl(t, jnp.concatenate([cv * beta, k * (beta * eg)], axis=2))
    u, w = sol[:, :, 0:DN_DIM], sol[:, :, DN_DIM:2 * DN_DIM]
    qk = jnp.where(cc <= r, mm_nt(q, k) * dec, 0.0)
    glast = jnp.sum(g, axis=1, keepdims=True)
    qd = q * eg
    kd = k * jnp.exp(glast - gam)
    un = u - mm(w, s)
    o = mm(qd, s) + mm(qk, un)
    s_new = s * jnp.exp(glast) + mm_tn(kd, un)
    on = o * lax.rsqrt(jnp.mean(o * o, axis=-1, keepdims=True) + NORM_EPS) * gain * _silu(z)
    return on, s_new, t


def _dn_gates(ab, al_row, dt_row):
    pre = ab + dt_row
    return -jnp.exp(al_row) * _softplus(pre), _sigmoid(ab), _sigmoid(pre)


def _dn_chains(cacts, gates, z_ref):
    cq, ck, cv, g, beta, z = [], [], [], [], [], []
    for bi, cact in enumerate(cacts):
        for h in range(DN_HEADS):
            cq.append(cact[:, h * DN_DIM:(h + 1) * DN_DIM])
            ck.append(cact[:, DN_WIDTH + h * DN_DIM:DN_WIDTH + (h + 1) * DN_DIM])
            cv.append(cact[:, 2 * DN_WIDTH + h * DN_DIM:2 * DN_WIDTH + (h + 1) * DN_DIM])
            g.append(gates[bi][0][:, h:h + 1])
            beta.append(gates[bi][1][:, DN_HEADS + h:DN_HEADS + h + 1])
            z.append(z_ref[bi, :, h * DN_DIM:(h + 1) * DN_DIM])
    return tuple(jnp.stack(v) for v in (cq, ck, cv, g, beta, z))


def _conv_rows(xe_ref, b, w_ref):
    y = w_ref[0:1, :] * xe_ref[b, pl.ds(5, DN_CHUNK), :]
    for i in range(1, DN_CONV):
        y = y + w_ref[i:i + 1, :] * xe_ref[b, pl.ds(5 + i, DN_CHUNK), :]
    return y


def dn_fwd(qkv, z, ab, conv_w, alog, dtb, gain):
    bsz, t, _ = qkv.shape
    nc = t // DN_CHUNK
    c = DN_CHUNK
    nh = bsz * DN_HEADS

    def body(qkv_ref, z_ref, ab_ref, w_ref, al_ref, dt_ref, g_ref, o_ref, sall_ref, tall_ref, xe, s_sc):
        n = pl.program_id(0)

        @pl.when(n == 0)
        def _():
            xe[:, 0:8, :] = jnp.zeros((bsz, 8, 3 * DN_WIDTH), f32)
            s_sc[...] = jnp.zeros_like(s_sc)

        lt, ltt = _dn_consts()
        cacts = []
        for b in range(bsz):
            xe[b, 8:8 + c, :] = qkv_ref[b]
            cacts.append(_silu(_conv_rows(xe, b, w_ref)))
            xe[b, 0:8, :] = xe[b, c:c + 8, :]
        gates = [_dn_gates(ab_ref[b], al_ref[...], dt_ref[...]) for b in range(bsz)]
        s = s_sc[...]
        sall_ref[0] = s
        on, sn, tt = dn_chunk(*_dn_chains(cacts, gates, z_ref), s, g_ref[...], lt, ltt)
        tall_ref[0] = tt
        s_sc[...] = sn
        for b in range(bsz):
            for h in range(DN_HEADS):
                o_ref[b, :, h * DN_DIM:(h + 1) * DN_DIM] = on[b * DN_HEADS + h].astype(bf16)

    blk = lambda w: pl.BlockSpec((bsz, c, w), lambda n: (0, n, 0))
    full = lambda shp: pl.BlockSpec(shp, lambda n: (0,) * len(shp))
    return pl.pallas_call(
        body, name="dn_fwd", grid=(nc,),
        in_specs=[blk(3 * DN_WIDTH), blk(DN_WIDTH), blk(128), full((8, 3 * DN_WIDTH)), full((1, 128)), full((1, 128)), full((1, 128))],
        out_specs=[blk(DN_WIDTH), pl.BlockSpec((1, nh, DN_DIM, DN_DIM), lambda n: (n, 0, 0, 0)),
                   pl.BlockSpec((1, nh, c, c), lambda n: (n, 0, 0, 0))],
        out_shape=[SDS((bsz, t, D_MODEL), bf16), SDS((nc, nh, DN_DIM, DN_DIM), f32), SDS((nc, nh, c, c), f32)],
        scratch_shapes=[pltpu.VMEM((bsz, c + 8, 3 * DN_WIDTH), f32), pltpu.VMEM((nh, DN_DIM, DN_DIM), f32)],
        compiler_params=_cp(("arbitrary",)),
    )(qkv, z, ab, conv_w, alog, dtb, gain)


def dn_bwd(qkv, z, ab, conv_w, alog, dtb, gain, sall, tall, do):
    bsz, t, _ = qkv.shape
    nc = t // DN_CHUNK
    c = DN_CHUNK
    nh = bsz * DN_HEADS
    w3 = 3 * DN_WIDTH

    def body(qkv_ref, prev_ref, z_ref, ab_ref, w_ref, al_ref, dt_ref, g_ref, sall_ref, tall_ref, do_ref,
             dp_ref, dw_ref, dal_ref, ddt_ref, dg_ref, xe, dye, dc_sc, ds_sc):
        n = pl.program_id(0)
        first = (nc - 1 - n) == 0

        @pl.when(n == 0)
        def _():
            dye[:, c:c + 8, :] = jnp.zeros((bsz, 8, w3), f32)
            ds_sc[...] = jnp.zeros_like(ds_sc)
            dw_ref[...] = jnp.zeros_like(dw_ref)
            dal_ref[...] = jnp.zeros_like(dal_ref)
            ddt_ref[...] = jnp.zeros_like(ddt_ref)
            dg_ref[...] = jnp.zeros_like(dg_ref)

        lt, ltt = _dn_consts()
        lane_c = lax.broadcasted_iota(jnp.int32, (c, 128), 1)
        ys, sigs = [], []
        for b in range(bsz):
            xe[b, 0:8, :] = jnp.where(first, 0.0, prev_ref[b])
            xe[b, 8:8 + c, :] = qkv_ref[b]
            ys.append(_conv_rows(xe, b, w_ref))
            sigs.append(_sigmoid(ys[b]))
        gates = [_dn_gates(ab_ref[b], al_ref[...], dt_ref[...]) for b in range(bsz)]
        ops = _dn_chains([y * sg for y, sg in zip(ys, sigs)], gates, z_ref)
        tt = tall_ref[0]
        _, vjp = jax.vjp(lambda *p: dn_chunk(*p, lt, ltt, t_given=tt)[0:2], *ops, sall_ref[0], g_ref[...])
        don = jnp.stack([do_ref[b, :, h * DN_DIM:(h + 1) * DN_DIM] for b in range(bsz) for h in range(DN_HEADS)])
        dcq, dck, dcv, dg, dbeta, dzz, dsp, dgn = vjp((don, ds_sc[...]))
        ds_sc[...] = dsp
        dg_ref[...] += dgn
        for b in range(bsz):
            dgate = jnp.zeros((c, 128), f32)
            for h in range(DN_HEADS):
                i = b * DN_HEADS + h
                dc_sc[b, :, h * DN_DIM:(h + 1) * DN_DIM] = dcq[i]
                dc_sc[b, :, DN_WIDTH + h * DN_DIM:DN_WIDTH + (h + 1) * DN_DIM] = dck[i]
                dc_sc[b, :, 2 * DN_WIDTH + h * DN_DIM:2 * DN_WIDTH + (h + 1) * DN_DIM] = dcv[i]
                dp_ref[b, :, C_Z + h * DN_DIM:C_Z + (h + 1) * DN_DIM] = dzz[i].astype(bf16)
                dgate = dgate + jnp.where(lane_c == h, dg[i], 0.0) + jnp.where(lane_c == DN_HEADS + h, dbeta[i], 0.0)
            gg, beta, sig_pre = gates[b]
            is_g = lane_c < DN_HEADS
            dpre = jnp.where(is_g, dgate * (-jnp.exp(al_ref[...])) * sig_pre, 0.0)
            dp_ref[b, :, C_AB:C_AB + 128] = (dpre + jnp.where(is_g, 0.0, dgate * beta * (1.0 - beta))).astype(bf16)
            dp_ref[b, :, C_AB + 128:DN_COLS] = jnp.zeros((c, DN_COLS - C_AB - 128), bf16)
            dal_ref[...] += jnp.sum(jnp.where(is_g, dgate * gg, 0.0), axis=0, keepdims=True)
            ddt_ref[...] += jnp.sum(dpre, axis=0, keepdims=True)
            y, sig = ys[b], sigs[b]
            dy = dc_sc[b] * (sig * (1.0 + y * (1.0 - sig)))
            dye[b, 0:c, :] = dy
            dx = w_ref[3:4, :] * dy
            for i in range(DN_CONV - 1):
                dx = dx + w_ref[i:i + 1, :] * dye[b, pl.ds(3 - i, c), :]
            dp_ref[b, :, 0:w3] = dx.astype(bf16)
            for i in range(DN_CONV):
                dw_ref[i:i + 1, :] += jnp.sum(dy * xe[b, pl.ds(5 + i, c), :], axis=0, keepdims=True)
            dye[b, c:c + 8, :] = dye[b, 0:8, :]

    rev = lambda w: pl.BlockSpec((bsz, c, w), lambda n: (0, nc - 1 - n, 0))
    full = lambda shp: pl.BlockSpec(shp, lambda n: (0,) * len(shp))
    prev = pl.BlockSpec((bsz, 8, w3), lambda n: (0, jnp.maximum((nc - 1 - n) * (c // 8) - 1, 0), 0))
    return pl.pallas_call(
        body, name="dn_bwd", grid=(nc,),
        in_specs=[rev(w3), prev, rev(DN_WIDTH), rev(128), full((8, w3)), full((1, 128)), full((1, 128)), full((1, 128)),
                  pl.BlockSpec((1, nh, DN_DIM, DN_DIM), lambda n: (nc - 1 - n, 0, 0, 0)),
                  pl.BlockSpec((1, nh, c, c), lambda n: (nc - 1 - n, 0, 0, 0)), rev(DN_WIDTH)],
        out_specs=[rev(DN_COLS), full((8, w3)), full((1, 128)), full((1, 128)), full((1, 128))],
        out_shape=[SDS((bsz, t, IN_PAD), bf16), SDS((8, w3), f32), SDS((1, 128), f32), SDS((1, 128), f32), SDS((1, 128), f32)],
        scratch_shapes=[pltpu.VMEM((bsz, c + 8, w3), f32), pltpu.VMEM((bsz, c + 8, w3), f32), pltpu.VMEM((bsz, c, w3), f32),
                        pltpu.VMEM((nh, DN_DIM, DN_DIM), f32)],
        compiler_params=_cp(("arbitrary",)),
    )(qkv, qkv, z, ab, conv_w, alog, dtb, gain, sall, tall, do)


SB_TILE = 256
SB_QTILE, SB_KTILE = 256, 256
SB_STRIP = 128
SB_PAIRS = SB_HEADS // 2


def sb_fwd(sbqkv, gq, gk, mix):
    bsz, t, _ = sbqkv.shape
    bq = min(SB_QTILE, t)
    blk = max(min(SB_KTILE, t), bq)
    nq = t // bq
    st = min(SB_STRIP, bq)
    nst = 2 * bq // st
    scale = SB_DIM ** -0.5

    def body(q_ref, k_ref, v_ref, gq_ref, gk_ref, mix_in, o_ref, l_ref, q2_sc, kn_sc, v_sc):
        bavg = _group_avg_mats()
        lane = lax.broadcasted_iota(jnp.int32, (1, 128), 1)
        first = lane < SB_DIM
        for p in range(SB_PAIRS):
            ls = slice(p * 128, (p + 1) * 128)
            qn = _pair_norm(q_ref[0, :, ls].astype(f32), gq_ref[...], bavg)
            kn_sc[p] = _pair_norm(k_ref[0, :, ls].astype(f32), gk_ref[...], bavg).astype(bf16)
            v_sc[p] = v_ref[0, :, ls].astype(bf16)
            q2_sc[2 * p] = jnp.where(first, qn, 0.0).astype(bf16)
            q2_sc[2 * p + 1] = jnp.where(first, 0.0, qn).astype(bf16)
        r, c = _iota2((blk, blk))
        ustrict = (r > c).astype(bf16)
        r2, c2 = _iota2((2 * bq, blk))

        def tile(q2s, ks, carry, causal):
            out = []
            for p in range(SB_PAIRS):
                for j in range(nst):
                    rows = slice(j * st, (j + 1) * st)
                    acc, rr = carry[2 * (p * nst + j)], carry[2 * (p * nst + j) + 1]
                    zz = lax.dot_general(q2s[p][rows], kn_sc[p, pl.ds(ks, blk), :], NT, preferred_element_type=f32) * scale
                    sp = _softplus(zz)
                    lm = -sp if causal is None else jnp.where(causal[rows], -sp, 0.0)
                    rem = _dot_x2c(lm, ustrict)
                    wgt = jnp.exp(zz - sp + rem + rr)
                    if causal is not None:
                        wgt = jnp.where(causal[rows], wgt, 0.0)
                    out += [acc + _pdot(wgt.astype(bf16), v_sc[p, pl.ds(ks, blk), :]), rr + jnp.sum(lm, axis=1, keepdims=True)]
            return tuple(out)

        def qloop(qi, _):
            qs = pl.multiple_of(qi * bq, bq)
            kd = qs // blk
            causal = c2 < (r2 & (bq - 1)) + (qs - kd * blk)
            q2s = [jnp.concatenate([q2_sc[2 * p, pl.ds(qs, bq), :], q2_sc[2 * p + 1, pl.ds(qs, bq), :]], axis=0)
                   for p in range(SB_PAIRS)]
            zero = (jnp.zeros((st, 128), f32), jnp.zeros((st, 1), f32)) * (SB_PAIRS * nst)
            carry = lax.fori_loop(1, kd + 1, lambda i, cr: tile(q2s, pl.multiple_of((kd - i) * blk, blk), cr, None),
                                  tile(q2s, pl.multiple_of(kd * blk, blk), zero, causal))
            for p in range(SB_PAIRS):
                for j in range(nst // 2):
                    a, b = 2 * (p * nst + j), 2 * (p * nst + j + nst // 2)
                    o_ref[0, pl.ds(qs + j * st, st), p * 128:(p + 1) * 128] = jnp.where(first, carry[a], carry[b]).astype(bf16)
                    l_ref[0, pl.ds(qs + j * st, st), p * 128:(p + 1) * 128] = jnp.where(first, carry[a + 1], carry[b + 1])
            return 0

        lax.fori_loop(0, nq, qloop, 0)

    col = lambda off: pl.BlockSpec((1, t, SB_WIDTH), lambda b: (b, 0, off))
    gsp = pl.BlockSpec((1, 128), lambda b: (0, 0))
    return pl.pallas_call(
        body, name="sb_fwd", grid=(bsz,),
        in_specs=[col(0), col(1), col(2), gsp, gsp, pl.BlockSpec(memory_space=pl.ANY)],
        out_specs=[col(DN_WIDTH // SB_WIDTH), col(0)],
        out_shape=[SDS(mix.shape, bf16), SDS((bsz, t, SB_WIDTH), f32)],
        input_output_aliases={5: 0},
        scratch_shapes=[pltpu.VMEM((2 * SB_PAIRS, t, 128), bf16), pltpu.VMEM((SB_PAIRS, t, 128), bf16),
                        pltpu.VMEM((SB_PAIRS, t, 128), bf16)],
        compiler_params=_cp(("arbitrary",)),
    )(sbqkv, sbqkv, sbqkv, gq, gk, mix)


def sb_bwd(sbqkv, gq, gk, ltot, do, dproj):
    bsz, t, _ = sbqkv.shape
    blk = min(SB_TILE, t)
    nq = t // blk
    scale = SB_DIM ** -0.5

    def body(q_ref, k_ref, v_ref, gq_ref, gk_ref, l_ref, do_ref, dp_in, dp_ref, dgq_ref, dgk_ref,
             q2_sc, kn_sc, v_sc, do2_sc, dqn_sc, dkn_sc, dv_sc):
        bavg = _group_avg_mats()
        lane = lax.broadcasted_iota(jnp.int32, (1, 128), 1)
        first = lane < SB_DIM
        fq = lambda x, g: _pair_norm(x, g, bavg)
        vjps = []
        for p in range(SB_PAIRS):
            ls = slice(p * 128, (p + 1) * 128)
            qn, q_vjp = jax.vjp(fq, q_ref[0, :, ls].astype(f32), gq_ref[...])
            kn, k_vjp = jax.vjp(fq, k_ref[0, :, ls].astype(f32), gk_ref[...])
            vjps.append((q_vjp, k_vjp))
            kn_sc[p] = kn.astype(bf16)
            v_sc[p] = v_ref[0, :, ls].astype(bf16)
            dov = do_ref[0, :, ls]
            q2_sc[2 * p] = jnp.where(first, qn, 0.0).astype(bf16)
            q2_sc[2 * p + 1] = jnp.where(first, 0.0, qn).astype(bf16)
            do2_sc[2 * p] = jnp.where(first, dov, 0.0).astype(bf16)
            do2_sc[2 * p + 1] = jnp.where(first, 0.0, dov).astype(bf16)
        dkn_sc[...] = jnp.zeros_like(dkn_sc)
        dv_sc[...] = jnp.zeros_like(dv_sc)
        r, c = _iota2((blk, blk))
        pincl = (r <= c).astype(bf16)
        pstrict = (r < c).astype(bf16)
        r2, c2 = _iota2((2 * blk, blk))
        causal = c2 < (r2 & (blk - 1))

        def tile(q2s, do2s, lts, ks, carry, diag):
            out = []
            for p in range(SB_PAIRS):
                dq, cs, ce = carry[3 * p:3 * p + 3]
                q2, do2 = q2s[p], do2s[p]
                kb = kn_sc[p, pl.ds(ks, blk), :]
                zz = lax.dot_general(q2, kb, NT, preferred_element_type=f32) * scale
                sp = _softplus(zz)
                lm = jnp.where(causal, -sp, 0.0) if diag else -sp
                pre = _dot_x2c(lm, pincl)
                lp = zz - sp
                wgt = jnp.exp(lp + (lts[p] - cs - pre))
                if diag:
                    wgt = jnp.where(causal, wgt, 0.0)
                dw = lax.dot_general(do2, v_sc[p, pl.ds(ks, blk), :], NT, preferred_element_type=f32)
                e = wgt * dw
                ee = ce + _dot_x2c(e, pstrict)
                sig = jnp.exp(lp)
                dz = (e * (1.0 - sig) - ee * sig) * scale
                if diag:
                    dz = jnp.where(causal, dz, 0.0)
                dz = dz.astype(bf16)
                dkn_sc[p, pl.ds(ks, blk), :] += lax.dot_general(dz, q2, TN, preferred_element_type=f32)
                dv_sc[p, pl.ds(ks, blk), :] += lax.dot_general(wgt.astype(bf16), do2, TN, preferred_element_type=f32)
                out += [dq + _pdot(dz, kb), cs + jnp.sum(lm, axis=1, keepdims=True), ce + jnp.sum(e, axis=1, keepdims=True)]
            return tuple(out)

        def qloop(qi, _):
            qs = pl.multiple_of(qi * blk, blk)
            rows = pl.ds(qs, blk)
            q2s = [jnp.concatenate([q2_sc[2 * p, rows, :], q2_sc[2 * p + 1, rows, :]], axis=0) for p in range(SB_PAIRS)]
            do2s = [jnp.concatenate([do2_sc[2 * p, rows, :], do2_sc[2 * p + 1, rows, :]], axis=0) for p in range(SB_PAIRS)]
            lts = [jnp.concatenate([l_ref[0, rows, p * 128:p * 128 + 1], l_ref[0, rows, p * 128 + SB_DIM:p * 128 + SB_DIM + 1]],
                                   axis=0) for p in range(SB_PAIRS)]
            z1 = jnp.zeros((2 * blk, 1), f32)
            carry = lax.fori_loop(0, qi, lambda kj, cr: tile(q2s, do2s, lts, pl.multiple_of(kj * blk, blk), cr, False),
                                  (jnp.zeros((2 * blk, 128), f32), z1, z1) * SB_PAIRS)
            carry = tile(q2s, do2s, lts, qs, carry, True)
            for p in range(SB_PAIRS):
                dq = carry[3 * p]
                dqn_sc[p, rows, :] = jnp.where(first, dq[0:blk], dq[blk:2 * blk])
            return 0

        lax.fori_loop(0, nq, qloop, 0)
        dgq_tot, dgk_tot = jnp.zeros((1, 128), f32), jnp.zeros((1, 128), f32)
        for p in range(SB_PAIRS):
            ls = slice(p * 128, (p + 1) * 128)
            dq_pre, dgq = vjps[p][0](dqn_sc[p])
            dk_pre, dgk = vjps[p][1](dkn_sc[p])
            dp_ref[0, :, p * 128:(p + 1) * 128] = dq_pre.astype(bf16)
            dp_ref[0, :, SB_WIDTH + p * 128:SB_WIDTH + (p + 1) * 128] = dk_pre.astype(bf16)
            dp_ref[0, :, 2 * SB_WIDTH + p * 128:2 * SB_WIDTH + (p + 1) * 128] = dv_sc[p].astype(bf16)
            dgq_tot, dgk_tot = dgq_tot + dgq, dgk_tot + dgk
        dgq_ref[0] = jnp.broadcast_to(dgq_tot, (8, 128))
        dgk_ref[0] = jnp.broadcast_to(dgk_tot, (8, 128))

    col = lambda off: pl.BlockSpec((1, t, SB_WIDTH), lambda b: (b, 0, off), pipeline_mode=pl.Buffered(1))
    gsp = pl.BlockSpec((1, 128), lambda b: (0, 0))
    gout = pl.BlockSpec((1, 8, 128), lambda b: (b, 0, 0))
    return pl.pallas_call(
        body, name="sb_bwd", grid=(bsz,),
        in_specs=[col(0), col(1), col(2), gsp, gsp, col(0), col(0), pl.BlockSpec(memory_space=pl.ANY)],
        out_specs=[pl.BlockSpec((1, t, 3 * SB_WIDTH), lambda b: (b, 0, C_SB // (3 * SB_WIDTH)), pipeline_mode=pl.Buffered(1)),
                   gout, gout],
        out_shape=[SDS(dproj.shape, bf16)] + [SDS((bsz, 8, 128), f32)] * 2,
        input_output_aliases={7: 0},
        scratch_shapes=[pltpu.VMEM((2 * SB_PAIRS, t, 128), bf16), pltpu.VMEM((SB_PAIRS, t, 128), bf16),
                        pltpu.VMEM((SB_PAIRS, t, 128), bf16), pltpu.VMEM((2 * SB_PAIRS, t, 128), bf16),
                        pltpu.VMEM((SB_PAIRS, t, 128), f32), pltpu.VMEM((SB_PAIRS, t, 128), f32), pltpu.VMEM((SB_PAIRS, t, 128), f32)],
        compiler_params=_cp(("arbitrary",)),
    )(sbqkv, sbqkv, sbqkv, gq, gk, ltot, do, dproj)


SG_STEP = 512


def sg_pair(u, v, gain, wa, wb, ba, bb, bavg):
    r, c = _iota2((SG_CHUNK, SG_CHUNK))
    lane = lax.broadcasted_iota(jnp.int32, (1, 128), 1)
    first = lane < SG_DIM
    vn = _pair_norm(_gelu(v), gain, bavg)
    tri = c <= r
    mixed = (mm(jnp.where(tri, wa, 0.0), jnp.where(first, vn, 0.0)) + mm(jnp.where(tri, wb, 0.0), jnp.where(first, 0.0, vn))
             + jnp.where(first, ba, bb))
    return _gelu(u) * mixed


def sg_fwd(sguv, gain, w, bt, mix):
    bsz, t, _ = sguv.shape
    rows = min(SG_STEP, t)

    def body(uv_ref, g_ref, w_ref, b_ref, mix_in, o_ref):
        bavg = _group_avg_mats()
        for r0 in range(0, rows, SG_CHUNK):
            rs = slice(r0, r0 + SG_CHUNK)
            for p in range(2):
                ls = slice(p * 128, (p + 1) * 128)
                o_ref[0, rs, ls] = sg_pair(uv_ref[0, rs, ls].astype(f32),
                                           uv_ref[0, rs, SG_WIDTH + p * 128:SG_WIDTH + (p + 1) * 128].astype(f32), g_ref[:, ls],
                                           w_ref[2 * p], w_ref[2 * p + 1], b_ref[:, 2 * p:2 * p + 1], b_ref[:, 2 * p + 1:2 * p + 2],
                                           bavg).astype(bf16)

    full = lambda shp: pl.BlockSpec(shp, lambda b, n: (0,) * len(shp))
    return pl.pallas_call(
        body, name="sg_fwd", grid=(bsz, t // rows),
        in_specs=[pl.BlockSpec((1, rows, 2 * SG_WIDTH), lambda b, n: (b, n, 0)), full((1, SG_WIDTH)),
                  full((SG_GROUPS, SG_CHUNK, SG_CHUNK)), full((SG_CHUNK, 128)), pl.BlockSpec(memory_space=pl.ANY)],
        out_specs=pl.BlockSpec((1, rows, SG_WIDTH), lambda b, n: (b, n, (DN_WIDTH + SB_WIDTH) // SG_WIDTH)),
        out_shape=SDS(mix.shape, bf16), input_output_aliases={4: 0},
        compiler_params=_cp(("arbitrary", "arbitrary")),
    )(sguv, gain, w, bt, mix)


def sg_bwd(sguv, gain, w, bt, do, dproj):
    bsz, t, _ = sguv.shape
    rows = min(SG_STEP, t)

    def body(uv_ref, g_ref, w_ref, b_ref, do_ref, dp_in, duv_ref, dg_ref, dw_ref, db_ref):
        @pl.when((pl.program_id(0) == 0) & (pl.program_id(1) == 0))
        def _():
            dg_ref[...] = jnp.zeros_like(dg_ref)
            dw_ref[...] = jnp.zeros_like(dw_ref)
            db_ref[...] = jnp.zeros_like(db_ref)

        bavg = _group_avg_mats()
        lane = lax.broadcasted_iota(jnp.int32, (SG_CHUNK, 128), 1)
        dbt = jnp.zeros((SG_CHUNK, 128), f32)
        dgs, dws = [jnp.zeros((1, 128), f32)] * 2, [jnp.zeros((SG_CHUNK, SG_CHUNK), f32)] * SG_GROUPS
        for r0 in range(0, rows, SG_CHUNK):
            rs = slice(r0, r0 + SG_CHUNK)
            for p in range(2):
                ls = slice(p * 128, (p + 1) * 128)
                vs = slice(SG_WIDTH + p * 128, SG_WIDTH + (p + 1) * 128)
                prim = (uv_ref[0, rs, ls].astype(f32), uv_ref[0, rs, vs].astype(f32), g_ref[:, ls], w_ref[2 * p], w_ref[2 * p + 1],
                        b_ref[:, 2 * p:2 * p + 1], b_ref[:, 2 * p + 1:2 * p + 2])
                _, vjp = jax.vjp(lambda *a: sg_pair(*a, bavg), *prim)
                du, dv, dgn, dwa, dwb, dba, dbb = vjp(do_ref[0, rs, ls])
                duv_ref[0, rs, ls] = du.astype(bf16)
                duv_ref[0, rs, vs] = dv.astype(bf16)
                dgs[p] = dgs[p] + dgn
                dws[2 * p], dws[2 * p + 1] = dws[2 * p] + dwa, dws[2 * p + 1] + dwb
                dbt = dbt + jnp.where(lane == 2 * p, dba, 0.0) + jnp.where(lane == 2 * p + 1, dbb, 0.0)
        for p in range(2):
            dg_ref[:, p * 128:(p + 1) * 128] += dgs[p]
        for gidx in range(SG_GROUPS):
            dw_ref[gidx] += dws[gidx]
        db_ref[...] += dbt

    full = lambda shp: pl.BlockSpec(shp, lambda b, n: (0,) * len(shp))
    return pl.pallas_call(
        body, name="sg_bwd", grid=(bsz, t // rows),
        in_specs=[pl.BlockSpec((1, rows, 2 * SG_WIDTH), lambda b, n: (b, n, 0)), full((1, SG_WIDTH)),
                  full((SG_GROUPS, SG_CHUNK, SG_CHUNK)), full((SG_CHUNK, 128)),
                  pl.BlockSpec((1, rows, SG_WIDTH), lambda b, n: (b, n, 0)), pl.BlockSpec(memory_space=pl.ANY)],
        out_specs=[pl.BlockSpec((1, rows, 2 * SG_WIDTH), lambda b, n: (b, n, C_SG // (2 * SG_WIDTH))), full((1, SG_WIDTH)),
                   full((SG_GROUPS, SG_CHUNK, SG_CHUNK)), full((SG_CHUNK, 128))],
        out_shape=[SDS(dproj.shape, bf16), SDS((1, SG_WIDTH), f32), SDS((SG_GROUPS, SG_CHUNK, SG_CHUNK), f32),
                   SDS((SG_CHUNK, 128), f32)],
        input_output_aliases={5: 0},
        compiler_params=_cp(("arbitrary", "arbitrary")),
    )(sguv, gain, w, bt, do, dproj)


def _pad_lanes(v, n=128):
    return jnp.pad(v.reshape(1, -1), ((0, 0), (0, n - v.size)))


def _w_in_runs():
    shard, runs = IN_DIM // N_CHIPS, []
    for s in range(N_CHIPS):
        for a, b, d in ((0, 2048, 0), (2048, 2056, C_AB), (2056, IN_DIM, C_SB)):
            lo, hi = max(shard * s, a), min(shard * (s + 1), b)
            if lo < hi:
                runs.append((s, lo - shard * s, hi - shard * s, d + lo - a))
    return runs


def w_in_from_shards(zone, tr=256):
    def body(z_ref, o_ref):
        o_ref[:, C_AB:C_SB] = jnp.zeros((tr, C_SB - C_AB), zone.dtype)
        for s, a, b, d in _w_in_runs():
            o_ref[:, d:d + b - a] = z_ref[s, :, a:b]

    return pl.pallas_call(
        body, name="w_in_from_shards", grid=(D_MODEL // tr,),
        in_specs=[pl.BlockSpec((N_CHIPS, tr, IN_DIM // N_CHIPS), lambda i: (0, i, 0))],
        out_specs=pl.BlockSpec((tr, IN_PAD), lambda i: (i, 0)), out_shape=SDS((D_MODEL, IN_PAD), zone.dtype),
        compiler_params=_cp(("arbitrary",)))(zone)


def w_in_grad_to_shards(g, tr=256):
    def body(g_ref, o_ref):
        for s, a, b, d in _w_in_runs():
            o_ref[s, :, a:b] = g_ref[:, d:d + b - a]

    return pl.pallas_call(
        body, name="w_in_grad_to_shards", grid=(D_MODEL // tr,),
        in_specs=[pl.BlockSpec((tr, IN_PAD), lambda i: (i, 0))],
        out_specs=pl.BlockSpec((N_CHIPS, tr, IN_DIM // N_CHIPS), lambda i: (0, i, 0)),
        out_shape=SDS((N_CHIPS, D_MODEL, IN_DIM // N_CHIPS), g.dtype), compiler_params=_cp(("arbitrary",)))(g)


def layer_params(p, l):
    return dict(
        g1=p["norm1_g"][l].reshape(1, -1), g2=p["norm2_g"][l].reshape(1, -1),
        conv=jnp.pad(p["conv_w"][l], ((0, 4), (0, 0))), alog=_pad_lanes(p["a_log"][l]), dtb=_pad_lanes(p["dt_bias"][l]),
        dng=p["dn_out_g"][l].reshape(1, -1), gq=jnp.tile(p["sb_q_g"][l].reshape(1, -1), (1, 2)),
        gk=jnp.tile(p["sb_k_g"][l].reshape(1, -1), (1, 2)), sgg=p["sg_v_g"][l].reshape(1, -1), sgw=p["sg_w"][l],
        sgb=jnp.pad(p["sg_b"][l].T, ((0, 0), (0, 124))))


def local_step(x, tgt, small, get_w, put_g, sync_g):
    bsz, t, _ = x.shape
    m = bsz * t
    r3 = lambda a: a.reshape(bsz, t, a.shape[-1])
    r2 = lambda a: a.reshape(m, a.shape[-1])
    xs, saved, ws = x.reshape(m, D_MODEL), [], []
    for l in range(DEPTH):
        sp, w = layer_params(small, l), {}
        w["w_in"] = get_w(l, "in", xs)
        qkv, z, ab, sb, sg, h1 = inproj_fwd(xs, sp["g1"], w["w_in"])
        mix, sall, tall = dn_fwd(r3(qkv), r3(z), r3(ab), sp["conv"], sp["alog"], sp["dtb"], sp["dng"])
        mix, ltot = sb_fwd(r3(sb), sp["gq"], sp["gk"], mix)
        mix = r2(sg_fwd(r3(sg), sp["sgg"], sp["sgw"], sp["sgb"], mix))
        w["w_out"] = get_w(l, "out", mix)
        x2 = outproj_fwd(xs, mix, w["w_out"])
        w["w_ff1"], w["w_ff2"], started = get_w(l, "ff", x2)
        if l + 1 < DEPTH:
            xs_next, rlb = ffn_fwd(x2, sp["g2"] + started, w["w_ff1"], w["w_ff2"])
        else:
            dx, rlb, lossp = ffn_fwd(x2, sp["g2"] + started, w["w_ff1"], w["w_ff2"], tgt=tgt.reshape(m, D_MODEL))
        saved.append(dict(rlb=rlb, h1=h1, x=xs, qkv=qkv, z=z, ab=ab, sb=sb, sg=sg, sall=sall, tall=tall, ltot=ltot, mix=mix, x2=x2))
        ws.append(w)
        xs = xs_next
    gsmall = [None] * DEPTH
    token = jnp.zeros((), f32)
    for l in reversed(range(DEPTH)):
        sp, w, s = layer_params(small, l), ws[l], saved[l]
        dx2, dg2, h2, act, df, dyb = ffn_bwd(s["x2"], sp["g2"] + token, w["w_ff1"], w["w_ff2"], s["rlb"], dx)
        g_ff1 = tn_matmul(h2, df, f"dw_ff1_{l}", col_shards=N_CHIPS)
        g_ff2 = tn_matmul(act, dyb, f"dw_ff2_{l}")
        dodn, dosb, dosg, dx2b = outproj_bwd(dx2, w["w_out"])
        g_out = tn_matmul(s["mix"], dx2b, f"dw_out_{l}")
        token = token + put_g(l, "rest", dict(w_out=g_out, w_ff1=g_ff1, w_ff2=g_ff2))
        dproj, dconv, dalog, ddtb, ddng = dn_bwd(r3(s["qkv"]), r3(s["z"]), r3(s["ab"]), sp["conv"], sp["alog"], sp["dtb"],
                                                 sp["dng"] + token, s["sall"], s["tall"], r3(dodn))
        token = sync_g(ddng)
        dproj, dgq, dgk = sb_bwd(r3(s["sb"]), sp["gq"] + token, sp["gk"], s["ltot"], r3(dosb), dproj)
        dproj, dsgg, dsgw, dsgb = sg_bwd(r3(s["sg"]), sp["sgg"], sp["sgw"], sp["sgb"], r3(dosg), dproj)
        dproj = r2(dproj)
        g_in = tn_matmul(s["h1"], dproj, f"dw_in_{l}")
        token = put_g(l, "in", dict(w_in=g_in))
        dx, dg1 = inproj_bwd(s["x"], sp["g1"] + token, w["w_in"], dproj, dx2)
        token = sync_g(dg1)
        fold = lambda a: (a[:, 0, :].sum(0).reshape(2, SB_DIM)).sum(0)
        gsmall[l] = dict(norm1_g=dg1[0], conv_w=dconv[0:DN_CONV], a_log=dalog[0, 0:DN_HEADS], dt_bias=ddtb[0, 0:DN_HEADS],
                         dn_out_g=ddng[0], sb_q_g=fold(dgq), sb_k_g=fold(dgk), sg_v_g=dsgg[0], sg_w=dsgw,
                         sg_b=dsgb[:, 0:SG_GROUPS].T, norm2_g=dg2[0])
    return lossp, dx.reshape(bsz, t, D_MODEL), gsmall


def _chip_peers(x, y):
    return [(1 - x, y), (x, 1 - y), (1 - x, 1 - y)]


_HBM = pl.BlockSpec(memory_space=pltpu.HBM)
_SEM = pl.BlockSpec(memory_space=pltpu.SEMAPHORE)
_EFFECT = pltpu.SideEffectType.DATAFLOW_SIDE_EFFECTING


def _hbm(a):
    return pltpu.with_memory_space_constraint(a, pltpu.HBM)


def _my_half(ref):
    half = ref.shape[0] // 2
    return ref.at[pl.ds(pl.multiple_of(lax.axis_index("c") * half, 8), half)]


def _exchange_copy(src, land, k, j, send, recv, scatter, halve, waiting):
    x, y, c = lax.axis_index("x"), lax.axis_index("y"), lax.axis_index("c")
    px, py = _chip_peers(x, y)[j]
    me, peer = 2 * x + y, 2 * px + py
    if scatter:
        src = src.at[me if waiting else peer]
    dst = land.at[peer if waiting else me]
    if halve:
        src, dst = _my_half(src), _my_half(dst)
    return pltpu.make_async_remote_copy(src_ref=src, dst_ref=dst, send_sem=send.at[3 * k + j],
                                        recv_sem=recv.at[3 * k + j], device_id=(px, py, c), device_id_type=MESH)


def exchange_start(items, name, scatter, after=None):
    arrs = []
    for a, _, _ in items:
        if not any(a is b for b in arrs):
            arrs.append(a)
    pos = [next(i for i, b in enumerate(arrs) if b is a) for a, _, _ in items]
    shapes = [a.shape if idx is None else a.shape[1:] for a, idx, _ in items]
    lands = [lax.empty(s if scatter else (N_CHIPS,) + s, a.dtype) for (a, _, _), s in zip(items, shapes)]
    na, nl = len(arrs), len(lands)
    n_in = na + nl + (after is not None)

    def body(*refs):
        ins, lnd = refs[:na], refs[na:na + nl]
        send, recv = refs[n_in], refs[n_in + 1]
        token = refs[-1]
        for k, (_, idx, halve) in enumerate(items):
            src = ins[pos[k]] if idx is None else ins[pos[k]].at[idx]
            for j in range(3):
                _exchange_copy(src, lnd[k], k, j, send, recv, scatter, halve, False).start()
        token[...] = jnp.zeros_like(token)

    sems = pltpu.SemaphoreType.DMA((3 * nl,))
    extra = [] if after is None else [after]
    out = pl.pallas_call(
        body, name=name,
        out_shape=(sems, sems, *[pltpu.HBM(a.shape, a.dtype) for a in arrs + lands], SDS((8, 128), f32)),
        in_specs=[_HBM] * (na + nl) + [pl.BlockSpec(memory_space=pl.ANY)] * len(extra),
        out_specs=(_SEM, _SEM, *[_HBM] * (na + nl), pl.BlockSpec(memory_space=pltpu.VMEM)),
        input_output_aliases={i: 2 + i for i in range(na + nl)},
        compiler_params=pltpu.CompilerParams(has_side_effects=_EFFECT),
    )(*[_hbm(a) for a in arrs + lands], *extra)
    thru = out[2:2 + na]
    return dict(send=out[0], recv=out[1], src=[(thru[pos[k]], idx) for k, (_, idx, _) in enumerate(items)],
                halve=[h for _, _, h in items], land=list(out[2 + na:2 + na + nl]), token=out[-1], scatter=scatter)


def exchange_wait(st, ks, after, name):
    arrs = []
    for k in ks:
        if not any(st["src"][k][0] is b for b in arrs):
            arrs.append(st["src"][k][0])
    pos = [next(i for i, b in enumerate(arrs) if b is st["src"][k][0]) for k in ks]
    lands = [st["land"][k] for k in ks]
    na, nl = len(arrs), len(lands)

    def body(*refs):
        ins, lnd = refs[:na], refs[na:na + nl]
        send, recv = refs[na + nl], refs[na + nl + 1]
        for t, k in enumerate(ks):
            idx = st["src"][k][1]
            src = ins[pos[t]] if idx is None else ins[pos[t]].at[idx]
            for j in range(3):
                cp = _exchange_copy(src, lnd[t], k, j, send, recv, st["scatter"], st["halve"][k], True)
                cp.wait_send()
                cp.wait_recv()

    out = pl.pallas_call(
        body, name=name, out_shape=tuple(pltpu.HBM(a.shape, a.dtype) for a in arrs + lands),
        in_specs=[_HBM] * (na + nl) + [_SEM, _SEM, pl.BlockSpec(memory_space=pl.ANY)], out_specs=tuple([_HBM] * (na + nl)),
        input_output_aliases={i: i for i in range(na + nl)},
        compiler_params=pltpu.CompilerParams(has_side_effects=_EFFECT),
    )(*arrs, *lands, st["send"], st["recv"], after)
    for k, (a, idx) in enumerate(st["src"]):
        for p, b in enumerate(arrs):
            if a is b:
                st["src"][k] = (out[p], idx)
    return list(out[na:na + nl])


def _sibling_copy(src, land, i, send, recv, other_half):
    x, y, c = lax.axis_index("x"), lax.axis_index("y"), lax.axis_index("c")
    return pltpu.make_async_remote_copy(src_ref=src.at[:, 1 - c] if other_half else src, dst_ref=land, send_sem=send.at[i],
                                        recv_sem=recv.at[i], device_id=(x, y, 1 - c), device_id_type=MESH)


def sibling_start(arrs, name, other_half=False):
    n = len(arrs)
    lands = [lax.empty((a.shape[0],) + a.shape[2:] if other_half else a.shape, a.dtype) for a in arrs]

    def body(*refs):
        ins, lnd = refs[:n], refs[n:2 * n]
        send, recv = refs[2 * n], refs[2 * n + 1]
        token = refs[-1]
        for i in range(n):
            _sibling_copy(ins[i], lnd[i], i, send, recv, other_half).start()
        token[...] = jnp.zeros_like(token)

    sems = pltpu.SemaphoreType.DMA((n,))
    out = pl.pallas_call(
        body, name=name,
        out_shape=(sems, sems, *[pltpu.HBM(a.shape, a.dtype) for a in arrs + lands], SDS((8, 128), f32)),
        in_specs=[_HBM] * (2 * n), out_specs=(_SEM, _SEM, *[_HBM] * (2 * n), pl.BlockSpec(memory_space=pltpu.VMEM)),
        input_output_aliases={i: 2 + i for i in range(2 * n)},
        compiler_params=pltpu.CompilerParams(has_side_effects=_EFFECT),
    )(*[_hbm(a) for a in arrs + lands])
    return dict(send=out[0], recv=out[1], src=list(out[2:2 + n]), land=list(out[2 + n:2 + 2 * n]), token=out[-1],
                other_half=other_half)


def sibling_wait(st, after, name):
    n = len(st["src"])

    def body(*refs):
        ins, lnd = refs[:n], refs[n:2 * n]
        send, recv = refs[2 * n], refs[2 * n + 1]
        for i in range(n):
            cp = _sibling_copy(ins[i], lnd[i], i, send, recv, st["other_half"])
            cp.wait_send()
            cp.wait_recv()

    out = pl.pallas_call(
        body, name=name, out_shape=tuple(pltpu.HBM(a.shape, a.dtype) for a in st["src"] + st["land"]),
        in_specs=[_HBM] * (2 * n) + [_SEM, _SEM, pl.BlockSpec(memory_space=pl.ANY)], out_specs=tuple([_HBM] * (2 * n)),
        input_output_aliases={i: i for i in range(2 * n)},
        compiler_params=pltpu.CompilerParams(has_side_effects=_EFFECT),
    )(*st["src"], *st["land"], st["send"], st["recv"], after)
    return list(out[:n]), list(out[n:])


def swap_halves(zones, name):
    n = len(zones)

    def body(*refs):
        outs = refs[n:2 * n]
        send, recv = refs[2 * n:]
        x, y, c = lax.axis_index("x"), lax.axis_index("y"), lax.axis_index("c")
        cps = []
        for i in range(n):
            for j, (px, py) in enumerate(_chip_peers(x, y)):
                part = _my_half(outs[i].at[2 * px + py])
                cps.append(pltpu.make_async_remote_copy(src_ref=part, dst_ref=part, send_sem=send.at[3 * i + j],
                                                        recv_sem=recv.at[3 * i + j], device_id=(x, y, 1 - c), device_id_type=MESH))
        for cp in cps:
            cp.start()
        for cp in cps:
            cp.wait_send()
            cp.wait_recv()

    any_spec = pl.BlockSpec(memory_space=pl.ANY)
    return pl.pallas_call(
        body, name=name, in_specs=[any_spec] * n, out_specs=[any_spec] * n, out_shape=[SDS(a.shape, a.dtype) for a in zones],
        input_output_aliases={i: i for i in range(n)},
        scratch_shapes=[pltpu.SemaphoreType.DMA((3 * n,)), pltpu.SemaphoreType.DMA((3 * n,))],
    )(*zones)


def _ids_spec(grid, in_specs, out_specs):
    return pltpu.PrefetchScalarGridSpec(num_scalar_prefetch=1, grid=grid, in_specs=in_specs, out_specs=out_specs)


def pair_sum(ids, a, b, name, tr=512):
    nd, _, rows, cols = a.shape
    tr = min(tr, rows)
    assert rows % tr == 0

    def body(ids_ref, a_ref, b_ref, o_ref):
        o_ref[...] = (a_ref[0].astype(f32) + b_ref[...].astype(f32)).astype(bf16)

    spec = pl.BlockSpec((1, tr, cols), lambda d, i, ids: (d, i, 0))
    return pl.pallas_call(
        body, name=name,
        grid_spec=_ids_spec((nd, rows // tr), [pl.BlockSpec((1, 1, tr, cols), lambda d, i, ids: (d, ids[1], i, 0)), spec], spec),
        out_shape=SDS((nd, rows, cols), bf16), compiler_params=_cp(("arbitrary", "arbitrary")))(ids, a, b)


def allreduce_small(v):
    half = v.shape[0] // 2
    assert half % 8 == 0

    def body(v_ref, o_ref, rbuf, send, recv):
        x, y, c = lax.axis_index("x"), lax.axis_index("y"), lax.axis_index("c")
        o_ref[...] = v_ref[...]

        def exchange(rows, peer, k):
            return pltpu.make_async_remote_copy(src_ref=o_ref.at[rows], dst_ref=rbuf.at[k, rows], send_sem=send.at[k],
                                                recv_sem=recv.at[k], device_id=peer, device_id_type=MESH)

        lo, hi, across_x, across_y = pl.ds(0, half), pl.ds(half, half), (1 - x, y, c), (x, 1 - y, c)
        stages = [[(pl.ds(0, 2 * half), (x, y, 1 - c))], [(lo, across_x), (hi, across_y)], [(lo, across_y), (hi, across_x)]]
        k = 0
        for stage in stages:
            cps = [exchange(rows, peer, k + i) for i, (rows, peer) in enumerate(stage)]
            for cp in cps:
                cp.start()
            for cp in cps:
                cp.wait()
            for i, (rows, _) in enumerate(stage):
                o_ref[rows] = o_ref[rows] + rbuf[k + i, rows]
            k += len(stage)

    vm = pl.BlockSpec(memory_space=pltpu.VMEM)
    return pl.pallas_call(
        body, name="allreduce_small", in_specs=[vm], out_specs=vm, out_shape=SDS(v.shape, f32),
        scratch_shapes=[pltpu.VMEM((5,) + v.shape, f32), pltpu.SemaphoreType.DMA((5,)), pltpu.SemaphoreType.DMA((5,))],
        compiler_params=_cp(),
    )(v)


def sum_partials(ids, zone, mine, name, tr=256):
    _, rows, cols = zone.shape
    tr = min(tr, rows)
    assert rows % tr == 0

    def body(ids_ref, m_ref, z1_ref, z2_ref, z3_ref, o_ref):
        o_ref[...] = ((m_ref[0].astype(f32) + z1_ref[0].astype(f32)) + z2_ref[0].astype(f32)) + z3_ref[0].astype(f32)

    slot = lambda flip: pl.BlockSpec((1, tr, cols), lambda i, ids: (ids[0] ^ flip, i, 0))
    return pl.pallas_call(
        body, name=name,
        grid_spec=_ids_spec((rows // tr,), [slot(0), slot(1), slot(2), slot(3)], pl.BlockSpec((tr, cols), lambda i, ids: (i, 0))),
        out_shape=SDS((rows, cols), f32), compiler_params=_cp(("arbitrary",)),
    )(ids, mine, zone, zone, zone)


def adamw(w, m, v, gs, name, layer=0, prev=None, tr=256):
    hrows, cols = gs[0].shape
    rows = hrows * len(gs)
    tr = min(tr, hrows)
    assert hrows % tr == 0 and w.shape[0] % rows == 0
    off, nth = layer * (rows // tr), hrows // tr

    def body(w_ref, m_ref, v_ref, *rest):
        g_ref, d_ref, mo_ref, vo_ref = rest[-4:]
        if len(gs) == 1:
            g = rest[0][...]
        else:
            g = jnp.where(pl.program_id(0) // nth == lax.axis_index("c"), rest[0][...], rest[1][...])
        mn = ADAM_B1 * m_ref[...] + (1.0 - ADAM_B1) * g
        vn = ADAM_B2 * v_ref[...] + (1.0 - ADAM_B2) * jnp.square(g)
        m_hat = mn / (1.0 - ADAM_B1 ** ADAM_STEP)
        v_hat = vn / (1.0 - ADAM_B2 ** ADAM_STEP)
        g_ref[...] = g
        d_ref[...] = -ADAM_LR * (m_hat / (jnp.sqrt(v_hat) + ADAM_EPS) + ADAM_WD * w_ref[...])
        mo_ref[...] = mn
        vo_ref[...] = vn

    loc = pl.BlockSpec((tr, cols), lambda i: (i % nth, 0))
    glob = pl.BlockSpec((tr, cols), lambda i: (off + i, 0))
    extra = [] if prev is None else list(prev)
    return pl.pallas_call(
        body, name=name, grid=(rows // tr,),
        in_specs=[glob] * 3 + [loc] * len(gs) + [pl.BlockSpec(memory_space=pl.ANY)] * len(extra),
        out_specs=[glob] * 4, out_shape=[SDS(w.shape, f32)] * 4,
        input_output_aliases={3 + len(gs) + j: j for j in range(len(extra))},
        compiler_params=_cp(("arbitrary",)),
    )(w, m, v, *gs, *extra)


BIG = ("w_in", "w_out", "w_ff1", "w_ff2")
SMALL = ("norm1_g", "conv_w", "a_log", "dt_bias", "dn_out_g", "sb_q_g", "sb_k_g", "sg_v_g", "sg_w", "sg_b", "norm2_g")
WEIGHTS = ("norm1_g", "w_in", "conv_w", "a_log", "dt_bias", "dn_out_g", "sb_q_g", "sb_k_g", "sg_v_g", "sg_w", "sg_b",
           "w_out", "norm2_g", "w_ff1", "w_ff2")


PACK_ROWS = 256


def _rows_of(shape):
    n = 1
    for d in shape:
        n *= d
    return -(-n // 1024) * 8, n


def _pack(arrs):
    parts = []
    for a in arrs:
        r, n = _rows_of(a.shape)
        parts.append(jnp.pad(a.reshape(-1), (0, r * 128 - n)).reshape(r, 128))
    rows = sum(p.shape[0] for p in parts)
    parts.append(jnp.zeros((-rows % PACK_ROWS, 128), arrs[0].dtype))
    return jnp.concatenate(parts, axis=0)


def _unpack(packed, shapes):
    out, o = [], 0
    for s in shapes:
        r, n = _rows_of(s)
        out.append(packed[o:o + r].reshape(-1)[0:n].reshape(s))
        o += r
    return out


def kernel(x, norm1_g, w_in, conv_w, a_log, dt_bias, dn_out_g, sb_q_g, sb_k_g, sg_v_g, sg_w, sg_b, w_out, norm2_g, w_ff1, w_ff2, loss_target, m_norm1_g, m_w_in, m_conv_w, m_a_log, m_dt_bias, m_dn_out_g, m_sb_q_g, m_sb_k_g, m_sg_v_g, m_sg_w, m_sg_b, m_w_out, m_norm2_g, m_w_ff1, m_w_ff2, v_norm1_g, v_w_in, v_conv_w, v_a_log, v_dt_bias, v_dn_out_g, v_sb_q_g, v_sb_k_g, v_sg_v_g, v_sg_w, v_sg_b, v_w_out, v_norm2_g, v_w_ff1, v_w_ff2):
    w = dict(norm1_g=norm1_g, w_in=w_in, conv_w=conv_w, a_log=a_log, dt_bias=dt_bias, dn_out_g=dn_out_g, sb_q_g=sb_q_g,
             sb_k_g=sb_k_g, sg_v_g=sg_v_g, sg_w=sg_w, sg_b=sg_b, w_out=w_out, norm2_g=norm2_g, w_ff1=w_ff1, w_ff2=w_ff2)
    mom = dict(norm1_g=m_norm1_g, w_in=m_w_in, conv_w=m_conv_w, a_log=m_a_log, dt_bias=m_dt_bias, dn_out_g=m_dn_out_g,
               sb_q_g=m_sb_q_g, sb_k_g=m_sb_k_g, sg_v_g=m_sg_v_g, sg_w=m_sg_w, sg_b=m_sg_b, w_out=m_w_out, norm2_g=m_norm2_g,
               w_ff1=m_w_ff1, w_ff2=m_w_ff2)
    var = dict(norm1_g=v_norm1_g, w_in=v_w_in, conv_w=v_conv_w, a_log=v_a_log, dt_bias=v_dt_bias, dn_out_g=v_dn_out_g,
               sb_q_g=v_sb_q_g, sb_k_g=v_sb_k_g, sg_v_g=v_sg_v_g, sg_w=v_sg_w, sg_b=v_sg_b, w_out=v_w_out, norm2_g=v_norm2_g,
               w_ff1=v_w_ff1, w_ff2=v_w_ff2)
    chip = 2 * lax.axis_index("x") + lax.axis_index("y")

    wb = [{k: w[k][l].astype(bf16) for k in BIG} for l in range(DEPTH)]
    ags = {0: exchange_start([(conv_w, None, False)] + [(wb[0][k], None, True) for k in BIG], "allgather_start_0", scatter=False)}
    item = lambda l, k: (l, (l == 0) + BIG.index(k))

    def landed(items, after, name):
        ag, ks = ags[items[0][0]], [k for _, k in items]
        zones = exchange_wait(ag, ks, after, name)
        halved = [t for t, k in enumerate(ks) if ag["halve"][k]]
        for t, z in zip(halved, swap_halves([zones[t] for t in halved], name.replace("wait", "pass"))):
            zones[t] = z
        return [lax.dynamic_update_slice_in_dim(z, ag["src"][k][0][None], chip, axis=0) for z, k in zip(zones, ks)]

    def whole(k, z):
        if k == "w_in":
            return w_in_from_shards(z)
        return z if k == "w_ff1" else z.reshape(-1, D_MODEL)

    g_conv, first_in = landed([(0, 0), item(0, "w_in")], x, "allgather_wait_in0")
    small = {k: w[k] for k in SMALL}
    small["conv_w"] = jnp.transpose(g_conv, (1, 2, 0, 3)).reshape(DEPTH, DN_CONV, 3 * DN_WIDTH)
    cache = {}

    def get_w(l, part, after):
        if part == "in":
            return whole("w_in", first_in if l == 0 else landed([item(l, "w_in")], after, f"allgather_wait_in{l}")[0])
        if part == "out":
            zs = landed([item(l, k) for k in ("w_out", "w_ff1", "w_ff2")], after, f"allgather_wait_rest{l}")
            token = jnp.zeros((), f32)
            if l + 1 < DEPTH:
                ags[l + 1] = exchange_start([(wb[l + 1][k], None, True) for k in BIG], f"allgather_start_{l + 1}",
                                            scatter=False, after=zs[0])
                token = ags[l + 1]["token"][0, 0]
            cache[l] = (whole("w_ff1", zs[1]), whole("w_ff2", zs[2]), token)
            return whole("w_out", zs[0])
        return cache[l]

    rs, pending = {}, []
    ids = jnp.stack([chip, lax.axis_index("c")]).astype(jnp.int32)

    def put_g(l, tag, g):
        names = [k for k in BIG if k in g]
        by_dest = [w_in_grad_to_shards(g[k]) if k == "w_in" else g[k] for k in names]
        halves = [a.reshape(N_CHIPS, 2, -1, a.shape[-1]) for a in by_dest]
        st = sibling_start(halves, f"pair_swap_start_{tag}{l}", other_half=True)
        pending.append((l, tag, names, st))
        return st["token"][0, 0]

    def sync_g(after):
        token = jnp.zeros((), f32)
        while pending:
            l, tag, names, st = pending.pop(0)
            halves, got = sibling_wait(st, after, f"pair_swap_wait_{tag}{l}")
            pair = [pair_sum(ids, a, b, f"pair_sum_{k}_{l}") for k, a, b in zip(names, halves, got)]
            rs[l, tag] = dict(exchange_start([(a, None, False) for a in pair], f"scatter_start_{tag}{l}", scatter=True), names=names)
            token = token + rs[l, tag]["token"][0, 0]
        return token

    lossp, grad_x, gsmall = local_step(x, loss_target, small, get_w, put_g, sync_g)

    def sum_group(l, tag, after):
        st = rs[l, tag]
        zones = exchange_wait(st, list(range(len(st["names"]))), after, f"scatter_wait_{tag}{l}")
        sums = [sum_partials(ids, zones[i], st["src"][i][0], f"sum_{k}_{l}") for i, k in enumerate(st["names"])]
        return sibling_start(sums, f"swap_sums_start_{tag}{l}")

    def update_group(l, tag, swap, after, prev):
        sums, others = sibling_wait(swap, after, f"swap_sums_wait_{tag}{l}")
        outs = dict(prev)
        for i, k in enumerate(rs[l, tag]["names"]):
            r2 = lambda a: a.reshape(-1, a.shape[-1])
            outs[k] = adamw(r2(w[k]), r2(mom[k]), r2(var[k]), (sums[i], others[i]), f"adamw_{k}_{l}", layer=l, prev=prev.get(k))
        return outs

    swap_r = sum_group(1, "rest", rs[0, "in"]["token"])
    swap_i = sum_group(1, "in", swap_r["token"])
    done = update_group(1, "rest", swap_r, swap_i["token"], {})
    done = update_group(1, "in", swap_i, done["w_ff2"][0], done)
    res = {}

    full_shapes = [(DEPTH,) + tuple(gsmall[0][k].shape) for k in SMALL]
    packed = _pack([jnp.stack([gsmall[l][k] for l in range(DEPTH)]) for k in SMALL] + [jnp.sum(lossp).reshape(1)])
    *totals, loss = _unpack(allreduce_small(packed), full_shapes + [(1,)])
    loss = loss[0]
    gfull = dict(zip(SMALL, totals))
    cs = 3 * DN_WIDTH // N_CHIPS
    gfull["conv_w"] = lax.dynamic_slice_in_dim(gfull["conv_w"], chip * cs, cs, axis=2)
    gp, wp, mp, vp = (_pack([d[k] for k in SMALL]) for d in (gfull, w, mom, var))
    outs = adamw(wp, mp, vp, (gp,), "adamw_small")
    loc_shapes = [w[k].shape for k in SMALL]
    unp = [_unpack(o, loc_shapes) for o in outs]
    for i, k in enumerate(SMALL):
        res[k] = [unp[j][i] for j in range(4)]

    swap_r = sum_group(0, "rest", outs[0])
    swap_i = sum_group(0, "in", swap_r["token"])
    done = update_group(0, "rest", swap_r, swap_i["token"], done)
    done = update_group(0, "in", swap_i, done["w_ff2"][0], done)
    for k in BIG:
        res[k] = [o.reshape(w[k].shape) for o in done[k]]

    return (loss, grad_x, *[res[k][0] for k in WEIGHTS], *[res[k][1] for k in WEIGHTS], *[res[k][2] for k in WEIGHTS],
            *[res[k][3] for k in WEIGHTS])
```
